```python
import math
import jax, jax.numpy as jnp
from jax import lax
import numpy as np

D_MODEL = 1024
BATCH = 16
SEQ = 2048
DEPTH = 1

CTX_LEN = 256
GRID_W = 64
D_MIX = D_MODEL
D_ATTN = D_MIX // 2
D_CONV = D_MIX - D_ATTN
HEAD_DIM = 64
N_Q_HEADS = D_ATTN // HEAD_DIM
N_KV_HEADS = 2
GROUP = N_Q_HEADS // N_KV_HEADS
ROPE_AXIS_DIM = HEAD_DIM // 2
ROPE_THETA = 10000.0
Q_BLOCK = 128
CONV_WIDTH = 31
CONV_PAD = (CONV_WIDTH - 1) // 2
EPS = 1e-6

Q_LO, Q_HI = 0, D_ATTN
K_LO, K_HI = Q_HI, Q_HI + N_KV_HEADS * HEAD_DIM
V_LO, V_HI = K_HI, K_HI + N_KV_HEADS * HEAD_DIM
ZA_LO, ZA_HI = V_HI, V_HI + D_ATTN
GLU_LO, GLU_HI = ZA_HI, ZA_HI + 2 * D_CONV
ZC_LO, ZC_HI = GLU_HI, GLU_HI + D_CONV
D_IN = ZC_HI

kernel_name = "hybrid_gqa_conformer_prefix_dit_layer"


def rms_norm(x, w):
    xf = x.astype(jnp.float32)
    y = xf * lax.rsqrt(jnp.mean(xf * xf, axis=-1, keepdims=True) + EPS)
    return (y * w.astype(jnp.float32)).astype(x.dtype)


def layer_norm(x, w, b):
    xf = x.astype(jnp.float32)
    mu = jnp.mean(xf, axis=-1, keepdims=True)
    var = jnp.mean(jnp.square(xf - mu), axis=-1, keepdims=True)
    y = (xf - mu) * lax.rsqrt(var + EPS)
    return (y * w.astype(jnp.float32) + b.astype(jnp.float32)).astype(x.dtype)


def grid_angles(n_tokens, dtype):
    rows = n_tokens // GRID_W
    row = jnp.repeat(jnp.arange(rows, dtype=jnp.float32), GRID_W)
    col = jnp.tile(jnp.arange(GRID_W, dtype=jnp.float32), rows)
    freqs = ROPE_THETA ** (-jnp.arange(0, ROPE_AXIS_DIM, 2, dtype=jnp.float32) / ROPE_AXIS_DIM)
    ang_r = row[:, None] * freqs[None, :]
    ang_c = col[:, None] * freqs[None, :]
    return (jnp.cos(ang_r).astype(dtype), jnp.sin(ang_r).astype(dtype),
            jnp.cos(ang_c).astype(dtype), jnp.sin(ang_c).astype(dtype))


def rope_axis(x, cos, sin):
    cos = cos[None, :, None, :]
    sin = sin[None, :, None, :]
    x1, x2 = jnp.split(x, 2, axis=-1)
    return jnp.concatenate([x1 * cos - x2 * sin, x1 * sin + x2 * cos], axis=-1)


def rope_2d(x, angles):
    cr, sr, cc, sc = angles
    return jnp.concatenate([rope_axis(x[..., :ROPE_AXIS_DIM], cr, sr),
                            rope_axis(x[..., ROPE_AXIS_DIM:], cc, sc)], axis=-1)


def qk_heads(p, qn_w, kn_w):
    B, S, _ = p.shape
    q = p[..., Q_LO:Q_HI].reshape(B, S, N_Q_HEADS, HEAD_DIM)
    k = p[..., K_LO:K_HI].reshape(B, S, N_KV_HEADS, HEAD_DIM)
    v = p[..., V_LO:V_HI].reshape(B, S, N_KV_HEADS, HEAD_DIM)
    return rms_norm(q, qn_w), rms_norm(k, kn_w), v


def kv_heads(p_kv, kn_w):
    B, S, _ = p_kv.shape
    kv = p_kv.reshape(B, S, 2, N_KV_HEADS, HEAD_DIM)
    return rms_norm(kv[:, :, 0], kn_w), kv[:, :, 1]


def attend_blocked(q, k, v):
    B, S, _, _ = q.shape
    scale = 1.0 / math.sqrt(HEAD_DIM)
    qb = q.reshape(B, S // Q_BLOCK, Q_BLOCK, N_KV_HEADS, GROUP, HEAD_DIM).transpose(1, 0, 2, 3, 4, 5)

    def one_block(qblk):
        s = jnp.einsum('bqkgd,bskd->bkgqs', qblk, k).astype(jnp.float32) * scale
        pr = jax.nn.softmax(s, axis=-1).astype(v.dtype)
        return jnp.einsum('bkgqs,bskd->bqkgd', pr, v)

    o = lax.map(one_block, qb)
    return o.transpose(1, 0, 2, 3, 4, 5).reshape(B, S, D_ATTN)


def attend_dense(q, k, v):
    B, C, _, _ = q.shape
    qg = q.reshape(B, C, N_KV_HEADS, GROUP, HEAD_DIM)
    s = jnp.einsum('bqkgd,bskd->bkgqs', qg, k).astype(jnp.float32) / math.sqrt(HEAD_DIM)
    pr = jax.nn.softmax(s, axis=-1).astype(v.dtype)
    return jnp.einsum('bkgqs,bskd->bqkgd', pr, v).reshape(B, C, D_ATTN)


def conformer_conv(glu_in, conv_w, conv_b, ln_w, ln_b, w_pw, b_pw):
    a, g = jnp.split(glu_in, 2, axis=-1)
    u = a * jax.nn.sigmoid(g)
    y = lax.conv_general_dilated(u, conv_w[:, None, :].astype(u.dtype), window_strides=(1,),
                                 padding=[(CONV_PAD, CONV_PAD)],
                                 dimension_numbers=('NWC', 'WIO', 'NWC'),
                                 feature_group_count=D_CONV) + conv_b
    y = jax.nn.silu(layer_norm(y, ln_w, ln_b))
    return y @ w_pw + b_pw


def _fwd_setup_inputs(seed: int = 0) -> dict:
    key = jax.random.key(seed)
    ks = jax.random.split(key, 20)
    nrm = jax.random.normal
    f32 = jnp.float32
    return {
        "x": nrm(ks[0], (BATCH, SEQ, D_MODEL), f32),
        "c": nrm(ks[1], (BATCH, D_MODEL), f32),
        "ctx": nrm(ks[2], (BATCH, CTX_LEN, D_MODEL), f32),
        "c_ctx": nrm(ks[3], (D_MODEL,), f32),
        "w_mod": nrm(ks[4], (DEPTH, D_MODEL, 3 * D_MODEL), f32) * D_MODEL ** -0.5,
        "b_mod": nrm(ks[5], (DEPTH, 3 * D_MODEL), f32) * 0.02,
        "norm_w": 1.0 + 0.05 * nrm(ks[6], (DEPTH, D_MODEL), f32),
        "w_in": nrm(ks[7], (DEPTH, D_MODEL, D_IN), f32) * D_MODEL ** -0.5,
        "q_norm_w": 1.0 + 0.05 * nrm(ks[8], (DEPTH, HEAD_DIM), f32),
        "k_norm_w": 1.0 + 0.05 * nrm(ks[9], (DEPTH, HEAD_DIM), f32),
        "conv_w": nrm(ks[10], (DEPTH, CONV_WIDTH, D_CONV), f32) * CONV_WIDTH ** -0.5,
        "conv_b": nrm(ks[11], (DEPTH, D_CONV), f32) * 0.02,
        "conv_ln_w": 1.0 + 0.05 * nrm(ks[12], (DEPTH, D_CONV), f32),
        "conv_ln_b": nrm(ks[13], (DEPTH, D_CONV), f32) * 0.02,
        "w_pw": nrm(ks[14], (DEPTH, D_CONV, D_CONV), f32) * D_CONV ** -0.5,
        "b_pw": nrm(ks[15], (DEPTH, D_CONV), f32) * 0.02,
        "w_out": nrm(ks[16], (DEPTH, D_MIX, D_MODEL), f32) * D_MIX ** -0.5,
    }


def _fwd_reference(x, c, ctx, c_ctx, w_mod, b_mod, norm_w, w_in, q_norm_w, k_norm_w,
              conv_w, conv_b, conv_ln_w, conv_ln_b, w_pw, b_pw, w_out):
    S = x.shape[1]
    angles = grid_angles(S, x.dtype)
    h, hc = x, ctx
    for l in range(DEPTH):
        mod = jax.nn.silu(c) @ w_mod[l] + b_mod[l]
        shift, scale, gate = jnp.split(mod, 3, axis=-1)
        mod_c = jax.nn.silu(c_ctx) @ w_mod[l] + b_mod[l]
        shift_c, scale_c, gate_c = jnp.split(mod_c, 3, axis=-1)

        u = rms_norm(h, norm_w[l]) * (1.0 + scale[:, None, :]) + shift[:, None, :]
        uc = rms_norm(hc, norm_w[l]) * (1.0 + scale_c) + shift_c

        p = u @ w_in[l]
        q, k, v = qk_heads(p, q_norm_w[l], k_norm_w[l])
        q = rope_2d(q, angles)
        k = rope_2d(k, angles)

        if l == DEPTH - 1:
            k_c, v_c = kv_heads(uc @ w_in[l][:, K_LO:V_HI], k_norm_w[l])
        else:
            pc = uc @ w_in[l]
            q_c, k_c, v_c = qk_heads(pc, q_norm_w[l], k_norm_w[l])
            attn_c = attend_dense(q_c, k_c, v_c) * jax.nn.silu(pc[..., ZA_LO:ZA_HI])
            conv_c = conformer_conv(pc[..., GLU_LO:GLU_HI], conv_w[l], conv_b[l], conv_ln_w[l],
                                    conv_ln_b[l], w_pw[l], b_pw[l]) * jax.nn.silu(pc[..., ZC_LO:ZC_HI])
            out_c = jnp.concatenate([attn_c, conv_c], axis=-1) @ w_out[l]

        k_all = jnp.concatenate([k_c, k], axis=1)
        v_all = jnp.concatenate([v_c, v], axis=1)
        attn = attend_blocked(q, k_all, v_all) * jax.nn.silu(p[..., ZA_LO:ZA_HI])

        conv = conformer_conv(p[..., GLU_LO:GLU_HI], conv_w[l], conv_b[l], conv_ln_w[l],
                              conv_ln_b[l], w_pw[l], b_pw[l]) * jax.nn.silu(p[..., ZC_LO:ZC_HI])

        out = jnp.concatenate([attn, conv], axis=-1) @ w_out[l]
        h = h + gate[:, None, :] * out
        if l < DEPTH - 1:
            hc = hc + gate_c * out_c
    return h


import jax as _jax
import jax.numpy as _jnp

TWIN_FORMAT = 'train_step'
FWD_PARAMS = ['x', 'c', 'ctx', 'c_ctx', 'w_mod', 'b_mod', 'norm_w', 'w_in', 'q_norm_w', 'k_norm_w', 'conv_w', 'conv_b', 'conv_ln_w', 'conv_ln_b', 'w_pw', 'b_pw', 'w_out']
TWIN_WEIGHTS = ['c_ctx', 'w_mod', 'b_mod', 'norm_w', 'w_in', 'q_norm_w', 'k_norm_w', 'conv_w', 'conv_b', 'conv_ln_w', 'conv_ln_b', 'w_pw', 'b_pw', 'w_out']
TWIN_DIFF_INPUT = 'x'
TWIN_INPUTS = ['x', 'c', 'ctx', 'c_ctx', 'w_mod', 'b_mod', 'norm_w', 'w_in', 'q_norm_w', 'k_norm_w', 'conv_w', 'conv_b', 'conv_ln_w', 'conv_ln_b', 'w_pw', 'b_pw', 'w_out', 'loss_target', 'm_c_ctx', 'm_w_mod', 'm_b_mod', 'm_norm_w', 'm_w_in', 'm_q_norm_w', 'm_k_norm_w', 'm_conv_w', 'm_conv_b', 'm_conv_ln_w', 'm_conv_ln_b', 'm_w_pw', 'm_b_pw', 'm_w_out', 'v_c_ctx', 'v_w_mod', 'v_b_mod', 'v_norm_w', 'v_w_in', 'v_q_norm_w', 'v_k_norm_w', 'v_conv_w', 'v_conv_b', 'v_conv_ln_w', 'v_conv_ln_b', 'v_w_pw', 'v_b_pw', 'v_w_out']
TWIN_OUTPUTS = ['loss', 'grad_x', 'grad_c_ctx', 'grad_w_mod', 'grad_b_mod', 'grad_norm_w', 'grad_w_in', 'grad_q_norm_w', 'grad_k_norm_w', 'grad_conv_w', 'grad_conv_b', 'grad_conv_ln_w', 'grad_conv_ln_b', 'grad_w_pw', 'grad_b_pw', 'grad_w_out', 'delta_c_ctx', 'delta_w_mod', 'delta_b_mod', 'delta_norm_w', 'delta_w_in', 'delta_q_norm_w', 'delta_k_norm_w', 'delta_conv_w', 'delta_conv_b', 'delta_conv_ln_w', 'delta_conv_ln_b', 'delta_w_pw', 'delta_b_pw', 'delta_w_out', 'new_m_c_ctx', 'new_m_w_mod', 'new_m_b_mod', 'new_m_norm_w', 'new_m_w_in', 'new_m_q_norm_w', 'new_m_k_norm_w', 'new_m_conv_w', 'new_m_conv_b', 'new_m_conv_ln_w', 'new_m_conv_ln_b', 'new_m_w_pw', 'new_m_b_pw', 'new_m_w_out', 'new_v_c_ctx', 'new_v_w_mod', 'new_v_b_mod', 'new_v_norm_w', 'new_v_w_in', 'new_v_q_norm_w', 'new_v_k_norm_w', 'new_v_conv_w', 'new_v_conv_b', 'new_v_conv_ln_w', 'new_v_conv_ln_b', 'new_v_w_pw', 'new_v_b_pw', 'new_v_w_out']
TWIN_LEAF_KINDS = {'loss': 'loss', 'grad_x': 'grad_x', 'grad_c_ctx': 'grad_w', 'grad_w_mod': 'grad_w', 'grad_b_mod': 'grad_w', 'grad_norm_w': 'grad_w', 'grad_w_in': 'grad_w', 'grad_q_norm_w': 'grad_w', 'grad_k_norm_w': 'grad_w', 'grad_conv_w': 'grad_w', 'grad_conv_b': 'grad_w', 'grad_conv_ln_w': 'grad_w', 'grad_conv_ln_b': 'grad_w', 'grad_w_pw': 'grad_w', 'grad_b_pw': 'grad_w', 'grad_w_out': 'grad_w', 'delta_c_ctx': 'delta_w', 'delta_w_mod': 'delta_w', 'delta_b_mod': 'delta_w', 'delta_norm_w': 'delta_w', 'delta_w_in': 'delta_w', 'delta_q_norm_w': 'delta_w', 'delta_k_norm_w': 'delta_w', 'delta_conv_w': 'delta_w', 'delta_conv_b': 'delta_w', 'delta_conv_ln_w': 'delta_w', 'delta_conv_ln_b': 'delta_w', 'delta_w_pw': 'delta_w', 'delta_b_pw': 'delta_w', 'delta_w_out': 'delta_w', 'new_m_c_ctx': 'new_m', 'new_m_w_mod': 'new_m', 'new_m_b_mod': 'new_m', 'new_m_norm_w': 'new_m', 'new_m_w_in': 'new_m', 'new_m_q_norm_w': 'new_m', 'new_m_k_norm_w': 'new_m', 'new_m_conv_w': 'new_m', 'new_m_conv_b': 'new_m', 'new_m_conv_ln_w': 'new_m', 'new_m_conv_ln_b': 'new_m', 'new_m_w_pw': 'new_m', 'new_m_b_pw': 'new_m', 'new_m_w_out': 'new_m', 'new_v_c_ctx': 'new_v', 'new_v_w_mod': 'new_v', 'new_v_b_mod': 'new_v', 'new_v_norm_w': 'new_v', 'new_v_w_in': 'new_v', 'new_v_q_norm_w': 'new_v', 'new_v_k_norm_w': 'new_v', 'new_v_conv_w': 'new_v', 'new_v_conv_b': 'new_v', 'new_v_conv_ln_w': 'new_v', 'new_v_conv_ln_b': 'new_v', 'new_v_w_pw': 'new_v', 'new_v_b_pw': 'new_v', 'new_v_w_out': 'new_v'}


def _forward(args):
    return _fwd_reference(*[args[k] for k in FWD_PARAMS])


def _output_shape():
    out = _jax.eval_shape(lambda: _forward(_fwd_setup_inputs(0)))
    return out.shape, out.dtype

N_MICROBATCH = 1
ADAM_LR = 0.001
ADAM_B1 = 0.9
ADAM_B2 = 0.999
ADAM_EPS = 1e-08
ADAM_WD = 0.01
ADAM_STEP = 10
PER_EXAMPLE_BATCH_AXIS = {'x': 0, 'c': 0, 'ctx': 0, 'loss_target': 0}
SHARED_INPUTS = []
_WEIGHT_DTYPES = {'c_ctx': _jnp.float32, 'w_mod': _jnp.float32, 'b_mod': _jnp.float32, 'norm_w': _jnp.float32, 'w_in': _jnp.float32, 'q_norm_w': _jnp.float32, 'k_norm_w': _jnp.float32, 'conv_w': _jnp.float32, 'conv_b': _jnp.float32, 'conv_ln_w': _jnp.float32, 'conv_ln_b': _jnp.float32, 'w_pw': _jnp.float32, 'b_pw': _jnp.float32, 'w_out': _jnp.float32}
MOMENT_SCALE = {'c_ctx': 3.578262e-01, 'w_mod': 1.765546e+00, 'b_mod': 3.558916e+00, 'norm_w': 2.812392e+00, 'w_in': 1.280556e+00, 'q_norm_w': 2.555468e-01, 'k_norm_w': 2.748240e-01, 'conv_w': 6.930278e-01, 'conv_b': 2.398364e+00, 'conv_ln_w': 5.060658e+00, 'conv_ln_b': 3.100541e+00, 'w_pw': 7.047764e-01, 'b_pw': 2.730025e+00, 'w_out': 4.095244e-01}


def _to_microbatches(a, axis):
    t = _jnp.moveaxis(a, axis, 0)
    t = t.reshape((N_MICROBATCH, t.shape[0] // N_MICROBATCH) + t.shape[1:])
    return _jnp.moveaxis(t, 1, axis + 1)


def setup_inputs(seed: int = 0) -> dict:
    inp = _fwd_setup_inputs(seed)
    key = _jax.random.fold_in(_jax.random.key(seed), 7919)
    shape, _ = _output_shape()
    out = dict(inp)
    out["loss_target"] = _jax.random.normal(_jax.random.fold_in(key, 0), shape, _jnp.float32)
    for i, name in enumerate(TWIN_WEIGHTS):
        w = inp[name].astype(_jnp.float32)
        if MOMENT_SCALE is None:
            s = _jnp.sqrt(_jnp.mean(_jnp.square(w)) + 1e-30)
        else:
            s = MOMENT_SCALE[name]
        km, kv = _jax.random.split(_jax.random.fold_in(key, i + 1))
        out[name] = w
        out["m_" + name] = s * _jax.random.normal(km, w.shape, _jnp.float32)
        out["v_" + name] = (s * s) * _jax.random.uniform(kv, w.shape, _jnp.float32, 0.5, 1.5)
    if N_MICROBATCH > 1:
        for name, axis in PER_EXAMPLE_BATCH_AXIS.items():
            out[name] = _to_microbatches(out[name], axis)
    return {'x': out['x'], 'c': out['c'], 'ctx': out['ctx'], 'c_ctx': out['c_ctx'], 'w_mod': out['w_mod'], 'b_mod': out['b_mod'], 'norm_w': out['norm_w'], 'w_in': out['w_in'], 'q_norm_w': out['q_norm_w'], 'k_norm_w': out['k_norm_w'], 'conv_w': out['conv_w'], 'conv_b': out['conv_b'], 'conv_ln_w': out['conv_ln_w'], 'conv_ln_b': out['conv_ln_b'], 'w_pw': out['w_pw'], 'b_pw': out['b_pw'], 'w_out': out['w_out'], 'loss_target': out['loss_target'], 'm_c_ctx': out['m_c_ctx'], 'm_w_mod': out['m_w_mod'], 'm_b_mod': out['m_b_mod'], 'm_norm_w': out['m_norm_w'], 'm_w_in': out['m_w_in'], 'm_q_norm_w': out['m_q_norm_w'], 'm_k_norm_w': out['m_k_norm_w'], 'm_conv_w': out['m_conv_w'], 'm_conv_b': out['m_conv_b'], 'm_conv_ln_w': out['m_conv_ln_w'], 'm_conv_ln_b': out['m_conv_ln_b'], 'm_w_pw': out['m_w_pw'], 'm_b_pw': out['m_b_pw'], 'm_w_out': out['m_w_out'], 'v_c_ctx': out['v_c_ctx'], 'v_w_mod': out['v_w_mod'], 'v_b_mod': out['v_b_mod'], 'v_norm_w': out['v_norm_w'], 'v_w_in': out['v_w_in'], 'v_q_norm_w': out['v_q_norm_w'], 'v_k_norm_w': out['v_k_norm_w'], 'v_conv_w': out['v_conv_w'], 'v_conv_b': out['v_conv_b'], 'v_conv_ln_w': out['v_conv_ln_w'], 'v_conv_ln_b': out['v_conv_ln_b'], 'v_w_pw': out['v_w_pw'], 'v_b_pw': out['v_b_pw'], 'v_w_out': out['v_w_out']}


def _loss(weights, diff, rest, loss_target):
    with _jax.named_scope("forward"):
        args = {**rest, TWIN_DIFF_INPUT: diff, **{k: w.astype(_WEIGHT_DTYPES[k]) for k, w in weights.items()}}
        y = _forward(args)
    with _jax.named_scope("loss_head"):
        err = _jnp.square(y.astype(_jnp.float32) - loss_target)
        return 0.5 * _jnp.sum(_jnp.mean(err, axis=-1)) if err.ndim else 0.5 * err


def _adamw(w, g, m, v):
    m = ADAM_B1 * m + (1.0 - ADAM_B1) * g
    v = ADAM_B2 * v + (1.0 - ADAM_B2) * _jnp.square(g)
    m_hat = m / (1.0 - ADAM_B1 ** ADAM_STEP)
    v_hat = v / (1.0 - ADAM_B2 ** ADAM_STEP)
    delta = -ADAM_LR * (m_hat / (_jnp.sqrt(v_hat) + ADAM_EPS) + ADAM_WD * w)
    return delta, m, v


def reference(x, c, ctx, c_ctx, w_mod, b_mod, norm_w, w_in, q_norm_w, k_norm_w, conv_w, conv_b, conv_ln_w, conv_ln_b, w_pw, b_pw, w_out, loss_target, m_c_ctx, m_w_mod, m_b_mod, m_norm_w, m_w_in, m_q_norm_w, m_k_norm_w, m_conv_w, m_conv_b, m_conv_ln_w, m_conv_ln_b, m_w_pw, m_b_pw, m_w_out, v_c_ctx, v_w_mod, v_b_mod, v_norm_w, v_w_in, v_q_norm_w, v_k_norm_w, v_conv_w, v_conv_b, v_conv_ln_w, v_conv_ln_b, v_w_pw, v_b_pw, v_w_out):
    given = dict(x=x, c=c, ctx=ctx, c_ctx=c_ctx, w_mod=w_mod, b_mod=b_mod, norm_w=norm_w, w_in=w_in, q_norm_w=q_norm_w, k_norm_w=k_norm_w, conv_w=conv_w, conv_b=conv_b, conv_ln_w=conv_ln_w, conv_ln_b=conv_ln_b, w_pw=w_pw, b_pw=b_pw, w_out=w_out, loss_target=loss_target, m_c_ctx=m_c_ctx, m_w_mod=m_w_mod, m_b_mod=m_b_mod, m_norm_w=m_norm_w, m_w_in=m_w_in, m_q_norm_w=m_q_norm_w, m_k_norm_w=m_k_norm_w, m_conv_w=m_conv_w, m_conv_b=m_conv_b, m_conv_ln_w=m_conv_ln_w, m_conv_ln_b=m_conv_ln_b, m_w_pw=m_w_pw, m_b_pw=m_b_pw, m_w_out=m_w_out, v_c_ctx=v_c_ctx, v_w_mod=v_w_mod, v_b_mod=v_b_mod, v_norm_w=v_norm_w, v_w_in=v_w_in, v_q_norm_w=v_q_norm_w, v_k_norm_w=v_k_norm_w, v_conv_w=v_conv_w, v_conv_b=v_conv_b, v_conv_ln_w=v_conv_ln_w, v_conv_ln_b=v_conv_ln_b, v_w_pw=v_w_pw, v_b_pw=v_b_pw, v_w_out=v_w_out)
    weights = {n: given[n] for n in TWIN_WEIGHTS}
    shared = {n: given[n] for n in SHARED_INPUTS}
    per_example = {n: given[n] for n in ['x', 'c', 'ctx']}
    grad_fn = _jax.value_and_grad(_loss, argnums=(0, 1))

    def one_microbatch(ex, loss_target):
        ex = dict(ex)
        diff = ex.pop(TWIN_DIFF_INPUT)
        return grad_fn(weights, diff, {**shared, **ex}, loss_target)

    if N_MICROBATCH == 1:
        loss, (grad_w, grad_x) = one_microbatch(per_example, given["loss_target"])
    else:
        def body(carry, xs):
            loss_sum, grad_sum = carry
            l_k, (gw_k, gx_k) = one_microbatch(xs[0], xs[1])
            with _jax.named_scope("update"):
                return (loss_sum + l_k, _jax.tree.map(_jnp.add, grad_sum, gw_k)), gx_k

        init = (_jnp.zeros((), _jnp.float32), _jax.tree.map(_jnp.zeros_like, weights))
        (loss, grad_w), grad_x = _jax.lax.scan(body, init, (per_example, given["loss_target"]))
    with _jax.named_scope("update"):
        delta_w, new_m, new_v = {}, {}, {}
        for n in TWIN_WEIGHTS:
            delta_w[n], new_m[n], new_v[n] = _adamw(weights[n], grad_w[n], given["m_" + n], given["v_" + n])
    return (loss, grad_x, *[grad_w[n] for n in TWIN_WEIGHTS], *[delta_w[n] for n in TWIN_WEIGHTS],
            *[new_m[n] for n in TWIN_WEIGHTS], *[new_v[n] for n in TWIN_WEIGHTS])
```

```python
import functools

import jax
import jax.numpy as jnp
from jax import lax
from jax.experimental import pallas as pl
from jax.experimental.pallas import tpu as pltpu

F32, BF16 = jnp.float32, jnp.bfloat16
MESH_ID = pl.DeviceIdType.MESH

N_DEV = 8
D = 1024
D_IN = 2816
DA = 512
DC = 512
HD = 64
KVW = 128
KW = 31
HALO = 16
EPS = 1e-6
ROPE_THETA = 10000.0
GRID_W = 64

ADAM_LR, ADAM_B1, ADAM_B2, ADAM_EPS, ADAM_WD, ADAM_STEP = 0.001, 0.9, 0.999, 1e-08, 0.01, 10

VMEM_LIMIT = 56 * 1024 * 1024

TM = 256
TQ = 128
TC = 512
CH = 64


def _params(sem, vmem=VMEM_LIMIT):
    return pltpu.CompilerParams(dimension_semantics=sem, vmem_limit_bytes=vmem)


def _dot(a, b):
    return jnp.dot(a, b, preferred_element_type=F32)


def _dot_nt(a, b):
    return lax.dot_general(a, b, (((1,), (1,)), ((), ())), preferred_element_type=F32)


def _dot_tn(a, b):
    return lax.dot_general(a, b, (((0,), (0,)), ((), ())), preferred_element_type=F32)


def _sigmoid(z):
    return 1.0 / (1.0 + jnp.exp(-z))


def _segsum(v, ones_bd):
    hi = v.astype(BF16)
    lo = (v - hi.astype(F32)).astype(BF16)
    return _dot(hi, ones_bd) + _dot(lo, ones_bd)


def _swap16(x):
    w = x.shape[-1]
    lane = lax.broadcasted_iota(jnp.int32, x.shape, 1)
    return jnp.where((lane % 32) < 16, pltpu.roll(x, w - 16, 1), pltpu.roll(x, 16, 1))


def _rope(x, cos, sins):
    return x * cos + _swap16(x) * sins


def _rope_bwd(d, cos, sins):
    return d * cos + _swap16(d * sins)


def _adamw(w, g, m, v):
    m2 = ADAM_B1 * m + (1.0 - ADAM_B1) * g
    v2 = ADAM_B2 * v + (1.0 - ADAM_B2) * (g * g)
    m_hat = m2 / (1.0 - ADAM_B1 ** ADAM_STEP)
    v_hat = v2 / (1.0 - ADAM_B2 ** ADAM_STEP)
    delta = -ADAM_LR * (m_hat / (jnp.sqrt(v_hat) + ADAM_EPS) + ADAM_WD * w)
    return delta, m2, v2


def _coords():
    return lax.axis_index("x"), lax.axis_index("y"), lax.axis_index("c")


def _lin(x, y, c):
    return 4 * x + 2 * y + c


def _all_gather_many(arrs, name):
    n = len(arrs)

    def body(*refs):
        in_refs, out_refs = refs[:n], refs[n:2 * n]
        send_sems, recv_sems, local_sems = refs[2 * n:]
        x, y, c = _coords()
        me, sib = (x, y, c), (x, y, 1 - c)
        chips = [(1 - x, y), (x, 1 - y), (1 - x, 1 - y)]

        def copy(a, k, block, to, src=None):
            slot = out_refs[a].at[_lin(*block)]
            return pltpu.make_async_remote_copy(
                src_ref=slot if src is None else src, dst_ref=slot,
                send_sem=send_sems.at[a * 7 + k], recv_sem=recv_sems.at[a * 7 + k],
                device_id=to, device_id_type=MESH_ID)

        mine = [pltpu.make_async_copy(in_refs[a], out_refs[a].at[_lin(*me)], local_sems.at[a]) for a in range(n)]
        for cp in mine:
            cp.start()
        first = []
        for a in range(n):
            first.append(copy(a, 0, me, sib, src=in_refs[a]))
            first += [copy(a, 1 + j, me, (*chip, c), src=in_refs[a]) for j, chip in enumerate(chips)]
        for cp in first:
            cp.start()
        passed = []
        for a in range(n):
            for j, chip in enumerate(chips):
                copy(a, 1 + j, (*chip, c), me).wait_recv()
                fwd = copy(a, 4 + j, (*chip, c), sib)
                fwd.start()
                passed.append(fwd)
        for a in range(n):
            copy(a, 0, sib, me).wait_recv()
            for j, chip in enumerate(chips):
                copy(a, 4 + j, (*chip, 1 - c), me).wait_recv()
        for cp in first + passed:
            cp.wait_send()
        for cp in mine:
            cp.wait()

    vm = pl.BlockSpec(memory_space=pltpu.VMEM)
    return pl.pallas_call(
        body, name=name,
        out_shape=[jax.ShapeDtypeStruct((N_DEV,) + a.shape, a.dtype) for a in arrs],
        in_specs=[vm] * n, out_specs=[vm] * n,
        scratch_shapes=[pltpu.SemaphoreType.DMA((7 * n,)), pltpu.SemaphoreType.DMA((7 * n,)),
                        pltpu.SemaphoreType.DMA((n,))],
        compiler_params=pltpu.CompilerParams(vmem_limit_bytes=VMEM_LIMIT),
    )(*arrs)


def _reduce_scatter_adamw(items, name):
    n = len(items)
    rb = 32

    def body(*refs):
        parts = refs[0:n]
        wmv = refs[n:4 * n]
        outs = refs[4 * n:8 * n]
        recv = refs[8 * n:9 * n]
        send_sems, recv_sems, local_sems = refs[9 * n:]
        x, y, c = _coords()
        me = _lin(x, y, c)

        def remote(a, k):
            px = 1 - x if k & 4 else x
            py = 1 - y if k & 2 else y
            pc = 1 - c if k & 1 else c
            return pltpu.make_async_remote_copy(
                src_ref=parts[a].at[_lin(px, py, pc)], dst_ref=recv[a].at[me],
                send_sem=send_sems.at[a * 7 + k - 1], recv_sem=recv_sems.at[a * 7 + k - 1],
                device_id=(px, py, pc), device_id_type=MESH_ID)

        mine = [pltpu.make_async_copy(parts[a].at[me], recv[a].at[me], local_sems.at[a]) for a in range(n)]
        for cp in mine:
            cp.start()
        sends = [remote(a, k) for a in range(n) for k in range(1, 8)]
        for cp in sends:
            cp.start()
        for cp in sends:
            cp.wait_recv()
        for cp in sends:
            cp.wait_send()
        for cp in mine:
            cp.wait()

        for a in range(n):
            w_ref, m_ref, v_ref = wmv[3 * a:3 * a + 3]
            g_ref, d_ref, nm_ref, nv_ref = outs[4 * a:4 * a + 4]
            rows = w_ref.shape[0]

            def blk(i, carry, a=a, w_ref=w_ref, m_ref=m_ref, v_ref=v_ref, g_ref=g_ref, d_ref=d_ref,
                    nm_ref=nm_ref, nv_ref=nv_ref):
                rs = pl.ds(pl.multiple_of(i * rb, rb), rb)
                g = recv[a][0, rs, :]
                for j in range(1, N_DEV):
                    g = g + recv[a][j, rs, :]
                delta, m2, v2 = _adamw(w_ref[rs, :], g, m_ref[rs, :], v_ref[rs, :])
                g_ref[rs, :] = g
                d_ref[rs, :] = delta
                nm_ref[rs, :] = m2
                nv_ref[rs, :] = v2
                return carry

            lax.fori_loop(0, rows // rb, blk, 0)

    vm = pl.BlockSpec(memory_space=pltpu.VMEM)
    anyspace = pl.BlockSpec(memory_space=pl.ANY)
    args, in_specs, out_shape, scratch = [], [], [], []
    for parts, w, m, v in items:
        assert w.shape[0] % rb == 0 and parts.shape == (N_DEV,) + w.shape
    args += [it[0] for it in items]
    in_specs += [anyspace] * n
    for _, w, m, v in items:
        args += [w, m, v]
        in_specs += [vm] * 3
        out_shape += [jax.ShapeDtypeStruct(w.shape, F32)] * 4
    scratch += [pltpu.VMEM((N_DEV,) + it[1].shape, F32) for it in items]
    scratch += [pltpu.SemaphoreType.DMA((7 * n,)), pltpu.SemaphoreType.DMA((7 * n,)), pltpu.SemaphoreType.DMA((n,))]
    outs = pl.pallas_call(
        body, name=name, out_shape=out_shape, in_specs=in_specs, out_specs=[vm] * (4 * n),
        scratch_shapes=scratch, compiler_params=pltpu.CompilerParams(vmem_limit_bytes=VMEM_LIMIT),
    )(*args)
    return [tuple(outs[4 * a:4 * a + 4]) for a in range(n)]


def _mod_fwd(c_rows, w_mod_loc, b_mod_loc):
    def body(c_ref, w_ref, b_ref, o_ref):
        cr = c_ref[...]
        a = (cr * _sigmoid(cr)).astype(BF16)
        o_ref[...] = _dot(a, w_ref[...].astype(BF16)) + b_ref[...]

    return pl.pallas_call(
        body, name="mod_fwd", out_shape=jax.ShapeDtypeStruct((c_rows.shape[0], w_mod_loc.shape[1]), F32),
        compiler_params=pltpu.CompilerParams(vmem_limit_bytes=VMEM_LIMIT),
    )(c_rows, w_mod_loc, b_mod_loc)


def _mod_bwd(c_rows, dmod_loc, w_mod_loc, m, v):
    def body(c_ref, dm_ref, w_ref, m_ref, v_ref, g_ref, d_ref, nm_ref, nv_ref, gc_ref):
        cr = c_ref[...]
        a = (cr * _sigmoid(cr)).astype(BF16)
        dm = dm_ref[...].astype(BF16)
        g = _dot_tn(a, dm)
        w = w_ref[...]
        delta, m2, v2 = _adamw(w, g, m_ref[...], v_ref[...])
        g_ref[...] = g
        d_ref[...] = delta
        nm_ref[...] = m2
        nv_ref[...] = v2
        gc_ref[...] = _dot_nt(dm[16:24, :], w.astype(BF16))

    shp = jax.ShapeDtypeStruct(w_mod_loc.shape, F32)
    return pl.pallas_call(
        body, name="mod_bwd", out_shape=[shp, shp, shp, shp, jax.ShapeDtypeStruct((8, D), F32)],
        compiler_params=pltpu.CompilerParams(vmem_limit_bytes=VMEM_LIMIT),
    )(c_rows, dmod_loc, w_mod_loc, m, v)


def _fwd_in(x, modrows, norm_w, w_in_b, cos, sins, qnw_t, knw_t, ones_bd):
    bl, s, _ = x.shape
    nt = s // TM

    def body(x_ref, mod_ref, nw_ref, win_ref, cos_ref, sin_ref, qnw_ref, knw_ref, bd_ref,
             q_ref, k_ref, v_ref, pq_ref, pkv_ref, za_ref, glu_ref, zc_ref):
        xv = x_ref[0]
        shift = mod_ref[0, 0:1, :]
        scale = mod_ref[0, 1:2, :]
        r = lax.rsqrt(jnp.mean(xv * xv, axis=-1, keepdims=True) + EPS)
        u = (xv * r * nw_ref[...]) * (1.0 + scale) + shift
        p = _dot(u.astype(BF16), win_ref[...])
        pq = p[:, 0:DA]
        pk = p[:, DA:DA + HD * 2]
        cs = cos_ref[...]
        sn = sin_ref[...]
        rq = lax.rsqrt(_segsum(pq * pq, bd_ref[...]) * (1.0 / HD) + EPS)
        qn = pq * rq * qnw_ref[...]
        q_ref[0] = (_rope(qn, cs, sn) * 0.125).astype(BF16)
        rk = lax.rsqrt(_segsum(pk * pk, bd_ref[0:KVW, 0:KVW]) * (1.0 / HD) + EPS)
        kn = pk * rk * knw_ref[...]
        k_ref[0] = _rope(kn, cs[:, 0:KVW], sn[:, 0:KVW]).astype(BF16)
        v_ref[0] = p[:, 640:768].astype(BF16)
        pq_ref[0] = pq
        pkv_ref[0] = p[:, 512:768]
        za_ref[0] = p[:, 768:1280]
        glu_ref[0] = p[:, 1280:2304]
        zc_ref[0] = p[:, 2304:2816]

    def tile(w):
        return pl.BlockSpec((1, TM, w), lambda b, i: (b, i, 0))

    def const(shape):
        return pl.BlockSpec(shape, lambda b, i: (0,) * len(shape))

    outs = [(DA, BF16), (KVW, BF16), (KVW, BF16), (DA, F32), (2 * KVW, F32), (DA, F32), (2 * DC, F32), (DC, F32)]
    return pl.pallas_call(
        body, name="fwd_in", grid=(bl, nt),
        in_specs=[tile(D), pl.BlockSpec((1, 3, D), lambda b, i: (b, 0, 0)), const((1, D)), const((D, D_IN)),
                  pl.BlockSpec((TM, DA), lambda b, i: (i, 0)), pl.BlockSpec((TM, DA), lambda b, i: (i, 0)),
                  const((1, DA)), const((1, KVW)), const((DA, DA))],
        out_specs=[tile(w) for w, _ in outs],
        out_shape=[jax.ShapeDtypeStruct((bl, s, w), dt) for w, dt in outs],
        compiler_params=_params(("arbitrary", "arbitrary")),
    )(x, modrows, norm_w, w_in_b, cos, sins, qnw_t, knw_t, ones_bd)


def _ctx_fwd(ctx, modc, norm_w, w_kv_b, knw_t, ones_bd):
    bl, cl, _ = ctx.shape

    def body(x_ref, mod_ref, nw_ref, w_ref, knw_ref, bd_ref, k_ref, v_ref, pkv_ref):
        xv = x_ref[0]
        shift = mod_ref[0, 0:1, :]
        scale = mod_ref[0, 1:2, :]
        r = lax.rsqrt(jnp.mean(xv * xv, axis=-1, keepdims=True) + EPS)
        u = (xv * r * nw_ref[...]) * (1.0 + scale) + shift
        p = _dot(u.astype(BF16), w_ref[...])
        pk = p[:, 0:KVW]
        rk = lax.rsqrt(_segsum(pk * pk, bd_ref[...]) * (1.0 / HD) + EPS)
        k_ref[0] = (pk * rk * knw_ref[...]).astype(BF16)
        v_ref[0] = p[:, KVW:2 * KVW].astype(BF16)
        pkv_ref[0] = p

    def const(shape):
        return pl.BlockSpec(shape, lambda b: (0,) * len(shape))

    def tile(w):
        return pl.BlockSpec((1, cl, w), lambda b: (b, 0, 0))

    return pl.pallas_call(
        body, name="ctx_fwd", grid=(bl,),
        in_specs=[tile(D), const((1, 3, D)), const((1, D)), const((D, 2 * KVW)), const((1, KVW)), const((KVW, KVW))],
        out_specs=[tile(KVW), tile(KVW), tile(2 * KVW)],
        out_shape=[jax.ShapeDtypeStruct((bl, cl, KVW), BF16), jax.ShapeDtypeStruct((bl, cl, KVW), BF16),
                   jax.ShapeDtypeStruct((bl, cl, 2 * KVW), F32)],
        compiler_params=_params(("arbitrary",)),
    )(ctx, modc, norm_w, w_kv_b, knw_t, ones_bd)


def _attn_specs(n_keys):
    qs = pl.BlockSpec((1, 4, TQ, HD), lambda b, g, i: (b, g, i, 0))
    ks = pl.BlockSpec((1, 1, n_keys, HD), lambda b, g, i: (b, g, 0, 0))
    return qs, ks


def _softmax_rows(q, k):
    s = _dot_nt(q, k)
    m = jnp.max(s, axis=-1, keepdims=True)
    e = jnp.exp(s - m)
    return e / jnp.sum(e, axis=-1, keepdims=True)


def _attn_fwd(q, k, v):
    bl, _, s, _ = q.shape
    n_keys = k.shape[2]

    def body(q_ref, k_ref, v_ref, o_ref):
        qv = q_ref[0].reshape(4 * TQ, HD)
        p = _softmax_rows(qv, k_ref[0, 0])
        o_ref[0] = _dot(p.astype(BF16), v_ref[0, 0]).reshape(4, TQ, HD)

    qs, ks = _attn_specs(n_keys)
    return pl.pallas_call(
        body, name="attn_fwd", grid=(bl, 2, s // TQ), in_specs=[qs, ks, ks], out_specs=qs,
        out_shape=jax.ShapeDtypeStruct(q.shape, F32),
        compiler_params=_params(("arbitrary", "arbitrary", "arbitrary")),
    )(q, k, v)


def _attn_bwd(q, k, v, do):
    bl, _, s, _ = q.shape
    n_keys = k.shape[2]

    def body(q_ref, k_ref, v_ref, do_ref, dq_ref, dk_ref, dv_ref):
        @pl.when(pl.program_id(2) == 0)
        def _():
            dk_ref[...] = jnp.zeros_like(dk_ref)
            dv_ref[...] = jnp.zeros_like(dv_ref)

        qv = q_ref[0].reshape(4 * TQ, HD)
        dov = do_ref[0].reshape(4 * TQ, HD)
        kv = k_ref[0, 0]
        p = _softmax_rows(qv, kv)
        dp = _dot_nt(dov, v_ref[0, 0])
        delta = jnp.sum(p * dp, axis=-1, keepdims=True)
        ds = (p * (dp - delta)).astype(BF16)
        dq_ref[0] = (_dot(ds, kv) * 0.125).reshape(4, TQ, HD)
        dk_ref[0, 0] += _dot_tn(ds, qv)
        dv_ref[0, 0] += _dot_tn(p.astype(BF16), dov)

    qs, ks = _attn_specs(n_keys)
    return pl.pallas_call(
        body, name="attn_bwd", grid=(bl, 2, s // TQ), in_specs=[qs, ks, ks, qs], out_specs=[qs, ks, ks],
        out_shape=[jax.ShapeDtypeStruct(q.shape, F32), jax.ShapeDtypeStruct(k.shape, F32),
                   jax.ShapeDtypeStruct(k.shape, F32)],
        compiler_params=_params(("arbitrary", "arbitrary", "arbitrary")),
    )(q, k, v, do)


def _halo_specs(width, s):
    per = TC // HALO
    last = s // HALO - 1
    main = pl.BlockSpec((1, TC, width), lambda b, i: (b, i, 0))
    prev = pl.BlockSpec((1, HALO, width), lambda b, i: (b, jnp.maximum(i * per - 1, 0), 0))
    nxt = pl.BlockSpec((1, HALO, width), lambda b, i: (b, jnp.minimum((i + 1) * per, last), 0))
    return main, prev, nxt


def _glu(g):
    return g[:, 0:DC] * _sigmoid(g[:, DC:2 * DC])


def _fill_padded(pad_ref, main, prev, nxt, first, last):
    pad_ref[0:HALO, :] = jnp.where(first, 0.0, prev)
    pad_ref[HALO:HALO + TC, :] = main
    pad_ref[HALO + TC:2 * HALO + TC, :] = jnp.where(last, 0.0, nxt)


def _conv_fwd(glu, conv_w, conv_b, ln_w, ln_b, w_pw_b, b_pw):
    bl, s, _ = glu.shape
    nt = s // TC

    def body(g_ref, gp_ref, gn_ref, cw_ref, cb_ref, lw_ref, lb_ref, wpw_ref, bpw_ref, y_ref, cp_ref, pad_ref):
        i = pl.program_id(1)
        _fill_padded(pad_ref, _glu(g_ref[0]), _glu(gp_ref[0]), _glu(gn_ref[0]), i == 0, i == nt - 1)
        for ck in range(TC // CH):
            acc = jnp.zeros((CH, DC), F32) + cb_ref[...]
            for t in range(KW):
                acc = acc + pad_ref[pl.ds(ck * CH + 1 + t, CH), :] * cw_ref[t:t + 1, :]
            y_ref[0, pl.ds(ck * CH, CH), :] = acc
        y = y_ref[0]
        mu = jnp.mean(y, axis=-1, keepdims=True)
        yc = y - mu
        var = jnp.mean(yc * yc, axis=-1, keepdims=True)
        z = yc * lax.rsqrt(var + EPS) * lw_ref[...] + lb_ref[...]
        act = z * _sigmoid(z)
        cp_ref[0] = _dot(act.astype(BF16), wpw_ref[...]) + bpw_ref[...]

    def const(shape):
        return pl.BlockSpec(shape, lambda b, i: (0,) * len(shape))

    main, prev, nxt = _halo_specs(2 * DC, s)
    tile = pl.BlockSpec((1, TC, DC), lambda b, i: (b, i, 0))
    return pl.pallas_call(
        body, name="conv_fwd", grid=(bl, nt),
        in_specs=[main, prev, nxt, const((32, DC)), const((1, DC)), const((1, DC)), const((1, DC)),
                  const((DC, DC)), const((1, DC))],
        out_specs=[tile, tile],
        out_shape=[jax.ShapeDtypeStruct((bl, s, DC), F32)] * 2,
        scratch_shapes=[pltpu.VMEM((TC + 2 * HALO, DC), F32)],
        compiler_params=_params(("arbitrary", "arbitrary")),
    )(glu, glu, glu, conv_w, conv_b, ln_w, ln_b, w_pw_b, b_pw)


def _conv_bwd_pointwise(y, dcp, ln_w, ln_b, w_pw_b):
    bl, s, _ = y.shape
    nt = s // TM

    def body(y_ref, dcp_ref, lw_ref, lb_ref, wpw_ref, dy_ref, gw_ref, rows_ref):
        @pl.when((pl.program_id(0) == 0) & (pl.program_id(1) == 0))
        def _():
            gw_ref[...] = jnp.zeros_like(gw_ref)
            rows_ref[...] = jnp.zeros_like(rows_ref)

        y = y_ref[0]
        dcp = dcp_ref[0]
        mu = jnp.mean(y, axis=-1, keepdims=True)
        yc = y - mu
        rstd = lax.rsqrt(jnp.mean(yc * yc, axis=-1, keepdims=True) + EPS)
        yn = yc * rstd
        lw = lw_ref[...]
        z = yn * lw + lb_ref[...]
        sg = _sigmoid(z)
        act = z * sg
        dcp_b = dcp.astype(BF16)
        gw_ref[...] += _dot_tn(act.astype(BF16), dcp_b)
        dact = _dot_nt(dcp_b, wpw_ref[...])
        dz = dact * (sg * (1.0 + z * (1.0 - sg)))
        dyn = dz * lw
        dy = rstd * (dyn - jnp.mean(dyn, axis=-1, keepdims=True) - yn * jnp.mean(dyn * yn, axis=-1, keepdims=True))
        dy_ref[0] = dy
        rows_ref[0:1, :] += jnp.sum(dcp, axis=0, keepdims=True)
        rows_ref[1:2, :] += jnp.sum(dz * yn, axis=0, keepdims=True)
        rows_ref[2:3, :] += jnp.sum(dz, axis=0, keepdims=True)
        rows_ref[3:4, :] += jnp.sum(dy, axis=0, keepdims=True)

    def const(shape):
        return pl.BlockSpec(shape, lambda b, i: (0,) * len(shape))

    tile = pl.BlockSpec((1, TM, DC), lambda b, i: (b, i, 0))
    return pl.pallas_call(
        body, name="conv_bwd_pointwise", grid=(bl, nt),
        in_specs=[tile, tile, const((1, DC)), const((1, DC)), const((DC, DC))],
        out_specs=[tile, const((DC, DC)), const((8, DC))],
        out_shape=[jax.ShapeDtypeStruct((bl, s, DC), F32), jax.ShapeDtypeStruct((DC, DC), F32),
                   jax.ShapeDtypeStruct((8, DC), F32)],
        compiler_params=_params(("arbitrary", "arbitrary")),
    )(y, dcp, ln_w, ln_b, w_pw_b)


def _conv_bwd_depthwise(glu, dy, conv_w):
    bl, s, _ = glu.shape
    nt = s // TC

    def body(g_ref, gp_ref, gn_ref, d_ref, dp_ref, dn_ref, cw_ref, dglu_ref, dcw_ref, padu_ref, padd_ref):
        i = pl.program_id(1)

        @pl.when((pl.program_id(0) == 0) & (i == 0))
        def _():
            dcw_ref[...] = jnp.zeros_like(dcw_ref)

        first, last = i == 0, i == nt - 1
        _fill_padded(padu_ref, _glu(g_ref[0]), _glu(gp_ref[0]), _glu(gn_ref[0]), first, last)
        _fill_padded(padd_ref, d_ref[0], dp_ref[0], dn_ref[0], first, last)
        for ck in range(TC // CH):
            acc = jnp.zeros((CH, DC), F32)
            for t in range(KW):
                acc = acc + padd_ref[pl.ds(ck * CH + 2 * HALO - 1 - t, CH), :] * cw_ref[t:t + 1, :]
            g = g_ref[0, pl.ds(ck * CH, CH), :]
            a = g[:, 0:DC]
            sg = _sigmoid(g[:, DC:2 * DC])
            dglu_ref[0, pl.ds(ck * CH, CH), 0:DC] = acc * sg
            dglu_ref[0, pl.ds(ck * CH, CH), DC:2 * DC] = acc * a * sg * (1.0 - sg)
        for t in range(KW):
            acc8 = jnp.zeros((8, DC), F32)
            for ck in range(TC // CH):
                prod = padu_ref[pl.ds(ck * CH + 1 + t, CH), :] * d_ref[0, pl.ds(ck * CH, CH), :]
                acc8 = acc8 + jnp.sum(prod.reshape(CH // 8, 8, DC), axis=0)
            dcw_ref[t:t + 1, :] += jnp.sum(acc8, axis=0, keepdims=True)

    gmain, gprev, gnext = _halo_specs(2 * DC, s)
    dmain, dprev, dnext = _halo_specs(DC, s)
    cw = pl.BlockSpec((32, DC), lambda b, i: (0, 0))
    return pl.pallas_call(
        body, name="conv_bwd_depthwise", grid=(bl, nt),
        in_specs=[gmain, gprev, gnext, dmain, dprev, dnext, cw],
        out_specs=[gmain, cw],
        out_shape=[jax.ShapeDtypeStruct((bl, s, 2 * DC), F32), jax.ShapeDtypeStruct((32, DC), F32)],
        scratch_shapes=[pltpu.VMEM((TC + 2 * HALO, DC), F32)] * 2,
        compiler_params=_params(("arbitrary", "arbitrary")),
    )(glu, glu, glu, dy, dy, dy, conv_w)


def _out_fwd_bwd(attn, za, cp, zc, x, target, modrows, w_out_b):
    bl, s, _ = x.shape
    nt = s // TM

    def body(o_ref, za_ref, cp_ref, zc_ref, x_ref, t_ref, mod_ref, w_ref,
             do_ref, dza_ref, dcp_ref, dzc_ref, dh_ref, dgate_ref, gw_ref, loss_ref):
        b, i = pl.program_id(0), pl.program_id(1)

        @pl.when((b == 0) & (i == 0))
        def _():
            gw_ref[...] = jnp.zeros_like(gw_ref)
            loss_ref[...] = jnp.zeros_like(loss_ref)

        @pl.when(i == 0)
        def _():
            dgate_ref[...] = jnp.zeros_like(dgate_ref)

        o, za_v, cp_v, zc_v = o_ref[0], za_ref[0], cp_ref[0], zc_ref[0]
        gate = mod_ref[0, 2:3, :]
        sa = _sigmoid(za_v)
        sc = _sigmoid(zc_v)
        silu_a = za_v * sa
        silu_c = zc_v * sc
        mix = jnp.concatenate([(o * silu_a).astype(BF16), (cp_v * silu_c).astype(BF16)], axis=-1)
        w = w_ref[...]
        out = _dot(mix, w)
        err = x_ref[0] + gate * out - t_ref[0]
        loss_ref[...] += jnp.sum(err * err, axis=0, keepdims=True)
        dh = err * (1.0 / D)
        dh_ref[0] = dh
        dgate_ref[0] += jnp.sum(dh * out, axis=0, keepdims=True)
        dout = (dh * gate).astype(BF16)
        gw_ref[...] += _dot_tn(mix, dout)
        dmix = _dot_nt(dout, w)
        dga = dmix[:, 0:DA]
        dgc = dmix[:, DA:DA + DC]
        do_ref[0] = (dga * silu_a).astype(BF16)
        dza_ref[0] = dga * o * (sa * (1.0 + za_v * (1.0 - sa)))
        dcp_ref[0] = dgc * silu_c
        dzc_ref[0] = dgc * cp_v * (sc * (1.0 + zc_v * (1.0 - sc)))

    def const(shape):
        return pl.BlockSpec(shape, lambda b, i: (0,) * len(shape))

    def tile(w):
        return pl.BlockSpec((1, TM, w), lambda b, i: (b, i, 0))

    return pl.pallas_call(
        body, name="out_fwd_bwd", grid=(bl, nt),
        in_specs=[tile(DA), tile(DA), tile(DC), tile(DC), tile(D), tile(D),
                  pl.BlockSpec((1, 3, D), lambda b, i: (b, 0, 0)), const((D, D))],
        out_specs=[tile(DA), tile(DA), tile(DC), tile(DC), tile(D),
                   pl.BlockSpec((1, 1, D), lambda b, i: (b, 0, 0)), const((D, D)), const((1, D))],
        out_shape=[jax.ShapeDtypeStruct((bl, s, DA), BF16), jax.ShapeDtypeStruct((bl, s, DA), F32),
                   jax.ShapeDtypeStruct((bl, s, DC), F32), jax.ShapeDtypeStruct((bl, s, DC), F32),
                   jax.ShapeDtypeStruct((bl, s, D), F32), jax.ShapeDtypeStruct((bl, 1, D), F32),
                   jax.ShapeDtypeStruct((D, D), F32), jax.ShapeDtypeStruct((1, D), F32)],
        compiler_params=_params(("arbitrary", "arbitrary")),
    )(attn, za, cp, zc, x, target, modrows, w_out_b)


def _rms_heads_bwd(dy, x, w_t, ones_bd):
    r = lax.rsqrt(_segsum(x * x, ones_bd) * (1.0 / HD) + EPS)
    xh = x * r
    g = dy * w_t
    dx = r * (g - xh * (_segsum(g * xh, ones_bd) * (1.0 / HD)))
    return dx, dy * xh


def _ctx_bwd(ctx, modc, norm_w, w_kv_b, pkv_c, dk_c, dv_c, knw_t, ones_bd):
    bl, cl, _ = ctx.shape

    def body(x_ref, mod_ref, nw_ref, w_ref, p_ref, dk_ref, dv_ref, knw_ref, bd_ref, gw_ref, rows_ref, dknw_ref):
        @pl.when(pl.program_id(0) == 0)
        def _():
            gw_ref[...] = jnp.zeros_like(gw_ref)
            rows_ref[...] = jnp.zeros_like(rows_ref)
            dknw_ref[...] = jnp.zeros_like(dknw_ref)

        xv = x_ref[0]
        shift = mod_ref[0, 0:1, :]
        scale = mod_ref[0, 1:2, :]
        nw = nw_ref[...]
        r = lax.rsqrt(jnp.mean(xv * xv, axis=-1, keepdims=True) + EPS)
        xn = xv * r
        yv = xn * nw
        u = yv * (1.0 + scale) + shift
        dpk, dknw = _rms_heads_bwd(dk_ref[0], p_ref[0][:, 0:KVW], knw_ref[...], bd_ref[...])
        dp = jnp.concatenate([dpk.astype(BF16), dv_ref[0].astype(BF16)], axis=-1)
        gw_ref[...] += _dot_tn(u.astype(BF16), dp)
        du = _dot_nt(dp, w_ref[...])
        rows_ref[0:1, :] += jnp.sum(du, axis=0, keepdims=True)
        rows_ref[1:2, :] += jnp.sum(du * yv, axis=0, keepdims=True)
        rows_ref[2:3, :] += jnp.sum(du * (1.0 + scale) * xn, axis=0, keepdims=True)
        dknw_ref[...] += jnp.sum(dknw, axis=0, keepdims=True)

    def const(shape):
        return pl.BlockSpec(shape, lambda b: (0,) * len(shape))

    def tile(w):
        return pl.BlockSpec((1, cl, w), lambda b: (b, 0, 0))

    return pl.pallas_call(
        body, name="ctx_bwd", grid=(bl,),
        in_specs=[tile(D), const((1, 3, D)), const((1, D)), const((D, 2 * KVW)), tile(2 * KVW), tile(KVW), tile(KVW),
                  const((1, KVW)), const((KVW, KVW))],
        out_specs=[const((D, 2 * KVW)), const((8, D)), const((1, KVW))],
        out_shape=[jax.ShapeDtypeStruct((D, 2 * KVW), F32), jax.ShapeDtypeStruct((8, D), F32),
                   jax.ShapeDtypeStruct((1, KVW), F32)],
        compiler_params=_params(("arbitrary",)),
    )(ctx, modc, norm_w, w_kv_b, pkv_c, dk_c, dv_c, knw_t, ones_bd)


def _bwd_in(x, modrows, norm_w, w_in_b, cos, sins, qnw_t, knw_t, ones_bd,
            pq, pkv, dq, dk, dv, dza, dglu, dzc, dh, gw_kv):
    bl, s, _ = x.shape
    nt = s // TM

    def body(x_ref, mod_ref, nw_ref, win_hbm, cos_ref, sin_ref, qnw_ref, knw_ref, bd_ref,
             pq_ref, pkv_ref, dq_ref, dk_ref, dv_ref, dza_ref, dglu_ref, dzc_ref, dh_ref, gwkv_ref,
             gx_ref, gw_hbm, dmod_ref, dnw_ref, dqnw_ref, dknw_ref, win_ref, gw_acc, sem):
        b, i = pl.program_id(0), pl.program_id(1)

        @pl.when((b == 0) & (i == 0))
        def _():
            cp = pltpu.make_async_copy(win_hbm, win_ref, sem)
            cp.start()
            gw_acc[...] = jnp.zeros_like(gw_acc)
            dnw_ref[...] = jnp.zeros_like(dnw_ref)
            dqnw_ref[...] = jnp.zeros_like(dqnw_ref)
            dknw_ref[...] = jnp.zeros_like(dknw_ref)
            cp.wait()

        @pl.when(i == 0)
        def _():
            dmod_ref[...] = jnp.zeros_like(dmod_ref)

        cs = cos_ref[...]
        sn = sin_ref[...]
        bd = bd_ref[...]
        dqn = _rope_bwd(dq_ref[0], cs, sn)
        dpq, dqnw = _rms_heads_bwd(dqn, pq_ref[0], qnw_ref[...], bd)
        dkn = _rope_bwd(dk_ref[0], cs[:, 0:KVW], sn[:, 0:KVW])
        dpk, dknw = _rms_heads_bwd(dkn, pkv_ref[0][:, 0:KVW], knw_ref[...], bd[0:KVW, 0:KVW])
        dqnw_ref[...] += jnp.sum(dqnw, axis=0, keepdims=True)
        dknw_ref[...] += jnp.sum(dknw, axis=0, keepdims=True)
        dp = jnp.concatenate(
            [dpq.astype(BF16), dpk.astype(BF16), dv_ref[0].astype(BF16), dza_ref[0].astype(BF16),
             dglu_ref[0].astype(BF16), dzc_ref[0].astype(BF16)], axis=-1)

        xv = x_ref[0]
        shift = mod_ref[0, 0:1, :]
        scale = mod_ref[0, 1:2, :]
        nw = nw_ref[...]
        r = lax.rsqrt(jnp.mean(xv * xv, axis=-1, keepdims=True) + EPS)
        xn = xv * r
        yv = xn * nw
        u = yv * (1.0 + scale) + shift
        gw_acc[...] += _dot_tn(u.astype(BF16), dp)
        du = _dot_nt(dp, win_ref[...])
        dmod_ref[0, 0:1, :] += jnp.sum(du, axis=0, keepdims=True)
        dmod_ref[0, 1:2, :] += jnp.sum(du * yv, axis=0, keepdims=True)
        dy = du * (1.0 + scale)
        dnw_ref[...] += jnp.sum(dy * xn, axis=0, keepdims=True)
        dxn = dy * nw
        gx_ref[0] = dh_ref[0] + r * (dxn - xn * jnp.mean(dxn * xn, axis=-1, keepdims=True))

        @pl.when((b == bl - 1) & (i == nt - 1))
        def _():
            gw_acc[:, 512:768] += gwkv_ref[...]
            pltpu.sync_copy(gw_acc, gw_hbm)

    def tile(w):
        return pl.BlockSpec((1, TM, w), lambda b, i: (b, i, 0))

    def const(shape):
        return pl.BlockSpec(shape, lambda b, i: (0,) * len(shape))

    anyspace = pl.BlockSpec(memory_space=pl.ANY)
    rope = pl.BlockSpec((TM, DA), lambda b, i: (i, 0))
    return pl.pallas_call(
        body, name="bwd_in", grid=(bl, nt),
        in_specs=[tile(D), pl.BlockSpec((1, 3, D), lambda b, i: (b, 0, 0)), const((1, D)), anyspace, rope, rope,
                  const((1, DA)), const((1, KVW)), const((DA, DA)),
                  tile(DA), tile(2 * KVW), tile(DA), tile(KVW), tile(KVW), tile(DA), tile(2 * DC), tile(DC), tile(D),
                  const((D, 2 * KVW))],
        out_specs=[tile(D), anyspace, pl.BlockSpec((1, 2, D), lambda b, i: (b, 0, 0)), const((1, D)),
                   const((1, DA)), const((1, KVW))],
        out_shape=[jax.ShapeDtypeStruct((bl, s, D), F32), jax.ShapeDtypeStruct((D, D_IN), F32),
                   jax.ShapeDtypeStruct((bl, 2, D), F32), jax.ShapeDtypeStruct((1, D), F32),
                   jax.ShapeDtypeStruct((1, DA), F32), jax.ShapeDtypeStruct((1, KVW), F32)],
        scratch_shapes=[pltpu.VMEM((D, D_IN), BF16), pltpu.VMEM((D, D_IN), F32), pltpu.SemaphoreType.DMA],
        compiler_params=_params(("arbitrary", "arbitrary")),
    )(x, modrows, norm_w, w_in_b, cos, sins, qnw_t, knw_t, ones_bd,
      pq, pkv, dq, dk, dv, dza, dglu, dzc, dh, gw_kv)


def _sum_devices(gathered):
    def body(g_ref, o_ref):
        acc = g_ref[0]
        for j in range(1, N_DEV):
            acc = acc + g_ref[j]
        o_ref[...] = acc

    return pl.pallas_call(
        body, name="sum_devices", out_shape=jax.ShapeDtypeStruct(gathered.shape[1:], F32),
    )(gathered)


def _mod_small_grads(dmod_full, gc_parts, c_ctx_row):
    def body(dm_ref, gc_ref, c_ref, gb_ref, gcc_ref):
        gb_ref[...] = jnp.sum(dm_ref[...], axis=0, keepdims=True)
        acc = gc_ref[0, 0:1, :]
        for j in range(1, N_DEV):
            acc = acc + gc_ref[j, 0:1, :]
        cv = c_ref[...]
        sg = _sigmoid(cv)
        gcc_ref[...] = acc * (sg * (1.0 + cv * (1.0 - sg)))

    return pl.pallas_call(
        body, name="mod_small_grads",
        out_shape=[jax.ShapeDtypeStruct((1, 3 * D), F32), jax.ShapeDtypeStruct((1, D), F32)],
    )(dmod_full, gc_parts, c_ctx_row)


def _adamw_packed(w, g, m, v):
    def body(w_ref, g_ref, m_ref, v_ref, d_ref, nm_ref, nv_ref):
        delta, m2, v2 = _adamw(w_ref[...], g_ref[...], m_ref[...], v_ref[...])
        d_ref[...] = delta
        nm_ref[...] = m2
        nv_ref[...] = v2

    shp = jax.ShapeDtypeStruct(w.shape, F32)
    return pl.pallas_call(body, name="adamw_small", out_shape=[shp, shp, shp])(w, g, m, v)


_SMALL = (("c_ctx", 1024), ("b_mod", 3072), ("norm_w", 1024), ("q_norm_w", 64), ("k_norm_w", 64),
          ("conv_b", 512), ("conv_ln_w", 512), ("conv_ln_b", 512), ("b_pw", 512))


def _rows_of(n):
    return -(-n // 1024) * 8


def _pack(vectors):
    rows = []
    for name, n in _SMALL:
        flat = vectors[name].reshape(-1).astype(F32)
        rows.append(jnp.pad(flat, (0, _rows_of(n) * 128 - n)).reshape(_rows_of(n), 128))
    return jnp.concatenate(rows, axis=0)


def _unpack(packed):
    out, r0 = {}, 0
    for name, n in _SMALL:
        out[name] = packed[r0:r0 + _rows_of(n)].reshape(-1)[:n]
        r0 += _rows_of(n)
    return out


def _rope_tables(s):
    t = jnp.arange(s, dtype=jnp.int32)
    row = (t // GRID_W).astype(F32)
    col = (t % GRID_W).astype(F32)
    freqs = ROPE_THETA ** (-jnp.arange(0, HD // 2, 2, dtype=F32) / (HD // 2))
    ang_r = row[:, None] * freqs[None, :]
    ang_c = col[:, None] * freqs[None, :]
    cr, sr, cc, sc = jnp.cos(ang_r), jnp.sin(ang_r), jnp.cos(ang_c), jnp.sin(ang_c)
    cos = jnp.concatenate([cr, cr, cc, cc], axis=-1)
    sins = jnp.concatenate([-sr, sr, -sc, sc], axis=-1)
    return jnp.tile(cos, (1, DA // HD)), jnp.tile(sins, (1, DA // HD))


def _heads(a, n_heads):
    b, s, _ = a.shape
    return a.reshape(b, s, n_heads, HD).transpose(0, 2, 1, 3)


def _unheads(a):
    b, h, s, _ = a.shape
    return a.transpose(0, 2, 1, 3).reshape(b, s, h * HD)


def kernel(x, c, ctx, c_ctx, w_mod, b_mod, norm_w, w_in, q_norm_w, k_norm_w, conv_w, conv_b, conv_ln_w, conv_ln_b, w_pw, b_pw, w_out, loss_target, m_c_ctx, m_w_mod, m_b_mod, m_norm_w, m_w_in, m_q_norm_w, m_k_norm_w, m_conv_w, m_conv_b, m_conv_ln_w, m_conv_ln_b, m_w_pw, m_b_pw, m_w_out, v_c_ctx, v_w_mod, v_b_mod, v_norm_w, v_w_in, v_q_norm_w, v_k_norm_w, v_conv_w, v_conv_b, v_conv_ln_w, v_conv_ln_b, v_w_pw, v_b_pw, v_w_out):
    bl, s, _ = x.shape
    cl = ctx.shape[1]
    me = _lin(*_coords())
    n_mod = w_mod.shape[2]

    conv_w_pad = jnp.pad(conv_w[0], ((0, 32 - KW), (0, 0)))
    c_pad = jnp.pad(c, ((0, 8 - bl), (0, 0)))
    g_win, g_wout, g_wpw, g_cw, g_c = _all_gather_many(
        [w_in[0].astype(BF16), w_out[0].astype(BF16), w_pw[0].astype(BF16), conv_w_pad, c_pad], "gather_weights")
    w_in_b = g_win.transpose(1, 0, 2).reshape(D, D_IN)
    w_out_b = g_wout.reshape(D, D)
    w_pw_b = g_wpw.reshape(DC, DC)
    conv_w_full = g_cw.transpose(1, 0, 2).reshape(32, DC)
    c_all = g_c[:, :bl, :].reshape(N_DEV * bl, D)
    n_ex = N_DEV * bl
    c_rows = jnp.concatenate([c_all, c_ctx[None, :], jnp.zeros((7, D), F32)], axis=0)

    b_mod_loc = lax.dynamic_slice_in_dim(b_mod, me * n_mod, n_mod, axis=1)
    mod_loc = _mod_fwd(c_rows, w_mod[0], b_mod_loc)
    (g_mod,) = _all_gather_many([mod_loc], "gather_mod")
    mod_all = g_mod.transpose(1, 0, 2).reshape(n_ex + 8, 3 * D)
    modrows = lax.dynamic_slice_in_dim(mod_all, me * bl, bl, axis=0).reshape(bl, 3, D)
    modc = mod_all[n_ex].reshape(1, 3, D)

    cos, sins = _rope_tables(s)
    qnw_t = jnp.tile(q_norm_w, (1, DA // HD))
    knw_t = jnp.tile(k_norm_w, (1, KVW // HD))
    lane = jnp.arange(DA, dtype=jnp.int32) // HD
    ones_bd = (lane[:, None] == lane[None, :]).astype(BF16)
    ones_kv = ones_bd[0:KVW, 0:KVW]
    w_kv_b = w_in_b[:, 512:768]

    q, k_lat, v_lat, pq, pkv, za, glu, zc = _fwd_in(x, modrows, norm_w, w_in_b, cos, sins, qnw_t, knw_t, ones_bd)
    k_c, v_c, pkv_c = _ctx_fwd(ctx, modc, norm_w, w_kv_b, knw_t, ones_kv)
    q_h = _heads(q, 8)
    k_h = _heads(jnp.concatenate([k_c, k_lat], axis=1), 2)
    v_h = _heads(jnp.concatenate([v_c, v_lat], axis=1), 2)
    attn = _unheads(_attn_fwd(q_h, k_h, v_h))
    y_conv, cp = _conv_fwd(glu, conv_w_full, conv_b, conv_ln_w, conv_ln_b, w_pw_b, b_pw)

    do, dza, dcp, dzc, dh, dgate, gw_out, loss_row = _out_fwd_bwd(attn, za, cp, zc, x, loss_target, modrows, w_out_b)
    dy_conv, gw_pw, conv_rows = _conv_bwd_pointwise(y_conv, dcp, conv_ln_w, conv_ln_b, w_pw_b)
    dglu, g_cw_full = _conv_bwd_depthwise(glu, dy_conv, conv_w_full)
    dq_h, dk_h, dv_h = _attn_bwd(q_h, k_h, v_h, _heads(do, 8))
    dq = _unheads(dq_h)
    dk_all = _unheads(dk_h)
    dv_all = _unheads(dv_h)
    gw_kv, ctx_rows, dknw_c = _ctx_bwd(ctx, modc, norm_w, w_kv_b, pkv_c, dk_all[:, :cl], dv_all[:, :cl], knw_t, ones_kv)
    grad_x, gw_in, dmod_ss, dnw, dqnw, dknw = _bwd_in(
        x, modrows, norm_w, w_in_b, cos, sins, qnw_t, knw_t, ones_bd,
        pq, pkv, dq, dk_all[:, cl:], dv_all[:, cl:], dza, dglu, dzc, dh, gw_kv)

    loss = lax.psum(0.5 / D * jnp.sum(loss_row), ("x", "y", "c"))

    parts_in = gw_in.reshape(D, N_DEV, D_IN // N_DEV).transpose(1, 0, 2)
    parts_out = gw_out.reshape(N_DEV, D // N_DEV, D)
    parts_pw = gw_pw.reshape(N_DEV, DC // N_DEV, DC)
    parts_cw = g_cw_full.reshape(32, N_DEV, DC // N_DEV).transpose(1, 0, 2)

    def pad_cw(a):
        return jnp.pad(a[0], ((0, 32 - KW), (0, 0)))

    (r_in, r_out, r_pw, r_cw) = _reduce_scatter_adamw(
        [(parts_in, w_in[0], m_w_in[0], v_w_in[0]), (parts_out, w_out[0], m_w_out[0], v_w_out[0]),
         (parts_pw, w_pw[0], m_w_pw[0], v_w_pw[0]), (parts_cw, pad_cw(conv_w), pad_cw(m_conv_w), pad_cw(v_conv_w))],
        "reduce_grads")
    r_cw = tuple(a[:KW] for a in r_cw)

    partial = {
        "c_ctx": jnp.zeros((D,), F32),
        "b_mod": jnp.concatenate([ctx_rows[0], ctx_rows[1], jnp.zeros((D,), F32)]),
        "norm_w": dnw[0] + ctx_rows[2],
        "q_norm_w": dqnw.reshape(DA // HD, HD).sum(axis=0),
        "k_norm_w": (dknw + dknw_c).reshape(KVW // HD, HD).sum(axis=0),
        "conv_b": conv_rows[3], "conv_ln_w": conv_rows[1], "conv_ln_b": conv_rows[2], "b_pw": conv_rows[0],
    }
    dmod_mine = jnp.concatenate([dmod_ss, dgate], axis=1).reshape(bl, 3 * D)
    dmod_rows = jnp.pad(dmod_mine, ((0, 8 - bl), (0, 0))).reshape(8 * 24, 128)
    small = jnp.concatenate([_pack(partial), dmod_rows], axis=0)
    (g_small,) = _all_gather_many([small], "gather_small")
    n_packed = small.shape[0] - 8 * 24
    summed = _unpack(_sum_devices(g_small[:, :n_packed]))
    dmod_all = g_small[:, n_packed:].reshape(N_DEV, 8, 3 * D)[:, :bl].reshape(n_ex, 3 * D)
    dmod_full = jnp.concatenate([dmod_all, summed["b_mod"][None, :], jnp.zeros((7, 3 * D), F32)], axis=0)

    dmod_loc = lax.dynamic_slice_in_dim(dmod_full, me * n_mod, n_mod, axis=1)
    g_wmod, d_wmod, nm_wmod, nv_wmod, gc_part = _mod_bwd(c_rows, dmod_loc, w_mod[0], m_w_mod[0], v_w_mod[0])
    (g_gc,) = _all_gather_many([gc_part], "gather_c_ctx")
    g_bmod, g_cctx = _mod_small_grads(dmod_full, g_gc, c_ctx[None, :])

    grads = dict(summed)
    grads["b_mod"] = g_bmod[0]
    grads["c_ctx"] = g_cctx[0]
    weights = {"c_ctx": c_ctx, "b_mod": b_mod, "norm_w": norm_w, "q_norm_w": q_norm_w, "k_norm_w": k_norm_w,
               "conv_b": conv_b, "conv_ln_w": conv_ln_w, "conv_ln_b": conv_ln_b, "b_pw": b_pw}
    moms = {"c_ctx": m_c_ctx, "b_mod": m_b_mod, "norm_w": m_norm_w, "q_norm_w": m_q_norm_w, "k_norm_w": m_k_norm_w,
            "conv_b": m_conv_b, "conv_ln_w": m_conv_ln_w, "conv_ln_b": m_conv_ln_b, "b_pw": m_b_pw}
    vars_ = {"c_ctx": v_c_ctx, "b_mod": v_b_mod, "norm_w": v_norm_w, "q_norm_w": v_q_norm_w, "k_norm_w": v_k_norm_w,
             "conv_b": v_conv_b, "conv_ln_w": v_conv_ln_w, "conv_ln_b": v_conv_ln_b, "b_pw": v_b_pw}
    d_small, nm_small, nv_small = (_unpack(a) for a in _adamw_packed(_pack(weights), _pack(grads), _pack(moms), _pack(vars_)))

    def small_out(table, name):
        return table[name].reshape(weights[name].shape)

    big = {"w_mod": (g_wmod, d_wmod, nm_wmod, nv_wmod), "w_in": r_in, "conv_w": r_cw, "w_pw": r_pw, "w_out": r_out}
    order = ["c_ctx", "w_mod", "b_mod", "norm_w", "w_in", "q_norm_w", "k_norm_w", "conv_w", "conv_b", "conv_ln_w",
             "conv_ln_b", "w_pw", "b_pw", "w_out"]
    outs = [loss, grad_x]
    for which, table in enumerate((grads, d_small, nm_small, nv_small)):
        for name in order:
            if name in big:
                outs.append(big[name][which][None])
            else:
                outs.append(small_out(table, name))
    return tuple(outs)
```

```python
import functools

import jax
import jax.numpy as jnp
from jax import lax
from jax.experimental import pallas as pl
from jax.experimental.pallas import tpu as pltpu

F32, BF16 = jnp.float32, jnp.bfloat16
MESH_ID = pl.DeviceIdType.MESH

N_DEV = 8
D = 1024
D_IN = 2816
DA = 512
DC = 512
HD = 64
KVW = 128
KW = 31
HALO = 16
EPS = 1e-6
ROPE_THETA = 10000.0
GRID_W = 64

ADAM_LR, ADAM_B1, ADAM_B2, ADAM_EPS, ADAM_WD, ADAM_STEP = 0.001, 0.9, 0.999, 1e-08, 0.01, 10

VMEM_LIMIT = 56 * 1024 * 1024

TM = 256
TQ = 128
TC = 512
CH = 64


def _params(sem, vmem=VMEM_LIMIT):
    return pltpu.CompilerParams(dimension_semantics=sem, vmem_limit_bytes=vmem)


def _dot(a, b):
    return jnp.dot(a, b, preferred_element_type=F32)


def _dot_nt(a, b):
    return lax.dot_general(a, b, (((1,), (1,)), ((), ())), preferred_element_type=F32)


def _dot_tn(a, b):
    return lax.dot_general(a, b, (((0,), (0,)), ((), ())), preferred_element_type=F32)


def _sigmoid(z):
    return 1.0 / (1.0 + jnp.exp(-z))


def _segsum(v, ones_bd):
    hi = v.astype(BF16)
    lo = (v - hi.astype(F32)).astype(BF16)
    return _dot(hi, ones_bd) + _dot(lo, ones_bd)


def _swap16(x):
    w = x.shape[-1]
    lane = lax.broadcasted_iota(jnp.int32, x.shape, 1)
    return jnp.where((lane % 32) < 16, pltpu.roll(x, w - 16, 1), pltpu.roll(x, 16, 1))


def _with_ones_column(v):
    one = (lax.broadcasted_iota(jnp.int32, v.shape, 1) == 0).astype(v.dtype)
    return jnp.concatenate([v, one], axis=-1)


def _rope(x, cos, sins):
    return x * cos + _swap16(x) * sins


def _rope_bwd(d, cos, sins):
    return d * cos + _swap16(d * sins)


def _adamw(w, g, m, v):
    m2 = ADAM_B1 * m + (1.0 - ADAM_B1) * g
    v2 = ADAM_B2 * v + (1.0 - ADAM_B2) * (g * g)
    m_hat = m2 / (1.0 - ADAM_B1 ** ADAM_STEP)
    v_hat = v2 / (1.0 - ADAM_B2 ** ADAM_STEP)
    delta = -ADAM_LR * (m_hat / (jnp.sqrt(v_hat) + ADAM_EPS) + ADAM_WD * w)
    return delta, m2, v2


def _coords():
    return lax.axis_index("x"), lax.axis_index("y"), lax.axis_index("c")


def _lin(x, y, c):
    return 4 * x + 2 * y + c


def _all_gather_many(arrs, name):
    n = len(arrs)

    def body(*refs):
        in_refs, out_refs = refs[:n], refs[n:2 * n]
        send_sems, recv_sems, local_sems = refs[2 * n:]
        x, y, c = _coords()
        me, sib = (x, y, c), (x, y, 1 - c)
        chips = [(1 - x, y), (x, 1 - y), (1 - x, 1 - y)]

        def copy(a, k, block, to, src=None):
            slot = out_refs[a].at[_lin(*block)]
            return pltpu.make_async_remote_copy(
                src_ref=slot if src is None else src, dst_ref=slot,
                send_sem=send_sems.at[a * 7 + k], recv_sem=recv_sems.at[a * 7 + k],
                device_id=to, device_id_type=MESH_ID)

        mine = [pltpu.make_async_copy(in_refs[a], out_refs[a].at[_lin(*me)], local_sems.at[a]) for a in range(n)]
        for cp in mine:
            cp.start()
        first = []
        for a in range(n):
            first.append(copy(a, 0, me, sib, src=in_refs[a]))
            first += [copy(a, 1 + j, me, (*chip, c), src=in_refs[a]) for j, chip in enumerate(chips)]
        for cp in first:
            cp.start()
        passed = []
        for a in range(n):
            for j, chip in enumerate(chips):
                copy(a, 1 + j, (*chip, c), me).wait_recv()
                fwd = copy(a, 4 + j, (*chip, c), sib)
                fwd.start()
                passed.append(fwd)
        for a in range(n):
            copy(a, 0, sib, me).wait_recv()
            for j, chip in enumerate(chips):
                copy(a, 4 + j, (*chip, 1 - c), me).wait_recv()
        for cp in first + passed:
            cp.wait_send()
        for cp in mine:
            cp.wait()

    vm = pl.BlockSpec(memory_space=pltpu.VMEM)
    return pl.pallas_call(
        body, name=name,
        out_shape=[jax.ShapeDtypeStruct((N_DEV,) + a.shape, a.dtype) for a in arrs],
        in_specs=[vm] * n, out_specs=[vm] * n,
        scratch_shapes=[pltpu.SemaphoreType.DMA((7 * n,)), pltpu.SemaphoreType.DMA((7 * n,)),
                        pltpu.SemaphoreType.DMA((n,))],
        compiler_params=pltpu.CompilerParams(vmem_limit_bytes=VMEM_LIMIT),
    )(*arrs)


def _reduce_scatter_adamw(items, name):
    n = len(items)
    rb = 32

    def body(*refs):
        parts = refs[0:n]
        wmv = refs[n:4 * n]
        outs = refs[4 * n:8 * n]
        bufs = [refs[8 * n + 4 * a:8 * n + 4 * a + 4] for a in range(n)]
        d2d_send, d2d_recv, ici_send, ici_recv, local_sems = refs[12 * n:]
        x, y, c = _coords()
        sib = (x, y, 1 - c)
        peers = [(1 - x, y), (x, 1 - y), (1 - x, 1 - y)]
        home = 2 * x + y

        def rows_loop(rows, fn):
            def step(i, carry):
                fn(pl.ds(pl.multiple_of(i * rb, rb), rb))
                return carry
            lax.fori_loop(0, rows // rb, step, 0)

        local, d2d, ici = [], [], []
        for a in range(n):
            mine, got_sib = bufs[a][0], bufs[a][1]
            for s in range(4):
                cp = pltpu.make_async_copy(parts[a].at[_lin(s // 2, s % 2, c)], mine.at[s], local_sems.at[4 * a + s])
                cp.start()
                local.append(cp)
                rc = pltpu.make_async_remote_copy(
                    src_ref=parts[a].at[_lin(s // 2, s % 2, 1 - c)], dst_ref=got_sib.at[s],
                    send_sem=d2d_send.at[4 * a + s], recv_sem=d2d_recv.at[4 * a + s],
                    device_id=sib, device_id_type=MESH_ID)
                rc.start()
                d2d.append(rc)

        for a in range(n):
            mine, got_sib, stage, got_chip = bufs[a]
            for s in range(4):
                local[4 * a + s].wait()
                d2d[4 * a + s].wait_recv()
            for k, (px, py) in enumerate(peers):
                slot = 2 * px + py

                def pair_sum(rs, k=k, slot=slot, mine=mine, got_sib=got_sib, stage=stage):
                    stage[k, rs, :] = (mine[slot, rs, :].astype(F32) + got_sib[slot, rs, :].astype(F32)).astype(BF16)

                rows_loop(wmv[3 * a].shape[0], pair_sum)
                rc = pltpu.make_async_remote_copy(
                    src_ref=stage.at[k], dst_ref=got_chip.at[k],
                    send_sem=ici_send.at[3 * a + k], recv_sem=ici_recv.at[3 * a + k],
                    device_id=(px, py, c), device_id_type=MESH_ID)
                rc.start()
                ici.append(rc)

        for a in range(n):
            mine, got_sib, stage, got_chip = bufs[a]
            w_ref, m_ref, v_ref = wmv[3 * a:3 * a + 3]
            g_ref, d_ref, nm_ref, nv_ref = outs[4 * a:4 * a + 4]
            for k in range(3):
                ici[3 * a + k].wait_recv()

            def finish(rs, mine=mine, got_sib=got_sib, got_chip=got_chip, w_ref=w_ref, m_ref=m_ref, v_ref=v_ref,
                       g_ref=g_ref, d_ref=d_ref, nm_ref=nm_ref, nv_ref=nv_ref):
                g = mine[home, rs, :].astype(F32) + got_sib[home, rs, :].astype(F32)
                for k in range(3):
                    g = g + got_chip[k, rs, :].astype(F32)
                delta, m2, v2 = _adamw(w_ref[rs, :], g, m_ref[rs, :], v_ref[rs, :])
                g_ref[rs, :] = g
                d_ref[rs, :] = delta
                nm_ref[rs, :] = m2
                nv_ref[rs, :] = v2

            rows_loop(w_ref.shape[0], finish)

        for rc in d2d + ici:
            rc.wait_send()

    vm = pl.BlockSpec(memory_space=pltpu.VMEM)
    anyspace = pl.BlockSpec(memory_space=pl.ANY)
    args, in_specs, out_shape, scratch = [], [], [], []
    for parts, w, m, v in items:
        assert w.shape[0] % rb == 0 and parts.shape == (N_DEV,) + w.shape and parts.dtype == BF16
    args += [it[0] for it in items]
    in_specs += [anyspace] * n
    for _, w, m, v in items:
        args += [w, m, v]
        in_specs += [vm] * 3
        out_shape += [jax.ShapeDtypeStruct(w.shape, F32)] * 4
    for it in items:
        shp = it[1].shape
        scratch += [pltpu.VMEM((4,) + shp, BF16), pltpu.VMEM((4,) + shp, BF16),
                    pltpu.VMEM((3,) + shp, BF16), pltpu.VMEM((3,) + shp, BF16)]
    scratch += [pltpu.SemaphoreType.DMA((4 * n,)), pltpu.SemaphoreType.DMA((4 * n,)),
                pltpu.SemaphoreType.DMA((3 * n,)), pltpu.SemaphoreType.DMA((3 * n,)), pltpu.SemaphoreType.DMA((4 * n,))]
    outs = pl.pallas_call(
        body, name=name, out_shape=out_shape, in_specs=in_specs, out_specs=[vm] * (4 * n),
        scratch_shapes=scratch, compiler_params=pltpu.CompilerParams(vmem_limit_bytes=VMEM_LIMIT),
    )(*args)
    return [tuple(outs[4 * a:4 * a + 4]) for a in range(n)]


def _mod_fwd(c_rows, w_mod_loc, b_mod_loc):
    def body(c_ref, w_ref, b_ref, o_ref):
        cr = c_ref[...]
        a = (cr * _sigmoid(cr)).astype(BF16)
        o_ref[...] = _dot(a, w_ref[...].astype(BF16)) + b_ref[...]

    return pl.pallas_call(
        body, name="mod_fwd", out_shape=jax.ShapeDtypeStruct((c_rows.shape[0], w_mod_loc.shape[1]), F32),
        compiler_params=pltpu.CompilerParams(vmem_limit_bytes=VMEM_LIMIT),
    )(c_rows, w_mod_loc, b_mod_loc)


def _mod_bwd(c_rows, dmod_loc, w_mod_loc, m, v):
    def body(c_ref, dm_ref, w_ref, m_ref, v_ref, g_ref, d_ref, nm_ref, nv_ref, gc_ref):
        cr = c_ref[...]
        a = (cr * _sigmoid(cr)).astype(BF16)
        dm = dm_ref[...].astype(BF16)
        g = _dot_tn(a, dm)
        w = w_ref[...]
        delta, m2, v2 = _adamw(w, g, m_ref[...], v_ref[...])
        g_ref[...] = g
        d_ref[...] = delta
        nm_ref[...] = m2
        nv_ref[...] = v2
        gc_ref[...] = _dot_nt(dm[16:24, :], w.astype(BF16))

    shp = jax.ShapeDtypeStruct(w_mod_loc.shape, F32)
    return pl.pallas_call(
        body, name="mod_bwd", out_shape=[shp, shp, shp, shp, jax.ShapeDtypeStruct((8, D), F32)],
        compiler_params=pltpu.CompilerParams(vmem_limit_bytes=VMEM_LIMIT),
    )(c_rows, dmod_loc, w_mod_loc, m, v)


def _fwd_in(x, modrows, norm_w, w_in_b, cos, sins, qnw_t, knw_t, ones_bd, k_all, v_all):
    bl, s, _ = x.shape
    nt = s // TM
    ctx_tiles = (k_all.shape[2] - s) // TM
    assert ctx_tiles * TM + s == k_all.shape[2]

    def body(x_ref, mod_ref, nw_ref, win_ref, cos_ref, sin_ref, qnw_ref, knw_ref, bd_ref, kin_ref, vin_ref,
             q_ref, k_ref, v_ref, pq_ref, pkv_ref, za_ref, glu_ref, zc_ref):
        xv = x_ref[0]
        shift = mod_ref[0, 0:1, :]
        scale = mod_ref[0, 1:2, :]
        r = lax.rsqrt(jnp.mean(xv * xv, axis=-1, keepdims=True) + EPS)
        u = (xv * r * nw_ref[...]) * (1.0 + scale) + shift
        p = _dot(u.astype(BF16), win_ref[...])
        pq = p[:, 0:DA]
        pk = p[:, DA:DA + HD * 2]
        ck = cos_ref[...]
        sk = sin_ref[...]
        cs = jnp.concatenate([ck] * (DA // KVW), axis=-1)
        sn = jnp.concatenate([sk] * (DA // KVW), axis=-1)
        rq = lax.rsqrt(_segsum(pq * pq, bd_ref[...]) * (1.0 / HD) + EPS)
        qn = pq * rq * qnw_ref[...]
        qr = _rope(qn, cs, sn) * 0.125
        for h in range(DA // HD):
            q_ref[0, h] = qr[:, h * HD:(h + 1) * HD].astype(BF16)
        rk = lax.rsqrt(_segsum(pk * pk, bd_ref[0:KVW, 0:KVW]) * (1.0 / HD) + EPS)
        kn = pk * rk * knw_ref[...]
        kr = _rope(kn, ck, sk)
        pv = p[:, 640:768]
        for h in range(KVW // HD):
            k_ref[0, h] = kr[:, h * HD:(h + 1) * HD].astype(BF16)
            v_ref[0, h] = _with_ones_column(pv[:, h * HD:(h + 1) * HD]).astype(BF16)
        pq_ref[0] = pq
        pkv_ref[0] = p[:, 512:768]
        za_ref[0] = p[:, 768:1280]
        glu_ref[0] = p[:, 1280:2304]
        zc_ref[0] = p[:, 2304:2816]

    def tile(w):
        return pl.BlockSpec((1, TM, w), lambda b, i: (b, i, 0))

    def const(shape):
        return pl.BlockSpec(shape, lambda b, i: (0,) * len(shape))

    outs = [(DA, F32), (2 * KVW, F32), (DA, F32), (2 * DC, F32), (DC, F32)]
    anyspace = pl.BlockSpec(memory_space=pl.ANY)
    rope = pl.BlockSpec((TM, KVW), lambda b, i: (i, 0))
    k_tile = pl.BlockSpec((1, KVW // HD, TM, HD), lambda b, i: (b, 0, ctx_tiles + i, 0))
    v_tile = pl.BlockSpec((1, KVW // HD, TM, 2 * HD), lambda b, i: (b, 0, ctx_tiles + i, 0))
    return pl.pallas_call(
        body, name="fwd_in", grid=(bl, nt),
        in_specs=[tile(D), pl.BlockSpec((1, 3, D), lambda b, i: (b, 0, 0)), const((1, D)), const((D, D_IN)),
                  rope, rope, const((1, DA)), const((1, KVW)), const((DA, DA)), anyspace, anyspace],
        out_specs=[pl.BlockSpec((1, DA // HD, TM, HD), lambda b, i: (b, 0, i, 0)), k_tile, v_tile]
        + [tile(w) for w, _ in outs],
        out_shape=[jax.ShapeDtypeStruct((bl, DA // HD, s, HD), BF16), jax.ShapeDtypeStruct(k_all.shape, BF16),
                   jax.ShapeDtypeStruct(v_all.shape, BF16)]
        + [jax.ShapeDtypeStruct((bl, s, w), dt) for w, dt in outs],
        input_output_aliases={9: 1, 10: 2},
        compiler_params=_params(("arbitrary", "arbitrary")),
    )(x, modrows, norm_w, w_in_b, cos, sins, qnw_t, knw_t, ones_bd, k_all, v_all)


def _ctx_fwd(ctx, modc, norm_w, w_kv_b, knw_t, ones_bd, n_keys):
    bl, cl, _ = ctx.shape

    def body(x_ref, mod_ref, nw_ref, w_ref, knw_ref, bd_ref, k_ref, v_ref, pkv_ref):
        xv = x_ref[0]
        shift = mod_ref[0, 0:1, :]
        scale = mod_ref[0, 1:2, :]
        r = lax.rsqrt(jnp.mean(xv * xv, axis=-1, keepdims=True) + EPS)
        u = (xv * r * nw_ref[...]) * (1.0 + scale) + shift
        p = _dot(u.astype(BF16), w_ref[...])
        pk = p[:, 0:KVW]
        rk = lax.rsqrt(_segsum(pk * pk, bd_ref[...]) * (1.0 / HD) + EPS)
        kn = pk * rk * knw_ref[...]
        pv = p[:, KVW:2 * KVW]
        for h in range(KVW // HD):
            k_ref[0, h] = kn[:, h * HD:(h + 1) * HD].astype(BF16)
            v_ref[0, h] = _with_ones_column(pv[:, h * HD:(h + 1) * HD]).astype(BF16)
        pkv_ref[0] = p

    def const(shape):
        return pl.BlockSpec(shape, lambda b: (0,) * len(shape))

    def tile(w):
        return pl.BlockSpec((1, cl, w), lambda b: (b, 0, 0))

    k_tile = pl.BlockSpec((1, KVW // HD, cl, HD), lambda b: (b, 0, 0, 0))
    v_tile = pl.BlockSpec((1, KVW // HD, cl, 2 * HD), lambda b: (b, 0, 0, 0))
    return pl.pallas_call(
        body, name="ctx_fwd", grid=(bl,),
        in_specs=[tile(D), const((1, 3, D)), const((1, D)), const((D, 2 * KVW)), const((1, KVW)), const((KVW, KVW))],
        out_specs=[k_tile, v_tile, tile(2 * KVW)],
        out_shape=[jax.ShapeDtypeStruct((bl, KVW // HD, n_keys, HD), BF16),
                   jax.ShapeDtypeStruct((bl, KVW // HD, n_keys, 2 * HD), BF16),
                   jax.ShapeDtypeStruct((bl, cl, 2 * KVW), F32)],
        compiler_params=_params(("arbitrary",)),
    )(ctx, modc, norm_w, w_kv_b, knw_t, ones_bd)


def _attn_specs(n_keys):
    qs = pl.BlockSpec((1, 4, TQ, HD), lambda b, g, i: (b, g, i, 0))
    ks = pl.BlockSpec((1, 1, n_keys, HD), lambda b, g, i: (b, g, 0, 0))
    return qs, ks


def _softmax_rows(q, k):
    s = _dot_nt(q, k)
    m = jnp.max(s, axis=-1, keepdims=True)
    e = jnp.exp(s - m)
    return e / jnp.sum(e, axis=-1, keepdims=True)


def _attn_fwd(q, k, v1):
    bl, _, s, _ = q.shape
    n_keys = k.shape[2]

    def body(q_ref, k_ref, v_ref, o_ref):
        kv = k_ref[0, 0]
        vv = v_ref[0, 0]
        heads = []
        for h in range(4):
            sc = _dot_nt(q_ref[0, h], kv)
            e = jnp.exp(sc - jnp.max(sc, axis=-1, keepdims=True)).astype(BF16)
            ov = _dot(e, vv)
            heads.append(ov[:, 0:HD] * (1.0 / ov[:, HD:HD + 1]))
        o_ref[0] = jnp.concatenate(heads, axis=-1)

    qs, ks = _attn_specs(n_keys)
    vs = pl.BlockSpec((1, 1, n_keys, 2 * HD), lambda b, g, i: (b, g, 0, 0))
    return pl.pallas_call(
        body, name="attn_fwd", grid=(bl, 2, s // TQ), in_specs=[qs, ks, vs],
        out_specs=pl.BlockSpec((1, TQ, 4 * HD), lambda b, g, i: (b, i, g)),
        out_shape=jax.ShapeDtypeStruct((bl, s, DA), F32),
        compiler_params=_params(("arbitrary", "arbitrary", "arbitrary")),
    )(q, k, v1)


def _attn_bwd(q, k, v1, do):
    bl, _, s, _ = q.shape
    n_keys = k.shape[2]

    def body(q_ref, k_ref, v_ref, do_ref, dq_ref, dk_ref, dv_ref):
        @pl.when(pl.program_id(2) == 0)
        def _():
            dk_ref[...] = jnp.zeros_like(dk_ref)
            dv_ref[...] = jnp.zeros_like(dv_ref)

        qv = q_ref[0].reshape(4 * TQ, HD)
        dov = do_ref[0].reshape(4 * TQ, HD)
        kv = k_ref[0, 0]
        p = _softmax_rows(qv, kv)
        dp = _dot_nt(dov, v_ref[0, 0][:, 0:HD])
        delta = jnp.sum(p * dp, axis=-1, keepdims=True)
        ds = (p * (dp - delta)).astype(BF16)
        dq = _dot(ds, kv) * 0.125
        dq_ref[0] = jnp.concatenate([dq[h * TQ:(h + 1) * TQ, :] for h in range(4)], axis=-1)
        dk_ref[0, 0] += _dot_tn(ds, qv)
        dv_ref[0, 0] += _dot_tn(p.astype(BF16), dov)

    qs, ks = _attn_specs(n_keys)
    vs = pl.BlockSpec((1, 1, n_keys, 2 * HD), lambda b, g, i: (b, g, 0, 0))
    kshape = jax.ShapeDtypeStruct(k.shape, F32)
    return pl.pallas_call(
        body, name="attn_bwd", grid=(bl, 2, s // TQ), in_specs=[qs, ks, vs, qs],
        out_specs=[pl.BlockSpec((1, TQ, 4 * HD), lambda b, g, i: (b, i, g)), ks, ks],
        out_shape=[jax.ShapeDtypeStruct((bl, s, DA), F32), kshape, kshape],
        compiler_params=_params(("arbitrary", "arbitrary", "arbitrary")),
    )(q, k, v1, do)


def _halo_specs(width, s):
    per = TC // HALO
    last = s // HALO - 1
    main = pl.BlockSpec((1, TC, width), lambda b, i: (b, i, 0))
    prev = pl.BlockSpec((1, HALO, width), lambda b, i: (b, jnp.maximum(i * per - 1, 0), 0))
    nxt = pl.BlockSpec((1, HALO, width), lambda b, i: (b, jnp.minimum((i + 1) * per, last), 0))
    return main, prev, nxt


def _glu(g):
    return g[:, 0:DC] * _sigmoid(g[:, DC:2 * DC])


def _fill_padded(pad_ref, main, prev, nxt, first, last):
    pad_ref[0:HALO, :] = jnp.where(first, 0.0, prev)
    pad_ref[HALO:HALO + TC, :] = main
    pad_ref[HALO + TC:2 * HALO + TC, :] = jnp.where(last, 0.0, nxt)


def _conv_fwd(glu, conv_w, conv_b, ln_w, ln_b, w_pw_b, b_pw):
    bl, s, _ = glu.shape
    nt = s // TC

    def body(g_ref, gp_ref, gn_ref, cw_ref, cb_ref, lw_ref, lb_ref, wpw_ref, bpw_ref, y_ref, cp_ref, pad_ref):
        i = pl.program_id(1)
        _fill_padded(pad_ref, _glu(g_ref[0]), _glu(gp_ref[0]), _glu(gn_ref[0]), i == 0, i == nt - 1)
        for ck in range(TC // CH):
            acc = jnp.zeros((CH, DC), F32) + cb_ref[...]
            for t in range(KW):
                acc = acc + pad_ref[pl.ds(ck * CH + 1 + t, CH), :] * cw_ref[t:t + 1, :]
            y_ref[0, pl.ds(ck * CH, CH), :] = acc
        y = y_ref[0]
        mu = jnp.mean(y, axis=-1, keepdims=True)
        yc = y - mu
        var = jnp.mean(yc * yc, axis=-1, keepdims=True)
        z = yc * lax.rsqrt(var + EPS) * lw_ref[...] + lb_ref[...]
        act = z * _sigmoid(z)
        cp_ref[0] = _dot(act.astype(BF16), wpw_ref[...]) + bpw_ref[...]

    def const(shape):
        return pl.BlockSpec(shape, lambda b, i: (0,) * len(shape))

    main, prev, nxt = _halo_specs(2 * DC, s)
    tile = pl.BlockSpec((1, TC, DC), lambda b, i: (b, i, 0))
    return pl.pallas_call(
        body, name="conv_fwd", grid=(bl, nt),
        in_specs=[main, prev, nxt, const((32, DC)), const((1, DC)), const((1, DC)), const((1, DC)),
                  const((DC, DC)), const((1, DC))],
        out_specs=[tile, tile],
        out_shape=[jax.ShapeDtypeStruct((bl, s, DC), F32)] * 2,
        scratch_shapes=[pltpu.VMEM((TC + 2 * HALO, DC), F32)],
        compiler_params=_params(("arbitrary", "arbitrary")),
    )(glu, glu, glu, conv_w, conv_b, ln_w, ln_b, w_pw_b, b_pw)


def _conv_bwd_pointwise(y, dcp, ln_w, ln_b, w_pw_b):
    bl, s, _ = y.shape
    nt = s // TM

    def body(y_ref, dcp_ref, lw_ref, lb_ref, wpw_ref, dy_ref, gw_ref, rows_ref):
        @pl.when((pl.program_id(0) == 0) & (pl.program_id(1) == 0))
        def _():
            gw_ref[...] = jnp.zeros_like(gw_ref)
            rows_ref[...] = jnp.zeros_like(rows_ref)

        y = y_ref[0]
        dcp = dcp_ref[0]
        mu = jnp.mean(y, axis=-1, keepdims=True)
        yc = y - mu
        rstd = lax.rsqrt(jnp.mean(yc * yc, axis=-1, keepdims=True) + EPS)
        yn = yc * rstd
        lw = lw_ref[...]
        z = yn * lw + lb_ref[...]
        sg = _sigmoid(z)
        act = z * sg
        dcp_b = dcp.astype(BF16)
        gw_ref[...] += _dot_tn(act.astype(BF16), dcp_b)
        dact = _dot_nt(dcp_b, wpw_ref[...])
        dz = dact * (sg * (1.0 + z * (1.0 - sg)))
        dyn = dz * lw
        dy = rstd * (dyn - jnp.mean(dyn, axis=-1, keepdims=True) - yn * jnp.mean(dyn * yn, axis=-1, keepdims=True))
        dy_ref[0] = dy
        rows_ref[0:1, :] += jnp.sum(dcp, axis=0, keepdims=True)
        rows_ref[1:2, :] += jnp.sum(dz * yn, axis=0, keepdims=True)
        rows_ref[2:3, :] += jnp.sum(dz, axis=0, keepdims=True)
        rows_ref[3:4, :] += jnp.sum(dy, axis=0, keepdims=True)

    def const(shape):
        return pl.BlockSpec(shape, lambda b, i: (0,) * len(shape))

    tile = pl.BlockSpec((1, TM, DC), lambda b, i: (b, i, 0))
    return pl.pallas_call(
        body, name="conv_bwd_pointwise", grid=(bl, nt),
        in_specs=[tile, tile, const((1, DC)), const((1, DC)), const((DC, DC))],
        out_specs=[tile, const((DC, DC)), const((8, DC))],
        out_shape=[jax.ShapeDtypeStruct((bl, s, DC), F32), jax.ShapeDtypeStruct((DC, DC), F32),
                   jax.ShapeDtypeStruct((8, DC), F32)],
        compiler_params=_params(("arbitrary", "arbitrary")),
    )(y, dcp, ln_w, ln_b, w_pw_b)


def _conv_bwd_depthwise(glu, dy, conv_w):
    bl, s, _ = glu.shape
    nt = s // TC

    def body(g_ref, gp_ref, gn_ref, d_ref, dp_ref, dn_ref, cw_ref, dglu_ref, dcw_ref, padu_ref, padd_ref):
        i = pl.program_id(1)

        @pl.when((pl.program_id(0) == 0) & (i == 0))
        def _():
            dcw_ref[...] = jnp.zeros_like(dcw_ref)

        first, last = i == 0, i == nt - 1
        _fill_padded(padu_ref, _glu(g_ref[0]), _glu(gp_ref[0]), _glu(gn_ref[0]), first, last)
        _fill_padded(padd_ref, d_ref[0], dp_ref[0], dn_ref[0], first, last)
        for ck in range(TC // CH):
            acc = jnp.zeros((CH, DC), F32)
            for t in range(KW):
                acc = acc + padd_ref[pl.ds(ck * CH + 2 * HALO - 1 - t, CH), :] * cw_ref[t:t + 1, :]
            g = g_ref[0, pl.ds(ck * CH, CH), :]
            a = g[:, 0:DC]
            sg = _sigmoid(g[:, DC:2 * DC])
            dglu_ref[0, pl.ds(ck * CH, CH), 0:DC] = acc * sg
            dglu_ref[0, pl.ds(ck * CH, CH), DC:2 * DC] = acc * a * sg * (1.0 - sg)
        for t in range(KW):
            acc8 = jnp.zeros((8, DC), F32)
            for ck in range(TC // CH):
                prod = padu_ref[pl.ds(ck * CH + 1 + t, CH), :] * d_ref[0, pl.ds(ck * CH, CH), :]
                acc8 = acc8 + jnp.sum(prod.reshape(CH // 8, 8, DC), axis=0)
            dcw_ref[t:t + 1, :] += jnp.sum(acc8, axis=0, keepdims=True)

    gmain, gprev, gnext = _halo_specs(2 * DC, s)
    dmain, dprev, dnext = _halo_specs(DC, s)
    cw = pl.BlockSpec((32, DC), lambda b, i: (0, 0))
    return pl.pallas_call(
        body, name="conv_bwd_depthwise", grid=(bl, nt),
        in_specs=[gmain, gprev, gnext, dmain, dprev, dnext, cw],
        out_specs=[gmain, cw],
        out_shape=[jax.ShapeDtypeStruct((bl, s, 2 * DC), F32), jax.ShapeDtypeStruct((32, DC), F32)],
        scratch_shapes=[pltpu.VMEM((TC + 2 * HALO, DC), F32)] * 2,
        compiler_params=_params(("arbitrary", "arbitrary")),
    )(glu, glu, glu, dy, dy, dy, conv_w)


def _out_fwd_bwd(attn, za, cp, zc, x, target, modrows, w_out_b):
    bl, s, _ = x.shape
    nt = s // TM

    def body(o_ref, za_ref, cp_ref, zc_ref, x_ref, t_ref, mod_ref, w_ref,
             do_ref, dza_ref, dcp_ref, dzc_ref, dh_ref, dgate_ref, gw_ref, loss_ref):
        b, i = pl.program_id(0), pl.program_id(1)

        @pl.when((b == 0) & (i == 0))
        def _():
            gw_ref[...] = jnp.zeros_like(gw_ref)
            loss_ref[...] = jnp.zeros_like(loss_ref)

        @pl.when(i == 0)
        def _():
            dgate_ref[...] = jnp.zeros_like(dgate_ref)

        o, za_v, cp_v, zc_v = o_ref[0], za_ref[0], cp_ref[0], zc_ref[0]
        gate = mod_ref[0, 2:3, :]
        sa = _sigmoid(za_v)
        sc = _sigmoid(zc_v)
        silu_a = za_v * sa
        silu_c = zc_v * sc
        mix = jnp.concatenate([(o * silu_a).astype(BF16), (cp_v * silu_c).astype(BF16)], axis=-1)
        w = w_ref[...]
        out = _dot(mix, w)
        err = x_ref[0] + gate * out - t_ref[0]
        loss_ref[...] += jnp.sum(err * err, axis=0, keepdims=True)
        dh = err * (1.0 / D)
        dh_ref[0] = dh
        dgate_ref[0] += jnp.sum(dh * out, axis=0, keepdims=True)
        dout = (dh * gate).astype(BF16)
        gw_ref[...] += _dot_tn(mix, dout)
        dmix = _dot_nt(dout, w)
        dga = dmix[:, 0:DA]
        dgc = dmix[:, DA:DA + DC]
        dov = dga * silu_a
        for h in range(DA // HD):
            do_ref[0, h] = dov[:, h * HD:(h + 1) * HD].astype(BF16)
        dza_ref[0] = dga * o * (sa * (1.0 + za_v * (1.0 - sa)))
        dcp_ref[0] = dgc * silu_c
        dzc_ref[0] = dgc * cp_v * (sc * (1.0 + zc_v * (1.0 - sc)))

    def const(shape):
        return pl.BlockSpec(shape, lambda b, i: (0,) * len(shape))

    def tile(w):
        return pl.BlockSpec((1, TM, w), lambda b, i: (b, i, 0))

    return pl.pallas_call(
        body, name="out_fwd_bwd", grid=(bl, nt),
        in_specs=[tile(DA), tile(DA), tile(DC), tile(DC), tile(D), tile(D),
                  pl.BlockSpec((1, 3, D), lambda b, i: (b, 0, 0)), const((D, D))],
        out_specs=[pl.BlockSpec((1, DA // HD, TM, HD), lambda b, i: (b, 0, i, 0)), tile(DA), tile(DC), tile(DC), tile(D),
                   pl.BlockSpec((1, 1, D), lambda b, i: (b, 0, 0)), const((D, D)), const((1, D))],
        out_shape=[jax.ShapeDtypeStruct((bl, DA // HD, s, HD), BF16), jax.ShapeDtypeStruct((bl, s, DA), F32),
                   jax.ShapeDtypeStruct((bl, s, DC), F32), jax.ShapeDtypeStruct((bl, s, DC), F32),
                   jax.ShapeDtypeStruct((bl, s, D), F32), jax.ShapeDtypeStruct((bl, 1, D), F32),
                   jax.ShapeDtypeStruct((D, D), F32), jax.ShapeDtypeStruct((1, D), F32)],
        compiler_params=_params(("arbitrary", "arbitrary")),
    )(attn, za, cp, zc, x, target, modrows, w_out_b)


def _rms_heads_bwd(dy, x, w_t, ones_bd):
    r = lax.rsqrt(_segsum(x * x, ones_bd) * (1.0 / HD) + EPS)
    xh = x * r
    g = dy * w_t
    dx = r * (g - xh * (_segsum(g * xh, ones_bd) * (1.0 / HD)))
    return dx, dy * xh


def _ctx_bwd(ctx, modc, norm_w, w_kv_b, pkv_c, dk_c, dv_c, knw_t, ones_bd):
    bl, cl, _ = ctx.shape

    def body(x_ref, mod_ref, nw_ref, w_ref, p_ref, dk_ref, dv_ref, knw_ref, bd_ref, gw_ref, rows_ref, dknw_ref):
        @pl.when(pl.program_id(0) == 0)
        def _():
            gw_ref[...] = jnp.zeros_like(gw_ref)
            rows_ref[...] = jnp.zeros_like(rows_ref)
            dknw_ref[...] = jnp.zeros_like(dknw_ref)

        xv = x_ref[0]
        shift = mod_ref[0, 0:1, :]
        scale = mod_ref[0, 1:2, :]
        nw = nw_ref[...]
        r = lax.rsqrt(jnp.mean(xv * xv, axis=-1, keepdims=True) + EPS)
        xn = xv * r
        yv = xn * nw
        u = yv * (1.0 + scale) + shift
        dkv = jnp.concatenate([dk_ref[0, 0], dk_ref[0, 1]], axis=-1)
        dpk, dknw = _rms_heads_bwd(dkv, p_ref[0][:, 0:KVW], knw_ref[...], bd_ref[...])
        dp = jnp.concatenate([dpk.astype(BF16), dv_ref[0, 0].astype(BF16), dv_ref[0, 1].astype(BF16)], axis=-1)
        gw_ref[...] += _dot_tn(u.astype(BF16), dp)
        du = _dot_nt(dp, w_ref[...])
        rows_ref[0:1, :] += jnp.sum(du, axis=0, keepdims=True)
        rows_ref[1:2, :] += jnp.sum(du * yv, axis=0, keepdims=True)
        rows_ref[2:3, :] += jnp.sum(du * (1.0 + scale) * xn, axis=0, keepdims=True)
        dknw_ref[...] += jnp.sum(dknw, axis=0, keepdims=True)

    def const(shape):
        return pl.BlockSpec(shape, lambda b: (0,) * len(shape))

    def tile(w):
        return pl.BlockSpec((1, cl, w), lambda b: (b, 0, 0))

    kv_tile = pl.BlockSpec((1, KVW // HD, cl, HD), lambda b: (b, 0, 0, 0))
    return pl.pallas_call(
        body, name="ctx_bwd", grid=(bl,),
        in_specs=[tile(D), const((1, 3, D)), const((1, D)), const((D, 2 * KVW)), tile(2 * KVW), kv_tile, kv_tile,
                  const((1, KVW)), const((KVW, KVW))],
        out_specs=[const((D, 2 * KVW)), const((8, D)), const((1, KVW))],
        out_shape=[jax.ShapeDtypeStruct((D, 2 * KVW), F32), jax.ShapeDtypeStruct((8, D), F32),
                   jax.ShapeDtypeStruct((1, KVW), F32)],
        compiler_params=_params(("arbitrary",)),
    )(ctx, modc, norm_w, w_kv_b, pkv_c, dk_c, dv_c, knw_t, ones_bd)


def _bwd_in(x, modrows, norm_w, w_in_b, cos, sins, qnw_t, knw_t, ones_bd,
            pq, pkv, dq, dk, dv, dza, dglu, dzc, dh, gw_kv):
    bl, s, _ = x.shape
    nt = s // TM

    def body(x_ref, mod_ref, nw_ref, win_hbm, cos_ref, sin_ref, qnw_ref, knw_ref, bd_ref,
             pq_ref, pkv_ref, dq_ref, dk_ref, dv_ref, dza_ref, dglu_ref, dzc_ref, dh_ref, gwkv_ref,
             gx_ref, gw_hbm, dmod_ref, dnw_ref, dqnw_ref, dknw_ref, win_ref, gw_acc, sem):
        b, i = pl.program_id(0), pl.program_id(1)

        @pl.when((b == 0) & (i == 0))
        def _():
            cp = pltpu.make_async_copy(win_hbm, win_ref, sem)
            cp.start()
            gw_acc[...] = jnp.zeros_like(gw_acc)
            dnw_ref[...] = jnp.zeros_like(dnw_ref)
            dqnw_ref[...] = jnp.zeros_like(dqnw_ref)
            dknw_ref[...] = jnp.zeros_like(dknw_ref)
            cp.wait()

        @pl.when(i == 0)
        def _():
            dmod_ref[...] = jnp.zeros_like(dmod_ref)

        ck = cos_ref[...]
        sk = sin_ref[...]
        cs = jnp.concatenate([ck] * (DA // KVW), axis=-1)
        sn = jnp.concatenate([sk] * (DA // KVW), axis=-1)
        bd = bd_ref[...]
        dqn = _rope_bwd(dq_ref[0], cs, sn)
        dpq, dqnw = _rms_heads_bwd(dqn, pq_ref[0], qnw_ref[...], bd)
        dkn = _rope_bwd(jnp.concatenate([dk_ref[0, 0], dk_ref[0, 1]], axis=-1), ck, sk)
        dpk, dknw = _rms_heads_bwd(dkn, pkv_ref[0][:, 0:KVW], knw_ref[...], bd[0:KVW, 0:KVW])
        dqnw_ref[...] += jnp.sum(dqnw, axis=0, keepdims=True)
        dknw_ref[...] += jnp.sum(dknw, axis=0, keepdims=True)
        dp = jnp.concatenate(
            [dpq.astype(BF16), dpk.astype(BF16), dv_ref[0, 0].astype(BF16), dv_ref[0, 1].astype(BF16),
             dza_ref[0].astype(BF16), dglu_ref[0].astype(BF16), dzc_ref[0].astype(BF16)], axis=-1)

        xv = x_ref[0]
        shift = mod_ref[0, 0:1, :]
        scale = mod_ref[0, 1:2, :]
        nw = nw_ref[...]
        r = lax.rsqrt(jnp.mean(xv * xv, axis=-1, keepdims=True) + EPS)
        xn = xv * r
        yv = xn * nw
        u = yv * (1.0 + scale) + shift
        gw_acc[...] += _dot_tn(u.astype(BF16), dp)
        du = _dot_nt(dp, win_ref[...])
        dmod_ref[0, 0:1, :] += jnp.sum(du, axis=0, keepdims=True)
        dmod_ref[0, 1:2, :] += jnp.sum(du * yv, axis=0, keepdims=True)
        dy = du * (1.0 + scale)
        dnw_ref[...] += jnp.sum(dy * xn, axis=0, keepdims=True)
        dxn = dy * nw
        gx_ref[0] = dh_ref[0] + r * (dxn - xn * jnp.mean(dxn * xn, axis=-1, keepdims=True))

        @pl.when((b == bl - 1) & (i == nt - 1))
        def _():
            gw_acc[:, 512:768] += gwkv_ref[...]
            pltpu.sync_copy(gw_acc, gw_hbm)

    def tile(w):
        return pl.BlockSpec((1, TM, w), lambda b, i: (b, i, 0))

    def const(shape):
        return pl.BlockSpec(shape, lambda b, i: (0,) * len(shape))

    anyspace = pl.BlockSpec(memory_space=pl.ANY)
    rope = pl.BlockSpec((TM, KVW), lambda b, i: (i, 0))
    ctx_tiles = (dk.shape[2] - s) // TM
    kv_tile = pl.BlockSpec((1, KVW // HD, TM, HD), lambda b, i: (b, 0, ctx_tiles + i, 0))
    return pl.pallas_call(
        body, name="bwd_in", grid=(bl, nt),
        in_specs=[tile(D), pl.BlockSpec((1, 3, D), lambda b, i: (b, 0, 0)), const((1, D)), anyspace, rope, rope,
                  const((1, DA)), const((1, KVW)), const((DA, DA)),
                  tile(DA), tile(2 * KVW), tile(DA), kv_tile, kv_tile, tile(DA), tile(2 * DC), tile(DC), tile(D),
                  const((D, 2 * KVW))],
        out_specs=[tile(D), anyspace, pl.BlockSpec((1, 2, D), lambda b, i: (b, 0, 0)), const((1, D)),
                   const((1, DA)), const((1, KVW))],
        out_shape=[jax.ShapeDtypeStruct((bl, s, D), F32), jax.ShapeDtypeStruct((D, D_IN), F32),
                   jax.ShapeDtypeStruct((bl, 2, D), F32), jax.ShapeDtypeStruct((1, D), F32),
                   jax.ShapeDtypeStruct((1, DA), F32), jax.ShapeDtypeStruct((1, KVW), F32)],
        scratch_shapes=[pltpu.VMEM((D, D_IN), BF16), pltpu.VMEM((D, D_IN), F32), pltpu.SemaphoreType.DMA],
        compiler_params=_params(("arbitrary", "arbitrary")),
    )(x, modrows, norm_w, w_in_b, cos, sins, qnw_t, knw_t, ones_bd,
      pq, pkv, dq, dk, dv, dza, dglu, dzc, dh, gw_kv)


def _sum_devices(gathered):
    def body(g_ref, o_ref):
        acc = g_ref[0]
        for j in range(1, N_DEV):
            acc = acc + g_ref[j]
        o_ref[...] = acc

    return pl.pallas_call(
        body, name="sum_devices", out_shape=jax.ShapeDtypeStruct(gathered.shape[1:], F32),
    )(gathered)


def _mod_small_grads(dmod_full, gc_parts, c_ctx_row):
    def body(dm_ref, gc_ref, c_ref, gb_ref, gcc_ref):
        gb_ref[...] = jnp.sum(dm_ref[...], axis=0, keepdims=True)
        acc = gc_ref[0, 0:1, :]
        for j in range(1, N_DEV):
            acc = acc + gc_ref[j, 0:1, :]
        cv = c_ref[...]
        sg = _sigmoid(cv)
        gcc_ref[...] = acc * (sg * (1.0 + cv * (1.0 - sg)))

    return pl.pallas_call(
        body, name="mod_small_grads",
        out_shape=[jax.ShapeDtypeStruct((1, 3 * D), F32), jax.ShapeDtypeStruct((1, D), F32)],
    )(dmod_full, gc_parts, c_ctx_row)


def _adamw_packed(w, g, m, v):
    def body(w_ref, g_ref, m_ref, v_ref, d_ref, nm_ref, nv_ref):
        delta, m2, v2 = _adamw(w_ref[...], g_ref[...], m_ref[...], v_ref[...])
        d_ref[...] = delta
        nm_ref[...] = m2
        nv_ref[...] = v2

    shp = jax.ShapeDtypeStruct(w.shape, F32)
    return pl.pallas_call(body, name="adamw_small", out_shape=[shp, shp, shp])(w, g, m, v)


_SMALL = (("c_ctx", 1024), ("b_mod", 3072), ("norm_w", 1024), ("q_norm_w", 64), ("k_norm_w", 64),
          ("conv_b", 512), ("conv_ln_w", 512), ("conv_ln_b", 512), ("b_pw", 512))


def _rows_of(n):
    return -(-n // 1024) * 8


def _pack(vectors):
    rows = []
    for name, n in _SMALL:
        flat = vectors[name].reshape(-1).astype(F32)
        rows.append(jnp.pad(flat, (0, _rows_of(n) * 128 - n)).reshape(_rows_of(n), 128))
    return jnp.concatenate(rows, axis=0)


def _unpack(packed):
    out, r0 = {}, 0
    for name, n in _SMALL:
        out[name] = packed[r0:r0 + _rows_of(n)].reshape(-1)[:n]
        r0 += _rows_of(n)
    return out


def _rope_tables(s):
    t = jnp.arange(s, dtype=jnp.int32)
    row = (t // GRID_W).astype(F32)
    col = (t % GRID_W).astype(F32)
    freqs = ROPE_THETA ** (-jnp.arange(0, HD // 2, 2, dtype=F32) / (HD // 2))
    ang_r = row[:, None] * freqs[None, :]
    ang_c = col[:, None] * freqs[None, :]
    cr, sr, cc, sc = jnp.cos(ang_r), jnp.sin(ang_r), jnp.cos(ang_c), jnp.sin(ang_c)
    cos = jnp.concatenate([cr, cr, cc, cc], axis=-1)
    sins = jnp.concatenate([-sr, sr, -sc, sc], axis=-1)
    return jnp.tile(cos, (1, KVW // HD)), jnp.tile(sins, (1, KVW // HD))


def kernel(x, c, ctx, c_ctx, w_mod, b_mod, norm_w, w_in, q_norm_w, k_norm_w, conv_w, conv_b, conv_ln_w, conv_ln_b, w_pw, b_pw, w_out, loss_target, m_c_ctx, m_w_mod, m_b_mod, m_norm_w, m_w_in, m_q_norm_w, m_k_norm_w, m_conv_w, m_conv_b, m_conv_ln_w, m_conv_ln_b, m_w_pw, m_b_pw, m_w_out, v_c_ctx, v_w_mod, v_b_mod, v_norm_w, v_w_in, v_q_norm_w, v_k_norm_w, v_conv_w, v_conv_b, v_conv_ln_w, v_conv_ln_b, v_w_pw, v_b_pw, v_w_out):
    bl, s, _ = x.shape
    cl = ctx.shape[1]
    me = _lin(*_coords())
    n_mod = w_mod.shape[2]

    conv_w_pad = jnp.pad(conv_w[0], ((0, 32 - KW), (0, 0)))
    c_pad = jnp.pad(c, ((0, 8 - bl), (0, 0)))
    g_win, g_wout, g_wpw, g_cw, g_c = _all_gather_many(
        [w_in[0].astype(BF16), w_out[0].astype(BF16), w_pw[0].astype(BF16), conv_w_pad, c_pad], "gather_weights")
    w_in_b = g_win.transpose(1, 0, 2).reshape(D, D_IN)
    w_out_b = g_wout.reshape(D, D)
    w_pw_b = g_wpw.reshape(DC, DC)
    conv_w_full = g_cw.transpose(1, 0, 2).reshape(32, DC)
    c_all = g_c[:, :bl, :].reshape(N_DEV * bl, D)
    n_ex = N_DEV * bl
    c_rows = jnp.concatenate([c_all, c_ctx[None, :], jnp.zeros((7, D), F32)], axis=0)

    b_mod_loc = lax.dynamic_slice_in_dim(b_mod, me * n_mod, n_mod, axis=1)
    mod_loc = _mod_fwd(c_rows, w_mod[0], b_mod_loc)
    (g_mod,) = _all_gather_many([mod_loc], "gather_mod")
    mod_all = g_mod.transpose(1, 0, 2).reshape(n_ex + 8, 3 * D)
    modrows = lax.dynamic_slice_in_dim(mod_all, me * bl, bl, axis=0).reshape(bl, 3, D)
    modc = mod_all[n_ex].reshape(1, 3, D)

    cos, sins = _rope_tables(s)
    qnw_t = jnp.tile(q_norm_w, (1, DA // HD))
    knw_t = jnp.tile(k_norm_w, (1, KVW // HD))
    lane = jnp.arange(DA, dtype=jnp.int32) // HD
    ones_bd = (lane[:, None] == lane[None, :]).astype(BF16)
    ones_kv = ones_bd[0:KVW, 0:KVW]
    w_kv_b = w_in_b[:, 512:768]

    k_ctx, v_ctx, pkv_c = _ctx_fwd(ctx, modc, norm_w, w_kv_b, knw_t, ones_kv, cl + s)
    q_h, k_h, v_h, pq, pkv, za, glu, zc = _fwd_in(
        x, modrows, norm_w, w_in_b, cos, sins, qnw_t, knw_t, ones_bd, k_ctx, v_ctx)
    attn = _attn_fwd(q_h, k_h, v_h)
    y_conv, cp = _conv_fwd(glu, conv_w_full, conv_b, conv_ln_w, conv_ln_b, w_pw_b, b_pw)

    do_h, dza, dcp, dzc, dh, dgate, gw_out, loss_row = _out_fwd_bwd(attn, za, cp, zc, x, loss_target, modrows, w_out_b)
    dy_conv, gw_pw, conv_rows = _conv_bwd_pointwise(y_conv, dcp, conv_ln_w, conv_ln_b, w_pw_b)
    dglu, g_cw_full = _conv_bwd_depthwise(glu, dy_conv, conv_w_full)
    dq, dk_h, dv_h = _attn_bwd(q_h, k_h, v_h, do_h)
    gw_kv, ctx_rows, dknw_c = _ctx_bwd(ctx, modc, norm_w, w_kv_b, pkv_c, dk_h, dv_h, knw_t, ones_kv)
    grad_x, gw_in, dmod_ss, dnw, dqnw, dknw = _bwd_in(
        x, modrows, norm_w, w_in_b, cos, sins, qnw_t, knw_t, ones_bd,
        pq, pkv, dq, dk_h, dv_h, dza, dglu, dzc, dh, gw_kv)

    parts_in = gw_in.astype(BF16).reshape(D, N_DEV, D_IN // N_DEV).transpose(1, 0, 2)
    parts_out = gw_out.astype(BF16).reshape(N_DEV, D // N_DEV, D)
    parts_pw = gw_pw.astype(BF16).reshape(N_DEV, DC // N_DEV, DC)
    parts_cw = g_cw_full.astype(BF16).reshape(32, N_DEV, DC // N_DEV).transpose(1, 0, 2)

    def pad_cw(a):
        return jnp.pad(a[0], ((0, 32 - KW), (0, 0)))

    (r_in, r_out, r_pw, r_cw) = _reduce_scatter_adamw(
        [(parts_in, w_in[0], m_w_in[0], v_w_in[0]), (parts_out, w_out[0], m_w_out[0], v_w_out[0]),
         (parts_pw, w_pw[0], m_w_pw[0], v_w_pw[0]), (parts_cw, pad_cw(conv_w), pad_cw(m_conv_w), pad_cw(v_conv_w))],
        "reduce_grads")
    r_cw = tuple(a[:KW] for a in r_cw)

    partial = {
        "c_ctx": loss_row[0],
        "b_mod": jnp.concatenate([ctx_rows[0], ctx_rows[1], jnp.zeros((D,), F32)]),
        "norm_w": dnw[0] + ctx_rows[2],
        "q_norm_w": dqnw.reshape(DA // HD, HD).sum(axis=0),
        "k_norm_w": (dknw + dknw_c).reshape(KVW // HD, HD).sum(axis=0),
        "conv_b": conv_rows[3], "conv_ln_w": conv_rows[1], "conv_ln_b": conv_rows[2], "b_pw": conv_rows[0],
    }
    dmod_mine = jnp.concatenate([dmod_ss, dgate], axis=1).reshape(bl, 3 * D)
    dmod_rows = jnp.pad(dmod_mine, ((0, 8 - bl), (0, 0))).reshape(8 * 24, 128)
    small = jnp.concatenate([_pack(partial), dmod_rows], axis=0)
    (g_small,) = _all_gather_many([small], "gather_small")
    n_packed = small.shape[0] - 8 * 24
    summed = _unpack(_sum_devices(g_small[:, :n_packed]))
    loss = (0.5 / D) * jnp.sum(summed["c_ctx"])
    dmod_all = g_small[:, n_packed:].reshape(N_DEV, 8, 3 * D)[:, :bl].reshape(n_ex, 3 * D)
    dmod_full = jnp.concatenate([dmod_all, summed["b_mod"][None, :], jnp.zeros((7, 3 * D), F32)], axis=0)

    dmod_loc = lax.dynamic_slice_in_dim(dmod_full, me * n_mod, n_mod, axis=1)
    g_wmod, d_wmod, nm_wmod, nv_wmod, gc_part = _mod_bwd(c_rows, dmod_loc, w_mod[0], m_w_mod[0], v_w_mod[0])
    (g_gc,) = _all_gather_many([gc_part], "gather_c_ctx")
    g_bmod, g_cctx = _mod_small_grads(dmod_full, g_gc, c_ctx[None, :])

    grads = dict(summed)
    grads["b_mod"] = g_bmod[0]
    grads["c_ctx"] = g_cctx[0]
    weights = {"c_ctx": c_ctx, "b_mod": b_mod, "norm_w": norm_w, "q_norm_w": q_norm_w, "k_norm_w": k_norm_w,
               "conv_b": conv_b, "conv_ln_w": conv_ln_w, "conv_ln_b": conv_ln_b, "b_pw": b_pw}
    moms = {"c_ctx": m_c_ctx, "b_mod": m_b_mod, "norm_w": m_norm_w, "q_norm_w": m_q_norm_w, "k_norm_w": m_k_norm_w,
            "conv_b": m_conv_b, "conv_ln_w": m_conv_ln_w, "conv_ln_b": m_conv_ln_b, "b_pw": m_b_pw}
    vars_ = {"c_ctx": v_c_ctx, "b_mod": v_b_mod, "norm_w": v_norm_w, "q_norm_w": v_q_norm_w, "k_norm_w": v_k_norm_w,
             "conv_b": v_conv_b, "conv_ln_w": v_conv_ln_w, "conv_ln_b": v_conv_ln_b, "b_pw": v_b_pw}
    d_small, nm_small, nv_small = (_unpack(a) for a in _adamw_packed(_pack(weights), _pack(grads), _pack(moms), _pack(vars_)))

    def small_out(table, name):
        return table[name].reshape(weights[name].shape)

    big = {"w_mod": (g_wmod, d_wmod, nm_wmod, nv_wmod), "w_in": r_in, "conv_w": r_cw, "w_pw": r_pw, "w_out": r_out}
    order = ["c_ctx", "w_mod", "b_mod", "norm_w", "w_in", "q_norm_w", "k_norm_w", "conv_w", "conv_b", "conv_ln_w",
             "conv_ln_b", "w_pw", "b_pw", "w_out"]
    outs = [loss, grad_x]
    for which, table in enumerate((grads, d_small, nm_small, nv_small)):
        for name in order:
            if name in big:
                outs.append(big[name][which][None])
            else:
                outs.append(small_out(table, name))
    return tuple(outs)
```

```python
import functools

import jax
import jax.numpy as jnp
from jax import lax
from jax.experimental import pallas as pl
from jax.experimental.pallas import tpu as pltpu

F32, BF16 = jnp.float32, jnp.bfloat16
MESH_ID = pl.DeviceIdType.MESH

N_DEV = 8
D = 1024
D_IN = 2816
DA = 512
DC = 512
HD = 64
KVW = 128
KW = 31
HALO = 16
EPS = 1e-6
ROPE_THETA = 10000.0
GRID_W = 64

ADAM_LR, ADAM_B1, ADAM_B2, ADAM_EPS, ADAM_WD, ADAM_STEP = 0.001, 0.9, 0.999, 1e-08, 0.01, 10

VMEM_LIMIT = 56 * 1024 * 1024

TM = 256
TQ = 128
TC = 512
CH = 64


def _params(sem, vmem=VMEM_LIMIT):
    return pltpu.CompilerParams(dimension_semantics=sem, vmem_limit_bytes=vmem)


def _dot(a, b):
    return jnp.dot(a, b, preferred_element_type=F32)


def _dot_nt(a, b):
    return lax.dot_general(a, b, (((1,), (1,)), ((), ())), preferred_element_type=F32)


def _dot_tn(a, b):
    return lax.dot_general(a, b, (((0,), (0,)), ((), ())), preferred_element_type=F32)


def _sigmoid(z):
    return 1.0 / (1.0 + jnp.exp(-z))


def _segsum(v, ones_bd):
    hi = v.astype(BF16)
    lo = (v - hi.astype(F32)).astype(BF16)
    return _dot(hi, ones_bd) + _dot(lo, ones_bd)


def _swap16(x):
    w = x.shape[-1]
    lane = lax.broadcasted_iota(jnp.int32, x.shape, 1)
    return jnp.where((lane % 32) < 16, pltpu.roll(x, w - 16, 1), pltpu.roll(x, 16, 1))


def _with_ones_column(v):
    one = (lax.broadcasted_iota(jnp.int32, v.shape, 1) == 0).astype(v.dtype)
    return jnp.concatenate([v, one], axis=-1)


def _rope(x, cos, sins):
    return x * cos + _swap16(x) * sins


def _rope_bwd(d, cos, sins):
    return d * cos + _swap16(d * sins)


def _adamw(w, g, m, v):
    m2 = ADAM_B1 * m + (1.0 - ADAM_B1) * g
    v2 = ADAM_B2 * v + (1.0 - ADAM_B2) * (g * g)
    m_hat = m2 / (1.0 - ADAM_B1 ** ADAM_STEP)
    v_hat = v2 / (1.0 - ADAM_B2 ** ADAM_STEP)
    delta = -ADAM_LR * (m_hat / (jnp.sqrt(v_hat) + ADAM_EPS) + ADAM_WD * w)
    return delta, m2, v2


def _coords():
    return lax.axis_index("x"), lax.axis_index("y"), lax.axis_index("c")


def _lin(x, y, c):
    return 4 * x + 2 * y + c


def _all_gather_many(arrs, name):
    n = len(arrs)

    def body(*refs):
        in_refs, out_refs = refs[:n], refs[n:2 * n]
        send_sems, recv_sems, local_sems = refs[2 * n:]
        x, y, c = _coords()
        me, sib = (x, y, c), (x, y, 1 - c)
        chips = [(1 - x, y), (x, 1 - y), (1 - x, 1 - y)]

        def copy(a, k, block, to, src=None):
            slot = out_refs[a].at[_lin(*block)]
            return pltpu.make_async_remote_copy(
                src_ref=slot if src is None else src, dst_ref=slot,
                send_sem=send_sems.at[a * 7 + k], recv_sem=recv_sems.at[a * 7 + k],
                device_id=to, device_id_type=MESH_ID)

        mine = [pltpu.make_async_copy(in_refs[a], out_refs[a].at[_lin(*me)], local_sems.at[a]) for a in range(n)]
        for cp in mine:
            cp.start()
        first = []
        for a in range(n):
            first.append(copy(a, 0, me, sib, src=in_refs[a]))
            first += [copy(a, 1 + j, me, (*chip, c), src=in_refs[a]) for j, chip in enumerate(chips)]
        for cp in first:
            cp.start()
        passed = []
        for a in range(n):
            for j, chip in enumerate(chips):
                copy(a, 1 + j, (*chip, c), me).wait_recv()
                fwd = copy(a, 4 + j, (*chip, c), sib)
                fwd.start()
                passed.append(fwd)
        for a in range(n):
            copy(a, 0, sib, me).wait_recv()
            for j, chip in enumerate(chips):
                copy(a, 4 + j, (*chip, 1 - c), me).wait_recv()
        for cp in first + passed:
            cp.wait_send()
        for cp in mine:
            cp.wait()

    vm = pl.BlockSpec(memory_space=pltpu.VMEM)
    return pl.pallas_call(
        body, name=name,
        out_shape=[jax.ShapeDtypeStruct((N_DEV,) + a.shape, a.dtype) for a in arrs],
        in_specs=[vm] * n, out_specs=[vm] * n,
        scratch_shapes=[pltpu.SemaphoreType.DMA((7 * n,)), pltpu.SemaphoreType.DMA((7 * n,)),
                        pltpu.SemaphoreType.DMA((n,))],
        compiler_params=pltpu.CompilerParams(vmem_limit_bytes=VMEM_LIMIT),
    )(*arrs)


def _reduce_scatter_adamw(items, name):
    n = len(items)
    rb = 32

    def body(*refs):
        parts = refs[0:n]
        wmv = refs[n:4 * n]
        outs = refs[4 * n:8 * n]
        bufs = [refs[8 * n + 4 * a:8 * n + 4 * a + 4] for a in range(n)]
        d2d_send, d2d_recv, ici_send, ici_recv, local_sems = refs[12 * n:]
        x, y, c = _coords()
        sib = (x, y, 1 - c)
        peers = [(1 - x, y), (x, 1 - y), (1 - x, 1 - y)]
        home = 2 * x + y

        def rows_loop(rows, fn):
            def step(i, carry):
                fn(pl.ds(pl.multiple_of(i * rb, rb), rb))
                return carry
            lax.fori_loop(0, rows // rb, step, 0)

        local, d2d, ici = [], [], []
        for a in range(n):
            mine, got_sib = bufs[a][0], bufs[a][1]
            for s in range(4):
                cp = pltpu.make_async_copy(parts[a].at[_lin(s // 2, s % 2, c)], mine.at[s], local_sems.at[4 * a + s])
                cp.start()
                local.append(cp)
                rc = pltpu.make_async_remote_copy(
                    src_ref=parts[a].at[_lin(s // 2, s % 2, 1 - c)], dst_ref=got_sib.at[s],
                    send_sem=d2d_send.at[4 * a + s], recv_sem=d2d_recv.at[4 * a + s],
                    device_id=sib, device_id_type=MESH_ID)
                rc.start()
                d2d.append(rc)

        for a in range(n):
            mine, got_sib, stage, got_chip = bufs[a]
            for s in range(4):
                local[4 * a + s].wait()
                d2d[4 * a + s].wait_recv()
            for k, (px, py) in enumerate(peers):
                slot = 2 * px + py

                def pair_sum(rs, k=k, slot=slot, mine=mine, got_sib=got_sib, stage=stage):
                    stage[k, rs, :] = (mine[slot, rs, :].astype(F32) + got_sib[slot, rs, :].astype(F32)).astype(BF16)

                rows_loop(wmv[3 * a].shape[0], pair_sum)
                rc = pltpu.make_async_remote_copy(
                    src_ref=stage.at[k], dst_ref=got_chip.at[k],
                    send_sem=ici_send.at[3 * a + k], recv_sem=ici_recv.at[3 * a + k],
                    device_id=(px, py, c), device_id_type=MESH_ID)
                rc.start()
                ici.append(rc)

        for a in range(n):
            mine, got_sib, stage, got_chip = bufs[a]
            w_ref, m_ref, v_ref = wmv[3 * a:3 * a + 3]
            g_ref, d_ref, nm_ref, nv_ref = outs[4 * a:4 * a + 4]
            for k in range(3):
                ici[3 * a + k].wait_recv()

            def finish(rs, mine=mine, got_sib=got_sib, got_chip=got_chip, w_ref=w_ref, m_ref=m_ref, v_ref=v_ref,
                       g_ref=g_ref, d_ref=d_ref, nm_ref=nm_ref, nv_ref=nv_ref):
                g = mine[home, rs, :].astype(F32) + got_sib[home, rs, :].astype(F32)
                for k in range(3):
                    g = g + got_chip[k, rs, :].astype(F32)
                delta, m2, v2 = _adamw(w_ref[rs, :], g, m_ref[rs, :], v_ref[rs, :])
                g_ref[rs, :] = g
                d_ref[rs, :] = delta
                nm_ref[rs, :] = m2
                nv_ref[rs, :] = v2

            rows_loop(w_ref.shape[0], finish)

        for rc in d2d + ici:
            rc.wait_send()

    vm = pl.BlockSpec(memory_space=pltpu.VMEM)
    anyspace = pl.BlockSpec(memory_space=pl.ANY)
    args, in_specs, out_shape, scratch = [], [], [], []
    for parts, w, m, v in items:
        assert w.shape[0] % rb == 0 and parts.shape == (N_DEV,) + w.shape and parts.dtype == BF16
    args += [it[0] for it in items]
    in_specs += [anyspace] * n
    for _, w, m, v in items:
        args += [w, m, v]
        in_specs += [vm] * 3
        out_shape += [jax.ShapeDtypeStruct(w.shape, F32)] * 4
    for it in items:
        shp = it[1].shape
        scratch += [pltpu.VMEM((4,) + shp, BF16), pltpu.VMEM((4,) + shp, BF16),
                    pltpu.VMEM((3,) + shp, BF16), pltpu.VMEM((3,) + shp, BF16)]
    scratch += [pltpu.SemaphoreType.DMA((4 * n,)), pltpu.SemaphoreType.DMA((4 * n,)),
                pltpu.SemaphoreType.DMA((3 * n,)), pltpu.SemaphoreType.DMA((3 * n,)), pltpu.SemaphoreType.DMA((4 * n,))]
    outs = pl.pallas_call(
        body, name=name, out_shape=out_shape, in_specs=in_specs, out_specs=[vm] * (4 * n),
        scratch_shapes=scratch, compiler_params=pltpu.CompilerParams(vmem_limit_bytes=VMEM_LIMIT),
    )(*args)
    return [tuple(outs[4 * a:4 * a + 4]) for a in range(n)]


def _mod_fwd(c_rows, w_mod_loc, b_mod_loc):
    def body(c_ref, w_ref, b_ref, o_ref):
        cr = c_ref[...]
        a = (cr * _sigmoid(cr)).astype(BF16)
        o_ref[...] = _dot(a, w_ref[...].astype(BF16)) + b_ref[...]

    return pl.pallas_call(
        body, name="mod_fwd", out_shape=jax.ShapeDtypeStruct((c_rows.shape[0], w_mod_loc.shape[1]), F32),
        compiler_params=pltpu.CompilerParams(vmem_limit_bytes=VMEM_LIMIT),
    )(c_rows, w_mod_loc, b_mod_loc)


def _mod_bwd(c_rows, dmod_loc, w_mod_loc, m, v):
    def body(c_ref, dm_ref, w_ref, m_ref, v_ref, g_ref, d_ref, nm_ref, nv_ref, gc_ref):
        cr = c_ref[...]
        a = (cr * _sigmoid(cr)).astype(BF16)
        dm = dm_ref[...].astype(BF16)
        g = _dot_tn(a, dm)
        w = w_ref[...]
        delta, m2, v2 = _adamw(w, g, m_ref[...], v_ref[...])
        g_ref[...] = g
        d_ref[...] = delta
        nm_ref[...] = m2
        nv_ref[...] = v2
        gc_ref[...] = _dot_nt(dm[16:24, :], w.astype(BF16))

    shp = jax.ShapeDtypeStruct(w_mod_loc.shape, F32)
    return pl.pallas_call(
        body, name="mod_bwd", out_shape=[shp, shp, shp, shp, jax.ShapeDtypeStruct((8, D), F32)],
        compiler_params=pltpu.CompilerParams(vmem_limit_bytes=VMEM_LIMIT),
    )(c_rows, dmod_loc, w_mod_loc, m, v)


def _fwd_in(x, modrows, norm_w, w_in_b, cos, sins, qnw_t, knw_t, ones_bd, k_all, v_all):
    bl, s, _ = x.shape
    nt = s // TM
    ctx_tiles = (k_all.shape[2] - s) // TM
    assert ctx_tiles * TM + s == k_all.shape[2]

    def body(x_ref, mod_ref, nw_ref, win_ref, cos_ref, sin_ref, qnw_ref, knw_ref, bd_ref, kin_ref, vin_ref,
             q_ref, k_ref, v_ref, pq_ref, pkv_ref, za_ref, glu_ref, zc_ref):
        xv = x_ref[0]
        shift = mod_ref[0, 0:1, :]
        scale = mod_ref[0, 1:2, :]
        r = lax.rsqrt(jnp.mean(xv * xv, axis=-1, keepdims=True) + EPS)
        u = (xv * r * nw_ref[...]) * (1.0 + scale) + shift
        p = _dot(u.astype(BF16), win_ref[...])
        pq = p[:, 0:DA]
        pk = p[:, DA:DA + HD * 2]
        ck = cos_ref[...]
        sk = sin_ref[...]
        cs = jnp.concatenate([ck] * (DA // KVW), axis=-1)
        sn = jnp.concatenate([sk] * (DA // KVW), axis=-1)
        rq = lax.rsqrt(_segsum(pq * pq, bd_ref[...]) * (1.0 / HD) + EPS)
        qn = pq * rq * qnw_ref[...]
        qr = _rope(qn, cs, sn) * 0.125
        for h in range(DA // HD):
            q_ref[0, h] = qr[:, h * HD:(h + 1) * HD].astype(BF16)
        rk = lax.rsqrt(_segsum(pk * pk, bd_ref[0:KVW, 0:KVW]) * (1.0 / HD) + EPS)
        kn = pk * rk * knw_ref[...]
        kr = _rope(kn, ck, sk)
        pv = p[:, 640:768]
        for h in range(KVW // HD):
            k_ref[0, h] = kr[:, h * HD:(h + 1) * HD].astype(BF16)
            v_ref[0, h] = _with_ones_column(pv[:, h * HD:(h + 1) * HD]).astype(BF16)
        pq_ref[0] = pq
        pkv_ref[0] = p[:, 512:768]
        za_ref[0] = p[:, 768:1280]
        glu_ref[0] = p[:, 1280:2304]
        zc_ref[0] = p[:, 2304:2816]

    def tile(w):
        return pl.BlockSpec((1, TM, w), lambda b, i: (b, i, 0))

    def const(shape):
        return pl.BlockSpec(shape, lambda b, i: (0,) * len(shape))

    outs = [(DA, F32), (2 * KVW, F32), (DA, F32), (2 * DC, F32), (DC, F32)]
    anyspace = pl.BlockSpec(memory_space=pl.ANY)
    rope = pl.BlockSpec((TM, KVW), lambda b, i: (i, 0))
    k_tile = pl.BlockSpec((1, KVW // HD, TM, HD), lambda b, i: (b, 0, ctx_tiles + i, 0))
    v_tile = pl.BlockSpec((1, KVW // HD, TM, 2 * HD), lambda b, i: (b, 0, ctx_tiles + i, 0))
    return pl.pallas_call(
        body, name="fwd_in", grid=(bl, nt),
        in_specs=[tile(D), pl.BlockSpec((1, 3, D), lambda b, i: (b, 0, 0)), const((1, D)), const((D, D_IN)),
                  rope, rope, const((1, DA)), const((1, KVW)), const((DA, DA)), anyspace, anyspace],
        out_specs=[pl.BlockSpec((1, DA // HD, TM, HD), lambda b, i: (b, 0, i, 0)), k_tile, v_tile]
        + [tile(w) for w, _ in outs],
        out_shape=[jax.ShapeDtypeStruct((bl, DA // HD, s, HD), BF16), jax.ShapeDtypeStruct(k_all.shape, BF16),
                   jax.ShapeDtypeStruct(v_all.shape, BF16)]
        + [jax.ShapeDtypeStruct((bl, s, w), dt) for w, dt in outs],
        input_output_aliases={9: 1, 10: 2},
        compiler_params=_params(("arbitrary", "arbitrary")),
    )(x, modrows, norm_w, w_in_b, cos, sins, qnw_t, knw_t, ones_bd, k_all, v_all)


def _ctx_fwd(ctx, modc, norm_w, w_kv_b, knw_t, ones_bd, n_keys):
    bl, cl, _ = ctx.shape

    def body(x_ref, mod_ref, nw_ref, w_ref, knw_ref, bd_ref, k_ref, v_ref, pkv_ref):
        xv = x_ref[0]
        shift = mod_ref[0, 0:1, :]
        scale = mod_ref[0, 1:2, :]
        r = lax.rsqrt(jnp.mean(xv * xv, axis=-1, keepdims=True) + EPS)
        u = (xv * r * nw_ref[...]) * (1.0 + scale) + shift
        p = _dot(u.astype(BF16), w_ref[...])
        pk = p[:, 0:KVW]
        rk = lax.rsqrt(_segsum(pk * pk, bd_ref[...]) * (1.0 / HD) + EPS)
        kn = pk * rk * knw_ref[...]
        pv = p[:, KVW:2 * KVW]
        for h in range(KVW // HD):
            k_ref[0, h] = kn[:, h * HD:(h + 1) * HD].astype(BF16)
            v_ref[0, h] = _with_ones_column(pv[:, h * HD:(h + 1) * HD]).astype(BF16)
        pkv_ref[0] = p

    def const(shape):
        return pl.BlockSpec(shape, lambda b: (0,) * len(shape))

    def tile(w):
        return pl.BlockSpec((1, cl, w), lambda b: (b, 0, 0))

    k_tile = pl.BlockSpec((1, KVW // HD, cl, HD), lambda b: (b, 0, 0, 0))
    v_tile = pl.BlockSpec((1, KVW // HD, cl, 2 * HD), lambda b: (b, 0, 0, 0))
    return pl.pallas_call(
        body, name="ctx_fwd", grid=(bl,),
        in_specs=[tile(D), const((1, 3, D)), const((1, D)), const((D, 2 * KVW)), const((1, KVW)), const((KVW, KVW))],
        out_specs=[k_tile, v_tile, tile(2 * KVW)],
        out_shape=[jax.ShapeDtypeStruct((bl, KVW // HD, n_keys, HD), BF16),
                   jax.ShapeDtypeStruct((bl, KVW // HD, n_keys, 2 * HD), BF16),
                   jax.ShapeDtypeStruct((bl, cl, 2 * KVW), F32)],
        compiler_params=_params(("arbitrary",)),
    )(ctx, modc, norm_w, w_kv_b, knw_t, ones_bd)


def _attn_specs(n_keys):
    qs = pl.BlockSpec((1, 4, TQ, HD), lambda b, g, i: (b, g, i, 0))
    ks = pl.BlockSpec((1, 1, n_keys, HD), lambda b, g, i: (b, g, 0, 0))
    return qs, ks


def _attn_fwd(q, k, v1):
    bl, _, s, _ = q.shape
    n_keys = k.shape[2]

    def body(q_ref, k_ref, v_ref, o_ref, lse_ref):
        kv = k_ref[0, 0]
        vv = v_ref[0, 0]
        lane = lax.broadcasted_iota(jnp.int32, (TQ, 2 * HD), 1)
        lse = jnp.zeros((TQ, 2 * HD), F32)
        heads = []
        for h in range(4):
            sc = _dot_nt(q_ref[0, h], kv)
            m = jnp.max(sc, axis=-1, keepdims=True)
            e = jnp.exp(sc - m).astype(BF16)
            ov = _dot(e, vv)
            denom = ov[:, HD:HD + 1]
            heads.append(ov[:, 0:HD] * (1.0 / denom))
            lse = jnp.where(lane == h, m + jnp.log(denom), lse)
        o_ref[0] = jnp.concatenate(heads, axis=-1)
        lse_ref[0, 0] = lse

    qs, ks = _attn_specs(n_keys)
    vs = pl.BlockSpec((1, 1, n_keys, 2 * HD), lambda b, g, i: (b, g, 0, 0))
    return pl.pallas_call(
        body, name="attn_fwd", grid=(bl, 2, s // TQ), in_specs=[qs, ks, vs],
        out_specs=[pl.BlockSpec((1, TQ, 4 * HD), lambda b, g, i: (b, i, g)),
                   pl.BlockSpec((1, 1, TQ, 2 * HD), lambda b, g, i: (b, g, i, 0))],
        out_shape=[jax.ShapeDtypeStruct((bl, s, DA), F32), jax.ShapeDtypeStruct((bl, 2, s, 2 * HD), F32)],
        compiler_params=_params(("arbitrary", "arbitrary", "arbitrary")),
    )(q, k, v1)


def _attn_bwd(q, k, v1, do, o, lse):
    bl, _, s, _ = q.shape
    n_keys = k.shape[2]
    nq = s // TQ

    def body(q_ref, k_ref, v_ref, do_ref, o_ref, lse_ref, dq_ref, dk_ref, dv_ref, p_sc, ds_sc, dkt, dvt):
        i = pl.program_id(2)

        @pl.when(i == 0)
        def _():
            dkt[...] = jnp.zeros_like(dkt)
            dvt[...] = jnp.zeros_like(dvt)

        kv = k_ref[0, 0]
        vv = v_ref[0, 0][:, 0:HD]
        lse = lse_ref[0, 0]
        ov = o_ref[0]
        dqs = []
        for h in range(4):
            qh = q_ref[0, h]
            doh = do_ref[0, h]
            delta = jnp.sum(ov[:, h * HD:(h + 1) * HD] * doh.astype(F32), axis=-1, keepdims=True)
            rows = pl.ds(h * TQ, TQ)
            p = jnp.exp(_dot_nt(qh, kv) - lse[:, h:h + 1])
            ds = (p * (_dot_nt(doh, vv) - delta)).astype(BF16)
            p_sc[rows, :] = p.astype(BF16)
            ds_sc[rows, :] = ds
            dqs.append(_dot(ds, kv) * 0.125)
        dq_ref[0] = jnp.concatenate(dqs, axis=-1)
        dvt[...] += _dot_tn(do_ref[0].reshape(4 * TQ, HD), p_sc[...])
        dkt[...] += _dot_tn(q_ref[0].reshape(4 * TQ, HD), ds_sc[...])

        @pl.when(i == nq - 1)
        def _():
            dk_ref[0, 0] = dkt[...].T
            dv_ref[0, 0] = dvt[...].T

    qs, ks = _attn_specs(n_keys)
    vs = pl.BlockSpec((1, 1, n_keys, 2 * HD), lambda b, g, i: (b, g, 0, 0))
    os_ = pl.BlockSpec((1, TQ, 4 * HD), lambda b, g, i: (b, i, g))
    kshape = jax.ShapeDtypeStruct(k.shape, F32)
    return pl.pallas_call(
        body, name="attn_bwd", grid=(bl, 2, nq),
        in_specs=[qs, ks, vs, qs, os_, pl.BlockSpec((1, 1, TQ, 2 * HD), lambda b, g, i: (b, g, i, 0))],
        out_specs=[os_, ks, ks],
        out_shape=[jax.ShapeDtypeStruct((bl, s, DA), F32), kshape, kshape],
        scratch_shapes=[pltpu.VMEM((4 * TQ, n_keys), BF16), pltpu.VMEM((4 * TQ, n_keys), BF16),
                        pltpu.VMEM((HD, n_keys), F32), pltpu.VMEM((HD, n_keys), F32)],
        compiler_params=_params(("arbitrary", "arbitrary", "arbitrary")),
    )(q, k, v1, do, o, lse)


def _halo_specs(width, s):
    per = TC // HALO
    last = s // HALO - 1
    main = pl.BlockSpec((1, TC, width), lambda b, i: (b, i, 0))
    prev = pl.BlockSpec((1, HALO, width), lambda b, i: (b, jnp.maximum(i * per - 1, 0), 0))
    nxt = pl.BlockSpec((1, HALO, width), lambda b, i: (b, jnp.minimum((i + 1) * per, last), 0))
    return main, prev, nxt


def _glu(g):
    return g[:, 0:DC] * _sigmoid(g[:, DC:2 * DC])


def _fill_padded(pad_ref, main, prev, nxt, first, last):
    pad_ref[0:HALO, :] = jnp.where(first, 0.0, prev)
    pad_ref[HALO:HALO + TC, :] = main
    pad_ref[HALO + TC:2 * HALO + TC, :] = jnp.where(last, 0.0, nxt)


PLANE_ROWS = TC + 2 * HALO - 8


def _shift_planes(pad_ref, planes_ref):
    for r in range(1, 8):
        planes_ref[r - 1] = pad_ref[pl.ds(r, PLANE_ROWS), :]


def _tap_rows(pad_ref, planes_ref, offset, start, n):
    a, r = divmod(offset, 8)
    if r == 0:
        return pad_ref[pl.ds(start + 8 * a, n), :]
    return planes_ref[r - 1, pl.ds(start + 8 * a, n), :]


def _conv_fwd(glu, conv_w, conv_b, ln_w, ln_b, w_pw_b, b_pw):
    bl, s, _ = glu.shape
    nt = s // TC

    def body(g_ref, gp_ref, gn_ref, cw_ref, cb_ref, lw_ref, lb_ref, wpw_ref, bpw_ref, y_ref, cp_ref, pad_ref, planes_ref):
        i = pl.program_id(1)
        _fill_padded(pad_ref, _glu(g_ref[0]), _glu(gp_ref[0]), _glu(gn_ref[0]), i == 0, i == nt - 1)
        _shift_planes(pad_ref, planes_ref)
        for ck in range(TC // CH):
            acc = jnp.zeros((CH, DC), F32) + cb_ref[...]
            for t in range(KW):
                acc = acc + _tap_rows(pad_ref, planes_ref, 1 + t, ck * CH, CH) * cw_ref[t:t + 1, :]
            y_ref[0, pl.ds(ck * CH, CH), :] = acc
        y = y_ref[0]
        mu = jnp.mean(y, axis=-1, keepdims=True)
        yc = y - mu
        var = jnp.mean(yc * yc, axis=-1, keepdims=True)
        z = yc * lax.rsqrt(var + EPS) * lw_ref[...] + lb_ref[...]
        act = z * _sigmoid(z)
        cp_ref[0] = _dot(act.astype(BF16), wpw_ref[...]) + bpw_ref[...]

    def const(shape):
        return pl.BlockSpec(shape, lambda b, i: (0,) * len(shape))

    main, prev, nxt = _halo_specs(2 * DC, s)
    tile = pl.BlockSpec((1, TC, DC), lambda b, i: (b, i, 0))
    return pl.pallas_call(
        body, name="conv_fwd", grid=(bl, nt),
        in_specs=[main, prev, nxt, const((32, DC)), const((1, DC)), const((1, DC)), const((1, DC)),
                  const((DC, DC)), const((1, DC))],
        out_specs=[tile, tile],
        out_shape=[jax.ShapeDtypeStruct((bl, s, DC), F32)] * 2,
        scratch_shapes=[pltpu.VMEM((TC + 2 * HALO, DC), F32), pltpu.VMEM((7, PLANE_ROWS, DC), F32)],
        compiler_params=_params(("arbitrary", "arbitrary")),
    )(glu, glu, glu, conv_w, conv_b, ln_w, ln_b, w_pw_b, b_pw)


def _conv_bwd_pointwise(y, dcp, ln_w, ln_b, w_pw_b):
    bl, s, _ = y.shape
    nt = s // TM

    def body(y_ref, dcp_ref, lw_ref, lb_ref, wpw_ref, dy_ref, gw_ref, rows_ref):
        @pl.when((pl.program_id(0) == 0) & (pl.program_id(1) == 0))
        def _():
            gw_ref[...] = jnp.zeros_like(gw_ref)
            rows_ref[...] = jnp.zeros_like(rows_ref)

        y = y_ref[0]
        dcp = dcp_ref[0]
        mu = jnp.mean(y, axis=-1, keepdims=True)
        yc = y - mu
        rstd = lax.rsqrt(jnp.mean(yc * yc, axis=-1, keepdims=True) + EPS)
        yn = yc * rstd
        lw = lw_ref[...]
        z = yn * lw + lb_ref[...]
        sg = _sigmoid(z)
        act = z * sg
        dcp_b = dcp.astype(BF16)
        gw_ref[...] += _dot_tn(act.astype(BF16), dcp_b)
        dact = _dot_nt(dcp_b, wpw_ref[...])
        dz = dact * (sg * (1.0 + z * (1.0 - sg)))
        dyn = dz * lw
        dy = rstd * (dyn - jnp.mean(dyn, axis=-1, keepdims=True) - yn * jnp.mean(dyn * yn, axis=-1, keepdims=True))
        dy_ref[0] = dy
        rows_ref[0:1, :] += jnp.sum(dcp, axis=0, keepdims=True)
        rows_ref[1:2, :] += jnp.sum(dz * yn, axis=0, keepdims=True)
        rows_ref[2:3, :] += jnp.sum(dz, axis=0, keepdims=True)
        rows_ref[3:4, :] += jnp.sum(dy, axis=0, keepdims=True)

    def const(shape):
        return pl.BlockSpec(shape, lambda b, i: (0,) * len(shape))

    tile = pl.BlockSpec((1, TM, DC), lambda b, i: (b, i, 0))
    return pl.pallas_call(
        body, name="conv_bwd_pointwise", grid=(bl, nt),
        in_specs=[tile, tile, const((1, DC)), const((1, DC)), const((DC, DC))],
        out_specs=[tile, const((DC, DC)), const((8, DC))],
        out_shape=[jax.ShapeDtypeStruct((bl, s, DC), F32), jax.ShapeDtypeStruct((DC, DC), F32),
                   jax.ShapeDtypeStruct((8, DC), F32)],
        compiler_params=_params(("arbitrary", "arbitrary")),
    )(y, dcp, ln_w, ln_b, w_pw_b)


def _conv_bwd_depthwise(glu, dy, conv_w):
    bl, s, _ = glu.shape
    nt = s // TC

    def body(g_ref, gp_ref, gn_ref, d_ref, dp_ref, dn_ref, cw_ref, dglu_ref, dcw_ref,
             padu_ref, padd_ref, planes_u, planes_d):
        i = pl.program_id(1)

        @pl.when((pl.program_id(0) == 0) & (i == 0))
        def _():
            dcw_ref[...] = jnp.zeros_like(dcw_ref)

        first, last = i == 0, i == nt - 1
        _fill_padded(padu_ref, _glu(g_ref[0]), _glu(gp_ref[0]), _glu(gn_ref[0]), first, last)
        _fill_padded(padd_ref, d_ref[0], dp_ref[0], dn_ref[0], first, last)
        _shift_planes(padu_ref, planes_u)
        _shift_planes(padd_ref, planes_d)
        for ck in range(TC // CH):
            acc = jnp.zeros((CH, DC), F32)
            for t in range(KW):
                acc = acc + _tap_rows(padd_ref, planes_d, 2 * HALO - 1 - t, ck * CH, CH) * cw_ref[t:t + 1, :]
            g = g_ref[0, pl.ds(ck * CH, CH), :]
            a = g[:, 0:DC]
            sg = _sigmoid(g[:, DC:2 * DC])
            dglu_ref[0, pl.ds(ck * CH, CH), 0:DC] = acc * sg
            dglu_ref[0, pl.ds(ck * CH, CH), DC:2 * DC] = acc * a * sg * (1.0 - sg)
        group = 4
        for t0 in range(0, KW, group):
            taps = range(t0, min(t0 + group, KW))
            acc8 = [jnp.zeros((8, DC), F32) for _ in taps]
            for ck in range(TC // CH):
                dchunk = d_ref[0, pl.ds(ck * CH, CH), :]
                for n, t in enumerate(taps):
                    prod = _tap_rows(padu_ref, planes_u, 1 + t, ck * CH, CH) * dchunk
                    acc8[n] = acc8[n] + jnp.sum(prod.reshape(CH // 8, 8, DC), axis=0)
            for n, t in enumerate(taps):
                dcw_ref[t:t + 1, :] += jnp.sum(acc8[n], axis=0, keepdims=True)

    gmain, gprev, gnext = _halo_specs(2 * DC, s)
    dmain, dprev, dnext = _halo_specs(DC, s)
    cw = pl.BlockSpec((32, DC), lambda b, i: (0, 0))
    return pl.pallas_call(
        body, name="conv_bwd_depthwise", grid=(bl, nt),
        in_specs=[gmain, gprev, gnext, dmain, dprev, dnext, cw],
        out_specs=[gmain, cw],
        out_shape=[jax.ShapeDtypeStruct((bl, s, 2 * DC), F32), jax.ShapeDtypeStruct((32, DC), F32)],
        scratch_shapes=[pltpu.VMEM((TC + 2 * HALO, DC), F32)] * 2 + [pltpu.VMEM((7, PLANE_ROWS, DC), F32)] * 2,
        compiler_params=_params(("arbitrary", "arbitrary")),
    )(glu, glu, glu, dy, dy, dy, conv_w)


def _out_fwd_bwd(attn, za, cp, zc, x, target, modrows, w_out_b):
    bl, s, _ = x.shape
    nt = s // TM

    def body(o_ref, za_ref, cp_ref, zc_ref, x_ref, t_ref, mod_ref, w_ref,
             do_ref, dza_ref, dcp_ref, dzc_ref, dh_ref, dgate_ref, gw_ref, loss_ref):
        b, i = pl.program_id(0), pl.program_id(1)

        @pl.when((b == 0) & (i == 0))
        def _():
            gw_ref[...] = jnp.zeros_like(gw_ref)
            loss_ref[...] = jnp.zeros_like(loss_ref)

        @pl.when(i == 0)
        def _():
            dgate_ref[...] = jnp.zeros_like(dgate_ref)

        o, za_v, cp_v, zc_v = o_ref[0], za_ref[0], cp_ref[0], zc_ref[0]
        gate = mod_ref[0, 2:3, :]
        sa = _sigmoid(za_v)
        sc = _sigmoid(zc_v)
        silu_a = za_v * sa
        silu_c = zc_v * sc
        mix = jnp.concatenate([(o * silu_a).astype(BF16), (cp_v * silu_c).astype(BF16)], axis=-1)
        w = w_ref[...]
        out = _dot(mix, w)
        err = x_ref[0] + gate * out - t_ref[0]
        loss_ref[...] += jnp.sum(err * err, axis=0, keepdims=True)
        dh = err * (1.0 / D)
        dh_ref[0] = dh
        dgate_ref[0] += jnp.sum(dh * out, axis=0, keepdims=True)
        dout = (dh * gate).astype(BF16)
        gw_ref[...] += _dot_tn(mix, dout)
        dmix = _dot_nt(dout, w)
        dga = dmix[:, 0:DA]
        dgc = dmix[:, DA:DA + DC]
        dov = dga * silu_a
        for h in range(DA // HD):
            do_ref[0, h] = dov[:, h * HD:(h + 1) * HD].astype(BF16)
        dza_ref[0] = dga * o * (sa * (1.0 + za_v * (1.0 - sa)))
        dcp_ref[0] = dgc * silu_c
        dzc_ref[0] = dgc * cp_v * (sc * (1.0 + zc_v * (1.0 - sc)))

    def const(shape):
        return pl.BlockSpec(shape, lambda b, i: (0,) * len(shape))

    def tile(w):
        return pl.BlockSpec((1, TM, w), lambda b, i: (b, i, 0))

    return pl.pallas_call(
        body, name="out_fwd_bwd", grid=(bl, nt),
        in_specs=[tile(DA), tile(DA), tile(DC), tile(DC), tile(D), tile(D),
                  pl.BlockSpec((1, 3, D), lambda b, i: (b, 0, 0)), const((D, D))],
        out_specs=[pl.BlockSpec((1, DA // HD, TM, HD), lambda b, i: (b, 0, i, 0)), tile(DA), tile(DC), tile(DC), tile(D),
                   pl.BlockSpec((1, 1, D), lambda b, i: (b, 0, 0)), const((D, D)), const((1, D))],
        out_shape=[jax.ShapeDtypeStruct((bl, DA // HD, s, HD), BF16), jax.ShapeDtypeStruct((bl, s, DA), F32),
                   jax.ShapeDtypeStruct((bl, s, DC), F32), jax.ShapeDtypeStruct((bl, s, DC), F32),
                   jax.ShapeDtypeStruct((bl, s, D), F32), jax.ShapeDtypeStruct((bl, 1, D), F32),
                   jax.ShapeDtypeStruct((D, D), F32), jax.ShapeDtypeStruct((1, D), F32)],
        compiler_params=_params(("arbitrary", "arbitrary")),
    )(attn, za, cp, zc, x, target, modrows, w_out_b)


def _rms_heads_bwd(dy, x, w_t, ones_bd):
    r = lax.rsqrt(_segsum(x * x, ones_bd) * (1.0 / HD) + EPS)
    xh = x * r
    g = dy * w_t
    dx = r * (g - xh * (_segsum(g * xh, ones_bd) * (1.0 / HD)))
    return dx, dy * xh


def _ctx_bwd(ctx, modc, norm_w, w_kv_b, pkv_c, dk_c, dv_c, knw_t, ones_bd):
    bl, cl, _ = ctx.shape

    def body(x_ref, mod_ref, nw_ref, w_ref, p_ref, dk_ref, dv_ref, knw_ref, bd_ref, gw_ref, rows_ref, dknw_ref):
        @pl.when(pl.program_id(0) == 0)
        def _():
            gw_ref[...] = jnp.zeros_like(gw_ref)
            rows_ref[...] = jnp.zeros_like(rows_ref)
            dknw_ref[...] = jnp.zeros_like(dknw_ref)

        xv = x_ref[0]
        shift = mod_ref[0, 0:1, :]
        scale = mod_ref[0, 1:2, :]
        nw = nw_ref[...]
        r = lax.rsqrt(jnp.mean(xv * xv, axis=-1, keepdims=True) + EPS)
        xn = xv * r
        yv = xn * nw
        u = yv * (1.0 + scale) + shift
        dkv = jnp.concatenate([dk_ref[0, 0], dk_ref[0, 1]], axis=-1)
        dpk, dknw = _rms_heads_bwd(dkv, p_ref[0][:, 0:KVW], knw_ref[...], bd_ref[...])
        dp = jnp.concatenate([dpk.astype(BF16), dv_ref[0, 0].astype(BF16), dv_ref[0, 1].astype(BF16)], axis=-1)
        gw_ref[...] += _dot_tn(u.astype(BF16), dp)
        du = _dot_nt(dp, w_ref[...])
        rows_ref[0:1, :] += jnp.sum(du, axis=0, keepdims=True)
        rows_ref[1:2, :] += jnp.sum(du * yv, axis=0, keepdims=True)
        rows_ref[2:3, :] += jnp.sum(du * (1.0 + scale) * xn, axis=0, keepdims=True)
        dknw_ref[...] += jnp.sum(dknw, axis=0, keepdims=True)

    def const(shape):
        return pl.BlockSpec(shape, lambda b: (0,) * len(shape))

    def tile(w):
        return pl.BlockSpec((1, cl, w), lambda b: (b, 0, 0))

    kv_tile = pl.BlockSpec((1, KVW // HD, cl, HD), lambda b: (b, 0, 0, 0))
    return pl.pallas_call(
        body, name="ctx_bwd", grid=(bl,),
        in_specs=[tile(D), const((1, 3, D)), const((1, D)), const((D, 2 * KVW)), tile(2 * KVW), kv_tile, kv_tile,
                  const((1, KVW)), const((KVW, KVW))],
        out_specs=[const((D, 2 * KVW)), const((8, D)), const((1, KVW))],
        out_shape=[jax.ShapeDtypeStruct((D, 2 * KVW), F32), jax.ShapeDtypeStruct((8, D), F32),
                   jax.ShapeDtypeStruct((1, KVW), F32)],
        compiler_params=_params(("arbitrary",)),
    )(ctx, modc, norm_w, w_kv_b, pkv_c, dk_c, dv_c, knw_t, ones_bd)


def _bwd_in(x, modrows, norm_w, w_in_b, cos, sins, qnw_t, knw_t, ones_bd,
            pq, pkv, dq, dk, dv, dza, dglu, dzc, dh, gw_kv):
    bl, s, _ = x.shape
    nt = s // TM

    def body(x_ref, mod_ref, nw_ref, win_hbm, cos_ref, sin_ref, qnw_ref, knw_ref, bd_ref,
             pq_ref, pkv_ref, dq_ref, dk_ref, dv_ref, dza_ref, dglu_ref, dzc_ref, dh_ref, gwkv_ref,
             gx_ref, gw_hbm, dmod_ref, dnw_ref, dqnw_ref, dknw_ref, win_ref, gw_acc, sem):
        b, i = pl.program_id(0), pl.program_id(1)

        @pl.when((b == 0) & (i == 0))
        def _():
            cp = pltpu.make_async_copy(win_hbm, win_ref, sem)
            cp.start()
            gw_acc[...] = jnp.zeros_like(gw_acc)
            dnw_ref[...] = jnp.zeros_like(dnw_ref)
            dqnw_ref[...] = jnp.zeros_like(dqnw_ref)
            dknw_ref[...] = jnp.zeros_like(dknw_ref)
            cp.wait()

        @pl.when(i == 0)
        def _():
            dmod_ref[...] = jnp.zeros_like(dmod_ref)

        ck = cos_ref[...]
        sk = sin_ref[...]
        cs = jnp.concatenate([ck] * (DA // KVW), axis=-1)
        sn = jnp.concatenate([sk] * (DA // KVW), axis=-1)
        bd = bd_ref[...]
        dqn = _rope_bwd(dq_ref[0], cs, sn)
        dpq, dqnw = _rms_heads_bwd(dqn, pq_ref[0], qnw_ref[...], bd)
        dkn = _rope_bwd(jnp.concatenate([dk_ref[0, 0], dk_ref[0, 1]], axis=-1), ck, sk)
        dpk, dknw = _rms_heads_bwd(dkn, pkv_ref[0][:, 0:KVW], knw_ref[...], bd[0:KVW, 0:KVW])
        dqnw_ref[...] += jnp.sum(dqnw, axis=0, keepdims=True)
        dknw_ref[...] += jnp.sum(dknw, axis=0, keepdims=True)
        dp = jnp.concatenate(
            [dpq.astype(BF16), dpk.astype(BF16), dv_ref[0, 0].astype(BF16), dv_ref[0, 1].astype(BF16),
             dza_ref[0].astype(BF16), dglu_ref[0].astype(BF16), dzc_ref[0].astype(BF16)], axis=-1)

        xv = x_ref[0]
        shift = mod_ref[0, 0:1, :]
        scale = mod_ref[0, 1:2, :]
        nw = nw_ref[...]
        r = lax.rsqrt(jnp.mean(xv * xv, axis=-1, keepdims=True) + EPS)
        xn = xv * r
        yv = xn * nw
        u = yv * (1.0 + scale) + shift
        gw_acc[...] += _dot_tn(u.astype(BF16), dp)
        du = _dot_nt(dp, win_ref[...])
        dmod_ref[0, 0:1, :] += jnp.sum(du, axis=0, keepdims=True)
        dmod_ref[0, 1:2, :] += jnp.sum(du * yv, axis=0, keepdims=True)
        dy = du * (1.0 + scale)
        dnw_ref[...] += jnp.sum(dy * xn, axis=0, keepdims=True)
        dxn = dy * nw
        gx_ref[0] = dh_ref[0] + r * (dxn - xn * jnp.mean(dxn * xn, axis=-1, keepdims=True))

        @pl.when((b == bl - 1) & (i == nt - 1))
        def _():
            gw_acc[:, 512:768] += gwkv_ref[...]
            pltpu.sync_copy(gw_acc, gw_hbm)

    def tile(w):
        return pl.BlockSpec((1, TM, w), lambda b, i: (b, i, 0))

    def const(shape):
        return pl.BlockSpec(shape, lambda b, i: (0,) * len(shape))

    anyspace = pl.BlockSpec(memory_space=pl.ANY)
    rope = pl.BlockSpec((TM, KVW), lambda b, i: (i, 0))
    ctx_tiles = (dk.shape[2] - s) // TM
    kv_tile = pl.BlockSpec((1, KVW // HD, TM, HD), lambda b, i: (b, 0, ctx_tiles + i, 0))
    return pl.pallas_call(
        body, name="bwd_in", grid=(bl, nt),
        in_specs=[tile(D), pl.BlockSpec((1, 3, D), lambda b, i: (b, 0, 0)), const((1, D)), anyspace, rope, rope,
                  const((1, DA)), const((1, KVW)), const((DA, DA)),
                  tile(DA), tile(2 * KVW), tile(DA), kv_tile, kv_tile, tile(DA), tile(2 * DC), tile(DC), tile(D),
                  const((D, 2 * KVW))],
        out_specs=[tile(D), anyspace, pl.BlockSpec((1, 2, D), lambda b, i: (b, 0, 0)), const((1, D)),
                   const((1, DA)), const((1, KVW))],
        out_shape=[jax.ShapeDtypeStruct((bl, s, D), F32), jax.ShapeDtypeStruct((D, D_IN), F32),
                   jax.ShapeDtypeStruct((bl, 2, D), F32), jax.ShapeDtypeStruct((1, D), F32),
                   jax.ShapeDtypeStruct((1, DA), F32), jax.ShapeDtypeStruct((1, KVW), F32)],
        scratch_shapes=[pltpu.VMEM((D, D_IN), BF16), pltpu.VMEM((D, D_IN), F32), pltpu.SemaphoreType.DMA],
        compiler_params=_params(("arbitrary", "arbitrary")),
    )(x, modrows, norm_w, w_in_b, cos, sins, qnw_t, knw_t, ones_bd,
      pq, pkv, dq, dk, dv, dza, dglu, dzc, dh, gw_kv)


def _sum_devices(gathered):
    def body(g_ref, o_ref):
        acc = g_ref[0]
        for j in range(1, N_DEV):
            acc = acc + g_ref[j]
        o_ref[...] = acc

    return pl.pallas_call(
        body, name="sum_devices", out_shape=jax.ShapeDtypeStruct(gathered.shape[1:], F32),
    )(gathered)


def _mod_small_grads(dmod_full, gc_parts, c_ctx_row):
    def body(dm_ref, gc_ref, c_ref, gb_ref, gcc_ref):
        gb_ref[...] = jnp.sum(dm_ref[...], axis=0, keepdims=True)
        acc = gc_ref[0, 0:1, :]
        for j in range(1, N_DEV):
            acc = acc + gc_ref[j, 0:1, :]
        cv = c_ref[...]
        sg = _sigmoid(cv)
        gcc_ref[...] = acc * (sg * (1.0 + cv * (1.0 - sg)))

    return pl.pallas_call(
        body, name="mod_small_grads",
        out_shape=[jax.ShapeDtypeStruct((1, 3 * D), F32), jax.ShapeDtypeStruct((1, D), F32)],
    )(dmod_full, gc_parts, c_ctx_row)


def _adamw_packed(w, g, m, v):
    def body(w_ref, g_ref, m_ref, v_ref, d_ref, nm_ref, nv_ref):
        delta, m2, v2 = _adamw(w_ref[...], g_ref[...], m_ref[...], v_ref[...])
        d_ref[...] = delta
        nm_ref[...] = m2
        nv_ref[...] = v2

    shp = jax.ShapeDtypeStruct(w.shape, F32)
    return pl.pallas_call(body, name="adamw_small", out_shape=[shp, shp, shp])(w, g, m, v)


_SMALL = (("c_ctx", 1024), ("b_mod", 3072), ("norm_w", 1024), ("q_norm_w", 64), ("k_norm_w", 64),
          ("conv_b", 512), ("conv_ln_w", 512), ("conv_ln_b", 512), ("b_pw", 512))


def _rows_of(n):
    return -(-n // 1024) * 8


def _pack(vectors):
    rows = []
    for name, n in _SMALL:
        flat = vectors[name].reshape(-1).astype(F32)
        rows.append(jnp.pad(flat, (0, _rows_of(n) * 128 - n)).reshape(_rows_of(n), 128))
    return jnp.concatenate(rows, axis=0)


def _unpack(packed):
    out, r0 = {}, 0
    for name, n in _SMALL:
        out[name] = packed[r0:r0 + _rows_of(n)].reshape(-1)[:n]
        r0 += _rows_of(n)
    return out


def _rope_tables(s):
    t = jnp.arange(s, dtype=jnp.int32)
    row = (t // GRID_W).astype(F32)
    col = (t % GRID_W).astype(F32)
    freqs = ROPE_THETA ** (-jnp.arange(0, HD // 2, 2, dtype=F32) / (HD // 2))
    ang_r = row[:, None] * freqs[None, :]
    ang_c = col[:, None] * freqs[None, :]
    cr, sr, cc, sc = jnp.cos(ang_r), jnp.sin(ang_r), jnp.cos(ang_c), jnp.sin(ang_c)
    cos = jnp.concatenate([cr, cr, cc, cc], axis=-1)
    sins = jnp.concatenate([-sr, sr, -sc, sc], axis=-1)
    return jnp.tile(cos, (1, KVW // HD)), jnp.tile(sins, (1, KVW // HD))


def kernel(x, c, ctx, c_ctx, w_mod, b_mod, norm_w, w_in, q_norm_w, k_norm_w, conv_w, conv_b, conv_ln_w, conv_ln_b, w_pw, b_pw, w_out, loss_target, m_c_ctx, m_w_mod, m_b_mod, m_norm_w, m_w_in, m_q_norm_w, m_k_norm_w, m_conv_w, m_conv_b, m_conv_ln_w, m_conv_ln_b, m_w_pw, m_b_pw, m_w_out, v_c_ctx, v_w_mod, v_b_mod, v_norm_w, v_w_in, v_q_norm_w, v_k_norm_w, v_conv_w, v_conv_b, v_conv_ln_w, v_conv_ln_b, v_w_pw, v_b_pw, v_w_out):
    bl, s, _ = x.shape
    cl = ctx.shape[1]
    me = _lin(*_coords())
    n_mod = w_mod.shape[2]

    conv_w_pad = jnp.pad(conv_w[0], ((0, 32 - KW), (0, 0)))
    c_pad = jnp.pad(c, ((0, 8 - bl), (0, 0)))
    g_win, g_wout, g_wpw, g_cw, g_c = _all_gather_many(
        [w_in[0].astype(BF16), w_out[0].astype(BF16), w_pw[0].astype(BF16), conv_w_pad, c_pad], "gather_weights")
    w_in_b = g_win.transpose(1, 0, 2).reshape(D, D_IN)
    w_out_b = g_wout.reshape(D, D)
    w_pw_b = g_wpw.reshape(DC, DC)
    conv_w_full = g_cw.transpose(1, 0, 2).reshape(32, DC)
    c_all = g_c[:, :bl, :].reshape(N_DEV * bl, D)
    n_ex = N_DEV * bl
    c_rows = jnp.concatenate([c_all, c_ctx[None, :], jnp.zeros((7, D), F32)], axis=0)

    b_mod_loc = lax.dynamic_slice_in_dim(b_mod, me * n_mod, n_mod, axis=1)
    mod_loc = _mod_fwd(c_rows, w_mod[0], b_mod_loc)
    (g_mod,) = _all_gather_many([mod_loc], "gather_mod")
    mod_all = g_mod.transpose(1, 0, 2).reshape(n_ex + 8, 3 * D)
    modrows = lax.dynamic_slice_in_dim(mod_all, me * bl, bl, axis=0).reshape(bl, 3, D)
    modc = mod_all[n_ex].reshape(1, 3, D)

    cos, sins = _rope_tables(s)
    qnw_t = jnp.tile(q_norm_w, (1, DA // HD))
    knw_t = jnp.tile(k_norm_w, (1, KVW // HD))
    lane = jnp.arange(DA, dtype=jnp.int32) // HD
    ones_bd = (lane[:, None] == lane[None, :]).astype(BF16)
    ones_kv = ones_bd[0:KVW, 0:KVW]
    w_kv_b = w_in_b[:, 512:768]

    k_ctx, v_ctx, pkv_c = _ctx_fwd(ctx, modc, norm_w, w_kv_b, knw_t, ones_kv, cl + s)
    q_h, k_h, v_h, pq, pkv, za, glu, zc = _fwd_in(
        x, modrows, norm_w, w_in_b, cos, sins, qnw_t, knw_t, ones_bd, k_ctx, v_ctx)
    attn, lse = _attn_fwd(q_h, k_h, v_h)
    y_conv, cp = _conv_fwd(glu, conv_w_full, conv_b, conv_ln_w, conv_ln_b, w_pw_b, b_pw)

    do_h, dza, dcp, dzc, dh, dgate, gw_out, loss_row = _out_fwd_bwd(attn, za, cp, zc, x, loss_target, modrows, w_out_b)
    dy_conv, gw_pw, conv_rows = _conv_bwd_pointwise(y_conv, dcp, conv_ln_w, conv_ln_b, w_pw_b)
    dglu, g_cw_full = _conv_bwd_depthwise(glu, dy_conv, conv_w_full)
    dq, dk_h, dv_h = _attn_bwd(q_h, k_h, v_h, do_h, attn, lse)
    gw_kv, ctx_rows, dknw_c = _ctx_bwd(ctx, modc, norm_w, w_kv_b, pkv_c, dk_h, dv_h, knw_t, ones_kv)
    grad_x, gw_in, dmod_ss, dnw, dqnw, dknw = _bwd_in(
        x, modrows, norm_w, w_in_b, cos, sins, qnw_t, knw_t, ones_bd,
        pq, pkv, dq, dk_h, dv_h, dza, dglu, dzc, dh, gw_kv)

    parts_in = gw_in.astype(BF16).reshape(D, N_DEV, D_IN // N_DEV).transpose(1, 0, 2)
    parts_out = gw_out.astype(BF16).reshape(N_DEV, D // N_DEV, D)
    parts_pw = gw_pw.astype(BF16).reshape(N_DEV, DC // N_DEV, DC)
    parts_cw = g_cw_full.astype(BF16).reshape(32, N_DEV, DC // N_DEV).transpose(1, 0, 2)

    def pad_cw(a):
        return jnp.pad(a[0], ((0, 32 - KW), (0, 0)))

    (r_in, r_out, r_pw, r_cw) = _reduce_scatter_adamw(
        [(parts_in, w_in[0], m_w_in[0], v_w_in[0]), (parts_out, w_out[0], m_w_out[0], v_w_out[0]),
         (parts_pw, w_pw[0], m_w_pw[0], v_w_pw[0]), (parts_cw, pad_cw(conv_w), pad_cw(m_conv_w), pad_cw(v_conv_w))],
        "reduce_grads")
    r_cw = tuple(a[:KW] for a in r_cw)

    partial = {
        "c_ctx": loss_row[0],
        "b_mod": jnp.concatenate([ctx_rows[0], ctx_rows[1], jnp.zeros((D,), F32)]),
        "norm_w": dnw[0] + ctx_rows[2],
        "q_norm_w": dqnw.reshape(DA // HD, HD).sum(axis=0),
        "k_norm_w": (dknw + dknw_c).reshape(KVW // HD, HD).sum(axis=0),
        "conv_b": conv_rows[3], "conv_ln_w": conv_rows[1], "conv_ln_b": conv_rows[2], "b_pw": conv_rows[0],
    }
    dmod_mine = jnp.concatenate([dmod_ss, dgate], axis=1).reshape(bl, 3 * D)
    dmod_rows = jnp.pad(dmod_mine, ((0, 8 - bl), (0, 0))).reshape(8 * 24, 128)
    small = jnp.concatenate([_pack(partial), dmod_rows], axis=0)
    (g_small,) = _all_gather_many([small], "gather_small")
    n_packed = small.shape[0] - 8 * 24
    summed = _unpack(_sum_devices(g_small[:, :n_packed]))
    loss = (0.5 / D) * jnp.sum(summed["c_ctx"])
    dmod_all = g_small[:, n_packed:].reshape(N_DEV, 8, 3 * D)[:, :bl].reshape(n_ex, 3 * D)
    dmod_full = jnp.concatenate([dmod_all, summed["b_mod"][None, :], jnp.zeros((7, 3 * D), F32)], axis=0)

    dmod_loc = lax.dynamic_slice_in_dim(dmod_full, me * n_mod, n_mod, axis=1)
    g_wmod, d_wmod, nm_wmod, nv_wmod, gc_part = _mod_bwd(c_rows, dmod_loc, w_mod[0], m_w_mod[0], v_w_mod[0])
    (g_gc,) = _all_gather_many([gc_part], "gather_c_ctx")
    g_bmod, g_cctx = _mod_small_grads(dmod_full, g_gc, c_ctx[None, :])

    grads = dict(summed)
    grads["b_mod"] = g_bmod[0]
    grads["c_ctx"] = g_cctx[0]
    weights = {"c_ctx": c_ctx, "b_mod": b_mod, "norm_w": norm_w, "q_norm_w": q_norm_w, "k_norm_w": k_norm_w,
               "conv_b": conv_b, "conv_ln_w": conv_ln_w, "conv_ln_b": conv_ln_b, "b_pw": b_pw}
    moms = {"c_ctx": m_c_ctx, "b_mod": m_b_mod, "norm_w": m_norm_w, "q_norm_w": m_q_norm_w, "k_norm_w": m_k_norm_w,
            "conv_b": m_conv_b, "conv_ln_w": m_conv_ln_w, "conv_ln_b": m_conv_ln_b, "b_pw": m_b_pw}
    vars_ = {"c_ctx": v_c_ctx, "b_mod": v_b_mod, "norm_w": v_norm_w, "q_norm_w": v_q_norm_w, "k_norm_w": v_k_norm_w,
             "conv_b": v_conv_b, "conv_ln_w": v_conv_ln_w, "conv_ln_b": v_conv_ln_b, "b_pw": v_b_pw}
    d_small, nm_small, nv_small = (_unpack(a) for a in _adamw_packed(_pack(weights), _pack(grads), _pack(moms), _pack(vars_)))

    def small_out(table, name):
        return table[name].reshape(weights[name].shape)

    big = {"w_mod": (g_wmod, d_wmod, nm_wmod, nv_wmod), "w_in": r_in, "conv_w": r_cw, "w_pw": r_pw, "w_out": r_out}
    order = ["c_ctx", "w_mod", "b_mod", "norm_w", "w_in", "q_norm_w", "k_norm_w", "conv_w", "conv_b", "conv_ln_w",
             "conv_ln_b", "w_pw", "b_pw", "w_out"]
    outs = [loss, grad_x]
    for which, table in enumerate((grads, d_small, nm_small, nv_small)):
        for name in order:
            if name in big:
                outs.append(big[name][which][None])
            else:
                outs.append(small_out(table, name))
    return tuple(outs)
```

```python
import functools

import jax
import jax.numpy as jnp
from jax import lax
from jax.experimental import pallas as pl
from jax.experimental.pallas import tpu as pltpu

F32, BF16 = jnp.float32, jnp.bfloat16
MESH_ID = pl.DeviceIdType.MESH

N_DEV = 8
D = 1024
D_IN = 2816
DA = 512
DC = 512
HD = 64
KVW = 128
KW = 31
HALO = 16
EPS = 1e-6
ROPE_THETA = 10000.0
GRID_W = 64

ADAM_LR, ADAM_B1, ADAM_B2, ADAM_EPS, ADAM_WD, ADAM_STEP = 0.001, 0.9, 0.999, 1e-08, 0.01, 10

VMEM_LIMIT = 56 * 1024 * 1024

TM = 256
TQ = 128
TC = 512
CH = 64


def _params(sem, vmem=VMEM_LIMIT):
    return pltpu.CompilerParams(dimension_semantics=sem, vmem_limit_bytes=vmem)


def _dot(a, b):
    return jnp.dot(a, b, preferred_element_type=F32)


def _dot_nt(a, b):
    return lax.dot_general(a, b, (((1,), (1,)), ((), ())), preferred_element_type=F32)


def _dot_tn(a, b):
    return lax.dot_general(a, b, (((0,), (0,)), ((), ())), preferred_element_type=F32)


def _sigmoid(z):
    return 1.0 / (1.0 + jnp.exp(-z))


def _segsum(v, ones_bd):
    hi = v.astype(BF16)
    lo = (v - hi.astype(F32)).astype(BF16)
    return _dot(hi, ones_bd) + _dot(lo, ones_bd)


def _swap16(x):
    w = x.shape[-1]
    lane = lax.broadcasted_iota(jnp.int32, x.shape, 1)
    return jnp.where((lane % 32) < 16, pltpu.roll(x, w - 16, 1), pltpu.roll(x, 16, 1))


def _with_ones_column(v):
    one = (lax.broadcasted_iota(jnp.int32, v.shape, 1) == 0).astype(v.dtype)
    return jnp.concatenate([v, one], axis=-1)


def _rope(x, cos, sins):
    return x * cos + _swap16(x) * sins


def _rope_bwd(d, cos, sins):
    return d * cos + _swap16(d * sins)


def _adamw(w, g, m, v):
    m2 = ADAM_B1 * m + (1.0 - ADAM_B1) * g
    v2 = ADAM_B2 * v + (1.0 - ADAM_B2) * (g * g)
    m_hat = m2 / (1.0 - ADAM_B1 ** ADAM_STEP)
    v_hat = v2 / (1.0 - ADAM_B2 ** ADAM_STEP)
    delta = -ADAM_LR * (m_hat / (jnp.sqrt(v_hat) + ADAM_EPS) + ADAM_WD * w)
    return delta, m2, v2


def _coords():
    return lax.axis_index("x"), lax.axis_index("y"), lax.axis_index("c")


def _lin(x, y, c):
    return 4 * x + 2 * y + c


def _all_gather_many(arrs, name):
    n = len(arrs)

    def body(*refs):
        in_refs, out_refs = refs[:n], refs[n:2 * n]
        send_sems, recv_sems, local_sems = refs[2 * n:]
        x, y, c = _coords()
        me, sib = (x, y, c), (x, y, 1 - c)
        chips = [(1 - x, y), (x, 1 - y), (1 - x, 1 - y)]

        def copy(a, k, block, to, src=None):
            slot = out_refs[a].at[_lin(*block)]
            return pltpu.make_async_remote_copy(
                src_ref=slot if src is None else src, dst_ref=slot,
                send_sem=send_sems.at[a * 7 + k], recv_sem=recv_sems.at[a * 7 + k],
                device_id=to, device_id_type=MESH_ID)

        mine = [pltpu.make_async_copy(in_refs[a], out_refs[a].at[_lin(*me)], local_sems.at[a]) for a in range(n)]
        for cp in mine:
            cp.start()
        first = []
        for a in range(n):
            first.append(copy(a, 0, me, sib, src=in_refs[a]))
            first += [copy(a, 1 + j, me, (*chip, c), src=in_refs[a]) for j, chip in enumerate(chips)]
        for cp in first:
            cp.start()
        passed = []
        for a in range(n):
            for j, chip in enumerate(chips):
                copy(a, 1 + j, (*chip, c), me).wait_recv()
                fwd = copy(a, 4 + j, (*chip, c), sib)
                fwd.start()
                passed.append(fwd)
        for a in range(n):
            copy(a, 0, sib, me).wait_recv()
            for j, chip in enumerate(chips):
                copy(a, 4 + j, (*chip, 1 - c), me).wait_recv()
        for cp in first + passed:
            cp.wait_send()
        for cp in mine:
            cp.wait()

    vm = pl.BlockSpec(memory_space=pltpu.VMEM)
    return pl.pallas_call(
        body, name=name,
        out_shape=[jax.ShapeDtypeStruct((N_DEV,) + a.shape, a.dtype) for a in arrs],
        in_specs=[vm] * n, out_specs=[vm] * n,
        scratch_shapes=[pltpu.SemaphoreType.DMA((7 * n,)), pltpu.SemaphoreType.DMA((7 * n,)),
                        pltpu.SemaphoreType.DMA((n,))],
        compiler_params=pltpu.CompilerParams(vmem_limit_bytes=VMEM_LIMIT),
    )(*arrs)


def _all_gather_direct(arr, name):
    def body(in_ref, out_ref, send_sems, recv_sems, local_sem):
        x, y, c = _coords()
        me = _lin(x, y, c)
        mine = pltpu.make_async_copy(in_ref, out_ref.at[me], local_sem)
        mine.start()
        copies = []
        for k in range(1, N_DEV):
            peer = (1 - x if k & 4 else x, 1 - y if k & 2 else y, 1 - c if k & 1 else c)
            cp = pltpu.make_async_remote_copy(
                src_ref=in_ref, dst_ref=out_ref.at[me], send_sem=send_sems.at[k - 1], recv_sem=recv_sems.at[k - 1],
                device_id=peer, device_id_type=MESH_ID)
            cp.start()
            copies.append(cp)
        for cp in copies:
            cp.wait_recv()
        for cp in copies:
            cp.wait_send()
        mine.wait()

    vm = pl.BlockSpec(memory_space=pltpu.VMEM)
    return pl.pallas_call(
        body, name=name, out_shape=jax.ShapeDtypeStruct((N_DEV,) + arr.shape, arr.dtype),
        in_specs=[vm], out_specs=vm,
        scratch_shapes=[pltpu.SemaphoreType.DMA((N_DEV - 1,)), pltpu.SemaphoreType.DMA((N_DEV - 1,)),
                        pltpu.SemaphoreType.DMA],
    )(arr)


def _reduce_scatter_adamw(items, name):
    n = len(items)
    rb = 32

    def body(*refs):
        parts = refs[0:n]
        wmv = refs[n:4 * n]
        outs = refs[4 * n:8 * n]
        bufs = [refs[8 * n + 4 * a:8 * n + 4 * a + 4] for a in range(n)]
        d2d_send, d2d_recv, ici_send, ici_recv, local_sems = refs[12 * n:]
        x, y, c = _coords()
        sib = (x, y, 1 - c)
        peers = [(1 - x, y), (x, 1 - y), (1 - x, 1 - y)]
        home = 2 * x + y

        def rows_loop(rows, fn):
            def step(i, carry):
                fn(pl.ds(pl.multiple_of(i * rb, rb), rb))
                return carry
            lax.fori_loop(0, rows // rb, step, 0)

        local, d2d, ici = [], [], []
        for a in range(n):
            mine, got_sib = bufs[a][0], bufs[a][1]
            for s in range(4):
                cp = pltpu.make_async_copy(parts[a].at[_lin(s // 2, s % 2, c)], mine.at[s], local_sems.at[4 * a + s])
                cp.start()
                local.append(cp)
                rc = pltpu.make_async_remote_copy(
                    src_ref=parts[a].at[_lin(s // 2, s % 2, 1 - c)], dst_ref=got_sib.at[s],
                    send_sem=d2d_send.at[4 * a + s], recv_sem=d2d_recv.at[4 * a + s],
                    device_id=sib, device_id_type=MESH_ID)
                rc.start()
                d2d.append(rc)

        for a in range(n):
            mine, got_sib, stage, got_chip = bufs[a]
            for s in range(4):
                local[4 * a + s].wait()
                d2d[4 * a + s].wait_recv()
            for k, (px, py) in enumerate(peers):
                slot = 2 * px + py

                def pair_sum(rs, k=k, slot=slot, mine=mine, got_sib=got_sib, stage=stage):
                    stage[k, rs, :] = (mine[slot, rs, :].astype(F32) + got_sib[slot, rs, :].astype(F32)).astype(BF16)

                rows_loop(wmv[3 * a].shape[0], pair_sum)
                rc = pltpu.make_async_remote_copy(
                    src_ref=stage.at[k], dst_ref=got_chip.at[k],
                    send_sem=ici_send.at[3 * a + k], recv_sem=ici_recv.at[3 * a + k],
                    device_id=(px, py, c), device_id_type=MESH_ID)
                rc.start()
                ici.append(rc)

        for a in range(n):
            mine, got_sib, stage, got_chip = bufs[a]
            w_ref, m_ref, v_ref = wmv[3 * a:3 * a + 3]
            g_ref, d_ref, nm_ref, nv_ref = outs[4 * a:4 * a + 4]
            for k in range(3):
                ici[3 * a + k].wait_recv()

            def finish(rs, mine=mine, got_sib=got_sib, got_chip=got_chip, w_ref=w_ref, m_ref=m_ref, v_ref=v_ref,
                       g_ref=g_ref, d_ref=d_ref, nm_ref=nm_ref, nv_ref=nv_ref):
                g = mine[home, rs, :].astype(F32) + got_sib[home, rs, :].astype(F32)
                for k in range(3):
                    g = g + got_chip[k, rs, :].astype(F32)
                delta, m2, v2 = _adamw(w_ref[rs, :], g, m_ref[rs, :], v_ref[rs, :])
                g_ref[rs, :] = g
                d_ref[rs, :] = delta
                nm_ref[rs, :] = m2
                nv_ref[rs, :] = v2

            rows_loop(w_ref.shape[0], finish)

        for rc in d2d + ici:
            rc.wait_send()

    vm = pl.BlockSpec(memory_space=pltpu.VMEM)
    anyspace = pl.BlockSpec(memory_space=pl.ANY)
    args, in_specs, out_shape, scratch = [], [], [], []
    for parts, w, m, v in items:
        assert w.shape[0] % rb == 0 and parts.shape == (N_DEV,) + w.shape and parts.dtype == BF16
    args += [it[0] for it in items]
    in_specs += [anyspace] * n
    for _, w, m, v in items:
        args += [w, m, v]
        in_specs += [vm] * 3
        out_shape += [jax.ShapeDtypeStruct(w.shape, F32)] * 4
    for it in items:
        shp = it[1].shape
        scratch += [pltpu.VMEM((4,) + shp, BF16), pltpu.VMEM((4,) + shp, BF16),
                    pltpu.VMEM((3,) + shp, BF16), pltpu.VMEM((3,) + shp, BF16)]
    scratch += [pltpu.SemaphoreType.DMA((4 * n,)), pltpu.SemaphoreType.DMA((4 * n,)),
                pltpu.SemaphoreType.DMA((3 * n,)), pltpu.SemaphoreType.DMA((3 * n,)), pltpu.SemaphoreType.DMA((4 * n,))]
    outs = pl.pallas_call(
        body, name=name, out_shape=out_shape, in_specs=in_specs, out_specs=[vm] * (4 * n),
        scratch_shapes=scratch, compiler_params=pltpu.CompilerParams(vmem_limit_bytes=VMEM_LIMIT),
    )(*args)
    return [tuple(outs[4 * a:4 * a + 4]) for a in range(n)]


def _mod_fwd(c_rows, w_mod_loc, b_mod_loc):
    def body(c_ref, w_ref, b_ref, o_ref):
        cr = c_ref[...]
        a = (cr * _sigmoid(cr)).astype(BF16)
        o_ref[...] = _dot(a, w_ref[...].astype(BF16)) + b_ref[...]

    return pl.pallas_call(
        body, name="mod_fwd", out_shape=jax.ShapeDtypeStruct((c_rows.shape[0], w_mod_loc.shape[1]), F32),
        compiler_params=pltpu.CompilerParams(vmem_limit_bytes=VMEM_LIMIT),
    )(c_rows, w_mod_loc, b_mod_loc)


def _mod_bwd(c_rows, dmod_loc, w_mod_loc, m, v):
    def body(c_ref, dm_ref, w_ref, m_ref, v_ref, g_ref, d_ref, nm_ref, nv_ref, gc_ref):
        cr = c_ref[...]
        a = (cr * _sigmoid(cr)).astype(BF16)
        dm = dm_ref[...].astype(BF16)
        g = _dot_tn(a, dm)
        w = w_ref[...]
        delta, m2, v2 = _adamw(w, g, m_ref[...], v_ref[...])
        g_ref[...] = g
        d_ref[...] = delta
        nm_ref[...] = m2
        nv_ref[...] = v2
        gc_ref[...] = _dot_nt(dm[16:24, :], w.astype(BF16))

    shp = jax.ShapeDtypeStruct(w_mod_loc.shape, F32)
    return pl.pallas_call(
        body, name="mod_bwd", out_shape=[shp, shp, shp, shp, jax.ShapeDtypeStruct((8, D), F32)],
        compiler_params=pltpu.CompilerParams(vmem_limit_bytes=VMEM_LIMIT),
    )(c_rows, dmod_loc, w_mod_loc, m, v)


def _fwd_in(x, modrows, norm_w, w_in_b, cos, sins, qnw_t, knw_t, ones_bd, k_all, v_all):
    bl, s, _ = x.shape
    nt = s // TM
    ctx_tiles = (k_all.shape[2] - s) // TM
    assert ctx_tiles * TM + s == k_all.shape[2]

    def body(x_ref, mod_ref, nw_ref, win_ref, cos_ref, sin_ref, qnw_ref, knw_ref, bd_ref, kin_ref, vin_ref,
             q_ref, k_ref, v_ref, pq_ref, pkv_ref, za_ref, glu_ref, zc_ref):
        xv = x_ref[0]
        shift = mod_ref[0, 0:1, :]
        scale = mod_ref[0, 1:2, :]
        r = lax.rsqrt(jnp.mean(xv * xv, axis=-1, keepdims=True) + EPS)
        u = (xv * r * nw_ref[...]) * (1.0 + scale) + shift
        p = _dot(u.astype(BF16), win_ref[...])
        pq = p[:, 0:DA]
        pk = p[:, DA:DA + HD * 2]
        ck = cos_ref[...]
        sk = sin_ref[...]
        cs = jnp.concatenate([ck] * (DA // KVW), axis=-1)
        sn = jnp.concatenate([sk] * (DA // KVW), axis=-1)
        rq = lax.rsqrt(_segsum(pq * pq, bd_ref[...]) * (1.0 / HD) + EPS)
        qn = pq * rq * qnw_ref[...]
        qr = _rope(qn, cs, sn) * 0.125
        for h in range(DA // HD):
            q_ref[0, h] = qr[:, h * HD:(h + 1) * HD].astype(BF16)
        rk = lax.rsqrt(_segsum(pk * pk, bd_ref[0:KVW, 0:KVW]) * (1.0 / HD) + EPS)
        kn = pk * rk * knw_ref[...]
        kr = _rope(kn, ck, sk)
        pv = p[:, 640:768]
        for h in range(KVW // HD):
            k_ref[0, h] = kr[:, h * HD:(h + 1) * HD].astype(BF16)
            v_ref[0, h] = _with_ones_column(pv[:, h * HD:(h + 1) * HD]).astype(BF16)
        pq_ref[0] = pq
        pkv_ref[0] = p[:, 512:768]
        za_ref[0] = p[:, 768:1280]
        glu_ref[0] = p[:, 1280:2304]
        zc_ref[0] = p[:, 2304:2816]

    def tile(w):
        return pl.BlockSpec((1, TM, w), lambda b, i: (b, i, 0))

    def const(shape):
        return pl.BlockSpec(shape, lambda b, i: (0,) * len(shape))

    outs = [(DA, F32), (2 * KVW, F32), (DA, F32), (2 * DC, F32), (DC, F32)]
    anyspace = pl.BlockSpec(memory_space=pl.ANY)
    rope = pl.BlockSpec((TM, KVW), lambda b, i: (i, 0))
    k_tile = pl.BlockSpec((1, KVW // HD, TM, HD), lambda b, i: (b, 0, ctx_tiles + i, 0))
    v_tile = pl.BlockSpec((1, KVW // HD, TM, 2 * HD), lambda b, i: (b, 0, ctx_tiles + i, 0))
    return pl.pallas_call(
        body, name="fwd_in", grid=(bl, nt),
        in_specs=[tile(D), pl.BlockSpec((1, 3, D), lambda b, i: (b, 0, 0)), const((1, D)), const((D, D_IN)),
                  rope, rope, const((1, DA)), const((1, KVW)), const((DA, DA)), anyspace, anyspace],
        out_specs=[pl.BlockSpec((1, DA // HD, TM, HD), lambda b, i: (b, 0, i, 0)), k_tile, v_tile]
        + [tile(w) for w, _ in outs],
        out_shape=[jax.ShapeDtypeStruct((bl, DA // HD, s, HD), BF16), jax.ShapeDtypeStruct(k_all.shape, BF16),
                   jax.ShapeDtypeStruct(v_all.shape, BF16)]
        + [jax.ShapeDtypeStruct((bl, s, w), dt) for w, dt in outs],
        input_output_aliases={9: 1, 10: 2},
        compiler_params=_params(("arbitrary", "arbitrary")),
    )(x, modrows, norm_w, w_in_b, cos, sins, qnw_t, knw_t, ones_bd, k_all, v_all)


def _ctx_fwd(ctx, modc, norm_w, w_kv_b, knw_t, ones_bd, n_keys):
    bl, cl, _ = ctx.shape

    def body(x_ref, mod_ref, nw_ref, w_ref, knw_ref, bd_ref, k_ref, v_ref, pkv_ref):
        xv = x_ref[0]
        shift = mod_ref[0, 0:1, :]
        scale = mod_ref[0, 1:2, :]
        r = lax.rsqrt(jnp.mean(xv * xv, axis=-1, keepdims=True) + EPS)
        u = (xv * r * nw_ref[...]) * (1.0 + scale) + shift
        p = _dot(u.astype(BF16), w_ref[...])
        pk = p[:, 0:KVW]
        rk = lax.rsqrt(_segsum(pk * pk, bd_ref[...]) * (1.0 / HD) + EPS)
        kn = pk * rk * knw_ref[...]
        pv = p[:, KVW:2 * KVW]
        for h in range(KVW // HD):
            k_ref[0, h] = kn[:, h * HD:(h + 1) * HD].astype(BF16)
            v_ref[0, h] = _with_ones_column(pv[:, h * HD:(h + 1) * HD]).astype(BF16)
        pkv_ref[0] = p

    def const(shape):
        return pl.BlockSpec(shape, lambda b: (0,) * len(shape))

    def tile(w):
        return pl.BlockSpec((1, cl, w), lambda b: (b, 0, 0))

    k_tile = pl.BlockSpec((1, KVW // HD, cl, HD), lambda b: (b, 0, 0, 0))
    v_tile = pl.BlockSpec((1, KVW // HD, cl, 2 * HD), lambda b: (b, 0, 0, 0))
    return pl.pallas_call(
        body, name="ctx_fwd", grid=(bl,),
        in_specs=[tile(D), const((1, 3, D)), const((1, D)), const((D, 2 * KVW)), const((1, KVW)), const((KVW, KVW))],
        out_specs=[k_tile, v_tile, tile(2 * KVW)],
        out_shape=[jax.ShapeDtypeStruct((bl, KVW // HD, n_keys, HD), BF16),
                   jax.ShapeDtypeStruct((bl, KVW // HD, n_keys, 2 * HD), BF16),
                   jax.ShapeDtypeStruct((bl, cl, 2 * KVW), F32)],
        compiler_params=_params(("arbitrary",)),
    )(ctx, modc, norm_w, w_kv_b, knw_t, ones_bd)


def _attn_specs(n_keys):
    qs = pl.BlockSpec((1, 4, TQ, HD), lambda b, g, i: (b, g, i, 0))
    ks = pl.BlockSpec((1, 1, n_keys, HD), lambda b, g, i: (b, g, 0, 0))
    return qs, ks


def _attn_fwd(q, k, v1):
    bl, _, s, _ = q.shape
    n_keys = k.shape[2]

    def body(q_ref, k_ref, v_ref, o_ref, lse_ref):
        kv = k_ref[0, 0]
        vv = v_ref[0, 0]
        lane = lax.broadcasted_iota(jnp.int32, (TQ, 2 * HD), 1)
        lse = jnp.zeros((TQ, 2 * HD), F32)
        heads = []
        sc_all = _dot_nt(q_ref[0].reshape(4 * TQ, HD), kv)
        for h in range(4):
            sc = sc_all[h * TQ:(h + 1) * TQ, :]
            m = jnp.max(sc, axis=-1, keepdims=True)
            e = jnp.exp(sc - m).astype(BF16)
            ov = _dot(e, vv)
            denom = ov[:, HD:HD + 1]
            heads.append(ov[:, 0:HD] * (1.0 / denom))
            lse = jnp.where(lane == h, m + jnp.log(denom), lse)
        o_ref[0] = jnp.concatenate(heads, axis=-1)
        lse_ref[0, 0] = lse

    qs, ks = _attn_specs(n_keys)
    vs = pl.BlockSpec((1, 1, n_keys, 2 * HD), lambda b, g, i: (b, g, 0, 0))
    return pl.pallas_call(
        body, name="attn_fwd", grid=(bl, 2, s // TQ), in_specs=[qs, ks, vs],
        out_specs=[pl.BlockSpec((1, TQ, 4 * HD), lambda b, g, i: (b, i, g)),
                   pl.BlockSpec((1, 1, TQ, 2 * HD), lambda b, g, i: (b, g, i, 0))],
        out_shape=[jax.ShapeDtypeStruct((bl, s, DA), F32), jax.ShapeDtypeStruct((bl, 2, s, 2 * HD), F32)],
        compiler_params=_params(("arbitrary", "arbitrary", "arbitrary")),
    )(q, k, v1)


def _attn_bwd(q, k, v1, do, o, lse):
    bl, _, s, _ = q.shape
    n_keys = k.shape[2]
    nq = s // TQ

    def body(q_ref, k_ref, v_ref, do_ref, o_ref, lse_ref, dq_ref, dk_ref, dv_ref, p_sc, ds_sc, dkt, dvt):
        i = pl.program_id(2)

        @pl.when(i == 0)
        def _():
            dkt[...] = jnp.zeros_like(dkt)
            dvt[...] = jnp.zeros_like(dvt)

        kv = k_ref[0, 0]
        vv = v_ref[0, 0][:, 0:HD]
        lse = lse_ref[0, 0]
        ov = o_ref[0]
        dqs = []
        q_cat = q_ref[0].reshape(4 * TQ, HD)
        do_cat = do_ref[0].reshape(4 * TQ, HD)
        sc_all = _dot_nt(q_cat, kv)
        for h in range(4):
            doh = do_ref[0, h]
            delta = jnp.sum(ov[:, h * HD:(h + 1) * HD] * doh.astype(F32), axis=-1, keepdims=True)
            rows = pl.ds(h * TQ, TQ)
            p = jnp.exp(sc_all[h * TQ:(h + 1) * TQ, :] - lse[:, h:h + 1])
            ds = (p * (_dot_nt(doh, vv) - delta)).astype(BF16)
            p_sc[rows, :] = p.astype(BF16)
            ds_sc[rows, :] = ds
            dqs.append(_dot(ds, kv) * 0.125)
        dq_ref[0] = jnp.concatenate(dqs, axis=-1)
        dvt[...] += _dot_tn(do_cat, p_sc[...])
        dkt[...] += _dot_tn(q_cat, ds_sc[...])

        @pl.when(i == nq - 1)
        def _():
            dk_ref[0, 0] = dkt[...].T
            dv_ref[0, 0] = dvt[...].T

    qs, ks = _attn_specs(n_keys)
    vs = pl.BlockSpec((1, 1, n_keys, 2 * HD), lambda b, g, i: (b, g, 0, 0))
    os_ = pl.BlockSpec((1, TQ, 4 * HD), lambda b, g, i: (b, i, g))
    kshape = jax.ShapeDtypeStruct(k.shape, F32)
    return pl.pallas_call(
        body, name="attn_bwd", grid=(bl, 2, nq),
        in_specs=[qs, ks, vs, qs, os_, pl.BlockSpec((1, 1, TQ, 2 * HD), lambda b, g, i: (b, g, i, 0))],
        out_specs=[os_, ks, ks],
        out_shape=[jax.ShapeDtypeStruct((bl, s, DA), F32), kshape, kshape],
        scratch_shapes=[pltpu.VMEM((4 * TQ, n_keys), BF16), pltpu.VMEM((4 * TQ, n_keys), BF16),
                        pltpu.VMEM((HD, n_keys), F32), pltpu.VMEM((HD, n_keys), F32)],
        compiler_params=_params(("arbitrary", "arbitrary", "arbitrary")),
    )(q, k, v1, do, o, lse)


def _halo_specs(width, s):
    per = TC // HALO
    last = s // HALO - 1
    main = pl.BlockSpec((1, TC, width), lambda b, i: (b, i, 0))
    prev = pl.BlockSpec((1, HALO, width), lambda b, i: (b, jnp.maximum(i * per - 1, 0), 0))
    nxt = pl.BlockSpec((1, HALO, width), lambda b, i: (b, jnp.minimum((i + 1) * per, last), 0))
    return main, prev, nxt


def _glu(g):
    return g[:, 0:DC] * _sigmoid(g[:, DC:2 * DC])


def _fill_padded(pad_ref, main, prev, nxt, first, last):
    pad_ref[0:HALO, :] = jnp.where(first, 0.0, prev)
    pad_ref[HALO:HALO + TC, :] = main
    pad_ref[HALO + TC:2 * HALO + TC, :] = jnp.where(last, 0.0, nxt)


PLANE_ROWS = TC + 2 * HALO - 8


def _shift_planes(pad_ref, planes_ref):
    for r in range(1, 8):
        planes_ref[r - 1] = pad_ref[pl.ds(r, PLANE_ROWS), :]


def _tap_rows(pad_ref, planes_ref, offset, start, n):
    a, r = divmod(offset, 8)
    if r == 0:
        return pad_ref[pl.ds(start + 8 * a, n), :]
    return planes_ref[r - 1, pl.ds(start + 8 * a, n), :]


def _conv_fwd(glu, conv_w, conv_b, ln_w, ln_b, w_pw_b, b_pw):
    bl, s, _ = glu.shape
    nt = s // TC

    def body(g_ref, gp_ref, gn_ref, cw_ref, cb_ref, lw_ref, lb_ref, wpw_ref, bpw_ref, y_ref, cp_ref, pad_ref, planes_ref):
        i = pl.program_id(1)
        _fill_padded(pad_ref, _glu(g_ref[0]), _glu(gp_ref[0]), _glu(gn_ref[0]), i == 0, i == nt - 1)
        _shift_planes(pad_ref, planes_ref)
        for ck in range(TC // CH):
            acc = jnp.zeros((CH, DC), F32) + cb_ref[...]
            for t in range(KW):
                acc = acc + _tap_rows(pad_ref, planes_ref, 1 + t, ck * CH, CH) * cw_ref[t:t + 1, :]
            y_ref[0, pl.ds(ck * CH, CH), :] = acc
        y = y_ref[0]
        mu = jnp.mean(y, axis=-1, keepdims=True)
        yc = y - mu
        var = jnp.mean(yc * yc, axis=-1, keepdims=True)
        z = yc * lax.rsqrt(var + EPS) * lw_ref[...] + lb_ref[...]
        act = z * _sigmoid(z)
        cp_ref[0] = _dot(act.astype(BF16), wpw_ref[...]) + bpw_ref[...]

    def const(shape):
        return pl.BlockSpec(shape, lambda b, i: (0,) * len(shape))

    main, prev, nxt = _halo_specs(2 * DC, s)
    tile = pl.BlockSpec((1, TC, DC), lambda b, i: (b, i, 0))
    return pl.pallas_call(
        body, name="conv_fwd", grid=(bl, nt),
        in_specs=[main, prev, nxt, const((32, DC)), const((1, DC)), const((1, DC)), const((1, DC)),
                  const((DC, DC)), const((1, DC))],
        out_specs=[tile, tile],
        out_shape=[jax.ShapeDtypeStruct((bl, s, DC), F32)] * 2,
        scratch_shapes=[pltpu.VMEM((TC + 2 * HALO, DC), F32), pltpu.VMEM((7, PLANE_ROWS, DC), F32)],
        compiler_params=_params(("arbitrary", "arbitrary")),
    )(glu, glu, glu, conv_w, conv_b, ln_w, ln_b, w_pw_b, b_pw)


def _conv_bwd_pointwise(y, dcp, ln_w, ln_b, w_pw_b):
    bl, s, _ = y.shape
    nt = s // TM

    def body(y_ref, dcp_ref, lw_ref, lb_ref, wpw_ref, dy_ref, gw_ref, rows_ref):
        @pl.when((pl.program_id(0) == 0) & (pl.program_id(1) == 0))
        def _():
            gw_ref[...] = jnp.zeros_like(gw_ref)
            rows_ref[...] = jnp.zeros_like(rows_ref)

        y = y_ref[0]
        dcp = dcp_ref[0]
        mu = jnp.mean(y, axis=-1, keepdims=True)
        yc = y - mu
        rstd = lax.rsqrt(jnp.mean(yc * yc, axis=-1, keepdims=True) + EPS)
        yn = yc * rstd
        lw = lw_ref[...]
        z = yn * lw + lb_ref[...]
        sg = _sigmoid(z)
        act = z * sg
        dcp_b = dcp.astype(BF16)
        gw_ref[...] += _dot_tn(act.astype(BF16), dcp_b)
        dact = _dot_nt(dcp_b, wpw_ref[...])
        dz = dact * (sg * (1.0 + z * (1.0 - sg)))
        dyn = dz * lw
        dy = rstd * (dyn - jnp.mean(dyn, axis=-1, keepdims=True) - yn * jnp.mean(dyn * yn, axis=-1, keepdims=True))
        dy_ref[0] = dy
        rows_ref[0:1, :] += jnp.sum(dcp, axis=0, keepdims=True)
        rows_ref[1:2, :] += jnp.sum(dz * yn, axis=0, keepdims=True)
        rows_ref[2:3, :] += jnp.sum(dz, axis=0, keepdims=True)
        rows_ref[3:4, :] += jnp.sum(dy, axis=0, keepdims=True)

    def const(shape):
        return pl.BlockSpec(shape, lambda b, i: (0,) * len(shape))

    tile = pl.BlockSpec((1, TM, DC), lambda b, i: (b, i, 0))
    return pl.pallas_call(
        body, name="conv_bwd_pointwise", grid=(bl, nt),
        in_specs=[tile, tile, const((1, DC)), const((1, DC)), const((DC, DC))],
        out_specs=[tile, const((DC, DC)), const((8, DC))],
        out_shape=[jax.ShapeDtypeStruct((bl, s, DC), F32), jax.ShapeDtypeStruct((DC, DC), F32),
                   jax.ShapeDtypeStruct((8, DC), F32)],
        compiler_params=_params(("arbitrary", "arbitrary")),
    )(y, dcp, ln_w, ln_b, w_pw_b)


def _conv_bwd_depthwise(glu, dy, conv_w):
    bl, s, _ = glu.shape
    nt = s // TC

    def body(g_ref, gp_ref, gn_ref, d_ref, dp_ref, dn_ref, cw_ref, dglu_ref, dcw_ref,
             padu_ref, padd_ref, planes_u, planes_d):
        i = pl.program_id(1)

        @pl.when((pl.program_id(0) == 0) & (i == 0))
        def _():
            dcw_ref[...] = jnp.zeros_like(dcw_ref)

        first, last = i == 0, i == nt - 1
        _fill_padded(padu_ref, _glu(g_ref[0]), _glu(gp_ref[0]), _glu(gn_ref[0]), first, last)
        _fill_padded(padd_ref, d_ref[0], dp_ref[0], dn_ref[0], first, last)
        _shift_planes(padu_ref, planes_u)
        _shift_planes(padd_ref, planes_d)
        for ck in range(TC // CH):
            acc = jnp.zeros((CH, DC), F32)
            for t in range(KW):
                acc = acc + _tap_rows(padd_ref, planes_d, 2 * HALO - 1 - t, ck * CH, CH) * cw_ref[t:t + 1, :]
            g = g_ref[0, pl.ds(ck * CH, CH), :]
            a = g[:, 0:DC]
            sg = _sigmoid(g[:, DC:2 * DC])
            dglu_ref[0, pl.ds(ck * CH, CH), 0:DC] = acc * sg
            dglu_ref[0, pl.ds(ck * CH, CH), DC:2 * DC] = acc * a * sg * (1.0 - sg)
        group = 4
        for t0 in range(0, KW, group):
            taps = range(t0, min(t0 + group, KW))
            acc8 = [jnp.zeros((8, DC), F32) for _ in taps]
            for ck in range(TC // CH):
                dchunk = d_ref[0, pl.ds(ck * CH, CH), :]
                for n, t in enumerate(taps):
                    prod = _tap_rows(padu_ref, planes_u, 1 + t, ck * CH, CH) * dchunk
                    acc8[n] = acc8[n] + jnp.sum(prod.reshape(CH // 8, 8, DC), axis=0)
            for n, t in enumerate(taps):
                dcw_ref[t:t + 1, :] += jnp.sum(acc8[n], axis=0, keepdims=True)

    gmain, gprev, gnext = _halo_specs(2 * DC, s)
    dmain, dprev, dnext = _halo_specs(DC, s)
    cw = pl.BlockSpec((32, DC), lambda b, i: (0, 0))
    return pl.pallas_call(
        body, name="conv_bwd_depthwise", grid=(bl, nt),
        in_specs=[gmain, gprev, gnext, dmain, dprev, dnext, cw],
        out_specs=[gmain, cw],
        out_shape=[jax.ShapeDtypeStruct((bl, s, 2 * DC), F32), jax.ShapeDtypeStruct((32, DC), F32)],
        scratch_shapes=[pltpu.VMEM((TC + 2 * HALO, DC), F32)] * 2 + [pltpu.VMEM((7, PLANE_ROWS, DC), F32)] * 2,
        compiler_params=_params(("arbitrary", "arbitrary")),
    )(glu, glu, glu, dy, dy, dy, conv_w)


def _out_fwd_bwd(attn, za, cp, zc, x, target, modrows, w_out_b):
    bl, s, _ = x.shape
    nt = s // TM

    def body(o_ref, za_ref, cp_ref, zc_ref, x_ref, t_ref, mod_ref, w_ref,
             do_ref, dza_ref, dcp_ref, dzc_ref, dh_ref, dgate_ref, gw_ref, loss_ref):
        b, i = pl.program_id(0), pl.program_id(1)

        @pl.when((b == 0) & (i == 0))
        def _():
            gw_ref[...] = jnp.zeros_like(gw_ref)
            loss_ref[...] = jnp.zeros_like(loss_ref)

        @pl.when(i == 0)
        def _():
            dgate_ref[...] = jnp.zeros_like(dgate_ref)

        o, za_v, cp_v, zc_v = o_ref[0], za_ref[0], cp_ref[0], zc_ref[0]
        gate = mod_ref[0, 2:3, :]
        sa = _sigmoid(za_v)
        sc = _sigmoid(zc_v)
        silu_a = za_v * sa
        silu_c = zc_v * sc
        mix = jnp.concatenate([(o * silu_a).astype(BF16), (cp_v * silu_c).astype(BF16)], axis=-1)
        w = w_ref[...]
        out = _dot(mix, w)
        err = x_ref[0] + gate * out - t_ref[0]
        loss_ref[...] += jnp.sum(err * err, axis=0, keepdims=True)
        dh = err * (1.0 / D)
        dh_ref[0] = dh
        dgate_ref[0] += jnp.sum(dh * out, axis=0, keepdims=True)
        dout = (dh * gate).astype(BF16)
        gw_ref[...] += _dot_tn(mix, dout)
        dmix = _dot_nt(dout, w)
        dga = dmix[:, 0:DA]
        dgc = dmix[:, DA:DA + DC]
        dov = dga * silu_a
        for h in range(DA // HD):
            do_ref[0, h] = dov[:, h * HD:(h + 1) * HD].astype(BF16)
        dza_ref[0] = dga * o * (sa * (1.0 + za_v * (1.0 - sa)))
        dcp_ref[0] = dgc * silu_c
        dzc_ref[0] = dgc * cp_v * (sc * (1.0 + zc_v * (1.0 - sc)))

    def const(shape):
        return pl.BlockSpec(shape, lambda b, i: (0,) * len(shape))

    def tile(w):
        return pl.BlockSpec((1, TM, w), lambda b, i: (b, i, 0))

    return pl.pallas_call(
        body, name="out_fwd_bwd", grid=(bl, nt),
        in_specs=[tile(DA), tile(DA), tile(DC), tile(DC), tile(D), tile(D),
                  pl.BlockSpec((1, 3, D), lambda b, i: (b, 0, 0)), const((D, D))],
        out_specs=[pl.BlockSpec((1, DA // HD, TM, HD), lambda b, i: (b, 0, i, 0)), tile(DA), tile(DC), tile(DC), tile(D),
                   pl.BlockSpec((1, 1, D), lambda b, i: (b, 0, 0)), const((D, D)), const((1, D))],
        out_shape=[jax.ShapeDtypeStruct((bl, DA // HD, s, HD), BF16), jax.ShapeDtypeStruct((bl, s, DA), F32),
                   jax.ShapeDtypeStruct((bl, s, DC), F32), jax.ShapeDtypeStruct((bl, s, DC), F32),
                   jax.ShapeDtypeStruct((bl, s, D), F32), jax.ShapeDtypeStruct((bl, 1, D), F32),
                   jax.ShapeDtypeStruct((D, D), F32), jax.ShapeDtypeStruct((1, D), F32)],
        compiler_params=_params(("arbitrary", "arbitrary")),
    )(attn, za, cp, zc, x, target, modrows, w_out_b)


def _rms_heads_bwd(dy, x, w_t, ones_bd):
    r = lax.rsqrt(_segsum(x * x, ones_bd) * (1.0 / HD) + EPS)
    xh = x * r
    g = dy * w_t
    dx = r * (g - xh * (_segsum(g * xh, ones_bd) * (1.0 / HD)))
    return dx, dy * xh


def _ctx_bwd(ctx, modc, norm_w, w_kv_b, pkv_c, dk_c, dv_c, knw_t, ones_bd):
    bl, cl, _ = ctx.shape

    def body(x_ref, mod_ref, nw_ref, w_ref, p_ref, dk_ref, dv_ref, knw_ref, bd_ref, gw_ref, rows_ref, dknw_ref):
        @pl.when(pl.program_id(0) == 0)
        def _():
            gw_ref[...] = jnp.zeros_like(gw_ref)
            rows_ref[...] = jnp.zeros_like(rows_ref)
            dknw_ref[...] = jnp.zeros_like(dknw_ref)

        xv = x_ref[0]
        shift = mod_ref[0, 0:1, :]
        scale = mod_ref[0, 1:2, :]
        nw = nw_ref[...]
        r = lax.rsqrt(jnp.mean(xv * xv, axis=-1, keepdims=True) + EPS)
        xn = xv * r
        yv = xn * nw
        u = yv * (1.0 + scale) + shift
        dkv = jnp.concatenate([dk_ref[0, 0], dk_ref[0, 1]], axis=-1)
        dpk, dknw = _rms_heads_bwd(dkv, p_ref[0][:, 0:KVW], knw_ref[...], bd_ref[...])
        dp = jnp.concatenate([dpk.astype(BF16), dv_ref[0, 0].astype(BF16), dv_ref[0, 1].astype(BF16)], axis=-1)
        gw_ref[...] += _dot_tn(u.astype(BF16), dp)
        du = _dot_nt(dp, w_ref[...])
        rows_ref[0:1, :] += jnp.sum(du, axis=0, keepdims=True)
        rows_ref[1:2, :] += jnp.sum(du * yv, axis=0, keepdims=True)
        rows_ref[2:3, :] += jnp.sum(du * (1.0 + scale) * xn, axis=0, keepdims=True)
        dknw_ref[...] += jnp.sum(dknw, axis=0, keepdims=True)

    def const(shape):
        return pl.BlockSpec(shape, lambda b: (0,) * len(shape))

    def tile(w):
        return pl.BlockSpec((1, cl, w), lambda b: (b, 0, 0))

    kv_tile = pl.BlockSpec((1, KVW // HD, cl, HD), lambda b: (b, 0, 0, 0))
    return pl.pallas_call(
        body, name="ctx_bwd", grid=(bl,),
        in_specs=[tile(D), const((1, 3, D)), const((1, D)), const((D, 2 * KVW)), tile(2 * KVW), kv_tile, kv_tile,
                  const((1, KVW)), const((KVW, KVW))],
        out_specs=[const((D, 2 * KVW)), const((8, D)), const((1, KVW))],
        out_shape=[jax.ShapeDtypeStruct((D, 2 * KVW), F32), jax.ShapeDtypeStruct((8, D), F32),
                   jax.ShapeDtypeStruct((1, KVW), F32)],
        compiler_params=_params(("arbitrary",)),
    )(ctx, modc, norm_w, w_kv_b, pkv_c, dk_c, dv_c, knw_t, ones_bd)


def _bwd_in(x, modrows, norm_w, w_in_b, cos, sins, qnw_t, knw_t, ones_bd,
            pq, pkv, dq, dk, dv, dza, dglu, dzc, dh, gw_kv):
    bl, s, _ = x.shape
    nt = s // TM

    def body(x_ref, mod_ref, nw_ref, win_hbm, cos_ref, sin_ref, qnw_ref, knw_ref, bd_ref,
             pq_ref, pkv_ref, dq_ref, dk_ref, dv_ref, dza_ref, dglu_ref, dzc_ref, dh_ref, gwkv_ref,
             gx_ref, gw_hbm, dmod_ref, dnw_ref, dqnw_ref, dknw_ref, win_ref, gw_acc, sem):
        b, i = pl.program_id(0), pl.program_id(1)

        @pl.when((b == 0) & (i == 0))
        def _():
            cp = pltpu.make_async_copy(win_hbm, win_ref, sem)
            cp.start()
            gw_acc[...] = jnp.zeros_like(gw_acc)
            dnw_ref[...] = jnp.zeros_like(dnw_ref)
            dqnw_ref[...] = jnp.zeros_like(dqnw_ref)
            dknw_ref[...] = jnp.zeros_like(dknw_ref)
            cp.wait()

        @pl.when(i == 0)
        def _():
            dmod_ref[...] = jnp.zeros_like(dmod_ref)

        ck = cos_ref[...]
        sk = sin_ref[...]
        cs = jnp.concatenate([ck] * (DA // KVW), axis=-1)
        sn = jnp.concatenate([sk] * (DA // KVW), axis=-1)
        bd = bd_ref[...]
        dqn = _rope_bwd(dq_ref[0], cs, sn)
        dpq, dqnw = _rms_heads_bwd(dqn, pq_ref[0], qnw_ref[...], bd)
        dkn = _rope_bwd(jnp.concatenate([dk_ref[0, 0], dk_ref[0, 1]], axis=-1), ck, sk)
        dpk, dknw = _rms_heads_bwd(dkn, pkv_ref[0][:, 0:KVW], knw_ref[...], bd[0:KVW, 0:KVW])
        dqnw_ref[...] += jnp.sum(dqnw, axis=0, keepdims=True)
        dknw_ref[...] += jnp.sum(dknw, axis=0, keepdims=True)
        dp = jnp.concatenate(
            [dpq.astype(BF16), dpk.astype(BF16), dv_ref[0, 0].astype(BF16), dv_ref[0, 1].astype(BF16),
             dza_ref[0].astype(BF16), dglu_ref[0].astype(BF16), dzc_ref[0].astype(BF16)], axis=-1)

        xv = x_ref[0]
        shift = mod_ref[0, 0:1, :]
        scale = mod_ref[0, 1:2, :]
        nw = nw_ref[...]
        r = lax.rsqrt(jnp.mean(xv * xv, axis=-1, keepdims=True) + EPS)
        xn = xv * r
        yv = xn * nw
        u = yv * (1.0 + scale) + shift
        gw_acc[...] += _dot_tn(u.astype(BF16), dp)
        du = _dot_nt(dp, win_ref[...])
        dmod_ref[0, 0:1, :] += jnp.sum(du, axis=0, keepdims=True)
        dmod_ref[0, 1:2, :] += jnp.sum(du * yv, axis=0, keepdims=True)
        dy = du * (1.0 + scale)
        dnw_ref[...] += jnp.sum(dy * xn, axis=0, keepdims=True)
        dxn = dy * nw
        gx_ref[0] = dh_ref[0] + r * (dxn - xn * jnp.mean(dxn * xn, axis=-1, keepdims=True))

        @pl.when((b == bl - 1) & (i == nt - 1))
        def _():
            gw_acc[:, 512:768] += gwkv_ref[...]
            pltpu.sync_copy(gw_acc, gw_hbm)

    def tile(w):
        return pl.BlockSpec((1, TM, w), lambda b, i: (b, i, 0))

    def const(shape):
        return pl.BlockSpec(shape, lambda b, i: (0,) * len(shape))

    anyspace = pl.BlockSpec(memory_space=pl.ANY)
    rope = pl.BlockSpec((TM, KVW), lambda b, i: (i, 0))
    ctx_tiles = (dk.shape[2] - s) // TM
    kv_tile = pl.BlockSpec((1, KVW // HD, TM, HD), lambda b, i: (b, 0, ctx_tiles + i, 0))
    return pl.pallas_call(
        body, name="bwd_in", grid=(bl, nt),
        in_specs=[tile(D), pl.BlockSpec((1, 3, D), lambda b, i: (b, 0, 0)), const((1, D)), anyspace, rope, rope,
                  const((1, DA)), const((1, KVW)), const((DA, DA)),
                  tile(DA), tile(2 * KVW), tile(DA), kv_tile, kv_tile, tile(DA), tile(2 * DC), tile(DC), tile(D),
                  const((D, 2 * KVW))],
        out_specs=[tile(D), anyspace, pl.BlockSpec((1, 2, D), lambda b, i: (b, 0, 0)), const((1, D)),
                   const((1, DA)), const((1, KVW))],
        out_shape=[jax.ShapeDtypeStruct((bl, s, D), F32), jax.ShapeDtypeStruct((D, D_IN), F32),
                   jax.ShapeDtypeStruct((bl, 2, D), F32), jax.ShapeDtypeStruct((1, D), F32),
                   jax.ShapeDtypeStruct((1, DA), F32), jax.ShapeDtypeStruct((1, KVW), F32)],
        scratch_shapes=[pltpu.VMEM((D, D_IN), BF16), pltpu.VMEM((D, D_IN), F32), pltpu.SemaphoreType.DMA],
        compiler_params=_params(("arbitrary", "arbitrary")),
    )(x, modrows, norm_w, w_in_b, cos, sins, qnw_t, knw_t, ones_bd,
      pq, pkv, dq, dk, dv, dza, dglu, dzc, dh, gw_kv)


_LOSS, _DMODC, _NW, _QN, _KN, _CB, _LW, _LB, _BPW, SMALL_W = 0, 1024, 4096, 5120, 5248, 5376, 5888, 6400, 6912, 7424


def _pack_small(loss_row, ctx_rows, dnw, dqnw, dknw, dknw_c, conv_rows, dmod_ss, dgate):
    bl = dmod_ss.shape[0]

    def body(loss_ref, ctx_ref, dnw_ref, dqnw_ref, dknw_ref, dknwc_ref, conv_ref, dss_ref, dgate_ref, o_ref):
        o_ref[...] = jnp.zeros_like(o_ref)
        o_ref[0:1, _LOSS:_LOSS + D] = loss_ref[...]
        o_ref[0:1, _DMODC:_DMODC + D] = ctx_ref[0:1, :]
        o_ref[0:1, _DMODC + D:_DMODC + 2 * D] = ctx_ref[1:2, :]
        o_ref[0:1, _NW:_NW + D] = dnw_ref[...] + ctx_ref[2:3, :]
        dq = dqnw_ref[...]
        qn = dq[:, 0:HD]
        for h in range(1, DA // HD):
            qn = qn + dq[:, h * HD:(h + 1) * HD]
        o_ref[0:1, _QN:_QN + HD] = qn
        dk = dknw_ref[...] + dknwc_ref[...]
        o_ref[0:1, _KN:_KN + HD] = dk[:, 0:HD] + dk[:, HD:2 * HD]
        o_ref[0:1, _BPW:_BPW + DC] = conv_ref[0:1, :]
        o_ref[0:1, _LW:_LW + DC] = conv_ref[1:2, :]
        o_ref[0:1, _LB:_LB + DC] = conv_ref[2:3, :]
        o_ref[0:1, _CB:_CB + DC] = conv_ref[3:4, :]
        for b in range(bl):
            o_ref[1 + b:2 + b, 0:D] = dss_ref[b, 0:1, :]
            o_ref[1 + b:2 + b, D:2 * D] = dss_ref[b, 1:2, :]
            o_ref[1 + b:2 + b, 2 * D:3 * D] = dgate_ref[b]

    return pl.pallas_call(
        body, name="pack_small", out_shape=jax.ShapeDtypeStruct((8, SMALL_W), F32),
    )(loss_row, ctx_rows, dnw, dqnw, dknw, dknw_c, conv_rows, dmod_ss, dgate)


def _finish_small(gathered, bl):
    n_ex = N_DEV * bl

    def body(g_ref, sum_ref, dmod_ref, gb_ref, loss_ref):
        acc = g_ref[0, 0:1, :]
        for j in range(1, N_DEV):
            acc = acc + g_ref[j, 0:1, :]
        sum_ref[...] = acc
        dmod_ref[...] = jnp.zeros_like(dmod_ref)
        for j in range(N_DEV):
            dmod_ref[j * bl:(j + 1) * bl, :] = g_ref[j, 1:1 + bl, 0:3 * D]
        dmod_ref[n_ex:n_ex + 1, :] = acc[:, _DMODC:_DMODC + 3 * D]
        gb_ref[...] = jnp.sum(dmod_ref[...], axis=0, keepdims=True)
        loss_ref[...] = (0.5 / D) * jnp.sum(acc[:, _LOSS:_LOSS + D], axis=-1, keepdims=True)

    return pl.pallas_call(
        body, name="finish_small",
        out_shape=[jax.ShapeDtypeStruct((1, SMALL_W), F32), jax.ShapeDtypeStruct((n_ex + 8, 3 * D), F32),
                   jax.ShapeDtypeStruct((1, 3 * D), F32), jax.ShapeDtypeStruct((1, 1), F32)],
    )(gathered)


_SMALL = (("b_mod", None), ("norm_w", _NW), ("q_norm_w", _QN), ("k_norm_w", _KN), ("conv_b", _CB),
          ("conv_ln_w", _LW), ("conv_ln_b", _LB), ("b_pw", _BPW), ("c_ctx", None))


def _adamw_small(summed, g_bmod, gc_parts, weights, moms, vars_):
    n = len(_SMALL)

    def body(*refs):
        sum_ref, gb_ref, gc_ref = refs[0:3]
        w_refs, m_refs, v_refs = refs[3:3 + n], refs[3 + n:3 + 2 * n], refs[3 + 2 * n:3 + 3 * n]
        outs = refs[3 + 3 * n:]
        for k, (name, off) in enumerate(_SMALL):
            w = w_refs[k][...]
            if name == "b_mod":
                g = gb_ref[...]
            elif name == "c_ctx":
                acc = gc_ref[0, 0:1, :]
                for j in range(1, N_DEV):
                    acc = acc + gc_ref[j, 0:1, :]
                sg = _sigmoid(w)
                g = acc * (sg * (1.0 + w * (1.0 - sg)))
            else:
                g = sum_ref[:, off:off + w.shape[1]]
            delta, m2, v2 = _adamw(w, g, m_refs[k][...], v_refs[k][...])
            outs[k][...] = g
            outs[n + k][...] = delta
            outs[2 * n + k][...] = m2
            outs[3 * n + k][...] = v2

    shapes = [jax.ShapeDtypeStruct(w.shape, F32) for w in weights]
    outs = pl.pallas_call(body, name="adamw_small", out_shape=shapes * 4)(
        summed, g_bmod, gc_parts, *weights, *moms, *vars_)
    return [outs[k * n:(k + 1) * n] for k in range(4)]


def _rope_tables(s):
    t = jnp.arange(s, dtype=jnp.int32)
    row = (t // GRID_W).astype(F32)
    col = (t % GRID_W).astype(F32)
    freqs = ROPE_THETA ** (-jnp.arange(0, HD // 2, 2, dtype=F32) / (HD // 2))
    ang_r = row[:, None] * freqs[None, :]
    ang_c = col[:, None] * freqs[None, :]
    cr, sr, cc, sc = jnp.cos(ang_r), jnp.sin(ang_r), jnp.cos(ang_c), jnp.sin(ang_c)
    cos = jnp.concatenate([cr, cr, cc, cc], axis=-1)
    sins = jnp.concatenate([-sr, sr, -sc, sc], axis=-1)
    return jnp.tile(cos, (1, KVW // HD)), jnp.tile(sins, (1, KVW // HD))


def kernel(x, c, ctx, c_ctx, w_mod, b_mod, norm_w, w_in, q_norm_w, k_norm_w, conv_w, conv_b, conv_ln_w, conv_ln_b, w_pw, b_pw, w_out, loss_target, m_c_ctx, m_w_mod, m_b_mod, m_norm_w, m_w_in, m_q_norm_w, m_k_norm_w, m_conv_w, m_conv_b, m_conv_ln_w, m_conv_ln_b, m_w_pw, m_b_pw, m_w_out, v_c_ctx, v_w_mod, v_b_mod, v_norm_w, v_w_in, v_q_norm_w, v_k_norm_w, v_conv_w, v_conv_b, v_conv_ln_w, v_conv_ln_b, v_w_pw, v_b_pw, v_w_out):
    bl, s, _ = x.shape
    cl = ctx.shape[1]
    me = _lin(*_coords())
    n_mod = w_mod.shape[2]

    conv_w_pad = jnp.pad(conv_w[0], ((0, 32 - KW), (0, 0)))
    c_pad = jnp.pad(c, ((0, 8 - bl), (0, 0)))
    g_win, g_wout, g_wpw, g_cw, g_c = _all_gather_many(
        [w_in[0].astype(BF16), w_out[0].astype(BF16), w_pw[0].astype(BF16), conv_w_pad, c_pad], "gather_weights")
    w_in_b = g_win.transpose(1, 0, 2).reshape(D, D_IN)
    w_out_b = g_wout.reshape(D, D)
    w_pw_b = g_wpw.reshape(DC, DC)
    conv_w_full = g_cw.transpose(1, 0, 2).reshape(32, DC)
    c_all = g_c[:, :bl, :].reshape(N_DEV * bl, D)
    n_ex = N_DEV * bl
    c_rows = jnp.concatenate([c_all, c_ctx[None, :], jnp.zeros((7, D), F32)], axis=0)

    b_mod_loc = lax.dynamic_slice_in_dim(b_mod, me * n_mod, n_mod, axis=1)
    mod_loc = _mod_fwd(c_rows, w_mod[0], b_mod_loc)
    g_mod = _all_gather_direct(mod_loc, "gather_mod")
    mod_all = g_mod.transpose(1, 0, 2).reshape(n_ex + 8, 3 * D)
    modrows = lax.dynamic_slice_in_dim(mod_all, me * bl, bl, axis=0).reshape(bl, 3, D)
    modc = mod_all[n_ex].reshape(1, 3, D)

    cos, sins = _rope_tables(s)
    qnw_t = jnp.tile(q_norm_w, (1, DA // HD))
    knw_t = jnp.tile(k_norm_w, (1, KVW // HD))
    lane = jnp.arange(DA, dtype=jnp.int32) // HD
    ones_bd = (lane[:, None] == lane[None, :]).astype(BF16)
    ones_kv = ones_bd[0:KVW, 0:KVW]
    w_kv_b = w_in_b[:, 512:768]

    k_ctx, v_ctx, pkv_c = _ctx_fwd(ctx, modc, norm_w, w_kv_b, knw_t, ones_kv, cl + s)
    q_h, k_h, v_h, pq, pkv, za, glu, zc = _fwd_in(
        x, modrows, norm_w, w_in_b, cos, sins, qnw_t, knw_t, ones_bd, k_ctx, v_ctx)
    attn, lse = _attn_fwd(q_h, k_h, v_h)
    y_conv, cp = _conv_fwd(glu, conv_w_full, conv_b, conv_ln_w, conv_ln_b, w_pw_b, b_pw)

    do_h, dza, dcp, dzc, dh, dgate, gw_out, loss_row = _out_fwd_bwd(attn, za, cp, zc, x, loss_target, modrows, w_out_b)
    dy_conv, gw_pw, conv_rows = _conv_bwd_pointwise(y_conv, dcp, conv_ln_w, conv_ln_b, w_pw_b)
    dglu, g_cw_full = _conv_bwd_depthwise(glu, dy_conv, conv_w_full)
    dq, dk_h, dv_h = _attn_bwd(q_h, k_h, v_h, do_h, attn, lse)
    gw_kv, ctx_rows, dknw_c = _ctx_bwd(ctx, modc, norm_w, w_kv_b, pkv_c, dk_h, dv_h, knw_t, ones_kv)
    grad_x, gw_in, dmod_ss, dnw, dqnw, dknw = _bwd_in(
        x, modrows, norm_w, w_in_b, cos, sins, qnw_t, knw_t, ones_bd,
        pq, pkv, dq, dk_h, dv_h, dza, dglu, dzc, dh, gw_kv)

    parts_in = gw_in.astype(BF16).reshape(D, N_DEV, D_IN // N_DEV).transpose(1, 0, 2)
    parts_out = gw_out.astype(BF16).reshape(N_DEV, D // N_DEV, D)
    parts_pw = gw_pw.astype(BF16).reshape(N_DEV, DC // N_DEV, DC)
    parts_cw = g_cw_full.astype(BF16).reshape(32, N_DEV, DC // N_DEV).transpose(1, 0, 2)

    def pad_cw(a):
        return jnp.pad(a[0], ((0, 32 - KW), (0, 0)))

    (r_in, r_out, r_pw, r_cw) = _reduce_scatter_adamw(
        [(parts_in, w_in[0], m_w_in[0], v_w_in[0]), (parts_out, w_out[0], m_w_out[0], v_w_out[0]),
         (parts_pw, w_pw[0], m_w_pw[0], v_w_pw[0]), (parts_cw, pad_cw(conv_w), pad_cw(m_conv_w), pad_cw(v_conv_w))],
        "reduce_grads")
    r_cw = tuple(a[:KW] for a in r_cw)

    small = _pack_small(loss_row, ctx_rows, dnw, dqnw, dknw, dknw_c, conv_rows, dmod_ss, dgate)
    g_small = _all_gather_direct(small, "gather_small")
    summed, dmod_full, g_bmod, loss11 = _finish_small(g_small, bl)

    dmod_loc = lax.dynamic_slice_in_dim(dmod_full, me * n_mod, n_mod, axis=1)
    g_wmod, d_wmod, nm_wmod, nv_wmod, gc_part = _mod_bwd(c_rows, dmod_loc, w_mod[0], m_w_mod[0], v_w_mod[0])
    g_gc = _all_gather_direct(gc_part, "gather_c_ctx")

    given = {"c_ctx": (c_ctx, m_c_ctx, v_c_ctx), "b_mod": (b_mod, m_b_mod, v_b_mod), "norm_w": (norm_w, m_norm_w, v_norm_w),
             "q_norm_w": (q_norm_w, m_q_norm_w, v_q_norm_w), "k_norm_w": (k_norm_w, m_k_norm_w, v_k_norm_w),
             "conv_b": (conv_b, m_conv_b, v_conv_b), "conv_ln_w": (conv_ln_w, m_conv_ln_w, v_conv_ln_w),
             "conv_ln_b": (conv_ln_b, m_conv_ln_b, v_conv_ln_b), "b_pw": (b_pw, m_b_pw, v_b_pw)}
    as_rows = [[given[name][which].reshape(1, -1) for name, _ in _SMALL] for which in range(3)]
    small_outs = _adamw_small(summed, g_bmod, g_gc, *as_rows)

    big = {"w_mod": (g_wmod, d_wmod, nm_wmod, nv_wmod), "w_in": r_in, "conv_w": r_cw, "w_pw": r_pw, "w_out": r_out}
    order = ["c_ctx", "w_mod", "b_mod", "norm_w", "w_in", "q_norm_w", "k_norm_w", "conv_w", "conv_b", "conv_ln_w",
             "conv_ln_b", "w_pw", "b_pw", "w_out"]
    small_index = {name: k for k, (name, _) in enumerate(_SMALL)}
    outs = [loss11.reshape(()), grad_x]
    for which in range(4):
        for name in order:
            if name in big:
                outs.append(big[name][which][None])
            else:
                outs.append(small_outs[which][small_index[name]].reshape(given[name][0].shape))
    return tuple(outs)
```

```python
import functools

import jax
import jax.numpy as jnp
from jax import lax
from jax.experimental import pallas as pl
from jax.experimental.pallas import tpu as pltpu

F32, BF16 = jnp.float32, jnp.bfloat16
MESH_ID = pl.DeviceIdType.MESH

N_DEV = 8
D = 1024
D_IN = 2816
DA = 512
DC = 512
HD = 64
KVW = 128
KW = 31
HALO = 16
EPS = 1e-6
ROPE_THETA = 10000.0
GRID_W = 64

ADAM_LR, ADAM_B1, ADAM_B2, ADAM_EPS, ADAM_WD, ADAM_STEP = 0.001, 0.9, 0.999, 1e-08, 0.01, 10

VMEM_LIMIT = 56 * 1024 * 1024

TM = 256
TQ = 128
BWD_PARTS = 1
FWD_PARTS = 4
TC = 512
CH = 64


def _params(sem, vmem=VMEM_LIMIT):
    return pltpu.CompilerParams(dimension_semantics=sem, vmem_limit_bytes=vmem)


def _dot(a, b):
    return jnp.dot(a, b, preferred_element_type=F32)


def _dot_nt(a, b):
    return lax.dot_general(a, b, (((1,), (1,)), ((), ())), preferred_element_type=F32)


def _dot_tn(a, b):
    return lax.dot_general(a, b, (((0,), (0,)), ((), ())), preferred_element_type=F32)


def _sigmoid(z):
    return 1.0 / (1.0 + jnp.exp(-z))


def _segsum(v, ones_bd):
    hi = v.astype(BF16)
    lo = (v - hi.astype(F32)).astype(BF16)
    return _dot(hi, ones_bd) + _dot(lo, ones_bd)


def _swap16(x):
    w = x.shape[-1]
    lane = lax.broadcasted_iota(jnp.int32, x.shape, 1)
    return jnp.where((lane % 32) < 16, pltpu.roll(x, w - 16, 1), pltpu.roll(x, 16, 1))


def _with_ones_column(v):
    one = (lax.broadcasted_iota(jnp.int32, v.shape, 1) == 0).astype(v.dtype)
    return jnp.concatenate([v, one], axis=-1)


def _rope(x, cos, sins):
    return x * cos + _swap16(x) * sins


def _rope_bwd(d, cos, sins):
    return d * cos + _swap16(d * sins)


def _adamw(w, g, m, v):
    m2 = ADAM_B1 * m + (1.0 - ADAM_B1) * g
    v2 = ADAM_B2 * v + (1.0 - ADAM_B2) * (g * g)
    m_hat = m2 / (1.0 - ADAM_B1 ** ADAM_STEP)
    v_hat = v2 / (1.0 - ADAM_B2 ** ADAM_STEP)
    delta = -ADAM_LR * (m_hat / (jnp.sqrt(v_hat) + ADAM_EPS) + ADAM_WD * w)
    return delta, m2, v2


def _coords():
    return lax.axis_index("x"), lax.axis_index("y"), lax.axis_index("c")


def _lin(x, y, c):
    return 4 * x + 2 * y + c


def _all_gather_many(arrs, name):
    n = len(arrs)

    def body(*refs):
        in_refs, out_refs = refs[:n], refs[n:2 * n]
        send_sems, recv_sems, local_sems = refs[2 * n:]
        x, y, c = _coords()
        me, sib = (x, y, c), (x, y, 1 - c)
        chips = [(1 - x, y), (x, 1 - y), (1 - x, 1 - y)]

        def copy(a, k, block, to, src=None):
            slot = out_refs[a].at[_lin(*block)]
            return pltpu.make_async_remote_copy(
                src_ref=slot if src is None else src, dst_ref=slot,
                send_sem=send_sems.at[a * 7 + k], recv_sem=recv_sems.at[a * 7 + k],
                device_id=to, device_id_type=MESH_ID)

        mine = [pltpu.make_async_copy(in_refs[a], out_refs[a].at[_lin(*me)], local_sems.at[a]) for a in range(n)]
        for cp in mine:
            cp.start()
        first = []
        for a in range(n):
            first.append(copy(a, 0, me, sib, src=in_refs[a]))
            first += [copy(a, 1 + j, me, (*chip, c), src=in_refs[a]) for j, chip in enumerate(chips)]
        for cp in first:
            cp.start()
        passed = []
        for a in range(n):
            for j, chip in enumerate(chips):
                copy(a, 1 + j, (*chip, c), me).wait_recv()
                fwd = copy(a, 4 + j, (*chip, c), sib)
                fwd.start()
                passed.append(fwd)
        for a in range(n):
            copy(a, 0, sib, me).wait_recv()
            for j, chip in enumerate(chips):
                copy(a, 4 + j, (*chip, 1 - c), me).wait_recv()
        for cp in first + passed:
            cp.wait_send()
        for cp in mine:
            cp.wait()

    vm = pl.BlockSpec(memory_space=pltpu.VMEM)
    return pl.pallas_call(
        body, name=name,
        out_shape=[jax.ShapeDtypeStruct((N_DEV,) + a.shape, a.dtype) for a in arrs],
        in_specs=[vm] * n, out_specs=[vm] * n,
        scratch_shapes=[pltpu.SemaphoreType.DMA((7 * n,)), pltpu.SemaphoreType.DMA((7 * n,)),
                        pltpu.SemaphoreType.DMA((n,))],
        compiler_params=pltpu.CompilerParams(vmem_limit_bytes=VMEM_LIMIT),
    )(*arrs)


def _exchange_copies(in_refs, out_refs, send_sems, recv_sems, local_sems, scatter):
    x, y, c = _coords()
    me = _lin(x, y, c)
    local, remote = [], []
    for a, (src, dst) in enumerate(zip(in_refs, out_refs)):
        local.append(pltpu.make_async_copy(src.at[me] if scatter else src, dst.at[me], local_sems.at[a]))
        for k in range(1, N_DEV):
            peer = (1 - x if k & 4 else x, 1 - y if k & 2 else y, 1 - c if k & 1 else c)
            remote.append(pltpu.make_async_remote_copy(
                src_ref=src.at[_lin(*peer)] if scatter else src, dst_ref=dst.at[me],
                send_sem=send_sems.at[a * (N_DEV - 1) + k - 1], recv_sem=recv_sems.at[a * (N_DEV - 1) + k - 1],
                device_id=peer, device_id_type=MESH_ID))
    return local, remote


def _exchange_scratch(n):
    return [pltpu.SemaphoreType.DMA((n * (N_DEV - 1),)), pltpu.SemaphoreType.DMA((n * (N_DEV - 1),)),
            pltpu.SemaphoreType.DMA((n,))]


def _all_gather_direct(arr, name):
    def body(in_ref, out_ref, send_sems, recv_sems, local_sem):
        x, y, c = _coords()
        me = _lin(x, y, c)
        mine = pltpu.make_async_copy(in_ref, out_ref.at[me], local_sem)
        mine.start()
        copies = []
        for k in range(1, N_DEV):
            peer = (1 - x if k & 4 else x, 1 - y if k & 2 else y, 1 - c if k & 1 else c)
            cp = pltpu.make_async_remote_copy(
                src_ref=in_ref, dst_ref=out_ref.at[me], send_sem=send_sems.at[k - 1], recv_sem=recv_sems.at[k - 1],
                device_id=peer, device_id_type=MESH_ID)
            cp.start()
            copies.append(cp)
        for cp in copies:
            cp.wait_recv()
        for cp in copies:
            cp.wait_send()
        mine.wait()

    vm = pl.BlockSpec(memory_space=pltpu.VMEM)
    return pl.pallas_call(
        body, name=name, out_shape=jax.ShapeDtypeStruct((N_DEV,) + arr.shape, arr.dtype),
        in_specs=[vm], out_specs=vm,
        scratch_shapes=[pltpu.SemaphoreType.DMA((N_DEV - 1,)), pltpu.SemaphoreType.DMA((N_DEV - 1,)),
                        pltpu.SemaphoreType.DMA],
    )(arr)


def _reduce_scatter_adamw(items, name):
    n = len(items)
    rb = 32

    def body(*refs):
        parts = refs[0:n]
        wmv = refs[n:4 * n]
        outs = refs[4 * n:8 * n]
        bufs = [refs[8 * n + 4 * a:8 * n + 4 * a + 4] for a in range(n)]
        d2d_send, d2d_recv, ici_send, ici_recv, local_sems = refs[12 * n:]
        x, y, c = _coords()
        sib = (x, y, 1 - c)
        peers = [(1 - x, y), (x, 1 - y), (1 - x, 1 - y)]
        home = 2 * x + y

        def rows_loop(rows, fn):
            def step(i, carry):
                fn(pl.ds(pl.multiple_of(i * rb, rb), rb))
                return carry
            lax.fori_loop(0, rows // rb, step, 0)

        local, d2d, ici = [], [], []
        for a in range(n):
            mine, got_sib = bufs[a][0], bufs[a][1]
            for s in range(4):
                cp = pltpu.make_async_copy(parts[a].at[_lin(s // 2, s % 2, c)], mine.at[s], local_sems.at[4 * a + s])
                cp.start()
                local.append(cp)
                rc = pltpu.make_async_remote_copy(
                    src_ref=parts[a].at[_lin(s // 2, s % 2, 1 - c)], dst_ref=got_sib.at[s],
                    send_sem=d2d_send.at[4 * a + s], recv_sem=d2d_recv.at[4 * a + s],
                    device_id=sib, device_id_type=MESH_ID)
                rc.start()
                d2d.append(rc)

        for a in range(n):
            mine, got_sib, stage, got_chip = bufs[a]
            for s in range(4):
                local[4 * a + s].wait()
                d2d[4 * a + s].wait_recv()
            for k, (px, py) in enumerate(peers):
                slot = 2 * px + py

                def pair_sum(rs, k=k, slot=slot, mine=mine, got_sib=got_sib, stage=stage):
                    stage[k, rs, :] = (mine[slot, rs, :].astype(F32) + got_sib[slot, rs, :].astype(F32)).astype(BF16)

                rows_loop(wmv[3 * a].shape[0], pair_sum)
                rc = pltpu.make_async_remote_copy(
                    src_ref=stage.at[k], dst_ref=got_chip.at[k],
                    send_sem=ici_send.at[3 * a + k], recv_sem=ici_recv.at[3 * a + k],
                    device_id=(px, py, c), device_id_type=MESH_ID)
                rc.start()
                ici.append(rc)

        for a in range(n):
            mine, got_sib, stage, got_chip = bufs[a]
            w_ref, m_ref, v_ref = wmv[3 * a:3 * a + 3]
            g_ref, d_ref, nm_ref, nv_ref = outs[4 * a:4 * a + 4]
            for k in range(3):
                ici[3 * a + k].wait_recv()

            def finish(rs, mine=mine, got_sib=got_sib, got_chip=got_chip, w_ref=w_ref, m_ref=m_ref, v_ref=v_ref,
                       g_ref=g_ref, d_ref=d_ref, nm_ref=nm_ref, nv_ref=nv_ref):
                g = mine[home, rs, :].astype(F32) + got_sib[home, rs, :].astype(F32)
                for k in range(3):
                    g = g + got_chip[k, rs, :].astype(F32)
                delta, m2, v2 = _adamw(w_ref[rs, :], g, m_ref[rs, :], v_ref[rs, :])
                g_ref[rs, :] = g
                d_ref[rs, :] = delta
                nm_ref[rs, :] = m2
                nv_ref[rs, :] = v2

            rows_loop(w_ref.shape[0], finish)

        for rc in d2d + ici:
            rc.wait_send()

    vm = pl.BlockSpec(memory_space=pltpu.VMEM)
    anyspace = pl.BlockSpec(memory_space=pl.ANY)
    args, in_specs, out_shape, scratch = [], [], [], []
    for parts, w, m, v in items:
        assert w.shape[0] % rb == 0 and parts.shape == (N_DEV,) + w.shape and parts.dtype == BF16
    args += [it[0] for it in items]
    in_specs += [anyspace] * n
    for _, w, m, v in items:
        args += [w, m, v]
        in_specs += [vm] * 3
        out_shape += [jax.ShapeDtypeStruct(w.shape, F32)] * 4
    for it in items:
        shp = it[1].shape
        scratch += [pltpu.VMEM((4,) + shp, BF16), pltpu.VMEM((4,) + shp, BF16),
                    pltpu.VMEM((3,) + shp, BF16), pltpu.VMEM((3,) + shp, BF16)]
    scratch += [pltpu.SemaphoreType.DMA((4 * n,)), pltpu.SemaphoreType.DMA((4 * n,)),
                pltpu.SemaphoreType.DMA((3 * n,)), pltpu.SemaphoreType.DMA((3 * n,)), pltpu.SemaphoreType.DMA((4 * n,))]
    outs = pl.pallas_call(
        body, name=name, out_shape=out_shape, in_specs=in_specs, out_specs=[vm] * (4 * n),
        scratch_shapes=scratch, compiler_params=pltpu.CompilerParams(vmem_limit_bytes=VMEM_LIMIT),
    )(*args)
    return [tuple(outs[4 * a:4 * a + 4]) for a in range(n)]


def _sum_devices_adamw(items):
    n = len(items)

    def body(*refs):
        for a in range(n):
            got, w_ref, m_ref, v_ref = refs[4 * a:4 * a + 4]
            g_ref, d_ref, nm_ref, nv_ref = refs[4 * n + 4 * a:4 * n + 4 * a + 4]
            g = got[0].astype(F32)
            for j in range(1, N_DEV):
                g = g + got[j].astype(F32)
            delta, m2, v2 = _adamw(w_ref[...], g, m_ref[...], v_ref[...])
            g_ref[...] = g
            d_ref[...] = delta
            nm_ref[...] = m2
            nv_ref[...] = v2

    args, out_shape = [], []
    for got, w, m, v in items:
        assert got.shape == (N_DEV,) + w.shape
        args += [got, w, m, v]
        out_shape += [jax.ShapeDtypeStruct(w.shape, F32)] * 4
    outs = pl.pallas_call(body, name="sum_devices_adamw", out_shape=out_shape,
                          compiler_params=pltpu.CompilerParams(vmem_limit_bytes=VMEM_LIMIT))(*args)
    return [tuple(outs[4 * a:4 * a + 4]) for a in range(n)]


def _mod_fwd(c_rows, w_mod_loc, b_mod_loc):
    def body(c_ref, w_ref, b_ref, o_ref):
        cr = c_ref[...]
        a = (cr * _sigmoid(cr)).astype(BF16)
        o_ref[...] = _dot(a, w_ref[...].astype(BF16)) + b_ref[...]

    return pl.pallas_call(
        body, name="mod_fwd", out_shape=jax.ShapeDtypeStruct((c_rows.shape[0], w_mod_loc.shape[1]), F32),
        compiler_params=pltpu.CompilerParams(vmem_limit_bytes=VMEM_LIMIT),
    )(c_rows, w_mod_loc, b_mod_loc)


def _mod_bwd(c_rows, dmod_loc, w_mod_loc, m, v):
    def body(c_ref, dm_ref, w_ref, m_ref, v_ref, g_ref, d_ref, nm_ref, nv_ref, gc_ref):
        cr = c_ref[...]
        a = (cr * _sigmoid(cr)).astype(BF16)
        dm = dm_ref[...].astype(BF16)
        g = _dot_tn(a, dm)
        w = w_ref[...]
        delta, m2, v2 = _adamw(w, g, m_ref[...], v_ref[...])
        g_ref[...] = g
        d_ref[...] = delta
        nm_ref[...] = m2
        nv_ref[...] = v2
        gc_ref[...] = _dot_nt(dm[16:24, :], w.astype(BF16))

    shp = jax.ShapeDtypeStruct(w_mod_loc.shape, F32)
    return pl.pallas_call(
        body, name="mod_bwd", out_shape=[shp, shp, shp, shp, jax.ShapeDtypeStruct((8, D), F32)],
        compiler_params=pltpu.CompilerParams(vmem_limit_bytes=VMEM_LIMIT),
    )(c_rows, dmod_loc, w_mod_loc, m, v)


def _fwd_in(x, modrows, norm_w, w_in_b, cos, sins, qnw_t, knw_t, ones_bd, k_all, v_all, shards):
    bl, s, _ = x.shape
    nt = s // TM
    ctx_tiles = (k_all.shape[2] - s) // TM
    assert ctx_tiles * TM + s == k_all.shape[2]
    n_sh = len(shards)

    def body(*refs):
        (x_ref, mod_ref, nw_ref, win_ref, cos_ref, sin_ref, qnw_ref, knw_ref, bd_ref, kin_ref, vin_ref) = refs[:11]
        shard_refs = refs[11:11 + n_sh]
        q_ref, k_ref, v_ref, pq_ref, pkv_ref, za_ref, glu_ref, zc_ref = refs[11 + n_sh:19 + n_sh]
        gathered_refs = refs[19 + n_sh:19 + 2 * n_sh]
        send_sems, recv_sems, local_sems = refs[19 + 2 * n_sh:]
        b, i = pl.program_id(0), pl.program_id(1)
        local, remote = _exchange_copies(shard_refs, gathered_refs, send_sems, recv_sems, local_sems, scatter=False)

        @pl.when((b == 0) & (i == 0))
        def _():
            for cp in local + remote:
                cp.start()

        xv = x_ref[0]
        shift = mod_ref[0, 0:1, :]
        scale = mod_ref[0, 1:2, :]
        r = lax.rsqrt(jnp.mean(xv * xv, axis=-1, keepdims=True) + EPS)
        u = (xv * r * nw_ref[...]) * (1.0 + scale) + shift
        p = _dot(u.astype(BF16), win_ref[...])
        pq = p[:, 0:DA]
        pk = p[:, DA:DA + HD * 2]
        ck = cos_ref[...]
        sk = sin_ref[...]
        cs = jnp.concatenate([ck] * (DA // KVW), axis=-1)
        sn = jnp.concatenate([sk] * (DA // KVW), axis=-1)
        rq = lax.rsqrt(_segsum(pq * pq, bd_ref[...]) * (1.0 / HD) + EPS)
        qn = pq * rq * qnw_ref[...]
        qr = _rope(qn, cs, sn) * 0.125
        for h in range(DA // HD):
            q_ref[0, h] = qr[:, h * HD:(h + 1) * HD].astype(BF16)
        rk = lax.rsqrt(_segsum(pk * pk, bd_ref[0:KVW, 0:KVW]) * (1.0 / HD) + EPS)
        kn = pk * rk * knw_ref[...]
        kr = _rope(kn, ck, sk)
        pv = p[:, 640:768]
        for h in range(KVW // HD):
            k_ref[0, h] = kr[:, h * HD:(h + 1) * HD].astype(BF16)
            v_ref[0, h] = _with_ones_column(pv[:, h * HD:(h + 1) * HD]).astype(BF16)
        pq_ref[0] = pq
        pkv_ref[0] = p[:, 512:768]
        za_ref[0] = p[:, 768:1280]
        glu_ref[0] = p[:, 1280:2304]
        zc_ref[0] = p[:, 2304:2816]

        @pl.when((b == bl - 1) & (i == nt - 1))
        def _():
            for cp in remote:
                cp.wait_recv()
            for cp in remote:
                cp.wait_send()
            for cp in local:
                cp.wait()

    def tile(w):
        return pl.BlockSpec((1, TM, w), lambda b, i: (b, i, 0))

    def const(shape):
        return pl.BlockSpec(shape, lambda b, i: (0,) * len(shape))

    outs = [(DA, F32), (2 * KVW, F32), (DA, F32), (2 * DC, F32), (DC, F32)]
    anyspace = pl.BlockSpec(memory_space=pl.ANY)
    rope = pl.BlockSpec((TM, KVW), lambda b, i: (i, 0))
    k_tile = pl.BlockSpec((1, KVW // HD, TM, HD), lambda b, i: (b, 0, ctx_tiles + i, 0))
    v_tile = pl.BlockSpec((1, KVW // HD, TM, 2 * HD), lambda b, i: (b, 0, ctx_tiles + i, 0))
    res = pl.pallas_call(
        body, name="fwd_in", grid=(bl, nt),
        in_specs=[tile(D), pl.BlockSpec((1, 3, D), lambda b, i: (b, 0, 0)), const((1, D)), const((D, D_IN)),
                  rope, rope, const((1, DA)), const((1, KVW)), const((DA, DA)), anyspace, anyspace]
        + [anyspace] * n_sh,
        out_specs=[pl.BlockSpec((1, DA // HD, TM, HD), lambda b, i: (b, 0, i, 0)), k_tile, v_tile]
        + [tile(w) for w, _ in outs] + [anyspace] * n_sh,
        out_shape=[jax.ShapeDtypeStruct((bl, DA // HD, s, HD), BF16), jax.ShapeDtypeStruct(k_all.shape, BF16),
                   jax.ShapeDtypeStruct(v_all.shape, BF16)]
        + [jax.ShapeDtypeStruct((bl, s, w), dt) for w, dt in outs]
        + [jax.ShapeDtypeStruct((N_DEV,) + a.shape, a.dtype) for a in shards],
        input_output_aliases={9: 1, 10: 2},
        scratch_shapes=_exchange_scratch(n_sh),
        compiler_params=_params(("arbitrary", "arbitrary")),
    )(x, modrows, norm_w, w_in_b, cos, sins, qnw_t, knw_t, ones_bd, k_all, v_all, *shards)
    return res[:8], res[8:]


def _ctx_fwd(ctx, modc, norm_w, w_kv_b, knw_t, ones_bd, n_keys):
    bl, cl, _ = ctx.shape

    def body(x_ref, mod_ref, nw_ref, w_ref, knw_ref, bd_ref, k_ref, v_ref, pkv_ref):
        xv = x_ref[0]
        shift = mod_ref[0, 0:1, :]
        scale = mod_ref[0, 1:2, :]
        r = lax.rsqrt(jnp.mean(xv * xv, axis=-1, keepdims=True) + EPS)
        u = (xv * r * nw_ref[...]) * (1.0 + scale) + shift
        p = _dot(u.astype(BF16), w_ref[...])
        pk = p[:, 0:KVW]
        rk = lax.rsqrt(_segsum(pk * pk, bd_ref[...]) * (1.0 / HD) + EPS)
        kn = pk * rk * knw_ref[...]
        pv = p[:, KVW:2 * KVW]
        for h in range(KVW // HD):
            k_ref[0, h] = kn[:, h * HD:(h + 1) * HD].astype(BF16)
            v_ref[0, h] = _with_ones_column(pv[:, h * HD:(h + 1) * HD]).astype(BF16)
        pkv_ref[0] = p

    def const(shape):
        return pl.BlockSpec(shape, lambda b: (0,) * len(shape))

    def tile(w):
        return pl.BlockSpec((1, cl, w), lambda b: (b, 0, 0))

    k_tile = pl.BlockSpec((1, KVW // HD, cl, HD), lambda b: (b, 0, 0, 0))
    v_tile = pl.BlockSpec((1, KVW // HD, cl, 2 * HD), lambda b: (b, 0, 0, 0))
    return pl.pallas_call(
        body, name="ctx_fwd", grid=(bl,),
        in_specs=[tile(D), const((1, 3, D)), const((1, D)), const((D, 2 * KVW)), const((1, KVW)), const((KVW, KVW))],
        out_specs=[k_tile, v_tile, tile(2 * KVW)],
        out_shape=[jax.ShapeDtypeStruct((bl, KVW // HD, n_keys, HD), BF16),
                   jax.ShapeDtypeStruct((bl, KVW // HD, n_keys, 2 * HD), BF16),
                   jax.ShapeDtypeStruct((bl, cl, 2 * KVW), F32)],
        compiler_params=_params(("arbitrary",)),
    )(ctx, modc, norm_w, w_kv_b, knw_t, ones_bd)


def _attn_fwd(q, k, v1):
    bl, _, s, _ = q.shape
    n_keys = k.shape[2]

    def body(q_ref, k_ref, v_ref, o_ref, lse_ref):
        kv = k_ref[0, 0]
        vv = v_ref[0, 0]
        lane = lax.broadcasted_iota(jnp.int32, (TQ, 2 * HD), 1)
        for part in range(FWD_PARTS):
            rows = pl.ds(part * TQ, TQ)
            lse = jnp.zeros((TQ, 2 * HD), F32)
            heads = []
            sc_all = _dot_nt(q_ref[0, :, rows, :].reshape(4 * TQ, HD), kv)
            for h in range(4):
                sc = sc_all[h * TQ:(h + 1) * TQ, :]
                m = jnp.max(sc, axis=-1, keepdims=True)
                e = jnp.exp(sc - m).astype(BF16)
                ov = _dot(e, vv)
                denom = ov[:, HD:HD + 1]
                heads.append(ov[:, 0:HD] * (1.0 / denom))
                lse = jnp.where(lane == h, m + jnp.log(denom), lse)
            o_ref[0, rows, :] = jnp.concatenate(heads, axis=-1)
            lse_ref[0, 0, rows, :] = lse

    tq = FWD_PARTS * TQ
    ks = pl.BlockSpec((1, 1, n_keys, HD), lambda b, g, i: (b, g, 0, 0))
    qs = pl.BlockSpec((1, 4, tq, HD), lambda b, g, i: (b, g, i, 0))
    vs = pl.BlockSpec((1, 1, n_keys, 2 * HD), lambda b, g, i: (b, g, 0, 0))
    return pl.pallas_call(
        body, name="attn_fwd", grid=(bl, 2, s // tq), in_specs=[qs, ks, vs],
        out_specs=[pl.BlockSpec((1, tq, 4 * HD), lambda b, g, i: (b, i, g)),
                   pl.BlockSpec((1, 1, tq, 2 * HD), lambda b, g, i: (b, g, i, 0))],
        out_shape=[jax.ShapeDtypeStruct((bl, s, DA), F32), jax.ShapeDtypeStruct((bl, 2, s, 2 * HD), F32)],
        compiler_params=_params(("arbitrary", "arbitrary", "arbitrary")),
    )(q, k, v1)


def _attn_bwd(q, k, v1, do, o, lse, exchange):
    bl, _, s, _ = q.shape
    n_keys = k.shape[2]
    tq = BWD_PARTS * TQ
    nq = s // tq
    n_ex = len(exchange)

    def body(*refs):
        q_ref, k_ref, v_ref, do_ref, o_ref, lse_ref = refs[:6]
        part_refs = refs[6:6 + n_ex]
        dq_ref, dk_ref, dv_ref = refs[6 + n_ex:9 + n_ex]
        got_refs = refs[9 + n_ex:9 + 2 * n_ex]
        p_sc, ds_sc, dkt, dvt, send_sems, recv_sems, local_sems = refs[9 + 2 * n_ex:]
        i = pl.program_id(2)
        first = (pl.program_id(0) == 0) & (pl.program_id(1) == 0) & (i == 0)
        last = (pl.program_id(0) == bl - 1) & (pl.program_id(1) == 1) & (i == nq - 1)
        local, remote = _exchange_copies(part_refs, got_refs, send_sems, recv_sems, local_sems, scatter=True)

        @pl.when(first)
        def _():
            for cp in local + remote:
                cp.start()

        @pl.when(i == 0)
        def _():
            dkt[...] = jnp.zeros_like(dkt)
            dvt[...] = jnp.zeros_like(dvt)

        kv = k_ref[0, 0]
        vv = v_ref[0, 0][:, 0:HD]
        q_cats, do_cats = [], []
        for part in range(BWD_PARTS):
            tq_rows = pl.ds(part * TQ, TQ)
            lse = lse_ref[0, 0, tq_rows, :]
            ov = o_ref[0, tq_rows, :]
            dqs = []
            q_cat = q_ref[0, :, tq_rows, :].reshape(4 * TQ, HD)
            do_cat = do_ref[0, :, tq_rows, :].reshape(4 * TQ, HD)
            sc_all = _dot_nt(q_cat, kv)
            for h in range(4):
                doh = do_cat[h * TQ:(h + 1) * TQ, :]
                delta = jnp.sum(ov[:, h * HD:(h + 1) * HD] * doh.astype(F32), axis=-1, keepdims=True)
                rows = pl.ds((part * 4 + h) * TQ, TQ)
                p = jnp.exp(sc_all[h * TQ:(h + 1) * TQ, :] - lse[:, h:h + 1])
                ds = (p * (_dot_nt(doh, vv) - delta)).astype(BF16)
                p_sc[rows, :] = p.astype(BF16)
                ds_sc[rows, :] = ds
                dqs.append(_dot(ds, kv) * 0.125)
            dq_ref[0, tq_rows, :] = jnp.concatenate(dqs, axis=-1)
            q_cats.append(q_cat)
            do_cats.append(do_cat)
        dvt[...] += _dot_tn(jnp.concatenate(do_cats, axis=0), p_sc[...])
        dkt[...] += _dot_tn(jnp.concatenate(q_cats, axis=0), ds_sc[...])

        @pl.when(i == nq - 1)
        def _():
            dk_ref[0, 0] = dkt[...].T
            dv_ref[0, 0] = dvt[...].T

        @pl.when(last)
        def _():
            for cp in remote:
                cp.wait_recv()
            for cp in remote:
                cp.wait_send()
            for cp in local:
                cp.wait()

    qs = pl.BlockSpec((1, 4, tq, HD), lambda b, g, i: (b, g, i, 0))
    ks = pl.BlockSpec((1, 1, n_keys, HD), lambda b, g, i: (b, g, 0, 0))
    vs = pl.BlockSpec((1, 1, n_keys, 2 * HD), lambda b, g, i: (b, g, 0, 0))
    os_ = pl.BlockSpec((1, tq, 4 * HD), lambda b, g, i: (b, i, g))
    kshape = jax.ShapeDtypeStruct(k.shape, F32)
    anyspace = pl.BlockSpec(memory_space=pl.ANY)
    res = pl.pallas_call(
        body, name="attn_bwd", grid=(bl, 2, nq),
        in_specs=[qs, ks, vs, qs, os_, pl.BlockSpec((1, 1, tq, 2 * HD), lambda b, g, i: (b, g, i, 0))]
        + [anyspace] * n_ex,
        out_specs=[os_, ks, ks] + [anyspace] * n_ex,
        out_shape=[jax.ShapeDtypeStruct((bl, s, DA), F32), kshape, kshape]
        + [jax.ShapeDtypeStruct(a.shape, a.dtype) for a in exchange],
        scratch_shapes=[pltpu.VMEM((4 * tq, n_keys), BF16), pltpu.VMEM((4 * tq, n_keys), BF16),
                        pltpu.VMEM((HD, n_keys), F32), pltpu.VMEM((HD, n_keys), F32)] + _exchange_scratch(n_ex),
        compiler_params=_params(("arbitrary", "arbitrary", "arbitrary")),
    )(q, k, v1, do, o, lse, *exchange)
    return res[:3], res[3:]


def _halo_specs(width, s):
    per = TC // HALO
    last = s // HALO - 1
    main = pl.BlockSpec((1, TC, width), lambda b, i: (b, i, 0))
    prev = pl.BlockSpec((1, HALO, width), lambda b, i: (b, jnp.maximum(i * per - 1, 0), 0))
    nxt = pl.BlockSpec((1, HALO, width), lambda b, i: (b, jnp.minimum((i + 1) * per, last), 0))
    return main, prev, nxt


def _glu(g):
    return g[:, 0:DC] * _sigmoid(g[:, DC:2 * DC])


def _fill_padded(pad_ref, main, prev, nxt, first, last):
    pad_ref[0:HALO, :] = jnp.where(first, 0.0, prev)
    pad_ref[HALO:HALO + TC, :] = main
    pad_ref[HALO + TC:2 * HALO + TC, :] = jnp.where(last, 0.0, nxt)


PLANE_ROWS = TC + 2 * HALO - 8


def _shift_planes(pad_ref, planes_ref):
    for r in range(1, 8):
        planes_ref[r - 1] = pad_ref[pl.ds(r, PLANE_ROWS), :]


def _tap_rows(pad_ref, planes_ref, offset, start, n):
    a, r = divmod(offset, 8)
    if r == 0:
        return pad_ref[pl.ds(start + 8 * a, n), :]
    return planes_ref[r - 1, pl.ds(start + 8 * a, n), :]


def _conv_fwd(glu, conv_w, conv_b, ln_w, ln_b, w_pw_b, b_pw):
    bl, s, _ = glu.shape
    nt = s // TC

    def body(g_ref, gp_ref, gn_ref, cw_ref, cb_ref, lw_ref, lb_ref, wpw_ref, bpw_ref, y_ref, cp_ref, pad_ref, planes_ref):
        i = pl.program_id(1)
        _fill_padded(pad_ref, _glu(g_ref[0]), _glu(gp_ref[0]), _glu(gn_ref[0]), i == 0, i == nt - 1)
        _shift_planes(pad_ref, planes_ref)
        for ck in range(TC // CH):
            acc = jnp.zeros((CH, DC), F32) + cb_ref[...]
            for t in range(KW):
                acc = acc + _tap_rows(pad_ref, planes_ref, 1 + t, ck * CH, CH) * cw_ref[t:t + 1, :]
            y_ref[0, pl.ds(ck * CH, CH), :] = acc
        y = y_ref[0]
        mu = jnp.mean(y, axis=-1, keepdims=True)
        yc = y - mu
        var = jnp.mean(yc * yc, axis=-1, keepdims=True)
        z = yc * lax.rsqrt(var + EPS) * lw_ref[...] + lb_ref[...]
        act = z * _sigmoid(z)
        cp_ref[0] = _dot(act.astype(BF16), wpw_ref[...]) + bpw_ref[...]

    def const(shape):
        return pl.BlockSpec(shape, lambda b, i: (0,) * len(shape))

    main, prev, nxt = _halo_specs(2 * DC, s)
    tile = pl.BlockSpec((1, TC, DC), lambda b, i: (b, i, 0))
    return pl.pallas_call(
        body, name="conv_fwd", grid=(bl, nt),
        in_specs=[main, prev, nxt, const((32, DC)), const((1, DC)), const((1, DC)), const((1, DC)),
                  const((DC, DC)), const((1, DC))],
        out_specs=[tile, tile],
        out_shape=[jax.ShapeDtypeStruct((bl, s, DC), F32)] * 2,
        scratch_shapes=[pltpu.VMEM((TC + 2 * HALO, DC), F32), pltpu.VMEM((7, PLANE_ROWS, DC), F32)],
        compiler_params=_params(("arbitrary", "arbitrary")),
    )(glu, glu, glu, conv_w, conv_b, ln_w, ln_b, w_pw_b, b_pw)


def _conv_bwd_pointwise(y, dcp, ln_w, ln_b, w_pw_b):
    bl, s, _ = y.shape
    nt = s // TM

    def body(y_ref, dcp_ref, lw_ref, lb_ref, wpw_ref, dy_ref, gw_ref, rows_ref):
        @pl.when((pl.program_id(0) == 0) & (pl.program_id(1) == 0))
        def _():
            gw_ref[...] = jnp.zeros_like(gw_ref)
            rows_ref[...] = jnp.zeros_like(rows_ref)

        y = y_ref[0]
        dcp = dcp_ref[0]
        mu = jnp.mean(y, axis=-1, keepdims=True)
        yc = y - mu
        rstd = lax.rsqrt(jnp.mean(yc * yc, axis=-1, keepdims=True) + EPS)
        yn = yc * rstd
        lw = lw_ref[...]
        z = yn * lw + lb_ref[...]
        sg = _sigmoid(z)
        act = z * sg
        dcp_b = dcp.astype(BF16)
        gw_ref[...] += _dot_tn(act.astype(BF16), dcp_b)
        dact = _dot_nt(dcp_b, wpw_ref[...])
        dz = dact * (sg * (1.0 + z * (1.0 - sg)))
        dyn = dz * lw
        dy = rstd * (dyn - jnp.mean(dyn, axis=-1, keepdims=True) - yn * jnp.mean(dyn * yn, axis=-1, keepdims=True))
        dy_ref[0] = dy
        rows_ref[0:1, :] += jnp.sum(dcp, axis=0, keepdims=True)
        rows_ref[1:2, :] += jnp.sum(dz * yn, axis=0, keepdims=True)
        rows_ref[2:3, :] += jnp.sum(dz, axis=0, keepdims=True)
        rows_ref[3:4, :] += jnp.sum(dy, axis=0, keepdims=True)

    def const(shape):
        return pl.BlockSpec(shape, lambda b, i: (0,) * len(shape))

    tile = pl.BlockSpec((1, TM, DC), lambda b, i: (b, i, 0))
    return pl.pallas_call(
        body, name="conv_bwd_pointwise", grid=(bl, nt),
        in_specs=[tile, tile, const((1, DC)), const((1, DC)), const((DC, DC))],
        out_specs=[tile, const((DC, DC)), const((8, DC))],
        out_shape=[jax.ShapeDtypeStruct((bl, s, DC), F32), jax.ShapeDtypeStruct((DC, DC), F32),
                   jax.ShapeDtypeStruct((8, DC), F32)],
        compiler_params=_params(("arbitrary", "arbitrary")),
    )(y, dcp, ln_w, ln_b, w_pw_b)


def _conv_bwd_depthwise(glu, dy, conv_w):
    bl, s, _ = glu.shape
    nt = s // TC

    def body(g_ref, gp_ref, gn_ref, d_ref, dp_ref, dn_ref, cw_ref, dglu_ref, dcw_ref,
             padu_ref, padd_ref, planes_u, planes_d):
        i = pl.program_id(1)

        @pl.when((pl.program_id(0) == 0) & (i == 0))
        def _():
            dcw_ref[...] = jnp.zeros_like(dcw_ref)

        first, last = i == 0, i == nt - 1
        _fill_padded(padu_ref, _glu(g_ref[0]), _glu(gp_ref[0]), _glu(gn_ref[0]), first, last)
        _fill_padded(padd_ref, d_ref[0], dp_ref[0], dn_ref[0], first, last)
        _shift_planes(padu_ref, planes_u)
        _shift_planes(padd_ref, planes_d)
        for ck in range(TC // CH):
            acc = jnp.zeros((CH, DC), F32)
            for t in range(KW):
                acc = acc + _tap_rows(padd_ref, planes_d, 2 * HALO - 1 - t, ck * CH, CH) * cw_ref[t:t + 1, :]
            g = g_ref[0, pl.ds(ck * CH, CH), :]
            a = g[:, 0:DC]
            sg = _sigmoid(g[:, DC:2 * DC])
            dglu_ref[0, pl.ds(ck * CH, CH), 0:DC] = acc * sg
            dglu_ref[0, pl.ds(ck * CH, CH), DC:2 * DC] = acc * a * sg * (1.0 - sg)
        group = 4
        for t0 in range(0, KW, group):
            taps = range(t0, min(t0 + group, KW))
            acc8 = [jnp.zeros((8, DC), F32) for _ in taps]
            for ck in range(TC // CH):
                dchunk = d_ref[0, pl.ds(ck * CH, CH), :]
                for n, t in enumerate(taps):
                    prod = _tap_rows(padu_ref, planes_u, 1 + t, ck * CH, CH) * dchunk
                    acc8[n] = acc8[n] + jnp.sum(prod.reshape(CH // 8, 8, DC), axis=0)
            for n, t in enumerate(taps):
                dcw_ref[t:t + 1, :] += jnp.sum(acc8[n], axis=0, keepdims=True)

    gmain, gprev, gnext = _halo_specs(2 * DC, s)
    dmain, dprev, dnext = _halo_specs(DC, s)
    cw = pl.BlockSpec((32, DC), lambda b, i: (0, 0))
    return pl.pallas_call(
        body, name="conv_bwd_depthwise", grid=(bl, nt),
        in_specs=[gmain, gprev, gnext, dmain, dprev, dnext, cw],
        out_specs=[gmain, cw],
        out_shape=[jax.ShapeDtypeStruct((bl, s, 2 * DC), F32), jax.ShapeDtypeStruct((32, DC), F32)],
        scratch_shapes=[pltpu.VMEM((TC + 2 * HALO, DC), F32)] * 2 + [pltpu.VMEM((7, PLANE_ROWS, DC), F32)] * 2,
        compiler_params=_params(("arbitrary", "arbitrary")),
    )(glu, glu, glu, dy, dy, dy, conv_w)


def _out_fwd_bwd(attn, za, cp, zc, x, target, modrows, w_out_b):
    bl, s, _ = x.shape
    nt = s // TM

    def body(o_ref, za_ref, cp_ref, zc_ref, x_ref, t_ref, mod_ref, w_ref,
             do_ref, dza_ref, dcp_ref, dzc_ref, dh_ref, dgate_ref, gw_ref, loss_ref):
        b, i = pl.program_id(0), pl.program_id(1)

        @pl.when((b == 0) & (i == 0))
        def _():
            gw_ref[...] = jnp.zeros_like(gw_ref)
            loss_ref[...] = jnp.zeros_like(loss_ref)

        @pl.when(i == 0)
        def _():
            dgate_ref[...] = jnp.zeros_like(dgate_ref)

        o, za_v, cp_v, zc_v = o_ref[0], za_ref[0], cp_ref[0], zc_ref[0]
        gate = mod_ref[0, 2:3, :]
        sa = _sigmoid(za_v)
        sc = _sigmoid(zc_v)
        silu_a = za_v * sa
        silu_c = zc_v * sc
        mix = jnp.concatenate([(o * silu_a).astype(BF16), (cp_v * silu_c).astype(BF16)], axis=-1)
        w = w_ref[...]
        out = _dot(mix, w)
        err = x_ref[0] + gate * out - t_ref[0]
        loss_ref[...] += jnp.sum(err * err, axis=0, keepdims=True)
        dh = err * (1.0 / D)
        dh_ref[0] = dh
        dgate_ref[0] += jnp.sum(dh * out, axis=0, keepdims=True)
        dout = (dh * gate).astype(BF16)
        gw_ref[...] += _dot_tn(mix, dout)
        dmix = _dot_nt(dout, w)
        dga = dmix[:, 0:DA]
        dgc = dmix[:, DA:DA + DC]
        dov = dga * silu_a
        for h in range(DA // HD):
            do_ref[0, h] = dov[:, h * HD:(h + 1) * HD].astype(BF16)
        dza_ref[0] = dga * o * (sa * (1.0 + za_v * (1.0 - sa)))
        dcp_ref[0] = dgc * silu_c
        dzc_ref[0] = dgc * cp_v * (sc * (1.0 + zc_v * (1.0 - sc)))

    def const(shape):
        return pl.BlockSpec(shape, lambda b, i: (0,) * len(shape))

    def tile(w):
        return pl.BlockSpec((1, TM, w), lambda b, i: (b, i, 0))

    return pl.pallas_call(
        body, name="out_fwd_bwd", grid=(bl, nt),
        in_specs=[tile(DA), tile(DA), tile(DC), tile(DC), tile(D), tile(D),
                  pl.BlockSpec((1, 3, D), lambda b, i: (b, 0, 0)), const((D, D))],
        out_specs=[pl.BlockSpec((1, DA // HD, TM, HD), lambda b, i: (b, 0, i, 0)), tile(DA), tile(DC), tile(DC), tile(D),
                   pl.BlockSpec((1, 1, D), lambda b, i: (b, 0, 0)), const((D, D)), const((1, D))],
        out_shape=[jax.ShapeDtypeStruct((bl, DA // HD, s, HD), BF16), jax.ShapeDtypeStruct((bl, s, DA), F32),
                   jax.ShapeDtypeStruct((bl, s, DC), F32), jax.ShapeDtypeStruct((bl, s, DC), F32),
                   jax.ShapeDtypeStruct((bl, s, D), F32), jax.ShapeDtypeStruct((bl, 1, D), F32),
                   jax.ShapeDtypeStruct((D, D), F32), jax.ShapeDtypeStruct((1, D), F32)],
        compiler_params=_params(("arbitrary", "arbitrary")),
    )(attn, za, cp, zc, x, target, modrows, w_out_b)


def _rms_heads_bwd(dy, x, w_t, ones_bd):
    r = lax.rsqrt(_segsum(x * x, ones_bd) * (1.0 / HD) + EPS)
    xh = x * r
    g = dy * w_t
    dx = r * (g - xh * (_segsum(g * xh, ones_bd) * (1.0 / HD)))
    return dx, dy * xh


def _ctx_bwd(ctx, modc, norm_w, w_kv_b, pkv_c, dk_c, dv_c, knw_t, ones_bd):
    bl, cl, _ = ctx.shape

    def body(x_ref, mod_ref, nw_ref, w_ref, p_ref, dk_ref, dv_ref, knw_ref, bd_ref, gw_ref, rows_ref, dknw_ref):
        @pl.when(pl.program_id(0) == 0)
        def _():
            gw_ref[...] = jnp.zeros_like(gw_ref)
            rows_ref[...] = jnp.zeros_like(rows_ref)
            dknw_ref[...] = jnp.zeros_like(dknw_ref)

        xv = x_ref[0]
        shift = mod_ref[0, 0:1, :]
        scale = mod_ref[0, 1:2, :]
        nw = nw_ref[...]
        r = lax.rsqrt(jnp.mean(xv * xv, axis=-1, keepdims=True) + EPS)
        xn = xv * r
        yv = xn * nw
        u = yv * (1.0 + scale) + shift
        dkv = jnp.concatenate([dk_ref[0, 0], dk_ref[0, 1]], axis=-1)
        dpk, dknw = _rms_heads_bwd(dkv, p_ref[0][:, 0:KVW], knw_ref[...], bd_ref[...])
        dp = jnp.concatenate([dpk.astype(BF16), dv_ref[0, 0].astype(BF16), dv_ref[0, 1].astype(BF16)], axis=-1)
        gw_ref[...] += _dot_tn(u.astype(BF16), dp)
        du = _dot_nt(dp, w_ref[...])
        rows_ref[0:1, :] += jnp.sum(du, axis=0, keepdims=True)
        rows_ref[1:2, :] += jnp.sum(du * yv, axis=0, keepdims=True)
        rows_ref[2:3, :] += jnp.sum(du * (1.0 + scale) * xn, axis=0, keepdims=True)
        dknw_ref[...] += jnp.sum(dknw, axis=0, keepdims=True)

    def const(shape):
        return pl.BlockSpec(shape, lambda b: (0,) * len(shape))

    def tile(w):
        return pl.BlockSpec((1, cl, w), lambda b: (b, 0, 0))

    kv_tile = pl.BlockSpec((1, KVW // HD, cl, HD), lambda b: (b, 0, 0, 0))
    return pl.pallas_call(
        body, name="ctx_bwd", grid=(bl,),
        in_specs=[tile(D), const((1, 3, D)), const((1, D)), const((D, 2 * KVW)), tile(2 * KVW), kv_tile, kv_tile,
                  const((1, KVW)), const((KVW, KVW))],
        out_specs=[const((D, 2 * KVW)), const((8, D)), const((1, KVW))],
        out_shape=[jax.ShapeDtypeStruct((D, 2 * KVW), F32), jax.ShapeDtypeStruct((8, D), F32),
                   jax.ShapeDtypeStruct((1, KVW), F32)],
        compiler_params=_params(("arbitrary",)),
    )(ctx, modc, norm_w, w_kv_b, pkv_c, dk_c, dv_c, knw_t, ones_bd)


def _bwd_in(x, modrows, norm_w, w_in_b, cos, sins, qnw_t, knw_t, ones_bd,
            pq, pkv, dq, dk, dv, dza, dglu, dzc, dh, gw_kv):
    bl, s, _ = x.shape
    nt = s // TM

    def body(x_ref, mod_ref, nw_ref, win_hbm, cos_ref, sin_ref, qnw_ref, knw_ref, bd_ref,
             pq_ref, pkv_ref, dq_ref, dk_ref, dv_ref, dza_ref, dglu_ref, dzc_ref, dh_ref, gwkv_ref,
             gx_ref, gw_hbm, dmod_ref, dnw_ref, dqnw_ref, dknw_ref, win_ref, gw_acc, sem):
        b, i = pl.program_id(0), pl.program_id(1)

        @pl.when((b == 0) & (i == 0))
        def _():
            cp = pltpu.make_async_copy(win_hbm, win_ref, sem)
            cp.start()
            gw_acc[...] = jnp.zeros_like(gw_acc)
            dnw_ref[...] = jnp.zeros_like(dnw_ref)
            dqnw_ref[...] = jnp.zeros_like(dqnw_ref)
            dknw_ref[...] = jnp.zeros_like(dknw_ref)
            cp.wait()

        @pl.when(i == 0)
        def _():
            dmod_ref[...] = jnp.zeros_like(dmod_ref)

        ck = cos_ref[...]
        sk = sin_ref[...]
        cs = jnp.concatenate([ck] * (DA // KVW), axis=-1)
        sn = jnp.concatenate([sk] * (DA // KVW), axis=-1)
        bd = bd_ref[...]
        dqn = _rope_bwd(dq_ref[0], cs, sn)
        dpq, dqnw = _rms_heads_bwd(dqn, pq_ref[0], qnw_ref[...], bd)
        dkn = _rope_bwd(jnp.concatenate([dk_ref[0, 0], dk_ref[0, 1]], axis=-1), ck, sk)
        dpk, dknw = _rms_heads_bwd(dkn, pkv_ref[0][:, 0:KVW], knw_ref[...], bd[0:KVW, 0:KVW])
        dqnw_ref[...] += jnp.sum(dqnw, axis=0, keepdims=True)
        dknw_ref[...] += jnp.sum(dknw, axis=0, keepdims=True)
        dp = jnp.concatenate(
            [dpq.astype(BF16), dpk.astype(BF16), dv_ref[0, 0].astype(BF16), dv_ref[0, 1].astype(BF16),
             dza_ref[0].astype(BF16), dglu_ref[0].astype(BF16), dzc_ref[0].astype(BF16)], axis=-1)

        xv = x_ref[0]
        shift = mod_ref[0, 0:1, :]
        scale = mod_ref[0, 1:2, :]
        nw = nw_ref[...]
        r = lax.rsqrt(jnp.mean(xv * xv, axis=-1, keepdims=True) + EPS)
        xn = xv * r
        yv = xn * nw
        u = yv * (1.0 + scale) + shift
        gw_acc[...] += _dot_tn(u.astype(BF16), dp)
        du = _dot_nt(dp, win_ref[...])
        dmod_ref[0, 0:1, :] += jnp.sum(du, axis=0, keepdims=True)
        dmod_ref[0, 1:2, :] += jnp.sum(du * yv, axis=0, keepdims=True)
        dy = du * (1.0 + scale)
        dnw_ref[...] += jnp.sum(dy * xn, axis=0, keepdims=True)
        dxn = dy * nw
        gx_ref[0] = dh_ref[0] + r * (dxn - xn * jnp.mean(dxn * xn, axis=-1, keepdims=True))

        @pl.when((b == bl - 1) & (i == nt - 1))
        def _():
            gw_acc[:, 512:768] += gwkv_ref[...]
            pltpu.sync_copy(gw_acc, gw_hbm)

    def tile(w):
        return pl.BlockSpec((1, TM, w), lambda b, i: (b, i, 0))

    def const(shape):
        return pl.BlockSpec(shape, lambda b, i: (0,) * len(shape))

    anyspace = pl.BlockSpec(memory_space=pl.ANY)
    rope = pl.BlockSpec((TM, KVW), lambda b, i: (i, 0))
    ctx_tiles = (dk.shape[2] - s) // TM
    kv_tile = pl.BlockSpec((1, KVW // HD, TM, HD), lambda b, i: (b, 0, ctx_tiles + i, 0))
    return pl.pallas_call(
        body, name="bwd_in", grid=(bl, nt),
        in_specs=[tile(D), pl.BlockSpec((1, 3, D), lambda b, i: (b, 0, 0)), const((1, D)), anyspace, rope, rope,
                  const((1, DA)), const((1, KVW)), const((DA, DA)),
                  tile(DA), tile(2 * KVW), tile(DA), kv_tile, kv_tile, tile(DA), tile(2 * DC), tile(DC), tile(D),
                  const((D, 2 * KVW))],
        out_specs=[tile(D), anyspace, pl.BlockSpec((1, 2, D), lambda b, i: (b, 0, 0)), const((1, D)),
                   const((1, DA)), const((1, KVW))],
        out_shape=[jax.ShapeDtypeStruct((bl, s, D), F32), jax.ShapeDtypeStruct((D, D_IN), F32),
                   jax.ShapeDtypeStruct((bl, 2, D), F32), jax.ShapeDtypeStruct((1, D), F32),
                   jax.ShapeDtypeStruct((1, DA), F32), jax.ShapeDtypeStruct((1, KVW), F32)],
        scratch_shapes=[pltpu.VMEM((D, D_IN), BF16), pltpu.VMEM((D, D_IN), F32), pltpu.SemaphoreType.DMA],
        compiler_params=_params(("arbitrary", "arbitrary")),
    )(x, modrows, norm_w, w_in_b, cos, sins, qnw_t, knw_t, ones_bd,
      pq, pkv, dq, dk, dv, dza, dglu, dzc, dh, gw_kv)


_LOSS, _DMODC, _NW, _QN, _KN, _CB, _LW, _LB, _BPW, SMALL_W = 0, 1024, 4096, 5120, 5248, 5376, 5888, 6400, 6912, 7424


ROW_W = 1792


def _put_flat(ref, off, value):
    n, done = value.shape[1], 0
    while done < n:
        r, c = divmod(off + done, ROW_W)
        take = min(n - done, ROW_W - c)
        ref[r:r + 1, c:c + take] = value[:, done:done + take]
        done += take


def _get_flat(arr, off, n):
    parts, done = [], 0
    while done < n:
        r, c = divmod(off + done, ROW_W)
        take = min(n - done, ROW_W - c)
        parts.append(arr[r:r + 1, c:c + take])
        done += take
    return parts[0] if len(parts) == 1 else jnp.concatenate(parts, axis=-1)


def _pack_small(loss_row, ctx_rows, dnw, dqnw, dknw, dknw_c, conv_rows, dmod_ss, dgate):
    bl = dmod_ss.shape[0]
    assert SMALL_W + bl * 3 * D <= 8 * ROW_W

    def body(loss_ref, ctx_ref, dnw_ref, dqnw_ref, dknw_ref, dknwc_ref, conv_ref, dss_ref, dgate_ref, o_ref):
        o_ref[...] = jnp.zeros_like(o_ref)
        _put_flat(o_ref, _LOSS, loss_ref[...])
        _put_flat(o_ref, _DMODC, ctx_ref[0:1, :])
        _put_flat(o_ref, _DMODC + D, ctx_ref[1:2, :])
        _put_flat(o_ref, _NW, dnw_ref[...] + ctx_ref[2:3, :])
        dq = dqnw_ref[...]
        qn = dq[:, 0:HD]
        for h in range(1, DA // HD):
            qn = qn + dq[:, h * HD:(h + 1) * HD]
        _put_flat(o_ref, _QN, qn)
        dk = dknw_ref[...] + dknwc_ref[...]
        _put_flat(o_ref, _KN, dk[:, 0:HD] + dk[:, HD:2 * HD])
        _put_flat(o_ref, _BPW, conv_ref[0:1, :])
        _put_flat(o_ref, _LW, conv_ref[1:2, :])
        _put_flat(o_ref, _LB, conv_ref[2:3, :])
        _put_flat(o_ref, _CB, conv_ref[3:4, :])
        for b in range(bl):
            _put_flat(o_ref, SMALL_W + b * 3 * D, dss_ref[b, 0:1, :])
            _put_flat(o_ref, SMALL_W + b * 3 * D + D, dss_ref[b, 1:2, :])
            _put_flat(o_ref, SMALL_W + b * 3 * D + 2 * D, dgate_ref[b])

    return pl.pallas_call(
        body, name="pack_small", out_shape=jax.ShapeDtypeStruct((8, ROW_W), F32),
    )(loss_row, ctx_rows, dnw, dqnw, dknw, dknw_c, conv_rows, dmod_ss, dgate)


def _finish_small(gathered, bl):
    n_ex = N_DEV * bl

    def body(g_ref, sum_ref, dmod_ref, gb_ref, loss_ref):
        tot = g_ref[0]
        for j in range(1, N_DEV):
            tot = tot + g_ref[j]
        acc = _get_flat(tot, 0, SMALL_W)
        sum_ref[...] = acc
        dmod_ref[...] = jnp.zeros_like(dmod_ref)
        for j in range(N_DEV):
            arr = g_ref[j]
            for b in range(bl):
                dmod_ref[j * bl + b:j * bl + b + 1, :] = _get_flat(arr, SMALL_W + b * 3 * D, 3 * D)
        dmod_ref[n_ex:n_ex + 1, :] = acc[:, _DMODC:_DMODC + 3 * D]
        gb_ref[...] = jnp.sum(dmod_ref[...], axis=0, keepdims=True)
        loss_ref[...] = (0.5 / D) * jnp.sum(acc[:, _LOSS:_LOSS + D], axis=-1, keepdims=True)

    return pl.pallas_call(
        body, name="finish_small",
        out_shape=[jax.ShapeDtypeStruct((1, SMALL_W), F32), jax.ShapeDtypeStruct((n_ex + 8, 3 * D), F32),
                   jax.ShapeDtypeStruct((1, 3 * D), F32), jax.ShapeDtypeStruct((1, 1), F32)],
    )(gathered)


_SMALL = (("b_mod", None), ("norm_w", _NW), ("q_norm_w", _QN), ("k_norm_w", _KN), ("conv_b", _CB),
          ("conv_ln_w", _LW), ("conv_ln_b", _LB), ("b_pw", _BPW), ("c_ctx", None))


def _adamw_small(summed, g_bmod, gc_parts, weights, moms, vars_):
    n = len(_SMALL)

    def body(*refs):
        sum_ref, gb_ref, gc_ref = refs[0:3]
        w_refs, m_refs, v_refs = refs[3:3 + n], refs[3 + n:3 + 2 * n], refs[3 + 2 * n:3 + 3 * n]
        outs = refs[3 + 3 * n:]
        for k, (name, off) in enumerate(_SMALL):
            w = w_refs[k][...]
            if name == "b_mod":
                g = gb_ref[...]
            elif name == "c_ctx":
                acc = gc_ref[0, 0:1, :]
                for j in range(1, N_DEV):
                    acc = acc + gc_ref[j, 0:1, :]
                sg = _sigmoid(w)
                g = acc * (sg * (1.0 + w * (1.0 - sg)))
            else:
                g = sum_ref[:, off:off + w.shape[1]]
            delta, m2, v2 = _adamw(w, g, m_refs[k][...], v_refs[k][...])
            outs[k][...] = g
            outs[n + k][...] = delta
            outs[2 * n + k][...] = m2
            outs[3 * n + k][...] = v2

    shapes = [jax.ShapeDtypeStruct(w.shape, F32) for w in weights]
    outs = pl.pallas_call(body, name="adamw_small", out_shape=shapes * 4)(
        summed, g_bmod, gc_parts, *weights, *moms, *vars_)
    return [outs[k * n:(k + 1) * n] for k in range(4)]


def _rope_tables(s):
    t = jnp.arange(s, dtype=jnp.int32)
    row = (t // GRID_W).astype(F32)
    col = (t % GRID_W).astype(F32)
    freqs = ROPE_THETA ** (-jnp.arange(0, HD // 2, 2, dtype=F32) / (HD // 2))
    ang_r = row[:, None] * freqs[None, :]
    ang_c = col[:, None] * freqs[None, :]
    cr, sr, cc, sc = jnp.cos(ang_r), jnp.sin(ang_r), jnp.cos(ang_c), jnp.sin(ang_c)
    cos = jnp.concatenate([cr, cr, cc, cc], axis=-1)
    sins = jnp.concatenate([-sr, sr, -sc, sc], axis=-1)
    return jnp.tile(cos, (1, KVW // HD)), jnp.tile(sins, (1, KVW // HD))


def kernel(x, c, ctx, c_ctx, w_mod, b_mod, norm_w, w_in, q_norm_w, k_norm_w, conv_w, conv_b, conv_ln_w, conv_ln_b, w_pw, b_pw, w_out, loss_target, m_c_ctx, m_w_mod, m_b_mod, m_norm_w, m_w_in, m_q_norm_w, m_k_norm_w, m_conv_w, m_conv_b, m_conv_ln_w, m_conv_ln_b, m_w_pw, m_b_pw, m_w_out, v_c_ctx, v_w_mod, v_b_mod, v_norm_w, v_w_in, v_q_norm_w, v_k_norm_w, v_conv_w, v_conv_b, v_conv_ln_w, v_conv_ln_b, v_w_pw, v_b_pw, v_w_out):
    bl, s, _ = x.shape
    cl = ctx.shape[1]
    me = _lin(*_coords())
    n_mod = w_mod.shape[2]

    conv_w_pad = jnp.pad(conv_w[0], ((0, 32 - KW), (0, 0)))
    c_pad = jnp.pad(c, ((0, 8 - bl), (0, 0)))
    g_win, g_c = _all_gather_many([w_in[0].astype(BF16), c_pad], "gather_weights")
    w_in_b = g_win.transpose(1, 0, 2).reshape(D, D_IN)
    c_all = g_c[:, :bl, :].reshape(N_DEV * bl, D)
    n_ex = N_DEV * bl
    c_rows = jnp.concatenate([c_all, c_ctx[None, :], jnp.zeros((7, D), F32)], axis=0)

    b_mod_loc = lax.dynamic_slice_in_dim(b_mod, me * n_mod, n_mod, axis=1)
    mod_loc = _mod_fwd(c_rows, w_mod[0], b_mod_loc)
    g_mod = _all_gather_direct(mod_loc, "gather_mod")
    mod_all = g_mod.transpose(1, 0, 2).reshape(n_ex + 8, 3 * D)
    modrows = lax.dynamic_slice_in_dim(mod_all, me * bl, bl, axis=0).reshape(bl, 3, D)
    modc = mod_all[n_ex].reshape(1, 3, D)

    cos, sins = _rope_tables(s)
    qnw_t = jnp.tile(q_norm_w, (1, DA // HD))
    knw_t = jnp.tile(k_norm_w, (1, KVW // HD))
    lane = jnp.arange(DA, dtype=jnp.int32) // HD
    ones_bd = (lane[:, None] == lane[None, :]).astype(BF16)
    ones_kv = ones_bd[0:KVW, 0:KVW]
    w_kv_b = w_in_b[:, 512:768]

    k_ctx, v_ctx, pkv_c = _ctx_fwd(ctx, modc, norm_w, w_kv_b, knw_t, ones_kv, cl + s)
    (q_h, k_h, v_h, pq, pkv, za, glu, zc), (g_wout, g_wpw, g_cw) = _fwd_in(
        x, modrows, norm_w, w_in_b, cos, sins, qnw_t, knw_t, ones_bd, k_ctx, v_ctx,
        [w_out[0].astype(BF16), w_pw[0].astype(BF16), conv_w_pad])
    w_out_b = g_wout.reshape(D, D)
    w_pw_b = g_wpw.reshape(DC, DC)
    conv_w_full = g_cw.transpose(1, 0, 2).reshape(32, DC)
    attn, lse = _attn_fwd(q_h, k_h, v_h)
    y_conv, cp = _conv_fwd(glu, conv_w_full, conv_b, conv_ln_w, conv_ln_b, w_pw_b, b_pw)

    do_h, dza, dcp, dzc, dh, dgate, gw_out, loss_row = _out_fwd_bwd(attn, za, cp, zc, x, loss_target, modrows, w_out_b)
    dy_conv, gw_pw, conv_rows = _conv_bwd_pointwise(y_conv, dcp, conv_ln_w, conv_ln_b, w_pw_b)
    dglu, g_cw_full = _conv_bwd_depthwise(glu, dy_conv, conv_w_full)
    parts_out = gw_out.astype(BF16).reshape(N_DEV, D // N_DEV, D)
    parts_pw = gw_pw.astype(BF16).reshape(N_DEV, DC // N_DEV, DC)
    parts_cw = g_cw_full.astype(BF16).reshape(32, N_DEV, DC // N_DEV).transpose(1, 0, 2)
    (dq, dk_h, dv_h), (got_out, got_pw, got_cw) = _attn_bwd(
        q_h, k_h, v_h, do_h, attn, lse, [parts_out, parts_pw, parts_cw])
    gw_kv, ctx_rows, dknw_c = _ctx_bwd(ctx, modc, norm_w, w_kv_b, pkv_c, dk_h, dv_h, knw_t, ones_kv)
    grad_x, gw_in, dmod_ss, dnw, dqnw, dknw = _bwd_in(
        x, modrows, norm_w, w_in_b, cos, sins, qnw_t, knw_t, ones_bd,
        pq, pkv, dq, dk_h, dv_h, dza, dglu, dzc, dh, gw_kv)

    parts_in = gw_in.astype(BF16).reshape(D, N_DEV, D_IN // N_DEV).transpose(1, 0, 2)

    def pad_cw(a):
        return jnp.pad(a[0], ((0, 32 - KW), (0, 0)))

    (r_in,) = _reduce_scatter_adamw([(parts_in, w_in[0], m_w_in[0], v_w_in[0])], "reduce_grads")
    r_out, r_pw, r_cw = _sum_devices_adamw(
        [(got_out, w_out[0], m_w_out[0], v_w_out[0]), (got_pw, w_pw[0], m_w_pw[0], v_w_pw[0]),
         (got_cw, pad_cw(conv_w), pad_cw(m_conv_w), pad_cw(v_conv_w))])
    r_cw = tuple(a[:KW] for a in r_cw)

    small = _pack_small(loss_row, ctx_rows, dnw, dqnw, dknw, dknw_c, conv_rows, dmod_ss, dgate)
    g_small = _all_gather_direct(small, "gather_small")
    summed, dmod_full, g_bmod, loss11 = _finish_small(g_small, bl)

    dmod_loc = lax.dynamic_slice_in_dim(dmod_full, me * n_mod, n_mod, axis=1)
    g_wmod, d_wmod, nm_wmod, nv_wmod, gc_part = _mod_bwd(c_rows, dmod_loc, w_mod[0], m_w_mod[0], v_w_mod[0])
    g_gc = _all_gather_direct(gc_part, "gather_c_ctx")

    given = {"c_ctx": (c_ctx, m_c_ctx, v_c_ctx), "b_mod": (b_mod, m_b_mod, v_b_mod), "norm_w": (norm_w, m_norm_w, v_norm_w),
             "q_norm_w": (q_norm_w, m_q_norm_w, v_q_norm_w), "k_norm_w": (k_norm_w, m_k_norm_w, v_k_norm_w),
             "conv_b": (conv_b, m_conv_b, v_conv_b), "conv_ln_w": (conv_ln_w, m_conv_ln_w, v_conv_ln_w),
             "conv_ln_b": (conv_ln_b, m_conv_ln_b, v_conv_ln_b), "b_pw": (b_pw, m_b_pw, v_b_pw)}
    as_rows = [[given[name][which].reshape(1, -1) for name, _ in _SMALL] for which in range(3)]
    small_outs = _adamw_small(summed, g_bmod, g_gc, *as_rows)

    big = {"w_mod": (g_wmod, d_wmod, nm_wmod, nv_wmod), "w_in": r_in, "conv_w": r_cw, "w_pw": r_pw, "w_out": r_out}
    order = ["c_ctx", "w_mod", "b_mod", "norm_w", "w_in", "q_norm_w", "k_norm_w", "conv_w", "conv_b", "conv_ln_w",
             "conv_ln_b", "w_pw", "b_pw", "w_out"]
    small_index = {name: k for k, (name, _) in enumerate(_SMALL)}
    outs = [loss11.reshape(()), grad_x]
    for which in range(4):
        for name in order:
            if name in big:
                outs.append(big[name][which][None])
            else:
                outs.append(small_outs[which][small_index[name]].reshape(given[name][0].shape))
    return tuple(outs)
```

```python
import functools

import jax
import jax.numpy as jnp
from jax import lax
from jax.experimental import pallas as pl
from jax.experimental.pallas import tpu as pltpu

F32, BF16 = jnp.float32, jnp.bfloat16
MESH_ID = pl.DeviceIdType.MESH

N_DEV = 8
D = 1024
D_IN = 2816
DA = 512
DC = 512
HD = 64
KVW = 128
KW = 31
HALO = 16
EPS = 1e-6
ROPE_THETA = 10000.0
GRID_W = 64

ADAM_LR, ADAM_B1, ADAM_B2, ADAM_EPS, ADAM_WD, ADAM_STEP = 0.001, 0.9, 0.999, 1e-08, 0.01, 10

VMEM_LIMIT = 56 * 1024 * 1024

TM = 256
TQ = 128
BWD_PARTS = 1
FWD_PARTS = 4
TC = 512
CH = 64


def _params(sem, vmem=VMEM_LIMIT):
    return pltpu.CompilerParams(dimension_semantics=sem, vmem_limit_bytes=vmem)


def _dot(a, b):
    return jnp.dot(a, b, preferred_element_type=F32)


def _dot_nt(a, b):
    return lax.dot_general(a, b, (((1,), (1,)), ((), ())), preferred_element_type=F32)


def _dot_tn(a, b):
    return lax.dot_general(a, b, (((0,), (0,)), ((), ())), preferred_element_type=F32)


def _sigmoid(z):
    return 1.0 / (1.0 + jnp.exp(-z))


def _segsum(v, ones_bd):
    hi = v.astype(BF16)
    lo = (v - hi.astype(F32)).astype(BF16)
    return _dot(hi, ones_bd) + _dot(lo, ones_bd)


def _swap16(x):
    w = x.shape[-1]
    lane = lax.broadcasted_iota(jnp.int32, x.shape, 1)
    return jnp.where((lane % 32) < 16, pltpu.roll(x, w - 16, 1), pltpu.roll(x, 16, 1))


def _with_ones_column(v):
    one = (lax.broadcasted_iota(jnp.int32, v.shape, 1) == 0).astype(v.dtype)
    return jnp.concatenate([v, one], axis=-1)


def _rope(x, cos, sins):
    return x * cos + _swap16(x) * sins


def _rope_bwd(d, cos, sins):
    return d * cos + _swap16(d * sins)


def _adamw(w, g, m, v):
    m2 = ADAM_B1 * m + (1.0 - ADAM_B1) * g
    v2 = ADAM_B2 * v + (1.0 - ADAM_B2) * (g * g)
    m_hat = m2 / (1.0 - ADAM_B1 ** ADAM_STEP)
    v_hat = v2 / (1.0 - ADAM_B2 ** ADAM_STEP)
    delta = -ADAM_LR * (m_hat / (jnp.sqrt(v_hat) + ADAM_EPS) + ADAM_WD * w)
    return delta, m2, v2


def _coords():
    return lax.axis_index("x"), lax.axis_index("y"), lax.axis_index("c")


def _lin(x, y, c):
    return 4 * x + 2 * y + c


def _all_gather_many(arrs, name):
    n = len(arrs)

    def body(*refs):
        in_refs, out_refs = refs[:n], refs[n:2 * n]
        send_sems, recv_sems, local_sems = refs[2 * n:]
        x, y, c = _coords()
        me, sib = (x, y, c), (x, y, 1 - c)
        chips = [(1 - x, y), (x, 1 - y), (1 - x, 1 - y)]

        def copy(a, k, block, to, src=None):
            slot = out_refs[a].at[_lin(*block)]
            return pltpu.make_async_remote_copy(
                src_ref=slot if src is None else src, dst_ref=slot,
                send_sem=send_sems.at[a * 7 + k], recv_sem=recv_sems.at[a * 7 + k],
                device_id=to, device_id_type=MESH_ID)

        mine = [pltpu.make_async_copy(in_refs[a], out_refs[a].at[_lin(*me)], local_sems.at[a]) for a in range(n)]
        for cp in mine:
            cp.start()
        first = []
        for a in range(n):
            first.append(copy(a, 0, me, sib, src=in_refs[a]))
            first += [copy(a, 1 + j, me, (*chip, c), src=in_refs[a]) for j, chip in enumerate(chips)]
        for cp in first:
            cp.start()
        passed = []
        for a in range(n):
            for j, chip in enumerate(chips):
                copy(a, 1 + j, (*chip, c), me).wait_recv()
                fwd = copy(a, 4 + j, (*chip, c), sib)
                fwd.start()
                passed.append(fwd)
        for a in range(n):
            copy(a, 0, sib, me).wait_recv()
            for j, chip in enumerate(chips):
                copy(a, 4 + j, (*chip, 1 - c), me).wait_recv()
        for cp in first + passed:
            cp.wait_send()
        for cp in mine:
            cp.wait()

    vm = pl.BlockSpec(memory_space=pltpu.VMEM)
    return pl.pallas_call(
        body, name=name,
        out_shape=[jax.ShapeDtypeStruct((N_DEV,) + a.shape, a.dtype) for a in arrs],
        in_specs=[vm] * n, out_specs=[vm] * n,
        scratch_shapes=[pltpu.SemaphoreType.DMA((7 * n,)), pltpu.SemaphoreType.DMA((7 * n,)),
                        pltpu.SemaphoreType.DMA((n,))],
        compiler_params=pltpu.CompilerParams(vmem_limit_bytes=VMEM_LIMIT),
    )(*arrs)


def _exchange_copies(in_refs, out_refs, send_sems, recv_sems, local_sems, scatter):
    x, y, c = _coords()
    me = _lin(x, y, c)
    local, remote = [], []
    for a, (src, dst) in enumerate(zip(in_refs, out_refs)):
        local.append(pltpu.make_async_copy(src.at[me] if scatter else src, dst.at[me], local_sems.at[a]))
        for k in range(1, N_DEV):
            peer = (1 - x if k & 4 else x, 1 - y if k & 2 else y, 1 - c if k & 1 else c)
            remote.append(pltpu.make_async_remote_copy(
                src_ref=src.at[_lin(*peer)] if scatter else src, dst_ref=dst.at[me],
                send_sem=send_sems.at[a * (N_DEV - 1) + k - 1], recv_sem=recv_sems.at[a * (N_DEV - 1) + k - 1],
                device_id=peer, device_id_type=MESH_ID))
    return local, remote


def _exchange_scratch(n):
    return [pltpu.SemaphoreType.DMA((n * (N_DEV - 1),)), pltpu.SemaphoreType.DMA((n * (N_DEV - 1),)),
            pltpu.SemaphoreType.DMA((n,))]


def _all_gather_direct(arr, name):
    def body(in_ref, out_ref, send_sems, recv_sems, local_sem):
        x, y, c = _coords()
        me = _lin(x, y, c)
        mine = pltpu.make_async_copy(in_ref, out_ref.at[me], local_sem)
        mine.start()
        copies = []
        for k in range(1, N_DEV):
            peer = (1 - x if k & 4 else x, 1 - y if k & 2 else y, 1 - c if k & 1 else c)
            cp = pltpu.make_async_remote_copy(
                src_ref=in_ref, dst_ref=out_ref.at[me], send_sem=send_sems.at[k - 1], recv_sem=recv_sems.at[k - 1],
                device_id=peer, device_id_type=MESH_ID)
            cp.start()
            copies.append(cp)
        for cp in copies:
            cp.wait_recv()
        for cp in copies:
            cp.wait_send()
        mine.wait()

    vm = pl.BlockSpec(memory_space=pltpu.VMEM)
    return pl.pallas_call(
        body, name=name, out_shape=jax.ShapeDtypeStruct((N_DEV,) + arr.shape, arr.dtype),
        in_specs=[vm], out_specs=vm,
        scratch_shapes=[pltpu.SemaphoreType.DMA((N_DEV - 1,)), pltpu.SemaphoreType.DMA((N_DEV - 1,)),
                        pltpu.SemaphoreType.DMA],
    )(arr)


def _reduce_scatter_adamw(items, name):
    n = len(items)
    rb = 32

    def body(*refs):
        parts = refs[0:n]
        wmv = refs[n:4 * n]
        outs = refs[4 * n:8 * n]
        bufs = [refs[8 * n + 4 * a:8 * n + 4 * a + 4] for a in range(n)]
        d2d_send, d2d_recv, ici_send, ici_recv, local_sems = refs[12 * n:]
        x, y, c = _coords()
        sib = (x, y, 1 - c)
        peers = [(1 - x, y), (x, 1 - y), (1 - x, 1 - y)]
        home = 2 * x + y

        def rows_loop(rows, fn):
            def step(i, carry):
                fn(pl.ds(pl.multiple_of(i * rb, rb), rb))
                return carry
            lax.fori_loop(0, rows // rb, step, 0)

        local, d2d, ici = [], [], []
        for a in range(n):
            mine, got_sib = bufs[a][0], bufs[a][1]
            for s in range(4):
                cp = pltpu.make_async_copy(parts[a].at[_lin(s // 2, s % 2, c)], mine.at[s], local_sems.at[4 * a + s])
                cp.start()
                local.append(cp)
                rc = pltpu.make_async_remote_copy(
                    src_ref=parts[a].at[_lin(s // 2, s % 2, 1 - c)], dst_ref=got_sib.at[s],
                    send_sem=d2d_send.at[4 * a + s], recv_sem=d2d_recv.at[4 * a + s],
                    device_id=sib, device_id_type=MESH_ID)
                rc.start()
                d2d.append(rc)

        for a in range(n):
            mine, got_sib, stage, got_chip = bufs[a]
            for s in range(4):
                local[4 * a + s].wait()
                d2d[4 * a + s].wait_recv()
            for k, (px, py) in enumerate(peers):
                slot = 2 * px + py

                def pair_sum(rs, k=k, slot=slot, mine=mine, got_sib=got_sib, stage=stage):
                    stage[k, rs, :] = (mine[slot, rs, :].astype(F32) + got_sib[slot, rs, :].astype(F32)).astype(BF16)

                rows_loop(wmv[3 * a].shape[0], pair_sum)
                rc = pltpu.make_async_remote_copy(
                    src_ref=stage.at[k], dst_ref=got_chip.at[k],
                    send_sem=ici_send.at[3 * a + k], recv_sem=ici_recv.at[3 * a + k],
                    device_id=(px, py, c), device_id_type=MESH_ID)
                rc.start()
                ici.append(rc)

        for a in range(n):
            mine, got_sib, stage, got_chip = bufs[a]
            w_ref, m_ref, v_ref = wmv[3 * a:3 * a + 3]
            g_ref, d_ref, nm_ref, nv_ref = outs[4 * a:4 * a + 4]
            for k in range(3):
                ici[3 * a + k].wait_recv()

            def finish(rs, mine=mine, got_sib=got_sib, got_chip=got_chip, w_ref=w_ref, m_ref=m_ref, v_ref=v_ref,
                       g_ref=g_ref, d_ref=d_ref, nm_ref=nm_ref, nv_ref=nv_ref):
                g = mine[home, rs, :].astype(F32) + got_sib[home, rs, :].astype(F32)
                for k in range(3):
                    g = g + got_chip[k, rs, :].astype(F32)
                delta, m2, v2 = _adamw(w_ref[rs, :], g, m_ref[rs, :], v_ref[rs, :])
                g_ref[rs, :] = g
                d_ref[rs, :] = delta
                nm_ref[rs, :] = m2
                nv_ref[rs, :] = v2

            rows_loop(w_ref.shape[0], finish)

        for rc in d2d + ici:
            rc.wait_send()

    vm = pl.BlockSpec(memory_space=pltpu.VMEM)
    anyspace = pl.BlockSpec(memory_space=pl.ANY)
    args, in_specs, out_shape, scratch = [], [], [], []
    for parts, w, m, v in items:
        assert w.shape[0] % rb == 0 and parts.shape == (N_DEV,) + w.shape and parts.dtype == BF16
    args += [it[0] for it in items]
    in_specs += [anyspace] * n
    for _, w, m, v in items:
        args += [w, m, v]
        in_specs += [vm] * 3
        out_shape += [jax.ShapeDtypeStruct(w.shape, F32)] * 4
    for it in items:
        shp = it[1].shape
        scratch += [pltpu.VMEM((4,) + shp, BF16), pltpu.VMEM((4,) + shp, BF16),
                    pltpu.VMEM((3,) + shp, BF16), pltpu.VMEM((3,) + shp, BF16)]
    scratch += [pltpu.SemaphoreType.DMA((4 * n,)), pltpu.SemaphoreType.DMA((4 * n,)),
                pltpu.SemaphoreType.DMA((3 * n,)), pltpu.SemaphoreType.DMA((3 * n,)), pltpu.SemaphoreType.DMA((4 * n,))]
    outs = pl.pallas_call(
        body, name=name, out_shape=out_shape, in_specs=in_specs, out_specs=[vm] * (4 * n),
        scratch_shapes=scratch, compiler_params=pltpu.CompilerParams(vmem_limit_bytes=VMEM_LIMIT),
    )(*args)
    return [tuple(outs[4 * a:4 * a + 4]) for a in range(n)]


def _sum_devices_adamw(items):
    n = len(items)

    def body(*refs):
        for a in range(n):
            got, w_ref, m_ref, v_ref = refs[4 * a:4 * a + 4]
            g_ref, d_ref, nm_ref, nv_ref = refs[4 * n + 4 * a:4 * n + 4 * a + 4]
            g = got[0].astype(F32)
            for j in range(1, N_DEV):
                g = g + got[j].astype(F32)
            delta, m2, v2 = _adamw(w_ref[...], g, m_ref[...], v_ref[...])
            g_ref[...] = g
            d_ref[...] = delta
            nm_ref[...] = m2
            nv_ref[...] = v2

    args, out_shape = [], []
    for got, w, m, v in items:
        assert got.shape == (N_DEV,) + w.shape
        args += [got, w, m, v]
        out_shape += [jax.ShapeDtypeStruct(w.shape, F32)] * 4
    outs = pl.pallas_call(body, name="sum_devices_adamw", out_shape=out_shape,
                          compiler_params=pltpu.CompilerParams(vmem_limit_bytes=VMEM_LIMIT))(*args)
    return [tuple(outs[4 * a:4 * a + 4]) for a in range(n)]


def _mod_fwd(c_rows, w_mod_loc, b_mod_loc):
    def body(c_ref, w_ref, b_ref, o_ref):
        cr = c_ref[...]
        a = (cr * _sigmoid(cr)).astype(BF16)
        o_ref[...] = _dot(a, w_ref[...].astype(BF16)) + b_ref[...]

    return pl.pallas_call(
        body, name="mod_fwd", out_shape=jax.ShapeDtypeStruct((c_rows.shape[0], w_mod_loc.shape[1]), F32),
        compiler_params=pltpu.CompilerParams(vmem_limit_bytes=VMEM_LIMIT),
    )(c_rows, w_mod_loc, b_mod_loc)


def _mod_bwd(c_rows, dmod_loc, w_mod_loc, m, v):
    def body(c_ref, dm_ref, w_ref, m_ref, v_ref, g_ref, d_ref, nm_ref, nv_ref, gc_ref):
        cr = c_ref[...]
        a = (cr * _sigmoid(cr)).astype(BF16)
        dm = dm_ref[...].astype(BF16)
        g = _dot_tn(a, dm)
        w = w_ref[...]
        delta, m2, v2 = _adamw(w, g, m_ref[...], v_ref[...])
        g_ref[...] = g
        d_ref[...] = delta
        nm_ref[...] = m2
        nv_ref[...] = v2
        gc_ref[...] = _dot_nt(dm[16:24, :], w.astype(BF16))

    shp = jax.ShapeDtypeStruct(w_mod_loc.shape, F32)
    return pl.pallas_call(
        body, name="mod_bwd", out_shape=[shp, shp, shp, shp, jax.ShapeDtypeStruct((8, D), F32)],
        compiler_params=pltpu.CompilerParams(vmem_limit_bytes=VMEM_LIMIT),
    )(c_rows, dmod_loc, w_mod_loc, m, v)


def _fwd_in(x, modrows, norm_w, w_in_b, cos, sins, qnw_t, knw_t, ones_bd, k_all, v_all, shards):
    bl, s, _ = x.shape
    nt = s // TM
    ctx_tiles = (k_all.shape[2] - s) // TM
    assert ctx_tiles * TM + s == k_all.shape[2]
    n_sh = len(shards)

    def body(*refs):
        (x_ref, mod_ref, nw_ref, win_ref, cos_ref, sin_ref, qnw_ref, knw_ref, bd_ref, kin_ref, vin_ref) = refs[:11]
        shard_refs = refs[11:11 + n_sh]
        q_ref, k_ref, v_ref, pq_ref, pkv_ref, za_ref, glu_ref, zc_ref = refs[11 + n_sh:19 + n_sh]
        gathered_refs = refs[19 + n_sh:19 + 2 * n_sh]
        send_sems, recv_sems, local_sems = refs[19 + 2 * n_sh:]
        b, i = pl.program_id(0), pl.program_id(1)
        local, remote = _exchange_copies(shard_refs, gathered_refs, send_sems, recv_sems, local_sems, scatter=False)

        @pl.when((b == 0) & (i == 0))
        def _():
            for cp in local + remote:
                cp.start()

        xv = x_ref[0]
        shift = mod_ref[0, 0:1, :]
        scale = mod_ref[0, 1:2, :]
        r = lax.rsqrt(jnp.mean(xv * xv, axis=-1, keepdims=True) + EPS)
        u = (xv * r * nw_ref[...]) * (1.0 + scale) + shift
        p = _dot_nt(u.astype(BF16), win_ref[...])
        pq = p[:, 0:DA]
        pk = p[:, DA:DA + HD * 2]
        ck = cos_ref[...]
        sk = sin_ref[...]
        cs = jnp.concatenate([ck] * (DA // KVW), axis=-1)
        sn = jnp.concatenate([sk] * (DA // KVW), axis=-1)
        rq = lax.rsqrt(_segsum(pq * pq, bd_ref[...]) * (1.0 / HD) + EPS)
        qn = pq * rq * qnw_ref[...]
        qr = _rope(qn, cs, sn) * 0.125
        for h in range(DA // HD):
            q_ref[0, h] = qr[:, h * HD:(h + 1) * HD].astype(BF16)
        rk = lax.rsqrt(_segsum(pk * pk, bd_ref[0:KVW, 0:KVW]) * (1.0 / HD) + EPS)
        kn = pk * rk * knw_ref[...]
        kr = _rope(kn, ck, sk)
        pv = p[:, 640:768]
        for h in range(KVW // HD):
            k_ref[0, h] = kr[:, h * HD:(h + 1) * HD].astype(BF16)
            v_ref[0, h] = _with_ones_column(pv[:, h * HD:(h + 1) * HD]).astype(BF16)
        pq_ref[0] = pq
        pkv_ref[0] = p[:, 512:768]
        za_ref[0] = p[:, 768:1280]
        glu_ref[0] = p[:, 1280:2304]
        zc_ref[0] = p[:, 2304:2816]

        @pl.when((b == bl - 1) & (i == nt - 1))
        def _():
            for cp in remote:
                cp.wait_recv()
            for cp in remote:
                cp.wait_send()
            for cp in local:
                cp.wait()

    def tile(w):
        return pl.BlockSpec((1, TM, w), lambda b, i: (b, i, 0))

    def const(shape):
        return pl.BlockSpec(shape, lambda b, i: (0,) * len(shape))

    outs = [(DA, F32), (2 * KVW, F32), (DA, F32), (2 * DC, F32), (DC, F32)]
    anyspace = pl.BlockSpec(memory_space=pl.ANY)
    rope = pl.BlockSpec((TM, KVW), lambda b, i: (i, 0))
    k_tile = pl.BlockSpec((1, KVW // HD, TM, HD), lambda b, i: (b, 0, ctx_tiles + i, 0))
    v_tile = pl.BlockSpec((1, KVW // HD, TM, 2 * HD), lambda b, i: (b, 0, ctx_tiles + i, 0))
    res = pl.pallas_call(
        body, name="fwd_in", grid=(bl, nt),
        in_specs=[tile(D), pl.BlockSpec((1, 3, D), lambda b, i: (b, 0, 0)), const((1, D)), const((D_IN, D)),
                  rope, rope, const((1, DA)), const((1, KVW)), const((DA, DA)), anyspace, anyspace]
        + [anyspace] * n_sh,
        out_specs=[pl.BlockSpec((1, DA // HD, TM, HD), lambda b, i: (b, 0, i, 0)), k_tile, v_tile]
        + [tile(w) for w, _ in outs] + [anyspace] * n_sh,
        out_shape=[jax.ShapeDtypeStruct((bl, DA // HD, s, HD), BF16), jax.ShapeDtypeStruct(k_all.shape, BF16),
                   jax.ShapeDtypeStruct(v_all.shape, BF16)]
        + [jax.ShapeDtypeStruct((bl, s, w), dt) for w, dt in outs]
        + [jax.ShapeDtypeStruct((N_DEV,) + a.shape, a.dtype) for a in shards],
        input_output_aliases={9: 1, 10: 2},
        scratch_shapes=_exchange_scratch(n_sh),
        compiler_params=_params(("arbitrary", "arbitrary")),
    )(x, modrows, norm_w, w_in_b, cos, sins, qnw_t, knw_t, ones_bd, k_all, v_all, *shards)
    return res[:8], res[8:]


_KV_ROWS_OF_W_IN_T = pl.BlockSpec((2 * KVW, D), lambda b: (DA // (2 * KVW), 0))


def _ctx_fwd(ctx, modc, norm_w, w_kv_b, knw_t, ones_bd, n_keys):
    bl, cl, _ = ctx.shape

    def body(x_ref, mod_ref, nw_ref, w_ref, knw_ref, bd_ref, k_ref, v_ref, pkv_ref):
        xv = x_ref[0]
        shift = mod_ref[0, 0:1, :]
        scale = mod_ref[0, 1:2, :]
        r = lax.rsqrt(jnp.mean(xv * xv, axis=-1, keepdims=True) + EPS)
        u = (xv * r * nw_ref[...]) * (1.0 + scale) + shift
        p = _dot_nt(u.astype(BF16), w_ref[...])
        pk = p[:, 0:KVW]
        rk = lax.rsqrt(_segsum(pk * pk, bd_ref[...]) * (1.0 / HD) + EPS)
        kn = pk * rk * knw_ref[...]
        pv = p[:, KVW:2 * KVW]
        for h in range(KVW // HD):
            k_ref[0, h] = kn[:, h * HD:(h + 1) * HD].astype(BF16)
            v_ref[0, h] = _with_ones_column(pv[:, h * HD:(h + 1) * HD]).astype(BF16)
        pkv_ref[0] = p

    def const(shape):
        return pl.BlockSpec(shape, lambda b: (0,) * len(shape))

    def tile(w):
        return pl.BlockSpec((1, cl, w), lambda b: (b, 0, 0))

    k_tile = pl.BlockSpec((1, KVW // HD, cl, HD), lambda b: (b, 0, 0, 0))
    v_tile = pl.BlockSpec((1, KVW // HD, cl, 2 * HD), lambda b: (b, 0, 0, 0))
    return pl.pallas_call(
        body, name="ctx_fwd", grid=(bl,),
        in_specs=[tile(D), const((1, 3, D)), const((1, D)), _KV_ROWS_OF_W_IN_T, const((1, KVW)), const((KVW, KVW))],
        out_specs=[k_tile, v_tile, tile(2 * KVW)],
        out_shape=[jax.ShapeDtypeStruct((bl, KVW // HD, n_keys, HD), BF16),
                   jax.ShapeDtypeStruct((bl, KVW // HD, n_keys, 2 * HD), BF16),
                   jax.ShapeDtypeStruct((bl, cl, 2 * KVW), F32)],
        compiler_params=_params(("arbitrary",)),
    )(ctx, modc, norm_w, w_kv_b, knw_t, ones_bd)


def _attn_fwd(q, k, v1):
    bl, _, s, _ = q.shape
    n_keys = k.shape[2]

    def body(q_ref, k_ref, v_ref, o_ref, lse_ref):
        kv = k_ref[0, 0]
        vv = v_ref[0, 0]
        lane = lax.broadcasted_iota(jnp.int32, (TQ, 2 * HD), 1)
        for part in range(FWD_PARTS):
            rows = pl.ds(part * TQ, TQ)
            lse = jnp.zeros((TQ, 2 * HD), F32)
            heads = []
            sc_all = _dot_nt(q_ref[0, :, rows, :].reshape(4 * TQ, HD), kv)
            for h in range(4):
                sc = sc_all[h * TQ:(h + 1) * TQ, :]
                m = jnp.max(sc, axis=-1, keepdims=True)
                e = jnp.exp(sc - m).astype(BF16)
                ov = _dot(e, vv)
                denom = ov[:, HD:HD + 1]
                heads.append(ov[:, 0:HD] * (1.0 / denom))
                lse = jnp.where(lane == h, m + jnp.log(denom), lse)
            o_ref[0, rows, :] = jnp.concatenate(heads, axis=-1)
            lse_ref[0, 0, rows, :] = lse

    tq = FWD_PARTS * TQ
    ks = pl.BlockSpec((1, 1, n_keys, HD), lambda b, g, i: (b, g, 0, 0))
    qs = pl.BlockSpec((1, 4, tq, HD), lambda b, g, i: (b, g, i, 0))
    vs = pl.BlockSpec((1, 1, n_keys, 2 * HD), lambda b, g, i: (b, g, 0, 0))
    return pl.pallas_call(
        body, name="attn_fwd", grid=(bl, 2, s // tq), in_specs=[qs, ks, vs],
        out_specs=[pl.BlockSpec((1, tq, 4 * HD), lambda b, g, i: (b, i, g)),
                   pl.BlockSpec((1, 1, tq, 2 * HD), lambda b, g, i: (b, g, i, 0))],
        out_shape=[jax.ShapeDtypeStruct((bl, s, DA), F32), jax.ShapeDtypeStruct((bl, 2, s, 2 * HD), F32)],
        compiler_params=_params(("arbitrary", "arbitrary", "arbitrary")),
    )(q, k, v1)


def _attn_bwd(q, k, v1, do, o, lse, exchange):
    bl, _, s, _ = q.shape
    n_keys = k.shape[2]
    tq = BWD_PARTS * TQ
    nq = s // tq
    n_ex = len(exchange)

    def body(*refs):
        q_ref, k_ref, v_ref, do_ref, o_ref, lse_ref = refs[:6]
        part_refs = refs[6:6 + n_ex]
        dq_ref, dk_ref, dv_ref = refs[6 + n_ex:9 + n_ex]
        got_refs = refs[9 + n_ex:9 + 2 * n_ex]
        p_sc, ds_sc, dkt, dvt, send_sems, recv_sems, local_sems = refs[9 + 2 * n_ex:]
        i = pl.program_id(2)
        first = (pl.program_id(0) == 0) & (pl.program_id(1) == 0) & (i == 0)
        last = (pl.program_id(0) == bl - 1) & (pl.program_id(1) == 1) & (i == nq - 1)
        local, remote = _exchange_copies(part_refs, got_refs, send_sems, recv_sems, local_sems, scatter=True)

        @pl.when(first)
        def _():
            for cp in local + remote:
                cp.start()

        @pl.when(i == 0)
        def _():
            dkt[...] = jnp.zeros_like(dkt)
            dvt[...] = jnp.zeros_like(dvt)

        kv = k_ref[0, 0]
        vv = v_ref[0, 0][:, 0:HD]
        q_cats, do_cats = [], []
        for part in range(BWD_PARTS):
            tq_rows = pl.ds(part * TQ, TQ)
            lse = lse_ref[0, 0, tq_rows, :]
            ov = o_ref[0, tq_rows, :]
            dqs = []
            q_cat = q_ref[0, :, tq_rows, :].reshape(4 * TQ, HD)
            do_cat = do_ref[0, :, tq_rows, :].reshape(4 * TQ, HD)
            sc_all = _dot_nt(q_cat, kv)
            for h in range(4):
                doh = do_cat[h * TQ:(h + 1) * TQ, :]
                delta = jnp.sum(ov[:, h * HD:(h + 1) * HD] * doh.astype(F32), axis=-1, keepdims=True)
                rows = pl.ds((part * 4 + h) * TQ, TQ)
                p = jnp.exp(sc_all[h * TQ:(h + 1) * TQ, :] - lse[:, h:h + 1])
                ds = (p * (_dot_nt(doh, vv) - delta)).astype(BF16)
                p_sc[rows, :] = p.astype(BF16)
                ds_sc[rows, :] = ds
                dqs.append(_dot(ds, kv) * 0.125)
            dq_ref[0, tq_rows, :] = jnp.concatenate(dqs, axis=-1)
            q_cats.append(q_cat)
            do_cats.append(do_cat)
        dvt[...] += _dot_tn(jnp.concatenate(do_cats, axis=0), p_sc[...])
        dkt[...] += _dot_tn(jnp.concatenate(q_cats, axis=0), ds_sc[...])

        @pl.when(i == nq - 1)
        def _():
            dk_ref[0, 0] = dkt[...].T
            dv_ref[0, 0] = dvt[...].T

        @pl.when(last)
        def _():
            for cp in remote:
                cp.wait_recv()
            for cp in remote:
                cp.wait_send()
            for cp in local:
                cp.wait()

    qs = pl.BlockSpec((1, 4, tq, HD), lambda b, g, i: (b, g, i, 0))
    ks = pl.BlockSpec((1, 1, n_keys, HD), lambda b, g, i: (b, g, 0, 0))
    vs = pl.BlockSpec((1, 1, n_keys, 2 * HD), lambda b, g, i: (b, g, 0, 0))
    os_ = pl.BlockSpec((1, tq, 4 * HD), lambda b, g, i: (b, i, g))
    kshape = jax.ShapeDtypeStruct(k.shape, F32)
    anyspace = pl.BlockSpec(memory_space=pl.ANY)
    res = pl.pallas_call(
        body, name="attn_bwd", grid=(bl, 2, nq),
        in_specs=[qs, ks, vs, qs, os_, pl.BlockSpec((1, 1, tq, 2 * HD), lambda b, g, i: (b, g, i, 0))]
        + [anyspace] * n_ex,
        out_specs=[os_, ks, ks] + [anyspace] * n_ex,
        out_shape=[jax.ShapeDtypeStruct((bl, s, DA), F32), kshape, kshape]
        + [jax.ShapeDtypeStruct(a.shape, a.dtype) for a in exchange],
        scratch_shapes=[pltpu.VMEM((4 * tq, n_keys), BF16), pltpu.VMEM((4 * tq, n_keys), BF16),
                        pltpu.VMEM((HD, n_keys), F32), pltpu.VMEM((HD, n_keys), F32)] + _exchange_scratch(n_ex),
        compiler_params=_params(("arbitrary", "arbitrary", "arbitrary")),
    )(q, k, v1, do, o, lse, *exchange)
    return res[:3], res[3:]


def _halo_specs(width, s):
    per = TC // HALO
    last = s // HALO - 1
    main = pl.BlockSpec((1, TC, width), lambda b, i: (b, i, 0))
    prev = pl.BlockSpec((1, HALO, width), lambda b, i: (b, jnp.maximum(i * per - 1, 0), 0))
    nxt = pl.BlockSpec((1, HALO, width), lambda b, i: (b, jnp.minimum((i + 1) * per, last), 0))
    return main, prev, nxt


def _glu(g):
    return g[:, 0:DC] * _sigmoid(g[:, DC:2 * DC])


def _fill_padded(pad_ref, main, prev, nxt, first, last):
    pad_ref[0:HALO, :] = jnp.where(first, 0.0, prev)
    pad_ref[HALO:HALO + TC, :] = main
    pad_ref[HALO + TC:2 * HALO + TC, :] = jnp.where(last, 0.0, nxt)


PLANE_ROWS = TC + 2 * HALO - 8


def _shift_planes(pad_ref, planes_ref):
    for r in range(1, 8):
        planes_ref[r - 1] = pad_ref[pl.ds(r, PLANE_ROWS), :]


def _tap_rows(pad_ref, planes_ref, offset, start, n):
    a, r = divmod(offset, 8)
    if r == 0:
        return pad_ref[pl.ds(start + 8 * a, n), :]
    return planes_ref[r - 1, pl.ds(start + 8 * a, n), :]


def _conv_fwd(glu, conv_w, conv_b, ln_w, ln_b, w_pw_b, b_pw):
    bl, s, _ = glu.shape
    nt = s // TC

    def body(g_ref, gp_ref, gn_ref, cw_ref, cb_ref, lw_ref, lb_ref, wpw_ref, bpw_ref, y_ref, cp_ref, pad_ref, planes_ref):
        i = pl.program_id(1)
        _fill_padded(pad_ref, _glu(g_ref[0]), _glu(gp_ref[0]), _glu(gn_ref[0]), i == 0, i == nt - 1)
        _shift_planes(pad_ref, planes_ref)
        for ck in range(TC // CH):
            acc = jnp.zeros((CH, DC), F32) + cb_ref[...]
            for t in range(KW):
                acc = acc + _tap_rows(pad_ref, planes_ref, 1 + t, ck * CH, CH) * cw_ref[t:t + 1, :]
            y_ref[0, pl.ds(ck * CH, CH), :] = acc
        y = y_ref[0]
        mu = jnp.mean(y, axis=-1, keepdims=True)
        yc = y - mu
        var = jnp.mean(yc * yc, axis=-1, keepdims=True)
        z = yc * lax.rsqrt(var + EPS) * lw_ref[...] + lb_ref[...]
        act = z * _sigmoid(z)
        cp_ref[0] = _dot(act.astype(BF16), wpw_ref[...]) + bpw_ref[...]

    def const(shape):
        return pl.BlockSpec(shape, lambda b, i: (0,) * len(shape))

    main, prev, nxt = _halo_specs(2 * DC, s)
    tile = pl.BlockSpec((1, TC, DC), lambda b, i: (b, i, 0))
    return pl.pallas_call(
        body, name="conv_fwd", grid=(bl, nt),
        in_specs=[main, prev, nxt, const((32, DC)), const((1, DC)), const((1, DC)), const((1, DC)),
                  const((DC, DC)), const((1, DC))],
        out_specs=[tile, tile],
        out_shape=[jax.ShapeDtypeStruct((bl, s, DC), F32)] * 2,
        scratch_shapes=[pltpu.VMEM((TC + 2 * HALO, DC), F32), pltpu.VMEM((7, PLANE_ROWS, DC), F32)],
        compiler_params=_params(("arbitrary", "arbitrary")),
    )(glu, glu, glu, conv_w, conv_b, ln_w, ln_b, w_pw_b, b_pw)


def _conv_bwd_pointwise(y, dcp, ln_w, ln_b, w_pw_b):
    bl, s, _ = y.shape
    nt = s // TM

    def body(y_ref, dcp_ref, lw_ref, lb_ref, wpw_ref, dy_ref, gw_ref, rows_ref):
        @pl.when((pl.program_id(0) == 0) & (pl.program_id(1) == 0))
        def _():
            gw_ref[...] = jnp.zeros_like(gw_ref)
            rows_ref[...] = jnp.zeros_like(rows_ref)

        y = y_ref[0]
        dcp = dcp_ref[0]
        mu = jnp.mean(y, axis=-1, keepdims=True)
        yc = y - mu
        rstd = lax.rsqrt(jnp.mean(yc * yc, axis=-1, keepdims=True) + EPS)
        yn = yc * rstd
        lw = lw_ref[...]
        z = yn * lw + lb_ref[...]
        sg = _sigmoid(z)
        act = z * sg
        dcp_b = dcp.astype(BF16)
        gw_ref[...] += _dot_tn(act.astype(BF16), dcp_b)
        dact = _dot_nt(dcp_b, wpw_ref[...])
        dz = dact * (sg * (1.0 + z * (1.0 - sg)))
        dyn = dz * lw
        dy = rstd * (dyn - jnp.mean(dyn, axis=-1, keepdims=True) - yn * jnp.mean(dyn * yn, axis=-1, keepdims=True))
        dy_ref[0] = dy
        rows_ref[0:1, :] += jnp.sum(dcp, axis=0, keepdims=True)
        rows_ref[1:2, :] += jnp.sum(dz * yn, axis=0, keepdims=True)
        rows_ref[2:3, :] += jnp.sum(dz, axis=0, keepdims=True)
        rows_ref[3:4, :] += jnp.sum(dy, axis=0, keepdims=True)

    def const(shape):
        return pl.BlockSpec(shape, lambda b, i: (0,) * len(shape))

    tile = pl.BlockSpec((1, TM, DC), lambda b, i: (b, i, 0))
    return pl.pallas_call(
        body, name="conv_bwd_pointwise", grid=(bl, nt),
        in_specs=[tile, tile, const((1, DC)), const((1, DC)), const((DC, DC))],
        out_specs=[tile, const((DC, DC)), const((8, DC))],
        out_shape=[jax.ShapeDtypeStruct((bl, s, DC), F32), jax.ShapeDtypeStruct((DC, DC), F32),
                   jax.ShapeDtypeStruct((8, DC), F32)],
        compiler_params=_params(("arbitrary", "arbitrary")),
    )(y, dcp, ln_w, ln_b, w_pw_b)


def _conv_bwd_depthwise(glu, dy, conv_w):
    bl, s, _ = glu.shape
    nt = s // TC

    def body(g_ref, gp_ref, gn_ref, d_ref, dp_ref, dn_ref, cw_ref, dglu_ref, dcw_ref,
             padu_ref, padd_ref, planes_u, planes_d):
        i = pl.program_id(1)

        @pl.when((pl.program_id(0) == 0) & (i == 0))
        def _():
            dcw_ref[...] = jnp.zeros_like(dcw_ref)

        first, last = i == 0, i == nt - 1
        _fill_padded(padu_ref, _glu(g_ref[0]), _glu(gp_ref[0]), _glu(gn_ref[0]), first, last)
        _fill_padded(padd_ref, d_ref[0], dp_ref[0], dn_ref[0], first, last)
        _shift_planes(padu_ref, planes_u)
        _shift_planes(padd_ref, planes_d)
        for ck in range(TC // CH):
            acc = jnp.zeros((CH, DC), F32)
            for t in range(KW):
                acc = acc + _tap_rows(padd_ref, planes_d, 2 * HALO - 1 - t, ck * CH, CH) * cw_ref[t:t + 1, :]
            g = g_ref[0, pl.ds(ck * CH, CH), :]
            a = g[:, 0:DC]
            sg = _sigmoid(g[:, DC:2 * DC])
            dglu_ref[0, pl.ds(ck * CH, CH), 0:DC] = acc * sg
            dglu_ref[0, pl.ds(ck * CH, CH), DC:2 * DC] = acc * a * sg * (1.0 - sg)
        group = 4
        for t0 in range(0, KW, group):
            taps = range(t0, min(t0 + group, KW))
            acc8 = [jnp.zeros((8, DC), F32) for _ in taps]
            for ck in range(TC // CH):
                dchunk = d_ref[0, pl.ds(ck * CH, CH), :]
                for n, t in enumerate(taps):
                    prod = _tap_rows(padu_ref, planes_u, 1 + t, ck * CH, CH) * dchunk
                    acc8[n] = acc8[n] + jnp.sum(prod.reshape(CH // 8, 8, DC), axis=0)
            for n, t in enumerate(taps):
                dcw_ref[t:t + 1, :] += jnp.sum(acc8[n], axis=0, keepdims=True)

    gmain, gprev, gnext = _halo_specs(2 * DC, s)
    dmain, dprev, dnext = _halo_specs(DC, s)
    cw = pl.BlockSpec((32, DC), lambda b, i: (0, 0))
    return pl.pallas_call(
        body, name="conv_bwd_depthwise", grid=(bl, nt),
        in_specs=[gmain, gprev, gnext, dmain, dprev, dnext, cw],
        out_specs=[gmain, cw],
        out_shape=[jax.ShapeDtypeStruct((bl, s, 2 * DC), F32), jax.ShapeDtypeStruct((32, DC), F32)],
        scratch_shapes=[pltpu.VMEM((TC + 2 * HALO, DC), F32)] * 2 + [pltpu.VMEM((7, PLANE_ROWS, DC), F32)] * 2,
        compiler_params=_params(("arbitrary", "arbitrary")),
    )(glu, glu, glu, dy, dy, dy, conv_w)


def _out_fwd_bwd(attn, za, cp, zc, x, target, modrows, w_out_b):
    bl, s, _ = x.shape
    nt = s // TM

    def body(o_ref, za_ref, cp_ref, zc_ref, x_ref, t_ref, mod_ref, w_ref,
             do_ref, dza_ref, dcp_ref, dzc_ref, dh_ref, dgate_ref, gw_ref, loss_ref):
        b, i = pl.program_id(0), pl.program_id(1)

        @pl.when((b == 0) & (i == 0))
        def _():
            gw_ref[...] = jnp.zeros_like(gw_ref)
            loss_ref[...] = jnp.zeros_like(loss_ref)

        @pl.when(i == 0)
        def _():
            dgate_ref[...] = jnp.zeros_like(dgate_ref)

        o, za_v, cp_v, zc_v = o_ref[0], za_ref[0], cp_ref[0], zc_ref[0]
        gate = mod_ref[0, 2:3, :]
        sa = _sigmoid(za_v)
        sc = _sigmoid(zc_v)
        silu_a = za_v * sa
        silu_c = zc_v * sc
        mix = jnp.concatenate([(o * silu_a).astype(BF16), (cp_v * silu_c).astype(BF16)], axis=-1)
        w = w_ref[...]
        out = _dot(mix, w)
        err = x_ref[0] + gate * out - t_ref[0]
        loss_ref[...] += jnp.sum(err * err, axis=0, keepdims=True)
        dh = err * (1.0 / D)
        dh_ref[0] = dh
        dgate_ref[0] += jnp.sum(dh * out, axis=0, keepdims=True)
        dout = (dh * gate).astype(BF16)
        gw_ref[...] += _dot_tn(mix, dout)
        dmix = _dot_nt(dout, w)
        dga = dmix[:, 0:DA]
        dgc = dmix[:, DA:DA + DC]
        dov = dga * silu_a
        for h in range(DA // HD):
            do_ref[0, h] = dov[:, h * HD:(h + 1) * HD].astype(BF16)
        dza_ref[0] = dga * o * (sa * (1.0 + za_v * (1.0 - sa)))
        dcp_ref[0] = dgc * silu_c
        dzc_ref[0] = dgc * cp_v * (sc * (1.0 + zc_v * (1.0 - sc)))

    def const(shape):
        return pl.BlockSpec(shape, lambda b, i: (0,) * len(shape))

    def tile(w):
        return pl.BlockSpec((1, TM, w), lambda b, i: (b, i, 0))

    return pl.pallas_call(
        body, name="out_fwd_bwd", grid=(bl, nt),
        in_specs=[tile(DA), tile(DA), tile(DC), tile(DC), tile(D), tile(D),
                  pl.BlockSpec((1, 3, D), lambda b, i: (b, 0, 0)), const((D, D))],
        out_specs=[pl.BlockSpec((1, DA // HD, TM, HD), lambda b, i: (b, 0, i, 0)), tile(DA), tile(DC), tile(DC), tile(D),
                   pl.BlockSpec((1, 1, D), lambda b, i: (b, 0, 0)), const((D, D)), const((1, D))],
        out_shape=[jax.ShapeDtypeStruct((bl, DA // HD, s, HD), BF16), jax.ShapeDtypeStruct((bl, s, DA), F32),
                   jax.ShapeDtypeStruct((bl, s, DC), F32), jax.ShapeDtypeStruct((bl, s, DC), F32),
                   jax.ShapeDtypeStruct((bl, s, D), F32), jax.ShapeDtypeStruct((bl, 1, D), F32),
                   jax.ShapeDtypeStruct((D, D), F32), jax.ShapeDtypeStruct((1, D), F32)],
        compiler_params=_params(("arbitrary", "arbitrary")),
    )(attn, za, cp, zc, x, target, modrows, w_out_b)


def _rms_heads_bwd(dy, x, w_t, ones_bd):
    r = lax.rsqrt(_segsum(x * x, ones_bd) * (1.0 / HD) + EPS)
    xh = x * r
    g = dy * w_t
    dx = r * (g - xh * (_segsum(g * xh, ones_bd) * (1.0 / HD)))
    return dx, dy * xh


def _ctx_bwd(ctx, modc, norm_w, w_kv_b, pkv_c, dk_c, dv_c, knw_t, ones_bd):
    bl, cl, _ = ctx.shape

    def body(x_ref, mod_ref, nw_ref, w_ref, p_ref, dk_ref, dv_ref, knw_ref, bd_ref, gw_ref, rows_ref, dknw_ref):
        @pl.when(pl.program_id(0) == 0)
        def _():
            gw_ref[...] = jnp.zeros_like(gw_ref)
            rows_ref[...] = jnp.zeros_like(rows_ref)
            dknw_ref[...] = jnp.zeros_like(dknw_ref)

        xv = x_ref[0]
        shift = mod_ref[0, 0:1, :]
        scale = mod_ref[0, 1:2, :]
        nw = nw_ref[...]
        r = lax.rsqrt(jnp.mean(xv * xv, axis=-1, keepdims=True) + EPS)
        xn = xv * r
        yv = xn * nw
        u = yv * (1.0 + scale) + shift
        dkv = jnp.concatenate([dk_ref[0, 0], dk_ref[0, 1]], axis=-1)
        dpk, dknw = _rms_heads_bwd(dkv, p_ref[0][:, 0:KVW], knw_ref[...], bd_ref[...])
        dp = jnp.concatenate([dpk.astype(BF16), dv_ref[0, 0].astype(BF16), dv_ref[0, 1].astype(BF16)], axis=-1)
        gw_ref[...] += _dot_tn(dp, u.astype(BF16))
        du = _dot(dp, w_ref[...])
        rows_ref[0:1, :] += jnp.sum(du, axis=0, keepdims=True)
        rows_ref[1:2, :] += jnp.sum(du * yv, axis=0, keepdims=True)
        rows_ref[2:3, :] += jnp.sum(du * (1.0 + scale) * xn, axis=0, keepdims=True)
        dknw_ref[...] += jnp.sum(dknw, axis=0, keepdims=True)

    def const(shape):
        return pl.BlockSpec(shape, lambda b: (0,) * len(shape))

    def tile(w):
        return pl.BlockSpec((1, cl, w), lambda b: (b, 0, 0))

    kv_tile = pl.BlockSpec((1, KVW // HD, cl, HD), lambda b: (b, 0, 0, 0))
    return pl.pallas_call(
        body, name="ctx_bwd", grid=(bl,),
        in_specs=[tile(D), const((1, 3, D)), const((1, D)), _KV_ROWS_OF_W_IN_T, tile(2 * KVW), kv_tile, kv_tile,
                  const((1, KVW)), const((KVW, KVW))],
        out_specs=[const((2 * KVW, D)), const((8, D)), const((1, KVW))],
        out_shape=[jax.ShapeDtypeStruct((2 * KVW, D), F32), jax.ShapeDtypeStruct((8, D), F32),
                   jax.ShapeDtypeStruct((1, KVW), F32)],
        compiler_params=_params(("arbitrary",)),
    )(ctx, modc, norm_w, w_kv_b, pkv_c, dk_c, dv_c, knw_t, ones_bd)


def _bwd_in(x, modrows, norm_w, w_in_b, cos, sins, qnw_t, knw_t, ones_bd,
            pq, pkv, dq, dk, dv, dza, dglu, dzc, dh, gw_kv):
    bl, s, _ = x.shape
    nt = s // TM

    def body(x_ref, mod_ref, nw_ref, win_hbm, cos_ref, sin_ref, qnw_ref, knw_ref, bd_ref,
             pq_ref, pkv_ref, dq_ref, dk_ref, dv_ref, dza_ref, dglu_ref, dzc_ref, dh_ref, gwkv_ref,
             gx_ref, gw_hbm, dmod_ref, dnw_ref, dqnw_ref, dknw_ref, win_ref, gw_acc, sem):
        b, i = pl.program_id(0), pl.program_id(1)

        @pl.when((b == 0) & (i == 0))
        def _():
            cp = pltpu.make_async_copy(win_hbm, win_ref, sem)
            cp.start()
            gw_acc[...] = jnp.zeros_like(gw_acc)
            dnw_ref[...] = jnp.zeros_like(dnw_ref)
            dqnw_ref[...] = jnp.zeros_like(dqnw_ref)
            dknw_ref[...] = jnp.zeros_like(dknw_ref)
            cp.wait()

        @pl.when(i == 0)
        def _():
            dmod_ref[...] = jnp.zeros_like(dmod_ref)

        ck = cos_ref[...]
        sk = sin_ref[...]
        cs = jnp.concatenate([ck] * (DA // KVW), axis=-1)
        sn = jnp.concatenate([sk] * (DA // KVW), axis=-1)
        bd = bd_ref[...]
        dqn = _rope_bwd(dq_ref[0], cs, sn)
        dpq, dqnw = _rms_heads_bwd(dqn, pq_ref[0], qnw_ref[...], bd)
        dkn = _rope_bwd(jnp.concatenate([dk_ref[0, 0], dk_ref[0, 1]], axis=-1), ck, sk)
        dpk, dknw = _rms_heads_bwd(dkn, pkv_ref[0][:, 0:KVW], knw_ref[...], bd[0:KVW, 0:KVW])
        dqnw_ref[...] += jnp.sum(dqnw, axis=0, keepdims=True)
        dknw_ref[...] += jnp.sum(dknw, axis=0, keepdims=True)
        dp = jnp.concatenate(
            [dpq.astype(BF16), dpk.astype(BF16), dv_ref[0, 0].astype(BF16), dv_ref[0, 1].astype(BF16),
             dza_ref[0].astype(BF16), dglu_ref[0].astype(BF16), dzc_ref[0].astype(BF16)], axis=-1)

        xv = x_ref[0]
        shift = mod_ref[0, 0:1, :]
        scale = mod_ref[0, 1:2, :]
        nw = nw_ref[...]
        r = lax.rsqrt(jnp.mean(xv * xv, axis=-1, keepdims=True) + EPS)
        xn = xv * r
        yv = xn * nw
        u = yv * (1.0 + scale) + shift
        gw_acc[...] += _dot_tn(dp, u.astype(BF16))
        du = _dot(dp, win_ref[...])
        dmod_ref[0, 0:1, :] += jnp.sum(du, axis=0, keepdims=True)
        dmod_ref[0, 1:2, :] += jnp.sum(du * yv, axis=0, keepdims=True)
        dy = du * (1.0 + scale)
        dnw_ref[...] += jnp.sum(dy * xn, axis=0, keepdims=True)
        dxn = dy * nw
        gx_ref[0] = dh_ref[0] + r * (dxn - xn * jnp.mean(dxn * xn, axis=-1, keepdims=True))

        @pl.when((b == bl - 1) & (i == nt - 1))
        def _():
            gw_acc[DA:DA + 2 * KVW, :] += gwkv_ref[...]

            def to_bf16(j, carry):
                rows = pl.ds(pl.multiple_of(j * 2 * KVW, 2 * KVW), 2 * KVW)
                win_ref[rows, :] = gw_acc[rows, :].astype(BF16)
                return carry

            lax.fori_loop(0, D_IN // (2 * KVW), to_bf16, 0)
            pltpu.sync_copy(win_ref, gw_hbm)

    def tile(w):
        return pl.BlockSpec((1, TM, w), lambda b, i: (b, i, 0))

    def const(shape):
        return pl.BlockSpec(shape, lambda b, i: (0,) * len(shape))

    anyspace = pl.BlockSpec(memory_space=pl.ANY)
    rope = pl.BlockSpec((TM, KVW), lambda b, i: (i, 0))
    ctx_tiles = (dk.shape[2] - s) // TM
    kv_tile = pl.BlockSpec((1, KVW // HD, TM, HD), lambda b, i: (b, 0, ctx_tiles + i, 0))
    return pl.pallas_call(
        body, name="bwd_in", grid=(bl, nt),
        in_specs=[tile(D), pl.BlockSpec((1, 3, D), lambda b, i: (b, 0, 0)), const((1, D)), anyspace, rope, rope,
                  const((1, DA)), const((1, KVW)), const((DA, DA)),
                  tile(DA), tile(2 * KVW), tile(DA), kv_tile, kv_tile, tile(DA), tile(2 * DC), tile(DC), tile(D),
                  const((2 * KVW, D))],
        out_specs=[tile(D), anyspace, pl.BlockSpec((1, 2, D), lambda b, i: (b, 0, 0)), const((1, D)),
                   const((1, DA)), const((1, KVW))],
        out_shape=[jax.ShapeDtypeStruct((bl, s, D), F32), jax.ShapeDtypeStruct((D_IN, D), BF16),
                   jax.ShapeDtypeStruct((bl, 2, D), F32), jax.ShapeDtypeStruct((1, D), F32),
                   jax.ShapeDtypeStruct((1, DA), F32), jax.ShapeDtypeStruct((1, KVW), F32)],
        scratch_shapes=[pltpu.VMEM((D_IN, D), BF16), pltpu.VMEM((D_IN, D), F32), pltpu.SemaphoreType.DMA],
        compiler_params=_params(("arbitrary", "arbitrary")),
    )(x, modrows, norm_w, w_in_b, cos, sins, qnw_t, knw_t, ones_bd,
      pq, pkv, dq, dk, dv, dza, dglu, dzc, dh, gw_kv)


_LOSS, _DMODC, _NW, _QN, _KN, _CB, _LW, _LB, _BPW, SMALL_W = 0, 1024, 4096, 5120, 5248, 5376, 5888, 6400, 6912, 7424


ROW_W = 1792


def _put_flat(ref, off, value):
    n, done = value.shape[1], 0
    while done < n:
        r, c = divmod(off + done, ROW_W)
        take = min(n - done, ROW_W - c)
        ref[r:r + 1, c:c + take] = value[:, done:done + take]
        done += take


def _get_flat(arr, off, n):
    parts, done = [], 0
    while done < n:
        r, c = divmod(off + done, ROW_W)
        take = min(n - done, ROW_W - c)
        parts.append(arr[r:r + 1, c:c + take])
        done += take
    return parts[0] if len(parts) == 1 else jnp.concatenate(parts, axis=-1)


def _pack_small(loss_row, ctx_rows, dnw, dqnw, dknw, dknw_c, conv_rows, dmod_ss, dgate):
    bl = dmod_ss.shape[0]
    assert SMALL_W + bl * 3 * D <= 8 * ROW_W

    def body(loss_ref, ctx_ref, dnw_ref, dqnw_ref, dknw_ref, dknwc_ref, conv_ref, dss_ref, dgate_ref, o_ref):
        o_ref[...] = jnp.zeros_like(o_ref)
        _put_flat(o_ref, _LOSS, loss_ref[...])
        _put_flat(o_ref, _DMODC, ctx_ref[0:1, :])
        _put_flat(o_ref, _DMODC + D, ctx_ref[1:2, :])
        _put_flat(o_ref, _NW, dnw_ref[...] + ctx_ref[2:3, :])
        dq = dqnw_ref[...]
        qn = dq[:, 0:HD]
        for h in range(1, DA // HD):
            qn = qn + dq[:, h * HD:(h + 1) * HD]
        _put_flat(o_ref, _QN, qn)
        dk = dknw_ref[...] + dknwc_ref[...]
        _put_flat(o_ref, _KN, dk[:, 0:HD] + dk[:, HD:2 * HD])
        _put_flat(o_ref, _BPW, conv_ref[0:1, :])
        _put_flat(o_ref, _LW, conv_ref[1:2, :])
        _put_flat(o_ref, _LB, conv_ref[2:3, :])
        _put_flat(o_ref, _CB, conv_ref[3:4, :])
        for b in range(bl):
            _put_flat(o_ref, SMALL_W + b * 3 * D, dss_ref[b, 0:1, :])
            _put_flat(o_ref, SMALL_W + b * 3 * D + D, dss_ref[b, 1:2, :])
            _put_flat(o_ref, SMALL_W + b * 3 * D + 2 * D, dgate_ref[b])

    return pl.pallas_call(
        body, name="pack_small", out_shape=jax.ShapeDtypeStruct((8, ROW_W), F32),
    )(loss_row, ctx_rows, dnw, dqnw, dknw, dknw_c, conv_rows, dmod_ss, dgate)


def _finish_small(gathered, bl):
    n_ex = N_DEV * bl

    def body(g_ref, sum_ref, dmod_ref, gb_ref, loss_ref):
        tot = g_ref[0]
        for j in range(1, N_DEV):
            tot = tot + g_ref[j]
        acc = _get_flat(tot, 0, SMALL_W)
        sum_ref[...] = acc
        dmod_ref[...] = jnp.zeros_like(dmod_ref)
        for j in range(N_DEV):
            arr = g_ref[j]
            for b in range(bl):
                dmod_ref[j * bl + b:j * bl + b + 1, :] = _get_flat(arr, SMALL_W + b * 3 * D, 3 * D)
        dmod_ref[n_ex:n_ex + 1, :] = acc[:, _DMODC:_DMODC + 3 * D]
        gb_ref[...] = jnp.sum(dmod_ref[...], axis=0, keepdims=True)
        loss_ref[...] = (0.5 / D) * jnp.sum(acc[:, _LOSS:_LOSS + D], axis=-1, keepdims=True)

    return pl.pallas_call(
        body, name="finish_small",
        out_shape=[jax.ShapeDtypeStruct((1, SMALL_W), F32), jax.ShapeDtypeStruct((n_ex + 8, 3 * D), F32),
                   jax.ShapeDtypeStruct((1, 3 * D), F32), jax.ShapeDtypeStruct((1, 1), F32)],
    )(gathered)


_SMALL = (("b_mod", None), ("norm_w", _NW), ("q_norm_w", _QN), ("k_norm_w", _KN), ("conv_b", _CB),
          ("conv_ln_w", _LW), ("conv_ln_b", _LB), ("b_pw", _BPW), ("c_ctx", None))


def _adamw_small(summed, g_bmod, gc_parts, weights, moms, vars_):
    n = len(_SMALL)

    def body(*refs):
        sum_ref, gb_ref, gc_ref = refs[0:3]
        w_refs, m_refs, v_refs = refs[3:3 + n], refs[3 + n:3 + 2 * n], refs[3 + 2 * n:3 + 3 * n]
        outs = refs[3 + 3 * n:]
        for k, (name, off) in enumerate(_SMALL):
            w = w_refs[k][...]
            if name == "b_mod":
                g = gb_ref[...]
            elif name == "c_ctx":
                acc = gc_ref[0, 0:1, :]
                for j in range(1, N_DEV):
                    acc = acc + gc_ref[j, 0:1, :]
                sg = _sigmoid(w)
                g = acc * (sg * (1.0 + w * (1.0 - sg)))
            else:
                g = sum_ref[:, off:off + w.shape[1]]
            delta, m2, v2 = _adamw(w, g, m_refs[k][...], v_refs[k][...])
            outs[k][...] = g
            outs[n + k][...] = delta
            outs[2 * n + k][...] = m2
            outs[3 * n + k][...] = v2

    shapes = [jax.ShapeDtypeStruct(w.shape, F32) for w in weights]
    outs = pl.pallas_call(body, name="adamw_small", out_shape=shapes * 4)(
        summed, g_bmod, gc_parts, *weights, *moms, *vars_)
    return [outs[k * n:(k + 1) * n] for k in range(4)]


def _rope_tables(s):
    t = jnp.arange(s, dtype=jnp.int32)
    row = (t // GRID_W).astype(F32)
    col = (t % GRID_W).astype(F32)
    freqs = ROPE_THETA ** (-jnp.arange(0, HD // 2, 2, dtype=F32) / (HD // 2))
    ang_r = row[:, None] * freqs[None, :]
    ang_c = col[:, None] * freqs[None, :]
    cr, sr, cc, sc = jnp.cos(ang_r), jnp.sin(ang_r), jnp.cos(ang_c), jnp.sin(ang_c)
    cos = jnp.concatenate([cr, cr, cc, cc], axis=-1)
    sins = jnp.concatenate([-sr, sr, -sc, sc], axis=-1)
    return jnp.tile(cos, (1, KVW // HD)), jnp.tile(sins, (1, KVW // HD))


def kernel(x, c, ctx, c_ctx, w_mod, b_mod, norm_w, w_in, q_norm_w, k_norm_w, conv_w, conv_b, conv_ln_w, conv_ln_b, w_pw, b_pw, w_out, loss_target, m_c_ctx, m_w_mod, m_b_mod, m_norm_w, m_w_in, m_q_norm_w, m_k_norm_w, m_conv_w, m_conv_b, m_conv_ln_w, m_conv_ln_b, m_w_pw, m_b_pw, m_w_out, v_c_ctx, v_w_mod, v_b_mod, v_norm_w, v_w_in, v_q_norm_w, v_k_norm_w, v_conv_w, v_conv_b, v_conv_ln_w, v_conv_ln_b, v_w_pw, v_b_pw, v_w_out):
    bl, s, _ = x.shape
    cl = ctx.shape[1]
    me = _lin(*_coords())
    n_mod = w_mod.shape[2]

    conv_w_pad = jnp.pad(conv_w[0], ((0, 32 - KW), (0, 0)))
    c_pad = jnp.pad(c, ((0, 8 - bl), (0, 0)))
    g_win, g_c = _all_gather_many([w_in[0].T.astype(BF16), c_pad], "gather_weights")
    w_in_b = g_win.reshape(D_IN, D)
    c_all = g_c[:, :bl, :].reshape(N_DEV * bl, D)
    n_ex = N_DEV * bl
    c_rows = jnp.concatenate([c_all, c_ctx[None, :], jnp.zeros((7, D), F32)], axis=0)

    b_mod_loc = lax.dynamic_slice_in_dim(b_mod, me * n_mod, n_mod, axis=1)
    mod_loc = _mod_fwd(c_rows, w_mod[0], b_mod_loc)
    g_mod = _all_gather_direct(mod_loc, "gather_mod")
    mod_all = g_mod.transpose(1, 0, 2).reshape(n_ex + 8, 3 * D)
    modrows = lax.dynamic_slice_in_dim(mod_all, me * bl, bl, axis=0).reshape(bl, 3, D)
    modc = mod_all[n_ex].reshape(1, 3, D)

    cos, sins = _rope_tables(s)
    qnw_t = jnp.tile(q_norm_w, (1, DA // HD))
    knw_t = jnp.tile(k_norm_w, (1, KVW // HD))
    lane = jnp.arange(DA, dtype=jnp.int32) // HD
    ones_bd = (lane[:, None] == lane[None, :]).astype(BF16)
    ones_kv = ones_bd[0:KVW, 0:KVW]
    w_kv_b = w_in_b

    k_ctx, v_ctx, pkv_c = _ctx_fwd(ctx, modc, norm_w, w_kv_b, knw_t, ones_kv, cl + s)
    (q_h, k_h, v_h, pq, pkv, za, glu, zc), (g_wout, g_wpw, g_cw) = _fwd_in(
        x, modrows, norm_w, w_in_b, cos, sins, qnw_t, knw_t, ones_bd, k_ctx, v_ctx,
        [w_out[0].astype(BF16), w_pw[0].astype(BF16), conv_w_pad])
    w_out_b = g_wout.reshape(D, D)
    w_pw_b = g_wpw.reshape(DC, DC)
    conv_w_full = g_cw.transpose(1, 0, 2).reshape(32, DC)
    attn, lse = _attn_fwd(q_h, k_h, v_h)
    y_conv, cp = _conv_fwd(glu, conv_w_full, conv_b, conv_ln_w, conv_ln_b, w_pw_b, b_pw)

    do_h, dza, dcp, dzc, dh, dgate, gw_out, loss_row = _out_fwd_bwd(attn, za, cp, zc, x, loss_target, modrows, w_out_b)
    dy_conv, gw_pw, conv_rows = _conv_bwd_pointwise(y_conv, dcp, conv_ln_w, conv_ln_b, w_pw_b)
    dglu, g_cw_full = _conv_bwd_depthwise(glu, dy_conv, conv_w_full)
    parts_out = gw_out.astype(BF16).reshape(N_DEV, D // N_DEV, D)
    parts_pw = gw_pw.astype(BF16).reshape(N_DEV, DC // N_DEV, DC)
    parts_cw = g_cw_full.astype(BF16).reshape(32, N_DEV, DC // N_DEV).transpose(1, 0, 2)
    (dq, dk_h, dv_h), (got_out, got_pw, got_cw) = _attn_bwd(
        q_h, k_h, v_h, do_h, attn, lse, [parts_out, parts_pw, parts_cw])
    gw_kv, ctx_rows, dknw_c = _ctx_bwd(ctx, modc, norm_w, w_kv_b, pkv_c, dk_h, dv_h, knw_t, ones_kv)
    grad_x, gw_in, dmod_ss, dnw, dqnw, dknw = _bwd_in(
        x, modrows, norm_w, w_in_b, cos, sins, qnw_t, knw_t, ones_bd,
        pq, pkv, dq, dk_h, dv_h, dza, dglu, dzc, dh, gw_kv)

    parts_in = gw_in.reshape(N_DEV, D_IN // N_DEV, D)

    def pad_cw(a):
        return jnp.pad(a[0], ((0, 32 - KW), (0, 0)))

    (r_in,) = _reduce_scatter_adamw([(parts_in, w_in[0].T, m_w_in[0].T, v_w_in[0].T)], "reduce_grads")
    r_in = tuple(a.T for a in r_in)
    r_out, r_pw, r_cw = _sum_devices_adamw(
        [(got_out, w_out[0], m_w_out[0], v_w_out[0]), (got_pw, w_pw[0], m_w_pw[0], v_w_pw[0]),
         (got_cw, pad_cw(conv_w), pad_cw(m_conv_w), pad_cw(v_conv_w))])
    r_cw = tuple(a[:KW] for a in r_cw)

    small = _pack_small(loss_row, ctx_rows, dnw, dqnw, dknw, dknw_c, conv_rows, dmod_ss, dgate)
    g_small = _all_gather_direct(small, "gather_small")
    summed, dmod_full, g_bmod, loss11 = _finish_small(g_small, bl)

    dmod_loc = lax.dynamic_slice_in_dim(dmod_full, me * n_mod, n_mod, axis=1)
    g_wmod, d_wmod, nm_wmod, nv_wmod, gc_part = _mod_bwd(c_rows, dmod_loc, w_mod[0], m_w_mod[0], v_w_mod[0])
    g_gc = _all_gather_direct(gc_part, "gather_c_ctx")

    given = {"c_ctx": (c_ctx, m_c_ctx, v_c_ctx), "b_mod": (b_mod, m_b_mod, v_b_mod), "norm_w": (norm_w, m_norm_w, v_norm_w),
             "q_norm_w": (q_norm_w, m_q_norm_w, v_q_norm_w), "k_norm_w": (k_norm_w, m_k_norm_w, v_k_norm_w),
             "conv_b": (conv_b, m_conv_b, v_conv_b), "conv_ln_w": (conv_ln_w, m_conv_ln_w, v_conv_ln_w),
             "conv_ln_b": (conv_ln_b, m_conv_ln_b, v_conv_ln_b), "b_pw": (b_pw, m_b_pw, v_b_pw)}
    as_rows = [[given[name][which].reshape(1, -1) for name, _ in _SMALL] for which in range(3)]
    small_outs = _adamw_small(summed, g_bmod, g_gc, *as_rows)

    big = {"w_mod": (g_wmod, d_wmod, nm_wmod, nv_wmod), "w_in": r_in, "conv_w": r_cw, "w_pw": r_pw, "w_out": r_out}
    order = ["c_ctx", "w_mod", "b_mod", "norm_w", "w_in", "q_norm_w", "k_norm_w", "conv_w", "conv_b", "conv_ln_w",
             "conv_ln_b", "w_pw", "b_pw", "w_out"]
    small_index = {name: k for k, (name, _) in enumerate(_SMALL)}
    outs = [loss11.reshape(()), grad_x]
    for which in range(4):
        for name in order:
            if name in big:
                outs.append(big[name][which][None])
            else:
                outs.append(small_outs[which][small_index[name]].reshape(given[name][0].shape))
    return tuple(outs)
```

```python
import functools

import jax
import jax.numpy as jnp
from jax import lax
from jax.experimental import pallas as pl
from jax.experimental.pallas import tpu as pltpu

F32, BF16 = jnp.float32, jnp.bfloat16
MESH_ID = pl.DeviceIdType.MESH

N_DEV = 8
D = 1024
D_IN = 2816
DA = 512
DC = 512
HD = 64
KVW = 128
KW = 31
HALO = 16
EPS = 1e-6
ROPE_THETA = 10000.0
GRID_W = 64

ADAM_LR, ADAM_B1, ADAM_B2, ADAM_EPS, ADAM_WD, ADAM_STEP = 0.001, 0.9, 0.999, 1e-08, 0.01, 10

VMEM_LIMIT = 56 * 1024 * 1024

TM = 256
TQ = 128
BWD_PARTS = 2
FWD_PARTS = 4
TC = 512
CH = 64


def _params(sem, vmem=VMEM_LIMIT):
    return pltpu.CompilerParams(dimension_semantics=sem, vmem_limit_bytes=vmem)


def _dot(a, b):
    return jnp.dot(a, b, preferred_element_type=F32)


def _dot_nt(a, b):
    return lax.dot_general(a, b, (((1,), (1,)), ((), ())), preferred_element_type=F32)


def _dot_tn(a, b):
    return lax.dot_general(a, b, (((0,), (0,)), ((), ())), preferred_element_type=F32)


def _sigmoid(z):
    return 1.0 / (1.0 + jnp.exp(-z))


def _segsum(v, ones_bd):
    return _dot(v.astype(BF16), ones_bd)


def _swap16(x):
    w = x.shape[-1]
    lane = lax.broadcasted_iota(jnp.int32, x.shape, 1)
    return jnp.where((lane % 32) < 16, pltpu.roll(x, w - 16, 1), pltpu.roll(x, 16, 1))


def _with_ones_column(v):
    one = (lax.broadcasted_iota(jnp.int32, v.shape, 1) == 0).astype(v.dtype)
    return jnp.concatenate([v, one], axis=-1)


def _rope(x, cos, sins):
    return x * cos + _swap16(x) * sins


def _rope_bwd(d, cos, sins):
    return d * cos + _swap16(d * sins)


def _adamw(w, g, m, v):
    m2 = ADAM_B1 * m + (1.0 - ADAM_B1) * g
    v2 = ADAM_B2 * v + (1.0 - ADAM_B2) * (g * g)
    m_hat = m2 / (1.0 - ADAM_B1 ** ADAM_STEP)
    v_hat = v2 / (1.0 - ADAM_B2 ** ADAM_STEP)
    delta = -ADAM_LR * (m_hat / (jnp.sqrt(v_hat) + ADAM_EPS) + ADAM_WD * w)
    return delta, m2, v2


def _coords():
    return lax.axis_index("x"), lax.axis_index("y"), lax.axis_index("c")


def _lin(x, y, c):
    return 4 * x + 2 * y + c


def _all_gather_many(arrs, name):
    n = len(arrs)

    def body(*refs):
        in_refs, out_refs = refs[:n], refs[n:2 * n]
        send_sems, recv_sems, local_sems = refs[2 * n:]
        x, y, c = _coords()
        me, sib = (x, y, c), (x, y, 1 - c)
        chips = [(1 - x, y), (x, 1 - y), (1 - x, 1 - y)]

        def copy(a, k, block, to, src=None):
            slot = out_refs[a].at[_lin(*block)]
            return pltpu.make_async_remote_copy(
                src_ref=slot if src is None else src, dst_ref=slot,
                send_sem=send_sems.at[a * 7 + k], recv_sem=recv_sems.at[a * 7 + k],
                device_id=to, device_id_type=MESH_ID)

        mine = [pltpu.make_async_copy(in_refs[a], out_refs[a].at[_lin(*me)], local_sems.at[a]) for a in range(n)]
        for cp in mine:
            cp.start()
        first = []
        for a in range(n):
            first.append(copy(a, 0, me, sib, src=in_refs[a]))
            first += [copy(a, 1 + j, me, (*chip, c), src=in_refs[a]) for j, chip in enumerate(chips)]
        for cp in first:
            cp.start()
        passed = []
        for a in range(n):
            for j, chip in enumerate(chips):
                copy(a, 1 + j, (*chip, c), me).wait_recv()
                fwd = copy(a, 4 + j, (*chip, c), sib)
                fwd.start()
                passed.append(fwd)
        for a in range(n):
            copy(a, 0, sib, me).wait_recv()
            for j, chip in enumerate(chips):
                copy(a, 4 + j, (*chip, 1 - c), me).wait_recv()
        for cp in first + passed:
            cp.wait_send()
        for cp in mine:
            cp.wait()

    vm = pl.BlockSpec(memory_space=pltpu.VMEM)
    return pl.pallas_call(
        body, name=name,
        out_shape=[jax.ShapeDtypeStruct((N_DEV,) + a.shape, a.dtype) for a in arrs],
        in_specs=[vm] * n, out_specs=[vm] * n,
        scratch_shapes=[pltpu.SemaphoreType.DMA((7 * n,)), pltpu.SemaphoreType.DMA((7 * n,)),
                        pltpu.SemaphoreType.DMA((n,))],
        compiler_params=pltpu.CompilerParams(vmem_limit_bytes=VMEM_LIMIT),
    )(*arrs)


def _exchange_copies(in_refs, out_refs, send_sems, recv_sems, local_sems, scatter):
    x, y, c = _coords()
    me = _lin(x, y, c)
    local, remote = [], []
    for a, (src, dst) in enumerate(zip(in_refs, out_refs)):
        local.append(pltpu.make_async_copy(src.at[me] if scatter else src, dst.at[me], local_sems.at[a]))
        for k in range(1, N_DEV):
            peer = (1 - x if k & 4 else x, 1 - y if k & 2 else y, 1 - c if k & 1 else c)
            remote.append(pltpu.make_async_remote_copy(
                src_ref=src.at[_lin(*peer)] if scatter else src, dst_ref=dst.at[me],
                send_sem=send_sems.at[a * (N_DEV - 1) + k - 1], recv_sem=recv_sems.at[a * (N_DEV - 1) + k - 1],
                device_id=peer, device_id_type=MESH_ID))
    return local, remote


def _exchange_scratch(n):
    return [pltpu.SemaphoreType.DMA((n * (N_DEV - 1),)), pltpu.SemaphoreType.DMA((n * (N_DEV - 1),)),
            pltpu.SemaphoreType.DMA((n,))]


def _all_gather_direct(arr, name):
    def body(in_ref, out_ref, send_sems, recv_sems, local_sem):
        x, y, c = _coords()
        me = _lin(x, y, c)
        mine = pltpu.make_async_copy(in_ref, out_ref.at[me], local_sem)
        mine.start()
        copies = []
        for k in range(1, N_DEV):
            peer = (1 - x if k & 4 else x, 1 - y if k & 2 else y, 1 - c if k & 1 else c)
            cp = pltpu.make_async_remote_copy(
                src_ref=in_ref, dst_ref=out_ref.at[me], send_sem=send_sems.at[k - 1], recv_sem=recv_sems.at[k - 1],
                device_id=peer, device_id_type=MESH_ID)
            cp.start()
            copies.append(cp)
        for cp in copies:
            cp.wait_recv()
        for cp in copies:
            cp.wait_send()
        mine.wait()

    vm = pl.BlockSpec(memory_space=pltpu.VMEM)
    return pl.pallas_call(
        body, name=name, out_shape=jax.ShapeDtypeStruct((N_DEV,) + arr.shape, arr.dtype),
        in_specs=[vm], out_specs=vm,
        scratch_shapes=[pltpu.SemaphoreType.DMA((N_DEV - 1,)), pltpu.SemaphoreType.DMA((N_DEV - 1,)),
                        pltpu.SemaphoreType.DMA],
    )(arr)


def _reduce_scatter_adamw(items, name):
    n = len(items)
    rb = 32

    def body(*refs):
        parts = refs[0:n]
        wmv = refs[n:4 * n]
        outs = refs[4 * n:8 * n]
        bufs = [refs[8 * n + 4 * a:8 * n + 4 * a + 4] for a in range(n)]
        d2d_send, d2d_recv, ici_send, ici_recv, local_sems = refs[12 * n:]
        x, y, c = _coords()
        sib = (x, y, 1 - c)
        peers = [(1 - x, y), (x, 1 - y), (1 - x, 1 - y)]
        home = 2 * x + y

        def rows_loop(rows, fn):
            def step(i, carry):
                fn(pl.ds(pl.multiple_of(i * rb, rb), rb))
                return carry
            lax.fori_loop(0, rows // rb, step, 0)

        local, d2d, ici = [], [], []
        for a in range(n):
            mine, got_sib = bufs[a][0], bufs[a][1]
            for s in range(4):
                cp = pltpu.make_async_copy(parts[a].at[_lin(s // 2, s % 2, c)], mine.at[s], local_sems.at[4 * a + s])
                cp.start()
                local.append(cp)
                rc = pltpu.make_async_remote_copy(
                    src_ref=parts[a].at[_lin(s // 2, s % 2, 1 - c)], dst_ref=got_sib.at[s],
                    send_sem=d2d_send.at[4 * a + s], recv_sem=d2d_recv.at[4 * a + s],
                    device_id=sib, device_id_type=MESH_ID)
                rc.start()
                d2d.append(rc)

        for a in range(n):
            mine, got_sib, stage, got_chip = bufs[a]
            for s in range(4):
                local[4 * a + s].wait()
                d2d[4 * a + s].wait_recv()
            for k, (px, py) in enumerate(peers):
                slot = 2 * px + py

                def pair_sum(rs, k=k, slot=slot, mine=mine, got_sib=got_sib, stage=stage):
                    stage[k, rs, :] = (mine[slot, rs, :].astype(F32) + got_sib[slot, rs, :].astype(F32)).astype(BF16)

                rows_loop(wmv[3 * a].shape[0], pair_sum)
                rc = pltpu.make_async_remote_copy(
                    src_ref=stage.at[k], dst_ref=got_chip.at[k],
                    send_sem=ici_send.at[3 * a + k], recv_sem=ici_recv.at[3 * a + k],
                    device_id=(px, py, c), device_id_type=MESH_ID)
                rc.start()
                ici.append(rc)

        for a in range(n):
            mine, got_sib, stage, got_chip = bufs[a]
            w_ref, m_ref, v_ref = wmv[3 * a:3 * a + 3]
            g_ref, d_ref, nm_ref, nv_ref = outs[4 * a:4 * a + 4]
            for k in range(3):
                ici[3 * a + k].wait_recv()

            def finish(rs, mine=mine, got_sib=got_sib, got_chip=got_chip, w_ref=w_ref, m_ref=m_ref, v_ref=v_ref,
                       g_ref=g_ref, d_ref=d_ref, nm_ref=nm_ref, nv_ref=nv_ref):
                g = mine[home, rs, :].astype(F32) + got_sib[home, rs, :].astype(F32)
                for k in range(3):
                    g = g + got_chip[k, rs, :].astype(F32)
                delta, m2, v2 = _adamw(w_ref[rs, :], g, m_ref[rs, :], v_ref[rs, :])
                g_ref[rs, :] = g
                d_ref[rs, :] = delta
                nm_ref[rs, :] = m2
                nv_ref[rs, :] = v2

            rows_loop(w_ref.shape[0], finish)

        for rc in d2d + ici:
            rc.wait_send()

    vm = pl.BlockSpec(memory_space=pltpu.VMEM)
    anyspace = pl.BlockSpec(memory_space=pl.ANY)
    args, in_specs, out_shape, scratch = [], [], [], []
    for parts, w, m, v in items:
        assert w.shape[0] % rb == 0 and parts.shape == (N_DEV,) + w.shape and parts.dtype == BF16
    args += [it[0] for it in items]
    in_specs += [anyspace] * n
    for _, w, m, v in items:
        args += [w, m, v]
        in_specs += [vm] * 3
        out_shape += [jax.ShapeDtypeStruct(w.shape, F32)] * 4
    for it in items:
        shp = it[1].shape
        scratch += [pltpu.VMEM((4,) + shp, BF16), pltpu.VMEM((4,) + shp, BF16),
                    pltpu.VMEM((3,) + shp, BF16), pltpu.VMEM((3,) + shp, BF16)]
    scratch += [pltpu.SemaphoreType.DMA((4 * n,)), pltpu.SemaphoreType.DMA((4 * n,)),
                pltpu.SemaphoreType.DMA((3 * n,)), pltpu.SemaphoreType.DMA((3 * n,)), pltpu.SemaphoreType.DMA((4 * n,))]
    outs = pl.pallas_call(
        body, name=name, out_shape=out_shape, in_specs=in_specs, out_specs=[vm] * (4 * n),
        scratch_shapes=scratch, compiler_params=pltpu.CompilerParams(vmem_limit_bytes=VMEM_LIMIT),
    )(*args)
    return [tuple(outs[4 * a:4 * a + 4]) for a in range(n)]


def _sum_devices_adamw(items):
    n = len(items)

    def body(*refs):
        for a in range(n):
            got, w_ref, m_ref, v_ref = refs[4 * a:4 * a + 4]
            g_ref, d_ref, nm_ref, nv_ref = refs[4 * n + 4 * a:4 * n + 4 * a + 4]
            g = got[0].astype(F32)
            for j in range(1, N_DEV):
                g = g + got[j].astype(F32)
            delta, m2, v2 = _adamw(w_ref[...], g, m_ref[...], v_ref[...])
            g_ref[...] = g
            d_ref[...] = delta
            nm_ref[...] = m2
            nv_ref[...] = v2

    args, out_shape = [], []
    for got, w, m, v in items:
        assert got.shape == (N_DEV,) + w.shape
        args += [got, w, m, v]
        out_shape += [jax.ShapeDtypeStruct(w.shape, F32)] * 4
    outs = pl.pallas_call(body, name="sum_devices_adamw", out_shape=out_shape,
                          compiler_params=pltpu.CompilerParams(vmem_limit_bytes=VMEM_LIMIT))(*args)
    return [tuple(outs[4 * a:4 * a + 4]) for a in range(n)]


def _mod_fwd(c_rows, w_mod_loc, b_mod_loc):
    def body(c_ref, w_ref, b_ref, o_ref):
        cr = c_ref[...]
        a = (cr * _sigmoid(cr)).astype(BF16)
        o_ref[...] = _dot(a, w_ref[...].astype(BF16)) + b_ref[...]

    return pl.pallas_call(
        body, name="mod_fwd", out_shape=jax.ShapeDtypeStruct((c_rows.shape[0], w_mod_loc.shape[1]), F32),
        compiler_params=pltpu.CompilerParams(vmem_limit_bytes=VMEM_LIMIT),
    )(c_rows, w_mod_loc, b_mod_loc)


def _fwd_in(x, modrows, norm_w, w_in_b, cos, sins, qnw_t, knw_t, ones_bd, k_all, v_all, shards):
    bl, s, _ = x.shape
    nt = s // TM
    ctx_tiles = (k_all.shape[2] - s) // TM
    assert ctx_tiles * TM + s == k_all.shape[2]
    n_sh = len(shards)

    def body(*refs):
        (x_ref, mod_ref, nw_ref, win_ref, cos_ref, sin_ref, qnw_ref, knw_ref, bd_ref, kin_ref, vin_ref) = refs[:11]
        shard_refs = refs[11:11 + n_sh]
        q_ref, k_ref, v_ref, pq_ref, pkv_ref, za_ref, glu_ref, zc_ref = refs[11 + n_sh:19 + n_sh]
        gathered_refs = refs[19 + n_sh:19 + 2 * n_sh]
        send_sems, recv_sems, local_sems = refs[19 + 2 * n_sh:]
        b, i = pl.program_id(0), pl.program_id(1)
        local, remote = _exchange_copies(shard_refs, gathered_refs, send_sems, recv_sems, local_sems, scatter=False)

        @pl.when((b == 0) & (i == 0))
        def _():
            for cp in local + remote:
                cp.start()

        xv = x_ref[0]
        shift = mod_ref[0, 0:1, :]
        scale = mod_ref[0, 1:2, :]
        r = lax.rsqrt(jnp.mean(xv * xv, axis=-1, keepdims=True) + EPS)
        u = (xv * r * nw_ref[...]) * (1.0 + scale) + shift
        p = _dot_nt(u.astype(BF16), win_ref[...])
        pq = p[:, 0:DA]
        pk = p[:, DA:DA + HD * 2]
        ck = cos_ref[...]
        sk = sin_ref[...]
        cs = jnp.concatenate([ck] * (DA // KVW), axis=-1)
        sn = jnp.concatenate([sk] * (DA // KVW), axis=-1)
        rq = lax.rsqrt(_segsum(pq * pq, bd_ref[...]) * (1.0 / HD) + EPS)
        qn = pq * rq * qnw_ref[...]
        qr = _rope(qn, cs, sn) * 0.125
        for h in range(DA // HD):
            q_ref[0, h] = qr[:, h * HD:(h + 1) * HD].astype(BF16)
        rk = lax.rsqrt(_segsum(pk * pk, bd_ref[0:KVW, 0:KVW]) * (1.0 / HD) + EPS)
        kn = pk * rk * knw_ref[...]
        kr = _rope(kn, ck, sk)
        pv = p[:, 640:768]
        for h in range(KVW // HD):
            k_ref[0, h] = kr[:, h * HD:(h + 1) * HD].astype(BF16)
            v_ref[0, h] = _with_ones_column(pv[:, h * HD:(h + 1) * HD]).astype(BF16)
        pq_ref[0] = pq
        pkv_ref[0] = p[:, 512:768]
        za_ref[0] = p[:, 768:1280]
        glu_ref[0] = p[:, 1280:2304]
        zc_ref[0] = p[:, 2304:2816]

        @pl.when((b == bl - 1) & (i == nt - 1))
        def _():
            for cp in remote:
                cp.wait_recv()
            for cp in remote:
                cp.wait_send()
            for cp in local:
                cp.wait()

    def tile(w):
        return pl.BlockSpec((1, TM, w), lambda b, i: (b, i, 0))

    def const(shape):
        return pl.BlockSpec(shape, lambda b, i: (0,) * len(shape))

    outs = [(DA, F32), (2 * KVW, F32), (DA, F32), (2 * DC, F32), (DC, F32)]
    anyspace = pl.BlockSpec(memory_space=pl.ANY)
    rope = pl.BlockSpec((TM, KVW), lambda b, i: (i, 0))
    k_tile = pl.BlockSpec((1, KVW // HD, TM, HD), lambda b, i: (b, 0, ctx_tiles + i, 0))
    v_tile = pl.BlockSpec((1, KVW // HD, TM, 2 * HD), lambda b, i: (b, 0, ctx_tiles + i, 0))
    res = pl.pallas_call(
        body, name="fwd_in", grid=(bl, nt),
        in_specs=[tile(D), pl.BlockSpec((1, 3, D), lambda b, i: (b, 0, 0)), const((1, D)), const((D_IN, D)),
                  rope, rope, const((1, DA)), const((1, KVW)), const((DA, DA)), anyspace, anyspace]
        + [anyspace] * n_sh,
        out_specs=[pl.BlockSpec((1, DA // HD, TM, HD), lambda b, i: (b, 0, i, 0)), k_tile, v_tile]
        + [tile(w) for w, _ in outs] + [anyspace] * n_sh,
        out_shape=[jax.ShapeDtypeStruct((bl, DA // HD, s, HD), BF16), jax.ShapeDtypeStruct(k_all.shape, BF16),
                   jax.ShapeDtypeStruct(v_all.shape, BF16)]
        + [jax.ShapeDtypeStruct((bl, s, w), dt) for w, dt in outs]
        + [jax.ShapeDtypeStruct((N_DEV,) + a.shape, a.dtype) for a in shards],
        input_output_aliases={9: 1, 10: 2},
        scratch_shapes=_exchange_scratch(n_sh),
        compiler_params=_params(("arbitrary", "arbitrary")),
    )(x, modrows, norm_w, w_in_b, cos, sins, qnw_t, knw_t, ones_bd, k_all, v_all, *shards)
    return res[:8], res[8:]


_KV_ROWS_OF_W_IN_T = pl.BlockSpec((2 * KVW, D), lambda b: (DA // (2 * KVW), 0))


def _ctx_fwd(ctx, modc, norm_w, w_kv_b, knw_t, ones_bd, n_keys):
    bl, cl, _ = ctx.shape

    def body(x_ref, mod_ref, nw_ref, w_ref, knw_ref, bd_ref, k_ref, v_ref, pkv_ref):
        xv = x_ref[0]
        shift = mod_ref[0, 0:1, :]
        scale = mod_ref[0, 1:2, :]
        r = lax.rsqrt(jnp.mean(xv * xv, axis=-1, keepdims=True) + EPS)
        u = (xv * r * nw_ref[...]) * (1.0 + scale) + shift
        p = _dot_nt(u.astype(BF16), w_ref[...])
        pk = p[:, 0:KVW]
        rk = lax.rsqrt(_segsum(pk * pk, bd_ref[...]) * (1.0 / HD) + EPS)
        kn = pk * rk * knw_ref[...]
        pv = p[:, KVW:2 * KVW]
        for h in range(KVW // HD):
            k_ref[0, h] = kn[:, h * HD:(h + 1) * HD].astype(BF16)
            v_ref[0, h] = _with_ones_column(pv[:, h * HD:(h + 1) * HD]).astype(BF16)
        pkv_ref[0] = p

    def const(shape):
        return pl.BlockSpec(shape, lambda b: (0,) * len(shape))

    def tile(w):
        return pl.BlockSpec((1, cl, w), lambda b: (b, 0, 0))

    k_tile = pl.BlockSpec((1, KVW // HD, cl, HD), lambda b: (b, 0, 0, 0))
    v_tile = pl.BlockSpec((1, KVW // HD, cl, 2 * HD), lambda b: (b, 0, 0, 0))
    return pl.pallas_call(
        body, name="ctx_fwd", grid=(bl,),
        in_specs=[tile(D), const((1, 3, D)), const((1, D)), _KV_ROWS_OF_W_IN_T, const((1, KVW)), const((KVW, KVW))],
        out_specs=[k_tile, v_tile, tile(2 * KVW)],
        out_shape=[jax.ShapeDtypeStruct((bl, KVW // HD, n_keys, HD), BF16),
                   jax.ShapeDtypeStruct((bl, KVW // HD, n_keys, 2 * HD), BF16),
                   jax.ShapeDtypeStruct((bl, cl, 2 * KVW), F32)],
        compiler_params=_params(("arbitrary",)),
    )(ctx, modc, norm_w, w_kv_b, knw_t, ones_bd)


def _attn_fwd(q, k, v1):
    bl, _, s, _ = q.shape
    n_keys = k.shape[2]

    def body(q_ref, k_ref, v_ref, o_ref, lse_ref):
        kv = k_ref[0, 0]
        vv = v_ref[0, 0]
        lane = lax.broadcasted_iota(jnp.int32, (TQ, 2 * HD), 1)
        for part in range(FWD_PARTS):
            rows = pl.ds(part * TQ, TQ)
            lse = jnp.zeros((TQ, 2 * HD), F32)
            heads = []
            sc_all = _dot_nt(q_ref[0, :, rows, :].reshape(4 * TQ, HD), kv)
            for h in range(4):
                sc = sc_all[h * TQ:(h + 1) * TQ, :]
                m = jnp.max(sc, axis=-1, keepdims=True)
                e = jnp.exp(sc - m).astype(BF16)
                ov = _dot(e, vv)
                denom = ov[:, HD:HD + 1]
                heads.append(ov[:, 0:HD] * (1.0 / denom))
                lse = jnp.where(lane == h, m + jnp.log(denom), lse)
            o_ref[0, rows, :] = jnp.concatenate(heads, axis=-1)
            lse_ref[0, 0, rows, :] = lse

    tq = FWD_PARTS * TQ
    ks = pl.BlockSpec((1, 1, n_keys, HD), lambda b, g, i: (b, g, 0, 0))
    qs = pl.BlockSpec((1, 4, tq, HD), lambda b, g, i: (b, g, i, 0))
    vs = pl.BlockSpec((1, 1, n_keys, 2 * HD), lambda b, g, i: (b, g, 0, 0))
    return pl.pallas_call(
        body, name="attn_fwd", grid=(bl, 2, s // tq), in_specs=[qs, ks, vs],
        out_specs=[pl.BlockSpec((1, tq, 4 * HD), lambda b, g, i: (b, i, g)),
                   pl.BlockSpec((1, 1, tq, 2 * HD), lambda b, g, i: (b, g, i, 0))],
        out_shape=[jax.ShapeDtypeStruct((bl, s, DA), F32), jax.ShapeDtypeStruct((bl, 2, s, 2 * HD), F32)],
        compiler_params=_params(("arbitrary", "arbitrary", "arbitrary")),
    )(q, k, v1)


def _attn_bwd(q, k, v1, do, o, lse, exchange):
    bl, _, s, _ = q.shape
    n_keys = k.shape[2]
    tq = BWD_PARTS * TQ
    nq = s // tq
    n_ex = len(exchange)

    def body(*refs):
        q_ref, k_ref, v_ref, do_ref, o_ref, lse_ref = refs[:6]
        part_refs = refs[6:6 + n_ex]
        dq_ref, dk_ref, dv_ref = refs[6 + n_ex:9 + n_ex]
        got_refs = refs[9 + n_ex:9 + 2 * n_ex]
        p_sc, ds_sc, dkt, dvt, send_sems, recv_sems, local_sems = refs[9 + 2 * n_ex:]
        i = pl.program_id(2)
        first = (pl.program_id(0) == 0) & (pl.program_id(1) == 0) & (i == 0)
        last = (pl.program_id(0) == bl - 1) & (pl.program_id(1) == 1) & (i == nq - 1)
        local, remote = _exchange_copies(part_refs, got_refs, send_sems, recv_sems, local_sems, scatter=True)

        @pl.when(first)
        def _():
            for cp in local + remote:
                cp.start()

        @pl.when(i == 0)
        def _():
            dkt[...] = jnp.zeros_like(dkt)
            dvt[...] = jnp.zeros_like(dvt)

        kv = k_ref[0, 0]
        vv = v_ref[0, 0][:, 0:HD]
        for part in range(BWD_PARTS):
            tq_rows = pl.ds(part * TQ, TQ)
            lse = lse_ref[0, 0, tq_rows, :]
            ov = o_ref[0, tq_rows, :]
            dqs = []
            q_cat = q_ref[0, :, tq_rows, :].reshape(4 * TQ, HD)
            do_cat = do_ref[0, :, tq_rows, :].reshape(4 * TQ, HD)
            sc_all = _dot_nt(q_cat, kv)
            for h in range(4):
                doh = do_cat[h * TQ:(h + 1) * TQ, :]
                delta = jnp.sum(ov[:, h * HD:(h + 1) * HD] * doh.astype(F32), axis=-1, keepdims=True)
                rows = pl.ds((part * 4 + h) * TQ, TQ)
                p = jnp.exp(sc_all[h * TQ:(h + 1) * TQ, :] - lse[:, h:h + 1])
                ds = (p * (_dot_nt(doh, vv) - delta)).astype(BF16)
                p_sc[rows, :] = p.astype(BF16)
                ds_sc[rows, :] = ds
                dqs.append(_dot(ds, kv) * 0.125)
            dq_ref[0, tq_rows, :] = jnp.concatenate(dqs, axis=-1)
            part_rows = pl.ds(part * 4 * TQ, 4 * TQ)
            dvt[...] += _dot_tn(do_cat, p_sc[part_rows, :])
            dkt[...] += _dot_tn(q_cat, ds_sc[part_rows, :])

        @pl.when(i == nq - 1)
        def _():
            dk_ref[0, 0] = dkt[...].T
            dv_ref[0, 0] = dvt[...].T

        @pl.when(last)
        def _():
            for cp in remote:
                cp.wait_recv()
            for cp in remote:
                cp.wait_send()
            for cp in local:
                cp.wait()

    qs = pl.BlockSpec((1, 4, tq, HD), lambda b, g, i: (b, g, i, 0))
    ks = pl.BlockSpec((1, 1, n_keys, HD), lambda b, g, i: (b, g, 0, 0))
    vs = pl.BlockSpec((1, 1, n_keys, 2 * HD), lambda b, g, i: (b, g, 0, 0))
    os_ = pl.BlockSpec((1, tq, 4 * HD), lambda b, g, i: (b, i, g))
    kshape = jax.ShapeDtypeStruct(k.shape, F32)
    anyspace = pl.BlockSpec(memory_space=pl.ANY)
    res = pl.pallas_call(
        body, name="attn_bwd", grid=(bl, 2, nq),
        in_specs=[qs, ks, vs, qs, os_, pl.BlockSpec((1, 1, tq, 2 * HD), lambda b, g, i: (b, g, i, 0))]
        + [anyspace] * n_ex,
        out_specs=[os_, ks, ks] + [anyspace] * n_ex,
        out_shape=[jax.ShapeDtypeStruct((bl, s, DA), F32), kshape, kshape]
        + [jax.ShapeDtypeStruct(a.shape, a.dtype) for a in exchange],
        scratch_shapes=[pltpu.VMEM((4 * tq, n_keys), BF16), pltpu.VMEM((4 * tq, n_keys), BF16),
                        pltpu.VMEM((HD, n_keys), F32), pltpu.VMEM((HD, n_keys), F32)] + _exchange_scratch(n_ex),
        compiler_params=_params(("arbitrary", "arbitrary", "arbitrary")),
    )(q, k, v1, do, o, lse, *exchange)
    return res[:3], res[3:]


def _halo_specs(width, s):
    per = TC // HALO
    last = s // HALO - 1
    main = pl.BlockSpec((1, TC, width), lambda b, i: (b, i, 0))
    prev = pl.BlockSpec((1, HALO, width), lambda b, i: (b, jnp.maximum(i * per - 1, 0), 0))
    nxt = pl.BlockSpec((1, HALO, width), lambda b, i: (b, jnp.minimum((i + 1) * per, last), 0))
    return main, prev, nxt


def _glu(g):
    return g[:, 0:DC] * _sigmoid(g[:, DC:2 * DC])


def _fill_padded(pad_ref, main, prev, nxt, first, last):
    pad_ref[0:HALO, :] = jnp.where(first, 0.0, prev)
    pad_ref[HALO:HALO + TC, :] = main
    pad_ref[HALO + TC:2 * HALO + TC, :] = jnp.where(last, 0.0, nxt)


PLANE_ROWS = TC + 2 * HALO - 8


def _shift_planes(pad_ref, planes_ref):
    for r in range(1, 8):
        planes_ref[r - 1] = pad_ref[pl.ds(r, PLANE_ROWS), :]


def _tap_rows(pad_ref, planes_ref, offset, start, n):
    a, r = divmod(offset, 8)
    if r == 0:
        return pad_ref[pl.ds(start + 8 * a, n), :]
    return planes_ref[r - 1, pl.ds(start + 8 * a, n), :]


def _conv_fwd(glu, conv_w, conv_b, ln_w, ln_b, w_pw_b, b_pw):
    bl, s, _ = glu.shape
    nt = s // TC

    def body(g_ref, gp_ref, gn_ref, cw_ref, cb_ref, lw_ref, lb_ref, wpw_ref, bpw_ref, y_ref, cp_ref, pad_ref, planes_ref):
        i = pl.program_id(1)
        _fill_padded(pad_ref, _glu(g_ref[0]), _glu(gp_ref[0]), _glu(gn_ref[0]), i == 0, i == nt - 1)
        _shift_planes(pad_ref, planes_ref)
        for ck in range(TC // CH):
            acc = jnp.zeros((CH, DC), F32) + cb_ref[...]
            for t in range(KW):
                acc = acc + _tap_rows(pad_ref, planes_ref, 1 + t, ck * CH, CH) * cw_ref[t:t + 1, :]
            y_ref[0, pl.ds(ck * CH, CH), :] = acc
        y = y_ref[0]
        mu = jnp.mean(y, axis=-1, keepdims=True)
        yc = y - mu
        var = jnp.mean(yc * yc, axis=-1, keepdims=True)
        z = yc * lax.rsqrt(var + EPS) * lw_ref[...] + lb_ref[...]
        act = z * _sigmoid(z)
        cp_ref[0] = _dot(act.astype(BF16), wpw_ref[...]) + bpw_ref[...]

    def const(shape):
        return pl.BlockSpec(shape, lambda b, i: (0,) * len(shape))

    main, prev, nxt = _halo_specs(2 * DC, s)
    tile = pl.BlockSpec((1, TC, DC), lambda b, i: (b, i, 0))
    return pl.pallas_call(
        body, name="conv_fwd", grid=(bl, nt),
        in_specs=[main, prev, nxt, const((32, DC)), const((1, DC)), const((1, DC)), const((1, DC)),
                  const((DC, DC)), const((1, DC))],
        out_specs=[tile, tile],
        out_shape=[jax.ShapeDtypeStruct((bl, s, DC), F32)] * 2,
        scratch_shapes=[pltpu.VMEM((TC + 2 * HALO, DC), F32), pltpu.VMEM((7, PLANE_ROWS, DC), F32)],
        compiler_params=_params(("arbitrary", "arbitrary")),
    )(glu, glu, glu, conv_w, conv_b, ln_w, ln_b, w_pw_b, b_pw)


def _conv_bwd_pointwise(y, dcp, ln_w, ln_b, w_pw_b):
    bl, s, _ = y.shape
    nt = s // TM

    def body(y_ref, dcp_ref, lw_ref, lb_ref, wpw_ref, dy_ref, gw_ref, rows_ref):
        @pl.when((pl.program_id(0) == 0) & (pl.program_id(1) == 0))
        def _():
            gw_ref[...] = jnp.zeros_like(gw_ref)
            rows_ref[...] = jnp.zeros_like(rows_ref)

        y = y_ref[0]
        dcp = dcp_ref[0]
        mu = jnp.mean(y, axis=-1, keepdims=True)
        yc = y - mu
        rstd = lax.rsqrt(jnp.mean(yc * yc, axis=-1, keepdims=True) + EPS)
        yn = yc * rstd
        lw = lw_ref[...]
        z = yn * lw + lb_ref[...]
        sg = _sigmoid(z)
        act = z * sg
        dcp_b = dcp.astype(BF16)
        gw_ref[...] += _dot_tn(act.astype(BF16), dcp_b)
        dact = _dot_nt(dcp_b, wpw_ref[...])
        dz = dact * (sg * (1.0 + z * (1.0 - sg)))
        dyn = dz * lw
        dy = rstd * (dyn - jnp.mean(dyn, axis=-1, keepdims=True) - yn * jnp.mean(dyn * yn, axis=-1, keepdims=True))
        dy_ref[0] = dy
        rows_ref[0:1, :] += jnp.sum(dcp, axis=0, keepdims=True)
        rows_ref[1:2, :] += jnp.sum(dz * yn, axis=0, keepdims=True)
        rows_ref[2:3, :] += jnp.sum(dz, axis=0, keepdims=True)
        rows_ref[3:4, :] += jnp.sum(dy, axis=0, keepdims=True)

    def const(shape):
        return pl.BlockSpec(shape, lambda b, i: (0,) * len(shape))

    tile = pl.BlockSpec((1, TM, DC), lambda b, i: (b, i, 0))
    return pl.pallas_call(
        body, name="conv_bwd_pointwise", grid=(bl, nt),
        in_specs=[tile, tile, const((1, DC)), const((1, DC)), const((DC, DC))],
        out_specs=[tile, const((DC, DC)), const((8, DC))],
        out_shape=[jax.ShapeDtypeStruct((bl, s, DC), F32), jax.ShapeDtypeStruct((DC, DC), F32),
                   jax.ShapeDtypeStruct((8, DC), F32)],
        compiler_params=_params(("arbitrary", "arbitrary")),
    )(y, dcp, ln_w, ln_b, w_pw_b)


def _conv_bwd_depthwise(glu, dy, conv_w):
    bl, s, _ = glu.shape
    nt = s // TC

    def body(g_ref, gp_ref, gn_ref, d_ref, dp_ref, dn_ref, cw_ref, dglu_ref, dcw_ref,
             padu_ref, padd_ref, planes_u, planes_d):
        i = pl.program_id(1)

        @pl.when((pl.program_id(0) == 0) & (i == 0))
        def _():
            dcw_ref[...] = jnp.zeros_like(dcw_ref)

        first, last = i == 0, i == nt - 1
        _fill_padded(padu_ref, _glu(g_ref[0]), _glu(gp_ref[0]), _glu(gn_ref[0]), first, last)
        _fill_padded(padd_ref, d_ref[0], dp_ref[0], dn_ref[0], first, last)
        _shift_planes(padu_ref, planes_u)
        _shift_planes(padd_ref, planes_d)
        for ck in range(TC // CH):
            acc = jnp.zeros((CH, DC), F32)
            for t in range(KW):
                acc = acc + _tap_rows(padd_ref, planes_d, 2 * HALO - 1 - t, ck * CH, CH) * cw_ref[t:t + 1, :]
            g = g_ref[0, pl.ds(ck * CH, CH), :]
            a = g[:, 0:DC]
            sg = _sigmoid(g[:, DC:2 * DC])
            dglu_ref[0, pl.ds(ck * CH, CH), 0:DC] = acc * sg
            dglu_ref[0, pl.ds(ck * CH, CH), DC:2 * DC] = acc * a * sg * (1.0 - sg)
        group = 4
        for t0 in range(0, KW, group):
            taps = range(t0, min(t0 + group, KW))
            acc8 = [jnp.zeros((8, DC), F32) for _ in taps]
            for ck in range(TC // CH):
                dchunk = d_ref[0, pl.ds(ck * CH, CH), :]
                for n, t in enumerate(taps):
                    prod = _tap_rows(padu_ref, planes_u, 1 + t, ck * CH, CH) * dchunk
                    acc8[n] = acc8[n] + jnp.sum(prod.reshape(CH // 8, 8, DC), axis=0)
            for n, t in enumerate(taps):
                dcw_ref[t:t + 1, :] += jnp.sum(acc8[n], axis=0, keepdims=True)

    gmain, gprev, gnext = _halo_specs(2 * DC, s)
    dmain, dprev, dnext = _halo_specs(DC, s)
    cw = pl.BlockSpec((32, DC), lambda b, i: (0, 0))
    return pl.pallas_call(
        body, name="conv_bwd_depthwise", grid=(bl, nt),
        in_specs=[gmain, gprev, gnext, dmain, dprev, dnext, cw],
        out_specs=[gmain, cw],
        out_shape=[jax.ShapeDtypeStruct((bl, s, 2 * DC), F32), jax.ShapeDtypeStruct((32, DC), F32)],
        scratch_shapes=[pltpu.VMEM((TC + 2 * HALO, DC), F32)] * 2 + [pltpu.VMEM((7, PLANE_ROWS, DC), F32)] * 2,
        compiler_params=_params(("arbitrary", "arbitrary")),
    )(glu, glu, glu, dy, dy, dy, conv_w)


def _out_fwd_bwd(attn, za, cp, zc, x, target, modrows, w_out_b):
    bl, s, _ = x.shape
    nt = s // TM

    def body(o_ref, za_ref, cp_ref, zc_ref, x_ref, t_ref, mod_ref, w_ref,
             do_ref, dza_ref, dcp_ref, dzc_ref, dh_ref, dgate_ref, gw_ref, loss_ref):
        b, i = pl.program_id(0), pl.program_id(1)

        @pl.when((b == 0) & (i == 0))
        def _():
            gw_ref[...] = jnp.zeros_like(gw_ref)
            loss_ref[...] = jnp.zeros_like(loss_ref)

        @pl.when(i == 0)
        def _():
            dgate_ref[...] = jnp.zeros_like(dgate_ref)

        o, za_v, cp_v, zc_v = o_ref[0], za_ref[0], cp_ref[0], zc_ref[0]
        gate = mod_ref[0, 2:3, :]
        sa = _sigmoid(za_v)
        sc = _sigmoid(zc_v)
        silu_a = za_v * sa
        silu_c = zc_v * sc
        mix = jnp.concatenate([(o * silu_a).astype(BF16), (cp_v * silu_c).astype(BF16)], axis=-1)
        w = w_ref[...]
        out = _dot(mix, w)
        err = x_ref[0] + gate * out - t_ref[0]
        loss_ref[...] += jnp.sum(err * err, axis=0, keepdims=True)
        dh = err * (1.0 / D)
        dh_ref[0] = dh
        dgate_ref[0] += jnp.sum(dh * out, axis=0, keepdims=True)
        dout = (dh * gate).astype(BF16)
        gw_ref[...] += _dot_tn(mix, dout)
        dmix = _dot_nt(dout, w)
        dga = dmix[:, 0:DA]
        dgc = dmix[:, DA:DA + DC]
        dov = dga * silu_a
        for h in range(DA // HD):
            do_ref[0, h] = dov[:, h * HD:(h + 1) * HD].astype(BF16)
        dza_ref[0] = dga * o * (sa * (1.0 + za_v * (1.0 - sa)))
        dcp_ref[0] = dgc * silu_c
        dzc_ref[0] = dgc * cp_v * (sc * (1.0 + zc_v * (1.0 - sc)))

    def const(shape):
        return pl.BlockSpec(shape, lambda b, i: (0,) * len(shape))

    def tile(w):
        return pl.BlockSpec((1, TM, w), lambda b, i: (b, i, 0))

    return pl.pallas_call(
        body, name="out_fwd_bwd", grid=(bl, nt),
        in_specs=[tile(DA), tile(DA), tile(DC), tile(DC), tile(D), tile(D),
                  pl.BlockSpec((1, 3, D), lambda b, i: (b, 0, 0)), const((D, D))],
        out_specs=[pl.BlockSpec((1, DA // HD, TM, HD), lambda b, i: (b, 0, i, 0)), tile(DA), tile(DC), tile(DC), tile(D),
                   pl.BlockSpec((1, 1, D), lambda b, i: (b, 0, 0)), const((D, D)), const((1, D))],
        out_shape=[jax.ShapeDtypeStruct((bl, DA // HD, s, HD), BF16), jax.ShapeDtypeStruct((bl, s, DA), F32),
                   jax.ShapeDtypeStruct((bl, s, DC), F32), jax.ShapeDtypeStruct((bl, s, DC), F32),
                   jax.ShapeDtypeStruct((bl, s, D), F32), jax.ShapeDtypeStruct((bl, 1, D), F32),
                   jax.ShapeDtypeStruct((D, D), F32), jax.ShapeDtypeStruct((1, D), F32)],
        compiler_params=_params(("arbitrary", "arbitrary")),
    )(attn, za, cp, zc, x, target, modrows, w_out_b)


def _rms_heads_bwd(dy, x, w_t, ones_bd):
    r = lax.rsqrt(_segsum(x * x, ones_bd) * (1.0 / HD) + EPS)
    xh = x * r
    g = dy * w_t
    dx = r * (g - xh * (_segsum(g * xh, ones_bd) * (1.0 / HD)))
    return dx, dy * xh


def _ctx_bwd(ctx, modc, norm_w, w_kv_b, pkv_c, dk_c, dv_c, knw_t, ones_bd):
    bl, cl, _ = ctx.shape

    def body(x_ref, mod_ref, nw_ref, w_ref, p_ref, dk_ref, dv_ref, knw_ref, bd_ref, gw_ref, rows_ref, dknw_ref):
        @pl.when(pl.program_id(0) == 0)
        def _():
            gw_ref[...] = jnp.zeros_like(gw_ref)
            rows_ref[...] = jnp.zeros_like(rows_ref)
            dknw_ref[...] = jnp.zeros_like(dknw_ref)

        xv = x_ref[0]
        shift = mod_ref[0, 0:1, :]
        scale = mod_ref[0, 1:2, :]
        nw = nw_ref[...]
        r = lax.rsqrt(jnp.mean(xv * xv, axis=-1, keepdims=True) + EPS)
        xn = xv * r
        yv = xn * nw
        u = yv * (1.0 + scale) + shift
        dkv = jnp.concatenate([dk_ref[0, 0], dk_ref[0, 1]], axis=-1)
        dpk, dknw = _rms_heads_bwd(dkv, p_ref[0][:, 0:KVW], knw_ref[...], bd_ref[...])
        dp = jnp.concatenate([dpk.astype(BF16), dv_ref[0, 0].astype(BF16), dv_ref[0, 1].astype(BF16)], axis=-1)
        gw_ref[...] += _dot_tn(dp, u.astype(BF16))
        du = _dot(dp, w_ref[...])
        rows_ref[0:1, :] += jnp.sum(du, axis=0, keepdims=True)
        rows_ref[1:2, :] += jnp.sum(du * yv, axis=0, keepdims=True)
        rows_ref[2:3, :] += jnp.sum(du * (1.0 + scale) * xn, axis=0, keepdims=True)
        dknw_ref[...] += jnp.sum(dknw, axis=0, keepdims=True)

    def const(shape):
        return pl.BlockSpec(shape, lambda b: (0,) * len(shape))

    def tile(w):
        return pl.BlockSpec((1, cl, w), lambda b: (b, 0, 0))

    kv_tile = pl.BlockSpec((1, KVW // HD, cl, HD), lambda b: (b, 0, 0, 0))
    return pl.pallas_call(
        body, name="ctx_bwd", grid=(bl,),
        in_specs=[tile(D), const((1, 3, D)), const((1, D)), _KV_ROWS_OF_W_IN_T, tile(2 * KVW), kv_tile, kv_tile,
                  const((1, KVW)), const((KVW, KVW))],
        out_specs=[const((2 * KVW, D)), const((8, D)), const((1, KVW))],
        out_shape=[jax.ShapeDtypeStruct((2 * KVW, D), F32), jax.ShapeDtypeStruct((8, D), F32),
                   jax.ShapeDtypeStruct((1, KVW), F32)],
        compiler_params=_params(("arbitrary",)),
    )(ctx, modc, norm_w, w_kv_b, pkv_c, dk_c, dv_c, knw_t, ones_bd)


def _bwd_in(x, modrows, norm_w, w_in_b, cos, sins, qnw_t, knw_t, ones_bd,
            pq, pkv, dq, dk, dv, dza, dglu, dzc, dh, gw_kv):
    bl, s, _ = x.shape
    nt = s // TM

    def body(x_ref, mod_ref, nw_ref, win_hbm, cos_ref, sin_ref, qnw_ref, knw_ref, bd_ref,
             pq_ref, pkv_ref, dq_ref, dk_ref, dv_ref, dza_ref, dglu_ref, dzc_ref, dh_ref, gwkv_ref,
             gx_ref, gw_hbm, dmod_ref, dnw_ref, dqnw_ref, dknw_ref, win_ref, gw_acc, sem):
        b, i = pl.program_id(0), pl.program_id(1)

        @pl.when((b == 0) & (i == 0))
        def _():
            cp = pltpu.make_async_copy(win_hbm, win_ref, sem)
            cp.start()
            gw_acc[...] = jnp.zeros_like(gw_acc)
            dnw_ref[...] = jnp.zeros_like(dnw_ref)
            dqnw_ref[...] = jnp.zeros_like(dqnw_ref)
            dknw_ref[...] = jnp.zeros_like(dknw_ref)
            cp.wait()

        @pl.when(i == 0)
        def _():
            dmod_ref[...] = jnp.zeros_like(dmod_ref)

        ck = cos_ref[...]
        sk = sin_ref[...]
        cs = jnp.concatenate([ck] * (DA // KVW), axis=-1)
        sn = jnp.concatenate([sk] * (DA // KVW), axis=-1)
        bd = bd_ref[...]
        dqn = _rope_bwd(dq_ref[0], cs, sn)
        dpq, dqnw = _rms_heads_bwd(dqn, pq_ref[0], qnw_ref[...], bd)
        dkn = _rope_bwd(jnp.concatenate([dk_ref[0, 0], dk_ref[0, 1]], axis=-1), ck, sk)
        dpk, dknw = _rms_heads_bwd(dkn, pkv_ref[0][:, 0:KVW], knw_ref[...], bd[0:KVW, 0:KVW])
        dqnw_ref[...] += jnp.sum(dqnw, axis=0, keepdims=True)
        dknw_ref[...] += jnp.sum(dknw, axis=0, keepdims=True)
        dp = jnp.concatenate(
            [dpq.astype(BF16), dpk.astype(BF16), dv_ref[0, 0].astype(BF16), dv_ref[0, 1].astype(BF16),
             dza_ref[0].astype(BF16), dglu_ref[0].astype(BF16), dzc_ref[0].astype(BF16)], axis=-1)

        xv = x_ref[0]
        shift = mod_ref[0, 0:1, :]
        scale = mod_ref[0, 1:2, :]
        nw = nw_ref[...]
        r = lax.rsqrt(jnp.mean(xv * xv, axis=-1, keepdims=True) + EPS)
        xn = xv * r
        yv = xn * nw
        u = yv * (1.0 + scale) + shift
        gw_acc[...] += _dot_tn(dp, u.astype(BF16))
        du = _dot(dp, win_ref[...])
        dmod_ref[0, 0:1, :] += jnp.sum(du, axis=0, keepdims=True)
        dmod_ref[0, 1:2, :] += jnp.sum(du * yv, axis=0, keepdims=True)
        dy = du * (1.0 + scale)
        dnw_ref[...] += jnp.sum(dy * xn, axis=0, keepdims=True)
        dxn = dy * nw
        gx_ref[0] = dh_ref[0] + r * (dxn - xn * jnp.mean(dxn * xn, axis=-1, keepdims=True))

        @pl.when((b == bl - 1) & (i == nt - 1))
        def _():
            gw_acc[DA:DA + 2 * KVW, :] += gwkv_ref[...]

            def to_bf16(j, carry):
                rows = pl.ds(pl.multiple_of(j * 2 * KVW, 2 * KVW), 2 * KVW)
                win_ref[rows, :] = gw_acc[rows, :].astype(BF16)
                return carry

            lax.fori_loop(0, D_IN // (2 * KVW), to_bf16, 0)
            pltpu.sync_copy(win_ref, gw_hbm)

    def tile(w):
        return pl.BlockSpec((1, TM, w), lambda b, i: (b, i, 0))

    def const(shape):
        return pl.BlockSpec(shape, lambda b, i: (0,) * len(shape))

    anyspace = pl.BlockSpec(memory_space=pl.ANY)
    rope = pl.BlockSpec((TM, KVW), lambda b, i: (i, 0))
    ctx_tiles = (dk.shape[2] - s) // TM
    kv_tile = pl.BlockSpec((1, KVW // HD, TM, HD), lambda b, i: (b, 0, ctx_tiles + i, 0))
    return pl.pallas_call(
        body, name="bwd_in", grid=(bl, nt),
        in_specs=[tile(D), pl.BlockSpec((1, 3, D), lambda b, i: (b, 0, 0)), const((1, D)), anyspace, rope, rope,
                  const((1, DA)), const((1, KVW)), const((DA, DA)),
                  tile(DA), tile(2 * KVW), tile(DA), kv_tile, kv_tile, tile(DA), tile(2 * DC), tile(DC), tile(D),
                  const((2 * KVW, D))],
        out_specs=[tile(D), anyspace, pl.BlockSpec((1, 2, D), lambda b, i: (b, 0, 0)), const((1, D)),
                   const((1, DA)), const((1, KVW))],
        out_shape=[jax.ShapeDtypeStruct((bl, s, D), F32), jax.ShapeDtypeStruct((D_IN, D), BF16),
                   jax.ShapeDtypeStruct((bl, 2, D), F32), jax.ShapeDtypeStruct((1, D), F32),
                   jax.ShapeDtypeStruct((1, DA), F32), jax.ShapeDtypeStruct((1, KVW), F32)],
        scratch_shapes=[pltpu.VMEM((D_IN, D), BF16), pltpu.VMEM((D_IN, D), F32), pltpu.SemaphoreType.DMA],
        compiler_params=_params(("arbitrary", "arbitrary")),
    )(x, modrows, norm_w, w_in_b, cos, sins, qnw_t, knw_t, ones_bd,
      pq, pkv, dq, dk, dv, dza, dglu, dzc, dh, gw_kv)


_LOSS, _DMODC, _NW, _QN, _KN, _CB, _LW, _LB, _BPW, SMALL_W = 0, 1024, 4096, 5120, 5248, 5376, 5888, 6400, 6912, 7424


ROW_W = 1792


def _put_flat(ref, off, value):
    n, done = value.shape[1], 0
    while done < n:
        r, c = divmod(off + done, ROW_W)
        take = min(n - done, ROW_W - c)
        ref[r:r + 1, c:c + take] = value[:, done:done + take]
        done += take


def _get_flat(arr, off, n):
    parts, done = [], 0
    while done < n:
        r, c = divmod(off + done, ROW_W)
        take = min(n - done, ROW_W - c)
        parts.append(arr[r:r + 1, c:c + take])
        done += take
    return parts[0] if len(parts) == 1 else jnp.concatenate(parts, axis=-1)


def _pack_small_body(loss_ref, ctx_ref, dnw_ref, dqnw_ref, dknw_ref, dknwc_ref, conv_ref, dss_ref, dgate_ref, o_ref):
    bl = dss_ref.shape[0]
    assert SMALL_W + bl * 3 * D <= 8 * ROW_W
    o_ref[...] = jnp.zeros_like(o_ref)
    _put_flat(o_ref, _LOSS, loss_ref[...])
    _put_flat(o_ref, _DMODC, ctx_ref[0:1, :])
    _put_flat(o_ref, _DMODC + D, ctx_ref[1:2, :])
    _put_flat(o_ref, _NW, dnw_ref[...] + ctx_ref[2:3, :])
    dq = dqnw_ref[...]
    qn = dq[:, 0:HD]
    for h in range(1, DA // HD):
        qn = qn + dq[:, h * HD:(h + 1) * HD]
    _put_flat(o_ref, _QN, qn)
    dk = dknw_ref[...] + dknwc_ref[...]
    _put_flat(o_ref, _KN, dk[:, 0:HD] + dk[:, HD:2 * HD])
    _put_flat(o_ref, _BPW, conv_ref[0:1, :])
    _put_flat(o_ref, _LW, conv_ref[1:2, :])
    _put_flat(o_ref, _LB, conv_ref[2:3, :])
    _put_flat(o_ref, _CB, conv_ref[3:4, :])
    for b in range(bl):
        _put_flat(o_ref, SMALL_W + b * 3 * D, dss_ref[b, 0:1, :])
        _put_flat(o_ref, SMALL_W + b * 3 * D + D, dss_ref[b, 1:2, :])
        _put_flat(o_ref, SMALL_W + b * 3 * D + 2 * D, dgate_ref[b])


_SMALL = (("b_mod", None), ("norm_w", _NW), ("q_norm_w", _QN), ("k_norm_w", _KN), ("conv_b", _CB),
          ("conv_ln_w", _LW), ("conv_ln_b", _LB), ("b_pw", _BPW), ("c_ctx", None))


def _epilogue(parts_in, w_in_t, m_in_t, v_in_t, pieces, c_rows, w_mod_loc, m_mod, v_mod, small_w, small_m, small_v):
    bl = pieces[7].shape[0]
    n_ex = N_DEV * bl
    n_mod = w_mod_loc.shape[1]
    ns = len(_SMALL)
    rb = 32
    rows_in = w_in_t.shape[0]

    def body(*refs):
        it = iter(refs)
        take = lambda k: [next(it) for _ in range(k)]
        (parts, w_ref, m_ref, v_ref) = take(4)
        piece_refs = take(9)
        (c_ref, wm_ref, mm_ref, vm_ref) = take(4)
        sw, sm, sv = take(ns), take(ns), take(ns)
        (g_in, d_in, nm_in, nv_in, g_wm, d_wm, nm_wm, nv_wm) = take(8)
        souts = take(4 * ns)
        (loss_ref,) = take(1)
        (mine, got_sib, stage, got_chip, payload, gathered, dmod_full, gc_mine, gc_all) = take(9)
        (d2d_send, d2d_recv, ici_send, ici_recv, local_sems, sg_send, sg_recv, gc_send, gc_recv, misc_sems) = take(10)

        x, y, c = _coords()
        me = _lin(x, y, c)
        sib = (x, y, 1 - c)
        peers = [(1 - x, y), (x, 1 - y), (1 - x, 1 - y)]
        home = 2 * x + y

        def rows_loop(fn):
            def step(i, carry):
                fn(pl.ds(pl.multiple_of(i * rb, rb), rb))
                return carry
            lax.fori_loop(0, rows_in // rb, step, 0)

        def direct_gather(src, dst, send_sems, recv_sems, local_sem):
            cps = [pltpu.make_async_copy(src, dst.at[me], local_sem)]
            for k in range(1, N_DEV):
                peer = (1 - x if k & 4 else x, 1 - y if k & 2 else y, 1 - c if k & 1 else c)
                cps.append(pltpu.make_async_remote_copy(
                    src_ref=src, dst_ref=dst.at[me], send_sem=send_sems.at[k - 1], recv_sem=recv_sems.at[k - 1],
                    device_id=peer, device_id_type=MESH_ID))
            for cp in cps:
                cp.start()
            return cps

        _pack_small_body(*piece_refs, payload)
        small_cps = direct_gather(payload, gathered, sg_send, sg_recv, misc_sems.at[0])

        local, d2d, ici = [], [], []
        for s in range(4):
            cp = pltpu.make_async_copy(parts.at[_lin(s // 2, s % 2, c)], mine.at[s], local_sems.at[s])
            cp.start()
            local.append(cp)
            rc = pltpu.make_async_remote_copy(
                src_ref=parts.at[_lin(s // 2, s % 2, 1 - c)], dst_ref=got_sib.at[s],
                send_sem=d2d_send.at[s], recv_sem=d2d_recv.at[s], device_id=sib, device_id_type=MESH_ID)
            rc.start()
            d2d.append(rc)

        for cp in small_cps[1:]:
            cp.wait_recv()
        small_cps[0].wait()
        tot = gathered[0]
        for j in range(1, N_DEV):
            tot = tot + gathered[j]
        summed = _get_flat(tot, 0, SMALL_W)
        dmod_full[...] = jnp.zeros_like(dmod_full)
        for j in range(N_DEV):
            arr = gathered[j]
            for b in range(bl):
                dmod_full[j * bl + b:j * bl + b + 1, :] = _get_flat(arr, SMALL_W + b * 3 * D, 3 * D)
        dmod_full[n_ex:n_ex + 1, :] = summed[:, _DMODC:_DMODC + 3 * D]
        g_bmod = jnp.sum(dmod_full[...], axis=0, keepdims=True)
        loss_ref[...] = (0.5 / D) * jnp.sum(summed[:, _LOSS:_LOSS + D], axis=-1, keepdims=True)

        for s in range(4):
            local[s].wait()
            d2d[s].wait_recv()
        for k, (px, py) in enumerate(peers):
            slot = 2 * px + py

            def pair_sum(rs, k=k, slot=slot):
                stage[k, rs, :] = (mine[slot, rs, :].astype(F32) + got_sib[slot, rs, :].astype(F32)).astype(BF16)

            rows_loop(pair_sum)
            rc = pltpu.make_async_remote_copy(
                src_ref=stage.at[k], dst_ref=got_chip.at[k], send_sem=ici_send.at[k], recv_sem=ici_recv.at[k],
                device_id=(px, py, c), device_id_type=MESH_ID)
            rc.start()
            ici.append(rc)

        cr = c_ref[...]
        act = (cr * _sigmoid(cr)).astype(BF16)
        dm = dmod_full[:, pl.ds(pl.multiple_of(me * n_mod, 128), n_mod)].astype(BF16)
        g = _dot_tn(act, dm)
        wm = wm_ref[...]
        delta, m2, v2 = _adamw(wm, g, mm_ref[...], vm_ref[...])
        g_wm[...] = g
        d_wm[...] = delta
        nm_wm[...] = m2
        nv_wm[...] = v2
        gc_mine[...] = _dot_nt(dm[n_ex:n_ex + 8, :], wm.astype(BF16))
        gc_cps = direct_gather(gc_mine, gc_all, gc_send, gc_recv, misc_sems.at[1])

        for rc in ici:
            rc.wait_recv()

        def finish(rs):
            gsum = mine[home, rs, :].astype(F32) + got_sib[home, rs, :].astype(F32)
            for k in range(3):
                gsum = gsum + got_chip[k, rs, :].astype(F32)
            dl, m_new, v_new = _adamw(w_ref[rs, :], gsum, m_ref[rs, :], v_ref[rs, :])
            g_in[rs, :] = gsum
            d_in[rs, :] = dl
            nm_in[rs, :] = m_new
            nv_in[rs, :] = v_new

        rows_loop(finish)

        for cp in gc_cps[1:]:
            cp.wait_recv()
        gc_cps[0].wait()
        for k, (name, off) in enumerate(_SMALL):
            w = sw[k][...]
            if name == "b_mod":
                gk = g_bmod
            elif name == "c_ctx":
                acc = gc_all[0, 0:1, :]
                for j in range(1, N_DEV):
                    acc = acc + gc_all[j, 0:1, :]
                sg = _sigmoid(w)
                gk = acc * (sg * (1.0 + w * (1.0 - sg)))
            else:
                gk = summed[:, off:off + w.shape[1]]
            dl, m_new, v_new = _adamw(w, gk, sm[k][...], sv[k][...])
            souts[k][...] = gk
            souts[ns + k][...] = dl
            souts[2 * ns + k][...] = m_new
            souts[3 * ns + k][...] = v_new

        for rc in d2d + ici + small_cps[1:] + gc_cps[1:]:
            rc.wait_send()

    vm = pl.BlockSpec(memory_space=pltpu.VMEM)
    anyspace = pl.BlockSpec(memory_space=pl.ANY)
    assert rows_in % rb == 0 and parts_in.shape == (N_DEV,) + w_in_t.shape and parts_in.dtype == BF16
    args = [parts_in, w_in_t, m_in_t, v_in_t, *pieces, c_rows, w_mod_loc, m_mod, v_mod, *small_w, *small_m, *small_v]
    in_specs = [anyspace] + [vm] * (len(args) - 1)
    big = jax.ShapeDtypeStruct(w_in_t.shape, F32)
    mod = jax.ShapeDtypeStruct(w_mod_loc.shape, F32)
    out_shape = [big] * 4 + [mod] * 4 + [jax.ShapeDtypeStruct(w.shape, F32) for w in small_w] * 4 \
        + [jax.ShapeDtypeStruct((1, 1), F32)]
    shp = w_in_t.shape
    scratch = [pltpu.VMEM((4,) + shp, BF16), pltpu.VMEM((4,) + shp, BF16), pltpu.VMEM((3,) + shp, BF16),
               pltpu.VMEM((3,) + shp, BF16), pltpu.VMEM((8, ROW_W), F32), pltpu.VMEM((N_DEV, 8, ROW_W), F32),
               pltpu.VMEM((n_ex + 8, 3 * D), F32), pltpu.VMEM((8, D), F32), pltpu.VMEM((N_DEV, 8, D), F32),
               pltpu.SemaphoreType.DMA((4,)), pltpu.SemaphoreType.DMA((4,)), pltpu.SemaphoreType.DMA((3,)),
               pltpu.SemaphoreType.DMA((3,)), pltpu.SemaphoreType.DMA((4,)),
               pltpu.SemaphoreType.DMA((N_DEV - 1,)), pltpu.SemaphoreType.DMA((N_DEV - 1,)),
               pltpu.SemaphoreType.DMA((N_DEV - 1,)), pltpu.SemaphoreType.DMA((N_DEV - 1,)),
               pltpu.SemaphoreType.DMA((2,))]
    outs = pl.pallas_call(
        body, name="epilogue", out_shape=out_shape, in_specs=in_specs, out_specs=[vm] * len(out_shape),
        scratch_shapes=scratch, compiler_params=pltpu.CompilerParams(vmem_limit_bytes=VMEM_LIMIT),
    )(*args)
    small_outs = [outs[8 + k * ns:8 + (k + 1) * ns] for k in range(4)]
    return tuple(outs[0:4]), tuple(outs[4:8]), small_outs, outs[8 + 4 * ns]


def _rope_tables(s):
    t = jnp.arange(s, dtype=jnp.int32)
    row = (t // GRID_W).astype(F32)
    col = (t % GRID_W).astype(F32)
    freqs = ROPE_THETA ** (-jnp.arange(0, HD // 2, 2, dtype=F32) / (HD // 2))
    ang_r = row[:, None] * freqs[None, :]
    ang_c = col[:, None] * freqs[None, :]
    cr, sr, cc, sc = jnp.cos(ang_r), jnp.sin(ang_r), jnp.cos(ang_c), jnp.sin(ang_c)
    cos = jnp.concatenate([cr, cr, cc, cc], axis=-1)
    sins = jnp.concatenate([-sr, sr, -sc, sc], axis=-1)
    return jnp.tile(cos, (1, KVW // HD)), jnp.tile(sins, (1, KVW // HD))


def kernel(x, c, ctx, c_ctx, w_mod, b_mod, norm_w, w_in, q_norm_w, k_norm_w, conv_w, conv_b, conv_ln_w, conv_ln_b, w_pw, b_pw, w_out, loss_target, m_c_ctx, m_w_mod, m_b_mod, m_norm_w, m_w_in, m_q_norm_w, m_k_norm_w, m_conv_w, m_conv_b, m_conv_ln_w, m_conv_ln_b, m_w_pw, m_b_pw, m_w_out, v_c_ctx, v_w_mod, v_b_mod, v_norm_w, v_w_in, v_q_norm_w, v_k_norm_w, v_conv_w, v_conv_b, v_conv_ln_w, v_conv_ln_b, v_w_pw, v_b_pw, v_w_out):
    bl, s, _ = x.shape
    cl = ctx.shape[1]
    me = _lin(*_coords())
    n_mod = w_mod.shape[2]

    conv_w_pad = jnp.pad(conv_w[0], ((0, 32 - KW), (0, 0)))
    c_pad = jnp.pad(c, ((0, 8 - bl), (0, 0)))
    g_win, g_c = _all_gather_many([w_in[0].T.astype(BF16), c_pad], "gather_weights")
    w_in_b = g_win.reshape(D_IN, D)
    c_all = g_c[:, :bl, :].reshape(N_DEV * bl, D)
    n_ex = N_DEV * bl
    c_rows = jnp.concatenate([c_all, c_ctx[None, :], jnp.zeros((7, D), F32)], axis=0)

    b_mod_loc = lax.dynamic_slice_in_dim(b_mod, me * n_mod, n_mod, axis=1)
    mod_loc = _mod_fwd(c_rows, w_mod[0], b_mod_loc)
    g_mod = _all_gather_direct(mod_loc, "gather_mod")
    mod_all = g_mod.transpose(1, 0, 2).reshape(n_ex + 8, 3 * D)
    modrows = lax.dynamic_slice_in_dim(mod_all, me * bl, bl, axis=0).reshape(bl, 3, D)
    modc = mod_all[n_ex].reshape(1, 3, D)

    cos, sins = _rope_tables(s)
    qnw_t = jnp.tile(q_norm_w, (1, DA // HD))
    knw_t = jnp.tile(k_norm_w, (1, KVW // HD))
    lane = jnp.arange(DA, dtype=jnp.int32) // HD
    ones_bd = (lane[:, None] == lane[None, :]).astype(BF16)
    ones_kv = ones_bd[0:KVW, 0:KVW]
    w_kv_b = w_in_b

    k_ctx, v_ctx, pkv_c = _ctx_fwd(ctx, modc, norm_w, w_kv_b, knw_t, ones_kv, cl + s)
    (q_h, k_h, v_h, pq, pkv, za, glu, zc), (g_wout, g_wpw, g_cw) = _fwd_in(
        x, modrows, norm_w, w_in_b, cos, sins, qnw_t, knw_t, ones_bd, k_ctx, v_ctx,
        [w_out[0].astype(BF16), w_pw[0].astype(BF16), conv_w_pad])
    w_out_b = g_wout.reshape(D, D)
    w_pw_b = g_wpw.reshape(DC, DC)
    conv_w_full = g_cw.transpose(1, 0, 2).reshape(32, DC)
    attn, lse = _attn_fwd(q_h, k_h, v_h)
    y_conv, cp = _conv_fwd(glu, conv_w_full, conv_b, conv_ln_w, conv_ln_b, w_pw_b, b_pw)

    do_h, dza, dcp, dzc, dh, dgate, gw_out, loss_row = _out_fwd_bwd(attn, za, cp, zc, x, loss_target, modrows, w_out_b)
    dy_conv, gw_pw, conv_rows = _conv_bwd_pointwise(y_conv, dcp, conv_ln_w, conv_ln_b, w_pw_b)
    dglu, g_cw_full = _conv_bwd_depthwise(glu, dy_conv, conv_w_full)
    parts_out = gw_out.astype(BF16).reshape(N_DEV, D // N_DEV, D)
    parts_pw = gw_pw.astype(BF16).reshape(N_DEV, DC // N_DEV, DC)
    parts_cw = g_cw_full.astype(BF16).reshape(32, N_DEV, DC // N_DEV).transpose(1, 0, 2)
    (dq, dk_h, dv_h), (got_out, got_pw, got_cw) = _attn_bwd(
        q_h, k_h, v_h, do_h, attn, lse, [parts_out, parts_pw, parts_cw])
    gw_kv, ctx_rows, dknw_c = _ctx_bwd(ctx, modc, norm_w, w_kv_b, pkv_c, dk_h, dv_h, knw_t, ones_kv)
    grad_x, gw_in, dmod_ss, dnw, dqnw, dknw = _bwd_in(
        x, modrows, norm_w, w_in_b, cos, sins, qnw_t, knw_t, ones_bd,
        pq, pkv, dq, dk_h, dv_h, dza, dglu, dzc, dh, gw_kv)

    def pad_cw(a):
        return jnp.pad(a[0], ((0, 32 - KW), (0, 0)))

    r_out, r_pw, r_cw = _sum_devices_adamw(
        [(got_out, w_out[0], m_w_out[0], v_w_out[0]), (got_pw, w_pw[0], m_w_pw[0], v_w_pw[0]),
         (got_cw, pad_cw(conv_w), pad_cw(m_conv_w), pad_cw(v_conv_w))])
    r_cw = tuple(a[:KW] for a in r_cw)

    given = {"c_ctx": (c_ctx, m_c_ctx, v_c_ctx), "b_mod": (b_mod, m_b_mod, v_b_mod), "norm_w": (norm_w, m_norm_w, v_norm_w),
             "q_norm_w": (q_norm_w, m_q_norm_w, v_q_norm_w), "k_norm_w": (k_norm_w, m_k_norm_w, v_k_norm_w),
             "conv_b": (conv_b, m_conv_b, v_conv_b), "conv_ln_w": (conv_ln_w, m_conv_ln_w, v_conv_ln_w),
             "conv_ln_b": (conv_ln_b, m_conv_ln_b, v_conv_ln_b), "b_pw": (b_pw, m_b_pw, v_b_pw)}
    as_rows = [[given[name][which].reshape(1, -1) for name, _ in _SMALL] for which in range(3)]
    r_in, r_wmod, small_outs, loss11 = _epilogue(
        gw_in.reshape(N_DEV, D_IN // N_DEV, D), w_in[0].T, m_w_in[0].T, v_w_in[0].T,
        [loss_row, ctx_rows, dnw, dqnw, dknw, dknw_c, conv_rows, dmod_ss, dgate],
        c_rows, w_mod[0], m_w_mod[0], v_w_mod[0], *as_rows)
    r_in = tuple(a.T for a in r_in)

    big = {"w_mod": r_wmod, "w_in": r_in, "conv_w": r_cw, "w_pw": r_pw, "w_out": r_out}
    order = ["c_ctx", "w_mod", "b_mod", "norm_w", "w_in", "q_norm_w", "k_norm_w", "conv_w", "conv_b", "conv_ln_w",
             "conv_ln_b", "w_pw", "b_pw", "w_out"]
    small_index = {name: k for k, (name, _) in enumerate(_SMALL)}
    outs = [loss11.reshape(()), grad_x]
    for which in range(4):
        for name in order:
            if name in big:
                outs.append(big[name][which][None])
            else:
                outs.append(small_outs[which][small_index[name]].reshape(given[name][0].shape))
    return tuple(outs)
```

```python
import functools

import jax
import jax.numpy as jnp
from jax import lax
from jax.experimental import pallas as pl
from jax.experimental.pallas import tpu as pltpu

F32, BF16 = jnp.float32, jnp.bfloat16
MESH_ID = pl.DeviceIdType.MESH

N_DEV = 8
D = 1024
D_IN = 2816
DA = 512
DC = 512
HD = 64
KVW = 128
KW = 31
HALO = 16
EPS = 1e-6
ROPE_THETA = 10000.0
GRID_W = 64

ADAM_LR, ADAM_B1, ADAM_B2, ADAM_EPS, ADAM_WD, ADAM_STEP = 0.001, 0.9, 0.999, 1e-08, 0.01, 10

VMEM_LIMIT = 56 * 1024 * 1024

TM = 256
TQ = 128
TOKEN_PARTS = 2
BWD_PARTS = 2
FWD_PARTS = 4
TC = 512
CH = 64


def _params(sem, vmem=VMEM_LIMIT):
    return pltpu.CompilerParams(dimension_semantics=sem, vmem_limit_bytes=vmem)


def _dot(a, b):
    return jnp.dot(a, b, preferred_element_type=F32)


def _dot_nt(a, b):
    return lax.dot_general(a, b, (((1,), (1,)), ((), ())), preferred_element_type=F32)


def _dot_tn(a, b):
    return lax.dot_general(a, b, (((0,), (0,)), ((), ())), preferred_element_type=F32)


def _sigmoid(z):
    return 1.0 / (1.0 + jnp.exp(-z))


def _segsum(v, ones_bd):
    return _dot(v.astype(BF16), ones_bd)


def _swap16(x):
    w = x.shape[-1]
    lane = lax.broadcasted_iota(jnp.int32, x.shape, 1)
    return jnp.where((lane % 32) < 16, pltpu.roll(x, w - 16, 1), pltpu.roll(x, 16, 1))


def _with_ones_column(v):
    one = (lax.broadcasted_iota(jnp.int32, v.shape, 1) == 0).astype(v.dtype)
    return jnp.concatenate([v, one], axis=-1)


def _rope(x, cos, sins):
    return x * cos + _swap16(x) * sins


def _rope_bwd(d, cos, sins):
    return d * cos + _swap16(d * sins)


def _adamw(w, g, m, v):
    m2 = ADAM_B1 * m + (1.0 - ADAM_B1) * g
    v2 = ADAM_B2 * v + (1.0 - ADAM_B2) * (g * g)
    m_hat = m2 / (1.0 - ADAM_B1 ** ADAM_STEP)
    v_hat = v2 / (1.0 - ADAM_B2 ** ADAM_STEP)
    delta = -ADAM_LR * (m_hat / (jnp.sqrt(v_hat) + ADAM_EPS) + ADAM_WD * w)
    return delta, m2, v2


def _coords():
    return lax.axis_index("x"), lax.axis_index("y"), lax.axis_index("c")


def _lin(x, y, c):
    return 4 * x + 2 * y + c


def _all_gather_many(arrs, name):
    n = len(arrs)

    def body(*refs):
        in_refs, out_refs = refs[:n], refs[n:2 * n]
        send_sems, recv_sems, local_sems = refs[2 * n:]
        x, y, c = _coords()
        me, sib = (x, y, c), (x, y, 1 - c)
        chips = [(1 - x, y), (x, 1 - y), (1 - x, 1 - y)]

        def copy(a, k, block, to, src=None):
            slot = out_refs[a].at[_lin(*block)]
            return pltpu.make_async_remote_copy(
                src_ref=slot if src is None else src, dst_ref=slot,
                send_sem=send_sems.at[a * 7 + k], recv_sem=recv_sems.at[a * 7 + k],
                device_id=to, device_id_type=MESH_ID)

        mine = [pltpu.make_async_copy(in_refs[a], out_refs[a].at[_lin(*me)], local_sems.at[a]) for a in range(n)]
        for cp in mine:
            cp.start()
        first = []
        for a in range(n):
            first.append(copy(a, 0, me, sib, src=in_refs[a]))
            first += [copy(a, 1 + j, me, (*chip, c), src=in_refs[a]) for j, chip in enumerate(chips)]
        for cp in first:
            cp.start()
        passed = []
        for a in range(n):
            for j, chip in enumerate(chips):
                copy(a, 1 + j, (*chip, c), me).wait_recv()
                fwd = copy(a, 4 + j, (*chip, c), sib)
                fwd.start()
                passed.append(fwd)
        for a in range(n):
            copy(a, 0, sib, me).wait_recv()
            for j, chip in enumerate(chips):
                copy(a, 4 + j, (*chip, 1 - c), me).wait_recv()
        for cp in first + passed:
            cp.wait_send()
        for cp in mine:
            cp.wait()

    vm = pl.BlockSpec(memory_space=pltpu.VMEM)
    return pl.pallas_call(
        body, name=name,
        out_shape=[jax.ShapeDtypeStruct((N_DEV,) + a.shape, a.dtype) for a in arrs],
        in_specs=[vm] * n, out_specs=[vm] * n,
        scratch_shapes=[pltpu.SemaphoreType.DMA((7 * n,)), pltpu.SemaphoreType.DMA((7 * n,)),
                        pltpu.SemaphoreType.DMA((n,))],
        compiler_params=pltpu.CompilerParams(vmem_limit_bytes=VMEM_LIMIT),
    )(*arrs)


def _exchange_copies(in_refs, out_refs, send_sems, recv_sems, local_sems, scatter):
    x, y, c = _coords()
    me = _lin(x, y, c)
    local, remote = [], []
    for a, (src, dst) in enumerate(zip(in_refs, out_refs)):
        local.append(pltpu.make_async_copy(src.at[me] if scatter else src, dst.at[me], local_sems.at[a]))
        for k in range(1, N_DEV):
            peer = (1 - x if k & 4 else x, 1 - y if k & 2 else y, 1 - c if k & 1 else c)
            remote.append(pltpu.make_async_remote_copy(
                src_ref=src.at[_lin(*peer)] if scatter else src, dst_ref=dst.at[me],
                send_sem=send_sems.at[a * (N_DEV - 1) + k - 1], recv_sem=recv_sems.at[a * (N_DEV - 1) + k - 1],
                device_id=peer, device_id_type=MESH_ID))
    return local, remote


def _exchange_scratch(n):
    return [pltpu.SemaphoreType.DMA((n * (N_DEV - 1),)), pltpu.SemaphoreType.DMA((n * (N_DEV - 1),)),
            pltpu.SemaphoreType.DMA((n,))]


def _all_gather_direct(arr, name):
    def body(in_ref, out_ref, send_sems, recv_sems, local_sem):
        x, y, c = _coords()
        me = _lin(x, y, c)
        mine = pltpu.make_async_copy(in_ref, out_ref.at[me], local_sem)
        mine.start()
        copies = []
        for k in range(1, N_DEV):
            peer = (1 - x if k & 4 else x, 1 - y if k & 2 else y, 1 - c if k & 1 else c)
            cp = pltpu.make_async_remote_copy(
                src_ref=in_ref, dst_ref=out_ref.at[me], send_sem=send_sems.at[k - 1], recv_sem=recv_sems.at[k - 1],
                device_id=peer, device_id_type=MESH_ID)
            cp.start()
            copies.append(cp)
        for cp in copies:
            cp.wait_recv()
        for cp in copies:
            cp.wait_send()
        mine.wait()

    vm = pl.BlockSpec(memory_space=pltpu.VMEM)
    return pl.pallas_call(
        body, name=name, out_shape=jax.ShapeDtypeStruct((N_DEV,) + arr.shape, arr.dtype),
        in_specs=[vm], out_specs=vm,
        scratch_shapes=[pltpu.SemaphoreType.DMA((N_DEV - 1,)), pltpu.SemaphoreType.DMA((N_DEV - 1,)),
                        pltpu.SemaphoreType.DMA],
    )(arr)


def _reduce_scatter_adamw(items, name):
    n = len(items)
    rb = 32

    def body(*refs):
        parts = refs[0:n]
        wmv = refs[n:4 * n]
        outs = refs[4 * n:8 * n]
        bufs = [refs[8 * n + 4 * a:8 * n + 4 * a + 4] for a in range(n)]
        d2d_send, d2d_recv, ici_send, ici_recv, local_sems = refs[12 * n:]
        x, y, c = _coords()
        sib = (x, y, 1 - c)
        peers = [(1 - x, y), (x, 1 - y), (1 - x, 1 - y)]
        home = 2 * x + y

        def rows_loop(rows, fn):
            def step(i, carry):
                fn(pl.ds(pl.multiple_of(i * rb, rb), rb))
                return carry
            lax.fori_loop(0, rows // rb, step, 0)

        local, d2d, ici = [], [], []
        for a in range(n):
            mine, got_sib = bufs[a][0], bufs[a][1]
            for s in range(4):
                cp = pltpu.make_async_copy(parts[a].at[_lin(s // 2, s % 2, c)], mine.at[s], local_sems.at[4 * a + s])
                cp.start()
                local.append(cp)
                rc = pltpu.make_async_remote_copy(
                    src_ref=parts[a].at[_lin(s // 2, s % 2, 1 - c)], dst_ref=got_sib.at[s],
                    send_sem=d2d_send.at[4 * a + s], recv_sem=d2d_recv.at[4 * a + s],
                    device_id=sib, device_id_type=MESH_ID)
                rc.start()
                d2d.append(rc)

        for a in range(n):
            mine, got_sib, stage, got_chip = bufs[a]
            for s in range(4):
                local[4 * a + s].wait()
                d2d[4 * a + s].wait_recv()
            for k, (px, py) in enumerate(peers):
                slot = 2 * px + py

                def pair_sum(rs, k=k, slot=slot, mine=mine, got_sib=got_sib, stage=stage):
                    stage[k, rs, :] = (mine[slot, rs, :].astype(F32) + got_sib[slot, rs, :].astype(F32)).astype(BF16)

                rows_loop(wmv[3 * a].shape[0], pair_sum)
                rc = pltpu.make_async_remote_copy(
                    src_ref=stage.at[k], dst_ref=got_chip.at[k],
                    send_sem=ici_send.at[3 * a + k], recv_sem=ici_recv.at[3 * a + k],
                    device_id=(px, py, c), device_id_type=MESH_ID)
                rc.start()
                ici.append(rc)

        for a in range(n):
            mine, got_sib, stage, got_chip = bufs[a]
            w_ref, m_ref, v_ref = wmv[3 * a:3 * a + 3]
            g_ref, d_ref, nm_ref, nv_ref = outs[4 * a:4 * a + 4]
            for k in range(3):
                ici[3 * a + k].wait_recv()

            def finish(rs, mine=mine, got_sib=got_sib, got_chip=got_chip, w_ref=w_ref, m_ref=m_ref, v_ref=v_ref,
                       g_ref=g_ref, d_ref=d_ref, nm_ref=nm_ref, nv_ref=nv_ref):
                g = mine[home, rs, :].astype(F32) + got_sib[home, rs, :].astype(F32)
                for k in range(3):
                    g = g + got_chip[k, rs, :].astype(F32)
                delta, m2, v2 = _adamw(w_ref[rs, :], g, m_ref[rs, :], v_ref[rs, :])
                g_ref[rs, :] = g
                d_ref[rs, :] = delta
                nm_ref[rs, :] = m2
                nv_ref[rs, :] = v2

            rows_loop(w_ref.shape[0], finish)

        for rc in d2d + ici:
            rc.wait_send()

    vm = pl.BlockSpec(memory_space=pltpu.VMEM)
    anyspace = pl.BlockSpec(memory_space=pl.ANY)
    args, in_specs, out_shape, scratch = [], [], [], []
    for parts, w, m, v in items:
        assert w.shape[0] % rb == 0 and parts.shape == (N_DEV,) + w.shape and parts.dtype == BF16
    args += [it[0] for it in items]
    in_specs += [anyspace] * n
    for _, w, m, v in items:
        args += [w, m, v]
        in_specs += [vm] * 3
        out_shape += [jax.ShapeDtypeStruct(w.shape, F32)] * 4
    for it in items:
        shp = it[1].shape
        scratch += [pltpu.VMEM((4,) + shp, BF16), pltpu.VMEM((4,) + shp, BF16),
                    pltpu.VMEM((3,) + shp, BF16), pltpu.VMEM((3,) + shp, BF16)]
    scratch += [pltpu.SemaphoreType.DMA((4 * n,)), pltpu.SemaphoreType.DMA((4 * n,)),
                pltpu.SemaphoreType.DMA((3 * n,)), pltpu.SemaphoreType.DMA((3 * n,)), pltpu.SemaphoreType.DMA((4 * n,))]
    outs = pl.pallas_call(
        body, name=name, out_shape=out_shape, in_specs=in_specs, out_specs=[vm] * (4 * n),
        scratch_shapes=scratch, compiler_params=pltpu.CompilerParams(vmem_limit_bytes=VMEM_LIMIT),
    )(*args)
    return [tuple(outs[4 * a:4 * a + 4]) for a in range(n)]


def _sum_devices_adamw(items):
    n = len(items)

    def body(*refs):
        for a in range(n):
            got, w_ref, m_ref, v_ref = refs[4 * a:4 * a + 4]
            g_ref, d_ref, nm_ref, nv_ref = refs[4 * n + 4 * a:4 * n + 4 * a + 4]
            g = got[0].astype(F32)
            for j in range(1, N_DEV):
                g = g + got[j].astype(F32)
            delta, m2, v2 = _adamw(w_ref[...], g, m_ref[...], v_ref[...])
            g_ref[...] = g
            d_ref[...] = delta
            nm_ref[...] = m2
            nv_ref[...] = v2

    args, out_shape = [], []
    for got, w, m, v in items:
        assert got.shape == (N_DEV,) + w.shape
        args += [got, w, m, v]
        out_shape += [jax.ShapeDtypeStruct(w.shape, F32)] * 4
    outs = pl.pallas_call(body, name="sum_devices_adamw", out_shape=out_shape,
                          compiler_params=pltpu.CompilerParams(vmem_limit_bytes=VMEM_LIMIT))(*args)
    return [tuple(outs[4 * a:4 * a + 4]) for a in range(n)]


def _mod_fwd(c_rows, w_mod_loc, b_mod_loc):
    def body(c_ref, w_ref, b_ref, o_ref):
        cr = c_ref[...]
        a = (cr * _sigmoid(cr)).astype(BF16)
        o_ref[...] = _dot(a, w_ref[...].astype(BF16)) + b_ref[...]

    return pl.pallas_call(
        body, name="mod_fwd", out_shape=jax.ShapeDtypeStruct((c_rows.shape[0], w_mod_loc.shape[1]), F32),
        compiler_params=pltpu.CompilerParams(vmem_limit_bytes=VMEM_LIMIT),
    )(c_rows, w_mod_loc, b_mod_loc)


def _fwd_in(x, modrows, norm_w, w_in_b, cos, sins, qnw_t, knw_t, ones_bd, k_all, v_all, shards):
    bl, s, _ = x.shape
    tm = TOKEN_PARTS * TM
    nt = s // tm
    n_sh = len(shards)

    def body(*refs):
        (x_ref, mod_ref, nw_ref, win_ref, cos_ref, sin_ref, qnw_ref, knw_ref, bd_ref, kin_ref, vin_ref) = refs[:11]
        shard_refs = refs[11:11 + n_sh]
        q_ref, k_ref, v_ref, pq_ref, pkv_ref, za_ref, glu_ref, zc_ref = refs[11 + n_sh:19 + n_sh]
        gathered_refs = refs[19 + n_sh:19 + 2 * n_sh]
        send_sems, recv_sems, local_sems = refs[19 + 2 * n_sh:]
        b, i = pl.program_id(0), pl.program_id(1)
        local, remote = _exchange_copies(shard_refs, gathered_refs, send_sems, recv_sems, local_sems, scatter=False)

        @pl.when((b == 0) & (i == 0))
        def _():
            for cp in local + remote:
                cp.start()

        shift = mod_ref[0, 0:1, :]
        scale = mod_ref[0, 1:2, :]
        for part in range(TOKEN_PARTS):
            rows = pl.ds(part * TM, TM)
            xv = x_ref[0, rows, :]
            r = lax.rsqrt(jnp.mean(xv * xv, axis=-1, keepdims=True) + EPS)
            u = (xv * r * nw_ref[...]) * (1.0 + scale) + shift
            p = _dot_nt(u.astype(BF16), win_ref[...])
            pq = p[:, 0:DA]
            pk = p[:, DA:DA + HD * 2]
            ck = cos_ref[rows, :]
            sk = sin_ref[rows, :]
            cs = jnp.concatenate([ck] * (DA // KVW), axis=-1)
            sn = jnp.concatenate([sk] * (DA // KVW), axis=-1)
            rq = lax.rsqrt(_segsum(pq * pq, bd_ref[...]) * (1.0 / HD) + EPS)
            qn = pq * rq * qnw_ref[...]
            qr = _rope(qn, cs, sn) * 0.125
            for h in range(DA // HD):
                q_ref[0, h, rows, :] = qr[:, h * HD:(h + 1) * HD].astype(BF16)
            rk = lax.rsqrt(_segsum(pk * pk, bd_ref[0:KVW, 0:KVW]) * (1.0 / HD) + EPS)
            kn = pk * rk * knw_ref[...]
            kr = _rope(kn, ck, sk)
            pv = p[:, 640:768]
            for h in range(KVW // HD):
                k_ref[0, h, rows, :] = kr[:, h * HD:(h + 1) * HD].astype(BF16)
                v_ref[0, h, rows, :] = _with_ones_column(pv[:, h * HD:(h + 1) * HD]).astype(BF16)
            pq_ref[0, rows, :] = pq
            pkv_ref[0, rows, :] = p[:, 512:768]
            za_ref[0, rows, :] = p[:, 768:1280]
            glu_ref[0, rows, :] = p[:, 1280:2304]
            zc_ref[0, rows, :] = p[:, 2304:2816]

        @pl.when((b == bl - 1) & (i == nt - 1))
        def _():
            for cp in remote:
                cp.wait_recv()
            for cp in remote:
                cp.wait_send()
            for cp in local:
                cp.wait()

    def tile(w):
        return pl.BlockSpec((1, tm, w), lambda b, i: (b, i, 0))

    def const(shape):
        return pl.BlockSpec(shape, lambda b, i: (0,) * len(shape))

    outs = [(DA, F32), (2 * KVW, F32), (DA, F32), (2 * DC, F32), (DC, F32)]
    anyspace = pl.BlockSpec(memory_space=pl.ANY)
    rope = pl.BlockSpec((tm, KVW), lambda b, i: (i, 0))
    k_tile = pl.BlockSpec((1, KVW // HD, tm, HD), lambda b, i: (b, 0, i, 0))
    v_tile = pl.BlockSpec((1, KVW // HD, tm, 2 * HD), lambda b, i: (b, 0, i, 0))
    res = pl.pallas_call(
        body, name="fwd_in", grid=(bl, nt),
        in_specs=[tile(D), pl.BlockSpec((1, 3, D), lambda b, i: (b, 0, 0)), const((1, D)), const((D_IN, D)),
                  rope, rope, const((1, DA)), const((1, KVW)), const((DA, DA)), anyspace, anyspace]
        + [anyspace] * n_sh,
        out_specs=[pl.BlockSpec((1, DA // HD, tm, HD), lambda b, i: (b, 0, i, 0)), k_tile, v_tile]
        + [tile(w) for w, _ in outs] + [anyspace] * n_sh,
        out_shape=[jax.ShapeDtypeStruct((bl, DA // HD, s, HD), BF16), jax.ShapeDtypeStruct(k_all.shape, BF16),
                   jax.ShapeDtypeStruct(v_all.shape, BF16)]
        + [jax.ShapeDtypeStruct((bl, s, w), dt) for w, dt in outs]
        + [jax.ShapeDtypeStruct((N_DEV,) + a.shape, a.dtype) for a in shards],
        input_output_aliases={9: 1, 10: 2},
        scratch_shapes=_exchange_scratch(n_sh),
        compiler_params=_params(("arbitrary", "arbitrary")),
    )(x, modrows, norm_w, w_in_b, cos, sins, qnw_t, knw_t, ones_bd, k_all, v_all, *shards)
    return res[:8], res[8:]


_KV_ROWS_OF_W_IN_T = pl.BlockSpec((2 * KVW, D), lambda b: (DA // (2 * KVW), 0))


def _ctx_fwd(ctx, modc, norm_w, w_kv_b, knw_t, ones_bd, n_keys):
    bl, cl, _ = ctx.shape

    def body(x_ref, mod_ref, nw_ref, w_ref, knw_ref, bd_ref, k_ref, v_ref, pkv_ref):
        xv = x_ref[0]
        shift = mod_ref[0, 0:1, :]
        scale = mod_ref[0, 1:2, :]
        r = lax.rsqrt(jnp.mean(xv * xv, axis=-1, keepdims=True) + EPS)
        u = (xv * r * nw_ref[...]) * (1.0 + scale) + shift
        p = _dot_nt(u.astype(BF16), w_ref[...])
        pk = p[:, 0:KVW]
        rk = lax.rsqrt(_segsum(pk * pk, bd_ref[...]) * (1.0 / HD) + EPS)
        kn = pk * rk * knw_ref[...]
        pv = p[:, KVW:2 * KVW]
        for h in range(KVW // HD):
            k_ref[0, h] = kn[:, h * HD:(h + 1) * HD].astype(BF16)
            v_ref[0, h] = _with_ones_column(pv[:, h * HD:(h + 1) * HD]).astype(BF16)
        pkv_ref[0] = p

    def const(shape):
        return pl.BlockSpec(shape, lambda b: (0,) * len(shape))

    def tile(w):
        return pl.BlockSpec((1, cl, w), lambda b: (b, 0, 0))

    ctx_block = (n_keys - cl) // cl
    assert ctx_block * cl + cl == n_keys
    k_tile = pl.BlockSpec((1, KVW // HD, cl, HD), lambda b: (b, 0, ctx_block, 0))
    v_tile = pl.BlockSpec((1, KVW // HD, cl, 2 * HD), lambda b: (b, 0, ctx_block, 0))
    return pl.pallas_call(
        body, name="ctx_fwd", grid=(bl,),
        in_specs=[tile(D), const((1, 3, D)), const((1, D)), _KV_ROWS_OF_W_IN_T, const((1, KVW)), const((KVW, KVW))],
        out_specs=[k_tile, v_tile, tile(2 * KVW)],
        out_shape=[jax.ShapeDtypeStruct((bl, KVW // HD, n_keys, HD), BF16),
                   jax.ShapeDtypeStruct((bl, KVW // HD, n_keys, 2 * HD), BF16),
                   jax.ShapeDtypeStruct((bl, cl, 2 * KVW), F32)],
        compiler_params=_params(("arbitrary",)),
    )(ctx, modc, norm_w, w_kv_b, knw_t, ones_bd)


def _attn_fwd(q, k, v1):
    bl, _, s, _ = q.shape
    n_keys = k.shape[2]

    def body(q_ref, k_ref, v_ref, o_ref, lse_ref):
        kv = k_ref[0, 0]
        vv = v_ref[0, 0]
        lane = lax.broadcasted_iota(jnp.int32, (TQ, 2 * HD), 1)
        for part in range(FWD_PARTS):
            rows = pl.ds(part * TQ, TQ)
            lse = jnp.zeros((TQ, 2 * HD), F32)
            heads = []
            sc_all = _dot_nt(q_ref[0, :, rows, :].reshape(4 * TQ, HD), kv)
            for h in range(4):
                sc = sc_all[h * TQ:(h + 1) * TQ, :]
                m = jnp.max(sc, axis=-1, keepdims=True)
                e = jnp.exp(sc - m).astype(BF16)
                ov = _dot(e, vv)
                denom = ov[:, HD:HD + 1]
                heads.append(ov[:, 0:HD] * (1.0 / denom))
                lse = jnp.where(lane == h, m + jnp.log(denom), lse)
            o_ref[0, rows, :] = jnp.concatenate(heads, axis=-1)
            lse_ref[0, 0, rows, :] = lse

    tq = FWD_PARTS * TQ
    ks = pl.BlockSpec((1, 1, n_keys, HD), lambda b, g, i: (b, g, 0, 0))
    qs = pl.BlockSpec((1, 4, tq, HD), lambda b, g, i: (b, g, i, 0))
    vs = pl.BlockSpec((1, 1, n_keys, 2 * HD), lambda b, g, i: (b, g, 0, 0))
    return pl.pallas_call(
        body, name="attn_fwd", grid=(bl, 2, s // tq), in_specs=[qs, ks, vs],
        out_specs=[pl.BlockSpec((1, tq, 4 * HD), lambda b, g, i: (b, i, g)),
                   pl.BlockSpec((1, 1, tq, 2 * HD), lambda b, g, i: (b, g, i, 0))],
        out_shape=[jax.ShapeDtypeStruct((bl, s, DA), F32), jax.ShapeDtypeStruct((bl, 2, s, 2 * HD), F32)],
        compiler_params=_params(("arbitrary", "arbitrary", "arbitrary")),
    )(q, k, v1)


def _attn_bwd(q, k, v1, do, o, lse, exchange):
    bl, _, s, _ = q.shape
    n_keys = k.shape[2]
    tq = BWD_PARTS * TQ
    nq = s // tq
    n_ex = len(exchange)

    def body(*refs):
        q_ref, k_ref, v_ref, do_ref, o_ref, lse_ref = refs[:6]
        part_refs = refs[6:6 + n_ex]
        dq_ref, dk_ref, dv_ref = refs[6 + n_ex:9 + n_ex]
        got_refs = refs[9 + n_ex:9 + 2 * n_ex]
        p_sc, ds_sc, dkt, dvt, send_sems, recv_sems, local_sems = refs[9 + 2 * n_ex:]
        i = pl.program_id(2)
        first = (pl.program_id(0) == 0) & (pl.program_id(1) == 0) & (i == 0)
        last = (pl.program_id(0) == bl - 1) & (pl.program_id(1) == 1) & (i == nq - 1)
        local, remote = _exchange_copies(part_refs, got_refs, send_sems, recv_sems, local_sems, scatter=True)

        @pl.when(first)
        def _():
            for cp in local + remote:
                cp.start()

        @pl.when(i == 0)
        def _():
            dkt[...] = jnp.zeros_like(dkt)
            dvt[...] = jnp.zeros_like(dvt)

        kv = k_ref[0, 0]
        vv = v_ref[0, 0][:, 0:HD]
        for part in range(BWD_PARTS):
            tq_rows = pl.ds(part * TQ, TQ)
            lse = lse_ref[0, 0, tq_rows, :]
            ov = o_ref[0, tq_rows, :]
            dqs = []
            q_cat = q_ref[0, :, tq_rows, :].reshape(4 * TQ, HD)
            do_cat = do_ref[0, :, tq_rows, :].reshape(4 * TQ, HD)
            sc_all = _dot_nt(q_cat, kv)
            for h in range(4):
                doh = do_cat[h * TQ:(h + 1) * TQ, :]
                delta = jnp.sum(ov[:, h * HD:(h + 1) * HD] * doh.astype(F32), axis=-1, keepdims=True)
                rows = pl.ds((part * 4 + h) * TQ, TQ)
                p = jnp.exp(sc_all[h * TQ:(h + 1) * TQ, :] - lse[:, h:h + 1])
                ds = (p * (_dot_nt(doh, vv) - delta)).astype(BF16)
                p_sc[rows, :] = p.astype(BF16)
                ds_sc[rows, :] = ds
                dqs.append(_dot(ds, kv) * 0.125)
            dq_ref[0, tq_rows, :] = jnp.concatenate(dqs, axis=-1)
            part_rows = pl.ds(part * 4 * TQ, 4 * TQ)
            dvt[...] += _dot_tn(do_cat, p_sc[part_rows, :])
            dkt[...] += _dot_tn(q_cat, ds_sc[part_rows, :])

        @pl.when(i == nq - 1)
        def _():
            dk_ref[0, 0] = dkt[...].T
            dv_ref[0, 0] = dvt[...].T

        @pl.when(last)
        def _():
            for cp in remote:
                cp.wait_recv()
            for cp in remote:
                cp.wait_send()
            for cp in local:
                cp.wait()

    qs = pl.BlockSpec((1, 4, tq, HD), lambda b, g, i: (b, g, i, 0))
    ks = pl.BlockSpec((1, 1, n_keys, HD), lambda b, g, i: (b, g, 0, 0))
    vs = pl.BlockSpec((1, 1, n_keys, 2 * HD), lambda b, g, i: (b, g, 0, 0))
    os_ = pl.BlockSpec((1, tq, 4 * HD), lambda b, g, i: (b, i, g))
    kshape = jax.ShapeDtypeStruct(k.shape, F32)
    anyspace = pl.BlockSpec(memory_space=pl.ANY)
    res = pl.pallas_call(
        body, name="attn_bwd", grid=(bl, 2, nq),
        in_specs=[qs, ks, vs, qs, os_, pl.BlockSpec((1, 1, tq, 2 * HD), lambda b, g, i: (b, g, i, 0))]
        + [anyspace] * n_ex,
        out_specs=[os_, ks, ks] + [anyspace] * n_ex,
        out_shape=[jax.ShapeDtypeStruct((bl, s, DA), F32), kshape, kshape]
        + [jax.ShapeDtypeStruct(a.shape, a.dtype) for a in exchange],
        scratch_shapes=[pltpu.VMEM((4 * tq, n_keys), BF16), pltpu.VMEM((4 * tq, n_keys), BF16),
                        pltpu.VMEM((HD, n_keys), F32), pltpu.VMEM((HD, n_keys), F32)] + _exchange_scratch(n_ex),
        compiler_params=_params(("arbitrary", "arbitrary", "arbitrary")),
    )(q, k, v1, do, o, lse, *exchange)
    return res[:3], res[3:]


def _halo_specs(width, s):
    per = TC // HALO
    last = s // HALO - 1
    main = pl.BlockSpec((1, TC, width), lambda b, i: (b, i, 0))
    prev = pl.BlockSpec((1, HALO, width), lambda b, i: (b, jnp.maximum(i * per - 1, 0), 0))
    nxt = pl.BlockSpec((1, HALO, width), lambda b, i: (b, jnp.minimum((i + 1) * per, last), 0))
    return main, prev, nxt


def _glu(g):
    return g[:, 0:DC] * _sigmoid(g[:, DC:2 * DC])


def _fill_padded(pad_ref, main, prev, nxt, first, last):
    pad_ref[0:HALO, :] = jnp.where(first, 0.0, prev)
    pad_ref[HALO:HALO + TC, :] = main
    pad_ref[HALO + TC:2 * HALO + TC, :] = jnp.where(last, 0.0, nxt)


PLANE_ROWS = TC + 2 * HALO - 8


def _shift_planes(pad_ref, planes_ref):
    for r in range(1, 8):
        planes_ref[r - 1] = pad_ref[pl.ds(r, PLANE_ROWS), :]


def _tap_rows(pad_ref, planes_ref, offset, start, n):
    a, r = divmod(offset, 8)
    if r == 0:
        return pad_ref[pl.ds(start + 8 * a, n), :]
    return planes_ref[r - 1, pl.ds(start + 8 * a, n), :]


def _conv_fwd(glu, conv_w, conv_b, ln_w, ln_b, w_pw_b, b_pw):
    bl, s, _ = glu.shape
    nt = s // TC

    def body(g_ref, gp_ref, gn_ref, cw_ref, cb_ref, lw_ref, lb_ref, wpw_ref, bpw_ref, y_ref, cp_ref, pad_ref, planes_ref):
        i = pl.program_id(1)
        _fill_padded(pad_ref, _glu(g_ref[0]), _glu(gp_ref[0]), _glu(gn_ref[0]), i == 0, i == nt - 1)
        _shift_planes(pad_ref, planes_ref)
        for ck in range(TC // CH):
            acc = jnp.zeros((CH, DC), F32) + cb_ref[...]
            for t in range(KW):
                acc = acc + _tap_rows(pad_ref, planes_ref, 1 + t, ck * CH, CH) * cw_ref[t:t + 1, :]
            y_ref[0, pl.ds(ck * CH, CH), :] = acc
        y = y_ref[0]
        mu = jnp.mean(y, axis=-1, keepdims=True)
        yc = y - mu
        var = jnp.mean(yc * yc, axis=-1, keepdims=True)
        z = yc * lax.rsqrt(var + EPS) * lw_ref[...] + lb_ref[...]
        act = z * _sigmoid(z)
        cp_ref[0] = _dot(act.astype(BF16), wpw_ref[...]) + bpw_ref[...]

    def const(shape):
        return pl.BlockSpec(shape, lambda b, i: (0,) * len(shape))

    main, prev, nxt = _halo_specs(2 * DC, s)
    tile = pl.BlockSpec((1, TC, DC), lambda b, i: (b, i, 0))
    return pl.pallas_call(
        body, name="conv_fwd", grid=(bl, nt),
        in_specs=[main, prev, nxt, const((32, DC)), const((1, DC)), const((1, DC)), const((1, DC)),
                  const((DC, DC)), const((1, DC))],
        out_specs=[tile, tile],
        out_shape=[jax.ShapeDtypeStruct((bl, s, DC), F32)] * 2,
        scratch_shapes=[pltpu.VMEM((TC + 2 * HALO, DC), F32), pltpu.VMEM((7, PLANE_ROWS, DC), F32)],
        compiler_params=_params(("arbitrary", "arbitrary")),
    )(glu, glu, glu, conv_w, conv_b, ln_w, ln_b, w_pw_b, b_pw)


def _conv_bwd_pointwise(y, dcp, ln_w, ln_b, w_pw_b):
    bl, s, _ = y.shape
    nt = s // TM

    def body(y_ref, dcp_ref, lw_ref, lb_ref, wpw_ref, dy_ref, gw_ref, rows_ref):
        @pl.when((pl.program_id(0) == 0) & (pl.program_id(1) == 0))
        def _():
            gw_ref[...] = jnp.zeros_like(gw_ref)
            rows_ref[...] = jnp.zeros_like(rows_ref)

        y = y_ref[0]
        dcp = dcp_ref[0]
        mu = jnp.mean(y, axis=-1, keepdims=True)
        yc = y - mu
        rstd = lax.rsqrt(jnp.mean(yc * yc, axis=-1, keepdims=True) + EPS)
        yn = yc * rstd
        lw = lw_ref[...]
        z = yn * lw + lb_ref[...]
        sg = _sigmoid(z)
        act = z * sg
        dcp_b = dcp.astype(BF16)
        gw_ref[...] += _dot_tn(act.astype(BF16), dcp_b)
        dact = _dot_nt(dcp_b, wpw_ref[...])
        dz = dact * (sg * (1.0 + z * (1.0 - sg)))
        dyn = dz * lw
        dy = rstd * (dyn - jnp.mean(dyn, axis=-1, keepdims=True) - yn * jnp.mean(dyn * yn, axis=-1, keepdims=True))
        dy_ref[0] = dy
        rows_ref[0:1, :] += jnp.sum(dcp, axis=0, keepdims=True)
        rows_ref[1:2, :] += jnp.sum(dz * yn, axis=0, keepdims=True)
        rows_ref[2:3, :] += jnp.sum(dz, axis=0, keepdims=True)
        rows_ref[3:4, :] += jnp.sum(dy, axis=0, keepdims=True)

    def const(shape):
        return pl.BlockSpec(shape, lambda b, i: (0,) * len(shape))

    tile = pl.BlockSpec((1, TM, DC), lambda b, i: (b, i, 0))
    return pl.pallas_call(
        body, name="conv_bwd_pointwise", grid=(bl, nt),
        in_specs=[tile, tile, const((1, DC)), const((1, DC)), const((DC, DC))],
        out_specs=[tile, const((DC, DC)), const((8, DC))],
        out_shape=[jax.ShapeDtypeStruct((bl, s, DC), F32), jax.ShapeDtypeStruct((DC, DC), F32),
                   jax.ShapeDtypeStruct((8, DC), F32)],
        compiler_params=_params(("arbitrary", "arbitrary")),
    )(y, dcp, ln_w, ln_b, w_pw_b)


def _conv_bwd_depthwise(glu, dy, conv_w):
    bl, s, _ = glu.shape
    nt = s // TC

    def body(g_ref, gp_ref, gn_ref, d_ref, dp_ref, dn_ref, cw_ref, dglu_ref, dcw_ref,
             padu_ref, padd_ref, planes_u, planes_d):
        i = pl.program_id(1)

        @pl.when((pl.program_id(0) == 0) & (i == 0))
        def _():
            dcw_ref[...] = jnp.zeros_like(dcw_ref)

        first, last = i == 0, i == nt - 1
        _fill_padded(padu_ref, _glu(g_ref[0]), _glu(gp_ref[0]), _glu(gn_ref[0]), first, last)
        _fill_padded(padd_ref, d_ref[0], dp_ref[0], dn_ref[0], first, last)
        _shift_planes(padu_ref, planes_u)
        _shift_planes(padd_ref, planes_d)
        for ck in range(TC // CH):
            acc = jnp.zeros((CH, DC), F32)
            for t in range(KW):
                acc = acc + _tap_rows(padd_ref, planes_d, 2 * HALO - 1 - t, ck * CH, CH) * cw_ref[t:t + 1, :]
            g = g_ref[0, pl.ds(ck * CH, CH), :]
            a = g[:, 0:DC]
            sg = _sigmoid(g[:, DC:2 * DC])
            dglu_ref[0, pl.ds(ck * CH, CH), 0:DC] = (acc * sg).astype(BF16)
            dglu_ref[0, pl.ds(ck * CH, CH), DC:2 * DC] = (acc * a * sg * (1.0 - sg)).astype(BF16)
        group = 4
        for t0 in range(0, KW, group):
            taps = range(t0, min(t0 + group, KW))
            acc8 = [jnp.zeros((8, DC), F32) for _ in taps]
            for ck in range(TC // CH):
                dchunk = d_ref[0, pl.ds(ck * CH, CH), :]
                for n, t in enumerate(taps):
                    prod = _tap_rows(padu_ref, planes_u, 1 + t, ck * CH, CH) * dchunk
                    acc8[n] = acc8[n] + jnp.sum(prod.reshape(CH // 8, 8, DC), axis=0)
            for n, t in enumerate(taps):
                dcw_ref[t:t + 1, :] += jnp.sum(acc8[n], axis=0, keepdims=True)

    gmain, gprev, gnext = _halo_specs(2 * DC, s)
    dmain, dprev, dnext = _halo_specs(DC, s)
    cw = pl.BlockSpec((32, DC), lambda b, i: (0, 0))
    return pl.pallas_call(
        body, name="conv_bwd_depthwise", grid=(bl, nt),
        in_specs=[gmain, gprev, gnext, dmain, dprev, dnext, cw],
        out_specs=[gmain, cw],
        out_shape=[jax.ShapeDtypeStruct((bl, s, 2 * DC), BF16), jax.ShapeDtypeStruct((32, DC), F32)],
        scratch_shapes=[pltpu.VMEM((TC + 2 * HALO, DC), F32)] * 2 + [pltpu.VMEM((7, PLANE_ROWS, DC), F32)] * 2,
        compiler_params=_params(("arbitrary", "arbitrary")),
    )(glu, glu, glu, dy, dy, dy, conv_w)


def _out_fwd_bwd(attn, za, cp, zc, x, target, modrows, w_out_b):
    bl, s, _ = x.shape
    nt = s // TM

    def body(o_ref, za_ref, cp_ref, zc_ref, x_ref, t_ref, mod_ref, w_ref,
             do_ref, dza_ref, dcp_ref, dzc_ref, dh_ref, dgate_ref, gw_ref, loss_ref):
        b, i = pl.program_id(0), pl.program_id(1)

        @pl.when((b == 0) & (i == 0))
        def _():
            gw_ref[...] = jnp.zeros_like(gw_ref)
            loss_ref[...] = jnp.zeros_like(loss_ref)

        @pl.when(i == 0)
        def _():
            dgate_ref[...] = jnp.zeros_like(dgate_ref)

        gate = mod_ref[0, 2:3, :]
        w = w_ref[...]
        mixes, douts = [], []
        for part in range(TOKEN_PARTS):
            rows = pl.ds(part * TM, TM)
            o, za_v, cp_v, zc_v = o_ref[0, rows, :], za_ref[0, rows, :], cp_ref[0, rows, :], zc_ref[0, rows, :]
            sa = _sigmoid(za_v)
            sc = _sigmoid(zc_v)
            silu_a = za_v * sa
            silu_c = zc_v * sc
            mix = jnp.concatenate([(o * silu_a).astype(BF16), (cp_v * silu_c).astype(BF16)], axis=-1)
            out = _dot(mix, w)
            err = x_ref[0, rows, :] + gate * out - t_ref[0, rows, :]
            loss_ref[...] += jnp.sum(err * err, axis=0, keepdims=True)
            dh = err * (1.0 / D)
            dh_ref[0, rows, :] = dh
            dgate_ref[0] += jnp.sum(dh * out, axis=0, keepdims=True)
            dout = (dh * gate).astype(BF16)
            mixes.append(mix)
            douts.append(dout)
            dmix = _dot_nt(dout, w)
            dga = dmix[:, 0:DA]
            dgc = dmix[:, DA:DA + DC]
            dov = dga * silu_a
            for h in range(DA // HD):
                do_ref[0, h, rows, :] = dov[:, h * HD:(h + 1) * HD].astype(BF16)
            dza_ref[0, rows, :] = (dga * o * (sa * (1.0 + za_v * (1.0 - sa)))).astype(BF16)
            dcp_ref[0, rows, :] = dgc * silu_c
            dzc_ref[0, rows, :] = (dgc * cp_v * (sc * (1.0 + zc_v * (1.0 - sc)))).astype(BF16)
        gw_ref[...] += _dot_tn(jnp.concatenate(mixes, axis=0), jnp.concatenate(douts, axis=0))

    def const(shape):
        return pl.BlockSpec(shape, lambda b, i: (0,) * len(shape))

    tm = TOKEN_PARTS * TM

    def tile(w):
        return pl.BlockSpec((1, tm, w), lambda b, i: (b, i, 0))

    return pl.pallas_call(
        body, name="out_fwd_bwd", grid=(bl, s // tm),
        in_specs=[tile(DA), tile(DA), tile(DC), tile(DC), tile(D), tile(D),
                  pl.BlockSpec((1, 3, D), lambda b, i: (b, 0, 0)), const((D, D))],
        out_specs=[pl.BlockSpec((1, DA // HD, tm, HD), lambda b, i: (b, 0, i, 0)), tile(DA), tile(DC), tile(DC), tile(D),
                   pl.BlockSpec((1, 1, D), lambda b, i: (b, 0, 0)), const((D, D)), const((1, D))],
        out_shape=[jax.ShapeDtypeStruct((bl, DA // HD, s, HD), BF16), jax.ShapeDtypeStruct((bl, s, DA), BF16),
                   jax.ShapeDtypeStruct((bl, s, DC), F32), jax.ShapeDtypeStruct((bl, s, DC), BF16),
                   jax.ShapeDtypeStruct((bl, s, D), F32), jax.ShapeDtypeStruct((bl, 1, D), F32),
                   jax.ShapeDtypeStruct((D, D), F32), jax.ShapeDtypeStruct((1, D), F32)],
        compiler_params=_params(("arbitrary", "arbitrary")),
    )(attn, za, cp, zc, x, target, modrows, w_out_b)


def _rms_heads_bwd(dy, x, w_t, ones_bd):
    r = lax.rsqrt(_segsum(x * x, ones_bd) * (1.0 / HD) + EPS)
    xh = x * r
    g = dy * w_t
    dx = r * (g - xh * (_segsum(g * xh, ones_bd) * (1.0 / HD)))
    return dx, dy * xh


def _ctx_bwd(ctx, modc, norm_w, w_kv_b, pkv_c, dk_c, dv_c, knw_t, ones_bd):
    bl, cl, _ = ctx.shape

    def body(x_ref, mod_ref, nw_ref, w_ref, p_ref, dk_ref, dv_ref, knw_ref, bd_ref, gw_ref, rows_ref, dknw_ref):
        @pl.when(pl.program_id(0) == 0)
        def _():
            gw_ref[...] = jnp.zeros_like(gw_ref)
            rows_ref[...] = jnp.zeros_like(rows_ref)
            dknw_ref[...] = jnp.zeros_like(dknw_ref)

        xv = x_ref[0]
        shift = mod_ref[0, 0:1, :]
        scale = mod_ref[0, 1:2, :]
        nw = nw_ref[...]
        r = lax.rsqrt(jnp.mean(xv * xv, axis=-1, keepdims=True) + EPS)
        xn = xv * r
        yv = xn * nw
        u = yv * (1.0 + scale) + shift
        dkv = jnp.concatenate([dk_ref[0, 0], dk_ref[0, 1]], axis=-1)
        dpk, dknw = _rms_heads_bwd(dkv, p_ref[0][:, 0:KVW], knw_ref[...], bd_ref[...])
        dp = jnp.concatenate([dpk.astype(BF16), dv_ref[0, 0].astype(BF16), dv_ref[0, 1].astype(BF16)], axis=-1)
        gw_ref[...] += _dot_tn(dp, u.astype(BF16))
        du = _dot(dp, w_ref[...])
        rows_ref[0:1, :] += jnp.sum(du, axis=0, keepdims=True)
        rows_ref[1:2, :] += jnp.sum(du * yv, axis=0, keepdims=True)
        rows_ref[2:3, :] += jnp.sum(du * (1.0 + scale) * xn, axis=0, keepdims=True)
        dknw_ref[...] += jnp.sum(dknw, axis=0, keepdims=True)

    def const(shape):
        return pl.BlockSpec(shape, lambda b: (0,) * len(shape))

    def tile(w):
        return pl.BlockSpec((1, cl, w), lambda b: (b, 0, 0))

    ctx_block = (dk_c.shape[2] - cl) // cl
    kv_tile = pl.BlockSpec((1, KVW // HD, cl, HD), lambda b: (b, 0, ctx_block, 0))
    return pl.pallas_call(
        body, name="ctx_bwd", grid=(bl,),
        in_specs=[tile(D), const((1, 3, D)), const((1, D)), _KV_ROWS_OF_W_IN_T, tile(2 * KVW), kv_tile, kv_tile,
                  const((1, KVW)), const((KVW, KVW))],
        out_specs=[const((2 * KVW, D)), const((8, D)), const((1, KVW))],
        out_shape=[jax.ShapeDtypeStruct((2 * KVW, D), F32), jax.ShapeDtypeStruct((8, D), F32),
                   jax.ShapeDtypeStruct((1, KVW), F32)],
        compiler_params=_params(("arbitrary",)),
    )(ctx, modc, norm_w, w_kv_b, pkv_c, dk_c, dv_c, knw_t, ones_bd)


def _bwd_in(x, modrows, norm_w, w_in_b, cos, sins, qnw_t, knw_t, ones_bd,
            pq, pkv, dq, dk, dv, dza, dglu, dzc, dh, gw_kv):
    bl, s, _ = x.shape
    tm = TOKEN_PARTS * TM
    nt = s // tm

    def body(x_ref, mod_ref, nw_ref, win_hbm, cos_ref, sin_ref, qnw_ref, knw_ref, bd_ref,
             pq_ref, pkv_ref, dq_ref, dk_ref, dv_ref, dza_ref, dglu_ref, dzc_ref, dh_ref, gwkv_ref,
             gx_ref, gw_hbm, dmod_ref, dnw_ref, dqnw_ref, dknw_ref, win_ref, gw_acc, sem):
        b, i = pl.program_id(0), pl.program_id(1)

        @pl.when((b == 0) & (i == 0))
        def _():
            cp = pltpu.make_async_copy(win_hbm, win_ref, sem)
            cp.start()
            gw_acc[...] = jnp.zeros_like(gw_acc)
            dnw_ref[...] = jnp.zeros_like(dnw_ref)
            dqnw_ref[...] = jnp.zeros_like(dqnw_ref)
            dknw_ref[...] = jnp.zeros_like(dknw_ref)
            cp.wait()

        @pl.when(i == 0)
        def _():
            dmod_ref[...] = jnp.zeros_like(dmod_ref)

        bd = bd_ref[...]
        shift = mod_ref[0, 0:1, :]
        scale = mod_ref[0, 1:2, :]
        nw = nw_ref[...]
        dps, us = [], []
        for part in range(TOKEN_PARTS):
            rows = pl.ds(part * TM, TM)
            ck = cos_ref[rows, :]
            sk = sin_ref[rows, :]
            cs = jnp.concatenate([ck] * (DA // KVW), axis=-1)
            sn = jnp.concatenate([sk] * (DA // KVW), axis=-1)
            dqn = _rope_bwd(dq_ref[0, rows, :], cs, sn)
            dpq, dqnw = _rms_heads_bwd(dqn, pq_ref[0, rows, :], qnw_ref[...], bd)
            dkn = _rope_bwd(jnp.concatenate([dk_ref[0, 0, rows, :], dk_ref[0, 1, rows, :]], axis=-1), ck, sk)
            dpk, dknw = _rms_heads_bwd(dkn, pkv_ref[0, rows, 0:KVW], knw_ref[...], bd[0:KVW, 0:KVW])
            dqnw_ref[...] += jnp.sum(dqnw, axis=0, keepdims=True)
            dknw_ref[...] += jnp.sum(dknw, axis=0, keepdims=True)
            dp = jnp.concatenate(
                [dpq.astype(BF16), dpk.astype(BF16), dv_ref[0, 0, rows, :].astype(BF16), dv_ref[0, 1, rows, :].astype(BF16),
                 dza_ref[0, rows, :], dglu_ref[0, rows, :], dzc_ref[0, rows, :]], axis=-1)

            xv = x_ref[0, rows, :]
            r = lax.rsqrt(jnp.mean(xv * xv, axis=-1, keepdims=True) + EPS)
            xn = xv * r
            yv = xn * nw
            u = yv * (1.0 + scale) + shift
            dps.append(dp)
            us.append(u.astype(BF16))
            du = _dot(dp, win_ref[...])
            dmod_ref[0, 0:1, :] += jnp.sum(du, axis=0, keepdims=True)
            dmod_ref[0, 1:2, :] += jnp.sum(du * yv, axis=0, keepdims=True)
            dy = du * (1.0 + scale)
            dnw_ref[...] += jnp.sum(dy * xn, axis=0, keepdims=True)
            dxn = dy * nw
            gx_ref[0, rows, :] = dh_ref[0, rows, :] + r * (dxn - xn * jnp.mean(dxn * xn, axis=-1, keepdims=True))
        gw_acc[...] += _dot_tn(jnp.concatenate(dps, axis=0), jnp.concatenate(us, axis=0))

        @pl.when((b == bl - 1) & (i == nt - 1))
        def _():
            gw_acc[DA:DA + 2 * KVW, :] += gwkv_ref[...]

            def to_bf16(j, carry):
                rows = pl.ds(pl.multiple_of(j * 2 * KVW, 2 * KVW), 2 * KVW)
                win_ref[rows, :] = gw_acc[rows, :].astype(BF16)
                return carry

            lax.fori_loop(0, D_IN // (2 * KVW), to_bf16, 0)
            pltpu.sync_copy(win_ref, gw_hbm)

    def tile(w):
        return pl.BlockSpec((1, tm, w), lambda b, i: (b, i, 0))

    def const(shape):
        return pl.BlockSpec(shape, lambda b, i: (0,) * len(shape))

    anyspace = pl.BlockSpec(memory_space=pl.ANY)
    rope = pl.BlockSpec((tm, KVW), lambda b, i: (i, 0))
    kv_tile = pl.BlockSpec((1, KVW // HD, tm, HD), lambda b, i: (b, 0, i, 0))
    return pl.pallas_call(
        body, name="bwd_in", grid=(bl, nt),
        in_specs=[tile(D), pl.BlockSpec((1, 3, D), lambda b, i: (b, 0, 0)), const((1, D)), anyspace, rope, rope,
                  const((1, DA)), const((1, KVW)), const((DA, DA)),
                  tile(DA), tile(2 * KVW), tile(DA), kv_tile, kv_tile, tile(DA), tile(2 * DC), tile(DC), tile(D),
                  const((2 * KVW, D))],
        out_specs=[tile(D), anyspace, pl.BlockSpec((1, 2, D), lambda b, i: (b, 0, 0)), const((1, D)),
                   const((1, DA)), const((1, KVW))],
        out_shape=[jax.ShapeDtypeStruct((bl, s, D), F32), jax.ShapeDtypeStruct((D_IN, D), BF16),
                   jax.ShapeDtypeStruct((bl, 2, D), F32), jax.ShapeDtypeStruct((1, D), F32),
                   jax.ShapeDtypeStruct((1, DA), F32), jax.ShapeDtypeStruct((1, KVW), F32)],
        scratch_shapes=[pltpu.VMEM((D_IN, D), BF16), pltpu.VMEM((D_IN, D), F32), pltpu.SemaphoreType.DMA],
        compiler_params=_params(("arbitrary", "arbitrary")),
    )(x, modrows, norm_w, w_in_b, cos, sins, qnw_t, knw_t, ones_bd,
      pq, pkv, dq, dk, dv, dza, dglu, dzc, dh, gw_kv)


_LOSS, _DMODC, _NW, _QN, _KN, _CB, _LW, _LB, _BPW, SMALL_W = 0, 1024, 4096, 5120, 5248, 5376, 5888, 6400, 6912, 7424


ROW_W = 1792


def _put_flat(ref, off, value):
    n, done = value.shape[1], 0
    while done < n:
        r, c = divmod(off + done, ROW_W)
        take = min(n - done, ROW_W - c)
        ref[r:r + 1, c:c + take] = value[:, done:done + take]
        done += take


def _get_flat(arr, off, n):
    parts, done = [], 0
    while done < n:
        r, c = divmod(off + done, ROW_W)
        take = min(n - done, ROW_W - c)
        parts.append(arr[r:r + 1, c:c + take])
        done += take
    return parts[0] if len(parts) == 1 else jnp.concatenate(parts, axis=-1)


def _pack_small_body(loss_ref, ctx_ref, dnw_ref, dqnw_ref, dknw_ref, dknwc_ref, conv_ref, dss_ref, dgate_ref, o_ref):
    bl = dss_ref.shape[0]
    assert SMALL_W + bl * 3 * D <= 8 * ROW_W
    o_ref[...] = jnp.zeros_like(o_ref)
    _put_flat(o_ref, _LOSS, loss_ref[...])
    _put_flat(o_ref, _DMODC, ctx_ref[0:1, :])
    _put_flat(o_ref, _DMODC + D, ctx_ref[1:2, :])
    _put_flat(o_ref, _NW, dnw_ref[...] + ctx_ref[2:3, :])
    dq = dqnw_ref[...]
    qn = dq[:, 0:HD]
    for h in range(1, DA // HD):
        qn = qn + dq[:, h * HD:(h + 1) * HD]
    _put_flat(o_ref, _QN, qn)
    dk = dknw_ref[...] + dknwc_ref[...]
    _put_flat(o_ref, _KN, dk[:, 0:HD] + dk[:, HD:2 * HD])
    _put_flat(o_ref, _BPW, conv_ref[0:1, :])
    _put_flat(o_ref, _LW, conv_ref[1:2, :])
    _put_flat(o_ref, _LB, conv_ref[2:3, :])
    _put_flat(o_ref, _CB, conv_ref[3:4, :])
    for b in range(bl):
        _put_flat(o_ref, SMALL_W + b * 3 * D, dss_ref[b, 0:1, :])
        _put_flat(o_ref, SMALL_W + b * 3 * D + D, dss_ref[b, 1:2, :])
        _put_flat(o_ref, SMALL_W + b * 3 * D + 2 * D, dgate_ref[b])


_SMALL = (("b_mod", None), ("norm_w", _NW), ("q_norm_w", _QN), ("k_norm_w", _KN), ("conv_b", _CB),
          ("conv_ln_w", _LW), ("conv_ln_b", _LB), ("b_pw", _BPW), ("c_ctx", None))


def _epilogue(parts_in, pieces, c_rows, w_mod_loc):
    bl = pieces[7].shape[0]
    n_ex = N_DEV * bl
    n_mod = w_mod_loc.shape[1]
    rb = 32
    shp = parts_in.shape[1:]
    rows_in = shp[0]

    def body(*refs):
        it = iter(refs)
        take = lambda k: [next(it) for _ in range(k)]
        (parts,) = take(1)
        piece_refs = take(9)
        (c_ref, wm_ref) = take(2)
        (g_in, g_wm, sum_ref, gb_ref, gc_all, loss_ref) = take(6)
        (mine, got_sib, stage, got_chip, payload, gathered, dmod_full, gc_mine) = take(8)
        (d2d_send, d2d_recv, ici_send, ici_recv, local_sems, sg_send, sg_recv, gc_send, gc_recv, misc_sems) = take(10)

        x, y, c = _coords()
        me = _lin(x, y, c)
        sib = (x, y, 1 - c)
        peers = [(1 - x, y), (x, 1 - y), (1 - x, 1 - y)]
        home = 2 * x + y

        def rows_loop(fn):
            def step(i, carry):
                fn(pl.ds(pl.multiple_of(i * rb, rb), rb))
                return carry
            lax.fori_loop(0, rows_in // rb, step, 0)

        def direct_gather(src, dst, send_sems, recv_sems, local_sem):
            cps = [pltpu.make_async_copy(src, dst.at[me], local_sem)]
            for k in range(1, N_DEV):
                peer = (1 - x if k & 4 else x, 1 - y if k & 2 else y, 1 - c if k & 1 else c)
                cps.append(pltpu.make_async_remote_copy(
                    src_ref=src, dst_ref=dst.at[me], send_sem=send_sems.at[k - 1], recv_sem=recv_sems.at[k - 1],
                    device_id=peer, device_id_type=MESH_ID))
            for cp in cps:
                cp.start()
            return cps

        _pack_small_body(*piece_refs, payload)
        small_cps = direct_gather(payload, gathered, sg_send, sg_recv, misc_sems.at[0])

        local, d2d, ici = [], [], []
        for s in range(4):
            cp = pltpu.make_async_copy(parts.at[_lin(s // 2, s % 2, c)], mine.at[s], local_sems.at[s])
            cp.start()
            local.append(cp)
            rc = pltpu.make_async_remote_copy(
                src_ref=parts.at[_lin(s // 2, s % 2, 1 - c)], dst_ref=got_sib.at[s],
                send_sem=d2d_send.at[s], recv_sem=d2d_recv.at[s], device_id=sib, device_id_type=MESH_ID)
            rc.start()
            d2d.append(rc)

        for cp in small_cps[1:]:
            cp.wait_recv()
        small_cps[0].wait()
        tot = gathered[0]
        for j in range(1, N_DEV):
            tot = tot + gathered[j]
        summed = _get_flat(tot, 0, SMALL_W)
        dmod_full[...] = jnp.zeros_like(dmod_full)
        for j in range(N_DEV):
            arr = gathered[j]
            for b in range(bl):
                dmod_full[j * bl + b:j * bl + b + 1, :] = _get_flat(arr, SMALL_W + b * 3 * D, 3 * D)
        dmod_full[n_ex:n_ex + 1, :] = summed[:, _DMODC:_DMODC + 3 * D]
        sum_ref[...] = summed
        gb_ref[...] = jnp.sum(dmod_full[...], axis=0, keepdims=True)
        loss_ref[...] = (0.5 / D) * jnp.sum(summed[:, _LOSS:_LOSS + D], axis=-1, keepdims=True)

        for s in range(4):
            local[s].wait()
            d2d[s].wait_recv()
        for k, (px, py) in enumerate(peers):
            slot = 2 * px + py

            def pair_sum(rs, k=k, slot=slot):
                stage[k, rs, :] = (mine[slot, rs, :].astype(F32) + got_sib[slot, rs, :].astype(F32)).astype(BF16)

            rows_loop(pair_sum)
            rc = pltpu.make_async_remote_copy(
                src_ref=stage.at[k], dst_ref=got_chip.at[k], send_sem=ici_send.at[k], recv_sem=ici_recv.at[k],
                device_id=(px, py, c), device_id_type=MESH_ID)
            rc.start()
            ici.append(rc)

        cr = c_ref[...]
        act = (cr * _sigmoid(cr)).astype(BF16)
        dm = dmod_full[:, pl.ds(pl.multiple_of(me * n_mod, 128), n_mod)].astype(BF16)
        g_wm[...] = _dot_tn(act, dm)
        gc_mine[...] = _dot_nt(dm[n_ex:n_ex + 8, :], wm_ref[...].astype(BF16))
        gc_cps = direct_gather(gc_mine, gc_all, gc_send, gc_recv, misc_sems.at[1])

        for rc in ici:
            rc.wait_recv()

        def finish(rs):
            gsum = mine[home, rs, :].astype(F32) + got_sib[home, rs, :].astype(F32)
            for k in range(3):
                gsum = gsum + got_chip[k, rs, :].astype(F32)
            g_in[rs, :] = gsum

        rows_loop(finish)

        for cp in gc_cps[1:]:
            cp.wait_recv()
        gc_cps[0].wait()
        for rc in d2d + ici + small_cps[1:] + gc_cps[1:]:
            rc.wait_send()

    vm = pl.BlockSpec(memory_space=pltpu.VMEM)
    anyspace = pl.BlockSpec(memory_space=pl.ANY)
    assert rows_in % rb == 0 and parts_in.dtype == BF16
    args = [parts_in, *pieces, c_rows, w_mod_loc]
    in_specs = [anyspace] + [vm] * (len(args) - 1)
    out_shape = [jax.ShapeDtypeStruct(shp, F32), jax.ShapeDtypeStruct(w_mod_loc.shape, F32),
                 jax.ShapeDtypeStruct((1, SMALL_W), F32), jax.ShapeDtypeStruct((1, 3 * D), F32),
                 jax.ShapeDtypeStruct((N_DEV, 8, D), F32), jax.ShapeDtypeStruct((1, 1), F32)]
    scratch = [pltpu.VMEM((4,) + shp, BF16), pltpu.VMEM((4,) + shp, BF16), pltpu.VMEM((3,) + shp, BF16),
               pltpu.VMEM((3,) + shp, BF16), pltpu.VMEM((8, ROW_W), F32), pltpu.VMEM((N_DEV, 8, ROW_W), F32),
               pltpu.VMEM((n_ex + 8, 3 * D), F32), pltpu.VMEM((8, D), F32),
               pltpu.SemaphoreType.DMA((4,)), pltpu.SemaphoreType.DMA((4,)), pltpu.SemaphoreType.DMA((3,)),
               pltpu.SemaphoreType.DMA((3,)), pltpu.SemaphoreType.DMA((4,)),
               pltpu.SemaphoreType.DMA((N_DEV - 1,)), pltpu.SemaphoreType.DMA((N_DEV - 1,)),
               pltpu.SemaphoreType.DMA((N_DEV - 1,)), pltpu.SemaphoreType.DMA((N_DEV - 1,)),
               pltpu.SemaphoreType.DMA((2,))]
    return pl.pallas_call(
        body, name="epilogue", out_shape=out_shape, in_specs=in_specs, out_specs=[vm] * len(out_shape),
        scratch_shapes=scratch, compiler_params=pltpu.CompilerParams(vmem_limit_bytes=VMEM_LIMIT),
    )(*args)


def _final_adamw(g_in, w_in_t, m_in_t, v_in_t, g_wm, w_mod_loc, m_mod, v_mod, summed, g_bmod, gc_all,
                 small_w, small_m, small_v):
    ns = len(_SMALL)
    rb = 32

    def body(*refs):
        it = iter(refs)
        take = lambda k: [next(it) for _ in range(k)]
        (gin_ref, w_ref, m_ref, v_ref, gwm_ref, wm_ref, mm_ref, vm_ref, sum_ref, gb_ref, gc_ref) = take(11)
        sw, sm, sv = take(ns), take(ns), take(ns)
        (d_in, nm_in, nv_in, d_wm, nm_wm, nv_wm) = take(6)
        souts = take(4 * ns)

        def big(g_r, w_r, m_r, v_r, d_o, nm_o, nv_o):
            def step(i, carry):
                rs = pl.ds(pl.multiple_of(i * rb, rb), rb)
                dl, m_new, v_new = _adamw(w_r[rs, :], g_r[rs, :], m_r[rs, :], v_r[rs, :])
                d_o[rs, :] = dl
                nm_o[rs, :] = m_new
                nv_o[rs, :] = v_new
                return carry
            lax.fori_loop(0, w_r.shape[0] // rb, step, 0)

        big(gin_ref, w_ref, m_ref, v_ref, d_in, nm_in, nv_in)
        big(gwm_ref, wm_ref, mm_ref, vm_ref, d_wm, nm_wm, nv_wm)
        for k, (name, off) in enumerate(_SMALL):
            w = sw[k][...]
            if name == "b_mod":
                gk = gb_ref[...]
            elif name == "c_ctx":
                acc = gc_ref[0, 0:1, :]
                for j in range(1, N_DEV):
                    acc = acc + gc_ref[j, 0:1, :]
                sg = _sigmoid(w)
                gk = acc * (sg * (1.0 + w * (1.0 - sg)))
            else:
                gk = sum_ref[:, off:off + w.shape[1]]
            dl, m_new, v_new = _adamw(w, gk, sm[k][...], sv[k][...])
            souts[k][...] = gk
            souts[ns + k][...] = dl
            souts[2 * ns + k][...] = m_new
            souts[3 * ns + k][...] = v_new

    assert w_in_t.shape[0] % rb == 0 and w_mod_loc.shape[0] % rb == 0
    big_shape = jax.ShapeDtypeStruct(w_in_t.shape, F32)
    mod_shape = jax.ShapeDtypeStruct(w_mod_loc.shape, F32)
    out_shape = [big_shape] * 3 + [mod_shape] * 3 + [jax.ShapeDtypeStruct(w.shape, F32) for w in small_w] * 4
    outs = pl.pallas_call(
        body, name="final_adamw", out_shape=out_shape,
        compiler_params=pltpu.CompilerParams(vmem_limit_bytes=VMEM_LIMIT),
    )(g_in, w_in_t, m_in_t, v_in_t, g_wm, w_mod_loc, m_mod, v_mod, summed, g_bmod, gc_all, *small_w, *small_m, *small_v)
    small_outs = [outs[6 + k * ns:6 + (k + 1) * ns] for k in range(4)]
    return (g_in,) + tuple(outs[0:3]), (g_wm,) + tuple(outs[3:6]), small_outs


def _rope_tables(s):
    t = jnp.arange(s, dtype=jnp.int32)
    row = (t // GRID_W).astype(F32)
    col = (t % GRID_W).astype(F32)
    freqs = ROPE_THETA ** (-jnp.arange(0, HD // 2, 2, dtype=F32) / (HD // 2))
    ang_r = row[:, None] * freqs[None, :]
    ang_c = col[:, None] * freqs[None, :]
    cr, sr, cc, sc = jnp.cos(ang_r), jnp.sin(ang_r), jnp.cos(ang_c), jnp.sin(ang_c)
    cos = jnp.concatenate([cr, cr, cc, cc], axis=-1)
    sins = jnp.concatenate([-sr, sr, -sc, sc], axis=-1)
    return jnp.tile(cos, (1, KVW // HD)), jnp.tile(sins, (1, KVW // HD))


def kernel(x, c, ctx, c_ctx, w_mod, b_mod, norm_w, w_in, q_norm_w, k_norm_w, conv_w, conv_b, conv_ln_w, conv_ln_b, w_pw, b_pw, w_out, loss_target, m_c_ctx, m_w_mod, m_b_mod, m_norm_w, m_w_in, m_q_norm_w, m_k_norm_w, m_conv_w, m_conv_b, m_conv_ln_w, m_conv_ln_b, m_w_pw, m_b_pw, m_w_out, v_c_ctx, v_w_mod, v_b_mod, v_norm_w, v_w_in, v_q_norm_w, v_k_norm_w, v_conv_w, v_conv_b, v_conv_ln_w, v_conv_ln_b, v_w_pw, v_b_pw, v_w_out):
    bl, s, _ = x.shape
    cl = ctx.shape[1]
    me = _lin(*_coords())
    n_mod = w_mod.shape[2]

    conv_w_pad = jnp.pad(conv_w[0], ((0, 32 - KW), (0, 0)))
    c_pad = jnp.pad(c, ((0, 8 - bl), (0, 0)))
    g_win, g_c = _all_gather_many([w_in[0].T.astype(BF16), c_pad], "gather_weights")
    w_in_b = g_win.reshape(D_IN, D)
    c_all = g_c[:, :bl, :].reshape(N_DEV * bl, D)
    n_ex = N_DEV * bl
    c_rows = jnp.concatenate([c_all, c_ctx[None, :], jnp.zeros((7, D), F32)], axis=0)

    b_mod_loc = lax.dynamic_slice_in_dim(b_mod, me * n_mod, n_mod, axis=1)
    mod_loc = _mod_fwd(c_rows, w_mod[0], b_mod_loc)
    g_mod = _all_gather_direct(mod_loc, "gather_mod")
    mod_all = g_mod.transpose(1, 0, 2).reshape(n_ex + 8, 3 * D)
    modrows = lax.dynamic_slice_in_dim(mod_all, me * bl, bl, axis=0).reshape(bl, 3, D)
    modc = mod_all[n_ex].reshape(1, 3, D)

    cos, sins = _rope_tables(s)
    qnw_t = jnp.tile(q_norm_w, (1, DA // HD))
    knw_t = jnp.tile(k_norm_w, (1, KVW // HD))
    lane = jnp.arange(DA, dtype=jnp.int32) // HD
    ones_bd = (lane[:, None] == lane[None, :]).astype(BF16)
    ones_kv = ones_bd[0:KVW, 0:KVW]
    w_kv_b = w_in_b

    k_ctx, v_ctx, pkv_c = _ctx_fwd(ctx, modc, norm_w, w_kv_b, knw_t, ones_kv, cl + s)
    (q_h, k_h, v_h, pq, pkv, za, glu, zc), (g_wout, g_wpw, g_cw) = _fwd_in(
        x, modrows, norm_w, w_in_b, cos, sins, qnw_t, knw_t, ones_bd, k_ctx, v_ctx,
        [w_out[0].astype(BF16), w_pw[0].astype(BF16), conv_w_pad])
    w_out_b = g_wout.reshape(D, D)
    w_pw_b = g_wpw.reshape(DC, DC)
    conv_w_full = g_cw.transpose(1, 0, 2).reshape(32, DC)
    attn, lse = _attn_fwd(q_h, k_h, v_h)
    y_conv, cp = _conv_fwd(glu, conv_w_full, conv_b, conv_ln_w, conv_ln_b, w_pw_b, b_pw)

    do_h, dza, dcp, dzc, dh, dgate, gw_out, loss_row = _out_fwd_bwd(attn, za, cp, zc, x, loss_target, modrows, w_out_b)
    dy_conv, gw_pw, conv_rows = _conv_bwd_pointwise(y_conv, dcp, conv_ln_w, conv_ln_b, w_pw_b)
    dglu, g_cw_full = _conv_bwd_depthwise(glu, dy_conv, conv_w_full)
    parts_out = gw_out.astype(BF16).reshape(N_DEV, D // N_DEV, D)
    parts_pw = gw_pw.astype(BF16).reshape(N_DEV, DC // N_DEV, DC)
    parts_cw = g_cw_full.astype(BF16).reshape(32, N_DEV, DC // N_DEV).transpose(1, 0, 2)
    (dq, dk_h, dv_h), (got_out, got_pw, got_cw) = _attn_bwd(
        q_h, k_h, v_h, do_h, attn, lse, [parts_out, parts_pw, parts_cw])
    gw_kv, ctx_rows, dknw_c = _ctx_bwd(ctx, modc, norm_w, w_kv_b, pkv_c, dk_h, dv_h, knw_t, ones_kv)
    grad_x, gw_in, dmod_ss, dnw, dqnw, dknw = _bwd_in(
        x, modrows, norm_w, w_in_b, cos, sins, qnw_t, knw_t, ones_bd,
        pq, pkv, dq, dk_h, dv_h, dza, dglu, dzc, dh, gw_kv)

    def pad_cw(a):
        return jnp.pad(a[0], ((0, 32 - KW), (0, 0)))

    r_out, r_pw, r_cw = _sum_devices_adamw(
        [(got_out, w_out[0], m_w_out[0], v_w_out[0]), (got_pw, w_pw[0], m_w_pw[0], v_w_pw[0]),
         (got_cw, pad_cw(conv_w), pad_cw(m_conv_w), pad_cw(v_conv_w))])
    r_cw = tuple(a[:KW] for a in r_cw)

    given = {"c_ctx": (c_ctx, m_c_ctx, v_c_ctx), "b_mod": (b_mod, m_b_mod, v_b_mod), "norm_w": (norm_w, m_norm_w, v_norm_w),
             "q_norm_w": (q_norm_w, m_q_norm_w, v_q_norm_w), "k_norm_w": (k_norm_w, m_k_norm_w, v_k_norm_w),
             "conv_b": (conv_b, m_conv_b, v_conv_b), "conv_ln_w": (conv_ln_w, m_conv_ln_w, v_conv_ln_w),
             "conv_ln_b": (conv_ln_b, m_conv_ln_b, v_conv_ln_b), "b_pw": (b_pw, m_b_pw, v_b_pw)}
    as_rows = [[given[name][which].reshape(1, -1) for name, _ in _SMALL] for which in range(3)]
    g_in_t, g_wmod, summed, g_bmod, gc_all, loss11 = _epilogue(
        gw_in.reshape(N_DEV, D_IN // N_DEV, D),
        [loss_row, ctx_rows, dnw, dqnw, dknw, dknw_c, conv_rows, dmod_ss, dgate], c_rows, w_mod[0])
    r_in, r_wmod, small_outs = _final_adamw(
        g_in_t, w_in[0].T, m_w_in[0].T, v_w_in[0].T, g_wmod, w_mod[0], m_w_mod[0], v_w_mod[0],
        summed, g_bmod, gc_all, *as_rows)
    r_in = tuple(a.T for a in r_in)

    big = {"w_mod": r_wmod, "w_in": r_in, "conv_w": r_cw, "w_pw": r_pw, "w_out": r_out}
    order = ["c_ctx", "w_mod", "b_mod", "norm_w", "w_in", "q_norm_w", "k_norm_w", "conv_w", "conv_b", "conv_ln_w",
             "conv_ln_b", "w_pw", "b_pw", "w_out"]
    small_index = {name: k for k, (name, _) in enumerate(_SMALL)}
    outs = [loss11.reshape(()), grad_x]
    for which in range(4):
        for name in order:
            if name in big:
                outs.append(big[name][which][None])
            else:
                outs.append(small_outs[which][small_index[name]].reshape(given[name][0].shape))
    return tuple(outs)
```

```python
import functools

import jax
import jax.numpy as jnp
from jax import lax
from jax.experimental import pallas as pl
from jax.experimental.pallas import tpu as pltpu

F32, BF16 = jnp.float32, jnp.bfloat16
MESH_ID = pl.DeviceIdType.MESH

N_DEV = 8
D = 1024
D_IN = 2816
DA = 512
DC = 512
HD = 64
KVW = 128
KW = 31
HALO = 16
EPS = 1e-6
ROPE_THETA = 10000.0
GRID_W = 64

ADAM_LR, ADAM_B1, ADAM_B2, ADAM_EPS, ADAM_WD, ADAM_STEP = 0.001, 0.9, 0.999, 1e-08, 0.01, 10

VMEM_LIMIT = 56 * 1024 * 1024

TM = 256
TQ = 128
TOKEN_PARTS = 2
BWD_PARTS = 2
FWD_PARTS = 4
TC = 512
CH = 64


def _params(sem, vmem=VMEM_LIMIT):
    return pltpu.CompilerParams(dimension_semantics=sem, vmem_limit_bytes=vmem)


def _dot(a, b):
    return jnp.dot(a, b, preferred_element_type=F32)


def _dot_nt(a, b):
    return lax.dot_general(a, b, (((1,), (1,)), ((), ())), preferred_element_type=F32)


def _dot_tn(a, b):
    return lax.dot_general(a, b, (((0,), (0,)), ((), ())), preferred_element_type=F32)


def _sigmoid(z):
    return 1.0 / (1.0 + jnp.exp(-z))


def _segsum(v, ones_bd):
    return _dot(v.astype(BF16), ones_bd)


def _swap16(x):
    w = x.shape[-1]
    lane = lax.broadcasted_iota(jnp.int32, x.shape, 1)
    return jnp.where((lane % 32) < 16, pltpu.roll(x, w - 16, 1), pltpu.roll(x, 16, 1))


def _with_ones_column(v):
    one = (lax.broadcasted_iota(jnp.int32, v.shape, 1) == 0).astype(v.dtype)
    return jnp.concatenate([v, one], axis=-1)


def _rope(x, cos, sins):
    return x * cos + _swap16(x) * sins


def _rope_bwd(d, cos, sins):
    return d * cos + _swap16(d * sins)


def _adamw(w, g, m, v):
    m2 = ADAM_B1 * m + (1.0 - ADAM_B1) * g
    v2 = ADAM_B2 * v + (1.0 - ADAM_B2) * (g * g)
    m_hat = m2 / (1.0 - ADAM_B1 ** ADAM_STEP)
    v_hat = v2 / (1.0 - ADAM_B2 ** ADAM_STEP)
    delta = -ADAM_LR * (m_hat / (jnp.sqrt(v_hat) + ADAM_EPS) + ADAM_WD * w)
    return delta, m2, v2


def _coords():
    return lax.axis_index("x"), lax.axis_index("y"), lax.axis_index("c")


def _lin(x, y, c):
    return 4 * x + 2 * y + c


def _all_gather_many(arrs, name):
    n = len(arrs)

    def body(*refs):
        in_refs, out_refs = refs[:n], refs[n:2 * n]
        send_sems, recv_sems, local_sems = refs[2 * n:]
        x, y, c = _coords()
        me, sib = (x, y, c), (x, y, 1 - c)
        xnb, ynb, diag = (1 - x, y), (x, 1 - y), (1 - x, 1 - y)
        north = c == 1

        def copy(a, k, block, to, src=None):
            slot = out_refs[a].at[_lin(*block)]
            return pltpu.make_async_remote_copy(
                src_ref=slot if src is None else src, dst_ref=slot,
                send_sem=send_sems.at[a * 7 + k], recv_sem=recv_sems.at[a * 7 + k],
                device_id=to, device_id_type=MESH_ID)

        mine = [pltpu.make_async_copy(in_refs[a], out_refs[a].at[_lin(*me)], local_sems.at[a]) for a in range(n)]
        for cp in mine:
            cp.start()
        first = []
        for a in range(n):
            first += [copy(a, 0, me, sib, src=in_refs[a]), copy(a, 1, me, (*xnb, c), src=in_refs[a]),
                      copy(a, 2, me, (*ynb, c), src=in_refs[a])]
        for cp in first:
            cp.start()
        passed, relays = [], []
        for a in range(n):
            relay_north = copy(a, 3, (*xnb, c), (*ynb, c))
            relay_south = copy(a, 3, (*ynb, c), (*xnb, c))
            copy(a, 1, (*xnb, c), me).wait_recv()
            pl.when(north)(relay_north.start)
            passed.append(copy(a, 4, (*xnb, c), sib))
            passed[-1].start()
            copy(a, 2, (*ynb, c), me).wait_recv()
            pl.when(jnp.logical_not(north))(relay_south.start)
            passed.append(copy(a, 5, (*ynb, c), sib))
            passed[-1].start()
            relays.append(relay_north)
        for a in range(n):
            copy(a, 3, (*diag, c), me).wait_recv()
            passed.append(copy(a, 6, (*diag, c), sib))
            passed[-1].start()
        for a in range(n):
            copy(a, 0, sib, me).wait_recv()
            for k, chip in ((4, xnb), (5, ynb), (6, diag)):
                copy(a, k, (*chip, 1 - c), me).wait_recv()
        for cp in first + passed + relays:
            cp.wait_send()
        for cp in mine:
            cp.wait()

    vm = pl.BlockSpec(memory_space=pltpu.VMEM)
    return pl.pallas_call(
        body, name=name,
        out_shape=[jax.ShapeDtypeStruct((N_DEV,) + a.shape, a.dtype) for a in arrs],
        in_specs=[vm] * n, out_specs=[vm] * n,
        scratch_shapes=[pltpu.SemaphoreType.DMA((7 * n,)), pltpu.SemaphoreType.DMA((7 * n,)),
                        pltpu.SemaphoreType.DMA((n,))],
        compiler_params=pltpu.CompilerParams(vmem_limit_bytes=VMEM_LIMIT),
    )(*arrs)


def _exchange_copies(in_refs, out_refs, send_sems, recv_sems, local_sems, scatter):
    x, y, c = _coords()
    me = _lin(x, y, c)
    local, remote = [], []
    for a, (src, dst) in enumerate(zip(in_refs, out_refs)):
        local.append(pltpu.make_async_copy(src.at[me] if scatter else src, dst.at[me], local_sems.at[a]))
        for k in range(1, N_DEV):
            peer = (1 - x if k & 4 else x, 1 - y if k & 2 else y, 1 - c if k & 1 else c)
            remote.append(pltpu.make_async_remote_copy(
                src_ref=src.at[_lin(*peer)] if scatter else src, dst_ref=dst.at[me],
                send_sem=send_sems.at[a * (N_DEV - 1) + k - 1], recv_sem=recv_sems.at[a * (N_DEV - 1) + k - 1],
                device_id=peer, device_id_type=MESH_ID))
    return local, remote


def _exchange_scratch(n):
    return [pltpu.SemaphoreType.DMA((n * (N_DEV - 1),)), pltpu.SemaphoreType.DMA((n * (N_DEV - 1),)),
            pltpu.SemaphoreType.DMA((n,))]


def _all_gather_direct(arr, name):
    def body(in_ref, out_ref, send_sems, recv_sems, local_sem):
        x, y, c = _coords()
        me = _lin(x, y, c)
        mine = pltpu.make_async_copy(in_ref, out_ref.at[me], local_sem)
        mine.start()
        copies = []
        for k in range(1, N_DEV):
            peer = (1 - x if k & 4 else x, 1 - y if k & 2 else y, 1 - c if k & 1 else c)
            cp = pltpu.make_async_remote_copy(
                src_ref=in_ref, dst_ref=out_ref.at[me], send_sem=send_sems.at[k - 1], recv_sem=recv_sems.at[k - 1],
                device_id=peer, device_id_type=MESH_ID)
            cp.start()
            copies.append(cp)
        for cp in copies:
            cp.wait_recv()
        for cp in copies:
            cp.wait_send()
        mine.wait()

    vm = pl.BlockSpec(memory_space=pltpu.VMEM)
    return pl.pallas_call(
        body, name=name, out_shape=jax.ShapeDtypeStruct((N_DEV,) + arr.shape, arr.dtype),
        in_specs=[vm], out_specs=vm,
        scratch_shapes=[pltpu.SemaphoreType.DMA((N_DEV - 1,)), pltpu.SemaphoreType.DMA((N_DEV - 1,)),
                        pltpu.SemaphoreType.DMA],
    )(arr)


def _reduce_scatter_adamw(items, name):
    n = len(items)
    rb = 32

    def body(*refs):
        parts = refs[0:n]
        wmv = refs[n:4 * n]
        outs = refs[4 * n:8 * n]
        bufs = [refs[8 * n + 4 * a:8 * n + 4 * a + 4] for a in range(n)]
        d2d_send, d2d_recv, ici_send, ici_recv, local_sems = refs[12 * n:]
        x, y, c = _coords()
        sib = (x, y, 1 - c)
        peers = [(1 - x, y), (x, 1 - y), (1 - x, 1 - y)]
        home = 2 * x + y

        def rows_loop(rows, fn):
            def step(i, carry):
                fn(pl.ds(pl.multiple_of(i * rb, rb), rb))
                return carry
            lax.fori_loop(0, rows // rb, step, 0)

        local, d2d, ici = [], [], []
        for a in range(n):
            mine, got_sib = bufs[a][0], bufs[a][1]
            for s in range(4):
                cp = pltpu.make_async_copy(parts[a].at[_lin(s // 2, s % 2, c)], mine.at[s], local_sems.at[4 * a + s])
                cp.start()
                local.append(cp)
                rc = pltpu.make_async_remote_copy(
                    src_ref=parts[a].at[_lin(s // 2, s % 2, 1 - c)], dst_ref=got_sib.at[s],
                    send_sem=d2d_send.at[4 * a + s], recv_sem=d2d_recv.at[4 * a + s],
                    device_id=sib, device_id_type=MESH_ID)
                rc.start()
                d2d.append(rc)

        for a in range(n):
            mine, got_sib, stage, got_chip = bufs[a]
            for s in range(4):
                local[4 * a + s].wait()
                d2d[4 * a + s].wait_recv()
            for k, (px, py) in enumerate(peers):
                slot = 2 * px + py

                def pair_sum(rs, k=k, slot=slot, mine=mine, got_sib=got_sib, stage=stage):
                    stage[k, rs, :] = (mine[slot, rs, :].astype(F32) + got_sib[slot, rs, :].astype(F32)).astype(BF16)

                rows_loop(wmv[3 * a].shape[0], pair_sum)
                rc = pltpu.make_async_remote_copy(
                    src_ref=stage.at[k], dst_ref=got_chip.at[k],
                    send_sem=ici_send.at[3 * a + k], recv_sem=ici_recv.at[3 * a + k],
                    device_id=(px, py, c), device_id_type=MESH_ID)
                rc.start()
                ici.append(rc)

        for a in range(n):
            mine, got_sib, stage, got_chip = bufs[a]
            w_ref, m_ref, v_ref = wmv[3 * a:3 * a + 3]
            g_ref, d_ref, nm_ref, nv_ref = outs[4 * a:4 * a + 4]
            for k in range(3):
                ici[3 * a + k].wait_recv()

            def finish(rs, mine=mine, got_sib=got_sib, got_chip=got_chip, w_ref=w_ref, m_ref=m_ref, v_ref=v_ref,
                       g_ref=g_ref, d_ref=d_ref, nm_ref=nm_ref, nv_ref=nv_ref):
                g = mine[home, rs, :].astype(F32) + got_sib[home, rs, :].astype(F32)
                for k in range(3):
                    g = g + got_chip[k, rs, :].astype(F32)
                delta, m2, v2 = _adamw(w_ref[rs, :], g, m_ref[rs, :], v_ref[rs, :])
                g_ref[rs, :] = g
                d_ref[rs, :] = delta
                nm_ref[rs, :] = m2
                nv_ref[rs, :] = v2

            rows_loop(w_ref.shape[0], finish)

        for rc in d2d + ici:
            rc.wait_send()

    vm = pl.BlockSpec(memory_space=pltpu.VMEM)
    anyspace = pl.BlockSpec(memory_space=pl.ANY)
    args, in_specs, out_shape, scratch = [], [], [], []
    for parts, w, m, v in items:
        assert w.shape[0] % rb == 0 and parts.shape == (N_DEV,) + w.shape and parts.dtype == BF16
    args += [it[0] for it in items]
    in_specs += [anyspace] * n
    for _, w, m, v in items:
        args += [w, m, v]
        in_specs += [vm] * 3
        out_shape += [jax.ShapeDtypeStruct(w.shape, F32)] * 4
    for it in items:
        shp = it[1].shape
        scratch += [pltpu.VMEM((4,) + shp, BF16), pltpu.VMEM((4,) + shp, BF16),
                    pltpu.VMEM((3,) + shp, BF16), pltpu.VMEM((3,) + shp, BF16)]
    scratch += [pltpu.SemaphoreType.DMA((4 * n,)), pltpu.SemaphoreType.DMA((4 * n,)),
                pltpu.SemaphoreType.DMA((3 * n,)), pltpu.SemaphoreType.DMA((3 * n,)), pltpu.SemaphoreType.DMA((4 * n,))]
    outs = pl.pallas_call(
        body, name=name, out_shape=out_shape, in_specs=in_specs, out_specs=[vm] * (4 * n),
        scratch_shapes=scratch, compiler_params=pltpu.CompilerParams(vmem_limit_bytes=VMEM_LIMIT),
    )(*args)
    return [tuple(outs[4 * a:4 * a + 4]) for a in range(n)]


def _sum_devices_adamw(items):
    n = len(items)

    def body(*refs):
        for a in range(n):
            got, w_ref, m_ref, v_ref = refs[4 * a:4 * a + 4]
            g_ref, d_ref, nm_ref, nv_ref = refs[4 * n + 4 * a:4 * n + 4 * a + 4]
            g = got[0].astype(F32)
            for j in range(1, N_DEV):
                g = g + got[j].astype(F32)
            delta, m2, v2 = _adamw(w_ref[...], g, m_ref[...], v_ref[...])
            g_ref[...] = g
            d_ref[...] = delta
            nm_ref[...] = m2
            nv_ref[...] = v2

    args, out_shape = [], []
    for got, w, m, v in items:
        assert got.shape == (N_DEV,) + w.shape
        args += [got, w, m, v]
        out_shape += [jax.ShapeDtypeStruct(w.shape, F32)] * 4
    outs = pl.pallas_call(body, name="sum_devices_adamw", out_shape=out_shape,
                          compiler_params=pltpu.CompilerParams(vmem_limit_bytes=VMEM_LIMIT))(*args)
    return [tuple(outs[4 * a:4 * a + 4]) for a in range(n)]


def _mod_fwd(c_rows, w_mod_loc, b_mod_loc):
    def body(c_ref, w_ref, b_ref, o_ref):
        cr = c_ref[...]
        a = (cr * _sigmoid(cr)).astype(BF16)
        o_ref[...] = _dot(a, w_ref[...].astype(BF16)) + b_ref[...]

    return pl.pallas_call(
        body, name="mod_fwd", out_shape=jax.ShapeDtypeStruct((c_rows.shape[0], w_mod_loc.shape[1]), F32),
        compiler_params=pltpu.CompilerParams(vmem_limit_bytes=VMEM_LIMIT),
    )(c_rows, w_mod_loc, b_mod_loc)


def _fwd_in(x, modrows, norm_w, w_in_b, cos, sins, qnw_t, knw_t, ones_bd, k_all, v_all, shards):
    bl, s, _ = x.shape
    tm = TOKEN_PARTS * TM
    nt = s // tm
    n_sh = len(shards)

    def body(*refs):
        (x_ref, mod_ref, nw_ref, win_ref, cos_ref, sin_ref, qnw_ref, knw_ref, bd_ref, kin_ref, vin_ref) = refs[:11]
        shard_refs = refs[11:11 + n_sh]
        q_ref, k_ref, v_ref, pq_ref, pkv_ref, za_ref, glu_ref, zc_ref = refs[11 + n_sh:19 + n_sh]
        gathered_refs = refs[19 + n_sh:19 + 2 * n_sh]
        send_sems, recv_sems, local_sems = refs[19 + 2 * n_sh:]
        b, i = pl.program_id(0), pl.program_id(1)
        local, remote = _exchange_copies(shard_refs, gathered_refs, send_sems, recv_sems, local_sems, scatter=False)

        @pl.when((b == 0) & (i == 0))
        def _():
            for cp in local + remote:
                cp.start()

        shift = mod_ref[0, 0:1, :]
        scale = mod_ref[0, 1:2, :]
        for part in range(TOKEN_PARTS):
            rows = pl.ds(part * TM, TM)
            xv = x_ref[0, rows, :]
            r = lax.rsqrt(jnp.mean(xv * xv, axis=-1, keepdims=True) + EPS)
            u = (xv * r * nw_ref[...]) * (1.0 + scale) + shift
            p = _dot_nt(u.astype(BF16), win_ref[...])
            pq = p[:, 0:DA]
            pk = p[:, DA:DA + HD * 2]
            ck = cos_ref[rows, :]
            sk = sin_ref[rows, :]
            cs = jnp.concatenate([ck] * (DA // KVW), axis=-1)
            sn = jnp.concatenate([sk] * (DA // KVW), axis=-1)
            rq = lax.rsqrt(_segsum(pq * pq, bd_ref[...]) * (1.0 / HD) + EPS)
            qn = pq * rq * qnw_ref[...]
            qr = _rope(qn, cs, sn) * 0.125
            for h in range(DA // HD):
                q_ref[0, h, rows, :] = qr[:, h * HD:(h + 1) * HD].astype(BF16)
            rk = lax.rsqrt(_segsum(pk * pk, bd_ref[0:KVW, 0:KVW]) * (1.0 / HD) + EPS)
            kn = pk * rk * knw_ref[...]
            kr = _rope(kn, ck, sk)
            pv = p[:, 640:768]
            for h in range(KVW // HD):
                k_ref[0, h, rows, :] = kr[:, h * HD:(h + 1) * HD].astype(BF16)
                v_ref[0, h, rows, :] = _with_ones_column(pv[:, h * HD:(h + 1) * HD]).astype(BF16)
            pq_ref[0, rows, :] = pq
            pkv_ref[0, rows, :] = p[:, 512:768]
            za_ref[0, rows, :] = p[:, 768:1280]
            glu_ref[0, rows, :] = p[:, 1280:2304]
            zc_ref[0, rows, :] = p[:, 2304:2816]

        @pl.when((b == bl - 1) & (i == nt - 1))
        def _():
            for cp in remote:
                cp.wait_recv()
            for cp in remote:
                cp.wait_send()
            for cp in local:
                cp.wait()

    def tile(w):
        return pl.BlockSpec((1, tm, w), lambda b, i: (b, i, 0))

    def const(shape):
        return pl.BlockSpec(shape, lambda b, i: (0,) * len(shape))

    outs = [(DA, F32), (2 * KVW, F32), (DA, F32), (2 * DC, F32), (DC, F32)]
    anyspace = pl.BlockSpec(memory_space=pl.ANY)
    rope = pl.BlockSpec((tm, KVW), lambda b, i: (i, 0))
    k_tile = pl.BlockSpec((1, KVW // HD, tm, HD), lambda b, i: (b, 0, i, 0))
    v_tile = pl.BlockSpec((1, KVW // HD, tm, 2 * HD), lambda b, i: (b, 0, i, 0))
    res = pl.pallas_call(
        body, name="fwd_in", grid=(bl, nt),
        in_specs=[tile(D), pl.BlockSpec((1, 3, D), lambda b, i: (b, 0, 0)), const((1, D)), const((D_IN, D)),
                  rope, rope, const((1, DA)), const((1, KVW)), const((DA, DA)), anyspace, anyspace]
        + [anyspace] * n_sh,
        out_specs=[pl.BlockSpec((1, DA // HD, tm, HD), lambda b, i: (b, 0, i, 0)), k_tile, v_tile]
        + [tile(w) for w, _ in outs] + [anyspace] * n_sh,
        out_shape=[jax.ShapeDtypeStruct((bl, DA // HD, s, HD), BF16), jax.ShapeDtypeStruct(k_all.shape, BF16),
                   jax.ShapeDtypeStruct(v_all.shape, BF16)]
        + [jax.ShapeDtypeStruct((bl, s, w), dt) for w, dt in outs]
        + [jax.ShapeDtypeStruct((N_DEV,) + a.shape, a.dtype) for a in shards],
        input_output_aliases={9: 1, 10: 2},
        scratch_shapes=_exchange_scratch(n_sh),
        compiler_params=_params(("arbitrary", "arbitrary")),
    )(x, modrows, norm_w, w_in_b, cos, sins, qnw_t, knw_t, ones_bd, k_all, v_all, *shards)
    return res[:8], res[8:]


_KV_ROWS_OF_W_IN_T = pl.BlockSpec((2 * KVW, D), lambda b: (DA // (2 * KVW), 0))


def _ctx_fwd(ctx, modc, norm_w, w_kv_b, knw_t, ones_bd, n_keys):
    bl, cl, _ = ctx.shape

    def body(x_ref, mod_ref, nw_ref, w_ref, knw_ref, bd_ref, k_ref, v_ref, pkv_ref):
        xv = x_ref[0]
        shift = mod_ref[0, 0:1, :]
        scale = mod_ref[0, 1:2, :]
        r = lax.rsqrt(jnp.mean(xv * xv, axis=-1, keepdims=True) + EPS)
        u = (xv * r * nw_ref[...]) * (1.0 + scale) + shift
        p = _dot_nt(u.astype(BF16), w_ref[...])
        pk = p[:, 0:KVW]
        rk = lax.rsqrt(_segsum(pk * pk, bd_ref[...]) * (1.0 / HD) + EPS)
        kn = pk * rk * knw_ref[...]
        pv = p[:, KVW:2 * KVW]
        for h in range(KVW // HD):
            k_ref[0, h] = kn[:, h * HD:(h + 1) * HD].astype(BF16)
            v_ref[0, h] = _with_ones_column(pv[:, h * HD:(h + 1) * HD]).astype(BF16)
        pkv_ref[0] = p

    def const(shape):
        return pl.BlockSpec(shape, lambda b: (0,) * len(shape))

    def tile(w):
        return pl.BlockSpec((1, cl, w), lambda b: (b, 0, 0))

    ctx_block = (n_keys - cl) // cl
    assert ctx_block * cl + cl == n_keys
    k_tile = pl.BlockSpec((1, KVW // HD, cl, HD), lambda b: (b, 0, ctx_block, 0))
    v_tile = pl.BlockSpec((1, KVW // HD, cl, 2 * HD), lambda b: (b, 0, ctx_block, 0))
    return pl.pallas_call(
        body, name="ctx_fwd", grid=(bl,),
        in_specs=[tile(D), const((1, 3, D)), const((1, D)), _KV_ROWS_OF_W_IN_T, const((1, KVW)), const((KVW, KVW))],
        out_specs=[k_tile, v_tile, tile(2 * KVW)],
        out_shape=[jax.ShapeDtypeStruct((bl, KVW // HD, n_keys, HD), BF16),
                   jax.ShapeDtypeStruct((bl, KVW // HD, n_keys, 2 * HD), BF16),
                   jax.ShapeDtypeStruct((bl, cl, 2 * KVW), F32)],
        compiler_params=_params(("arbitrary",)),
    )(ctx, modc, norm_w, w_kv_b, knw_t, ones_bd)


def _attn_fwd(q, k, v1):
    bl, _, s, _ = q.shape
    n_keys = k.shape[2]

    def body(q_ref, k_ref, v_ref, o_ref, lse_ref):
        kv = k_ref[0, 0]
        vv = v_ref[0, 0]
        lane = lax.broadcasted_iota(jnp.int32, (TQ, 2 * HD), 1)
        for part in range(FWD_PARTS):
            rows = pl.ds(part * TQ, TQ)
            lse = jnp.zeros((TQ, 2 * HD), F32)
            heads = []
            sc_all = _dot_nt(q_ref[0, :, rows, :].reshape(4 * TQ, HD), kv)
            for h in range(4):
                sc = sc_all[h * TQ:(h + 1) * TQ, :]
                m = jnp.max(sc, axis=-1, keepdims=True)
                e = jnp.exp(sc - m).astype(BF16)
                ov = _dot(e, vv)
                denom = ov[:, HD:HD + 1]
                heads.append(ov[:, 0:HD] * (1.0 / denom))
                lse = jnp.where(lane == h, m + jnp.log(denom), lse)
            o_ref[0, rows, :] = jnp.concatenate(heads, axis=-1)
            lse_ref[0, 0, rows, :] = lse

    tq = FWD_PARTS * TQ
    ks = pl.BlockSpec((1, 1, n_keys, HD), lambda b, g, i: (b, g, 0, 0))
    qs = pl.BlockSpec((1, 4, tq, HD), lambda b, g, i: (b, g, i, 0))
    vs = pl.BlockSpec((1, 1, n_keys, 2 * HD), lambda b, g, i: (b, g, 0, 0))
    return pl.pallas_call(
        body, name="attn_fwd", grid=(bl, 2, s // tq), in_specs=[qs, ks, vs],
        out_specs=[pl.BlockSpec((1, tq, 4 * HD), lambda b, g, i: (b, i, g)),
                   pl.BlockSpec((1, 1, tq, 2 * HD), lambda b, g, i: (b, g, i, 0))],
        out_shape=[jax.ShapeDtypeStruct((bl, s, DA), F32), jax.ShapeDtypeStruct((bl, 2, s, 2 * HD), F32)],
        compiler_params=_params(("arbitrary", "arbitrary", "arbitrary")),
    )(q, k, v1)


def _attn_bwd(q, k, v1, do, o, lse, exchange):
    bl, _, s, _ = q.shape
    n_keys = k.shape[2]
    tq = BWD_PARTS * TQ
    nq = s // tq
    n_ex = len(exchange)

    def body(*refs):
        q_ref, k_ref, v_ref, do_ref, o_ref, lse_ref = refs[:6]
        part_refs = refs[6:6 + n_ex]
        dq_ref, dk_ref, dv_ref = refs[6 + n_ex:9 + n_ex]
        got_refs = refs[9 + n_ex:9 + 2 * n_ex]
        p_sc, ds_sc, dkt, dvt, send_sems, recv_sems, local_sems = refs[9 + 2 * n_ex:]
        i = pl.program_id(2)
        first = (pl.program_id(0) == 0) & (pl.program_id(1) == 0) & (i == 0)
        last = (pl.program_id(0) == bl - 1) & (pl.program_id(1) == 1) & (i == nq - 1)
        local, remote = _exchange_copies(part_refs, got_refs, send_sems, recv_sems, local_sems, scatter=True)

        @pl.when(first)
        def _():
            for cp in local + remote:
                cp.start()

        @pl.when(i == 0)
        def _():
            dkt[...] = jnp.zeros_like(dkt)
            dvt[...] = jnp.zeros_like(dvt)

        kv = k_ref[0, 0]
        vv = v_ref[0, 0][:, 0:HD]
        for part in range(BWD_PARTS):
            tq_rows = pl.ds(part * TQ, TQ)
            lse = lse_ref[0, 0, tq_rows, :]
            ov = o_ref[0, tq_rows, :]
            dqs = []
            q_cat = q_ref[0, :, tq_rows, :].reshape(4 * TQ, HD)
            do_cat = do_ref[0, :, tq_rows, :].reshape(4 * TQ, HD)
            sc_all = _dot_nt(q_cat, kv)
            for h in range(4):
                doh = do_cat[h * TQ:(h + 1) * TQ, :]
                delta = jnp.sum(ov[:, h * HD:(h + 1) * HD] * doh.astype(F32), axis=-1, keepdims=True)
                rows = pl.ds((part * 4 + h) * TQ, TQ)
                p = jnp.exp(sc_all[h * TQ:(h + 1) * TQ, :] - lse[:, h:h + 1])
                ds = (p * (_dot_nt(doh, vv) - delta)).astype(BF16)
                p_sc[rows, :] = p.astype(BF16)
                ds_sc[rows, :] = ds
                dqs.append(_dot(ds, kv) * 0.125)
            dq_ref[0, tq_rows, :] = jnp.concatenate(dqs, axis=-1)
            part_rows = pl.ds(part * 4 * TQ, 4 * TQ)
            dvt[...] += _dot_tn(do_cat, p_sc[part_rows, :])
            dkt[...] += _dot_tn(q_cat, ds_sc[part_rows, :])

        @pl.when(i == nq - 1)
        def _():
            dk_ref[0, 0] = dkt[...].T
            dv_ref[0, 0] = dvt[...].T

        @pl.when(last)
        def _():
            for cp in remote:
                cp.wait_recv()
            for cp in remote:
                cp.wait_send()
            for cp in local:
                cp.wait()

    qs = pl.BlockSpec((1, 4, tq, HD), lambda b, g, i: (b, g, i, 0))
    ks = pl.BlockSpec((1, 1, n_keys, HD), lambda b, g, i: (b, g, 0, 0))
    vs = pl.BlockSpec((1, 1, n_keys, 2 * HD), lambda b, g, i: (b, g, 0, 0))
    os_ = pl.BlockSpec((1, tq, 4 * HD), lambda b, g, i: (b, i, g))
    kshape = jax.ShapeDtypeStruct(k.shape, F32)
    anyspace = pl.BlockSpec(memory_space=pl.ANY)
    res = pl.pallas_call(
        body, name="attn_bwd", grid=(bl, 2, nq),
        in_specs=[qs, ks, vs, qs, os_, pl.BlockSpec((1, 1, tq, 2 * HD), lambda b, g, i: (b, g, i, 0))]
        + [anyspace] * n_ex,
        out_specs=[os_, ks, ks] + [anyspace] * n_ex,
        out_shape=[jax.ShapeDtypeStruct((bl, s, DA), F32), kshape, kshape]
        + [jax.ShapeDtypeStruct(a.shape, a.dtype) for a in exchange],
        scratch_shapes=[pltpu.VMEM((4 * tq, n_keys), BF16), pltpu.VMEM((4 * tq, n_keys), BF16),
                        pltpu.VMEM((HD, n_keys), F32), pltpu.VMEM((HD, n_keys), F32)] + _exchange_scratch(n_ex),
        compiler_params=_params(("arbitrary", "arbitrary", "arbitrary")),
    )(q, k, v1, do, o, lse, *exchange)
    return res[:3], res[3:]


def _halo_specs(width, s):
    per = TC // HALO
    last = s // HALO - 1
    main = pl.BlockSpec((1, TC, width), lambda b, i: (b, i, 0))
    prev = pl.BlockSpec((1, HALO, width), lambda b, i: (b, jnp.maximum(i * per - 1, 0), 0))
    nxt = pl.BlockSpec((1, HALO, width), lambda b, i: (b, jnp.minimum((i + 1) * per, last), 0))
    return main, prev, nxt


def _glu(g):
    return g[:, 0:DC] * _sigmoid(g[:, DC:2 * DC])


def _fill_padded(pad_ref, main, prev, nxt, first, last):
    pad_ref[0:HALO, :] = jnp.where(first, 0.0, prev)
    pad_ref[HALO:HALO + TC, :] = main
    pad_ref[HALO + TC:2 * HALO + TC, :] = jnp.where(last, 0.0, nxt)


PLANE_ROWS = TC + 2 * HALO - 8


def _shift_planes(pad_ref, planes_ref):
    for r in range(1, 8):
        planes_ref[r - 1] = pad_ref[pl.ds(r, PLANE_ROWS), :]


def _tap_rows(pad_ref, planes_ref, offset, start, n):
    a, r = divmod(offset, 8)
    if r == 0:
        return pad_ref[pl.ds(start + 8 * a, n), :]
    return planes_ref[r - 1, pl.ds(start + 8 * a, n), :]


def _conv_fwd(glu, conv_w, conv_b, ln_w, ln_b, w_pw_b, b_pw):
    bl, s, _ = glu.shape
    nt = s // TC

    def body(g_ref, gp_ref, gn_ref, cw_ref, cb_ref, lw_ref, lb_ref, wpw_ref, bpw_ref, y_ref, cp_ref, pad_ref, planes_ref):
        i = pl.program_id(1)
        _fill_padded(pad_ref, _glu(g_ref[0]), _glu(gp_ref[0]), _glu(gn_ref[0]), i == 0, i == nt - 1)
        _shift_planes(pad_ref, planes_ref)
        for ck in range(TC // CH):
            acc = jnp.zeros((CH, DC), F32) + cb_ref[...]
            for t in range(KW):
                acc = acc + _tap_rows(pad_ref, planes_ref, 1 + t, ck * CH, CH) * cw_ref[t:t + 1, :]
            y_ref[0, pl.ds(ck * CH, CH), :] = acc
        y = y_ref[0]
        mu = jnp.mean(y, axis=-1, keepdims=True)
        yc = y - mu
        var = jnp.mean(yc * yc, axis=-1, keepdims=True)
        z = yc * lax.rsqrt(var + EPS) * lw_ref[...] + lb_ref[...]
        act = z * _sigmoid(z)
        cp_ref[0] = _dot(act.astype(BF16), wpw_ref[...]) + bpw_ref[...]

    def const(shape):
        return pl.BlockSpec(shape, lambda b, i: (0,) * len(shape))

    main, prev, nxt = _halo_specs(2 * DC, s)
    tile = pl.BlockSpec((1, TC, DC), lambda b, i: (b, i, 0))
    return pl.pallas_call(
        body, name="conv_fwd", grid=(bl, nt),
        in_specs=[main, prev, nxt, const((32, DC)), const((1, DC)), const((1, DC)), const((1, DC)),
                  const((DC, DC)), const((1, DC))],
        out_specs=[tile, tile],
        out_shape=[jax.ShapeDtypeStruct((bl, s, DC), F32)] * 2,
        scratch_shapes=[pltpu.VMEM((TC + 2 * HALO, DC), F32), pltpu.VMEM((7, PLANE_ROWS, DC), F32)],
        compiler_params=_params(("arbitrary", "arbitrary")),
    )(glu, glu, glu, conv_w, conv_b, ln_w, ln_b, w_pw_b, b_pw)


def _conv_bwd_pointwise(y, dcp, ln_w, ln_b, w_pw_b):
    bl, s, _ = y.shape
    nt = s // TM

    def body(y_ref, dcp_ref, lw_ref, lb_ref, wpw_ref, dy_ref, gw_ref, rows_ref):
        @pl.when((pl.program_id(0) == 0) & (pl.program_id(1) == 0))
        def _():
            gw_ref[...] = jnp.zeros_like(gw_ref)
            rows_ref[...] = jnp.zeros_like(rows_ref)

        y = y_ref[0]
        dcp = dcp_ref[0]
        mu = jnp.mean(y, axis=-1, keepdims=True)
        yc = y - mu
        rstd = lax.rsqrt(jnp.mean(yc * yc, axis=-1, keepdims=True) + EPS)
        yn = yc * rstd
        lw = lw_ref[...]
        z = yn * lw + lb_ref[...]
        sg = _sigmoid(z)
        act = z * sg
        dcp_b = dcp.astype(BF16)
        gw_ref[...] += _dot_tn(act.astype(BF16), dcp_b)
        dact = _dot_nt(dcp_b, wpw_ref[...])
        dz = dact * (sg * (1.0 + z * (1.0 - sg)))
        dyn = dz * lw
        dy = rstd * (dyn - jnp.mean(dyn, axis=-1, keepdims=True) - yn * jnp.mean(dyn * yn, axis=-1, keepdims=True))
        dy_ref[0] = dy
        rows_ref[0:1, :] += jnp.sum(dcp, axis=0, keepdims=True)
        rows_ref[1:2, :] += jnp.sum(dz * yn, axis=0, keepdims=True)
        rows_ref[2:3, :] += jnp.sum(dz, axis=0, keepdims=True)
        rows_ref[3:4, :] += jnp.sum(dy, axis=0, keepdims=True)

    def const(shape):
        return pl.BlockSpec(shape, lambda b, i: (0,) * len(shape))

    tile = pl.BlockSpec((1, TM, DC), lambda b, i: (b, i, 0))
    return pl.pallas_call(
        body, name="conv_bwd_pointwise", grid=(bl, nt),
        in_specs=[tile, tile, const((1, DC)), const((1, DC)), const((DC, DC))],
        out_specs=[tile, const((DC, DC)), const((8, DC))],
        out_shape=[jax.ShapeDtypeStruct((bl, s, DC), F32), jax.ShapeDtypeStruct((DC, DC), F32),
                   jax.ShapeDtypeStruct((8, DC), F32)],
        compiler_params=_params(("arbitrary", "arbitrary")),
    )(y, dcp, ln_w, ln_b, w_pw_b)


def _conv_bwd_depthwise(glu, dy, conv_w):
    bl, s, _ = glu.shape
    nt = s // TC

    def body(g_ref, gp_ref, gn_ref, d_ref, dp_ref, dn_ref, cw_ref, dglu_ref, dcw_ref,
             padu_ref, padd_ref, planes_u, planes_d):
        i = pl.program_id(1)

        @pl.when((pl.program_id(0) == 0) & (i == 0))
        def _():
            dcw_ref[...] = jnp.zeros_like(dcw_ref)

        first, last = i == 0, i == nt - 1
        _fill_padded(padu_ref, _glu(g_ref[0]), _glu(gp_ref[0]), _glu(gn_ref[0]), first, last)
        _fill_padded(padd_ref, d_ref[0], dp_ref[0], dn_ref[0], first, last)
        _shift_planes(padu_ref, planes_u)
        _shift_planes(padd_ref, planes_d)
        for ck in range(TC // CH):
            acc = jnp.zeros((CH, DC), F32)
            for t in range(KW):
                acc = acc + _tap_rows(padd_ref, planes_d, 2 * HALO - 1 - t, ck * CH, CH) * cw_ref[t:t + 1, :]
            g = g_ref[0, pl.ds(ck * CH, CH), :]
            a = g[:, 0:DC]
            sg = _sigmoid(g[:, DC:2 * DC])
            dglu_ref[0, pl.ds(ck * CH, CH), 0:DC] = (acc * sg).astype(BF16)
            dglu_ref[0, pl.ds(ck * CH, CH), DC:2 * DC] = (acc * a * sg * (1.0 - sg)).astype(BF16)
        group = 4
        for t0 in range(0, KW, group):
            taps = range(t0, min(t0 + group, KW))
            acc8 = [jnp.zeros((8, DC), F32) for _ in taps]
            for ck in range(TC // CH):
                dchunk = d_ref[0, pl.ds(ck * CH, CH), :]
                for n, t in enumerate(taps):
                    prod = _tap_rows(padu_ref, planes_u, 1 + t, ck * CH, CH) * dchunk
                    acc8[n] = acc8[n] + jnp.sum(prod.reshape(CH // 8, 8, DC), axis=0)
            for n, t in enumerate(taps):
                dcw_ref[t:t + 1, :] += jnp.sum(acc8[n], axis=0, keepdims=True)

    gmain, gprev, gnext = _halo_specs(2 * DC, s)
    dmain, dprev, dnext = _halo_specs(DC, s)
    cw = pl.BlockSpec((32, DC), lambda b, i: (0, 0))
    return pl.pallas_call(
        body, name="conv_bwd_depthwise", grid=(bl, nt),
        in_specs=[gmain, gprev, gnext, dmain, dprev, dnext, cw],
        out_specs=[gmain, cw],
        out_shape=[jax.ShapeDtypeStruct((bl, s, 2 * DC), BF16), jax.ShapeDtypeStruct((32, DC), F32)],
        scratch_shapes=[pltpu.VMEM((TC + 2 * HALO, DC), F32)] * 2 + [pltpu.VMEM((7, PLANE_ROWS, DC), F32)] * 2,
        compiler_params=_params(("arbitrary", "arbitrary")),
    )(glu, glu, glu, dy, dy, dy, conv_w)


def _out_fwd_bwd(attn, za, cp, zc, x, target, modrows, w_out_b):
    bl, s, _ = x.shape
    nt = s // TM

    def body(o_ref, za_ref, cp_ref, zc_ref, x_ref, t_ref, mod_ref, w_ref,
             do_ref, dza_ref, dcp_ref, dzc_ref, dh_ref, dgate_ref, gw_ref, loss_ref):
        b, i = pl.program_id(0), pl.program_id(1)

        @pl.when((b == 0) & (i == 0))
        def _():
            gw_ref[...] = jnp.zeros_like(gw_ref)
            loss_ref[...] = jnp.zeros_like(loss_ref)

        @pl.when(i == 0)
        def _():
            dgate_ref[...] = jnp.zeros_like(dgate_ref)

        gate = mod_ref[0, 2:3, :]
        w = w_ref[...]
        mixes, douts = [], []
        for part in range(TOKEN_PARTS):
            rows = pl.ds(part * TM, TM)
            o, za_v, cp_v, zc_v = o_ref[0, rows, :], za_ref[0, rows, :], cp_ref[0, rows, :], zc_ref[0, rows, :]
            sa = _sigmoid(za_v)
            sc = _sigmoid(zc_v)
            silu_a = za_v * sa
            silu_c = zc_v * sc
            mix = jnp.concatenate([(o * silu_a).astype(BF16), (cp_v * silu_c).astype(BF16)], axis=-1)
            out = _dot(mix, w)
            err = x_ref[0, rows, :] + gate * out - t_ref[0, rows, :]
            loss_ref[...] += jnp.sum(err * err, axis=0, keepdims=True)
            dh = err * (1.0 / D)
            dh_ref[0, rows, :] = dh
            dgate_ref[0] += jnp.sum(dh * out, axis=0, keepdims=True)
            dout = (dh * gate).astype(BF16)
            mixes.append(mix)
            douts.append(dout)
            dmix = _dot_nt(dout, w)
            dga = dmix[:, 0:DA]
            dgc = dmix[:, DA:DA + DC]
            dov = dga * silu_a
            for h in range(DA // HD):
                do_ref[0, h, rows, :] = dov[:, h * HD:(h + 1) * HD].astype(BF16)
            dza_ref[0, rows, :] = (dga * o * (sa * (1.0 + za_v * (1.0 - sa)))).astype(BF16)
            dcp_ref[0, rows, :] = dgc * silu_c
            dzc_ref[0, rows, :] = (dgc * cp_v * (sc * (1.0 + zc_v * (1.0 - sc)))).astype(BF16)
        gw_ref[...] += _dot_tn(jnp.concatenate(mixes, axis=0), jnp.concatenate(douts, axis=0))

    def const(shape):
        return pl.BlockSpec(shape, lambda b, i: (0,) * len(shape))

    tm = TOKEN_PARTS * TM

    def tile(w):
        return pl.BlockSpec((1, tm, w), lambda b, i: (b, i, 0))

    return pl.pallas_call(
        body, name="out_fwd_bwd", grid=(bl, s // tm),
        in_specs=[tile(DA), tile(DA), tile(DC), tile(DC), tile(D), tile(D),
                  pl.BlockSpec((1, 3, D), lambda b, i: (b, 0, 0)), const((D, D))],
        out_specs=[pl.BlockSpec((1, DA // HD, tm, HD), lambda b, i: (b, 0, i, 0)), tile(DA), tile(DC), tile(DC), tile(D),
                   pl.BlockSpec((1, 1, D), lambda b, i: (b, 0, 0)), const((D, D)), const((1, D))],
        out_shape=[jax.ShapeDtypeStruct((bl, DA // HD, s, HD), BF16), jax.ShapeDtypeStruct((bl, s, DA), BF16),
                   jax.ShapeDtypeStruct((bl, s, DC), F32), jax.ShapeDtypeStruct((bl, s, DC), BF16),
                   jax.ShapeDtypeStruct((bl, s, D), F32), jax.ShapeDtypeStruct((bl, 1, D), F32),
                   jax.ShapeDtypeStruct((D, D), F32), jax.ShapeDtypeStruct((1, D), F32)],
        compiler_params=_params(("arbitrary", "arbitrary")),
    )(attn, za, cp, zc, x, target, modrows, w_out_b)


def _rms_heads_bwd(dy, x, w_t, ones_bd):
    r = lax.rsqrt(_segsum(x * x, ones_bd) * (1.0 / HD) + EPS)
    xh = x * r
    g = dy * w_t
    dx = r * (g - xh * (_segsum(g * xh, ones_bd) * (1.0 / HD)))
    return dx, dy * xh


def _ctx_bwd(ctx, modc, norm_w, w_kv_b, pkv_c, dk_c, dv_c, knw_t, ones_bd):
    bl, cl, _ = ctx.shape

    def body(x_ref, mod_ref, nw_ref, w_ref, p_ref, dk_ref, dv_ref, knw_ref, bd_ref, gw_ref, rows_ref, dknw_ref):
        @pl.when(pl.program_id(0) == 0)
        def _():
            gw_ref[...] = jnp.zeros_like(gw_ref)
            rows_ref[...] = jnp.zeros_like(rows_ref)
            dknw_ref[...] = jnp.zeros_like(dknw_ref)

        xv = x_ref[0]
        shift = mod_ref[0, 0:1, :]
        scale = mod_ref[0, 1:2, :]
        nw = nw_ref[...]
        r = lax.rsqrt(jnp.mean(xv * xv, axis=-1, keepdims=True) + EPS)
        xn = xv * r
        yv = xn * nw
        u = yv * (1.0 + scale) + shift
        dkv = jnp.concatenate([dk_ref[0, 0], dk_ref[0, 1]], axis=-1)
        dpk, dknw = _rms_heads_bwd(dkv, p_ref[0][:, 0:KVW], knw_ref[...], bd_ref[...])
        dp = jnp.concatenate([dpk.astype(BF16), dv_ref[0, 0].astype(BF16), dv_ref[0, 1].astype(BF16)], axis=-1)
        gw_ref[...] += _dot_tn(dp, u.astype(BF16))
        du = _dot(dp, w_ref[...])
        rows_ref[0:1, :] += jnp.sum(du, axis=0, keepdims=True)
        rows_ref[1:2, :] += jnp.sum(du * yv, axis=0, keepdims=True)
        rows_ref[2:3, :] += jnp.sum(du * (1.0 + scale) * xn, axis=0, keepdims=True)
        dknw_ref[...] += jnp.sum(dknw, axis=0, keepdims=True)

    def const(shape):
        return pl.BlockSpec(shape, lambda b: (0,) * len(shape))

    def tile(w):
        return pl.BlockSpec((1, cl, w), lambda b: (b, 0, 0))

    ctx_block = (dk_c.shape[2] - cl) // cl
    kv_tile = pl.BlockSpec((1, KVW // HD, cl, HD), lambda b: (b, 0, ctx_block, 0))
    return pl.pallas_call(
        body, name="ctx_bwd", grid=(bl,),
        in_specs=[tile(D), const((1, 3, D)), const((1, D)), _KV_ROWS_OF_W_IN_T, tile(2 * KVW), kv_tile, kv_tile,
                  const((1, KVW)), const((KVW, KVW))],
        out_specs=[const((2 * KVW, D)), const((8, D)), const((1, KVW))],
        out_shape=[jax.ShapeDtypeStruct((2 * KVW, D), F32), jax.ShapeDtypeStruct((8, D), F32),
                   jax.ShapeDtypeStruct((1, KVW), F32)],
        compiler_params=_params(("arbitrary",)),
    )(ctx, modc, norm_w, w_kv_b, pkv_c, dk_c, dv_c, knw_t, ones_bd)


def _bwd_in(x, modrows, norm_w, w_in_b, cos, sins, qnw_t, knw_t, ones_bd,
            pq, pkv, dq, dk, dv, dza, dglu, dzc, dh, gw_kv):
    bl, s, _ = x.shape
    tm = TOKEN_PARTS * TM
    nt = s // tm

    def body(x_ref, mod_ref, nw_ref, win_hbm, cos_ref, sin_ref, qnw_ref, knw_ref, bd_ref,
             pq_ref, pkv_ref, dq_ref, dk_ref, dv_ref, dza_ref, dglu_ref, dzc_ref, dh_ref, gwkv_ref,
             gx_ref, gw_hbm, dmod_ref, dnw_ref, dqnw_ref, dknw_ref, win_ref, gw_acc, sem):
        b, i = pl.program_id(0), pl.program_id(1)

        @pl.when((b == 0) & (i == 0))
        def _():
            cp = pltpu.make_async_copy(win_hbm, win_ref, sem)
            cp.start()
            gw_acc[...] = jnp.zeros_like(gw_acc)
            dnw_ref[...] = jnp.zeros_like(dnw_ref)
            dqnw_ref[...] = jnp.zeros_like(dqnw_ref)
            dknw_ref[...] = jnp.zeros_like(dknw_ref)
            cp.wait()

        @pl.when(i == 0)
        def _():
            dmod_ref[...] = jnp.zeros_like(dmod_ref)

        bd = bd_ref[...]
        shift = mod_ref[0, 0:1, :]
        scale = mod_ref[0, 1:2, :]
        nw = nw_ref[...]
        dps, us = [], []
        for part in range(TOKEN_PARTS):
            rows = pl.ds(part * TM, TM)
            ck = cos_ref[rows, :]
            sk = sin_ref[rows, :]
            cs = jnp.concatenate([ck] * (DA // KVW), axis=-1)
            sn = jnp.concatenate([sk] * (DA // KVW), axis=-1)
            dqn = _rope_bwd(dq_ref[0, rows, :], cs, sn)
            dpq, dqnw = _rms_heads_bwd(dqn, pq_ref[0, rows, :], qnw_ref[...], bd)
            dkn = _rope_bwd(jnp.concatenate([dk_ref[0, 0, rows, :], dk_ref[0, 1, rows, :]], axis=-1), ck, sk)
            dpk, dknw = _rms_heads_bwd(dkn, pkv_ref[0, rows, 0:KVW], knw_ref[...], bd[0:KVW, 0:KVW])
            dqnw_ref[...] += jnp.sum(dqnw, axis=0, keepdims=True)
            dknw_ref[...] += jnp.sum(dknw, axis=0, keepdims=True)
            dp = jnp.concatenate(
                [dpq.astype(BF16), dpk.astype(BF16), dv_ref[0, 0, rows, :].astype(BF16), dv_ref[0, 1, rows, :].astype(BF16),
                 dza_ref[0, rows, :], dglu_ref[0, rows, :], dzc_ref[0, rows, :]], axis=-1)

            xv = x_ref[0, rows, :]
            r = lax.rsqrt(jnp.mean(xv * xv, axis=-1, keepdims=True) + EPS)
            xn = xv * r
            yv = xn * nw
            u = yv * (1.0 + scale) + shift
            dps.append(dp)
            us.append(u.astype(BF16))
            du = _dot(dp, win_ref[...])
            dmod_ref[0, 0:1, :] += jnp.sum(du, axis=0, keepdims=True)
            dmod_ref[0, 1:2, :] += jnp.sum(du * yv, axis=0, keepdims=True)
            dy = du * (1.0 + scale)
            dnw_ref[...] += jnp.sum(dy * xn, axis=0, keepdims=True)
            dxn = dy * nw
            gx_ref[0, rows, :] = dh_ref[0, rows, :] + r * (dxn - xn * jnp.mean(dxn * xn, axis=-1, keepdims=True))
        gw_acc[...] += _dot_tn(jnp.concatenate(dps, axis=0), jnp.concatenate(us, axis=0))

        @pl.when((b == bl - 1) & (i == nt - 1))
        def _():
            gw_acc[DA:DA + 2 * KVW, :] += gwkv_ref[...]

            def to_bf16(j, carry):
                rows = pl.ds(pl.multiple_of(j * 2 * KVW, 2 * KVW), 2 * KVW)
                win_ref[rows, :] = gw_acc[rows, :].astype(BF16)
                return carry

            lax.fori_loop(0, D_IN // (2 * KVW), to_bf16, 0)
            pltpu.sync_copy(win_ref, gw_hbm)

    def tile(w):
        return pl.BlockSpec((1, tm, w), lambda b, i: (b, i, 0))

    def const(shape):
        return pl.BlockSpec(shape, lambda b, i: (0,) * len(shape))

    anyspace = pl.BlockSpec(memory_space=pl.ANY)
    rope = pl.BlockSpec((tm, KVW), lambda b, i: (i, 0))
    kv_tile = pl.BlockSpec((1, KVW // HD, tm, HD), lambda b, i: (b, 0, i, 0))
    return pl.pallas_call(
        body, name="bwd_in", grid=(bl, nt),
        in_specs=[tile(D), pl.BlockSpec((1, 3, D), lambda b, i: (b, 0, 0)), const((1, D)), anyspace, rope, rope,
                  const((1, DA)), const((1, KVW)), const((DA, DA)),
                  tile(DA), tile(2 * KVW), tile(DA), kv_tile, kv_tile, tile(DA), tile(2 * DC), tile(DC), tile(D),
                  const((2 * KVW, D))],
        out_specs=[tile(D), anyspace, pl.BlockSpec((1, 2, D), lambda b, i: (b, 0, 0)), const((1, D)),
                   const((1, DA)), const((1, KVW))],
        out_shape=[jax.ShapeDtypeStruct((bl, s, D), F32), jax.ShapeDtypeStruct((D_IN, D), BF16),
                   jax.ShapeDtypeStruct((bl, 2, D), F32), jax.ShapeDtypeStruct((1, D), F32),
                   jax.ShapeDtypeStruct((1, DA), F32), jax.ShapeDtypeStruct((1, KVW), F32)],
        scratch_shapes=[pltpu.VMEM((D_IN, D), BF16), pltpu.VMEM((D_IN, D), F32), pltpu.SemaphoreType.DMA],
        compiler_params=_params(("arbitrary", "arbitrary")),
    )(x, modrows, norm_w, w_in_b, cos, sins, qnw_t, knw_t, ones_bd,
      pq, pkv, dq, dk, dv, dza, dglu, dzc, dh, gw_kv)


_LOSS, _DMODC, _NW, _QN, _KN, _CB, _LW, _LB, _BPW, SMALL_W = 0, 1024, 4096, 5120, 5248, 5376, 5888, 6400, 6912, 7424


ROW_W = 1792


def _put_flat(ref, off, value):
    n, done = value.shape[1], 0
    while done < n:
        r, c = divmod(off + done, ROW_W)
        take = min(n - done, ROW_W - c)
        ref[r:r + 1, c:c + take] = value[:, done:done + take]
        done += take


def _get_flat(arr, off, n):
    parts, done = [], 0
    while done < n:
        r, c = divmod(off + done, ROW_W)
        take = min(n - done, ROW_W - c)
        parts.append(arr[r:r + 1, c:c + take])
        done += take
    return parts[0] if len(parts) == 1 else jnp.concatenate(parts, axis=-1)


def _pack_small_body(loss_ref, ctx_ref, dnw_ref, dqnw_ref, dknw_ref, dknwc_ref, conv_ref, dss_ref, dgate_ref, o_ref):
    bl = dss_ref.shape[0]
    assert SMALL_W + bl * 3 * D <= 8 * ROW_W
    o_ref[...] = jnp.zeros_like(o_ref)
    _put_flat(o_ref, _LOSS, loss_ref[...])
    _put_flat(o_ref, _DMODC, ctx_ref[0:1, :])
    _put_flat(o_ref, _DMODC + D, ctx_ref[1:2, :])
    _put_flat(o_ref, _NW, dnw_ref[...] + ctx_ref[2:3, :])
    dq = dqnw_ref[...]
    qn = dq[:, 0:HD]
    for h in range(1, DA // HD):
        qn = qn + dq[:, h * HD:(h + 1) * HD]
    _put_flat(o_ref, _QN, qn)
    dk = dknw_ref[...] + dknwc_ref[...]
    _put_flat(o_ref, _KN, dk[:, 0:HD] + dk[:, HD:2 * HD])
    _put_flat(o_ref, _BPW, conv_ref[0:1, :])
    _put_flat(o_ref, _LW, conv_ref[1:2, :])
    _put_flat(o_ref, _LB, conv_ref[2:3, :])
    _put_flat(o_ref, _CB, conv_ref[3:4, :])
    for b in range(bl):
        _put_flat(o_ref, SMALL_W + b * 3 * D, dss_ref[b, 0:1, :])
        _put_flat(o_ref, SMALL_W + b * 3 * D + D, dss_ref[b, 1:2, :])
        _put_flat(o_ref, SMALL_W + b * 3 * D + 2 * D, dgate_ref[b])


_SMALL = (("b_mod", None), ("norm_w", _NW), ("q_norm_w", _QN), ("k_norm_w", _KN), ("conv_b", _CB),
          ("conv_ln_w", _LW), ("conv_ln_b", _LB), ("b_pw", _BPW), ("c_ctx", None))


def _epilogue(parts_in, pieces, c_rows, w_mod_loc):
    bl = pieces[7].shape[0]
    n_ex = N_DEV * bl
    n_mod = w_mod_loc.shape[1]
    rb = 32
    shp = parts_in.shape[1:]
    rows_in = shp[0]

    def body(*refs):
        it = iter(refs)
        take = lambda k: [next(it) for _ in range(k)]
        (parts,) = take(1)
        piece_refs = take(9)
        (c_ref, wm_ref) = take(2)
        (g_in, g_wm, sum_ref, gb_ref, gc_all, loss_ref) = take(6)
        (mine, got_sib, stage, got_chip, payload, gathered, dmod_full, gc_mine) = take(8)
        (d2d_send, d2d_recv, ici_send, ici_recv, local_sems, sg_send, sg_recv, gc_send, gc_recv, misc_sems) = take(10)

        x, y, c = _coords()
        me = _lin(x, y, c)
        sib = (x, y, 1 - c)
        peers = [(1 - x, y), (x, 1 - y), (1 - x, 1 - y)]
        home = 2 * x + y

        def rows_loop(fn):
            def step(i, carry):
                fn(pl.ds(pl.multiple_of(i * rb, rb), rb))
                return carry
            lax.fori_loop(0, rows_in // rb, step, 0)

        def direct_gather(src, dst, send_sems, recv_sems, local_sem):
            cps = [pltpu.make_async_copy(src, dst.at[me], local_sem)]
            for k in range(1, N_DEV):
                peer = (1 - x if k & 4 else x, 1 - y if k & 2 else y, 1 - c if k & 1 else c)
                cps.append(pltpu.make_async_remote_copy(
                    src_ref=src, dst_ref=dst.at[me], send_sem=send_sems.at[k - 1], recv_sem=recv_sems.at[k - 1],
                    device_id=peer, device_id_type=MESH_ID))
            for cp in cps:
                cp.start()
            return cps

        _pack_small_body(*piece_refs, payload)
        small_cps = direct_gather(payload, gathered, sg_send, sg_recv, misc_sems.at[0])

        local, d2d, ici = [], [], []
        for s in range(4):
            cp = pltpu.make_async_copy(parts.at[_lin(s // 2, s % 2, c)], mine.at[s], local_sems.at[s])
            cp.start()
            local.append(cp)
            rc = pltpu.make_async_remote_copy(
                src_ref=parts.at[_lin(s // 2, s % 2, 1 - c)], dst_ref=got_sib.at[s],
                send_sem=d2d_send.at[s], recv_sem=d2d_recv.at[s], device_id=sib, device_id_type=MESH_ID)
            rc.start()
            d2d.append(rc)

        for cp in small_cps[1:]:
            cp.wait_recv()
        small_cps[0].wait()
        tot = gathered[0]
        for j in range(1, N_DEV):
            tot = tot + gathered[j]
        summed = _get_flat(tot, 0, SMALL_W)
        dmod_full[...] = jnp.zeros_like(dmod_full)
        for j in range(N_DEV):
            arr = gathered[j]
            for b in range(bl):
                dmod_full[j * bl + b:j * bl + b + 1, :] = _get_flat(arr, SMALL_W + b * 3 * D, 3 * D)
        dmod_full[n_ex:n_ex + 1, :] = summed[:, _DMODC:_DMODC + 3 * D]
        sum_ref[...] = summed
        gb_ref[...] = jnp.sum(dmod_full[...], axis=0, keepdims=True)
        loss_ref[...] = (0.5 / D) * jnp.sum(summed[:, _LOSS:_LOSS + D], axis=-1, keepdims=True)

        north = c == 1
        first = (jnp.where(north, 1 - x, x), jnp.where(north, y, 1 - y))
        second = (jnp.where(north, x, 1 - x), jnp.where(north, 1 - y, y))
        for s in range(4):
            local[s].wait()
            d2d[s].wait_recv()

        def chip_sum(k, chip, relayed):
            slot = 2 * chip[0] + chip[1]

            def pair_sum(rs):
                acc = mine[slot, rs, :].astype(F32) + got_sib[slot, rs, :].astype(F32)
                if relayed:
                    acc = acc + got_chip[1, rs, :].astype(F32)
                stage[k, rs, :] = acc.astype(BF16)

            rows_loop(pair_sum)

        def send(k, to):
            rc = pltpu.make_async_remote_copy(
                src_ref=stage.at[k], dst_ref=got_chip.at[k], send_sem=ici_send.at[k], recv_sem=ici_recv.at[k],
                device_id=(to[0], to[1], c), device_id_type=MESH_ID)
            rc.start()
            ici.append(rc)

        chip_sum(0, first, False)
        send(0, first)
        chip_sum(1, (1 - x, 1 - y), False)
        send(1, first)

        cr = c_ref[...]
        act = (cr * _sigmoid(cr)).astype(BF16)
        dm = dmod_full[:, pl.ds(pl.multiple_of(me * n_mod, 128), n_mod)].astype(BF16)
        g_wm[...] = _dot_tn(act, dm)
        gc_mine[...] = _dot_nt(dm[n_ex:n_ex + 8, :], wm_ref[...].astype(BF16))
        gc_cps = direct_gather(gc_mine, gc_all, gc_send, gc_recv, misc_sems.at[1])

        ici[1].wait_recv()
        chip_sum(2, second, True)
        send(2, second)
        ici[0].wait_recv()
        ici[2].wait_recv()

        def finish(rs):
            gsum = mine[home, rs, :].astype(F32) + got_sib[home, rs, :].astype(F32)
            g_in[rs, :] = gsum + got_chip[0, rs, :].astype(F32) + got_chip[2, rs, :].astype(F32)

        rows_loop(finish)

        for cp in gc_cps[1:]:
            cp.wait_recv()
        gc_cps[0].wait()
        for rc in d2d + ici + small_cps[1:] + gc_cps[1:]:
            rc.wait_send()

    vm = pl.BlockSpec(memory_space=pltpu.VMEM)
    anyspace = pl.BlockSpec(memory_space=pl.ANY)
    assert rows_in % rb == 0 and parts_in.dtype == BF16
    args = [parts_in, *pieces, c_rows, w_mod_loc]
    in_specs = [anyspace] + [vm] * (len(args) - 1)
    out_shape = [jax.ShapeDtypeStruct(shp, F32), jax.ShapeDtypeStruct(w_mod_loc.shape, F32),
                 jax.ShapeDtypeStruct((1, SMALL_W), F32), jax.ShapeDtypeStruct((1, 3 * D), F32),
                 jax.ShapeDtypeStruct((N_DEV, 8, D), F32), jax.ShapeDtypeStruct((1, 1), F32)]
    scratch = [pltpu.VMEM((4,) + shp, BF16), pltpu.VMEM((4,) + shp, BF16), pltpu.VMEM((3,) + shp, BF16),
               pltpu.VMEM((3,) + shp, BF16), pltpu.VMEM((8, ROW_W), F32), pltpu.VMEM((N_DEV, 8, ROW_W), F32),
               pltpu.VMEM((n_ex + 8, 3 * D), F32), pltpu.VMEM((8, D), F32),
               pltpu.SemaphoreType.DMA((4,)), pltpu.SemaphoreType.DMA((4,)), pltpu.SemaphoreType.DMA((3,)),
               pltpu.SemaphoreType.DMA((3,)), pltpu.SemaphoreType.DMA((4,)),
               pltpu.SemaphoreType.DMA((N_DEV - 1,)), pltpu.SemaphoreType.DMA((N_DEV - 1,)),
               pltpu.SemaphoreType.DMA((N_DEV - 1,)), pltpu.SemaphoreType.DMA((N_DEV - 1,)),
               pltpu.SemaphoreType.DMA((2,))]
    return pl.pallas_call(
        body, name="epilogue", out_shape=out_shape, in_specs=in_specs, out_specs=[vm] * len(out_shape),
        scratch_shapes=scratch, compiler_params=pltpu.CompilerParams(vmem_limit_bytes=VMEM_LIMIT),
    )(*args)


def _final_adamw(g_in, w_in_t, m_in_t, v_in_t, g_wm, w_mod_loc, m_mod, v_mod, summed, g_bmod, gc_all,
                 small_w, small_m, small_v):
    ns = len(_SMALL)
    rb = 32

    def body(*refs):
        it = iter(refs)
        take = lambda k: [next(it) for _ in range(k)]
        (gin_ref, w_ref, m_ref, v_ref, gwm_ref, wm_ref, mm_ref, vm_ref, sum_ref, gb_ref, gc_ref) = take(11)
        sw, sm, sv = take(ns), take(ns), take(ns)
        (d_in, nm_in, nv_in, d_wm, nm_wm, nv_wm) = take(6)
        souts = take(4 * ns)

        def big(g_r, w_r, m_r, v_r, d_o, nm_o, nv_o):
            def step(i, carry):
                rs = pl.ds(pl.multiple_of(i * rb, rb), rb)
                dl, m_new, v_new = _adamw(w_r[rs, :], g_r[rs, :], m_r[rs, :], v_r[rs, :])
                d_o[rs, :] = dl
                nm_o[rs, :] = m_new
                nv_o[rs, :] = v_new
                return carry
            lax.fori_loop(0, w_r.shape[0] // rb, step, 0)

        big(gin_ref, w_ref, m_ref, v_ref, d_in, nm_in, nv_in)
        big(gwm_ref, wm_ref, mm_ref, vm_ref, d_wm, nm_wm, nv_wm)
        for k, (name, off) in enumerate(_SMALL):
            w = sw[k][...]
            if name == "b_mod":
                gk = gb_ref[...]
            elif name == "c_ctx":
                acc = gc_ref[0, 0:1, :]
                for j in range(1, N_DEV):
                    acc = acc + gc_ref[j, 0:1, :]
                sg = _sigmoid(w)
                gk = acc * (sg * (1.0 + w * (1.0 - sg)))
            else:
                gk = sum_ref[:, off:off + w.shape[1]]
            dl, m_new, v_new = _adamw(w, gk, sm[k][...], sv[k][...])
            souts[k][...] = gk
            souts[ns + k][...] = dl
            souts[2 * ns + k][...] = m_new
            souts[3 * ns + k][...] = v_new

    assert w_in_t.shape[0] % rb == 0 and w_mod_loc.shape[0] % rb == 0
    big_shape = jax.ShapeDtypeStruct(w_in_t.shape, F32)
    mod_shape = jax.ShapeDtypeStruct(w_mod_loc.shape, F32)
    out_shape = [big_shape] * 3 + [mod_shape] * 3 + [jax.ShapeDtypeStruct(w.shape, F32) for w in small_w] * 4
    outs = pl.pallas_call(
        body, name="final_adamw", out_shape=out_shape,
        compiler_params=pltpu.CompilerParams(vmem_limit_bytes=VMEM_LIMIT),
    )(g_in, w_in_t, m_in_t, v_in_t, g_wm, w_mod_loc, m_mod, v_mod, summed, g_bmod, gc_all, *small_w, *small_m, *small_v)
    small_outs = [outs[6 + k * ns:6 + (k + 1) * ns] for k in range(4)]
    return (g_in,) + tuple(outs[0:3]), (g_wm,) + tuple(outs[3:6]), small_outs


def _rope_tables(s):
    t = jnp.arange(s, dtype=jnp.int32)
    row = (t // GRID_W).astype(F32)
    col = (t % GRID_W).astype(F32)
    freqs = ROPE_THETA ** (-jnp.arange(0, HD // 2, 2, dtype=F32) / (HD // 2))
    ang_r = row[:, None] * freqs[None, :]
    ang_c = col[:, None] * freqs[None, :]
    cr, sr, cc, sc = jnp.cos(ang_r), jnp.sin(ang_r), jnp.cos(ang_c), jnp.sin(ang_c)
    cos = jnp.concatenate([cr, cr, cc, cc], axis=-1)
    sins = jnp.concatenate([-sr, sr, -sc, sc], axis=-1)
    return jnp.tile(cos, (1, KVW // HD)), jnp.tile(sins, (1, KVW // HD))


def kernel(x, c, ctx, c_ctx, w_mod, b_mod, norm_w, w_in, q_norm_w, k_norm_w, conv_w, conv_b, conv_ln_w, conv_ln_b, w_pw, b_pw, w_out, loss_target, m_c_ctx, m_w_mod, m_b_mod, m_norm_w, m_w_in, m_q_norm_w, m_k_norm_w, m_conv_w, m_conv_b, m_conv_ln_w, m_conv_ln_b, m_w_pw, m_b_pw, m_w_out, v_c_ctx, v_w_mod, v_b_mod, v_norm_w, v_w_in, v_q_norm_w, v_k_norm_w, v_conv_w, v_conv_b, v_conv_ln_w, v_conv_ln_b, v_w_pw, v_b_pw, v_w_out):
    bl, s, _ = x.shape
    cl = ctx.shape[1]
    me = _lin(*_coords())
    n_mod = w_mod.shape[2]

    conv_w_pad = jnp.pad(conv_w[0], ((0, 32 - KW), (0, 0)))
    c_pad = jnp.pad(c, ((0, 8 - bl), (0, 0)))
    g_win, g_c = _all_gather_many([w_in[0].T.astype(BF16), c_pad], "gather_weights")
    w_in_b = g_win.reshape(D_IN, D)
    c_all = g_c[:, :bl, :].reshape(N_DEV * bl, D)
    n_ex = N_DEV * bl
    c_rows = jnp.concatenate([c_all, c_ctx[None, :], jnp.zeros((7, D), F32)], axis=0)

    b_mod_loc = lax.dynamic_slice_in_dim(b_mod, me * n_mod, n_mod, axis=1)
    mod_loc = _mod_fwd(c_rows, w_mod[0], b_mod_loc)
    g_mod = _all_gather_direct(mod_loc, "gather_mod")
    mod_all = g_mod.transpose(1, 0, 2).reshape(n_ex + 8, 3 * D)
    modrows = lax.dynamic_slice_in_dim(mod_all, me * bl, bl, axis=0).reshape(bl, 3, D)
    modc = mod_all[n_ex].reshape(1, 3, D)

    cos, sins = _rope_tables(s)
    qnw_t = jnp.tile(q_norm_w, (1, DA // HD))
    knw_t = jnp.tile(k_norm_w, (1, KVW // HD))
    lane = jnp.arange(DA, dtype=jnp.int32) // HD
    ones_bd = (lane[:, None] == lane[None, :]).astype(BF16)
    ones_kv = ones_bd[0:KVW, 0:KVW]
    w_kv_b = w_in_b

    k_ctx, v_ctx, pkv_c = _ctx_fwd(ctx, modc, norm_w, w_kv_b, knw_t, ones_kv, cl + s)
    (q_h, k_h, v_h, pq, pkv, za, glu, zc), (g_wout, g_wpw, g_cw) = _fwd_in(
        x, modrows, norm_w, w_in_b, cos, sins, qnw_t, knw_t, ones_bd, k_ctx, v_ctx,
        [w_out[0].astype(BF16), w_pw[0].astype(BF16), conv_w_pad])
    w_out_b = g_wout.reshape(D, D)
    w_pw_b = g_wpw.reshape(DC, DC)
    conv_w_full = g_cw.transpose(1, 0, 2).reshape(32, DC)
    attn, lse = _attn_fwd(q_h, k_h, v_h)
    y_conv, cp = _conv_fwd(glu, conv_w_full, conv_b, conv_ln_w, conv_ln_b, w_pw_b, b_pw)

    do_h, dza, dcp, dzc, dh, dgate, gw_out, loss_row = _out_fwd_bwd(attn, za, cp, zc, x, loss_target, modrows, w_out_b)
    dy_conv, gw_pw, conv_rows = _conv_bwd_pointwise(y_conv, dcp, conv_ln_w, conv_ln_b, w_pw_b)
    dglu, g_cw_full = _conv_bwd_depthwise(glu, dy_conv, conv_w_full)
    parts_out = gw_out.astype(BF16).reshape(N_DEV, D // N_DEV, D)
    parts_pw = gw_pw.astype(BF16).reshape(N_DEV, DC // N_DEV, DC)
    parts_cw = g_cw_full.astype(BF16).reshape(32, N_DEV, DC // N_DEV).transpose(1, 0, 2)
    (dq, dk_h, dv_h), (got_out, got_pw, got_cw) = _attn_bwd(
        q_h, k_h, v_h, do_h, attn, lse, [parts_out, parts_pw, parts_cw])
    gw_kv, ctx_rows, dknw_c = _ctx_bwd(ctx, modc, norm_w, w_kv_b, pkv_c, dk_h, dv_h, knw_t, ones_kv)
    grad_x, gw_in, dmod_ss, dnw, dqnw, dknw = _bwd_in(
        x, modrows, norm_w, w_in_b, cos, sins, qnw_t, knw_t, ones_bd,
        pq, pkv, dq, dk_h, dv_h, dza, dglu, dzc, dh, gw_kv)

    def pad_cw(a):
        return jnp.pad(a[0], ((0, 32 - KW), (0, 0)))

    r_out, r_pw, r_cw = _sum_devices_adamw(
        [(got_out, w_out[0], m_w_out[0], v_w_out[0]), (got_pw, w_pw[0], m_w_pw[0], v_w_pw[0]),
         (got_cw, pad_cw(conv_w), pad_cw(m_conv_w), pad_cw(v_conv_w))])
    r_cw = tuple(a[:KW] for a in r_cw)

    given = {"c_ctx": (c_ctx, m_c_ctx, v_c_ctx), "b_mod": (b_mod, m_b_mod, v_b_mod), "norm_w": (norm_w, m_norm_w, v_norm_w),
             "q_norm_w": (q_norm_w, m_q_norm_w, v_q_norm_w), "k_norm_w": (k_norm_w, m_k_norm_w, v_k_norm_w),
             "conv_b": (conv_b, m_conv_b, v_conv_b), "conv_ln_w": (conv_ln_w, m_conv_ln_w, v_conv_ln_w),
             "conv_ln_b": (conv_ln_b, m_conv_ln_b, v_conv_ln_b), "b_pw": (b_pw, m_b_pw, v_b_pw)}
    as_rows = [[given[name][which].reshape(1, -1) for name, _ in _SMALL] for which in range(3)]
    g_in_t, g_wmod, summed, g_bmod, gc_all, loss11 = _epilogue(
        gw_in.reshape(N_DEV, D_IN // N_DEV, D),
        [loss_row, ctx_rows, dnw, dqnw, dknw, dknw_c, conv_rows, dmod_ss, dgate], c_rows, w_mod[0])
    r_in, r_wmod, small_outs = _final_adamw(
        g_in_t, w_in[0].T, m_w_in[0].T, v_w_in[0].T, g_wmod, w_mod[0], m_w_mod[0], v_w_mod[0],
        summed, g_bmod, gc_all, *as_rows)
    r_in = tuple(a.T for a in r_in)

    big = {"w_mod": r_wmod, "w_in": r_in, "conv_w": r_cw, "w_pw": r_pw, "w_out": r_out}
    order = ["c_ctx", "w_mod", "b_mod", "norm_w", "w_in", "q_norm_w", "k_norm_w", "conv_w", "conv_b", "conv_ln_w",
             "conv_ln_b", "w_pw", "b_pw", "w_out"]
    small_index = {name: k for k, (name, _) in enumerate(_SMALL)}
    outs = [loss11.reshape(()), grad_x]
    for which in range(4):
        for name in order:
            if name in big:
                outs.append(big[name][which][None])
            else:
                outs.append(small_outs[which][small_index[name]].reshape(given[name][0].shape))
    return tuple(outs)
```

```python
import functools

import jax
import jax.numpy as jnp
from jax import lax
from jax.experimental import pallas as pl
from jax.experimental.pallas import tpu as pltpu

F32, BF16 = jnp.float32, jnp.bfloat16
MESH_ID = pl.DeviceIdType.MESH

N_DEV = 8
D = 1024
D_IN = 2816
DA = 512
DC = 512
HD = 64
KVW = 128
KW = 31
HALO = 16
EPS = 1e-6
ROPE_THETA = 10000.0
GRID_W = 64

ADAM_LR, ADAM_B1, ADAM_B2, ADAM_EPS, ADAM_WD, ADAM_STEP = 0.001, 0.9, 0.999, 1e-08, 0.01, 10

VMEM_LIMIT = 56 * 1024 * 1024

TM = 256
TQ = 128
TOKEN_PARTS = 2
BWD_PARTS = 2
FWD_PARTS = 4
TC = 512
CH = 64


def _params(sem, vmem=VMEM_LIMIT):
    return pltpu.CompilerParams(dimension_semantics=sem, vmem_limit_bytes=vmem)


def _dot(a, b):
    return jnp.dot(a, b, preferred_element_type=F32)


def _dot_nt(a, b):
    return lax.dot_general(a, b, (((1,), (1,)), ((), ())), preferred_element_type=F32)


def _dot_tn(a, b):
    return lax.dot_general(a, b, (((0,), (0,)), ((), ())), preferred_element_type=F32)


def _sigmoid(z):
    return 1.0 / (1.0 + jnp.exp(-z))


def _segsum(v, ones_bd):
    return _dot(v.astype(BF16), ones_bd)


def _swap16(x):
    w = x.shape[-1]
    lane = lax.broadcasted_iota(jnp.int32, x.shape, 1)
    return jnp.where((lane % 32) < 16, pltpu.roll(x, w - 16, 1), pltpu.roll(x, 16, 1))


def _with_ones_column(v):
    one = (lax.broadcasted_iota(jnp.int32, v.shape, 1) == 0).astype(v.dtype)
    return jnp.concatenate([v, one], axis=-1)


def _rope(x, cos, sins):
    return x * cos + _swap16(x) * sins


def _rope_bwd(d, cos, sins):
    return d * cos + _swap16(d * sins)


def _adamw(w, g, m, v):
    m2 = ADAM_B1 * m + (1.0 - ADAM_B1) * g
    v2 = ADAM_B2 * v + (1.0 - ADAM_B2) * (g * g)
    m_hat = m2 / (1.0 - ADAM_B1 ** ADAM_STEP)
    v_hat = v2 / (1.0 - ADAM_B2 ** ADAM_STEP)
    delta = -ADAM_LR * (m_hat / (jnp.sqrt(v_hat) + ADAM_EPS) + ADAM_WD * w)
    return delta, m2, v2


def _coords():
    return lax.axis_index("x"), lax.axis_index("y"), lax.axis_index("c")


def _lin(x, y, c):
    return 4 * x + 2 * y + c


def _all_gather_many(arrs, name):
    n = len(arrs)

    def body(*refs):
        in_refs, out_refs = refs[:n], refs[n:2 * n]
        send_sems, recv_sems, local_sems = refs[2 * n:]
        x, y, c = _coords()
        me, sib = (x, y, c), (x, y, 1 - c)
        xnb, ynb, diag = (1 - x, y), (x, 1 - y), (1 - x, 1 - y)
        north = c == 1

        def copy(a, k, block, to, src=None):
            slot = out_refs[a].at[_lin(*block)]
            return pltpu.make_async_remote_copy(
                src_ref=slot if src is None else src, dst_ref=slot,
                send_sem=send_sems.at[a * 7 + k], recv_sem=recv_sems.at[a * 7 + k],
                device_id=to, device_id_type=MESH_ID)

        mine = [pltpu.make_async_copy(in_refs[a], out_refs[a].at[_lin(*me)], local_sems.at[a]) for a in range(n)]
        for cp in mine:
            cp.start()
        first = []
        for a in range(n):
            first += [copy(a, 0, me, sib, src=in_refs[a]), copy(a, 1, me, (*xnb, c), src=in_refs[a]),
                      copy(a, 2, me, (*ynb, c), src=in_refs[a])]
        for cp in first:
            cp.start()
        passed, relays = [], []
        for a in range(n):
            relay_north = copy(a, 3, (*xnb, c), (*ynb, c))
            relay_south = copy(a, 3, (*ynb, c), (*xnb, c))
            copy(a, 1, (*xnb, c), me).wait_recv()
            pl.when(north)(relay_north.start)
            passed.append(copy(a, 4, (*xnb, c), sib))
            passed[-1].start()
            copy(a, 2, (*ynb, c), me).wait_recv()
            pl.when(jnp.logical_not(north))(relay_south.start)
            passed.append(copy(a, 5, (*ynb, c), sib))
            passed[-1].start()
            relays.append(relay_north)
        for a in range(n):
            copy(a, 3, (*diag, c), me).wait_recv()
            passed.append(copy(a, 6, (*diag, c), sib))
            passed[-1].start()
        for a in range(n):
            copy(a, 0, sib, me).wait_recv()
            for k, chip in ((4, xnb), (5, ynb), (6, diag)):
                copy(a, k, (*chip, 1 - c), me).wait_recv()
        for cp in first + passed + relays:
            cp.wait_send()
        for cp in mine:
            cp.wait()

    vm = pl.BlockSpec(memory_space=pltpu.VMEM)
    return pl.pallas_call(
        body, name=name,
        out_shape=[jax.ShapeDtypeStruct((N_DEV,) + a.shape, a.dtype) for a in arrs],
        in_specs=[vm] * n, out_specs=[vm] * n,
        scratch_shapes=[pltpu.SemaphoreType.DMA((7 * n,)), pltpu.SemaphoreType.DMA((7 * n,)),
                        pltpu.SemaphoreType.DMA((n,))],
        compiler_params=pltpu.CompilerParams(vmem_limit_bytes=VMEM_LIMIT),
    )(*arrs)


def _prologue(w_in_t_b, c_pad, bl, c_ctx_row, w_mod_loc, b_mod):
    n_ex = N_DEV * bl
    n_mod = w_mod_loc.shape[1]

    def body(w_ref, c_ref, cctx_ref, wm_ref, b_ref, out_w, crows_ref, mod_out, c_gath, mod_mine,
             w_send, w_recv, c_send, c_recv, m_send, m_recv, local_sems):
        x, y, c = _coords()
        me_lin = _lin(x, y, c)
        me, sib = (x, y, c), (x, y, 1 - c)
        xnb, ynb, diag = (1 - x, y), (x, 1 - y), (1 - x, 1 - y)
        north = c == 1

        def direct_gather(src, dst, send_sems, recv_sems, local_sem):
            cps = [pltpu.make_async_copy(src, dst.at[me_lin], local_sem)]
            for k in range(1, N_DEV):
                peer = (1 - x if k & 4 else x, 1 - y if k & 2 else y, 1 - c if k & 1 else c)
                cps.append(pltpu.make_async_remote_copy(
                    src_ref=src, dst_ref=dst.at[me_lin], send_sem=send_sems.at[k - 1], recv_sem=recv_sems.at[k - 1],
                    device_id=peer, device_id_type=MESH_ID))
            for cp in cps:
                cp.start()
            return cps

        def copy(k, block, to, src=None):
            slot = out_w.at[_lin(*block)]
            return pltpu.make_async_remote_copy(
                src_ref=slot if src is None else src, dst_ref=slot, send_sem=w_send.at[k], recv_sem=w_recv.at[k],
                device_id=to, device_id_type=MESH_ID)

        c_cps = direct_gather(c_ref, c_gath, c_send, c_recv, local_sems.at[0])
        mine = pltpu.make_async_copy(w_ref, out_w.at[me_lin], local_sems.at[1])
        mine.start()
        first = [copy(0, me, sib, src=w_ref), copy(1, me, (*xnb, c), src=w_ref), copy(2, me, (*ynb, c), src=w_ref)]
        for cp in first:
            cp.start()

        for cp in c_cps[1:]:
            cp.wait_recv()
        c_cps[0].wait()
        crows_ref[...] = jnp.zeros_like(crows_ref)
        for j in range(N_DEV):
            crows_ref[j * bl:(j + 1) * bl, :] = c_gath[j, 0:bl, :]
        crows_ref[n_ex:n_ex + 1, :] = cctx_ref[...]
        cr = crows_ref[...]
        act = (cr * _sigmoid(cr)).astype(BF16)
        mod_mine[...] = _dot(act, wm_ref[...].astype(BF16)) + b_ref[:, pl.ds(pl.multiple_of(me_lin * n_mod, 128), n_mod)]
        mod_cps = direct_gather(mod_mine, mod_out, m_send, m_recv, local_sems.at[2])

        relay_north = copy(3, (*xnb, c), (*ynb, c))
        relay_south = copy(3, (*ynb, c), (*xnb, c))
        passed = []
        copy(1, (*xnb, c), me).wait_recv()
        pl.when(north)(relay_north.start)
        passed.append(copy(4, (*xnb, c), sib))
        passed[-1].start()
        copy(2, (*ynb, c), me).wait_recv()
        pl.when(jnp.logical_not(north))(relay_south.start)
        passed.append(copy(5, (*ynb, c), sib))
        passed[-1].start()
        copy(3, (*diag, c), me).wait_recv()
        passed.append(copy(6, (*diag, c), sib))
        passed[-1].start()
        copy(0, sib, me).wait_recv()
        for k, chip in ((4, xnb), (5, ynb), (6, diag)):
            copy(k, (*chip, 1 - c), me).wait_recv()
        for cp in mod_cps[1:]:
            cp.wait_recv()
        mod_cps[0].wait()
        for cp in first + passed + [relay_north] + c_cps[1:] + mod_cps[1:]:
            cp.wait_send()
        mine.wait()

    vm = pl.BlockSpec(memory_space=pltpu.VMEM)
    seven = pltpu.SemaphoreType.DMA((N_DEV - 1,))
    return pl.pallas_call(
        body, name="prologue",
        out_shape=[jax.ShapeDtypeStruct((N_DEV,) + w_in_t_b.shape, BF16), jax.ShapeDtypeStruct((n_ex + 8, D), F32),
                   jax.ShapeDtypeStruct((N_DEV, n_ex + 8, n_mod), F32)],
        in_specs=[vm] * 5, out_specs=[vm] * 3,
        scratch_shapes=[pltpu.VMEM((N_DEV,) + c_pad.shape, F32), pltpu.VMEM((n_ex + 8, n_mod), F32),
                        seven, seven, seven, seven, seven, seven, pltpu.SemaphoreType.DMA((3,))],
        compiler_params=pltpu.CompilerParams(vmem_limit_bytes=VMEM_LIMIT),
    )(w_in_t_b, c_pad, c_ctx_row, w_mod_loc, b_mod)


def _exchange_copies(in_refs, out_refs, send_sems, recv_sems, local_sems, scatter):
    x, y, c = _coords()
    me = _lin(x, y, c)
    local, remote = [], []
    for a, (src, dst) in enumerate(zip(in_refs, out_refs)):
        local.append(pltpu.make_async_copy(src.at[me] if scatter else src, dst.at[me], local_sems.at[a]))
        for k in range(1, N_DEV):
            peer = (1 - x if k & 4 else x, 1 - y if k & 2 else y, 1 - c if k & 1 else c)
            remote.append(pltpu.make_async_remote_copy(
                src_ref=src.at[_lin(*peer)] if scatter else src, dst_ref=dst.at[me],
                send_sem=send_sems.at[a * (N_DEV - 1) + k - 1], recv_sem=recv_sems.at[a * (N_DEV - 1) + k - 1],
                device_id=peer, device_id_type=MESH_ID))
    return local, remote


def _exchange_scratch(n):
    return [pltpu.SemaphoreType.DMA((n * (N_DEV - 1),)), pltpu.SemaphoreType.DMA((n * (N_DEV - 1),)),
            pltpu.SemaphoreType.DMA((n,))]


def _all_gather_direct(arr, name):
    def body(in_ref, out_ref, send_sems, recv_sems, local_sem):
        x, y, c = _coords()
        me = _lin(x, y, c)
        mine = pltpu.make_async_copy(in_ref, out_ref.at[me], local_sem)
        mine.start()
        copies = []
        for k in range(1, N_DEV):
            peer = (1 - x if k & 4 else x, 1 - y if k & 2 else y, 1 - c if k & 1 else c)
            cp = pltpu.make_async_remote_copy(
                src_ref=in_ref, dst_ref=out_ref.at[me], send_sem=send_sems.at[k - 1], recv_sem=recv_sems.at[k - 1],
                device_id=peer, device_id_type=MESH_ID)
            cp.start()
            copies.append(cp)
        for cp in copies:
            cp.wait_recv()
        for cp in copies:
            cp.wait_send()
        mine.wait()

    vm = pl.BlockSpec(memory_space=pltpu.VMEM)
    return pl.pallas_call(
        body, name=name, out_shape=jax.ShapeDtypeStruct((N_DEV,) + arr.shape, arr.dtype),
        in_specs=[vm], out_specs=vm,
        scratch_shapes=[pltpu.SemaphoreType.DMA((N_DEV - 1,)), pltpu.SemaphoreType.DMA((N_DEV - 1,)),
                        pltpu.SemaphoreType.DMA],
    )(arr)


def _reduce_scatter_adamw(items, name):
    n = len(items)
    rb = 32

    def body(*refs):
        parts = refs[0:n]
        wmv = refs[n:4 * n]
        outs = refs[4 * n:8 * n]
        bufs = [refs[8 * n + 4 * a:8 * n + 4 * a + 4] for a in range(n)]
        d2d_send, d2d_recv, ici_send, ici_recv, local_sems = refs[12 * n:]
        x, y, c = _coords()
        sib = (x, y, 1 - c)
        peers = [(1 - x, y), (x, 1 - y), (1 - x, 1 - y)]
        home = 2 * x + y

        def rows_loop(rows, fn):
            def step(i, carry):
                fn(pl.ds(pl.multiple_of(i * rb, rb), rb))
                return carry
            lax.fori_loop(0, rows // rb, step, 0)

        local, d2d, ici = [], [], []
        for a in range(n):
            mine, got_sib = bufs[a][0], bufs[a][1]
            for s in range(4):
                cp = pltpu.make_async_copy(parts[a].at[_lin(s // 2, s % 2, c)], mine.at[s], local_sems.at[4 * a + s])
                cp.start()
                local.append(cp)
                rc = pltpu.make_async_remote_copy(
                    src_ref=parts[a].at[_lin(s // 2, s % 2, 1 - c)], dst_ref=got_sib.at[s],
                    send_sem=d2d_send.at[4 * a + s], recv_sem=d2d_recv.at[4 * a + s],
                    device_id=sib, device_id_type=MESH_ID)
                rc.start()
                d2d.append(rc)

        for a in range(n):
            mine, got_sib, stage, got_chip = bufs[a]
            for s in range(4):
                local[4 * a + s].wait()
                d2d[4 * a + s].wait_recv()
            for k, (px, py) in enumerate(peers):
                slot = 2 * px + py

                def pair_sum(rs, k=k, slot=slot, mine=mine, got_sib=got_sib, stage=stage):
                    stage[k, rs, :] = (mine[slot, rs, :].astype(F32) + got_sib[slot, rs, :].astype(F32)).astype(BF16)

                rows_loop(wmv[3 * a].shape[0], pair_sum)
                rc = pltpu.make_async_remote_copy(
                    src_ref=stage.at[k], dst_ref=got_chip.at[k],
                    send_sem=ici_send.at[3 * a + k], recv_sem=ici_recv.at[3 * a + k],
                    device_id=(px, py, c), device_id_type=MESH_ID)
                rc.start()
                ici.append(rc)

        for a in range(n):
            mine, got_sib, stage, got_chip = bufs[a]
            w_ref, m_ref, v_ref = wmv[3 * a:3 * a + 3]
            g_ref, d_ref, nm_ref, nv_ref = outs[4 * a:4 * a + 4]
            for k in range(3):
                ici[3 * a + k].wait_recv()

            def finish(rs, mine=mine, got_sib=got_sib, got_chip=got_chip, w_ref=w_ref, m_ref=m_ref, v_ref=v_ref,
                       g_ref=g_ref, d_ref=d_ref, nm_ref=nm_ref, nv_ref=nv_ref):
                g = mine[home, rs, :].astype(F32) + got_sib[home, rs, :].astype(F32)
                for k in range(3):
                    g = g + got_chip[k, rs, :].astype(F32)
                delta, m2, v2 = _adamw(w_ref[rs, :], g, m_ref[rs, :], v_ref[rs, :])
                g_ref[rs, :] = g
                d_ref[rs, :] = delta
                nm_ref[rs, :] = m2
                nv_ref[rs, :] = v2

            rows_loop(w_ref.shape[0], finish)

        for rc in d2d + ici:
            rc.wait_send()

    vm = pl.BlockSpec(memory_space=pltpu.VMEM)
    anyspace = pl.BlockSpec(memory_space=pl.ANY)
    args, in_specs, out_shape, scratch = [], [], [], []
    for parts, w, m, v in items:
        assert w.shape[0] % rb == 0 and parts.shape == (N_DEV,) + w.shape and parts.dtype == BF16
    args += [it[0] for it in items]
    in_specs += [anyspace] * n
    for _, w, m, v in items:
        args += [w, m, v]
        in_specs += [vm] * 3
        out_shape += [jax.ShapeDtypeStruct(w.shape, F32)] * 4
    for it in items:
        shp = it[1].shape
        scratch += [pltpu.VMEM((4,) + shp, BF16), pltpu.VMEM((4,) + shp, BF16),
                    pltpu.VMEM((3,) + shp, BF16), pltpu.VMEM((3,) + shp, BF16)]
    scratch += [pltpu.SemaphoreType.DMA((4 * n,)), pltpu.SemaphoreType.DMA((4 * n,)),
                pltpu.SemaphoreType.DMA((3 * n,)), pltpu.SemaphoreType.DMA((3 * n,)), pltpu.SemaphoreType.DMA((4 * n,))]
    outs = pl.pallas_call(
        body, name=name, out_shape=out_shape, in_specs=in_specs, out_specs=[vm] * (4 * n),
        scratch_shapes=scratch, compiler_params=pltpu.CompilerParams(vmem_limit_bytes=VMEM_LIMIT),
    )(*args)
    return [tuple(outs[4 * a:4 * a + 4]) for a in range(n)]


def _sum_devices_adamw(items):
    n = len(items)

    def body(*refs):
        for a in range(n):
            got, w_ref, m_ref, v_ref = refs[4 * a:4 * a + 4]
            g_ref, d_ref, nm_ref, nv_ref = refs[4 * n + 4 * a:4 * n + 4 * a + 4]
            g = got[0].astype(F32)
            for j in range(1, N_DEV):
                g = g + got[j].astype(F32)
            delta, m2, v2 = _adamw(w_ref[...], g, m_ref[...], v_ref[...])
            g_ref[...] = g
            d_ref[...] = delta
            nm_ref[...] = m2
            nv_ref[...] = v2

    args, out_shape = [], []
    for got, w, m, v in items:
        assert got.shape == (N_DEV,) + w.shape
        args += [got, w, m, v]
        out_shape += [jax.ShapeDtypeStruct(w.shape, F32)] * 4
    outs = pl.pallas_call(body, name="sum_devices_adamw", out_shape=out_shape,
                          compiler_params=pltpu.CompilerParams(vmem_limit_bytes=VMEM_LIMIT))(*args)
    return [tuple(outs[4 * a:4 * a + 4]) for a in range(n)]


def _mod_fwd(c_rows, w_mod_loc, b_mod_loc):
    def body(c_ref, w_ref, b_ref, o_ref):
        cr = c_ref[...]
        a = (cr * _sigmoid(cr)).astype(BF16)
        o_ref[...] = _dot(a, w_ref[...].astype(BF16)) + b_ref[...]

    return pl.pallas_call(
        body, name="mod_fwd", out_shape=jax.ShapeDtypeStruct((c_rows.shape[0], w_mod_loc.shape[1]), F32),
        compiler_params=pltpu.CompilerParams(vmem_limit_bytes=VMEM_LIMIT),
    )(c_rows, w_mod_loc, b_mod_loc)


def _fwd_in(x, modrows, norm_w, w_in_b, cos, sins, qnw_t, knw_t, ones_bd, k_all, v_all, shards, wire_dtypes):
    bl, s, _ = x.shape
    tm = TOKEN_PARTS * TM
    nt = s // tm
    n_sh = len(shards)

    def body(*refs):
        (x_ref, mod_ref, nw_ref, win_ref, cos_ref, sin_ref, qnw_ref, knw_ref, bd_ref, kin_ref, vin_ref) = refs[:11]
        shard_refs = refs[11:11 + n_sh]
        q_ref, k_ref, v_ref, pq_ref, pkv_ref, za_ref, glu_ref, zc_ref = refs[11 + n_sh:19 + n_sh]
        gathered_refs = refs[19 + n_sh:19 + 2 * n_sh]
        stage_refs = refs[19 + 2 * n_sh:19 + 3 * n_sh]
        send_sems, recv_sems, local_sems = refs[19 + 3 * n_sh:]
        b, i = pl.program_id(0), pl.program_id(1)
        local, remote = _exchange_copies(stage_refs, gathered_refs, send_sems, recv_sems, local_sems, scatter=False)

        @pl.when((b == 0) & (i == 0))
        def _():
            for src, stage in zip(shard_refs, stage_refs):
                stage[...] = src[...].astype(stage.dtype)
            for cp in local + remote:
                cp.start()

        shift = mod_ref[0, 0:1, :]
        scale = mod_ref[0, 1:2, :]
        for part in range(TOKEN_PARTS):
            rows = pl.ds(part * TM, TM)
            xv = x_ref[0, rows, :]
            r = lax.rsqrt(jnp.mean(xv * xv, axis=-1, keepdims=True) + EPS)
            u = (xv * r * nw_ref[...]) * (1.0 + scale) + shift
            p = _dot_nt(u.astype(BF16), win_ref[...])
            pq = p[:, 0:DA]
            pk = p[:, DA:DA + HD * 2]
            ck = cos_ref[rows, :]
            sk = sin_ref[rows, :]
            cs = jnp.concatenate([ck] * (DA // KVW), axis=-1)
            sn = jnp.concatenate([sk] * (DA // KVW), axis=-1)
            rq = lax.rsqrt(_segsum(pq * pq, bd_ref[...]) * (1.0 / HD) + EPS)
            qn = pq * rq * qnw_ref[...]
            qr = _rope(qn, cs, sn) * 0.125
            for h in range(DA // HD):
                q_ref[0, h, rows, :] = qr[:, h * HD:(h + 1) * HD].astype(BF16)
            rk = lax.rsqrt(_segsum(pk * pk, bd_ref[0:KVW, 0:KVW]) * (1.0 / HD) + EPS)
            kn = pk * rk * knw_ref[...]
            kr = _rope(kn, ck, sk)
            pv = p[:, 640:768]
            for h in range(KVW // HD):
                k_ref[0, h, rows, :] = kr[:, h * HD:(h + 1) * HD].astype(BF16)
                v_ref[0, h, rows, :] = _with_ones_column(pv[:, h * HD:(h + 1) * HD]).astype(BF16)
            pq_ref[0, rows, :] = pq
            pkv_ref[0, rows, :] = p[:, 512:768]
            za_ref[0, rows, :] = p[:, 768:1280]
            glu_ref[0, rows, :] = p[:, 1280:2304]
            zc_ref[0, rows, :] = p[:, 2304:2816]

        @pl.when((b == bl - 1) & (i == nt - 1))
        def _():
            for cp in remote:
                cp.wait_recv()
            for cp in remote:
                cp.wait_send()
            for cp in local:
                cp.wait()

    def tile(w):
        return pl.BlockSpec((1, tm, w), lambda b, i: (b, i, 0))

    def const(shape):
        return pl.BlockSpec(shape, lambda b, i: (0,) * len(shape))

    outs = [(DA, F32), (2 * KVW, F32), (DA, F32), (2 * DC, F32), (DC, F32)]
    anyspace = pl.BlockSpec(memory_space=pl.ANY)
    rope = pl.BlockSpec((tm, KVW), lambda b, i: (i, 0))
    k_tile = pl.BlockSpec((1, KVW // HD, tm, HD), lambda b, i: (b, 0, i, 0))
    v_tile = pl.BlockSpec((1, KVW // HD, tm, 2 * HD), lambda b, i: (b, 0, i, 0))
    res = pl.pallas_call(
        body, name="fwd_in", grid=(bl, nt),
        in_specs=[tile(D), pl.BlockSpec((1, 3, D), lambda b, i: (b, 0, 0)), const((1, D)), const((D_IN, D)),
                  rope, rope, const((1, DA)), const((1, KVW)), const((DA, DA)), anyspace, anyspace]
        + [const(a.shape) for a in shards],
        out_specs=[pl.BlockSpec((1, DA // HD, tm, HD), lambda b, i: (b, 0, i, 0)), k_tile, v_tile]
        + [tile(w) for w, _ in outs] + [anyspace] * n_sh,
        out_shape=[jax.ShapeDtypeStruct((bl, DA // HD, s, HD), BF16), jax.ShapeDtypeStruct(k_all.shape, BF16),
                   jax.ShapeDtypeStruct(v_all.shape, BF16)]
        + [jax.ShapeDtypeStruct((bl, s, w), dt) for w, dt in outs]
        + [jax.ShapeDtypeStruct((N_DEV,) + a.shape, dt) for a, dt in zip(shards, wire_dtypes)],
        input_output_aliases={9: 1, 10: 2},
        scratch_shapes=[pltpu.VMEM(a.shape, dt) for a, dt in zip(shards, wire_dtypes)] + _exchange_scratch(n_sh),
        compiler_params=_params(("arbitrary", "arbitrary")),
    )(x, modrows, norm_w, w_in_b, cos, sins, qnw_t, knw_t, ones_bd, k_all, v_all, *shards)
    return res[:8], res[8:]


_KV_ROWS_OF_W_IN_T = pl.BlockSpec((2 * KVW, D), lambda b: (DA // (2 * KVW), 0))


def _ctx_fwd(ctx, modc, norm_w, w_kv_b, knw_t, ones_bd, n_keys):
    bl, cl, _ = ctx.shape

    def body(x_ref, mod_ref, nw_ref, w_ref, knw_ref, bd_ref, k_ref, v_ref, pkv_ref):
        xv = x_ref[0]
        shift = mod_ref[0, 0:1, :]
        scale = mod_ref[0, 1:2, :]
        r = lax.rsqrt(jnp.mean(xv * xv, axis=-1, keepdims=True) + EPS)
        u = (xv * r * nw_ref[...]) * (1.0 + scale) + shift
        p = _dot_nt(u.astype(BF16), w_ref[...])
        pk = p[:, 0:KVW]
        rk = lax.rsqrt(_segsum(pk * pk, bd_ref[...]) * (1.0 / HD) + EPS)
        kn = pk * rk * knw_ref[...]
        pv = p[:, KVW:2 * KVW]
        for h in range(KVW // HD):
            k_ref[0, h] = kn[:, h * HD:(h + 1) * HD].astype(BF16)
            v_ref[0, h] = _with_ones_column(pv[:, h * HD:(h + 1) * HD]).astype(BF16)
        pkv_ref[0] = p

    def const(shape):
        return pl.BlockSpec(shape, lambda b: (0,) * len(shape))

    def tile(w):
        return pl.BlockSpec((1, cl, w), lambda b: (b, 0, 0))

    ctx_block = (n_keys - cl) // cl
    assert ctx_block * cl + cl == n_keys
    k_tile = pl.BlockSpec((1, KVW // HD, cl, HD), lambda b: (b, 0, ctx_block, 0))
    v_tile = pl.BlockSpec((1, KVW // HD, cl, 2 * HD), lambda b: (b, 0, ctx_block, 0))
    return pl.pallas_call(
        body, name="ctx_fwd", grid=(bl,),
        in_specs=[tile(D), const((1, 3, D)), const((1, D)), _KV_ROWS_OF_W_IN_T, const((1, KVW)), const((KVW, KVW))],
        out_specs=[k_tile, v_tile, tile(2 * KVW)],
        out_shape=[jax.ShapeDtypeStruct((bl, KVW // HD, n_keys, HD), BF16),
                   jax.ShapeDtypeStruct((bl, KVW // HD, n_keys, 2 * HD), BF16),
                   jax.ShapeDtypeStruct((bl, cl, 2 * KVW), F32)],
        compiler_params=_params(("arbitrary",)),
    )(ctx, modc, norm_w, w_kv_b, knw_t, ones_bd)


def _attn_fwd(q, k, v1):
    bl, _, s, _ = q.shape
    n_keys = k.shape[2]

    def body(q_ref, k_ref, v_ref, o_ref, lse_ref):
        kv = k_ref[0, 0]
        vv = v_ref[0, 0]
        lane = lax.broadcasted_iota(jnp.int32, (TQ, 2 * HD), 1)
        for part in range(FWD_PARTS):
            rows = pl.ds(part * TQ, TQ)
            lse = jnp.zeros((TQ, 2 * HD), F32)
            heads = []
            sc_all = _dot_nt(q_ref[0, :, rows, :].reshape(4 * TQ, HD), kv)
            for h in range(4):
                sc = sc_all[h * TQ:(h + 1) * TQ, :]
                m = jnp.max(sc, axis=-1, keepdims=True)
                e = jnp.exp(sc - m).astype(BF16)
                ov = _dot(e, vv)
                denom = ov[:, HD:HD + 1]
                heads.append(ov[:, 0:HD] * (1.0 / denom))
                lse = jnp.where(lane == h, m + jnp.log(denom), lse)
            o_ref[0, rows, :] = jnp.concatenate(heads, axis=-1)
            lse_ref[0, 0, rows, :] = lse

    tq = FWD_PARTS * TQ
    ks = pl.BlockSpec((1, 1, n_keys, HD), lambda b, g, i: (b, g, 0, 0))
    qs = pl.BlockSpec((1, 4, tq, HD), lambda b, g, i: (b, g, i, 0))
    vs = pl.BlockSpec((1, 1, n_keys, 2 * HD), lambda b, g, i: (b, g, 0, 0))
    return pl.pallas_call(
        body, name="attn_fwd", grid=(bl, 2, s // tq), in_specs=[qs, ks, vs],
        out_specs=[pl.BlockSpec((1, tq, 4 * HD), lambda b, g, i: (b, i, g)),
                   pl.BlockSpec((1, 1, tq, 2 * HD), lambda b, g, i: (b, g, i, 0))],
        out_shape=[jax.ShapeDtypeStruct((bl, s, DA), F32), jax.ShapeDtypeStruct((bl, 2, s, 2 * HD), F32)],
        compiler_params=_params(("arbitrary", "arbitrary", "arbitrary")),
    )(q, k, v1)


def _attn_bwd(q, k, v1, do, o, lse, exchange):
    bl, _, s, _ = q.shape
    n_keys = k.shape[2]
    tq = BWD_PARTS * TQ
    nq = s // tq
    n_ex = len(exchange)

    def body(*refs):
        q_ref, k_ref, v_ref, do_ref, o_ref, lse_ref = refs[:6]
        part_refs = refs[6:6 + n_ex]
        dq_ref, dk_ref, dv_ref = refs[6 + n_ex:9 + n_ex]
        got_refs = refs[9 + n_ex:9 + 2 * n_ex]
        p_sc, ds_sc, dkt, dvt, send_sems, recv_sems, local_sems = refs[9 + 2 * n_ex:]
        i = pl.program_id(2)
        first = (pl.program_id(0) == 0) & (pl.program_id(1) == 0) & (i == 0)
        last = (pl.program_id(0) == bl - 1) & (pl.program_id(1) == 1) & (i == nq - 1)
        local, remote = _exchange_copies(part_refs, got_refs, send_sems, recv_sems, local_sems, scatter=True)

        @pl.when(first)
        def _():
            for cp in local + remote:
                cp.start()

        @pl.when(i == 0)
        def _():
            dkt[...] = jnp.zeros_like(dkt)
            dvt[...] = jnp.zeros_like(dvt)

        kv = k_ref[0, 0]
        vv = v_ref[0, 0][:, 0:HD]
        for part in range(BWD_PARTS):
            tq_rows = pl.ds(part * TQ, TQ)
            lse = lse_ref[0, 0, tq_rows, :]
            ov = o_ref[0, tq_rows, :]
            dqs = []
            q_cat = q_ref[0, :, tq_rows, :].reshape(4 * TQ, HD)
            do_cat = do_ref[0, :, tq_rows, :].reshape(4 * TQ, HD)
            sc_all = _dot_nt(q_cat, kv)
            for h in range(4):
                doh = do_cat[h * TQ:(h + 1) * TQ, :]
                delta = jnp.sum(ov[:, h * HD:(h + 1) * HD] * doh.astype(F32), axis=-1, keepdims=True)
                rows = pl.ds((part * 4 + h) * TQ, TQ)
                p = jnp.exp(sc_all[h * TQ:(h + 1) * TQ, :] - lse[:, h:h + 1])
                ds = (p * (_dot_nt(doh, vv) - delta)).astype(BF16)
                p_sc[rows, :] = p.astype(BF16)
                ds_sc[rows, :] = ds
                dqs.append(_dot(ds, kv) * 0.125)
            dq_ref[0, tq_rows, :] = jnp.concatenate(dqs, axis=-1)
            part_rows = pl.ds(part * 4 * TQ, 4 * TQ)
            dvt[...] += _dot_tn(do_cat, p_sc[part_rows, :])
            dkt[...] += _dot_tn(q_cat, ds_sc[part_rows, :])

        @pl.when(i == nq - 1)
        def _():
            dk_ref[0, 0] = dkt[...].T
            dv_ref[0, 0] = dvt[...].T

        @pl.when(last)
        def _():
            for cp in remote:
                cp.wait_recv()
            for cp in remote:
                cp.wait_send()
            for cp in local:
                cp.wait()

    qs = pl.BlockSpec((1, 4, tq, HD), lambda b, g, i: (b, g, i, 0))
    ks = pl.BlockSpec((1, 1, n_keys, HD), lambda b, g, i: (b, g, 0, 0))
    vs = pl.BlockSpec((1, 1, n_keys, 2 * HD), lambda b, g, i: (b, g, 0, 0))
    os_ = pl.BlockSpec((1, tq, 4 * HD), lambda b, g, i: (b, i, g))
    kshape = jax.ShapeDtypeStruct(k.shape, F32)
    anyspace = pl.BlockSpec(memory_space=pl.ANY)
    res = pl.pallas_call(
        body, name="attn_bwd", grid=(bl, 2, nq),
        in_specs=[qs, ks, vs, qs, os_, pl.BlockSpec((1, 1, tq, 2 * HD), lambda b, g, i: (b, g, i, 0))]
        + [anyspace] * n_ex,
        out_specs=[os_, ks, ks] + [anyspace] * n_ex,
        out_shape=[jax.ShapeDtypeStruct((bl, s, DA), F32), kshape, kshape]
        + [jax.ShapeDtypeStruct(a.shape, a.dtype) for a in exchange],
        scratch_shapes=[pltpu.VMEM((4 * tq, n_keys), BF16), pltpu.VMEM((4 * tq, n_keys), BF16),
                        pltpu.VMEM((HD, n_keys), F32), pltpu.VMEM((HD, n_keys), F32)] + _exchange_scratch(n_ex),
        compiler_params=_params(("arbitrary", "arbitrary", "arbitrary")),
    )(q, k, v1, do, o, lse, *exchange)
    return res[:3], res[3:]


def _halo_specs(width, s):
    per = TC // HALO
    last = s // HALO - 1
    main = pl.BlockSpec((1, TC, width), lambda b, i: (b, i, 0))
    prev = pl.BlockSpec((1, HALO, width), lambda b, i: (b, jnp.maximum(i * per - 1, 0), 0))
    nxt = pl.BlockSpec((1, HALO, width), lambda b, i: (b, jnp.minimum((i + 1) * per, last), 0))
    return main, prev, nxt


def _glu(g):
    return g[:, 0:DC] * _sigmoid(g[:, DC:2 * DC])


def _fill_padded(pad_ref, main, prev, nxt, first, last):
    pad_ref[0:HALO, :] = jnp.where(first, 0.0, prev)
    pad_ref[HALO:HALO + TC, :] = main
    pad_ref[HALO + TC:2 * HALO + TC, :] = jnp.where(last, 0.0, nxt)


PLANE_ROWS = TC + 2 * HALO - 8


def _shift_planes(pad_ref, planes_ref):
    for r in range(1, 8):
        planes_ref[r - 1] = pad_ref[pl.ds(r, PLANE_ROWS), :]


def _tap_rows(pad_ref, planes_ref, offset, start, n):
    a, r = divmod(offset, 8)
    if r == 0:
        return pad_ref[pl.ds(start + 8 * a, n), :]
    return planes_ref[r - 1, pl.ds(start + 8 * a, n), :]


def _conv_fwd(glu, conv_w, conv_b, ln_w, ln_b, w_pw_b, b_pw):
    bl, s, _ = glu.shape
    nt = s // TC

    def body(g_ref, gp_ref, gn_ref, cw_ref, cb_ref, lw_ref, lb_ref, wpw_ref, bpw_ref, y_ref, cp_ref, pad_ref, planes_ref):
        i = pl.program_id(1)
        _fill_padded(pad_ref, _glu(g_ref[0]), _glu(gp_ref[0]), _glu(gn_ref[0]), i == 0, i == nt - 1)
        _shift_planes(pad_ref, planes_ref)
        for ck in range(TC // CH):
            acc = jnp.zeros((CH, DC), F32) + cb_ref[...]
            for t in range(KW):
                acc = acc + _tap_rows(pad_ref, planes_ref, 1 + t, ck * CH, CH) * cw_ref[t:t + 1, :]
            y_ref[0, pl.ds(ck * CH, CH), :] = acc
        y = y_ref[0]
        mu = jnp.mean(y, axis=-1, keepdims=True)
        yc = y - mu
        var = jnp.mean(yc * yc, axis=-1, keepdims=True)
        z = yc * lax.rsqrt(var + EPS) * lw_ref[...] + lb_ref[...]
        act = z * _sigmoid(z)
        cp_ref[0] = _dot(act.astype(BF16), wpw_ref[...]) + bpw_ref[...]

    def const(shape):
        return pl.BlockSpec(shape, lambda b, i: (0,) * len(shape))

    main, prev, nxt = _halo_specs(2 * DC, s)
    tile = pl.BlockSpec((1, TC, DC), lambda b, i: (b, i, 0))
    return pl.pallas_call(
        body, name="conv_fwd", grid=(bl, nt),
        in_specs=[main, prev, nxt, const((32, DC)), const((1, DC)), const((1, DC)), const((1, DC)),
                  const((DC, DC)), const((1, DC))],
        out_specs=[tile, tile],
        out_shape=[jax.ShapeDtypeStruct((bl, s, DC), F32)] * 2,
        scratch_shapes=[pltpu.VMEM((TC + 2 * HALO, DC), F32), pltpu.VMEM((7, PLANE_ROWS, DC), F32)],
        compiler_params=_params(("arbitrary", "arbitrary")),
    )(glu, glu, glu, conv_w, conv_b, ln_w, ln_b, w_pw_b, b_pw)


def _conv_bwd_pointwise(y, dcp, ln_w, ln_b, w_pw_b):
    bl, s, _ = y.shape
    nt = s // TM

    def body(y_ref, dcp_ref, lw_ref, lb_ref, wpw_ref, dy_ref, gwb_ref, rows_ref, gw_ref):
        @pl.when((pl.program_id(0) == 0) & (pl.program_id(1) == 0))
        def _():
            gw_ref[...] = jnp.zeros_like(gw_ref)
            rows_ref[...] = jnp.zeros_like(rows_ref)

        y = y_ref[0]
        dcp = dcp_ref[0]
        mu = jnp.mean(y, axis=-1, keepdims=True)
        yc = y - mu
        rstd = lax.rsqrt(jnp.mean(yc * yc, axis=-1, keepdims=True) + EPS)
        yn = yc * rstd
        lw = lw_ref[...]
        z = yn * lw + lb_ref[...]
        sg = _sigmoid(z)
        act = z * sg
        dcp_b = dcp.astype(BF16)
        gw_ref[...] += _dot_tn(act.astype(BF16), dcp_b)
        dact = _dot_nt(dcp_b, wpw_ref[...])
        dz = dact * (sg * (1.0 + z * (1.0 - sg)))
        dyn = dz * lw
        dy = rstd * (dyn - jnp.mean(dyn, axis=-1, keepdims=True) - yn * jnp.mean(dyn * yn, axis=-1, keepdims=True))
        dy_ref[0] = dy
        rows_ref[0:1, :] += jnp.sum(dcp, axis=0, keepdims=True)
        rows_ref[1:2, :] += jnp.sum(dz * yn, axis=0, keepdims=True)
        rows_ref[2:3, :] += jnp.sum(dz, axis=0, keepdims=True)
        rows_ref[3:4, :] += jnp.sum(dy, axis=0, keepdims=True)

        @pl.when((pl.program_id(0) == bl - 1) & (pl.program_id(1) == nt - 1))
        def _():
            gwb_ref[...] = gw_ref[...].astype(BF16)

    def const(shape):
        return pl.BlockSpec(shape, lambda b, i: (0,) * len(shape))

    tile = pl.BlockSpec((1, TM, DC), lambda b, i: (b, i, 0))
    return pl.pallas_call(
        body, name="conv_bwd_pointwise", grid=(bl, nt),
        in_specs=[tile, tile, const((1, DC)), const((1, DC)), const((DC, DC))],
        out_specs=[tile, const((DC, DC)), const((8, DC))],
        out_shape=[jax.ShapeDtypeStruct((bl, s, DC), F32), jax.ShapeDtypeStruct((DC, DC), BF16),
                   jax.ShapeDtypeStruct((8, DC), F32)],
        scratch_shapes=[pltpu.VMEM((DC, DC), F32)],
        compiler_params=_params(("arbitrary", "arbitrary")),
    )(y, dcp, ln_w, ln_b, w_pw_b)


def _conv_bwd_depthwise(glu, dy, conv_w):
    bl, s, _ = glu.shape
    nt = s // TC

    def body(g_ref, gp_ref, gn_ref, d_ref, dp_ref, dn_ref, cw_ref, dglu_ref, dcw_ref,
             padu_ref, padd_ref, planes_u, planes_d):
        i = pl.program_id(1)

        @pl.when((pl.program_id(0) == 0) & (i == 0))
        def _():
            dcw_ref[...] = jnp.zeros_like(dcw_ref)

        first, last = i == 0, i == nt - 1
        _fill_padded(padu_ref, _glu(g_ref[0]), _glu(gp_ref[0]), _glu(gn_ref[0]), first, last)
        _fill_padded(padd_ref, d_ref[0], dp_ref[0], dn_ref[0], first, last)
        _shift_planes(padu_ref, planes_u)
        _shift_planes(padd_ref, planes_d)
        for ck in range(TC // CH):
            acc = jnp.zeros((CH, DC), F32)
            for t in range(KW):
                acc = acc + _tap_rows(padd_ref, planes_d, 2 * HALO - 1 - t, ck * CH, CH) * cw_ref[t:t + 1, :]
            g = g_ref[0, pl.ds(ck * CH, CH), :]
            a = g[:, 0:DC]
            sg = _sigmoid(g[:, DC:2 * DC])
            dglu_ref[0, pl.ds(ck * CH, CH), 0:DC] = (acc * sg).astype(BF16)
            dglu_ref[0, pl.ds(ck * CH, CH), DC:2 * DC] = (acc * a * sg * (1.0 - sg)).astype(BF16)
        group = 4
        for t0 in range(0, KW, group):
            taps = range(t0, min(t0 + group, KW))
            acc8 = [jnp.zeros((8, DC), F32) for _ in taps]
            for ck in range(TC // CH):
                dchunk = d_ref[0, pl.ds(ck * CH, CH), :]
                for n, t in enumerate(taps):
                    prod = _tap_rows(padu_ref, planes_u, 1 + t, ck * CH, CH) * dchunk
                    acc8[n] = acc8[n] + jnp.sum(prod.reshape(CH // 8, 8, DC), axis=0)
            for n, t in enumerate(taps):
                dcw_ref[t:t + 1, :] += jnp.sum(acc8[n], axis=0, keepdims=True)

    gmain, gprev, gnext = _halo_specs(2 * DC, s)
    dmain, dprev, dnext = _halo_specs(DC, s)
    cw = pl.BlockSpec((32, DC), lambda b, i: (0, 0))
    return pl.pallas_call(
        body, name="conv_bwd_depthwise", grid=(bl, nt),
        in_specs=[gmain, gprev, gnext, dmain, dprev, dnext, cw],
        out_specs=[gmain, cw],
        out_shape=[jax.ShapeDtypeStruct((bl, s, 2 * DC), BF16), jax.ShapeDtypeStruct((32, DC), F32)],
        scratch_shapes=[pltpu.VMEM((TC + 2 * HALO, DC), F32)] * 2 + [pltpu.VMEM((7, PLANE_ROWS, DC), F32)] * 2,
        compiler_params=_params(("arbitrary", "arbitrary")),
    )(glu, glu, glu, dy, dy, dy, conv_w)


def _out_fwd_bwd(attn, za, cp, zc, x, target, modrows, w_out_b):
    bl, s, _ = x.shape
    tm = TOKEN_PARTS * TM

    def body(o_ref, za_ref, cp_ref, zc_ref, x_ref, t_ref, mod_ref, w_ref,
             do_ref, dza_ref, dcp_ref, dzc_ref, dh_ref, dgate_ref, gwb_ref, loss_ref, gw_ref):
        b, i = pl.program_id(0), pl.program_id(1)

        @pl.when((b == 0) & (i == 0))
        def _():
            gw_ref[...] = jnp.zeros_like(gw_ref)
            loss_ref[...] = jnp.zeros_like(loss_ref)

        @pl.when(i == 0)
        def _():
            dgate_ref[...] = jnp.zeros_like(dgate_ref)

        gate = mod_ref[0, 2:3, :]
        w = w_ref[...]
        mixes, douts = [], []
        for part in range(TOKEN_PARTS):
            rows = pl.ds(part * TM, TM)
            o, za_v, cp_v, zc_v = o_ref[0, rows, :], za_ref[0, rows, :], cp_ref[0, rows, :], zc_ref[0, rows, :]
            sa = _sigmoid(za_v)
            sc = _sigmoid(zc_v)
            silu_a = za_v * sa
            silu_c = zc_v * sc
            mix = jnp.concatenate([(o * silu_a).astype(BF16), (cp_v * silu_c).astype(BF16)], axis=-1)
            out = _dot(mix, w)
            err = x_ref[0, rows, :] + gate * out - t_ref[0, rows, :]
            loss_ref[...] += jnp.sum(err * err, axis=0, keepdims=True)
            dh = err * (1.0 / D)
            dh_ref[0, rows, :] = dh
            dgate_ref[0] += jnp.sum(dh * out, axis=0, keepdims=True)
            dout = (dh * gate).astype(BF16)
            mixes.append(mix)
            douts.append(dout)
            dmix = _dot_nt(dout, w)
            dga = dmix[:, 0:DA]
            dgc = dmix[:, DA:DA + DC]
            dov = dga * silu_a
            for h in range(DA // HD):
                do_ref[0, h, rows, :] = dov[:, h * HD:(h + 1) * HD].astype(BF16)
            dza_ref[0, rows, :] = (dga * o * (sa * (1.0 + za_v * (1.0 - sa)))).astype(BF16)
            dcp_ref[0, rows, :] = dgc * silu_c
            dzc_ref[0, rows, :] = (dgc * cp_v * (sc * (1.0 + zc_v * (1.0 - sc)))).astype(BF16)
        gw_ref[...] += _dot_tn(jnp.concatenate(mixes, axis=0), jnp.concatenate(douts, axis=0))

        @pl.when((b == bl - 1) & (i == s // tm - 1))
        def _():
            gwb_ref[...] = gw_ref[...].astype(BF16)

    def const(shape):
        return pl.BlockSpec(shape, lambda b, i: (0,) * len(shape))

    def tile(w):
        return pl.BlockSpec((1, tm, w), lambda b, i: (b, i, 0))

    return pl.pallas_call(
        body, name="out_fwd_bwd", grid=(bl, s // tm),
        in_specs=[tile(DA), tile(DA), tile(DC), tile(DC), tile(D), tile(D),
                  pl.BlockSpec((1, 3, D), lambda b, i: (b, 0, 0)), const((D, D))],
        out_specs=[pl.BlockSpec((1, DA // HD, tm, HD), lambda b, i: (b, 0, i, 0)), tile(DA), tile(DC), tile(DC), tile(D),
                   pl.BlockSpec((1, 1, D), lambda b, i: (b, 0, 0)), const((D, D)), const((1, D))],
        out_shape=[jax.ShapeDtypeStruct((bl, DA // HD, s, HD), BF16), jax.ShapeDtypeStruct((bl, s, DA), BF16),
                   jax.ShapeDtypeStruct((bl, s, DC), F32), jax.ShapeDtypeStruct((bl, s, DC), BF16),
                   jax.ShapeDtypeStruct((bl, s, D), F32), jax.ShapeDtypeStruct((bl, 1, D), F32),
                   jax.ShapeDtypeStruct((D, D), BF16), jax.ShapeDtypeStruct((1, D), F32)],
        scratch_shapes=[pltpu.VMEM((D, D), F32)],
        compiler_params=_params(("arbitrary", "arbitrary")),
    )(attn, za, cp, zc, x, target, modrows, w_out_b)


def _rms_heads_bwd(dy, x, w_t, ones_bd):
    r = lax.rsqrt(_segsum(x * x, ones_bd) * (1.0 / HD) + EPS)
    xh = x * r
    g = dy * w_t
    dx = r * (g - xh * (_segsum(g * xh, ones_bd) * (1.0 / HD)))
    return dx, dy * xh


def _ctx_bwd(ctx, modc, norm_w, w_kv_b, pkv_c, dk_c, dv_c, knw_t, ones_bd):
    bl, cl, _ = ctx.shape

    def body(x_ref, mod_ref, nw_ref, w_ref, p_ref, dk_ref, dv_ref, knw_ref, bd_ref, gw_ref, rows_ref, dknw_ref):
        @pl.when(pl.program_id(0) == 0)
        def _():
            gw_ref[...] = jnp.zeros_like(gw_ref)
            rows_ref[...] = jnp.zeros_like(rows_ref)
            dknw_ref[...] = jnp.zeros_like(dknw_ref)

        xv = x_ref[0]
        shift = mod_ref[0, 0:1, :]
        scale = mod_ref[0, 1:2, :]
        nw = nw_ref[...]
        r = lax.rsqrt(jnp.mean(xv * xv, axis=-1, keepdims=True) + EPS)
        xn = xv * r
        yv = xn * nw
        u = yv * (1.0 + scale) + shift
        dkv = jnp.concatenate([dk_ref[0, 0], dk_ref[0, 1]], axis=-1)
        dpk, dknw = _rms_heads_bwd(dkv, p_ref[0][:, 0:KVW], knw_ref[...], bd_ref[...])
        dp = jnp.concatenate([dpk.astype(BF16), dv_ref[0, 0].astype(BF16), dv_ref[0, 1].astype(BF16)], axis=-1)
        gw_ref[...] += _dot_tn(dp, u.astype(BF16))
        du = _dot(dp, w_ref[...])
        rows_ref[0:1, :] += jnp.sum(du, axis=0, keepdims=True)
        rows_ref[1:2, :] += jnp.sum(du * yv, axis=0, keepdims=True)
        rows_ref[2:3, :] += jnp.sum(du * (1.0 + scale) * xn, axis=0, keepdims=True)
        dknw_ref[...] += jnp.sum(dknw, axis=0, keepdims=True)

    def const(shape):
        return pl.BlockSpec(shape, lambda b: (0,) * len(shape))

    def tile(w):
        return pl.BlockSpec((1, cl, w), lambda b: (b, 0, 0))

    ctx_block = (dk_c.shape[2] - cl) // cl
    kv_tile = pl.BlockSpec((1, KVW // HD, cl, HD), lambda b: (b, 0, ctx_block, 0))
    return pl.pallas_call(
        body, name="ctx_bwd", grid=(bl,),
        in_specs=[tile(D), const((1, 3, D)), const((1, D)), _KV_ROWS_OF_W_IN_T, tile(2 * KVW), kv_tile, kv_tile,
                  const((1, KVW)), const((KVW, KVW))],
        out_specs=[const((2 * KVW, D)), const((8, D)), const((1, KVW))],
        out_shape=[jax.ShapeDtypeStruct((2 * KVW, D), F32), jax.ShapeDtypeStruct((8, D), F32),
                   jax.ShapeDtypeStruct((1, KVW), F32)],
        compiler_params=_params(("arbitrary",)),
    )(ctx, modc, norm_w, w_kv_b, pkv_c, dk_c, dv_c, knw_t, ones_bd)


def _bwd_in(x, modrows, norm_w, w_in_b, cos, sins, qnw_t, knw_t, ones_bd,
            pq, pkv, dq, dk, dv, dza, dglu, dzc, dh, gw_kv):
    bl, s, _ = x.shape
    tm = TOKEN_PARTS * TM
    nt = s // tm

    def body(x_ref, mod_ref, nw_ref, win_hbm, cos_ref, sin_ref, qnw_ref, knw_ref, bd_ref,
             pq_ref, pkv_ref, dq_ref, dk_ref, dv_ref, dza_ref, dglu_ref, dzc_ref, dh_ref, gwkv_ref,
             gx_ref, gw_hbm, dmod_ref, dnw_ref, dqnw_ref, dknw_ref, win_ref, gw_acc, sem):
        b, i = pl.program_id(0), pl.program_id(1)

        @pl.when((b == 0) & (i == 0))
        def _():
            cp = pltpu.make_async_copy(win_hbm, win_ref, sem)
            cp.start()
            gw_acc[...] = jnp.zeros_like(gw_acc)
            dnw_ref[...] = jnp.zeros_like(dnw_ref)
            dqnw_ref[...] = jnp.zeros_like(dqnw_ref)
            dknw_ref[...] = jnp.zeros_like(dknw_ref)
            cp.wait()

        @pl.when(i == 0)
        def _():
            dmod_ref[...] = jnp.zeros_like(dmod_ref)

        bd = bd_ref[...]
        shift = mod_ref[0, 0:1, :]
        scale = mod_ref[0, 1:2, :]
        nw = nw_ref[...]
        dps, us = [], []
        for part in range(TOKEN_PARTS):
            rows = pl.ds(part * TM, TM)
            ck = cos_ref[rows, :]
            sk = sin_ref[rows, :]
            cs = jnp.concatenate([ck] * (DA // KVW), axis=-1)
            sn = jnp.concatenate([sk] * (DA // KVW), axis=-1)
            dqn = _rope_bwd(dq_ref[0, rows, :], cs, sn)
            dpq, dqnw = _rms_heads_bwd(dqn, pq_ref[0, rows, :], qnw_ref[...], bd)
            dkn = _rope_bwd(jnp.concatenate([dk_ref[0, 0, rows, :], dk_ref[0, 1, rows, :]], axis=-1), ck, sk)
            dpk, dknw = _rms_heads_bwd(dkn, pkv_ref[0, rows, 0:KVW], knw_ref[...], bd[0:KVW, 0:KVW])
            dqnw_ref[...] += jnp.sum(dqnw, axis=0, keepdims=True)
            dknw_ref[...] += jnp.sum(dknw, axis=0, keepdims=True)
            dp = jnp.concatenate(
                [dpq.astype(BF16), dpk.astype(BF16), dv_ref[0, 0, rows, :].astype(BF16), dv_ref[0, 1, rows, :].astype(BF16),
                 dza_ref[0, rows, :], dglu_ref[0, rows, :], dzc_ref[0, rows, :]], axis=-1)

            xv = x_ref[0, rows, :]
            r = lax.rsqrt(jnp.mean(xv * xv, axis=-1, keepdims=True) + EPS)
            xn = xv * r
            yv = xn * nw
            u = yv * (1.0 + scale) + shift
            dps.append(dp)
            us.append(u.astype(BF16))
            du = _dot(dp, win_ref[...])
            dmod_ref[0, 0:1, :] += jnp.sum(du, axis=0, keepdims=True)
            dmod_ref[0, 1:2, :] += jnp.sum(du * yv, axis=0, keepdims=True)
            dy = du * (1.0 + scale)
            dnw_ref[...] += jnp.sum(dy * xn, axis=0, keepdims=True)
            dxn = dy * nw
            gx_ref[0, rows, :] = dh_ref[0, rows, :] + r * (dxn - xn * jnp.mean(dxn * xn, axis=-1, keepdims=True))
        gw_acc[...] += _dot_tn(jnp.concatenate(dps, axis=0), jnp.concatenate(us, axis=0))

        @pl.when((b == bl - 1) & (i == nt - 1))
        def _():
            gw_acc[DA:DA + 2 * KVW, :] += gwkv_ref[...]

            def to_bf16(j, carry):
                rows = pl.ds(pl.multiple_of(j * 2 * KVW, 2 * KVW), 2 * KVW)
                win_ref[rows, :] = gw_acc[rows, :].astype(BF16)
                return carry

            lax.fori_loop(0, D_IN // (2 * KVW), to_bf16, 0)
            pltpu.sync_copy(win_ref, gw_hbm)

    def tile(w):
        return pl.BlockSpec((1, tm, w), lambda b, i: (b, i, 0))

    def const(shape):
        return pl.BlockSpec(shape, lambda b, i: (0,) * len(shape))

    anyspace = pl.BlockSpec(memory_space=pl.ANY)
    rope = pl.BlockSpec((tm, KVW), lambda b, i: (i, 0))
    kv_tile = pl.BlockSpec((1, KVW // HD, tm, HD), lambda b, i: (b, 0, i, 0))
    return pl.pallas_call(
        body, name="bwd_in", grid=(bl, nt),
        in_specs=[tile(D), pl.BlockSpec((1, 3, D), lambda b, i: (b, 0, 0)), const((1, D)), anyspace, rope, rope,
                  const((1, DA)), const((1, KVW)), const((DA, DA)),
                  tile(DA), tile(2 * KVW), tile(DA), kv_tile, kv_tile, tile(DA), tile(2 * DC), tile(DC), tile(D),
                  const((2 * KVW, D))],
        out_specs=[tile(D), anyspace, pl.BlockSpec((1, 2, D), lambda b, i: (b, 0, 0)), const((1, D)),
                   const((1, DA)), const((1, KVW))],
        out_shape=[jax.ShapeDtypeStruct((bl, s, D), F32), jax.ShapeDtypeStruct((D_IN, D), BF16),
                   jax.ShapeDtypeStruct((bl, 2, D), F32), jax.ShapeDtypeStruct((1, D), F32),
                   jax.ShapeDtypeStruct((1, DA), F32), jax.ShapeDtypeStruct((1, KVW), F32)],
        scratch_shapes=[pltpu.VMEM((D_IN, D), BF16), pltpu.VMEM((D_IN, D), F32), pltpu.SemaphoreType.DMA],
        compiler_params=_params(("arbitrary", "arbitrary")),
    )(x, modrows, norm_w, w_in_b, cos, sins, qnw_t, knw_t, ones_bd,
      pq, pkv, dq, dk, dv, dza, dglu, dzc, dh, gw_kv)


_LOSS, _DMODC, _NW, _QN, _KN, _CB, _LW, _LB, _BPW, SMALL_W = 0, 1024, 4096, 5120, 5248, 5376, 5888, 6400, 6912, 7424


ROW_W = 1792


def _put_flat(ref, off, value):
    n, done = value.shape[1], 0
    while done < n:
        r, c = divmod(off + done, ROW_W)
        take = min(n - done, ROW_W - c)
        ref[r:r + 1, c:c + take] = value[:, done:done + take]
        done += take


def _get_flat(arr, off, n):
    parts, done = [], 0
    while done < n:
        r, c = divmod(off + done, ROW_W)
        take = min(n - done, ROW_W - c)
        parts.append(arr[r:r + 1, c:c + take])
        done += take
    return parts[0] if len(parts) == 1 else jnp.concatenate(parts, axis=-1)


def _pack_small_body(loss_ref, ctx_ref, dnw_ref, dqnw_ref, dknw_ref, dknwc_ref, conv_ref, dss_ref, dgate_ref, o_ref):
    bl = dss_ref.shape[0]
    assert SMALL_W + bl * 3 * D <= 8 * ROW_W
    o_ref[...] = jnp.zeros_like(o_ref)
    _put_flat(o_ref, _LOSS, loss_ref[...])
    _put_flat(o_ref, _DMODC, ctx_ref[0:1, :])
    _put_flat(o_ref, _DMODC + D, ctx_ref[1:2, :])
    _put_flat(o_ref, _NW, dnw_ref[...] + ctx_ref[2:3, :])
    dq = dqnw_ref[...]
    qn = dq[:, 0:HD]
    for h in range(1, DA // HD):
        qn = qn + dq[:, h * HD:(h + 1) * HD]
    _put_flat(o_ref, _QN, qn)
    dk = dknw_ref[...] + dknwc_ref[...]
    _put_flat(o_ref, _KN, dk[:, 0:HD] + dk[:, HD:2 * HD])
    _put_flat(o_ref, _BPW, conv_ref[0:1, :])
    _put_flat(o_ref, _LW, conv_ref[1:2, :])
    _put_flat(o_ref, _LB, conv_ref[2:3, :])
    _put_flat(o_ref, _CB, conv_ref[3:4, :])
    for b in range(bl):
        _put_flat(o_ref, SMALL_W + b * 3 * D, dss_ref[b, 0:1, :])
        _put_flat(o_ref, SMALL_W + b * 3 * D + D, dss_ref[b, 1:2, :])
        _put_flat(o_ref, SMALL_W + b * 3 * D + 2 * D, dgate_ref[b])


_SMALL = (("b_mod", None), ("norm_w", _NW), ("q_norm_w", _QN), ("k_norm_w", _KN), ("conv_b", _CB),
          ("conv_ln_w", _LW), ("conv_ln_b", _LB), ("b_pw", _BPW), ("c_ctx", None))


def _epilogue(parts_in, pieces, c_rows, w_mod_loc):
    bl = pieces[7].shape[0]
    n_ex = N_DEV * bl
    n_mod = w_mod_loc.shape[1]
    rb = 32
    shp = parts_in.shape[1:]
    rows_in = shp[0]

    def body(*refs):
        it = iter(refs)
        take = lambda k: [next(it) for _ in range(k)]
        (parts,) = take(1)
        piece_refs = take(9)
        (c_ref, wm_ref) = take(2)
        (g_in, g_wm, sum_ref, gb_ref, gc_all, loss_ref) = take(6)
        (mine, got_sib, stage, got_chip, payload, gathered, dmod_full, gc_mine) = take(8)
        (d2d_send, d2d_recv, ici_send, ici_recv, local_sems, sg_send, sg_recv, gc_send, gc_recv, misc_sems) = take(10)

        x, y, c = _coords()
        me = _lin(x, y, c)
        sib = (x, y, 1 - c)
        peers = [(1 - x, y), (x, 1 - y), (1 - x, 1 - y)]
        home = 2 * x + y

        def rows_loop(fn):
            def step(i, carry):
                fn(pl.ds(pl.multiple_of(i * rb, rb), rb))
                return carry
            lax.fori_loop(0, rows_in // rb, step, 0)

        def direct_gather(src, dst, send_sems, recv_sems, local_sem):
            cps = [pltpu.make_async_copy(src, dst.at[me], local_sem)]
            for k in range(1, N_DEV):
                peer = (1 - x if k & 4 else x, 1 - y if k & 2 else y, 1 - c if k & 1 else c)
                cps.append(pltpu.make_async_remote_copy(
                    src_ref=src, dst_ref=dst.at[me], send_sem=send_sems.at[k - 1], recv_sem=recv_sems.at[k - 1],
                    device_id=peer, device_id_type=MESH_ID))
            for cp in cps:
                cp.start()
            return cps

        _pack_small_body(*piece_refs, payload)
        small_cps = direct_gather(payload, gathered, sg_send, sg_recv, misc_sems.at[0])

        local, d2d, ici = [], [], []
        for s in range(4):
            cp = pltpu.make_async_copy(parts.at[_lin(s // 2, s % 2, c)], mine.at[s], local_sems.at[s])
            cp.start()
            local.append(cp)
            rc = pltpu.make_async_remote_copy(
                src_ref=parts.at[_lin(s // 2, s % 2, 1 - c)], dst_ref=got_sib.at[s],
                send_sem=d2d_send.at[s], recv_sem=d2d_recv.at[s], device_id=sib, device_id_type=MESH_ID)
            rc.start()
            d2d.append(rc)

        for cp in small_cps[1:]:
            cp.wait_recv()
        small_cps[0].wait()
        tot = gathered[0]
        for j in range(1, N_DEV):
            tot = tot + gathered[j]
        summed = _get_flat(tot, 0, SMALL_W)
        dmod_full[...] = jnp.zeros_like(dmod_full)
        for j in range(N_DEV):
            arr = gathered[j]
            for b in range(bl):
                dmod_full[j * bl + b:j * bl + b + 1, :] = _get_flat(arr, SMALL_W + b * 3 * D, 3 * D)
        dmod_full[n_ex:n_ex + 1, :] = summed[:, _DMODC:_DMODC + 3 * D]
        sum_ref[...] = summed
        gb_ref[...] = jnp.sum(dmod_full[...], axis=0, keepdims=True)
        loss_ref[...] = (0.5 / D) * jnp.sum(summed[:, _LOSS:_LOSS + D], axis=-1, keepdims=True)

        north = c == 1
        first = (jnp.where(north, 1 - x, x), jnp.where(north, y, 1 - y))
        second = (jnp.where(north, x, 1 - x), jnp.where(north, 1 - y, y))
        for s in range(4):
            local[s].wait()
            d2d[s].wait_recv()

        def chip_sum(k, chip, relayed):
            slot = 2 * chip[0] + chip[1]

            def pair_sum(rs):
                acc = mine[slot, rs, :].astype(F32) + got_sib[slot, rs, :].astype(F32)
                if relayed:
                    acc = acc + got_chip[1, rs, :].astype(F32)
                stage[k, rs, :] = acc.astype(BF16)

            rows_loop(pair_sum)

        def send(k, to):
            rc = pltpu.make_async_remote_copy(
                src_ref=stage.at[k], dst_ref=got_chip.at[k], send_sem=ici_send.at[k], recv_sem=ici_recv.at[k],
                device_id=(to[0], to[1], c), device_id_type=MESH_ID)
            rc.start()
            ici.append(rc)

        chip_sum(0, first, False)
        send(0, first)
        chip_sum(1, (1 - x, 1 - y), False)
        send(1, first)

        cr = c_ref[...]
        act = (cr * _sigmoid(cr)).astype(BF16)
        dm = dmod_full[:, pl.ds(pl.multiple_of(me * n_mod, 128), n_mod)].astype(BF16)
        g_wm[...] = _dot_tn(act, dm)
        gc_mine[...] = _dot_nt(dm[n_ex:n_ex + 8, :], wm_ref[...].astype(BF16))
        gc_cps = direct_gather(gc_mine, gc_all, gc_send, gc_recv, misc_sems.at[1])

        ici[1].wait_recv()
        chip_sum(2, second, True)
        send(2, second)
        ici[0].wait_recv()
        ici[2].wait_recv()

        def finish(rs):
            gsum = mine[home, rs, :].astype(F32) + got_sib[home, rs, :].astype(F32)
            g_in[rs, :] = gsum + got_chip[0, rs, :].astype(F32) + got_chip[2, rs, :].astype(F32)

        rows_loop(finish)

        for cp in gc_cps[1:]:
            cp.wait_recv()
        gc_cps[0].wait()
        for rc in d2d + ici + small_cps[1:] + gc_cps[1:]:
            rc.wait_send()

    vm = pl.BlockSpec(memory_space=pltpu.VMEM)
    anyspace = pl.BlockSpec(memory_space=pl.ANY)
    assert rows_in % rb == 0 and parts_in.dtype == BF16
    args = [parts_in, *pieces, c_rows, w_mod_loc]
    in_specs = [anyspace] + [vm] * (len(args) - 1)
    out_shape = [jax.ShapeDtypeStruct(shp, F32), jax.ShapeDtypeStruct(w_mod_loc.shape, F32),
                 jax.ShapeDtypeStruct((1, SMALL_W), F32), jax.ShapeDtypeStruct((1, 3 * D), F32),
                 jax.ShapeDtypeStruct((N_DEV, 8, D), F32), jax.ShapeDtypeStruct((1, 1), F32)]
    scratch = [pltpu.VMEM((4,) + shp, BF16), pltpu.VMEM((4,) + shp, BF16), pltpu.VMEM((3,) + shp, BF16),
               pltpu.VMEM((3,) + shp, BF16), pltpu.VMEM((8, ROW_W), F32), pltpu.VMEM((N_DEV, 8, ROW_W), F32),
               pltpu.VMEM((n_ex + 8, 3 * D), F32), pltpu.VMEM((8, D), F32),
               pltpu.SemaphoreType.DMA((4,)), pltpu.SemaphoreType.DMA((4,)), pltpu.SemaphoreType.DMA((3,)),
               pltpu.SemaphoreType.DMA((3,)), pltpu.SemaphoreType.DMA((4,)),
               pltpu.SemaphoreType.DMA((N_DEV - 1,)), pltpu.SemaphoreType.DMA((N_DEV - 1,)),
               pltpu.SemaphoreType.DMA((N_DEV - 1,)), pltpu.SemaphoreType.DMA((N_DEV - 1,)),
               pltpu.SemaphoreType.DMA((2,))]
    return pl.pallas_call(
        body, name="epilogue", out_shape=out_shape, in_specs=in_specs, out_specs=[vm] * len(out_shape),
        scratch_shapes=scratch, compiler_params=pltpu.CompilerParams(vmem_limit_bytes=VMEM_LIMIT),
    )(*args)


def _final_adamw(g_in, w_in_t, m_in_t, v_in_t, g_wm, w_mod_loc, m_mod, v_mod, summed, g_bmod, gc_all,
                 small_w, small_m, small_v):
    ns = len(_SMALL)
    rb = 32

    def body(*refs):
        it = iter(refs)
        take = lambda k: [next(it) for _ in range(k)]
        (gin_ref, w_ref, m_ref, v_ref, gwm_ref, wm_ref, mm_ref, vm_ref, sum_ref, gb_ref, gc_ref) = take(11)
        sw, sm, sv = take(ns), take(ns), take(ns)
        (d_in, nm_in, nv_in, d_wm, nm_wm, nv_wm) = take(6)
        souts = take(4 * ns)

        def big(g_r, w_r, m_r, v_r, d_o, nm_o, nv_o):
            def step(i, carry):
                rs = pl.ds(pl.multiple_of(i * rb, rb), rb)
                dl, m_new, v_new = _adamw(w_r[rs, :], g_r[rs, :], m_r[rs, :], v_r[rs, :])
                d_o[rs, :] = dl
                nm_o[rs, :] = m_new
                nv_o[rs, :] = v_new
                return carry
            lax.fori_loop(0, w_r.shape[0] // rb, step, 0)

        big(gin_ref, w_ref, m_ref, v_ref, d_in, nm_in, nv_in)
        big(gwm_ref, wm_ref, mm_ref, vm_ref, d_wm, nm_wm, nv_wm)
        for k, (name, off) in enumerate(_SMALL):
            w = sw[k][...]
            if name == "b_mod":
                gk = gb_ref[...]
            elif name == "c_ctx":
                acc = gc_ref[0, 0:1, :]
                for j in range(1, N_DEV):
                    acc = acc + gc_ref[j, 0:1, :]
                sg = _sigmoid(w)
                gk = acc * (sg * (1.0 + w * (1.0 - sg)))
            else:
                gk = sum_ref[:, off:off + w.shape[1]]
            dl, m_new, v_new = _adamw(w, gk, sm[k][...], sv[k][...])
            souts[k][...] = gk
            souts[ns + k][...] = dl
            souts[2 * ns + k][...] = m_new
            souts[3 * ns + k][...] = v_new

    assert w_in_t.shape[0] % rb == 0 and w_mod_loc.shape[0] % rb == 0
    big_shape = jax.ShapeDtypeStruct(w_in_t.shape, F32)
    mod_shape = jax.ShapeDtypeStruct(w_mod_loc.shape, F32)
    out_shape = [big_shape] * 3 + [mod_shape] * 3 + [jax.ShapeDtypeStruct(w.shape, F32) for w in small_w] * 4
    outs = pl.pallas_call(
        body, name="final_adamw", out_shape=out_shape,
        compiler_params=pltpu.CompilerParams(vmem_limit_bytes=VMEM_LIMIT),
    )(g_in, w_in_t, m_in_t, v_in_t, g_wm, w_mod_loc, m_mod, v_mod, summed, g_bmod, gc_all, *small_w, *small_m, *small_v)
    small_outs = [outs[6 + k * ns:6 + (k + 1) * ns] for k in range(4)]
    return (g_in,) + tuple(outs[0:3]), (g_wm,) + tuple(outs[3:6]), small_outs


def _rope_tables(s):
    t = jnp.arange(s, dtype=jnp.int32)
    row = (t // GRID_W).astype(F32)
    col = (t % GRID_W).astype(F32)
    freqs = ROPE_THETA ** (-jnp.arange(0, HD // 2, 2, dtype=F32) / (HD // 2))
    ang_r = row[:, None] * freqs[None, :]
    ang_c = col[:, None] * freqs[None, :]
    cr, sr, cc, sc = jnp.cos(ang_r), jnp.sin(ang_r), jnp.cos(ang_c), jnp.sin(ang_c)
    cos = jnp.concatenate([cr, cr, cc, cc], axis=-1)
    sins = jnp.concatenate([-sr, sr, -sc, sc], axis=-1)
    return jnp.tile(cos, (1, KVW // HD)), jnp.tile(sins, (1, KVW // HD))


def kernel(x, c, ctx, c_ctx, w_mod, b_mod, norm_w, w_in, q_norm_w, k_norm_w, conv_w, conv_b, conv_ln_w, conv_ln_b, w_pw, b_pw, w_out, loss_target, m_c_ctx, m_w_mod, m_b_mod, m_norm_w, m_w_in, m_q_norm_w, m_k_norm_w, m_conv_w, m_conv_b, m_conv_ln_w, m_conv_ln_b, m_w_pw, m_b_pw, m_w_out, v_c_ctx, v_w_mod, v_b_mod, v_norm_w, v_w_in, v_q_norm_w, v_k_norm_w, v_conv_w, v_conv_b, v_conv_ln_w, v_conv_ln_b, v_w_pw, v_b_pw, v_w_out):
    bl, s, _ = x.shape
    cl = ctx.shape[1]
    me = _lin(*_coords())
    n_mod = w_mod.shape[2]

    conv_w_pad = jnp.pad(conv_w[0], ((0, 32 - KW), (0, 0)))
    c_pad = jnp.pad(c, ((0, 8 - bl), (0, 0)))
    n_ex = N_DEV * bl
    g_win, c_rows, g_mod = _prologue(w_in[0].T.astype(BF16), c_pad, bl, c_ctx[None, :], w_mod[0], b_mod)
    w_in_b = g_win.reshape(D_IN, D)
    mod_all = g_mod.transpose(1, 0, 2).reshape(n_ex + 8, 3 * D)
    modrows = lax.dynamic_slice_in_dim(mod_all, me * bl, bl, axis=0).reshape(bl, 3, D)
    modc = mod_all[n_ex].reshape(1, 3, D)

    cos, sins = _rope_tables(s)
    qnw_t = jnp.tile(q_norm_w, (1, DA // HD))
    knw_t = jnp.tile(k_norm_w, (1, KVW // HD))
    lane = jnp.arange(DA, dtype=jnp.int32) // HD
    ones_bd = (lane[:, None] == lane[None, :]).astype(BF16)
    ones_kv = ones_bd[0:KVW, 0:KVW]
    w_kv_b = w_in_b

    k_ctx, v_ctx, pkv_c = _ctx_fwd(ctx, modc, norm_w, w_kv_b, knw_t, ones_kv, cl + s)
    (q_h, k_h, v_h, pq, pkv, za, glu, zc), (g_wout, g_wpw, g_cw) = _fwd_in(
        x, modrows, norm_w, w_in_b, cos, sins, qnw_t, knw_t, ones_bd, k_ctx, v_ctx,
        [w_out[0], w_pw[0], conv_w_pad], [BF16, BF16, F32])
    w_out_b = g_wout.reshape(D, D)
    w_pw_b = g_wpw.reshape(DC, DC)
    conv_w_full = g_cw.transpose(1, 0, 2).reshape(32, DC)
    attn, lse = _attn_fwd(q_h, k_h, v_h)
    y_conv, cp = _conv_fwd(glu, conv_w_full, conv_b, conv_ln_w, conv_ln_b, w_pw_b, b_pw)

    do_h, dza, dcp, dzc, dh, dgate, gw_out, loss_row = _out_fwd_bwd(attn, za, cp, zc, x, loss_target, modrows, w_out_b)
    dy_conv, gw_pw, conv_rows = _conv_bwd_pointwise(y_conv, dcp, conv_ln_w, conv_ln_b, w_pw_b)
    dglu, g_cw_full = _conv_bwd_depthwise(glu, dy_conv, conv_w_full)
    parts_out = gw_out.reshape(N_DEV, D // N_DEV, D)
    parts_pw = gw_pw.reshape(N_DEV, DC // N_DEV, DC)
    parts_cw = g_cw_full.astype(BF16).reshape(32, N_DEV, DC // N_DEV).transpose(1, 0, 2)
    (dq, dk_h, dv_h), (got_out, got_pw, got_cw) = _attn_bwd(
        q_h, k_h, v_h, do_h, attn, lse, [parts_out, parts_pw, parts_cw])
    gw_kv, ctx_rows, dknw_c = _ctx_bwd(ctx, modc, norm_w, w_kv_b, pkv_c, dk_h, dv_h, knw_t, ones_kv)
    grad_x, gw_in, dmod_ss, dnw, dqnw, dknw = _bwd_in(
        x, modrows, norm_w, w_in_b, cos, sins, qnw_t, knw_t, ones_bd,
        pq, pkv, dq, dk_h, dv_h, dza, dglu, dzc, dh, gw_kv)

    def pad_cw(a):
        return jnp.pad(a[0], ((0, 32 - KW), (0, 0)))

    r_out, r_pw, r_cw = _sum_devices_adamw(
        [(got_out, w_out[0], m_w_out[0], v_w_out[0]), (got_pw, w_pw[0], m_w_pw[0], v_w_pw[0]),
         (got_cw, pad_cw(conv_w), pad_cw(m_conv_w), pad_cw(v_conv_w))])
    r_cw = tuple(a[:KW] for a in r_cw)

    given = {"c_ctx": (c_ctx, m_c_ctx, v_c_ctx), "b_mod": (b_mod, m_b_mod, v_b_mod), "norm_w": (norm_w, m_norm_w, v_norm_w),
             "q_norm_w": (q_norm_w, m_q_norm_w, v_q_norm_w), "k_norm_w": (k_norm_w, m_k_norm_w, v_k_norm_w),
             "conv_b": (conv_b, m_conv_b, v_conv_b), "conv_ln_w": (conv_ln_w, m_conv_ln_w, v_conv_ln_w),
             "conv_ln_b": (conv_ln_b, m_conv_ln_b, v_conv_ln_b), "b_pw": (b_pw, m_b_pw, v_b_pw)}
    as_rows = [[given[name][which].reshape(1, -1) for name, _ in _SMALL] for which in range(3)]
    g_in_t, g_wmod, summed, g_bmod, gc_all, loss11 = _epilogue(
        gw_in.reshape(N_DEV, D_IN // N_DEV, D),
        [loss_row, ctx_rows, dnw, dqnw, dknw, dknw_c, conv_rows, dmod_ss, dgate], c_rows, w_mod[0])
    r_in, r_wmod, small_outs = _final_adamw(
        g_in_t, w_in[0].T, m_w_in[0].T, v_w_in[0].T, g_wmod, w_mod[0], m_w_mod[0], v_w_mod[0],
        summed, g_bmod, gc_all, *as_rows)
    r_in = tuple(a.T for a in r_in)

    big = {"w_mod": r_wmod, "w_in": r_in, "conv_w": r_cw, "w_pw": r_pw, "w_out": r_out}
    order = ["c_ctx", "w_mod", "b_mod", "norm_w", "w_in", "q_norm_w", "k_norm_w", "conv_w", "conv_b", "conv_ln_w",
             "conv_ln_b", "w_pw", "b_pw", "w_out"]
    small_index = {name: k for k, (name, _) in enumerate(_SMALL)}
    outs = [loss11.reshape(()), grad_x]
    for which in range(4):
        for name in order:
            if name in big:
                outs.append(big[name][which][None])
            else:
                outs.append(small_outs[which][small_index[name]].reshape(given[name][0].shape))
    return tuple(outs)
```

```python
import jax
import jax.numpy as jnp
from jax import lax
from jax.experimental import pallas as pl
from jax.experimental.pallas import tpu as pltpu

F32, BF16 = jnp.float32, jnp.bfloat16
MESH_ID = pl.DeviceIdType.MESH

N_DEV = 8
D = 1024
D_IN = 2816
DA = 512
DC = 512
HD = 64
KVW = 128
KW = 31
HALO = 16
EPS = 1e-6
ROPE_THETA = 10000.0
GRID_W = 64

ADAM_LR, ADAM_B1, ADAM_B2, ADAM_EPS, ADAM_WD, ADAM_STEP = 0.001, 0.9, 0.999, 1e-08, 0.01, 10

VMEM_LIMIT = 56 * 1024 * 1024

TM = 256
TQ = 128
TOKEN_PARTS = 2
BWD_PARTS = 2
FWD_PARTS = 4
TC = 512
CH = 64


def _params(sem, vmem=VMEM_LIMIT):
    return pltpu.CompilerParams(dimension_semantics=sem, vmem_limit_bytes=vmem)


def _dot(a, b):
    return jnp.dot(a, b, preferred_element_type=F32)


def _dot_nt(a, b):
    return lax.dot_general(a, b, (((1,), (1,)), ((), ())), preferred_element_type=F32)


def _dot_tn(a, b):
    return lax.dot_general(a, b, (((0,), (0,)), ((), ())), preferred_element_type=F32)


def _sigmoid(z):
    return 1.0 / (1.0 + jnp.exp(-z))


def _segsum(v, ones_bd):
    return _dot(v.astype(BF16), ones_bd)


def _swap16(x):
    w = x.shape[-1]
    lane = lax.broadcasted_iota(jnp.int32, x.shape, 1)
    return jnp.where((lane % 32) < 16, pltpu.roll(x, w - 16, 1), pltpu.roll(x, 16, 1))


def _with_ones_column(v):
    one = (lax.broadcasted_iota(jnp.int32, v.shape, 1) == 0).astype(v.dtype)
    return jnp.concatenate([v, one], axis=-1)


def _rope(x, cos, sins):
    return x * cos + _swap16(x) * sins


def _rope_bwd(d, cos, sins):
    return d * cos + _swap16(d * sins)


def _adamw(w, g, m, v):
    m2 = ADAM_B1 * m + (1.0 - ADAM_B1) * g
    v2 = ADAM_B2 * v + (1.0 - ADAM_B2) * (g * g)
    m_hat = m2 / (1.0 - ADAM_B1 ** ADAM_STEP)
    v_hat = v2 / (1.0 - ADAM_B2 ** ADAM_STEP)
    delta = -ADAM_LR * (m_hat / (jnp.sqrt(v_hat) + ADAM_EPS) + ADAM_WD * w)
    return delta, m2, v2


def _coords():
    return lax.axis_index("x"), lax.axis_index("y"), lax.axis_index("c")


def _lin(x, y, c):
    return 4 * x + 2 * y + c


def _prologue(w_in_t_b, c_pad, bl, c_ctx_row, w_mod_loc, b_mod):
    n_ex = N_DEV * bl
    n_mod = w_mod_loc.shape[1]

    def body(w_ref, c_ref, cctx_ref, wm_ref, b_ref, out_w, crows_ref, mod_out, c_gath, mod_mine,
             w_send, w_recv, c_send, c_recv, m_send, m_recv, local_sems):
        x, y, c = _coords()
        me_lin = _lin(x, y, c)
        me, sib = (x, y, c), (x, y, 1 - c)
        xnb, ynb, diag = (1 - x, y), (x, 1 - y), (1 - x, 1 - y)
        north = c == 1

        def direct_gather(src, dst, send_sems, recv_sems, local_sem):
            cps = [pltpu.make_async_copy(src, dst.at[me_lin], local_sem)]
            for k in range(1, N_DEV):
                peer = (1 - x if k & 4 else x, 1 - y if k & 2 else y, 1 - c if k & 1 else c)
                cps.append(pltpu.make_async_remote_copy(
                    src_ref=src, dst_ref=dst.at[me_lin], send_sem=send_sems.at[k - 1], recv_sem=recv_sems.at[k - 1],
                    device_id=peer, device_id_type=MESH_ID))
            for cp in cps:
                cp.start()
            return cps

        def copy(k, block, to, src=None):
            slot = out_w.at[_lin(*block)]
            return pltpu.make_async_remote_copy(
                src_ref=slot if src is None else src, dst_ref=slot, send_sem=w_send.at[k], recv_sem=w_recv.at[k],
                device_id=to, device_id_type=MESH_ID)

        c_cps = direct_gather(c_ref, c_gath, c_send, c_recv, local_sems.at[0])
        mine = pltpu.make_async_copy(w_ref, out_w.at[me_lin], local_sems.at[1])
        mine.start()
        first = [copy(0, me, sib, src=w_ref), copy(1, me, (*xnb, c), src=w_ref), copy(2, me, (*ynb, c), src=w_ref)]
        for cp in first:
            cp.start()

        for cp in c_cps[1:]:
            cp.wait_recv()
        c_cps[0].wait()
        crows_ref[...] = jnp.zeros_like(crows_ref)
        for j in range(N_DEV):
            crows_ref[j * bl:(j + 1) * bl, :] = c_gath[j, 0:bl, :]
        crows_ref[n_ex:n_ex + 1, :] = cctx_ref[...]
        cr = crows_ref[...]
        act = (cr * _sigmoid(cr)).astype(BF16)
        mod_mine[...] = _dot(act, wm_ref[...].astype(BF16)) + b_ref[:, pl.ds(pl.multiple_of(me_lin * n_mod, 128), n_mod)]
        mod_cps = direct_gather(mod_mine, mod_out, m_send, m_recv, local_sems.at[2])

        relay_north = copy(3, (*xnb, c), (*ynb, c))
        relay_south = copy(3, (*ynb, c), (*xnb, c))
        passed = []
        copy(1, (*xnb, c), me).wait_recv()
        pl.when(north)(relay_north.start)
        passed.append(copy(4, (*xnb, c), sib))
        passed[-1].start()
        copy(2, (*ynb, c), me).wait_recv()
        pl.when(jnp.logical_not(north))(relay_south.start)
        passed.append(copy(5, (*ynb, c), sib))
        passed[-1].start()
        copy(3, (*diag, c), me).wait_recv()
        passed.append(copy(6, (*diag, c), sib))
        passed[-1].start()
        copy(0, sib, me).wait_recv()
        for k, chip in ((4, xnb), (5, ynb), (6, diag)):
            copy(k, (*chip, 1 - c), me).wait_recv()
        for cp in mod_cps[1:]:
            cp.wait_recv()
        mod_cps[0].wait()
        for cp in first + passed + [relay_north] + c_cps[1:] + mod_cps[1:]:
            cp.wait_send()
        mine.wait()

    vm = pl.BlockSpec(memory_space=pltpu.VMEM)
    seven = pltpu.SemaphoreType.DMA((N_DEV - 1,))
    return pl.pallas_call(
        body, name="prologue",
        out_shape=[jax.ShapeDtypeStruct((N_DEV,) + w_in_t_b.shape, BF16), jax.ShapeDtypeStruct((n_ex + 8, D), F32),
                   jax.ShapeDtypeStruct((N_DEV, n_ex + 8, n_mod), F32)],
        in_specs=[vm] * 5, out_specs=[vm] * 3,
        scratch_shapes=[pltpu.VMEM((N_DEV,) + c_pad.shape, F32), pltpu.VMEM((n_ex + 8, n_mod), F32),
                        seven, seven, seven, seven, seven, seven, pltpu.SemaphoreType.DMA((3,))],
        compiler_params=pltpu.CompilerParams(vmem_limit_bytes=VMEM_LIMIT),
    )(w_in_t_b, c_pad, c_ctx_row, w_mod_loc, b_mod)


def _exchange_copies(in_refs, out_refs, send_sems, recv_sems, local_sems, scatter):
    x, y, c = _coords()
    me = _lin(x, y, c)
    local, remote = [], []
    for a, (src, dst) in enumerate(zip(in_refs, out_refs)):
        local.append(pltpu.make_async_copy(src.at[me] if scatter else src, dst.at[me], local_sems.at[a]))
        for k in range(1, N_DEV):
            peer = (1 - x if k & 4 else x, 1 - y if k & 2 else y, 1 - c if k & 1 else c)
            remote.append(pltpu.make_async_remote_copy(
                src_ref=src.at[_lin(*peer)] if scatter else src, dst_ref=dst.at[me],
                send_sem=send_sems.at[a * (N_DEV - 1) + k - 1], recv_sem=recv_sems.at[a * (N_DEV - 1) + k - 1],
                device_id=peer, device_id_type=MESH_ID))
    return local, remote


def _exchange_scratch(n):
    return [pltpu.SemaphoreType.DMA((n * (N_DEV - 1),)), pltpu.SemaphoreType.DMA((n * (N_DEV - 1),)),
            pltpu.SemaphoreType.DMA((n,))]


def _sum_devices_adamw(items):
    n = len(items)

    def body(*refs):
        for a in range(n):
            got, w_ref, m_ref, v_ref = refs[4 * a:4 * a + 4]
            g_ref, d_ref, nm_ref, nv_ref = refs[4 * n + 4 * a:4 * n + 4 * a + 4]
            g = got[0].astype(F32)
            for j in range(1, N_DEV):
                g = g + got[j].astype(F32)
            delta, m2, v2 = _adamw(w_ref[...], g, m_ref[...], v_ref[...])
            g_ref[...] = g
            d_ref[...] = delta
            nm_ref[...] = m2
            nv_ref[...] = v2

    args, out_shape = [], []
    for got, w, m, v in items:
        assert got.shape == (N_DEV,) + w.shape
        args += [got, w, m, v]
        out_shape += [jax.ShapeDtypeStruct(w.shape, F32)] * 4
    outs = pl.pallas_call(body, name="sum_devices_adamw", out_shape=out_shape,
                          compiler_params=pltpu.CompilerParams(vmem_limit_bytes=VMEM_LIMIT))(*args)
    return [tuple(outs[4 * a:4 * a + 4]) for a in range(n)]


def _fwd_in(x, modrows, norm_w, w_in_b, cos, sins, qnw_t, knw_t, ones_bd, k_all, v_all, shards, wire_dtypes):
    bl, s, _ = x.shape
    tm = TOKEN_PARTS * TM
    nt = s // tm
    n_sh = len(shards)

    def body(*refs):
        (x_ref, mod_ref, nw_ref, win_ref, cos_ref, sin_ref, qnw_ref, knw_ref, bd_ref, kin_ref, vin_ref) = refs[:11]
        shard_refs = refs[11:11 + n_sh]
        q_ref, k_ref, v_ref, pq_ref, pkv_ref, za_ref, glu_ref, zc_ref = refs[11 + n_sh:19 + n_sh]
        gathered_refs = refs[19 + n_sh:19 + 2 * n_sh]
        stage_refs = refs[19 + 2 * n_sh:19 + 3 * n_sh]
        send_sems, recv_sems, local_sems = refs[19 + 3 * n_sh:]
        b, i = pl.program_id(0), pl.program_id(1)
        local, remote = _exchange_copies(stage_refs, gathered_refs, send_sems, recv_sems, local_sems, scatter=False)

        @pl.when((b == 0) & (i == 0))
        def _():
            for src, stage in zip(shard_refs, stage_refs):
                stage[...] = src[...].astype(stage.dtype)
            for cp in local + remote:
                cp.start()

        shift = mod_ref[0, 0:1, :]
        scale = mod_ref[0, 1:2, :]
        for part in range(TOKEN_PARTS):
            rows = pl.ds(part * TM, TM)
            xv = x_ref[0, rows, :]
            r = lax.rsqrt(jnp.mean(xv * xv, axis=-1, keepdims=True) + EPS)
            u = (xv * r * nw_ref[...]) * (1.0 + scale) + shift
            p = _dot_nt(u.astype(BF16), win_ref[...])
            pq = p[:, 0:DA]
            pk = p[:, DA:DA + HD * 2]
            ck = cos_ref[rows, :]
            sk = sin_ref[rows, :]
            cs = jnp.concatenate([ck] * (DA // KVW), axis=-1)
            sn = jnp.concatenate([sk] * (DA // KVW), axis=-1)
            rq = lax.rsqrt(_segsum(pq * pq, bd_ref[...]) * (1.0 / HD) + EPS)
            qn = pq * rq * qnw_ref[...]
            qr = _rope(qn, cs, sn) * 0.125
            for h in range(DA // HD):
                q_ref[0, h, rows, :] = qr[:, h * HD:(h + 1) * HD].astype(BF16)
            rk = lax.rsqrt(_segsum(pk * pk, bd_ref[0:KVW, 0:KVW]) * (1.0 / HD) + EPS)
            kn = pk * rk * knw_ref[...]
            kr = _rope(kn, ck, sk)
            pv = p[:, 640:768]
            for h in range(KVW // HD):
                k_ref[0, h, rows, :] = kr[:, h * HD:(h + 1) * HD].astype(BF16)
                v_ref[0, h, rows, :] = _with_ones_column(pv[:, h * HD:(h + 1) * HD]).astype(BF16)
            pq_ref[0, rows, :] = pq
            pkv_ref[0, rows, :] = p[:, 512:768]
            za_ref[0, rows, :] = p[:, 768:1280]
            glu_ref[0, rows, :] = p[:, 1280:2304]
            zc_ref[0, rows, :] = p[:, 2304:2816]

        @pl.when((b == bl - 1) & (i == nt - 1))
        def _():
            for cp in remote:
                cp.wait_recv()
            for cp in remote:
                cp.wait_send()
            for cp in local:
                cp.wait()

    def tile(w):
        return pl.BlockSpec((1, tm, w), lambda b, i: (b, i, 0))

    def const(shape):
        return pl.BlockSpec(shape, lambda b, i: (0,) * len(shape))

    outs = [(DA, F32), (2 * KVW, F32), (DA, F32), (2 * DC, F32), (DC, F32)]
    anyspace = pl.BlockSpec(memory_space=pl.ANY)
    rope = pl.BlockSpec((tm, KVW), lambda b, i: (i, 0))
    k_tile = pl.BlockSpec((1, KVW // HD, tm, HD), lambda b, i: (b, 0, i, 0))
    v_tile = pl.BlockSpec((1, KVW // HD, tm, 2 * HD), lambda b, i: (b, 0, i, 0))
    res = pl.pallas_call(
        body, name="fwd_in", grid=(bl, nt),
        in_specs=[tile(D), pl.BlockSpec((1, 3, D), lambda b, i: (b, 0, 0)), const((1, D)), const((D_IN, D)),
                  rope, rope, const((1, DA)), const((1, KVW)), const((DA, DA)), anyspace, anyspace]
        + [const(a.shape) for a in shards],
        out_specs=[pl.BlockSpec((1, DA // HD, tm, HD), lambda b, i: (b, 0, i, 0)), k_tile, v_tile]
        + [tile(w) for w, _ in outs] + [anyspace] * n_sh,
        out_shape=[jax.ShapeDtypeStruct((bl, DA // HD, s, HD), BF16), jax.ShapeDtypeStruct(k_all.shape, BF16),
                   jax.ShapeDtypeStruct(v_all.shape, BF16)]
        + [jax.ShapeDtypeStruct((bl, s, w), dt) for w, dt in outs]
        + [jax.ShapeDtypeStruct((N_DEV,) + a.shape, dt) for a, dt in zip(shards, wire_dtypes)],
        input_output_aliases={9: 1, 10: 2},
        scratch_shapes=[pltpu.VMEM(a.shape, dt) for a, dt in zip(shards, wire_dtypes)] + _exchange_scratch(n_sh),
        compiler_params=_params(("arbitrary", "arbitrary")),
    )(x, modrows, norm_w, w_in_b, cos, sins, qnw_t, knw_t, ones_bd, k_all, v_all, *shards)
    return res[:8], res[8:]


_KV_ROWS_OF_W_IN_T = pl.BlockSpec((2 * KVW, D), lambda b: (DA // (2 * KVW), 0))


def _ctx_fwd(ctx, modc, norm_w, w_kv_b, knw_t, ones_bd, n_keys):
    bl, cl, _ = ctx.shape

    def body(x_ref, mod_ref, nw_ref, w_ref, knw_ref, bd_ref, k_ref, v_ref, pkv_ref):
        xv = x_ref[0]
        shift = mod_ref[0, 0:1, :]
        scale = mod_ref[0, 1:2, :]
        r = lax.rsqrt(jnp.mean(xv * xv, axis=-1, keepdims=True) + EPS)
        u = (xv * r * nw_ref[...]) * (1.0 + scale) + shift
        p = _dot_nt(u.astype(BF16), w_ref[...])
        pk = p[:, 0:KVW]
        rk = lax.rsqrt(_segsum(pk * pk, bd_ref[...]) * (1.0 / HD) + EPS)
        kn = pk * rk * knw_ref[...]
        pv = p[:, KVW:2 * KVW]
        for h in range(KVW // HD):
            k_ref[0, h] = kn[:, h * HD:(h + 1) * HD].astype(BF16)
            v_ref[0, h] = _with_ones_column(pv[:, h * HD:(h + 1) * HD]).astype(BF16)
        pkv_ref[0] = p

    def const(shape):
        return pl.BlockSpec(shape, lambda b: (0,) * len(shape))

    def tile(w):
        return pl.BlockSpec((1, cl, w), lambda b: (b, 0, 0))

    ctx_block = (n_keys - cl) // cl
    assert ctx_block * cl + cl == n_keys
    k_tile = pl.BlockSpec((1, KVW // HD, cl, HD), lambda b: (b, 0, ctx_block, 0))
    v_tile = pl.BlockSpec((1, KVW // HD, cl, 2 * HD), lambda b: (b, 0, ctx_block, 0))
    return pl.pallas_call(
        body, name="ctx_fwd", grid=(bl,),
        in_specs=[tile(D), const((1, 3, D)), const((1, D)), _KV_ROWS_OF_W_IN_T, const((1, KVW)), const((KVW, KVW))],
        out_specs=[k_tile, v_tile, tile(2 * KVW)],
        out_shape=[jax.ShapeDtypeStruct((bl, KVW // HD, n_keys, HD), BF16),
                   jax.ShapeDtypeStruct((bl, KVW // HD, n_keys, 2 * HD), BF16),
                   jax.ShapeDtypeStruct((bl, cl, 2 * KVW), F32)],
        compiler_params=_params(("arbitrary",)),
    )(ctx, modc, norm_w, w_kv_b, knw_t, ones_bd)


def _attn_fwd(q, k, v1):
    bl, _, s, _ = q.shape
    n_keys = k.shape[2]

    def body(q_ref, k_ref, v_ref, o_ref, lse_ref):
        kv = k_ref[0, 0]
        vv = v_ref[0, 0]
        lane = lax.broadcasted_iota(jnp.int32, (TQ, 2 * HD), 1)
        for part in range(FWD_PARTS):
            rows = pl.ds(part * TQ, TQ)
            lse = jnp.zeros((TQ, 2 * HD), F32)
            heads = []
            sc_all = _dot_nt(q_ref[0, :, rows, :].reshape(4 * TQ, HD), kv)
            for h in range(4):
                sc = sc_all[h * TQ:(h + 1) * TQ, :]
                m = jnp.max(sc, axis=-1, keepdims=True)
                e = jnp.exp(sc - m).astype(BF16)
                ov = _dot(e, vv)
                denom = ov[:, HD:HD + 1]
                heads.append(ov[:, 0:HD] * (1.0 / denom))
                lse = jnp.where(lane == h, m + jnp.log(denom), lse)
            o_ref[0, rows, :] = jnp.concatenate(heads, axis=-1)
            lse_ref[0, 0, rows, :] = lse

    tq = FWD_PARTS * TQ
    ks = pl.BlockSpec((1, 1, n_keys, HD), lambda b, g, i: (b, g, 0, 0))
    qs = pl.BlockSpec((1, 4, tq, HD), lambda b, g, i: (b, g, i, 0))
    vs = pl.BlockSpec((1, 1, n_keys, 2 * HD), lambda b, g, i: (b, g, 0, 0))
    return pl.pallas_call(
        body, name="attn_fwd", grid=(bl, 2, s // tq), in_specs=[qs, ks, vs],
        out_specs=[pl.BlockSpec((1, tq, 4 * HD), lambda b, g, i: (b, i, g)),
                   pl.BlockSpec((1, 1, tq, 2 * HD), lambda b, g, i: (b, g, i, 0))],
        out_shape=[jax.ShapeDtypeStruct((bl, s, DA), F32), jax.ShapeDtypeStruct((bl, 2, s, 2 * HD), F32)],
        compiler_params=_params(("arbitrary", "arbitrary", "arbitrary")),
    )(q, k, v1)


def _attn_bwd(q, k, v1, do, o, lse, exchange):
    bl, _, s, _ = q.shape
    n_keys = k.shape[2]
    tq = BWD_PARTS * TQ
    nq = s // tq
    n_ex = len(exchange)

    def body(*refs):
        q_ref, k_ref, v_ref, do_ref, o_ref, lse_ref = refs[:6]
        part_refs = refs[6:6 + n_ex]
        dq_ref, dk_ref, dv_ref = refs[6 + n_ex:9 + n_ex]
        got_refs = refs[9 + n_ex:9 + 2 * n_ex]
        p_sc, ds_sc, dkt, dvt, send_sems, recv_sems, local_sems = refs[9 + 2 * n_ex:]
        i = pl.program_id(2)
        first = (pl.program_id(0) == 0) & (pl.program_id(1) == 0) & (i == 0)
        last = (pl.program_id(0) == bl - 1) & (pl.program_id(1) == 1) & (i == nq - 1)
        local, remote = _exchange_copies(part_refs, got_refs, send_sems, recv_sems, local_sems, scatter=True)

        @pl.when(first)
        def _():
            for cp in local + remote:
                cp.start()

        @pl.when(i == 0)
        def _():
            dkt[...] = jnp.zeros_like(dkt)
            dvt[...] = jnp.zeros_like(dvt)

        kv = k_ref[0, 0]
        vv = v_ref[0, 0][:, 0:HD]
        for part in range(BWD_PARTS):
            tq_rows = pl.ds(part * TQ, TQ)
            lse = lse_ref[0, 0, tq_rows, :]
            ov = o_ref[0, tq_rows, :]
            dqs = []
            q_cat = q_ref[0, :, tq_rows, :].reshape(4 * TQ, HD)
            do_cat = do_ref[0, :, tq_rows, :].reshape(4 * TQ, HD)
            sc_all = _dot_nt(q_cat, kv)
            for h in range(4):
                doh = do_cat[h * TQ:(h + 1) * TQ, :]
                delta = jnp.sum(ov[:, h * HD:(h + 1) * HD] * doh.astype(F32), axis=-1, keepdims=True)
                rows = pl.ds((part * 4 + h) * TQ, TQ)
                p = jnp.exp(sc_all[h * TQ:(h + 1) * TQ, :] - lse[:, h:h + 1])
                ds = (p * (_dot_nt(doh, vv) - delta)).astype(BF16)
                p_sc[rows, :] = p.astype(BF16)
                ds_sc[rows, :] = ds
                dqs.append(_dot(ds, kv) * 0.125)
            dq_ref[0, tq_rows, :] = jnp.concatenate(dqs, axis=-1)
            part_rows = pl.ds(part * 4 * TQ, 4 * TQ)
            dvt[...] += _dot_tn(do_cat, p_sc[part_rows, :])
            dkt[...] += _dot_tn(q_cat, ds_sc[part_rows, :])

        @pl.when(i == nq - 1)
        def _():
            dk_ref[0, 0] = dkt[...].T
            dv_ref[0, 0] = dvt[...].T

        @pl.when(last)
        def _():
            for cp in remote:
                cp.wait_recv()
            for cp in remote:
                cp.wait_send()
            for cp in local:
                cp.wait()

    qs = pl.BlockSpec((1, 4, tq, HD), lambda b, g, i: (b, g, i, 0))
    ks = pl.BlockSpec((1, 1, n_keys, HD), lambda b, g, i: (b, g, 0, 0))
    vs = pl.BlockSpec((1, 1, n_keys, 2 * HD), lambda b, g, i: (b, g, 0, 0))
    os_ = pl.BlockSpec((1, tq, 4 * HD), lambda b, g, i: (b, i, g))
    kshape = jax.ShapeDtypeStruct(k.shape, F32)
    anyspace = pl.BlockSpec(memory_space=pl.ANY)
    res = pl.pallas_call(
        body, name="attn_bwd", grid=(bl, 2, nq),
        in_specs=[qs, ks, vs, qs, os_, pl.BlockSpec((1, 1, tq, 2 * HD), lambda b, g, i: (b, g, i, 0))]
        + [anyspace] * n_ex,
        out_specs=[os_, ks, ks] + [anyspace] * n_ex,
        out_shape=[jax.ShapeDtypeStruct((bl, s, DA), F32), kshape, kshape]
        + [jax.ShapeDtypeStruct(a.shape, a.dtype) for a in exchange],
        scratch_shapes=[pltpu.VMEM((4 * tq, n_keys), BF16), pltpu.VMEM((4 * tq, n_keys), BF16),
                        pltpu.VMEM((HD, n_keys), F32), pltpu.VMEM((HD, n_keys), F32)] + _exchange_scratch(n_ex),
        compiler_params=_params(("arbitrary", "arbitrary", "arbitrary")),
    )(q, k, v1, do, o, lse, *exchange)
    return res[:3], res[3:]


def _halo_specs(width, s):
    per = TC // HALO
    last = s // HALO - 1
    main = pl.BlockSpec((1, TC, width), lambda b, i: (b, i, 0))
    prev = pl.BlockSpec((1, HALO, width), lambda b, i: (b, jnp.maximum(i * per - 1, 0), 0))
    nxt = pl.BlockSpec((1, HALO, width), lambda b, i: (b, jnp.minimum((i + 1) * per, last), 0))
    return main, prev, nxt


def _glu(g):
    return g[:, 0:DC] * _sigmoid(g[:, DC:2 * DC])


def _fill_padded(pad_ref, main, prev, nxt, first, last):
    pad_ref[0:HALO, :] = jnp.where(first, 0.0, prev)
    pad_ref[HALO:HALO + TC, :] = main
    pad_ref[HALO + TC:2 * HALO + TC, :] = jnp.where(last, 0.0, nxt)


PLANE_ROWS = TC + 2 * HALO - 8


def _shift_planes(pad_ref, planes_ref):
    for r in range(1, 8):
        planes_ref[r - 1] = pad_ref[pl.ds(r, PLANE_ROWS), :]


def _tap_rows(pad_ref, planes_ref, offset, start, n):
    a, r = divmod(offset, 8)
    if r == 0:
        return pad_ref[pl.ds(start + 8 * a, n), :]
    return planes_ref[r - 1, pl.ds(start + 8 * a, n), :]


def _conv_fwd(glu, conv_w, conv_b, ln_w, ln_b, w_pw_b, b_pw):
    bl, s, _ = glu.shape
    nt = s // TC

    def body(g_ref, gp_ref, gn_ref, cw_ref, cb_ref, lw_ref, lb_ref, wpw_ref, bpw_ref, y_ref, cp_ref, pad_ref, planes_ref):
        i = pl.program_id(1)
        _fill_padded(pad_ref, _glu(g_ref[0]), _glu(gp_ref[0]), _glu(gn_ref[0]), i == 0, i == nt - 1)
        _shift_planes(pad_ref, planes_ref)
        for ck in range(TC // CH):
            acc = jnp.zeros((CH, DC), F32) + cb_ref[...]
            for t in range(KW):
                acc = acc + _tap_rows(pad_ref, planes_ref, 1 + t, ck * CH, CH) * cw_ref[t:t + 1, :]
            y_ref[0, pl.ds(ck * CH, CH), :] = acc
        y = y_ref[0]
        mu = jnp.mean(y, axis=-1, keepdims=True)
        yc = y - mu
        var = jnp.mean(yc * yc, axis=-1, keepdims=True)
        z = yc * lax.rsqrt(var + EPS) * lw_ref[...] + lb_ref[...]
        act = z * _sigmoid(z)
        cp_ref[0] = _dot(act.astype(BF16), wpw_ref[...]) + bpw_ref[...]

    def const(shape):
        return pl.BlockSpec(shape, lambda b, i: (0,) * len(shape))

    main, prev, nxt = _halo_specs(2 * DC, s)
    tile = pl.BlockSpec((1, TC, DC), lambda b, i: (b, i, 0))
    return pl.pallas_call(
        body, name="conv_fwd", grid=(bl, nt),
        in_specs=[main, prev, nxt, const((32, DC)), const((1, DC)), const((1, DC)), const((1, DC)),
                  const((DC, DC)), const((1, DC))],
        out_specs=[tile, tile],
        out_shape=[jax.ShapeDtypeStruct((bl, s, DC), F32)] * 2,
        scratch_shapes=[pltpu.VMEM((TC + 2 * HALO, DC), F32), pltpu.VMEM((7, PLANE_ROWS, DC), F32)],
        compiler_params=_params(("arbitrary", "arbitrary")),
    )(glu, glu, glu, conv_w, conv_b, ln_w, ln_b, w_pw_b, b_pw)


def _conv_bwd_depthwise(glu, dy, conv_w):
    bl, s, _ = glu.shape
    nt = s // TC

    def body(g_ref, gp_ref, gn_ref, d_ref, dp_ref, dn_ref, cw_ref, dglu_ref, dcw_ref,
             padu_ref, padd_ref, planes_u, planes_d):
        i = pl.program_id(1)

        @pl.when((pl.program_id(0) == 0) & (i == 0))
        def _():
            dcw_ref[...] = jnp.zeros_like(dcw_ref)

        first, last = i == 0, i == nt - 1
        _fill_padded(padu_ref, _glu(g_ref[0]), _glu(gp_ref[0]), _glu(gn_ref[0]), first, last)
        _fill_padded(padd_ref, d_ref[0], dp_ref[0], dn_ref[0], first, last)
        _shift_planes(padu_ref, planes_u)
        _shift_planes(padd_ref, planes_d)
        for ck in range(TC // CH):
            acc = jnp.zeros((CH, DC), F32)
            for t in range(KW):
                acc = acc + _tap_rows(padd_ref, planes_d, 2 * HALO - 1 - t, ck * CH, CH) * cw_ref[t:t + 1, :]
            g = g_ref[0, pl.ds(ck * CH, CH), :]
            a = g[:, 0:DC]
            sg = _sigmoid(g[:, DC:2 * DC])
            dglu_ref[0, pl.ds(ck * CH, CH), 0:DC] = (acc * sg).astype(BF16)
            dglu_ref[0, pl.ds(ck * CH, CH), DC:2 * DC] = (acc * a * sg * (1.0 - sg)).astype(BF16)
        group = 4
        for t0 in range(0, KW, group):
            taps = range(t0, min(t0 + group, KW))
            acc8 = [jnp.zeros((8, DC), F32) for _ in taps]
            for ck in range(TC // CH):
                dchunk = d_ref[0, pl.ds(ck * CH, CH), :]
                for n, t in enumerate(taps):
                    prod = _tap_rows(padu_ref, planes_u, 1 + t, ck * CH, CH) * dchunk
                    acc8[n] = acc8[n] + jnp.sum(prod.reshape(CH // 8, 8, DC), axis=0)
            for n, t in enumerate(taps):
                dcw_ref[t:t + 1, :] += jnp.sum(acc8[n], axis=0, keepdims=True)

    gmain, gprev, gnext = _halo_specs(2 * DC, s)
    dmain, dprev, dnext = _halo_specs(DC, s)
    cw = pl.BlockSpec((32, DC), lambda b, i: (0, 0))
    return pl.pallas_call(
        body, name="conv_bwd_depthwise", grid=(bl, nt),
        in_specs=[gmain, gprev, gnext, dmain, dprev, dnext, cw],
        out_specs=[gmain, cw],
        out_shape=[jax.ShapeDtypeStruct((bl, s, 2 * DC), BF16), jax.ShapeDtypeStruct((32, DC), F32)],
        scratch_shapes=[pltpu.VMEM((TC + 2 * HALO, DC), F32)] * 2 + [pltpu.VMEM((7, PLANE_ROWS, DC), F32)] * 2,
        compiler_params=_params(("arbitrary", "arbitrary")),
    )(glu, glu, glu, dy, dy, dy, conv_w)


def _out_fwd_bwd(attn, za, cp, zc, x, target, modrows, w_out_b, y_conv, ln_w, ln_b, w_pw_b):
    bl, s, _ = x.shape
    tm = TOKEN_PARTS * TM

    def body(o_ref, za_ref, cp_ref, zc_ref, x_ref, t_ref, mod_ref, w_ref, y_ref, lw_ref, lb_ref, wpw_ref,
             do_ref, dza_ref, dy_ref, dzc_ref, dh_ref, dgate_ref, gwb_ref, loss_ref, gpwb_ref, rows_ref,
             gw_ref, gpw_ref):
        b, i = pl.program_id(0), pl.program_id(1)

        @pl.when((b == 0) & (i == 0))
        def _():
            gw_ref[...] = jnp.zeros_like(gw_ref)
            gpw_ref[...] = jnp.zeros_like(gpw_ref)
            rows_ref[...] = jnp.zeros_like(rows_ref)
            loss_ref[...] = jnp.zeros_like(loss_ref)

        @pl.when(i == 0)
        def _():
            dgate_ref[...] = jnp.zeros_like(dgate_ref)

        gate = mod_ref[0, 2:3, :]
        w = w_ref[...]
        mixes, douts = [], []
        for part in range(TOKEN_PARTS):
            rows = pl.ds(part * TM, TM)
            o, za_v, cp_v, zc_v = o_ref[0, rows, :], za_ref[0, rows, :], cp_ref[0, rows, :], zc_ref[0, rows, :]
            sa = _sigmoid(za_v)
            sc = _sigmoid(zc_v)
            silu_a = za_v * sa
            silu_c = zc_v * sc
            mix = jnp.concatenate([(o * silu_a).astype(BF16), (cp_v * silu_c).astype(BF16)], axis=-1)
            out = _dot(mix, w)
            err = x_ref[0, rows, :] + gate * out - t_ref[0, rows, :]
            loss_ref[...] += jnp.sum(err * err, axis=0, keepdims=True)
            dh = err * (1.0 / D)
            dh_ref[0, rows, :] = dh
            dgate_ref[0] += jnp.sum(dh * out, axis=0, keepdims=True)
            dout = (dh * gate).astype(BF16)
            mixes.append(mix)
            douts.append(dout)
            dmix = _dot_nt(dout, w)
            dga = dmix[:, 0:DA]
            dgc = dmix[:, DA:DA + DC]
            dov = dga * silu_a
            for h in range(DA // HD):
                do_ref[0, h, rows, :] = dov[:, h * HD:(h + 1) * HD].astype(BF16)
            dza_ref[0, rows, :] = (dga * o * (sa * (1.0 + za_v * (1.0 - sa)))).astype(BF16)
            dzc_ref[0, rows, :] = (dgc * cp_v * (sc * (1.0 + zc_v * (1.0 - sc)))).astype(BF16)
            dcp = dgc * silu_c
            y = y_ref[0, rows, :]
            yc = y - jnp.mean(y, axis=-1, keepdims=True)
            rstd = lax.rsqrt(jnp.mean(yc * yc, axis=-1, keepdims=True) + EPS)
            yn = yc * rstd
            lw = lw_ref[...]
            z = yn * lw + lb_ref[...]
            sg = _sigmoid(z)
            dcp_b = dcp.astype(BF16)
            gpw_ref[...] += _dot_tn((z * sg).astype(BF16), dcp_b)
            dz = _dot_nt(dcp_b, wpw_ref[...]) * (sg * (1.0 + z * (1.0 - sg)))
            dyn = dz * lw
            dy = rstd * (dyn - jnp.mean(dyn, axis=-1, keepdims=True) - yn * jnp.mean(dyn * yn, axis=-1, keepdims=True))
            dy_ref[0, rows, :] = dy
            rows_ref[0:1, :] += jnp.sum(dcp, axis=0, keepdims=True)
            rows_ref[1:2, :] += jnp.sum(dz * yn, axis=0, keepdims=True)
            rows_ref[2:3, :] += jnp.sum(dz, axis=0, keepdims=True)
            rows_ref[3:4, :] += jnp.sum(dy, axis=0, keepdims=True)
        gw_ref[...] += _dot_tn(jnp.concatenate(mixes, axis=0), jnp.concatenate(douts, axis=0))

        @pl.when((b == bl - 1) & (i == s // tm - 1))
        def _():
            gwb_ref[...] = gw_ref[...].astype(BF16)
            gpwb_ref[...] = gpw_ref[...].astype(BF16)

    def const(shape):
        return pl.BlockSpec(shape, lambda b, i: (0,) * len(shape))

    def tile(w):
        return pl.BlockSpec((1, tm, w), lambda b, i: (b, i, 0))

    return pl.pallas_call(
        body, name="out_fwd_bwd", grid=(bl, s // tm),
        in_specs=[tile(DA), tile(DA), tile(DC), tile(DC), tile(D), tile(D),
                  pl.BlockSpec((1, 3, D), lambda b, i: (b, 0, 0)), const((D, D)),
                  tile(DC), const((1, DC)), const((1, DC)), const((DC, DC))],
        out_specs=[pl.BlockSpec((1, DA // HD, tm, HD), lambda b, i: (b, 0, i, 0)), tile(DA), tile(DC), tile(DC), tile(D),
                   pl.BlockSpec((1, 1, D), lambda b, i: (b, 0, 0)), const((D, D)), const((1, D)),
                   const((DC, DC)), const((8, DC))],
        out_shape=[jax.ShapeDtypeStruct((bl, DA // HD, s, HD), BF16), jax.ShapeDtypeStruct((bl, s, DA), BF16),
                   jax.ShapeDtypeStruct((bl, s, DC), F32), jax.ShapeDtypeStruct((bl, s, DC), BF16),
                   jax.ShapeDtypeStruct((bl, s, D), F32), jax.ShapeDtypeStruct((bl, 1, D), F32),
                   jax.ShapeDtypeStruct((D, D), BF16), jax.ShapeDtypeStruct((1, D), F32),
                   jax.ShapeDtypeStruct((DC, DC), BF16), jax.ShapeDtypeStruct((8, DC), F32)],
        scratch_shapes=[pltpu.VMEM((D, D), F32), pltpu.VMEM((DC, DC), F32)],
        compiler_params=_params(("arbitrary", "arbitrary")),
    )(attn, za, cp, zc, x, target, modrows, w_out_b, y_conv, ln_w, ln_b, w_pw_b)


def _rms_heads_bwd(dy, x, w_t, ones_bd):
    r = lax.rsqrt(_segsum(x * x, ones_bd) * (1.0 / HD) + EPS)
    xh = x * r
    g = dy * w_t
    dx = r * (g - xh * (_segsum(g * xh, ones_bd) * (1.0 / HD)))
    return dx, dy * xh


def _ctx_bwd(ctx, modc, norm_w, w_kv_b, pkv_c, dk_c, dv_c, knw_t, ones_bd):
    bl, cl, _ = ctx.shape

    def body(x_ref, mod_ref, nw_ref, w_ref, p_ref, dk_ref, dv_ref, knw_ref, bd_ref, gw_ref, rows_ref, dknw_ref):
        @pl.when(pl.program_id(0) == 0)
        def _():
            gw_ref[...] = jnp.zeros_like(gw_ref)
            rows_ref[...] = jnp.zeros_like(rows_ref)
            dknw_ref[...] = jnp.zeros_like(dknw_ref)

        xv = x_ref[0]
        shift = mod_ref[0, 0:1, :]
        scale = mod_ref[0, 1:2, :]
        nw = nw_ref[...]
        r = lax.rsqrt(jnp.mean(xv * xv, axis=-1, keepdims=True) + EPS)
        xn = xv * r
        yv = xn * nw
        u = yv * (1.0 + scale) + shift
        dkv = jnp.concatenate([dk_ref[0, 0], dk_ref[0, 1]], axis=-1)
        dpk, dknw = _rms_heads_bwd(dkv, p_ref[0][:, 0:KVW], knw_ref[...], bd_ref[...])
        dp = jnp.concatenate([dpk.astype(BF16), dv_ref[0, 0].astype(BF16), dv_ref[0, 1].astype(BF16)], axis=-1)
        gw_ref[...] += _dot_tn(dp, u.astype(BF16))
        du = _dot(dp, w_ref[...])
        rows_ref[0:1, :] += jnp.sum(du, axis=0, keepdims=True)
        rows_ref[1:2, :] += jnp.sum(du * yv, axis=0, keepdims=True)
        rows_ref[2:3, :] += jnp.sum(du * (1.0 + scale) * xn, axis=0, keepdims=True)
        dknw_ref[...] += jnp.sum(dknw, axis=0, keepdims=True)

    def const(shape):
        return pl.BlockSpec(shape, lambda b: (0,) * len(shape))

    def tile(w):
        return pl.BlockSpec((1, cl, w), lambda b: (b, 0, 0))

    ctx_block = (dk_c.shape[2] - cl) // cl
    kv_tile = pl.BlockSpec((1, KVW // HD, cl, HD), lambda b: (b, 0, ctx_block, 0))
    return pl.pallas_call(
        body, name="ctx_bwd", grid=(bl,),
        in_specs=[tile(D), const((1, 3, D)), const((1, D)), _KV_ROWS_OF_W_IN_T, tile(2 * KVW), kv_tile, kv_tile,
                  const((1, KVW)), const((KVW, KVW))],
        out_specs=[const((2 * KVW, D)), const((8, D)), const((1, KVW))],
        out_shape=[jax.ShapeDtypeStruct((2 * KVW, D), F32), jax.ShapeDtypeStruct((8, D), F32),
                   jax.ShapeDtypeStruct((1, KVW), F32)],
        compiler_params=_params(("arbitrary",)),
    )(ctx, modc, norm_w, w_kv_b, pkv_c, dk_c, dv_c, knw_t, ones_bd)


def _bwd_in(x, modrows, norm_w, w_in_b, cos, sins, qnw_t, knw_t, ones_bd,
            pq, pkv, dq, dk, dv, dza, dglu, dzc, dh, gw_kv):
    bl, s, _ = x.shape
    tm = TOKEN_PARTS * TM
    nt = s // tm

    def body(x_ref, mod_ref, nw_ref, win_hbm, cos_ref, sin_ref, qnw_ref, knw_ref, bd_ref,
             pq_ref, pkv_ref, dq_ref, dk_ref, dv_ref, dza_ref, dglu_ref, dzc_ref, dh_ref, gwkv_ref,
             gx_ref, gw_hbm, dmod_ref, dnw_ref, dqnw_ref, dknw_ref, win_ref, gw_acc, sem):
        b, i = pl.program_id(0), pl.program_id(1)

        @pl.when((b == 0) & (i == 0))
        def _():
            cp = pltpu.make_async_copy(win_hbm, win_ref, sem)
            cp.start()
            gw_acc[...] = jnp.zeros_like(gw_acc)
            dnw_ref[...] = jnp.zeros_like(dnw_ref)
            dqnw_ref[...] = jnp.zeros_like(dqnw_ref)
            dknw_ref[...] = jnp.zeros_like(dknw_ref)
            cp.wait()

        @pl.when(i == 0)
        def _():
            dmod_ref[...] = jnp.zeros_like(dmod_ref)

        bd = bd_ref[...]
        shift = mod_ref[0, 0:1, :]
        scale = mod_ref[0, 1:2, :]
        nw = nw_ref[...]
        dps, us = [], []
        for part in range(TOKEN_PARTS):
            rows = pl.ds(part * TM, TM)
            ck = cos_ref[rows, :]
            sk = sin_ref[rows, :]
            cs = jnp.concatenate([ck] * (DA // KVW), axis=-1)
            sn = jnp.concatenate([sk] * (DA // KVW), axis=-1)
            dqn = _rope_bwd(dq_ref[0, rows, :], cs, sn)
            dpq, dqnw = _rms_heads_bwd(dqn, pq_ref[0, rows, :], qnw_ref[...], bd)
            dkn = _rope_bwd(jnp.concatenate([dk_ref[0, 0, rows, :], dk_ref[0, 1, rows, :]], axis=-1), ck, sk)
            dpk, dknw = _rms_heads_bwd(dkn, pkv_ref[0, rows, 0:KVW], knw_ref[...], bd[0:KVW, 0:KVW])
            dqnw_ref[...] += jnp.sum(dqnw, axis=0, keepdims=True)
            dknw_ref[...] += jnp.sum(dknw, axis=0, keepdims=True)
            dp = jnp.concatenate(
                [dpq.astype(BF16), dpk.astype(BF16), dv_ref[0, 0, rows, :].astype(BF16), dv_ref[0, 1, rows, :].astype(BF16),
                 dza_ref[0, rows, :], dglu_ref[0, rows, :], dzc_ref[0, rows, :]], axis=-1)

            xv = x_ref[0, rows, :]
            r = lax.rsqrt(jnp.mean(xv * xv, axis=-1, keepdims=True) + EPS)
            xn = xv * r
            yv = xn * nw
            u = yv * (1.0 + scale) + shift
            dps.append(dp)
            us.append(u.astype(BF16))
            du = _dot(dp, win_ref[...])
            dmod_ref[0, 0:1, :] += jnp.sum(du, axis=0, keepdims=True)
            dmod_ref[0, 1:2, :] += jnp.sum(du * yv, axis=0, keepdims=True)
            dy = du * (1.0 + scale)
            dnw_ref[...] += jnp.sum(dy * xn, axis=0, keepdims=True)
            dxn = dy * nw
            gx_ref[0, rows, :] = dh_ref[0, rows, :] + r * (dxn - xn * jnp.mean(dxn * xn, axis=-1, keepdims=True))
        gw_acc[...] += _dot_tn(jnp.concatenate(dps, axis=0), jnp.concatenate(us, axis=0))

        @pl.when((b == bl - 1) & (i == nt - 1))
        def _():
            gw_acc[DA:DA + 2 * KVW, :] += gwkv_ref[...]

            def to_bf16(j, carry):
                rows = pl.ds(pl.multiple_of(j * 2 * KVW, 2 * KVW), 2 * KVW)
                win_ref[rows, :] = gw_acc[rows, :].astype(BF16)
                return carry

            lax.fori_loop(0, D_IN // (2 * KVW), to_bf16, 0)
            pltpu.sync_copy(win_ref, gw_hbm)

    def tile(w):
        return pl.BlockSpec((1, tm, w), lambda b, i: (b, i, 0))

    def const(shape):
        return pl.BlockSpec(shape, lambda b, i: (0,) * len(shape))

    anyspace = pl.BlockSpec(memory_space=pl.ANY)
    rope = pl.BlockSpec((tm, KVW), lambda b, i: (i, 0))
    kv_tile = pl.BlockSpec((1, KVW // HD, tm, HD), lambda b, i: (b, 0, i, 0))
    return pl.pallas_call(
        body, name="bwd_in", grid=(bl, nt),
        in_specs=[tile(D), pl.BlockSpec((1, 3, D), lambda b, i: (b, 0, 0)), const((1, D)), anyspace, rope, rope,
                  const((1, DA)), const((1, KVW)), const((DA, DA)),
                  tile(DA), tile(2 * KVW), tile(DA), kv_tile, kv_tile, tile(DA), tile(2 * DC), tile(DC), tile(D),
                  const((2 * KVW, D))],
        out_specs=[tile(D), anyspace, pl.BlockSpec((1, 2, D), lambda b, i: (b, 0, 0)), const((1, D)),
                   const((1, DA)), const((1, KVW))],
        out_shape=[jax.ShapeDtypeStruct((bl, s, D), F32), jax.ShapeDtypeStruct((D_IN, D), BF16),
                   jax.ShapeDtypeStruct((bl, 2, D), F32), jax.ShapeDtypeStruct((1, D), F32),
                   jax.ShapeDtypeStruct((1, DA), F32), jax.ShapeDtypeStruct((1, KVW), F32)],
        scratch_shapes=[pltpu.VMEM((D_IN, D), BF16), pltpu.VMEM((D_IN, D), F32), pltpu.SemaphoreType.DMA],
        compiler_params=_params(("arbitrary", "arbitrary")),
    )(x, modrows, norm_w, w_in_b, cos, sins, qnw_t, knw_t, ones_bd,
      pq, pkv, dq, dk, dv, dza, dglu, dzc, dh, gw_kv)


_LOSS, _DMODC, _NW, _QN, _KN, _CB, _LW, _LB, _BPW, SMALL_W = 0, 1024, 4096, 5120, 5248, 5376, 5888, 6400, 6912, 7424


ROW_W = 1792


def _put_flat(ref, off, value):
    n, done = value.shape[1], 0
    while done < n:
        r, c = divmod(off + done, ROW_W)
        take = min(n - done, ROW_W - c)
        ref[r:r + 1, c:c + take] = value[:, done:done + take]
        done += take


def _get_flat(arr, off, n):
    parts, done = [], 0
    while done < n:
        r, c = divmod(off + done, ROW_W)
        take = min(n - done, ROW_W - c)
        parts.append(arr[r:r + 1, c:c + take])
        done += take
    return parts[0] if len(parts) == 1 else jnp.concatenate(parts, axis=-1)


def _pack_small_body(loss_ref, ctx_ref, dnw_ref, dqnw_ref, dknw_ref, dknwc_ref, conv_ref, dss_ref, dgate_ref, o_ref):
    bl = dss_ref.shape[0]
    assert SMALL_W + bl * 3 * D <= 8 * ROW_W
    o_ref[...] = jnp.zeros_like(o_ref)
    _put_flat(o_ref, _LOSS, loss_ref[...])
    _put_flat(o_ref, _DMODC, ctx_ref[0:1, :])
    _put_flat(o_ref, _DMODC + D, ctx_ref[1:2, :])
    _put_flat(o_ref, _NW, dnw_ref[...] + ctx_ref[2:3, :])
    dq = dqnw_ref[...]
    qn = dq[:, 0:HD]
    for h in range(1, DA // HD):
        qn = qn + dq[:, h * HD:(h + 1) * HD]
    _put_flat(o_ref, _QN, qn)
    dk = dknw_ref[...] + dknwc_ref[...]
    _put_flat(o_ref, _KN, dk[:, 0:HD] + dk[:, HD:2 * HD])
    _put_flat(o_ref, _BPW, conv_ref[0:1, :])
    _put_flat(o_ref, _LW, conv_ref[1:2, :])
    _put_flat(o_ref, _LB, conv_ref[2:3, :])
    _put_flat(o_ref, _CB, conv_ref[3:4, :])
    for b in range(bl):
        _put_flat(o_ref, SMALL_W + b * 3 * D, dss_ref[b, 0:1, :])
        _put_flat(o_ref, SMALL_W + b * 3 * D + D, dss_ref[b, 1:2, :])
        _put_flat(o_ref, SMALL_W + b * 3 * D + 2 * D, dgate_ref[b])


_SMALL = (("b_mod", None), ("norm_w", _NW), ("q_norm_w", _QN), ("k_norm_w", _KN), ("conv_b", _CB),
          ("conv_ln_w", _LW), ("conv_ln_b", _LB), ("b_pw", _BPW), ("c_ctx", None))


def _epilogue(parts_in, pieces, c_rows, w_mod_loc):
    bl = pieces[7].shape[0]
    n_ex = N_DEV * bl
    n_mod = w_mod_loc.shape[1]
    rb = 32
    shp = parts_in.shape[1:]
    rows_in = shp[0]

    def body(*refs):
        it = iter(refs)
        take = lambda k: [next(it) for _ in range(k)]
        (parts,) = take(1)
        piece_refs = take(9)
        (c_ref, wm_ref) = take(2)
        (g_in, g_wm, sum_ref, gb_ref, gc_all, loss_ref) = take(6)
        (mine, got_sib, stage, got_chip, payload, gathered, dmod_full, gc_mine) = take(8)
        (d2d_send, d2d_recv, ici_send, ici_recv, local_sems, sg_send, sg_recv, gc_send, gc_recv, misc_sems) = take(10)

        x, y, c = _coords()
        me = _lin(x, y, c)
        sib = (x, y, 1 - c)
        home = 2 * x + y

        def rows_loop(fn):
            def step(i, carry):
                fn(pl.ds(pl.multiple_of(i * rb, rb), rb))
                return carry
            lax.fori_loop(0, rows_in // rb, step, 0)

        def direct_gather(src, dst, send_sems, recv_sems, local_sem):
            cps = [pltpu.make_async_copy(src, dst.at[me], local_sem)]
            for k in range(1, N_DEV):
                peer = (1 - x if k & 4 else x, 1 - y if k & 2 else y, 1 - c if k & 1 else c)
                cps.append(pltpu.make_async_remote_copy(
                    src_ref=src, dst_ref=dst.at[me], send_sem=send_sems.at[k - 1], recv_sem=recv_sems.at[k - 1],
                    device_id=peer, device_id_type=MESH_ID))
            for cp in cps:
                cp.start()
            return cps

        _pack_small_body(*piece_refs, payload)
        small_cps = direct_gather(payload, gathered, sg_send, sg_recv, misc_sems.at[0])

        local, d2d, ici = [], [], []
        for s in range(4):
            cp = pltpu.make_async_copy(parts.at[_lin(s // 2, s % 2, c)], mine.at[s], local_sems.at[s])
            cp.start()
            local.append(cp)
            rc = pltpu.make_async_remote_copy(
                src_ref=parts.at[_lin(s // 2, s % 2, 1 - c)], dst_ref=got_sib.at[s],
                send_sem=d2d_send.at[s], recv_sem=d2d_recv.at[s], device_id=sib, device_id_type=MESH_ID)
            rc.start()
            d2d.append(rc)

        for cp in small_cps[1:]:
            cp.wait_recv()
        small_cps[0].wait()
        tot = gathered[0]
        for j in range(1, N_DEV):
            tot = tot + gathered[j]
        summed = _get_flat(tot, 0, SMALL_W)
        dmod_full[...] = jnp.zeros_like(dmod_full)
        for j in range(N_DEV):
            arr = gathered[j]
            for b in range(bl):
                dmod_full[j * bl + b:j * bl + b + 1, :] = _get_flat(arr, SMALL_W + b * 3 * D, 3 * D)
        dmod_full[n_ex:n_ex + 1, :] = summed[:, _DMODC:_DMODC + 3 * D]
        sum_ref[...] = summed
        gb_ref[...] = jnp.sum(dmod_full[...], axis=0, keepdims=True)
        loss_ref[...] = (0.5 / D) * jnp.sum(summed[:, _LOSS:_LOSS + D], axis=-1, keepdims=True)

        north = c == 1
        first = (jnp.where(north, 1 - x, x), jnp.where(north, y, 1 - y))
        second = (jnp.where(north, x, 1 - x), jnp.where(north, 1 - y, y))
        for s in range(4):
            local[s].wait()
            d2d[s].wait_recv()

        def chip_sum(k, chip, relayed):
            slot = 2 * chip[0] + chip[1]

            def pair_sum(rs):
                acc = mine[slot, rs, :].astype(F32) + got_sib[slot, rs, :].astype(F32)
                if relayed:
                    acc = acc + got_chip[1, rs, :].astype(F32)
                stage[k, rs, :] = acc.astype(BF16)

            rows_loop(pair_sum)

        def send(k, to):
            rc = pltpu.make_async_remote_copy(
                src_ref=stage.at[k], dst_ref=got_chip.at[k], send_sem=ici_send.at[k], recv_sem=ici_recv.at[k],
                device_id=(to[0], to[1], c), device_id_type=MESH_ID)
            rc.start()
            ici.append(rc)

        chip_sum(0, first, False)
        send(0, first)
        chip_sum(1, (1 - x, 1 - y), False)
        send(1, first)

        cr = c_ref[...]
        act = (cr * _sigmoid(cr)).astype(BF16)
        dm = dmod_full[:, pl.ds(pl.multiple_of(me * n_mod, 128), n_mod)].astype(BF16)
        g_wm[...] = _dot_tn(act, dm)
        gc_mine[...] = _dot_nt(dm[n_ex:n_ex + 8, :], wm_ref[...].astype(BF16))
        gc_cps = direct_gather(gc_mine, gc_all, gc_send, gc_recv, misc_sems.at[1])

        ici[1].wait_recv()
        chip_sum(2, second, True)
        send(2, second)
        ici[0].wait_recv()
        ici[2].wait_recv()

        def finish(rs):
            gsum = mine[home, rs, :].astype(F32) + got_sib[home, rs, :].astype(F32)
            g_in[rs, :] = gsum + got_chip[0, rs, :].astype(F32) + got_chip[2, rs, :].astype(F32)

        rows_loop(finish)

        for cp in gc_cps[1:]:
            cp.wait_recv()
        gc_cps[0].wait()
        for rc in d2d + ici + small_cps[1:] + gc_cps[1:]:
            rc.wait_send()

    vm = pl.BlockSpec(memory_space=pltpu.VMEM)
    anyspace = pl.BlockSpec(memory_space=pl.ANY)
    assert rows_in % rb == 0 and parts_in.dtype == BF16
    args = [parts_in, *pieces, c_rows, w_mod_loc]
    in_specs = [anyspace] + [vm] * (len(args) - 1)
    out_shape = [jax.ShapeDtypeStruct(shp, F32), jax.ShapeDtypeStruct(w_mod_loc.shape, F32),
                 jax.ShapeDtypeStruct((1, SMALL_W), F32), jax.ShapeDtypeStruct((1, 3 * D), F32),
                 jax.ShapeDtypeStruct((N_DEV, 8, D), F32), jax.ShapeDtypeStruct((1, 1), F32)]
    scratch = [pltpu.VMEM((4,) + shp, BF16), pltpu.VMEM((4,) + shp, BF16), pltpu.VMEM((3,) + shp, BF16),
               pltpu.VMEM((3,) + shp, BF16), pltpu.VMEM((8, ROW_W), F32), pltpu.VMEM((N_DEV, 8, ROW_W), F32),
               pltpu.VMEM((n_ex + 8, 3 * D), F32), pltpu.VMEM((8, D), F32),
               pltpu.SemaphoreType.DMA((4,)), pltpu.SemaphoreType.DMA((4,)), pltpu.SemaphoreType.DMA((3,)),
               pltpu.SemaphoreType.DMA((3,)), pltpu.SemaphoreType.DMA((4,)),
               pltpu.SemaphoreType.DMA((N_DEV - 1,)), pltpu.SemaphoreType.DMA((N_DEV - 1,)),
               pltpu.SemaphoreType.DMA((N_DEV - 1,)), pltpu.SemaphoreType.DMA((N_DEV - 1,)),
               pltpu.SemaphoreType.DMA((2,))]
    return pl.pallas_call(
        body, name="epilogue", out_shape=out_shape, in_specs=in_specs, out_specs=[vm] * len(out_shape),
        scratch_shapes=scratch, compiler_params=pltpu.CompilerParams(vmem_limit_bytes=VMEM_LIMIT),
    )(*args)


def _final_adamw(g_in, w_in_t, m_in_t, v_in_t, g_wm, w_mod_loc, m_mod, v_mod, summed, g_bmod, gc_all,
                 small_w, small_m, small_v):
    ns = len(_SMALL)
    rb = 32

    def body(*refs):
        it = iter(refs)
        take = lambda k: [next(it) for _ in range(k)]
        (gin_ref, w_ref, m_ref, v_ref, gwm_ref, wm_ref, mm_ref, vm_ref, sum_ref, gb_ref, gc_ref) = take(11)
        sw, sm, sv = take(ns), take(ns), take(ns)
        (d_in, nm_in, nv_in, d_wm, nm_wm, nv_wm) = take(6)
        souts = take(4 * ns)

        def big(g_r, w_r, m_r, v_r, d_o, nm_o, nv_o):
            def step(i, carry):
                rs = pl.ds(pl.multiple_of(i * rb, rb), rb)
                dl, m_new, v_new = _adamw(w_r[rs, :], g_r[rs, :], m_r[rs, :], v_r[rs, :])
                d_o[rs, :] = dl
                nm_o[rs, :] = m_new
                nv_o[rs, :] = v_new
                return carry
            lax.fori_loop(0, w_r.shape[0] // rb, step, 0)

        big(gin_ref, w_ref, m_ref, v_ref, d_in, nm_in, nv_in)
        big(gwm_ref, wm_ref, mm_ref, vm_ref, d_wm, nm_wm, nv_wm)
        for k, (name, off) in enumerate(_SMALL):
            w = sw[k][...]
            if name == "b_mod":
                gk = gb_ref[...]
            elif name == "c_ctx":
                acc = gc_ref[0, 0:1, :]
                for j in range(1, N_DEV):
                    acc = acc + gc_ref[j, 0:1, :]
                sg = _sigmoid(w)
                gk = acc * (sg * (1.0 + w * (1.0 - sg)))
            else:
                gk = sum_ref[:, off:off + w.shape[1]]
            dl, m_new, v_new = _adamw(w, gk, sm[k][...], sv[k][...])
            souts[k][...] = gk
            souts[ns + k][...] = dl
            souts[2 * ns + k][...] = m_new
            souts[3 * ns + k][...] = v_new

    assert w_in_t.shape[0] % rb == 0 and w_mod_loc.shape[0] % rb == 0
    big_shape = jax.ShapeDtypeStruct(w_in_t.shape, F32)
    mod_shape = jax.ShapeDtypeStruct(w_mod_loc.shape, F32)
    out_shape = [big_shape] * 3 + [mod_shape] * 3 + [jax.ShapeDtypeStruct(w.shape, F32) for w in small_w] * 4
    outs = pl.pallas_call(
        body, name="final_adamw", out_shape=out_shape,
        compiler_params=pltpu.CompilerParams(vmem_limit_bytes=VMEM_LIMIT),
    )(g_in, w_in_t, m_in_t, v_in_t, g_wm, w_mod_loc, m_mod, v_mod, summed, g_bmod, gc_all, *small_w, *small_m, *small_v)
    small_outs = [outs[6 + k * ns:6 + (k + 1) * ns] for k in range(4)]
    return (g_in,) + tuple(outs[0:3]), (g_wm,) + tuple(outs[3:6]), small_outs


def _rope_tables(s):
    t = jnp.arange(s, dtype=jnp.int32)
    row = (t // GRID_W).astype(F32)
    col = (t % GRID_W).astype(F32)
    freqs = ROPE_THETA ** (-jnp.arange(0, HD // 2, 2, dtype=F32) / (HD // 2))
    ang_r = row[:, None] * freqs[None, :]
    ang_c = col[:, None] * freqs[None, :]
    cr, sr, cc, sc = jnp.cos(ang_r), jnp.sin(ang_r), jnp.cos(ang_c), jnp.sin(ang_c)
    cos = jnp.concatenate([cr, cr, cc, cc], axis=-1)
    sins = jnp.concatenate([-sr, sr, -sc, sc], axis=-1)
    return jnp.tile(cos, (1, KVW // HD)), jnp.tile(sins, (1, KVW // HD))


def kernel(x, c, ctx, c_ctx, w_mod, b_mod, norm_w, w_in, q_norm_w, k_norm_w, conv_w, conv_b, conv_ln_w, conv_ln_b, w_pw, b_pw, w_out, loss_target, m_c_ctx, m_w_mod, m_b_mod, m_norm_w, m_w_in, m_q_norm_w, m_k_norm_w, m_conv_w, m_conv_b, m_conv_ln_w, m_conv_ln_b, m_w_pw, m_b_pw, m_w_out, v_c_ctx, v_w_mod, v_b_mod, v_norm_w, v_w_in, v_q_norm_w, v_k_norm_w, v_conv_w, v_conv_b, v_conv_ln_w, v_conv_ln_b, v_w_pw, v_b_pw, v_w_out):
    bl, s, _ = x.shape
    cl = ctx.shape[1]
    me = _lin(*_coords())

    conv_w_pad = jnp.pad(conv_w[0], ((0, 32 - KW), (0, 0)))
    c_pad = jnp.pad(c, ((0, 8 - bl), (0, 0)))
    n_ex = N_DEV * bl
    g_win, c_rows, g_mod = _prologue(w_in[0].T.astype(BF16), c_pad, bl, c_ctx[None, :], w_mod[0], b_mod)
    w_in_b = g_win.reshape(D_IN, D)
    mod_all = g_mod.transpose(1, 0, 2).reshape(n_ex + 8, 3 * D)
    modrows = lax.dynamic_slice_in_dim(mod_all, me * bl, bl, axis=0).reshape(bl, 3, D)
    modc = mod_all[n_ex].reshape(1, 3, D)

    cos, sins = _rope_tables(s)
    qnw_t = jnp.tile(q_norm_w, (1, DA // HD))
    knw_t = jnp.tile(k_norm_w, (1, KVW // HD))
    lane = jnp.arange(DA, dtype=jnp.int32) // HD
    ones_bd = (lane[:, None] == lane[None, :]).astype(BF16)
    ones_kv = ones_bd[0:KVW, 0:KVW]
    w_kv_b = w_in_b

    k_ctx, v_ctx, pkv_c = _ctx_fwd(ctx, modc, norm_w, w_kv_b, knw_t, ones_kv, cl + s)
    (q_h, k_h, v_h, pq, pkv, za, glu, zc), (g_wout, g_wpw, g_cw) = _fwd_in(
        x, modrows, norm_w, w_in_b, cos, sins, qnw_t, knw_t, ones_bd, k_ctx, v_ctx,
        [w_out[0], w_pw[0], conv_w_pad], [BF16, BF16, F32])
    w_out_b = g_wout.reshape(D, D)
    w_pw_b = g_wpw.reshape(DC, DC)
    conv_w_full = g_cw.transpose(1, 0, 2).reshape(32, DC)
    attn, lse = _attn_fwd(q_h, k_h, v_h)
    y_conv, cp = _conv_fwd(glu, conv_w_full, conv_b, conv_ln_w, conv_ln_b, w_pw_b, b_pw)

    do_h, dza, dy_conv, dzc, dh, dgate, gw_out, loss_row, gw_pw, conv_rows = _out_fwd_bwd(
        attn, za, cp, zc, x, loss_target, modrows, w_out_b, y_conv, conv_ln_w, conv_ln_b, w_pw_b)
    dglu, g_cw_full = _conv_bwd_depthwise(glu, dy_conv, conv_w_full)
    parts_out = gw_out.reshape(N_DEV, D // N_DEV, D)
    parts_pw = gw_pw.reshape(N_DEV, DC // N_DEV, DC)
    parts_cw = g_cw_full.astype(BF16).reshape(32, N_DEV, DC // N_DEV).transpose(1, 0, 2)
    (dq, dk_h, dv_h), (got_out, got_pw, got_cw) = _attn_bwd(
        q_h, k_h, v_h, do_h, attn, lse, [parts_out, parts_pw, parts_cw])
    gw_kv, ctx_rows, dknw_c = _ctx_bwd(ctx, modc, norm_w, w_kv_b, pkv_c, dk_h, dv_h, knw_t, ones_kv)
    grad_x, gw_in, dmod_ss, dnw, dqnw, dknw = _bwd_in(
        x, modrows, norm_w, w_in_b, cos, sins, qnw_t, knw_t, ones_bd,
        pq, pkv, dq, dk_h, dv_h, dza, dglu, dzc, dh, gw_kv)

    def pad_cw(a):
        return jnp.pad(a[0], ((0, 32 - KW), (0, 0)))

    r_out, r_pw, r_cw = _sum_devices_adamw(
        [(got_out, w_out[0], m_w_out[0], v_w_out[0]), (got_pw, w_pw[0], m_w_pw[0], v_w_pw[0]),
         (got_cw, pad_cw(conv_w), pad_cw(m_conv_w), pad_cw(v_conv_w))])
    r_cw = tuple(a[:KW] for a in r_cw)

    given = {"c_ctx": (c_ctx, m_c_ctx, v_c_ctx), "b_mod": (b_mod, m_b_mod, v_b_mod), "norm_w": (norm_w, m_norm_w, v_norm_w),
             "q_norm_w": (q_norm_w, m_q_norm_w, v_q_norm_w), "k_norm_w": (k_norm_w, m_k_norm_w, v_k_norm_w),
             "conv_b": (conv_b, m_conv_b, v_conv_b), "conv_ln_w": (conv_ln_w, m_conv_ln_w, v_conv_ln_w),
             "conv_ln_b": (conv_ln_b, m_conv_ln_b, v_conv_ln_b), "b_pw": (b_pw, m_b_pw, v_b_pw)}
    as_rows = [[given[name][which].reshape(1, -1) for name, _ in _SMALL] for which in range(3)]
    g_in_t, g_wmod, summed, g_bmod, gc_all, loss11 = _epilogue(
        gw_in.reshape(N_DEV, D_IN // N_DEV, D),
        [loss_row, ctx_rows, dnw, dqnw, dknw, dknw_c, conv_rows, dmod_ss, dgate], c_rows, w_mod[0])
    r_in, r_wmod, small_outs = _final_adamw(
        g_in_t, w_in[0].T, m_w_in[0].T, v_w_in[0].T, g_wmod, w_mod[0], m_w_mod[0], v_w_mod[0],
        summed, g_bmod, gc_all, *as_rows)
    r_in = tuple(a.T for a in r_in)

    big = {"w_mod": r_wmod, "w_in": r_in, "conv_w": r_cw, "w_pw": r_pw, "w_out": r_out}
    order = ["c_ctx", "w_mod", "b_mod", "norm_w", "w_in", "q_norm_w", "k_norm_w", "conv_w", "conv_b", "conv_ln_w",
             "conv_ln_b", "w_pw", "b_pw", "w_out"]
    small_index = {name: k for k, (name, _) in enumerate(_SMALL)}
    outs = [loss11.reshape(()), grad_x]
    for which in range(4):
        for name in order:
            if name in big:
                outs.append(big[name][which][None])
            else:
                outs.append(small_outs[which][small_index[name]].reshape(given[name][0].shape))
    return tuple(outs)
```

```python
import jax
import jax.numpy as jnp
from jax import lax
from jax.experimental import pallas as pl
from jax.experimental.pallas import tpu as pltpu

F32, BF16 = jnp.float32, jnp.bfloat16
MESH_ID = pl.DeviceIdType.MESH

N_DEV = 8
D = 1024
D_IN = 2816
DA = 512
DC = 512
HD = 64
KVW = 128
KW = 31
HALO = 16
EPS = 1e-6
ROPE_THETA = 10000.0
GRID_W = 64

ADAM_LR, ADAM_B1, ADAM_B2, ADAM_EPS, ADAM_WD, ADAM_STEP = 0.001, 0.9, 0.999, 1e-08, 0.01, 10

VMEM_LIMIT = 56 * 1024 * 1024

TM = 256
TQ = 128
TOKEN_PARTS = 2
OUT_TM = 512
BWD_PARTS = 2
FWD_PARTS = 4
TC = 512
CH = 64


def _params(sem, vmem=VMEM_LIMIT):
    return pltpu.CompilerParams(dimension_semantics=sem, vmem_limit_bytes=vmem)


def _dot(a, b):
    return jnp.dot(a, b, preferred_element_type=F32)


def _dot_nt(a, b):
    return lax.dot_general(a, b, (((1,), (1,)), ((), ())), preferred_element_type=F32)


def _dot_tn(a, b):
    return lax.dot_general(a, b, (((0,), (0,)), ((), ())), preferred_element_type=F32)


def _sigmoid(z):
    return 1.0 / (1.0 + jnp.exp(-z))


def _segsum(v, ones_bd):
    return _dot(v.astype(BF16), ones_bd)


def _swap16(x):
    w = x.shape[-1]
    lane = lax.broadcasted_iota(jnp.int32, x.shape, 1)
    return jnp.where((lane % 32) < 16, pltpu.roll(x, w - 16, 1), pltpu.roll(x, 16, 1))


def _with_ones_column(v):
    one = (lax.broadcasted_iota(jnp.int32, v.shape, 1) == 0).astype(v.dtype)
    return jnp.concatenate([v, one], axis=-1)


def _rope(x, cos, sins):
    return x * cos + _swap16(x) * sins


def _rope_bwd(d, cos, sins):
    return d * cos + _swap16(d * sins)


def _adamw(w, g, m, v):
    m2 = ADAM_B1 * m + (1.0 - ADAM_B1) * g
    v2 = ADAM_B2 * v + (1.0 - ADAM_B2) * (g * g)
    m_hat = m2 / (1.0 - ADAM_B1 ** ADAM_STEP)
    v_hat = v2 / (1.0 - ADAM_B2 ** ADAM_STEP)
    delta = -ADAM_LR * (m_hat / (jnp.sqrt(v_hat) + ADAM_EPS) + ADAM_WD * w)
    return delta, m2, v2


def _coords():
    return lax.axis_index("x"), lax.axis_index("y"), lax.axis_index("c")


def _lin(x, y, c):
    return 4 * x + 2 * y + c


def _prologue(w_in_t_b, c_pad, bl, c_ctx_row, w_mod_loc, b_mod):
    n_ex = N_DEV * bl
    n_mod = w_mod_loc.shape[1]

    def body(w_ref, c_ref, cctx_ref, wm_ref, b_ref, out_w, crows_ref, mod_out, c_gath, mod_mine,
             w_send, w_recv, c_send, c_recv, m_send, m_recv, local_sems):
        x, y, c = _coords()
        me_lin = _lin(x, y, c)
        me, sib = (x, y, c), (x, y, 1 - c)
        xnb, ynb, diag = (1 - x, y), (x, 1 - y), (1 - x, 1 - y)
        north = c == 1

        def direct_gather(src, dst, send_sems, recv_sems, local_sem):
            cps = [pltpu.make_async_copy(src, dst.at[me_lin], local_sem)]
            for k in range(1, N_DEV):
                peer = (1 - x if k & 4 else x, 1 - y if k & 2 else y, 1 - c if k & 1 else c)
                cps.append(pltpu.make_async_remote_copy(
                    src_ref=src, dst_ref=dst.at[me_lin], send_sem=send_sems.at[k - 1], recv_sem=recv_sems.at[k - 1],
                    device_id=peer, device_id_type=MESH_ID))
            for cp in cps:
                cp.start()
            return cps

        def copy(k, block, to, src=None):
            slot = out_w.at[_lin(*block)]
            return pltpu.make_async_remote_copy(
                src_ref=slot if src is None else src, dst_ref=slot, send_sem=w_send.at[k], recv_sem=w_recv.at[k],
                device_id=to, device_id_type=MESH_ID)

        c_cps = direct_gather(c_ref, c_gath, c_send, c_recv, local_sems.at[0])
        mine = pltpu.make_async_copy(w_ref, out_w.at[me_lin], local_sems.at[1])
        mine.start()
        first = [copy(0, me, sib, src=w_ref), copy(1, me, (*xnb, c), src=w_ref), copy(2, me, (*ynb, c), src=w_ref)]
        for cp in first:
            cp.start()

        for cp in c_cps[1:]:
            cp.wait_recv()
        c_cps[0].wait()
        crows_ref[...] = jnp.zeros_like(crows_ref)
        for j in range(N_DEV):
            crows_ref[j * bl:(j + 1) * bl, :] = c_gath[j, 0:bl, :]
        crows_ref[n_ex:n_ex + 1, :] = cctx_ref[...]
        cr = crows_ref[...]
        act = (cr * _sigmoid(cr)).astype(BF16)
        mod_mine[...] = _dot(act, wm_ref[...].astype(BF16)) + b_ref[:, pl.ds(pl.multiple_of(me_lin * n_mod, 128), n_mod)]
        mod_cps = direct_gather(mod_mine, mod_out, m_send, m_recv, local_sems.at[2])

        relay_north = copy(3, (*xnb, c), (*ynb, c))
        relay_south = copy(3, (*ynb, c), (*xnb, c))
        passed = []
        copy(1, (*xnb, c), me).wait_recv()
        pl.when(north)(relay_north.start)
        passed.append(copy(4, (*xnb, c), sib))
        passed[-1].start()
        copy(2, (*ynb, c), me).wait_recv()
        pl.when(jnp.logical_not(north))(relay_south.start)
        passed.append(copy(5, (*ynb, c), sib))
        passed[-1].start()
        copy(3, (*diag, c), me).wait_recv()
        passed.append(copy(6, (*diag, c), sib))
        passed[-1].start()
        copy(0, sib, me).wait_recv()
        for k, chip in ((4, xnb), (5, ynb), (6, diag)):
            copy(k, (*chip, 1 - c), me).wait_recv()
        for cp in mod_cps[1:]:
            cp.wait_recv()
        mod_cps[0].wait()
        for cp in first + passed + [relay_north] + c_cps[1:] + mod_cps[1:]:
            cp.wait_send()
        mine.wait()

    vm = pl.BlockSpec(memory_space=pltpu.VMEM)
    seven = pltpu.SemaphoreType.DMA((N_DEV - 1,))
    return pl.pallas_call(
        body, name="prologue",
        out_shape=[jax.ShapeDtypeStruct((N_DEV,) + w_in_t_b.shape, BF16), jax.ShapeDtypeStruct((n_ex + 8, D), F32),
                   jax.ShapeDtypeStruct((N_DEV, n_ex + 8, n_mod), F32)],
        in_specs=[vm] * 5, out_specs=[vm] * 3,
        scratch_shapes=[pltpu.VMEM((N_DEV,) + c_pad.shape, F32), pltpu.VMEM((n_ex + 8, n_mod), F32),
                        seven, seven, seven, seven, seven, seven, pltpu.SemaphoreType.DMA((3,))],
        compiler_params=pltpu.CompilerParams(vmem_limit_bytes=VMEM_LIMIT),
    )(w_in_t_b, c_pad, c_ctx_row, w_mod_loc, b_mod)


def _exchange_copies(in_refs, out_refs, send_sems, recv_sems, local_sems, scatter):
    x, y, c = _coords()
    me = _lin(x, y, c)
    local, remote = [], []
    for a, (src, dst) in enumerate(zip(in_refs, out_refs)):
        local.append(pltpu.make_async_copy(src.at[me] if scatter else src, dst.at[me], local_sems.at[a]))
        for k in range(1, N_DEV):
            peer = (1 - x if k & 4 else x, 1 - y if k & 2 else y, 1 - c if k & 1 else c)
            remote.append(pltpu.make_async_remote_copy(
                src_ref=src.at[_lin(*peer)] if scatter else src, dst_ref=dst.at[me],
                send_sem=send_sems.at[a * (N_DEV - 1) + k - 1], recv_sem=recv_sems.at[a * (N_DEV - 1) + k - 1],
                device_id=peer, device_id_type=MESH_ID))
    return local, remote


def _exchange_scratch(n):
    return [pltpu.SemaphoreType.DMA((n * (N_DEV - 1),)), pltpu.SemaphoreType.DMA((n * (N_DEV - 1),)),
            pltpu.SemaphoreType.DMA((n,))]


def _sum_devices_adamw(items):
    n = len(items)

    def body(*refs):
        for a in range(n):
            got, w_ref, m_ref, v_ref = refs[4 * a:4 * a + 4]
            g_ref, d_ref, nm_ref, nv_ref = refs[4 * n + 4 * a:4 * n + 4 * a + 4]
            g = got[0].astype(F32)
            for j in range(1, N_DEV):
                g = g + got[j].astype(F32)
            delta, m2, v2 = _adamw(w_ref[...], g, m_ref[...], v_ref[...])
            g_ref[...] = g
            d_ref[...] = delta
            nm_ref[...] = m2
            nv_ref[...] = v2

    args, out_shape = [], []
    for got, w, m, v in items:
        assert got.shape == (N_DEV,) + w.shape
        args += [got, w, m, v]
        out_shape += [jax.ShapeDtypeStruct(w.shape, F32)] * 4
    outs = pl.pallas_call(body, name="sum_devices_adamw", out_shape=out_shape,
                          compiler_params=pltpu.CompilerParams(vmem_limit_bytes=VMEM_LIMIT))(*args)
    return [tuple(outs[4 * a:4 * a + 4]) for a in range(n)]


def _fwd_in(x, modrows, norm_w, w_in_b, cos, sins, qnw_t, knw_t, ones_bd, k_all, v_all, shards, wire_dtypes):
    bl, s, _ = x.shape
    tm = TOKEN_PARTS * TM
    nt = s // tm
    n_sh = len(shards)

    def body(*refs):
        (x_ref, mod_ref, nw_ref, win_ref, cos_ref, sin_ref, qnw_ref, knw_ref, bd_ref, kin_ref, vin_ref) = refs[:11]
        shard_refs = refs[11:11 + n_sh]
        q_ref, k_ref, v_ref, pq_ref, pkv_ref, za_ref, glu_ref, zc_ref = refs[11 + n_sh:19 + n_sh]
        gathered_refs = refs[19 + n_sh:19 + 2 * n_sh]
        stage_refs = refs[19 + 2 * n_sh:19 + 3 * n_sh]
        send_sems, recv_sems, local_sems = refs[19 + 3 * n_sh:]
        b, i = pl.program_id(0), pl.program_id(1)
        local, remote = _exchange_copies(stage_refs, gathered_refs, send_sems, recv_sems, local_sems, scatter=False)

        @pl.when((b == 0) & (i == 0))
        def _():
            for src, stage in zip(shard_refs, stage_refs):
                stage[...] = src[...].astype(stage.dtype)
            for cp in local + remote:
                cp.start()

        shift = mod_ref[0, 0:1, :]
        scale = mod_ref[0, 1:2, :]
        for part in range(TOKEN_PARTS):
            rows = pl.ds(part * TM, TM)
            xv = x_ref[0, rows, :]
            r = lax.rsqrt(jnp.mean(xv * xv, axis=-1, keepdims=True) + EPS)
            u = (xv * r * nw_ref[...]) * (1.0 + scale) + shift
            p = _dot_nt(u.astype(BF16), win_ref[...])
            pq = p[:, 0:DA]
            pk = p[:, DA:DA + HD * 2]
            ck = cos_ref[rows, :]
            sk = sin_ref[rows, :]
            cs = jnp.concatenate([ck] * (DA // KVW), axis=-1)
            sn = jnp.concatenate([sk] * (DA // KVW), axis=-1)
            rq = lax.rsqrt(_segsum(pq * pq, bd_ref[...]) * (1.0 / HD) + EPS)
            qn = pq * rq * qnw_ref[...]
            qr = _rope(qn, cs, sn) * 0.125
            for h in range(DA // HD):
                q_ref[0, h, rows, :] = qr[:, h * HD:(h + 1) * HD].astype(BF16)
            rk = lax.rsqrt(_segsum(pk * pk, bd_ref[0:KVW, 0:KVW]) * (1.0 / HD) + EPS)
            kn = pk * rk * knw_ref[...]
            kr = _rope(kn, ck, sk)
            pv = p[:, 640:768]
            for h in range(KVW // HD):
                k_ref[0, h, rows, :] = kr[:, h * HD:(h + 1) * HD].astype(BF16)
                v_ref[0, h, rows, :] = _with_ones_column(pv[:, h * HD:(h + 1) * HD]).astype(BF16)
            pq_ref[0, rows, :] = pq
            pkv_ref[0, rows, :] = p[:, 512:768]
            za_ref[0, rows, :] = p[:, 768:1280]
            glu_ref[0, rows, :] = p[:, 1280:2304]
            zc_ref[0, rows, :] = p[:, 2304:2816]

        @pl.when((b == bl - 1) & (i == nt - 1))
        def _():
            for cp in remote:
                cp.wait_recv()
            for cp in remote:
                cp.wait_send()
            for cp in local:
                cp.wait()

    def tile(w):
        return pl.BlockSpec((1, tm, w), lambda b, i: (b, i, 0))

    def const(shape):
        return pl.BlockSpec(shape, lambda b, i: (0,) * len(shape))

    outs = [(DA, F32), (2 * KVW, F32), (DA, F32), (2 * DC, F32), (DC, F32)]
    anyspace = pl.BlockSpec(memory_space=pl.ANY)
    rope = pl.BlockSpec((tm, KVW), lambda b, i: (i, 0))
    k_tile = pl.BlockSpec((1, KVW // HD, tm, HD), lambda b, i: (b, 0, i, 0))
    v_tile = pl.BlockSpec((1, KVW // HD, tm, 2 * HD), lambda b, i: (b, 0, i, 0))
    res = pl.pallas_call(
        body, name="fwd_in", grid=(bl, nt),
        in_specs=[tile(D), pl.BlockSpec((1, 3, D), lambda b, i: (b, 0, 0)), const((1, D)), const((D_IN, D)),
                  rope, rope, const((1, DA)), const((1, KVW)), const((DA, DA)), anyspace, anyspace]
        + [const(a.shape) for a in shards],
        out_specs=[pl.BlockSpec((1, DA // HD, tm, HD), lambda b, i: (b, 0, i, 0)), k_tile, v_tile]
        + [tile(w) for w, _ in outs] + [anyspace] * n_sh,
        out_shape=[jax.ShapeDtypeStruct((bl, DA // HD, s, HD), BF16), jax.ShapeDtypeStruct(k_all.shape, BF16),
                   jax.ShapeDtypeStruct(v_all.shape, BF16)]
        + [jax.ShapeDtypeStruct((bl, s, w), dt) for w, dt in outs]
        + [jax.ShapeDtypeStruct((N_DEV,) + a.shape, dt) for a, dt in zip(shards, wire_dtypes)],
        input_output_aliases={9: 1, 10: 2},
        scratch_shapes=[pltpu.VMEM(a.shape, dt) for a, dt in zip(shards, wire_dtypes)] + _exchange_scratch(n_sh),
        compiler_params=_params(("arbitrary", "arbitrary")),
    )(x, modrows, norm_w, w_in_b, cos, sins, qnw_t, knw_t, ones_bd, k_all, v_all, *shards)
    return res[:8], res[8:]


_KV_ROWS_OF_W_IN_T = pl.BlockSpec((2 * KVW, D), lambda b: (DA // (2 * KVW), 0))


def _ctx_fwd(ctx, modc, norm_w, w_kv_b, knw_t, ones_bd, n_keys):
    bl, cl, _ = ctx.shape

    def body(x_ref, mod_ref, nw_ref, w_ref, knw_ref, bd_ref, k_ref, v_ref, pkv_ref):
        xv = x_ref[0]
        shift = mod_ref[0, 0:1, :]
        scale = mod_ref[0, 1:2, :]
        r = lax.rsqrt(jnp.mean(xv * xv, axis=-1, keepdims=True) + EPS)
        u = (xv * r * nw_ref[...]) * (1.0 + scale) + shift
        p = _dot_nt(u.astype(BF16), w_ref[...])
        pk = p[:, 0:KVW]
        rk = lax.rsqrt(_segsum(pk * pk, bd_ref[...]) * (1.0 / HD) + EPS)
        kn = pk * rk * knw_ref[...]
        pv = p[:, KVW:2 * KVW]
        for h in range(KVW // HD):
            k_ref[0, h] = kn[:, h * HD:(h + 1) * HD].astype(BF16)
            v_ref[0, h] = _with_ones_column(pv[:, h * HD:(h + 1) * HD]).astype(BF16)
        pkv_ref[0] = p

    def const(shape):
        return pl.BlockSpec(shape, lambda b: (0,) * len(shape))

    def tile(w):
        return pl.BlockSpec((1, cl, w), lambda b: (b, 0, 0))

    ctx_block = (n_keys - cl) // cl
    assert ctx_block * cl + cl == n_keys
    k_tile = pl.BlockSpec((1, KVW // HD, cl, HD), lambda b: (b, 0, ctx_block, 0))
    v_tile = pl.BlockSpec((1, KVW // HD, cl, 2 * HD), lambda b: (b, 0, ctx_block, 0))
    return pl.pallas_call(
        body, name="ctx_fwd", grid=(bl,),
        in_specs=[tile(D), const((1, 3, D)), const((1, D)), _KV_ROWS_OF_W_IN_T, const((1, KVW)), const((KVW, KVW))],
        out_specs=[k_tile, v_tile, tile(2 * KVW)],
        out_shape=[jax.ShapeDtypeStruct((bl, KVW // HD, n_keys, HD), BF16),
                   jax.ShapeDtypeStruct((bl, KVW // HD, n_keys, 2 * HD), BF16),
                   jax.ShapeDtypeStruct((bl, cl, 2 * KVW), F32)],
        compiler_params=_params(("arbitrary",)),
    )(ctx, modc, norm_w, w_kv_b, knw_t, ones_bd)


def _attn_fwd(q, k, v1):
    bl, _, s, _ = q.shape
    n_keys = k.shape[2]

    def body(q_ref, k_ref, v_ref, o_ref, lse_ref):
        kv = k_ref[0, 0]
        vv = v_ref[0, 0]
        lane = lax.broadcasted_iota(jnp.int32, (TQ, 2 * HD), 1)
        for part in range(FWD_PARTS):
            rows = pl.ds(part * TQ, TQ)
            lse = jnp.zeros((TQ, 2 * HD), F32)
            heads = []
            sc_all = _dot_nt(q_ref[0, :, rows, :].reshape(4 * TQ, HD), kv)
            for h in range(4):
                sc = sc_all[h * TQ:(h + 1) * TQ, :]
                m = jnp.max(sc, axis=-1, keepdims=True)
                e = jnp.exp(sc - m).astype(BF16)
                ov = _dot(e, vv)
                denom = ov[:, HD:HD + 1]
                heads.append(ov[:, 0:HD] * (1.0 / denom))
                lse = jnp.where(lane == h, m + jnp.log(denom), lse)
            o_ref[0, rows, :] = jnp.concatenate(heads, axis=-1)
            lse_ref[0, 0, rows, :] = lse

    tq = FWD_PARTS * TQ
    ks = pl.BlockSpec((1, 1, n_keys, HD), lambda b, g, i: (b, g, 0, 0))
    qs = pl.BlockSpec((1, 4, tq, HD), lambda b, g, i: (b, g, i, 0))
    vs = pl.BlockSpec((1, 1, n_keys, 2 * HD), lambda b, g, i: (b, g, 0, 0))
    return pl.pallas_call(
        body, name="attn_fwd", grid=(bl, 2, s // tq), in_specs=[qs, ks, vs],
        out_specs=[pl.BlockSpec((1, tq, 4 * HD), lambda b, g, i: (b, i, g)),
                   pl.BlockSpec((1, 1, tq, 2 * HD), lambda b, g, i: (b, g, i, 0))],
        out_shape=[jax.ShapeDtypeStruct((bl, s, DA), F32), jax.ShapeDtypeStruct((bl, 2, s, 2 * HD), F32)],
        compiler_params=_params(("arbitrary", "arbitrary", "arbitrary")),
    )(q, k, v1)


def _attn_bwd(q, k, v1, do, o, lse, exchange):
    bl, _, s, _ = q.shape
    n_keys = k.shape[2]
    tq = BWD_PARTS * TQ
    nq = s // tq
    n_ex = len(exchange)

    def body(*refs):
        q_ref, k_ref, v_ref, do_ref, o_ref, lse_ref = refs[:6]
        part_refs = refs[6:6 + n_ex]
        dq_ref, dk_ref, dv_ref = refs[6 + n_ex:9 + n_ex]
        got_refs = refs[9 + n_ex:9 + 2 * n_ex]
        p_sc, ds_sc, dkt, dvt, send_sems, recv_sems, local_sems = refs[9 + 2 * n_ex:]
        i = pl.program_id(2)
        first = (pl.program_id(0) == 0) & (pl.program_id(1) == 0) & (i == 0)
        last = (pl.program_id(0) == bl - 1) & (pl.program_id(1) == 1) & (i == nq - 1)
        local, remote = _exchange_copies(part_refs, got_refs, send_sems, recv_sems, local_sems, scatter=True)

        @pl.when(first)
        def _():
            for cp in local + remote:
                cp.start()

        @pl.when(i == 0)
        def _():
            dkt[...] = jnp.zeros_like(dkt)
            dvt[...] = jnp.zeros_like(dvt)

        kv = k_ref[0, 0]
        vv = v_ref[0, 0][:, 0:HD]
        for part in range(BWD_PARTS):
            tq_rows = pl.ds(part * TQ, TQ)
            lse = lse_ref[0, 0, tq_rows, :]
            ov = o_ref[0, tq_rows, :]
            dqs = []
            q_cat = q_ref[0, :, tq_rows, :].reshape(4 * TQ, HD)
            do_cat = do_ref[0, :, tq_rows, :].reshape(4 * TQ, HD)
            sc_all = _dot_nt(q_cat, kv)
            for h in range(4):
                doh = do_cat[h * TQ:(h + 1) * TQ, :]
                delta = jnp.sum(ov[:, h * HD:(h + 1) * HD] * doh.astype(F32), axis=-1, keepdims=True)
                rows = pl.ds((part * 4 + h) * TQ, TQ)
                p = jnp.exp(sc_all[h * TQ:(h + 1) * TQ, :] - lse[:, h:h + 1])
                ds = (p * (_dot_nt(doh, vv) - delta)).astype(BF16)
                p_sc[rows, :] = p.astype(BF16)
                ds_sc[rows, :] = ds
                dqs.append(_dot(ds, kv) * 0.125)
            dq_ref[0, tq_rows, :] = jnp.concatenate(dqs, axis=-1)
            part_rows = pl.ds(part * 4 * TQ, 4 * TQ)
            dvt[...] += _dot_tn(do_cat, p_sc[part_rows, :])
            dkt[...] += _dot_tn(q_cat, ds_sc[part_rows, :])

        @pl.when(i == nq - 1)
        def _():
            dk_ref[0, 0] = dkt[...].T
            dv_ref[0, 0] = dvt[...].T

        @pl.when(last)
        def _():
            for cp in remote:
                cp.wait_recv()
            for cp in remote:
                cp.wait_send()
            for cp in local:
                cp.wait()

    qs = pl.BlockSpec((1, 4, tq, HD), lambda b, g, i: (b, g, i, 0))
    ks = pl.BlockSpec((1, 1, n_keys, HD), lambda b, g, i: (b, g, 0, 0))
    vs = pl.BlockSpec((1, 1, n_keys, 2 * HD), lambda b, g, i: (b, g, 0, 0))
    os_ = pl.BlockSpec((1, tq, 4 * HD), lambda b, g, i: (b, i, g))
    kshape = jax.ShapeDtypeStruct(k.shape, F32)
    anyspace = pl.BlockSpec(memory_space=pl.ANY)
    res = pl.pallas_call(
        body, name="attn_bwd", grid=(bl, 2, nq),
        in_specs=[qs, ks, vs, qs, os_, pl.BlockSpec((1, 1, tq, 2 * HD), lambda b, g, i: (b, g, i, 0))]
        + [anyspace] * n_ex,
        out_specs=[os_, ks, ks] + [anyspace] * n_ex,
        out_shape=[jax.ShapeDtypeStruct((bl, s, DA), F32), kshape, kshape]
        + [jax.ShapeDtypeStruct(a.shape, a.dtype) for a in exchange],
        scratch_shapes=[pltpu.VMEM((4 * tq, n_keys), BF16), pltpu.VMEM((4 * tq, n_keys), BF16),
                        pltpu.VMEM((HD, n_keys), F32), pltpu.VMEM((HD, n_keys), F32)] + _exchange_scratch(n_ex),
        compiler_params=_params(("arbitrary", "arbitrary", "arbitrary")),
    )(q, k, v1, do, o, lse, *exchange)
    return res[:3], res[3:]


def _halo_specs(width, s):
    per = TC // HALO
    last = s // HALO - 1
    main = pl.BlockSpec((1, TC, width), lambda b, i: (b, i, 0))
    prev = pl.BlockSpec((1, HALO, width), lambda b, i: (b, jnp.maximum(i * per - 1, 0), 0))
    nxt = pl.BlockSpec((1, HALO, width), lambda b, i: (b, jnp.minimum((i + 1) * per, last), 0))
    return main, prev, nxt


def _glu(g):
    return g[:, 0:DC] * _sigmoid(g[:, DC:2 * DC])


def _fill_padded(pad_ref, main, prev, nxt, first, last):
    pad_ref[0:HALO, :] = jnp.where(first, 0.0, prev)
    pad_ref[HALO:HALO + TC, :] = main
    pad_ref[HALO + TC:2 * HALO + TC, :] = jnp.where(last, 0.0, nxt)


PLANE_ROWS = TC + 2 * HALO - 8


def _shift_planes(pad_ref, planes_ref):
    for r in range(1, 8):
        planes_ref[r - 1] = pad_ref[pl.ds(r, PLANE_ROWS), :]


def _tap_rows(pad_ref, planes_ref, offset, start, n):
    a, r = divmod(offset, 8)
    if r == 0:
        return pad_ref[pl.ds(start + 8 * a, n), :]
    return planes_ref[r - 1, pl.ds(start + 8 * a, n), :]


def _conv_fwd(glu, conv_w, conv_b, ln_w, ln_b, w_pw_b, b_pw):
    bl, s, _ = glu.shape
    nt = s // TC

    def body(g_ref, gp_ref, gn_ref, cw_ref, cb_ref, lw_ref, lb_ref, wpw_ref, bpw_ref, y_ref, cp_ref, pad_ref, planes_ref):
        i = pl.program_id(1)
        _fill_padded(pad_ref, _glu(g_ref[0]), _glu(gp_ref[0]), _glu(gn_ref[0]), i == 0, i == nt - 1)
        _shift_planes(pad_ref, planes_ref)
        for ck in range(TC // CH):
            acc = jnp.zeros((CH, DC), F32) + cb_ref[...]
            for t in range(KW):
                acc = acc + _tap_rows(pad_ref, planes_ref, 1 + t, ck * CH, CH) * cw_ref[t:t + 1, :]
            y_ref[0, pl.ds(ck * CH, CH), :] = acc
        y = y_ref[0]
        mu = jnp.mean(y, axis=-1, keepdims=True)
        yc = y - mu
        var = jnp.mean(yc * yc, axis=-1, keepdims=True)
        z = yc * lax.rsqrt(var + EPS) * lw_ref[...] + lb_ref[...]
        act = z * _sigmoid(z)
        cp_ref[0] = _dot(act.astype(BF16), wpw_ref[...]) + bpw_ref[...]

    def const(shape):
        return pl.BlockSpec(shape, lambda b, i: (0,) * len(shape))

    main, prev, nxt = _halo_specs(2 * DC, s)
    tile = pl.BlockSpec((1, TC, DC), lambda b, i: (b, i, 0))
    return pl.pallas_call(
        body, name="conv_fwd", grid=(bl, nt),
        in_specs=[main, prev, nxt, const((32, DC)), const((1, DC)), const((1, DC)), const((1, DC)),
                  const((DC, DC)), const((1, DC))],
        out_specs=[tile, tile],
        out_shape=[jax.ShapeDtypeStruct((bl, s, DC), F32)] * 2,
        scratch_shapes=[pltpu.VMEM((TC + 2 * HALO, DC), F32), pltpu.VMEM((7, PLANE_ROWS, DC), F32)],
        compiler_params=_params(("arbitrary", "arbitrary")),
    )(glu, glu, glu, conv_w, conv_b, ln_w, ln_b, w_pw_b, b_pw)


def _conv_bwd_depthwise(glu, dy, conv_w):
    bl, s, _ = glu.shape
    nt = s // TC

    def body(g_ref, gp_ref, gn_ref, d_ref, dp_ref, dn_ref, cw_ref, dglu_ref, dcw_ref,
             padu_ref, padd_ref, planes_u, planes_d):
        i = pl.program_id(1)

        @pl.when((pl.program_id(0) == 0) & (i == 0))
        def _():
            dcw_ref[...] = jnp.zeros_like(dcw_ref)

        first, last = i == 0, i == nt - 1
        _fill_padded(padu_ref, _glu(g_ref[0]), _glu(gp_ref[0]), _glu(gn_ref[0]), first, last)
        _fill_padded(padd_ref, d_ref[0], dp_ref[0], dn_ref[0], first, last)
        _shift_planes(padu_ref, planes_u)
        _shift_planes(padd_ref, planes_d)
        for ck in range(TC // CH):
            acc = jnp.zeros((CH, DC), F32)
            for t in range(KW):
                acc = acc + _tap_rows(padd_ref, planes_d, 2 * HALO - 1 - t, ck * CH, CH) * cw_ref[t:t + 1, :]
            g = g_ref[0, pl.ds(ck * CH, CH), :]
            a = g[:, 0:DC]
            sg = _sigmoid(g[:, DC:2 * DC])
            dglu_ref[0, pl.ds(ck * CH, CH), 0:DC] = (acc * sg).astype(BF16)
            dglu_ref[0, pl.ds(ck * CH, CH), DC:2 * DC] = (acc * a * sg * (1.0 - sg)).astype(BF16)
        group = 4
        for t0 in range(0, KW, group):
            taps = range(t0, min(t0 + group, KW))
            acc8 = [jnp.zeros((8, DC), F32) for _ in taps]
            for ck in range(TC // CH):
                dchunk = d_ref[0, pl.ds(ck * CH, CH), :]
                for n, t in enumerate(taps):
                    prod = _tap_rows(padu_ref, planes_u, 1 + t, ck * CH, CH) * dchunk
                    acc8[n] = acc8[n] + jnp.sum(prod.reshape(CH // 8, 8, DC), axis=0)
            for n, t in enumerate(taps):
                dcw_ref[t:t + 1, :] += jnp.sum(acc8[n], axis=0, keepdims=True)

    gmain, gprev, gnext = _halo_specs(2 * DC, s)
    dmain, dprev, dnext = _halo_specs(DC, s)
    cw = pl.BlockSpec((32, DC), lambda b, i: (0, 0))
    return pl.pallas_call(
        body, name="conv_bwd_depthwise", grid=(bl, nt),
        in_specs=[gmain, gprev, gnext, dmain, dprev, dnext, cw],
        out_specs=[gmain, cw],
        out_shape=[jax.ShapeDtypeStruct((bl, s, 2 * DC), BF16), jax.ShapeDtypeStruct((32, DC), F32)],
        scratch_shapes=[pltpu.VMEM((TC + 2 * HALO, DC), F32)] * 2 + [pltpu.VMEM((7, PLANE_ROWS, DC), F32)] * 2,
        compiler_params=_params(("arbitrary", "arbitrary")),
    )(glu, glu, glu, dy, dy, dy, conv_w)


def _out_fwd_bwd(attn, za, cp, zc, x, target, modrows, w_out_b, y_conv, ln_w, ln_b, w_pw_b):
    bl, s, _ = x.shape
    tm = OUT_TM

    def body(o_ref, za_ref, cp_ref, zc_ref, x_ref, t_ref, mod_ref, w_ref, y_ref, lw_ref, lb_ref, wpw_ref,
             do_ref, dza_ref, dy_ref, dzc_ref, dh_ref, dgate_ref, gwb_ref, loss_ref, gpwb_ref, rows_ref,
             gw_ref, gpw_ref):
        b, i = pl.program_id(0), pl.program_id(1)

        @pl.when((b == 0) & (i == 0))
        def _():
            gw_ref[...] = jnp.zeros_like(gw_ref)
            gpw_ref[...] = jnp.zeros_like(gpw_ref)
            rows_ref[...] = jnp.zeros_like(rows_ref)
            loss_ref[...] = jnp.zeros_like(loss_ref)

        @pl.when(i == 0)
        def _():
            dgate_ref[...] = jnp.zeros_like(dgate_ref)

        gate = mod_ref[0, 2:3, :]
        w = w_ref[...]
        o, za_v, cp_v, zc_v = o_ref[0], za_ref[0], cp_ref[0], zc_ref[0]
        sa = _sigmoid(za_v)
        sc = _sigmoid(zc_v)
        silu_a = za_v * sa
        silu_c = zc_v * sc
        mix = jnp.concatenate([(o * silu_a).astype(BF16), (cp_v * silu_c).astype(BF16)], axis=-1)
        out = _dot(mix, w)
        err = x_ref[0] + gate * out - t_ref[0]
        loss_ref[...] += jnp.sum(err * err, axis=0, keepdims=True)
        dh = err * (1.0 / D)
        dh_ref[0] = dh
        dgate_ref[0] += jnp.sum(dh * out, axis=0, keepdims=True)
        dout = (dh * gate).astype(BF16)
        gw_ref[...] += _dot_tn(mix, dout)
        dmix = _dot_nt(dout, w)
        dga = dmix[:, 0:DA]
        dgc = dmix[:, DA:DA + DC]
        dov = dga * silu_a
        for h in range(DA // HD):
            do_ref[0, h] = dov[:, h * HD:(h + 1) * HD].astype(BF16)
        dza_ref[0] = (dga * o * (sa * (1.0 + za_v * (1.0 - sa)))).astype(BF16)
        dzc_ref[0] = (dgc * cp_v * (sc * (1.0 + zc_v * (1.0 - sc)))).astype(BF16)
        dcp = dgc * silu_c
        y = y_ref[0]
        yc = y - jnp.mean(y, axis=-1, keepdims=True)
        rstd = lax.rsqrt(jnp.mean(yc * yc, axis=-1, keepdims=True) + EPS)
        yn = yc * rstd
        lw = lw_ref[...]
        z = yn * lw + lb_ref[...]
        sg = _sigmoid(z)
        dcp_b = dcp.astype(BF16)
        gpw_ref[...] += _dot_tn((z * sg).astype(BF16), dcp_b)
        dz = _dot_nt(dcp_b, wpw_ref[...]) * (sg * (1.0 + z * (1.0 - sg)))
        dyn = dz * lw
        dy = rstd * (dyn - jnp.mean(dyn, axis=-1, keepdims=True) - yn * jnp.mean(dyn * yn, axis=-1, keepdims=True))
        dy_ref[0] = dy
        rows_ref[0:1, :] += jnp.sum(dcp, axis=0, keepdims=True)
        rows_ref[1:2, :] += jnp.sum(dz * yn, axis=0, keepdims=True)
        rows_ref[2:3, :] += jnp.sum(dz, axis=0, keepdims=True)
        rows_ref[3:4, :] += jnp.sum(dy, axis=0, keepdims=True)

        @pl.when((b == bl - 1) & (i == s // tm - 1))
        def _():
            gwb_ref[...] = gw_ref[...].astype(BF16)
            gpwb_ref[...] = gpw_ref[...].astype(BF16)

    def const(shape):
        return pl.BlockSpec(shape, lambda b, i: (0,) * len(shape))

    def tile(w):
        return pl.BlockSpec((1, tm, w), lambda b, i: (b, i, 0))

    return pl.pallas_call(
        body, name="out_fwd_bwd", grid=(bl, s // tm),
        in_specs=[tile(DA), tile(DA), tile(DC), tile(DC), tile(D), tile(D),
                  pl.BlockSpec((1, 3, D), lambda b, i: (b, 0, 0)), const((D, D)),
                  tile(DC), const((1, DC)), const((1, DC)), const((DC, DC))],
        out_specs=[pl.BlockSpec((1, DA // HD, tm, HD), lambda b, i: (b, 0, i, 0)), tile(DA), tile(DC), tile(DC), tile(D),
                   pl.BlockSpec((1, 1, D), lambda b, i: (b, 0, 0)), const((D, D)), const((1, D)),
                   const((DC, DC)), const((8, DC))],
        out_shape=[jax.ShapeDtypeStruct((bl, DA // HD, s, HD), BF16), jax.ShapeDtypeStruct((bl, s, DA), BF16),
                   jax.ShapeDtypeStruct((bl, s, DC), F32), jax.ShapeDtypeStruct((bl, s, DC), BF16),
                   jax.ShapeDtypeStruct((bl, s, D), F32), jax.ShapeDtypeStruct((bl, 1, D), F32),
                   jax.ShapeDtypeStruct((D, D), BF16), jax.ShapeDtypeStruct((1, D), F32),
                   jax.ShapeDtypeStruct((DC, DC), BF16), jax.ShapeDtypeStruct((8, DC), F32)],
        scratch_shapes=[pltpu.VMEM((D, D), F32), pltpu.VMEM((DC, DC), F32)],
        compiler_params=_params(("arbitrary", "arbitrary")),
    )(attn, za, cp, zc, x, target, modrows, w_out_b, y_conv, ln_w, ln_b, w_pw_b)


def _rms_heads_bwd(dy, x, w_t, ones_bd):
    r = lax.rsqrt(_segsum(x * x, ones_bd) * (1.0 / HD) + EPS)
    xh = x * r
    g = dy * w_t
    dx = r * (g - xh * (_segsum(g * xh, ones_bd) * (1.0 / HD)))
    return dx, dy * xh


def _ctx_bwd(ctx, modc, norm_w, w_kv_b, pkv_c, dk_c, dv_c, knw_t, ones_bd):
    bl, cl, _ = ctx.shape

    def body(x_ref, mod_ref, nw_ref, w_ref, p_ref, dk_ref, dv_ref, knw_ref, bd_ref, gw_ref, rows_ref, dknw_ref):
        @pl.when(pl.program_id(0) == 0)
        def _():
            gw_ref[...] = jnp.zeros_like(gw_ref)
            rows_ref[...] = jnp.zeros_like(rows_ref)
            dknw_ref[...] = jnp.zeros_like(dknw_ref)

        xv = x_ref[0]
        shift = mod_ref[0, 0:1, :]
        scale = mod_ref[0, 1:2, :]
        nw = nw_ref[...]
        r = lax.rsqrt(jnp.mean(xv * xv, axis=-1, keepdims=True) + EPS)
        xn = xv * r
        yv = xn * nw
        u = yv * (1.0 + scale) + shift
        dkv = jnp.concatenate([dk_ref[0, 0], dk_ref[0, 1]], axis=-1)
        dpk, dknw = _rms_heads_bwd(dkv, p_ref[0][:, 0:KVW], knw_ref[...], bd_ref[...])
        dp = jnp.concatenate([dpk.astype(BF16), dv_ref[0, 0].astype(BF16), dv_ref[0, 1].astype(BF16)], axis=-1)
        gw_ref[...] += _dot_tn(dp, u.astype(BF16))
        du = _dot(dp, w_ref[...])
        rows_ref[0:1, :] += jnp.sum(du, axis=0, keepdims=True)
        rows_ref[1:2, :] += jnp.sum(du * yv, axis=0, keepdims=True)
        rows_ref[2:3, :] += jnp.sum(du * (1.0 + scale) * xn, axis=0, keepdims=True)
        dknw_ref[...] += jnp.sum(dknw, axis=0, keepdims=True)

    def const(shape):
        return pl.BlockSpec(shape, lambda b: (0,) * len(shape))

    def tile(w):
        return pl.BlockSpec((1, cl, w), lambda b: (b, 0, 0))

    ctx_block = (dk_c.shape[2] - cl) // cl
    kv_tile = pl.BlockSpec((1, KVW // HD, cl, HD), lambda b: (b, 0, ctx_block, 0))
    return pl.pallas_call(
        body, name="ctx_bwd", grid=(bl,),
        in_specs=[tile(D), const((1, 3, D)), const((1, D)), _KV_ROWS_OF_W_IN_T, tile(2 * KVW), kv_tile, kv_tile,
                  const((1, KVW)), const((KVW, KVW))],
        out_specs=[const((2 * KVW, D)), const((8, D)), const((1, KVW))],
        out_shape=[jax.ShapeDtypeStruct((2 * KVW, D), F32), jax.ShapeDtypeStruct((8, D), F32),
                   jax.ShapeDtypeStruct((1, KVW), F32)],
        compiler_params=_params(("arbitrary",)),
    )(ctx, modc, norm_w, w_kv_b, pkv_c, dk_c, dv_c, knw_t, ones_bd)


def _bwd_in(x, modrows, norm_w, w_in_b, cos, sins, qnw_t, knw_t, ones_bd,
            pq, pkv, dq, dk, dv, dza, dglu, dzc, dh, gw_kv):
    bl, s, _ = x.shape
    tm = TOKEN_PARTS * TM
    nt = s // tm

    def body(x_ref, mod_ref, nw_ref, win_hbm, cos_ref, sin_ref, qnw_ref, knw_ref, bd_ref,
             pq_ref, pkv_ref, dq_ref, dk_ref, dv_ref, dza_ref, dglu_ref, dzc_ref, dh_ref, gwkv_ref,
             gx_ref, gw_hbm, dmod_ref, dnw_ref, dqnw_ref, dknw_ref, win_ref, gw_acc, sem):
        b, i = pl.program_id(0), pl.program_id(1)

        @pl.when((b == 0) & (i == 0))
        def _():
            cp = pltpu.make_async_copy(win_hbm, win_ref, sem)
            cp.start()
            gw_acc[...] = jnp.zeros_like(gw_acc)
            dnw_ref[...] = jnp.zeros_like(dnw_ref)
            dqnw_ref[...] = jnp.zeros_like(dqnw_ref)
            dknw_ref[...] = jnp.zeros_like(dknw_ref)
            cp.wait()

        @pl.when(i == 0)
        def _():
            dmod_ref[...] = jnp.zeros_like(dmod_ref)

        bd = bd_ref[...]
        shift = mod_ref[0, 0:1, :]
        scale = mod_ref[0, 1:2, :]
        nw = nw_ref[...]
        dps, us = [], []
        for part in range(TOKEN_PARTS):
            rows = pl.ds(part * TM, TM)
            ck = cos_ref[rows, :]
            sk = sin_ref[rows, :]
            cs = jnp.concatenate([ck] * (DA // KVW), axis=-1)
            sn = jnp.concatenate([sk] * (DA // KVW), axis=-1)
            dqn = _rope_bwd(dq_ref[0, rows, :], cs, sn)
            dpq, dqnw = _rms_heads_bwd(dqn, pq_ref[0, rows, :], qnw_ref[...], bd)
            dkn = _rope_bwd(jnp.concatenate([dk_ref[0, 0, rows, :], dk_ref[0, 1, rows, :]], axis=-1), ck, sk)
            dpk, dknw = _rms_heads_bwd(dkn, pkv_ref[0, rows, 0:KVW], knw_ref[...], bd[0:KVW, 0:KVW])
            dqnw_ref[...] += jnp.sum(dqnw, axis=0, keepdims=True)
            dknw_ref[...] += jnp.sum(dknw, axis=0, keepdims=True)
            dp = jnp.concatenate(
                [dpq.astype(BF16), dpk.astype(BF16), dv_ref[0, 0, rows, :].astype(BF16), dv_ref[0, 1, rows, :].astype(BF16),
                 dza_ref[0, rows, :], dglu_ref[0, rows, :], dzc_ref[0, rows, :]], axis=-1)

            xv = x_ref[0, rows, :]
            r = lax.rsqrt(jnp.mean(xv * xv, axis=-1, keepdims=True) + EPS)
            xn = xv * r
            yv = xn * nw
            u = yv * (1.0 + scale) + shift
            dps.append(dp)
            us.append(u.astype(BF16))
            du = _dot(dp, win_ref[...])
            dmod_ref[0, 0:1, :] += jnp.sum(du, axis=0, keepdims=True)
            dmod_ref[0, 1:2, :] += jnp.sum(du * yv, axis=0, keepdims=True)
            dy = du * (1.0 + scale)
            dnw_ref[...] += jnp.sum(dy * xn, axis=0, keepdims=True)
            dxn = dy * nw
            gx_ref[0, rows, :] = dh_ref[0, rows, :] + r * (dxn - xn * jnp.mean(dxn * xn, axis=-1, keepdims=True))
        gw_acc[...] += _dot_tn(jnp.concatenate(dps, axis=0), jnp.concatenate(us, axis=0))

        @pl.when((b == bl - 1) & (i == nt - 1))
        def _():
            gw_acc[DA:DA + 2 * KVW, :] += gwkv_ref[...]

            def to_bf16(j, carry):
                rows = pl.ds(pl.multiple_of(j * 2 * KVW, 2 * KVW), 2 * KVW)
                win_ref[rows, :] = gw_acc[rows, :].astype(BF16)
                return carry

            lax.fori_loop(0, D_IN // (2 * KVW), to_bf16, 0)
            pltpu.sync_copy(win_ref, gw_hbm)

    def tile(w):
        return pl.BlockSpec((1, tm, w), lambda b, i: (b, i, 0))

    def const(shape):
        return pl.BlockSpec(shape, lambda b, i: (0,) * len(shape))

    anyspace = pl.BlockSpec(memory_space=pl.ANY)
    rope = pl.BlockSpec((tm, KVW), lambda b, i: (i, 0))
    kv_tile = pl.BlockSpec((1, KVW // HD, tm, HD), lambda b, i: (b, 0, i, 0))
    return pl.pallas_call(
        body, name="bwd_in", grid=(bl, nt),
        in_specs=[tile(D), pl.BlockSpec((1, 3, D), lambda b, i: (b, 0, 0)), const((1, D)), anyspace, rope, rope,
                  const((1, DA)), const((1, KVW)), const((DA, DA)),
                  tile(DA), tile(2 * KVW), tile(DA), kv_tile, kv_tile, tile(DA), tile(2 * DC), tile(DC), tile(D),
                  const((2 * KVW, D))],
        out_specs=[tile(D), anyspace, pl.BlockSpec((1, 2, D), lambda b, i: (b, 0, 0)), const((1, D)),
                   const((1, DA)), const((1, KVW))],
        out_shape=[jax.ShapeDtypeStruct((bl, s, D), F32), jax.ShapeDtypeStruct((D_IN, D), BF16),
                   jax.ShapeDtypeStruct((bl, 2, D), F32), jax.ShapeDtypeStruct((1, D), F32),
                   jax.ShapeDtypeStruct((1, DA), F32), jax.ShapeDtypeStruct((1, KVW), F32)],
        scratch_shapes=[pltpu.VMEM((D_IN, D), BF16), pltpu.VMEM((D_IN, D), F32), pltpu.SemaphoreType.DMA],
        compiler_params=_params(("arbitrary", "arbitrary")),
    )(x, modrows, norm_w, w_in_b, cos, sins, qnw_t, knw_t, ones_bd,
      pq, pkv, dq, dk, dv, dza, dglu, dzc, dh, gw_kv)


_LOSS, _DMODC, _NW, _QN, _KN, _CB, _LW, _LB, _BPW, SMALL_W = 0, 1024, 4096, 5120, 5248, 5376, 5888, 6400, 6912, 7424


ROW_W = 1792


def _put_flat(ref, off, value):
    n, done = value.shape[1], 0
    while done < n:
        r, c = divmod(off + done, ROW_W)
        take = min(n - done, ROW_W - c)
        ref[r:r + 1, c:c + take] = value[:, done:done + take]
        done += take


def _get_flat(arr, off, n):
    parts, done = [], 0
    while done < n:
        r, c = divmod(off + done, ROW_W)
        take = min(n - done, ROW_W - c)
        parts.append(arr[r:r + 1, c:c + take])
        done += take
    return parts[0] if len(parts) == 1 else jnp.concatenate(parts, axis=-1)


def _pack_small_body(loss_ref, ctx_ref, dnw_ref, dqnw_ref, dknw_ref, dknwc_ref, conv_ref, dss_ref, dgate_ref, o_ref):
    bl = dss_ref.shape[0]
    assert SMALL_W + bl * 3 * D <= 8 * ROW_W
    o_ref[...] = jnp.zeros_like(o_ref)
    _put_flat(o_ref, _LOSS, loss_ref[...])
    _put_flat(o_ref, _DMODC, ctx_ref[0:1, :])
    _put_flat(o_ref, _DMODC + D, ctx_ref[1:2, :])
    _put_flat(o_ref, _NW, dnw_ref[...] + ctx_ref[2:3, :])
    dq = dqnw_ref[...]
    qn = dq[:, 0:HD]
    for h in range(1, DA // HD):
        qn = qn + dq[:, h * HD:(h + 1) * HD]
    _put_flat(o_ref, _QN, qn)
    dk = dknw_ref[...] + dknwc_ref[...]
    _put_flat(o_ref, _KN, dk[:, 0:HD] + dk[:, HD:2 * HD])
    _put_flat(o_ref, _BPW, conv_ref[0:1, :])
    _put_flat(o_ref, _LW, conv_ref[1:2, :])
    _put_flat(o_ref, _LB, conv_ref[2:3, :])
    _put_flat(o_ref, _CB, conv_ref[3:4, :])
    for b in range(bl):
        _put_flat(o_ref, SMALL_W + b * 3 * D, dss_ref[b, 0:1, :])
        _put_flat(o_ref, SMALL_W + b * 3 * D + D, dss_ref[b, 1:2, :])
        _put_flat(o_ref, SMALL_W + b * 3 * D + 2 * D, dgate_ref[b])


_SMALL = (("b_mod", None), ("norm_w", _NW), ("q_norm_w", _QN), ("k_norm_w", _KN), ("conv_b", _CB),
          ("conv_ln_w", _LW), ("conv_ln_b", _LB), ("b_pw", _BPW), ("c_ctx", None))


def _epilogue(parts_in, pieces, c_rows, w_mod_loc):
    bl = pieces[7].shape[0]
    n_ex = N_DEV * bl
    n_mod = w_mod_loc.shape[1]
    rb = 32
    shp = parts_in.shape[1:]
    rows_in = shp[0]

    def body(*refs):
        it = iter(refs)
        take = lambda k: [next(it) for _ in range(k)]
        (parts,) = take(1)
        piece_refs = take(9)
        (c_ref, wm_ref) = take(2)
        (g_in, g_wm, sum_ref, gb_ref, gc_all, loss_ref) = take(6)
        (mine, got_sib, stage, got_chip, payload, gathered, dmod_full, gc_mine) = take(8)
        (d2d_send, d2d_recv, ici_send, ici_recv, local_sems, sg_send, sg_recv, gc_send, gc_recv, misc_sems) = take(10)

        x, y, c = _coords()
        me = _lin(x, y, c)
        sib = (x, y, 1 - c)
        home = 2 * x + y

        def rows_loop(fn):
            def step(i, carry):
                fn(pl.ds(pl.multiple_of(i * rb, rb), rb))
                return carry
            lax.fori_loop(0, rows_in // rb, step, 0)

        def direct_gather(src, dst, send_sems, recv_sems, local_sem):
            cps = [pltpu.make_async_copy(src, dst.at[me], local_sem)]
            for k in range(1, N_DEV):
                peer = (1 - x if k & 4 else x, 1 - y if k & 2 else y, 1 - c if k & 1 else c)
                cps.append(pltpu.make_async_remote_copy(
                    src_ref=src, dst_ref=dst.at[me], send_sem=send_sems.at[k - 1], recv_sem=recv_sems.at[k - 1],
                    device_id=peer, device_id_type=MESH_ID))
            for cp in cps:
                cp.start()
            return cps

        _pack_small_body(*piece_refs, payload)
        small_cps = direct_gather(payload, gathered, sg_send, sg_recv, misc_sems.at[0])

        local, d2d, ici = [], [], []
        for s in range(4):
            cp = pltpu.make_async_copy(parts.at[_lin(s // 2, s % 2, c)], mine.at[s], local_sems.at[s])
            cp.start()
            local.append(cp)
            rc = pltpu.make_async_remote_copy(
                src_ref=parts.at[_lin(s // 2, s % 2, 1 - c)], dst_ref=got_sib.at[s],
                send_sem=d2d_send.at[s], recv_sem=d2d_recv.at[s], device_id=sib, device_id_type=MESH_ID)
            rc.start()
            d2d.append(rc)

        for cp in small_cps[1:]:
            cp.wait_recv()
        small_cps[0].wait()
        tot = gathered[0]
        for j in range(1, N_DEV):
            tot = tot + gathered[j]
        summed = _get_flat(tot, 0, SMALL_W)
        dmod_full[...] = jnp.zeros_like(dmod_full)
        for j in range(N_DEV):
            arr = gathered[j]
            for b in range(bl):
                dmod_full[j * bl + b:j * bl + b + 1, :] = _get_flat(arr, SMALL_W + b * 3 * D, 3 * D)
        dmod_full[n_ex:n_ex + 1, :] = summed[:, _DMODC:_DMODC + 3 * D]
        sum_ref[...] = summed
        gb_ref[...] = jnp.sum(dmod_full[...], axis=0, keepdims=True)
        loss_ref[...] = (0.5 / D) * jnp.sum(summed[:, _LOSS:_LOSS + D], axis=-1, keepdims=True)

        north = c == 1
        first = (jnp.where(north, 1 - x, x), jnp.where(north, y, 1 - y))
        second = (jnp.where(north, x, 1 - x), jnp.where(north, 1 - y, y))
        for s in range(4):
            local[s].wait()
            d2d[s].wait_recv()

        def chip_sum(k, chip, relayed):
            slot = 2 * chip[0] + chip[1]

            def pair_sum(rs):
                acc = mine[slot, rs, :].astype(F32) + got_sib[slot, rs, :].astype(F32)
                if relayed:
                    acc = acc + got_chip[1, rs, :].astype(F32)
                stage[k, rs, :] = acc.astype(BF16)

            rows_loop(pair_sum)

        def send(k, to):
            rc = pltpu.make_async_remote_copy(
                src_ref=stage.at[k], dst_ref=got_chip.at[k], send_sem=ici_send.at[k], recv_sem=ici_recv.at[k],
                device_id=(to[0], to[1], c), device_id_type=MESH_ID)
            rc.start()
            ici.append(rc)

        chip_sum(0, first, False)
        send(0, first)
        chip_sum(1, (1 - x, 1 - y), False)
        send(1, first)

        cr = c_ref[...]
        act = (cr * _sigmoid(cr)).astype(BF16)
        dm = dmod_full[:, pl.ds(pl.multiple_of(me * n_mod, 128), n_mod)].astype(BF16)
        g_wm[...] = _dot_tn(act, dm)
        gc_mine[...] = _dot_nt(dm[n_ex:n_ex + 8, :], wm_ref[...].astype(BF16))
        gc_cps = direct_gather(gc_mine, gc_all, gc_send, gc_recv, misc_sems.at[1])

        ici[1].wait_recv()
        chip_sum(2, second, True)
        send(2, second)
        ici[0].wait_recv()
        ici[2].wait_recv()

        def finish(rs):
            gsum = mine[home, rs, :].astype(F32) + got_sib[home, rs, :].astype(F32)
            g_in[rs, :] = gsum + got_chip[0, rs, :].astype(F32) + got_chip[2, rs, :].astype(F32)

        rows_loop(finish)

        for cp in gc_cps[1:]:
            cp.wait_recv()
        gc_cps[0].wait()
        for rc in d2d + ici + small_cps[1:] + gc_cps[1:]:
            rc.wait_send()

    vm = pl.BlockSpec(memory_space=pltpu.VMEM)
    anyspace = pl.BlockSpec(memory_space=pl.ANY)
    assert rows_in % rb == 0 and parts_in.dtype == BF16
    args = [parts_in, *pieces, c_rows, w_mod_loc]
    in_specs = [anyspace] + [vm] * (len(args) - 1)
    out_shape = [jax.ShapeDtypeStruct(shp, F32), jax.ShapeDtypeStruct(w_mod_loc.shape, F32),
                 jax.ShapeDtypeStruct((1, SMALL_W), F32), jax.ShapeDtypeStruct((1, 3 * D), F32),
                 jax.ShapeDtypeStruct((N_DEV, 8, D), F32), jax.ShapeDtypeStruct((1, 1), F32)]
    scratch = [pltpu.VMEM((4,) + shp, BF16), pltpu.VMEM((4,) + shp, BF16), pltpu.VMEM((3,) + shp, BF16),
               pltpu.VMEM((3,) + shp, BF16), pltpu.VMEM((8, ROW_W), F32), pltpu.VMEM((N_DEV, 8, ROW_W), F32),
               pltpu.VMEM((n_ex + 8, 3 * D), F32), pltpu.VMEM((8, D), F32),
               pltpu.SemaphoreType.DMA((4,)), pltpu.SemaphoreType.DMA((4,)), pltpu.SemaphoreType.DMA((3,)),
               pltpu.SemaphoreType.DMA((3,)), pltpu.SemaphoreType.DMA((4,)),
               pltpu.SemaphoreType.DMA((N_DEV - 1,)), pltpu.SemaphoreType.DMA((N_DEV - 1,)),
               pltpu.SemaphoreType.DMA((N_DEV - 1,)), pltpu.SemaphoreType.DMA((N_DEV - 1,)),
               pltpu.SemaphoreType.DMA((2,))]
    return pl.pallas_call(
        body, name="epilogue", out_shape=out_shape, in_specs=in_specs, out_specs=[vm] * len(out_shape),
        scratch_shapes=scratch, compiler_params=pltpu.CompilerParams(vmem_limit_bytes=VMEM_LIMIT),
    )(*args)


def _final_adamw(g_in, w_in_t, m_in_t, v_in_t, g_wm, w_mod_loc, m_mod, v_mod, summed, g_bmod, gc_all,
                 small_w, small_m, small_v):
    ns = len(_SMALL)
    rb = 32

    def body(*refs):
        it = iter(refs)
        take = lambda k: [next(it) for _ in range(k)]
        (gin_ref, w_ref, m_ref, v_ref, gwm_ref, wm_ref, mm_ref, vm_ref, sum_ref, gb_ref, gc_ref) = take(11)
        sw, sm, sv = take(ns), take(ns), take(ns)
        (d_in, nm_in, nv_in, d_wm, nm_wm, nv_wm) = take(6)
        souts = take(4 * ns)

        def big(g_r, w_r, m_r, v_r, d_o, nm_o, nv_o):
            def step(i, carry):
                rs = pl.ds(pl.multiple_of(i * rb, rb), rb)
                dl, m_new, v_new = _adamw(w_r[rs, :], g_r[rs, :], m_r[rs, :], v_r[rs, :])
                d_o[rs, :] = dl
                nm_o[rs, :] = m_new
                nv_o[rs, :] = v_new
                return carry
            lax.fori_loop(0, w_r.shape[0] // rb, step, 0)

        big(gin_ref, w_ref, m_ref, v_ref, d_in, nm_in, nv_in)
        big(gwm_ref, wm_ref, mm_ref, vm_ref, d_wm, nm_wm, nv_wm)
        for k, (name, off) in enumerate(_SMALL):
            w = sw[k][...]
            if name == "b_mod":
                gk = gb_ref[...]
            elif name == "c_ctx":
                acc = gc_ref[0, 0:1, :]
                for j in range(1, N_DEV):
                    acc = acc + gc_ref[j, 0:1, :]
                sg = _sigmoid(w)
                gk = acc * (sg * (1.0 + w * (1.0 - sg)))
            else:
                gk = sum_ref[:, off:off + w.shape[1]]
            dl, m_new, v_new = _adamw(w, gk, sm[k][...], sv[k][...])
            souts[k][...] = gk
            souts[ns + k][...] = dl
            souts[2 * ns + k][...] = m_new
            souts[3 * ns + k][...] = v_new

    assert w_in_t.shape[0] % rb == 0 and w_mod_loc.shape[0] % rb == 0
    big_shape = jax.ShapeDtypeStruct(w_in_t.shape, F32)
    mod_shape = jax.ShapeDtypeStruct(w_mod_loc.shape, F32)
    out_shape = [big_shape] * 3 + [mod_shape] * 3 + [jax.ShapeDtypeStruct(w.shape, F32) for w in small_w] * 4
    outs = pl.pallas_call(
        body, name="final_adamw", out_shape=out_shape,
        compiler_params=pltpu.CompilerParams(vmem_limit_bytes=VMEM_LIMIT),
    )(g_in, w_in_t, m_in_t, v_in_t, g_wm, w_mod_loc, m_mod, v_mod, summed, g_bmod, gc_all, *small_w, *small_m, *small_v)
    small_outs = [outs[6 + k * ns:6 + (k + 1) * ns] for k in range(4)]
    return (g_in,) + tuple(outs[0:3]), (g_wm,) + tuple(outs[3:6]), small_outs


def _rope_tables(s):
    t = jnp.arange(s, dtype=jnp.int32)
    row = (t // GRID_W).astype(F32)
    col = (t % GRID_W).astype(F32)
    freqs = ROPE_THETA ** (-jnp.arange(0, HD // 2, 2, dtype=F32) / (HD // 2))
    ang_r = row[:, None] * freqs[None, :]
    ang_c = col[:, None] * freqs[None, :]
    cr, sr, cc, sc = jnp.cos(ang_r), jnp.sin(ang_r), jnp.cos(ang_c), jnp.sin(ang_c)
    cos = jnp.concatenate([cr, cr, cc, cc], axis=-1)
    sins = jnp.concatenate([-sr, sr, -sc, sc], axis=-1)
    return jnp.tile(cos, (1, KVW // HD)), jnp.tile(sins, (1, KVW // HD))


def kernel(x, c, ctx, c_ctx, w_mod, b_mod, norm_w, w_in, q_norm_w, k_norm_w, conv_w, conv_b, conv_ln_w, conv_ln_b, w_pw, b_pw, w_out, loss_target, m_c_ctx, m_w_mod, m_b_mod, m_norm_w, m_w_in, m_q_norm_w, m_k_norm_w, m_conv_w, m_conv_b, m_conv_ln_w, m_conv_ln_b, m_w_pw, m_b_pw, m_w_out, v_c_ctx, v_w_mod, v_b_mod, v_norm_w, v_w_in, v_q_norm_w, v_k_norm_w, v_conv_w, v_conv_b, v_conv_ln_w, v_conv_ln_b, v_w_pw, v_b_pw, v_w_out):
    bl, s, _ = x.shape
    cl = ctx.shape[1]
    me = _lin(*_coords())

    conv_w_pad = jnp.pad(conv_w[0], ((0, 32 - KW), (0, 0)))
    c_pad = jnp.pad(c, ((0, 8 - bl), (0, 0)))
    n_ex = N_DEV * bl
    g_win, c_rows, g_mod = _prologue(w_in[0].T.astype(BF16), c_pad, bl, c_ctx[None, :], w_mod[0], b_mod)
    w_in_b = g_win.reshape(D_IN, D)
    mod_all = g_mod.transpose(1, 0, 2).reshape(n_ex + 8, 3 * D)
    modrows = lax.dynamic_slice_in_dim(mod_all, me * bl, bl, axis=0).reshape(bl, 3, D)
    modc = mod_all[n_ex].reshape(1, 3, D)

    cos, sins = _rope_tables(s)
    qnw_t = jnp.tile(q_norm_w, (1, DA // HD))
    knw_t = jnp.tile(k_norm_w, (1, KVW // HD))
    lane = jnp.arange(DA, dtype=jnp.int32) // HD
    ones_bd = (lane[:, None] == lane[None, :]).astype(BF16)
    ones_kv = ones_bd[0:KVW, 0:KVW]
    w_kv_b = w_in_b

    k_ctx, v_ctx, pkv_c = _ctx_fwd(ctx, modc, norm_w, w_kv_b, knw_t, ones_kv, cl + s)
    (q_h, k_h, v_h, pq, pkv, za, glu, zc), (g_wout, g_wpw, g_cw) = _fwd_in(
        x, modrows, norm_w, w_in_b, cos, sins, qnw_t, knw_t, ones_bd, k_ctx, v_ctx,
        [w_out[0], w_pw[0], conv_w_pad], [BF16, BF16, F32])
    w_out_b = g_wout.reshape(D, D)
    w_pw_b = g_wpw.reshape(DC, DC)
    conv_w_full = g_cw.transpose(1, 0, 2).reshape(32, DC)
    attn, lse = _attn_fwd(q_h, k_h, v_h)
    y_conv, cp = _conv_fwd(glu, conv_w_full, conv_b, conv_ln_w, conv_ln_b, w_pw_b, b_pw)

    do_h, dza, dy_conv, dzc, dh, dgate, gw_out, loss_row, gw_pw, conv_rows = _out_fwd_bwd(
        attn, za, cp, zc, x, loss_target, modrows, w_out_b, y_conv, conv_ln_w, conv_ln_b, w_pw_b)
    dglu, g_cw_full = _conv_bwd_depthwise(glu, dy_conv, conv_w_full)
    parts_out = gw_out.reshape(N_DEV, D // N_DEV, D)
    parts_pw = gw_pw.reshape(N_DEV, DC // N_DEV, DC)
    parts_cw = g_cw_full.astype(BF16).reshape(32, N_DEV, DC // N_DEV).transpose(1, 0, 2)
    (dq, dk_h, dv_h), (got_out, got_pw, got_cw) = _attn_bwd(
        q_h, k_h, v_h, do_h, attn, lse, [parts_out, parts_pw, parts_cw])
    gw_kv, ctx_rows, dknw_c = _ctx_bwd(ctx, modc, norm_w, w_kv_b, pkv_c, dk_h, dv_h, knw_t, ones_kv)
    grad_x, gw_in, dmod_ss, dnw, dqnw, dknw = _bwd_in(
        x, modrows, norm_w, w_in_b, cos, sins, qnw_t, knw_t, ones_bd,
        pq, pkv, dq, dk_h, dv_h, dza, dglu, dzc, dh, gw_kv)

    def pad_cw(a):
        return jnp.pad(a[0], ((0, 32 - KW), (0, 0)))

    r_out, r_pw, r_cw = _sum_devices_adamw(
        [(got_out, w_out[0], m_w_out[0], v_w_out[0]), (got_pw, w_pw[0], m_w_pw[0], v_w_pw[0]),
         (got_cw, pad_cw(conv_w), pad_cw(m_conv_w), pad_cw(v_conv_w))])
    r_cw = tuple(a[:KW] for a in r_cw)

    given = {"c_ctx": (c_ctx, m_c_ctx, v_c_ctx), "b_mod": (b_mod, m_b_mod, v_b_mod), "norm_w": (norm_w, m_norm_w, v_norm_w),
             "q_norm_w": (q_norm_w, m_q_norm_w, v_q_norm_w), "k_norm_w": (k_norm_w, m_k_norm_w, v_k_norm_w),
             "conv_b": (conv_b, m_conv_b, v_conv_b), "conv_ln_w": (conv_ln_w, m_conv_ln_w, v_conv_ln_w),
             "conv_ln_b": (conv_ln_b, m_conv_ln_b, v_conv_ln_b), "b_pw": (b_pw, m_b_pw, v_b_pw)}
    as_rows = [[given[name][which].reshape(1, -1) for name, _ in _SMALL] for which in range(3)]
    g_in_t, g_wmod, summed, g_bmod, gc_all, loss11 = _epilogue(
        gw_in.reshape(N_DEV, D_IN // N_DEV, D),
        [loss_row, ctx_rows, dnw, dqnw, dknw, dknw_c, conv_rows, dmod_ss, dgate], c_rows, w_mod[0])
    r_in, r_wmod, small_outs = _final_adamw(
        g_in_t, w_in[0].T, m_w_in[0].T, v_w_in[0].T, g_wmod, w_mod[0], m_w_mod[0], v_w_mod[0],
        summed, g_bmod, gc_all, *as_rows)
    r_in = tuple(a.T for a in r_in)

    big = {"w_mod": r_wmod, "w_in": r_in, "conv_w": r_cw, "w_pw": r_pw, "w_out": r_out}
    order = ["c_ctx", "w_mod", "b_mod", "norm_w", "w_in", "q_norm_w", "k_norm_w", "conv_w", "conv_b", "conv_ln_w",
             "conv_ln_b", "w_pw", "b_pw", "w_out"]
    small_index = {name: k for k, (name, _) in enumerate(_SMALL)}
    outs = [loss11.reshape(()), grad_x]
    for which in range(4):
        for name in order:
            if name in big:
                outs.append(big[name][which][None])
            else:
                outs.append(small_outs[which][small_index[name]].reshape(given[name][0].shape))
    return tuple(outs)
```

```python
import jax
import jax.numpy as jnp
from jax import lax
from jax.experimental import pallas as pl
from jax.experimental.pallas import tpu as pltpu

F32, BF16 = jnp.float32, jnp.bfloat16
MESH_ID = pl.DeviceIdType.MESH

N_DEV = 8
D = 1024
D_IN = 2816
DA = 512
DC = 512
HD = 64
KVW = 128
KW = 31
HALO = 16
EPS = 1e-6
ROPE_THETA = 10000.0
GRID_W = 64

ADAM_LR, ADAM_B1, ADAM_B2, ADAM_EPS, ADAM_WD, ADAM_STEP = 0.001, 0.9, 0.999, 1e-08, 0.01, 10

VMEM_LIMIT = 56 * 1024 * 1024

TM = 256
TQ = 128
TOKEN_PARTS = 2
OUT_TM = 512
BWD_PARTS = 2
FWD_PARTS = 4
TC = 512
CH = 64


def _params(sem, vmem=VMEM_LIMIT):
    return pltpu.CompilerParams(dimension_semantics=sem, vmem_limit_bytes=vmem)


def _dot(a, b):
    return jnp.dot(a, b, preferred_element_type=F32)


def _dot_nt(a, b):
    return lax.dot_general(a, b, (((1,), (1,)), ((), ())), preferred_element_type=F32)


def _dot_tn(a, b):
    return lax.dot_general(a, b, (((0,), (0,)), ((), ())), preferred_element_type=F32)


def _sigmoid(z):
    return 1.0 / (1.0 + jnp.exp(-z))


def _segsum(v, ones_bd):
    return _dot(v.astype(BF16), ones_bd)


def _swap16(x):
    w = x.shape[-1]
    lane = lax.broadcasted_iota(jnp.int32, x.shape, 1)
    return jnp.where((lane % 32) < 16, pltpu.roll(x, w - 16, 1), pltpu.roll(x, 16, 1))


def _with_ones_column(v):
    one = (lax.broadcasted_iota(jnp.int32, v.shape, 1) == 0).astype(v.dtype)
    return jnp.concatenate([v, one], axis=-1)


def _rope(x, cos, sins):
    return x * cos + _swap16(x) * sins


def _rope_bwd(d, cos, sins):
    return d * cos + _swap16(d * sins)


def _adamw(w, g, m, v):
    m2 = ADAM_B1 * m + (1.0 - ADAM_B1) * g
    v2 = ADAM_B2 * v + (1.0 - ADAM_B2) * (g * g)
    m_hat = m2 / (1.0 - ADAM_B1 ** ADAM_STEP)
    v_hat = v2 / (1.0 - ADAM_B2 ** ADAM_STEP)
    delta = -ADAM_LR * (m_hat / (jnp.sqrt(v_hat) + ADAM_EPS) + ADAM_WD * w)
    return delta, m2, v2


def _coords():
    return lax.axis_index("x"), lax.axis_index("y"), lax.axis_index("c")


def _lin(x, y, c):
    return 4 * x + 2 * y + c


def _prologue(w_in_t, c, c_ctx_row, w_mod_loc, b_mod):
    bl = c.shape[0]
    n_ex = N_DEV * bl
    n_mod = w_mod_loc.shape[1]

    def body(w32_ref, c_in_ref, cctx_ref, wm_ref, b_ref, out_w, crows_ref, mod_out, w_ref, c_ref, c_gath, mod_mine,
             w_send, w_recv, c_send, c_recv, m_send, m_recv, local_sems):
        x, y, c = _coords()
        me_lin = _lin(x, y, c)
        c_ref[...] = jnp.zeros_like(c_ref)
        c_ref[0:bl, :] = c_in_ref[...]
        w_ref[...] = w32_ref[...].astype(BF16)
        me, sib = (x, y, c), (x, y, 1 - c)
        xnb, ynb, diag = (1 - x, y), (x, 1 - y), (1 - x, 1 - y)
        north = c == 1

        def direct_gather(src, dst, send_sems, recv_sems, local_sem):
            cps = [pltpu.make_async_copy(src, dst.at[me_lin], local_sem)]
            for k in range(1, N_DEV):
                peer = (1 - x if k & 4 else x, 1 - y if k & 2 else y, 1 - c if k & 1 else c)
                cps.append(pltpu.make_async_remote_copy(
                    src_ref=src, dst_ref=dst.at[me_lin], send_sem=send_sems.at[k - 1], recv_sem=recv_sems.at[k - 1],
                    device_id=peer, device_id_type=MESH_ID))
            for cp in cps:
                cp.start()
            return cps

        def copy(k, block, to, src=None):
            slot = out_w.at[_lin(*block)]
            return pltpu.make_async_remote_copy(
                src_ref=slot if src is None else src, dst_ref=slot, send_sem=w_send.at[k], recv_sem=w_recv.at[k],
                device_id=to, device_id_type=MESH_ID)

        c_cps = direct_gather(c_ref, c_gath, c_send, c_recv, local_sems.at[0])
        mine = pltpu.make_async_copy(w_ref, out_w.at[me_lin], local_sems.at[1])
        mine.start()
        first = [copy(0, me, sib, src=w_ref), copy(1, me, (*xnb, c), src=w_ref), copy(2, me, (*ynb, c), src=w_ref)]
        for cp in first:
            cp.start()

        for cp in c_cps[1:]:
            cp.wait_recv()
        c_cps[0].wait()
        crows_ref[...] = jnp.zeros_like(crows_ref)
        for j in range(N_DEV):
            crows_ref[j * bl:(j + 1) * bl, :] = c_gath[j, 0:bl, :]
        crows_ref[n_ex:n_ex + 1, :] = cctx_ref[...]
        cr = crows_ref[...]
        act = (cr * _sigmoid(cr)).astype(BF16)
        mod_mine[...] = _dot(act, wm_ref[...].astype(BF16)) + b_ref[:, pl.ds(pl.multiple_of(me_lin * n_mod, 128), n_mod)]
        mod_cps = direct_gather(mod_mine, mod_out, m_send, m_recv, local_sems.at[2])

        relay_north = copy(3, (*xnb, c), (*ynb, c))
        relay_south = copy(3, (*ynb, c), (*xnb, c))
        passed = []
        copy(1, (*xnb, c), me).wait_recv()
        pl.when(north)(relay_north.start)
        passed.append(copy(4, (*xnb, c), sib))
        passed[-1].start()
        copy(2, (*ynb, c), me).wait_recv()
        pl.when(jnp.logical_not(north))(relay_south.start)
        passed.append(copy(5, (*ynb, c), sib))
        passed[-1].start()
        copy(3, (*diag, c), me).wait_recv()
        passed.append(copy(6, (*diag, c), sib))
        passed[-1].start()
        copy(0, sib, me).wait_recv()
        for k, chip in ((4, xnb), (5, ynb), (6, diag)):
            copy(k, (*chip, 1 - c), me).wait_recv()
        for cp in mod_cps[1:]:
            cp.wait_recv()
        mod_cps[0].wait()
        for cp in first + passed + [relay_north] + c_cps[1:] + mod_cps[1:]:
            cp.wait_send()
        mine.wait()

    vm = pl.BlockSpec(memory_space=pltpu.VMEM)
    seven = pltpu.SemaphoreType.DMA((N_DEV - 1,))
    return pl.pallas_call(
        body, name="prologue",
        out_shape=[jax.ShapeDtypeStruct((N_DEV,) + w_in_t.shape, BF16), jax.ShapeDtypeStruct((n_ex + 8, D), F32),
                   jax.ShapeDtypeStruct((N_DEV, n_ex + 8, n_mod), F32)],
        in_specs=[vm] * 5, out_specs=[vm] * 3,
        scratch_shapes=[pltpu.VMEM(w_in_t.shape, BF16), pltpu.VMEM((8, D), F32), pltpu.VMEM((N_DEV, 8, D), F32),
                        pltpu.VMEM((n_ex + 8, n_mod), F32),
                        seven, seven, seven, seven, seven, seven, pltpu.SemaphoreType.DMA((3,))],
        compiler_params=pltpu.CompilerParams(vmem_limit_bytes=VMEM_LIMIT),
    )(w_in_t, c, c_ctx_row, w_mod_loc, b_mod)


def _exchange_copies(in_refs, out_refs, send_sems, recv_sems, local_sems, scatter):
    x, y, c = _coords()
    me = _lin(x, y, c)
    local, remote = [], []
    for a, (src, dst) in enumerate(zip(in_refs, out_refs)):
        local.append(pltpu.make_async_copy(src.at[me] if scatter else src, dst.at[me], local_sems.at[a]))
        for k in range(1, N_DEV):
            peer = (1 - x if k & 4 else x, 1 - y if k & 2 else y, 1 - c if k & 1 else c)
            remote.append(pltpu.make_async_remote_copy(
                src_ref=src.at[_lin(*peer)] if scatter else src, dst_ref=dst.at[me],
                send_sem=send_sems.at[a * (N_DEV - 1) + k - 1], recv_sem=recv_sems.at[a * (N_DEV - 1) + k - 1],
                device_id=peer, device_id_type=MESH_ID))
    return local, remote


def _exchange_scratch(n):
    return [pltpu.SemaphoreType.DMA((n * (N_DEV - 1),)), pltpu.SemaphoreType.DMA((n * (N_DEV - 1),)),
            pltpu.SemaphoreType.DMA((n,))]


def _sum_devices_adamw(items):
    n = len(items)

    def body(*refs):
        for a in range(n):
            got, w_ref, m_ref, v_ref = refs[4 * a:4 * a + 4]
            g_ref, d_ref, nm_ref, nv_ref = refs[4 * n + 4 * a:4 * n + 4 * a + 4]
            g = got[0].astype(F32)
            for j in range(1, N_DEV):
                g = g + got[j].astype(F32)
            g = g[0:w_ref.shape[0], :]
            delta, m2, v2 = _adamw(w_ref[...], g, m_ref[...], v_ref[...])
            g_ref[...] = g
            d_ref[...] = delta
            nm_ref[...] = m2
            nv_ref[...] = v2

    args, out_shape = [], []
    for got, w, m, v in items:
        assert got.shape[0] == N_DEV and got.shape[1] >= w.shape[0] and got.shape[2:] == w.shape[1:]
        args += [got, w, m, v]
        out_shape += [jax.ShapeDtypeStruct(w.shape, F32)] * 4
    outs = pl.pallas_call(body, name="sum_devices_adamw", out_shape=out_shape,
                          compiler_params=pltpu.CompilerParams(vmem_limit_bytes=VMEM_LIMIT))(*args)
    return [tuple(outs[4 * a:4 * a + 4]) for a in range(n)]


def _fwd_in(x, modrows, norm_w, w_in_b, cos, sins, qnw_t, knw_t, ones_bd, k_all, v_all, shards, wire_dtypes):
    bl, s, _ = x.shape
    tm = TOKEN_PARTS * TM
    nt = s // tm
    n_sh = len(shards)

    def body(*refs):
        (x_ref, mod_ref, nw_ref, win_ref, cos_ref, sin_ref, qnw_ref, knw_ref, bd_ref, kin_ref, vin_ref) = refs[:11]
        shard_refs = refs[11:11 + n_sh]
        q_ref, k_ref, v_ref, pq_ref, pkv_ref, za_ref, glu_ref, zc_ref = refs[11 + n_sh:19 + n_sh]
        gathered_refs = refs[19 + n_sh:19 + 2 * n_sh]
        stage_refs = refs[19 + 2 * n_sh:19 + 3 * n_sh]
        send_sems, recv_sems, local_sems = refs[19 + 3 * n_sh:]
        b, i = pl.program_id(0), pl.program_id(1)
        local, remote = _exchange_copies(stage_refs, gathered_refs, send_sems, recv_sems, local_sems, scatter=False)

        @pl.when((b == 0) & (i == 0))
        def _():
            for src, stage in zip(shard_refs, stage_refs):
                stage[...] = src[...].astype(stage.dtype)
            for cp in local + remote:
                cp.start()

        shift = mod_ref[0, 0:1, :]
        scale = mod_ref[0, 1:2, :]
        for part in range(TOKEN_PARTS):
            rows = pl.ds(part * TM, TM)
            xv = x_ref[0, rows, :]
            r = lax.rsqrt(jnp.mean(xv * xv, axis=-1, keepdims=True) + EPS)
            u = (xv * r * nw_ref[...]) * (1.0 + scale) + shift
            p = _dot_nt(u.astype(BF16), win_ref[...])
            pq = p[:, 0:DA]
            pk = p[:, DA:DA + HD * 2]
            ck = cos_ref[rows, :]
            sk = sin_ref[rows, :]
            cs = jnp.concatenate([ck] * (DA // KVW), axis=-1)
            sn = jnp.concatenate([sk] * (DA // KVW), axis=-1)
            rq = lax.rsqrt(_segsum(pq * pq, bd_ref[...]) * (1.0 / HD) + EPS)
            qn = pq * rq * qnw_ref[...]
            qr = _rope(qn, cs, sn) * 0.125
            for h in range(DA // HD):
                q_ref[0, h, rows, :] = qr[:, h * HD:(h + 1) * HD].astype(BF16)
            rk = lax.rsqrt(_segsum(pk * pk, bd_ref[0:KVW, 0:KVW]) * (1.0 / HD) + EPS)
            kn = pk * rk * knw_ref[...]
            kr = _rope(kn, ck, sk)
            pv = p[:, 640:768]
            for h in range(KVW // HD):
                k_ref[0, h, rows, :] = kr[:, h * HD:(h + 1) * HD].astype(BF16)
                v_ref[0, h, rows, :] = _with_ones_column(pv[:, h * HD:(h + 1) * HD]).astype(BF16)
            pq_ref[0, rows, :] = pq
            pkv_ref[0, rows, :] = p[:, 512:768]
            za_ref[0, rows, :] = p[:, 768:1280]
            glu_ref[0, rows, :] = p[:, 1280:2304]
            zc_ref[0, rows, :] = p[:, 2304:2816]

        @pl.when((b == bl - 1) & (i == nt - 1))
        def _():
            for cp in remote:
                cp.wait_recv()
            for cp in remote:
                cp.wait_send()
            for cp in local:
                cp.wait()

    def tile(w):
        return pl.BlockSpec((1, tm, w), lambda b, i: (b, i, 0))

    def const(shape):
        return pl.BlockSpec(shape, lambda b, i: (0,) * len(shape))

    outs = [(DA, F32), (2 * KVW, F32), (DA, F32), (2 * DC, F32), (DC, F32)]
    anyspace = pl.BlockSpec(memory_space=pl.ANY)
    rope = pl.BlockSpec((tm, KVW), lambda b, i: (i, 0))
    k_tile = pl.BlockSpec((1, KVW // HD, tm, HD), lambda b, i: (b, 0, i, 0))
    v_tile = pl.BlockSpec((1, KVW // HD, tm, 2 * HD), lambda b, i: (b, 0, i, 0))
    res = pl.pallas_call(
        body, name="fwd_in", grid=(bl, nt),
        in_specs=[tile(D), pl.BlockSpec((1, 3, D), lambda b, i: (b, 0, 0)), const((1, D)), const((D_IN, D)),
                  rope, rope, const((1, DA)), const((1, KVW)), const((DA, DA)), anyspace, anyspace]
        + [const(a.shape) for a in shards],
        out_specs=[pl.BlockSpec((1, DA // HD, tm, HD), lambda b, i: (b, 0, i, 0)), k_tile, v_tile]
        + [tile(w) for w, _ in outs] + [anyspace] * n_sh,
        out_shape=[jax.ShapeDtypeStruct((bl, DA // HD, s, HD), BF16), jax.ShapeDtypeStruct(k_all.shape, BF16),
                   jax.ShapeDtypeStruct(v_all.shape, BF16)]
        + [jax.ShapeDtypeStruct((bl, s, w), dt) for w, dt in outs]
        + [jax.ShapeDtypeStruct((N_DEV,) + a.shape, dt) for a, dt in zip(shards, wire_dtypes)],
        input_output_aliases={9: 1, 10: 2},
        scratch_shapes=[pltpu.VMEM(a.shape, dt) for a, dt in zip(shards, wire_dtypes)] + _exchange_scratch(n_sh),
        compiler_params=_params(("arbitrary", "arbitrary")),
    )(x, modrows, norm_w, w_in_b, cos, sins, qnw_t, knw_t, ones_bd, k_all, v_all, *shards)
    return res[:8], res[8:]


_KV_ROWS_OF_W_IN_T = pl.BlockSpec((2 * KVW, D), lambda b: (DA // (2 * KVW), 0))


def _ctx_fwd(ctx, modc, norm_w, w_kv_b, knw_t, ones_bd, n_keys):
    bl, cl, _ = ctx.shape

    def body(x_ref, mod_ref, nw_ref, w_ref, knw_ref, bd_ref, k_ref, v_ref, pkv_ref):
        xv = x_ref[0]
        shift = mod_ref[0, 0:1, :]
        scale = mod_ref[0, 1:2, :]
        r = lax.rsqrt(jnp.mean(xv * xv, axis=-1, keepdims=True) + EPS)
        u = (xv * r * nw_ref[...]) * (1.0 + scale) + shift
        p = _dot_nt(u.astype(BF16), w_ref[...])
        pk = p[:, 0:KVW]
        rk = lax.rsqrt(_segsum(pk * pk, bd_ref[...]) * (1.0 / HD) + EPS)
        kn = pk * rk * knw_ref[...]
        pv = p[:, KVW:2 * KVW]
        for h in range(KVW // HD):
            k_ref[0, h] = kn[:, h * HD:(h + 1) * HD].astype(BF16)
            v_ref[0, h] = _with_ones_column(pv[:, h * HD:(h + 1) * HD]).astype(BF16)
        pkv_ref[0] = p

    def const(shape):
        return pl.BlockSpec(shape, lambda b: (0,) * len(shape))

    def tile(w):
        return pl.BlockSpec((1, cl, w), lambda b: (b, 0, 0))

    ctx_block = (n_keys - cl) // cl
    assert ctx_block * cl + cl == n_keys
    k_tile = pl.BlockSpec((1, KVW // HD, cl, HD), lambda b: (b, 0, ctx_block, 0))
    v_tile = pl.BlockSpec((1, KVW // HD, cl, 2 * HD), lambda b: (b, 0, ctx_block, 0))
    return pl.pallas_call(
        body, name="ctx_fwd", grid=(bl,),
        in_specs=[tile(D), const((1, 3, D)), const((1, D)), _KV_ROWS_OF_W_IN_T, const((1, KVW)), const((KVW, KVW))],
        out_specs=[k_tile, v_tile, tile(2 * KVW)],
        out_shape=[jax.ShapeDtypeStruct((bl, KVW // HD, n_keys, HD), BF16),
                   jax.ShapeDtypeStruct((bl, KVW // HD, n_keys, 2 * HD), BF16),
                   jax.ShapeDtypeStruct((bl, cl, 2 * KVW), F32)],
        compiler_params=_params(("arbitrary",)),
    )(ctx, modc, norm_w, w_kv_b, knw_t, ones_bd)


def _attn_fwd(q, k, v1):
    bl, _, s, _ = q.shape
    n_keys = k.shape[2]

    def body(q_ref, k_ref, v_ref, o_ref, lse_ref):
        kv = k_ref[0, 0]
        vv = v_ref[0, 0]
        lane = lax.broadcasted_iota(jnp.int32, (TQ, 2 * HD), 1)
        for part in range(FWD_PARTS):
            rows = pl.ds(part * TQ, TQ)
            lse = jnp.zeros((TQ, 2 * HD), F32)
            heads = []
            sc_all = _dot_nt(q_ref[0, :, rows, :].reshape(4 * TQ, HD), kv)
            for h in range(4):
                sc = sc_all[h * TQ:(h + 1) * TQ, :]
                m = jnp.max(sc, axis=-1, keepdims=True)
                e = jnp.exp(sc - m).astype(BF16)
                ov = _dot(e, vv)
                denom = ov[:, HD:HD + 1]
                heads.append(ov[:, 0:HD] * (1.0 / denom))
                lse = jnp.where(lane == h, m + jnp.log(denom), lse)
            o_ref[0, rows, :] = jnp.concatenate(heads, axis=-1)
            lse_ref[0, 0, rows, :] = lse

    tq = FWD_PARTS * TQ
    ks = pl.BlockSpec((1, 1, n_keys, HD), lambda b, g, i: (b, g, 0, 0))
    qs = pl.BlockSpec((1, 4, tq, HD), lambda b, g, i: (b, g, i, 0))
    vs = pl.BlockSpec((1, 1, n_keys, 2 * HD), lambda b, g, i: (b, g, 0, 0))
    return pl.pallas_call(
        body, name="attn_fwd", grid=(bl, 2, s // tq), in_specs=[qs, ks, vs],
        out_specs=[pl.BlockSpec((1, tq, 4 * HD), lambda b, g, i: (b, i, g)),
                   pl.BlockSpec((1, 1, tq, 2 * HD), lambda b, g, i: (b, g, i, 0))],
        out_shape=[jax.ShapeDtypeStruct((bl, s, DA), F32), jax.ShapeDtypeStruct((bl, 2, s, 2 * HD), F32)],
        compiler_params=_params(("arbitrary", "arbitrary", "arbitrary")),
    )(q, k, v1)


def _attn_bwd(q, k, v1, do, o, lse, exchange):
    bl, _, s, _ = q.shape
    n_keys = k.shape[2]
    tq = BWD_PARTS * TQ
    nq = s // tq
    n_ex = len(exchange)

    def body(*refs):
        q_ref, k_ref, v_ref, do_ref, o_ref, lse_ref = refs[:6]
        part_refs = refs[6:6 + n_ex]
        dq_ref, dk_ref, dv_ref = refs[6 + n_ex:9 + n_ex]
        got_refs = refs[9 + n_ex:9 + 2 * n_ex]
        p_sc, ds_sc, dkt, dvt, send_sems, recv_sems, local_sems = refs[9 + 2 * n_ex:]
        i = pl.program_id(2)
        first = (pl.program_id(0) == 0) & (pl.program_id(1) == 0) & (i == 0)
        last = (pl.program_id(0) == bl - 1) & (pl.program_id(1) == 1) & (i == nq - 1)
        local, remote = _exchange_copies(part_refs, got_refs, send_sems, recv_sems, local_sems, scatter=True)

        @pl.when(first)
        def _():
            for cp in local + remote:
                cp.start()

        @pl.when(i == 0)
        def _():
            dkt[...] = jnp.zeros_like(dkt)
            dvt[...] = jnp.zeros_like(dvt)

        kv = k_ref[0, 0]
        vv = v_ref[0, 0][:, 0:HD]
        for part in range(BWD_PARTS):
            tq_rows = pl.ds(part * TQ, TQ)
            lse = lse_ref[0, 0, tq_rows, :]
            ov = o_ref[0, tq_rows, :]
            dqs = []
            q_cat = q_ref[0, :, tq_rows, :].reshape(4 * TQ, HD)
            do_cat = do_ref[0, :, tq_rows, :].reshape(4 * TQ, HD)
            sc_all = _dot_nt(q_cat, kv)
            for h in range(4):
                doh = do_cat[h * TQ:(h + 1) * TQ, :]
                delta = jnp.sum(ov[:, h * HD:(h + 1) * HD] * doh.astype(F32), axis=-1, keepdims=True)
                rows = pl.ds((part * 4 + h) * TQ, TQ)
                p = jnp.exp(sc_all[h * TQ:(h + 1) * TQ, :] - lse[:, h:h + 1])
                ds = (p * (_dot_nt(doh, vv) - delta)).astype(BF16)
                p_sc[rows, :] = p.astype(BF16)
                ds_sc[rows, :] = ds
                dqs.append(_dot(ds, kv) * 0.125)
            dq_ref[0, tq_rows, :] = jnp.concatenate(dqs, axis=-1)
            part_rows = pl.ds(part * 4 * TQ, 4 * TQ)
            dvt[...] += _dot_tn(do_cat, p_sc[part_rows, :])
            dkt[...] += _dot_tn(q_cat, ds_sc[part_rows, :])

        @pl.when(i == nq - 1)
        def _():
            dk_ref[0, 0] = dkt[...].T
            dv_ref[0, 0] = dvt[...].T

        @pl.when(last)
        def _():
            for cp in remote:
                cp.wait_recv()
            for cp in remote:
                cp.wait_send()
            for cp in local:
                cp.wait()

    qs = pl.BlockSpec((1, 4, tq, HD), lambda b, g, i: (b, g, i, 0))
    ks = pl.BlockSpec((1, 1, n_keys, HD), lambda b, g, i: (b, g, 0, 0))
    vs = pl.BlockSpec((1, 1, n_keys, 2 * HD), lambda b, g, i: (b, g, 0, 0))
    os_ = pl.BlockSpec((1, tq, 4 * HD), lambda b, g, i: (b, i, g))
    kshape = jax.ShapeDtypeStruct(k.shape, F32)
    anyspace = pl.BlockSpec(memory_space=pl.ANY)
    res = pl.pallas_call(
        body, name="attn_bwd", grid=(bl, 2, nq),
        in_specs=[qs, ks, vs, qs, os_, pl.BlockSpec((1, 1, tq, 2 * HD), lambda b, g, i: (b, g, i, 0))]
        + [anyspace] * n_ex,
        out_specs=[os_, ks, ks] + [anyspace] * n_ex,
        out_shape=[jax.ShapeDtypeStruct((bl, s, DA), F32), kshape, kshape]
        + [jax.ShapeDtypeStruct(a.shape, a.dtype) for a in exchange],
        scratch_shapes=[pltpu.VMEM((4 * tq, n_keys), BF16), pltpu.VMEM((4 * tq, n_keys), BF16),
                        pltpu.VMEM((HD, n_keys), F32), pltpu.VMEM((HD, n_keys), F32)] + _exchange_scratch(n_ex),
        compiler_params=_params(("arbitrary", "arbitrary", "arbitrary")),
    )(q, k, v1, do, o, lse, *exchange)
    return res[:3], res[3:]


def _halo_specs(width, s):
    per = TC // HALO
    last = s // HALO - 1
    main = pl.BlockSpec((1, TC, width), lambda b, i: (b, i, 0))
    prev = pl.BlockSpec((1, HALO, width), lambda b, i: (b, jnp.maximum(i * per - 1, 0), 0))
    nxt = pl.BlockSpec((1, HALO, width), lambda b, i: (b, jnp.minimum((i + 1) * per, last), 0))
    return main, prev, nxt


def _glu(g):
    return g[:, 0:DC] * _sigmoid(g[:, DC:2 * DC])


def _fill_padded(pad_ref, main, prev, nxt, first, last):
    pad_ref[0:HALO, :] = jnp.where(first, 0.0, prev)
    pad_ref[HALO:HALO + TC, :] = main
    pad_ref[HALO + TC:2 * HALO + TC, :] = jnp.where(last, 0.0, nxt)


PLANE_ROWS = TC + 2 * HALO - 8


def _shift_planes(pad_ref, planes_ref):
    for r in range(1, 8):
        planes_ref[r - 1] = pad_ref[pl.ds(r, PLANE_ROWS), :]


def _tap_rows(pad_ref, planes_ref, offset, start, n):
    a, r = divmod(offset, 8)
    if r == 0:
        return pad_ref[pl.ds(start + 8 * a, n), :]
    return planes_ref[r - 1, pl.ds(start + 8 * a, n), :]


def _conv_fwd(glu, conv_w, conv_b, ln_w, ln_b, w_pw_b, b_pw):
    bl, s, _ = glu.shape
    nt = s // TC

    def body(g_ref, gp_ref, gn_ref, cw_ref, cb_ref, lw_ref, lb_ref, wpw_ref, bpw_ref, y_ref, cp_ref, pad_ref, planes_ref):
        i = pl.program_id(1)
        _fill_padded(pad_ref, _glu(g_ref[0]), _glu(gp_ref[0]), _glu(gn_ref[0]), i == 0, i == nt - 1)
        _shift_planes(pad_ref, planes_ref)
        for ck in range(TC // CH):
            acc = jnp.zeros((CH, DC), F32) + cb_ref[...]
            for t in range(KW):
                acc = acc + _tap_rows(pad_ref, planes_ref, 1 + t, ck * CH, CH) * cw_ref[t:t + 1, :]
            y_ref[0, pl.ds(ck * CH, CH), :] = acc
        y = y_ref[0]
        mu = jnp.mean(y, axis=-1, keepdims=True)
        yc = y - mu
        var = jnp.mean(yc * yc, axis=-1, keepdims=True)
        z = yc * lax.rsqrt(var + EPS) * lw_ref[...] + lb_ref[...]
        act = z * _sigmoid(z)
        cp_ref[0] = _dot(act.astype(BF16), wpw_ref[...]) + bpw_ref[...]

    def const(shape):
        return pl.BlockSpec(shape, lambda b, i: (0,) * len(shape))

    main, prev, nxt = _halo_specs(2 * DC, s)
    tile = pl.BlockSpec((1, TC, DC), lambda b, i: (b, i, 0))
    return pl.pallas_call(
        body, name="conv_fwd", grid=(bl, nt),
        in_specs=[main, prev, nxt, const((32, DC)), const((1, DC)), const((1, DC)), const((1, DC)),
                  const((DC, DC)), const((1, DC))],
        out_specs=[tile, tile],
        out_shape=[jax.ShapeDtypeStruct((bl, s, DC), F32)] * 2,
        scratch_shapes=[pltpu.VMEM((TC + 2 * HALO, DC), F32), pltpu.VMEM((7, PLANE_ROWS, DC), F32)],
        compiler_params=_params(("arbitrary", "arbitrary")),
    )(glu, glu, glu, conv_w, conv_b, ln_w, ln_b, w_pw_b, b_pw)


def _conv_bwd_depthwise(glu, dy, conv_w):
    bl, s, _ = glu.shape
    nt = s // TC

    def body(g_ref, gp_ref, gn_ref, d_ref, dp_ref, dn_ref, cw_ref, dglu_ref, dcw_ref,
             padu_ref, padd_ref, planes_u, planes_d):
        i = pl.program_id(1)

        @pl.when((pl.program_id(0) == 0) & (i == 0))
        def _():
            dcw_ref[...] = jnp.zeros_like(dcw_ref)

        first, last = i == 0, i == nt - 1
        _fill_padded(padu_ref, _glu(g_ref[0]), _glu(gp_ref[0]), _glu(gn_ref[0]), first, last)
        _fill_padded(padd_ref, d_ref[0], dp_ref[0], dn_ref[0], first, last)
        _shift_planes(padu_ref, planes_u)
        _shift_planes(padd_ref, planes_d)
        for ck in range(TC // CH):
            acc = jnp.zeros((CH, DC), F32)
            for t in range(KW):
                acc = acc + _tap_rows(padd_ref, planes_d, 2 * HALO - 1 - t, ck * CH, CH) * cw_ref[t:t + 1, :]
            g = g_ref[0, pl.ds(ck * CH, CH), :]
            a = g[:, 0:DC]
            sg = _sigmoid(g[:, DC:2 * DC])
            dglu_ref[0, pl.ds(ck * CH, CH), 0:DC] = (acc * sg).astype(BF16)
            dglu_ref[0, pl.ds(ck * CH, CH), DC:2 * DC] = (acc * a * sg * (1.0 - sg)).astype(BF16)
        group = 4
        for t0 in range(0, KW, group):
            taps = range(t0, min(t0 + group, KW))
            acc8 = [jnp.zeros((8, DC), F32) for _ in taps]
            for ck in range(TC // CH):
                dchunk = d_ref[0, pl.ds(ck * CH, CH), :]
                for n, t in enumerate(taps):
                    prod = _tap_rows(padu_ref, planes_u, 1 + t, ck * CH, CH) * dchunk
                    acc8[n] = acc8[n] + jnp.sum(prod.reshape(CH // 8, 8, DC), axis=0)
            for n, t in enumerate(taps):
                dcw_ref[t:t + 1, :] += jnp.sum(acc8[n], axis=0, keepdims=True)

    gmain, gprev, gnext = _halo_specs(2 * DC, s)
    dmain, dprev, dnext = _halo_specs(DC, s)
    cw = pl.BlockSpec((32, DC), lambda b, i: (0, 0))
    return pl.pallas_call(
        body, name="conv_bwd_depthwise", grid=(bl, nt),
        in_specs=[gmain, gprev, gnext, dmain, dprev, dnext, cw],
        out_specs=[gmain, cw],
        out_shape=[jax.ShapeDtypeStruct((bl, s, 2 * DC), BF16), jax.ShapeDtypeStruct((32, DC), F32)],
        scratch_shapes=[pltpu.VMEM((TC + 2 * HALO, DC), F32)] * 2 + [pltpu.VMEM((7, PLANE_ROWS, DC), F32)] * 2,
        compiler_params=_params(("arbitrary", "arbitrary")),
    )(glu, glu, glu, dy, dy, dy, conv_w)


def _out_fwd_bwd(attn, za, cp, zc, x, target, modrows, w_out_b, y_conv, ln_w, ln_b, w_pw_b):
    bl, s, _ = x.shape
    tm = OUT_TM

    def body(o_ref, za_ref, cp_ref, zc_ref, x_ref, t_ref, mod_ref, w_ref, y_ref, lw_ref, lb_ref, wpw_ref,
             do_ref, dza_ref, dy_ref, dzc_ref, dh_ref, dgate_ref, gwb_ref, loss_ref, gpwb_ref, rows_ref,
             gw_ref, gpw_ref):
        b, i = pl.program_id(0), pl.program_id(1)

        @pl.when((b == 0) & (i == 0))
        def _():
            gw_ref[...] = jnp.zeros_like(gw_ref)
            gpw_ref[...] = jnp.zeros_like(gpw_ref)
            rows_ref[...] = jnp.zeros_like(rows_ref)
            loss_ref[...] = jnp.zeros_like(loss_ref)

        @pl.when(i == 0)
        def _():
            dgate_ref[...] = jnp.zeros_like(dgate_ref)

        gate = mod_ref[0, 2:3, :]
        w = w_ref[...]
        o, za_v, cp_v, zc_v = o_ref[0], za_ref[0], cp_ref[0], zc_ref[0]
        sa = _sigmoid(za_v)
        sc = _sigmoid(zc_v)
        silu_a = za_v * sa
        silu_c = zc_v * sc
        mix = jnp.concatenate([(o * silu_a).astype(BF16), (cp_v * silu_c).astype(BF16)], axis=-1)
        out = _dot(mix, w)
        err = x_ref[0] + gate * out - t_ref[0]
        loss_ref[...] += jnp.sum(err * err, axis=0, keepdims=True)
        dh = err * (1.0 / D)
        dh_ref[0] = dh
        dgate_ref[0] += jnp.sum(dh * out, axis=0, keepdims=True)
        dout = (dh * gate).astype(BF16)
        gw_ref[...] += _dot_tn(mix, dout)
        dmix = _dot_nt(dout, w)
        dga = dmix[:, 0:DA]
        dgc = dmix[:, DA:DA + DC]
        dov = dga * silu_a
        for h in range(DA // HD):
            do_ref[0, h] = dov[:, h * HD:(h + 1) * HD].astype(BF16)
        dza_ref[0] = (dga * o * (sa * (1.0 + za_v * (1.0 - sa)))).astype(BF16)
        dzc_ref[0] = (dgc * cp_v * (sc * (1.0 + zc_v * (1.0 - sc)))).astype(BF16)
        dcp = dgc * silu_c
        y = y_ref[0]
        yc = y - jnp.mean(y, axis=-1, keepdims=True)
        rstd = lax.rsqrt(jnp.mean(yc * yc, axis=-1, keepdims=True) + EPS)
        yn = yc * rstd
        lw = lw_ref[...]
        z = yn * lw + lb_ref[...]
        sg = _sigmoid(z)
        dcp_b = dcp.astype(BF16)
        gpw_ref[...] += _dot_tn((z * sg).astype(BF16), dcp_b)
        dz = _dot_nt(dcp_b, wpw_ref[...]) * (sg * (1.0 + z * (1.0 - sg)))
        dyn = dz * lw
        dy = rstd * (dyn - jnp.mean(dyn, axis=-1, keepdims=True) - yn * jnp.mean(dyn * yn, axis=-1, keepdims=True))
        dy_ref[0] = dy
        rows_ref[0:1, :] += jnp.sum(dcp, axis=0, keepdims=True)
        rows_ref[1:2, :] += jnp.sum(dz * yn, axis=0, keepdims=True)
        rows_ref[2:3, :] += jnp.sum(dz, axis=0, keepdims=True)
        rows_ref[3:4, :] += jnp.sum(dy, axis=0, keepdims=True)

        @pl.when((b == bl - 1) & (i == s // tm - 1))
        def _():
            gwb_ref[...] = gw_ref[...].astype(BF16)
            gpwb_ref[...] = gpw_ref[...].astype(BF16)

    def const(shape):
        return pl.BlockSpec(shape, lambda b, i: (0,) * len(shape))

    def tile(w):
        return pl.BlockSpec((1, tm, w), lambda b, i: (b, i, 0))

    return pl.pallas_call(
        body, name="out_fwd_bwd", grid=(bl, s // tm),
        in_specs=[tile(DA), tile(DA), tile(DC), tile(DC), tile(D), tile(D),
                  pl.BlockSpec((1, 3, D), lambda b, i: (b, 0, 0)), const((D, D)),
                  tile(DC), const((1, DC)), const((1, DC)), const((DC, DC))],
        out_specs=[pl.BlockSpec((1, DA // HD, tm, HD), lambda b, i: (b, 0, i, 0)), tile(DA), tile(DC), tile(DC), tile(D),
                   pl.BlockSpec((1, 1, D), lambda b, i: (b, 0, 0)), const((D, D)), const((1, D)),
                   const((DC, DC)), const((8, DC))],
        out_shape=[jax.ShapeDtypeStruct((bl, DA // HD, s, HD), BF16), jax.ShapeDtypeStruct((bl, s, DA), BF16),
                   jax.ShapeDtypeStruct((bl, s, DC), F32), jax.ShapeDtypeStruct((bl, s, DC), BF16),
                   jax.ShapeDtypeStruct((bl, s, D), F32), jax.ShapeDtypeStruct((bl, 1, D), F32),
                   jax.ShapeDtypeStruct((D, D), BF16), jax.ShapeDtypeStruct((1, D), F32),
                   jax.ShapeDtypeStruct((DC, DC), BF16), jax.ShapeDtypeStruct((8, DC), F32)],
        scratch_shapes=[pltpu.VMEM((D, D), F32), pltpu.VMEM((DC, DC), F32)],
        compiler_params=_params(("arbitrary", "arbitrary")),
    )(attn, za, cp, zc, x, target, modrows, w_out_b, y_conv, ln_w, ln_b, w_pw_b)


def _rms_heads_bwd(dy, x, w_t, ones_bd):
    r = lax.rsqrt(_segsum(x * x, ones_bd) * (1.0 / HD) + EPS)
    xh = x * r
    g = dy * w_t
    dx = r * (g - xh * (_segsum(g * xh, ones_bd) * (1.0 / HD)))
    return dx, dy * xh


def _ctx_bwd(ctx, modc, norm_w, w_kv_b, pkv_c, dk_c, dv_c, knw_t, ones_bd):
    bl, cl, _ = ctx.shape

    def body(x_ref, mod_ref, nw_ref, w_ref, p_ref, dk_ref, dv_ref, knw_ref, bd_ref, gw_ref, rows_ref, dknw_ref):
        @pl.when(pl.program_id(0) == 0)
        def _():
            gw_ref[...] = jnp.zeros_like(gw_ref)
            rows_ref[...] = jnp.zeros_like(rows_ref)
            dknw_ref[...] = jnp.zeros_like(dknw_ref)

        xv = x_ref[0]
        shift = mod_ref[0, 0:1, :]
        scale = mod_ref[0, 1:2, :]
        nw = nw_ref[...]
        r = lax.rsqrt(jnp.mean(xv * xv, axis=-1, keepdims=True) + EPS)
        xn = xv * r
        yv = xn * nw
        u = yv * (1.0 + scale) + shift
        dkv = jnp.concatenate([dk_ref[0, 0], dk_ref[0, 1]], axis=-1)
        dpk, dknw = _rms_heads_bwd(dkv, p_ref[0][:, 0:KVW], knw_ref[...], bd_ref[...])
        dp = jnp.concatenate([dpk.astype(BF16), dv_ref[0, 0].astype(BF16), dv_ref[0, 1].astype(BF16)], axis=-1)
        gw_ref[...] += _dot_tn(dp, u.astype(BF16))
        du = _dot(dp, w_ref[...])
        rows_ref[0:1, :] += jnp.sum(du, axis=0, keepdims=True)
        rows_ref[1:2, :] += jnp.sum(du * yv, axis=0, keepdims=True)
        rows_ref[2:3, :] += jnp.sum(du * (1.0 + scale) * xn, axis=0, keepdims=True)
        dknw_ref[...] += jnp.sum(dknw, axis=0, keepdims=True)

    def const(shape):
        return pl.BlockSpec(shape, lambda b: (0,) * len(shape))

    def tile(w):
        return pl.BlockSpec((1, cl, w), lambda b: (b, 0, 0))

    ctx_block = (dk_c.shape[2] - cl) // cl
    kv_tile = pl.BlockSpec((1, KVW // HD, cl, HD), lambda b: (b, 0, ctx_block, 0))
    return pl.pallas_call(
        body, name="ctx_bwd", grid=(bl,),
        in_specs=[tile(D), const((1, 3, D)), const((1, D)), _KV_ROWS_OF_W_IN_T, tile(2 * KVW), kv_tile, kv_tile,
                  const((1, KVW)), const((KVW, KVW))],
        out_specs=[const((2 * KVW, D)), const((8, D)), const((1, KVW))],
        out_shape=[jax.ShapeDtypeStruct((2 * KVW, D), F32), jax.ShapeDtypeStruct((8, D), F32),
                   jax.ShapeDtypeStruct((1, KVW), F32)],
        compiler_params=_params(("arbitrary",)),
    )(ctx, modc, norm_w, w_kv_b, pkv_c, dk_c, dv_c, knw_t, ones_bd)


def _bwd_in(x, modrows, norm_w, w_in_b, cos, sins, qnw_t, knw_t, ones_bd,
            pq, pkv, dq, dk, dv, dza, dglu, dzc, dh, gw_kv):
    bl, s, _ = x.shape
    tm = TOKEN_PARTS * TM
    nt = s // tm

    def body(x_ref, mod_ref, nw_ref, win_hbm, cos_ref, sin_ref, qnw_ref, knw_ref, bd_ref,
             pq_ref, pkv_ref, dq_ref, dk_ref, dv_ref, dza_ref, dglu_ref, dzc_ref, dh_ref, gwkv_ref,
             gx_ref, gw_hbm, dmod_ref, dnw_ref, dqnw_ref, dknw_ref, win_ref, gw_acc, sem):
        b, i = pl.program_id(0), pl.program_id(1)

        @pl.when((b == 0) & (i == 0))
        def _():
            cp = pltpu.make_async_copy(win_hbm, win_ref, sem)
            cp.start()
            gw_acc[...] = jnp.zeros_like(gw_acc)
            dnw_ref[...] = jnp.zeros_like(dnw_ref)
            dqnw_ref[...] = jnp.zeros_like(dqnw_ref)
            dknw_ref[...] = jnp.zeros_like(dknw_ref)
            cp.wait()

        @pl.when(i == 0)
        def _():
            dmod_ref[...] = jnp.zeros_like(dmod_ref)

        bd = bd_ref[...]
        shift = mod_ref[0, 0:1, :]
        scale = mod_ref[0, 1:2, :]
        nw = nw_ref[...]
        dps, us = [], []
        for part in range(TOKEN_PARTS):
            rows = pl.ds(part * TM, TM)
            ck = cos_ref[rows, :]
            sk = sin_ref[rows, :]
            cs = jnp.concatenate([ck] * (DA // KVW), axis=-1)
            sn = jnp.concatenate([sk] * (DA // KVW), axis=-1)
            dqn = _rope_bwd(dq_ref[0, rows, :], cs, sn)
            dpq, dqnw = _rms_heads_bwd(dqn, pq_ref[0, rows, :], qnw_ref[...], bd)
            dkn = _rope_bwd(jnp.concatenate([dk_ref[0, 0, rows, :], dk_ref[0, 1, rows, :]], axis=-1), ck, sk)
            dpk, dknw = _rms_heads_bwd(dkn, pkv_ref[0, rows, 0:KVW], knw_ref[...], bd[0:KVW, 0:KVW])
            dqnw_ref[...] += jnp.sum(dqnw, axis=0, keepdims=True)
            dknw_ref[...] += jnp.sum(dknw, axis=0, keepdims=True)
            dp = jnp.concatenate(
                [dpq.astype(BF16), dpk.astype(BF16), dv_ref[0, 0, rows, :].astype(BF16), dv_ref[0, 1, rows, :].astype(BF16),
                 dza_ref[0, rows, :], dglu_ref[0, rows, :], dzc_ref[0, rows, :]], axis=-1)

            xv = x_ref[0, rows, :]
            r = lax.rsqrt(jnp.mean(xv * xv, axis=-1, keepdims=True) + EPS)
            xn = xv * r
            yv = xn * nw
            u = yv * (1.0 + scale) + shift
            dps.append(dp)
            us.append(u.astype(BF16))
            du = _dot(dp, win_ref[...])
            dmod_ref[0, 0:1, :] += jnp.sum(du, axis=0, keepdims=True)
            dmod_ref[0, 1:2, :] += jnp.sum(du * yv, axis=0, keepdims=True)
            dy = du * (1.0 + scale)
            dnw_ref[...] += jnp.sum(dy * xn, axis=0, keepdims=True)
            dxn = dy * nw
            gx_ref[0, rows, :] = dh_ref[0, rows, :] + r * (dxn - xn * jnp.mean(dxn * xn, axis=-1, keepdims=True))
        gw_acc[...] += _dot_tn(jnp.concatenate(dps, axis=0), jnp.concatenate(us, axis=0))

        @pl.when((b == bl - 1) & (i == nt - 1))
        def _():
            gw_acc[DA:DA + 2 * KVW, :] += gwkv_ref[...]

            def to_bf16(j, carry):
                rows = pl.ds(pl.multiple_of(j * 2 * KVW, 2 * KVW), 2 * KVW)
                win_ref[rows, :] = gw_acc[rows, :].astype(BF16)
                return carry

            lax.fori_loop(0, D_IN // (2 * KVW), to_bf16, 0)
            pltpu.sync_copy(win_ref, gw_hbm)

    def tile(w):
        return pl.BlockSpec((1, tm, w), lambda b, i: (b, i, 0))

    def const(shape):
        return pl.BlockSpec(shape, lambda b, i: (0,) * len(shape))

    anyspace = pl.BlockSpec(memory_space=pl.ANY)
    rope = pl.BlockSpec((tm, KVW), lambda b, i: (i, 0))
    kv_tile = pl.BlockSpec((1, KVW // HD, tm, HD), lambda b, i: (b, 0, i, 0))
    return pl.pallas_call(
        body, name="bwd_in", grid=(bl, nt),
        in_specs=[tile(D), pl.BlockSpec((1, 3, D), lambda b, i: (b, 0, 0)), const((1, D)), anyspace, rope, rope,
                  const((1, DA)), const((1, KVW)), const((DA, DA)),
                  tile(DA), tile(2 * KVW), tile(DA), kv_tile, kv_tile, tile(DA), tile(2 * DC), tile(DC), tile(D),
                  const((2 * KVW, D))],
        out_specs=[tile(D), anyspace, pl.BlockSpec((1, 2, D), lambda b, i: (b, 0, 0)), const((1, D)),
                   const((1, DA)), const((1, KVW))],
        out_shape=[jax.ShapeDtypeStruct((bl, s, D), F32), jax.ShapeDtypeStruct((D_IN, D), BF16),
                   jax.ShapeDtypeStruct((bl, 2, D), F32), jax.ShapeDtypeStruct((1, D), F32),
                   jax.ShapeDtypeStruct((1, DA), F32), jax.ShapeDtypeStruct((1, KVW), F32)],
        scratch_shapes=[pltpu.VMEM((D_IN, D), BF16), pltpu.VMEM((D_IN, D), F32), pltpu.SemaphoreType.DMA],
        compiler_params=_params(("arbitrary", "arbitrary")),
    )(x, modrows, norm_w, w_in_b, cos, sins, qnw_t, knw_t, ones_bd,
      pq, pkv, dq, dk, dv, dza, dglu, dzc, dh, gw_kv)


_LOSS, _DMODC, _NW, _QN, _KN, _CB, _LW, _LB, _BPW, SMALL_W = 0, 1024, 4096, 5120, 5248, 5376, 5888, 6400, 6912, 7424


ROW_W = 1792


def _put_flat(ref, off, value):
    n, done = value.shape[1], 0
    while done < n:
        r, c = divmod(off + done, ROW_W)
        take = min(n - done, ROW_W - c)
        ref[r:r + 1, c:c + take] = value[:, done:done + take]
        done += take


def _get_flat(arr, off, n):
    parts, done = [], 0
    while done < n:
        r, c = divmod(off + done, ROW_W)
        take = min(n - done, ROW_W - c)
        parts.append(arr[r:r + 1, c:c + take])
        done += take
    return parts[0] if len(parts) == 1 else jnp.concatenate(parts, axis=-1)


def _pack_small_body(loss_ref, ctx_ref, dnw_ref, dqnw_ref, dknw_ref, dknwc_ref, conv_ref, dss_ref, dgate_ref, o_ref):
    bl = dss_ref.shape[0]
    assert SMALL_W + bl * 3 * D <= 8 * ROW_W
    o_ref[...] = jnp.zeros_like(o_ref)
    _put_flat(o_ref, _LOSS, loss_ref[...])
    _put_flat(o_ref, _DMODC, ctx_ref[0:1, :])
    _put_flat(o_ref, _DMODC + D, ctx_ref[1:2, :])
    _put_flat(o_ref, _NW, dnw_ref[...] + ctx_ref[2:3, :])
    dq = dqnw_ref[...]
    qn = dq[:, 0:HD]
    for h in range(1, DA // HD):
        qn = qn + dq[:, h * HD:(h + 1) * HD]
    _put_flat(o_ref, _QN, qn)
    dk = dknw_ref[...] + dknwc_ref[...]
    _put_flat(o_ref, _KN, dk[:, 0:HD] + dk[:, HD:2 * HD])
    _put_flat(o_ref, _BPW, conv_ref[0:1, :])
    _put_flat(o_ref, _LW, conv_ref[1:2, :])
    _put_flat(o_ref, _LB, conv_ref[2:3, :])
    _put_flat(o_ref, _CB, conv_ref[3:4, :])
    for b in range(bl):
        _put_flat(o_ref, SMALL_W + b * 3 * D, dss_ref[b, 0:1, :])
        _put_flat(o_ref, SMALL_W + b * 3 * D + D, dss_ref[b, 1:2, :])
        _put_flat(o_ref, SMALL_W + b * 3 * D + 2 * D, dgate_ref[b])


_SMALL = (("b_mod", None), ("norm_w", _NW), ("q_norm_w", _QN), ("k_norm_w", _KN), ("conv_b", _CB),
          ("conv_ln_w", _LW), ("conv_ln_b", _LB), ("b_pw", _BPW), ("c_ctx", None))


def _epilogue(parts_in, pieces, c_rows, w_mod_loc):
    bl = pieces[7].shape[0]
    n_ex = N_DEV * bl
    n_mod = w_mod_loc.shape[1]
    rb = 32
    shp = parts_in.shape[1:]
    rows_in = shp[0]

    def body(*refs):
        it = iter(refs)
        take = lambda k: [next(it) for _ in range(k)]
        (parts,) = take(1)
        piece_refs = take(9)
        (c_ref, wm_ref) = take(2)
        (g_in, g_wm, sum_ref, gb_ref, gc_all, loss_ref) = take(6)
        (mine, got_sib, stage, got_chip, payload, gathered, dmod_full, gc_mine) = take(8)
        (d2d_send, d2d_recv, ici_send, ici_recv, local_sems, sg_send, sg_recv, gc_send, gc_recv, misc_sems) = take(10)

        x, y, c = _coords()
        me = _lin(x, y, c)
        sib = (x, y, 1 - c)
        home = 2 * x + y

        def rows_loop(fn):
            def step(i, carry):
                fn(pl.ds(pl.multiple_of(i * rb, rb), rb))
                return carry
            lax.fori_loop(0, rows_in // rb, step, 0)

        def direct_gather(src, dst, send_sems, recv_sems, local_sem):
            cps = [pltpu.make_async_copy(src, dst.at[me], local_sem)]
            for k in range(1, N_DEV):
                peer = (1 - x if k & 4 else x, 1 - y if k & 2 else y, 1 - c if k & 1 else c)
                cps.append(pltpu.make_async_remote_copy(
                    src_ref=src, dst_ref=dst.at[me], send_sem=send_sems.at[k - 1], recv_sem=recv_sems.at[k - 1],
                    device_id=peer, device_id_type=MESH_ID))
            for cp in cps:
                cp.start()
            return cps

        _pack_small_body(*piece_refs, payload)
        small_cps = direct_gather(payload, gathered, sg_send, sg_recv, misc_sems.at[0])

        local, d2d, ici = [], [], []
        for s in range(4):
            cp = pltpu.make_async_copy(parts.at[_lin(s // 2, s % 2, c)], mine.at[s], local_sems.at[s])
            cp.start()
            local.append(cp)
            rc = pltpu.make_async_remote_copy(
                src_ref=parts.at[_lin(s // 2, s % 2, 1 - c)], dst_ref=got_sib.at[s],
                send_sem=d2d_send.at[s], recv_sem=d2d_recv.at[s], device_id=sib, device_id_type=MESH_ID)
            rc.start()
            d2d.append(rc)

        for cp in small_cps[1:]:
            cp.wait_recv()
        small_cps[0].wait()
        tot = gathered[0]
        for j in range(1, N_DEV):
            tot = tot + gathered[j]
        summed = _get_flat(tot, 0, SMALL_W)
        dmod_full[...] = jnp.zeros_like(dmod_full)
        for j in range(N_DEV):
            arr = gathered[j]
            for b in range(bl):
                dmod_full[j * bl + b:j * bl + b + 1, :] = _get_flat(arr, SMALL_W + b * 3 * D, 3 * D)
        dmod_full[n_ex:n_ex + 1, :] = summed[:, _DMODC:_DMODC + 3 * D]
        sum_ref[...] = summed
        gb_ref[...] = jnp.sum(dmod_full[...], axis=0, keepdims=True)
        loss_ref[...] = (0.5 / D) * jnp.sum(summed[:, _LOSS:_LOSS + D], axis=-1, keepdims=True)

        north = c == 1
        first = (jnp.where(north, 1 - x, x), jnp.where(north, y, 1 - y))
        second = (jnp.where(north, x, 1 - x), jnp.where(north, 1 - y, y))
        for s in range(4):
            local[s].wait()
            d2d[s].wait_recv()

        def chip_sum(k, chip, relayed):
            slot = 2 * chip[0] + chip[1]

            def pair_sum(rs):
                acc = mine[slot, rs, :].astype(F32) + got_sib[slot, rs, :].astype(F32)
                if relayed:
                    acc = acc + got_chip[1, rs, :].astype(F32)
                stage[k, rs, :] = acc.astype(BF16)

            rows_loop(pair_sum)

        def send(k, to):
            rc = pltpu.make_async_remote_copy(
                src_ref=stage.at[k], dst_ref=got_chip.at[k], send_sem=ici_send.at[k], recv_sem=ici_recv.at[k],
                device_id=(to[0], to[1], c), device_id_type=MESH_ID)
            rc.start()
            ici.append(rc)

        chip_sum(0, first, False)
        send(0, first)
        chip_sum(1, (1 - x, 1 - y), False)
        send(1, first)

        cr = c_ref[...]
        act = (cr * _sigmoid(cr)).astype(BF16)
        dm = dmod_full[:, pl.ds(pl.multiple_of(me * n_mod, 128), n_mod)].astype(BF16)
        g_wm[...] = _dot_tn(act, dm)
        gc_mine[...] = _dot_nt(dm[n_ex:n_ex + 8, :], wm_ref[...].astype(BF16))
        gc_cps = direct_gather(gc_mine, gc_all, gc_send, gc_recv, misc_sems.at[1])

        ici[1].wait_recv()
        chip_sum(2, second, True)
        send(2, second)
        ici[0].wait_recv()
        ici[2].wait_recv()

        def finish(rs):
            gsum = mine[home, rs, :].astype(F32) + got_sib[home, rs, :].astype(F32)
            g_in[rs, :] = gsum + got_chip[0, rs, :].astype(F32) + got_chip[2, rs, :].astype(F32)

        rows_loop(finish)

        for cp in gc_cps[1:]:
            cp.wait_recv()
        gc_cps[0].wait()
        for rc in d2d + ici + small_cps[1:] + gc_cps[1:]:
            rc.wait_send()

    vm = pl.BlockSpec(memory_space=pltpu.VMEM)
    anyspace = pl.BlockSpec(memory_space=pl.ANY)
    assert rows_in % rb == 0 and parts_in.dtype == BF16
    args = [parts_in, *pieces, c_rows, w_mod_loc]
    in_specs = [anyspace] + [vm] * (len(args) - 1)
    out_shape = [jax.ShapeDtypeStruct(shp, F32), jax.ShapeDtypeStruct(w_mod_loc.shape, F32),
                 jax.ShapeDtypeStruct((1, SMALL_W), F32), jax.ShapeDtypeStruct((1, 3 * D), F32),
                 jax.ShapeDtypeStruct((N_DEV, 8, D), F32), jax.ShapeDtypeStruct((1, 1), F32)]
    scratch = [pltpu.VMEM((4,) + shp, BF16), pltpu.VMEM((4,) + shp, BF16), pltpu.VMEM((3,) + shp, BF16),
               pltpu.VMEM((3,) + shp, BF16), pltpu.VMEM((8, ROW_W), F32), pltpu.VMEM((N_DEV, 8, ROW_W), F32),
               pltpu.VMEM((n_ex + 8, 3 * D), F32), pltpu.VMEM((8, D), F32),
               pltpu.SemaphoreType.DMA((4,)), pltpu.SemaphoreType.DMA((4,)), pltpu.SemaphoreType.DMA((3,)),
               pltpu.SemaphoreType.DMA((3,)), pltpu.SemaphoreType.DMA((4,)),
               pltpu.SemaphoreType.DMA((N_DEV - 1,)), pltpu.SemaphoreType.DMA((N_DEV - 1,)),
               pltpu.SemaphoreType.DMA((N_DEV - 1,)), pltpu.SemaphoreType.DMA((N_DEV - 1,)),
               pltpu.SemaphoreType.DMA((2,))]
    return pl.pallas_call(
        body, name="epilogue", out_shape=out_shape, in_specs=in_specs, out_specs=[vm] * len(out_shape),
        scratch_shapes=scratch, compiler_params=pltpu.CompilerParams(vmem_limit_bytes=VMEM_LIMIT),
    )(*args)


def _final_adamw(g_in, w_in_t, m_in_t, v_in_t, g_wm, w_mod_loc, m_mod, v_mod, summed, g_bmod, gc_all,
                 small_w, small_m, small_v):
    ns = len(_SMALL)
    rb = 32

    def body(*refs):
        it = iter(refs)
        take = lambda k: [next(it) for _ in range(k)]
        (gin_ref, w_ref, m_ref, v_ref, gwm_ref, wm_ref, mm_ref, vm_ref, sum_ref, gb_ref, gc_ref) = take(11)
        sw, sm, sv = take(ns), take(ns), take(ns)
        (d_in, nm_in, nv_in, d_wm, nm_wm, nv_wm) = take(6)
        souts = take(4 * ns)

        def big(g_r, w_r, m_r, v_r, d_o, nm_o, nv_o):
            def step(i, carry):
                rs = pl.ds(pl.multiple_of(i * rb, rb), rb)
                dl, m_new, v_new = _adamw(w_r[rs, :], g_r[rs, :], m_r[rs, :], v_r[rs, :])
                d_o[rs, :] = dl
                nm_o[rs, :] = m_new
                nv_o[rs, :] = v_new
                return carry
            lax.fori_loop(0, w_r.shape[0] // rb, step, 0)

        big(gin_ref, w_ref, m_ref, v_ref, d_in, nm_in, nv_in)
        big(gwm_ref, wm_ref, mm_ref, vm_ref, d_wm, nm_wm, nv_wm)
        for k, (name, off) in enumerate(_SMALL):
            w = sw[k][...]
            if name == "b_mod":
                gk = gb_ref[...]
            elif name == "c_ctx":
                acc = gc_ref[0, 0:1, :]
                for j in range(1, N_DEV):
                    acc = acc + gc_ref[j, 0:1, :]
                sg = _sigmoid(w)
                gk = acc * (sg * (1.0 + w * (1.0 - sg)))
            else:
                gk = sum_ref[:, off:off + w.shape[1]]
            dl, m_new, v_new = _adamw(w, gk, sm[k][...], sv[k][...])
            souts[k][...] = gk
            souts[ns + k][...] = dl
            souts[2 * ns + k][...] = m_new
            souts[3 * ns + k][...] = v_new

    assert w_in_t.shape[0] % rb == 0 and w_mod_loc.shape[0] % rb == 0
    big_shape = jax.ShapeDtypeStruct(w_in_t.shape, F32)
    mod_shape = jax.ShapeDtypeStruct(w_mod_loc.shape, F32)
    out_shape = [big_shape] * 3 + [mod_shape] * 3 + [jax.ShapeDtypeStruct(w.shape, F32) for w in small_w] * 4
    outs = pl.pallas_call(
        body, name="final_adamw", out_shape=out_shape,
        compiler_params=pltpu.CompilerParams(vmem_limit_bytes=VMEM_LIMIT),
    )(g_in, w_in_t, m_in_t, v_in_t, g_wm, w_mod_loc, m_mod, v_mod, summed, g_bmod, gc_all, *small_w, *small_m, *small_v)
    small_outs = [outs[6 + k * ns:6 + (k + 1) * ns] for k in range(4)]
    return (g_in,) + tuple(outs[0:3]), (g_wm,) + tuple(outs[3:6]), small_outs


def _rope_tables(s):
    t = jnp.arange(s, dtype=jnp.int32)
    row = (t // GRID_W).astype(F32)
    col = (t % GRID_W).astype(F32)
    freqs = ROPE_THETA ** (-jnp.arange(0, HD // 2, 2, dtype=F32) / (HD // 2))
    ang_r = row[:, None] * freqs[None, :]
    ang_c = col[:, None] * freqs[None, :]
    cr, sr, cc, sc = jnp.cos(ang_r), jnp.sin(ang_r), jnp.cos(ang_c), jnp.sin(ang_c)
    cos = jnp.concatenate([cr, cr, cc, cc], axis=-1)
    sins = jnp.concatenate([-sr, sr, -sc, sc], axis=-1)
    return jnp.tile(cos, (1, KVW // HD)), jnp.tile(sins, (1, KVW // HD))


def kernel(x, c, ctx, c_ctx, w_mod, b_mod, norm_w, w_in, q_norm_w, k_norm_w, conv_w, conv_b, conv_ln_w, conv_ln_b, w_pw, b_pw, w_out, loss_target, m_c_ctx, m_w_mod, m_b_mod, m_norm_w, m_w_in, m_q_norm_w, m_k_norm_w, m_conv_w, m_conv_b, m_conv_ln_w, m_conv_ln_b, m_w_pw, m_b_pw, m_w_out, v_c_ctx, v_w_mod, v_b_mod, v_norm_w, v_w_in, v_q_norm_w, v_k_norm_w, v_conv_w, v_conv_b, v_conv_ln_w, v_conv_ln_b, v_w_pw, v_b_pw, v_w_out):
    bl, s, _ = x.shape
    cl = ctx.shape[1]
    me = _lin(*_coords())

    conv_w_pad = jnp.pad(conv_w[0], ((0, 32 - KW), (0, 0)))
    n_ex = N_DEV * bl
    g_win, c_rows, g_mod = _prologue(w_in[0].T, c, c_ctx[None, :], w_mod[0], b_mod)
    w_in_b = g_win.reshape(D_IN, D)
    mod_all = g_mod.transpose(1, 0, 2).reshape(n_ex + 8, 3 * D)
    modrows = lax.dynamic_slice_in_dim(mod_all, me * bl, bl, axis=0).reshape(bl, 3, D)
    modc = mod_all[n_ex].reshape(1, 3, D)

    cos, sins = _rope_tables(s)
    qnw_t = jnp.tile(q_norm_w, (1, DA // HD))
    knw_t = jnp.tile(k_norm_w, (1, KVW // HD))
    lane = jnp.arange(DA, dtype=jnp.int32) // HD
    ones_bd = (lane[:, None] == lane[None, :]).astype(BF16)
    ones_kv = ones_bd[0:KVW, 0:KVW]
    w_kv_b = w_in_b

    k_ctx, v_ctx, pkv_c = _ctx_fwd(ctx, modc, norm_w, w_kv_b, knw_t, ones_kv, cl + s)
    (q_h, k_h, v_h, pq, pkv, za, glu, zc), (g_wout, g_wpw, g_cw) = _fwd_in(
        x, modrows, norm_w, w_in_b, cos, sins, qnw_t, knw_t, ones_bd, k_ctx, v_ctx,
        [w_out[0], w_pw[0], conv_w_pad], [BF16, BF16, F32])
    w_out_b = g_wout.reshape(D, D)
    w_pw_b = g_wpw.reshape(DC, DC)
    conv_w_full = g_cw.transpose(1, 0, 2).reshape(32, DC)
    attn, lse = _attn_fwd(q_h, k_h, v_h)
    y_conv, cp = _conv_fwd(glu, conv_w_full, conv_b, conv_ln_w, conv_ln_b, w_pw_b, b_pw)

    do_h, dza, dy_conv, dzc, dh, dgate, gw_out, loss_row, gw_pw, conv_rows = _out_fwd_bwd(
        attn, za, cp, zc, x, loss_target, modrows, w_out_b, y_conv, conv_ln_w, conv_ln_b, w_pw_b)
    dglu, g_cw_full = _conv_bwd_depthwise(glu, dy_conv, conv_w_full)
    parts_out = gw_out.reshape(N_DEV, D // N_DEV, D)
    parts_pw = gw_pw.reshape(N_DEV, DC // N_DEV, DC)
    parts_cw = g_cw_full.astype(BF16).reshape(32, N_DEV, DC // N_DEV).transpose(1, 0, 2)
    (dq, dk_h, dv_h), (got_out, got_pw, got_cw) = _attn_bwd(
        q_h, k_h, v_h, do_h, attn, lse, [parts_out, parts_pw, parts_cw])
    gw_kv, ctx_rows, dknw_c = _ctx_bwd(ctx, modc, norm_w, w_kv_b, pkv_c, dk_h, dv_h, knw_t, ones_kv)
    grad_x, gw_in, dmod_ss, dnw, dqnw, dknw = _bwd_in(
        x, modrows, norm_w, w_in_b, cos, sins, qnw_t, knw_t, ones_bd,
        pq, pkv, dq, dk_h, dv_h, dza, dglu, dzc, dh, gw_kv)

    r_out, r_pw, r_cw = _sum_devices_adamw(
        [(got_out, w_out[0], m_w_out[0], v_w_out[0]), (got_pw, w_pw[0], m_w_pw[0], v_w_pw[0]),
         (got_cw, conv_w[0], m_conv_w[0], v_conv_w[0])])

    given = {"c_ctx": (c_ctx, m_c_ctx, v_c_ctx), "b_mod": (b_mod, m_b_mod, v_b_mod), "norm_w": (norm_w, m_norm_w, v_norm_w),
             "q_norm_w": (q_norm_w, m_q_norm_w, v_q_norm_w), "k_norm_w": (k_norm_w, m_k_norm_w, v_k_norm_w),
             "conv_b": (conv_b, m_conv_b, v_conv_b), "conv_ln_w": (conv_ln_w, m_conv_ln_w, v_conv_ln_w),
             "conv_ln_b": (conv_ln_b, m_conv_ln_b, v_conv_ln_b), "b_pw": (b_pw, m_b_pw, v_b_pw)}
    as_rows = [[given[name][which].reshape(1, -1) for name, _ in _SMALL] for which in range(3)]
    g_in_t, g_wmod, summed, g_bmod, gc_all, loss11 = _epilogue(
        gw_in.reshape(N_DEV, D_IN // N_DEV, D),
        [loss_row, ctx_rows, dnw, dqnw, dknw, dknw_c, conv_rows, dmod_ss, dgate], c_rows, w_mod[0])
    r_in, r_wmod, small_outs = _final_adamw(
        g_in_t, w_in[0].T, m_w_in[0].T, v_w_in[0].T, g_wmod, w_mod[0], m_w_mod[0], v_w_mod[0],
        summed, g_bmod, gc_all, *as_rows)
    r_in = tuple(a.T for a in r_in)

    big = {"w_mod": r_wmod, "w_in": r_in, "conv_w": r_cw, "w_pw": r_pw, "w_out": r_out}
    order = ["c_ctx", "w_mod", "b_mod", "norm_w", "w_in", "q_norm_w", "k_norm_w", "conv_w", "conv_b", "conv_ln_w",
             "conv_ln_b", "w_pw", "b_pw", "w_out"]
    small_index = {name: k for k, (name, _) in enumerate(_SMALL)}
    outs = [loss11.reshape(()), grad_x]
    for which in range(4):
        for name in order:
            if name in big:
                outs.append(big[name][which][None])
            else:
                outs.append(small_outs[which][small_index[name]].reshape(given[name][0].shape))
    return tuple(outs)
```

```python
import jax
import jax.numpy as jnp
from jax import lax
from jax.experimental import pallas as pl
from jax.experimental.pallas import tpu as pltpu

F32, BF16 = jnp.float32, jnp.bfloat16
MESH_ID = pl.DeviceIdType.MESH

N_DEV = 8
D = 1024
D_IN = 2816
DA = 512
DC = 512
HD = 64
KVW = 128
KW = 31
HALO = 16
EPS = 1e-6
ROPE_THETA = 10000.0
GRID_W = 64

ADAM_LR, ADAM_B1, ADAM_B2, ADAM_EPS, ADAM_WD, ADAM_STEP = 0.001, 0.9, 0.999, 1e-08, 0.01, 10

VMEM_LIMIT = 56 * 1024 * 1024

TM = 256
TQ = 128
TOKEN_PARTS = 2
OUT_TM = 512
BWD_PARTS = 2
FWD_PARTS = 4
TC = 512
CH = 32


def _params(sem, vmem=VMEM_LIMIT):
    return pltpu.CompilerParams(dimension_semantics=sem, vmem_limit_bytes=vmem)


def _dot(a, b):
    return jnp.dot(a, b, preferred_element_type=F32)


def _dot_nt(a, b):
    return lax.dot_general(a, b, (((1,), (1,)), ((), ())), preferred_element_type=F32)


def _dot_tn(a, b):
    return lax.dot_general(a, b, (((0,), (0,)), ((), ())), preferred_element_type=F32)


def _sigmoid(z):
    return 1.0 / (1.0 + jnp.exp(-z))


def _segsum(v, ones_bd):
    return _dot(v.astype(BF16), ones_bd)


def _swap16(x):
    w = x.shape[-1]
    lane = lax.broadcasted_iota(jnp.int32, x.shape, 1)
    return jnp.where((lane % 32) < 16, pltpu.roll(x, w - 16, 1), pltpu.roll(x, 16, 1))


def _with_ones_column(v):
    one = (lax.broadcasted_iota(jnp.int32, v.shape, 1) == 0).astype(v.dtype)
    return jnp.concatenate([v, one], axis=-1)


def _rope(x, cos, sins):
    return x * cos + _swap16(x) * sins


def _rope_bwd(d, cos, sins):
    return d * cos + _swap16(d * sins)


def _adamw(w, g, m, v):
    m2 = ADAM_B1 * m + (1.0 - ADAM_B1) * g
    v2 = ADAM_B2 * v + (1.0 - ADAM_B2) * (g * g)
    m_hat = m2 / (1.0 - ADAM_B1 ** ADAM_STEP)
    v_hat = v2 / (1.0 - ADAM_B2 ** ADAM_STEP)
    delta = -ADAM_LR * (m_hat / (jnp.sqrt(v_hat) + ADAM_EPS) + ADAM_WD * w)
    return delta, m2, v2


def _coords():
    return lax.axis_index("x"), lax.axis_index("y"), lax.axis_index("c")


def _lin(x, y, c):
    return 4 * x + 2 * y + c


def _prologue(w_in_t, c, c_ctx_row, w_mod_loc, b_mod):
    bl = c.shape[0]
    n_ex = N_DEV * bl
    n_mod = w_mod_loc.shape[1]

    def body(w32_ref, c_in_ref, cctx_ref, wm_ref, b_ref, out_w, crows_ref, mod_out, w_ref, c_ref, c_gath, mod_mine,
             w_send, w_recv, c_send, c_recv, m_send, m_recv, local_sems):
        x, y, c = _coords()
        me_lin = _lin(x, y, c)
        c_ref[...] = jnp.zeros_like(c_ref)
        c_ref[0:bl, :] = c_in_ref[...]
        w_ref[...] = w32_ref[...].astype(BF16)
        me, sib = (x, y, c), (x, y, 1 - c)
        xnb, ynb, diag = (1 - x, y), (x, 1 - y), (1 - x, 1 - y)
        north = c == 1

        def direct_gather(src, dst, send_sems, recv_sems, local_sem):
            cps = [pltpu.make_async_copy(src, dst.at[me_lin], local_sem)]
            for k in range(1, N_DEV):
                peer = (1 - x if k & 4 else x, 1 - y if k & 2 else y, 1 - c if k & 1 else c)
                cps.append(pltpu.make_async_remote_copy(
                    src_ref=src, dst_ref=dst.at[me_lin], send_sem=send_sems.at[k - 1], recv_sem=recv_sems.at[k - 1],
                    device_id=peer, device_id_type=MESH_ID))
            for cp in cps:
                cp.start()
            return cps

        def copy(k, block, to, src=None):
            slot = out_w.at[_lin(*block)]
            return pltpu.make_async_remote_copy(
                src_ref=slot if src is None else src, dst_ref=slot, send_sem=w_send.at[k], recv_sem=w_recv.at[k],
                device_id=to, device_id_type=MESH_ID)

        c_cps = direct_gather(c_ref, c_gath, c_send, c_recv, local_sems.at[0])
        mine = pltpu.make_async_copy(w_ref, out_w.at[me_lin], local_sems.at[1])
        mine.start()
        first = [copy(0, me, sib, src=w_ref), copy(1, me, (*xnb, c), src=w_ref), copy(2, me, (*ynb, c), src=w_ref)]
        for cp in first:
            cp.start()

        for cp in c_cps[1:]:
            cp.wait_recv()
        c_cps[0].wait()
        crows_ref[...] = jnp.zeros_like(crows_ref)
        for j in range(N_DEV):
            crows_ref[j * bl:(j + 1) * bl, :] = c_gath[j, 0:bl, :]
        crows_ref[n_ex:n_ex + 1, :] = cctx_ref[...]
        cr = crows_ref[...]
        act = (cr * _sigmoid(cr)).astype(BF16)
        mod_mine[...] = _dot(act, wm_ref[...].astype(BF16)) + b_ref[:, pl.ds(pl.multiple_of(me_lin * n_mod, 128), n_mod)]
        mod_cps = direct_gather(mod_mine, mod_out, m_send, m_recv, local_sems.at[2])

        relay_north = copy(3, (*xnb, c), (*ynb, c))
        relay_south = copy(3, (*ynb, c), (*xnb, c))
        passed = []
        copy(1, (*xnb, c), me).wait_recv()
        pl.when(north)(relay_north.start)
        passed.append(copy(4, (*xnb, c), sib))
        passed[-1].start()
        copy(2, (*ynb, c), me).wait_recv()
        pl.when(jnp.logical_not(north))(relay_south.start)
        passed.append(copy(5, (*ynb, c), sib))
        passed[-1].start()
        copy(3, (*diag, c), me).wait_recv()
        passed.append(copy(6, (*diag, c), sib))
        passed[-1].start()
        copy(0, sib, me).wait_recv()
        for k, chip in ((4, xnb), (5, ynb), (6, diag)):
            copy(k, (*chip, 1 - c), me).wait_recv()
        for cp in mod_cps[1:]:
            cp.wait_recv()
        mod_cps[0].wait()
        for cp in first + passed + [relay_north] + c_cps[1:] + mod_cps[1:]:
            cp.wait_send()
        mine.wait()

    vm = pl.BlockSpec(memory_space=pltpu.VMEM)
    seven = pltpu.SemaphoreType.DMA((N_DEV - 1,))
    return pl.pallas_call(
        body, name="prologue",
        out_shape=[jax.ShapeDtypeStruct((N_DEV,) + w_in_t.shape, BF16), jax.ShapeDtypeStruct((n_ex + 8, D), F32),
                   jax.ShapeDtypeStruct((N_DEV, n_ex + 8, n_mod), F32)],
        in_specs=[vm] * 5, out_specs=[vm] * 3,
        scratch_shapes=[pltpu.VMEM(w_in_t.shape, BF16), pltpu.VMEM((8, D), F32), pltpu.VMEM((N_DEV, 8, D), F32),
                        pltpu.VMEM((n_ex + 8, n_mod), F32),
                        seven, seven, seven, seven, seven, seven, pltpu.SemaphoreType.DMA((3,))],
        compiler_params=pltpu.CompilerParams(vmem_limit_bytes=VMEM_LIMIT),
    )(w_in_t, c, c_ctx_row, w_mod_loc, b_mod)


def _exchange_copies(in_refs, out_refs, send_sems, recv_sems, local_sems, scatter):
    x, y, c = _coords()
    me = _lin(x, y, c)
    local, remote = [], []
    for a, (src, dst) in enumerate(zip(in_refs, out_refs)):
        local.append(pltpu.make_async_copy(src.at[me] if scatter else src, dst.at[me], local_sems.at[a]))
        for k in range(1, N_DEV):
            peer = (1 - x if k & 4 else x, 1 - y if k & 2 else y, 1 - c if k & 1 else c)
            remote.append(pltpu.make_async_remote_copy(
                src_ref=src.at[_lin(*peer)] if scatter else src, dst_ref=dst.at[me],
                send_sem=send_sems.at[a * (N_DEV - 1) + k - 1], recv_sem=recv_sems.at[a * (N_DEV - 1) + k - 1],
                device_id=peer, device_id_type=MESH_ID))
    return local, remote


def _exchange_scratch(n):
    return [pltpu.SemaphoreType.DMA((n * (N_DEV - 1),)), pltpu.SemaphoreType.DMA((n * (N_DEV - 1),)),
            pltpu.SemaphoreType.DMA((n,))]


def _sum_devices_adamw(items):
    n = len(items)

    def body(*refs):
        for a in range(n):
            got, w_ref, m_ref, v_ref = refs[4 * a:4 * a + 4]
            g_ref, d_ref, nm_ref, nv_ref = refs[4 * n + 4 * a:4 * n + 4 * a + 4]
            g = got[0].astype(F32)
            for j in range(1, N_DEV):
                g = g + got[j].astype(F32)
            g = g[0:w_ref.shape[0], :]
            delta, m2, v2 = _adamw(w_ref[...], g, m_ref[...], v_ref[...])
            g_ref[...] = g
            d_ref[...] = delta
            nm_ref[...] = m2
            nv_ref[...] = v2

    args, out_shape = [], []
    for got, w, m, v in items:
        assert got.shape[0] == N_DEV and got.shape[1] >= w.shape[0] and got.shape[2:] == w.shape[1:]
        args += [got, w, m, v]
        out_shape += [jax.ShapeDtypeStruct(w.shape, F32)] * 4
    outs = pl.pallas_call(body, name="sum_devices_adamw", out_shape=out_shape,
                          compiler_params=pltpu.CompilerParams(vmem_limit_bytes=VMEM_LIMIT))(*args)
    return [tuple(outs[4 * a:4 * a + 4]) for a in range(n)]


def _fwd_in(x, modrows, norm_w, w_in_b, cos, sins, qnw_t, knw_t, ones_bd, k_all, v_all, shards, wire_dtypes):
    bl, s, _ = x.shape
    tm = TOKEN_PARTS * TM
    nt = s // tm
    n_sh = len(shards)

    def body(*refs):
        (x_ref, mod_ref, nw_ref, win_ref, cos_ref, sin_ref, qnw_ref, knw_ref, bd_ref, kin_ref, vin_ref) = refs[:11]
        shard_refs = refs[11:11 + n_sh]
        q_ref, k_ref, v_ref, pq_ref, pkv_ref, za_ref, glu_ref, zc_ref = refs[11 + n_sh:19 + n_sh]
        gathered_refs = refs[19 + n_sh:19 + 2 * n_sh]
        stage_refs = refs[19 + 2 * n_sh:19 + 3 * n_sh]
        send_sems, recv_sems, local_sems = refs[19 + 3 * n_sh:]
        b, i = pl.program_id(0), pl.program_id(1)
        local, remote = _exchange_copies(stage_refs, gathered_refs, send_sems, recv_sems, local_sems, scatter=False)

        @pl.when((b == 0) & (i == 0))
        def _():
            for src, stage in zip(shard_refs, stage_refs):
                stage[...] = src[...].astype(stage.dtype)
            for cp in local + remote:
                cp.start()

        shift = mod_ref[0, 0:1, :]
        scale = mod_ref[0, 1:2, :]
        for part in range(TOKEN_PARTS):
            rows = pl.ds(part * TM, TM)
            xv = x_ref[0, rows, :]
            r = lax.rsqrt(jnp.mean(xv * xv, axis=-1, keepdims=True) + EPS)
            u = (xv * r * nw_ref[...]) * (1.0 + scale) + shift
            p = _dot_nt(u.astype(BF16), win_ref[...])
            pq = p[:, 0:DA]
            pk = p[:, DA:DA + HD * 2]
            ck = cos_ref[rows, :]
            sk = sin_ref[rows, :]
            cs = jnp.concatenate([ck] * (DA // KVW), axis=-1)
            sn = jnp.concatenate([sk] * (DA // KVW), axis=-1)
            rq = lax.rsqrt(_segsum(pq * pq, bd_ref[...]) * (1.0 / HD) + EPS)
            qn = pq * rq * qnw_ref[...]
            qr = _rope(qn, cs, sn) * 0.125
            for h in range(DA // HD):
                q_ref[0, h, rows, :] = qr[:, h * HD:(h + 1) * HD].astype(BF16)
            rk = lax.rsqrt(_segsum(pk * pk, bd_ref[0:KVW, 0:KVW]) * (1.0 / HD) + EPS)
            kn = pk * rk * knw_ref[...]
            kr = _rope(kn, ck, sk)
            pv = p[:, 640:768]
            for h in range(KVW // HD):
                k_ref[0, h, rows, :] = kr[:, h * HD:(h + 1) * HD].astype(BF16)
                v_ref[0, h, rows, :] = _with_ones_column(pv[:, h * HD:(h + 1) * HD]).astype(BF16)
            pq_ref[0, rows, :] = pq
            pkv_ref[0, rows, :] = p[:, 512:768]
            za_ref[0, rows, :] = p[:, 768:1280]
            glu_ref[0, rows, :] = p[:, 1280:2304]
            zc_ref[0, rows, :] = p[:, 2304:2816]

        @pl.when((b == bl - 1) & (i == nt - 1))
        def _():
            for cp in remote:
                cp.wait_recv()
            for cp in remote:
                cp.wait_send()
            for cp in local:
                cp.wait()

    def tile(w):
        return pl.BlockSpec((1, tm, w), lambda b, i: (b, i, 0))

    def const(shape):
        return pl.BlockSpec(shape, lambda b, i: (0,) * len(shape))

    outs = [(DA, F32), (2 * KVW, F32), (DA, F32), (2 * DC, F32), (DC, F32)]
    anyspace = pl.BlockSpec(memory_space=pl.ANY)
    rope = pl.BlockSpec((tm, KVW), lambda b, i: (i, 0))
    k_tile = pl.BlockSpec((1, KVW // HD, tm, HD), lambda b, i: (b, 0, i, 0))
    v_tile = pl.BlockSpec((1, KVW // HD, tm, 2 * HD), lambda b, i: (b, 0, i, 0))
    res = pl.pallas_call(
        body, name="fwd_in", grid=(bl, nt),
        in_specs=[tile(D), pl.BlockSpec((1, 3, D), lambda b, i: (b, 0, 0)), const((1, D)), const((D_IN, D)),
                  rope, rope, const((1, DA)), const((1, KVW)), const((DA, DA)), anyspace, anyspace]
        + [const(a.shape) for a in shards],
        out_specs=[pl.BlockSpec((1, DA // HD, tm, HD), lambda b, i: (b, 0, i, 0)), k_tile, v_tile]
        + [tile(w) for w, _ in outs] + [anyspace] * n_sh,
        out_shape=[jax.ShapeDtypeStruct((bl, DA // HD, s, HD), BF16), jax.ShapeDtypeStruct(k_all.shape, BF16),
                   jax.ShapeDtypeStruct(v_all.shape, BF16)]
        + [jax.ShapeDtypeStruct((bl, s, w), dt) for w, dt in outs]
        + [jax.ShapeDtypeStruct((N_DEV,) + a.shape, dt) for a, dt in zip(shards, wire_dtypes)],
        input_output_aliases={9: 1, 10: 2},
        scratch_shapes=[pltpu.VMEM(a.shape, dt) for a, dt in zip(shards, wire_dtypes)] + _exchange_scratch(n_sh),
        compiler_params=_params(("arbitrary", "arbitrary")),
    )(x, modrows, norm_w, w_in_b, cos, sins, qnw_t, knw_t, ones_bd, k_all, v_all, *shards)
    return res[:8], res[8:]


_KV_ROWS_OF_W_IN_T = pl.BlockSpec((2 * KVW, D), lambda b: (DA // (2 * KVW), 0))


def _ctx_fwd(ctx, modc, norm_w, w_kv_b, knw_t, ones_bd, n_keys):
    bl, cl, _ = ctx.shape

    def body(x_ref, mod_ref, nw_ref, w_ref, knw_ref, bd_ref, k_ref, v_ref, pkv_ref):
        xv = x_ref[0]
        shift = mod_ref[0, 0:1, :]
        scale = mod_ref[0, 1:2, :]
        r = lax.rsqrt(jnp.mean(xv * xv, axis=-1, keepdims=True) + EPS)
        u = (xv * r * nw_ref[...]) * (1.0 + scale) + shift
        p = _dot_nt(u.astype(BF16), w_ref[...])
        pk = p[:, 0:KVW]
        rk = lax.rsqrt(_segsum(pk * pk, bd_ref[...]) * (1.0 / HD) + EPS)
        kn = pk * rk * knw_ref[...]
        pv = p[:, KVW:2 * KVW]
        for h in range(KVW // HD):
            k_ref[0, h] = kn[:, h * HD:(h + 1) * HD].astype(BF16)
            v_ref[0, h] = _with_ones_column(pv[:, h * HD:(h + 1) * HD]).astype(BF16)
        pkv_ref[0] = p

    def const(shape):
        return pl.BlockSpec(shape, lambda b: (0,) * len(shape))

    def tile(w):
        return pl.BlockSpec((1, cl, w), lambda b: (b, 0, 0))

    ctx_block = (n_keys - cl) // cl
    assert ctx_block * cl + cl == n_keys
    k_tile = pl.BlockSpec((1, KVW // HD, cl, HD), lambda b: (b, 0, ctx_block, 0))
    v_tile = pl.BlockSpec((1, KVW // HD, cl, 2 * HD), lambda b: (b, 0, ctx_block, 0))
    return pl.pallas_call(
        body, name="ctx_fwd", grid=(bl,),
        in_specs=[tile(D), const((1, 3, D)), const((1, D)), _KV_ROWS_OF_W_IN_T, const((1, KVW)), const((KVW, KVW))],
        out_specs=[k_tile, v_tile, tile(2 * KVW)],
        out_shape=[jax.ShapeDtypeStruct((bl, KVW // HD, n_keys, HD), BF16),
                   jax.ShapeDtypeStruct((bl, KVW // HD, n_keys, 2 * HD), BF16),
                   jax.ShapeDtypeStruct((bl, cl, 2 * KVW), F32)],
        compiler_params=_params(("arbitrary",)),
    )(ctx, modc, norm_w, w_kv_b, knw_t, ones_bd)


def _attn_fwd(q, k, v1):
    bl, _, s, _ = q.shape
    n_keys = k.shape[2]

    def body(q_ref, k_ref, v_ref, o_ref, lse_ref):
        kv = k_ref[0, 0]
        vv = v_ref[0, 0]
        lane = lax.broadcasted_iota(jnp.int32, (TQ, 2 * HD), 1)
        for part in range(FWD_PARTS):
            rows = pl.ds(part * TQ, TQ)
            lse = jnp.zeros((TQ, 2 * HD), F32)
            heads = []
            sc_all = _dot_nt(q_ref[0, :, rows, :].reshape(4 * TQ, HD), kv)
            for h in range(4):
                sc = sc_all[h * TQ:(h + 1) * TQ, :]
                m = jnp.max(sc, axis=-1, keepdims=True)
                e = jnp.exp(sc - m).astype(BF16)
                ov = _dot(e, vv)
                denom = ov[:, HD:HD + 1]
                heads.append(ov[:, 0:HD] * (1.0 / denom))
                lse = jnp.where(lane == h, m + jnp.log(denom), lse)
            o_ref[0, rows, :] = jnp.concatenate(heads, axis=-1)
            lse_ref[0, 0, rows, :] = lse

    tq = FWD_PARTS * TQ
    ks = pl.BlockSpec((1, 1, n_keys, HD), lambda b, g, i: (b, g, 0, 0))
    qs = pl.BlockSpec((1, 4, tq, HD), lambda b, g, i: (b, g, i, 0))
    vs = pl.BlockSpec((1, 1, n_keys, 2 * HD), lambda b, g, i: (b, g, 0, 0))
    return pl.pallas_call(
        body, name="attn_fwd", grid=(bl, 2, s // tq), in_specs=[qs, ks, vs],
        out_specs=[pl.BlockSpec((1, tq, 4 * HD), lambda b, g, i: (b, i, g)),
                   pl.BlockSpec((1, 1, tq, 2 * HD), lambda b, g, i: (b, g, i, 0))],
        out_shape=[jax.ShapeDtypeStruct((bl, s, DA), F32), jax.ShapeDtypeStruct((bl, 2, s, 2 * HD), F32)],
        compiler_params=_params(("arbitrary", "arbitrary", "arbitrary")),
    )(q, k, v1)


def _attn_bwd(q, k, v1, do, o, lse, exchange):
    bl, _, s, _ = q.shape
    n_keys = k.shape[2]
    tq = BWD_PARTS * TQ
    nq = s // tq
    n_ex = len(exchange)

    def body(*refs):
        q_ref, k_ref, v_ref, do_ref, o_ref, lse_ref = refs[:6]
        part_refs = refs[6:6 + n_ex]
        dq_ref, dk_ref, dv_ref = refs[6 + n_ex:9 + n_ex]
        got_refs = refs[9 + n_ex:9 + 2 * n_ex]
        p_sc, ds_sc, dkt, dvt, send_sems, recv_sems, local_sems = refs[9 + 2 * n_ex:]
        i = pl.program_id(2)
        first = (pl.program_id(0) == 0) & (pl.program_id(1) == 0) & (i == 0)
        last = (pl.program_id(0) == bl - 1) & (pl.program_id(1) == 1) & (i == nq - 1)
        local, remote = _exchange_copies(part_refs, got_refs, send_sems, recv_sems, local_sems, scatter=True)

        @pl.when(first)
        def _():
            for cp in local + remote:
                cp.start()

        @pl.when(i == 0)
        def _():
            dkt[...] = jnp.zeros_like(dkt)
            dvt[...] = jnp.zeros_like(dvt)

        kv = k_ref[0, 0]
        vv = v_ref[0, 0][:, 0:HD]
        for part in range(BWD_PARTS):
            tq_rows = pl.ds(part * TQ, TQ)
            lse = lse_ref[0, 0, tq_rows, :]
            ov = o_ref[0, tq_rows, :]
            dqs = []
            q_cat = q_ref[0, :, tq_rows, :].reshape(4 * TQ, HD)
            do_cat = do_ref[0, :, tq_rows, :].reshape(4 * TQ, HD)
            sc_all = _dot_nt(q_cat, kv)
            for h in range(4):
                doh = do_cat[h * TQ:(h + 1) * TQ, :]
                delta = jnp.sum(ov[:, h * HD:(h + 1) * HD] * doh.astype(F32), axis=-1, keepdims=True)
                rows = pl.ds((part * 4 + h) * TQ, TQ)
                p = jnp.exp(sc_all[h * TQ:(h + 1) * TQ, :] - lse[:, h:h + 1])
                ds = (p * (_dot_nt(doh, vv) - delta)).astype(BF16)
                p_sc[rows, :] = p.astype(BF16)
                ds_sc[rows, :] = ds
                dqs.append(_dot(ds, kv) * 0.125)
            dq_ref[0, tq_rows, :] = jnp.concatenate(dqs, axis=-1)
            part_rows = pl.ds(part * 4 * TQ, 4 * TQ)
            dvt[...] += _dot_tn(do_cat, p_sc[part_rows, :])
            dkt[...] += _dot_tn(q_cat, ds_sc[part_rows, :])

        @pl.when(i == nq - 1)
        def _():
            dk_ref[0, 0] = dkt[...].T
            dv_ref[0, 0] = dvt[...].T

        @pl.when(last)
        def _():
            for cp in remote:
                cp.wait_recv()
            for cp in remote:
                cp.wait_send()
            for cp in local:
                cp.wait()

    qs = pl.BlockSpec((1, 4, tq, HD), lambda b, g, i: (b, g, i, 0))
    ks = pl.BlockSpec((1, 1, n_keys, HD), lambda b, g, i: (b, g, 0, 0))
    vs = pl.BlockSpec((1, 1, n_keys, 2 * HD), lambda b, g, i: (b, g, 0, 0))
    os_ = pl.BlockSpec((1, tq, 4 * HD), lambda b, g, i: (b, i, g))
    kshape = jax.ShapeDtypeStruct(k.shape, F32)
    anyspace = pl.BlockSpec(memory_space=pl.ANY)
    res = pl.pallas_call(
        body, name="attn_bwd", grid=(bl, 2, nq),
        in_specs=[qs, ks, vs, qs, os_, pl.BlockSpec((1, 1, tq, 2 * HD), lambda b, g, i: (b, g, i, 0))]
        + [anyspace] * n_ex,
        out_specs=[os_, ks, ks] + [anyspace] * n_ex,
        out_shape=[jax.ShapeDtypeStruct((bl, s, DA), F32), kshape, kshape]
        + [jax.ShapeDtypeStruct(a.shape, a.dtype) for a in exchange],
        scratch_shapes=[pltpu.VMEM((4 * tq, n_keys), BF16), pltpu.VMEM((4 * tq, n_keys), BF16),
                        pltpu.VMEM((HD, n_keys), F32), pltpu.VMEM((HD, n_keys), F32)] + _exchange_scratch(n_ex),
        compiler_params=_params(("arbitrary", "arbitrary", "arbitrary")),
    )(q, k, v1, do, o, lse, *exchange)
    return res[:3], res[3:]


def _halo_specs(width, s):
    per = TC // HALO
    last = s // HALO - 1
    main = pl.BlockSpec((1, TC, width), lambda b, i: (b, i, 0))
    prev = pl.BlockSpec((1, HALO, width), lambda b, i: (b, jnp.maximum(i * per - 1, 0), 0))
    nxt = pl.BlockSpec((1, HALO, width), lambda b, i: (b, jnp.minimum((i + 1) * per, last), 0))
    return main, prev, nxt


def _glu(g):
    return g[:, 0:DC] * _sigmoid(g[:, DC:2 * DC])


def _fill_padded(pad_ref, main, prev, nxt, first, last):
    pad_ref[0:HALO, :] = jnp.where(first, 0.0, prev)
    pad_ref[HALO:HALO + TC, :] = main
    pad_ref[HALO + TC:2 * HALO + TC, :] = jnp.where(last, 0.0, nxt)


PLANE_ROWS = TC + 2 * HALO - 8


def _shift_planes(pad_ref, planes_ref):
    for r in range(1, 8):
        planes_ref[r - 1] = pad_ref[pl.ds(r, PLANE_ROWS), :]


def _tap_rows(pad_ref, planes_ref, offset, start, n):
    a, r = divmod(offset, 8)
    if r == 0:
        return pad_ref[pl.ds(start + 8 * a, n), :]
    return planes_ref[r - 1, pl.ds(start + 8 * a, n), :]


def _conv_fwd(glu, conv_w, conv_b, ln_w, ln_b, w_pw_b, b_pw):
    bl, s, _ = glu.shape
    nt = s // TC

    def body(g_ref, gp_ref, gn_ref, cw_ref, cb_ref, lw_ref, lb_ref, wpw_ref, bpw_ref, y_ref, cp_ref, pad_ref, planes_ref):
        i = pl.program_id(1)
        _fill_padded(pad_ref, _glu(g_ref[0]), _glu(gp_ref[0]), _glu(gn_ref[0]), i == 0, i == nt - 1)
        _shift_planes(pad_ref, planes_ref)
        for ck in range(TC // CH):
            acc = jnp.zeros((CH, DC), F32) + cb_ref[...]
            for t in range(KW):
                acc = acc + _tap_rows(pad_ref, planes_ref, 1 + t, ck * CH, CH) * cw_ref[t:t + 1, :]
            y_ref[0, pl.ds(ck * CH, CH), :] = acc
        y = y_ref[0]
        mu = jnp.mean(y, axis=-1, keepdims=True)
        yc = y - mu
        var = jnp.mean(yc * yc, axis=-1, keepdims=True)
        z = yc * lax.rsqrt(var + EPS) * lw_ref[...] + lb_ref[...]
        act = z * _sigmoid(z)
        cp_ref[0] = _dot(act.astype(BF16), wpw_ref[...]) + bpw_ref[...]

    def const(shape):
        return pl.BlockSpec(shape, lambda b, i: (0,) * len(shape))

    main, prev, nxt = _halo_specs(2 * DC, s)
    tile = pl.BlockSpec((1, TC, DC), lambda b, i: (b, i, 0))
    return pl.pallas_call(
        body, name="conv_fwd", grid=(bl, nt),
        in_specs=[main, prev, nxt, const((32, DC)), const((1, DC)), const((1, DC)), const((1, DC)),
                  const((DC, DC)), const((1, DC))],
        out_specs=[tile, tile],
        out_shape=[jax.ShapeDtypeStruct((bl, s, DC), F32)] * 2,
        scratch_shapes=[pltpu.VMEM((TC + 2 * HALO, DC), F32), pltpu.VMEM((7, PLANE_ROWS, DC), F32)],
        compiler_params=_params(("arbitrary", "arbitrary")),
    )(glu, glu, glu, conv_w, conv_b, ln_w, ln_b, w_pw_b, b_pw)


def _conv_bwd_depthwise(glu, dy, conv_w):
    bl, s, _ = glu.shape
    nt = s // TC

    def body(g_ref, gp_ref, gn_ref, d_ref, dp_ref, dn_ref, cw_ref, dglu_ref, dcw_ref,
             padu_ref, padd_ref, planes_u, planes_d):
        i = pl.program_id(1)

        @pl.when((pl.program_id(0) == 0) & (i == 0))
        def _():
            dcw_ref[...] = jnp.zeros_like(dcw_ref)

        first, last = i == 0, i == nt - 1
        _fill_padded(padu_ref, _glu(g_ref[0]), _glu(gp_ref[0]), _glu(gn_ref[0]), first, last)
        _fill_padded(padd_ref, d_ref[0], dp_ref[0], dn_ref[0], first, last)
        _shift_planes(padu_ref, planes_u)
        _shift_planes(padd_ref, planes_d)
        for ck in range(TC // CH):
            acc = jnp.zeros((CH, DC), F32)
            for t in range(KW):
                acc = acc + _tap_rows(padd_ref, planes_d, 2 * HALO - 1 - t, ck * CH, CH) * cw_ref[t:t + 1, :]
            g = g_ref[0, pl.ds(ck * CH, CH), :]
            a = g[:, 0:DC]
            sg = _sigmoid(g[:, DC:2 * DC])
            dglu_ref[0, pl.ds(ck * CH, CH), 0:DC] = (acc * sg).astype(BF16)
            dglu_ref[0, pl.ds(ck * CH, CH), DC:2 * DC] = (acc * a * sg * (1.0 - sg)).astype(BF16)
        group = 4
        for t0 in range(0, KW, group):
            taps = range(t0, min(t0 + group, KW))
            acc8 = [jnp.zeros((8, DC), F32) for _ in taps]
            for ck in range(TC // CH):
                dchunk = d_ref[0, pl.ds(ck * CH, CH), :]
                for n, t in enumerate(taps):
                    prod = _tap_rows(padu_ref, planes_u, 1 + t, ck * CH, CH) * dchunk
                    acc8[n] = acc8[n] + jnp.sum(prod.reshape(CH // 8, 8, DC), axis=0)
            for n, t in enumerate(taps):
                dcw_ref[t:t + 1, :] += jnp.sum(acc8[n], axis=0, keepdims=True)

    gmain, gprev, gnext = _halo_specs(2 * DC, s)
    dmain, dprev, dnext = _halo_specs(DC, s)
    cw = pl.BlockSpec((32, DC), lambda b, i: (0, 0))
    return pl.pallas_call(
        body, name="conv_bwd_depthwise", grid=(bl, nt),
        in_specs=[gmain, gprev, gnext, dmain, dprev, dnext, cw],
        out_specs=[gmain, cw],
        out_shape=[jax.ShapeDtypeStruct((bl, s, 2 * DC), BF16), jax.ShapeDtypeStruct((32, DC), F32)],
        scratch_shapes=[pltpu.VMEM((TC + 2 * HALO, DC), F32)] * 2 + [pltpu.VMEM((7, PLANE_ROWS, DC), F32)] * 2,
        compiler_params=_params(("arbitrary", "arbitrary")),
    )(glu, glu, glu, dy, dy, dy, conv_w)


def _out_fwd_bwd(attn, za, cp, zc, x, target, modrows, w_out_b, y_conv, ln_w, ln_b, w_pw_b):
    bl, s, _ = x.shape
    tm = OUT_TM

    def body(o_ref, za_ref, cp_ref, zc_ref, x_ref, t_ref, mod_ref, w_ref, y_ref, lw_ref, lb_ref, wpw_ref,
             do_ref, dza_ref, dy_ref, dzc_ref, dh_ref, dgate_ref, gwb_ref, loss_ref, gpwb_ref, rows_ref,
             gw_ref, gpw_ref):
        b, i = pl.program_id(0), pl.program_id(1)

        @pl.when((b == 0) & (i == 0))
        def _():
            gw_ref[...] = jnp.zeros_like(gw_ref)
            gpw_ref[...] = jnp.zeros_like(gpw_ref)
            rows_ref[...] = jnp.zeros_like(rows_ref)
            loss_ref[...] = jnp.zeros_like(loss_ref)

        @pl.when(i == 0)
        def _():
            dgate_ref[...] = jnp.zeros_like(dgate_ref)

        gate = mod_ref[0, 2:3, :]
        w = w_ref[...]
        o, za_v, cp_v, zc_v = o_ref[0], za_ref[0], cp_ref[0], zc_ref[0]
        sa = _sigmoid(za_v)
        sc = _sigmoid(zc_v)
        silu_a = za_v * sa
        silu_c = zc_v * sc
        mix = jnp.concatenate([(o * silu_a).astype(BF16), (cp_v * silu_c).astype(BF16)], axis=-1)
        out = _dot(mix, w)
        err = x_ref[0] + gate * out - t_ref[0]
        loss_ref[...] += jnp.sum(err * err, axis=0, keepdims=True)
        dh = err * (1.0 / D)
        dh_ref[0] = dh
        dgate_ref[0] += jnp.sum(dh * out, axis=0, keepdims=True)
        dout = (dh * gate).astype(BF16)
        gw_ref[...] += _dot_tn(mix, dout)
        dmix = _dot_nt(dout, w)
        dga = dmix[:, 0:DA]
        dgc = dmix[:, DA:DA + DC]
        dov = dga * silu_a
        for h in range(DA // HD):
            do_ref[0, h] = dov[:, h * HD:(h + 1) * HD].astype(BF16)
        dza_ref[0] = (dga * o * (sa * (1.0 + za_v * (1.0 - sa)))).astype(BF16)
        dzc_ref[0] = (dgc * cp_v * (sc * (1.0 + zc_v * (1.0 - sc)))).astype(BF16)
        dcp = dgc * silu_c
        y = y_ref[0]
        yc = y - jnp.mean(y, axis=-1, keepdims=True)
        rstd = lax.rsqrt(jnp.mean(yc * yc, axis=-1, keepdims=True) + EPS)
        yn = yc * rstd
        lw = lw_ref[...]
        z = yn * lw + lb_ref[...]
        sg = _sigmoid(z)
        dcp_b = dcp.astype(BF16)
        gpw_ref[...] += _dot_tn((z * sg).astype(BF16), dcp_b)
        dz = _dot_nt(dcp_b, wpw_ref[...]) * (sg * (1.0 + z * (1.0 - sg)))
        dyn = dz * lw
        dy = rstd * (dyn - jnp.mean(dyn, axis=-1, keepdims=True) - yn * jnp.mean(dyn * yn, axis=-1, keepdims=True))
        dy_ref[0] = dy
        rows_ref[0:1, :] += jnp.sum(dcp, axis=0, keepdims=True)
        rows_ref[1:2, :] += jnp.sum(dz * yn, axis=0, keepdims=True)
        rows_ref[2:3, :] += jnp.sum(dz, axis=0, keepdims=True)
        rows_ref[3:4, :] += jnp.sum(dy, axis=0, keepdims=True)

        @pl.when((b == bl - 1) & (i == s // tm - 1))
        def _():
            gwb_ref[...] = gw_ref[...].astype(BF16)
            gpwb_ref[...] = gpw_ref[...].astype(BF16)

    def const(shape):
        return pl.BlockSpec(shape, lambda b, i: (0,) * len(shape))

    def tile(w):
        return pl.BlockSpec((1, tm, w), lambda b, i: (b, i, 0))

    return pl.pallas_call(
        body, name="out_fwd_bwd", grid=(bl, s // tm),
        in_specs=[tile(DA), tile(DA), tile(DC), tile(DC), tile(D), tile(D),
                  pl.BlockSpec((1, 3, D), lambda b, i: (b, 0, 0)), const((D, D)),
                  tile(DC), const((1, DC)), const((1, DC)), const((DC, DC))],
        out_specs=[pl.BlockSpec((1, DA // HD, tm, HD), lambda b, i: (b, 0, i, 0)), tile(DA), tile(DC), tile(DC), tile(D),
                   pl.BlockSpec((1, 1, D), lambda b, i: (b, 0, 0)), const((D, D)), const((1, D)),
                   const((DC, DC)), const((8, DC))],
        out_shape=[jax.ShapeDtypeStruct((bl, DA // HD, s, HD), BF16), jax.ShapeDtypeStruct((bl, s, DA), BF16),
                   jax.ShapeDtypeStruct((bl, s, DC), F32), jax.ShapeDtypeStruct((bl, s, DC), BF16),
                   jax.ShapeDtypeStruct((bl, s, D), F32), jax.ShapeDtypeStruct((bl, 1, D), F32),
                   jax.ShapeDtypeStruct((D, D), BF16), jax.ShapeDtypeStruct((1, D), F32),
                   jax.ShapeDtypeStruct((DC, DC), BF16), jax.ShapeDtypeStruct((8, DC), F32)],
        scratch_shapes=[pltpu.VMEM((D, D), F32), pltpu.VMEM((DC, DC), F32)],
        compiler_params=_params(("arbitrary", "arbitrary")),
    )(attn, za, cp, zc, x, target, modrows, w_out_b, y_conv, ln_w, ln_b, w_pw_b)


def _rms_heads_bwd(dy, x, w_t, ones_bd):
    r = lax.rsqrt(_segsum(x * x, ones_bd) * (1.0 / HD) + EPS)
    xh = x * r
    g = dy * w_t
    dx = r * (g - xh * (_segsum(g * xh, ones_bd) * (1.0 / HD)))
    return dx, dy * xh


def _ctx_bwd(ctx, modc, norm_w, w_kv_b, pkv_c, dk_c, dv_c, knw_t, ones_bd):
    bl, cl, _ = ctx.shape

    def body(x_ref, mod_ref, nw_ref, w_ref, p_ref, dk_ref, dv_ref, knw_ref, bd_ref, gw_ref, rows_ref, dknw_ref):
        @pl.when(pl.program_id(0) == 0)
        def _():
            gw_ref[...] = jnp.zeros_like(gw_ref)
            rows_ref[...] = jnp.zeros_like(rows_ref)
            dknw_ref[...] = jnp.zeros_like(dknw_ref)

        xv = x_ref[0]
        shift = mod_ref[0, 0:1, :]
        scale = mod_ref[0, 1:2, :]
        nw = nw_ref[...]
        r = lax.rsqrt(jnp.mean(xv * xv, axis=-1, keepdims=True) + EPS)
        xn = xv * r
        yv = xn * nw
        u = yv * (1.0 + scale) + shift
        dkv = jnp.concatenate([dk_ref[0, 0], dk_ref[0, 1]], axis=-1)
        dpk, dknw = _rms_heads_bwd(dkv, p_ref[0][:, 0:KVW], knw_ref[...], bd_ref[...])
        dp = jnp.concatenate([dpk.astype(BF16), dv_ref[0, 0].astype(BF16), dv_ref[0, 1].astype(BF16)], axis=-1)
        gw_ref[...] += _dot_tn(dp, u.astype(BF16))
        du = _dot(dp, w_ref[...])
        rows_ref[0:1, :] += jnp.sum(du, axis=0, keepdims=True)
        rows_ref[1:2, :] += jnp.sum(du * yv, axis=0, keepdims=True)
        rows_ref[2:3, :] += jnp.sum(du * (1.0 + scale) * xn, axis=0, keepdims=True)
        dknw_ref[...] += jnp.sum(dknw, axis=0, keepdims=True)

    def const(shape):
        return pl.BlockSpec(shape, lambda b: (0,) * len(shape))

    def tile(w):
        return pl.BlockSpec((1, cl, w), lambda b: (b, 0, 0))

    ctx_block = (dk_c.shape[2] - cl) // cl
    kv_tile = pl.BlockSpec((1, KVW // HD, cl, HD), lambda b: (b, 0, ctx_block, 0))
    return pl.pallas_call(
        body, name="ctx_bwd", grid=(bl,),
        in_specs=[tile(D), const((1, 3, D)), const((1, D)), _KV_ROWS_OF_W_IN_T, tile(2 * KVW), kv_tile, kv_tile,
                  const((1, KVW)), const((KVW, KVW))],
        out_specs=[const((2 * KVW, D)), const((8, D)), const((1, KVW))],
        out_shape=[jax.ShapeDtypeStruct((2 * KVW, D), F32), jax.ShapeDtypeStruct((8, D), F32),
                   jax.ShapeDtypeStruct((1, KVW), F32)],
        compiler_params=_params(("arbitrary",)),
    )(ctx, modc, norm_w, w_kv_b, pkv_c, dk_c, dv_c, knw_t, ones_bd)


def _bwd_in(x, modrows, norm_w, w_in_b, cos, sins, qnw_t, knw_t, ones_bd,
            pq, pkv, dq, dk, dv, dza, dglu, dzc, dh, gw_kv):
    bl, s, _ = x.shape
    tm = TOKEN_PARTS * TM
    nt = s // tm

    def body(x_ref, mod_ref, nw_ref, win_hbm, cos_ref, sin_ref, qnw_ref, knw_ref, bd_ref,
             pq_ref, pkv_ref, dq_ref, dk_ref, dv_ref, dza_ref, dglu_ref, dzc_ref, dh_ref, gwkv_ref,
             gx_ref, gw_hbm, dmod_ref, dnw_ref, dqnw_ref, dknw_ref, win_ref, gw_acc, sem):
        b, i = pl.program_id(0), pl.program_id(1)

        @pl.when((b == 0) & (i == 0))
        def _():
            cp = pltpu.make_async_copy(win_hbm, win_ref, sem)
            cp.start()
            gw_acc[...] = jnp.zeros_like(gw_acc)
            dnw_ref[...] = jnp.zeros_like(dnw_ref)
            dqnw_ref[...] = jnp.zeros_like(dqnw_ref)
            dknw_ref[...] = jnp.zeros_like(dknw_ref)
            cp.wait()

        @pl.when(i == 0)
        def _():
            dmod_ref[...] = jnp.zeros_like(dmod_ref)

        bd = bd_ref[...]
        shift = mod_ref[0, 0:1, :]
        scale = mod_ref[0, 1:2, :]
        nw = nw_ref[...]
        dps, us = [], []
        for part in range(TOKEN_PARTS):
            rows = pl.ds(part * TM, TM)
            ck = cos_ref[rows, :]
            sk = sin_ref[rows, :]
            cs = jnp.concatenate([ck] * (DA // KVW), axis=-1)
            sn = jnp.concatenate([sk] * (DA // KVW), axis=-1)
            dqn = _rope_bwd(dq_ref[0, rows, :], cs, sn)
            dpq, dqnw = _rms_heads_bwd(dqn, pq_ref[0, rows, :], qnw_ref[...], bd)
            dkn = _rope_bwd(jnp.concatenate([dk_ref[0, 0, rows, :], dk_ref[0, 1, rows, :]], axis=-1), ck, sk)
            dpk, dknw = _rms_heads_bwd(dkn, pkv_ref[0, rows, 0:KVW], knw_ref[...], bd[0:KVW, 0:KVW])
            dqnw_ref[...] += jnp.sum(dqnw, axis=0, keepdims=True)
            dknw_ref[...] += jnp.sum(dknw, axis=0, keepdims=True)
            dp = jnp.concatenate(
                [dpq.astype(BF16), dpk.astype(BF16), dv_ref[0, 0, rows, :].astype(BF16), dv_ref[0, 1, rows, :].astype(BF16),
                 dza_ref[0, rows, :], dglu_ref[0, rows, :], dzc_ref[0, rows, :]], axis=-1)

            xv = x_ref[0, rows, :]
            r = lax.rsqrt(jnp.mean(xv * xv, axis=-1, keepdims=True) + EPS)
            xn = xv * r
            yv = xn * nw
            u = yv * (1.0 + scale) + shift
            dps.append(dp)
            us.append(u.astype(BF16))
            du = _dot(dp, win_ref[...])
            dmod_ref[0, 0:1, :] += jnp.sum(du, axis=0, keepdims=True)
            dmod_ref[0, 1:2, :] += jnp.sum(du * yv, axis=0, keepdims=True)
            dy = du * (1.0 + scale)
            dnw_ref[...] += jnp.sum(dy * xn, axis=0, keepdims=True)
            dxn = dy * nw
            gx_ref[0, rows, :] = dh_ref[0, rows, :] + r * (dxn - xn * jnp.mean(dxn * xn, axis=-1, keepdims=True))
        gw_acc[...] += _dot_tn(jnp.concatenate(dps, axis=0), jnp.concatenate(us, axis=0))

        @pl.when((b == bl - 1) & (i == nt - 1))
        def _():
            gw_acc[DA:DA + 2 * KVW, :] += gwkv_ref[...]

            def to_bf16(j, carry):
                rows = pl.ds(pl.multiple_of(j * 2 * KVW, 2 * KVW), 2 * KVW)
                win_ref[rows, :] = gw_acc[rows, :].astype(BF16)
                return carry

            lax.fori_loop(0, D_IN // (2 * KVW), to_bf16, 0)
            pltpu.sync_copy(win_ref, gw_hbm)

    def tile(w):
        return pl.BlockSpec((1, tm, w), lambda b, i: (b, i, 0))

    def const(shape):
        return pl.BlockSpec(shape, lambda b, i: (0,) * len(shape))

    anyspace = pl.BlockSpec(memory_space=pl.ANY)
    rope = pl.BlockSpec((tm, KVW), lambda b, i: (i, 0))
    kv_tile = pl.BlockSpec((1, KVW // HD, tm, HD), lambda b, i: (b, 0, i, 0))
    return pl.pallas_call(
        body, name="bwd_in", grid=(bl, nt),
        in_specs=[tile(D), pl.BlockSpec((1, 3, D), lambda b, i: (b, 0, 0)), const((1, D)), anyspace, rope, rope,
                  const((1, DA)), const((1, KVW)), const((DA, DA)),
                  tile(DA), tile(2 * KVW), tile(DA), kv_tile, kv_tile, tile(DA), tile(2 * DC), tile(DC), tile(D),
                  const((2 * KVW, D))],
        out_specs=[tile(D), anyspace, pl.BlockSpec((1, 2, D), lambda b, i: (b, 0, 0)), const((1, D)),
                   const((1, DA)), const((1, KVW))],
        out_shape=[jax.ShapeDtypeStruct((bl, s, D), F32), jax.ShapeDtypeStruct((D_IN, D), BF16),
                   jax.ShapeDtypeStruct((bl, 2, D), F32), jax.ShapeDtypeStruct((1, D), F32),
                   jax.ShapeDtypeStruct((1, DA), F32), jax.ShapeDtypeStruct((1, KVW), F32)],
        scratch_shapes=[pltpu.VMEM((D_IN, D), BF16), pltpu.VMEM((D_IN, D), F32), pltpu.SemaphoreType.DMA],
        compiler_params=_params(("arbitrary", "arbitrary")),
    )(x, modrows, norm_w, w_in_b, cos, sins, qnw_t, knw_t, ones_bd,
      pq, pkv, dq, dk, dv, dza, dglu, dzc, dh, gw_kv)


_LOSS, _DMODC, _NW, _QN, _KN, _CB, _LW, _LB, _BPW, SMALL_W = 0, 1024, 4096, 5120, 5248, 5376, 5888, 6400, 6912, 7424


ROW_W = 1792


def _put_flat(ref, off, value):
    n, done = value.shape[1], 0
    while done < n:
        r, c = divmod(off + done, ROW_W)
        take = min(n - done, ROW_W - c)
        ref[r:r + 1, c:c + take] = value[:, done:done + take]
        done += take


def _get_flat(arr, off, n):
    parts, done = [], 0
    while done < n:
        r, c = divmod(off + done, ROW_W)
        take = min(n - done, ROW_W - c)
        parts.append(arr[r:r + 1, c:c + take])
        done += take
    return parts[0] if len(parts) == 1 else jnp.concatenate(parts, axis=-1)


def _pack_small_body(loss_ref, ctx_ref, dnw_ref, dqnw_ref, dknw_ref, dknwc_ref, conv_ref, dss_ref, dgate_ref, o_ref):
    bl = dss_ref.shape[0]
    assert SMALL_W + bl * 3 * D <= 8 * ROW_W
    o_ref[...] = jnp.zeros_like(o_ref)
    _put_flat(o_ref, _LOSS, loss_ref[...])
    _put_flat(o_ref, _DMODC, ctx_ref[0:1, :])
    _put_flat(o_ref, _DMODC + D, ctx_ref[1:2, :])
    _put_flat(o_ref, _NW, dnw_ref[...] + ctx_ref[2:3, :])
    dq = dqnw_ref[...]
    qn = dq[:, 0:HD]
    for h in range(1, DA // HD):
        qn = qn + dq[:, h * HD:(h + 1) * HD]
    _put_flat(o_ref, _QN, qn)
    dk = dknw_ref[...] + dknwc_ref[...]
    _put_flat(o_ref, _KN, dk[:, 0:HD] + dk[:, HD:2 * HD])
    _put_flat(o_ref, _BPW, conv_ref[0:1, :])
    _put_flat(o_ref, _LW, conv_ref[1:2, :])
    _put_flat(o_ref, _LB, conv_ref[2:3, :])
    _put_flat(o_ref, _CB, conv_ref[3:4, :])
    for b in range(bl):
        _put_flat(o_ref, SMALL_W + b * 3 * D, dss_ref[b, 0:1, :])
        _put_flat(o_ref, SMALL_W + b * 3 * D + D, dss_ref[b, 1:2, :])
        _put_flat(o_ref, SMALL_W + b * 3 * D + 2 * D, dgate_ref[b])


_SMALL = (("b_mod", None), ("norm_w", _NW), ("q_norm_w", _QN), ("k_norm_w", _KN), ("conv_b", _CB),
          ("conv_ln_w", _LW), ("conv_ln_b", _LB), ("b_pw", _BPW), ("c_ctx", None))


def _epilogue(parts_in, pieces, c_rows, w_mod_loc):
    bl = pieces[7].shape[0]
    n_ex = N_DEV * bl
    n_mod = w_mod_loc.shape[1]
    rb = 32
    shp = parts_in.shape[1:]
    rows_in = shp[0]

    def body(*refs):
        it = iter(refs)
        take = lambda k: [next(it) for _ in range(k)]
        (parts,) = take(1)
        piece_refs = take(9)
        (c_ref, wm_ref) = take(2)
        (g_in, g_wm, sum_ref, gb_ref, gc_all, loss_ref) = take(6)
        (mine, got_sib, stage, got_chip, payload, gathered, dmod_full, gc_mine) = take(8)
        (d2d_send, d2d_recv, ici_send, ici_recv, local_sems, sg_send, sg_recv, gc_send, gc_recv, misc_sems) = take(10)

        x, y, c = _coords()
        me = _lin(x, y, c)
        sib = (x, y, 1 - c)
        home = 2 * x + y

        def rows_loop(fn):
            def step(i, carry):
                fn(pl.ds(pl.multiple_of(i * rb, rb), rb))
                return carry
            lax.fori_loop(0, rows_in // rb, step, 0)

        def direct_gather(src, dst, send_sems, recv_sems, local_sem):
            cps = [pltpu.make_async_copy(src, dst.at[me], local_sem)]
            for k in range(1, N_DEV):
                peer = (1 - x if k & 4 else x, 1 - y if k & 2 else y, 1 - c if k & 1 else c)
                cps.append(pltpu.make_async_remote_copy(
                    src_ref=src, dst_ref=dst.at[me], send_sem=send_sems.at[k - 1], recv_sem=recv_sems.at[k - 1],
                    device_id=peer, device_id_type=MESH_ID))
            for cp in cps:
                cp.start()
            return cps

        _pack_small_body(*piece_refs, payload)
        small_cps = direct_gather(payload, gathered, sg_send, sg_recv, misc_sems.at[0])

        local, d2d, ici = [], [], []
        for s in range(4):
            cp = pltpu.make_async_copy(parts.at[_lin(s // 2, s % 2, c)], mine.at[s], local_sems.at[s])
            cp.start()
            local.append(cp)
            rc = pltpu.make_async_remote_copy(
                src_ref=parts.at[_lin(s // 2, s % 2, 1 - c)], dst_ref=got_sib.at[s],
                send_sem=d2d_send.at[s], recv_sem=d2d_recv.at[s], device_id=sib, device_id_type=MESH_ID)
            rc.start()
            d2d.append(rc)

        for cp in small_cps[1:]:
            cp.wait_recv()
        small_cps[0].wait()
        tot = gathered[0]
        for j in range(1, N_DEV):
            tot = tot + gathered[j]
        summed = _get_flat(tot, 0, SMALL_W)
        dmod_full[...] = jnp.zeros_like(dmod_full)
        for j in range(N_DEV):
            arr = gathered[j]
            for b in range(bl):
                dmod_full[j * bl + b:j * bl + b + 1, :] = _get_flat(arr, SMALL_W + b * 3 * D, 3 * D)
        dmod_full[n_ex:n_ex + 1, :] = summed[:, _DMODC:_DMODC + 3 * D]
        sum_ref[...] = summed
        gb_ref[...] = jnp.sum(dmod_full[...], axis=0, keepdims=True)
        loss_ref[...] = (0.5 / D) * jnp.sum(summed[:, _LOSS:_LOSS + D], axis=-1, keepdims=True)

        north = c == 1
        first = (jnp.where(north, 1 - x, x), jnp.where(north, y, 1 - y))
        second = (jnp.where(north, x, 1 - x), jnp.where(north, 1 - y, y))
        for s in range(4):
            local[s].wait()
            d2d[s].wait_recv()

        def chip_sum(k, chip, relayed):
            slot = 2 * chip[0] + chip[1]

            def pair_sum(rs):
                acc = mine[slot, rs, :].astype(F32) + got_sib[slot, rs, :].astype(F32)
                if relayed:
                    acc = acc + got_chip[1, rs, :].astype(F32)
                stage[k, rs, :] = acc.astype(BF16)

            rows_loop(pair_sum)

        def send(k, to):
            rc = pltpu.make_async_remote_copy(
                src_ref=stage.at[k], dst_ref=got_chip.at[k], send_sem=ici_send.at[k], recv_sem=ici_recv.at[k],
                device_id=(to[0], to[1], c), device_id_type=MESH_ID)
            rc.start()
            ici.append(rc)

        chip_sum(0, first, False)
        send(0, first)
        chip_sum(1, (1 - x, 1 - y), False)
        send(1, first)

        cr = c_ref[...]
        act = (cr * _sigmoid(cr)).astype(BF16)
        dm = dmod_full[:, pl.ds(pl.multiple_of(me * n_mod, 128), n_mod)].astype(BF16)
        g_wm[...] = _dot_tn(act, dm)
        gc_mine[...] = _dot_nt(dm[n_ex:n_ex + 8, :], wm_ref[...].astype(BF16))
        gc_cps = direct_gather(gc_mine, gc_all, gc_send, gc_recv, misc_sems.at[1])

        ici[1].wait_recv()
        chip_sum(2, second, True)
        send(2, second)
        ici[0].wait_recv()
        ici[2].wait_recv()

        def finish(rs):
            gsum = mine[home, rs, :].astype(F32) + got_sib[home, rs, :].astype(F32)
            g_in[rs, :] = gsum + got_chip[0, rs, :].astype(F32) + got_chip[2, rs, :].astype(F32)

        rows_loop(finish)

        for cp in gc_cps[1:]:
            cp.wait_recv()
        gc_cps[0].wait()
        for rc in d2d + ici + small_cps[1:] + gc_cps[1:]:
            rc.wait_send()

    vm = pl.BlockSpec(memory_space=pltpu.VMEM)
    anyspace = pl.BlockSpec(memory_space=pl.ANY)
    assert rows_in % rb == 0 and parts_in.dtype == BF16
    args = [parts_in, *pieces, c_rows, w_mod_loc]
    in_specs = [anyspace] + [vm] * (len(args) - 1)
    out_shape = [jax.ShapeDtypeStruct(shp, F32), jax.ShapeDtypeStruct(w_mod_loc.shape, F32),
                 jax.ShapeDtypeStruct((1, SMALL_W), F32), jax.ShapeDtypeStruct((1, 3 * D), F32),
                 jax.ShapeDtypeStruct((N_DEV, 8, D), F32), jax.ShapeDtypeStruct((1, 1), F32)]
    scratch = [pltpu.VMEM((4,) + shp, BF16), pltpu.VMEM((4,) + shp, BF16), pltpu.VMEM((3,) + shp, BF16),
               pltpu.VMEM((3,) + shp, BF16), pltpu.VMEM((8, ROW_W), F32), pltpu.VMEM((N_DEV, 8, ROW_W), F32),
               pltpu.VMEM((n_ex + 8, 3 * D), F32), pltpu.VMEM((8, D), F32),
               pltpu.SemaphoreType.DMA((4,)), pltpu.SemaphoreType.DMA((4,)), pltpu.SemaphoreType.DMA((3,)),
               pltpu.SemaphoreType.DMA((3,)), pltpu.SemaphoreType.DMA((4,)),
               pltpu.SemaphoreType.DMA((N_DEV - 1,)), pltpu.SemaphoreType.DMA((N_DEV - 1,)),
               pltpu.SemaphoreType.DMA((N_DEV - 1,)), pltpu.SemaphoreType.DMA((N_DEV - 1,)),
               pltpu.SemaphoreType.DMA((2,))]
    return pl.pallas_call(
        body, name="epilogue", out_shape=out_shape, in_specs=in_specs, out_specs=[vm] * len(out_shape),
        scratch_shapes=scratch, compiler_params=pltpu.CompilerParams(vmem_limit_bytes=VMEM_LIMIT),
    )(*args)


def _final_adamw(g_in, w_in_t, m_in_t, v_in_t, g_wm, w_mod_loc, m_mod, v_mod, summed, g_bmod, gc_all,
                 small_w, small_m, small_v):
    ns = len(_SMALL)
    rb = 32

    def body(*refs):
        it = iter(refs)
        take = lambda k: [next(it) for _ in range(k)]
        (gin_ref, w_ref, m_ref, v_ref, gwm_ref, wm_ref, mm_ref, vm_ref, sum_ref, gb_ref, gc_ref) = take(11)
        sw, sm, sv = take(ns), take(ns), take(ns)
        (d_in, nm_in, nv_in, d_wm, nm_wm, nv_wm) = take(6)
        souts = take(4 * ns)

        def big(g_r, w_r, m_r, v_r, d_o, nm_o, nv_o):
            def step(i, carry):
                rs = pl.ds(pl.multiple_of(i * rb, rb), rb)
                dl, m_new, v_new = _adamw(w_r[rs, :], g_r[rs, :], m_r[rs, :], v_r[rs, :])
                d_o[rs, :] = dl
                nm_o[rs, :] = m_new
                nv_o[rs, :] = v_new
                return carry
            lax.fori_loop(0, w_r.shape[0] // rb, step, 0)

        big(gin_ref, w_ref, m_ref, v_ref, d_in, nm_in, nv_in)
        big(gwm_ref, wm_ref, mm_ref, vm_ref, d_wm, nm_wm, nv_wm)
        for k, (name, off) in enumerate(_SMALL):
            w = sw[k][...]
            if name == "b_mod":
                gk = gb_ref[...]
            elif name == "c_ctx":
                acc = gc_ref[0, 0:1, :]
                for j in range(1, N_DEV):
                    acc = acc + gc_ref[j, 0:1, :]
                sg = _sigmoid(w)
                gk = acc * (sg * (1.0 + w * (1.0 - sg)))
            else:
                gk = sum_ref[:, off:off + w.shape[1]]
            dl, m_new, v_new = _adamw(w, gk, sm[k][...], sv[k][...])
            souts[k][...] = gk
            souts[ns + k][...] = dl
            souts[2 * ns + k][...] = m_new
            souts[3 * ns + k][...] = v_new

    assert w_in_t.shape[0] % rb == 0 and w_mod_loc.shape[0] % rb == 0
    big_shape = jax.ShapeDtypeStruct(w_in_t.shape, F32)
    mod_shape = jax.ShapeDtypeStruct(w_mod_loc.shape, F32)
    out_shape = [big_shape] * 3 + [mod_shape] * 3 + [jax.ShapeDtypeStruct(w.shape, F32) for w in small_w] * 4
    outs = pl.pallas_call(
        body, name="final_adamw", out_shape=out_shape,
        compiler_params=pltpu.CompilerParams(vmem_limit_bytes=VMEM_LIMIT),
    )(g_in, w_in_t, m_in_t, v_in_t, g_wm, w_mod_loc, m_mod, v_mod, summed, g_bmod, gc_all, *small_w, *small_m, *small_v)
    small_outs = [outs[6 + k * ns:6 + (k + 1) * ns] for k in range(4)]
    return (g_in,) + tuple(outs[0:3]), (g_wm,) + tuple(outs[3:6]), small_outs


def _rope_tables(s):
    t = jnp.arange(s, dtype=jnp.int32)
    row = (t // GRID_W).astype(F32)
    col = (t % GRID_W).astype(F32)
    freqs = ROPE_THETA ** (-jnp.arange(0, HD // 2, 2, dtype=F32) / (HD // 2))
    ang_r = row[:, None] * freqs[None, :]
    ang_c = col[:, None] * freqs[None, :]
    cr, sr, cc, sc = jnp.cos(ang_r), jnp.sin(ang_r), jnp.cos(ang_c), jnp.sin(ang_c)
    cos = jnp.concatenate([cr, cr, cc, cc], axis=-1)
    sins = jnp.concatenate([-sr, sr, -sc, sc], axis=-1)
    return jnp.tile(cos, (1, KVW // HD)), jnp.tile(sins, (1, KVW // HD))


def kernel(x, c, ctx, c_ctx, w_mod, b_mod, norm_w, w_in, q_norm_w, k_norm_w, conv_w, conv_b, conv_ln_w, conv_ln_b, w_pw, b_pw, w_out, loss_target, m_c_ctx, m_w_mod, m_b_mod, m_norm_w, m_w_in, m_q_norm_w, m_k_norm_w, m_conv_w, m_conv_b, m_conv_ln_w, m_conv_ln_b, m_w_pw, m_b_pw, m_w_out, v_c_ctx, v_w_mod, v_b_mod, v_norm_w, v_w_in, v_q_norm_w, v_k_norm_w, v_conv_w, v_conv_b, v_conv_ln_w, v_conv_ln_b, v_w_pw, v_b_pw, v_w_out):
    bl, s, _ = x.shape
    cl = ctx.shape[1]
    me = _lin(*_coords())

    conv_w_pad = jnp.pad(conv_w[0], ((0, 32 - KW), (0, 0)))
    n_ex = N_DEV * bl
    g_win, c_rows, g_mod = _prologue(w_in[0].T, c, c_ctx[None, :], w_mod[0], b_mod)
    w_in_b = g_win.reshape(D_IN, D)
    mod_all = g_mod.transpose(1, 0, 2).reshape(n_ex + 8, 3 * D)
    modrows = lax.dynamic_slice_in_dim(mod_all, me * bl, bl, axis=0).reshape(bl, 3, D)
    modc = mod_all[n_ex].reshape(1, 3, D)

    cos, sins = _rope_tables(s)
    qnw_t = jnp.tile(q_norm_w, (1, DA // HD))
    knw_t = jnp.tile(k_norm_w, (1, KVW // HD))
    lane = jnp.arange(DA, dtype=jnp.int32) // HD
    ones_bd = (lane[:, None] == lane[None, :]).astype(BF16)
    ones_kv = ones_bd[0:KVW, 0:KVW]
    w_kv_b = w_in_b

    k_ctx, v_ctx, pkv_c = _ctx_fwd(ctx, modc, norm_w, w_kv_b, knw_t, ones_kv, cl + s)
    (q_h, k_h, v_h, pq, pkv, za, glu, zc), (g_wout, g_wpw, g_cw) = _fwd_in(
        x, modrows, norm_w, w_in_b, cos, sins, qnw_t, knw_t, ones_bd, k_ctx, v_ctx,
        [w_out[0], w_pw[0], conv_w_pad], [BF16, BF16, F32])
    w_out_b = g_wout.reshape(D, D)
    w_pw_b = g_wpw.reshape(DC, DC)
    conv_w_full = g_cw.transpose(1, 0, 2).reshape(32, DC)
    attn, lse = _attn_fwd(q_h, k_h, v_h)
    y_conv, cp = _conv_fwd(glu, conv_w_full, conv_b, conv_ln_w, conv_ln_b, w_pw_b, b_pw)

    do_h, dza, dy_conv, dzc, dh, dgate, gw_out, loss_row, gw_pw, conv_rows = _out_fwd_bwd(
        attn, za, cp, zc, x, loss_target, modrows, w_out_b, y_conv, conv_ln_w, conv_ln_b, w_pw_b)
    dglu, g_cw_full = _conv_bwd_depthwise(glu, dy_conv, conv_w_full)
    parts_out = gw_out.reshape(N_DEV, D // N_DEV, D)
    parts_pw = gw_pw.reshape(N_DEV, DC // N_DEV, DC)
    parts_cw = g_cw_full.astype(BF16).reshape(32, N_DEV, DC // N_DEV).transpose(1, 0, 2)
    (dq, dk_h, dv_h), (got_out, got_pw, got_cw) = _attn_bwd(
        q_h, k_h, v_h, do_h, attn, lse, [parts_out, parts_pw, parts_cw])
    gw_kv, ctx_rows, dknw_c = _ctx_bwd(ctx, modc, norm_w, w_kv_b, pkv_c, dk_h, dv_h, knw_t, ones_kv)
    grad_x, gw_in, dmod_ss, dnw, dqnw, dknw = _bwd_in(
        x, modrows, norm_w, w_in_b, cos, sins, qnw_t, knw_t, ones_bd,
        pq, pkv, dq, dk_h, dv_h, dza, dglu, dzc, dh, gw_kv)

    r_out, r_pw, r_cw = _sum_devices_adamw(
        [(got_out, w_out[0], m_w_out[0], v_w_out[0]), (got_pw, w_pw[0], m_w_pw[0], v_w_pw[0]),
         (got_cw, conv_w[0], m_conv_w[0], v_conv_w[0])])

    given = {"c_ctx": (c_ctx, m_c_ctx, v_c_ctx), "b_mod": (b_mod, m_b_mod, v_b_mod), "norm_w": (norm_w, m_norm_w, v_norm_w),
             "q_norm_w": (q_norm_w, m_q_norm_w, v_q_norm_w), "k_norm_w": (k_norm_w, m_k_norm_w, v_k_norm_w),
             "conv_b": (conv_b, m_conv_b, v_conv_b), "conv_ln_w": (conv_ln_w, m_conv_ln_w, v_conv_ln_w),
             "conv_ln_b": (conv_ln_b, m_conv_ln_b, v_conv_ln_b), "b_pw": (b_pw, m_b_pw, v_b_pw)}
    as_rows = [[given[name][which].reshape(1, -1) for name, _ in _SMALL] for which in range(3)]
    g_in_t, g_wmod, summed, g_bmod, gc_all, loss11 = _epilogue(
        gw_in.reshape(N_DEV, D_IN // N_DEV, D),
        [loss_row, ctx_rows, dnw, dqnw, dknw, dknw_c, conv_rows, dmod_ss, dgate], c_rows, w_mod[0])
    r_in, r_wmod, small_outs = _final_adamw(
        g_in_t, w_in[0].T, m_w_in[0].T, v_w_in[0].T, g_wmod, w_mod[0], m_w_mod[0], v_w_mod[0],
        summed, g_bmod, gc_all, *as_rows)
    r_in = tuple(a.T for a in r_in)

    big = {"w_mod": r_wmod, "w_in": r_in, "conv_w": r_cw, "w_pw": r_pw, "w_out": r_out}
    order = ["c_ctx", "w_mod", "b_mod", "norm_w", "w_in", "q_norm_w", "k_norm_w", "conv_w", "conv_b", "conv_ln_w",
             "conv_ln_b", "w_pw", "b_pw", "w_out"]
    small_index = {name: k for k, (name, _) in enumerate(_SMALL)}
    outs = [loss11.reshape(()), grad_x]
    for which in range(4):
        for name in order:
            if name in big:
                outs.append(big[name][which][None])
            else:
                outs.append(small_outs[which][small_index[name]].reshape(given[name][0].shape))
    return tuple(outs)
```

```python
import jax
import jax.numpy as jnp
from jax import lax
from jax.experimental import pallas as pl
from jax.experimental.pallas import tpu as pltpu

F32, BF16 = jnp.float32, jnp.bfloat16
MESH_ID = pl.DeviceIdType.MESH

N_DEV = 8
D = 1024
D_IN = 2816
DA = 512
DC = 512
HD = 64
KVW = 128
KW = 31
HALO = 16
EPS = 1e-6
ROPE_THETA = 10000.0
GRID_W = 64

ADAM_LR, ADAM_B1, ADAM_B2, ADAM_EPS, ADAM_WD, ADAM_STEP = 0.001, 0.9, 0.999, 1e-08, 0.01, 10

VMEM_LIMIT = 56 * 1024 * 1024

TM = 256
TQ = 128
TOKEN_PARTS = 2
OUT_TM = 512
BWD_PARTS = 2
FWD_PARTS = 4
TC = 512
CH = 32
ADAM_STEPS = 4


def _params(sem, vmem=VMEM_LIMIT):
    return pltpu.CompilerParams(dimension_semantics=sem, vmem_limit_bytes=vmem)


def _dot(a, b):
    return jnp.dot(a, b, preferred_element_type=F32)


def _dot_nt(a, b):
    return lax.dot_general(a, b, (((1,), (1,)), ((), ())), preferred_element_type=F32)


def _dot_tn(a, b):
    return lax.dot_general(a, b, (((0,), (0,)), ((), ())), preferred_element_type=F32)


def _sigmoid(z):
    return 1.0 / (1.0 + jnp.exp(-z))


def _segsum(v, ones_bd):
    return _dot(v.astype(BF16), ones_bd)


def _swap16(x):
    w = x.shape[-1]
    lane = lax.broadcasted_iota(jnp.int32, x.shape, 1)
    return jnp.where((lane % 32) < 16, pltpu.roll(x, w - 16, 1), pltpu.roll(x, 16, 1))


def _with_ones_column(v):
    one = (lax.broadcasted_iota(jnp.int32, v.shape, 1) == 0).astype(v.dtype)
    return jnp.concatenate([v, one], axis=-1)


def _rope(x, cos, sins):
    return x * cos + _swap16(x) * sins


def _rope_bwd(d, cos, sins):
    return d * cos + _swap16(d * sins)


def _adamw(w, g, m, v):
    m2 = ADAM_B1 * m + (1.0 - ADAM_B1) * g
    v2 = ADAM_B2 * v + (1.0 - ADAM_B2) * (g * g)
    m_hat = m2 / (1.0 - ADAM_B1 ** ADAM_STEP)
    v_hat = v2 / (1.0 - ADAM_B2 ** ADAM_STEP)
    delta = -ADAM_LR * (m_hat / (jnp.sqrt(v_hat) + ADAM_EPS) + ADAM_WD * w)
    return delta, m2, v2


def _coords():
    return lax.axis_index("x"), lax.axis_index("y"), lax.axis_index("c")


def _lin(x, y, c):
    return 4 * x + 2 * y + c


def _prologue(w_in_t, c, c_ctx_row, w_mod_loc, b_mod):
    bl = c.shape[0]
    n_ex = N_DEV * bl
    n_mod = w_mod_loc.shape[1]

    def body(w32_ref, c_in_ref, cctx_ref, wm_ref, b_ref, out_w, crows_ref, mod_out, w_ref, c_ref, c_gath, mod_mine,
             w_send, w_recv, c_send, c_recv, m_send, m_recv, local_sems):
        x, y, c = _coords()
        me_lin = _lin(x, y, c)
        c_ref[...] = jnp.zeros_like(c_ref)
        c_ref[0:bl, :] = c_in_ref[...]
        w_ref[...] = w32_ref[...].astype(BF16)
        me, sib = (x, y, c), (x, y, 1 - c)
        xnb, ynb, diag = (1 - x, y), (x, 1 - y), (1 - x, 1 - y)
        north = c == 1

        def direct_gather(src, dst, send_sems, recv_sems, local_sem):
            cps = [pltpu.make_async_copy(src, dst.at[me_lin], local_sem)]
            for k in range(1, N_DEV):
                peer = (1 - x if k & 4 else x, 1 - y if k & 2 else y, 1 - c if k & 1 else c)
                cps.append(pltpu.make_async_remote_copy(
                    src_ref=src, dst_ref=dst.at[me_lin], send_sem=send_sems.at[k - 1], recv_sem=recv_sems.at[k - 1],
                    device_id=peer, device_id_type=MESH_ID))
            for cp in cps:
                cp.start()
            return cps

        def copy(k, block, to, src=None):
            slot = out_w.at[_lin(*block)]
            return pltpu.make_async_remote_copy(
                src_ref=slot if src is None else src, dst_ref=slot, send_sem=w_send.at[k], recv_sem=w_recv.at[k],
                device_id=to, device_id_type=MESH_ID)

        c_cps = direct_gather(c_ref, c_gath, c_send, c_recv, local_sems.at[0])
        mine = pltpu.make_async_copy(w_ref, out_w.at[me_lin], local_sems.at[1])
        mine.start()
        first = [copy(0, me, sib, src=w_ref), copy(1, me, (*xnb, c), src=w_ref), copy(2, me, (*ynb, c), src=w_ref)]
        for cp in first:
            cp.start()

        for cp in c_cps[1:]:
            cp.wait_recv()
        c_cps[0].wait()
        crows_ref[...] = jnp.zeros_like(crows_ref)
        for j in range(N_DEV):
            crows_ref[j * bl:(j + 1) * bl, :] = c_gath[j, 0:bl, :]
        crows_ref[n_ex:n_ex + 1, :] = cctx_ref[...]
        cr = crows_ref[...]
        act = (cr * _sigmoid(cr)).astype(BF16)
        mod_mine[...] = _dot(act, wm_ref[...].astype(BF16)) + b_ref[:, pl.ds(pl.multiple_of(me_lin * n_mod, 128), n_mod)]
        mod_cps = direct_gather(mod_mine, mod_out, m_send, m_recv, local_sems.at[2])

        relay_north = copy(3, (*xnb, c), (*ynb, c))
        relay_south = copy(3, (*ynb, c), (*xnb, c))
        passed = []
        copy(1, (*xnb, c), me).wait_recv()
        pl.when(north)(relay_north.start)
        passed.append(copy(4, (*xnb, c), sib))
        passed[-1].start()
        copy(2, (*ynb, c), me).wait_recv()
        pl.when(jnp.logical_not(north))(relay_south.start)
        passed.append(copy(5, (*ynb, c), sib))
        passed[-1].start()
        copy(3, (*diag, c), me).wait_recv()
        passed.append(copy(6, (*diag, c), sib))
        passed[-1].start()
        copy(0, sib, me).wait_recv()
        for k, chip in ((4, xnb), (5, ynb), (6, diag)):
            copy(k, (*chip, 1 - c), me).wait_recv()
        for cp in mod_cps[1:]:
            cp.wait_recv()
        mod_cps[0].wait()
        for cp in first + passed + [relay_north] + c_cps[1:] + mod_cps[1:]:
            cp.wait_send()
        mine.wait()

    vm = pl.BlockSpec(memory_space=pltpu.VMEM)
    seven = pltpu.SemaphoreType.DMA((N_DEV - 1,))
    return pl.pallas_call(
        body, name="prologue",
        out_shape=[jax.ShapeDtypeStruct((N_DEV,) + w_in_t.shape, BF16), jax.ShapeDtypeStruct((n_ex + 8, D), F32),
                   jax.ShapeDtypeStruct((N_DEV, n_ex + 8, n_mod), F32)],
        in_specs=[vm] * 5, out_specs=[vm] * 3,
        scratch_shapes=[pltpu.VMEM(w_in_t.shape, BF16), pltpu.VMEM((8, D), F32), pltpu.VMEM((N_DEV, 8, D), F32),
                        pltpu.VMEM((n_ex + 8, n_mod), F32),
                        seven, seven, seven, seven, seven, seven, pltpu.SemaphoreType.DMA((3,))],
        compiler_params=pltpu.CompilerParams(vmem_limit_bytes=VMEM_LIMIT),
    )(w_in_t, c, c_ctx_row, w_mod_loc, b_mod)


def _exchange_copies(in_refs, out_refs, send_sems, recv_sems, local_sems, scatter):
    x, y, c = _coords()
    me = _lin(x, y, c)
    local, remote = [], []
    for a, (src, dst) in enumerate(zip(in_refs, out_refs)):
        local.append(pltpu.make_async_copy(src.at[me] if scatter else src, dst.at[me], local_sems.at[a]))
        for k in range(1, N_DEV):
            peer = (1 - x if k & 4 else x, 1 - y if k & 2 else y, 1 - c if k & 1 else c)
            remote.append(pltpu.make_async_remote_copy(
                src_ref=src.at[_lin(*peer)] if scatter else src, dst_ref=dst.at[me],
                send_sem=send_sems.at[a * (N_DEV - 1) + k - 1], recv_sem=recv_sems.at[a * (N_DEV - 1) + k - 1],
                device_id=peer, device_id_type=MESH_ID))
    return local, remote


def _exchange_scratch(n):
    return [pltpu.SemaphoreType.DMA((n * (N_DEV - 1),)), pltpu.SemaphoreType.DMA((n * (N_DEV - 1),)),
            pltpu.SemaphoreType.DMA((n,))]


def _sum_devices_adamw(items):
    n = len(items)

    def body(*refs):
        for a in range(n):
            got, w_ref, m_ref, v_ref = refs[4 * a:4 * a + 4]
            g_ref, d_ref, nm_ref, nv_ref = refs[4 * n + 4 * a:4 * n + 4 * a + 4]
            g = got[0].astype(F32)
            for j in range(1, N_DEV):
                g = g + got[j].astype(F32)
            g = g[0:w_ref.shape[0], :]
            delta, m2, v2 = _adamw(w_ref[...], g, m_ref[...], v_ref[...])
            g_ref[...] = g
            d_ref[...] = delta
            nm_ref[...] = m2
            nv_ref[...] = v2

    args, out_shape = [], []
    for got, w, m, v in items:
        assert got.shape[0] == N_DEV and got.shape[1] >= w.shape[0] and got.shape[2:] == w.shape[1:]
        args += [got, w, m, v]
        out_shape += [jax.ShapeDtypeStruct(w.shape, F32)] * 4
    outs = pl.pallas_call(body, name="sum_devices_adamw", out_shape=out_shape,
                          compiler_params=pltpu.CompilerParams(vmem_limit_bytes=VMEM_LIMIT))(*args)
    return [tuple(outs[4 * a:4 * a + 4]) for a in range(n)]


def _fwd_in(x, modrows, norm_w, w_in_b, cos, sins, qnw_t, knw_t, ones_bd, k_all, v_all, shards, wire_dtypes):
    bl, s, _ = x.shape
    tm = TOKEN_PARTS * TM
    nt = s // tm
    n_sh = len(shards)

    def body(*refs):
        (x_ref, mod_ref, nw_ref, win_ref, cos_ref, sin_ref, qnw_ref, knw_ref, bd_ref, kin_ref, vin_ref) = refs[:11]
        shard_refs = refs[11:11 + n_sh]
        q_ref, k_ref, v_ref, pq_ref, pkv_ref, za_ref, glu_ref, zc_ref = refs[11 + n_sh:19 + n_sh]
        gathered_refs = refs[19 + n_sh:19 + 2 * n_sh]
        stage_refs = refs[19 + 2 * n_sh:19 + 3 * n_sh]
        send_sems, recv_sems, local_sems = refs[19 + 3 * n_sh:]
        b, i = pl.program_id(0), pl.program_id(1)
        local, remote = _exchange_copies(stage_refs, gathered_refs, send_sems, recv_sems, local_sems, scatter=False)

        @pl.when((b == 0) & (i == 0))
        def _():
            for src, stage in zip(shard_refs, stage_refs):
                stage[...] = src[...].astype(stage.dtype)
            for cp in local + remote:
                cp.start()

        shift = mod_ref[0, 0:1, :]
        scale = mod_ref[0, 1:2, :]
        for part in range(TOKEN_PARTS):
            rows = pl.ds(part * TM, TM)
            xv = x_ref[0, rows, :]
            r = lax.rsqrt(jnp.mean(xv * xv, axis=-1, keepdims=True) + EPS)
            u = (xv * r * nw_ref[...]) * (1.0 + scale) + shift
            p = _dot_nt(u.astype(BF16), win_ref[...])
            pq = p[:, 0:DA]
            pk = p[:, DA:DA + HD * 2]
            ck = cos_ref[rows, :]
            sk = sin_ref[rows, :]
            cs = jnp.concatenate([ck] * (DA // KVW), axis=-1)
            sn = jnp.concatenate([sk] * (DA // KVW), axis=-1)
            rq = lax.rsqrt(_segsum(pq * pq, bd_ref[...]) * (1.0 / HD) + EPS)
            qn = pq * rq * qnw_ref[...]
            qr = _rope(qn, cs, sn) * 0.125
            for h in range(DA // HD):
                q_ref[0, h, rows, :] = qr[:, h * HD:(h + 1) * HD].astype(BF16)
            rk = lax.rsqrt(_segsum(pk * pk, bd_ref[0:KVW, 0:KVW]) * (1.0 / HD) + EPS)
            kn = pk * rk * knw_ref[...]
            kr = _rope(kn, ck, sk)
            pv = p[:, 640:768]
            for h in range(KVW // HD):
                k_ref[0, h, rows, :] = kr[:, h * HD:(h + 1) * HD].astype(BF16)
                v_ref[0, h, rows, :] = _with_ones_column(pv[:, h * HD:(h + 1) * HD]).astype(BF16)
            pq_ref[0, rows, :] = pq
            pkv_ref[0, rows, :] = p[:, 512:768]
            za_ref[0, rows, :] = p[:, 768:1280]
            glu_ref[0, rows, :] = p[:, 1280:2304]
            zc_ref[0, rows, :] = p[:, 2304:2816]

        @pl.when((b == bl - 1) & (i == nt - 1))
        def _():
            for cp in remote:
                cp.wait_recv()
            for cp in remote:
                cp.wait_send()
            for cp in local:
                cp.wait()

    def tile(w):
        return pl.BlockSpec((1, tm, w), lambda b, i: (b, i, 0))

    def const(shape):
        return pl.BlockSpec(shape, lambda b, i: (0,) * len(shape))

    outs = [(DA, F32), (2 * KVW, F32), (DA, F32), (2 * DC, F32), (DC, F32)]
    anyspace = pl.BlockSpec(memory_space=pl.ANY)
    rope = pl.BlockSpec((tm, KVW), lambda b, i: (i, 0))
    k_tile = pl.BlockSpec((1, KVW // HD, tm, HD), lambda b, i: (b, 0, i, 0))
    v_tile = pl.BlockSpec((1, KVW // HD, tm, 2 * HD), lambda b, i: (b, 0, i, 0))
    res = pl.pallas_call(
        body, name="fwd_in", grid=(bl, nt),
        in_specs=[tile(D), pl.BlockSpec((1, 3, D), lambda b, i: (b, 0, 0)), const((1, D)), const((D_IN, D)),
                  rope, rope, const((1, DA)), const((1, KVW)), const((DA, DA)), anyspace, anyspace]
        + [const(a.shape) for a in shards],
        out_specs=[pl.BlockSpec((1, DA // HD, tm, HD), lambda b, i: (b, 0, i, 0)), k_tile, v_tile]
        + [tile(w) for w, _ in outs] + [anyspace] * n_sh,
        out_shape=[jax.ShapeDtypeStruct((bl, DA // HD, s, HD), BF16), jax.ShapeDtypeStruct(k_all.shape, BF16),
                   jax.ShapeDtypeStruct(v_all.shape, BF16)]
        + [jax.ShapeDtypeStruct((bl, s, w), dt) for w, dt in outs]
        + [jax.ShapeDtypeStruct((N_DEV,) + a.shape, dt) for a, dt in zip(shards, wire_dtypes)],
        input_output_aliases={9: 1, 10: 2},
        scratch_shapes=[pltpu.VMEM(a.shape, dt) for a, dt in zip(shards, wire_dtypes)] + _exchange_scratch(n_sh),
        compiler_params=_params(("arbitrary", "arbitrary")),
    )(x, modrows, norm_w, w_in_b, cos, sins, qnw_t, knw_t, ones_bd, k_all, v_all, *shards)
    return res[:8], res[8:]


_KV_ROWS_OF_W_IN_T = pl.BlockSpec((2 * KVW, D), lambda b: (DA // (2 * KVW), 0))


def _ctx_fwd(ctx, modc, norm_w, w_kv_b, knw_t, ones_bd, n_keys):
    bl, cl, _ = ctx.shape

    def body(x_ref, mod_ref, nw_ref, w_ref, knw_ref, bd_ref, k_ref, v_ref, pkv_ref):
        xv = x_ref[0]
        shift = mod_ref[0, 0:1, :]
        scale = mod_ref[0, 1:2, :]
        r = lax.rsqrt(jnp.mean(xv * xv, axis=-1, keepdims=True) + EPS)
        u = (xv * r * nw_ref[...]) * (1.0 + scale) + shift
        p = _dot_nt(u.astype(BF16), w_ref[...])
        pk = p[:, 0:KVW]
        rk = lax.rsqrt(_segsum(pk * pk, bd_ref[...]) * (1.0 / HD) + EPS)
        kn = pk * rk * knw_ref[...]
        pv = p[:, KVW:2 * KVW]
        for h in range(KVW // HD):
            k_ref[0, h] = kn[:, h * HD:(h + 1) * HD].astype(BF16)
            v_ref[0, h] = _with_ones_column(pv[:, h * HD:(h + 1) * HD]).astype(BF16)
        pkv_ref[0] = p

    def const(shape):
        return pl.BlockSpec(shape, lambda b: (0,) * len(shape))

    def tile(w):
        return pl.BlockSpec((1, cl, w), lambda b: (b, 0, 0))

    ctx_block = (n_keys - cl) // cl
    assert ctx_block * cl + cl == n_keys
    k_tile = pl.BlockSpec((1, KVW // HD, cl, HD), lambda b: (b, 0, ctx_block, 0))
    v_tile = pl.BlockSpec((1, KVW // HD, cl, 2 * HD), lambda b: (b, 0, ctx_block, 0))
    return pl.pallas_call(
        body, name="ctx_fwd", grid=(bl,),
        in_specs=[tile(D), const((1, 3, D)), const((1, D)), _KV_ROWS_OF_W_IN_T, const((1, KVW)), const((KVW, KVW))],
        out_specs=[k_tile, v_tile, tile(2 * KVW)],
        out_shape=[jax.ShapeDtypeStruct((bl, KVW // HD, n_keys, HD), BF16),
                   jax.ShapeDtypeStruct((bl, KVW // HD, n_keys, 2 * HD), BF16),
                   jax.ShapeDtypeStruct((bl, cl, 2 * KVW), F32)],
        compiler_params=_params(("arbitrary",)),
    )(ctx, modc, norm_w, w_kv_b, knw_t, ones_bd)


def _attn_fwd(q, k, v1):
    bl, _, s, _ = q.shape
    n_keys = k.shape[2]

    def body(q_ref, k_ref, v_ref, o_ref, lse_ref):
        kv = k_ref[0, 0]
        vv = v_ref[0, 0]
        lane = lax.broadcasted_iota(jnp.int32, (TQ, 2 * HD), 1)
        for part in range(FWD_PARTS):
            rows = pl.ds(part * TQ, TQ)
            lse = jnp.zeros((TQ, 2 * HD), F32)
            heads = []
            sc_all = _dot_nt(q_ref[0, :, rows, :].reshape(4 * TQ, HD), kv)
            for h in range(4):
                sc = sc_all[h * TQ:(h + 1) * TQ, :]
                m = jnp.max(sc, axis=-1, keepdims=True)
                e = jnp.exp(sc - m).astype(BF16)
                ov = _dot(e, vv)
                denom = ov[:, HD:HD + 1]
                heads.append(ov[:, 0:HD] * (1.0 / denom))
                lse = jnp.where(lane == h, m + jnp.log(denom), lse)
            o_ref[0, rows, :] = jnp.concatenate(heads, axis=-1)
            lse_ref[0, 0, rows, :] = lse

    tq = FWD_PARTS * TQ
    ks = pl.BlockSpec((1, 1, n_keys, HD), lambda b, g, i: (b, g, 0, 0))
    qs = pl.BlockSpec((1, 4, tq, HD), lambda b, g, i: (b, g, i, 0))
    vs = pl.BlockSpec((1, 1, n_keys, 2 * HD), lambda b, g, i: (b, g, 0, 0))
    return pl.pallas_call(
        body, name="attn_fwd", grid=(bl, 2, s // tq), in_specs=[qs, ks, vs],
        out_specs=[pl.BlockSpec((1, tq, 4 * HD), lambda b, g, i: (b, i, g)),
                   pl.BlockSpec((1, 1, tq, 2 * HD), lambda b, g, i: (b, g, i, 0))],
        out_shape=[jax.ShapeDtypeStruct((bl, s, DA), F32), jax.ShapeDtypeStruct((bl, 2, s, 2 * HD), F32)],
        compiler_params=_params(("arbitrary", "arbitrary", "arbitrary")),
    )(q, k, v1)


def _attn_bwd(q, k, v1, do, o, lse, exchange):
    bl, _, s, _ = q.shape
    n_keys = k.shape[2]
    tq = BWD_PARTS * TQ
    nq = s // tq
    n_ex = len(exchange)

    def body(*refs):
        q_ref, k_ref, v_ref, do_ref, o_ref, lse_ref = refs[:6]
        part_refs = refs[6:6 + n_ex]
        dq_ref, dk_ref, dv_ref = refs[6 + n_ex:9 + n_ex]
        got_refs = refs[9 + n_ex:9 + 2 * n_ex]
        p_sc, ds_sc, dkt, dvt, send_sems, recv_sems, local_sems = refs[9 + 2 * n_ex:]
        i = pl.program_id(2)
        first = (pl.program_id(0) == 0) & (pl.program_id(1) == 0) & (i == 0)
        last = (pl.program_id(0) == bl - 1) & (pl.program_id(1) == 1) & (i == nq - 1)
        local, remote = _exchange_copies(part_refs, got_refs, send_sems, recv_sems, local_sems, scatter=True)

        @pl.when(first)
        def _():
            for cp in local + remote:
                cp.start()

        @pl.when(i == 0)
        def _():
            dkt[...] = jnp.zeros_like(dkt)
            dvt[...] = jnp.zeros_like(dvt)

        kv = k_ref[0, 0]
        vv = v_ref[0, 0][:, 0:HD]
        for part in range(BWD_PARTS):
            tq_rows = pl.ds(part * TQ, TQ)
            lse = lse_ref[0, 0, tq_rows, :]
            ov = o_ref[0, tq_rows, :]
            dqs = []
            q_cat = q_ref[0, :, tq_rows, :].reshape(4 * TQ, HD)
            do_cat = do_ref[0, :, tq_rows, :].reshape(4 * TQ, HD)
            sc_all = _dot_nt(q_cat, kv)
            for h in range(4):
                doh = do_cat[h * TQ:(h + 1) * TQ, :]
                delta = jnp.sum(ov[:, h * HD:(h + 1) * HD] * doh.astype(F32), axis=-1, keepdims=True)
                rows = pl.ds((part * 4 + h) * TQ, TQ)
                p = jnp.exp(sc_all[h * TQ:(h + 1) * TQ, :] - lse[:, h:h + 1])
                ds = (p * (_dot_nt(doh, vv) - delta)).astype(BF16)
                p_sc[rows, :] = p.astype(BF16)
                ds_sc[rows, :] = ds
                dqs.append(_dot(ds, kv) * 0.125)
            dq_ref[0, tq_rows, :] = jnp.concatenate(dqs, axis=-1)
            part_rows = pl.ds(part * 4 * TQ, 4 * TQ)
            dvt[...] += _dot_tn(do_cat, p_sc[part_rows, :])
            dkt[...] += _dot_tn(q_cat, ds_sc[part_rows, :])

        @pl.when(i == nq - 1)
        def _():
            dk_ref[0, 0] = dkt[...].T
            dv_ref[0, 0] = dvt[...].T

        @pl.when(last)
        def _():
            for cp in remote:
                cp.wait_recv()
            for cp in remote:
                cp.wait_send()
            for cp in local:
                cp.wait()

    qs = pl.BlockSpec((1, 4, tq, HD), lambda b, g, i: (b, g, i, 0))
    ks = pl.BlockSpec((1, 1, n_keys, HD), lambda b, g, i: (b, g, 0, 0))
    vs = pl.BlockSpec((1, 1, n_keys, 2 * HD), lambda b, g, i: (b, g, 0, 0))
    os_ = pl.BlockSpec((1, tq, 4 * HD), lambda b, g, i: (b, i, g))
    kshape = jax.ShapeDtypeStruct(k.shape, F32)
    anyspace = pl.BlockSpec(memory_space=pl.ANY)
    res = pl.pallas_call(
        body, name="attn_bwd", grid=(bl, 2, nq),
        in_specs=[qs, ks, vs, qs, os_, pl.BlockSpec((1, 1, tq, 2 * HD), lambda b, g, i: (b, g, i, 0))]
        + [anyspace] * n_ex,
        out_specs=[os_, ks, ks] + [anyspace] * n_ex,
        out_shape=[jax.ShapeDtypeStruct((bl, s, DA), F32), kshape, kshape]
        + [jax.ShapeDtypeStruct(a.shape, a.dtype) for a in exchange],
        scratch_shapes=[pltpu.VMEM((4 * tq, n_keys), BF16), pltpu.VMEM((4 * tq, n_keys), BF16),
                        pltpu.VMEM((HD, n_keys), F32), pltpu.VMEM((HD, n_keys), F32)] + _exchange_scratch(n_ex),
        compiler_params=_params(("arbitrary", "arbitrary", "arbitrary")),
    )(q, k, v1, do, o, lse, *exchange)
    return res[:3], res[3:]


def _halo_specs(width, s):
    per = TC // HALO
    last = s // HALO - 1
    main = pl.BlockSpec((1, TC, width), lambda b, i: (b, i, 0))
    prev = pl.BlockSpec((1, HALO, width), lambda b, i: (b, jnp.maximum(i * per - 1, 0), 0))
    nxt = pl.BlockSpec((1, HALO, width), lambda b, i: (b, jnp.minimum((i + 1) * per, last), 0))
    return main, prev, nxt


def _glu(g):
    return g[:, 0:DC] * _sigmoid(g[:, DC:2 * DC])


def _fill_padded(pad_ref, main, prev, nxt, first, last):
    pad_ref[0:HALO, :] = jnp.where(first, 0.0, prev)
    pad_ref[HALO:HALO + TC, :] = main
    pad_ref[HALO + TC:2 * HALO + TC, :] = jnp.where(last, 0.0, nxt)


PLANE_ROWS = TC + 2 * HALO - 8


def _shift_planes(pad_ref, planes_ref):
    for r in range(1, 8):
        planes_ref[r - 1] = pad_ref[pl.ds(r, PLANE_ROWS), :]


def _tap_rows(pad_ref, planes_ref, offset, start, n):
    a, r = divmod(offset, 8)
    if r == 0:
        return pad_ref[pl.ds(start + 8 * a, n), :]
    return planes_ref[r - 1, pl.ds(start + 8 * a, n), :]


def _conv_fwd(glu, conv_w, conv_b, ln_w, ln_b, w_pw_b, b_pw):
    bl, s, _ = glu.shape
    nt = s // TC

    def body(g_ref, gp_ref, gn_ref, cw_ref, cb_ref, lw_ref, lb_ref, wpw_ref, bpw_ref, y_ref, cp_ref, pad_ref, planes_ref):
        i = pl.program_id(1)
        _fill_padded(pad_ref, _glu(g_ref[0]), _glu(gp_ref[0]), _glu(gn_ref[0]), i == 0, i == nt - 1)
        _shift_planes(pad_ref, planes_ref)
        for ck in range(TC // CH):
            acc = jnp.zeros((CH, DC), F32) + cb_ref[...]
            for t in range(KW):
                acc = acc + _tap_rows(pad_ref, planes_ref, 1 + t, ck * CH, CH) * cw_ref[t:t + 1, :]
            y_ref[0, pl.ds(ck * CH, CH), :] = acc
        y = y_ref[0]
        mu = jnp.mean(y, axis=-1, keepdims=True)
        yc = y - mu
        var = jnp.mean(yc * yc, axis=-1, keepdims=True)
        z = yc * lax.rsqrt(var + EPS) * lw_ref[...] + lb_ref[...]
        act = z * _sigmoid(z)
        cp_ref[0] = _dot(act.astype(BF16), wpw_ref[...]) + bpw_ref[...]

    def const(shape):
        return pl.BlockSpec(shape, lambda b, i: (0,) * len(shape))

    main, prev, nxt = _halo_specs(2 * DC, s)
    tile = pl.BlockSpec((1, TC, DC), lambda b, i: (b, i, 0))
    return pl.pallas_call(
        body, name="conv_fwd", grid=(bl, nt),
        in_specs=[main, prev, nxt, const((32, DC)), const((1, DC)), const((1, DC)), const((1, DC)),
                  const((DC, DC)), const((1, DC))],
        out_specs=[tile, tile],
        out_shape=[jax.ShapeDtypeStruct((bl, s, DC), F32)] * 2,
        scratch_shapes=[pltpu.VMEM((TC + 2 * HALO, DC), F32), pltpu.VMEM((7, PLANE_ROWS, DC), F32)],
        compiler_params=_params(("arbitrary", "arbitrary")),
    )(glu, glu, glu, conv_w, conv_b, ln_w, ln_b, w_pw_b, b_pw)


def _conv_bwd_depthwise(glu, dy, conv_w):
    bl, s, _ = glu.shape
    nt = s // TC

    def body(g_ref, gp_ref, gn_ref, d_ref, dp_ref, dn_ref, cw_ref, dglu_ref, dcw_ref,
             padu_ref, padd_ref, planes_u, planes_d):
        i = pl.program_id(1)

        @pl.when((pl.program_id(0) == 0) & (i == 0))
        def _():
            dcw_ref[...] = jnp.zeros_like(dcw_ref)

        first, last = i == 0, i == nt - 1
        _fill_padded(padu_ref, _glu(g_ref[0]), _glu(gp_ref[0]), _glu(gn_ref[0]), first, last)
        _fill_padded(padd_ref, d_ref[0], dp_ref[0], dn_ref[0], first, last)
        _shift_planes(padu_ref, planes_u)
        _shift_planes(padd_ref, planes_d)
        for ck in range(TC // CH):
            acc = jnp.zeros((CH, DC), F32)
            for t in range(KW):
                acc = acc + _tap_rows(padd_ref, planes_d, 2 * HALO - 1 - t, ck * CH, CH) * cw_ref[t:t + 1, :]
            g = g_ref[0, pl.ds(ck * CH, CH), :]
            a = g[:, 0:DC]
            sg = _sigmoid(g[:, DC:2 * DC])
            dglu_ref[0, pl.ds(ck * CH, CH), 0:DC] = (acc * sg).astype(BF16)
            dglu_ref[0, pl.ds(ck * CH, CH), DC:2 * DC] = (acc * a * sg * (1.0 - sg)).astype(BF16)
        group = 4
        for t0 in range(0, KW, group):
            taps = range(t0, min(t0 + group, KW))
            acc8 = [jnp.zeros((8, DC), F32) for _ in taps]
            for ck in range(TC // CH):
                dchunk = d_ref[0, pl.ds(ck * CH, CH), :]
                for n, t in enumerate(taps):
                    prod = _tap_rows(padu_ref, planes_u, 1 + t, ck * CH, CH) * dchunk
                    acc8[n] = acc8[n] + jnp.sum(prod.reshape(CH // 8, 8, DC), axis=0)
            for n, t in enumerate(taps):
                dcw_ref[t:t + 1, :] += jnp.sum(acc8[n], axis=0, keepdims=True)

    gmain, gprev, gnext = _halo_specs(2 * DC, s)
    dmain, dprev, dnext = _halo_specs(DC, s)
    cw = pl.BlockSpec((32, DC), lambda b, i: (0, 0))
    return pl.pallas_call(
        body, name="conv_bwd_depthwise", grid=(bl, nt),
        in_specs=[gmain, gprev, gnext, dmain, dprev, dnext, cw],
        out_specs=[gmain, cw],
        out_shape=[jax.ShapeDtypeStruct((bl, s, 2 * DC), BF16), jax.ShapeDtypeStruct((32, DC), F32)],
        scratch_shapes=[pltpu.VMEM((TC + 2 * HALO, DC), F32)] * 2 + [pltpu.VMEM((7, PLANE_ROWS, DC), F32)] * 2,
        compiler_params=_params(("arbitrary", "arbitrary")),
    )(glu, glu, glu, dy, dy, dy, conv_w)


def _out_fwd_bwd(attn, za, cp, zc, x, target, modrows, w_out_b, y_conv, ln_w, ln_b, w_pw_b):
    bl, s, _ = x.shape
    tm = OUT_TM

    def body(o_ref, za_ref, cp_ref, zc_ref, x_ref, t_ref, mod_ref, w_ref, y_ref, lw_ref, lb_ref, wpw_ref,
             do_ref, dza_ref, dy_ref, dzc_ref, dh_ref, dgate_ref, gwb_ref, loss_ref, gpwb_ref, rows_ref,
             gw_ref, gpw_ref):
        b, i = pl.program_id(0), pl.program_id(1)

        @pl.when((b == 0) & (i == 0))
        def _():
            gw_ref[...] = jnp.zeros_like(gw_ref)
            gpw_ref[...] = jnp.zeros_like(gpw_ref)
            rows_ref[...] = jnp.zeros_like(rows_ref)
            loss_ref[...] = jnp.zeros_like(loss_ref)

        @pl.when(i == 0)
        def _():
            dgate_ref[...] = jnp.zeros_like(dgate_ref)

        gate = mod_ref[0, 2:3, :]
        w = w_ref[...]
        o, za_v, cp_v, zc_v = o_ref[0], za_ref[0], cp_ref[0], zc_ref[0]
        sa = _sigmoid(za_v)
        sc = _sigmoid(zc_v)
        silu_a = za_v * sa
        silu_c = zc_v * sc
        mix = jnp.concatenate([(o * silu_a).astype(BF16), (cp_v * silu_c).astype(BF16)], axis=-1)
        out = _dot(mix, w)
        err = x_ref[0] + gate * out - t_ref[0]
        loss_ref[...] += jnp.sum(err * err, axis=0, keepdims=True)
        dh = err * (1.0 / D)
        dh_ref[0] = dh
        dgate_ref[0] += jnp.sum(dh * out, axis=0, keepdims=True)
        dout = (dh * gate).astype(BF16)
        gw_ref[...] += _dot_tn(mix, dout)
        dmix = _dot_nt(dout, w)
        dga = dmix[:, 0:DA]
        dgc = dmix[:, DA:DA + DC]
        dov = dga * silu_a
        for h in range(DA // HD):
            do_ref[0, h] = dov[:, h * HD:(h + 1) * HD].astype(BF16)
        dza_ref[0] = (dga * o * (sa * (1.0 + za_v * (1.0 - sa)))).astype(BF16)
        dzc_ref[0] = (dgc * cp_v * (sc * (1.0 + zc_v * (1.0 - sc)))).astype(BF16)
        dcp = dgc * silu_c
        y = y_ref[0]
        yc = y - jnp.mean(y, axis=-1, keepdims=True)
        rstd = lax.rsqrt(jnp.mean(yc * yc, axis=-1, keepdims=True) + EPS)
        yn = yc * rstd
        lw = lw_ref[...]
        z = yn * lw + lb_ref[...]
        sg = _sigmoid(z)
        dcp_b = dcp.astype(BF16)
        gpw_ref[...] += _dot_tn((z * sg).astype(BF16), dcp_b)
        dz = _dot_nt(dcp_b, wpw_ref[...]) * (sg * (1.0 + z * (1.0 - sg)))
        dyn = dz * lw
        dy = rstd * (dyn - jnp.mean(dyn, axis=-1, keepdims=True) - yn * jnp.mean(dyn * yn, axis=-1, keepdims=True))
        dy_ref[0] = dy
        rows_ref[0:1, :] += jnp.sum(dcp, axis=0, keepdims=True)
        rows_ref[1:2, :] += jnp.sum(dz * yn, axis=0, keepdims=True)
        rows_ref[2:3, :] += jnp.sum(dz, axis=0, keepdims=True)
        rows_ref[3:4, :] += jnp.sum(dy, axis=0, keepdims=True)

        @pl.when((b == bl - 1) & (i == s // tm - 1))
        def _():
            gwb_ref[...] = gw_ref[...].astype(BF16)
            gpwb_ref[...] = gpw_ref[...].astype(BF16)

    def const(shape):
        return pl.BlockSpec(shape, lambda b, i: (0,) * len(shape))

    def tile(w):
        return pl.BlockSpec((1, tm, w), lambda b, i: (b, i, 0))

    return pl.pallas_call(
        body, name="out_fwd_bwd", grid=(bl, s // tm),
        in_specs=[tile(DA), tile(DA), tile(DC), tile(DC), tile(D), tile(D),
                  pl.BlockSpec((1, 3, D), lambda b, i: (b, 0, 0)), const((D, D)),
                  tile(DC), const((1, DC)), const((1, DC)), const((DC, DC))],
        out_specs=[pl.BlockSpec((1, DA // HD, tm, HD), lambda b, i: (b, 0, i, 0)), tile(DA), tile(DC), tile(DC), tile(D),
                   pl.BlockSpec((1, 1, D), lambda b, i: (b, 0, 0)), const((D, D)), const((1, D)),
                   const((DC, DC)), const((8, DC))],
        out_shape=[jax.ShapeDtypeStruct((bl, DA // HD, s, HD), BF16), jax.ShapeDtypeStruct((bl, s, DA), BF16),
                   jax.ShapeDtypeStruct((bl, s, DC), F32), jax.ShapeDtypeStruct((bl, s, DC), BF16),
                   jax.ShapeDtypeStruct((bl, s, D), F32), jax.ShapeDtypeStruct((bl, 1, D), F32),
                   jax.ShapeDtypeStruct((D, D), BF16), jax.ShapeDtypeStruct((1, D), F32),
                   jax.ShapeDtypeStruct((DC, DC), BF16), jax.ShapeDtypeStruct((8, DC), F32)],
        scratch_shapes=[pltpu.VMEM((D, D), F32), pltpu.VMEM((DC, DC), F32)],
        compiler_params=_params(("arbitrary", "arbitrary")),
    )(attn, za, cp, zc, x, target, modrows, w_out_b, y_conv, ln_w, ln_b, w_pw_b)


def _rms_heads_bwd(dy, x, w_t, ones_bd):
    r = lax.rsqrt(_segsum(x * x, ones_bd) * (1.0 / HD) + EPS)
    xh = x * r
    g = dy * w_t
    dx = r * (g - xh * (_segsum(g * xh, ones_bd) * (1.0 / HD)))
    return dx, dy * xh


def _ctx_bwd(ctx, modc, norm_w, w_kv_b, pkv_c, dk_c, dv_c, knw_t, ones_bd):
    bl, cl, _ = ctx.shape

    def body(x_ref, mod_ref, nw_ref, w_ref, p_ref, dk_ref, dv_ref, knw_ref, bd_ref, gw_ref, rows_ref, dknw_ref):
        @pl.when(pl.program_id(0) == 0)
        def _():
            gw_ref[...] = jnp.zeros_like(gw_ref)
            rows_ref[...] = jnp.zeros_like(rows_ref)
            dknw_ref[...] = jnp.zeros_like(dknw_ref)

        xv = x_ref[0]
        shift = mod_ref[0, 0:1, :]
        scale = mod_ref[0, 1:2, :]
        nw = nw_ref[...]
        r = lax.rsqrt(jnp.mean(xv * xv, axis=-1, keepdims=True) + EPS)
        xn = xv * r
        yv = xn * nw
        u = yv * (1.0 + scale) + shift
        dkv = jnp.concatenate([dk_ref[0, 0], dk_ref[0, 1]], axis=-1)
        dpk, dknw = _rms_heads_bwd(dkv, p_ref[0][:, 0:KVW], knw_ref[...], bd_ref[...])
        dp = jnp.concatenate([dpk.astype(BF16), dv_ref[0, 0].astype(BF16), dv_ref[0, 1].astype(BF16)], axis=-1)
        gw_ref[...] += _dot_tn(dp, u.astype(BF16))
        du = _dot(dp, w_ref[...])
        rows_ref[0:1, :] += jnp.sum(du, axis=0, keepdims=True)
        rows_ref[1:2, :] += jnp.sum(du * yv, axis=0, keepdims=True)
        rows_ref[2:3, :] += jnp.sum(du * (1.0 + scale) * xn, axis=0, keepdims=True)
        dknw_ref[...] += jnp.sum(dknw, axis=0, keepdims=True)

    def const(shape):
        return pl.BlockSpec(shape, lambda b: (0,) * len(shape))

    def tile(w):
        return pl.BlockSpec((1, cl, w), lambda b: (b, 0, 0))

    ctx_block = (dk_c.shape[2] - cl) // cl
    kv_tile = pl.BlockSpec((1, KVW // HD, cl, HD), lambda b: (b, 0, ctx_block, 0))
    return pl.pallas_call(
        body, name="ctx_bwd", grid=(bl,),
        in_specs=[tile(D), const((1, 3, D)), const((1, D)), _KV_ROWS_OF_W_IN_T, tile(2 * KVW), kv_tile, kv_tile,
                  const((1, KVW)), const((KVW, KVW))],
        out_specs=[const((2 * KVW, D)), const((8, D)), const((1, KVW))],
        out_shape=[jax.ShapeDtypeStruct((2 * KVW, D), F32), jax.ShapeDtypeStruct((8, D), F32),
                   jax.ShapeDtypeStruct((1, KVW), F32)],
        compiler_params=_params(("arbitrary",)),
    )(ctx, modc, norm_w, w_kv_b, pkv_c, dk_c, dv_c, knw_t, ones_bd)


def _bwd_in(x, modrows, norm_w, w_in_b, cos, sins, qnw_t, knw_t, ones_bd,
            pq, pkv, dq, dk, dv, dza, dglu, dzc, dh, gw_kv):
    bl, s, _ = x.shape
    tm = TOKEN_PARTS * TM
    nt = s // tm

    def body(x_ref, mod_ref, nw_ref, win_hbm, cos_ref, sin_ref, qnw_ref, knw_ref, bd_ref,
             pq_ref, pkv_ref, dq_ref, dk_ref, dv_ref, dza_ref, dglu_ref, dzc_ref, dh_ref, gwkv_ref,
             gx_ref, gw_hbm, dmod_ref, dnw_ref, dqnw_ref, dknw_ref, win_ref, gw_acc, sem):
        b, i = pl.program_id(0), pl.program_id(1)

        @pl.when((b == 0) & (i == 0))
        def _():
            cp = pltpu.make_async_copy(win_hbm, win_ref, sem)
            cp.start()
            gw_acc[...] = jnp.zeros_like(gw_acc)
            dnw_ref[...] = jnp.zeros_like(dnw_ref)
            dqnw_ref[...] = jnp.zeros_like(dqnw_ref)
            dknw_ref[...] = jnp.zeros_like(dknw_ref)
            cp.wait()

        @pl.when(i == 0)
        def _():
            dmod_ref[...] = jnp.zeros_like(dmod_ref)

        bd = bd_ref[...]
        shift = mod_ref[0, 0:1, :]
        scale = mod_ref[0, 1:2, :]
        nw = nw_ref[...]
        dps, us = [], []
        for part in range(TOKEN_PARTS):
            rows = pl.ds(part * TM, TM)
            ck = cos_ref[rows, :]
            sk = sin_ref[rows, :]
            cs = jnp.concatenate([ck] * (DA // KVW), axis=-1)
            sn = jnp.concatenate([sk] * (DA // KVW), axis=-1)
            dqn = _rope_bwd(dq_ref[0, rows, :], cs, sn)
            dpq, dqnw = _rms_heads_bwd(dqn, pq_ref[0, rows, :], qnw_ref[...], bd)
            dkn = _rope_bwd(jnp.concatenate([dk_ref[0, 0, rows, :], dk_ref[0, 1, rows, :]], axis=-1), ck, sk)
            dpk, dknw = _rms_heads_bwd(dkn, pkv_ref[0, rows, 0:KVW], knw_ref[...], bd[0:KVW, 0:KVW])
            dqnw_ref[...] += jnp.sum(dqnw, axis=0, keepdims=True)
            dknw_ref[...] += jnp.sum(dknw, axis=0, keepdims=True)
            dp = jnp.concatenate(
                [dpq.astype(BF16), dpk.astype(BF16), dv_ref[0, 0, rows, :].astype(BF16), dv_ref[0, 1, rows, :].astype(BF16),
                 dza_ref[0, rows, :], dglu_ref[0, rows, :], dzc_ref[0, rows, :]], axis=-1)

            xv = x_ref[0, rows, :]
            r = lax.rsqrt(jnp.mean(xv * xv, axis=-1, keepdims=True) + EPS)
            xn = xv * r
            yv = xn * nw
            u = yv * (1.0 + scale) + shift
            dps.append(dp)
            us.append(u.astype(BF16))
            du = _dot(dp, win_ref[...])
            dmod_ref[0, 0:1, :] += jnp.sum(du, axis=0, keepdims=True)
            dmod_ref[0, 1:2, :] += jnp.sum(du * yv, axis=0, keepdims=True)
            dy = du * (1.0 + scale)
            dnw_ref[...] += jnp.sum(dy * xn, axis=0, keepdims=True)
            dxn = dy * nw
            gx_ref[0, rows, :] = dh_ref[0, rows, :] + r * (dxn - xn * jnp.mean(dxn * xn, axis=-1, keepdims=True))
        gw_acc[...] += _dot_tn(jnp.concatenate(dps, axis=0), jnp.concatenate(us, axis=0))

        @pl.when((b == bl - 1) & (i == nt - 1))
        def _():
            gw_acc[DA:DA + 2 * KVW, :] += gwkv_ref[...]

            def to_bf16(j, carry):
                rows = pl.ds(pl.multiple_of(j * 2 * KVW, 2 * KVW), 2 * KVW)
                win_ref[rows, :] = gw_acc[rows, :].astype(BF16)
                return carry

            lax.fori_loop(0, D_IN // (2 * KVW), to_bf16, 0)
            pltpu.sync_copy(win_ref, gw_hbm)

    def tile(w):
        return pl.BlockSpec((1, tm, w), lambda b, i: (b, i, 0))

    def const(shape):
        return pl.BlockSpec(shape, lambda b, i: (0,) * len(shape))

    anyspace = pl.BlockSpec(memory_space=pl.ANY)
    rope = pl.BlockSpec((tm, KVW), lambda b, i: (i, 0))
    kv_tile = pl.BlockSpec((1, KVW // HD, tm, HD), lambda b, i: (b, 0, i, 0))
    return pl.pallas_call(
        body, name="bwd_in", grid=(bl, nt),
        in_specs=[tile(D), pl.BlockSpec((1, 3, D), lambda b, i: (b, 0, 0)), const((1, D)), anyspace, rope, rope,
                  const((1, DA)), const((1, KVW)), const((DA, DA)),
                  tile(DA), tile(2 * KVW), tile(DA), kv_tile, kv_tile, tile(DA), tile(2 * DC), tile(DC), tile(D),
                  const((2 * KVW, D))],
        out_specs=[tile(D), anyspace, pl.BlockSpec((1, 2, D), lambda b, i: (b, 0, 0)), const((1, D)),
                   const((1, DA)), const((1, KVW))],
        out_shape=[jax.ShapeDtypeStruct((bl, s, D), F32), jax.ShapeDtypeStruct((D_IN, D), BF16),
                   jax.ShapeDtypeStruct((bl, 2, D), F32), jax.ShapeDtypeStruct((1, D), F32),
                   jax.ShapeDtypeStruct((1, DA), F32), jax.ShapeDtypeStruct((1, KVW), F32)],
        scratch_shapes=[pltpu.VMEM((D_IN, D), BF16), pltpu.VMEM((D_IN, D), F32), pltpu.SemaphoreType.DMA],
        compiler_params=_params(("arbitrary", "arbitrary")),
    )(x, modrows, norm_w, w_in_b, cos, sins, qnw_t, knw_t, ones_bd,
      pq, pkv, dq, dk, dv, dza, dglu, dzc, dh, gw_kv)


_LOSS, _DMODC, _NW, _QN, _KN, _CB, _LW, _LB, _BPW, SMALL_W = 0, 1024, 4096, 5120, 5248, 5376, 5888, 6400, 6912, 7424


ROW_W = 1792


def _put_flat(ref, off, value):
    n, done = value.shape[1], 0
    while done < n:
        r, c = divmod(off + done, ROW_W)
        take = min(n - done, ROW_W - c)
        ref[r:r + 1, c:c + take] = value[:, done:done + take]
        done += take


def _get_flat(arr, off, n):
    parts, done = [], 0
    while done < n:
        r, c = divmod(off + done, ROW_W)
        take = min(n - done, ROW_W - c)
        parts.append(arr[r:r + 1, c:c + take])
        done += take
    return parts[0] if len(parts) == 1 else jnp.concatenate(parts, axis=-1)


def _pack_small_body(loss_ref, ctx_ref, dnw_ref, dqnw_ref, dknw_ref, dknwc_ref, conv_ref, dss_ref, dgate_ref, o_ref):
    bl = dss_ref.shape[0]
    assert SMALL_W + bl * 3 * D <= 8 * ROW_W
    o_ref[...] = jnp.zeros_like(o_ref)
    _put_flat(o_ref, _LOSS, loss_ref[...])
    _put_flat(o_ref, _DMODC, ctx_ref[0:1, :])
    _put_flat(o_ref, _DMODC + D, ctx_ref[1:2, :])
    _put_flat(o_ref, _NW, dnw_ref[...] + ctx_ref[2:3, :])
    dq = dqnw_ref[...]
    qn = dq[:, 0:HD]
    for h in range(1, DA // HD):
        qn = qn + dq[:, h * HD:(h + 1) * HD]
    _put_flat(o_ref, _QN, qn)
    dk = dknw_ref[...] + dknwc_ref[...]
    _put_flat(o_ref, _KN, dk[:, 0:HD] + dk[:, HD:2 * HD])
    _put_flat(o_ref, _BPW, conv_ref[0:1, :])
    _put_flat(o_ref, _LW, conv_ref[1:2, :])
    _put_flat(o_ref, _LB, conv_ref[2:3, :])
    _put_flat(o_ref, _CB, conv_ref[3:4, :])
    for b in range(bl):
        _put_flat(o_ref, SMALL_W + b * 3 * D, dss_ref[b, 0:1, :])
        _put_flat(o_ref, SMALL_W + b * 3 * D + D, dss_ref[b, 1:2, :])
        _put_flat(o_ref, SMALL_W + b * 3 * D + 2 * D, dgate_ref[b])


_SMALL = (("b_mod", None), ("norm_w", _NW), ("q_norm_w", _QN), ("k_norm_w", _KN), ("conv_b", _CB),
          ("conv_ln_w", _LW), ("conv_ln_b", _LB), ("b_pw", _BPW), ("c_ctx", None))


def _epilogue(parts_in, pieces, c_rows, w_mod_loc):
    bl = pieces[7].shape[0]
    n_ex = N_DEV * bl
    n_mod = w_mod_loc.shape[1]
    rb = 32
    shp = parts_in.shape[1:]
    rows_in = shp[0]

    def body(*refs):
        it = iter(refs)
        take = lambda k: [next(it) for _ in range(k)]
        (parts,) = take(1)
        piece_refs = take(9)
        (c_ref, wm_ref) = take(2)
        (g_in, g_wm, sum_ref, gb_ref, gc_all, loss_ref) = take(6)
        (mine, got_sib, stage, got_chip, payload, gathered, dmod_full, gc_mine) = take(8)
        (d2d_send, d2d_recv, ici_send, ici_recv, local_sems, sg_send, sg_recv, gc_send, gc_recv, misc_sems) = take(10)

        x, y, c = _coords()
        me = _lin(x, y, c)
        sib = (x, y, 1 - c)
        home = 2 * x + y

        def rows_loop(fn):
            def step(i, carry):
                fn(pl.ds(pl.multiple_of(i * rb, rb), rb))
                return carry
            lax.fori_loop(0, rows_in // rb, step, 0)

        def direct_gather(src, dst, send_sems, recv_sems, local_sem):
            cps = [pltpu.make_async_copy(src, dst.at[me], local_sem)]
            for k in range(1, N_DEV):
                peer = (1 - x if k & 4 else x, 1 - y if k & 2 else y, 1 - c if k & 1 else c)
                cps.append(pltpu.make_async_remote_copy(
                    src_ref=src, dst_ref=dst.at[me], send_sem=send_sems.at[k - 1], recv_sem=recv_sems.at[k - 1],
                    device_id=peer, device_id_type=MESH_ID))
            for cp in cps:
                cp.start()
            return cps

        _pack_small_body(*piece_refs, payload)
        small_cps = direct_gather(payload, gathered, sg_send, sg_recv, misc_sems.at[0])

        local, d2d, ici = [], [], []
        for s in range(4):
            cp = pltpu.make_async_copy(parts.at[_lin(s // 2, s % 2, c)], mine.at[s], local_sems.at[s])
            cp.start()
            local.append(cp)
            rc = pltpu.make_async_remote_copy(
                src_ref=parts.at[_lin(s // 2, s % 2, 1 - c)], dst_ref=got_sib.at[s],
                send_sem=d2d_send.at[s], recv_sem=d2d_recv.at[s], device_id=sib, device_id_type=MESH_ID)
            rc.start()
            d2d.append(rc)

        for cp in small_cps[1:]:
            cp.wait_recv()
        small_cps[0].wait()
        tot = gathered[0]
        for j in range(1, N_DEV):
            tot = tot + gathered[j]
        summed = _get_flat(tot, 0, SMALL_W)
        dmod_full[...] = jnp.zeros_like(dmod_full)
        for j in range(N_DEV):
            arr = gathered[j]
            for b in range(bl):
                dmod_full[j * bl + b:j * bl + b + 1, :] = _get_flat(arr, SMALL_W + b * 3 * D, 3 * D)
        dmod_full[n_ex:n_ex + 1, :] = summed[:, _DMODC:_DMODC + 3 * D]
        sum_ref[...] = summed
        gb_ref[...] = jnp.sum(dmod_full[...], axis=0, keepdims=True)
        loss_ref[...] = (0.5 / D) * jnp.sum(summed[:, _LOSS:_LOSS + D], axis=-1, keepdims=True)

        north = c == 1
        first = (jnp.where(north, 1 - x, x), jnp.where(north, y, 1 - y))
        second = (jnp.where(north, x, 1 - x), jnp.where(north, 1 - y, y))
        for s in range(4):
            local[s].wait()
            d2d[s].wait_recv()

        def chip_sum(k, chip, relayed):
            slot = 2 * chip[0] + chip[1]

            def pair_sum(rs):
                acc = mine[slot, rs, :].astype(F32) + got_sib[slot, rs, :].astype(F32)
                if relayed:
                    acc = acc + got_chip[1, rs, :].astype(F32)
                stage[k, rs, :] = acc.astype(BF16)

            rows_loop(pair_sum)

        def send(k, to):
            rc = pltpu.make_async_remote_copy(
                src_ref=stage.at[k], dst_ref=got_chip.at[k], send_sem=ici_send.at[k], recv_sem=ici_recv.at[k],
                device_id=(to[0], to[1], c), device_id_type=MESH_ID)
            rc.start()
            ici.append(rc)

        chip_sum(0, first, False)
        send(0, first)
        chip_sum(1, (1 - x, 1 - y), False)
        send(1, first)

        cr = c_ref[...]
        act = (cr * _sigmoid(cr)).astype(BF16)
        dm = dmod_full[:, pl.ds(pl.multiple_of(me * n_mod, 128), n_mod)].astype(BF16)
        g_wm[...] = _dot_tn(act, dm)
        gc_mine[...] = _dot_nt(dm[n_ex:n_ex + 8, :], wm_ref[...].astype(BF16))
        gc_cps = direct_gather(gc_mine, gc_all, gc_send, gc_recv, misc_sems.at[1])

        ici[1].wait_recv()
        chip_sum(2, second, True)
        send(2, second)
        ici[0].wait_recv()
        ici[2].wait_recv()

        def finish(rs):
            gsum = mine[home, rs, :].astype(F32) + got_sib[home, rs, :].astype(F32)
            g_in[rs, :] = gsum + got_chip[0, rs, :].astype(F32) + got_chip[2, rs, :].astype(F32)

        rows_loop(finish)

        for cp in gc_cps[1:]:
            cp.wait_recv()
        gc_cps[0].wait()
        for rc in d2d + ici + small_cps[1:] + gc_cps[1:]:
            rc.wait_send()

    vm = pl.BlockSpec(memory_space=pltpu.VMEM)
    anyspace = pl.BlockSpec(memory_space=pl.ANY)
    assert rows_in % rb == 0 and parts_in.dtype == BF16
    args = [parts_in, *pieces, c_rows, w_mod_loc]
    in_specs = [anyspace] + [vm] * (len(args) - 1)
    out_shape = [jax.ShapeDtypeStruct(shp, F32), jax.ShapeDtypeStruct(w_mod_loc.shape, F32),
                 jax.ShapeDtypeStruct((1, SMALL_W), F32), jax.ShapeDtypeStruct((1, 3 * D), F32),
                 jax.ShapeDtypeStruct((N_DEV, 8, D), F32), jax.ShapeDtypeStruct((1, 1), F32)]
    scratch = [pltpu.VMEM((4,) + shp, BF16), pltpu.VMEM((4,) + shp, BF16), pltpu.VMEM((3,) + shp, BF16),
               pltpu.VMEM((3,) + shp, BF16), pltpu.VMEM((8, ROW_W), F32), pltpu.VMEM((N_DEV, 8, ROW_W), F32),
               pltpu.VMEM((n_ex + 8, 3 * D), F32), pltpu.VMEM((8, D), F32),
               pltpu.SemaphoreType.DMA((4,)), pltpu.SemaphoreType.DMA((4,)), pltpu.SemaphoreType.DMA((3,)),
               pltpu.SemaphoreType.DMA((3,)), pltpu.SemaphoreType.DMA((4,)),
               pltpu.SemaphoreType.DMA((N_DEV - 1,)), pltpu.SemaphoreType.DMA((N_DEV - 1,)),
               pltpu.SemaphoreType.DMA((N_DEV - 1,)), pltpu.SemaphoreType.DMA((N_DEV - 1,)),
               pltpu.SemaphoreType.DMA((2,))]
    return pl.pallas_call(
        body, name="epilogue", out_shape=out_shape, in_specs=in_specs, out_specs=[vm] * len(out_shape),
        scratch_shapes=scratch, compiler_params=pltpu.CompilerParams(vmem_limit_bytes=VMEM_LIMIT),
    )(*args)


def _final_adamw(g_in, w_in_t, m_in_t, v_in_t, g_wm, w_mod_loc, m_mod, v_mod, summed, g_bmod, gc_all,
                 small_w, small_m, small_v):
    ns = len(_SMALL)

    def body(*refs):
        it = iter(refs)
        take = lambda k: [next(it) for _ in range(k)]
        (gin_ref, w_ref, m_ref, v_ref, gwm_ref, wm_ref, mm_ref, vm_ref, sum_ref, gb_ref, gc_ref) = take(11)
        sw, sm, sv = take(ns), take(ns), take(ns)
        (d_in, nm_in, nv_in, d_wm, nm_wm, nv_wm) = take(6)
        souts = take(4 * ns)

        for g_r, w_r, m_r, v_r, outs3 in ((gin_ref, w_ref, m_ref, v_ref, (d_in, nm_in, nv_in)),
                                          (gwm_ref, wm_ref, mm_ref, vm_ref, (d_wm, nm_wm, nv_wm))):
            for o_r, val in zip(outs3, _adamw(w_r[...], g_r[...], m_r[...], v_r[...])):
                o_r[...] = val

        @pl.when(pl.program_id(0) == 0)
        def _():
            for k, (name, off) in enumerate(_SMALL):
                w = sw[k][...]
                if name == "b_mod":
                    gk = gb_ref[...]
                elif name == "c_ctx":
                    acc = gc_ref[0, 0:1, :]
                    for j in range(1, N_DEV):
                        acc = acc + gc_ref[j, 0:1, :]
                    sg = _sigmoid(w)
                    gk = acc * (sg * (1.0 + w * (1.0 - sg)))
                else:
                    gk = sum_ref[:, off:off + w.shape[1]]
                dl, m_new, v_new = _adamw(w, gk, sm[k][...], sv[k][...])
                souts[k][...] = gk
                souts[ns + k][...] = dl
                souts[2 * ns + k][...] = m_new
                souts[3 * ns + k][...] = v_new

    rows_in, rows_mod = w_in_t.shape[0] // ADAM_STEPS, w_mod_loc.shape[0] // ADAM_STEPS
    assert rows_in % 8 == 0 and rows_mod % 8 == 0

    def whole(a):
        return pl.BlockSpec(a.shape, lambda i: (0,) * a.ndim)

    in_tile = pl.BlockSpec((rows_in, w_in_t.shape[1]), lambda i: (i, 0))
    mod_tile = pl.BlockSpec((rows_mod, w_mod_loc.shape[1]), lambda i: (i, 0))
    small_in = [summed, g_bmod, gc_all, *small_w, *small_m, *small_v]
    big_shape = jax.ShapeDtypeStruct(w_in_t.shape, F32)
    mod_shape = jax.ShapeDtypeStruct(w_mod_loc.shape, F32)
    out_shape = [big_shape] * 3 + [mod_shape] * 3 + [jax.ShapeDtypeStruct(w.shape, F32) for w in small_w] * 4
    outs = pl.pallas_call(
        body, name="final_adamw", grid=(ADAM_STEPS,), out_shape=out_shape,
        in_specs=[in_tile] * 4 + [mod_tile] * 4 + [whole(a) for a in small_in],
        out_specs=[in_tile] * 3 + [mod_tile] * 3 + [whole(w) for w in small_w] * 4,
        compiler_params=_params(("arbitrary",)),
    )(g_in, w_in_t, m_in_t, v_in_t, g_wm, w_mod_loc, m_mod, v_mod, *small_in)
    small_outs = [outs[6 + k * ns:6 + (k + 1) * ns] for k in range(4)]
    return (g_in,) + tuple(outs[0:3]), (g_wm,) + tuple(outs[3:6]), small_outs


def _rope_tables(s):
    t = jnp.arange(s, dtype=jnp.int32)
    row = (t // GRID_W).astype(F32)
    col = (t % GRID_W).astype(F32)
    freqs = ROPE_THETA ** (-jnp.arange(0, HD // 2, 2, dtype=F32) / (HD // 2))
    ang_r = row[:, None] * freqs[None, :]
    ang_c = col[:, None] * freqs[None, :]
    cr, sr, cc, sc = jnp.cos(ang_r), jnp.sin(ang_r), jnp.cos(ang_c), jnp.sin(ang_c)
    cos = jnp.concatenate([cr, cr, cc, cc], axis=-1)
    sins = jnp.concatenate([-sr, sr, -sc, sc], axis=-1)
    return jnp.tile(cos, (1, KVW // HD)), jnp.tile(sins, (1, KVW // HD))


def kernel(x, c, ctx, c_ctx, w_mod, b_mod, norm_w, w_in, q_norm_w, k_norm_w, conv_w, conv_b, conv_ln_w, conv_ln_b, w_pw, b_pw, w_out, loss_target, m_c_ctx, m_w_mod, m_b_mod, m_norm_w, m_w_in, m_q_norm_w, m_k_norm_w, m_conv_w, m_conv_b, m_conv_ln_w, m_conv_ln_b, m_w_pw, m_b_pw, m_w_out, v_c_ctx, v_w_mod, v_b_mod, v_norm_w, v_w_in, v_q_norm_w, v_k_norm_w, v_conv_w, v_conv_b, v_conv_ln_w, v_conv_ln_b, v_w_pw, v_b_pw, v_w_out):
    bl, s, _ = x.shape
    cl = ctx.shape[1]
    me = _lin(*_coords())

    conv_w_pad = jnp.pad(conv_w[0], ((0, 32 - KW), (0, 0)))
    n_ex = N_DEV * bl
    g_win, c_rows, g_mod = _prologue(w_in[0].T, c, c_ctx[None, :], w_mod[0], b_mod)
    w_in_b = g_win.reshape(D_IN, D)
    mod_all = g_mod.transpose(1, 0, 2).reshape(n_ex + 8, 3 * D)
    modrows = lax.dynamic_slice_in_dim(mod_all, me * bl, bl, axis=0).reshape(bl, 3, D)
    modc = mod_all[n_ex].reshape(1, 3, D)

    cos, sins = _rope_tables(s)
    qnw_t = jnp.tile(q_norm_w, (1, DA // HD))
    knw_t = jnp.tile(k_norm_w, (1, KVW // HD))
    lane = jnp.arange(DA, dtype=jnp.int32) // HD
    ones_bd = (lane[:, None] == lane[None, :]).astype(BF16)
    ones_kv = ones_bd[0:KVW, 0:KVW]
    w_kv_b = w_in_b

    k_ctx, v_ctx, pkv_c = _ctx_fwd(ctx, modc, norm_w, w_kv_b, knw_t, ones_kv, cl + s)
    (q_h, k_h, v_h, pq, pkv, za, glu, zc), (g_wout, g_wpw, g_cw) = _fwd_in(
        x, modrows, norm_w, w_in_b, cos, sins, qnw_t, knw_t, ones_bd, k_ctx, v_ctx,
        [w_out[0], w_pw[0], conv_w_pad], [BF16, BF16, F32])
    w_out_b = g_wout.reshape(D, D)
    w_pw_b = g_wpw.reshape(DC, DC)
    conv_w_full = g_cw.transpose(1, 0, 2).reshape(32, DC)
    attn, lse = _attn_fwd(q_h, k_h, v_h)
    y_conv, cp = _conv_fwd(glu, conv_w_full, conv_b, conv_ln_w, conv_ln_b, w_pw_b, b_pw)

    do_h, dza, dy_conv, dzc, dh, dgate, gw_out, loss_row, gw_pw, conv_rows = _out_fwd_bwd(
        attn, za, cp, zc, x, loss_target, modrows, w_out_b, y_conv, conv_ln_w, conv_ln_b, w_pw_b)
    dglu, g_cw_full = _conv_bwd_depthwise(glu, dy_conv, conv_w_full)
    parts_out = gw_out.reshape(N_DEV, D // N_DEV, D)
    parts_pw = gw_pw.reshape(N_DEV, DC // N_DEV, DC)
    parts_cw = g_cw_full.astype(BF16).reshape(32, N_DEV, DC // N_DEV).transpose(1, 0, 2)
    (dq, dk_h, dv_h), (got_out, got_pw, got_cw) = _attn_bwd(
        q_h, k_h, v_h, do_h, attn, lse, [parts_out, parts_pw, parts_cw])
    gw_kv, ctx_rows, dknw_c = _ctx_bwd(ctx, modc, norm_w, w_kv_b, pkv_c, dk_h, dv_h, knw_t, ones_kv)
    grad_x, gw_in, dmod_ss, dnw, dqnw, dknw = _bwd_in(
        x, modrows, norm_w, w_in_b, cos, sins, qnw_t, knw_t, ones_bd,
        pq, pkv, dq, dk_h, dv_h, dza, dglu, dzc, dh, gw_kv)

    r_out, r_pw, r_cw = _sum_devices_adamw(
        [(got_out, w_out[0], m_w_out[0], v_w_out[0]), (got_pw, w_pw[0], m_w_pw[0], v_w_pw[0]),
         (got_cw, conv_w[0], m_conv_w[0], v_conv_w[0])])

    given = {"c_ctx": (c_ctx, m_c_ctx, v_c_ctx), "b_mod": (b_mod, m_b_mod, v_b_mod), "norm_w": (norm_w, m_norm_w, v_norm_w),
             "q_norm_w": (q_norm_w, m_q_norm_w, v_q_norm_w), "k_norm_w": (k_norm_w, m_k_norm_w, v_k_norm_w),
             "conv_b": (conv_b, m_conv_b, v_conv_b), "conv_ln_w": (conv_ln_w, m_conv_ln_w, v_conv_ln_w),
             "conv_ln_b": (conv_ln_b, m_conv_ln_b, v_conv_ln_b), "b_pw": (b_pw, m_b_pw, v_b_pw)}
    as_rows = [[given[name][which].reshape(1, -1) for name, _ in _SMALL] for which in range(3)]
    g_in_t, g_wmod, summed, g_bmod, gc_all, loss11 = _epilogue(
        gw_in.reshape(N_DEV, D_IN // N_DEV, D),
        [loss_row, ctx_rows, dnw, dqnw, dknw, dknw_c, conv_rows, dmod_ss, dgate], c_rows, w_mod[0])
    r_in, r_wmod, small_outs = _final_adamw(
        g_in_t, w_in[0].T, m_w_in[0].T, v_w_in[0].T, g_wmod, w_mod[0], m_w_mod[0], v_w_mod[0],
        summed, g_bmod, gc_all, *as_rows)
    r_in = tuple(a.T for a in r_in)

    big = {"w_mod": r_wmod, "w_in": r_in, "conv_w": r_cw, "w_pw": r_pw, "w_out": r_out}
    order = ["c_ctx", "w_mod", "b_mod", "norm_w", "w_in", "q_norm_w", "k_norm_w", "conv_w", "conv_b", "conv_ln_w",
             "conv_ln_b", "w_pw", "b_pw", "w_out"]
    small_index = {name: k for k, (name, _) in enumerate(_SMALL)}
    outs = [loss11.reshape(()), grad_x]
    for which in range(4):
        for name in order:
            if name in big:
                outs.append(big[name][which][None])
            else:
                outs.append(small_outs[which][small_index[name]].reshape(given[name][0].shape))
    return tuple(outs)
```

```python
import jax
import jax.numpy as jnp
from jax import lax
from jax.experimental import pallas as pl
from jax.experimental.pallas import tpu as pltpu

F32, BF16 = jnp.float32, jnp.bfloat16
MESH_ID = pl.DeviceIdType.MESH

N_DEV = 8
D = 1024
D_IN = 2816
DA = 512
DC = 512
HD = 64
KVW = 128
KW = 31
HALO = 16
EPS = 1e-6
ROPE_THETA = 10000.0
GRID_W = 64

ADAM_LR, ADAM_B1, ADAM_B2, ADAM_EPS, ADAM_WD, ADAM_STEP = 0.001, 0.9, 0.999, 1e-08, 0.01, 10

VMEM_LIMIT = 56 * 1024 * 1024

TM = 256
TQ = 128
TOKEN_PARTS = 2
OUT_TM = 512
BWD_PARTS = 4
FWD_PARTS = 8
TC = 512
CH = 32
ADAM_STEPS = 4


def _params(sem, vmem=VMEM_LIMIT):
    return pltpu.CompilerParams(dimension_semantics=sem, vmem_limit_bytes=vmem)


def _dot(a, b):
    return jnp.dot(a, b, preferred_element_type=F32)


def _dot_nt(a, b):
    return lax.dot_general(a, b, (((1,), (1,)), ((), ())), preferred_element_type=F32)


def _dot_tn(a, b):
    return lax.dot_general(a, b, (((0,), (0,)), ((), ())), preferred_element_type=F32)


def _sigmoid(z):
    return 1.0 / (1.0 + jnp.exp(-z))


def _segsum(v, ones_bd):
    return _dot(v.astype(BF16), ones_bd)


def _swap16(x):
    w = x.shape[-1]
    lane = lax.broadcasted_iota(jnp.int32, x.shape, 1)
    return jnp.where((lane % 32) < 16, pltpu.roll(x, w - 16, 1), pltpu.roll(x, 16, 1))


def _with_ones_column(v):
    one = (lax.broadcasted_iota(jnp.int32, v.shape, 1) == 0).astype(v.dtype)
    return jnp.concatenate([v, one], axis=-1)


def _rope(x, cos, sins):
    return x * cos + _swap16(x) * sins


def _rope_bwd(d, cos, sins):
    return d * cos + _swap16(d * sins)


def _adamw(w, g, m, v):
    m2 = ADAM_B1 * m + (1.0 - ADAM_B1) * g
    v2 = ADAM_B2 * v + (1.0 - ADAM_B2) * (g * g)
    m_hat = m2 / (1.0 - ADAM_B1 ** ADAM_STEP)
    v_hat = v2 / (1.0 - ADAM_B2 ** ADAM_STEP)
    delta = -ADAM_LR * (m_hat / (jnp.sqrt(v_hat) + ADAM_EPS) + ADAM_WD * w)
    return delta, m2, v2


def _coords():
    return lax.axis_index("x"), lax.axis_index("y"), lax.axis_index("c")


def _lin(x, y, c):
    return 4 * x + 2 * y + c


def _prologue(w_in_t, c, c_ctx_row, w_mod_loc, b_mod):
    bl = c.shape[0]
    n_ex = N_DEV * bl
    n_mod = w_mod_loc.shape[1]

    def body(w32_ref, c_in_ref, cctx_ref, wm_ref, b_ref, out_w, crows_ref, mod_out, w_ref, c_ref, c_gath, mod_mine,
             w_send, w_recv, c_send, c_recv, m_send, m_recv, local_sems):
        x, y, c = _coords()
        me_lin = _lin(x, y, c)
        c_ref[...] = jnp.zeros_like(c_ref)
        c_ref[0:bl, :] = c_in_ref[...]
        w_ref[...] = w32_ref[...].astype(BF16)
        me, sib = (x, y, c), (x, y, 1 - c)
        xnb, ynb, diag = (1 - x, y), (x, 1 - y), (1 - x, 1 - y)
        north = c == 1

        def direct_gather(src, dst, send_sems, recv_sems, local_sem):
            cps = [pltpu.make_async_copy(src, dst.at[me_lin], local_sem)]
            for k in range(1, N_DEV):
                peer = (1 - x if k & 4 else x, 1 - y if k & 2 else y, 1 - c if k & 1 else c)
                cps.append(pltpu.make_async_remote_copy(
                    src_ref=src, dst_ref=dst.at[me_lin], send_sem=send_sems.at[k - 1], recv_sem=recv_sems.at[k - 1],
                    device_id=peer, device_id_type=MESH_ID))
            for cp in cps:
                cp.start()
            return cps

        def copy(k, block, to, src=None):
            slot = out_w.at[_lin(*block)]
            return pltpu.make_async_remote_copy(
                src_ref=slot if src is None else src, dst_ref=slot, send_sem=w_send.at[k], recv_sem=w_recv.at[k],
                device_id=to, device_id_type=MESH_ID)

        c_cps = direct_gather(c_ref, c_gath, c_send, c_recv, local_sems.at[0])
        mine = pltpu.make_async_copy(w_ref, out_w.at[me_lin], local_sems.at[1])
        mine.start()
        first = [copy(0, me, sib, src=w_ref), copy(1, me, (*xnb, c), src=w_ref), copy(2, me, (*ynb, c), src=w_ref)]
        for cp in first:
            cp.start()

        for cp in c_cps[1:]:
            cp.wait_recv()
        c_cps[0].wait()
        crows_ref[...] = jnp.zeros_like(crows_ref)
        for j in range(N_DEV):
            crows_ref[j * bl:(j + 1) * bl, :] = c_gath[j, 0:bl, :]
        crows_ref[n_ex:n_ex + 1, :] = cctx_ref[...]
        cr = crows_ref[...]
        act = (cr * _sigmoid(cr)).astype(BF16)
        mod_mine[...] = _dot(act, wm_ref[...].astype(BF16)) + b_ref[:, pl.ds(pl.multiple_of(me_lin * n_mod, 128), n_mod)]
        mod_cps = direct_gather(mod_mine, mod_out, m_send, m_recv, local_sems.at[2])

        relay_north = copy(3, (*xnb, c), (*ynb, c))
        relay_south = copy(3, (*ynb, c), (*xnb, c))
        passed = []
        copy(1, (*xnb, c), me).wait_recv()
        pl.when(north)(relay_north.start)
        passed.append(copy(4, (*xnb, c), sib))
        passed[-1].start()
        copy(2, (*ynb, c), me).wait_recv()
        pl.when(jnp.logical_not(north))(relay_south.start)
        passed.append(copy(5, (*ynb, c), sib))
        passed[-1].start()
        copy(3, (*diag, c), me).wait_recv()
        passed.append(copy(6, (*diag, c), sib))
        passed[-1].start()
        copy(0, sib, me).wait_recv()
        for k, chip in ((4, xnb), (5, ynb), (6, diag)):
            copy(k, (*chip, 1 - c), me).wait_recv()
        for cp in mod_cps[1:]:
            cp.wait_recv()
        mod_cps[0].wait()
        for cp in first + passed + [relay_north] + c_cps[1:] + mod_cps[1:]:
            cp.wait_send()
        mine.wait()

    vm = pl.BlockSpec(memory_space=pltpu.VMEM)
    seven = pltpu.SemaphoreType.DMA((N_DEV - 1,))
    return pl.pallas_call(
        body, name="prologue",
        out_shape=[jax.ShapeDtypeStruct((N_DEV,) + w_in_t.shape, BF16), jax.ShapeDtypeStruct((n_ex + 8, D), F32),
                   jax.ShapeDtypeStruct((N_DEV, n_ex + 8, n_mod), F32)],
        in_specs=[vm] * 5, out_specs=[vm] * 3,
        scratch_shapes=[pltpu.VMEM(w_in_t.shape, BF16), pltpu.VMEM((8, D), F32), pltpu.VMEM((N_DEV, 8, D), F32),
                        pltpu.VMEM((n_ex + 8, n_mod), F32),
                        seven, seven, seven, seven, seven, seven, pltpu.SemaphoreType.DMA((3,))],
        compiler_params=pltpu.CompilerParams(vmem_limit_bytes=VMEM_LIMIT),
    )(w_in_t, c, c_ctx_row, w_mod_loc, b_mod)


def _exchange_copies(in_refs, out_refs, send_sems, recv_sems, local_sems, scatter):
    x, y, c = _coords()
    me = _lin(x, y, c)
    local, remote = [], []
    for a, (src, dst) in enumerate(zip(in_refs, out_refs)):
        local.append(pltpu.make_async_copy(src.at[me] if scatter else src, dst.at[me], local_sems.at[a]))
        for k in range(1, N_DEV):
            peer = (1 - x if k & 4 else x, 1 - y if k & 2 else y, 1 - c if k & 1 else c)
            remote.append(pltpu.make_async_remote_copy(
                src_ref=src.at[_lin(*peer)] if scatter else src, dst_ref=dst.at[me],
                send_sem=send_sems.at[a * (N_DEV - 1) + k - 1], recv_sem=recv_sems.at[a * (N_DEV - 1) + k - 1],
                device_id=peer, device_id_type=MESH_ID))
    return local, remote


def _exchange_scratch(n):
    return [pltpu.SemaphoreType.DMA((n * (N_DEV - 1),)), pltpu.SemaphoreType.DMA((n * (N_DEV - 1),)),
            pltpu.SemaphoreType.DMA((n,))]


def _sum_devices_adamw(items):
    n = len(items)

    def body(*refs):
        for a in range(n):
            got, w_ref, m_ref, v_ref = refs[4 * a:4 * a + 4]
            g_ref, d_ref, nm_ref, nv_ref = refs[4 * n + 4 * a:4 * n + 4 * a + 4]
            g = got[0].astype(F32)
            for j in range(1, N_DEV):
                g = g + got[j].astype(F32)
            g = g[0:w_ref.shape[0], :]
            delta, m2, v2 = _adamw(w_ref[...], g, m_ref[...], v_ref[...])
            g_ref[...] = g
            d_ref[...] = delta
            nm_ref[...] = m2
            nv_ref[...] = v2

    args, out_shape = [], []
    for got, w, m, v in items:
        assert got.shape[0] == N_DEV and got.shape[1] >= w.shape[0] and got.shape[2:] == w.shape[1:]
        args += [got, w, m, v]
        out_shape += [jax.ShapeDtypeStruct(w.shape, F32)] * 4
    outs = pl.pallas_call(body, name="sum_devices_adamw", out_shape=out_shape,
                          compiler_params=pltpu.CompilerParams(vmem_limit_bytes=VMEM_LIMIT))(*args)
    return [tuple(outs[4 * a:4 * a + 4]) for a in range(n)]


def _fwd_in(x, modrows, norm_w, w_in_b, cos, sins, qnw_t, knw_t, ones_bd, k_all, v_all, shards, wire_dtypes):
    bl, s, _ = x.shape
    tm = TOKEN_PARTS * TM
    nt = s // tm
    n_sh = len(shards)

    def body(*refs):
        (x_ref, mod_ref, nw_ref, win_ref, cos_ref, sin_ref, qnw_ref, knw_ref, bd_ref, kin_ref, vin_ref) = refs[:11]
        shard_refs = refs[11:11 + n_sh]
        q_ref, k_ref, v_ref, pq_ref, pkv_ref, za_ref, glu_ref, zc_ref = refs[11 + n_sh:19 + n_sh]
        gathered_refs = refs[19 + n_sh:19 + 2 * n_sh]
        stage_refs = refs[19 + 2 * n_sh:19 + 3 * n_sh]
        send_sems, recv_sems, local_sems = refs[19 + 3 * n_sh:]
        b, i = pl.program_id(0), pl.program_id(1)
        local, remote = _exchange_copies(stage_refs, gathered_refs, send_sems, recv_sems, local_sems, scatter=False)

        @pl.when((b == 0) & (i == 0))
        def _():
            for src, stage in zip(shard_refs, stage_refs):
                stage[...] = src[...].astype(stage.dtype)
            for cp in local + remote:
                cp.start()

        shift = mod_ref[0, 0:1, :]
        scale = mod_ref[0, 1:2, :]
        for part in range(TOKEN_PARTS):
            rows = pl.ds(part * TM, TM)
            xv = x_ref[0, rows, :]
            r = lax.rsqrt(jnp.mean(xv * xv, axis=-1, keepdims=True) + EPS)
            u = (xv * r * nw_ref[...]) * (1.0 + scale) + shift
            p = _dot_nt(u.astype(BF16), win_ref[...])
            pq = p[:, 0:DA]
            pk = p[:, DA:DA + HD * 2]
            ck = cos_ref[rows, :]
            sk = sin_ref[rows, :]
            cs = jnp.concatenate([ck] * (DA // KVW), axis=-1)
            sn = jnp.concatenate([sk] * (DA // KVW), axis=-1)
            rq = lax.rsqrt(_segsum(pq * pq, bd_ref[...]) * (1.0 / HD) + EPS)
            qn = pq * rq * qnw_ref[...]
            qr = _rope(qn, cs, sn) * 0.125
            for h in range(DA // HD):
                q_ref[0, h, rows, :] = qr[:, h * HD:(h + 1) * HD].astype(BF16)
            rk = lax.rsqrt(_segsum(pk * pk, bd_ref[0:KVW, 0:KVW]) * (1.0 / HD) + EPS)
            kn = pk * rk * knw_ref[...]
            kr = _rope(kn, ck, sk)
            pv = p[:, 640:768]
            for h in range(KVW // HD):
                k_ref[0, h, rows, :] = kr[:, h * HD:(h + 1) * HD].astype(BF16)
                v_ref[0, h, rows, :] = _with_ones_column(pv[:, h * HD:(h + 1) * HD]).astype(BF16)
            pq_ref[0, rows, :] = pq
            pkv_ref[0, rows, :] = p[:, 512:768]
            za_ref[0, rows, :] = p[:, 768:1280]
            glu_ref[0, rows, :] = p[:, 1280:2304]
            zc_ref[0, rows, :] = p[:, 2304:2816]

        @pl.when((b == bl - 1) & (i == nt - 1))
        def _():
            for cp in remote:
                cp.wait_recv()
            for cp in remote:
                cp.wait_send()
            for cp in local:
                cp.wait()

    def tile(w):
        return pl.BlockSpec((1, tm, w), lambda b, i: (b, i, 0))

    def const(shape):
        return pl.BlockSpec(shape, lambda b, i: (0,) * len(shape))

    outs = [(DA, F32), (2 * KVW, F32), (DA, F32), (2 * DC, F32), (DC, F32)]
    anyspace = pl.BlockSpec(memory_space=pl.ANY)
    rope = pl.BlockSpec((tm, KVW), lambda b, i: (i, 0))
    k_tile = pl.BlockSpec((1, KVW // HD, tm, HD), lambda b, i: (b, 0, i, 0))
    v_tile = pl.BlockSpec((1, KVW // HD, tm, 2 * HD), lambda b, i: (b, 0, i, 0))
    res = pl.pallas_call(
        body, name="fwd_in", grid=(bl, nt),
        in_specs=[tile(D), pl.BlockSpec((1, 3, D), lambda b, i: (b, 0, 0)), const((1, D)), const((D_IN, D)),
                  rope, rope, const((1, DA)), const((1, KVW)), const((DA, DA)), anyspace, anyspace]
        + [const(a.shape) for a in shards],
        out_specs=[pl.BlockSpec((1, DA // HD, tm, HD), lambda b, i: (b, 0, i, 0)), k_tile, v_tile]
        + [tile(w) for w, _ in outs] + [anyspace] * n_sh,
        out_shape=[jax.ShapeDtypeStruct((bl, DA // HD, s, HD), BF16), jax.ShapeDtypeStruct(k_all.shape, BF16),
                   jax.ShapeDtypeStruct(v_all.shape, BF16)]
        + [jax.ShapeDtypeStruct((bl, s, w), dt) for w, dt in outs]
        + [jax.ShapeDtypeStruct((N_DEV,) + a.shape, dt) for a, dt in zip(shards, wire_dtypes)],
        input_output_aliases={9: 1, 10: 2},
        scratch_shapes=[pltpu.VMEM(a.shape, dt) for a, dt in zip(shards, wire_dtypes)] + _exchange_scratch(n_sh),
        compiler_params=_params(("arbitrary", "arbitrary")),
    )(x, modrows, norm_w, w_in_b, cos, sins, qnw_t, knw_t, ones_bd, k_all, v_all, *shards)
    return res[:8], res[8:]


_KV_ROWS_OF_W_IN_T = pl.BlockSpec((2 * KVW, D), lambda b: (DA // (2 * KVW), 0))


def _ctx_fwd(ctx, modc, norm_w, w_kv_b, knw_t, ones_bd, n_keys):
    bl, cl, _ = ctx.shape

    def body(x_ref, mod_ref, nw_ref, w_ref, knw_ref, bd_ref, k_ref, v_ref, pkv_ref):
        xv = x_ref[0]
        shift = mod_ref[0, 0:1, :]
        scale = mod_ref[0, 1:2, :]
        r = lax.rsqrt(jnp.mean(xv * xv, axis=-1, keepdims=True) + EPS)
        u = (xv * r * nw_ref[...]) * (1.0 + scale) + shift
        p = _dot_nt(u.astype(BF16), w_ref[...])
        pk = p[:, 0:KVW]
        rk = lax.rsqrt(_segsum(pk * pk, bd_ref[...]) * (1.0 / HD) + EPS)
        kn = pk * rk * knw_ref[...]
        pv = p[:, KVW:2 * KVW]
        for h in range(KVW // HD):
            k_ref[0, h] = kn[:, h * HD:(h + 1) * HD].astype(BF16)
            v_ref[0, h] = _with_ones_column(pv[:, h * HD:(h + 1) * HD]).astype(BF16)
        pkv_ref[0] = p

    def const(shape):
        return pl.BlockSpec(shape, lambda b: (0,) * len(shape))

    def tile(w):
        return pl.BlockSpec((1, cl, w), lambda b: (b, 0, 0))

    ctx_block = (n_keys - cl) // cl
    assert ctx_block * cl + cl == n_keys
    k_tile = pl.BlockSpec((1, KVW // HD, cl, HD), lambda b: (b, 0, ctx_block, 0))
    v_tile = pl.BlockSpec((1, KVW // HD, cl, 2 * HD), lambda b: (b, 0, ctx_block, 0))
    return pl.pallas_call(
        body, name="ctx_fwd", grid=(bl,),
        in_specs=[tile(D), const((1, 3, D)), const((1, D)), _KV_ROWS_OF_W_IN_T, const((1, KVW)), const((KVW, KVW))],
        out_specs=[k_tile, v_tile, tile(2 * KVW)],
        out_shape=[jax.ShapeDtypeStruct((bl, KVW // HD, n_keys, HD), BF16),
                   jax.ShapeDtypeStruct((bl, KVW // HD, n_keys, 2 * HD), BF16),
                   jax.ShapeDtypeStruct((bl, cl, 2 * KVW), F32)],
        compiler_params=_params(("arbitrary",)),
    )(ctx, modc, norm_w, w_kv_b, knw_t, ones_bd)


def _attn_fwd(q, k, v1):
    bl, _, s, _ = q.shape
    n_keys = k.shape[2]

    def body(q_ref, k_ref, v_ref, o_ref, lse_ref):
        kv = k_ref[0, 0]
        vv = v_ref[0, 0]
        lane = lax.broadcasted_iota(jnp.int32, (TQ, 2 * HD), 1)
        for part in range(FWD_PARTS):
            rows = pl.ds(part * TQ, TQ)
            lse = jnp.zeros((TQ, 2 * HD), F32)
            heads = []
            sc_all = _dot_nt(q_ref[0, :, rows, :].reshape(4 * TQ, HD), kv)
            for h in range(4):
                sc = sc_all[h * TQ:(h + 1) * TQ, :]
                m = jnp.max(sc, axis=-1, keepdims=True)
                e = jnp.exp(sc - m).astype(BF16)
                ov = _dot(e, vv)
                denom = ov[:, HD:HD + 1]
                heads.append(ov[:, 0:HD] * (1.0 / denom))
                lse = jnp.where(lane == h, m + jnp.log(denom), lse)
            o_ref[0, rows, :] = jnp.concatenate(heads, axis=-1)
            lse_ref[0, 0, rows, :] = lse

    tq = FWD_PARTS * TQ
    ks = pl.BlockSpec((1, 1, n_keys, HD), lambda b, g, i: (b, g, 0, 0))
    qs = pl.BlockSpec((1, 4, tq, HD), lambda b, g, i: (b, g, i, 0))
    vs = pl.BlockSpec((1, 1, n_keys, 2 * HD), lambda b, g, i: (b, g, 0, 0))
    return pl.pallas_call(
        body, name="attn_fwd", grid=(bl, 2, s // tq), in_specs=[qs, ks, vs],
        out_specs=[pl.BlockSpec((1, tq, 4 * HD), lambda b, g, i: (b, i, g)),
                   pl.BlockSpec((1, 1, tq, 2 * HD), lambda b, g, i: (b, g, i, 0))],
        out_shape=[jax.ShapeDtypeStruct((bl, s, DA), F32), jax.ShapeDtypeStruct((bl, 2, s, 2 * HD), F32)],
        compiler_params=_params(("arbitrary", "arbitrary", "arbitrary")),
    )(q, k, v1)


def _attn_bwd(q, k, v1, do, o, lse, exchange):
    bl, _, s, _ = q.shape
    n_keys = k.shape[2]
    tq = BWD_PARTS * TQ
    nq = s // tq
    n_ex = len(exchange)

    def body(*refs):
        q_ref, k_ref, v_ref, do_ref, o_ref, lse_ref = refs[:6]
        part_refs = refs[6:6 + n_ex]
        dq_ref, dk_ref, dv_ref = refs[6 + n_ex:9 + n_ex]
        got_refs = refs[9 + n_ex:9 + 2 * n_ex]
        p_sc, ds_sc, dkt, dvt, send_sems, recv_sems, local_sems = refs[9 + 2 * n_ex:]
        i = pl.program_id(2)
        first = (pl.program_id(0) == 0) & (pl.program_id(1) == 0) & (i == 0)
        last = (pl.program_id(0) == bl - 1) & (pl.program_id(1) == 1) & (i == nq - 1)
        local, remote = _exchange_copies(part_refs, got_refs, send_sems, recv_sems, local_sems, scatter=True)

        @pl.when(first)
        def _():
            for cp in local + remote:
                cp.start()

        @pl.when(i == 0)
        def _():
            dkt[...] = jnp.zeros_like(dkt)
            dvt[...] = jnp.zeros_like(dvt)

        kv = k_ref[0, 0]
        vv = v_ref[0, 0][:, 0:HD]
        for part in range(BWD_PARTS):
            tq_rows = pl.ds(part * TQ, TQ)
            lse = lse_ref[0, 0, tq_rows, :]
            ov = o_ref[0, tq_rows, :]
            dqs = []
            q_cat = q_ref[0, :, tq_rows, :].reshape(4 * TQ, HD)
            do_cat = do_ref[0, :, tq_rows, :].reshape(4 * TQ, HD)
            sc_all = _dot_nt(q_cat, kv)
            for h in range(4):
                doh = do_cat[h * TQ:(h + 1) * TQ, :]
                delta = jnp.sum(ov[:, h * HD:(h + 1) * HD] * doh.astype(F32), axis=-1, keepdims=True)
                rows = pl.ds((part * 4 + h) * TQ, TQ)
                p = jnp.exp(sc_all[h * TQ:(h + 1) * TQ, :] - lse[:, h:h + 1])
                ds = (p * (_dot_nt(doh, vv) - delta)).astype(BF16)
                p_sc[rows, :] = p.astype(BF16)
                ds_sc[rows, :] = ds
                dqs.append(_dot(ds, kv) * 0.125)
            dq_ref[0, tq_rows, :] = jnp.concatenate(dqs, axis=-1)
            part_rows = pl.ds(part * 4 * TQ, 4 * TQ)
            dvt[...] += _dot_tn(do_cat, p_sc[part_rows, :])
            dkt[...] += _dot_tn(q_cat, ds_sc[part_rows, :])

        @pl.when(i == nq - 1)
        def _():
            dk_ref[0, 0] = dkt[...].T
            dv_ref[0, 0] = dvt[...].T

        @pl.when(last)
        def _():
            for cp in remote:
                cp.wait_recv()
            for cp in remote:
                cp.wait_send()
            for cp in local:
                cp.wait()

    qs = pl.BlockSpec((1, 4, tq, HD), lambda b, g, i: (b, g, i, 0))
    ks = pl.BlockSpec((1, 1, n_keys, HD), lambda b, g, i: (b, g, 0, 0))
    vs = pl.BlockSpec((1, 1, n_keys, 2 * HD), lambda b, g, i: (b, g, 0, 0))
    os_ = pl.BlockSpec((1, tq, 4 * HD), lambda b, g, i: (b, i, g))
    kshape = jax.ShapeDtypeStruct(k.shape, F32)
    anyspace = pl.BlockSpec(memory_space=pl.ANY)
    res = pl.pallas_call(
        body, name="attn_bwd", grid=(bl, 2, nq),
        in_specs=[qs, ks, vs, qs, os_, pl.BlockSpec((1, 1, tq, 2 * HD), lambda b, g, i: (b, g, i, 0))]
        + [anyspace] * n_ex,
        out_specs=[os_, ks, ks] + [anyspace] * n_ex,
        out_shape=[jax.ShapeDtypeStruct((bl, s, DA), F32), kshape, kshape]
        + [jax.ShapeDtypeStruct(a.shape, a.dtype) for a in exchange],
        scratch_shapes=[pltpu.VMEM((4 * tq, n_keys), BF16), pltpu.VMEM((4 * tq, n_keys), BF16),
                        pltpu.VMEM((HD, n_keys), F32), pltpu.VMEM((HD, n_keys), F32)] + _exchange_scratch(n_ex),
        compiler_params=_params(("arbitrary", "arbitrary", "arbitrary")),
    )(q, k, v1, do, o, lse, *exchange)
    return res[:3], res[3:]


def _halo_specs(width, s):
    per = TC // HALO
    last = s // HALO - 1
    main = pl.BlockSpec((1, TC, width), lambda b, i: (b, i, 0))
    prev = pl.BlockSpec((1, HALO, width), lambda b, i: (b, jnp.maximum(i * per - 1, 0), 0))
    nxt = pl.BlockSpec((1, HALO, width), lambda b, i: (b, jnp.minimum((i + 1) * per, last), 0))
    return main, prev, nxt


def _glu(g):
    return g[:, 0:DC] * _sigmoid(g[:, DC:2 * DC])


def _fill_padded(pad_ref, main, prev, nxt, first, last):
    tc = main.shape[0]
    pad_ref[0:HALO, :] = jnp.where(first, 0.0, prev)
    pad_ref[HALO:HALO + tc, :] = main
    pad_ref[HALO + tc:2 * HALO + tc, :] = jnp.where(last, 0.0, nxt)


PLANE_ROWS = TC + 2 * HALO - 8


def _shift_planes(pad_ref, planes_ref):
    for r in range(1, 8):
        planes_ref[r - 1] = pad_ref[pl.ds(r, planes_ref.shape[1]), :]


def _tap_rows(pad_ref, planes_ref, offset, start, n):
    a, r = divmod(offset, 8)
    if r == 0:
        return pad_ref[pl.ds(start + 8 * a, n), :]
    return planes_ref[r - 1, pl.ds(start + 8 * a, n), :]


def _conv_fwd(glu, conv_w, conv_b, ln_w, ln_b, w_pw_b, b_pw):
    bl, s, _ = glu.shape
    nt = s // TC

    def body(g_ref, gp_ref, gn_ref, cw_ref, cb_ref, lw_ref, lb_ref, wpw_ref, bpw_ref, y_ref, cp_ref, pad_ref, planes_ref):
        i = pl.program_id(1)
        _fill_padded(pad_ref, _glu(g_ref[0]), _glu(gp_ref[0]), _glu(gn_ref[0]), i == 0, i == nt - 1)
        _shift_planes(pad_ref, planes_ref)
        for ck in range(TC // CH):
            acc = jnp.zeros((CH, DC), F32) + cb_ref[...]
            for t in range(KW):
                acc = acc + _tap_rows(pad_ref, planes_ref, 1 + t, ck * CH, CH) * cw_ref[t:t + 1, :]
            y_ref[0, pl.ds(ck * CH, CH), :] = acc
        y = y_ref[0]
        mu = jnp.mean(y, axis=-1, keepdims=True)
        yc = y - mu
        var = jnp.mean(yc * yc, axis=-1, keepdims=True)
        z = yc * lax.rsqrt(var + EPS) * lw_ref[...] + lb_ref[...]
        act = z * _sigmoid(z)
        cp_ref[0] = _dot(act.astype(BF16), wpw_ref[...]) + bpw_ref[...]

    def const(shape):
        return pl.BlockSpec(shape, lambda b, i: (0,) * len(shape))

    main, prev, nxt = _halo_specs(2 * DC, s)
    tile = pl.BlockSpec((1, TC, DC), lambda b, i: (b, i, 0))
    return pl.pallas_call(
        body, name="conv_fwd", grid=(bl, nt),
        in_specs=[main, prev, nxt, const((32, DC)), const((1, DC)), const((1, DC)), const((1, DC)),
                  const((DC, DC)), const((1, DC))],
        out_specs=[tile, tile],
        out_shape=[jax.ShapeDtypeStruct((bl, s, DC), F32)] * 2,
        scratch_shapes=[pltpu.VMEM((TC + 2 * HALO, DC), F32), pltpu.VMEM((7, PLANE_ROWS, DC), F32)],
        compiler_params=_params(("arbitrary", "arbitrary")),
    )(glu, glu, glu, conv_w, conv_b, ln_w, ln_b, w_pw_b, b_pw)


def _conv_bwd_depthwise(glu, dy, conv_w):
    bl, s, _ = glu.shape
    nt = s // TC

    def body(g_ref, gp_ref, gn_ref, d_ref, dp_ref, dn_ref, cw_ref, dglu_ref, dcw_ref,
             padu_ref, padd_ref, planes_u, planes_d):
        i = pl.program_id(1)

        @pl.when((pl.program_id(0) == 0) & (i == 0))
        def _():
            dcw_ref[...] = jnp.zeros_like(dcw_ref)

        first, last = i == 0, i == nt - 1
        _fill_padded(padu_ref, _glu(g_ref[0]), _glu(gp_ref[0]), _glu(gn_ref[0]), first, last)
        _fill_padded(padd_ref, d_ref[0], dp_ref[0], dn_ref[0], first, last)
        _shift_planes(padu_ref, planes_u)
        _shift_planes(padd_ref, planes_d)
        for ck in range(TC // CH):
            acc = jnp.zeros((CH, DC), F32)
            for t in range(KW):
                acc = acc + _tap_rows(padd_ref, planes_d, 2 * HALO - 1 - t, ck * CH, CH) * cw_ref[t:t + 1, :]
            g = g_ref[0, pl.ds(ck * CH, CH), :]
            a = g[:, 0:DC]
            sg = _sigmoid(g[:, DC:2 * DC])
            dglu_ref[0, pl.ds(ck * CH, CH), 0:DC] = (acc * sg).astype(BF16)
            dglu_ref[0, pl.ds(ck * CH, CH), DC:2 * DC] = (acc * a * sg * (1.0 - sg)).astype(BF16)
        group = 4
        for t0 in range(0, KW, group):
            taps = range(t0, min(t0 + group, KW))
            acc8 = [jnp.zeros((8, DC), F32) for _ in taps]
            for ck in range(TC // CH):
                dchunk = d_ref[0, pl.ds(ck * CH, CH), :]
                for n, t in enumerate(taps):
                    prod = _tap_rows(padu_ref, planes_u, 1 + t, ck * CH, CH) * dchunk
                    acc8[n] = acc8[n] + jnp.sum(prod.reshape(CH // 8, 8, DC), axis=0)
            for n, t in enumerate(taps):
                dcw_ref[t:t + 1, :] += jnp.sum(acc8[n], axis=0, keepdims=True)

    gmain, gprev, gnext = _halo_specs(2 * DC, s)
    dmain, dprev, dnext = _halo_specs(DC, s)
    cw = pl.BlockSpec((32, DC), lambda b, i: (0, 0))
    return pl.pallas_call(
        body, name="conv_bwd_depthwise", grid=(bl, nt),
        in_specs=[gmain, gprev, gnext, dmain, dprev, dnext, cw],
        out_specs=[gmain, cw],
        out_shape=[jax.ShapeDtypeStruct((bl, s, 2 * DC), BF16), jax.ShapeDtypeStruct((32, DC), F32)],
        scratch_shapes=[pltpu.VMEM((TC + 2 * HALO, DC), F32)] * 2 + [pltpu.VMEM((7, PLANE_ROWS, DC), F32)] * 2,
        compiler_params=_params(("arbitrary", "arbitrary")),
    )(glu, glu, glu, dy, dy, dy, conv_w)


def _out_fwd_bwd(attn, za, cp, zc, x, target, modrows, w_out_b, y_conv, ln_w, ln_b, w_pw_b):
    bl, s, _ = x.shape
    tm = OUT_TM

    def body(o_ref, za_ref, cp_ref, zc_ref, x_ref, t_ref, mod_ref, w_ref, y_ref, lw_ref, lb_ref, wpw_ref,
             do_ref, dza_ref, dy_ref, dzc_ref, dh_ref, dgate_ref, gwb_ref, loss_ref, gpwb_ref, rows_ref,
             gw_ref, gpw_ref):
        b, i = pl.program_id(0), pl.program_id(1)

        @pl.when((b == 0) & (i == 0))
        def _():
            gw_ref[...] = jnp.zeros_like(gw_ref)
            gpw_ref[...] = jnp.zeros_like(gpw_ref)
            rows_ref[...] = jnp.zeros_like(rows_ref)
            loss_ref[...] = jnp.zeros_like(loss_ref)

        @pl.when(i == 0)
        def _():
            dgate_ref[...] = jnp.zeros_like(dgate_ref)

        gate = mod_ref[0, 2:3, :]
        w = w_ref[...]
        o, za_v, cp_v, zc_v = o_ref[0], za_ref[0], cp_ref[0], zc_ref[0]
        sa = _sigmoid(za_v)
        sc = _sigmoid(zc_v)
        silu_a = za_v * sa
        silu_c = zc_v * sc
        mix = jnp.concatenate([(o * silu_a).astype(BF16), (cp_v * silu_c).astype(BF16)], axis=-1)
        out = _dot(mix, w)
        err = x_ref[0] + gate * out - t_ref[0]
        loss_ref[...] += jnp.sum(err * err, axis=0, keepdims=True)
        dh = err * (1.0 / D)
        dh_ref[0] = dh
        dgate_ref[0] += jnp.sum(dh * out, axis=0, keepdims=True)
        dout = (dh * gate).astype(BF16)
        gw_ref[...] += _dot_tn(mix, dout)
        dmix = _dot_nt(dout, w)
        dga = dmix[:, 0:DA]
        dgc = dmix[:, DA:DA + DC]
        dov = dga * silu_a
        for h in range(DA // HD):
            do_ref[0, h] = dov[:, h * HD:(h + 1) * HD].astype(BF16)
        dza_ref[0] = (dga * o * (sa * (1.0 + za_v * (1.0 - sa)))).astype(BF16)
        dzc_ref[0] = (dgc * cp_v * (sc * (1.0 + zc_v * (1.0 - sc)))).astype(BF16)
        dcp = dgc * silu_c
        y = y_ref[0]
        yc = y - jnp.mean(y, axis=-1, keepdims=True)
        rstd = lax.rsqrt(jnp.mean(yc * yc, axis=-1, keepdims=True) + EPS)
        yn = yc * rstd
        lw = lw_ref[...]
        z = yn * lw + lb_ref[...]
        sg = _sigmoid(z)
        dcp_b = dcp.astype(BF16)
        gpw_ref[...] += _dot_tn((z * sg).astype(BF16), dcp_b)
        dz = _dot_nt(dcp_b, wpw_ref[...]) * (sg * (1.0 + z * (1.0 - sg)))
        dyn = dz * lw
        dy = rstd * (dyn - jnp.mean(dyn, axis=-1, keepdims=True) - yn * jnp.mean(dyn * yn, axis=-1, keepdims=True))
        dy_ref[0] = dy
        rows_ref[0:1, :] += jnp.sum(dcp, axis=0, keepdims=True)
        rows_ref[1:2, :] += jnp.sum(dz * yn, axis=0, keepdims=True)
        rows_ref[2:3, :] += jnp.sum(dz, axis=0, keepdims=True)
        rows_ref[3:4, :] += jnp.sum(dy, axis=0, keepdims=True)

        @pl.when((b == bl - 1) & (i == s // tm - 1))
        def _():
            gwb_ref[...] = gw_ref[...].astype(BF16)
            gpwb_ref[...] = gpw_ref[...].astype(BF16)

    def const(shape):
        return pl.BlockSpec(shape, lambda b, i: (0,) * len(shape))

    def tile(w):
        return pl.BlockSpec((1, tm, w), lambda b, i: (b, i, 0))

    return pl.pallas_call(
        body, name="out_fwd_bwd", grid=(bl, s // tm),
        in_specs=[tile(DA), tile(DA), tile(DC), tile(DC), tile(D), tile(D),
                  pl.BlockSpec((1, 3, D), lambda b, i: (b, 0, 0)), const((D, D)),
                  tile(DC), const((1, DC)), const((1, DC)), const((DC, DC))],
        out_specs=[pl.BlockSpec((1, DA // HD, tm, HD), lambda b, i: (b, 0, i, 0)), tile(DA), tile(DC), tile(DC), tile(D),
                   pl.BlockSpec((1, 1, D), lambda b, i: (b, 0, 0)), const((D, D)), const((1, D)),
                   const((DC, DC)), const((8, DC))],
        out_shape=[jax.ShapeDtypeStruct((bl, DA // HD, s, HD), BF16), jax.ShapeDtypeStruct((bl, s, DA), BF16),
                   jax.ShapeDtypeStruct((bl, s, DC), F32), jax.ShapeDtypeStruct((bl, s, DC), BF16),
                   jax.ShapeDtypeStruct((bl, s, D), F32), jax.ShapeDtypeStruct((bl, 1, D), F32),
                   jax.ShapeDtypeStruct((D, D), BF16), jax.ShapeDtypeStruct((1, D), F32),
                   jax.ShapeDtypeStruct((DC, DC), BF16), jax.ShapeDtypeStruct((8, DC), F32)],
        scratch_shapes=[pltpu.VMEM((D, D), F32), pltpu.VMEM((DC, DC), F32)],
        compiler_params=_params(("arbitrary", "arbitrary")),
    )(attn, za, cp, zc, x, target, modrows, w_out_b, y_conv, ln_w, ln_b, w_pw_b)


def _rms_heads_bwd(dy, x, w_t, ones_bd):
    r = lax.rsqrt(_segsum(x * x, ones_bd) * (1.0 / HD) + EPS)
    xh = x * r
    g = dy * w_t
    dx = r * (g - xh * (_segsum(g * xh, ones_bd) * (1.0 / HD)))
    return dx, dy * xh


def _ctx_bwd(ctx, modc, norm_w, w_kv_b, pkv_c, dk_c, dv_c, knw_t, ones_bd):
    bl, cl, _ = ctx.shape

    def body(x_ref, mod_ref, nw_ref, w_ref, p_ref, dk_ref, dv_ref, knw_ref, bd_ref, gw_ref, rows_ref, dknw_ref):
        @pl.when(pl.program_id(0) == 0)
        def _():
            gw_ref[...] = jnp.zeros_like(gw_ref)
            rows_ref[...] = jnp.zeros_like(rows_ref)
            dknw_ref[...] = jnp.zeros_like(dknw_ref)

        xv = x_ref[0]
        shift = mod_ref[0, 0:1, :]
        scale = mod_ref[0, 1:2, :]
        nw = nw_ref[...]
        r = lax.rsqrt(jnp.mean(xv * xv, axis=-1, keepdims=True) + EPS)
        xn = xv * r
        yv = xn * nw
        u = yv * (1.0 + scale) + shift
        dkv = jnp.concatenate([dk_ref[0, 0], dk_ref[0, 1]], axis=-1)
        dpk, dknw = _rms_heads_bwd(dkv, p_ref[0][:, 0:KVW], knw_ref[...], bd_ref[...])
        dp = jnp.concatenate([dpk.astype(BF16), dv_ref[0, 0].astype(BF16), dv_ref[0, 1].astype(BF16)], axis=-1)
        gw_ref[...] += _dot_tn(dp, u.astype(BF16))
        du = _dot(dp, w_ref[...])
        rows_ref[0:1, :] += jnp.sum(du, axis=0, keepdims=True)
        rows_ref[1:2, :] += jnp.sum(du * yv, axis=0, keepdims=True)
        rows_ref[2:3, :] += jnp.sum(du * (1.0 + scale) * xn, axis=0, keepdims=True)
        dknw_ref[...] += jnp.sum(dknw, axis=0, keepdims=True)

    def const(shape):
        return pl.BlockSpec(shape, lambda b: (0,) * len(shape))

    def tile(w):
        return pl.BlockSpec((1, cl, w), lambda b: (b, 0, 0))

    ctx_block = (dk_c.shape[2] - cl) // cl
    kv_tile = pl.BlockSpec((1, KVW // HD, cl, HD), lambda b: (b, 0, ctx_block, 0))
    return pl.pallas_call(
        body, name="ctx_bwd", grid=(bl,),
        in_specs=[tile(D), const((1, 3, D)), const((1, D)), _KV_ROWS_OF_W_IN_T, tile(2 * KVW), kv_tile, kv_tile,
                  const((1, KVW)), const((KVW, KVW))],
        out_specs=[const((2 * KVW, D)), const((8, D)), const((1, KVW))],
        out_shape=[jax.ShapeDtypeStruct((2 * KVW, D), F32), jax.ShapeDtypeStruct((8, D), F32),
                   jax.ShapeDtypeStruct((1, KVW), F32)],
        compiler_params=_params(("arbitrary",)),
    )(ctx, modc, norm_w, w_kv_b, pkv_c, dk_c, dv_c, knw_t, ones_bd)


def _bwd_in(x, modrows, norm_w, w_in_b, cos, sins, qnw_t, knw_t, ones_bd,
            pq, pkv, dq, dk, dv, dza, dglu, dzc, dh, gw_kv):
    bl, s, _ = x.shape
    tm = TOKEN_PARTS * TM
    nt = s // tm

    def body(x_ref, mod_ref, nw_ref, win_hbm, cos_ref, sin_ref, qnw_ref, knw_ref, bd_ref,
             pq_ref, pkv_ref, dq_ref, dk_ref, dv_ref, dza_ref, dglu_ref, dzc_ref, dh_ref, gwkv_ref,
             gx_ref, gw_hbm, dmod_ref, dnw_ref, dqnw_ref, dknw_ref, win_ref, gw_acc, sem):
        b, i = pl.program_id(0), pl.program_id(1)

        @pl.when((b == 0) & (i == 0))
        def _():
            cp = pltpu.make_async_copy(win_hbm, win_ref, sem)
            cp.start()
            gw_acc[...] = jnp.zeros_like(gw_acc)
            dnw_ref[...] = jnp.zeros_like(dnw_ref)
            dqnw_ref[...] = jnp.zeros_like(dqnw_ref)
            dknw_ref[...] = jnp.zeros_like(dknw_ref)
            cp.wait()

        @pl.when(i == 0)
        def _():
            dmod_ref[...] = jnp.zeros_like(dmod_ref)

        bd = bd_ref[...]
        shift = mod_ref[0, 0:1, :]
        scale = mod_ref[0, 1:2, :]
        nw = nw_ref[...]
        dps, us = [], []
        for part in range(TOKEN_PARTS):
            rows = pl.ds(part * TM, TM)
            ck = cos_ref[rows, :]
            sk = sin_ref[rows, :]
            cs = jnp.concatenate([ck] * (DA // KVW), axis=-1)
            sn = jnp.concatenate([sk] * (DA // KVW), axis=-1)
            dqn = _rope_bwd(dq_ref[0, rows, :], cs, sn)
            dpq, dqnw = _rms_heads_bwd(dqn, pq_ref[0, rows, :], qnw_ref[...], bd)
            dkn = _rope_bwd(jnp.concatenate([dk_ref[0, 0, rows, :], dk_ref[0, 1, rows, :]], axis=-1), ck, sk)
            dpk, dknw = _rms_heads_bwd(dkn, pkv_ref[0, rows, 0:KVW], knw_ref[...], bd[0:KVW, 0:KVW])
            dqnw_ref[...] += jnp.sum(dqnw, axis=0, keepdims=True)
            dknw_ref[...] += jnp.sum(dknw, axis=0, keepdims=True)
            dp = jnp.concatenate(
                [dpq.astype(BF16), dpk.astype(BF16), dv_ref[0, 0, rows, :].astype(BF16), dv_ref[0, 1, rows, :].astype(BF16),
                 dza_ref[0, rows, :], dglu_ref[0, rows, :], dzc_ref[0, rows, :]], axis=-1)

            xv = x_ref[0, rows, :]
            r = lax.rsqrt(jnp.mean(xv * xv, axis=-1, keepdims=True) + EPS)
            xn = xv * r
            yv = xn * nw
            u = yv * (1.0 + scale) + shift
            dps.append(dp)
            us.append(u.astype(BF16))
            du = _dot(dp, win_ref[...])
            dmod_ref[0, 0:1, :] += jnp.sum(du, axis=0, keepdims=True)
            dmod_ref[0, 1:2, :] += jnp.sum(du * yv, axis=0, keepdims=True)
            dy = du * (1.0 + scale)
            dnw_ref[...] += jnp.sum(dy * xn, axis=0, keepdims=True)
            dxn = dy * nw
            gx_ref[0, rows, :] = dh_ref[0, rows, :] + r * (dxn - xn * jnp.mean(dxn * xn, axis=-1, keepdims=True))
        gw_acc[...] += _dot_tn(jnp.concatenate(dps, axis=0), jnp.concatenate(us, axis=0))

        @pl.when((b == bl - 1) & (i == nt - 1))
        def _():
            gw_acc[DA:DA + 2 * KVW, :] += gwkv_ref[...]

            def to_bf16(j, carry):
                rows = pl.ds(pl.multiple_of(j * 2 * KVW, 2 * KVW), 2 * KVW)
                win_ref[rows, :] = gw_acc[rows, :].astype(BF16)
                return carry

            lax.fori_loop(0, D_IN // (2 * KVW), to_bf16, 0)
            pltpu.sync_copy(win_ref, gw_hbm)

    def tile(w):
        return pl.BlockSpec((1, tm, w), lambda b, i: (b, i, 0))

    def const(shape):
        return pl.BlockSpec(shape, lambda b, i: (0,) * len(shape))

    anyspace = pl.BlockSpec(memory_space=pl.ANY)
    rope = pl.BlockSpec((tm, KVW), lambda b, i: (i, 0))
    kv_tile = pl.BlockSpec((1, KVW // HD, tm, HD), lambda b, i: (b, 0, i, 0))
    return pl.pallas_call(
        body, name="bwd_in", grid=(bl, nt),
        in_specs=[tile(D), pl.BlockSpec((1, 3, D), lambda b, i: (b, 0, 0)), const((1, D)), anyspace, rope, rope,
                  const((1, DA)), const((1, KVW)), const((DA, DA)),
                  tile(DA), tile(2 * KVW), tile(DA), kv_tile, kv_tile, tile(DA), tile(2 * DC), tile(DC), tile(D),
                  const((2 * KVW, D))],
        out_specs=[tile(D), anyspace, pl.BlockSpec((1, 2, D), lambda b, i: (b, 0, 0)), const((1, D)),
                   const((1, DA)), const((1, KVW))],
        out_shape=[jax.ShapeDtypeStruct((bl, s, D), F32), jax.ShapeDtypeStruct((D_IN, D), BF16),
                   jax.ShapeDtypeStruct((bl, 2, D), F32), jax.ShapeDtypeStruct((1, D), F32),
                   jax.ShapeDtypeStruct((1, DA), F32), jax.ShapeDtypeStruct((1, KVW), F32)],
        scratch_shapes=[pltpu.VMEM((D_IN, D), BF16), pltpu.VMEM((D_IN, D), F32), pltpu.SemaphoreType.DMA],
        compiler_params=_params(("arbitrary", "arbitrary")),
    )(x, modrows, norm_w, w_in_b, cos, sins, qnw_t, knw_t, ones_bd,
      pq, pkv, dq, dk, dv, dza, dglu, dzc, dh, gw_kv)


_LOSS, _DMODC, _NW, _QN, _KN, _CB, _LW, _LB, _BPW, SMALL_W = 0, 1024, 4096, 5120, 5248, 5376, 5888, 6400, 6912, 7424


ROW_W = 1792


def _put_flat(ref, off, value):
    n, done = value.shape[1], 0
    while done < n:
        r, c = divmod(off + done, ROW_W)
        take = min(n - done, ROW_W - c)
        ref[r:r + 1, c:c + take] = value[:, done:done + take]
        done += take


def _get_flat(arr, off, n):
    parts, done = [], 0
    while done < n:
        r, c = divmod(off + done, ROW_W)
        take = min(n - done, ROW_W - c)
        parts.append(arr[r:r + 1, c:c + take])
        done += take
    return parts[0] if len(parts) == 1 else jnp.concatenate(parts, axis=-1)


def _pack_small_body(loss_ref, ctx_ref, dnw_ref, dqnw_ref, dknw_ref, dknwc_ref, conv_ref, dss_ref, dgate_ref, o_ref):
    bl = dss_ref.shape[0]
    assert SMALL_W + bl * 3 * D <= 8 * ROW_W
    o_ref[...] = jnp.zeros_like(o_ref)
    _put_flat(o_ref, _LOSS, loss_ref[...])
    _put_flat(o_ref, _DMODC, ctx_ref[0:1, :])
    _put_flat(o_ref, _DMODC + D, ctx_ref[1:2, :])
    _put_flat(o_ref, _NW, dnw_ref[...] + ctx_ref[2:3, :])
    dq = dqnw_ref[...]
    qn = dq[:, 0:HD]
    for h in range(1, DA // HD):
        qn = qn + dq[:, h * HD:(h + 1) * HD]
    _put_flat(o_ref, _QN, qn)
    dk = dknw_ref[...] + dknwc_ref[...]
    _put_flat(o_ref, _KN, dk[:, 0:HD] + dk[:, HD:2 * HD])
    _put_flat(o_ref, _BPW, conv_ref[0:1, :])
    _put_flat(o_ref, _LW, conv_ref[1:2, :])
    _put_flat(o_ref, _LB, conv_ref[2:3, :])
    _put_flat(o_ref, _CB, conv_ref[3:4, :])
    for b in range(bl):
        _put_flat(o_ref, SMALL_W + b * 3 * D, dss_ref[b, 0:1, :])
        _put_flat(o_ref, SMALL_W + b * 3 * D + D, dss_ref[b, 1:2, :])
        _put_flat(o_ref, SMALL_W + b * 3 * D + 2 * D, dgate_ref[b])


_SMALL = (("b_mod", None), ("norm_w", _NW), ("q_norm_w", _QN), ("k_norm_w", _KN), ("conv_b", _CB),
          ("conv_ln_w", _LW), ("conv_ln_b", _LB), ("b_pw", _BPW), ("c_ctx", None))


def _epilogue(parts_in, pieces, c_rows, w_mod_loc):
    bl = pieces[7].shape[0]
    n_ex = N_DEV * bl
    n_mod = w_mod_loc.shape[1]
    rb = 32
    shp = parts_in.shape[1:]
    rows_in = shp[0]

    def body(*refs):
        it = iter(refs)
        take = lambda k: [next(it) for _ in range(k)]
        (parts,) = take(1)
        piece_refs = take(9)
        (c_ref, wm_ref) = take(2)
        (g_in, g_wm, sum_ref, gb_ref, gc_all, loss_ref) = take(6)
        (mine, got_sib, stage, got_chip, payload, gathered, dmod_full, gc_mine) = take(8)
        (d2d_send, d2d_recv, ici_send, ici_recv, local_sems, sg_send, sg_recv, gc_send, gc_recv, misc_sems) = take(10)

        x, y, c = _coords()
        me = _lin(x, y, c)
        sib = (x, y, 1 - c)
        home = 2 * x + y

        def rows_loop(fn):
            def step(i, carry):
                fn(pl.ds(pl.multiple_of(i * rb, rb), rb))
                return carry
            lax.fori_loop(0, rows_in // rb, step, 0)

        def direct_gather(src, dst, send_sems, recv_sems, local_sem):
            cps = [pltpu.make_async_copy(src, dst.at[me], local_sem)]
            for k in range(1, N_DEV):
                peer = (1 - x if k & 4 else x, 1 - y if k & 2 else y, 1 - c if k & 1 else c)
                cps.append(pltpu.make_async_remote_copy(
                    src_ref=src, dst_ref=dst.at[me], send_sem=send_sems.at[k - 1], recv_sem=recv_sems.at[k - 1],
                    device_id=peer, device_id_type=MESH_ID))
            for cp in cps:
                cp.start()
            return cps

        _pack_small_body(*piece_refs, payload)
        small_cps = direct_gather(payload, gathered, sg_send, sg_recv, misc_sems.at[0])

        local, d2d, ici = [], [], []
        for s in range(4):
            cp = pltpu.make_async_copy(parts.at[_lin(s // 2, s % 2, c)], mine.at[s], local_sems.at[s])
            cp.start()
            local.append(cp)
            rc = pltpu.make_async_remote_copy(
                src_ref=parts.at[_lin(s // 2, s % 2, 1 - c)], dst_ref=got_sib.at[s],
                send_sem=d2d_send.at[s], recv_sem=d2d_recv.at[s], device_id=sib, device_id_type=MESH_ID)
            rc.start()
            d2d.append(rc)

        for cp in small_cps[1:]:
            cp.wait_recv()
        small_cps[0].wait()
        tot = gathered[0]
        for j in range(1, N_DEV):
            tot = tot + gathered[j]
        summed = _get_flat(tot, 0, SMALL_W)
        dmod_full[...] = jnp.zeros_like(dmod_full)
        for j in range(N_DEV):
            arr = gathered[j]
            for b in range(bl):
                dmod_full[j * bl + b:j * bl + b + 1, :] = _get_flat(arr, SMALL_W + b * 3 * D, 3 * D)
        dmod_full[n_ex:n_ex + 1, :] = summed[:, _DMODC:_DMODC + 3 * D]
        sum_ref[...] = summed
        gb_ref[...] = jnp.sum(dmod_full[...], axis=0, keepdims=True)
        loss_ref[...] = (0.5 / D) * jnp.sum(summed[:, _LOSS:_LOSS + D], axis=-1, keepdims=True)

        north = c == 1
        first = (jnp.where(north, 1 - x, x), jnp.where(north, y, 1 - y))
        second = (jnp.where(north, x, 1 - x), jnp.where(north, 1 - y, y))
        for s in range(4):
            local[s].wait()
            d2d[s].wait_recv()

        def chip_sum(k, chip, relayed):
            slot = 2 * chip[0] + chip[1]

            def pair_sum(rs):
                acc = mine[slot, rs, :].astype(F32) + got_sib[slot, rs, :].astype(F32)
                if relayed:
                    acc = acc + got_chip[1, rs, :].astype(F32)
                stage[k, rs, :] = acc.astype(BF16)

            rows_loop(pair_sum)

        def send(k, to):
            rc = pltpu.make_async_remote_copy(
                src_ref=stage.at[k], dst_ref=got_chip.at[k], send_sem=ici_send.at[k], recv_sem=ici_recv.at[k],
                device_id=(to[0], to[1], c), device_id_type=MESH_ID)
            rc.start()
            ici.append(rc)

        chip_sum(0, first, False)
        send(0, first)
        chip_sum(1, (1 - x, 1 - y), False)
        send(1, first)

        cr = c_ref[...]
        act = (cr * _sigmoid(cr)).astype(BF16)
        dm = dmod_full[:, pl.ds(pl.multiple_of(me * n_mod, 128), n_mod)].astype(BF16)
        g_wm[...] = _dot_tn(act, dm)
        gc_mine[...] = _dot_nt(dm[n_ex:n_ex + 8, :], wm_ref[...].astype(BF16))
        gc_cps = direct_gather(gc_mine, gc_all, gc_send, gc_recv, misc_sems.at[1])

        ici[1].wait_recv()
        chip_sum(2, second, True)
        send(2, second)
        ici[0].wait_recv()
        ici[2].wait_recv()

        def finish(rs):
            gsum = mine[home, rs, :].astype(F32) + got_sib[home, rs, :].astype(F32)
            g_in[rs, :] = gsum + got_chip[0, rs, :].astype(F32) + got_chip[2, rs, :].astype(F32)

        rows_loop(finish)

        for cp in gc_cps[1:]:
            cp.wait_recv()
        gc_cps[0].wait()
        for rc in d2d + ici + small_cps[1:] + gc_cps[1:]:
            rc.wait_send()

    vm = pl.BlockSpec(memory_space=pltpu.VMEM)
    anyspace = pl.BlockSpec(memory_space=pl.ANY)
    assert rows_in % rb == 0 and parts_in.dtype == BF16
    args = [parts_in, *pieces, c_rows, w_mod_loc]
    in_specs = [anyspace] + [vm] * (len(args) - 1)
    out_shape = [jax.ShapeDtypeStruct(shp, F32), jax.ShapeDtypeStruct(w_mod_loc.shape, F32),
                 jax.ShapeDtypeStruct((1, SMALL_W), F32), jax.ShapeDtypeStruct((1, 3 * D), F32),
                 jax.ShapeDtypeStruct((N_DEV, 8, D), F32), jax.ShapeDtypeStruct((1, 1), F32)]
    scratch = [pltpu.VMEM((4,) + shp, BF16), pltpu.VMEM((4,) + shp, BF16), pltpu.VMEM((3,) + shp, BF16),
               pltpu.VMEM((3,) + shp, BF16), pltpu.VMEM((8, ROW_W), F32), pltpu.VMEM((N_DEV, 8, ROW_W), F32),
               pltpu.VMEM((n_ex + 8, 3 * D), F32), pltpu.VMEM((8, D), F32),
               pltpu.SemaphoreType.DMA((4,)), pltpu.SemaphoreType.DMA((4,)), pltpu.SemaphoreType.DMA((3,)),
               pltpu.SemaphoreType.DMA((3,)), pltpu.SemaphoreType.DMA((4,)),
               pltpu.SemaphoreType.DMA((N_DEV - 1,)), pltpu.SemaphoreType.DMA((N_DEV - 1,)),
               pltpu.SemaphoreType.DMA((N_DEV - 1,)), pltpu.SemaphoreType.DMA((N_DEV - 1,)),
               pltpu.SemaphoreType.DMA((2,))]
    return pl.pallas_call(
        body, name="epilogue", out_shape=out_shape, in_specs=in_specs, out_specs=[vm] * len(out_shape),
        scratch_shapes=scratch, compiler_params=pltpu.CompilerParams(vmem_limit_bytes=VMEM_LIMIT),
    )(*args)


def _final_adamw(g_in, w_in_t, m_in_t, v_in_t, g_wm, w_mod_loc, m_mod, v_mod, summed, g_bmod, gc_all,
                 small_w, small_m, small_v):
    ns = len(_SMALL)

    def body(*refs):
        it = iter(refs)
        take = lambda k: [next(it) for _ in range(k)]
        (gin_ref, w_ref, m_ref, v_ref, gwm_ref, wm_ref, mm_ref, vm_ref, sum_ref, gb_ref, gc_ref) = take(11)
        sw, sm, sv = take(ns), take(ns), take(ns)
        (d_in, nm_in, nv_in, d_wm, nm_wm, nv_wm) = take(6)
        souts = take(4 * ns)

        for g_r, w_r, m_r, v_r, outs3 in ((gin_ref, w_ref, m_ref, v_ref, (d_in, nm_in, nv_in)),
                                          (gwm_ref, wm_ref, mm_ref, vm_ref, (d_wm, nm_wm, nv_wm))):
            for o_r, val in zip(outs3, _adamw(w_r[...], g_r[...], m_r[...], v_r[...])):
                o_r[...] = val

        @pl.when(pl.program_id(0) == 0)
        def _():
            for k, (name, off) in enumerate(_SMALL):
                w = sw[k][...]
                if name == "b_mod":
                    gk = gb_ref[...]
                elif name == "c_ctx":
                    acc = gc_ref[0, 0:1, :]
                    for j in range(1, N_DEV):
                        acc = acc + gc_ref[j, 0:1, :]
                    sg = _sigmoid(w)
                    gk = acc * (sg * (1.0 + w * (1.0 - sg)))
                else:
                    gk = sum_ref[:, off:off + w.shape[1]]
                dl, m_new, v_new = _adamw(w, gk, sm[k][...], sv[k][...])
                souts[k][...] = gk
                souts[ns + k][...] = dl
                souts[2 * ns + k][...] = m_new
                souts[3 * ns + k][...] = v_new

    rows_in, rows_mod = w_in_t.shape[0] // ADAM_STEPS, w_mod_loc.shape[0] // ADAM_STEPS
    assert rows_in % 8 == 0 and rows_mod % 8 == 0

    def whole(a):
        return pl.BlockSpec(a.shape, lambda i: (0,) * a.ndim)

    in_tile = pl.BlockSpec((rows_in, w_in_t.shape[1]), lambda i: (i, 0))
    mod_tile = pl.BlockSpec((rows_mod, w_mod_loc.shape[1]), lambda i: (i, 0))
    small_in = [summed, g_bmod, gc_all, *small_w, *small_m, *small_v]
    big_shape = jax.ShapeDtypeStruct(w_in_t.shape, F32)
    mod_shape = jax.ShapeDtypeStruct(w_mod_loc.shape, F32)
    out_shape = [big_shape] * 3 + [mod_shape] * 3 + [jax.ShapeDtypeStruct(w.shape, F32) for w in small_w] * 4
    outs = pl.pallas_call(
        body, name="final_adamw", grid=(ADAM_STEPS,), out_shape=out_shape,
        in_specs=[in_tile] * 4 + [mod_tile] * 4 + [whole(a) for a in small_in],
        out_specs=[in_tile] * 3 + [mod_tile] * 3 + [whole(w) for w in small_w] * 4,
        compiler_params=_params(("arbitrary",)),
    )(g_in, w_in_t, m_in_t, v_in_t, g_wm, w_mod_loc, m_mod, v_mod, *small_in)
    small_outs = [outs[6 + k * ns:6 + (k + 1) * ns] for k in range(4)]
    return (g_in,) + tuple(outs[0:3]), (g_wm,) + tuple(outs[3:6]), small_outs


def _rope_tables(s):
    t = jnp.arange(s, dtype=jnp.int32)
    row = (t // GRID_W).astype(F32)
    col = (t % GRID_W).astype(F32)
    freqs = ROPE_THETA ** (-jnp.arange(0, HD // 2, 2, dtype=F32) / (HD // 2))
    ang_r = row[:, None] * freqs[None, :]
    ang_c = col[:, None] * freqs[None, :]
    cr, sr, cc, sc = jnp.cos(ang_r), jnp.sin(ang_r), jnp.cos(ang_c), jnp.sin(ang_c)
    cos = jnp.concatenate([cr, cr, cc, cc], axis=-1)
    sins = jnp.concatenate([-sr, sr, -sc, sc], axis=-1)
    return jnp.tile(cos, (1, KVW // HD)), jnp.tile(sins, (1, KVW // HD))


def kernel(x, c, ctx, c_ctx, w_mod, b_mod, norm_w, w_in, q_norm_w, k_norm_w, conv_w, conv_b, conv_ln_w, conv_ln_b, w_pw, b_pw, w_out, loss_target, m_c_ctx, m_w_mod, m_b_mod, m_norm_w, m_w_in, m_q_norm_w, m_k_norm_w, m_conv_w, m_conv_b, m_conv_ln_w, m_conv_ln_b, m_w_pw, m_b_pw, m_w_out, v_c_ctx, v_w_mod, v_b_mod, v_norm_w, v_w_in, v_q_norm_w, v_k_norm_w, v_conv_w, v_conv_b, v_conv_ln_w, v_conv_ln_b, v_w_pw, v_b_pw, v_w_out):
    bl, s, _ = x.shape
    cl = ctx.shape[1]
    me = _lin(*_coords())

    conv_w_pad = jnp.pad(conv_w[0], ((0, 32 - KW), (0, 0)))
    n_ex = N_DEV * bl
    g_win, c_rows, g_mod = _prologue(w_in[0].T, c, c_ctx[None, :], w_mod[0], b_mod)
    w_in_b = g_win.reshape(D_IN, D)
    mod_all = g_mod.transpose(1, 0, 2).reshape(n_ex + 8, 3 * D)
    modrows = lax.dynamic_slice_in_dim(mod_all, me * bl, bl, axis=0).reshape(bl, 3, D)
    modc = mod_all[n_ex].reshape(1, 3, D)

    cos, sins = _rope_tables(s)
    qnw_t = jnp.tile(q_norm_w, (1, DA // HD))
    knw_t = jnp.tile(k_norm_w, (1, KVW // HD))
    lane = jnp.arange(DA, dtype=jnp.int32) // HD
    ones_bd = (lane[:, None] == lane[None, :]).astype(BF16)
    ones_kv = ones_bd[0:KVW, 0:KVW]
    w_kv_b = w_in_b

    k_ctx, v_ctx, pkv_c = _ctx_fwd(ctx, modc, norm_w, w_kv_b, knw_t, ones_kv, cl + s)
    (q_h, k_h, v_h, pq, pkv, za, glu, zc), (g_wout, g_wpw, g_cw) = _fwd_in(
        x, modrows, norm_w, w_in_b, cos, sins, qnw_t, knw_t, ones_bd, k_ctx, v_ctx,
        [w_out[0], w_pw[0], conv_w_pad], [BF16, BF16, F32])
    w_out_b = g_wout.reshape(D, D)
    w_pw_b = g_wpw.reshape(DC, DC)
    conv_w_full = g_cw.transpose(1, 0, 2).reshape(32, DC)
    attn, lse = _attn_fwd(q_h, k_h, v_h)
    y_conv, cp = _conv_fwd(glu, conv_w_full, conv_b, conv_ln_w, conv_ln_b, w_pw_b, b_pw)

    do_h, dza, dy_conv, dzc, dh, dgate, gw_out, loss_row, gw_pw, conv_rows = _out_fwd_bwd(
        attn, za, cp, zc, x, loss_target, modrows, w_out_b, y_conv, conv_ln_w, conv_ln_b, w_pw_b)
    dglu, g_cw_full = _conv_bwd_depthwise(glu, dy_conv, conv_w_full)
    parts_out = gw_out.reshape(N_DEV, D // N_DEV, D)
    parts_pw = gw_pw.reshape(N_DEV, DC // N_DEV, DC)
    parts_cw = g_cw_full.astype(BF16).reshape(32, N_DEV, DC // N_DEV).transpose(1, 0, 2)
    (dq, dk_h, dv_h), (got_out, got_pw, got_cw) = _attn_bwd(
        q_h, k_h, v_h, do_h, attn, lse, [parts_out, parts_pw, parts_cw])
    gw_kv, ctx_rows, dknw_c = _ctx_bwd(ctx, modc, norm_w, w_kv_b, pkv_c, dk_h, dv_h, knw_t, ones_kv)
    grad_x, gw_in, dmod_ss, dnw, dqnw, dknw = _bwd_in(
        x, modrows, norm_w, w_in_b, cos, sins, qnw_t, knw_t, ones_bd,
        pq, pkv, dq, dk_h, dv_h, dza, dglu, dzc, dh, gw_kv)

    r_out, r_pw, r_cw = _sum_devices_adamw(
        [(got_out, w_out[0], m_w_out[0], v_w_out[0]), (got_pw, w_pw[0], m_w_pw[0], v_w_pw[0]),
         (got_cw, conv_w[0], m_conv_w[0], v_conv_w[0])])

    given = {"c_ctx": (c_ctx, m_c_ctx, v_c_ctx), "b_mod": (b_mod, m_b_mod, v_b_mod), "norm_w": (norm_w, m_norm_w, v_norm_w),
             "q_norm_w": (q_norm_w, m_q_norm_w, v_q_norm_w), "k_norm_w": (k_norm_w, m_k_norm_w, v_k_norm_w),
             "conv_b": (conv_b, m_conv_b, v_conv_b), "conv_ln_w": (conv_ln_w, m_conv_ln_w, v_conv_ln_w),
             "conv_ln_b": (conv_ln_b, m_conv_ln_b, v_conv_ln_b), "b_pw": (b_pw, m_b_pw, v_b_pw)}
    as_rows = [[given[name][which].reshape(1, -1) for name, _ in _SMALL] for which in range(3)]
    g_in_t, g_wmod, summed, g_bmod, gc_all, loss11 = _epilogue(
        gw_in.reshape(N_DEV, D_IN // N_DEV, D),
        [loss_row, ctx_rows, dnw, dqnw, dknw, dknw_c, conv_rows, dmod_ss, dgate], c_rows, w_mod[0])
    r_in, r_wmod, small_outs = _final_adamw(
        g_in_t, w_in[0].T, m_w_in[0].T, v_w_in[0].T, g_wmod, w_mod[0], m_w_mod[0], v_w_mod[0],
        summed, g_bmod, gc_all, *as_rows)
    r_in = tuple(a.T for a in r_in)

    big = {"w_mod": r_wmod, "w_in": r_in, "conv_w": r_cw, "w_pw": r_pw, "w_out": r_out}
    order = ["c_ctx", "w_mod", "b_mod", "norm_w", "w_in", "q_norm_w", "k_norm_w", "conv_w", "conv_b", "conv_ln_w",
             "conv_ln_b", "w_pw", "b_pw", "w_out"]
    small_index = {name: k for k, (name, _) in enumerate(_SMALL)}
    outs = [loss11.reshape(()), grad_x]
    for which in range(4):
        for name in order:
            if name in big:
                outs.append(big[name][which][None])
            else:
                outs.append(small_outs[which][small_index[name]].reshape(given[name][0].shape))
    return tuple(outs)
```

```python
import jax
import jax.numpy as jnp
from jax import lax
from jax.experimental import pallas as pl
from jax.experimental.pallas import tpu as pltpu

F32, BF16 = jnp.float32, jnp.bfloat16
MESH_ID = pl.DeviceIdType.MESH

N_DEV = 8
D = 1024
D_IN = 2816
DA = 512
DC = 512
HD = 64
KVW = 128
KW = 31
HALO = 16
EPS = 1e-6
ROPE_THETA = 10000.0
GRID_W = 64

ADAM_LR, ADAM_B1, ADAM_B2, ADAM_EPS, ADAM_WD, ADAM_STEP = 0.001, 0.9, 0.999, 1e-08, 0.01, 10

VMEM_LIMIT = 56 * 1024 * 1024

TM = 256
TQ = 128
TOKEN_PARTS = 2
OUT_TM = 512
BWD_SLOTS = 4
BWD_PARTS = 8
FWD_PARTS = 8
TC = 512
CH = 32
ADAM_STEPS = 4


def _params(sem, vmem=VMEM_LIMIT):
    return pltpu.CompilerParams(dimension_semantics=sem, vmem_limit_bytes=vmem)


def _dot(a, b):
    return jnp.dot(a, b, preferred_element_type=F32)


def _dot_nt(a, b):
    return lax.dot_general(a, b, (((1,), (1,)), ((), ())), preferred_element_type=F32)


def _dot_tn(a, b):
    return lax.dot_general(a, b, (((0,), (0,)), ((), ())), preferred_element_type=F32)


def _sigmoid(z):
    return 1.0 / (1.0 + jnp.exp(-z))


def _segsum(v, ones_bd):
    return _dot(v.astype(BF16), ones_bd)


def _swap16(x):
    w = x.shape[-1]
    lane = lax.broadcasted_iota(jnp.int32, x.shape, 1)
    return jnp.where((lane % 32) < 16, pltpu.roll(x, w - 16, 1), pltpu.roll(x, 16, 1))


def _with_ones_column(v):
    one = (lax.broadcasted_iota(jnp.int32, v.shape, 1) == 0).astype(v.dtype)
    return jnp.concatenate([v, one], axis=-1)


def _rope(x, cos, sins):
    return x * cos + _swap16(x) * sins


def _rope_bwd(d, cos, sins):
    return d * cos + _swap16(d * sins)


def _adamw(w, g, m, v):
    m2 = ADAM_B1 * m + (1.0 - ADAM_B1) * g
    v2 = ADAM_B2 * v + (1.0 - ADAM_B2) * (g * g)
    m_hat = m2 / (1.0 - ADAM_B1 ** ADAM_STEP)
    v_hat = v2 / (1.0 - ADAM_B2 ** ADAM_STEP)
    delta = -ADAM_LR * (m_hat / (jnp.sqrt(v_hat) + ADAM_EPS) + ADAM_WD * w)
    return delta, m2, v2


def _coords():
    return lax.axis_index("x"), lax.axis_index("y"), lax.axis_index("c")


def _lin(x, y, c):
    return 4 * x + 2 * y + c


def _prologue(w_in_t, c, c_ctx_row, w_mod_loc, b_mod):
    bl = c.shape[0]
    n_ex = N_DEV * bl
    n_mod = w_mod_loc.shape[1]

    def body(w32_ref, c_in_ref, cctx_ref, wm_ref, b_ref, out_w, crows_ref, mod_out, w_ref, c_ref, c_gath, mod_mine,
             w_send, w_recv, c_send, c_recv, m_send, m_recv, local_sems):
        x, y, c = _coords()
        me_lin = _lin(x, y, c)
        c_ref[...] = jnp.zeros_like(c_ref)
        c_ref[0:bl, :] = c_in_ref[...]
        w_ref[...] = w32_ref[...].astype(BF16)
        me, sib = (x, y, c), (x, y, 1 - c)
        xnb, ynb, diag = (1 - x, y), (x, 1 - y), (1 - x, 1 - y)
        north = c == 1

        def direct_gather(src, dst, send_sems, recv_sems, local_sem):
            cps = [pltpu.make_async_copy(src, dst.at[me_lin], local_sem)]
            for k in range(1, N_DEV):
                peer = (1 - x if k & 4 else x, 1 - y if k & 2 else y, 1 - c if k & 1 else c)
                cps.append(pltpu.make_async_remote_copy(
                    src_ref=src, dst_ref=dst.at[me_lin], send_sem=send_sems.at[k - 1], recv_sem=recv_sems.at[k - 1],
                    device_id=peer, device_id_type=MESH_ID))
            for cp in cps:
                cp.start()
            return cps

        def copy(k, block, to, src=None):
            slot = out_w.at[_lin(*block)]
            return pltpu.make_async_remote_copy(
                src_ref=slot if src is None else src, dst_ref=slot, send_sem=w_send.at[k], recv_sem=w_recv.at[k],
                device_id=to, device_id_type=MESH_ID)

        c_cps = direct_gather(c_ref, c_gath, c_send, c_recv, local_sems.at[0])
        mine = pltpu.make_async_copy(w_ref, out_w.at[me_lin], local_sems.at[1])
        mine.start()
        first = [copy(0, me, sib, src=w_ref), copy(1, me, (*xnb, c), src=w_ref), copy(2, me, (*ynb, c), src=w_ref)]
        for cp in first:
            cp.start()

        for cp in c_cps[1:]:
            cp.wait_recv()
        c_cps[0].wait()
        crows_ref[...] = jnp.zeros_like(crows_ref)
        for j in range(N_DEV):
            crows_ref[j * bl:(j + 1) * bl, :] = c_gath[j, 0:bl, :]
        crows_ref[n_ex:n_ex + 1, :] = cctx_ref[...]
        cr = crows_ref[...]
        act = (cr * _sigmoid(cr)).astype(BF16)
        mod_mine[...] = _dot(act, wm_ref[...].astype(BF16)) + b_ref[:, pl.ds(pl.multiple_of(me_lin * n_mod, 128), n_mod)]
        mod_cps = direct_gather(mod_mine, mod_out, m_send, m_recv, local_sems.at[2])

        relay_north = copy(3, (*xnb, c), (*ynb, c))
        relay_south = copy(3, (*ynb, c), (*xnb, c))
        passed = []
        copy(1, (*xnb, c), me).wait_recv()
        pl.when(north)(relay_north.start)
        passed.append(copy(4, (*xnb, c), sib))
        passed[-1].start()
        copy(2, (*ynb, c), me).wait_recv()
        pl.when(jnp.logical_not(north))(relay_south.start)
        passed.append(copy(5, (*ynb, c), sib))
        passed[-1].start()
        copy(3, (*diag, c), me).wait_recv()
        passed.append(copy(6, (*diag, c), sib))
        passed[-1].start()
        copy(0, sib, me).wait_recv()
        for k, chip in ((4, xnb), (5, ynb), (6, diag)):
            copy(k, (*chip, 1 - c), me).wait_recv()
        for cp in mod_cps[1:]:
            cp.wait_recv()
        mod_cps[0].wait()
        for cp in first + passed + [relay_north] + c_cps[1:] + mod_cps[1:]:
            cp.wait_send()
        mine.wait()

    vm = pl.BlockSpec(memory_space=pltpu.VMEM)
    seven = pltpu.SemaphoreType.DMA((N_DEV - 1,))
    return pl.pallas_call(
        body, name="prologue",
        out_shape=[jax.ShapeDtypeStruct((N_DEV,) + w_in_t.shape, BF16), jax.ShapeDtypeStruct((n_ex + 8, D), F32),
                   jax.ShapeDtypeStruct((N_DEV, n_ex + 8, n_mod), F32)],
        in_specs=[vm] * 5, out_specs=[vm] * 3,
        scratch_shapes=[pltpu.VMEM(w_in_t.shape, BF16), pltpu.VMEM((8, D), F32), pltpu.VMEM((N_DEV, 8, D), F32),
                        pltpu.VMEM((n_ex + 8, n_mod), F32),
                        seven, seven, seven, seven, seven, seven, pltpu.SemaphoreType.DMA((3,))],
        compiler_params=pltpu.CompilerParams(vmem_limit_bytes=VMEM_LIMIT),
    )(w_in_t, c, c_ctx_row, w_mod_loc, b_mod)


def _exchange_copies(in_refs, out_refs, send_sems, recv_sems, local_sems, scatter):
    x, y, c = _coords()
    me = _lin(x, y, c)
    local, remote = [], []
    for a, (src, dst) in enumerate(zip(in_refs, out_refs)):
        local.append(pltpu.make_async_copy(src.at[me] if scatter else src, dst.at[me], local_sems.at[a]))
        for k in range(1, N_DEV):
            peer = (1 - x if k & 4 else x, 1 - y if k & 2 else y, 1 - c if k & 1 else c)
            remote.append(pltpu.make_async_remote_copy(
                src_ref=src.at[_lin(*peer)] if scatter else src, dst_ref=dst.at[me],
                send_sem=send_sems.at[a * (N_DEV - 1) + k - 1], recv_sem=recv_sems.at[a * (N_DEV - 1) + k - 1],
                device_id=peer, device_id_type=MESH_ID))
    return local, remote


def _exchange_scratch(n):
    return [pltpu.SemaphoreType.DMA((n * (N_DEV - 1),)), pltpu.SemaphoreType.DMA((n * (N_DEV - 1),)),
            pltpu.SemaphoreType.DMA((n,))]


def _sum_devices_adamw(items):
    n = len(items)

    def body(*refs):
        for a in range(n):
            got, w_ref, m_ref, v_ref = refs[4 * a:4 * a + 4]
            g_ref, d_ref, nm_ref, nv_ref = refs[4 * n + 4 * a:4 * n + 4 * a + 4]
            g = got[0].astype(F32)
            for j in range(1, N_DEV):
                g = g + got[j].astype(F32)
            g = g[0:w_ref.shape[0], :]
            delta, m2, v2 = _adamw(w_ref[...], g, m_ref[...], v_ref[...])
            g_ref[...] = g
            d_ref[...] = delta
            nm_ref[...] = m2
            nv_ref[...] = v2

    args, out_shape = [], []
    for got, w, m, v in items:
        assert got.shape[0] == N_DEV and got.shape[1] >= w.shape[0] and got.shape[2:] == w.shape[1:]
        args += [got, w, m, v]
        out_shape += [jax.ShapeDtypeStruct(w.shape, F32)] * 4
    outs = pl.pallas_call(body, name="sum_devices_adamw", out_shape=out_shape,
                          compiler_params=pltpu.CompilerParams(vmem_limit_bytes=VMEM_LIMIT))(*args)
    return [tuple(outs[4 * a:4 * a + 4]) for a in range(n)]


def _fwd_in(x, modrows, norm_w, w_in_b, cos, sins, qnw_t, knw_t, ones_bd, k_all, v_all, shards, wire_dtypes):
    bl, s, _ = x.shape
    tm = TOKEN_PARTS * TM
    nt = s // tm
    n_sh = len(shards)

    def body(*refs):
        (x_ref, mod_ref, nw_ref, win_ref, cos_ref, sin_ref, qnw_ref, knw_ref, bd_ref, kin_ref, vin_ref) = refs[:11]
        shard_refs = refs[11:11 + n_sh]
        q_ref, k_ref, v_ref, pq_ref, pkv_ref, za_ref, glu_ref, zc_ref = refs[11 + n_sh:19 + n_sh]
        gathered_refs = refs[19 + n_sh:19 + 2 * n_sh]
        stage_refs = refs[19 + 2 * n_sh:19 + 3 * n_sh]
        send_sems, recv_sems, local_sems = refs[19 + 3 * n_sh:]
        b, i = pl.program_id(0), pl.program_id(1)
        local, remote = _exchange_copies(stage_refs, gathered_refs, send_sems, recv_sems, local_sems, scatter=False)

        @pl.when((b == 0) & (i == 0))
        def _():
            for src, stage in zip(shard_refs, stage_refs):
                stage[...] = src[...].astype(stage.dtype)
            for cp in local + remote:
                cp.start()

        shift = mod_ref[0, 0:1, :]
        scale = mod_ref[0, 1:2, :]
        for part in range(TOKEN_PARTS):
            rows = pl.ds(part * TM, TM)
            xv = x_ref[0, rows, :]
            r = lax.rsqrt(jnp.mean(xv * xv, axis=-1, keepdims=True) + EPS)
            u = (xv * r * nw_ref[...]) * (1.0 + scale) + shift
            p = _dot_nt(u.astype(BF16), win_ref[...])
            pq = p[:, 0:DA]
            pk = p[:, DA:DA + HD * 2]
            ck = cos_ref[rows, :]
            sk = sin_ref[rows, :]
            cs = jnp.concatenate([ck] * (DA // KVW), axis=-1)
            sn = jnp.concatenate([sk] * (DA // KVW), axis=-1)
            rq = lax.rsqrt(_segsum(pq * pq, bd_ref[...]) * (1.0 / HD) + EPS)
            qn = pq * rq * qnw_ref[...]
            qr = _rope(qn, cs, sn) * 0.125
            for h in range(DA // HD):
                q_ref[0, h, rows, :] = qr[:, h * HD:(h + 1) * HD].astype(BF16)
            rk = lax.rsqrt(_segsum(pk * pk, bd_ref[0:KVW, 0:KVW]) * (1.0 / HD) + EPS)
            kn = pk * rk * knw_ref[...]
            kr = _rope(kn, ck, sk)
            pv = p[:, 640:768]
            for h in range(KVW // HD):
                k_ref[0, h, rows, :] = kr[:, h * HD:(h + 1) * HD].astype(BF16)
                v_ref[0, h, rows, :] = _with_ones_column(pv[:, h * HD:(h + 1) * HD]).astype(BF16)
            pq_ref[0, rows, :] = pq
            pkv_ref[0, rows, :] = p[:, 512:768]
            za_ref[0, rows, :] = p[:, 768:1280]
            glu_ref[0, rows, :] = p[:, 1280:2304]
            zc_ref[0, rows, :] = p[:, 2304:2816]

        @pl.when((b == bl - 1) & (i == nt - 1))
        def _():
            for cp in remote:
                cp.wait_recv()
            for cp in remote:
                cp.wait_send()
            for cp in local:
                cp.wait()

    def tile(w):
        return pl.BlockSpec((1, tm, w), lambda b, i: (b, i, 0))

    def const(shape):
        return pl.BlockSpec(shape, lambda b, i: (0,) * len(shape))

    outs = [(DA, F32), (2 * KVW, F32), (DA, F32), (2 * DC, F32), (DC, F32)]
    anyspace = pl.BlockSpec(memory_space=pl.ANY)
    rope = pl.BlockSpec((tm, KVW), lambda b, i: (i, 0))
    k_tile = pl.BlockSpec((1, KVW // HD, tm, HD), lambda b, i: (b, 0, i, 0))
    v_tile = pl.BlockSpec((1, KVW // HD, tm, 2 * HD), lambda b, i: (b, 0, i, 0))
    res = pl.pallas_call(
        body, name="fwd_in", grid=(bl, nt),
        in_specs=[tile(D), pl.BlockSpec((1, 3, D), lambda b, i: (b, 0, 0)), const((1, D)), const((D_IN, D)),
                  rope, rope, const((1, DA)), const((1, KVW)), const((DA, DA)), anyspace, anyspace]
        + [const(a.shape) for a in shards],
        out_specs=[pl.BlockSpec((1, DA // HD, tm, HD), lambda b, i: (b, 0, i, 0)), k_tile, v_tile]
        + [tile(w) for w, _ in outs] + [anyspace] * n_sh,
        out_shape=[jax.ShapeDtypeStruct((bl, DA // HD, s, HD), BF16), jax.ShapeDtypeStruct(k_all.shape, BF16),
                   jax.ShapeDtypeStruct(v_all.shape, BF16)]
        + [jax.ShapeDtypeStruct((bl, s, w), dt) for w, dt in outs]
        + [jax.ShapeDtypeStruct((N_DEV,) + a.shape, dt) for a, dt in zip(shards, wire_dtypes)],
        input_output_aliases={9: 1, 10: 2},
        scratch_shapes=[pltpu.VMEM(a.shape, dt) for a, dt in zip(shards, wire_dtypes)] + _exchange_scratch(n_sh),
        compiler_params=_params(("arbitrary", "arbitrary")),
    )(x, modrows, norm_w, w_in_b, cos, sins, qnw_t, knw_t, ones_bd, k_all, v_all, *shards)
    return res[:8], res[8:]


_KV_ROWS_OF_W_IN_T = pl.BlockSpec((2 * KVW, D), lambda b: (DA // (2 * KVW), 0))


def _ctx_fwd(ctx, modc, norm_w, w_kv_b, knw_t, ones_bd, n_keys):
    bl, cl, _ = ctx.shape

    def body(x_ref, mod_ref, nw_ref, w_ref, knw_ref, bd_ref, k_ref, v_ref, pkv_ref):
        xv = x_ref[0]
        shift = mod_ref[0, 0:1, :]
        scale = mod_ref[0, 1:2, :]
        r = lax.rsqrt(jnp.mean(xv * xv, axis=-1, keepdims=True) + EPS)
        u = (xv * r * nw_ref[...]) * (1.0 + scale) + shift
        p = _dot_nt(u.astype(BF16), w_ref[...])
        pk = p[:, 0:KVW]
        rk = lax.rsqrt(_segsum(pk * pk, bd_ref[...]) * (1.0 / HD) + EPS)
        kn = pk * rk * knw_ref[...]
        pv = p[:, KVW:2 * KVW]
        for h in range(KVW // HD):
            k_ref[0, h] = kn[:, h * HD:(h + 1) * HD].astype(BF16)
            v_ref[0, h] = _with_ones_column(pv[:, h * HD:(h + 1) * HD]).astype(BF16)
        pkv_ref[0] = p

    def const(shape):
        return pl.BlockSpec(shape, lambda b: (0,) * len(shape))

    def tile(w):
        return pl.BlockSpec((1, cl, w), lambda b: (b, 0, 0))

    ctx_block = (n_keys - cl) // cl
    assert ctx_block * cl + cl == n_keys
    k_tile = pl.BlockSpec((1, KVW // HD, cl, HD), lambda b: (b, 0, ctx_block, 0))
    v_tile = pl.BlockSpec((1, KVW // HD, cl, 2 * HD), lambda b: (b, 0, ctx_block, 0))
    return pl.pallas_call(
        body, name="ctx_fwd", grid=(bl,),
        in_specs=[tile(D), const((1, 3, D)), const((1, D)), _KV_ROWS_OF_W_IN_T, const((1, KVW)), const((KVW, KVW))],
        out_specs=[k_tile, v_tile, tile(2 * KVW)],
        out_shape=[jax.ShapeDtypeStruct((bl, KVW // HD, n_keys, HD), BF16),
                   jax.ShapeDtypeStruct((bl, KVW // HD, n_keys, 2 * HD), BF16),
                   jax.ShapeDtypeStruct((bl, cl, 2 * KVW), F32)],
        compiler_params=_params(("arbitrary",)),
    )(ctx, modc, norm_w, w_kv_b, knw_t, ones_bd)


def _attn_fwd(q, k, v1):
    bl, _, s, _ = q.shape
    n_keys = k.shape[2]

    def body(q_ref, k_ref, v_ref, o_ref, lse_ref):
        kv = k_ref[0, 0]
        vv = v_ref[0, 0]
        lane = lax.broadcasted_iota(jnp.int32, (TQ, 2 * HD), 1)
        for part in range(FWD_PARTS):
            rows = pl.ds(part * TQ, TQ)
            lse = jnp.zeros((TQ, 2 * HD), F32)
            heads = []
            sc_all = _dot_nt(q_ref[0, :, rows, :].reshape(4 * TQ, HD), kv)
            for h in range(4):
                sc = sc_all[h * TQ:(h + 1) * TQ, :]
                m = jnp.max(sc, axis=-1, keepdims=True)
                e = jnp.exp(sc - m).astype(BF16)
                ov = _dot(e, vv)
                denom = ov[:, HD:HD + 1]
                heads.append(ov[:, 0:HD] * (1.0 / denom))
                lse = jnp.where(lane == h, m + jnp.log(denom), lse)
            o_ref[0, rows, :] = jnp.concatenate(heads, axis=-1)
            lse_ref[0, 0, rows, :] = lse

    tq = FWD_PARTS * TQ
    ks = pl.BlockSpec((1, 1, n_keys, HD), lambda b, g, i: (b, g, 0, 0))
    qs = pl.BlockSpec((1, 4, tq, HD), lambda b, g, i: (b, g, i, 0))
    vs = pl.BlockSpec((1, 1, n_keys, 2 * HD), lambda b, g, i: (b, g, 0, 0))
    return pl.pallas_call(
        body, name="attn_fwd", grid=(bl, 2, s // tq), in_specs=[qs, ks, vs],
        out_specs=[pl.BlockSpec((1, tq, 4 * HD), lambda b, g, i: (b, i, g)),
                   pl.BlockSpec((1, 1, tq, 2 * HD), lambda b, g, i: (b, g, i, 0))],
        out_shape=[jax.ShapeDtypeStruct((bl, s, DA), F32), jax.ShapeDtypeStruct((bl, 2, s, 2 * HD), F32)],
        compiler_params=_params(("arbitrary", "arbitrary", "arbitrary")),
    )(q, k, v1)


def _attn_bwd(q, k, v1, do, o, lse, exchange):
    bl, _, s, _ = q.shape
    n_keys = k.shape[2]
    tq = BWD_PARTS * TQ
    nq = s // tq
    n_ex = len(exchange)

    def body(*refs):
        q_ref, k_ref, v_ref, do_ref, o_ref, lse_ref = refs[:6]
        part_refs = refs[6:6 + n_ex]
        dq_ref, dk_ref, dv_ref = refs[6 + n_ex:9 + n_ex]
        got_refs = refs[9 + n_ex:9 + 2 * n_ex]
        p_sc, ds_sc, dkt, dvt, send_sems, recv_sems, local_sems = refs[9 + 2 * n_ex:]
        i = pl.program_id(2)
        first = (pl.program_id(0) == 0) & (pl.program_id(1) == 0) & (i == 0)
        last = (pl.program_id(0) == bl - 1) & (pl.program_id(1) == 1) & (i == nq - 1)
        local, remote = _exchange_copies(part_refs, got_refs, send_sems, recv_sems, local_sems, scatter=True)

        @pl.when(first)
        def _():
            for cp in local + remote:
                cp.start()

        @pl.when(i == 0)
        def _():
            dkt[...] = jnp.zeros_like(dkt)
            dvt[...] = jnp.zeros_like(dvt)

        kv = k_ref[0, 0]
        vv = v_ref[0, 0][:, 0:HD]
        for part in range(BWD_PARTS):
            tq_rows = pl.ds(part * TQ, TQ)
            lse = lse_ref[0, 0, tq_rows, :]
            ov = o_ref[0, tq_rows, :]
            dqs = []
            q_cat = q_ref[0, :, tq_rows, :].reshape(4 * TQ, HD)
            do_cat = do_ref[0, :, tq_rows, :].reshape(4 * TQ, HD)
            sc_all = _dot_nt(q_cat, kv)
            for h in range(4):
                doh = do_cat[h * TQ:(h + 1) * TQ, :]
                delta = jnp.sum(ov[:, h * HD:(h + 1) * HD] * doh.astype(F32), axis=-1, keepdims=True)
                rows = pl.ds(((part % BWD_SLOTS) * 4 + h) * TQ, TQ)
                p = jnp.exp(sc_all[h * TQ:(h + 1) * TQ, :] - lse[:, h:h + 1])
                ds = (p * (_dot_nt(doh, vv) - delta)).astype(BF16)
                p_sc[rows, :] = p.astype(BF16)
                ds_sc[rows, :] = ds
                dqs.append(_dot(ds, kv) * 0.125)
            dq_ref[0, tq_rows, :] = jnp.concatenate(dqs, axis=-1)
            part_rows = pl.ds((part % BWD_SLOTS) * 4 * TQ, 4 * TQ)
            dvt[...] += _dot_tn(do_cat, p_sc[part_rows, :])
            dkt[...] += _dot_tn(q_cat, ds_sc[part_rows, :])

        @pl.when(i == nq - 1)
        def _():
            dk_ref[0, 0] = dkt[...].T
            dv_ref[0, 0] = dvt[...].T

        @pl.when(last)
        def _():
            for cp in remote:
                cp.wait_recv()
            for cp in remote:
                cp.wait_send()
            for cp in local:
                cp.wait()

    qs = pl.BlockSpec((1, 4, tq, HD), lambda b, g, i: (b, g, i, 0))
    ks = pl.BlockSpec((1, 1, n_keys, HD), lambda b, g, i: (b, g, 0, 0))
    vs = pl.BlockSpec((1, 1, n_keys, 2 * HD), lambda b, g, i: (b, g, 0, 0))
    os_ = pl.BlockSpec((1, tq, 4 * HD), lambda b, g, i: (b, i, g))
    kshape = jax.ShapeDtypeStruct(k.shape, F32)
    anyspace = pl.BlockSpec(memory_space=pl.ANY)
    res = pl.pallas_call(
        body, name="attn_bwd", grid=(bl, 2, nq),
        in_specs=[qs, ks, vs, qs, os_, pl.BlockSpec((1, 1, tq, 2 * HD), lambda b, g, i: (b, g, i, 0))]
        + [anyspace] * n_ex,
        out_specs=[os_, ks, ks] + [anyspace] * n_ex,
        out_shape=[jax.ShapeDtypeStruct((bl, s, DA), F32), kshape, kshape]
        + [jax.ShapeDtypeStruct(a.shape, a.dtype) for a in exchange],
        scratch_shapes=[pltpu.VMEM((BWD_SLOTS * 4 * TQ, n_keys), BF16), pltpu.VMEM((BWD_SLOTS * 4 * TQ, n_keys), BF16),
                        pltpu.VMEM((HD, n_keys), F32), pltpu.VMEM((HD, n_keys), F32)] + _exchange_scratch(n_ex),
        compiler_params=_params(("arbitrary", "arbitrary", "arbitrary")),
    )(q, k, v1, do, o, lse, *exchange)
    return res[:3], res[3:]


def _halo_specs(width, s):
    per = TC // HALO
    last = s // HALO - 1
    main = pl.BlockSpec((1, TC, width), lambda b, i: (b, i, 0))
    prev = pl.BlockSpec((1, HALO, width), lambda b, i: (b, jnp.maximum(i * per - 1, 0), 0))
    nxt = pl.BlockSpec((1, HALO, width), lambda b, i: (b, jnp.minimum((i + 1) * per, last), 0))
    return main, prev, nxt


def _glu(g):
    return g[:, 0:DC] * _sigmoid(g[:, DC:2 * DC])


def _fill_padded(pad_ref, main, prev, nxt, first, last):
    tc = main.shape[0]
    pad_ref[0:HALO, :] = jnp.where(first, 0.0, prev)
    pad_ref[HALO:HALO + tc, :] = main
    pad_ref[HALO + tc:2 * HALO + tc, :] = jnp.where(last, 0.0, nxt)


PLANE_ROWS = TC + 2 * HALO - 8


def _shift_planes(pad_ref, planes_ref):
    for r in range(1, 8):
        planes_ref[r - 1] = pad_ref[pl.ds(r, planes_ref.shape[1]), :]


def _tap_rows(pad_ref, planes_ref, offset, start, n):
    a, r = divmod(offset, 8)
    if r == 0:
        return pad_ref[pl.ds(start + 8 * a, n), :]
    return planes_ref[r - 1, pl.ds(start + 8 * a, n), :]


def _conv_fwd(glu, conv_w, conv_b, ln_w, ln_b, w_pw_b, b_pw):
    bl, s, _ = glu.shape
    nt = s // TC

    def body(g_ref, gp_ref, gn_ref, cw_ref, cb_ref, lw_ref, lb_ref, wpw_ref, bpw_ref, y_ref, cp_ref, pad_ref, planes_ref):
        i = pl.program_id(1)
        _fill_padded(pad_ref, _glu(g_ref[0]), _glu(gp_ref[0]), _glu(gn_ref[0]), i == 0, i == nt - 1)
        _shift_planes(pad_ref, planes_ref)
        for ck in range(TC // CH):
            acc = jnp.zeros((CH, DC), F32) + cb_ref[...]
            for t in range(KW):
                acc = acc + _tap_rows(pad_ref, planes_ref, 1 + t, ck * CH, CH) * cw_ref[t:t + 1, :]
            y_ref[0, pl.ds(ck * CH, CH), :] = acc
        y = y_ref[0]
        mu = jnp.mean(y, axis=-1, keepdims=True)
        yc = y - mu
        var = jnp.mean(yc * yc, axis=-1, keepdims=True)
        z = yc * lax.rsqrt(var + EPS) * lw_ref[...] + lb_ref[...]
        act = z * _sigmoid(z)
        cp_ref[0] = _dot(act.astype(BF16), wpw_ref[...]) + bpw_ref[...]

    def const(shape):
        return pl.BlockSpec(shape, lambda b, i: (0,) * len(shape))

    main, prev, nxt = _halo_specs(2 * DC, s)
    tile = pl.BlockSpec((1, TC, DC), lambda b, i: (b, i, 0))
    return pl.pallas_call(
        body, name="conv_fwd", grid=(bl, nt),
        in_specs=[main, prev, nxt, const((32, DC)), const((1, DC)), const((1, DC)), const((1, DC)),
                  const((DC, DC)), const((1, DC))],
        out_specs=[tile, tile],
        out_shape=[jax.ShapeDtypeStruct((bl, s, DC), F32)] * 2,
        scratch_shapes=[pltpu.VMEM((TC + 2 * HALO, DC), F32), pltpu.VMEM((7, PLANE_ROWS, DC), F32)],
        compiler_params=_params(("arbitrary", "arbitrary")),
    )(glu, glu, glu, conv_w, conv_b, ln_w, ln_b, w_pw_b, b_pw)


def _conv_bwd_depthwise(glu, dy, conv_w):
    bl, s, _ = glu.shape
    nt = s // TC

    def body(g_ref, gp_ref, gn_ref, d_ref, dp_ref, dn_ref, cw_ref, dglu_ref, dcw_ref,
             padu_ref, padd_ref, planes_u, planes_d):
        i = pl.program_id(1)

        @pl.when((pl.program_id(0) == 0) & (i == 0))
        def _():
            dcw_ref[...] = jnp.zeros_like(dcw_ref)

        first, last = i == 0, i == nt - 1
        _fill_padded(padu_ref, _glu(g_ref[0]), _glu(gp_ref[0]), _glu(gn_ref[0]), first, last)
        _fill_padded(padd_ref, d_ref[0], dp_ref[0], dn_ref[0], first, last)
        _shift_planes(padu_ref, planes_u)
        _shift_planes(padd_ref, planes_d)
        for ck in range(TC // CH):
            acc = jnp.zeros((CH, DC), F32)
            for t in range(KW):
                acc = acc + _tap_rows(padd_ref, planes_d, 2 * HALO - 1 - t, ck * CH, CH) * cw_ref[t:t + 1, :]
            g = g_ref[0, pl.ds(ck * CH, CH), :]
            a = g[:, 0:DC]
            sg = _sigmoid(g[:, DC:2 * DC])
            dglu_ref[0, pl.ds(ck * CH, CH), 0:DC] = (acc * sg).astype(BF16)
            dglu_ref[0, pl.ds(ck * CH, CH), DC:2 * DC] = (acc * a * sg * (1.0 - sg)).astype(BF16)
        group = 4
        for t0 in range(0, KW, group):
            taps = range(t0, min(t0 + group, KW))
            acc8 = [jnp.zeros((8, DC), F32) for _ in taps]
            for ck in range(TC // CH):
                dchunk = d_ref[0, pl.ds(ck * CH, CH), :]
                for n, t in enumerate(taps):
                    prod = _tap_rows(padu_ref, planes_u, 1 + t, ck * CH, CH) * dchunk
                    acc8[n] = acc8[n] + jnp.sum(prod.reshape(CH // 8, 8, DC), axis=0)
            for n, t in enumerate(taps):
                dcw_ref[t:t + 1, :] += jnp.sum(acc8[n], axis=0, keepdims=True)

    gmain, gprev, gnext = _halo_specs(2 * DC, s)
    dmain, dprev, dnext = _halo_specs(DC, s)
    cw = pl.BlockSpec((32, DC), lambda b, i: (0, 0))
    return pl.pallas_call(
        body, name="conv_bwd_depthwise", grid=(bl, nt),
        in_specs=[gmain, gprev, gnext, dmain, dprev, dnext, cw],
        out_specs=[gmain, cw],
        out_shape=[jax.ShapeDtypeStruct((bl, s, 2 * DC), BF16), jax.ShapeDtypeStruct((32, DC), F32)],
        scratch_shapes=[pltpu.VMEM((TC + 2 * HALO, DC), F32)] * 2 + [pltpu.VMEM((7, PLANE_ROWS, DC), F32)] * 2,
        compiler_params=_params(("arbitrary", "arbitrary")),
    )(glu, glu, glu, dy, dy, dy, conv_w)


def _out_fwd_bwd(attn, za, cp, zc, x, target, modrows, w_out_b, y_conv, ln_w, ln_b, w_pw_b):
    bl, s, _ = x.shape
    tm = OUT_TM

    def body(o_ref, za_ref, cp_ref, zc_ref, x_ref, t_ref, mod_ref, w_ref, y_ref, lw_ref, lb_ref, wpw_ref,
             do_ref, dza_ref, dy_ref, dzc_ref, dh_ref, dgate_ref, gwb_ref, loss_ref, gpwb_ref, rows_ref,
             gw_ref, gpw_ref):
        b, i = pl.program_id(0), pl.program_id(1)

        @pl.when((b == 0) & (i == 0))
        def _():
            gw_ref[...] = jnp.zeros_like(gw_ref)
            gpw_ref[...] = jnp.zeros_like(gpw_ref)
            rows_ref[...] = jnp.zeros_like(rows_ref)
            loss_ref[...] = jnp.zeros_like(loss_ref)

        @pl.when(i == 0)
        def _():
            dgate_ref[...] = jnp.zeros_like(dgate_ref)

        gate = mod_ref[0, 2:3, :]
        w = w_ref[...]
        o, za_v, cp_v, zc_v = o_ref[0], za_ref[0], cp_ref[0], zc_ref[0]
        sa = _sigmoid(za_v)
        sc = _sigmoid(zc_v)
        silu_a = za_v * sa
        silu_c = zc_v * sc
        mix = jnp.concatenate([(o * silu_a).astype(BF16), (cp_v * silu_c).astype(BF16)], axis=-1)
        out = _dot(mix, w)
        err = x_ref[0] + gate * out - t_ref[0]
        loss_ref[...] += jnp.sum(err * err, axis=0, keepdims=True)
        dh = err * (1.0 / D)
        dh_ref[0] = dh
        dgate_ref[0] += jnp.sum(dh * out, axis=0, keepdims=True)
        dout = (dh * gate).astype(BF16)
        gw_ref[...] += _dot_tn(mix, dout)
        dmix = _dot_nt(dout, w)
        dga = dmix[:, 0:DA]
        dgc = dmix[:, DA:DA + DC]
        dov = dga * silu_a
        for h in range(DA // HD):
            do_ref[0, h] = dov[:, h * HD:(h + 1) * HD].astype(BF16)
        dza_ref[0] = (dga * o * (sa * (1.0 + za_v * (1.0 - sa)))).astype(BF16)
        dzc_ref[0] = (dgc * cp_v * (sc * (1.0 + zc_v * (1.0 - sc)))).astype(BF16)
        dcp = dgc * silu_c
        y = y_ref[0]
        yc = y - jnp.mean(y, axis=-1, keepdims=True)
        rstd = lax.rsqrt(jnp.mean(yc * yc, axis=-1, keepdims=True) + EPS)
        yn = yc * rstd
        lw = lw_ref[...]
        z = yn * lw + lb_ref[...]
        sg = _sigmoid(z)
        dcp_b = dcp.astype(BF16)
        gpw_ref[...] += _dot_tn((z * sg).astype(BF16), dcp_b)
        dz = _dot_nt(dcp_b, wpw_ref[...]) * (sg * (1.0 + z * (1.0 - sg)))
        dyn = dz * lw
        dy = rstd * (dyn - jnp.mean(dyn, axis=-1, keepdims=True) - yn * jnp.mean(dyn * yn, axis=-1, keepdims=True))
        dy_ref[0] = dy
        rows_ref[0:1, :] += jnp.sum(dcp, axis=0, keepdims=True)
        rows_ref[1:2, :] += jnp.sum(dz * yn, axis=0, keepdims=True)
        rows_ref[2:3, :] += jnp.sum(dz, axis=0, keepdims=True)
        rows_ref[3:4, :] += jnp.sum(dy, axis=0, keepdims=True)

        @pl.when((b == bl - 1) & (i == s // tm - 1))
        def _():
            gwb_ref[...] = gw_ref[...].astype(BF16)
            gpwb_ref[...] = gpw_ref[...].astype(BF16)

    def const(shape):
        return pl.BlockSpec(shape, lambda b, i: (0,) * len(shape))

    def tile(w):
        return pl.BlockSpec((1, tm, w), lambda b, i: (b, i, 0))

    return pl.pallas_call(
        body, name="out_fwd_bwd", grid=(bl, s // tm),
        in_specs=[tile(DA), tile(DA), tile(DC), tile(DC), tile(D), tile(D),
                  pl.BlockSpec((1, 3, D), lambda b, i: (b, 0, 0)), const((D, D)),
                  tile(DC), const((1, DC)), const((1, DC)), const((DC, DC))],
        out_specs=[pl.BlockSpec((1, DA // HD, tm, HD), lambda b, i: (b, 0, i, 0)), tile(DA), tile(DC), tile(DC), tile(D),
                   pl.BlockSpec((1, 1, D), lambda b, i: (b, 0, 0)), const((D, D)), const((1, D)),
                   const((DC, DC)), const((8, DC))],
        out_shape=[jax.ShapeDtypeStruct((bl, DA // HD, s, HD), BF16), jax.ShapeDtypeStruct((bl, s, DA), BF16),
                   jax.ShapeDtypeStruct((bl, s, DC), F32), jax.ShapeDtypeStruct((bl, s, DC), BF16),
                   jax.ShapeDtypeStruct((bl, s, D), F32), jax.ShapeDtypeStruct((bl, 1, D), F32),
                   jax.ShapeDtypeStruct((D, D), BF16), jax.ShapeDtypeStruct((1, D), F32),
                   jax.ShapeDtypeStruct((DC, DC), BF16), jax.ShapeDtypeStruct((8, DC), F32)],
        scratch_shapes=[pltpu.VMEM((D, D), F32), pltpu.VMEM((DC, DC), F32)],
        compiler_params=_params(("arbitrary", "arbitrary")),
    )(attn, za, cp, zc, x, target, modrows, w_out_b, y_conv, ln_w, ln_b, w_pw_b)


def _rms_heads_bwd(dy, x, w_t, ones_bd):
    r = lax.rsqrt(_segsum(x * x, ones_bd) * (1.0 / HD) + EPS)
    xh = x * r
    g = dy * w_t
    dx = r * (g - xh * (_segsum(g * xh, ones_bd) * (1.0 / HD)))
    return dx, dy * xh


def _ctx_bwd(ctx, modc, norm_w, w_kv_b, pkv_c, dk_c, dv_c, knw_t, ones_bd):
    bl, cl, _ = ctx.shape

    def body(x_ref, mod_ref, nw_ref, w_ref, p_ref, dk_ref, dv_ref, knw_ref, bd_ref, gw_ref, rows_ref, dknw_ref):
        @pl.when(pl.program_id(0) == 0)
        def _():
            gw_ref[...] = jnp.zeros_like(gw_ref)
            rows_ref[...] = jnp.zeros_like(rows_ref)
            dknw_ref[...] = jnp.zeros_like(dknw_ref)

        xv = x_ref[0]
        shift = mod_ref[0, 0:1, :]
        scale = mod_ref[0, 1:2, :]
        nw = nw_ref[...]
        r = lax.rsqrt(jnp.mean(xv * xv, axis=-1, keepdims=True) + EPS)
        xn = xv * r
        yv = xn * nw
        u = yv * (1.0 + scale) + shift
        dkv = jnp.concatenate([dk_ref[0, 0], dk_ref[0, 1]], axis=-1)
        dpk, dknw = _rms_heads_bwd(dkv, p_ref[0][:, 0:KVW], knw_ref[...], bd_ref[...])
        dp = jnp.concatenate([dpk.astype(BF16), dv_ref[0, 0].astype(BF16), dv_ref[0, 1].astype(BF16)], axis=-1)
        gw_ref[...] += _dot_tn(dp, u.astype(BF16))
        du = _dot(dp, w_ref[...])
        rows_ref[0:1, :] += jnp.sum(du, axis=0, keepdims=True)
        rows_ref[1:2, :] += jnp.sum(du * yv, axis=0, keepdims=True)
        rows_ref[2:3, :] += jnp.sum(du * (1.0 + scale) * xn, axis=0, keepdims=True)
        dknw_ref[...] += jnp.sum(dknw, axis=0, keepdims=True)

    def const(shape):
        return pl.BlockSpec(shape, lambda b: (0,) * len(shape))

    def tile(w):
        return pl.BlockSpec((1, cl, w), lambda b: (b, 0, 0))

    ctx_block = (dk_c.shape[2] - cl) // cl
    kv_tile = pl.BlockSpec((1, KVW // HD, cl, HD), lambda b: (b, 0, ctx_block, 0))
    return pl.pallas_call(
        body, name="ctx_bwd", grid=(bl,),
        in_specs=[tile(D), const((1, 3, D)), const((1, D)), _KV_ROWS_OF_W_IN_T, tile(2 * KVW), kv_tile, kv_tile,
                  const((1, KVW)), const((KVW, KVW))],
        out_specs=[const((2 * KVW, D)), const((8, D)), const((1, KVW))],
        out_shape=[jax.ShapeDtypeStruct((2 * KVW, D), F32), jax.ShapeDtypeStruct((8, D), F32),
                   jax.ShapeDtypeStruct((1, KVW), F32)],
        compiler_params=_params(("arbitrary",)),
    )(ctx, modc, norm_w, w_kv_b, pkv_c, dk_c, dv_c, knw_t, ones_bd)


def _bwd_in(x, modrows, norm_w, w_in_b, cos, sins, qnw_t, knw_t, ones_bd,
            pq, pkv, dq, dk, dv, dza, dglu, dzc, dh, gw_kv):
    bl, s, _ = x.shape
    tm = TOKEN_PARTS * TM
    nt = s // tm

    def body(x_ref, mod_ref, nw_ref, win_hbm, cos_ref, sin_ref, qnw_ref, knw_ref, bd_ref,
             pq_ref, pkv_ref, dq_ref, dk_ref, dv_ref, dza_ref, dglu_ref, dzc_ref, dh_ref, gwkv_ref,
             gx_ref, gw_hbm, dmod_ref, dnw_ref, dqnw_ref, dknw_ref, win_ref, gw_acc, sem):
        b, i = pl.program_id(0), pl.program_id(1)

        @pl.when((b == 0) & (i == 0))
        def _():
            cp = pltpu.make_async_copy(win_hbm, win_ref, sem)
            cp.start()
            gw_acc[...] = jnp.zeros_like(gw_acc)
            dnw_ref[...] = jnp.zeros_like(dnw_ref)
            dqnw_ref[...] = jnp.zeros_like(dqnw_ref)
            dknw_ref[...] = jnp.zeros_like(dknw_ref)
            cp.wait()

        @pl.when(i == 0)
        def _():
            dmod_ref[...] = jnp.zeros_like(dmod_ref)

        bd = bd_ref[...]
        shift = mod_ref[0, 0:1, :]
        scale = mod_ref[0, 1:2, :]
        nw = nw_ref[...]
        dps, us = [], []
        for part in range(TOKEN_PARTS):
            rows = pl.ds(part * TM, TM)
            ck = cos_ref[rows, :]
            sk = sin_ref[rows, :]
            cs = jnp.concatenate([ck] * (DA // KVW), axis=-1)
            sn = jnp.concatenate([sk] * (DA // KVW), axis=-1)
            dqn = _rope_bwd(dq_ref[0, rows, :], cs, sn)
            dpq, dqnw = _rms_heads_bwd(dqn, pq_ref[0, rows, :], qnw_ref[...], bd)
            dkn = _rope_bwd(jnp.concatenate([dk_ref[0, 0, rows, :], dk_ref[0, 1, rows, :]], axis=-1), ck, sk)
            dpk, dknw = _rms_heads_bwd(dkn, pkv_ref[0, rows, 0:KVW], knw_ref[...], bd[0:KVW, 0:KVW])
            dqnw_ref[...] += jnp.sum(dqnw, axis=0, keepdims=True)
            dknw_ref[...] += jnp.sum(dknw, axis=0, keepdims=True)
            dp = jnp.concatenate(
                [dpq.astype(BF16), dpk.astype(BF16), dv_ref[0, 0, rows, :].astype(BF16), dv_ref[0, 1, rows, :].astype(BF16),
                 dza_ref[0, rows, :], dglu_ref[0, rows, :], dzc_ref[0, rows, :]], axis=-1)

            xv = x_ref[0, rows, :]
            r = lax.rsqrt(jnp.mean(xv * xv, axis=-1, keepdims=True) + EPS)
            xn = xv * r
            yv = xn * nw
            u = yv * (1.0 + scale) + shift
            dps.append(dp)
            us.append(u.astype(BF16))
            du = _dot(dp, win_ref[...])
            dmod_ref[0, 0:1, :] += jnp.sum(du, axis=0, keepdims=True)
            dmod_ref[0, 1:2, :] += jnp.sum(du * yv, axis=0, keepdims=True)
            dy = du * (1.0 + scale)
            dnw_ref[...] += jnp.sum(dy * xn, axis=0, keepdims=True)
            dxn = dy * nw
            gx_ref[0, rows, :] = dh_ref[0, rows, :] + r * (dxn - xn * jnp.mean(dxn * xn, axis=-1, keepdims=True))
        gw_acc[...] += _dot_tn(jnp.concatenate(dps, axis=0), jnp.concatenate(us, axis=0))

        @pl.when((b == bl - 1) & (i == nt - 1))
        def _():
            gw_acc[DA:DA + 2 * KVW, :] += gwkv_ref[...]

            def to_bf16(j, carry):
                rows = pl.ds(pl.multiple_of(j * 2 * KVW, 2 * KVW), 2 * KVW)
                win_ref[rows, :] = gw_acc[rows, :].astype(BF16)
                return carry

            lax.fori_loop(0, D_IN // (2 * KVW), to_bf16, 0)
            pltpu.sync_copy(win_ref, gw_hbm)

    def tile(w):
        return pl.BlockSpec((1, tm, w), lambda b, i: (b, i, 0))

    def const(shape):
        return pl.BlockSpec(shape, lambda b, i: (0,) * len(shape))

    anyspace = pl.BlockSpec(memory_space=pl.ANY)
    rope = pl.BlockSpec((tm, KVW), lambda b, i: (i, 0))
    kv_tile = pl.BlockSpec((1, KVW // HD, tm, HD), lambda b, i: (b, 0, i, 0))
    return pl.pallas_call(
        body, name="bwd_in", grid=(bl, nt),
        in_specs=[tile(D), pl.BlockSpec((1, 3, D), lambda b, i: (b, 0, 0)), const((1, D)), anyspace, rope, rope,
                  const((1, DA)), const((1, KVW)), const((DA, DA)),
                  tile(DA), tile(2 * KVW), tile(DA), kv_tile, kv_tile, tile(DA), tile(2 * DC), tile(DC), tile(D),
                  const((2 * KVW, D))],
        out_specs=[tile(D), anyspace, pl.BlockSpec((1, 2, D), lambda b, i: (b, 0, 0)), const((1, D)),
                   const((1, DA)), const((1, KVW))],
        out_shape=[jax.ShapeDtypeStruct((bl, s, D), F32), jax.ShapeDtypeStruct((D_IN, D), BF16),
                   jax.ShapeDtypeStruct((bl, 2, D), F32), jax.ShapeDtypeStruct((1, D), F32),
                   jax.ShapeDtypeStruct((1, DA), F32), jax.ShapeDtypeStruct((1, KVW), F32)],
        scratch_shapes=[pltpu.VMEM((D_IN, D), BF16), pltpu.VMEM((D_IN, D), F32), pltpu.SemaphoreType.DMA],
        compiler_params=_params(("arbitrary", "arbitrary")),
    )(x, modrows, norm_w, w_in_b, cos, sins, qnw_t, knw_t, ones_bd,
      pq, pkv, dq, dk, dv, dza, dglu, dzc, dh, gw_kv)


_LOSS, _DMODC, _NW, _QN, _KN, _CB, _LW, _LB, _BPW, SMALL_W = 0, 1024, 4096, 5120, 5248, 5376, 5888, 6400, 6912, 7424


ROW_W = 1792


def _put_flat(ref, off, value):
    n, done = value.shape[1], 0
    while done < n:
        r, c = divmod(off + done, ROW_W)
        take = min(n - done, ROW_W - c)
        ref[r:r + 1, c:c + take] = value[:, done:done + take]
        done += take


def _get_flat(arr, off, n):
    parts, done = [], 0
    while done < n:
        r, c = divmod(off + done, ROW_W)
        take = min(n - done, ROW_W - c)
        parts.append(arr[r:r + 1, c:c + take])
        done += take
    return parts[0] if len(parts) == 1 else jnp.concatenate(parts, axis=-1)


def _pack_small_body(loss_ref, ctx_ref, dnw_ref, dqnw_ref, dknw_ref, dknwc_ref, conv_ref, dss_ref, dgate_ref, o_ref):
    bl = dss_ref.shape[0]
    assert SMALL_W + bl * 3 * D <= 8 * ROW_W
    o_ref[...] = jnp.zeros_like(o_ref)
    _put_flat(o_ref, _LOSS, loss_ref[...])
    _put_flat(o_ref, _DMODC, ctx_ref[0:1, :])
    _put_flat(o_ref, _DMODC + D, ctx_ref[1:2, :])
    _put_flat(o_ref, _NW, dnw_ref[...] + ctx_ref[2:3, :])
    dq = dqnw_ref[...]
    qn = dq[:, 0:HD]
    for h in range(1, DA // HD):
        qn = qn + dq[:, h * HD:(h + 1) * HD]
    _put_flat(o_ref, _QN, qn)
    dk = dknw_ref[...] + dknwc_ref[...]
    _put_flat(o_ref, _KN, dk[:, 0:HD] + dk[:, HD:2 * HD])
    _put_flat(o_ref, _BPW, conv_ref[0:1, :])
    _put_flat(o_ref, _LW, conv_ref[1:2, :])
    _put_flat(o_ref, _LB, conv_ref[2:3, :])
    _put_flat(o_ref, _CB, conv_ref[3:4, :])
    for b in range(bl):
        _put_flat(o_ref, SMALL_W + b * 3 * D, dss_ref[b, 0:1, :])
        _put_flat(o_ref, SMALL_W + b * 3 * D + D, dss_ref[b, 1:2, :])
        _put_flat(o_ref, SMALL_W + b * 3 * D + 2 * D, dgate_ref[b])


_SMALL = (("b_mod", None), ("norm_w", _NW), ("q_norm_w", _QN), ("k_norm_w", _KN), ("conv_b", _CB),
          ("conv_ln_w", _LW), ("conv_ln_b", _LB), ("b_pw", _BPW), ("c_ctx", None))


def _epilogue(parts_in, pieces, c_rows, w_mod_loc):
    bl = pieces[7].shape[0]
    n_ex = N_DEV * bl
    n_mod = w_mod_loc.shape[1]
    rb = 32
    shp = parts_in.shape[1:]
    rows_in = shp[0]

    def body(*refs):
        it = iter(refs)
        take = lambda k: [next(it) for _ in range(k)]
        (parts,) = take(1)
        piece_refs = take(9)
        (c_ref, wm_ref) = take(2)
        (g_in, g_wm, sum_ref, gb_ref, gc_all, loss_ref) = take(6)
        (mine, got_sib, stage, got_chip, payload, gathered, dmod_full, gc_mine) = take(8)
        (d2d_send, d2d_recv, ici_send, ici_recv, local_sems, sg_send, sg_recv, gc_send, gc_recv, misc_sems) = take(10)

        x, y, c = _coords()
        me = _lin(x, y, c)
        sib = (x, y, 1 - c)
        home = 2 * x + y

        def rows_loop(fn):
            def step(i, carry):
                fn(pl.ds(pl.multiple_of(i * rb, rb), rb))
                return carry
            lax.fori_loop(0, rows_in // rb, step, 0)

        def direct_gather(src, dst, send_sems, recv_sems, local_sem):
            cps = [pltpu.make_async_copy(src, dst.at[me], local_sem)]
            for k in range(1, N_DEV):
                peer = (1 - x if k & 4 else x, 1 - y if k & 2 else y, 1 - c if k & 1 else c)
                cps.append(pltpu.make_async_remote_copy(
                    src_ref=src, dst_ref=dst.at[me], send_sem=send_sems.at[k - 1], recv_sem=recv_sems.at[k - 1],
                    device_id=peer, device_id_type=MESH_ID))
            for cp in cps:
                cp.start()
            return cps

        _pack_small_body(*piece_refs, payload)
        small_cps = direct_gather(payload, gathered, sg_send, sg_recv, misc_sems.at[0])

        local, d2d, ici = [], [], []
        for s in range(4):
            cp = pltpu.make_async_copy(parts.at[_lin(s // 2, s % 2, c)], mine.at[s], local_sems.at[s])
            cp.start()
            local.append(cp)
            rc = pltpu.make_async_remote_copy(
                src_ref=parts.at[_lin(s // 2, s % 2, 1 - c)], dst_ref=got_sib.at[s],
                send_sem=d2d_send.at[s], recv_sem=d2d_recv.at[s], device_id=sib, device_id_type=MESH_ID)
            rc.start()
            d2d.append(rc)

        for cp in small_cps[1:]:
            cp.wait_recv()
        small_cps[0].wait()
        tot = gathered[0]
        for j in range(1, N_DEV):
            tot = tot + gathered[j]
        summed = _get_flat(tot, 0, SMALL_W)
        dmod_full[...] = jnp.zeros_like(dmod_full)
        for j in range(N_DEV):
            arr = gathered[j]
            for b in range(bl):
                dmod_full[j * bl + b:j * bl + b + 1, :] = _get_flat(arr, SMALL_W + b * 3 * D, 3 * D)
        dmod_full[n_ex:n_ex + 1, :] = summed[:, _DMODC:_DMODC + 3 * D]
        sum_ref[...] = summed
        gb_ref[...] = jnp.sum(dmod_full[...], axis=0, keepdims=True)
        loss_ref[...] = (0.5 / D) * jnp.sum(summed[:, _LOSS:_LOSS + D], axis=-1, keepdims=True)

        north = c == 1
        first = (jnp.where(north, 1 - x, x), jnp.where(north, y, 1 - y))
        second = (jnp.where(north, x, 1 - x), jnp.where(north, 1 - y, y))
        for s in range(4):
            local[s].wait()
            d2d[s].wait_recv()

        def chip_sum(k, chip, relayed):
            slot = 2 * chip[0] + chip[1]

            def pair_sum(rs):
                acc = mine[slot, rs, :].astype(F32) + got_sib[slot, rs, :].astype(F32)
                if relayed:
                    acc = acc + got_chip[1, rs, :].astype(F32)
                stage[k, rs, :] = acc.astype(BF16)

            rows_loop(pair_sum)

        def send(k, to):
            rc = pltpu.make_async_remote_copy(
                src_ref=stage.at[k], dst_ref=got_chip.at[k], send_sem=ici_send.at[k], recv_sem=ici_recv.at[k],
                device_id=(to[0], to[1], c), device_id_type=MESH_ID)
            rc.start()
            ici.append(rc)

        chip_sum(0, first, False)
        send(0, first)
        chip_sum(1, (1 - x, 1 - y), False)
        send(1, first)

        cr = c_ref[...]
        act = (cr * _sigmoid(cr)).astype(BF16)
        dm = dmod_full[:, pl.ds(pl.multiple_of(me * n_mod, 128), n_mod)].astype(BF16)
        g_wm[...] = _dot_tn(act, dm)
        gc_mine[...] = _dot_nt(dm[n_ex:n_ex + 8, :], wm_ref[...].astype(BF16))
        gc_cps = direct_gather(gc_mine, gc_all, gc_send, gc_recv, misc_sems.at[1])

        ici[1].wait_recv()
        chip_sum(2, second, True)
        send(2, second)
        ici[0].wait_recv()
        ici[2].wait_recv()

        def finish(rs):
            gsum = mine[home, rs, :].astype(F32) + got_sib[home, rs, :].astype(F32)
            g_in[rs, :] = gsum + got_chip[0, rs, :].astype(F32) + got_chip[2, rs, :].astype(F32)

        rows_loop(finish)

        for cp in gc_cps[1:]:
            cp.wait_recv()
        gc_cps[0].wait()
        for rc in d2d + ici + small_cps[1:] + gc_cps[1:]:
            rc.wait_send()

    vm = pl.BlockSpec(memory_space=pltpu.VMEM)
    anyspace = pl.BlockSpec(memory_space=pl.ANY)
    assert rows_in % rb == 0 and parts_in.dtype == BF16
    args = [parts_in, *pieces, c_rows, w_mod_loc]
    in_specs = [anyspace] + [vm] * (len(args) - 1)
    out_shape = [jax.ShapeDtypeStruct(shp, F32), jax.ShapeDtypeStruct(w_mod_loc.shape, F32),
                 jax.ShapeDtypeStruct((1, SMALL_W), F32), jax.ShapeDtypeStruct((1, 3 * D), F32),
                 jax.ShapeDtypeStruct((N_DEV, 8, D), F32), jax.ShapeDtypeStruct((1, 1), F32)]
    scratch = [pltpu.VMEM((4,) + shp, BF16), pltpu.VMEM((4,) + shp, BF16), pltpu.VMEM((3,) + shp, BF16),
               pltpu.VMEM((3,) + shp, BF16), pltpu.VMEM((8, ROW_W), F32), pltpu.VMEM((N_DEV, 8, ROW_W), F32),
               pltpu.VMEM((n_ex + 8, 3 * D), F32), pltpu.VMEM((8, D), F32),
               pltpu.SemaphoreType.DMA((4,)), pltpu.SemaphoreType.DMA((4,)), pltpu.SemaphoreType.DMA((3,)),
               pltpu.SemaphoreType.DMA((3,)), pltpu.SemaphoreType.DMA((4,)),
               pltpu.SemaphoreType.DMA((N_DEV - 1,)), pltpu.SemaphoreType.DMA((N_DEV - 1,)),
               pltpu.SemaphoreType.DMA((N_DEV - 1,)), pltpu.SemaphoreType.DMA((N_DEV - 1,)),
               pltpu.SemaphoreType.DMA((2,))]
    return pl.pallas_call(
        body, name="epilogue", out_shape=out_shape, in_specs=in_specs, out_specs=[vm] * len(out_shape),
        scratch_shapes=scratch, compiler_params=pltpu.CompilerParams(vmem_limit_bytes=VMEM_LIMIT),
    )(*args)


def _final_adamw(g_in, w_in_t, m_in_t, v_in_t, g_wm, w_mod_loc, m_mod, v_mod, summed, g_bmod, gc_all,
                 small_w, small_m, small_v):
    ns = len(_SMALL)

    def body(*refs):
        it = iter(refs)
        take = lambda k: [next(it) for _ in range(k)]
        (gin_ref, w_ref, m_ref, v_ref, gwm_ref, wm_ref, mm_ref, vm_ref, sum_ref, gb_ref, gc_ref) = take(11)
        sw, sm, sv = take(ns), take(ns), take(ns)
        (d_in, nm_in, nv_in, d_wm, nm_wm, nv_wm) = take(6)
        souts = take(4 * ns)

        for g_r, w_r, m_r, v_r, outs3 in ((gin_ref, w_ref, m_ref, v_ref, (d_in, nm_in, nv_in)),
                                          (gwm_ref, wm_ref, mm_ref, vm_ref, (d_wm, nm_wm, nv_wm))):
            for o_r, val in zip(outs3, _adamw(w_r[...], g_r[...], m_r[...], v_r[...])):
                o_r[...] = val

        @pl.when(pl.program_id(0) == 0)
        def _():
            for k, (name, off) in enumerate(_SMALL):
                w = sw[k][...]
                if name == "b_mod":
                    gk = gb_ref[...]
                elif name == "c_ctx":
                    acc = gc_ref[0, 0:1, :]
                    for j in range(1, N_DEV):
                        acc = acc + gc_ref[j, 0:1, :]
                    sg = _sigmoid(w)
                    gk = acc * (sg * (1.0 + w * (1.0 - sg)))
                else:
                    gk = sum_ref[:, off:off + w.shape[1]]
                dl, m_new, v_new = _adamw(w, gk, sm[k][...], sv[k][...])
                souts[k][...] = gk
                souts[ns + k][...] = dl
                souts[2 * ns + k][...] = m_new
                souts[3 * ns + k][...] = v_new

    rows_in, rows_mod = w_in_t.shape[0] // ADAM_STEPS, w_mod_loc.shape[0] // ADAM_STEPS
    assert rows_in % 8 == 0 and rows_mod % 8 == 0

    def whole(a):
        return pl.BlockSpec(a.shape, lambda i: (0,) * a.ndim)

    in_tile = pl.BlockSpec((rows_in, w_in_t.shape[1]), lambda i: (i, 0))
    mod_tile = pl.BlockSpec((rows_mod, w_mod_loc.shape[1]), lambda i: (i, 0))
    small_in = [summed, g_bmod, gc_all, *small_w, *small_m, *small_v]
    big_shape = jax.ShapeDtypeStruct(w_in_t.shape, F32)
    mod_shape = jax.ShapeDtypeStruct(w_mod_loc.shape, F32)
    out_shape = [big_shape] * 3 + [mod_shape] * 3 + [jax.ShapeDtypeStruct(w.shape, F32) for w in small_w] * 4
    outs = pl.pallas_call(
        body, name="final_adamw", grid=(ADAM_STEPS,), out_shape=out_shape,
        in_specs=[in_tile] * 4 + [mod_tile] * 4 + [whole(a) for a in small_in],
        out_specs=[in_tile] * 3 + [mod_tile] * 3 + [whole(w) for w in small_w] * 4,
        compiler_params=_params(("arbitrary",)),
    )(g_in, w_in_t, m_in_t, v_in_t, g_wm, w_mod_loc, m_mod, v_mod, *small_in)
    small_outs = [outs[6 + k * ns:6 + (k + 1) * ns] for k in range(4)]
    return (g_in,) + tuple(outs[0:3]), (g_wm,) + tuple(outs[3:6]), small_outs


def _rope_tables(s):
    t = jnp.arange(s, dtype=jnp.int32)
    row = (t // GRID_W).astype(F32)
    col = (t % GRID_W).astype(F32)
    freqs = ROPE_THETA ** (-jnp.arange(0, HD // 2, 2, dtype=F32) / (HD // 2))
    ang_r = row[:, None] * freqs[None, :]
    ang_c = col[:, None] * freqs[None, :]
    cr, sr, cc, sc = jnp.cos(ang_r), jnp.sin(ang_r), jnp.cos(ang_c), jnp.sin(ang_c)
    cos = jnp.concatenate([cr, cr, cc, cc], axis=-1)
    sins = jnp.concatenate([-sr, sr, -sc, sc], axis=-1)
    return jnp.tile(cos, (1, KVW // HD)), jnp.tile(sins, (1, KVW // HD))


def kernel(x, c, ctx, c_ctx, w_mod, b_mod, norm_w, w_in, q_norm_w, k_norm_w, conv_w, conv_b, conv_ln_w, conv_ln_b, w_pw, b_pw, w_out, loss_target, m_c_ctx, m_w_mod, m_b_mod, m_norm_w, m_w_in, m_q_norm_w, m_k_norm_w, m_conv_w, m_conv_b, m_conv_ln_w, m_conv_ln_b, m_w_pw, m_b_pw, m_w_out, v_c_ctx, v_w_mod, v_b_mod, v_norm_w, v_w_in, v_q_norm_w, v_k_norm_w, v_conv_w, v_conv_b, v_conv_ln_w, v_conv_ln_b, v_w_pw, v_b_pw, v_w_out):
    bl, s, _ = x.shape
    cl = ctx.shape[1]
    me = _lin(*_coords())

    conv_w_pad = jnp.pad(conv_w[0], ((0, 32 - KW), (0, 0)))
    n_ex = N_DEV * bl
    g_win, c_rows, g_mod = _prologue(w_in[0].T, c, c_ctx[None, :], w_mod[0], b_mod)
    w_in_b = g_win.reshape(D_IN, D)
    mod_all = g_mod.transpose(1, 0, 2).reshape(n_ex + 8, 3 * D)
    modrows = lax.dynamic_slice_in_dim(mod_all, me * bl, bl, axis=0).reshape(bl, 3, D)
    modc = mod_all[n_ex].reshape(1, 3, D)

    cos, sins = _rope_tables(s)
    qnw_t = jnp.tile(q_norm_w, (1, DA // HD))
    knw_t = jnp.tile(k_norm_w, (1, KVW // HD))
    lane = jnp.arange(DA, dtype=jnp.int32) // HD
    ones_bd = (lane[:, None] == lane[None, :]).astype(BF16)
    ones_kv = ones_bd[0:KVW, 0:KVW]
    w_kv_b = w_in_b

    k_ctx, v_ctx, pkv_c = _ctx_fwd(ctx, modc, norm_w, w_kv_b, knw_t, ones_kv, cl + s)
    (q_h, k_h, v_h, pq, pkv, za, glu, zc), (g_wout, g_wpw, g_cw) = _fwd_in(
        x, modrows, norm_w, w_in_b, cos, sins, qnw_t, knw_t, ones_bd, k_ctx, v_ctx,
        [w_out[0], w_pw[0], conv_w_pad], [BF16, BF16, F32])
    w_out_b = g_wout.reshape(D, D)
    w_pw_b = g_wpw.reshape(DC, DC)
    conv_w_full = g_cw.transpose(1, 0, 2).reshape(32, DC)
    attn, lse = _attn_fwd(q_h, k_h, v_h)
    y_conv, cp = _conv_fwd(glu, conv_w_full, conv_b, conv_ln_w, conv_ln_b, w_pw_b, b_pw)

    do_h, dza, dy_conv, dzc, dh, dgate, gw_out, loss_row, gw_pw, conv_rows = _out_fwd_bwd(
        attn, za, cp, zc, x, loss_target, modrows, w_out_b, y_conv, conv_ln_w, conv_ln_b, w_pw_b)
    dglu, g_cw_full = _conv_bwd_depthwise(glu, dy_conv, conv_w_full)
    parts_out = gw_out.reshape(N_DEV, D // N_DEV, D)
    parts_pw = gw_pw.reshape(N_DEV, DC // N_DEV, DC)
    parts_cw = g_cw_full.astype(BF16).reshape(32, N_DEV, DC // N_DEV).transpose(1, 0, 2)
    (dq, dk_h, dv_h), (got_out, got_pw, got_cw) = _attn_bwd(
        q_h, k_h, v_h, do_h, attn, lse, [parts_out, parts_pw, parts_cw])
    gw_kv, ctx_rows, dknw_c = _ctx_bwd(ctx, modc, norm_w, w_kv_b, pkv_c, dk_h, dv_h, knw_t, ones_kv)
    grad_x, gw_in, dmod_ss, dnw, dqnw, dknw = _bwd_in(
        x, modrows, norm_w, w_in_b, cos, sins, qnw_t, knw_t, ones_bd,
        pq, pkv, dq, dk_h, dv_h, dza, dglu, dzc, dh, gw_kv)

    r_out, r_pw, r_cw = _sum_devices_adamw(
        [(got_out, w_out[0], m_w_out[0], v_w_out[0]), (got_pw, w_pw[0], m_w_pw[0], v_w_pw[0]),
         (got_cw, conv_w[0], m_conv_w[0], v_conv_w[0])])

    given = {"c_ctx": (c_ctx, m_c_ctx, v_c_ctx), "b_mod": (b_mod, m_b_mod, v_b_mod), "norm_w": (norm_w, m_norm_w, v_norm_w),
             "q_norm_w": (q_norm_w, m_q_norm_w, v_q_norm_w), "k_norm_w": (k_norm_w, m_k_norm_w, v_k_norm_w),
             "conv_b": (conv_b, m_conv_b, v_conv_b), "conv_ln_w": (conv_ln_w, m_conv_ln_w, v_conv_ln_w),
             "conv_ln_b": (conv_ln_b, m_conv_ln_b, v_conv_ln_b), "b_pw": (b_pw, m_b_pw, v_b_pw)}
    as_rows = [[given[name][which].reshape(1, -1) for name, _ in _SMALL] for which in range(3)]
    g_in_t, g_wmod, summed, g_bmod, gc_all, loss11 = _epilogue(
        gw_in.reshape(N_DEV, D_IN // N_DEV, D),
        [loss_row, ctx_rows, dnw, dqnw, dknw, dknw_c, conv_rows, dmod_ss, dgate], c_rows, w_mod[0])
    r_in, r_wmod, small_outs = _final_adamw(
        g_in_t, w_in[0].T, m_w_in[0].T, v_w_in[0].T, g_wmod, w_mod[0], m_w_mod[0], v_w_mod[0],
        summed, g_bmod, gc_all, *as_rows)
    r_in = tuple(a.T for a in r_in)

    big = {"w_mod": r_wmod, "w_in": r_in, "conv_w": r_cw, "w_pw": r_pw, "w_out": r_out}
    order = ["c_ctx", "w_mod", "b_mod", "norm_w", "w_in", "q_norm_w", "k_norm_w", "conv_w", "conv_b", "conv_ln_w",
             "conv_ln_b", "w_pw", "b_pw", "w_out"]
    small_index = {name: k for k, (name, _) in enumerate(_SMALL)}
    outs = [loss11.reshape(()), grad_x]
    for which in range(4):
        for name in order:
            if name in big:
                outs.append(big[name][which][None])
            else:
                outs.append(small_outs[which][small_index[name]].reshape(given[name][0].shape))
    return tuple(outs)
```

```python
import jax
import jax.numpy as jnp
from jax import lax
from jax.experimental import pallas as pl
from jax.experimental.pallas import tpu as pltpu

F32, BF16 = jnp.float32, jnp.bfloat16
MESH_ID = pl.DeviceIdType.MESH

N_DEV = 8
D = 1024
D_IN = 2816
DA = 512
DC = 512
HD = 64
KVW = 128
KW = 31
HALO = 16
EPS = 1e-6
ROPE_THETA = 10000.0
GRID_W = 64

ADAM_LR, ADAM_B1, ADAM_B2, ADAM_EPS, ADAM_WD, ADAM_STEP = 0.001, 0.9, 0.999, 1e-08, 0.01, 10

VMEM_LIMIT = 56 * 1024 * 1024

TM = 256
TQ = 128
TOKEN_PARTS = 2
OUT_TM = 512
BWD_PARTS = 4
FWD_PARTS = 8
TC = 512
CH = 32
ADAM_STEPS = 4


def _params(sem, vmem=VMEM_LIMIT):
    return pltpu.CompilerParams(dimension_semantics=sem, vmem_limit_bytes=vmem)


def _dot(a, b):
    return jnp.dot(a, b, preferred_element_type=F32)


def _dot_nt(a, b):
    return lax.dot_general(a, b, (((1,), (1,)), ((), ())), preferred_element_type=F32)


def _dot_tn(a, b):
    return lax.dot_general(a, b, (((0,), (0,)), ((), ())), preferred_element_type=F32)


def _sigmoid(z):
    return 1.0 / (1.0 + jnp.exp(-z))


def _segsum(v, ones_bd):
    return _dot(v.astype(BF16), ones_bd)


def _swap16(x):
    w = x.shape[-1]
    lane = lax.broadcasted_iota(jnp.int32, x.shape, 1)
    return jnp.where((lane % 32) < 16, pltpu.roll(x, w - 16, 1), pltpu.roll(x, 16, 1))


def _with_ones_column(v):
    one = (lax.broadcasted_iota(jnp.int32, v.shape, 1) == 0).astype(v.dtype)
    return jnp.concatenate([v, one], axis=-1)


def _rope(x, cos, sins):
    return x * cos + _swap16(x) * sins


def _rope_bwd(d, cos, sins):
    return d * cos + _swap16(d * sins)


def _adamw(w, g, m, v):
    m2 = ADAM_B1 * m + (1.0 - ADAM_B1) * g
    v2 = ADAM_B2 * v + (1.0 - ADAM_B2) * (g * g)
    m_hat = m2 / (1.0 - ADAM_B1 ** ADAM_STEP)
    v_hat = v2 / (1.0 - ADAM_B2 ** ADAM_STEP)
    delta = -ADAM_LR * (m_hat / (jnp.sqrt(v_hat) + ADAM_EPS) + ADAM_WD * w)
    return delta, m2, v2


def _coords():
    return lax.axis_index("x"), lax.axis_index("y"), lax.axis_index("c")


def _lin(x, y, c):
    return 4 * x + 2 * y + c


def _prologue(w_in_t, c, c_ctx_row, w_mod_loc, b_mod):
    bl = c.shape[0]
    n_ex = N_DEV * bl
    n_mod = w_mod_loc.shape[1]

    def body(w32_ref, c_in_ref, cctx_ref, wm_ref, b_ref, out_w, crows_ref, mod_out, w_ref, c_ref, c_gath, mod_mine,
             w_send, w_recv, c_send, c_recv, m_send, m_recv, local_sems):
        x, y, c = _coords()
        me_lin = _lin(x, y, c)
        c_ref[...] = jnp.zeros_like(c_ref)
        c_ref[0:bl, :] = c_in_ref[...]
        w_ref[...] = w32_ref[...].astype(BF16)
        me, sib = (x, y, c), (x, y, 1 - c)
        xnb, ynb, diag = (1 - x, y), (x, 1 - y), (1 - x, 1 - y)
        north = c == 1

        def direct_gather(src, dst, send_sems, recv_sems, local_sem):
            cps = [pltpu.make_async_copy(src, dst.at[me_lin], local_sem)]
            for k in range(1, N_DEV):
                peer = (1 - x if k & 4 else x, 1 - y if k & 2 else y, 1 - c if k & 1 else c)
                cps.append(pltpu.make_async_remote_copy(
                    src_ref=src, dst_ref=dst.at[me_lin], send_sem=send_sems.at[k - 1], recv_sem=recv_sems.at[k - 1],
                    device_id=peer, device_id_type=MESH_ID))
            for cp in cps:
                cp.start()
            return cps

        def copy(k, block, to, src=None):
            slot = out_w.at[_lin(*block)]
            return pltpu.make_async_remote_copy(
                src_ref=slot if src is None else src, dst_ref=slot, send_sem=w_send.at[k], recv_sem=w_recv.at[k],
                device_id=to, device_id_type=MESH_ID)

        c_cps = direct_gather(c_ref, c_gath, c_send, c_recv, local_sems.at[0])
        mine = pltpu.make_async_copy(w_ref, out_w.at[me_lin], local_sems.at[1])
        mine.start()
        first = [copy(0, me, sib, src=w_ref), copy(1, me, (*xnb, c), src=w_ref), copy(2, me, (*ynb, c), src=w_ref)]
        for cp in first:
            cp.start()

        for cp in c_cps[1:]:
            cp.wait_recv()
        c_cps[0].wait()
        crows_ref[...] = jnp.zeros_like(crows_ref)
        for j in range(N_DEV):
            crows_ref[j * bl:(j + 1) * bl, :] = c_gath[j, 0:bl, :]
        crows_ref[n_ex:n_ex + 1, :] = cctx_ref[...]
        cr = crows_ref[...]
        act = (cr * _sigmoid(cr)).astype(BF16)
        mod_mine[...] = _dot(act, wm_ref[...].astype(BF16)) + b_ref[:, pl.ds(pl.multiple_of(me_lin * n_mod, 128), n_mod)]
        mod_cps = direct_gather(mod_mine, mod_out, m_send, m_recv, local_sems.at[2])

        relay_north = copy(3, (*xnb, c), (*ynb, c))
        relay_south = copy(3, (*ynb, c), (*xnb, c))
        passed = []
        copy(1, (*xnb, c), me).wait_recv()
        pl.when(north)(relay_north.start)
        passed.append(copy(4, (*xnb, c), sib))
        passed[-1].start()
        copy(2, (*ynb, c), me).wait_recv()
        pl.when(jnp.logical_not(north))(relay_south.start)
        passed.append(copy(5, (*ynb, c), sib))
        passed[-1].start()
        copy(3, (*diag, c), me).wait_recv()
        passed.append(copy(6, (*diag, c), sib))
        passed[-1].start()
        copy(0, sib, me).wait_recv()
        for k, chip in ((4, xnb), (5, ynb), (6, diag)):
            copy(k, (*chip, 1 - c), me).wait_recv()
        for cp in mod_cps[1:]:
            cp.wait_recv()
        mod_cps[0].wait()
        for cp in first + passed + [relay_north] + c_cps[1:] + mod_cps[1:]:
            cp.wait_send()
        mine.wait()

    vm = pl.BlockSpec(memory_space=pltpu.VMEM)
    seven = pltpu.SemaphoreType.DMA((N_DEV - 1,))
    return pl.pallas_call(
        body, name="prologue",
        out_shape=[jax.ShapeDtypeStruct((N_DEV,) + w_in_t.shape, BF16), jax.ShapeDtypeStruct((n_ex + 8, D), F32),
                   jax.ShapeDtypeStruct((N_DEV, n_ex + 8, n_mod), F32)],
        in_specs=[vm] * 5, out_specs=[pl.BlockSpec(memory_space=pl.ANY), vm, vm],
        scratch_shapes=[pltpu.VMEM(w_in_t.shape, BF16), pltpu.VMEM((8, D), F32), pltpu.VMEM((N_DEV, 8, D), F32),
                        pltpu.VMEM((n_ex + 8, n_mod), F32),
                        seven, seven, seven, seven, seven, seven, pltpu.SemaphoreType.DMA((3,))],
        compiler_params=pltpu.CompilerParams(vmem_limit_bytes=VMEM_LIMIT),
    )(w_in_t, c, c_ctx_row, w_mod_loc, b_mod)


def _exchange_copies(in_refs, out_refs, send_sems, recv_sems, local_sems, scatter):
    x, y, c = _coords()
    me = _lin(x, y, c)
    local, remote = [], []
    for a, (src, dst) in enumerate(zip(in_refs, out_refs)):
        local.append(pltpu.make_async_copy(src.at[me] if scatter else src, dst.at[me], local_sems.at[a]))
        for k in range(1, N_DEV):
            peer = (1 - x if k & 4 else x, 1 - y if k & 2 else y, 1 - c if k & 1 else c)
            remote.append(pltpu.make_async_remote_copy(
                src_ref=src.at[_lin(*peer)] if scatter else src, dst_ref=dst.at[me],
                send_sem=send_sems.at[a * (N_DEV - 1) + k - 1], recv_sem=recv_sems.at[a * (N_DEV - 1) + k - 1],
                device_id=peer, device_id_type=MESH_ID))
    return local, remote


def _exchange_scratch(n):
    return [pltpu.SemaphoreType.DMA((n * (N_DEV - 1),)), pltpu.SemaphoreType.DMA((n * (N_DEV - 1),)),
            pltpu.SemaphoreType.DMA((n,))]


def _sum_devices_adamw(items):
    n = len(items)

    def body(*refs):
        for a in range(n):
            got, w_ref, m_ref, v_ref = refs[4 * a:4 * a + 4]
            g_ref, d_ref, nm_ref, nv_ref = refs[4 * n + 4 * a:4 * n + 4 * a + 4]
            g = got[0].astype(F32)
            for j in range(1, N_DEV):
                g = g + got[j].astype(F32)
            g = g[0:w_ref.shape[0], :]
            delta, m2, v2 = _adamw(w_ref[...], g, m_ref[...], v_ref[...])
            g_ref[...] = g
            d_ref[...] = delta
            nm_ref[...] = m2
            nv_ref[...] = v2

    args, out_shape = [], []
    for got, w, m, v in items:
        assert got.shape[0] == N_DEV and got.shape[1] >= w.shape[0] and got.shape[2:] == w.shape[1:]
        args += [got, w, m, v]
        out_shape += [jax.ShapeDtypeStruct(w.shape, F32)] * 4
    outs = pl.pallas_call(body, name="sum_devices_adamw", out_shape=out_shape,
                          compiler_params=pltpu.CompilerParams(vmem_limit_bytes=VMEM_LIMIT))(*args)
    return [tuple(outs[4 * a:4 * a + 4]) for a in range(n)]


def _fwd_in(x, modrows, norm_w, w_in_b, cos, sins, qnw_t, knw_t, ones_bd, k_all, v_all, shards, wire_dtypes):
    bl, s, _ = x.shape
    tm = TOKEN_PARTS * TM
    nt = s // tm
    n_sh = len(shards)

    def body(*refs):
        (x_ref, mod_ref, nw_ref, win_ref, cos_ref, sin_ref, qnw_ref, knw_ref, bd_ref, kin_ref, vin_ref) = refs[:11]
        shard_refs = refs[11:11 + n_sh]
        q_ref, k_ref, v_ref, pq_ref, pkv_ref, za_ref, glu_ref, zc_ref = refs[11 + n_sh:19 + n_sh]
        gathered_refs = refs[19 + n_sh:19 + 2 * n_sh]
        stage_refs = refs[19 + 2 * n_sh:19 + 3 * n_sh]
        send_sems, recv_sems, local_sems = refs[19 + 3 * n_sh:]
        b, i = pl.program_id(0), pl.program_id(1)
        local, remote = _exchange_copies(stage_refs, gathered_refs, send_sems, recv_sems, local_sems, scatter=False)

        @pl.when((b == 0) & (i == 0))
        def _():
            for src, stage in zip(shard_refs, stage_refs):
                stage[...] = src[...].astype(stage.dtype)
            for cp in local + remote:
                cp.start()

        shift = mod_ref[0, 0:1, :]
        scale = mod_ref[0, 1:2, :]
        for part in range(TOKEN_PARTS):
            rows = pl.ds(part * TM, TM)
            xv = x_ref[0, rows, :]
            r = lax.rsqrt(jnp.mean(xv * xv, axis=-1, keepdims=True) + EPS)
            u = (xv * r * nw_ref[...]) * (1.0 + scale) + shift
            p = _dot_nt(u.astype(BF16), win_ref[...])
            pq = p[:, 0:DA]
            pk = p[:, DA:DA + HD * 2]
            ck = cos_ref[rows, :]
            sk = sin_ref[rows, :]
            cs = jnp.concatenate([ck] * (DA // KVW), axis=-1)
            sn = jnp.concatenate([sk] * (DA // KVW), axis=-1)
            rq = lax.rsqrt(_segsum(pq * pq, bd_ref[...]) * (1.0 / HD) + EPS)
            qn = pq * rq * qnw_ref[...]
            qr = _rope(qn, cs, sn) * 0.125
            for h in range(DA // HD):
                q_ref[0, h, rows, :] = qr[:, h * HD:(h + 1) * HD].astype(BF16)
            rk = lax.rsqrt(_segsum(pk * pk, bd_ref[0:KVW, 0:KVW]) * (1.0 / HD) + EPS)
            kn = pk * rk * knw_ref[...]
            kr = _rope(kn, ck, sk)
            pv = p[:, 640:768]
            for h in range(KVW // HD):
                k_ref[0, h, rows, :] = kr[:, h * HD:(h + 1) * HD].astype(BF16)
                v_ref[0, h, rows, :] = _with_ones_column(pv[:, h * HD:(h + 1) * HD]).astype(BF16)
            pq_ref[0, rows, :] = pq
            pkv_ref[0, rows, :] = p[:, 512:768]
            za_ref[0, rows, :] = p[:, 768:1280]
            glu_ref[0, rows, :] = p[:, 1280:2304]
            zc_ref[0, rows, :] = p[:, 2304:2816]

        @pl.when((b == bl - 1) & (i == nt - 1))
        def _():
            for cp in remote:
                cp.wait_recv()
            for cp in remote:
                cp.wait_send()
            for cp in local:
                cp.wait()

    def tile(w):
        return pl.BlockSpec((1, tm, w), lambda b, i: (b, i, 0))

    def const(shape):
        return pl.BlockSpec(shape, lambda b, i: (0,) * len(shape))

    outs = [(DA, F32), (2 * KVW, F32), (DA, F32), (2 * DC, F32), (DC, F32)]
    anyspace = pl.BlockSpec(memory_space=pl.ANY)
    rope = pl.BlockSpec((tm, KVW), lambda b, i: (i, 0))
    k_tile = pl.BlockSpec((1, KVW // HD, tm, HD), lambda b, i: (b, 0, i, 0))
    v_tile = pl.BlockSpec((1, KVW // HD, tm, 2 * HD), lambda b, i: (b, 0, i, 0))
    res = pl.pallas_call(
        body, name="fwd_in", grid=(bl, nt),
        in_specs=[tile(D), pl.BlockSpec((1, 3, D), lambda b, i: (b, 0, 0)), const((1, D)), const((D_IN, D)),
                  rope, rope, const((1, DA)), const((1, KVW)), const((DA, DA)), anyspace, anyspace]
        + [const(a.shape) for a in shards],
        out_specs=[pl.BlockSpec((1, DA // HD, tm, HD), lambda b, i: (b, 0, i, 0)), k_tile, v_tile]
        + [tile(w) for w, _ in outs] + [anyspace] * n_sh,
        out_shape=[jax.ShapeDtypeStruct((bl, DA // HD, s, HD), BF16), jax.ShapeDtypeStruct(k_all.shape, BF16),
                   jax.ShapeDtypeStruct(v_all.shape, BF16)]
        + [jax.ShapeDtypeStruct((bl, s, w), dt) for w, dt in outs]
        + [jax.ShapeDtypeStruct((N_DEV,) + a.shape, dt) for a, dt in zip(shards, wire_dtypes)],
        input_output_aliases={9: 1, 10: 2},
        scratch_shapes=[pltpu.VMEM(a.shape, dt) for a, dt in zip(shards, wire_dtypes)] + _exchange_scratch(n_sh),
        compiler_params=_params(("arbitrary", "arbitrary")),
    )(x, modrows, norm_w, w_in_b, cos, sins, qnw_t, knw_t, ones_bd, k_all, v_all, *shards)
    return res[:8], res[8:]


_KV_ROWS_OF_W_IN_T = pl.BlockSpec((2 * KVW, D), lambda b: (DA // (2 * KVW), 0))


def _ctx_fwd(ctx, modc, norm_w, w_kv_b, knw_t, ones_bd, n_keys):
    bl, cl, _ = ctx.shape

    def body(x_ref, mod_ref, nw_ref, w_ref, knw_ref, bd_ref, k_ref, v_ref, pkv_ref):
        xv = x_ref[0]
        shift = mod_ref[0, 0:1, :]
        scale = mod_ref[0, 1:2, :]
        r = lax.rsqrt(jnp.mean(xv * xv, axis=-1, keepdims=True) + EPS)
        u = (xv * r * nw_ref[...]) * (1.0 + scale) + shift
        p = _dot_nt(u.astype(BF16), w_ref[...])
        pk = p[:, 0:KVW]
        rk = lax.rsqrt(_segsum(pk * pk, bd_ref[...]) * (1.0 / HD) + EPS)
        kn = pk * rk * knw_ref[...]
        pv = p[:, KVW:2 * KVW]
        for h in range(KVW // HD):
            k_ref[0, h] = kn[:, h * HD:(h + 1) * HD].astype(BF16)
            v_ref[0, h] = _with_ones_column(pv[:, h * HD:(h + 1) * HD]).astype(BF16)
        pkv_ref[0] = p

    def const(shape):
        return pl.BlockSpec(shape, lambda b: (0,) * len(shape))

    def tile(w):
        return pl.BlockSpec((1, cl, w), lambda b: (b, 0, 0))

    ctx_block = (n_keys - cl) // cl
    assert ctx_block * cl + cl == n_keys
    k_tile = pl.BlockSpec((1, KVW // HD, cl, HD), lambda b: (b, 0, ctx_block, 0))
    v_tile = pl.BlockSpec((1, KVW // HD, cl, 2 * HD), lambda b: (b, 0, ctx_block, 0))
    return pl.pallas_call(
        body, name="ctx_fwd", grid=(bl,),
        in_specs=[tile(D), const((1, 3, D)), const((1, D)), _KV_ROWS_OF_W_IN_T, const((1, KVW)), const((KVW, KVW))],
        out_specs=[k_tile, v_tile, tile(2 * KVW)],
        out_shape=[jax.ShapeDtypeStruct((bl, KVW // HD, n_keys, HD), BF16),
                   jax.ShapeDtypeStruct((bl, KVW // HD, n_keys, 2 * HD), BF16),
                   jax.ShapeDtypeStruct((bl, cl, 2 * KVW), F32)],
        compiler_params=_params(("arbitrary",)),
    )(ctx, modc, norm_w, w_kv_b, knw_t, ones_bd)


def _attn_fwd(q, k, v1):
    bl, _, s, _ = q.shape
    n_keys = k.shape[2]

    def body(q_ref, k_ref, v_ref, o_ref, lse_ref):
        kv = k_ref[0, 0]
        vv = v_ref[0, 0]
        lane = lax.broadcasted_iota(jnp.int32, (TQ, 2 * HD), 1)
        for part in range(FWD_PARTS):
            rows = pl.ds(part * TQ, TQ)
            lse = jnp.zeros((TQ, 2 * HD), F32)
            heads = []
            sc_all = _dot_nt(q_ref[0, :, rows, :].reshape(4 * TQ, HD), kv)
            for h in range(4):
                sc = sc_all[h * TQ:(h + 1) * TQ, :]
                m = jnp.max(sc, axis=-1, keepdims=True)
                e = jnp.exp(sc - m).astype(BF16)
                ov = _dot(e, vv)
                denom = ov[:, HD:HD + 1]
                heads.append(ov[:, 0:HD] * (1.0 / denom))
                lse = jnp.where(lane == h, m + jnp.log(denom), lse)
            o_ref[0, rows, :] = jnp.concatenate(heads, axis=-1)
            lse_ref[0, 0, rows, :] = lse

    tq = FWD_PARTS * TQ
    ks = pl.BlockSpec((1, 1, n_keys, HD), lambda b, g, i: (b, g, 0, 0))
    qs = pl.BlockSpec((1, 4, tq, HD), lambda b, g, i: (b, g, i, 0))
    vs = pl.BlockSpec((1, 1, n_keys, 2 * HD), lambda b, g, i: (b, g, 0, 0))
    return pl.pallas_call(
        body, name="attn_fwd", grid=(bl, 2, s // tq), in_specs=[qs, ks, vs],
        out_specs=[pl.BlockSpec((1, tq, 4 * HD), lambda b, g, i: (b, i, g)),
                   pl.BlockSpec((1, 1, tq, 2 * HD), lambda b, g, i: (b, g, i, 0))],
        out_shape=[jax.ShapeDtypeStruct((bl, s, DA), F32), jax.ShapeDtypeStruct((bl, 2, s, 2 * HD), F32)],
        compiler_params=_params(("arbitrary", "arbitrary", "arbitrary")),
    )(q, k, v1)


def _attn_bwd(q, k, v1, do, o, lse, exchange):
    bl, _, s, _ = q.shape
    n_keys = k.shape[2]
    tq = BWD_PARTS * TQ
    nq = s // tq
    n_ex = len(exchange)

    def body(*refs):
        q_ref, k_ref, v_ref, do_ref, o_ref, lse_ref = refs[:6]
        part_refs = refs[6:6 + n_ex]
        dq_ref, dk_ref, dv_ref = refs[6 + n_ex:9 + n_ex]
        got_refs = refs[9 + n_ex:9 + 2 * n_ex]
        p_sc, ds_sc, dkt, dvt, send_sems, recv_sems, local_sems = refs[9 + 2 * n_ex:]
        i = pl.program_id(2)
        first = (pl.program_id(0) == 0) & (pl.program_id(1) == 0) & (i == 0)
        last = (pl.program_id(0) == bl - 1) & (pl.program_id(1) == 1) & (i == nq - 1)
        local, remote = _exchange_copies(part_refs, got_refs, send_sems, recv_sems, local_sems, scatter=True)

        @pl.when(first)
        def _():
            for cp in local + remote:
                cp.start()

        @pl.when(i == 0)
        def _():
            dkt[...] = jnp.zeros_like(dkt)
            dvt[...] = jnp.zeros_like(dvt)

        kv = k_ref[0, 0]
        vv = v_ref[0, 0][:, 0:HD]
        for part in range(BWD_PARTS):
            tq_rows = pl.ds(part * TQ, TQ)
            lse = lse_ref[0, 0, tq_rows, :]
            ov = o_ref[0, tq_rows, :]
            dqs = []
            q_cat = q_ref[0, :, tq_rows, :].reshape(4 * TQ, HD)
            do_cat = do_ref[0, :, tq_rows, :].reshape(4 * TQ, HD)
            sc_all = _dot_nt(q_cat, kv)
            for h in range(4):
                doh = do_cat[h * TQ:(h + 1) * TQ, :]
                delta = jnp.sum(ov[:, h * HD:(h + 1) * HD] * doh.astype(F32), axis=-1, keepdims=True)
                rows = pl.ds((part * 4 + h) * TQ, TQ)
                p = jnp.exp(sc_all[h * TQ:(h + 1) * TQ, :] - lse[:, h:h + 1])
                ds = (p * (_dot_nt(doh, vv) - delta)).astype(BF16)
                p_sc[rows, :] = p.astype(BF16)
                ds_sc[rows, :] = ds
                dqs.append(_dot(ds, kv) * 0.125)
            dq_ref[0, tq_rows, :] = jnp.concatenate(dqs, axis=-1)
            part_rows = pl.ds(part * 4 * TQ, 4 * TQ)
            dvt[...] += _dot_tn(do_cat, p_sc[part_rows, :])
            dkt[...] += _dot_tn(q_cat, ds_sc[part_rows, :])

        @pl.when(i == nq - 1)
        def _():
            dk_ref[0, 0] = dkt[...].T
            dv_ref[0, 0] = dvt[...].T

        @pl.when(last)
        def _():
            for cp in remote:
                cp.wait_recv()
            for cp in remote:
                cp.wait_send()
            for cp in local:
                cp.wait()

    qs = pl.BlockSpec((1, 4, tq, HD), lambda b, g, i: (b, g, i, 0))
    ks = pl.BlockSpec((1, 1, n_keys, HD), lambda b, g, i: (b, g, 0, 0))
    vs = pl.BlockSpec((1, 1, n_keys, 2 * HD), lambda b, g, i: (b, g, 0, 0))
    os_ = pl.BlockSpec((1, tq, 4 * HD), lambda b, g, i: (b, i, g))
    kshape = jax.ShapeDtypeStruct(k.shape, F32)
    anyspace = pl.BlockSpec(memory_space=pl.ANY)
    res = pl.pallas_call(
        body, name="attn_bwd", grid=(bl, 2, nq),
        in_specs=[qs, ks, vs, qs, os_, pl.BlockSpec((1, 1, tq, 2 * HD), lambda b, g, i: (b, g, i, 0))]
        + [anyspace] * n_ex,
        out_specs=[os_, ks, ks] + [anyspace] * n_ex,
        out_shape=[jax.ShapeDtypeStruct((bl, s, DA), F32), kshape, kshape]
        + [jax.ShapeDtypeStruct(a.shape, a.dtype) for a in exchange],
        scratch_shapes=[pltpu.VMEM((4 * tq, n_keys), BF16), pltpu.VMEM((4 * tq, n_keys), BF16),
                        pltpu.VMEM((HD, n_keys), F32), pltpu.VMEM((HD, n_keys), F32)] + _exchange_scratch(n_ex),
        compiler_params=_params(("arbitrary", "arbitrary", "arbitrary")),
    )(q, k, v1, do, o, lse, *exchange)
    return res[:3], res[3:]


def _halo_specs(width, s):
    per = TC // HALO
    last = s // HALO - 1
    main = pl.BlockSpec((1, TC, width), lambda b, i: (b, i, 0))
    prev = pl.BlockSpec((1, HALO, width), lambda b, i: (b, jnp.maximum(i * per - 1, 0), 0))
    nxt = pl.BlockSpec((1, HALO, width), lambda b, i: (b, jnp.minimum((i + 1) * per, last), 0))
    return main, prev, nxt


def _glu(g):
    return g[:, 0:DC] * _sigmoid(g[:, DC:2 * DC])


def _fill_padded(pad_ref, main, prev, nxt, first, last):
    tc = main.shape[0]
    pad_ref[0:HALO, :] = jnp.where(first, 0.0, prev)
    pad_ref[HALO:HALO + tc, :] = main
    pad_ref[HALO + tc:2 * HALO + tc, :] = jnp.where(last, 0.0, nxt)


PLANE_ROWS = TC + 2 * HALO - 8


def _shift_planes(pad_ref, planes_ref):
    for r in range(1, 8):
        planes_ref[r - 1] = pad_ref[pl.ds(r, planes_ref.shape[1]), :]


def _tap_rows(pad_ref, planes_ref, offset, start, n):
    a, r = divmod(offset, 8)
    if r == 0:
        return pad_ref[pl.ds(start + 8 * a, n), :]
    return planes_ref[r - 1, pl.ds(start + 8 * a, n), :]


def _conv_fwd(glu, conv_w, conv_b, ln_w, ln_b, w_pw_b, b_pw):
    bl, s, _ = glu.shape
    nt = s // TC

    def body(g_ref, gp_ref, gn_ref, cw_ref, cb_ref, lw_ref, lb_ref, wpw_ref, bpw_ref, y_ref, cp_ref, pad_ref, planes_ref):
        i = pl.program_id(1)
        _fill_padded(pad_ref, _glu(g_ref[0]), _glu(gp_ref[0]), _glu(gn_ref[0]), i == 0, i == nt - 1)
        _shift_planes(pad_ref, planes_ref)
        for ck in range(TC // CH):
            acc = jnp.zeros((CH, DC), F32) + cb_ref[...]
            for t in range(KW):
                acc = acc + _tap_rows(pad_ref, planes_ref, 1 + t, ck * CH, CH) * cw_ref[t:t + 1, :]
            y_ref[0, pl.ds(ck * CH, CH), :] = acc
        y = y_ref[0]
        mu = jnp.mean(y, axis=-1, keepdims=True)
        yc = y - mu
        var = jnp.mean(yc * yc, axis=-1, keepdims=True)
        z = yc * lax.rsqrt(var + EPS) * lw_ref[...] + lb_ref[...]
        act = z * _sigmoid(z)
        cp_ref[0] = _dot(act.astype(BF16), wpw_ref[...]) + bpw_ref[...]

    def const(shape):
        return pl.BlockSpec(shape, lambda b, i: (0,) * len(shape))

    main, prev, nxt = _halo_specs(2 * DC, s)
    tile = pl.BlockSpec((1, TC, DC), lambda b, i: (b, i, 0))
    return pl.pallas_call(
        body, name="conv_fwd", grid=(bl, nt),
        in_specs=[main, prev, nxt, const((32, DC)), const((1, DC)), const((1, DC)), const((1, DC)),
                  const((DC, DC)), const((1, DC))],
        out_specs=[tile, tile],
        out_shape=[jax.ShapeDtypeStruct((bl, s, DC), F32)] * 2,
        scratch_shapes=[pltpu.VMEM((TC + 2 * HALO, DC), F32), pltpu.VMEM((7, PLANE_ROWS, DC), F32)],
        compiler_params=_params(("arbitrary", "arbitrary")),
    )(glu, glu, glu, conv_w, conv_b, ln_w, ln_b, w_pw_b, b_pw)


def _conv_bwd_depthwise(glu, dy, conv_w):
    bl, s, _ = glu.shape
    nt = s // TC

    def body(g_ref, gp_ref, gn_ref, d_ref, dp_ref, dn_ref, cw_ref, dglu_ref, dcw_ref,
             padu_ref, padd_ref, planes_u, planes_d):
        i = pl.program_id(1)

        @pl.when((pl.program_id(0) == 0) & (i == 0))
        def _():
            dcw_ref[...] = jnp.zeros_like(dcw_ref)

        first, last = i == 0, i == nt - 1
        _fill_padded(padu_ref, _glu(g_ref[0]), _glu(gp_ref[0]), _glu(gn_ref[0]), first, last)
        _fill_padded(padd_ref, d_ref[0], dp_ref[0], dn_ref[0], first, last)
        _shift_planes(padu_ref, planes_u)
        _shift_planes(padd_ref, planes_d)
        for ck in range(TC // CH):
            acc = jnp.zeros((CH, DC), F32)
            for t in range(KW):
                acc = acc + _tap_rows(padd_ref, planes_d, 2 * HALO - 1 - t, ck * CH, CH) * cw_ref[t:t + 1, :]
            g = g_ref[0, pl.ds(ck * CH, CH), :]
            a = g[:, 0:DC]
            sg = _sigmoid(g[:, DC:2 * DC])
            dglu_ref[0, pl.ds(ck * CH, CH), 0:DC] = (acc * sg).astype(BF16)
            dglu_ref[0, pl.ds(ck * CH, CH), DC:2 * DC] = (acc * a * sg * (1.0 - sg)).astype(BF16)
        group = 4
        for t0 in range(0, KW, group):
            taps = range(t0, min(t0 + group, KW))
            acc8 = [jnp.zeros((8, DC), F32) for _ in taps]
            for ck in range(TC // CH):
                dchunk = d_ref[0, pl.ds(ck * CH, CH), :]
                for n, t in enumerate(taps):
                    prod = _tap_rows(padu_ref, planes_u, 1 + t, ck * CH, CH) * dchunk
                    acc8[n] = acc8[n] + jnp.sum(prod.reshape(CH // 8, 8, DC), axis=0)
            for n, t in enumerate(taps):
                dcw_ref[t:t + 1, :] += jnp.sum(acc8[n], axis=0, keepdims=True)

    gmain, gprev, gnext = _halo_specs(2 * DC, s)
    dmain, dprev, dnext = _halo_specs(DC, s)
    cw = pl.BlockSpec((32, DC), lambda b, i: (0, 0))
    return pl.pallas_call(
        body, name="conv_bwd_depthwise", grid=(bl, nt),
        in_specs=[gmain, gprev, gnext, dmain, dprev, dnext, cw],
        out_specs=[gmain, cw],
        out_shape=[jax.ShapeDtypeStruct((bl, s, 2 * DC), BF16), jax.ShapeDtypeStruct((32, DC), F32)],
        scratch_shapes=[pltpu.VMEM((TC + 2 * HALO, DC), F32)] * 2 + [pltpu.VMEM((7, PLANE_ROWS, DC), F32)] * 2,
        compiler_params=_params(("arbitrary", "arbitrary")),
    )(glu, glu, glu, dy, dy, dy, conv_w)


def _out_fwd_bwd(attn, za, cp, zc, x, target, modrows, w_out_b, y_conv, ln_w, ln_b, w_pw_b):
    bl, s, _ = x.shape
    tm = OUT_TM

    def body(o_ref, za_ref, cp_ref, zc_ref, x_ref, t_ref, mod_ref, w_ref, y_ref, lw_ref, lb_ref, wpw_ref,
             do_ref, dza_ref, dy_ref, dzc_ref, dh_ref, dgate_ref, gwb_ref, loss_ref, gpwb_ref, rows_ref,
             gw_ref, gpw_ref):
        b, i = pl.program_id(0), pl.program_id(1)

        @pl.when((b == 0) & (i == 0))
        def _():
            gw_ref[...] = jnp.zeros_like(gw_ref)
            gpw_ref[...] = jnp.zeros_like(gpw_ref)
            rows_ref[...] = jnp.zeros_like(rows_ref)
            loss_ref[...] = jnp.zeros_like(loss_ref)

        @pl.when(i == 0)
        def _():
            dgate_ref[...] = jnp.zeros_like(dgate_ref)

        gate = mod_ref[0, 2:3, :]
        w = w_ref[...]
        o, za_v, cp_v, zc_v = o_ref[0], za_ref[0], cp_ref[0], zc_ref[0]
        sa = _sigmoid(za_v)
        sc = _sigmoid(zc_v)
        silu_a = za_v * sa
        silu_c = zc_v * sc
        mix = jnp.concatenate([(o * silu_a).astype(BF16), (cp_v * silu_c).astype(BF16)], axis=-1)
        out = _dot(mix, w)
        err = x_ref[0] + gate * out - t_ref[0]
        loss_ref[...] += jnp.sum(err * err, axis=0, keepdims=True)
        dh = err * (1.0 / D)
        dh_ref[0] = dh
        dgate_ref[0] += jnp.sum(dh * out, axis=0, keepdims=True)
        dout = (dh * gate).astype(BF16)
        gw_ref[...] += _dot_tn(mix, dout)
        dmix = _dot_nt(dout, w)
        dga = dmix[:, 0:DA]
        dgc = dmix[:, DA:DA + DC]
        dov = dga * silu_a
        for h in range(DA // HD):
            do_ref[0, h] = dov[:, h * HD:(h + 1) * HD].astype(BF16)
        dza_ref[0] = (dga * o * (sa * (1.0 + za_v * (1.0 - sa)))).astype(BF16)
        dzc_ref[0] = (dgc * cp_v * (sc * (1.0 + zc_v * (1.0 - sc)))).astype(BF16)
        dcp = dgc * silu_c
        y = y_ref[0]
        yc = y - jnp.mean(y, axis=-1, keepdims=True)
        rstd = lax.rsqrt(jnp.mean(yc * yc, axis=-1, keepdims=True) + EPS)
        yn = yc * rstd
        lw = lw_ref[...]
        z = yn * lw + lb_ref[...]
        sg = _sigmoid(z)
        dcp_b = dcp.astype(BF16)
        gpw_ref[...] += _dot_tn((z * sg).astype(BF16), dcp_b)
        dz = _dot_nt(dcp_b, wpw_ref[...]) * (sg * (1.0 + z * (1.0 - sg)))
        dyn = dz * lw
        dy = rstd * (dyn - jnp.mean(dyn, axis=-1, keepdims=True) - yn * jnp.mean(dyn * yn, axis=-1, keepdims=True))
        dy_ref[0] = dy
        rows_ref[0:1, :] += jnp.sum(dcp, axis=0, keepdims=True)
        rows_ref[1:2, :] += jnp.sum(dz * yn, axis=0, keepdims=True)
        rows_ref[2:3, :] += jnp.sum(dz, axis=0, keepdims=True)
        rows_ref[3:4, :] += jnp.sum(dy, axis=0, keepdims=True)

        @pl.when((b == bl - 1) & (i == s // tm - 1))
        def _():
            gwb_ref[...] = gw_ref[...].astype(BF16)
            gpwb_ref[...] = gpw_ref[...].astype(BF16)

    def const(shape):
        return pl.BlockSpec(shape, lambda b, i: (0,) * len(shape))

    def tile(w):
        return pl.BlockSpec((1, tm, w), lambda b, i: (b, i, 0))

    return pl.pallas_call(
        body, name="out_fwd_bwd", grid=(bl, s // tm),
        in_specs=[tile(DA), tile(DA), tile(DC), tile(DC), tile(D), tile(D),
                  pl.BlockSpec((1, 3, D), lambda b, i: (b, 0, 0)), const((D, D)),
                  tile(DC), const((1, DC)), const((1, DC)), const((DC, DC))],
        out_specs=[pl.BlockSpec((1, DA // HD, tm, HD), lambda b, i: (b, 0, i, 0)), tile(DA), tile(DC), tile(DC), tile(D),
                   pl.BlockSpec((1, 1, D), lambda b, i: (b, 0, 0)), const((D, D)), const((1, D)),
                   const((DC, DC)), const((8, DC))],
        out_shape=[jax.ShapeDtypeStruct((bl, DA // HD, s, HD), BF16), jax.ShapeDtypeStruct((bl, s, DA), BF16),
                   jax.ShapeDtypeStruct((bl, s, DC), F32), jax.ShapeDtypeStruct((bl, s, DC), BF16),
                   jax.ShapeDtypeStruct((bl, s, D), F32), jax.ShapeDtypeStruct((bl, 1, D), F32),
                   jax.ShapeDtypeStruct((D, D), BF16), jax.ShapeDtypeStruct((1, D), F32),
                   jax.ShapeDtypeStruct((DC, DC), BF16), jax.ShapeDtypeStruct((8, DC), F32)],
        scratch_shapes=[pltpu.VMEM((D, D), F32), pltpu.VMEM((DC, DC), F32)],
        compiler_params=_params(("arbitrary", "arbitrary")),
    )(attn, za, cp, zc, x, target, modrows, w_out_b, y_conv, ln_w, ln_b, w_pw_b)


def _rms_heads_bwd(dy, x, w_t, ones_bd):
    r = lax.rsqrt(_segsum(x * x, ones_bd) * (1.0 / HD) + EPS)
    xh = x * r
    g = dy * w_t
    dx = r * (g - xh * (_segsum(g * xh, ones_bd) * (1.0 / HD)))
    return dx, dy * xh


def _ctx_bwd(ctx, modc, norm_w, w_kv_b, pkv_c, dk_c, dv_c, knw_t, ones_bd):
    bl, cl, _ = ctx.shape

    def body(x_ref, mod_ref, nw_ref, w_ref, p_ref, dk_ref, dv_ref, knw_ref, bd_ref, gw_ref, rows_ref, dknw_ref):
        @pl.when(pl.program_id(0) == 0)
        def _():
            gw_ref[...] = jnp.zeros_like(gw_ref)
            rows_ref[...] = jnp.zeros_like(rows_ref)
            dknw_ref[...] = jnp.zeros_like(dknw_ref)

        xv = x_ref[0]
        shift = mod_ref[0, 0:1, :]
        scale = mod_ref[0, 1:2, :]
        nw = nw_ref[...]
        r = lax.rsqrt(jnp.mean(xv * xv, axis=-1, keepdims=True) + EPS)
        xn = xv * r
        yv = xn * nw
        u = yv * (1.0 + scale) + shift
        dkv = jnp.concatenate([dk_ref[0, 0], dk_ref[0, 1]], axis=-1)
        dpk, dknw = _rms_heads_bwd(dkv, p_ref[0][:, 0:KVW], knw_ref[...], bd_ref[...])
        dp = jnp.concatenate([dpk.astype(BF16), dv_ref[0, 0].astype(BF16), dv_ref[0, 1].astype(BF16)], axis=-1)
        gw_ref[...] += _dot_tn(dp, u.astype(BF16))
        du = _dot(dp, w_ref[...])
        rows_ref[0:1, :] += jnp.sum(du, axis=0, keepdims=True)
        rows_ref[1:2, :] += jnp.sum(du * yv, axis=0, keepdims=True)
        rows_ref[2:3, :] += jnp.sum(du * (1.0 + scale) * xn, axis=0, keepdims=True)
        dknw_ref[...] += jnp.sum(dknw, axis=0, keepdims=True)

    def const(shape):
        return pl.BlockSpec(shape, lambda b: (0,) * len(shape))

    def tile(w):
        return pl.BlockSpec((1, cl, w), lambda b: (b, 0, 0))

    ctx_block = (dk_c.shape[2] - cl) // cl
    kv_tile = pl.BlockSpec((1, KVW // HD, cl, HD), lambda b: (b, 0, ctx_block, 0))
    return pl.pallas_call(
        body, name="ctx_bwd", grid=(bl,),
        in_specs=[tile(D), const((1, 3, D)), const((1, D)), _KV_ROWS_OF_W_IN_T, tile(2 * KVW), kv_tile, kv_tile,
                  const((1, KVW)), const((KVW, KVW))],
        out_specs=[const((2 * KVW, D)), const((8, D)), const((1, KVW))],
        out_shape=[jax.ShapeDtypeStruct((2 * KVW, D), F32), jax.ShapeDtypeStruct((8, D), F32),
                   jax.ShapeDtypeStruct((1, KVW), F32)],
        compiler_params=_params(("arbitrary",)),
    )(ctx, modc, norm_w, w_kv_b, pkv_c, dk_c, dv_c, knw_t, ones_bd)


def _bwd_in(x, modrows, norm_w, w_in_b, cos, sins, qnw_t, knw_t, ones_bd,
            pq, pkv, dq, dk, dv, dza, dglu, dzc, dh, gw_kv):
    bl, s, _ = x.shape
    tm = TOKEN_PARTS * TM
    nt = s // tm

    def body(x_ref, mod_ref, nw_ref, win_hbm, cos_ref, sin_ref, qnw_ref, knw_ref, bd_ref,
             pq_ref, pkv_ref, dq_ref, dk_ref, dv_ref, dza_ref, dglu_ref, dzc_ref, dh_ref, gwkv_ref,
             gx_ref, gw_hbm, dmod_ref, dnw_ref, dqnw_ref, dknw_ref, win_ref, gw_acc, sem):
        b, i = pl.program_id(0), pl.program_id(1)

        @pl.when((b == 0) & (i == 0))
        def _():
            cp = pltpu.make_async_copy(win_hbm, win_ref, sem)
            cp.start()
            gw_acc[...] = jnp.zeros_like(gw_acc)
            dnw_ref[...] = jnp.zeros_like(dnw_ref)
            dqnw_ref[...] = jnp.zeros_like(dqnw_ref)
            dknw_ref[...] = jnp.zeros_like(dknw_ref)
            cp.wait()

        @pl.when(i == 0)
        def _():
            dmod_ref[...] = jnp.zeros_like(dmod_ref)

        bd = bd_ref[...]
        shift = mod_ref[0, 0:1, :]
        scale = mod_ref[0, 1:2, :]
        nw = nw_ref[...]
        dps, us = [], []
        for part in range(TOKEN_PARTS):
            rows = pl.ds(part * TM, TM)
            ck = cos_ref[rows, :]
            sk = sin_ref[rows, :]
            cs = jnp.concatenate([ck] * (DA // KVW), axis=-1)
            sn = jnp.concatenate([sk] * (DA // KVW), axis=-1)
            dqn = _rope_bwd(dq_ref[0, rows, :], cs, sn)
            dpq, dqnw = _rms_heads_bwd(dqn, pq_ref[0, rows, :], qnw_ref[...], bd)
            dkn = _rope_bwd(jnp.concatenate([dk_ref[0, 0, rows, :], dk_ref[0, 1, rows, :]], axis=-1), ck, sk)
            dpk, dknw = _rms_heads_bwd(dkn, pkv_ref[0, rows, 0:KVW], knw_ref[...], bd[0:KVW, 0:KVW])
            dqnw_ref[...] += jnp.sum(dqnw, axis=0, keepdims=True)
            dknw_ref[...] += jnp.sum(dknw, axis=0, keepdims=True)
            dp = jnp.concatenate(
                [dpq.astype(BF16), dpk.astype(BF16), dv_ref[0, 0, rows, :].astype(BF16), dv_ref[0, 1, rows, :].astype(BF16),
                 dza_ref[0, rows, :], dglu_ref[0, rows, :], dzc_ref[0, rows, :]], axis=-1)

            xv = x_ref[0, rows, :]
            r = lax.rsqrt(jnp.mean(xv * xv, axis=-1, keepdims=True) + EPS)
            xn = xv * r
            yv = xn * nw
            u = yv * (1.0 + scale) + shift
            dps.append(dp)
            us.append(u.astype(BF16))
            du = _dot(dp, win_ref[...])
            dmod_ref[0, 0:1, :] += jnp.sum(du, axis=0, keepdims=True)
            dmod_ref[0, 1:2, :] += jnp.sum(du * yv, axis=0, keepdims=True)
            dy = du * (1.0 + scale)
            dnw_ref[...] += jnp.sum(dy * xn, axis=0, keepdims=True)
            dxn = dy * nw
            gx_ref[0, rows, :] = dh_ref[0, rows, :] + r * (dxn - xn * jnp.mean(dxn * xn, axis=-1, keepdims=True))
        gw_acc[...] += _dot_tn(jnp.concatenate(dps, axis=0), jnp.concatenate(us, axis=0))

        @pl.when((b == bl - 1) & (i == nt - 1))
        def _():
            gw_acc[DA:DA + 2 * KVW, :] += gwkv_ref[...]

            def to_bf16(j, carry):
                rows = pl.ds(pl.multiple_of(j * 2 * KVW, 2 * KVW), 2 * KVW)
                win_ref[rows, :] = gw_acc[rows, :].astype(BF16)
                return carry

            lax.fori_loop(0, D_IN // (2 * KVW), to_bf16, 0)
            pltpu.sync_copy(win_ref, gw_hbm)

    def tile(w):
        return pl.BlockSpec((1, tm, w), lambda b, i: (b, i, 0))

    def const(shape):
        return pl.BlockSpec(shape, lambda b, i: (0,) * len(shape))

    anyspace = pl.BlockSpec(memory_space=pl.ANY)
    rope = pl.BlockSpec((tm, KVW), lambda b, i: (i, 0))
    kv_tile = pl.BlockSpec((1, KVW // HD, tm, HD), lambda b, i: (b, 0, i, 0))
    return pl.pallas_call(
        body, name="bwd_in", grid=(bl, nt),
        in_specs=[tile(D), pl.BlockSpec((1, 3, D), lambda b, i: (b, 0, 0)), const((1, D)), anyspace, rope, rope,
                  const((1, DA)), const((1, KVW)), const((DA, DA)),
                  tile(DA), tile(2 * KVW), tile(DA), kv_tile, kv_tile, tile(DA), tile(2 * DC), tile(DC), tile(D),
                  const((2 * KVW, D))],
        out_specs=[tile(D), anyspace, pl.BlockSpec((1, 2, D), lambda b, i: (b, 0, 0)), const((1, D)),
                   const((1, DA)), const((1, KVW))],
        out_shape=[jax.ShapeDtypeStruct((bl, s, D), F32), jax.ShapeDtypeStruct((D_IN, D), BF16),
                   jax.ShapeDtypeStruct((bl, 2, D), F32), jax.ShapeDtypeStruct((1, D), F32),
                   jax.ShapeDtypeStruct((1, DA), F32), jax.ShapeDtypeStruct((1, KVW), F32)],
        scratch_shapes=[pltpu.VMEM((D_IN, D), BF16), pltpu.VMEM((D_IN, D), F32), pltpu.SemaphoreType.DMA],
        compiler_params=_params(("arbitrary", "arbitrary")),
    )(x, modrows, norm_w, w_in_b, cos, sins, qnw_t, knw_t, ones_bd,
      pq, pkv, dq, dk, dv, dza, dglu, dzc, dh, gw_kv)


_LOSS, _DMODC, _NW, _QN, _KN, _CB, _LW, _LB, _BPW, SMALL_W = 0, 1024, 4096, 5120, 5248, 5376, 5888, 6400, 6912, 7424


ROW_W = 1792


def _put_flat(ref, off, value):
    n, done = value.shape[1], 0
    while done < n:
        r, c = divmod(off + done, ROW_W)
        take = min(n - done, ROW_W - c)
        ref[r:r + 1, c:c + take] = value[:, done:done + take]
        done += take


def _get_flat(arr, off, n):
    parts, done = [], 0
    while done < n:
        r, c = divmod(off + done, ROW_W)
        take = min(n - done, ROW_W - c)
        parts.append(arr[r:r + 1, c:c + take])
        done += take
    return parts[0] if len(parts) == 1 else jnp.concatenate(parts, axis=-1)


def _pack_small_body(loss_ref, ctx_ref, dnw_ref, dqnw_ref, dknw_ref, dknwc_ref, conv_ref, dss_ref, dgate_ref, o_ref):
    bl = dss_ref.shape[0]
    assert SMALL_W + bl * 3 * D <= 8 * ROW_W
    o_ref[...] = jnp.zeros_like(o_ref)
    _put_flat(o_ref, _LOSS, loss_ref[...])
    _put_flat(o_ref, _DMODC, ctx_ref[0:1, :])
    _put_flat(o_ref, _DMODC + D, ctx_ref[1:2, :])
    _put_flat(o_ref, _NW, dnw_ref[...] + ctx_ref[2:3, :])
    dq = dqnw_ref[...]
    qn = dq[:, 0:HD]
    for h in range(1, DA // HD):
        qn = qn + dq[:, h * HD:(h + 1) * HD]
    _put_flat(o_ref, _QN, qn)
    dk = dknw_ref[...] + dknwc_ref[...]
    _put_flat(o_ref, _KN, dk[:, 0:HD] + dk[:, HD:2 * HD])
    _put_flat(o_ref, _BPW, conv_ref[0:1, :])
    _put_flat(o_ref, _LW, conv_ref[1:2, :])
    _put_flat(o_ref, _LB, conv_ref[2:3, :])
    _put_flat(o_ref, _CB, conv_ref[3:4, :])
    for b in range(bl):
        _put_flat(o_ref, SMALL_W + b * 3 * D, dss_ref[b, 0:1, :])
        _put_flat(o_ref, SMALL_W + b * 3 * D + D, dss_ref[b, 1:2, :])
        _put_flat(o_ref, SMALL_W + b * 3 * D + 2 * D, dgate_ref[b])


_SMALL = (("b_mod", None), ("norm_w", _NW), ("q_norm_w", _QN), ("k_norm_w", _KN), ("conv_b", _CB),
          ("conv_ln_w", _LW), ("conv_ln_b", _LB), ("b_pw", _BPW), ("c_ctx", None))


def _epilogue(parts_in, pieces, c_rows, w_mod_loc):
    bl = pieces[7].shape[0]
    n_ex = N_DEV * bl
    n_mod = w_mod_loc.shape[1]
    rb = 32
    shp = parts_in.shape[1:]
    rows_in = shp[0]

    def body(*refs):
        it = iter(refs)
        take = lambda k: [next(it) for _ in range(k)]
        (parts,) = take(1)
        piece_refs = take(9)
        (c_ref, wm_ref) = take(2)
        (g_in, g_wm, sum_ref, gb_ref, gc_all, loss_ref) = take(6)
        (mine, got_sib, stage, got_chip, payload, gathered, dmod_full, gc_mine) = take(8)
        (d2d_send, d2d_recv, ici_send, ici_recv, local_sems, sg_send, sg_recv, gc_send, gc_recv, misc_sems) = take(10)

        x, y, c = _coords()
        me = _lin(x, y, c)
        sib = (x, y, 1 - c)
        home = 2 * x + y

        def rows_loop(fn):
            def step(i, carry):
                fn(pl.ds(pl.multiple_of(i * rb, rb), rb))
                return carry
            lax.fori_loop(0, rows_in // rb, step, 0)

        def direct_gather(src, dst, send_sems, recv_sems, local_sem):
            cps = [pltpu.make_async_copy(src, dst.at[me], local_sem)]
            for k in range(1, N_DEV):
                peer = (1 - x if k & 4 else x, 1 - y if k & 2 else y, 1 - c if k & 1 else c)
                cps.append(pltpu.make_async_remote_copy(
                    src_ref=src, dst_ref=dst.at[me], send_sem=send_sems.at[k - 1], recv_sem=recv_sems.at[k - 1],
                    device_id=peer, device_id_type=MESH_ID))
            for cp in cps:
                cp.start()
            return cps

        _pack_small_body(*piece_refs, payload)
        small_cps = direct_gather(payload, gathered, sg_send, sg_recv, misc_sems.at[0])

        local, d2d, ici = [], [], []
        for s in range(4):
            cp = pltpu.make_async_copy(parts.at[_lin(s // 2, s % 2, c)], mine.at[s], local_sems.at[s])
            cp.start()
            local.append(cp)
            rc = pltpu.make_async_remote_copy(
                src_ref=parts.at[_lin(s // 2, s % 2, 1 - c)], dst_ref=got_sib.at[s],
                send_sem=d2d_send.at[s], recv_sem=d2d_recv.at[s], device_id=sib, device_id_type=MESH_ID)
            rc.start()
            d2d.append(rc)

        for cp in small_cps[1:]:
            cp.wait_recv()
        small_cps[0].wait()
        tot = gathered[0]
        for j in range(1, N_DEV):
            tot = tot + gathered[j]
        summed = _get_flat(tot, 0, SMALL_W)
        dmod_full[...] = jnp.zeros_like(dmod_full)
        for j in range(N_DEV):
            arr = gathered[j]
            for b in range(bl):
                dmod_full[j * bl + b:j * bl + b + 1, :] = _get_flat(arr, SMALL_W + b * 3 * D, 3 * D)
        dmod_full[n_ex:n_ex + 1, :] = summed[:, _DMODC:_DMODC + 3 * D]
        sum_ref[...] = summed
        gb_ref[...] = jnp.sum(dmod_full[...], axis=0, keepdims=True)
        loss_ref[...] = (0.5 / D) * jnp.sum(summed[:, _LOSS:_LOSS + D], axis=-1, keepdims=True)

        north = c == 1
        first = (jnp.where(north, 1 - x, x), jnp.where(north, y, 1 - y))
        second = (jnp.where(north, x, 1 - x), jnp.where(north, 1 - y, y))
        for s in range(4):
            local[s].wait()
            d2d[s].wait_recv()

        def chip_sum(k, chip, relayed):
            slot = 2 * chip[0] + chip[1]

            def pair_sum(rs):
                acc = mine[slot, rs, :].astype(F32) + got_sib[slot, rs, :].astype(F32)
                if relayed:
                    acc = acc + got_chip[1, rs, :].astype(F32)
                stage[k, rs, :] = acc.astype(BF16)

            rows_loop(pair_sum)

        def send(k, to):
            rc = pltpu.make_async_remote_copy(
                src_ref=stage.at[k], dst_ref=got_chip.at[k], send_sem=ici_send.at[k], recv_sem=ici_recv.at[k],
                device_id=(to[0], to[1], c), device_id_type=MESH_ID)
            rc.start()
            ici.append(rc)

        chip_sum(0, first, False)
        send(0, first)
        chip_sum(1, (1 - x, 1 - y), False)
        send(1, first)

        cr = c_ref[...]
        act = (cr * _sigmoid(cr)).astype(BF16)
        dm = dmod_full[:, pl.ds(pl.multiple_of(me * n_mod, 128), n_mod)].astype(BF16)
        g_wm[...] = _dot_tn(act, dm)
        gc_mine[...] = _dot_nt(dm[n_ex:n_ex + 8, :], wm_ref[...].astype(BF16))
        gc_cps = direct_gather(gc_mine, gc_all, gc_send, gc_recv, misc_sems.at[1])

        ici[1].wait_recv()
        chip_sum(2, second, True)
        send(2, second)
        ici[0].wait_recv()
        ici[2].wait_recv()

        def finish(rs):
            gsum = mine[home, rs, :].astype(F32) + got_sib[home, rs, :].astype(F32)
            g_in[rs, :] = gsum + got_chip[0, rs, :].astype(F32) + got_chip[2, rs, :].astype(F32)

        rows_loop(finish)

        for cp in gc_cps[1:]:
            cp.wait_recv()
        gc_cps[0].wait()
        for rc in d2d + ici + small_cps[1:] + gc_cps[1:]:
            rc.wait_send()

    vm = pl.BlockSpec(memory_space=pltpu.VMEM)
    anyspace = pl.BlockSpec(memory_space=pl.ANY)
    assert rows_in % rb == 0 and parts_in.dtype == BF16
    args = [parts_in, *pieces, c_rows, w_mod_loc]
    in_specs = [anyspace] + [vm] * (len(args) - 1)
    out_shape = [jax.ShapeDtypeStruct(shp, F32), jax.ShapeDtypeStruct(w_mod_loc.shape, F32),
                 jax.ShapeDtypeStruct((1, SMALL_W), F32), jax.ShapeDtypeStruct((1, 3 * D), F32),
                 jax.ShapeDtypeStruct((N_DEV, 8, D), F32), jax.ShapeDtypeStruct((1, 1), F32)]
    scratch = [pltpu.VMEM((4,) + shp, BF16), pltpu.VMEM((4,) + shp, BF16), pltpu.VMEM((3,) + shp, BF16),
               pltpu.VMEM((3,) + shp, BF16), pltpu.VMEM((8, ROW_W), F32), pltpu.VMEM((N_DEV, 8, ROW_W), F32),
               pltpu.VMEM((n_ex + 8, 3 * D), F32), pltpu.VMEM((8, D), F32),
               pltpu.SemaphoreType.DMA((4,)), pltpu.SemaphoreType.DMA((4,)), pltpu.SemaphoreType.DMA((3,)),
               pltpu.SemaphoreType.DMA((3,)), pltpu.SemaphoreType.DMA((4,)),
               pltpu.SemaphoreType.DMA((N_DEV - 1,)), pltpu.SemaphoreType.DMA((N_DEV - 1,)),
               pltpu.SemaphoreType.DMA((N_DEV - 1,)), pltpu.SemaphoreType.DMA((N_DEV - 1,)),
               pltpu.SemaphoreType.DMA((2,))]
    return pl.pallas_call(
        body, name="epilogue", out_shape=out_shape, in_specs=in_specs, out_specs=[vm] * len(out_shape),
        scratch_shapes=scratch, compiler_params=pltpu.CompilerParams(vmem_limit_bytes=VMEM_LIMIT),
    )(*args)


def _final_adamw(g_in, w_in_t, m_in_t, v_in_t, g_wm, w_mod_loc, m_mod, v_mod, summed, g_bmod, gc_all,
                 small_w, small_m, small_v):
    ns = len(_SMALL)

    def body(*refs):
        it = iter(refs)
        take = lambda k: [next(it) for _ in range(k)]
        (gin_ref, w_ref, m_ref, v_ref, gwm_ref, wm_ref, mm_ref, vm_ref, sum_ref, gb_ref, gc_ref) = take(11)
        sw, sm, sv = take(ns), take(ns), take(ns)
        (d_in, nm_in, nv_in, d_wm, nm_wm, nv_wm) = take(6)
        souts = take(4 * ns)

        for g_r, w_r, m_r, v_r, outs3 in ((gin_ref, w_ref, m_ref, v_ref, (d_in, nm_in, nv_in)),
                                          (gwm_ref, wm_ref, mm_ref, vm_ref, (d_wm, nm_wm, nv_wm))):
            for o_r, val in zip(outs3, _adamw(w_r[...], g_r[...], m_r[...], v_r[...])):
                o_r[...] = val

        @pl.when(pl.program_id(0) == 0)
        def _():
            for k, (name, off) in enumerate(_SMALL):
                w = sw[k][...]
                if name == "b_mod":
                    gk = gb_ref[...]
                elif name == "c_ctx":
                    acc = gc_ref[0, 0:1, :]
                    for j in range(1, N_DEV):
                        acc = acc + gc_ref[j, 0:1, :]
                    sg = _sigmoid(w)
                    gk = acc * (sg * (1.0 + w * (1.0 - sg)))
                else:
                    gk = sum_ref[:, off:off + w.shape[1]]
                dl, m_new, v_new = _adamw(w, gk, sm[k][...], sv[k][...])
                souts[k][...] = gk
                souts[ns + k][...] = dl
                souts[2 * ns + k][...] = m_new
                souts[3 * ns + k][...] = v_new

    rows_in, rows_mod = w_in_t.shape[0] // ADAM_STEPS, w_mod_loc.shape[0] // ADAM_STEPS
    assert rows_in % 8 == 0 and rows_mod % 8 == 0

    def whole(a):
        return pl.BlockSpec(a.shape, lambda i: (0,) * a.ndim)

    in_tile = pl.BlockSpec((rows_in, w_in_t.shape[1]), lambda i: (i, 0))
    mod_tile = pl.BlockSpec((rows_mod, w_mod_loc.shape[1]), lambda i: (i, 0))
    small_in = [summed, g_bmod, gc_all, *small_w, *small_m, *small_v]
    big_shape = jax.ShapeDtypeStruct(w_in_t.shape, F32)
    mod_shape = jax.ShapeDtypeStruct(w_mod_loc.shape, F32)
    out_shape = [big_shape] * 3 + [mod_shape] * 3 + [jax.ShapeDtypeStruct(w.shape, F32) for w in small_w] * 4
    outs = pl.pallas_call(
        body, name="final_adamw", grid=(ADAM_STEPS,), out_shape=out_shape,
        in_specs=[in_tile] * 4 + [mod_tile] * 4 + [whole(a) for a in small_in],
        out_specs=[in_tile] * 3 + [mod_tile] * 3 + [whole(w) for w in small_w] * 4,
        compiler_params=_params(("arbitrary",)),
    )(g_in, w_in_t, m_in_t, v_in_t, g_wm, w_mod_loc, m_mod, v_mod, *small_in)
    small_outs = [outs[6 + k * ns:6 + (k + 1) * ns] for k in range(4)]
    return (g_in,) + tuple(outs[0:3]), (g_wm,) + tuple(outs[3:6]), small_outs


def _rope_tables(s):
    t = jnp.arange(s, dtype=jnp.int32)
    row = (t // GRID_W).astype(F32)
    col = (t % GRID_W).astype(F32)
    freqs = ROPE_THETA ** (-jnp.arange(0, HD // 2, 2, dtype=F32) / (HD // 2))
    ang_r = row[:, None] * freqs[None, :]
    ang_c = col[:, None] * freqs[None, :]
    cr, sr, cc, sc = jnp.cos(ang_r), jnp.sin(ang_r), jnp.cos(ang_c), jnp.sin(ang_c)
    cos = jnp.concatenate([cr, cr, cc, cc], axis=-1)
    sins = jnp.concatenate([-sr, sr, -sc, sc], axis=-1)
    return jnp.tile(cos, (1, KVW // HD)), jnp.tile(sins, (1, KVW // HD))


def kernel(x, c, ctx, c_ctx, w_mod, b_mod, norm_w, w_in, q_norm_w, k_norm_w, conv_w, conv_b, conv_ln_w, conv_ln_b, w_pw, b_pw, w_out, loss_target, m_c_ctx, m_w_mod, m_b_mod, m_norm_w, m_w_in, m_q_norm_w, m_k_norm_w, m_conv_w, m_conv_b, m_conv_ln_w, m_conv_ln_b, m_w_pw, m_b_pw, m_w_out, v_c_ctx, v_w_mod, v_b_mod, v_norm_w, v_w_in, v_q_norm_w, v_k_norm_w, v_conv_w, v_conv_b, v_conv_ln_w, v_conv_ln_b, v_w_pw, v_b_pw, v_w_out):
    bl, s, _ = x.shape
    cl = ctx.shape[1]
    me = _lin(*_coords())

    conv_w_pad = jnp.pad(conv_w[0], ((0, 32 - KW), (0, 0)))
    n_ex = N_DEV * bl
    g_win, c_rows, g_mod = _prologue(w_in[0].T, c, c_ctx[None, :], w_mod[0], b_mod)
    w_in_b = g_win.reshape(D_IN, D)
    mod_all = g_mod.transpose(1, 0, 2).reshape(n_ex + 8, 3 * D)
    modrows = lax.dynamic_slice_in_dim(mod_all, me * bl, bl, axis=0).reshape(bl, 3, D)
    modc = mod_all[n_ex].reshape(1, 3, D)

    cos, sins = _rope_tables(s)
    qnw_t = jnp.tile(q_norm_w, (1, DA // HD))
    knw_t = jnp.tile(k_norm_w, (1, KVW // HD))
    lane = jnp.arange(DA, dtype=jnp.int32) // HD
    ones_bd = (lane[:, None] == lane[None, :]).astype(BF16)
    ones_kv = ones_bd[0:KVW, 0:KVW]
    w_kv_b = w_in_b

    k_ctx, v_ctx, pkv_c = _ctx_fwd(ctx, modc, norm_w, w_kv_b, knw_t, ones_kv, cl + s)
    (q_h, k_h, v_h, pq, pkv, za, glu, zc), (g_wout, g_wpw, g_cw) = _fwd_in(
        x, modrows, norm_w, w_in_b, cos, sins, qnw_t, knw_t, ones_bd, k_ctx, v_ctx,
        [w_out[0], w_pw[0], conv_w_pad], [BF16, BF16, F32])
    w_out_b = g_wout.reshape(D, D)
    w_pw_b = g_wpw.reshape(DC, DC)
    conv_w_full = g_cw.transpose(1, 0, 2).reshape(32, DC)
    attn, lse = _attn_fwd(q_h, k_h, v_h)
    y_conv, cp = _conv_fwd(glu, conv_w_full, conv_b, conv_ln_w, conv_ln_b, w_pw_b, b_pw)

    do_h, dza, dy_conv, dzc, dh, dgate, gw_out, loss_row, gw_pw, conv_rows = _out_fwd_bwd(
        attn, za, cp, zc, x, loss_target, modrows, w_out_b, y_conv, conv_ln_w, conv_ln_b, w_pw_b)
    dglu, g_cw_full = _conv_bwd_depthwise(glu, dy_conv, conv_w_full)
    parts_out = gw_out.reshape(N_DEV, D // N_DEV, D)
    parts_pw = gw_pw.reshape(N_DEV, DC // N_DEV, DC)
    parts_cw = g_cw_full.astype(BF16).reshape(32, N_DEV, DC // N_DEV).transpose(1, 0, 2)
    (dq, dk_h, dv_h), (got_out, got_pw, got_cw) = _attn_bwd(
        q_h, k_h, v_h, do_h, attn, lse, [parts_out, parts_pw, parts_cw])
    gw_kv, ctx_rows, dknw_c = _ctx_bwd(ctx, modc, norm_w, w_kv_b, pkv_c, dk_h, dv_h, knw_t, ones_kv)
    grad_x, gw_in, dmod_ss, dnw, dqnw, dknw = _bwd_in(
        x, modrows, norm_w, w_in_b, cos, sins, qnw_t, knw_t, ones_bd,
        pq, pkv, dq, dk_h, dv_h, dza, dglu, dzc, dh, gw_kv)

    r_out, r_pw, r_cw = _sum_devices_adamw(
        [(got_out, w_out[0], m_w_out[0], v_w_out[0]), (got_pw, w_pw[0], m_w_pw[0], v_w_pw[0]),
         (got_cw, conv_w[0], m_conv_w[0], v_conv_w[0])])

    given = {"c_ctx": (c_ctx, m_c_ctx, v_c_ctx), "b_mod": (b_mod, m_b_mod, v_b_mod), "norm_w": (norm_w, m_norm_w, v_norm_w),
             "q_norm_w": (q_norm_w, m_q_norm_w, v_q_norm_w), "k_norm_w": (k_norm_w, m_k_norm_w, v_k_norm_w),
             "conv_b": (conv_b, m_conv_b, v_conv_b), "conv_ln_w": (conv_ln_w, m_conv_ln_w, v_conv_ln_w),
             "conv_ln_b": (conv_ln_b, m_conv_ln_b, v_conv_ln_b), "b_pw": (b_pw, m_b_pw, v_b_pw)}
    as_rows = [[given[name][which].reshape(1, -1) for name, _ in _SMALL] for which in range(3)]
    g_in_t, g_wmod, summed, g_bmod, gc_all, loss11 = _epilogue(
        gw_in.reshape(N_DEV, D_IN // N_DEV, D),
        [loss_row, ctx_rows, dnw, dqnw, dknw, dknw_c, conv_rows, dmod_ss, dgate], c_rows, w_mod[0])
    r_in, r_wmod, small_outs = _final_adamw(
        g_in_t, w_in[0].T, m_w_in[0].T, v_w_in[0].T, g_wmod, w_mod[0], m_w_mod[0], v_w_mod[0],
        summed, g_bmod, gc_all, *as_rows)
    r_in = tuple(a.T for a in r_in)

    big = {"w_mod": r_wmod, "w_in": r_in, "conv_w": r_cw, "w_pw": r_pw, "w_out": r_out}
    order = ["c_ctx", "w_mod", "b_mod", "norm_w", "w_in", "q_norm_w", "k_norm_w", "conv_w", "conv_b", "conv_ln_w",
             "conv_ln_b", "w_pw", "b_pw", "w_out"]
    small_index = {name: k for k, (name, _) in enumerate(_SMALL)}
    outs = [loss11.reshape(()), grad_x]
    for which in range(4):
        for name in order:
            if name in big:
                outs.append(big[name][which][None])
            else:
                outs.append(small_outs[which][small_index[name]].reshape(given[name][0].shape))
    return tuple(outs)
```

```python
import jax
import jax.numpy as jnp
from jax import lax
from jax.experimental import pallas as pl
from jax.experimental.pallas import tpu as pltpu

F32, BF16 = jnp.float32, jnp.bfloat16
MESH_ID = pl.DeviceIdType.MESH

N_DEV = 8
D = 1024
D_IN = 2816
DA = 512
DC = 512
HD = 64
KVW = 128
KW = 31
HALO = 16
EPS = 1e-6
ROPE_THETA = 10000.0
GRID_W = 64

ADAM_LR, ADAM_B1, ADAM_B2, ADAM_EPS, ADAM_WD, ADAM_STEP = 0.001, 0.9, 0.999, 1e-08, 0.01, 10

VMEM_LIMIT = 56 * 1024 * 1024

TM = 256
TQ = 128
TOKEN_PARTS = 2
OUT_TM = 512
BWD_PARTS = 4
FWD_PARTS = 8
TC = 512
CH = 32
ADAM_STEPS = 4


def _params(sem, vmem=VMEM_LIMIT):
    return pltpu.CompilerParams(dimension_semantics=sem, vmem_limit_bytes=vmem)


def _dot(a, b):
    return jnp.dot(a, b, preferred_element_type=F32)


def _dot_nt(a, b):
    return lax.dot_general(a, b, (((1,), (1,)), ((), ())), preferred_element_type=F32)


def _dot_tn(a, b):
    return lax.dot_general(a, b, (((0,), (0,)), ((), ())), preferred_element_type=F32)


def _sigmoid(z):
    return 1.0 / (1.0 + jnp.exp(-z))


def _segsum(v, ones_bd):
    return _dot(v.astype(BF16), ones_bd)


def _swap16(x):
    w = x.shape[-1]
    lane = lax.broadcasted_iota(jnp.int32, x.shape, 1)
    return jnp.where((lane % 32) < 16, pltpu.roll(x, w - 16, 1), pltpu.roll(x, 16, 1))


def _with_ones_column(v):
    one = (lax.broadcasted_iota(jnp.int32, v.shape, 1) == 0).astype(v.dtype)
    return jnp.concatenate([v, one], axis=-1)


def _rope(x, cos, sins):
    return x * cos + _swap16(x) * sins


def _rope_bwd(d, cos, sins):
    return d * cos + _swap16(d * sins)


def _adamw(w, g, m, v):
    m2 = ADAM_B1 * m + (1.0 - ADAM_B1) * g
    v2 = ADAM_B2 * v + (1.0 - ADAM_B2) * (g * g)
    m_hat = m2 / (1.0 - ADAM_B1 ** ADAM_STEP)
    v_hat = v2 / (1.0 - ADAM_B2 ** ADAM_STEP)
    delta = -ADAM_LR * (m_hat / (jnp.sqrt(v_hat) + ADAM_EPS) + ADAM_WD * w)
    return delta, m2, v2


def _coords():
    return lax.axis_index("x"), lax.axis_index("y"), lax.axis_index("c")


def _lin(x, y, c):
    return 4 * x + 2 * y + c


def _prologue(w_in_t, c, c_ctx_row, w_mod_loc, b_mod):
    bl = c.shape[0]
    n_ex = N_DEV * bl
    n_mod = w_mod_loc.shape[1]

    def body(w32_ref, c_in_ref, cctx_ref, wm_ref, b_ref, out_w, crows_ref, mod_out, w_ref, c_ref, c_gath, mod_mine,
             w_send, w_recv, c_send, c_recv, m_send, m_recv, local_sems):
        x, y, c = _coords()
        me_lin = _lin(x, y, c)
        c_ref[...] = jnp.zeros_like(c_ref)
        c_ref[0:bl, :] = c_in_ref[...]
        w_ref[...] = w32_ref[...].astype(BF16)
        me, sib = (x, y, c), (x, y, 1 - c)
        xnb, ynb, diag = (1 - x, y), (x, 1 - y), (1 - x, 1 - y)
        north = c == 1

        def direct_gather(src, dst, send_sems, recv_sems, local_sem):
            cps = [pltpu.make_async_copy(src, dst.at[me_lin], local_sem)]
            for k in range(1, N_DEV):
                peer = (1 - x if k & 4 else x, 1 - y if k & 2 else y, 1 - c if k & 1 else c)
                cps.append(pltpu.make_async_remote_copy(
                    src_ref=src, dst_ref=dst.at[me_lin], send_sem=send_sems.at[k - 1], recv_sem=recv_sems.at[k - 1],
                    device_id=peer, device_id_type=MESH_ID))
            for cp in cps:
                cp.start()
            return cps

        def copy(k, block, to, src=None):
            slot = out_w.at[_lin(*block)]
            return pltpu.make_async_remote_copy(
                src_ref=slot if src is None else src, dst_ref=slot, send_sem=w_send.at[k], recv_sem=w_recv.at[k],
                device_id=to, device_id_type=MESH_ID)

        c_cps = direct_gather(c_ref, c_gath, c_send, c_recv, local_sems.at[0])
        mine = pltpu.make_async_copy(w_ref, out_w.at[me_lin], local_sems.at[1])
        mine.start()
        first = [copy(0, me, sib, src=w_ref), copy(1, me, (*xnb, c), src=w_ref), copy(2, me, (*ynb, c), src=w_ref)]
        for cp in first:
            cp.start()

        for cp in c_cps[1:]:
            cp.wait_recv()
        c_cps[0].wait()
        crows_ref[...] = jnp.zeros_like(crows_ref)
        for j in range(N_DEV):
            crows_ref[j * bl:(j + 1) * bl, :] = c_gath[j, 0:bl, :]
        crows_ref[n_ex:n_ex + 1, :] = cctx_ref[...]
        cr = crows_ref[...]
        act = (cr * _sigmoid(cr)).astype(BF16)
        mod_mine[...] = _dot(act, wm_ref[...].astype(BF16)) + b_ref[:, pl.ds(pl.multiple_of(me_lin * n_mod, 128), n_mod)]
        mod_cps = direct_gather(mod_mine, mod_out, m_send, m_recv, local_sems.at[2])

        relay_north = copy(3, (*xnb, c), (*ynb, c))
        relay_south = copy(3, (*ynb, c), (*xnb, c))
        passed = []
        copy(1, (*xnb, c), me).wait_recv()
        pl.when(north)(relay_north.start)
        passed.append(copy(4, (*xnb, c), sib))
        passed[-1].start()
        copy(2, (*ynb, c), me).wait_recv()
        pl.when(jnp.logical_not(north))(relay_south.start)
        passed.append(copy(5, (*ynb, c), sib))
        passed[-1].start()
        copy(3, (*diag, c), me).wait_recv()
        passed.append(copy(6, (*diag, c), sib))
        passed[-1].start()
        copy(0, sib, me).wait_recv()
        for k, chip in ((4, xnb), (5, ynb), (6, diag)):
            copy(k, (*chip, 1 - c), me).wait_recv()
        for cp in mod_cps[1:]:
            cp.wait_recv()
        mod_cps[0].wait()
        for cp in first + passed + [relay_north] + c_cps[1:] + mod_cps[1:]:
            cp.wait_send()
        mine.wait()

    vm = pl.BlockSpec(memory_space=pltpu.VMEM)
    seven = pltpu.SemaphoreType.DMA((N_DEV - 1,))
    return pl.pallas_call(
        body, name="prologue",
        out_shape=[jax.ShapeDtypeStruct((N_DEV,) + w_in_t.shape, BF16), jax.ShapeDtypeStruct((n_ex + 8, D), F32),
                   jax.ShapeDtypeStruct((N_DEV, n_ex + 8, n_mod), F32)],
        in_specs=[vm] * 5, out_specs=[pl.BlockSpec(memory_space=pl.ANY), vm, vm],
        scratch_shapes=[pltpu.VMEM(w_in_t.shape, BF16), pltpu.VMEM((8, D), F32), pltpu.VMEM((N_DEV, 8, D), F32),
                        pltpu.VMEM((n_ex + 8, n_mod), F32),
                        seven, seven, seven, seven, seven, seven, pltpu.SemaphoreType.DMA((3,))],
        compiler_params=pltpu.CompilerParams(vmem_limit_bytes=VMEM_LIMIT),
    )(w_in_t, c, c_ctx_row, w_mod_loc, b_mod)


def _exchange_copies(in_refs, out_refs, send_sems, recv_sems, local_sems, scatter):
    x, y, c = _coords()
    me = _lin(x, y, c)
    local, remote = [], []
    for a, (src, dst) in enumerate(zip(in_refs, out_refs)):
        local.append(pltpu.make_async_copy(src.at[me] if scatter else src, dst.at[me], local_sems.at[a]))
        for k in range(1, N_DEV):
            peer = (1 - x if k & 4 else x, 1 - y if k & 2 else y, 1 - c if k & 1 else c)
            remote.append(pltpu.make_async_remote_copy(
                src_ref=src.at[_lin(*peer)] if scatter else src, dst_ref=dst.at[me],
                send_sem=send_sems.at[a * (N_DEV - 1) + k - 1], recv_sem=recv_sems.at[a * (N_DEV - 1) + k - 1],
                device_id=peer, device_id_type=MESH_ID))
    return local, remote


def _exchange_scratch(n):
    return [pltpu.SemaphoreType.DMA((n * (N_DEV - 1),)), pltpu.SemaphoreType.DMA((n * (N_DEV - 1),)),
            pltpu.SemaphoreType.DMA((n,))]


def _sum_devices_adamw(items):
    n = len(items)
    tiled = [w.shape[0] % (8 * ADAM_STEPS) == 0 and got.shape[1] == w.shape[0] for got, w, _, _ in items]

    def body(*refs):
        for a in range(n):
            got, w_ref, m_ref, v_ref = refs[4 * a:4 * a + 4]
            g_ref, d_ref, nm_ref, nv_ref = refs[4 * n + 4 * a:4 * n + 4 * a + 4]

            def update(got=got, w_ref=w_ref, m_ref=m_ref, v_ref=v_ref, g_ref=g_ref, d_ref=d_ref, nm_ref=nm_ref,
                       nv_ref=nv_ref):
                g = got[0].astype(F32)
                for j in range(1, N_DEV):
                    g = g + got[j].astype(F32)
                g = g[0:w_ref.shape[0], :]
                delta, m2, v2 = _adamw(w_ref[...], g, m_ref[...], v_ref[...])
                g_ref[...] = g
                d_ref[...] = delta
                nm_ref[...] = m2
                nv_ref[...] = v2

            if tiled[a]:
                update()
            else:
                pl.when(pl.program_id(0) == 0)(update)

    args, out_shape, in_specs, out_specs = [], [], [], []
    for (got, w, m, v), is_tiled in zip(items, tiled):
        assert got.shape[0] == N_DEV and got.shape[1] >= w.shape[0] and got.shape[2:] == w.shape[1:]
        args += [got, w, m, v]
        out_shape += [jax.ShapeDtypeStruct(w.shape, F32)] * 4
        if is_tiled:
            rows = w.shape[0] // ADAM_STEPS
            w_spec = pl.BlockSpec((rows, w.shape[1]), lambda i: (i, 0))
            in_specs += [pl.BlockSpec((N_DEV, rows, w.shape[1]), lambda i: (0, i, 0)), w_spec, w_spec, w_spec]
        else:
            w_spec = pl.BlockSpec(w.shape, lambda i: (0, 0))
            in_specs += [pl.BlockSpec(got.shape, lambda i: (0, 0, 0)), w_spec, w_spec, w_spec]
        out_specs += [w_spec] * 4
    outs = pl.pallas_call(body, name="sum_devices_adamw", grid=(ADAM_STEPS,), out_shape=out_shape,
                          in_specs=in_specs, out_specs=out_specs, compiler_params=_params(("arbitrary",)))(*args)
    return [tuple(outs[4 * a:4 * a + 4]) for a in range(n)]


def _fwd_in(x, modrows, norm_w, w_in_b, cos, sins, qnw_t, knw_t, ones_bd, k_all, v_all, shards, wire_dtypes):
    bl, s, _ = x.shape
    tm = TOKEN_PARTS * TM
    nt = s // tm
    n_sh = len(shards)

    def body(*refs):
        (x_ref, mod_ref, nw_ref, win_ref, cos_ref, sin_ref, qnw_ref, knw_ref, bd_ref, kin_ref, vin_ref) = refs[:11]
        shard_refs = refs[11:11 + n_sh]
        q_ref, k_ref, v_ref, pq_ref, pkv_ref, za_ref, glu_ref, zc_ref = refs[11 + n_sh:19 + n_sh]
        gathered_refs = refs[19 + n_sh:19 + 2 * n_sh]
        stage_refs = refs[19 + 2 * n_sh:19 + 3 * n_sh]
        send_sems, recv_sems, local_sems = refs[19 + 3 * n_sh:]
        b, i = pl.program_id(0), pl.program_id(1)
        local, remote = _exchange_copies(stage_refs, gathered_refs, send_sems, recv_sems, local_sems, scatter=False)

        @pl.when((b == 0) & (i == 0))
        def _():
            for src, stage in zip(shard_refs, stage_refs):
                stage[...] = src[...].astype(stage.dtype)
            for cp in local + remote:
                cp.start()

        shift = mod_ref[0, 0:1, :]
        scale = mod_ref[0, 1:2, :]
        for part in range(TOKEN_PARTS):
            rows = pl.ds(part * TM, TM)
            xv = x_ref[0, rows, :]
            r = lax.rsqrt(jnp.mean(xv * xv, axis=-1, keepdims=True) + EPS)
            u = (xv * r * nw_ref[...]) * (1.0 + scale) + shift
            p = _dot_nt(u.astype(BF16), win_ref[...])
            pq = p[:, 0:DA]
            pk = p[:, DA:DA + HD * 2]
            ck = cos_ref[rows, :]
            sk = sin_ref[rows, :]
            cs = jnp.concatenate([ck] * (DA // KVW), axis=-1)
            sn = jnp.concatenate([sk] * (DA // KVW), axis=-1)
            rq = lax.rsqrt(_segsum(pq * pq, bd_ref[...]) * (1.0 / HD) + EPS)
            qn = pq * rq * qnw_ref[...]
            qr = _rope(qn, cs, sn) * 0.125
            for h in range(DA // HD):
                q_ref[0, h, rows, :] = qr[:, h * HD:(h + 1) * HD].astype(BF16)
            rk = lax.rsqrt(_segsum(pk * pk, bd_ref[0:KVW, 0:KVW]) * (1.0 / HD) + EPS)
            kn = pk * rk * knw_ref[...]
            kr = _rope(kn, ck, sk)
            pv = p[:, 640:768]
            for h in range(KVW // HD):
                k_ref[0, h, rows, :] = kr[:, h * HD:(h + 1) * HD].astype(BF16)
                v_ref[0, h, rows, :] = _with_ones_column(pv[:, h * HD:(h + 1) * HD]).astype(BF16)
            pq_ref[0, rows, :] = pq
            pkv_ref[0, rows, :] = p[:, 512:768]
            za_ref[0, rows, :] = p[:, 768:1280]
            glu_ref[0, rows, :] = p[:, 1280:2304]
            zc_ref[0, rows, :] = p[:, 2304:2816]

        @pl.when((b == bl - 1) & (i == nt - 1))
        def _():
            for cp in remote:
                cp.wait_recv()
            for cp in remote:
                cp.wait_send()
            for cp in local:
                cp.wait()

    def tile(w):
        return pl.BlockSpec((1, tm, w), lambda b, i: (b, i, 0))

    def const(shape):
        return pl.BlockSpec(shape, lambda b, i: (0,) * len(shape))

    outs = [(DA, F32), (2 * KVW, F32), (DA, F32), (2 * DC, F32), (DC, F32)]
    anyspace = pl.BlockSpec(memory_space=pl.ANY)
    rope = pl.BlockSpec((tm, KVW), lambda b, i: (i, 0))
    k_tile = pl.BlockSpec((1, KVW // HD, tm, HD), lambda b, i: (b, 0, i, 0))
    v_tile = pl.BlockSpec((1, KVW // HD, tm, 2 * HD), lambda b, i: (b, 0, i, 0))
    res = pl.pallas_call(
        body, name="fwd_in", grid=(bl, nt),
        in_specs=[tile(D), pl.BlockSpec((1, 3, D), lambda b, i: (b, 0, 0)), const((1, D)), const((D_IN, D)),
                  rope, rope, const((1, DA)), const((1, KVW)), const((DA, DA)), anyspace, anyspace]
        + [const(a.shape) for a in shards],
        out_specs=[pl.BlockSpec((1, DA // HD, tm, HD), lambda b, i: (b, 0, i, 0)), k_tile, v_tile]
        + [tile(w) for w, _ in outs] + [anyspace] * n_sh,
        out_shape=[jax.ShapeDtypeStruct((bl, DA // HD, s, HD), BF16), jax.ShapeDtypeStruct(k_all.shape, BF16),
                   jax.ShapeDtypeStruct(v_all.shape, BF16)]
        + [jax.ShapeDtypeStruct((bl, s, w), dt) for w, dt in outs]
        + [jax.ShapeDtypeStruct((N_DEV,) + a.shape, dt) for a, dt in zip(shards, wire_dtypes)],
        input_output_aliases={9: 1, 10: 2},
        scratch_shapes=[pltpu.VMEM(a.shape, dt) for a, dt in zip(shards, wire_dtypes)] + _exchange_scratch(n_sh),
        compiler_params=_params(("arbitrary", "arbitrary")),
    )(x, modrows, norm_w, w_in_b, cos, sins, qnw_t, knw_t, ones_bd, k_all, v_all, *shards)
    return res[:8], res[8:]


_KV_ROWS_OF_W_IN_T = pl.BlockSpec((2 * KVW, D), lambda b: (DA // (2 * KVW), 0))


def _ctx_fwd(ctx, modc, norm_w, w_kv_b, knw_t, ones_bd, n_keys):
    bl, cl, _ = ctx.shape

    def body(x_ref, mod_ref, nw_ref, w_ref, knw_ref, bd_ref, k_ref, v_ref, pkv_ref):
        xv = x_ref[0]
        shift = mod_ref[0, 0:1, :]
        scale = mod_ref[0, 1:2, :]
        r = lax.rsqrt(jnp.mean(xv * xv, axis=-1, keepdims=True) + EPS)
        u = (xv * r * nw_ref[...]) * (1.0 + scale) + shift
        p = _dot_nt(u.astype(BF16), w_ref[...])
        pk = p[:, 0:KVW]
        rk = lax.rsqrt(_segsum(pk * pk, bd_ref[...]) * (1.0 / HD) + EPS)
        kn = pk * rk * knw_ref[...]
        pv = p[:, KVW:2 * KVW]
        for h in range(KVW // HD):
            k_ref[0, h] = kn[:, h * HD:(h + 1) * HD].astype(BF16)
            v_ref[0, h] = _with_ones_column(pv[:, h * HD:(h + 1) * HD]).astype(BF16)
        pkv_ref[0] = p

    def const(shape):
        return pl.BlockSpec(shape, lambda b: (0,) * len(shape))

    def tile(w):
        return pl.BlockSpec((1, cl, w), lambda b: (b, 0, 0))

    ctx_block = (n_keys - cl) // cl
    assert ctx_block * cl + cl == n_keys
    k_tile = pl.BlockSpec((1, KVW // HD, cl, HD), lambda b: (b, 0, ctx_block, 0))
    v_tile = pl.BlockSpec((1, KVW // HD, cl, 2 * HD), lambda b: (b, 0, ctx_block, 0))
    return pl.pallas_call(
        body, name="ctx_fwd", grid=(bl,),
        in_specs=[tile(D), const((1, 3, D)), const((1, D)), _KV_ROWS_OF_W_IN_T, const((1, KVW)), const((KVW, KVW))],
        out_specs=[k_tile, v_tile, tile(2 * KVW)],
        out_shape=[jax.ShapeDtypeStruct((bl, KVW // HD, n_keys, HD), BF16),
                   jax.ShapeDtypeStruct((bl, KVW // HD, n_keys, 2 * HD), BF16),
                   jax.ShapeDtypeStruct((bl, cl, 2 * KVW), F32)],
        compiler_params=_params(("arbitrary",)),
    )(ctx, modc, norm_w, w_kv_b, knw_t, ones_bd)


def _attn_fwd(q, k, v1):
    bl, _, s, _ = q.shape
    n_keys = k.shape[2]

    def body(q_ref, k_ref, v_ref, o_ref, lse_ref):
        kv = k_ref[0, 0]
        vv = v_ref[0, 0]
        lane = lax.broadcasted_iota(jnp.int32, (TQ, 2 * HD), 1)
        for part in range(FWD_PARTS):
            rows = pl.ds(part * TQ, TQ)
            lse = jnp.zeros((TQ, 2 * HD), F32)
            heads = []
            sc_all = _dot_nt(q_ref[0, :, rows, :].reshape(4 * TQ, HD), kv)
            for h in range(4):
                sc = sc_all[h * TQ:(h + 1) * TQ, :]
                m = jnp.max(sc, axis=-1, keepdims=True)
                e = jnp.exp(sc - m).astype(BF16)
                ov = _dot(e, vv)
                denom = ov[:, HD:HD + 1]
                heads.append(ov[:, 0:HD] * (1.0 / denom))
                lse = jnp.where(lane == h, m + jnp.log(denom), lse)
            o_ref[0, rows, :] = jnp.concatenate(heads, axis=-1)
            lse_ref[0, 0, rows, :] = lse

    tq = FWD_PARTS * TQ
    ks = pl.BlockSpec((1, 1, n_keys, HD), lambda b, g, i: (b, g, 0, 0))
    qs = pl.BlockSpec((1, 4, tq, HD), lambda b, g, i: (b, g, i, 0))
    vs = pl.BlockSpec((1, 1, n_keys, 2 * HD), lambda b, g, i: (b, g, 0, 0))
    return pl.pallas_call(
        body, name="attn_fwd", grid=(bl, 2, s // tq), in_specs=[qs, ks, vs],
        out_specs=[pl.BlockSpec((1, tq, 4 * HD), lambda b, g, i: (b, i, g)),
                   pl.BlockSpec((1, 1, tq, 2 * HD), lambda b, g, i: (b, g, i, 0))],
        out_shape=[jax.ShapeDtypeStruct((bl, s, DA), F32), jax.ShapeDtypeStruct((bl, 2, s, 2 * HD), F32)],
        compiler_params=_params(("arbitrary", "arbitrary", "arbitrary")),
    )(q, k, v1)


def _attn_bwd(q, k, v1, do, o, lse, exchange):
    bl, _, s, _ = q.shape
    n_keys = k.shape[2]
    tq = BWD_PARTS * TQ
    nq = s // tq
    n_ex = len(exchange)

    def body(*refs):
        q_ref, k_ref, v_ref, do_ref, o_ref, lse_ref = refs[:6]
        part_refs = refs[6:6 + n_ex]
        dq_ref, dk_ref, dv_ref = refs[6 + n_ex:9 + n_ex]
        got_refs = refs[9 + n_ex:9 + 2 * n_ex]
        p_sc, ds_sc, dkt, dvt, send_sems, recv_sems, local_sems = refs[9 + 2 * n_ex:]
        i = pl.program_id(2)
        first = (pl.program_id(0) == 0) & (pl.program_id(1) == 0) & (i == 0)
        last = (pl.program_id(0) == bl - 1) & (pl.program_id(1) == 1) & (i == nq - 1)
        local, remote = _exchange_copies(part_refs, got_refs, send_sems, recv_sems, local_sems, scatter=True)

        @pl.when(first)
        def _():
            for cp in local + remote:
                cp.start()

        @pl.when(i == 0)
        def _():
            dkt[...] = jnp.zeros_like(dkt)
            dvt[...] = jnp.zeros_like(dvt)

        kv = k_ref[0, 0]
        vv = v_ref[0, 0][:, 0:HD]
        for part in range(BWD_PARTS):
            tq_rows = pl.ds(part * TQ, TQ)
            lse = lse_ref[0, 0, tq_rows, :]
            ov = o_ref[0, tq_rows, :]
            dqs = []
            q_cat = q_ref[0, :, tq_rows, :].reshape(4 * TQ, HD)
            do_cat = do_ref[0, :, tq_rows, :].reshape(4 * TQ, HD)
            sc_all = _dot_nt(q_cat, kv)
            for h in range(4):
                doh = do_cat[h * TQ:(h + 1) * TQ, :]
                delta = jnp.sum(ov[:, h * HD:(h + 1) * HD] * doh.astype(F32), axis=-1, keepdims=True)
                rows = pl.ds((part * 4 + h) * TQ, TQ)
                p = jnp.exp(sc_all[h * TQ:(h + 1) * TQ, :] - lse[:, h:h + 1])
                ds = (p * (_dot_nt(doh, vv) - delta)).astype(BF16)
                p_sc[rows, :] = p.astype(BF16)
                ds_sc[rows, :] = ds
                dqs.append(_dot(ds, kv) * 0.125)
            dq_ref[0, tq_rows, :] = jnp.concatenate(dqs, axis=-1)
            part_rows = pl.ds(part * 4 * TQ, 4 * TQ)
            dvt[...] += _dot_tn(do_cat, p_sc[part_rows, :])
            dkt[...] += _dot_tn(q_cat, ds_sc[part_rows, :])

        @pl.when(i == nq - 1)
        def _():
            dk_ref[0, 0] = dkt[...].T
            dv_ref[0, 0] = dvt[...].T

        @pl.when(last)
        def _():
            for cp in remote:
                cp.wait_recv()
            for cp in remote:
                cp.wait_send()
            for cp in local:
                cp.wait()

    qs = pl.BlockSpec((1, 4, tq, HD), lambda b, g, i: (b, g, i, 0))
    ks = pl.BlockSpec((1, 1, n_keys, HD), lambda b, g, i: (b, g, 0, 0))
    vs = pl.BlockSpec((1, 1, n_keys, 2 * HD), lambda b, g, i: (b, g, 0, 0))
    os_ = pl.BlockSpec((1, tq, 4 * HD), lambda b, g, i: (b, i, g))
    kshape = jax.ShapeDtypeStruct(k.shape, F32)
    anyspace = pl.BlockSpec(memory_space=pl.ANY)
    res = pl.pallas_call(
        body, name="attn_bwd", grid=(bl, 2, nq),
        in_specs=[qs, ks, vs, qs, os_, pl.BlockSpec((1, 1, tq, 2 * HD), lambda b, g, i: (b, g, i, 0))]
        + [anyspace] * n_ex,
        out_specs=[os_, ks, ks] + [anyspace] * n_ex,
        out_shape=[jax.ShapeDtypeStruct((bl, s, DA), F32), kshape, kshape]
        + [jax.ShapeDtypeStruct(a.shape, a.dtype) for a in exchange],
        scratch_shapes=[pltpu.VMEM((4 * tq, n_keys), BF16), pltpu.VMEM((4 * tq, n_keys), BF16),
                        pltpu.VMEM((HD, n_keys), F32), pltpu.VMEM((HD, n_keys), F32)] + _exchange_scratch(n_ex),
        compiler_params=_params(("arbitrary", "arbitrary", "arbitrary")),
    )(q, k, v1, do, o, lse, *exchange)
    return res[:3], res[3:]


def _halo_specs(width, s):
    per = TC // HALO
    last = s // HALO - 1
    main = pl.BlockSpec((1, TC, width), lambda b, i: (b, i, 0))
    prev = pl.BlockSpec((1, HALO, width), lambda b, i: (b, jnp.maximum(i * per - 1, 0), 0))
    nxt = pl.BlockSpec((1, HALO, width), lambda b, i: (b, jnp.minimum((i + 1) * per, last), 0))
    return main, prev, nxt


def _glu(g):
    return g[:, 0:DC] * _sigmoid(g[:, DC:2 * DC])


def _fill_padded(pad_ref, main, prev, nxt, first, last):
    tc = main.shape[0]
    pad_ref[0:HALO, :] = jnp.where(first, 0.0, prev)
    pad_ref[HALO:HALO + tc, :] = main
    pad_ref[HALO + tc:2 * HALO + tc, :] = jnp.where(last, 0.0, nxt)


PLANE_ROWS = TC + 2 * HALO - 8


def _shift_planes(pad_ref, planes_ref):
    for r in range(1, 8):
        planes_ref[r - 1] = pad_ref[pl.ds(r, planes_ref.shape[1]), :]


def _tap_rows(pad_ref, planes_ref, offset, start, n):
    a, r = divmod(offset, 8)
    if r == 0:
        return pad_ref[pl.ds(start + 8 * a, n), :]
    return planes_ref[r - 1, pl.ds(start + 8 * a, n), :]


def _conv_fwd(glu, conv_w, conv_b, ln_w, ln_b, w_pw_b, b_pw):
    bl, s, _ = glu.shape
    nt = s // TC

    def body(g_ref, gp_ref, gn_ref, cw_ref, cb_ref, lw_ref, lb_ref, wpw_ref, bpw_ref, y_ref, cp_ref, pad_ref, planes_ref):
        i = pl.program_id(1)
        _fill_padded(pad_ref, _glu(g_ref[0]), _glu(gp_ref[0]), _glu(gn_ref[0]), i == 0, i == nt - 1)
        _shift_planes(pad_ref, planes_ref)
        for ck in range(TC // CH):
            acc = jnp.zeros((CH, DC), F32) + cb_ref[...]
            for t in range(KW):
                acc = acc + _tap_rows(pad_ref, planes_ref, 1 + t, ck * CH, CH) * cw_ref[t:t + 1, :]
            y_ref[0, pl.ds(ck * CH, CH), :] = acc
        y = y_ref[0]
        mu = jnp.mean(y, axis=-1, keepdims=True)
        yc = y - mu
        var = jnp.mean(yc * yc, axis=-1, keepdims=True)
        z = yc * lax.rsqrt(var + EPS) * lw_ref[...] + lb_ref[...]
        act = z * _sigmoid(z)
        cp_ref[0] = _dot(act.astype(BF16), wpw_ref[...]) + bpw_ref[...]

    def const(shape):
        return pl.BlockSpec(shape, lambda b, i: (0,) * len(shape))

    main, prev, nxt = _halo_specs(2 * DC, s)
    tile = pl.BlockSpec((1, TC, DC), lambda b, i: (b, i, 0))
    return pl.pallas_call(
        body, name="conv_fwd", grid=(bl, nt),
        in_specs=[main, prev, nxt, const((32, DC)), const((1, DC)), const((1, DC)), const((1, DC)),
                  const((DC, DC)), const((1, DC))],
        out_specs=[tile, tile],
        out_shape=[jax.ShapeDtypeStruct((bl, s, DC), F32)] * 2,
        scratch_shapes=[pltpu.VMEM((TC + 2 * HALO, DC), F32), pltpu.VMEM((7, PLANE_ROWS, DC), F32)],
        compiler_params=_params(("arbitrary", "arbitrary")),
    )(glu, glu, glu, conv_w, conv_b, ln_w, ln_b, w_pw_b, b_pw)


def _conv_bwd_depthwise(glu, dy, conv_w):
    bl, s, _ = glu.shape
    nt = s // TC

    def body(g_ref, gp_ref, gn_ref, d_ref, dp_ref, dn_ref, cw_ref, dglu_ref, dcw_ref,
             padu_ref, padd_ref, planes_u, planes_d):
        i = pl.program_id(1)

        @pl.when((pl.program_id(0) == 0) & (i == 0))
        def _():
            dcw_ref[...] = jnp.zeros_like(dcw_ref)

        first, last = i == 0, i == nt - 1
        _fill_padded(padu_ref, _glu(g_ref[0]), _glu(gp_ref[0]), _glu(gn_ref[0]), first, last)
        _fill_padded(padd_ref, d_ref[0], dp_ref[0], dn_ref[0], first, last)
        _shift_planes(padu_ref, planes_u)
        _shift_planes(padd_ref, planes_d)
        for ck in range(TC // CH):
            acc = jnp.zeros((CH, DC), F32)
            for t in range(KW):
                acc = acc + _tap_rows(padd_ref, planes_d, 2 * HALO - 1 - t, ck * CH, CH) * cw_ref[t:t + 1, :]
            g = g_ref[0, pl.ds(ck * CH, CH), :]
            a = g[:, 0:DC]
            sg = _sigmoid(g[:, DC:2 * DC])
            dglu_ref[0, pl.ds(ck * CH, CH), 0:DC] = (acc * sg).astype(BF16)
            dglu_ref[0, pl.ds(ck * CH, CH), DC:2 * DC] = (acc * a * sg * (1.0 - sg)).astype(BF16)
        group = 4
        for t0 in range(0, KW, group):
            taps = range(t0, min(t0 + group, KW))
            acc8 = [jnp.zeros((8, DC), F32) for _ in taps]
            for ck in range(TC // CH):
                dchunk = d_ref[0, pl.ds(ck * CH, CH), :]
                for n, t in enumerate(taps):
                    prod = _tap_rows(padu_ref, planes_u, 1 + t, ck * CH, CH) * dchunk
                    acc8[n] = acc8[n] + jnp.sum(prod.reshape(CH // 8, 8, DC), axis=0)
            for n, t in enumerate(taps):
                dcw_ref[t:t + 1, :] += jnp.sum(acc8[n], axis=0, keepdims=True)

    gmain, gprev, gnext = _halo_specs(2 * DC, s)
    dmain, dprev, dnext = _halo_specs(DC, s)
    cw = pl.BlockSpec((32, DC), lambda b, i: (0, 0))
    return pl.pallas_call(
        body, name="conv_bwd_depthwise", grid=(bl, nt),
        in_specs=[gmain, gprev, gnext, dmain, dprev, dnext, cw],
        out_specs=[gmain, cw],
        out_shape=[jax.ShapeDtypeStruct((bl, s, 2 * DC), BF16), jax.ShapeDtypeStruct((32, DC), F32)],
        scratch_shapes=[pltpu.VMEM((TC + 2 * HALO, DC), F32)] * 2 + [pltpu.VMEM((7, PLANE_ROWS, DC), F32)] * 2,
        compiler_params=_params(("arbitrary", "arbitrary")),
    )(glu, glu, glu, dy, dy, dy, conv_w)


def _out_fwd_bwd(attn, za, cp, zc, x, target, modrows, w_out_b, y_conv, ln_w, ln_b, w_pw_b):
    bl, s, _ = x.shape
    tm = OUT_TM

    def body(o_ref, za_ref, cp_ref, zc_ref, x_ref, t_ref, mod_ref, w_ref, y_ref, lw_ref, lb_ref, wpw_ref,
             do_ref, dza_ref, dy_ref, dzc_ref, dh_ref, dgate_ref, gwb_ref, loss_ref, gpwb_ref, rows_ref,
             gw_ref, gpw_ref):
        b, i = pl.program_id(0), pl.program_id(1)

        @pl.when((b == 0) & (i == 0))
        def _():
            gw_ref[...] = jnp.zeros_like(gw_ref)
            gpw_ref[...] = jnp.zeros_like(gpw_ref)
            rows_ref[...] = jnp.zeros_like(rows_ref)
            loss_ref[...] = jnp.zeros_like(loss_ref)

        @pl.when(i == 0)
        def _():
            dgate_ref[...] = jnp.zeros_like(dgate_ref)

        gate = mod_ref[0, 2:3, :]
        w = w_ref[...]
        o, za_v, cp_v, zc_v = o_ref[0], za_ref[0], cp_ref[0], zc_ref[0]
        sa = _sigmoid(za_v)
        sc = _sigmoid(zc_v)
        silu_a = za_v * sa
        silu_c = zc_v * sc
        mix = jnp.concatenate([(o * silu_a).astype(BF16), (cp_v * silu_c).astype(BF16)], axis=-1)
        out = _dot(mix, w)
        err = x_ref[0] + gate * out - t_ref[0]
        loss_ref[...] += jnp.sum(err * err, axis=0, keepdims=True)
        dh = err * (1.0 / D)
        dh_ref[0] = dh
        dgate_ref[0] += jnp.sum(dh * out, axis=0, keepdims=True)
        dout = (dh * gate).astype(BF16)
        gw_ref[...] += _dot_tn(mix, dout)
        dmix = _dot_nt(dout, w)
        dga = dmix[:, 0:DA]
        dgc = dmix[:, DA:DA + DC]
        dov = dga * silu_a
        for h in range(DA // HD):
            do_ref[0, h] = dov[:, h * HD:(h + 1) * HD].astype(BF16)
        dza_ref[0] = (dga * o * (sa * (1.0 + za_v * (1.0 - sa)))).astype(BF16)
        dzc_ref[0] = (dgc * cp_v * (sc * (1.0 + zc_v * (1.0 - sc)))).astype(BF16)
        dcp = dgc * silu_c
        y = y_ref[0]
        yc = y - jnp.mean(y, axis=-1, keepdims=True)
        rstd = lax.rsqrt(jnp.mean(yc * yc, axis=-1, keepdims=True) + EPS)
        yn = yc * rstd
        lw = lw_ref[...]
        z = yn * lw + lb_ref[...]
        sg = _sigmoid(z)
        dcp_b = dcp.astype(BF16)
        gpw_ref[...] += _dot_tn((z * sg).astype(BF16), dcp_b)
        dz = _dot_nt(dcp_b, wpw_ref[...]) * (sg * (1.0 + z * (1.0 - sg)))
        dyn = dz * lw
        dy = rstd * (dyn - jnp.mean(dyn, axis=-1, keepdims=True) - yn * jnp.mean(dyn * yn, axis=-1, keepdims=True))
        dy_ref[0] = dy
        rows_ref[0:1, :] += jnp.sum(dcp, axis=0, keepdims=True)
        rows_ref[1:2, :] += jnp.sum(dz * yn, axis=0, keepdims=True)
        rows_ref[2:3, :] += jnp.sum(dz, axis=0, keepdims=True)
        rows_ref[3:4, :] += jnp.sum(dy, axis=0, keepdims=True)

        @pl.when((b == bl - 1) & (i == s // tm - 1))
        def _():
            gwb_ref[...] = gw_ref[...].astype(BF16)
            gpwb_ref[...] = gpw_ref[...].astype(BF16)

    def const(shape):
        return pl.BlockSpec(shape, lambda b, i: (0,) * len(shape))

    def tile(w):
        return pl.BlockSpec((1, tm, w), lambda b, i: (b, i, 0))

    return pl.pallas_call(
        body, name="out_fwd_bwd", grid=(bl, s // tm),
        in_specs=[tile(DA), tile(DA), tile(DC), tile(DC), tile(D), tile(D),
                  pl.BlockSpec((1, 3, D), lambda b, i: (b, 0, 0)), const((D, D)),
                  tile(DC), const((1, DC)), const((1, DC)), const((DC, DC))],
        out_specs=[pl.BlockSpec((1, DA // HD, tm, HD), lambda b, i: (b, 0, i, 0)), tile(DA), tile(DC), tile(DC), tile(D),
                   pl.BlockSpec((1, 1, D), lambda b, i: (b, 0, 0)), const((D, D)), const((1, D)),
                   const((DC, DC)), const((8, DC))],
        out_shape=[jax.ShapeDtypeStruct((bl, DA // HD, s, HD), BF16), jax.ShapeDtypeStruct((bl, s, DA), BF16),
                   jax.ShapeDtypeStruct((bl, s, DC), F32), jax.ShapeDtypeStruct((bl, s, DC), BF16),
                   jax.ShapeDtypeStruct((bl, s, D), F32), jax.ShapeDtypeStruct((bl, 1, D), F32),
                   jax.ShapeDtypeStruct((D, D), BF16), jax.ShapeDtypeStruct((1, D), F32),
                   jax.ShapeDtypeStruct((DC, DC), BF16), jax.ShapeDtypeStruct((8, DC), F32)],
        scratch_shapes=[pltpu.VMEM((D, D), F32), pltpu.VMEM((DC, DC), F32)],
        compiler_params=_params(("arbitrary", "arbitrary")),
    )(attn, za, cp, zc, x, target, modrows, w_out_b, y_conv, ln_w, ln_b, w_pw_b)


def _rms_heads_bwd(dy, x, w_t, ones_bd):
    r = lax.rsqrt(_segsum(x * x, ones_bd) * (1.0 / HD) + EPS)
    xh = x * r
    g = dy * w_t
    dx = r * (g - xh * (_segsum(g * xh, ones_bd) * (1.0 / HD)))
    return dx, dy * xh


def _ctx_bwd(ctx, modc, norm_w, w_kv_b, pkv_c, dk_c, dv_c, knw_t, ones_bd):
    bl, cl, _ = ctx.shape

    def body(x_ref, mod_ref, nw_ref, w_ref, p_ref, dk_ref, dv_ref, knw_ref, bd_ref, gw_ref, rows_ref, dknw_ref):
        @pl.when(pl.program_id(0) == 0)
        def _():
            gw_ref[...] = jnp.zeros_like(gw_ref)
            rows_ref[...] = jnp.zeros_like(rows_ref)
            dknw_ref[...] = jnp.zeros_like(dknw_ref)

        xv = x_ref[0]
        shift = mod_ref[0, 0:1, :]
        scale = mod_ref[0, 1:2, :]
        nw = nw_ref[...]
        r = lax.rsqrt(jnp.mean(xv * xv, axis=-1, keepdims=True) + EPS)
        xn = xv * r
        yv = xn * nw
        u = yv * (1.0 + scale) + shift
        dkv = jnp.concatenate([dk_ref[0, 0], dk_ref[0, 1]], axis=-1)
        dpk, dknw = _rms_heads_bwd(dkv, p_ref[0][:, 0:KVW], knw_ref[...], bd_ref[...])
        dp = jnp.concatenate([dpk.astype(BF16), dv_ref[0, 0].astype(BF16), dv_ref[0, 1].astype(BF16)], axis=-1)
        gw_ref[...] += _dot_tn(dp, u.astype(BF16))
        du = _dot(dp, w_ref[...])
        rows_ref[0:1, :] += jnp.sum(du, axis=0, keepdims=True)
        rows_ref[1:2, :] += jnp.sum(du * yv, axis=0, keepdims=True)
        rows_ref[2:3, :] += jnp.sum(du * (1.0 + scale) * xn, axis=0, keepdims=True)
        dknw_ref[...] += jnp.sum(dknw, axis=0, keepdims=True)

    def const(shape):
        return pl.BlockSpec(shape, lambda b: (0,) * len(shape))

    def tile(w):
        return pl.BlockSpec((1, cl, w), lambda b: (b, 0, 0))

    ctx_block = (dk_c.shape[2] - cl) // cl
    kv_tile = pl.BlockSpec((1, KVW // HD, cl, HD), lambda b: (b, 0, ctx_block, 0))
    return pl.pallas_call(
        body, name="ctx_bwd", grid=(bl,),
        in_specs=[tile(D), const((1, 3, D)), const((1, D)), _KV_ROWS_OF_W_IN_T, tile(2 * KVW), kv_tile, kv_tile,
                  const((1, KVW)), const((KVW, KVW))],
        out_specs=[const((2 * KVW, D)), const((8, D)), const((1, KVW))],
        out_shape=[jax.ShapeDtypeStruct((2 * KVW, D), F32), jax.ShapeDtypeStruct((8, D), F32),
                   jax.ShapeDtypeStruct((1, KVW), F32)],
        compiler_params=_params(("arbitrary",)),
    )(ctx, modc, norm_w, w_kv_b, pkv_c, dk_c, dv_c, knw_t, ones_bd)


def _bwd_in(x, modrows, norm_w, w_in_b, cos, sins, qnw_t, knw_t, ones_bd,
            pq, pkv, dq, dk, dv, dza, dglu, dzc, dh, gw_kv):
    bl, s, _ = x.shape
    tm = TOKEN_PARTS * TM
    nt = s // tm

    def body(x_ref, mod_ref, nw_ref, win_hbm, cos_ref, sin_ref, qnw_ref, knw_ref, bd_ref,
             pq_ref, pkv_ref, dq_ref, dk_ref, dv_ref, dza_ref, dglu_ref, dzc_ref, dh_ref, gwkv_ref,
             gx_ref, gw_hbm, dmod_ref, dnw_ref, dqnw_ref, dknw_ref, win_ref, gw_acc, sem):
        b, i = pl.program_id(0), pl.program_id(1)

        @pl.when((b == 0) & (i == 0))
        def _():
            cp = pltpu.make_async_copy(win_hbm, win_ref, sem)
            cp.start()
            gw_acc[...] = jnp.zeros_like(gw_acc)
            dnw_ref[...] = jnp.zeros_like(dnw_ref)
            dqnw_ref[...] = jnp.zeros_like(dqnw_ref)
            dknw_ref[...] = jnp.zeros_like(dknw_ref)
            cp.wait()

        @pl.when(i == 0)
        def _():
            dmod_ref[...] = jnp.zeros_like(dmod_ref)

        bd = bd_ref[...]
        shift = mod_ref[0, 0:1, :]
        scale = mod_ref[0, 1:2, :]
        nw = nw_ref[...]
        dps, us = [], []
        for part in range(TOKEN_PARTS):
            rows = pl.ds(part * TM, TM)
            ck = cos_ref[rows, :]
            sk = sin_ref[rows, :]
            cs = jnp.concatenate([ck] * (DA // KVW), axis=-1)
            sn = jnp.concatenate([sk] * (DA // KVW), axis=-1)
            dqn = _rope_bwd(dq_ref[0, rows, :], cs, sn)
            dpq, dqnw = _rms_heads_bwd(dqn, pq_ref[0, rows, :], qnw_ref[...], bd)
            dkn = _rope_bwd(jnp.concatenate([dk_ref[0, 0, rows, :], dk_ref[0, 1, rows, :]], axis=-1), ck, sk)
            dpk, dknw = _rms_heads_bwd(dkn, pkv_ref[0, rows, 0:KVW], knw_ref[...], bd[0:KVW, 0:KVW])
            dqnw_ref[...] += jnp.sum(dqnw, axis=0, keepdims=True)
            dknw_ref[...] += jnp.sum(dknw, axis=0, keepdims=True)
            dp = jnp.concatenate(
                [dpq.astype(BF16), dpk.astype(BF16), dv_ref[0, 0, rows, :].astype(BF16), dv_ref[0, 1, rows, :].astype(BF16),
                 dza_ref[0, rows, :], dglu_ref[0, rows, :], dzc_ref[0, rows, :]], axis=-1)

            xv = x_ref[0, rows, :]
            r = lax.rsqrt(jnp.mean(xv * xv, axis=-1, keepdims=True) + EPS)
            xn = xv * r
            yv = xn * nw
            u = yv * (1.0 + scale) + shift
            dps.append(dp)
            us.append(u.astype(BF16))
            du = _dot(dp, win_ref[...])
            dmod_ref[0, 0:1, :] += jnp.sum(du, axis=0, keepdims=True)
            dmod_ref[0, 1:2, :] += jnp.sum(du * yv, axis=0, keepdims=True)
            dy = du * (1.0 + scale)
            dnw_ref[...] += jnp.sum(dy * xn, axis=0, keepdims=True)
            dxn = dy * nw
            gx_ref[0, rows, :] = dh_ref[0, rows, :] + r * (dxn - xn * jnp.mean(dxn * xn, axis=-1, keepdims=True))
        gw_acc[...] += _dot_tn(jnp.concatenate(dps, axis=0), jnp.concatenate(us, axis=0))

        @pl.when((b == bl - 1) & (i == nt - 1))
        def _():
            gw_acc[DA:DA + 2 * KVW, :] += gwkv_ref[...]

            def to_bf16(j, carry):
                rows = pl.ds(pl.multiple_of(j * 2 * KVW, 2 * KVW), 2 * KVW)
                win_ref[rows, :] = gw_acc[rows, :].astype(BF16)
                return carry

            lax.fori_loop(0, D_IN // (2 * KVW), to_bf16, 0)
            pltpu.sync_copy(win_ref, gw_hbm)

    def tile(w):
        return pl.BlockSpec((1, tm, w), lambda b, i: (b, i, 0))

    def const(shape):
        return pl.BlockSpec(shape, lambda b, i: (0,) * len(shape))

    anyspace = pl.BlockSpec(memory_space=pl.ANY)
    rope = pl.BlockSpec((tm, KVW), lambda b, i: (i, 0))
    kv_tile = pl.BlockSpec((1, KVW // HD, tm, HD), lambda b, i: (b, 0, i, 0))
    return pl.pallas_call(
        body, name="bwd_in", grid=(bl, nt),
        in_specs=[tile(D), pl.BlockSpec((1, 3, D), lambda b, i: (b, 0, 0)), const((1, D)), anyspace, rope, rope,
                  const((1, DA)), const((1, KVW)), const((DA, DA)),
                  tile(DA), tile(2 * KVW), tile(DA), kv_tile, kv_tile, tile(DA), tile(2 * DC), tile(DC), tile(D),
                  const((2 * KVW, D))],
        out_specs=[tile(D), anyspace, pl.BlockSpec((1, 2, D), lambda b, i: (b, 0, 0)), const((1, D)),
                   const((1, DA)), const((1, KVW))],
        out_shape=[jax.ShapeDtypeStruct((bl, s, D), F32), jax.ShapeDtypeStruct((D_IN, D), BF16),
                   jax.ShapeDtypeStruct((bl, 2, D), F32), jax.ShapeDtypeStruct((1, D), F32),
                   jax.ShapeDtypeStruct((1, DA), F32), jax.ShapeDtypeStruct((1, KVW), F32)],
        scratch_shapes=[pltpu.VMEM((D_IN, D), BF16), pltpu.VMEM((D_IN, D), F32), pltpu.SemaphoreType.DMA],
        compiler_params=_params(("arbitrary", "arbitrary")),
    )(x, modrows, norm_w, w_in_b, cos, sins, qnw_t, knw_t, ones_bd,
      pq, pkv, dq, dk, dv, dza, dglu, dzc, dh, gw_kv)


_LOSS, _DMODC, _NW, _QN, _KN, _CB, _LW, _LB, _BPW, SMALL_W = 0, 1024, 4096, 5120, 5248, 5376, 5888, 6400, 6912, 7424


ROW_W = 1792


def _put_flat(ref, off, value):
    n, done = value.shape[1], 0
    while done < n:
        r, c = divmod(off + done, ROW_W)
        take = min(n - done, ROW_W - c)
        ref[r:r + 1, c:c + take] = value[:, done:done + take]
        done += take


def _get_flat(arr, off, n):
    parts, done = [], 0
    while done < n:
        r, c = divmod(off + done, ROW_W)
        take = min(n - done, ROW_W - c)
        parts.append(arr[r:r + 1, c:c + take])
        done += take
    return parts[0] if len(parts) == 1 else jnp.concatenate(parts, axis=-1)


def _pack_small_body(loss_ref, ctx_ref, dnw_ref, dqnw_ref, dknw_ref, dknwc_ref, conv_ref, dss_ref, dgate_ref, o_ref):
    bl = dss_ref.shape[0]
    assert SMALL_W + bl * 3 * D <= 8 * ROW_W
    o_ref[...] = jnp.zeros_like(o_ref)
    _put_flat(o_ref, _LOSS, loss_ref[...])
    _put_flat(o_ref, _DMODC, ctx_ref[0:1, :])
    _put_flat(o_ref, _DMODC + D, ctx_ref[1:2, :])
    _put_flat(o_ref, _NW, dnw_ref[...] + ctx_ref[2:3, :])
    dq = dqnw_ref[...]
    qn = dq[:, 0:HD]
    for h in range(1, DA // HD):
        qn = qn + dq[:, h * HD:(h + 1) * HD]
    _put_flat(o_ref, _QN, qn)
    dk = dknw_ref[...] + dknwc_ref[...]
    _put_flat(o_ref, _KN, dk[:, 0:HD] + dk[:, HD:2 * HD])
    _put_flat(o_ref, _BPW, conv_ref[0:1, :])
    _put_flat(o_ref, _LW, conv_ref[1:2, :])
    _put_flat(o_ref, _LB, conv_ref[2:3, :])
    _put_flat(o_ref, _CB, conv_ref[3:4, :])
    for b in range(bl):
        _put_flat(o_ref, SMALL_W + b * 3 * D, dss_ref[b, 0:1, :])
        _put_flat(o_ref, SMALL_W + b * 3 * D + D, dss_ref[b, 1:2, :])
        _put_flat(o_ref, SMALL_W + b * 3 * D + 2 * D, dgate_ref[b])


_SMALL = (("b_mod", None), ("norm_w", _NW), ("q_norm_w", _QN), ("k_norm_w", _KN), ("conv_b", _CB),
          ("conv_ln_w", _LW), ("conv_ln_b", _LB), ("b_pw", _BPW), ("c_ctx", None))


def _epilogue(parts_in, pieces, c_rows, w_mod_loc):
    bl = pieces[7].shape[0]
    n_ex = N_DEV * bl
    n_mod = w_mod_loc.shape[1]
    rb = 32
    shp = parts_in.shape[1:]
    rows_in = shp[0]

    def body(*refs):
        it = iter(refs)
        take = lambda k: [next(it) for _ in range(k)]
        (parts,) = take(1)
        piece_refs = take(9)
        (c_ref, wm_ref) = take(2)
        (g_in, g_wm, sum_ref, gb_ref, gc_all, loss_ref) = take(6)
        (mine, got_sib, stage, got_chip, payload, gathered, dmod_full, gc_mine) = take(8)
        (d2d_send, d2d_recv, ici_send, ici_recv, local_sems, sg_send, sg_recv, gc_send, gc_recv, misc_sems) = take(10)

        x, y, c = _coords()
        me = _lin(x, y, c)
        sib = (x, y, 1 - c)
        home = 2 * x + y

        def rows_loop(fn):
            def step(i, carry):
                fn(pl.ds(pl.multiple_of(i * rb, rb), rb))
                return carry
            lax.fori_loop(0, rows_in // rb, step, 0)

        def direct_gather(src, dst, send_sems, recv_sems, local_sem):
            cps = [pltpu.make_async_copy(src, dst.at[me], local_sem)]
            for k in range(1, N_DEV):
                peer = (1 - x if k & 4 else x, 1 - y if k & 2 else y, 1 - c if k & 1 else c)
                cps.append(pltpu.make_async_remote_copy(
                    src_ref=src, dst_ref=dst.at[me], send_sem=send_sems.at[k - 1], recv_sem=recv_sems.at[k - 1],
                    device_id=peer, device_id_type=MESH_ID))
            for cp in cps:
                cp.start()
            return cps

        _pack_small_body(*piece_refs, payload)
        small_cps = direct_gather(payload, gathered, sg_send, sg_recv, misc_sems.at[0])

        local, d2d, ici = [], [], []
        for s in range(4):
            cp = pltpu.make_async_copy(parts.at[_lin(s // 2, s % 2, c)], mine.at[s], local_sems.at[s])
            cp.start()
            local.append(cp)
            rc = pltpu.make_async_remote_copy(
                src_ref=parts.at[_lin(s // 2, s % 2, 1 - c)], dst_ref=got_sib.at[s],
                send_sem=d2d_send.at[s], recv_sem=d2d_recv.at[s], device_id=sib, device_id_type=MESH_ID)
            rc.start()
            d2d.append(rc)

        for cp in small_cps[1:]:
            cp.wait_recv()
        small_cps[0].wait()
        tot = gathered[0]
        for j in range(1, N_DEV):
            tot = tot + gathered[j]
        summed = _get_flat(tot, 0, SMALL_W)
        dmod_full[...] = jnp.zeros_like(dmod_full)
        for j in range(N_DEV):
            arr = gathered[j]
            for b in range(bl):
                dmod_full[j * bl + b:j * bl + b + 1, :] = _get_flat(arr, SMALL_W + b * 3 * D, 3 * D)
        dmod_full[n_ex:n_ex + 1, :] = summed[:, _DMODC:_DMODC + 3 * D]
        sum_ref[...] = summed
        gb_ref[...] = jnp.sum(dmod_full[...], axis=0, keepdims=True)
        loss_ref[...] = (0.5 / D) * jnp.sum(summed[:, _LOSS:_LOSS + D], axis=-1, keepdims=True)

        north = c == 1
        first = (jnp.where(north, 1 - x, x), jnp.where(north, y, 1 - y))
        second = (jnp.where(north, x, 1 - x), jnp.where(north, 1 - y, y))
        for s in range(4):
            local[s].wait()
            d2d[s].wait_recv()

        def chip_sum(k, chip, relayed):
            slot = 2 * chip[0] + chip[1]

            def pair_sum(rs):
                acc = mine[slot, rs, :].astype(F32) + got_sib[slot, rs, :].astype(F32)
                if relayed:
                    acc = acc + got_chip[1, rs, :].astype(F32)
                stage[k, rs, :] = acc.astype(BF16)

            rows_loop(pair_sum)

        def send(k, to):
            rc = pltpu.make_async_remote_copy(
                src_ref=stage.at[k], dst_ref=got_chip.at[k], send_sem=ici_send.at[k], recv_sem=ici_recv.at[k],
                device_id=(to[0], to[1], c), device_id_type=MESH_ID)
            rc.start()
            ici.append(rc)

        chip_sum(0, first, False)
        send(0, first)
        chip_sum(1, (1 - x, 1 - y), False)
        send(1, first)

        cr = c_ref[...]
        act = (cr * _sigmoid(cr)).astype(BF16)
        dm = dmod_full[:, pl.ds(pl.multiple_of(me * n_mod, 128), n_mod)].astype(BF16)
        g_wm[...] = _dot_tn(act, dm)
        gc_mine[...] = _dot_nt(dm[n_ex:n_ex + 8, :], wm_ref[...].astype(BF16))
        gc_cps = direct_gather(gc_mine, gc_all, gc_send, gc_recv, misc_sems.at[1])

        ici[1].wait_recv()
        chip_sum(2, second, True)
        send(2, second)
        ici[0].wait_recv()
        ici[2].wait_recv()

        def finish(rs):
            gsum = mine[home, rs, :].astype(F32) + got_sib[home, rs, :].astype(F32)
            g_in[rs, :] = gsum + got_chip[0, rs, :].astype(F32) + got_chip[2, rs, :].astype(F32)

        rows_loop(finish)

        for cp in gc_cps[1:]:
            cp.wait_recv()
        gc_cps[0].wait()
        for rc in d2d + ici + small_cps[1:] + gc_cps[1:]:
            rc.wait_send()

    vm = pl.BlockSpec(memory_space=pltpu.VMEM)
    anyspace = pl.BlockSpec(memory_space=pl.ANY)
    assert rows_in % rb == 0 and parts_in.dtype == BF16
    args = [parts_in, *pieces, c_rows, w_mod_loc]
    in_specs = [anyspace] + [vm] * (len(args) - 1)
    out_shape = [jax.ShapeDtypeStruct(shp, F32), jax.ShapeDtypeStruct(w_mod_loc.shape, F32),
                 jax.ShapeDtypeStruct((1, SMALL_W), F32), jax.ShapeDtypeStruct((1, 3 * D), F32),
                 jax.ShapeDtypeStruct((N_DEV, 8, D), F32), jax.ShapeDtypeStruct((1, 1), F32)]
    scratch = [pltpu.VMEM((4,) + shp, BF16), pltpu.VMEM((4,) + shp, BF16), pltpu.VMEM((3,) + shp, BF16),
               pltpu.VMEM((3,) + shp, BF16), pltpu.VMEM((8, ROW_W), F32), pltpu.VMEM((N_DEV, 8, ROW_W), F32),
               pltpu.VMEM((n_ex + 8, 3 * D), F32), pltpu.VMEM((8, D), F32),
               pltpu.SemaphoreType.DMA((4,)), pltpu.SemaphoreType.DMA((4,)), pltpu.SemaphoreType.DMA((3,)),
               pltpu.SemaphoreType.DMA((3,)), pltpu.SemaphoreType.DMA((4,)),
               pltpu.SemaphoreType.DMA((N_DEV - 1,)), pltpu.SemaphoreType.DMA((N_DEV - 1,)),
               pltpu.SemaphoreType.DMA((N_DEV - 1,)), pltpu.SemaphoreType.DMA((N_DEV - 1,)),
               pltpu.SemaphoreType.DMA((2,))]
    return pl.pallas_call(
        body, name="epilogue", out_shape=out_shape, in_specs=in_specs, out_specs=[vm] * len(out_shape),
        scratch_shapes=scratch, compiler_params=pltpu.CompilerParams(vmem_limit_bytes=VMEM_LIMIT),
    )(*args)


def _final_adamw(g_in, w_in_t, m_in_t, v_in_t, g_wm, w_mod_loc, m_mod, v_mod, summed, g_bmod, gc_all,
                 small_w, small_m, small_v):
    ns = len(_SMALL)

    def body(*refs):
        it = iter(refs)
        take = lambda k: [next(it) for _ in range(k)]
        (gin_ref, w_ref, m_ref, v_ref, gwm_ref, wm_ref, mm_ref, vm_ref, sum_ref, gb_ref, gc_ref) = take(11)
        sw, sm, sv = take(ns), take(ns), take(ns)
        (d_in, nm_in, nv_in, d_wm, nm_wm, nv_wm) = take(6)
        souts = take(4 * ns)

        for g_r, w_r, m_r, v_r, outs3 in ((gin_ref, w_ref, m_ref, v_ref, (d_in, nm_in, nv_in)),
                                          (gwm_ref, wm_ref, mm_ref, vm_ref, (d_wm, nm_wm, nv_wm))):
            for o_r, val in zip(outs3, _adamw(w_r[...], g_r[...], m_r[...], v_r[...])):
                o_r[...] = val

        @pl.when(pl.program_id(0) == 0)
        def _():
            for k, (name, off) in enumerate(_SMALL):
                w = sw[k][...]
                if name == "b_mod":
                    gk = gb_ref[...]
                elif name == "c_ctx":
                    acc = gc_ref[0, 0:1, :]
                    for j in range(1, N_DEV):
                        acc = acc + gc_ref[j, 0:1, :]
                    sg = _sigmoid(w)
                    gk = acc * (sg * (1.0 + w * (1.0 - sg)))
                else:
                    gk = sum_ref[:, off:off + w.shape[1]]
                dl, m_new, v_new = _adamw(w, gk, sm[k][...], sv[k][...])
                souts[k][...] = gk
                souts[ns + k][...] = dl
                souts[2 * ns + k][...] = m_new
                souts[3 * ns + k][...] = v_new

    rows_in, rows_mod = w_in_t.shape[0] // ADAM_STEPS, w_mod_loc.shape[0] // ADAM_STEPS
    assert rows_in % 8 == 0 and rows_mod % 8 == 0

    def whole(a):
        return pl.BlockSpec(a.shape, lambda i: (0,) * a.ndim)

    in_tile = pl.BlockSpec((rows_in, w_in_t.shape[1]), lambda i: (i, 0))
    mod_tile = pl.BlockSpec((rows_mod, w_mod_loc.shape[1]), lambda i: (i, 0))
    small_in = [summed, g_bmod, gc_all, *small_w, *small_m, *small_v]
    big_shape = jax.ShapeDtypeStruct(w_in_t.shape, F32)
    mod_shape = jax.ShapeDtypeStruct(w_mod_loc.shape, F32)
    out_shape = [big_shape] * 3 + [mod_shape] * 3 + [jax.ShapeDtypeStruct(w.shape, F32) for w in small_w] * 4
    outs = pl.pallas_call(
        body, name="final_adamw", grid=(ADAM_STEPS,), out_shape=out_shape,
        in_specs=[in_tile] * 4 + [mod_tile] * 4 + [whole(a) for a in small_in],
        out_specs=[in_tile] * 3 + [mod_tile] * 3 + [whole(w) for w in small_w] * 4,
        compiler_params=_params(("arbitrary",)),
    )(g_in, w_in_t, m_in_t, v_in_t, g_wm, w_mod_loc, m_mod, v_mod, *small_in)
    small_outs = [outs[6 + k * ns:6 + (k + 1) * ns] for k in range(4)]
    return (g_in,) + tuple(outs[0:3]), (g_wm,) + tuple(outs[3:6]), small_outs


def _rope_tables(s):
    t = jnp.arange(s, dtype=jnp.int32)
    row = (t // GRID_W).astype(F32)
    col = (t % GRID_W).astype(F32)
    freqs = ROPE_THETA ** (-jnp.arange(0, HD // 2, 2, dtype=F32) / (HD // 2))
    ang_r = row[:, None] * freqs[None, :]
    ang_c = col[:, None] * freqs[None, :]
    cr, sr, cc, sc = jnp.cos(ang_r), jnp.sin(ang_r), jnp.cos(ang_c), jnp.sin(ang_c)
    cos = jnp.concatenate([cr, cr, cc, cc], axis=-1)
    sins = jnp.concatenate([-sr, sr, -sc, sc], axis=-1)
    return jnp.tile(cos, (1, KVW // HD)), jnp.tile(sins, (1, KVW // HD))


def kernel(x, c, ctx, c_ctx, w_mod, b_mod, norm_w, w_in, q_norm_w, k_norm_w, conv_w, conv_b, conv_ln_w, conv_ln_b, w_pw, b_pw, w_out, loss_target, m_c_ctx, m_w_mod, m_b_mod, m_norm_w, m_w_in, m_q_norm_w, m_k_norm_w, m_conv_w, m_conv_b, m_conv_ln_w, m_conv_ln_b, m_w_pw, m_b_pw, m_w_out, v_c_ctx, v_w_mod, v_b_mod, v_norm_w, v_w_in, v_q_norm_w, v_k_norm_w, v_conv_w, v_conv_b, v_conv_ln_w, v_conv_ln_b, v_w_pw, v_b_pw, v_w_out):
    bl, s, _ = x.shape
    cl = ctx.shape[1]
    me = _lin(*_coords())

    conv_w_pad = jnp.pad(conv_w[0], ((0, 32 - KW), (0, 0)))
    n_ex = N_DEV * bl
    g_win, c_rows, g_mod = _prologue(w_in[0].T, c, c_ctx[None, :], w_mod[0], b_mod)
    w_in_b = g_win.reshape(D_IN, D)
    mod_all = g_mod.transpose(1, 0, 2).reshape(n_ex + 8, 3 * D)
    modrows = lax.dynamic_slice_in_dim(mod_all, me * bl, bl, axis=0).reshape(bl, 3, D)
    modc = mod_all[n_ex].reshape(1, 3, D)

    cos, sins = _rope_tables(s)
    qnw_t = jnp.tile(q_norm_w, (1, DA // HD))
    knw_t = jnp.tile(k_norm_w, (1, KVW // HD))
    lane = jnp.arange(DA, dtype=jnp.int32) // HD
    ones_bd = (lane[:, None] == lane[None, :]).astype(BF16)
    ones_kv = ones_bd[0:KVW, 0:KVW]
    w_kv_b = w_in_b

    k_ctx, v_ctx, pkv_c = _ctx_fwd(ctx, modc, norm_w, w_kv_b, knw_t, ones_kv, cl + s)
    (q_h, k_h, v_h, pq, pkv, za, glu, zc), (g_wout, g_wpw, g_cw) = _fwd_in(
        x, modrows, norm_w, w_in_b, cos, sins, qnw_t, knw_t, ones_bd, k_ctx, v_ctx,
        [w_out[0], w_pw[0], conv_w_pad], [BF16, BF16, F32])
    w_out_b = g_wout.reshape(D, D)
    w_pw_b = g_wpw.reshape(DC, DC)
    conv_w_full = g_cw.transpose(1, 0, 2).reshape(32, DC)
    attn, lse = _attn_fwd(q_h, k_h, v_h)
    y_conv, cp = _conv_fwd(glu, conv_w_full, conv_b, conv_ln_w, conv_ln_b, w_pw_b, b_pw)

    do_h, dza, dy_conv, dzc, dh, dgate, gw_out, loss_row, gw_pw, conv_rows = _out_fwd_bwd(
        attn, za, cp, zc, x, loss_target, modrows, w_out_b, y_conv, conv_ln_w, conv_ln_b, w_pw_b)
    dglu, g_cw_full = _conv_bwd_depthwise(glu, dy_conv, conv_w_full)
    parts_out = gw_out.reshape(N_DEV, D // N_DEV, D)
    parts_pw = gw_pw.reshape(N_DEV, DC // N_DEV, DC)
    parts_cw = g_cw_full.astype(BF16).reshape(32, N_DEV, DC // N_DEV).transpose(1, 0, 2)
    (dq, dk_h, dv_h), (got_out, got_pw, got_cw) = _attn_bwd(
        q_h, k_h, v_h, do_h, attn, lse, [parts_out, parts_pw, parts_cw])
    gw_kv, ctx_rows, dknw_c = _ctx_bwd(ctx, modc, norm_w, w_kv_b, pkv_c, dk_h, dv_h, knw_t, ones_kv)
    grad_x, gw_in, dmod_ss, dnw, dqnw, dknw = _bwd_in(
        x, modrows, norm_w, w_in_b, cos, sins, qnw_t, knw_t, ones_bd,
        pq, pkv, dq, dk_h, dv_h, dza, dglu, dzc, dh, gw_kv)

    r_out, r_pw, r_cw = _sum_devices_adamw(
        [(got_out, w_out[0], m_w_out[0], v_w_out[0]), (got_pw, w_pw[0], m_w_pw[0], v_w_pw[0]),
         (got_cw, conv_w[0], m_conv_w[0], v_conv_w[0])])

    given = {"c_ctx": (c_ctx, m_c_ctx, v_c_ctx), "b_mod": (b_mod, m_b_mod, v_b_mod), "norm_w": (norm_w, m_norm_w, v_norm_w),
             "q_norm_w": (q_norm_w, m_q_norm_w, v_q_norm_w), "k_norm_w": (k_norm_w, m_k_norm_w, v_k_norm_w),
             "conv_b": (conv_b, m_conv_b, v_conv_b), "conv_ln_w": (conv_ln_w, m_conv_ln_w, v_conv_ln_w),
             "conv_ln_b": (conv_ln_b, m_conv_ln_b, v_conv_ln_b), "b_pw": (b_pw, m_b_pw, v_b_pw)}
    as_rows = [[given[name][which].reshape(1, -1) for name, _ in _SMALL] for which in range(3)]
    g_in_t, g_wmod, summed, g_bmod, gc_all, loss11 = _epilogue(
        gw_in.reshape(N_DEV, D_IN // N_DEV, D),
        [loss_row, ctx_rows, dnw, dqnw, dknw, dknw_c, conv_rows, dmod_ss, dgate], c_rows, w_mod[0])
    r_in, r_wmod, small_outs = _final_adamw(
        g_in_t, w_in[0].T, m_w_in[0].T, v_w_in[0].T, g_wmod, w_mod[0], m_w_mod[0], v_w_mod[0],
        summed, g_bmod, gc_all, *as_rows)
    r_in = tuple(a.T for a in r_in)

    big = {"w_mod": r_wmod, "w_in": r_in, "conv_w": r_cw, "w_pw": r_pw, "w_out": r_out}
    order = ["c_ctx", "w_mod", "b_mod", "norm_w", "w_in", "q_norm_w", "k_norm_w", "conv_w", "conv_b", "conv_ln_w",
             "conv_ln_b", "w_pw", "b_pw", "w_out"]
    small_index = {name: k for k, (name, _) in enumerate(_SMALL)}
    outs = [loss11.reshape(()), grad_x]
    for which in range(4):
        for name in order:
            if name in big:
                outs.append(big[name][which][None])
            else:
                outs.append(small_outs[which][small_index[name]].reshape(given[name][0].shape))
    return tuple(outs)
```

```python
import jax
import jax.numpy as jnp
from jax import lax
from jax.experimental import pallas as pl
from jax.experimental.pallas import tpu as pltpu

F32, BF16 = jnp.float32, jnp.bfloat16
MESH_ID = pl.DeviceIdType.MESH

N_DEV = 8
D = 1024
D_IN = 2816
DA = 512
DC = 512
HD = 64
KVW = 128
KW = 31
HALO = 16
EPS = 1e-6
ROPE_THETA = 10000.0
GRID_W = 64

ADAM_LR, ADAM_B1, ADAM_B2, ADAM_EPS, ADAM_WD, ADAM_STEP = 0.001, 0.9, 0.999, 1e-08, 0.01, 10

VMEM_LIMIT = 56 * 1024 * 1024

TM = 256
TQ = 128
TOKEN_PARTS = 2
OUT_TM = 512
BWD_PARTS = 4
FWD_PARTS = 8
TC = 512
CH = 32
ADAM_STEPS = 4


def _params(sem, vmem=VMEM_LIMIT):
    return pltpu.CompilerParams(dimension_semantics=sem, vmem_limit_bytes=vmem)


def _dot(a, b):
    return jnp.dot(a, b, preferred_element_type=F32)


def _dot_nt(a, b):
    return lax.dot_general(a, b, (((1,), (1,)), ((), ())), preferred_element_type=F32)


def _dot_tn(a, b):
    return lax.dot_general(a, b, (((0,), (0,)), ((), ())), preferred_element_type=F32)


def _sigmoid(z):
    return 1.0 / (1.0 + jnp.exp(-z))


def _segsum(v, ones_bd):
    return _dot(v.astype(BF16), ones_bd)


def _swap16(x):
    w = x.shape[-1]
    lane = lax.broadcasted_iota(jnp.int32, x.shape, 1)
    return jnp.where((lane % 32) < 16, pltpu.roll(x, w - 16, 1), pltpu.roll(x, 16, 1))


def _with_ones_column(v):
    one = (lax.broadcasted_iota(jnp.int32, v.shape, 1) == 0).astype(v.dtype)
    return jnp.concatenate([v, one], axis=-1)


def _rope(x, cos, sins):
    return x * cos + _swap16(x) * sins


def _rope_bwd(d, cos, sins):
    return d * cos + _swap16(d * sins)


def _adamw(w, g, m, v):
    m2 = ADAM_B1 * m + (1.0 - ADAM_B1) * g
    v2 = ADAM_B2 * v + (1.0 - ADAM_B2) * (g * g)
    m_hat = m2 / (1.0 - ADAM_B1 ** ADAM_STEP)
    v_hat = v2 / (1.0 - ADAM_B2 ** ADAM_STEP)
    delta = -ADAM_LR * (m_hat / (jnp.sqrt(v_hat) + ADAM_EPS) + ADAM_WD * w)
    return delta, m2, v2


def _coords():
    return lax.axis_index("x"), lax.axis_index("y"), lax.axis_index("c")


def _lin(x, y, c):
    return 4 * x + 2 * y + c


def _prologue(w_in_t, c, c_ctx_row, w_mod_loc, b_mod):
    bl = c.shape[0]
    n_ex = N_DEV * bl
    n_mod = w_mod_loc.shape[1]

    def body(w32_ref, c_in_ref, cctx_ref, wm_ref, b_ref, out_w, crows_ref, mod_out, w_ref, c_ref, c_gath, mod_mine,
             w_send, w_recv, c_send, c_recv, m_send, m_recv, local_sems):
        x, y, c = _coords()
        me_lin = _lin(x, y, c)
        c_ref[...] = jnp.zeros_like(c_ref)
        c_ref[0:bl, :] = c_in_ref[...]
        w_ref[...] = w32_ref[...].astype(BF16)
        me, sib = (x, y, c), (x, y, 1 - c)
        xnb, ynb, diag = (1 - x, y), (x, 1 - y), (1 - x, 1 - y)
        north = c == 1

        def direct_gather(src, dst, send_sems, recv_sems, local_sem):
            cps = [pltpu.make_async_copy(src, dst.at[me_lin], local_sem)]
            for k in range(1, N_DEV):
                peer = (1 - x if k & 4 else x, 1 - y if k & 2 else y, 1 - c if k & 1 else c)
                cps.append(pltpu.make_async_remote_copy(
                    src_ref=src, dst_ref=dst.at[me_lin], send_sem=send_sems.at[k - 1], recv_sem=recv_sems.at[k - 1],
                    device_id=peer, device_id_type=MESH_ID))
            for cp in cps:
                cp.start()
            return cps

        def copy(k, block, to, src=None):
            slot = out_w.at[_lin(*block)]
            return pltpu.make_async_remote_copy(
                src_ref=slot if src is None else src, dst_ref=slot, send_sem=w_send.at[k], recv_sem=w_recv.at[k],
                device_id=to, device_id_type=MESH_ID)

        c_cps = direct_gather(c_ref, c_gath, c_send, c_recv, local_sems.at[0])
        mine = pltpu.make_async_copy(w_ref, out_w.at[me_lin], local_sems.at[1])
        mine.start()
        first = [copy(0, me, sib, src=w_ref), copy(1, me, (*xnb, c), src=w_ref), copy(2, me, (*ynb, c), src=w_ref)]
        for cp in first:
            cp.start()

        for cp in c_cps[1:]:
            cp.wait_recv()
        c_cps[0].wait()
        crows_ref[...] = jnp.zeros_like(crows_ref)
        for j in range(N_DEV):
            crows_ref[j * bl:(j + 1) * bl, :] = c_gath[j, 0:bl, :]
        crows_ref[n_ex:n_ex + 1, :] = cctx_ref[...]
        cr = crows_ref[...]
        act = (cr * _sigmoid(cr)).astype(BF16)
        mod_mine[...] = _dot(act, wm_ref[...].astype(BF16)) + b_ref[:, pl.ds(pl.multiple_of(me_lin * n_mod, 128), n_mod)]
        mod_cps = direct_gather(mod_mine, mod_out, m_send, m_recv, local_sems.at[2])

        relay_north = copy(3, (*xnb, c), (*ynb, c))
        relay_south = copy(3, (*ynb, c), (*xnb, c))
        passed = []
        copy(1, (*xnb, c), me).wait_recv()
        pl.when(north)(relay_north.start)
        passed.append(copy(4, (*xnb, c), sib))
        passed[-1].start()
        copy(2, (*ynb, c), me).wait_recv()
        pl.when(jnp.logical_not(north))(relay_south.start)
        passed.append(copy(5, (*ynb, c), sib))
        passed[-1].start()
        copy(3, (*diag, c), me).wait_recv()
        passed.append(copy(6, (*diag, c), sib))
        passed[-1].start()
        copy(0, sib, me).wait_recv()
        for k, chip in ((4, xnb), (5, ynb), (6, diag)):
            copy(k, (*chip, 1 - c), me).wait_recv()
        for cp in mod_cps[1:]:
            cp.wait_recv()
        mod_cps[0].wait()
        for cp in first + passed + [relay_north] + c_cps[1:] + mod_cps[1:]:
            cp.wait_send()
        mine.wait()

    vm = pl.BlockSpec(memory_space=pltpu.VMEM)
    seven = pltpu.SemaphoreType.DMA((N_DEV - 1,))
    return pl.pallas_call(
        body, name="prologue",
        out_shape=[jax.ShapeDtypeStruct((N_DEV,) + w_in_t.shape, BF16), jax.ShapeDtypeStruct((n_ex + 8, D), F32),
                   jax.ShapeDtypeStruct((N_DEV, n_ex + 8, n_mod), F32)],
        in_specs=[vm] * 5, out_specs=[pl.BlockSpec(memory_space=pl.ANY), vm, vm],
        scratch_shapes=[pltpu.VMEM(w_in_t.shape, BF16), pltpu.VMEM((8, D), F32), pltpu.VMEM((N_DEV, 8, D), F32),
                        pltpu.VMEM((n_ex + 8, n_mod), F32),
                        seven, seven, seven, seven, seven, seven, pltpu.SemaphoreType.DMA((3,))],
        compiler_params=pltpu.CompilerParams(vmem_limit_bytes=VMEM_LIMIT),
    )(w_in_t, c, c_ctx_row, w_mod_loc, b_mod)


def _exchange_copies(in_refs, out_refs, send_sems, recv_sems, local_sems, scatter):
    x, y, c = _coords()
    me = _lin(x, y, c)
    local, remote = [], []
    for a, (src, dst) in enumerate(zip(in_refs, out_refs)):
        local.append(pltpu.make_async_copy(src.at[me] if scatter else src, dst.at[me], local_sems.at[a]))
        for k in range(1, N_DEV):
            peer = (1 - x if k & 4 else x, 1 - y if k & 2 else y, 1 - c if k & 1 else c)
            remote.append(pltpu.make_async_remote_copy(
                src_ref=src.at[_lin(*peer)] if scatter else src, dst_ref=dst.at[me],
                send_sem=send_sems.at[a * (N_DEV - 1) + k - 1], recv_sem=recv_sems.at[a * (N_DEV - 1) + k - 1],
                device_id=peer, device_id_type=MESH_ID))
    return local, remote


def _exchange_scratch(n):
    return [pltpu.SemaphoreType.DMA((n * (N_DEV - 1),)), pltpu.SemaphoreType.DMA((n * (N_DEV - 1),)),
            pltpu.SemaphoreType.DMA((n,))]


def _fwd_in(x, modrows, norm_w, w_in_b, cos, sins, qnw_t, knw_t, ones_bd, k_all, v_all, shards, wire_dtypes):
    bl, s, _ = x.shape
    tm = TOKEN_PARTS * TM
    nt = s // tm
    n_sh = len(shards)

    def body(*refs):
        (x_ref, mod_ref, nw_ref, win_ref, cos_ref, sin_ref, qnw_ref, knw_ref, bd_ref, kin_ref, vin_ref) = refs[:11]
        shard_refs = refs[11:11 + n_sh]
        q_ref, k_ref, v_ref, pq_ref, pkv_ref, za_ref, glu_ref, zc_ref = refs[11 + n_sh:19 + n_sh]
        gathered_refs = refs[19 + n_sh:19 + 2 * n_sh]
        stage_refs = refs[19 + 2 * n_sh:19 + 3 * n_sh]
        send_sems, recv_sems, local_sems = refs[19 + 3 * n_sh:]
        b, i = pl.program_id(0), pl.program_id(1)
        local, remote = _exchange_copies(stage_refs, gathered_refs, send_sems, recv_sems, local_sems, scatter=False)

        @pl.when((b == 0) & (i == 0))
        def _():
            for src, stage in zip(shard_refs, stage_refs):
                stage[...] = src[...].astype(stage.dtype)
            for cp in local + remote:
                cp.start()

        shift = mod_ref[0, 0:1, :]
        scale = mod_ref[0, 1:2, :]
        for part in range(TOKEN_PARTS):
            rows = pl.ds(part * TM, TM)
            xv = x_ref[0, rows, :]
            r = lax.rsqrt(jnp.mean(xv * xv, axis=-1, keepdims=True) + EPS)
            u = (xv * r * nw_ref[...]) * (1.0 + scale) + shift
            p = _dot_nt(u.astype(BF16), win_ref[...])
            pq = p[:, 0:DA]
            pk = p[:, DA:DA + HD * 2]
            ck = cos_ref[rows, :]
            sk = sin_ref[rows, :]
            cs = jnp.concatenate([ck] * (DA // KVW), axis=-1)
            sn = jnp.concatenate([sk] * (DA // KVW), axis=-1)
            rq = lax.rsqrt(_segsum(pq * pq, bd_ref[...]) * (1.0 / HD) + EPS)
            qn = pq * rq * qnw_ref[...]
            qr = _rope(qn, cs, sn) * 0.125
            for h in range(DA // HD):
                q_ref[0, h, rows, :] = qr[:, h * HD:(h + 1) * HD].astype(BF16)
            rk = lax.rsqrt(_segsum(pk * pk, bd_ref[0:KVW, 0:KVW]) * (1.0 / HD) + EPS)
            kn = pk * rk * knw_ref[...]
            kr = _rope(kn, ck, sk)
            pv = p[:, 640:768]
            for h in range(KVW // HD):
                k_ref[0, h, rows, :] = kr[:, h * HD:(h + 1) * HD].astype(BF16)
                v_ref[0, h, rows, :] = _with_ones_column(pv[:, h * HD:(h + 1) * HD]).astype(BF16)
            pq_ref[0, rows, :] = pq
            pkv_ref[0, rows, :] = p[:, 512:768]
            za_ref[0, rows, :] = p[:, 768:1280]
            glu_ref[0, rows, :] = p[:, 1280:2304]
            zc_ref[0, rows, :] = p[:, 2304:2816]

        @pl.when((b == bl - 1) & (i == nt - 1))
        def _():
            for cp in remote:
                cp.wait_recv()
            for cp in remote:
                cp.wait_send()
            for cp in local:
                cp.wait()

    def tile(w):
        return pl.BlockSpec((1, tm, w), lambda b, i: (b, i, 0))

    def const(shape):
        return pl.BlockSpec(shape, lambda b, i: (0,) * len(shape))

    outs = [(DA, F32), (2 * KVW, F32), (DA, F32), (2 * DC, F32), (DC, F32)]
    anyspace = pl.BlockSpec(memory_space=pl.ANY)
    rope = pl.BlockSpec((tm, KVW), lambda b, i: (i, 0))
    k_tile = pl.BlockSpec((1, KVW // HD, tm, HD), lambda b, i: (b, 0, i, 0))
    v_tile = pl.BlockSpec((1, KVW // HD, tm, 2 * HD), lambda b, i: (b, 0, i, 0))
    res = pl.pallas_call(
        body, name="fwd_in", grid=(bl, nt),
        in_specs=[tile(D), pl.BlockSpec((1, 3, D), lambda b, i: (b, 0, 0)), const((1, D)), const((D_IN, D)),
                  rope, rope, const((1, DA)), const((1, KVW)), const((DA, DA)), anyspace, anyspace]
        + [const(a.shape) for a in shards],
        out_specs=[pl.BlockSpec((1, DA // HD, tm, HD), lambda b, i: (b, 0, i, 0)), k_tile, v_tile]
        + [tile(w) for w, _ in outs] + [anyspace] * n_sh,
        out_shape=[jax.ShapeDtypeStruct((bl, DA // HD, s, HD), BF16), jax.ShapeDtypeStruct(k_all.shape, BF16),
                   jax.ShapeDtypeStruct(v_all.shape, BF16)]
        + [jax.ShapeDtypeStruct((bl, s, w), dt) for w, dt in outs]
        + [jax.ShapeDtypeStruct((N_DEV,) + a.shape, dt) for a, dt in zip(shards, wire_dtypes)],
        input_output_aliases={9: 1, 10: 2},
        scratch_shapes=[pltpu.VMEM(a.shape, dt) for a, dt in zip(shards, wire_dtypes)] + _exchange_scratch(n_sh),
        compiler_params=_params(("arbitrary", "arbitrary")),
    )(x, modrows, norm_w, w_in_b, cos, sins, qnw_t, knw_t, ones_bd, k_all, v_all, *shards)
    return res[:8], res[8:]


_KV_ROWS_OF_W_IN_T = pl.BlockSpec((2 * KVW, D), lambda b: (DA // (2 * KVW), 0))


def _ctx_fwd(ctx, modc, norm_w, w_kv_b, knw_t, ones_bd, n_keys):
    bl, cl, _ = ctx.shape

    def body(x_ref, mod_ref, nw_ref, w_ref, knw_ref, bd_ref, k_ref, v_ref, pkv_ref):
        xv = x_ref[0]
        shift = mod_ref[0, 0:1, :]
        scale = mod_ref[0, 1:2, :]
        r = lax.rsqrt(jnp.mean(xv * xv, axis=-1, keepdims=True) + EPS)
        u = (xv * r * nw_ref[...]) * (1.0 + scale) + shift
        p = _dot_nt(u.astype(BF16), w_ref[...])
        pk = p[:, 0:KVW]
        rk = lax.rsqrt(_segsum(pk * pk, bd_ref[...]) * (1.0 / HD) + EPS)
        kn = pk * rk * knw_ref[...]
        pv = p[:, KVW:2 * KVW]
        for h in range(KVW // HD):
            k_ref[0, h] = kn[:, h * HD:(h + 1) * HD].astype(BF16)
            v_ref[0, h] = _with_ones_column(pv[:, h * HD:(h + 1) * HD]).astype(BF16)
        pkv_ref[0] = p

    def const(shape):
        return pl.BlockSpec(shape, lambda b: (0,) * len(shape))

    def tile(w):
        return pl.BlockSpec((1, cl, w), lambda b: (b, 0, 0))

    ctx_block = (n_keys - cl) // cl
    assert ctx_block * cl + cl == n_keys
    k_tile = pl.BlockSpec((1, KVW // HD, cl, HD), lambda b: (b, 0, ctx_block, 0))
    v_tile = pl.BlockSpec((1, KVW // HD, cl, 2 * HD), lambda b: (b, 0, ctx_block, 0))
    return pl.pallas_call(
        body, name="ctx_fwd", grid=(bl,),
        in_specs=[tile(D), const((1, 3, D)), const((1, D)), _KV_ROWS_OF_W_IN_T, const((1, KVW)), const((KVW, KVW))],
        out_specs=[k_tile, v_tile, tile(2 * KVW)],
        out_shape=[jax.ShapeDtypeStruct((bl, KVW // HD, n_keys, HD), BF16),
                   jax.ShapeDtypeStruct((bl, KVW // HD, n_keys, 2 * HD), BF16),
                   jax.ShapeDtypeStruct((bl, cl, 2 * KVW), F32)],
        compiler_params=_params(("arbitrary",)),
    )(ctx, modc, norm_w, w_kv_b, knw_t, ones_bd)


def _attn_fwd(q, k, v1):
    bl, _, s, _ = q.shape
    n_keys = k.shape[2]

    def body(q_ref, k_ref, v_ref, o_ref, lse_ref):
        kv = k_ref[0, 0]
        vv = v_ref[0, 0]
        lane = lax.broadcasted_iota(jnp.int32, (TQ, 2 * HD), 1)
        for part in range(FWD_PARTS):
            rows = pl.ds(part * TQ, TQ)
            lse = jnp.zeros((TQ, 2 * HD), F32)
            heads = []
            sc_all = _dot_nt(q_ref[0, :, rows, :].reshape(4 * TQ, HD), kv)
            for h in range(4):
                sc = sc_all[h * TQ:(h + 1) * TQ, :]
                m = jnp.max(sc, axis=-1, keepdims=True)
                e = jnp.exp(sc - m).astype(BF16)
                ov = _dot(e, vv)
                denom = ov[:, HD:HD + 1]
                heads.append(ov[:, 0:HD] * (1.0 / denom))
                lse = jnp.where(lane == h, m + jnp.log(denom), lse)
            o_ref[0, rows, :] = jnp.concatenate(heads, axis=-1)
            lse_ref[0, 0, rows, :] = lse

    tq = FWD_PARTS * TQ
    ks = pl.BlockSpec((1, 1, n_keys, HD), lambda b, g, i: (b, g, 0, 0))
    qs = pl.BlockSpec((1, 4, tq, HD), lambda b, g, i: (b, g, i, 0))
    vs = pl.BlockSpec((1, 1, n_keys, 2 * HD), lambda b, g, i: (b, g, 0, 0))
    return pl.pallas_call(
        body, name="attn_fwd", grid=(bl, 2, s // tq), in_specs=[qs, ks, vs],
        out_specs=[pl.BlockSpec((1, tq, 4 * HD), lambda b, g, i: (b, i, g)),
                   pl.BlockSpec((1, 1, tq, 2 * HD), lambda b, g, i: (b, g, i, 0))],
        out_shape=[jax.ShapeDtypeStruct((bl, s, DA), F32), jax.ShapeDtypeStruct((bl, 2, s, 2 * HD), F32)],
        compiler_params=_params(("arbitrary", "arbitrary", "arbitrary")),
    )(q, k, v1)


def _attn_bwd(q, k, v1, do, o, lse, exchange):
    bl, _, s, _ = q.shape
    n_keys = k.shape[2]
    tq = BWD_PARTS * TQ
    nq = s // tq
    n_ex = len(exchange)

    def body(*refs):
        q_ref, k_ref, v_ref, do_ref, o_ref, lse_ref = refs[:6]
        part_refs = refs[6:6 + n_ex]
        dq_ref, dk_ref, dv_ref = refs[6 + n_ex:9 + n_ex]
        got_refs = refs[9 + n_ex:9 + 2 * n_ex]
        p_sc, ds_sc, dkt, dvt, send_sems, recv_sems, local_sems = refs[9 + 2 * n_ex:]
        i = pl.program_id(2)
        first = (pl.program_id(0) == 0) & (pl.program_id(1) == 0) & (i == 0)
        last = (pl.program_id(0) == bl - 1) & (pl.program_id(1) == 1) & (i == nq - 1)
        local, remote = _exchange_copies(part_refs, got_refs, send_sems, recv_sems, local_sems, scatter=True)

        @pl.when(first)
        def _():
            for cp in local + remote:
                cp.start()

        @pl.when(i == 0)
        def _():
            dkt[...] = jnp.zeros_like(dkt)
            dvt[...] = jnp.zeros_like(dvt)

        kv = k_ref[0, 0]
        vv = v_ref[0, 0][:, 0:HD]
        for part in range(BWD_PARTS):
            tq_rows = pl.ds(part * TQ, TQ)
            lse = lse_ref[0, 0, tq_rows, :]
            ov = o_ref[0, tq_rows, :]
            dqs = []
            q_cat = q_ref[0, :, tq_rows, :].reshape(4 * TQ, HD)
            do_cat = do_ref[0, :, tq_rows, :].reshape(4 * TQ, HD)
            sc_all = _dot_nt(q_cat, kv)
            for h in range(4):
                doh = do_cat[h * TQ:(h + 1) * TQ, :]
                delta = jnp.sum(ov[:, h * HD:(h + 1) * HD] * doh.astype(F32), axis=-1, keepdims=True)
                rows = pl.ds((part * 4 + h) * TQ, TQ)
                p = jnp.exp(sc_all[h * TQ:(h + 1) * TQ, :] - lse[:, h:h + 1])
                ds = (p * (_dot_nt(doh, vv) - delta)).astype(BF16)
                p_sc[rows, :] = p.astype(BF16)
                ds_sc[rows, :] = ds
                dqs.append(_dot(ds, kv) * 0.125)
            dq_ref[0, tq_rows, :] = jnp.concatenate(dqs, axis=-1)
            part_rows = pl.ds(part * 4 * TQ, 4 * TQ)
            dvt[...] += _dot_tn(do_cat, p_sc[part_rows, :])
            dkt[...] += _dot_tn(q_cat, ds_sc[part_rows, :])

        @pl.when(i == nq - 1)
        def _():
            dk_ref[0, 0] = dkt[...].T
            dv_ref[0, 0] = dvt[...].T

        @pl.when(last)
        def _():
            for cp in remote:
                cp.wait_recv()
            for cp in remote:
                cp.wait_send()
            for cp in local:
                cp.wait()

    qs = pl.BlockSpec((1, 4, tq, HD), lambda b, g, i: (b, g, i, 0))
    ks = pl.BlockSpec((1, 1, n_keys, HD), lambda b, g, i: (b, g, 0, 0))
    vs = pl.BlockSpec((1, 1, n_keys, 2 * HD), lambda b, g, i: (b, g, 0, 0))
    os_ = pl.BlockSpec((1, tq, 4 * HD), lambda b, g, i: (b, i, g))
    kshape = jax.ShapeDtypeStruct(k.shape, F32)
    anyspace = pl.BlockSpec(memory_space=pl.ANY)
    res = pl.pallas_call(
        body, name="attn_bwd", grid=(bl, 2, nq),
        in_specs=[qs, ks, vs, qs, os_, pl.BlockSpec((1, 1, tq, 2 * HD), lambda b, g, i: (b, g, i, 0))]
        + [anyspace] * n_ex,
        out_specs=[os_, ks, ks] + [anyspace] * n_ex,
        out_shape=[jax.ShapeDtypeStruct((bl, s, DA), F32), kshape, kshape]
        + [jax.ShapeDtypeStruct(a.shape, a.dtype) for a in exchange],
        scratch_shapes=[pltpu.VMEM((4 * tq, n_keys), BF16), pltpu.VMEM((4 * tq, n_keys), BF16),
                        pltpu.VMEM((HD, n_keys), F32), pltpu.VMEM((HD, n_keys), F32)] + _exchange_scratch(n_ex),
        compiler_params=_params(("arbitrary", "arbitrary", "arbitrary")),
    )(q, k, v1, do, o, lse, *exchange)
    return res[:3], res[3:]


def _halo_specs(width, s):
    per = TC // HALO
    last = s // HALO - 1
    main = pl.BlockSpec((1, TC, width), lambda b, i: (b, i, 0))
    prev = pl.BlockSpec((1, HALO, width), lambda b, i: (b, jnp.maximum(i * per - 1, 0), 0))
    nxt = pl.BlockSpec((1, HALO, width), lambda b, i: (b, jnp.minimum((i + 1) * per, last), 0))
    return main, prev, nxt


def _glu(g):
    return g[:, 0:DC] * _sigmoid(g[:, DC:2 * DC])


def _fill_padded(pad_ref, main, prev, nxt, first, last):
    tc = main.shape[0]
    pad_ref[0:HALO, :] = jnp.where(first, 0.0, prev)
    pad_ref[HALO:HALO + tc, :] = main
    pad_ref[HALO + tc:2 * HALO + tc, :] = jnp.where(last, 0.0, nxt)


PLANE_ROWS = TC + 2 * HALO - 8


def _shift_planes(pad_ref, planes_ref):
    for r in range(1, 8):
        planes_ref[r - 1] = pad_ref[pl.ds(r, planes_ref.shape[1]), :]


def _tap_rows(pad_ref, planes_ref, offset, start, n):
    a, r = divmod(offset, 8)
    if r == 0:
        return pad_ref[pl.ds(start + 8 * a, n), :]
    return planes_ref[r - 1, pl.ds(start + 8 * a, n), :]


def _conv_fwd(glu, conv_w, conv_b, ln_w, ln_b, w_pw_b, b_pw):
    bl, s, _ = glu.shape
    nt = s // TC

    def body(g_ref, gp_ref, gn_ref, cw_ref, cb_ref, lw_ref, lb_ref, wpw_ref, bpw_ref, y_ref, cp_ref, pad_ref, planes_ref):
        i = pl.program_id(1)
        _fill_padded(pad_ref, _glu(g_ref[0]), _glu(gp_ref[0]), _glu(gn_ref[0]), i == 0, i == nt - 1)
        _shift_planes(pad_ref, planes_ref)
        for ck in range(TC // CH):
            acc = jnp.zeros((CH, DC), F32) + cb_ref[...]
            for t in range(KW):
                acc = acc + _tap_rows(pad_ref, planes_ref, 1 + t, ck * CH, CH) * cw_ref[t:t + 1, :]
            y_ref[0, pl.ds(ck * CH, CH), :] = acc
        y = y_ref[0]
        mu = jnp.mean(y, axis=-1, keepdims=True)
        yc = y - mu
        var = jnp.mean(yc * yc, axis=-1, keepdims=True)
        z = yc * lax.rsqrt(var + EPS) * lw_ref[...] + lb_ref[...]
        act = z * _sigmoid(z)
        cp_ref[0] = _dot(act.astype(BF16), wpw_ref[...]) + bpw_ref[...]

    def const(shape):
        return pl.BlockSpec(shape, lambda b, i: (0,) * len(shape))

    main, prev, nxt = _halo_specs(2 * DC, s)
    tile = pl.BlockSpec((1, TC, DC), lambda b, i: (b, i, 0))
    return pl.pallas_call(
        body, name="conv_fwd", grid=(bl, nt),
        in_specs=[main, prev, nxt, const((32, DC)), const((1, DC)), const((1, DC)), const((1, DC)),
                  const((DC, DC)), const((1, DC))],
        out_specs=[tile, tile],
        out_shape=[jax.ShapeDtypeStruct((bl, s, DC), F32)] * 2,
        scratch_shapes=[pltpu.VMEM((TC + 2 * HALO, DC), F32), pltpu.VMEM((7, PLANE_ROWS, DC), F32)],
        compiler_params=_params(("arbitrary", "arbitrary")),
    )(glu, glu, glu, conv_w, conv_b, ln_w, ln_b, w_pw_b, b_pw)


def _conv_bwd_depthwise(glu, dy, conv_w):
    bl, s, _ = glu.shape
    nt = s // TC

    def body(g_ref, gp_ref, gn_ref, d_ref, dp_ref, dn_ref, cw_ref, dglu_ref, dcw_ref,
             padu_ref, padd_ref, planes_u, planes_d):
        i = pl.program_id(1)

        @pl.when((pl.program_id(0) == 0) & (i == 0))
        def _():
            dcw_ref[...] = jnp.zeros_like(dcw_ref)

        first, last = i == 0, i == nt - 1
        _fill_padded(padu_ref, _glu(g_ref[0]), _glu(gp_ref[0]), _glu(gn_ref[0]), first, last)
        _fill_padded(padd_ref, d_ref[0], dp_ref[0], dn_ref[0], first, last)
        _shift_planes(padu_ref, planes_u)
        _shift_planes(padd_ref, planes_d)
        for ck in range(TC // CH):
            acc = jnp.zeros((CH, DC), F32)
            for t in range(KW):
                acc = acc + _tap_rows(padd_ref, planes_d, 2 * HALO - 1 - t, ck * CH, CH) * cw_ref[t:t + 1, :]
            g = g_ref[0, pl.ds(ck * CH, CH), :]
            a = g[:, 0:DC]
            sg = _sigmoid(g[:, DC:2 * DC])
            dglu_ref[0, pl.ds(ck * CH, CH), 0:DC] = (acc * sg).astype(BF16)
            dglu_ref[0, pl.ds(ck * CH, CH), DC:2 * DC] = (acc * a * sg * (1.0 - sg)).astype(BF16)
        group = 4
        for t0 in range(0, KW, group):
            taps = range(t0, min(t0 + group, KW))
            acc8 = [jnp.zeros((8, DC), F32) for _ in taps]
            for ck in range(TC // CH):
                dchunk = d_ref[0, pl.ds(ck * CH, CH), :]
                for n, t in enumerate(taps):
                    prod = _tap_rows(padu_ref, planes_u, 1 + t, ck * CH, CH) * dchunk
                    acc8[n] = acc8[n] + jnp.sum(prod.reshape(CH // 8, 8, DC), axis=0)
            for n, t in enumerate(taps):
                dcw_ref[t:t + 1, :] += jnp.sum(acc8[n], axis=0, keepdims=True)

    gmain, gprev, gnext = _halo_specs(2 * DC, s)
    dmain, dprev, dnext = _halo_specs(DC, s)
    cw = pl.BlockSpec((32, DC), lambda b, i: (0, 0))
    return pl.pallas_call(
        body, name="conv_bwd_depthwise", grid=(bl, nt),
        in_specs=[gmain, gprev, gnext, dmain, dprev, dnext, cw],
        out_specs=[gmain, cw],
        out_shape=[jax.ShapeDtypeStruct((bl, s, 2 * DC), BF16), jax.ShapeDtypeStruct((32, DC), F32)],
        scratch_shapes=[pltpu.VMEM((TC + 2 * HALO, DC), F32)] * 2 + [pltpu.VMEM((7, PLANE_ROWS, DC), F32)] * 2,
        compiler_params=_params(("arbitrary", "arbitrary")),
    )(glu, glu, glu, dy, dy, dy, conv_w)


def _out_fwd_bwd(attn, za, cp, zc, x, target, modrows, w_out_b, y_conv, ln_w, ln_b, w_pw_b):
    bl, s, _ = x.shape
    tm = OUT_TM

    def body(o_ref, za_ref, cp_ref, zc_ref, x_ref, t_ref, mod_ref, w_ref, y_ref, lw_ref, lb_ref, wpw_ref,
             do_ref, dza_ref, dy_ref, dzc_ref, dh_ref, dgate_ref, gwb_ref, loss_ref, gpwb_ref, rows_ref,
             gw_ref, gpw_ref):
        b, i = pl.program_id(0), pl.program_id(1)

        @pl.when((b == 0) & (i == 0))
        def _():
            gw_ref[...] = jnp.zeros_like(gw_ref)
            gpw_ref[...] = jnp.zeros_like(gpw_ref)
            rows_ref[...] = jnp.zeros_like(rows_ref)
            loss_ref[...] = jnp.zeros_like(loss_ref)

        @pl.when(i == 0)
        def _():
            dgate_ref[...] = jnp.zeros_like(dgate_ref)

        gate = mod_ref[0, 2:3, :]
        w = w_ref[...]
        o, za_v, cp_v, zc_v = o_ref[0], za_ref[0], cp_ref[0], zc_ref[0]
        sa = _sigmoid(za_v)
        sc = _sigmoid(zc_v)
        silu_a = za_v * sa
        silu_c = zc_v * sc
        mix = jnp.concatenate([(o * silu_a).astype(BF16), (cp_v * silu_c).astype(BF16)], axis=-1)
        out = _dot(mix, w)
        err = x_ref[0] + gate * out - t_ref[0]
        loss_ref[...] += jnp.sum(err * err, axis=0, keepdims=True)
        dh = err * (1.0 / D)
        dh_ref[0] = dh
        dgate_ref[0] += jnp.sum(dh * out, axis=0, keepdims=True)
        dout = (dh * gate).astype(BF16)
        gw_ref[...] += _dot_tn(mix, dout)
        dmix = _dot_nt(dout, w)
        dga = dmix[:, 0:DA]
        dgc = dmix[:, DA:DA + DC]
        dov = dga * silu_a
        for h in range(DA // HD):
            do_ref[0, h] = dov[:, h * HD:(h + 1) * HD].astype(BF16)
        dza_ref[0] = (dga * o * (sa * (1.0 + za_v * (1.0 - sa)))).astype(BF16)
        dzc_ref[0] = (dgc * cp_v * (sc * (1.0 + zc_v * (1.0 - sc)))).astype(BF16)
        dcp = dgc * silu_c
        y = y_ref[0]
        yc = y - jnp.mean(y, axis=-1, keepdims=True)
        rstd = lax.rsqrt(jnp.mean(yc * yc, axis=-1, keepdims=True) + EPS)
        yn = yc * rstd
        lw = lw_ref[...]
        z = yn * lw + lb_ref[...]
        sg = _sigmoid(z)
        dcp_b = dcp.astype(BF16)
        gpw_ref[...] += _dot_tn((z * sg).astype(BF16), dcp_b)
        dz = _dot_nt(dcp_b, wpw_ref[...]) * (sg * (1.0 + z * (1.0 - sg)))
        dyn = dz * lw
        dy = rstd * (dyn - jnp.mean(dyn, axis=-1, keepdims=True) - yn * jnp.mean(dyn * yn, axis=-1, keepdims=True))
        dy_ref[0] = dy
        rows_ref[0:1, :] += jnp.sum(dcp, axis=0, keepdims=True)
        rows_ref[1:2, :] += jnp.sum(dz * yn, axis=0, keepdims=True)
        rows_ref[2:3, :] += jnp.sum(dz, axis=0, keepdims=True)
        rows_ref[3:4, :] += jnp.sum(dy, axis=0, keepdims=True)

        @pl.when((b == bl - 1) & (i == s // tm - 1))
        def _():
            gwb_ref[...] = gw_ref[...].astype(BF16)
            gpwb_ref[...] = gpw_ref[...].astype(BF16)

    def const(shape):
        return pl.BlockSpec(shape, lambda b, i: (0,) * len(shape))

    def tile(w):
        return pl.BlockSpec((1, tm, w), lambda b, i: (b, i, 0))

    return pl.pallas_call(
        body, name="out_fwd_bwd", grid=(bl, s // tm),
        in_specs=[tile(DA), tile(DA), tile(DC), tile(DC), tile(D), tile(D),
                  pl.BlockSpec((1, 3, D), lambda b, i: (b, 0, 0)), const((D, D)),
                  tile(DC), const((1, DC)), const((1, DC)), const((DC, DC))],
        out_specs=[pl.BlockSpec((1, DA // HD, tm, HD), lambda b, i: (b, 0, i, 0)), tile(DA), tile(DC), tile(DC), tile(D),
                   pl.BlockSpec((1, 1, D), lambda b, i: (b, 0, 0)), const((D, D)), const((1, D)),
                   const((DC, DC)), const((8, DC))],
        out_shape=[jax.ShapeDtypeStruct((bl, DA // HD, s, HD), BF16), jax.ShapeDtypeStruct((bl, s, DA), BF16),
                   jax.ShapeDtypeStruct((bl, s, DC), F32), jax.ShapeDtypeStruct((bl, s, DC), BF16),
                   jax.ShapeDtypeStruct((bl, s, D), F32), jax.ShapeDtypeStruct((bl, 1, D), F32),
                   jax.ShapeDtypeStruct((D, D), BF16), jax.ShapeDtypeStruct((1, D), F32),
                   jax.ShapeDtypeStruct((DC, DC), BF16), jax.ShapeDtypeStruct((8, DC), F32)],
        scratch_shapes=[pltpu.VMEM((D, D), F32), pltpu.VMEM((DC, DC), F32)],
        compiler_params=_params(("arbitrary", "arbitrary")),
    )(attn, za, cp, zc, x, target, modrows, w_out_b, y_conv, ln_w, ln_b, w_pw_b)


def _rms_heads_bwd(dy, x, w_t, ones_bd):
    r = lax.rsqrt(_segsum(x * x, ones_bd) * (1.0 / HD) + EPS)
    xh = x * r
    g = dy * w_t
    dx = r * (g - xh * (_segsum(g * xh, ones_bd) * (1.0 / HD)))
    return dx, dy * xh


def _ctx_bwd(ctx, modc, norm_w, w_kv_b, pkv_c, dk_c, dv_c, knw_t, ones_bd):
    bl, cl, _ = ctx.shape

    def body(x_ref, mod_ref, nw_ref, w_ref, p_ref, dk_ref, dv_ref, knw_ref, bd_ref, gw_ref, rows_ref, dknw_ref):
        @pl.when(pl.program_id(0) == 0)
        def _():
            gw_ref[...] = jnp.zeros_like(gw_ref)
            rows_ref[...] = jnp.zeros_like(rows_ref)
            dknw_ref[...] = jnp.zeros_like(dknw_ref)

        xv = x_ref[0]
        shift = mod_ref[0, 0:1, :]
        scale = mod_ref[0, 1:2, :]
        nw = nw_ref[...]
        r = lax.rsqrt(jnp.mean(xv * xv, axis=-1, keepdims=True) + EPS)
        xn = xv * r
        yv = xn * nw
        u = yv * (1.0 + scale) + shift
        dkv = jnp.concatenate([dk_ref[0, 0], dk_ref[0, 1]], axis=-1)
        dpk, dknw = _rms_heads_bwd(dkv, p_ref[0][:, 0:KVW], knw_ref[...], bd_ref[...])
        dp = jnp.concatenate([dpk.astype(BF16), dv_ref[0, 0].astype(BF16), dv_ref[0, 1].astype(BF16)], axis=-1)
        gw_ref[...] += _dot_tn(dp, u.astype(BF16))
        du = _dot(dp, w_ref[...])
        rows_ref[0:1, :] += jnp.sum(du, axis=0, keepdims=True)
        rows_ref[1:2, :] += jnp.sum(du * yv, axis=0, keepdims=True)
        rows_ref[2:3, :] += jnp.sum(du * (1.0 + scale) * xn, axis=0, keepdims=True)
        dknw_ref[...] += jnp.sum(dknw, axis=0, keepdims=True)

    def const(shape):
        return pl.BlockSpec(shape, lambda b: (0,) * len(shape))

    def tile(w):
        return pl.BlockSpec((1, cl, w), lambda b: (b, 0, 0))

    ctx_block = (dk_c.shape[2] - cl) // cl
    kv_tile = pl.BlockSpec((1, KVW // HD, cl, HD), lambda b: (b, 0, ctx_block, 0))
    return pl.pallas_call(
        body, name="ctx_bwd", grid=(bl,),
        in_specs=[tile(D), const((1, 3, D)), const((1, D)), _KV_ROWS_OF_W_IN_T, tile(2 * KVW), kv_tile, kv_tile,
                  const((1, KVW)), const((KVW, KVW))],
        out_specs=[const((2 * KVW, D)), const((8, D)), const((1, KVW))],
        out_shape=[jax.ShapeDtypeStruct((2 * KVW, D), F32), jax.ShapeDtypeStruct((8, D), F32),
                   jax.ShapeDtypeStruct((1, KVW), F32)],
        compiler_params=_params(("arbitrary",)),
    )(ctx, modc, norm_w, w_kv_b, pkv_c, dk_c, dv_c, knw_t, ones_bd)


def _bwd_in(x, modrows, norm_w, w_in_b, cos, sins, qnw_t, knw_t, ones_bd,
            pq, pkv, dq, dk, dv, dza, dglu, dzc, dh, gw_kv):
    bl, s, _ = x.shape
    tm = TOKEN_PARTS * TM
    nt = s // tm

    def body(x_ref, mod_ref, nw_ref, win_hbm, cos_ref, sin_ref, qnw_ref, knw_ref, bd_ref,
             pq_ref, pkv_ref, dq_ref, dk_ref, dv_ref, dza_ref, dglu_ref, dzc_ref, dh_ref, gwkv_ref,
             gx_ref, gw_hbm, dmod_ref, dnw_ref, dqnw_ref, dknw_ref, win_ref, gw_acc, sem):
        b, i = pl.program_id(0), pl.program_id(1)

        @pl.when((b == 0) & (i == 0))
        def _():
            cp = pltpu.make_async_copy(win_hbm, win_ref, sem)
            cp.start()
            gw_acc[...] = jnp.zeros_like(gw_acc)
            dnw_ref[...] = jnp.zeros_like(dnw_ref)
            dqnw_ref[...] = jnp.zeros_like(dqnw_ref)
            dknw_ref[...] = jnp.zeros_like(dknw_ref)
            cp.wait()

        @pl.when(i == 0)
        def _():
            dmod_ref[...] = jnp.zeros_like(dmod_ref)

        bd = bd_ref[...]
        shift = mod_ref[0, 0:1, :]
        scale = mod_ref[0, 1:2, :]
        nw = nw_ref[...]
        dps, us = [], []
        for part in range(TOKEN_PARTS):
            rows = pl.ds(part * TM, TM)
            ck = cos_ref[rows, :]
            sk = sin_ref[rows, :]
            cs = jnp.concatenate([ck] * (DA // KVW), axis=-1)
            sn = jnp.concatenate([sk] * (DA // KVW), axis=-1)
            dqn = _rope_bwd(dq_ref[0, rows, :], cs, sn)
            dpq, dqnw = _rms_heads_bwd(dqn, pq_ref[0, rows, :], qnw_ref[...], bd)
            dkn = _rope_bwd(jnp.concatenate([dk_ref[0, 0, rows, :], dk_ref[0, 1, rows, :]], axis=-1), ck, sk)
            dpk, dknw = _rms_heads_bwd(dkn, pkv_ref[0, rows, 0:KVW], knw_ref[...], bd[0:KVW, 0:KVW])
            dqnw_ref[...] += jnp.sum(dqnw, axis=0, keepdims=True)
            dknw_ref[...] += jnp.sum(dknw, axis=0, keepdims=True)
            dp = jnp.concatenate(
                [dpq.astype(BF16), dpk.astype(BF16), dv_ref[0, 0, rows, :].astype(BF16), dv_ref[0, 1, rows, :].astype(BF16),
                 dza_ref[0, rows, :], dglu_ref[0, rows, :], dzc_ref[0, rows, :]], axis=-1)

            xv = x_ref[0, rows, :]
            r = lax.rsqrt(jnp.mean(xv * xv, axis=-1, keepdims=True) + EPS)
            xn = xv * r
            yv = xn * nw
            u = yv * (1.0 + scale) + shift
            dps.append(dp)
            us.append(u.astype(BF16))
            du = _dot(dp, win_ref[...])
            dmod_ref[0, 0:1, :] += jnp.sum(du, axis=0, keepdims=True)
            dmod_ref[0, 1:2, :] += jnp.sum(du * yv, axis=0, keepdims=True)
            dy = du * (1.0 + scale)
            dnw_ref[...] += jnp.sum(dy * xn, axis=0, keepdims=True)
            dxn = dy * nw
            gx_ref[0, rows, :] = dh_ref[0, rows, :] + r * (dxn - xn * jnp.mean(dxn * xn, axis=-1, keepdims=True))
        gw_acc[...] += _dot_tn(jnp.concatenate(dps, axis=0), jnp.concatenate(us, axis=0))

        @pl.when((b == bl - 1) & (i == nt - 1))
        def _():
            gw_acc[DA:DA + 2 * KVW, :] += gwkv_ref[...]

            def to_bf16(j, carry):
                rows = pl.ds(pl.multiple_of(j * 2 * KVW, 2 * KVW), 2 * KVW)
                win_ref[rows, :] = gw_acc[rows, :].astype(BF16)
                return carry

            lax.fori_loop(0, D_IN // (2 * KVW), to_bf16, 0)
            pltpu.sync_copy(win_ref, gw_hbm)

    def tile(w):
        return pl.BlockSpec((1, tm, w), lambda b, i: (b, i, 0))

    def const(shape):
        return pl.BlockSpec(shape, lambda b, i: (0,) * len(shape))

    anyspace = pl.BlockSpec(memory_space=pl.ANY)
    rope = pl.BlockSpec((tm, KVW), lambda b, i: (i, 0))
    kv_tile = pl.BlockSpec((1, KVW // HD, tm, HD), lambda b, i: (b, 0, i, 0))
    return pl.pallas_call(
        body, name="bwd_in", grid=(bl, nt),
        in_specs=[tile(D), pl.BlockSpec((1, 3, D), lambda b, i: (b, 0, 0)), const((1, D)), anyspace, rope, rope,
                  const((1, DA)), const((1, KVW)), const((DA, DA)),
                  tile(DA), tile(2 * KVW), tile(DA), kv_tile, kv_tile, tile(DA), tile(2 * DC), tile(DC), tile(D),
                  const((2 * KVW, D))],
        out_specs=[tile(D), anyspace, pl.BlockSpec((1, 2, D), lambda b, i: (b, 0, 0)), const((1, D)),
                   const((1, DA)), const((1, KVW))],
        out_shape=[jax.ShapeDtypeStruct((bl, s, D), F32), jax.ShapeDtypeStruct((D_IN, D), BF16),
                   jax.ShapeDtypeStruct((bl, 2, D), F32), jax.ShapeDtypeStruct((1, D), F32),
                   jax.ShapeDtypeStruct((1, DA), F32), jax.ShapeDtypeStruct((1, KVW), F32)],
        scratch_shapes=[pltpu.VMEM((D_IN, D), BF16), pltpu.VMEM((D_IN, D), F32), pltpu.SemaphoreType.DMA],
        compiler_params=_params(("arbitrary", "arbitrary")),
    )(x, modrows, norm_w, w_in_b, cos, sins, qnw_t, knw_t, ones_bd,
      pq, pkv, dq, dk, dv, dza, dglu, dzc, dh, gw_kv)


_LOSS, _DMODC, _NW, _QN, _KN, _CB, _LW, _LB, _BPW, SMALL_W = 0, 1024, 4096, 5120, 5248, 5376, 5888, 6400, 6912, 7424


ROW_W = 1792


def _put_flat(ref, off, value):
    n, done = value.shape[1], 0
    while done < n:
        r, c = divmod(off + done, ROW_W)
        take = min(n - done, ROW_W - c)
        ref[r:r + 1, c:c + take] = value[:, done:done + take]
        done += take


def _get_flat(arr, off, n):
    parts, done = [], 0
    while done < n:
        r, c = divmod(off + done, ROW_W)
        take = min(n - done, ROW_W - c)
        parts.append(arr[r:r + 1, c:c + take])
        done += take
    return parts[0] if len(parts) == 1 else jnp.concatenate(parts, axis=-1)


def _pack_small_body(loss_ref, ctx_ref, dnw_ref, dqnw_ref, dknw_ref, dknwc_ref, conv_ref, dss_ref, dgate_ref, o_ref):
    bl = dss_ref.shape[0]
    assert SMALL_W + bl * 3 * D <= 8 * ROW_W
    o_ref[...] = jnp.zeros_like(o_ref)
    _put_flat(o_ref, _LOSS, loss_ref[...])
    _put_flat(o_ref, _DMODC, ctx_ref[0:1, :])
    _put_flat(o_ref, _DMODC + D, ctx_ref[1:2, :])
    _put_flat(o_ref, _NW, dnw_ref[...] + ctx_ref[2:3, :])
    dq = dqnw_ref[...]
    qn = dq[:, 0:HD]
    for h in range(1, DA // HD):
        qn = qn + dq[:, h * HD:(h + 1) * HD]
    _put_flat(o_ref, _QN, qn)
    dk = dknw_ref[...] + dknwc_ref[...]
    _put_flat(o_ref, _KN, dk[:, 0:HD] + dk[:, HD:2 * HD])
    _put_flat(o_ref, _BPW, conv_ref[0:1, :])
    _put_flat(o_ref, _LW, conv_ref[1:2, :])
    _put_flat(o_ref, _LB, conv_ref[2:3, :])
    _put_flat(o_ref, _CB, conv_ref[3:4, :])
    for b in range(bl):
        _put_flat(o_ref, SMALL_W + b * 3 * D, dss_ref[b, 0:1, :])
        _put_flat(o_ref, SMALL_W + b * 3 * D + D, dss_ref[b, 1:2, :])
        _put_flat(o_ref, SMALL_W + b * 3 * D + 2 * D, dgate_ref[b])


_SMALL = (("b_mod", None), ("norm_w", _NW), ("q_norm_w", _QN), ("k_norm_w", _KN), ("conv_b", _CB),
          ("conv_ln_w", _LW), ("conv_ln_b", _LB), ("b_pw", _BPW), ("c_ctx", None))


def _epilogue(parts_in, pieces, c_rows, w_mod_loc):
    bl = pieces[7].shape[0]
    n_ex = N_DEV * bl
    n_mod = w_mod_loc.shape[1]
    rb = 32
    shp = parts_in.shape[1:]
    rows_in = shp[0]

    def body(*refs):
        it = iter(refs)
        take = lambda k: [next(it) for _ in range(k)]
        (parts,) = take(1)
        piece_refs = take(9)
        (c_ref, wm_ref) = take(2)
        (g_in, g_wm, sum_ref, gb_ref, gc_all, loss_ref) = take(6)
        (mine, got_sib, stage, got_chip, payload, gathered, dmod_full, gc_mine) = take(8)
        (d2d_send, d2d_recv, ici_send, ici_recv, local_sems, sg_send, sg_recv, gc_send, gc_recv, misc_sems) = take(10)

        x, y, c = _coords()
        me = _lin(x, y, c)
        sib = (x, y, 1 - c)
        home = 2 * x + y

        def rows_loop(fn):
            def step(i, carry):
                fn(pl.ds(pl.multiple_of(i * rb, rb), rb))
                return carry
            lax.fori_loop(0, rows_in // rb, step, 0)

        def direct_gather(src, dst, send_sems, recv_sems, local_sem):
            cps = [pltpu.make_async_copy(src, dst.at[me], local_sem)]
            for k in range(1, N_DEV):
                peer = (1 - x if k & 4 else x, 1 - y if k & 2 else y, 1 - c if k & 1 else c)
                cps.append(pltpu.make_async_remote_copy(
                    src_ref=src, dst_ref=dst.at[me], send_sem=send_sems.at[k - 1], recv_sem=recv_sems.at[k - 1],
                    device_id=peer, device_id_type=MESH_ID))
            for cp in cps:
                cp.start()
            return cps

        _pack_small_body(*piece_refs, payload)
        small_cps = direct_gather(payload, gathered, sg_send, sg_recv, misc_sems.at[0])

        local, d2d, ici = [], [], []
        for s in range(4):
            cp = pltpu.make_async_copy(parts.at[_lin(s // 2, s % 2, c)], mine.at[s], local_sems.at[s])
            cp.start()
            local.append(cp)
            rc = pltpu.make_async_remote_copy(
                src_ref=parts.at[_lin(s // 2, s % 2, 1 - c)], dst_ref=got_sib.at[s],
                send_sem=d2d_send.at[s], recv_sem=d2d_recv.at[s], device_id=sib, device_id_type=MESH_ID)
            rc.start()
            d2d.append(rc)

        for cp in small_cps[1:]:
            cp.wait_recv()
        small_cps[0].wait()
        tot = gathered[0]
        for j in range(1, N_DEV):
            tot = tot + gathered[j]
        summed = _get_flat(tot, 0, SMALL_W)
        dmod_full[...] = jnp.zeros_like(dmod_full)
        for j in range(N_DEV):
            arr = gathered[j]
            for b in range(bl):
                dmod_full[j * bl + b:j * bl + b + 1, :] = _get_flat(arr, SMALL_W + b * 3 * D, 3 * D)
        dmod_full[n_ex:n_ex + 1, :] = summed[:, _DMODC:_DMODC + 3 * D]
        sum_ref[...] = summed
        gb_ref[...] = jnp.sum(dmod_full[...], axis=0, keepdims=True)
        loss_ref[...] = (0.5 / D) * jnp.sum(summed[:, _LOSS:_LOSS + D], axis=-1, keepdims=True)

        north = c == 1
        first = (jnp.where(north, 1 - x, x), jnp.where(north, y, 1 - y))
        second = (jnp.where(north, x, 1 - x), jnp.where(north, 1 - y, y))
        for s in range(4):
            local[s].wait()
            d2d[s].wait_recv()

        def chip_sum(k, chip, relayed):
            slot = 2 * chip[0] + chip[1]

            def pair_sum(rs):
                acc = mine[slot, rs, :].astype(F32) + got_sib[slot, rs, :].astype(F32)
                if relayed:
                    acc = acc + got_chip[1, rs, :].astype(F32)
                stage[k, rs, :] = acc.astype(BF16)

            rows_loop(pair_sum)

        def send(k, to):
            rc = pltpu.make_async_remote_copy(
                src_ref=stage.at[k], dst_ref=got_chip.at[k], send_sem=ici_send.at[k], recv_sem=ici_recv.at[k],
                device_id=(to[0], to[1], c), device_id_type=MESH_ID)
            rc.start()
            ici.append(rc)

        chip_sum(0, first, False)
        send(0, first)
        chip_sum(1, (1 - x, 1 - y), False)
        send(1, first)

        cr = c_ref[...]
        act = (cr * _sigmoid(cr)).astype(BF16)
        dm = dmod_full[:, pl.ds(pl.multiple_of(me * n_mod, 128), n_mod)].astype(BF16)
        g_wm[...] = _dot_tn(act, dm)
        gc_mine[...] = _dot_nt(dm[n_ex:n_ex + 8, :], wm_ref[...].astype(BF16))
        gc_cps = direct_gather(gc_mine, gc_all, gc_send, gc_recv, misc_sems.at[1])

        ici[1].wait_recv()
        chip_sum(2, second, True)
        send(2, second)
        ici[0].wait_recv()
        ici[2].wait_recv()

        def finish(rs):
            gsum = mine[home, rs, :].astype(F32) + got_sib[home, rs, :].astype(F32)
            g_in[rs, :] = gsum + got_chip[0, rs, :].astype(F32) + got_chip[2, rs, :].astype(F32)

        rows_loop(finish)

        for cp in gc_cps[1:]:
            cp.wait_recv()
        gc_cps[0].wait()
        for rc in d2d + ici + small_cps[1:] + gc_cps[1:]:
            rc.wait_send()

    vm = pl.BlockSpec(memory_space=pltpu.VMEM)
    anyspace = pl.BlockSpec(memory_space=pl.ANY)
    assert rows_in % rb == 0 and parts_in.dtype == BF16
    args = [parts_in, *pieces, c_rows, w_mod_loc]
    in_specs = [anyspace] + [vm] * (len(args) - 1)
    out_shape = [jax.ShapeDtypeStruct(shp, F32), jax.ShapeDtypeStruct(w_mod_loc.shape, F32),
                 jax.ShapeDtypeStruct((1, SMALL_W), F32), jax.ShapeDtypeStruct((1, 3 * D), F32),
                 jax.ShapeDtypeStruct((N_DEV, 8, D), F32), jax.ShapeDtypeStruct((1, 1), F32)]
    scratch = [pltpu.VMEM((4,) + shp, BF16), pltpu.VMEM((4,) + shp, BF16), pltpu.VMEM((3,) + shp, BF16),
               pltpu.VMEM((3,) + shp, BF16), pltpu.VMEM((8, ROW_W), F32), pltpu.VMEM((N_DEV, 8, ROW_W), F32),
               pltpu.VMEM((n_ex + 8, 3 * D), F32), pltpu.VMEM((8, D), F32),
               pltpu.SemaphoreType.DMA((4,)), pltpu.SemaphoreType.DMA((4,)), pltpu.SemaphoreType.DMA((3,)),
               pltpu.SemaphoreType.DMA((3,)), pltpu.SemaphoreType.DMA((4,)),
               pltpu.SemaphoreType.DMA((N_DEV - 1,)), pltpu.SemaphoreType.DMA((N_DEV - 1,)),
               pltpu.SemaphoreType.DMA((N_DEV - 1,)), pltpu.SemaphoreType.DMA((N_DEV - 1,)),
               pltpu.SemaphoreType.DMA((2,))]
    return pl.pallas_call(
        body, name="epilogue", out_shape=out_shape, in_specs=in_specs, out_specs=[vm] * len(out_shape),
        scratch_shapes=scratch, compiler_params=pltpu.CompilerParams(vmem_limit_bytes=VMEM_LIMIT),
    )(*args)


def _final_adamw(tiled, streamed_sums, local_sums, summed, g_bmod, gc_all, small_w, small_m, small_v):
    ns = len(_SMALL)
    n_tiled, n_sums, n_local = len(tiled), len(streamed_sums), len(local_sums)
    streams = list(tiled) + list(streamed_sums)
    n_results = [3] * n_tiled + [4] * n_sums
    tile_jobs = [[(a, t * (w.shape[0] // ADAM_STEPS), w.shape[0] // ADAM_STEPS) for a, (_, w, _, _) in enumerate(tiled)]
                 for t in range(ADAM_STEPS)]
    sum_jobs = [(n_tiled + a, 0, w.shape[0]) for a, (_, w, _, _) in enumerate(streamed_sums)]
    jobs = sum(tile_jobs[:ADAM_STEPS // 2], []) + sum_jobs + sum(tile_jobs[ADAM_STEPS // 2:], [])
    for _, w, _, _ in tiled:
        assert w.shape[0] % (8 * ADAM_STEPS) == 0
    for got, w, _, _ in list(streamed_sums) + list(local_sums):
        assert got.shape[0] == N_DEV and got.shape[1] >= w.shape[0] and got.shape[2:] == w.shape[1:]
    n_reads = 4 * len(jobs)
    n_writes = sum(n_results[a] for a, _, _ in jobs)

    def body(*refs):
        it = iter(refs)
        take = lambda k: [next(it) for _ in range(k)]
        src = [take(4) for _ in streams]
        loc = [take(4) for _ in range(n_local)]
        sum_ref, gb_ref, gc_ref = take(3)
        sw, sm, sv = take(ns), take(ns), take(ns)
        dst = [take(k) for k in n_results]
        loc_out = [take(4) for _ in range(n_local)]
        souts = take(4 * ns)
        src_buf = [take(4) for _ in streams]
        dst_buf = [take(k) for k in n_results]
        read_sem, write_sem = take(2)

        def rows_of(ref, r0, nr):
            return ref.at[:, pl.ds(r0, nr), :] if len(ref.shape) == 3 else ref.at[pl.ds(r0, nr), :]

        reads = [[pltpu.make_async_copy(rows_of(src[a][k], r0, nr), rows_of(src_buf[a][k], r0, nr),
                                        read_sem.at[4 * j + k]) for k in range(4)]
                 for j, (a, r0, nr) in enumerate(jobs)]
        for job_reads in reads:
            for cp in job_reads:
                cp.start()

        def summed_shares(got, n_rows):
            g = got[0].astype(F32)
            for d in range(1, N_DEV):
                g = g + got[d].astype(F32)
            return g[0:n_rows, :]

        for (got, w_r, m_r, v_r), outs4 in zip(loc, loc_out):
            g = summed_shares(got, w_r.shape[0])
            for o_r, val in zip(outs4, (g,) + _adamw(w_r[...], g, m_r[...], v_r[...])):
                o_r[...] = val
        for k, (name, off) in enumerate(_SMALL):
            w = sw[k][...]
            if name == "b_mod":
                gk = gb_ref[...]
            elif name == "c_ctx":
                acc = gc_ref[0, 0:1, :]
                for j in range(1, N_DEV):
                    acc = acc + gc_ref[j, 0:1, :]
                sg = _sigmoid(w)
                gk = acc * (sg * (1.0 + w * (1.0 - sg)))
            else:
                gk = sum_ref[:, off:off + w.shape[1]]
            dl, m_new, v_new = _adamw(w, gk, sm[k][...], sv[k][...])
            souts[k][...] = gk
            souts[ns + k][...] = dl
            souts[2 * ns + k][...] = m_new
            souts[3 * ns + k][...] = v_new

        writes, n_started = [], 0
        for j, (a, r0, nr) in enumerate(jobs):
            for cp in reads[j]:
                cp.wait()
            rows = pl.ds(r0, nr)
            g_b, w_b, m_b, v_b = src_buf[a]
            if a < n_tiled:
                g = g_b[rows, :]
                vals = _adamw(w_b[rows, :], g, m_b[rows, :], v_b[rows, :])
            else:
                g = summed_shares(g_b, nr)
                vals = (g,) + _adamw(w_b[...], g, m_b[...], v_b[...])
            for k, val in enumerate(vals):
                dst_buf[a][k][rows, :] = val
                cp = pltpu.make_async_copy(rows_of(dst_buf[a][k], r0, nr), rows_of(dst[a][k], r0, nr),
                                           write_sem.at[n_started])
                cp.start()
                writes.append(cp)
                n_started += 1
        for cp in writes:
            cp.wait()

    vm = pl.BlockSpec(memory_space=pltpu.VMEM)
    anyspace = pl.BlockSpec(memory_space=pl.ANY)
    args, out_shape, scratch = [], [], []
    for item in streams:
        args += list(item)
        scratch += [pltpu.VMEM(a.shape, a.dtype) for a in item]
    for item in local_sums:
        args += list(item)
    args += [summed, g_bmod, gc_all, *small_w, *small_m, *small_v]
    for (_, w, _, _), k in zip(streams, n_results):
        out_shape += [jax.ShapeDtypeStruct(w.shape, F32)] * k
    scratch += [pltpu.VMEM(s.shape, F32) for s in out_shape]
    n_streamed_out = len(out_shape)
    for _, w, _, _ in local_sums:
        out_shape += [jax.ShapeDtypeStruct(w.shape, F32)] * 4
    out_shape += [jax.ShapeDtypeStruct(w.shape, F32) for w in small_w] * 4
    scratch += [pltpu.SemaphoreType.DMA((n_reads,)), pltpu.SemaphoreType.DMA((n_writes,))]
    outs = pl.pallas_call(
        body, name="final_adamw", out_shape=out_shape,
        in_specs=[anyspace] * (4 * len(streams)) + [vm] * (len(args) - 4 * len(streams)),
        out_specs=[anyspace] * n_streamed_out + [vm] * (len(out_shape) - n_streamed_out),
        scratch_shapes=scratch, compiler_params=pltpu.CompilerParams(vmem_limit_bytes=VMEM_LIMIT),
    )(*args)
    it = iter(outs)
    take = lambda k: tuple(next(it) for _ in range(k))
    r_tiled = [(item[0],) + take(3) for item in tiled]
    r_sums = [take(4) for _ in streamed_sums]
    r_local = [take(4) for _ in local_sums]
    small_outs = [list(take(ns)) for _ in range(4)]
    return r_tiled, r_sums, r_local, small_outs


def _rope_tables(s):
    t = jnp.arange(s, dtype=jnp.int32)
    row = (t // GRID_W).astype(F32)
    col = (t % GRID_W).astype(F32)
    freqs = ROPE_THETA ** (-jnp.arange(0, HD // 2, 2, dtype=F32) / (HD // 2))
    ang_r = row[:, None] * freqs[None, :]
    ang_c = col[:, None] * freqs[None, :]
    cr, sr, cc, sc = jnp.cos(ang_r), jnp.sin(ang_r), jnp.cos(ang_c), jnp.sin(ang_c)
    cos = jnp.concatenate([cr, cr, cc, cc], axis=-1)
    sins = jnp.concatenate([-sr, sr, -sc, sc], axis=-1)
    return jnp.tile(cos, (1, KVW // HD)), jnp.tile(sins, (1, KVW // HD))


def kernel(x, c, ctx, c_ctx, w_mod, b_mod, norm_w, w_in, q_norm_w, k_norm_w, conv_w, conv_b, conv_ln_w, conv_ln_b, w_pw, b_pw, w_out, loss_target, m_c_ctx, m_w_mod, m_b_mod, m_norm_w, m_w_in, m_q_norm_w, m_k_norm_w, m_conv_w, m_conv_b, m_conv_ln_w, m_conv_ln_b, m_w_pw, m_b_pw, m_w_out, v_c_ctx, v_w_mod, v_b_mod, v_norm_w, v_w_in, v_q_norm_w, v_k_norm_w, v_conv_w, v_conv_b, v_conv_ln_w, v_conv_ln_b, v_w_pw, v_b_pw, v_w_out):
    bl, s, _ = x.shape
    cl = ctx.shape[1]
    me = _lin(*_coords())

    conv_w_pad = jnp.pad(conv_w[0], ((0, 32 - KW), (0, 0)))
    n_ex = N_DEV * bl
    g_win, c_rows, g_mod = _prologue(w_in[0].T, c, c_ctx[None, :], w_mod[0], b_mod)
    w_in_b = g_win.reshape(D_IN, D)
    mod_all = g_mod.transpose(1, 0, 2).reshape(n_ex + 8, 3 * D)
    modrows = lax.dynamic_slice_in_dim(mod_all, me * bl, bl, axis=0).reshape(bl, 3, D)
    modc = mod_all[n_ex].reshape(1, 3, D)

    cos, sins = _rope_tables(s)
    qnw_t = jnp.tile(q_norm_w, (1, DA // HD))
    knw_t = jnp.tile(k_norm_w, (1, KVW // HD))
    lane = jnp.arange(DA, dtype=jnp.int32) // HD
    ones_bd = (lane[:, None] == lane[None, :]).astype(BF16)
    ones_kv = ones_bd[0:KVW, 0:KVW]
    w_kv_b = w_in_b

    k_ctx, v_ctx, pkv_c = _ctx_fwd(ctx, modc, norm_w, w_kv_b, knw_t, ones_kv, cl + s)
    (q_h, k_h, v_h, pq, pkv, za, glu, zc), (g_wout, g_wpw, g_cw) = _fwd_in(
        x, modrows, norm_w, w_in_b, cos, sins, qnw_t, knw_t, ones_bd, k_ctx, v_ctx,
        [w_out[0], w_pw[0], conv_w_pad], [BF16, BF16, F32])
    w_out_b = g_wout.reshape(D, D)
    w_pw_b = g_wpw.reshape(DC, DC)
    conv_w_full = g_cw.transpose(1, 0, 2).reshape(32, DC)
    attn, lse = _attn_fwd(q_h, k_h, v_h)
    y_conv, cp = _conv_fwd(glu, conv_w_full, conv_b, conv_ln_w, conv_ln_b, w_pw_b, b_pw)

    do_h, dza, dy_conv, dzc, dh, dgate, gw_out, loss_row, gw_pw, conv_rows = _out_fwd_bwd(
        attn, za, cp, zc, x, loss_target, modrows, w_out_b, y_conv, conv_ln_w, conv_ln_b, w_pw_b)
    dglu, g_cw_full = _conv_bwd_depthwise(glu, dy_conv, conv_w_full)
    parts_out = gw_out.reshape(N_DEV, D // N_DEV, D)
    parts_pw = gw_pw.reshape(N_DEV, DC // N_DEV, DC)
    parts_cw = g_cw_full.astype(BF16).reshape(32, N_DEV, DC // N_DEV).transpose(1, 0, 2)
    (dq, dk_h, dv_h), (got_out, got_pw, got_cw) = _attn_bwd(
        q_h, k_h, v_h, do_h, attn, lse, [parts_out, parts_pw, parts_cw])
    gw_kv, ctx_rows, dknw_c = _ctx_bwd(ctx, modc, norm_w, w_kv_b, pkv_c, dk_h, dv_h, knw_t, ones_kv)
    grad_x, gw_in, dmod_ss, dnw, dqnw, dknw = _bwd_in(
        x, modrows, norm_w, w_in_b, cos, sins, qnw_t, knw_t, ones_bd,
        pq, pkv, dq, dk_h, dv_h, dza, dglu, dzc, dh, gw_kv)

    given = {"c_ctx": (c_ctx, m_c_ctx, v_c_ctx), "b_mod": (b_mod, m_b_mod, v_b_mod), "norm_w": (norm_w, m_norm_w, v_norm_w),
             "q_norm_w": (q_norm_w, m_q_norm_w, v_q_norm_w), "k_norm_w": (k_norm_w, m_k_norm_w, v_k_norm_w),
             "conv_b": (conv_b, m_conv_b, v_conv_b), "conv_ln_w": (conv_ln_w, m_conv_ln_w, v_conv_ln_w),
             "conv_ln_b": (conv_ln_b, m_conv_ln_b, v_conv_ln_b), "b_pw": (b_pw, m_b_pw, v_b_pw)}
    as_rows = [[given[name][which].reshape(1, -1) for name, _ in _SMALL] for which in range(3)]
    g_in_t, g_wmod, summed, g_bmod, gc_all, loss11 = _epilogue(
        gw_in.reshape(N_DEV, D_IN // N_DEV, D),
        [loss_row, ctx_rows, dnw, dqnw, dknw, dknw_c, conv_rows, dmod_ss, dgate], c_rows, w_mod[0])
    (r_in, r_wmod), (r_out,), (r_pw, r_cw), small_outs = _final_adamw(
        [(g_in_t, w_in[0].T, m_w_in[0].T, v_w_in[0].T), (g_wmod, w_mod[0], m_w_mod[0], v_w_mod[0])],
        [(got_out, w_out[0], m_w_out[0], v_w_out[0])],
        [(got_pw, w_pw[0], m_w_pw[0], v_w_pw[0]), (got_cw, conv_w[0], m_conv_w[0], v_conv_w[0])],
        summed, g_bmod, gc_all, *as_rows)
    r_in = tuple(a.T for a in r_in)

    big = {"w_mod": r_wmod, "w_in": r_in, "conv_w": r_cw, "w_pw": r_pw, "w_out": r_out}
    order = ["c_ctx", "w_mod", "b_mod", "norm_w", "w_in", "q_norm_w", "k_norm_w", "conv_w", "conv_b", "conv_ln_w",
             "conv_ln_b", "w_pw", "b_pw", "w_out"]
    small_index = {name: k for k, (name, _) in enumerate(_SMALL)}
    outs = [loss11.reshape(()), grad_x]
    for which in range(4):
        for name in order:
            if name in big:
                outs.append(big[name][which][None])
            else:
                outs.append(small_outs[which][small_index[name]].reshape(given[name][0].shape))
    return tuple(outs)
```

```python
import jax
import jax.numpy as jnp
from jax import lax
from jax.experimental import pallas as pl
from jax.experimental.pallas import tpu as pltpu

F32, BF16 = jnp.float32, jnp.bfloat16
MESH_ID = pl.DeviceIdType.MESH

N_DEV = 8
D = 1024
D_IN = 2816
DA = 512
DC = 512
HD = 64
KVW = 128
KW = 31
HALO = 16
EPS = 1e-6
ROPE_THETA = 10000.0
GRID_W = 64

ADAM_LR, ADAM_B1, ADAM_B2, ADAM_EPS, ADAM_WD, ADAM_STEP = 0.001, 0.9, 0.999, 1e-08, 0.01, 10

VMEM_LIMIT = 56 * 1024 * 1024

TM = 256
TQ = 128
TOKEN_PARTS = 2
OUT_TM = 512
BWD_PARTS = 4
FWD_PARTS = 8
TC = 512
CH = 32
ADAM_STEPS = 4


def _params(sem, vmem=VMEM_LIMIT):
    return pltpu.CompilerParams(dimension_semantics=sem, vmem_limit_bytes=vmem)


def _dot(a, b):
    return jnp.dot(a, b, preferred_element_type=F32)


def _dot_nt(a, b):
    return lax.dot_general(a, b, (((1,), (1,)), ((), ())), preferred_element_type=F32)


def _dot_tn(a, b):
    return lax.dot_general(a, b, (((0,), (0,)), ((), ())), preferred_element_type=F32)


def _sigmoid(z):
    return 1.0 / (1.0 + jnp.exp(-z))


def _segsum(v, ones_bd):
    return _dot(v.astype(BF16), ones_bd)


def _swap16(x):
    w = x.shape[-1]
    lane = lax.broadcasted_iota(jnp.int32, x.shape, 1)
    return jnp.where((lane % 32) < 16, pltpu.roll(x, w - 16, 1), pltpu.roll(x, 16, 1))


def _with_ones_column(v):
    one = (lax.broadcasted_iota(jnp.int32, v.shape, 1) == 0).astype(v.dtype)
    return jnp.concatenate([v, one], axis=-1)


def _rope(x, cos, sins):
    return x * cos + _swap16(x) * sins


def _rope_bwd(d, cos, sins):
    return d * cos + _swap16(d * sins)


def _adamw(w, g, m, v):
    m2 = ADAM_B1 * m + (1.0 - ADAM_B1) * g
    v2 = ADAM_B2 * v + (1.0 - ADAM_B2) * (g * g)
    m_hat = m2 / (1.0 - ADAM_B1 ** ADAM_STEP)
    v_hat = v2 / (1.0 - ADAM_B2 ** ADAM_STEP)
    delta = -ADAM_LR * (m_hat / (jnp.sqrt(v_hat) + ADAM_EPS) + ADAM_WD * w)
    return delta, m2, v2


def _coords():
    return lax.axis_index("x"), lax.axis_index("y"), lax.axis_index("c")


def _lin(x, y, c):
    return 4 * x + 2 * y + c


def _prologue(w_in_t, c, c_ctx_row, w_mod_loc, b_mod):
    bl = c.shape[0]
    n_ex = N_DEV * bl
    n_mod = w_mod_loc.shape[1]

    def body(w32_ref, c_in_ref, cctx_ref, wm_ref, b_ref, out_w, crows_ref, mod_out, w_ref, c_ref, c_gath, mod_mine,
             w_send, w_recv, c_send, c_recv, m_send, m_recv, local_sems):
        x, y, c = _coords()
        me_lin = _lin(x, y, c)
        c_ref[...] = jnp.zeros_like(c_ref)
        c_ref[0:bl, :] = c_in_ref[...]
        w_ref[...] = w32_ref[...].astype(BF16)
        me, sib = (x, y, c), (x, y, 1 - c)
        xnb, ynb, diag = (1 - x, y), (x, 1 - y), (1 - x, 1 - y)
        north = c == 1

        def direct_gather(src, dst, send_sems, recv_sems, local_sem):
            cps = [pltpu.make_async_copy(src, dst.at[me_lin], local_sem)]
            for k in range(1, N_DEV):
                peer = (1 - x if k & 4 else x, 1 - y if k & 2 else y, 1 - c if k & 1 else c)
                cps.append(pltpu.make_async_remote_copy(
                    src_ref=src, dst_ref=dst.at[me_lin], send_sem=send_sems.at[k - 1], recv_sem=recv_sems.at[k - 1],
                    device_id=peer, device_id_type=MESH_ID))
            for cp in cps:
                cp.start()
            return cps

        def copy(k, block, to, src=None):
            slot = out_w.at[_lin(*block)]
            return pltpu.make_async_remote_copy(
                src_ref=slot if src is None else src, dst_ref=slot, send_sem=w_send.at[k], recv_sem=w_recv.at[k],
                device_id=to, device_id_type=MESH_ID)

        c_cps = direct_gather(c_ref, c_gath, c_send, c_recv, local_sems.at[0])
        mine = pltpu.make_async_copy(w_ref, out_w.at[me_lin], local_sems.at[1])
        mine.start()
        first = [copy(0, me, sib, src=w_ref), copy(1, me, (*xnb, c), src=w_ref), copy(2, me, (*ynb, c), src=w_ref)]
        for cp in first:
            cp.start()

        for cp in c_cps[1:]:
            cp.wait_recv()
        c_cps[0].wait()
        crows_ref[...] = jnp.zeros_like(crows_ref)
        for j in range(N_DEV):
            crows_ref[j * bl:(j + 1) * bl, :] = c_gath[j, 0:bl, :]
        crows_ref[n_ex:n_ex + 1, :] = cctx_ref[...]
        cr = crows_ref[...]
        act = (cr * _sigmoid(cr)).astype(BF16)
        mod_mine[...] = _dot(act, wm_ref[...].astype(BF16)) + b_ref[:, pl.ds(pl.multiple_of(me_lin * n_mod, 128), n_mod)]
        mod_cps = direct_gather(mod_mine, mod_out, m_send, m_recv, local_sems.at[2])

        relay_north = copy(3, (*xnb, c), (*ynb, c))
        relay_south = copy(3, (*ynb, c), (*xnb, c))
        passed = []
        copy(1, (*xnb, c), me).wait_recv()
        pl.when(north)(relay_north.start)
        passed.append(copy(4, (*xnb, c), sib))
        passed[-1].start()
        copy(2, (*ynb, c), me).wait_recv()
        pl.when(jnp.logical_not(north))(relay_south.start)
        passed.append(copy(5, (*ynb, c), sib))
        passed[-1].start()
        copy(3, (*diag, c), me).wait_recv()
        passed.append(copy(6, (*diag, c), sib))
        passed[-1].start()
        copy(0, sib, me).wait_recv()
        for k, chip in ((4, xnb), (5, ynb), (6, diag)):
            copy(k, (*chip, 1 - c), me).wait_recv()
        for cp in mod_cps[1:]:
            cp.wait_recv()
        mod_cps[0].wait()
        for cp in first + passed + [relay_north] + c_cps[1:] + mod_cps[1:]:
            cp.wait_send()
        mine.wait()

    vm = pl.BlockSpec(memory_space=pltpu.VMEM)
    seven = pltpu.SemaphoreType.DMA((N_DEV - 1,))
    return pl.pallas_call(
        body, name="prologue",
        out_shape=[jax.ShapeDtypeStruct((N_DEV,) + w_in_t.shape, BF16), jax.ShapeDtypeStruct((n_ex + 8, D), F32),
                   jax.ShapeDtypeStruct((N_DEV, n_ex + 8, n_mod), F32)],
        in_specs=[vm] * 5, out_specs=[pl.BlockSpec(memory_space=pl.ANY), vm, vm],
        scratch_shapes=[pltpu.VMEM(w_in_t.shape, BF16), pltpu.VMEM((8, D), F32), pltpu.VMEM((N_DEV, 8, D), F32),
                        pltpu.VMEM((n_ex + 8, n_mod), F32),
                        seven, seven, seven, seven, seven, seven, pltpu.SemaphoreType.DMA((3,))],
        compiler_params=pltpu.CompilerParams(vmem_limit_bytes=VMEM_LIMIT),
    )(w_in_t, c, c_ctx_row, w_mod_loc, b_mod)


def _exchange_copies(in_refs, out_refs, send_sems, recv_sems, local_sems, scatter):
    x, y, c = _coords()
    me = _lin(x, y, c)
    local, remote = [], []
    for a, (src, dst) in enumerate(zip(in_refs, out_refs)):
        local.append(pltpu.make_async_copy(src.at[me] if scatter else src, dst.at[me], local_sems.at[a]))
        for k in range(1, N_DEV):
            peer = (1 - x if k & 4 else x, 1 - y if k & 2 else y, 1 - c if k & 1 else c)
            remote.append(pltpu.make_async_remote_copy(
                src_ref=src.at[_lin(*peer)] if scatter else src, dst_ref=dst.at[me],
                send_sem=send_sems.at[a * (N_DEV - 1) + k - 1], recv_sem=recv_sems.at[a * (N_DEV - 1) + k - 1],
                device_id=peer, device_id_type=MESH_ID))
    return local, remote


def _exchange_scratch(n):
    return [pltpu.SemaphoreType.DMA((n * (N_DEV - 1),)), pltpu.SemaphoreType.DMA((n * (N_DEV - 1),)),
            pltpu.SemaphoreType.DMA((n,))]


def _fwd_in(x, modrows, norm_w, w_in_b, cos, sins, qnw_t, knw_t, ones_bd, k_all, v_all, shards, wire_dtypes):
    bl, s, _ = x.shape
    tm = TOKEN_PARTS * TM
    nt = s // tm
    n_sh = len(shards)

    def body(*refs):
        (x_ref, mod_ref, nw_ref, win_ref, cos_ref, sin_ref, qnw_ref, knw_ref, bd_ref, kin_ref, vin_ref) = refs[:11]
        shard_refs = refs[11:11 + n_sh]
        q_ref, k_ref, v_ref, pq_ref, pkv_ref, za_ref, glu_ref, zc_ref = refs[11 + n_sh:19 + n_sh]
        gathered_refs = refs[19 + n_sh:19 + 2 * n_sh]
        stage_refs = refs[19 + 2 * n_sh:19 + 3 * n_sh]
        send_sems, recv_sems, local_sems = refs[19 + 3 * n_sh:]
        b, i = pl.program_id(0), pl.program_id(1)
        local, remote = _exchange_copies(stage_refs, gathered_refs, send_sems, recv_sems, local_sems, scatter=False)

        @pl.when((b == 0) & (i == 0))
        def _():
            for src, stage in zip(shard_refs, stage_refs):
                stage[...] = src[...].astype(stage.dtype)
            for cp in local + remote:
                cp.start()

        shift = mod_ref[0, 0:1, :]
        scale = mod_ref[0, 1:2, :]
        for part in range(TOKEN_PARTS):
            rows = pl.ds(part * TM, TM)
            xv = x_ref[0, rows, :]
            r = lax.rsqrt(jnp.mean(xv * xv, axis=-1, keepdims=True) + EPS)
            u = (xv * r * nw_ref[...]) * (1.0 + scale) + shift
            p = _dot_nt(u.astype(BF16), win_ref[...])
            pq = p[:, 0:DA]
            pk = p[:, DA:DA + HD * 2]
            ck = cos_ref[rows, :]
            sk = sin_ref[rows, :]
            cs = jnp.concatenate([ck] * (DA // KVW), axis=-1)
            sn = jnp.concatenate([sk] * (DA // KVW), axis=-1)
            rq = lax.rsqrt(_segsum(pq * pq, bd_ref[...]) * (1.0 / HD) + EPS)
            qn = pq * rq * qnw_ref[...]
            qr = _rope(qn, cs, sn) * 0.125
            for h in range(DA // HD):
                q_ref[0, h, rows, :] = qr[:, h * HD:(h + 1) * HD].astype(BF16)
            rk = lax.rsqrt(_segsum(pk * pk, bd_ref[0:KVW, 0:KVW]) * (1.0 / HD) + EPS)
            kn = pk * rk * knw_ref[...]
            kr = _rope(kn, ck, sk)
            pv = p[:, 640:768]
            for h in range(KVW // HD):
                k_ref[0, h, rows, :] = kr[:, h * HD:(h + 1) * HD].astype(BF16)
                v_ref[0, h, rows, :] = _with_ones_column(pv[:, h * HD:(h + 1) * HD]).astype(BF16)
            pq_ref[0, rows, :] = pq
            pkv_ref[0, rows, :] = p[:, 512:768]
            za_ref[0, rows, :] = p[:, 768:1280]
            glu_ref[0, rows, :] = p[:, 1280:2304]
            zc_ref[0, rows, :] = p[:, 2304:2816]

        @pl.when((b == bl - 1) & (i == nt - 1))
        def _():
            for cp in remote:
                cp.wait_recv()
            for cp in remote:
                cp.wait_send()
            for cp in local:
                cp.wait()

    def tile(w):
        return pl.BlockSpec((1, tm, w), lambda b, i: (b, i, 0))

    def const(shape):
        return pl.BlockSpec(shape, lambda b, i: (0,) * len(shape))

    outs = [(DA, F32), (2 * KVW, F32), (DA, F32), (2 * DC, F32), (DC, F32)]
    anyspace = pl.BlockSpec(memory_space=pl.ANY)
    rope = pl.BlockSpec((tm, KVW), lambda b, i: (i, 0))
    k_tile = pl.BlockSpec((1, KVW // HD, tm, HD), lambda b, i: (b, 0, i, 0))
    v_tile = pl.BlockSpec((1, KVW // HD, tm, 2 * HD), lambda b, i: (b, 0, i, 0))
    res = pl.pallas_call(
        body, name="fwd_in", grid=(bl, nt),
        in_specs=[tile(D), pl.BlockSpec((1, 3, D), lambda b, i: (b, 0, 0)), const((1, D)), const((D_IN, D)),
                  rope, rope, const((1, DA)), const((1, KVW)), const((DA, DA)), anyspace, anyspace]
        + [const(a.shape) for a in shards],
        out_specs=[pl.BlockSpec((1, DA // HD, tm, HD), lambda b, i: (b, 0, i, 0)), k_tile, v_tile]
        + [tile(w) for w, _ in outs] + [anyspace] * n_sh,
        out_shape=[jax.ShapeDtypeStruct((bl, DA // HD, s, HD), BF16), jax.ShapeDtypeStruct(k_all.shape, BF16),
                   jax.ShapeDtypeStruct(v_all.shape, BF16)]
        + [jax.ShapeDtypeStruct((bl, s, w), dt) for w, dt in outs]
        + [jax.ShapeDtypeStruct((N_DEV,) + a.shape, dt) for a, dt in zip(shards, wire_dtypes)],
        input_output_aliases={9: 1, 10: 2},
        scratch_shapes=[pltpu.VMEM(a.shape, dt) for a, dt in zip(shards, wire_dtypes)] + _exchange_scratch(n_sh),
        compiler_params=_params(("arbitrary", "arbitrary")),
    )(x, modrows, norm_w, w_in_b, cos, sins, qnw_t, knw_t, ones_bd, k_all, v_all, *shards)
    return res[:8], res[8:]


_KV_ROWS_OF_W_IN_T = pl.BlockSpec((2 * KVW, D), lambda b: (DA // (2 * KVW), 0))


def _ctx_fwd(ctx, modc, norm_w, w_kv_b, knw_t, ones_bd, n_keys):
    bl, cl, _ = ctx.shape

    def body(x_ref, mod_ref, nw_ref, w_ref, knw_ref, bd_ref, k_ref, v_ref, pkv_ref):
        xv = x_ref[0]
        shift = mod_ref[0, 0:1, :]
        scale = mod_ref[0, 1:2, :]
        r = lax.rsqrt(jnp.mean(xv * xv, axis=-1, keepdims=True) + EPS)
        u = (xv * r * nw_ref[...]) * (1.0 + scale) + shift
        p = _dot_nt(u.astype(BF16), w_ref[...])
        pk = p[:, 0:KVW]
        rk = lax.rsqrt(_segsum(pk * pk, bd_ref[...]) * (1.0 / HD) + EPS)
        kn = pk * rk * knw_ref[...]
        pv = p[:, KVW:2 * KVW]
        for h in range(KVW // HD):
            k_ref[0, h] = kn[:, h * HD:(h + 1) * HD].astype(BF16)
            v_ref[0, h] = _with_ones_column(pv[:, h * HD:(h + 1) * HD]).astype(BF16)
        pkv_ref[0] = p

    def const(shape):
        return pl.BlockSpec(shape, lambda b: (0,) * len(shape))

    def tile(w):
        return pl.BlockSpec((1, cl, w), lambda b: (b, 0, 0))

    ctx_block = (n_keys - cl) // cl
    assert ctx_block * cl + cl == n_keys
    k_tile = pl.BlockSpec((1, KVW // HD, cl, HD), lambda b: (b, 0, ctx_block, 0))
    v_tile = pl.BlockSpec((1, KVW // HD, cl, 2 * HD), lambda b: (b, 0, ctx_block, 0))
    return pl.pallas_call(
        body, name="ctx_fwd", grid=(bl,),
        in_specs=[tile(D), const((1, 3, D)), const((1, D)), _KV_ROWS_OF_W_IN_T, const((1, KVW)), const((KVW, KVW))],
        out_specs=[k_tile, v_tile, tile(2 * KVW)],
        out_shape=[jax.ShapeDtypeStruct((bl, KVW // HD, n_keys, HD), BF16),
                   jax.ShapeDtypeStruct((bl, KVW // HD, n_keys, 2 * HD), BF16),
                   jax.ShapeDtypeStruct((bl, cl, 2 * KVW), F32)],
        compiler_params=_params(("arbitrary",)),
    )(ctx, modc, norm_w, w_kv_b, knw_t, ones_bd)


def _attn_fwd(q, k, v1):
    bl, _, s, _ = q.shape
    n_keys = k.shape[2]

    def body(q_ref, k_ref, v_ref, o_ref, lse_ref):
        kv = k_ref[0, 0]
        vv = v_ref[0, 0]
        lane = lax.broadcasted_iota(jnp.int32, (TQ, 2 * HD), 1)
        for part in range(FWD_PARTS):
            rows = pl.ds(part * TQ, TQ)
            lse = jnp.zeros((TQ, 2 * HD), F32)
            heads = []
            sc_all = _dot_nt(q_ref[0, :, rows, :].reshape(4 * TQ, HD), kv)
            for h in range(4):
                sc = sc_all[h * TQ:(h + 1) * TQ, :]
                m = jnp.max(sc, axis=-1, keepdims=True)
                e = jnp.exp(sc - m).astype(BF16)
                ov = _dot(e, vv)
                denom = ov[:, HD:HD + 1]
                heads.append(ov[:, 0:HD] * (1.0 / denom))
                lse = jnp.where(lane == h, m + jnp.log(denom), lse)
            o_ref[0, rows, :] = jnp.concatenate(heads, axis=-1)
            lse_ref[0, 0, rows, :] = lse

    tq = FWD_PARTS * TQ
    ks = pl.BlockSpec((1, 1, n_keys, HD), lambda b, g, i: (b, g, 0, 0))
    qs = pl.BlockSpec((1, 4, tq, HD), lambda b, g, i: (b, g, i, 0))
    vs = pl.BlockSpec((1, 1, n_keys, 2 * HD), lambda b, g, i: (b, g, 0, 0))
    return pl.pallas_call(
        body, name="attn_fwd", grid=(bl, 2, s // tq), in_specs=[qs, ks, vs],
        out_specs=[pl.BlockSpec((1, tq, 4 * HD), lambda b, g, i: (b, i, g)),
                   pl.BlockSpec((1, 1, tq, 2 * HD), lambda b, g, i: (b, g, i, 0))],
        out_shape=[jax.ShapeDtypeStruct((bl, s, DA), F32), jax.ShapeDtypeStruct((bl, 2, s, 2 * HD), F32)],
        compiler_params=_params(("arbitrary", "arbitrary", "arbitrary")),
    )(q, k, v1)


def _attn_bwd(q, k, v1, do, o, lse, exchange):
    bl, _, s, _ = q.shape
    n_keys = k.shape[2]
    tq = BWD_PARTS * TQ
    nq = s // tq
    n_ex = len(exchange)

    def body(*refs):
        q_ref, k_ref, v_ref, do_ref, o_ref, lse_ref = refs[:6]
        part_refs = refs[6:6 + n_ex]
        dq_ref, dk_ref, dv_ref = refs[6 + n_ex:9 + n_ex]
        got_refs = refs[9 + n_ex:9 + 2 * n_ex]
        p_sc, ds_sc, dkt, dvt, send_sems, recv_sems, local_sems = refs[9 + 2 * n_ex:]
        i = pl.program_id(2)
        first = (pl.program_id(0) == 0) & (pl.program_id(1) == 0) & (i == 0)
        last = (pl.program_id(0) == bl - 1) & (pl.program_id(1) == 1) & (i == nq - 1)
        local, remote = _exchange_copies(part_refs, got_refs, send_sems, recv_sems, local_sems, scatter=True)

        @pl.when(first)
        def _():
            for cp in local + remote:
                cp.start()

        @pl.when(i == 0)
        def _():
            dkt[...] = jnp.zeros_like(dkt)
            dvt[...] = jnp.zeros_like(dvt)

        kv = k_ref[0, 0]
        vv = v_ref[0, 0][:, 0:HD]
        for part in range(BWD_PARTS):
            tq_rows = pl.ds(part * TQ, TQ)
            lse = lse_ref[0, 0, tq_rows, :]
            ov = o_ref[0, tq_rows, :]
            dqs = []
            q_cat = q_ref[0, :, tq_rows, :].reshape(4 * TQ, HD)
            do_cat = do_ref[0, :, tq_rows, :].reshape(4 * TQ, HD)
            sc_all = _dot_nt(q_cat, kv)
            for h in range(4):
                doh = do_cat[h * TQ:(h + 1) * TQ, :]
                delta = jnp.sum(ov[:, h * HD:(h + 1) * HD] * doh.astype(F32), axis=-1, keepdims=True)
                rows = pl.ds((part * 4 + h) * TQ, TQ)
                p = jnp.exp(sc_all[h * TQ:(h + 1) * TQ, :] - lse[:, h:h + 1])
                ds = (p * (_dot_nt(doh, vv) - delta)).astype(BF16)
                p_sc[rows, :] = p.astype(BF16)
                ds_sc[rows, :] = ds
                dqs.append(_dot(ds, kv) * 0.125)
            dq_ref[0, tq_rows, :] = jnp.concatenate(dqs, axis=-1)
            part_rows = pl.ds(part * 4 * TQ, 4 * TQ)
            dvt[...] += _dot_tn(do_cat, p_sc[part_rows, :])
            dkt[...] += _dot_tn(q_cat, ds_sc[part_rows, :])

        @pl.when(i == nq - 1)
        def _():
            dk_ref[0, 0] = dkt[...].T
            dv_ref[0, 0] = dvt[...].T

        @pl.when(last)
        def _():
            for cp in remote:
                cp.wait_recv()
            for cp in remote:
                cp.wait_send()
            for cp in local:
                cp.wait()

    qs = pl.BlockSpec((1, 4, tq, HD), lambda b, g, i: (b, g, i, 0))
    ks = pl.BlockSpec((1, 1, n_keys, HD), lambda b, g, i: (b, g, 0, 0))
    vs = pl.BlockSpec((1, 1, n_keys, 2 * HD), lambda b, g, i: (b, g, 0, 0))
    os_ = pl.BlockSpec((1, tq, 4 * HD), lambda b, g, i: (b, i, g))
    kshape = jax.ShapeDtypeStruct(k.shape, F32)
    anyspace = pl.BlockSpec(memory_space=pl.ANY)
    res = pl.pallas_call(
        body, name="attn_bwd", grid=(bl, 2, nq),
        in_specs=[qs, ks, vs, qs, os_, pl.BlockSpec((1, 1, tq, 2 * HD), lambda b, g, i: (b, g, i, 0))]
        + [anyspace] * n_ex,
        out_specs=[os_, ks, ks] + [anyspace] * n_ex,
        out_shape=[jax.ShapeDtypeStruct((bl, s, DA), F32), kshape, kshape]
        + [jax.ShapeDtypeStruct(a.shape, a.dtype) for a in exchange],
        scratch_shapes=[pltpu.VMEM((4 * tq, n_keys), BF16), pltpu.VMEM((4 * tq, n_keys), BF16),
                        pltpu.VMEM((HD, n_keys), F32), pltpu.VMEM((HD, n_keys), F32)] + _exchange_scratch(n_ex),
        compiler_params=_params(("arbitrary", "arbitrary", "arbitrary")),
    )(q, k, v1, do, o, lse, *exchange)
    return res[:3], res[3:]


def _halo_specs(width, s):
    per = TC // HALO
    last = s // HALO - 1
    main = pl.BlockSpec((1, TC, width), lambda b, i: (b, i, 0))
    prev = pl.BlockSpec((1, HALO, width), lambda b, i: (b, jnp.maximum(i * per - 1, 0), 0))
    nxt = pl.BlockSpec((1, HALO, width), lambda b, i: (b, jnp.minimum((i + 1) * per, last), 0))
    return main, prev, nxt


def _glu(g):
    return g[:, 0:DC] * _sigmoid(g[:, DC:2 * DC])


def _fill_padded(pad_ref, main, prev, nxt, first, last):
    tc = main.shape[0]
    pad_ref[0:HALO, :] = jnp.where(first, 0.0, prev)
    pad_ref[HALO:HALO + tc, :] = main
    pad_ref[HALO + tc:2 * HALO + tc, :] = jnp.where(last, 0.0, nxt)


PLANE_ROWS = TC + 2 * HALO - 8


def _shift_planes(pad_ref, planes_ref):
    for r in range(1, 8):
        planes_ref[r - 1] = pad_ref[pl.ds(r, planes_ref.shape[1]), :]


def _tap_rows(pad_ref, planes_ref, offset, start, n):
    a, r = divmod(offset, 8)
    if r == 0:
        return pad_ref[pl.ds(start + 8 * a, n), :]
    return planes_ref[r - 1, pl.ds(start + 8 * a, n), :]


def _conv_fwd(glu, conv_w, conv_b, ln_w, ln_b, w_pw_b, b_pw):
    bl, s, _ = glu.shape
    nt = s // TC

    def body(g_ref, gp_ref, gn_ref, cw_ref, cb_ref, lw_ref, lb_ref, wpw_ref, bpw_ref, y_ref, cp_ref, pad_ref, planes_ref):
        i = pl.program_id(1)
        _fill_padded(pad_ref, _glu(g_ref[0]), _glu(gp_ref[0]), _glu(gn_ref[0]), i == 0, i == nt - 1)
        _shift_planes(pad_ref, planes_ref)
        for ck in range(TC // CH):
            acc = jnp.zeros((CH, DC), F32) + cb_ref[...]
            for t in range(KW):
                acc = acc + _tap_rows(pad_ref, planes_ref, 1 + t, ck * CH, CH) * cw_ref[t:t + 1, :]
            y_ref[0, pl.ds(ck * CH, CH), :] = acc
        y = y_ref[0]
        mu = jnp.mean(y, axis=-1, keepdims=True)
        yc = y - mu
        var = jnp.mean(yc * yc, axis=-1, keepdims=True)
        z = yc * lax.rsqrt(var + EPS) * lw_ref[...] + lb_ref[...]
        act = z * _sigmoid(z)
        cp_ref[0] = _dot(act.astype(BF16), wpw_ref[...]) + bpw_ref[...]

    def const(shape):
        return pl.BlockSpec(shape, lambda b, i: (0,) * len(shape))

    main, prev, nxt = _halo_specs(2 * DC, s)
    tile = pl.BlockSpec((1, TC, DC), lambda b, i: (b, i, 0))
    return pl.pallas_call(
        body, name="conv_fwd", grid=(bl, nt),
        in_specs=[main, prev, nxt, const((32, DC)), const((1, DC)), const((1, DC)), const((1, DC)),
                  const((DC, DC)), const((1, DC))],
        out_specs=[tile, tile],
        out_shape=[jax.ShapeDtypeStruct((bl, s, DC), F32)] * 2,
        scratch_shapes=[pltpu.VMEM((TC + 2 * HALO, DC), F32), pltpu.VMEM((7, PLANE_ROWS, DC), F32)],
        compiler_params=_params(("arbitrary", "arbitrary")),
    )(glu, glu, glu, conv_w, conv_b, ln_w, ln_b, w_pw_b, b_pw)


def _conv_bwd_depthwise(glu, dy, conv_w):
    bl, s, _ = glu.shape
    nt = s // TC

    def body(g_ref, gp_ref, gn_ref, d_ref, dp_ref, dn_ref, cw_ref, dglu_ref, dcw_ref,
             padu_ref, padd_ref, planes_u, planes_d):
        i = pl.program_id(1)

        @pl.when((pl.program_id(0) == 0) & (i == 0))
        def _():
            dcw_ref[...] = jnp.zeros_like(dcw_ref)

        first, last = i == 0, i == nt - 1
        _fill_padded(padu_ref, _glu(g_ref[0]), _glu(gp_ref[0]), _glu(gn_ref[0]), first, last)
        _fill_padded(padd_ref, d_ref[0], dp_ref[0], dn_ref[0], first, last)
        _shift_planes(padu_ref, planes_u)
        _shift_planes(padd_ref, planes_d)
        for ck in range(TC // CH):
            acc = jnp.zeros((CH, DC), F32)
            for t in range(KW):
                acc = acc + _tap_rows(padd_ref, planes_d, 2 * HALO - 1 - t, ck * CH, CH) * cw_ref[t:t + 1, :]
            g = g_ref[0, pl.ds(ck * CH, CH), :]
            a = g[:, 0:DC]
            sg = _sigmoid(g[:, DC:2 * DC])
            dglu_ref[0, pl.ds(ck * CH, CH), 0:DC] = (acc * sg).astype(BF16)
            dglu_ref[0, pl.ds(ck * CH, CH), DC:2 * DC] = (acc * a * sg * (1.0 - sg)).astype(BF16)
        group = 4
        for t0 in range(0, KW, group):
            taps = range(t0, min(t0 + group, KW))
            acc8 = [jnp.zeros((8, DC), F32) for _ in taps]
            for ck in range(TC // CH):
                dchunk = d_ref[0, pl.ds(ck * CH, CH), :]
                for n, t in enumerate(taps):
                    prod = _tap_rows(padu_ref, planes_u, 1 + t, ck * CH, CH) * dchunk
                    acc8[n] = acc8[n] + jnp.sum(prod.reshape(CH // 8, 8, DC), axis=0)
            for n, t in enumerate(taps):
                dcw_ref[t:t + 1, :] += jnp.sum(acc8[n], axis=0, keepdims=True)

    gmain, gprev, gnext = _halo_specs(2 * DC, s)
    dmain, dprev, dnext = _halo_specs(DC, s)
    cw = pl.BlockSpec((32, DC), lambda b, i: (0, 0))
    return pl.pallas_call(
        body, name="conv_bwd_depthwise", grid=(bl, nt),
        in_specs=[gmain, gprev, gnext, dmain, dprev, dnext, cw],
        out_specs=[gmain, cw],
        out_shape=[jax.ShapeDtypeStruct((bl, s, 2 * DC), BF16), jax.ShapeDtypeStruct((32, DC), F32)],
        scratch_shapes=[pltpu.VMEM((TC + 2 * HALO, DC), F32)] * 2 + [pltpu.VMEM((7, PLANE_ROWS, DC), F32)] * 2,
        compiler_params=_params(("arbitrary", "arbitrary")),
    )(glu, glu, glu, dy, dy, dy, conv_w)


def _out_fwd_bwd(attn, za, cp, zc, x, target, modrows, w_out_b, y_conv, ln_w, ln_b, w_pw_b):
    bl, s, _ = x.shape
    tm = OUT_TM

    def body(o_ref, za_ref, cp_ref, zc_ref, x_ref, t_ref, mod_ref, w_ref, y_ref, lw_ref, lb_ref, wpw_ref,
             do_ref, dza_ref, dy_ref, dzc_ref, dh_ref, dgate_ref, gwb_ref, loss_ref, gpwb_ref, rows_ref,
             gw_ref, gpw_ref):
        b, i = pl.program_id(0), pl.program_id(1)

        @pl.when((b == 0) & (i == 0))
        def _():
            gw_ref[...] = jnp.zeros_like(gw_ref)
            gpw_ref[...] = jnp.zeros_like(gpw_ref)
            rows_ref[...] = jnp.zeros_like(rows_ref)
            loss_ref[...] = jnp.zeros_like(loss_ref)

        @pl.when(i == 0)
        def _():
            dgate_ref[...] = jnp.zeros_like(dgate_ref)

        gate = mod_ref[0, 2:3, :]
        w = w_ref[...]
        o, za_v, cp_v, zc_v = o_ref[0], za_ref[0], cp_ref[0], zc_ref[0]
        sa = _sigmoid(za_v)
        sc = _sigmoid(zc_v)
        silu_a = za_v * sa
        silu_c = zc_v * sc
        mix = jnp.concatenate([(o * silu_a).astype(BF16), (cp_v * silu_c).astype(BF16)], axis=-1)
        out = _dot(mix, w)
        err = x_ref[0] + gate * out - t_ref[0]
        loss_ref[...] += jnp.sum(err * err, axis=0, keepdims=True)
        dh = err * (1.0 / D)
        dh_ref[0] = dh
        dgate_ref[0] += jnp.sum(dh * out, axis=0, keepdims=True)
        dout = (dh * gate).astype(BF16)
        gw_ref[...] += _dot_tn(mix, dout)
        dmix = _dot_nt(dout, w)
        dga = dmix[:, 0:DA]
        dgc = dmix[:, DA:DA + DC]
        dov = dga * silu_a
        for h in range(DA // HD):
            do_ref[0, h] = dov[:, h * HD:(h + 1) * HD].astype(BF16)
        dza_ref[0] = (dga * o * (sa * (1.0 + za_v * (1.0 - sa)))).astype(BF16)
        dzc_ref[0] = (dgc * cp_v * (sc * (1.0 + zc_v * (1.0 - sc)))).astype(BF16)
        dcp = dgc * silu_c
        y = y_ref[0]
        yc = y - jnp.mean(y, axis=-1, keepdims=True)
        rstd = lax.rsqrt(jnp.mean(yc * yc, axis=-1, keepdims=True) + EPS)
        yn = yc * rstd
        lw = lw_ref[...]
        z = yn * lw + lb_ref[...]
        sg = _sigmoid(z)
        dcp_b = dcp.astype(BF16)
        gpw_ref[...] += _dot_tn((z * sg).astype(BF16), dcp_b)
        dz = _dot_nt(dcp_b, wpw_ref[...]) * (sg * (1.0 + z * (1.0 - sg)))
        dyn = dz * lw
        dy = rstd * (dyn - jnp.mean(dyn, axis=-1, keepdims=True) - yn * jnp.mean(dyn * yn, axis=-1, keepdims=True))
        dy_ref[0] = dy
        rows_ref[0:1, :] += jnp.sum(dcp, axis=0, keepdims=True)
        rows_ref[1:2, :] += jnp.sum(dz * yn, axis=0, keepdims=True)
        rows_ref[2:3, :] += jnp.sum(dz, axis=0, keepdims=True)
        rows_ref[3:4, :] += jnp.sum(dy, axis=0, keepdims=True)

        @pl.when((b == bl - 1) & (i == s // tm - 1))
        def _():
            gwb_ref[...] = gw_ref[...].astype(BF16)
            gpwb_ref[...] = gpw_ref[...].astype(BF16)

    def const(shape):
        return pl.BlockSpec(shape, lambda b, i: (0,) * len(shape))

    def tile(w):
        return pl.BlockSpec((1, tm, w), lambda b, i: (b, i, 0))

    return pl.pallas_call(
        body, name="out_fwd_bwd", grid=(bl, s // tm),
        in_specs=[tile(DA), tile(DA), tile(DC), tile(DC), tile(D), tile(D),
                  pl.BlockSpec((1, 3, D), lambda b, i: (b, 0, 0)), const((D, D)),
                  tile(DC), const((1, DC)), const((1, DC)), const((DC, DC))],
        out_specs=[pl.BlockSpec((1, DA // HD, tm, HD), lambda b, i: (b, 0, i, 0)), tile(DA), tile(DC), tile(DC), tile(D),
                   pl.BlockSpec((1, 1, D), lambda b, i: (b, 0, 0)), const((D, D)), const((1, D)),
                   const((DC, DC)), const((8, DC))],
        out_shape=[jax.ShapeDtypeStruct((bl, DA // HD, s, HD), BF16), jax.ShapeDtypeStruct((bl, s, DA), BF16),
                   jax.ShapeDtypeStruct((bl, s, DC), F32), jax.ShapeDtypeStruct((bl, s, DC), BF16),
                   jax.ShapeDtypeStruct((bl, s, D), F32), jax.ShapeDtypeStruct((bl, 1, D), F32),
                   jax.ShapeDtypeStruct((D, D), BF16), jax.ShapeDtypeStruct((1, D), F32),
                   jax.ShapeDtypeStruct((DC, DC), BF16), jax.ShapeDtypeStruct((8, DC), F32)],
        scratch_shapes=[pltpu.VMEM((D, D), F32), pltpu.VMEM((DC, DC), F32)],
        compiler_params=_params(("arbitrary", "arbitrary")),
    )(attn, za, cp, zc, x, target, modrows, w_out_b, y_conv, ln_w, ln_b, w_pw_b)


def _rms_heads_bwd(dy, x, w_t, ones_bd):
    r = lax.rsqrt(_segsum(x * x, ones_bd) * (1.0 / HD) + EPS)
    xh = x * r
    g = dy * w_t
    dx = r * (g - xh * (_segsum(g * xh, ones_bd) * (1.0 / HD)))
    return dx, dy * xh


def _ctx_bwd(ctx, modc, norm_w, w_kv_b, pkv_c, dk_c, dv_c, knw_t, ones_bd):
    bl, cl, _ = ctx.shape

    def body(x_ref, mod_ref, nw_ref, w_ref, p_ref, dk_ref, dv_ref, knw_ref, bd_ref, gw_ref, rows_ref, dknw_ref):
        @pl.when(pl.program_id(0) == 0)
        def _():
            gw_ref[...] = jnp.zeros_like(gw_ref)
            rows_ref[...] = jnp.zeros_like(rows_ref)
            dknw_ref[...] = jnp.zeros_like(dknw_ref)

        xv = x_ref[0]
        shift = mod_ref[0, 0:1, :]
        scale = mod_ref[0, 1:2, :]
        nw = nw_ref[...]
        r = lax.rsqrt(jnp.mean(xv * xv, axis=-1, keepdims=True) + EPS)
        xn = xv * r
        yv = xn * nw
        u = yv * (1.0 + scale) + shift
        dkv = jnp.concatenate([dk_ref[0, 0], dk_ref[0, 1]], axis=-1)
        dpk, dknw = _rms_heads_bwd(dkv, p_ref[0][:, 0:KVW], knw_ref[...], bd_ref[...])
        dp = jnp.concatenate([dpk.astype(BF16), dv_ref[0, 0].astype(BF16), dv_ref[0, 1].astype(BF16)], axis=-1)
        gw_ref[...] += _dot_tn(dp, u.astype(BF16))
        du = _dot(dp, w_ref[...])
        rows_ref[0:1, :] += jnp.sum(du, axis=0, keepdims=True)
        rows_ref[1:2, :] += jnp.sum(du * yv, axis=0, keepdims=True)
        rows_ref[2:3, :] += jnp.sum(du * (1.0 + scale) * xn, axis=0, keepdims=True)
        dknw_ref[...] += jnp.sum(dknw, axis=0, keepdims=True)

    def const(shape):
        return pl.BlockSpec(shape, lambda b: (0,) * len(shape))

    def tile(w):
        return pl.BlockSpec((1, cl, w), lambda b: (b, 0, 0))

    ctx_block = (dk_c.shape[2] - cl) // cl
    kv_tile = pl.BlockSpec((1, KVW // HD, cl, HD), lambda b: (b, 0, ctx_block, 0))
    return pl.pallas_call(
        body, name="ctx_bwd", grid=(bl,),
        in_specs=[tile(D), const((1, 3, D)), const((1, D)), _KV_ROWS_OF_W_IN_T, tile(2 * KVW), kv_tile, kv_tile,
                  const((1, KVW)), const((KVW, KVW))],
        out_specs=[const((2 * KVW, D)), const((8, D)), const((1, KVW))],
        out_shape=[jax.ShapeDtypeStruct((2 * KVW, D), F32), jax.ShapeDtypeStruct((8, D), F32),
                   jax.ShapeDtypeStruct((1, KVW), F32)],
        compiler_params=_params(("arbitrary",)),
    )(ctx, modc, norm_w, w_kv_b, pkv_c, dk_c, dv_c, knw_t, ones_bd)


def _bwd_in(x, modrows, norm_w, w_in_b, cos, sins, qnw_t, knw_t, ones_bd,
            pq, pkv, dq, dk, dv, dza, dglu, dzc, dh, gw_kv):
    bl, s, _ = x.shape
    tm = TOKEN_PARTS * TM
    nt = s // tm

    def body(x_ref, mod_ref, nw_ref, win_hbm, cos_ref, sin_ref, qnw_ref, knw_ref, bd_ref,
             pq_ref, pkv_ref, dq_ref, dk_ref, dv_ref, dza_ref, dglu_ref, dzc_ref, dh_ref, gwkv_ref,
             gx_ref, gw_hbm, dmod_ref, dnw_ref, dqnw_ref, dknw_ref, win_ref, gw_acc, sem):
        b, i = pl.program_id(0), pl.program_id(1)

        @pl.when((b == 0) & (i == 0))
        def _():
            cp = pltpu.make_async_copy(win_hbm, win_ref, sem)
            cp.start()
            gw_acc[...] = jnp.zeros_like(gw_acc)
            dnw_ref[...] = jnp.zeros_like(dnw_ref)
            dqnw_ref[...] = jnp.zeros_like(dqnw_ref)
            dknw_ref[...] = jnp.zeros_like(dknw_ref)
            cp.wait()

        @pl.when(i == 0)
        def _():
            dmod_ref[...] = jnp.zeros_like(dmod_ref)

        bd = bd_ref[...]
        shift = mod_ref[0, 0:1, :]
        scale = mod_ref[0, 1:2, :]
        nw = nw_ref[...]
        dps, us = [], []
        for part in range(TOKEN_PARTS):
            rows = pl.ds(part * TM, TM)
            ck = cos_ref[rows, :]
            sk = sin_ref[rows, :]
            cs = jnp.concatenate([ck] * (DA // KVW), axis=-1)
            sn = jnp.concatenate([sk] * (DA // KVW), axis=-1)
            dqn = _rope_bwd(dq_ref[0, rows, :], cs, sn)
            dpq, dqnw = _rms_heads_bwd(dqn, pq_ref[0, rows, :], qnw_ref[...], bd)
            dkn = _rope_bwd(jnp.concatenate([dk_ref[0, 0, rows, :], dk_ref[0, 1, rows, :]], axis=-1), ck, sk)
            dpk, dknw = _rms_heads_bwd(dkn, pkv_ref[0, rows, 0:KVW], knw_ref[...], bd[0:KVW, 0:KVW])
            dqnw_ref[...] += jnp.sum(dqnw, axis=0, keepdims=True)
            dknw_ref[...] += jnp.sum(dknw, axis=0, keepdims=True)
            dp = jnp.concatenate(
                [dpq.astype(BF16), dpk.astype(BF16), dv_ref[0, 0, rows, :].astype(BF16), dv_ref[0, 1, rows, :].astype(BF16),
                 dza_ref[0, rows, :], dglu_ref[0, rows, :], dzc_ref[0, rows, :]], axis=-1)

            xv = x_ref[0, rows, :]
            r = lax.rsqrt(jnp.mean(xv * xv, axis=-1, keepdims=True) + EPS)
            xn = xv * r
            yv = xn * nw
            u = yv * (1.0 + scale) + shift
            dps.append(dp)
            us.append(u.astype(BF16))
            du = _dot(dp, win_ref[...])
            dmod_ref[0, 0:1, :] += jnp.sum(du, axis=0, keepdims=True)
            dmod_ref[0, 1:2, :] += jnp.sum(du * yv, axis=0, keepdims=True)
            dy = du * (1.0 + scale)
            dnw_ref[...] += jnp.sum(dy * xn, axis=0, keepdims=True)
            dxn = dy * nw
            gx_ref[0, rows, :] = dh_ref[0, rows, :] + r * (dxn - xn * jnp.mean(dxn * xn, axis=-1, keepdims=True))
        gw_acc[...] += _dot_tn(jnp.concatenate(dps, axis=0), jnp.concatenate(us, axis=0))

        @pl.when((b == bl - 1) & (i == nt - 1))
        def _():
            gw_acc[DA:DA + 2 * KVW, :] += gwkv_ref[...]

            def to_bf16(j, carry):
                rows = pl.ds(pl.multiple_of(j * 2 * KVW, 2 * KVW), 2 * KVW)
                win_ref[rows, :] = gw_acc[rows, :].astype(BF16)
                return carry

            lax.fori_loop(0, D_IN // (2 * KVW), to_bf16, 0)
            pltpu.sync_copy(win_ref, gw_hbm)

    def tile(w):
        return pl.BlockSpec((1, tm, w), lambda b, i: (b, i, 0))

    def const(shape):
        return pl.BlockSpec(shape, lambda b, i: (0,) * len(shape))

    anyspace = pl.BlockSpec(memory_space=pl.ANY)
    rope = pl.BlockSpec((tm, KVW), lambda b, i: (i, 0))
    kv_tile = pl.BlockSpec((1, KVW // HD, tm, HD), lambda b, i: (b, 0, i, 0))
    return pl.pallas_call(
        body, name="bwd_in", grid=(bl, nt),
        in_specs=[tile(D), pl.BlockSpec((1, 3, D), lambda b, i: (b, 0, 0)), const((1, D)), anyspace, rope, rope,
                  const((1, DA)), const((1, KVW)), const((DA, DA)),
                  tile(DA), tile(2 * KVW), tile(DA), kv_tile, kv_tile, tile(DA), tile(2 * DC), tile(DC), tile(D),
                  const((2 * KVW, D))],
        out_specs=[tile(D), anyspace, pl.BlockSpec((1, 2, D), lambda b, i: (b, 0, 0)), const((1, D)),
                   const((1, DA)), const((1, KVW))],
        out_shape=[jax.ShapeDtypeStruct((bl, s, D), F32), jax.ShapeDtypeStruct((D_IN, D), BF16),
                   jax.ShapeDtypeStruct((bl, 2, D), F32), jax.ShapeDtypeStruct((1, D), F32),
                   jax.ShapeDtypeStruct((1, DA), F32), jax.ShapeDtypeStruct((1, KVW), F32)],
        scratch_shapes=[pltpu.VMEM((D_IN, D), BF16), pltpu.VMEM((D_IN, D), F32), pltpu.SemaphoreType.DMA],
        compiler_params=_params(("arbitrary", "arbitrary")),
    )(x, modrows, norm_w, w_in_b, cos, sins, qnw_t, knw_t, ones_bd,
      pq, pkv, dq, dk, dv, dza, dglu, dzc, dh, gw_kv)


_LOSS, _DMODC, _NW, _QN, _KN, _CB, _LW, _LB, _BPW, SMALL_W = 0, 1024, 4096, 5120, 5248, 5376, 5888, 6400, 6912, 7424


ROW_W = 1792


def _put_flat(ref, off, value):
    n, done = value.shape[1], 0
    while done < n:
        r, c = divmod(off + done, ROW_W)
        take = min(n - done, ROW_W - c)
        ref[r:r + 1, c:c + take] = value[:, done:done + take]
        done += take


def _get_flat(arr, off, n):
    parts, done = [], 0
    while done < n:
        r, c = divmod(off + done, ROW_W)
        take = min(n - done, ROW_W - c)
        parts.append(arr[r:r + 1, c:c + take])
        done += take
    return parts[0] if len(parts) == 1 else jnp.concatenate(parts, axis=-1)


def _pack_small_body(loss_ref, ctx_ref, dnw_ref, dqnw_ref, dknw_ref, dknwc_ref, conv_ref, dss_ref, dgate_ref, o_ref):
    bl = dss_ref.shape[0]
    assert SMALL_W + bl * 3 * D <= 8 * ROW_W
    o_ref[...] = jnp.zeros_like(o_ref)
    _put_flat(o_ref, _LOSS, loss_ref[...])
    _put_flat(o_ref, _DMODC, ctx_ref[0:1, :])
    _put_flat(o_ref, _DMODC + D, ctx_ref[1:2, :])
    _put_flat(o_ref, _NW, dnw_ref[...] + ctx_ref[2:3, :])
    dq = dqnw_ref[...]
    qn = dq[:, 0:HD]
    for h in range(1, DA // HD):
        qn = qn + dq[:, h * HD:(h + 1) * HD]
    _put_flat(o_ref, _QN, qn)
    dk = dknw_ref[...] + dknwc_ref[...]
    _put_flat(o_ref, _KN, dk[:, 0:HD] + dk[:, HD:2 * HD])
    _put_flat(o_ref, _BPW, conv_ref[0:1, :])
    _put_flat(o_ref, _LW, conv_ref[1:2, :])
    _put_flat(o_ref, _LB, conv_ref[2:3, :])
    _put_flat(o_ref, _CB, conv_ref[3:4, :])
    for b in range(bl):
        _put_flat(o_ref, SMALL_W + b * 3 * D, dss_ref[b, 0:1, :])
        _put_flat(o_ref, SMALL_W + b * 3 * D + D, dss_ref[b, 1:2, :])
        _put_flat(o_ref, SMALL_W + b * 3 * D + 2 * D, dgate_ref[b])


_SMALL = (("b_mod", None), ("norm_w", _NW), ("q_norm_w", _QN), ("k_norm_w", _KN), ("conv_b", _CB),
          ("conv_ln_w", _LW), ("conv_ln_b", _LB), ("b_pw", _BPW), ("c_ctx", None))


def _epilogue(parts_in, pieces, c_rows, w_mod_loc):
    bl = pieces[7].shape[0]
    n_ex = N_DEV * bl
    n_mod = w_mod_loc.shape[1]
    rb = 32
    shp = parts_in.shape[1:]
    rows_in = shp[0]

    def body(*refs):
        it = iter(refs)
        take = lambda k: [next(it) for _ in range(k)]
        (parts,) = take(1)
        piece_refs = take(9)
        (c_ref, wm_ref) = take(2)
        (g_in, g_wm, sum_ref, gb_ref, gc_all, loss_ref) = take(6)
        (mine, got_sib, stage, got_chip, payload, gathered, dmod_full, gc_mine) = take(8)
        (d2d_send, d2d_recv, ici_send, ici_recv, local_sems, sg_send, sg_recv, gc_send, gc_recv, misc_sems) = take(10)

        x, y, c = _coords()
        me = _lin(x, y, c)
        sib = (x, y, 1 - c)
        home = 2 * x + y

        def rows_loop(fn):
            def step(i, carry):
                fn(pl.ds(pl.multiple_of(i * rb, rb), rb))
                return carry
            lax.fori_loop(0, rows_in // rb, step, 0)

        def direct_gather(src, dst, send_sems, recv_sems, local_sem):
            cps = [pltpu.make_async_copy(src, dst.at[me], local_sem)]
            for k in range(1, N_DEV):
                peer = (1 - x if k & 4 else x, 1 - y if k & 2 else y, 1 - c if k & 1 else c)
                cps.append(pltpu.make_async_remote_copy(
                    src_ref=src, dst_ref=dst.at[me], send_sem=send_sems.at[k - 1], recv_sem=recv_sems.at[k - 1],
                    device_id=peer, device_id_type=MESH_ID))
            for cp in cps:
                cp.start()
            return cps

        _pack_small_body(*piece_refs, payload)
        small_cps = direct_gather(payload, gathered, sg_send, sg_recv, misc_sems.at[0])

        local, d2d, ici = [], [], []
        for s in range(4):
            cp = pltpu.make_async_copy(parts.at[_lin(s // 2, s % 2, c)], mine.at[s], local_sems.at[s])
            cp.start()
            local.append(cp)
            rc = pltpu.make_async_remote_copy(
                src_ref=parts.at[_lin(s // 2, s % 2, 1 - c)], dst_ref=got_sib.at[s],
                send_sem=d2d_send.at[s], recv_sem=d2d_recv.at[s], device_id=sib, device_id_type=MESH_ID)
            rc.start()
            d2d.append(rc)

        for cp in small_cps[1:]:
            cp.wait_recv()
        small_cps[0].wait()
        tot = gathered[0]
        for j in range(1, N_DEV):
            tot = tot + gathered[j]
        summed = _get_flat(tot, 0, SMALL_W)
        dmod_full[...] = jnp.zeros_like(dmod_full)
        for j in range(N_DEV):
            arr = gathered[j]
            for b in range(bl):
                dmod_full[j * bl + b:j * bl + b + 1, :] = _get_flat(arr, SMALL_W + b * 3 * D, 3 * D)
        dmod_full[n_ex:n_ex + 1, :] = summed[:, _DMODC:_DMODC + 3 * D]
        sum_ref[...] = summed
        gb_ref[...] = jnp.sum(dmod_full[...], axis=0, keepdims=True)
        loss_ref[...] = (0.5 / D) * jnp.sum(summed[:, _LOSS:_LOSS + D], axis=-1, keepdims=True)

        north = c == 1
        first = (jnp.where(north, 1 - x, x), jnp.where(north, y, 1 - y))
        second = (jnp.where(north, x, 1 - x), jnp.where(north, 1 - y, y))
        for s in range(4):
            local[s].wait()
            d2d[s].wait_recv()

        def chip_sum(k, chip, relayed):
            slot = 2 * chip[0] + chip[1]

            def pair_sum(rs):
                acc = mine[slot, rs, :].astype(F32) + got_sib[slot, rs, :].astype(F32)
                if relayed:
                    acc = acc + got_chip[1, rs, :].astype(F32)
                stage[k, rs, :] = acc.astype(BF16)

            rows_loop(pair_sum)

        def send(k, to):
            rc = pltpu.make_async_remote_copy(
                src_ref=stage.at[k], dst_ref=got_chip.at[k], send_sem=ici_send.at[k], recv_sem=ici_recv.at[k],
                device_id=(to[0], to[1], c), device_id_type=MESH_ID)
            rc.start()
            ici.append(rc)

        chip_sum(0, first, False)
        send(0, first)
        chip_sum(1, (1 - x, 1 - y), False)
        send(1, first)

        cr = c_ref[...]
        act = (cr * _sigmoid(cr)).astype(BF16)
        dm = dmod_full[:, pl.ds(pl.multiple_of(me * n_mod, 128), n_mod)].astype(BF16)
        g_wm[...] = _dot_tn(act, dm)
        gc_mine[...] = _dot_nt(dm[n_ex:n_ex + 8, :], wm_ref[...].astype(BF16))
        gc_cps = direct_gather(gc_mine, gc_all, gc_send, gc_recv, misc_sems.at[1])

        ici[1].wait_recv()
        chip_sum(2, second, True)
        send(2, second)
        ici[0].wait_recv()
        ici[2].wait_recv()

        def finish(rs):
            gsum = mine[home, rs, :].astype(F32) + got_sib[home, rs, :].astype(F32)
            g_in[rs, :] = gsum + got_chip[0, rs, :].astype(F32) + got_chip[2, rs, :].astype(F32)

        rows_loop(finish)

        for cp in gc_cps[1:]:
            cp.wait_recv()
        gc_cps[0].wait()
        for rc in d2d + ici + small_cps[1:] + gc_cps[1:]:
            rc.wait_send()

    vm = pl.BlockSpec(memory_space=pltpu.VMEM)
    anyspace = pl.BlockSpec(memory_space=pl.ANY)
    assert rows_in % rb == 0 and parts_in.dtype == BF16
    args = [parts_in, *pieces, c_rows, w_mod_loc]
    in_specs = [anyspace] + [vm] * (len(args) - 1)
    out_shape = [jax.ShapeDtypeStruct(shp, F32), jax.ShapeDtypeStruct(w_mod_loc.shape, F32),
                 jax.ShapeDtypeStruct((1, SMALL_W), F32), jax.ShapeDtypeStruct((1, 3 * D), F32),
                 jax.ShapeDtypeStruct((N_DEV, 8, D), F32), jax.ShapeDtypeStruct((1, 1), F32)]
    scratch = [pltpu.VMEM((4,) + shp, BF16), pltpu.VMEM((4,) + shp, BF16), pltpu.VMEM((3,) + shp, BF16),
               pltpu.VMEM((3,) + shp, BF16), pltpu.VMEM((8, ROW_W), F32), pltpu.VMEM((N_DEV, 8, ROW_W), F32),
               pltpu.VMEM((n_ex + 8, 3 * D), F32), pltpu.VMEM((8, D), F32),
               pltpu.SemaphoreType.DMA((4,)), pltpu.SemaphoreType.DMA((4,)), pltpu.SemaphoreType.DMA((3,)),
               pltpu.SemaphoreType.DMA((3,)), pltpu.SemaphoreType.DMA((4,)),
               pltpu.SemaphoreType.DMA((N_DEV - 1,)), pltpu.SemaphoreType.DMA((N_DEV - 1,)),
               pltpu.SemaphoreType.DMA((N_DEV - 1,)), pltpu.SemaphoreType.DMA((N_DEV - 1,)),
               pltpu.SemaphoreType.DMA((2,))]
    return pl.pallas_call(
        body, name="epilogue", out_shape=out_shape, in_specs=in_specs, out_specs=[vm] * len(out_shape),
        scratch_shapes=scratch, compiler_params=pltpu.CompilerParams(vmem_limit_bytes=VMEM_LIMIT),
    )(*args)


def _final_adamw(tiled, streamed_sums, local_sums, summed, g_bmod, gc_all, small_w, small_m, small_v):
    ns = len(_SMALL)
    n_tiled, n_sums, n_local = len(tiled), len(streamed_sums), len(local_sums)
    streams = list(tiled) + list(streamed_sums)
    n_results = [3] * n_tiled + [4] * n_sums
    tile_jobs = [[(a, t * (w.shape[0] // ADAM_STEPS), w.shape[0] // ADAM_STEPS) for a, (_, w, _, _) in enumerate(tiled)]
                 for t in range(ADAM_STEPS)]
    sum_jobs = [(n_tiled + a, 0, w.shape[0]) for a, (_, w, _, _) in enumerate(streamed_sums)]
    jobs = sum(tile_jobs[:ADAM_STEPS // 2], []) + sum_jobs + sum(tile_jobs[ADAM_STEPS // 2:], [])
    for _, w, _, _ in tiled:
        assert w.shape[0] % (8 * ADAM_STEPS) == 0
    for got, w, _, _ in list(streamed_sums) + list(local_sums):
        assert got.shape[0] == N_DEV and got.shape[1] >= w.shape[0] and got.shape[2] == w.shape[-1]
    n_reads = 4 * len(jobs)
    n_writes = sum(n_results[a] for a, _, _ in jobs)

    def body(*refs):
        it = iter(refs)
        take = lambda k: [next(it) for _ in range(k)]
        src = [take(4) for _ in streams]
        loc = [take(4) for _ in range(n_local)]
        sum_ref, gb_ref, gc_ref = take(3)
        sw, sm, sv = take(ns), take(ns), take(ns)
        dst = [take(k) for k in n_results]
        loc_out = [take(4) for _ in range(n_local)]
        souts = take(4 * ns)
        src_buf = [take(4) for _ in streams]
        dst_buf = [take(k) for k in n_results]
        read_sem, write_sem = take(2)

        def rows_of(ref, r0, nr):
            return ref.at[:, pl.ds(r0, nr), :] if len(ref.shape) == 3 else ref.at[pl.ds(r0, nr), :]

        reads = [[pltpu.make_async_copy(rows_of(src[a][k], r0, nr), rows_of(src_buf[a][k], r0, nr),
                                        read_sem.at[4 * j + k]) for k in range(4)]
                 for j, (a, r0, nr) in enumerate(jobs)]
        for job_reads in reads:
            for cp in job_reads:
                cp.start()

        def summed_shares(got, n_rows):
            g = got[0].astype(F32)
            for d in range(1, N_DEV):
                g = g + got[d].astype(F32)
            return g[0:n_rows, :]

        for (got, w_r, m_r, v_r), outs4 in zip(loc, loc_out):
            g = summed_shares(got, w_r.shape[0])
            if len(w_r.shape) == 3:
                for r in range(w_r.shape[0]):
                    g_row = g[r:r + 1, :]
                    for o_r, val in zip(outs4, (g_row,) + _adamw(w_r[r], g_row, m_r[r], v_r[r])):
                        o_r[r] = val
            else:
                for o_r, val in zip(outs4, (g,) + _adamw(w_r[...], g, m_r[...], v_r[...])):
                    o_r[...] = val
        for k, (name, off) in enumerate(_SMALL):
            w = sw[k][...]
            if name == "b_mod":
                gk = gb_ref[...]
            elif name == "c_ctx":
                acc = gc_ref[0, 0:1, :]
                for j in range(1, N_DEV):
                    acc = acc + gc_ref[j, 0:1, :]
                sg = _sigmoid(w)
                gk = acc * (sg * (1.0 + w * (1.0 - sg)))
            else:
                gk = sum_ref[:, off:off + w.shape[1]]
            dl, m_new, v_new = _adamw(w, gk, sm[k][...], sv[k][...])
            souts[k][...] = gk
            souts[ns + k][...] = dl
            souts[2 * ns + k][...] = m_new
            souts[3 * ns + k][...] = v_new

        writes, n_started = [], 0
        for j, (a, r0, nr) in enumerate(jobs):
            for cp in reads[j]:
                cp.wait()
            rows = pl.ds(r0, nr)
            g_b, w_b, m_b, v_b = src_buf[a]
            if a < n_tiled:
                g = g_b[rows, :]
                vals = _adamw(w_b[rows, :], g, m_b[rows, :], v_b[rows, :])
            else:
                g = summed_shares(g_b, nr)
                vals = (g,) + _adamw(w_b[...], g, m_b[...], v_b[...])
            for k, val in enumerate(vals):
                dst_buf[a][k][rows, :] = val
                cp = pltpu.make_async_copy(rows_of(dst_buf[a][k], r0, nr), rows_of(dst[a][k], r0, nr),
                                           write_sem.at[n_started])
                cp.start()
                writes.append(cp)
                n_started += 1
        for cp in writes:
            cp.wait()

    vm = pl.BlockSpec(memory_space=pltpu.VMEM)
    anyspace = pl.BlockSpec(memory_space=pl.ANY)
    args, out_shape, scratch = [], [], []
    for item in streams:
        args += list(item)
        scratch += [pltpu.VMEM(a.shape, a.dtype) for a in item]
    for item in local_sums:
        args += list(item)
    args += [summed, g_bmod, gc_all, *small_w, *small_m, *small_v]
    for (_, w, _, _), k in zip(streams, n_results):
        out_shape += [jax.ShapeDtypeStruct(w.shape, F32)] * k
    scratch += [pltpu.VMEM(s.shape, F32) for s in out_shape]
    n_streamed_out = len(out_shape)
    for _, w, _, _ in local_sums:
        out_shape += [jax.ShapeDtypeStruct(w.shape, F32)] * 4
    out_shape += [jax.ShapeDtypeStruct(w.shape, F32) for w in small_w] * 4
    scratch += [pltpu.SemaphoreType.DMA((n_reads,)), pltpu.SemaphoreType.DMA((n_writes,))]
    outs = pl.pallas_call(
        body, name="final_adamw", out_shape=out_shape,
        in_specs=[anyspace] * (4 * len(streams)) + [vm] * (len(args) - 4 * len(streams)),
        out_specs=[anyspace] * n_streamed_out + [vm] * (len(out_shape) - n_streamed_out),
        scratch_shapes=scratch, compiler_params=pltpu.CompilerParams(vmem_limit_bytes=VMEM_LIMIT),
    )(*args)
    it = iter(outs)
    take = lambda k: tuple(next(it) for _ in range(k))
    r_tiled = [(item[0],) + take(3) for item in tiled]
    r_sums = [take(4) for _ in streamed_sums]
    r_local = [take(4) for _ in local_sums]
    small_outs = [list(take(ns)) for _ in range(4)]
    return r_tiled, r_sums, r_local, small_outs


def _rope_tables(s):
    t = jnp.arange(s, dtype=jnp.int32)
    row = (t // GRID_W).astype(F32)
    col = (t % GRID_W).astype(F32)
    freqs = ROPE_THETA ** (-jnp.arange(0, HD // 2, 2, dtype=F32) / (HD // 2))
    ang_r = row[:, None] * freqs[None, :]
    ang_c = col[:, None] * freqs[None, :]
    cr, sr, cc, sc = jnp.cos(ang_r), jnp.sin(ang_r), jnp.cos(ang_c), jnp.sin(ang_c)
    cos = jnp.concatenate([cr, cr, cc, cc], axis=-1)
    sins = jnp.concatenate([-sr, sr, -sc, sc], axis=-1)
    return jnp.tile(cos, (1, KVW // HD)), jnp.tile(sins, (1, KVW // HD))


def kernel(x, c, ctx, c_ctx, w_mod, b_mod, norm_w, w_in, q_norm_w, k_norm_w, conv_w, conv_b, conv_ln_w, conv_ln_b, w_pw, b_pw, w_out, loss_target, m_c_ctx, m_w_mod, m_b_mod, m_norm_w, m_w_in, m_q_norm_w, m_k_norm_w, m_conv_w, m_conv_b, m_conv_ln_w, m_conv_ln_b, m_w_pw, m_b_pw, m_w_out, v_c_ctx, v_w_mod, v_b_mod, v_norm_w, v_w_in, v_q_norm_w, v_k_norm_w, v_conv_w, v_conv_b, v_conv_ln_w, v_conv_ln_b, v_w_pw, v_b_pw, v_w_out):
    bl, s, _ = x.shape
    cl = ctx.shape[1]
    me = _lin(*_coords())

    conv_w_pad = jnp.pad(conv_w[0], ((0, 32 - KW), (0, 0)))
    n_ex = N_DEV * bl
    g_win, c_rows, g_mod = _prologue(w_in[0].T, c, c_ctx[None, :], w_mod[0], b_mod)
    w_in_b = g_win.reshape(D_IN, D)
    mod_all = g_mod.transpose(1, 0, 2).reshape(n_ex + 8, 3 * D)
    modrows = lax.dynamic_slice_in_dim(mod_all, me * bl, bl, axis=0).reshape(bl, 3, D)
    modc = mod_all[n_ex].reshape(1, 3, D)

    cos, sins = _rope_tables(s)
    qnw_t = jnp.tile(q_norm_w, (1, DA // HD))
    knw_t = jnp.tile(k_norm_w, (1, KVW // HD))
    lane = jnp.arange(DA, dtype=jnp.int32) // HD
    ones_bd = (lane[:, None] == lane[None, :]).astype(BF16)
    ones_kv = ones_bd[0:KVW, 0:KVW]
    w_kv_b = w_in_b

    k_ctx, v_ctx, pkv_c = _ctx_fwd(ctx, modc, norm_w, w_kv_b, knw_t, ones_kv, cl + s)
    (q_h, k_h, v_h, pq, pkv, za, glu, zc), (g_wout, g_wpw, g_cw) = _fwd_in(
        x, modrows, norm_w, w_in_b, cos, sins, qnw_t, knw_t, ones_bd, k_ctx, v_ctx,
        [w_out[0], w_pw[0], conv_w_pad], [BF16, BF16, F32])
    w_out_b = g_wout.reshape(D, D)
    w_pw_b = g_wpw.reshape(DC, DC)
    conv_w_full = g_cw.transpose(1, 0, 2).reshape(32, DC)
    attn, lse = _attn_fwd(q_h, k_h, v_h)
    y_conv, cp = _conv_fwd(glu, conv_w_full, conv_b, conv_ln_w, conv_ln_b, w_pw_b, b_pw)

    do_h, dza, dy_conv, dzc, dh, dgate, gw_out, loss_row, gw_pw, conv_rows = _out_fwd_bwd(
        attn, za, cp, zc, x, loss_target, modrows, w_out_b, y_conv, conv_ln_w, conv_ln_b, w_pw_b)
    dglu, g_cw_full = _conv_bwd_depthwise(glu, dy_conv, conv_w_full)
    parts_out = gw_out.reshape(N_DEV, D // N_DEV, D)
    parts_pw = gw_pw.reshape(N_DEV, DC // N_DEV, DC)
    parts_cw = g_cw_full.astype(BF16).reshape(32, N_DEV, DC // N_DEV).transpose(1, 0, 2)
    (dq, dk_h, dv_h), (got_out, got_pw, got_cw) = _attn_bwd(
        q_h, k_h, v_h, do_h, attn, lse, [parts_out, parts_pw, parts_cw])
    gw_kv, ctx_rows, dknw_c = _ctx_bwd(ctx, modc, norm_w, w_kv_b, pkv_c, dk_h, dv_h, knw_t, ones_kv)
    grad_x, gw_in, dmod_ss, dnw, dqnw, dknw = _bwd_in(
        x, modrows, norm_w, w_in_b, cos, sins, qnw_t, knw_t, ones_bd,
        pq, pkv, dq, dk_h, dv_h, dza, dglu, dzc, dh, gw_kv)

    given = {"c_ctx": (c_ctx, m_c_ctx, v_c_ctx), "b_mod": (b_mod, m_b_mod, v_b_mod), "norm_w": (norm_w, m_norm_w, v_norm_w),
             "q_norm_w": (q_norm_w, m_q_norm_w, v_q_norm_w), "k_norm_w": (k_norm_w, m_k_norm_w, v_k_norm_w),
             "conv_b": (conv_b, m_conv_b, v_conv_b), "conv_ln_w": (conv_ln_w, m_conv_ln_w, v_conv_ln_w),
             "conv_ln_b": (conv_ln_b, m_conv_ln_b, v_conv_ln_b), "b_pw": (b_pw, m_b_pw, v_b_pw)}
    as_rows = [[given[name][which].reshape(1, -1) for name, _ in _SMALL] for which in range(3)]
    g_in_t, g_wmod, summed, g_bmod, gc_all, loss11 = _epilogue(
        gw_in.reshape(N_DEV, D_IN // N_DEV, D),
        [loss_row, ctx_rows, dnw, dqnw, dknw, dknw_c, conv_rows, dmod_ss, dgate], c_rows, w_mod[0])
    (r_in, r_wmod), (r_out,), (r_pw, r_cw), small_outs = _final_adamw(
        [(g_in_t, w_in[0].T, m_w_in[0].T, v_w_in[0].T), (g_wmod, w_mod[0], m_w_mod[0], v_w_mod[0])],
        [(got_out, w_out[0], m_w_out[0], v_w_out[0])],
        [(got_pw, w_pw[0], m_w_pw[0], v_w_pw[0]),
         (got_cw, conv_w.transpose(1, 0, 2), m_conv_w.transpose(1, 0, 2), v_conv_w.transpose(1, 0, 2))],
        summed, g_bmod, gc_all, *as_rows)
    r_in = tuple(a.T for a in r_in)
    r_cw = tuple(a.transpose(1, 0, 2)[0] for a in r_cw)

    big = {"w_mod": r_wmod, "w_in": r_in, "conv_w": r_cw, "w_pw": r_pw, "w_out": r_out}
    order = ["c_ctx", "w_mod", "b_mod", "norm_w", "w_in", "q_norm_w", "k_norm_w", "conv_w", "conv_b", "conv_ln_w",
             "conv_ln_b", "w_pw", "b_pw", "w_out"]
    small_index = {name: k for k, (name, _) in enumerate(_SMALL)}
    outs = [loss11.reshape(()), grad_x]
    for which in range(4):
        for name in order:
            if name in big:
                outs.append(big[name][which][None])
            else:
                outs.append(small_outs[which][small_index[name]].reshape(given[name][0].shape))
    return tuple(outs)
```

```python
import jax
import jax.numpy as jnp
from jax import lax
from jax.experimental import pallas as pl
from jax.experimental.pallas import tpu as pltpu

F32, BF16 = jnp.float32, jnp.bfloat16
MESH_ID = pl.DeviceIdType.MESH

N_DEV = 8
D = 1024
D_IN = 2816
DA = 512
DC = 512
HD = 64
KVW = 128
KW = 31
HALO = 16
EPS = 1e-6
ROPE_THETA = 10000.0
GRID_W = 64

ADAM_LR, ADAM_B1, ADAM_B2, ADAM_EPS, ADAM_WD, ADAM_STEP = 0.001, 0.9, 0.999, 1e-08, 0.01, 10

VMEM_LIMIT = 56 * 1024 * 1024

TM = 256
TQ = 128
TOKEN_PARTS = 2
OUT_TM = 512
BWD_PARTS = 4
FWD_PARTS = 8
TC = 512
CH = 32
ADAM_STEPS = 4


def _params(sem, vmem=VMEM_LIMIT):
    return pltpu.CompilerParams(dimension_semantics=sem, vmem_limit_bytes=vmem)


def _dot(a, b):
    return jnp.dot(a, b, preferred_element_type=F32)


def _dot_nt(a, b):
    return lax.dot_general(a, b, (((1,), (1,)), ((), ())), preferred_element_type=F32)


def _dot_tn(a, b):
    return lax.dot_general(a, b, (((0,), (0,)), ((), ())), preferred_element_type=F32)


def _sigmoid(z):
    return 1.0 / (1.0 + jnp.exp(-z))


def _segsum(v, ones_bd):
    return _dot(v.astype(BF16), ones_bd)


def _swap16(x):
    w = x.shape[-1]
    lane = lax.broadcasted_iota(jnp.int32, x.shape, 1)
    return jnp.where((lane % 32) < 16, pltpu.roll(x, w - 16, 1), pltpu.roll(x, 16, 1))


def _with_ones_column(v):
    one = (lax.broadcasted_iota(jnp.int32, v.shape, 1) == 0).astype(v.dtype)
    return jnp.concatenate([v, one], axis=-1)


def _rope(x, cos, sins):
    return x * cos + _swap16(x) * sins


def _rope_bwd(d, cos, sins):
    return d * cos + _swap16(d * sins)


def _adamw(w, g, m, v):
    m2 = ADAM_B1 * m + (1.0 - ADAM_B1) * g
    v2 = ADAM_B2 * v + (1.0 - ADAM_B2) * (g * g)
    m_hat = m2 / (1.0 - ADAM_B1 ** ADAM_STEP)
    v_hat = v2 / (1.0 - ADAM_B2 ** ADAM_STEP)
    delta = -ADAM_LR * (m_hat / (jnp.sqrt(v_hat) + ADAM_EPS) + ADAM_WD * w)
    return delta, m2, v2


def _coords():
    return lax.axis_index("x"), lax.axis_index("y"), lax.axis_index("c")


def _lin(x, y, c):
    return 4 * x + 2 * y + c


def _prologue(w_in_t, c, c_ctx_row, w_mod_loc, b_mod):
    bl = c.shape[0]
    n_ex = N_DEV * bl
    n_mod = w_mod_loc.shape[1]

    def body(w32_ref, c_in_ref, cctx_ref, wm_ref, b_ref, out_w, crows_ref, mod_out, w_ref, c_ref, c_gath, mod_mine,
             w_send, w_recv, c_send, c_recv, m_send, m_recv, local_sems):
        x, y, c = _coords()
        me_lin = _lin(x, y, c)
        c_ref[...] = jnp.zeros_like(c_ref)
        c_ref[0:bl, :] = c_in_ref[...]
        w_ref[...] = w32_ref[...].astype(BF16)
        me, sib = (x, y, c), (x, y, 1 - c)
        xnb, ynb, diag = (1 - x, y), (x, 1 - y), (1 - x, 1 - y)
        north = c == 1

        def direct_gather(src, dst, send_sems, recv_sems, local_sem):
            cps = [pltpu.make_async_copy(src, dst.at[me_lin], local_sem)]
            for k in range(1, N_DEV):
                peer = (1 - x if k & 4 else x, 1 - y if k & 2 else y, 1 - c if k & 1 else c)
                cps.append(pltpu.make_async_remote_copy(
                    src_ref=src, dst_ref=dst.at[me_lin], send_sem=send_sems.at[k - 1], recv_sem=recv_sems.at[k - 1],
                    device_id=peer, device_id_type=MESH_ID))
            for cp in cps:
                cp.start()
            return cps

        def copy(k, block, to, src=None):
            slot = out_w.at[_lin(*block)]
            return pltpu.make_async_remote_copy(
                src_ref=slot if src is None else src, dst_ref=slot, send_sem=w_send.at[k], recv_sem=w_recv.at[k],
                device_id=to, device_id_type=MESH_ID)

        c_cps = direct_gather(c_ref, c_gath, c_send, c_recv, local_sems.at[0])
        mine = pltpu.make_async_copy(w_ref, out_w.at[me_lin], local_sems.at[1])
        mine.start()
        first = [copy(0, me, sib, src=w_ref), copy(1, me, (*xnb, c), src=w_ref), copy(2, me, (*ynb, c), src=w_ref)]
        for cp in first:
            cp.start()

        for cp in c_cps[1:]:
            cp.wait_recv()
        c_cps[0].wait()
        crows_ref[...] = jnp.zeros_like(crows_ref)
        for j in range(N_DEV):
            crows_ref[j * bl:(j + 1) * bl, :] = c_gath[j, 0:bl, :]
        crows_ref[n_ex:n_ex + 1, :] = cctx_ref[...]
        cr = crows_ref[...]
        act = (cr * _sigmoid(cr)).astype(BF16)
        mod_mine[...] = _dot(act, wm_ref[...].astype(BF16)) + b_ref[:, pl.ds(pl.multiple_of(me_lin * n_mod, 128), n_mod)]
        mod_cps = direct_gather(mod_mine, mod_out, m_send, m_recv, local_sems.at[2])

        relay_north = copy(3, (*xnb, c), (*ynb, c))
        relay_south = copy(3, (*ynb, c), (*xnb, c))
        passed = []
        copy(1, (*xnb, c), me).wait_recv()
        pl.when(north)(relay_north.start)
        passed.append(copy(4, (*xnb, c), sib))
        passed[-1].start()
        copy(2, (*ynb, c), me).wait_recv()
        pl.when(jnp.logical_not(north))(relay_south.start)
        passed.append(copy(5, (*ynb, c), sib))
        passed[-1].start()
        copy(3, (*diag, c), me).wait_recv()
        passed.append(copy(6, (*diag, c), sib))
        passed[-1].start()
        copy(0, sib, me).wait_recv()
        for k, chip in ((4, xnb), (5, ynb), (6, diag)):
            copy(k, (*chip, 1 - c), me).wait_recv()
        for cp in mod_cps[1:]:
            cp.wait_recv()
        mod_cps[0].wait()
        for cp in first + passed + [relay_north] + c_cps[1:] + mod_cps[1:]:
            cp.wait_send()
        mine.wait()

    vm = pl.BlockSpec(memory_space=pltpu.VMEM)
    seven = pltpu.SemaphoreType.DMA((N_DEV - 1,))
    return pl.pallas_call(
        body, name="prologue",
        out_shape=[jax.ShapeDtypeStruct((N_DEV,) + w_in_t.shape, BF16), jax.ShapeDtypeStruct((n_ex + 8, D), F32),
                   jax.ShapeDtypeStruct((N_DEV, n_ex + 8, n_mod), F32)],
        in_specs=[vm] * 5, out_specs=[pl.BlockSpec(memory_space=pl.ANY), vm, vm],
        scratch_shapes=[pltpu.VMEM(w_in_t.shape, BF16), pltpu.VMEM((8, D), F32), pltpu.VMEM((N_DEV, 8, D), F32),
                        pltpu.VMEM((n_ex + 8, n_mod), F32),
                        seven, seven, seven, seven, seven, seven, pltpu.SemaphoreType.DMA((3,))],
        compiler_params=pltpu.CompilerParams(vmem_limit_bytes=VMEM_LIMIT),
    )(w_in_t, c, c_ctx_row, w_mod_loc, b_mod)


def _exchange_copies(in_refs, out_refs, send_sems, recv_sems, local_sems, scatter):
    x, y, c = _coords()
    me = _lin(x, y, c)
    local, remote = [], []
    for a, (src, dst) in enumerate(zip(in_refs, out_refs)):
        local.append(pltpu.make_async_copy(src.at[me] if scatter else src, dst.at[me], local_sems.at[a]))
        for k in range(1, N_DEV):
            peer = (1 - x if k & 4 else x, 1 - y if k & 2 else y, 1 - c if k & 1 else c)
            remote.append(pltpu.make_async_remote_copy(
                src_ref=src.at[_lin(*peer)] if scatter else src, dst_ref=dst.at[me],
                send_sem=send_sems.at[a * (N_DEV - 1) + k - 1], recv_sem=recv_sems.at[a * (N_DEV - 1) + k - 1],
                device_id=peer, device_id_type=MESH_ID))
    return local, remote


def _exchange_scratch(n):
    return [pltpu.SemaphoreType.DMA((n * (N_DEV - 1),)), pltpu.SemaphoreType.DMA((n * (N_DEV - 1),)),
            pltpu.SemaphoreType.DMA((n,))]


def _fwd_in(x, modrows, norm_w, w_in_b, cos, sins, qnw_t, knw_t, ones_bd, k_all, v_all, shards, wire_dtypes):
    bl, s, _ = x.shape
    tm = TOKEN_PARTS * TM
    nt = s // tm
    n_sh = len(shards)

    def body(*refs):
        (x_ref, mod_ref, nw_ref, win_ref, cos_ref, sin_ref, qnw_ref, knw_ref, bd_ref, kin_ref, vin_ref) = refs[:11]
        shard_refs = refs[11:11 + n_sh]
        q_ref, k_ref, v_ref, pq_ref, pkv_ref, za_ref, glu_ref, zc_ref = refs[11 + n_sh:19 + n_sh]
        gathered_refs = refs[19 + n_sh:19 + 2 * n_sh]
        stage_refs = refs[19 + 2 * n_sh:19 + 3 * n_sh]
        send_sems, recv_sems, local_sems = refs[19 + 3 * n_sh:]
        b, i = pl.program_id(0), pl.program_id(1)
        local, remote = _exchange_copies(stage_refs, gathered_refs, send_sems, recv_sems, local_sems, scatter=False)

        @pl.when((b == 0) & (i == 0))
        def _():
            for src, stage in zip(shard_refs, stage_refs):
                stage[...] = src[...].astype(stage.dtype)
            for cp in local + remote:
                cp.start()

        shift = mod_ref[0, 0:1, :]
        scale = mod_ref[0, 1:2, :]
        for part in range(TOKEN_PARTS):
            rows = pl.ds(part * TM, TM)
            xv = x_ref[0, rows, :]
            r = lax.rsqrt(jnp.mean(xv * xv, axis=-1, keepdims=True) + EPS)
            u = (xv * r * nw_ref[...]) * (1.0 + scale) + shift
            p = _dot_nt(u.astype(BF16), win_ref[...])
            pq = p[:, 0:DA]
            pk = p[:, DA:DA + HD * 2]
            ck = cos_ref[rows, :]
            sk = sin_ref[rows, :]
            cs = jnp.concatenate([ck] * (DA // KVW), axis=-1)
            sn = jnp.concatenate([sk] * (DA // KVW), axis=-1)
            rq = lax.rsqrt(_segsum(pq * pq, bd_ref[...]) * (1.0 / HD) + EPS)
            qn = pq * rq * qnw_ref[...]
            qr = _rope(qn, cs, sn) * 0.125
            for h in range(DA // HD):
                q_ref[0, h, rows, :] = qr[:, h * HD:(h + 1) * HD].astype(BF16)
            rk = lax.rsqrt(_segsum(pk * pk, bd_ref[0:KVW, 0:KVW]) * (1.0 / HD) + EPS)
            kn = pk * rk * knw_ref[...]
            kr = _rope(kn, ck, sk)
            pv = p[:, 640:768]
            for h in range(KVW // HD):
                k_ref[0, h, rows, :] = kr[:, h * HD:(h + 1) * HD].astype(BF16)
                v_ref[0, h, rows, :] = _with_ones_column(pv[:, h * HD:(h + 1) * HD]).astype(BF16)
            pq_ref[0, rows, :] = pq
            pkv_ref[0, rows, :] = p[:, 512:768]
            za_ref[0, rows, :] = p[:, 768:1280]
            glu_ref[0, rows, :] = p[:, 1280:2304]
            zc_ref[0, rows, :] = p[:, 2304:2816]

        @pl.when((b == bl - 1) & (i == nt - 1))
        def _():
            for cp in remote:
                cp.wait_recv()
            for cp in remote:
                cp.wait_send()
            for cp in local:
                cp.wait()

    def tile(w):
        return pl.BlockSpec((1, tm, w), lambda b, i: (b, i, 0))

    def const(shape):
        return pl.BlockSpec(shape, lambda b, i: (0,) * len(shape))

    outs = [(DA, F32), (2 * KVW, F32), (DA, F32), (2 * DC, F32), (DC, F32)]
    anyspace = pl.BlockSpec(memory_space=pl.ANY)
    rope = pl.BlockSpec((tm, KVW), lambda b, i: (i, 0))
    k_tile = pl.BlockSpec((1, KVW // HD, tm, HD), lambda b, i: (b, 0, i, 0))
    v_tile = pl.BlockSpec((1, KVW // HD, tm, 2 * HD), lambda b, i: (b, 0, i, 0))
    res = pl.pallas_call(
        body, name="fwd_in", grid=(bl, nt),
        in_specs=[tile(D), pl.BlockSpec((1, 3, D), lambda b, i: (b, 0, 0)), const((1, D)), const((D_IN, D)),
                  rope, rope, const((1, DA)), const((1, KVW)), const((DA, DA)), anyspace, anyspace]
        + [const(a.shape) for a in shards],
        out_specs=[pl.BlockSpec((1, DA // HD, tm, HD), lambda b, i: (b, 0, i, 0)), k_tile, v_tile]
        + [tile(w) for w, _ in outs] + [anyspace] * n_sh,
        out_shape=[jax.ShapeDtypeStruct((bl, DA // HD, s, HD), BF16), jax.ShapeDtypeStruct(k_all.shape, BF16),
                   jax.ShapeDtypeStruct(v_all.shape, BF16)]
        + [jax.ShapeDtypeStruct((bl, s, w), dt) for w, dt in outs]
        + [jax.ShapeDtypeStruct((N_DEV,) + a.shape, dt) for a, dt in zip(shards, wire_dtypes)],
        input_output_aliases={9: 1, 10: 2},
        scratch_shapes=[pltpu.VMEM(a.shape, dt) for a, dt in zip(shards, wire_dtypes)] + _exchange_scratch(n_sh),
        compiler_params=_params(("arbitrary", "arbitrary")),
    )(x, modrows, norm_w, w_in_b, cos, sins, qnw_t, knw_t, ones_bd, k_all, v_all, *shards)
    return res[:8], res[8:]


_KV_ROWS_OF_W_IN_T = pl.BlockSpec((2 * KVW, D), lambda b: (DA // (2 * KVW), 0))


def _ctx_fwd(ctx, modc, norm_w, w_kv_b, knw_t, ones_bd, n_keys):
    bl, cl, _ = ctx.shape

    def body(x_ref, mod_ref, nw_ref, w_ref, knw_ref, bd_ref, k_ref, v_ref, pkv_ref):
        xv = x_ref[0]
        shift = mod_ref[0, 0:1, :]
        scale = mod_ref[0, 1:2, :]
        r = lax.rsqrt(jnp.mean(xv * xv, axis=-1, keepdims=True) + EPS)
        u = (xv * r * nw_ref[...]) * (1.0 + scale) + shift
        p = _dot_nt(u.astype(BF16), w_ref[...])
        pk = p[:, 0:KVW]
        rk = lax.rsqrt(_segsum(pk * pk, bd_ref[...]) * (1.0 / HD) + EPS)
        kn = pk * rk * knw_ref[...]
        pv = p[:, KVW:2 * KVW]
        for h in range(KVW // HD):
            k_ref[0, h] = kn[:, h * HD:(h + 1) * HD].astype(BF16)
            v_ref[0, h] = _with_ones_column(pv[:, h * HD:(h + 1) * HD]).astype(BF16)
        pkv_ref[0] = p

    def const(shape):
        return pl.BlockSpec(shape, lambda b: (0,) * len(shape))

    def tile(w):
        return pl.BlockSpec((1, cl, w), lambda b: (b, 0, 0))

    ctx_block = (n_keys - cl) // cl
    assert ctx_block * cl + cl == n_keys
    k_tile = pl.BlockSpec((1, KVW // HD, cl, HD), lambda b: (b, 0, ctx_block, 0))
    v_tile = pl.BlockSpec((1, KVW // HD, cl, 2 * HD), lambda b: (b, 0, ctx_block, 0))
    return pl.pallas_call(
        body, name="ctx_fwd", grid=(bl,),
        in_specs=[tile(D), const((1, 3, D)), const((1, D)), _KV_ROWS_OF_W_IN_T, const((1, KVW)), const((KVW, KVW))],
        out_specs=[k_tile, v_tile, tile(2 * KVW)],
        out_shape=[jax.ShapeDtypeStruct((bl, KVW // HD, n_keys, HD), BF16),
                   jax.ShapeDtypeStruct((bl, KVW // HD, n_keys, 2 * HD), BF16),
                   jax.ShapeDtypeStruct((bl, cl, 2 * KVW), F32)],
        compiler_params=_params(("arbitrary",)),
    )(ctx, modc, norm_w, w_kv_b, knw_t, ones_bd)


def _attn_fwd(q, k, v1):
    bl, _, s, _ = q.shape
    n_keys = k.shape[2]

    def body(q_ref, k_ref, v_ref, o_ref, lse_ref):
        kv = k_ref[0, 0]
        vv = v_ref[0, 0]
        lane = lax.broadcasted_iota(jnp.int32, (TQ, 2 * HD), 1)
        for part in range(FWD_PARTS):
            rows = pl.ds(part * TQ, TQ)
            lse = jnp.zeros((TQ, 2 * HD), F32)
            heads = []
            sc_all = _dot_nt(q_ref[0, :, rows, :].reshape(4 * TQ, HD), kv)
            for h in range(4):
                sc = sc_all[h * TQ:(h + 1) * TQ, :]
                m = jnp.max(sc, axis=-1, keepdims=True)
                e = jnp.exp(sc - m).astype(BF16)
                ov = _dot(e, vv)
                denom = ov[:, HD:HD + 1]
                heads.append(ov[:, 0:HD] * (1.0 / denom))
                lse = jnp.where(lane == h, m + jnp.log(denom), lse)
            o_ref[0, rows, :] = jnp.concatenate(heads, axis=-1)
            lse_ref[0, 0, rows, :] = lse

    tq = FWD_PARTS * TQ
    ks = pl.BlockSpec((1, 1, n_keys, HD), lambda b, g, i: (b, g, 0, 0))
    qs = pl.BlockSpec((1, 4, tq, HD), lambda b, g, i: (b, g, i, 0))
    vs = pl.BlockSpec((1, 1, n_keys, 2 * HD), lambda b, g, i: (b, g, 0, 0))
    return pl.pallas_call(
        body, name="attn_fwd", grid=(bl, 2, s // tq), in_specs=[qs, ks, vs],
        out_specs=[pl.BlockSpec((1, tq, 4 * HD), lambda b, g, i: (b, i, g)),
                   pl.BlockSpec((1, 1, tq, 2 * HD), lambda b, g, i: (b, g, i, 0))],
        out_shape=[jax.ShapeDtypeStruct((bl, s, DA), F32), jax.ShapeDtypeStruct((bl, 2, s, 2 * HD), F32)],
        compiler_params=_params(("arbitrary", "arbitrary", "arbitrary")),
    )(q, k, v1)


def _attn_bwd(q, k, v1, do, o, lse, exchange):
    bl, _, s, _ = q.shape
    n_keys = k.shape[2]
    tq = BWD_PARTS * TQ
    nq = s // tq
    n_ex = len(exchange)

    def body(*refs):
        q_ref, k_ref, v_ref, do_ref, o_ref, lse_ref = refs[:6]
        part_refs = refs[6:6 + n_ex]
        dq_ref, dk_ref, dv_ref = refs[6 + n_ex:9 + n_ex]
        got_refs = refs[9 + n_ex:9 + 2 * n_ex]
        p_sc, ds_sc, dkt, dvt, send_sems, recv_sems, local_sems = refs[9 + 2 * n_ex:]
        i = pl.program_id(2)
        first = (pl.program_id(0) == 0) & (pl.program_id(1) == 0) & (i == 0)
        last = (pl.program_id(0) == bl - 1) & (pl.program_id(1) == 1) & (i == nq - 1)
        local, remote = _exchange_copies(part_refs, got_refs, send_sems, recv_sems, local_sems, scatter=True)

        @pl.when(first)
        def _():
            for cp in local + remote:
                cp.start()

        @pl.when(i == 0)
        def _():
            dkt[...] = jnp.zeros_like(dkt)
            dvt[...] = jnp.zeros_like(dvt)

        kv = k_ref[0, 0]
        vv = v_ref[0, 0][:, 0:HD]
        for part in range(BWD_PARTS):
            tq_rows = pl.ds(part * TQ, TQ)
            lse = lse_ref[0, 0, tq_rows, :]
            ov = o_ref[0, tq_rows, :]
            dqs = []
            q_cat = q_ref[0, :, tq_rows, :].reshape(4 * TQ, HD)
            do_cat = do_ref[0, :, tq_rows, :].reshape(4 * TQ, HD)
            sc_all = _dot_nt(q_cat, kv)
            for h in range(4):
                doh = do_cat[h * TQ:(h + 1) * TQ, :]
                delta = jnp.sum(ov[:, h * HD:(h + 1) * HD] * doh.astype(F32), axis=-1, keepdims=True)
                rows = pl.ds((part * 4 + h) * TQ, TQ)
                p = jnp.exp(sc_all[h * TQ:(h + 1) * TQ, :] - lse[:, h:h + 1])
                ds = (p * (_dot_nt(doh, vv) - delta)).astype(BF16)
                p_sc[rows, :] = p.astype(BF16)
                ds_sc[rows, :] = ds
                dqs.append(_dot(ds, kv) * 0.125)
            dq_ref[0, tq_rows, :] = jnp.concatenate(dqs, axis=-1)
            part_rows = pl.ds(part * 4 * TQ, 4 * TQ)
            dvt[...] += _dot_tn(do_cat, p_sc[part_rows, :])
            dkt[...] += _dot_tn(q_cat, ds_sc[part_rows, :])

        @pl.when(i == nq - 1)
        def _():
            dk_ref[0, 0] = dkt[...].T
            dv_ref[0, 0] = dvt[...].T

        @pl.when(last)
        def _():
            for cp in remote:
                cp.wait_recv()
            for cp in remote:
                cp.wait_send()
            for cp in local:
                cp.wait()

    qs = pl.BlockSpec((1, 4, tq, HD), lambda b, g, i: (b, g, i, 0))
    ks = pl.BlockSpec((1, 1, n_keys, HD), lambda b, g, i: (b, g, 0, 0))
    vs = pl.BlockSpec((1, 1, n_keys, 2 * HD), lambda b, g, i: (b, g, 0, 0))
    os_ = pl.BlockSpec((1, tq, 4 * HD), lambda b, g, i: (b, i, g))
    kshape = jax.ShapeDtypeStruct(k.shape, F32)
    anyspace = pl.BlockSpec(memory_space=pl.ANY)
    res = pl.pallas_call(
        body, name="attn_bwd", grid=(bl, 2, nq),
        in_specs=[qs, ks, vs, qs, os_, pl.BlockSpec((1, 1, tq, 2 * HD), lambda b, g, i: (b, g, i, 0))]
        + [anyspace] * n_ex,
        out_specs=[os_, ks, ks] + [anyspace] * n_ex,
        out_shape=[jax.ShapeDtypeStruct((bl, s, DA), F32), kshape, kshape]
        + [jax.ShapeDtypeStruct(a.shape, a.dtype) for a in exchange],
        scratch_shapes=[pltpu.VMEM((4 * tq, n_keys), BF16), pltpu.VMEM((4 * tq, n_keys), BF16),
                        pltpu.VMEM((HD, n_keys), F32), pltpu.VMEM((HD, n_keys), F32)] + _exchange_scratch(n_ex),
        compiler_params=_params(("arbitrary", "arbitrary", "arbitrary")),
    )(q, k, v1, do, o, lse, *exchange)
    return res[:3], res[3:]


def _halo_specs(width, s):
    per = TC // HALO
    last = s // HALO - 1
    main = pl.BlockSpec((1, TC, width), lambda b, i: (b, i, 0))
    prev = pl.BlockSpec((1, HALO, width), lambda b, i: (b, jnp.maximum(i * per - 1, 0), 0))
    nxt = pl.BlockSpec((1, HALO, width), lambda b, i: (b, jnp.minimum((i + 1) * per, last), 0))
    return main, prev, nxt


def _glu(g):
    return g[:, 0:DC] * _sigmoid(g[:, DC:2 * DC])


def _fill_padded(pad_ref, main, prev, nxt, first, last):
    tc = main.shape[0]
    pad_ref[0:HALO, :] = jnp.where(first, 0.0, prev)
    pad_ref[HALO:HALO + tc, :] = main
    pad_ref[HALO + tc:2 * HALO + tc, :] = jnp.where(last, 0.0, nxt)


PLANE_ROWS = TC + 2 * HALO - 8


def _shift_planes(pad_ref, planes_ref):
    for r in range(1, 8):
        planes_ref[r - 1] = pad_ref[pl.ds(r, planes_ref.shape[1]), :]


def _tap_rows(pad_ref, planes_ref, offset, start, n):
    a, r = divmod(offset, 8)
    if r == 0:
        return pad_ref[pl.ds(start + 8 * a, n), :]
    return planes_ref[r - 1, pl.ds(start + 8 * a, n), :]


def _conv_fwd(glu, conv_w, conv_b, ln_w, ln_b, w_pw_b, b_pw):
    bl, s, _ = glu.shape
    nt = s // TC

    def body(g_ref, gp_ref, gn_ref, cw_ref, cb_ref, lw_ref, lb_ref, wpw_ref, bpw_ref, y_ref, cp_ref, pad_ref, planes_ref):
        i = pl.program_id(1)
        _fill_padded(pad_ref, _glu(g_ref[0]), _glu(gp_ref[0]), _glu(gn_ref[0]), i == 0, i == nt - 1)
        _shift_planes(pad_ref, planes_ref)
        for ck in range(TC // CH):
            acc = jnp.zeros((CH, DC), F32) + cb_ref[...]
            for t in range(KW):
                acc = acc + _tap_rows(pad_ref, planes_ref, 1 + t, ck * CH, CH) * cw_ref[t:t + 1, :]
            y_ref[0, pl.ds(ck * CH, CH), :] = acc
        y = y_ref[0]
        mu = jnp.mean(y, axis=-1, keepdims=True)
        yc = y - mu
        var = jnp.mean(yc * yc, axis=-1, keepdims=True)
        z = yc * lax.rsqrt(var + EPS) * lw_ref[...] + lb_ref[...]
        act = z * _sigmoid(z)
        cp_ref[0] = _dot(act.astype(BF16), wpw_ref[...]) + bpw_ref[...]

    def const(shape):
        return pl.BlockSpec(shape, lambda b, i: (0,) * len(shape))

    main, prev, nxt = _halo_specs(2 * DC, s)
    tile = pl.BlockSpec((1, TC, DC), lambda b, i: (b, i, 0))
    return pl.pallas_call(
        body, name="conv_fwd", grid=(bl, nt),
        in_specs=[main, prev, nxt, const((32, DC)), const((1, DC)), const((1, DC)), const((1, DC)),
                  const((DC, DC)), const((1, DC))],
        out_specs=[tile, tile],
        out_shape=[jax.ShapeDtypeStruct((bl, s, DC), F32)] * 2,
        scratch_shapes=[pltpu.VMEM((TC + 2 * HALO, DC), F32), pltpu.VMEM((7, PLANE_ROWS, DC), F32)],
        compiler_params=_params(("arbitrary", "arbitrary")),
    )(glu, glu, glu, conv_w, conv_b, ln_w, ln_b, w_pw_b, b_pw)


def _conv_bwd_depthwise(glu, dy, conv_w):
    bl, s, _ = glu.shape
    nt = s // TC

    def body(g_ref, gp_ref, gn_ref, d_ref, dp_ref, dn_ref, cw_ref, dglu_ref, shares_ref,
             padu_ref, padd_ref, planes_u, planes_d, dcw_ref):
        i = pl.program_id(1)

        @pl.when((pl.program_id(0) == 0) & (i == 0))
        def _():
            dcw_ref[...] = jnp.zeros_like(dcw_ref)

        first, last = i == 0, i == nt - 1
        _fill_padded(padu_ref, _glu(g_ref[0]), _glu(gp_ref[0]), _glu(gn_ref[0]), first, last)
        _fill_padded(padd_ref, d_ref[0], dp_ref[0], dn_ref[0], first, last)
        _shift_planes(padu_ref, planes_u)
        _shift_planes(padd_ref, planes_d)
        for ck in range(TC // CH):
            acc = jnp.zeros((CH, DC), F32)
            for t in range(KW):
                acc = acc + _tap_rows(padd_ref, planes_d, 2 * HALO - 1 - t, ck * CH, CH) * cw_ref[t:t + 1, :]
            g = g_ref[0, pl.ds(ck * CH, CH), :]
            a = g[:, 0:DC]
            sg = _sigmoid(g[:, DC:2 * DC])
            dglu_ref[0, pl.ds(ck * CH, CH), 0:DC] = (acc * sg).astype(BF16)
            dglu_ref[0, pl.ds(ck * CH, CH), DC:2 * DC] = (acc * a * sg * (1.0 - sg)).astype(BF16)
        group = 4
        for t0 in range(0, KW, group):
            taps = range(t0, min(t0 + group, KW))
            acc8 = [jnp.zeros((8, DC), F32) for _ in taps]
            for ck in range(TC // CH):
                dchunk = d_ref[0, pl.ds(ck * CH, CH), :]
                for n, t in enumerate(taps):
                    prod = _tap_rows(padu_ref, planes_u, 1 + t, ck * CH, CH) * dchunk
                    acc8[n] = acc8[n] + jnp.sum(prod.reshape(CH // 8, 8, DC), axis=0)
            for n, t in enumerate(taps):
                dcw_ref[t:t + 1, :] += jnp.sum(acc8[n], axis=0, keepdims=True)

        @pl.when((pl.program_id(0) == bl - 1) & last)
        def _():
            for j in range(N_DEV):
                shares_ref[j] = dcw_ref[:, j * cpd:(j + 1) * cpd].astype(BF16)

    gmain, gprev, gnext = _halo_specs(2 * DC, s)
    dmain, dprev, dnext = _halo_specs(DC, s)
    cw = pl.BlockSpec((32, DC), lambda b, i: (0, 0))
    cpd = DC // N_DEV
    return pl.pallas_call(
        body, name="conv_bwd_depthwise", grid=(bl, nt),
        in_specs=[gmain, gprev, gnext, dmain, dprev, dnext, cw],
        out_specs=[gmain, pl.BlockSpec((N_DEV, 32, cpd), lambda b, i: (0, 0, 0))],
        out_shape=[jax.ShapeDtypeStruct((bl, s, 2 * DC), BF16), jax.ShapeDtypeStruct((N_DEV, 32, cpd), BF16)],
        scratch_shapes=[pltpu.VMEM((TC + 2 * HALO, DC), F32)] * 2 + [pltpu.VMEM((7, PLANE_ROWS, DC), F32)] * 2
        + [pltpu.VMEM((32, DC), F32)],
        compiler_params=_params(("arbitrary", "arbitrary")),
    )(glu, glu, glu, dy, dy, dy, conv_w)


def _out_fwd_bwd(attn, za, cp, zc, x, target, modrows, w_out_b, y_conv, ln_w, ln_b, w_pw_b):
    bl, s, _ = x.shape
    tm = OUT_TM

    def body(o_ref, za_ref, cp_ref, zc_ref, x_ref, t_ref, mod_ref, w_ref, y_ref, lw_ref, lb_ref, wpw_ref,
             do_ref, dza_ref, dy_ref, dzc_ref, dh_ref, dgate_ref, gwb_ref, loss_ref, gpwb_ref, rows_ref,
             gw_ref, gpw_ref):
        b, i = pl.program_id(0), pl.program_id(1)

        @pl.when((b == 0) & (i == 0))
        def _():
            gw_ref[...] = jnp.zeros_like(gw_ref)
            gpw_ref[...] = jnp.zeros_like(gpw_ref)
            rows_ref[...] = jnp.zeros_like(rows_ref)
            loss_ref[...] = jnp.zeros_like(loss_ref)

        @pl.when(i == 0)
        def _():
            dgate_ref[...] = jnp.zeros_like(dgate_ref)

        gate = mod_ref[0, 2:3, :]
        w = w_ref[...]
        o, za_v, cp_v, zc_v = o_ref[0], za_ref[0], cp_ref[0], zc_ref[0]
        sa = _sigmoid(za_v)
        sc = _sigmoid(zc_v)
        silu_a = za_v * sa
        silu_c = zc_v * sc
        mix = jnp.concatenate([(o * silu_a).astype(BF16), (cp_v * silu_c).astype(BF16)], axis=-1)
        out = _dot(mix, w)
        err = x_ref[0] + gate * out - t_ref[0]
        loss_ref[...] += jnp.sum(err * err, axis=0, keepdims=True)
        dh = err * (1.0 / D)
        dh_ref[0] = dh
        dgate_ref[0] += jnp.sum(dh * out, axis=0, keepdims=True)
        dout = (dh * gate).astype(BF16)
        gw_ref[...] += _dot_tn(mix, dout)
        dmix = _dot_nt(dout, w)
        dga = dmix[:, 0:DA]
        dgc = dmix[:, DA:DA + DC]
        dov = dga * silu_a
        for h in range(DA // HD):
            do_ref[0, h] = dov[:, h * HD:(h + 1) * HD].astype(BF16)
        dza_ref[0] = (dga * o * (sa * (1.0 + za_v * (1.0 - sa)))).astype(BF16)
        dzc_ref[0] = (dgc * cp_v * (sc * (1.0 + zc_v * (1.0 - sc)))).astype(BF16)
        dcp = dgc * silu_c
        y = y_ref[0]
        yc = y - jnp.mean(y, axis=-1, keepdims=True)
        rstd = lax.rsqrt(jnp.mean(yc * yc, axis=-1, keepdims=True) + EPS)
        yn = yc * rstd
        lw = lw_ref[...]
        z = yn * lw + lb_ref[...]
        sg = _sigmoid(z)
        dcp_b = dcp.astype(BF16)
        gpw_ref[...] += _dot_tn((z * sg).astype(BF16), dcp_b)
        dz = _dot_nt(dcp_b, wpw_ref[...]) * (sg * (1.0 + z * (1.0 - sg)))
        dyn = dz * lw
        dy = rstd * (dyn - jnp.mean(dyn, axis=-1, keepdims=True) - yn * jnp.mean(dyn * yn, axis=-1, keepdims=True))
        dy_ref[0] = dy
        rows_ref[0:1, :] += jnp.sum(dcp, axis=0, keepdims=True)
        rows_ref[1:2, :] += jnp.sum(dz * yn, axis=0, keepdims=True)
        rows_ref[2:3, :] += jnp.sum(dz, axis=0, keepdims=True)
        rows_ref[3:4, :] += jnp.sum(dy, axis=0, keepdims=True)

        @pl.when((b == bl - 1) & (i == s // tm - 1))
        def _():
            gwb_ref[...] = gw_ref[...].astype(BF16)
            gpwb_ref[...] = gpw_ref[...].astype(BF16)

    def const(shape):
        return pl.BlockSpec(shape, lambda b, i: (0,) * len(shape))

    def tile(w):
        return pl.BlockSpec((1, tm, w), lambda b, i: (b, i, 0))

    return pl.pallas_call(
        body, name="out_fwd_bwd", grid=(bl, s // tm),
        in_specs=[tile(DA), tile(DA), tile(DC), tile(DC), tile(D), tile(D),
                  pl.BlockSpec((1, 3, D), lambda b, i: (b, 0, 0)), const((D, D)),
                  tile(DC), const((1, DC)), const((1, DC)), const((DC, DC))],
        out_specs=[pl.BlockSpec((1, DA // HD, tm, HD), lambda b, i: (b, 0, i, 0)), tile(DA), tile(DC), tile(DC), tile(D),
                   pl.BlockSpec((1, 1, D), lambda b, i: (b, 0, 0)), const((D, D)), const((1, D)),
                   const((DC, DC)), const((8, DC))],
        out_shape=[jax.ShapeDtypeStruct((bl, DA // HD, s, HD), BF16), jax.ShapeDtypeStruct((bl, s, DA), BF16),
                   jax.ShapeDtypeStruct((bl, s, DC), F32), jax.ShapeDtypeStruct((bl, s, DC), BF16),
                   jax.ShapeDtypeStruct((bl, s, D), F32), jax.ShapeDtypeStruct((bl, 1, D), F32),
                   jax.ShapeDtypeStruct((D, D), BF16), jax.ShapeDtypeStruct((1, D), F32),
                   jax.ShapeDtypeStruct((DC, DC), BF16), jax.ShapeDtypeStruct((8, DC), F32)],
        scratch_shapes=[pltpu.VMEM((D, D), F32), pltpu.VMEM((DC, DC), F32)],
        compiler_params=_params(("arbitrary", "arbitrary")),
    )(attn, za, cp, zc, x, target, modrows, w_out_b, y_conv, ln_w, ln_b, w_pw_b)


def _rms_heads_bwd(dy, x, w_t, ones_bd):
    r = lax.rsqrt(_segsum(x * x, ones_bd) * (1.0 / HD) + EPS)
    xh = x * r
    g = dy * w_t
    dx = r * (g - xh * (_segsum(g * xh, ones_bd) * (1.0 / HD)))
    return dx, dy * xh


def _ctx_bwd(ctx, modc, norm_w, w_kv_b, pkv_c, dk_c, dv_c, knw_t, ones_bd):
    bl, cl, _ = ctx.shape

    def body(x_ref, mod_ref, nw_ref, w_ref, p_ref, dk_ref, dv_ref, knw_ref, bd_ref, gw_ref, rows_ref, dknw_ref):
        @pl.when(pl.program_id(0) == 0)
        def _():
            gw_ref[...] = jnp.zeros_like(gw_ref)
            rows_ref[...] = jnp.zeros_like(rows_ref)
            dknw_ref[...] = jnp.zeros_like(dknw_ref)

        xv = x_ref[0]
        shift = mod_ref[0, 0:1, :]
        scale = mod_ref[0, 1:2, :]
        nw = nw_ref[...]
        r = lax.rsqrt(jnp.mean(xv * xv, axis=-1, keepdims=True) + EPS)
        xn = xv * r
        yv = xn * nw
        u = yv * (1.0 + scale) + shift
        dkv = jnp.concatenate([dk_ref[0, 0], dk_ref[0, 1]], axis=-1)
        dpk, dknw = _rms_heads_bwd(dkv, p_ref[0][:, 0:KVW], knw_ref[...], bd_ref[...])
        dp = jnp.concatenate([dpk.astype(BF16), dv_ref[0, 0].astype(BF16), dv_ref[0, 1].astype(BF16)], axis=-1)
        gw_ref[...] += _dot_tn(dp, u.astype(BF16))
        du = _dot(dp, w_ref[...])
        rows_ref[0:1, :] += jnp.sum(du, axis=0, keepdims=True)
        rows_ref[1:2, :] += jnp.sum(du * yv, axis=0, keepdims=True)
        rows_ref[2:3, :] += jnp.sum(du * (1.0 + scale) * xn, axis=0, keepdims=True)
        dknw_ref[...] += jnp.sum(dknw, axis=0, keepdims=True)

    def const(shape):
        return pl.BlockSpec(shape, lambda b: (0,) * len(shape))

    def tile(w):
        return pl.BlockSpec((1, cl, w), lambda b: (b, 0, 0))

    ctx_block = (dk_c.shape[2] - cl) // cl
    kv_tile = pl.BlockSpec((1, KVW // HD, cl, HD), lambda b: (b, 0, ctx_block, 0))
    return pl.pallas_call(
        body, name="ctx_bwd", grid=(bl,),
        in_specs=[tile(D), const((1, 3, D)), const((1, D)), _KV_ROWS_OF_W_IN_T, tile(2 * KVW), kv_tile, kv_tile,
                  const((1, KVW)), const((KVW, KVW))],
        out_specs=[const((2 * KVW, D)), const((8, D)), const((1, KVW))],
        out_shape=[jax.ShapeDtypeStruct((2 * KVW, D), F32), jax.ShapeDtypeStruct((8, D), F32),
                   jax.ShapeDtypeStruct((1, KVW), F32)],
        compiler_params=_params(("arbitrary",)),
    )(ctx, modc, norm_w, w_kv_b, pkv_c, dk_c, dv_c, knw_t, ones_bd)


def _bwd_in(x, modrows, norm_w, w_in_b, cos, sins, qnw_t, knw_t, ones_bd,
            pq, pkv, dq, dk, dv, dza, dglu, dzc, dh, gw_kv):
    bl, s, _ = x.shape
    tm = TOKEN_PARTS * TM
    nt = s // tm

    def body(x_ref, mod_ref, nw_ref, win_hbm, cos_ref, sin_ref, qnw_ref, knw_ref, bd_ref,
             pq_ref, pkv_ref, dq_ref, dk_ref, dv_ref, dza_ref, dglu_ref, dzc_ref, dh_ref, gwkv_ref,
             gx_ref, gw_hbm, dmod_ref, dnw_ref, dqnw_ref, dknw_ref, win_ref, gw_acc, sem):
        b, i = pl.program_id(0), pl.program_id(1)

        @pl.when((b == 0) & (i == 0))
        def _():
            cp = pltpu.make_async_copy(win_hbm, win_ref, sem)
            cp.start()
            gw_acc[...] = jnp.zeros_like(gw_acc)
            dnw_ref[...] = jnp.zeros_like(dnw_ref)
            dqnw_ref[...] = jnp.zeros_like(dqnw_ref)
            dknw_ref[...] = jnp.zeros_like(dknw_ref)
            cp.wait()

        @pl.when(i == 0)
        def _():
            dmod_ref[...] = jnp.zeros_like(dmod_ref)

        bd = bd_ref[...]
        shift = mod_ref[0, 0:1, :]
        scale = mod_ref[0, 1:2, :]
        nw = nw_ref[...]
        dps, us = [], []
        for part in range(TOKEN_PARTS):
            rows = pl.ds(part * TM, TM)
            ck = cos_ref[rows, :]
            sk = sin_ref[rows, :]
            cs = jnp.concatenate([ck] * (DA // KVW), axis=-1)
            sn = jnp.concatenate([sk] * (DA // KVW), axis=-1)
            dqn = _rope_bwd(dq_ref[0, rows, :], cs, sn)
            dpq, dqnw = _rms_heads_bwd(dqn, pq_ref[0, rows, :], qnw_ref[...], bd)
            dkn = _rope_bwd(jnp.concatenate([dk_ref[0, 0, rows, :], dk_ref[0, 1, rows, :]], axis=-1), ck, sk)
            dpk, dknw = _rms_heads_bwd(dkn, pkv_ref[0, rows, 0:KVW], knw_ref[...], bd[0:KVW, 0:KVW])
            dqnw_ref[...] += jnp.sum(dqnw, axis=0, keepdims=True)
            dknw_ref[...] += jnp.sum(dknw, axis=0, keepdims=True)
            dp = jnp.concatenate(
                [dpq.astype(BF16), dpk.astype(BF16), dv_ref[0, 0, rows, :].astype(BF16), dv_ref[0, 1, rows, :].astype(BF16),
                 dza_ref[0, rows, :], dglu_ref[0, rows, :], dzc_ref[0, rows, :]], axis=-1)

            xv = x_ref[0, rows, :]
            r = lax.rsqrt(jnp.mean(xv * xv, axis=-1, keepdims=True) + EPS)
            xn = xv * r
            yv = xn * nw
            u = yv * (1.0 + scale) + shift
            dps.append(dp)
            us.append(u.astype(BF16))
            du = _dot(dp, win_ref[...])
            dmod_ref[0, 0:1, :] += jnp.sum(du, axis=0, keepdims=True)
            dmod_ref[0, 1:2, :] += jnp.sum(du * yv, axis=0, keepdims=True)
            dy = du * (1.0 + scale)
            dnw_ref[...] += jnp.sum(dy * xn, axis=0, keepdims=True)
            dxn = dy * nw
            gx_ref[0, rows, :] = dh_ref[0, rows, :] + r * (dxn - xn * jnp.mean(dxn * xn, axis=-1, keepdims=True))
        gw_acc[...] += _dot_tn(jnp.concatenate(dps, axis=0), jnp.concatenate(us, axis=0))

        @pl.when((b == bl - 1) & (i == nt - 1))
        def _():
            gw_acc[DA:DA + 2 * KVW, :] += gwkv_ref[...]

            def to_bf16(j, carry):
                rows = pl.ds(pl.multiple_of(j * 2 * KVW, 2 * KVW), 2 * KVW)
                win_ref[rows, :] = gw_acc[rows, :].astype(BF16)
                return carry

            lax.fori_loop(0, D_IN // (2 * KVW), to_bf16, 0)
            pltpu.sync_copy(win_ref, gw_hbm)

    def tile(w):
        return pl.BlockSpec((1, tm, w), lambda b, i: (b, i, 0))

    def const(shape):
        return pl.BlockSpec(shape, lambda b, i: (0,) * len(shape))

    anyspace = pl.BlockSpec(memory_space=pl.ANY)
    rope = pl.BlockSpec((tm, KVW), lambda b, i: (i, 0))
    kv_tile = pl.BlockSpec((1, KVW // HD, tm, HD), lambda b, i: (b, 0, i, 0))
    return pl.pallas_call(
        body, name="bwd_in", grid=(bl, nt),
        in_specs=[tile(D), pl.BlockSpec((1, 3, D), lambda b, i: (b, 0, 0)), const((1, D)), anyspace, rope, rope,
                  const((1, DA)), const((1, KVW)), const((DA, DA)),
                  tile(DA), tile(2 * KVW), tile(DA), kv_tile, kv_tile, tile(DA), tile(2 * DC), tile(DC), tile(D),
                  const((2 * KVW, D))],
        out_specs=[tile(D), anyspace, pl.BlockSpec((1, 2, D), lambda b, i: (b, 0, 0)), const((1, D)),
                   const((1, DA)), const((1, KVW))],
        out_shape=[jax.ShapeDtypeStruct((bl, s, D), F32), jax.ShapeDtypeStruct((D_IN, D), BF16),
                   jax.ShapeDtypeStruct((bl, 2, D), F32), jax.ShapeDtypeStruct((1, D), F32),
                   jax.ShapeDtypeStruct((1, DA), F32), jax.ShapeDtypeStruct((1, KVW), F32)],
        scratch_shapes=[pltpu.VMEM((D_IN, D), BF16), pltpu.VMEM((D_IN, D), F32), pltpu.SemaphoreType.DMA],
        compiler_params=_params(("arbitrary", "arbitrary")),
    )(x, modrows, norm_w, w_in_b, cos, sins, qnw_t, knw_t, ones_bd,
      pq, pkv, dq, dk, dv, dza, dglu, dzc, dh, gw_kv)


_LOSS, _DMODC, _NW, _QN, _KN, _CB, _LW, _LB, _BPW, SMALL_W = 0, 1024, 4096, 5120, 5248, 5376, 5888, 6400, 6912, 7424


ROW_W = 1792


def _put_flat(ref, off, value):
    n, done = value.shape[1], 0
    while done < n:
        r, c = divmod(off + done, ROW_W)
        take = min(n - done, ROW_W - c)
        ref[r:r + 1, c:c + take] = value[:, done:done + take]
        done += take


def _get_flat(arr, off, n):
    parts, done = [], 0
    while done < n:
        r, c = divmod(off + done, ROW_W)
        take = min(n - done, ROW_W - c)
        parts.append(arr[r:r + 1, c:c + take])
        done += take
    return parts[0] if len(parts) == 1 else jnp.concatenate(parts, axis=-1)


def _pack_small_body(loss_ref, ctx_ref, dnw_ref, dqnw_ref, dknw_ref, dknwc_ref, conv_ref, dss_ref, dgate_ref, o_ref):
    bl = dss_ref.shape[0]
    assert SMALL_W + bl * 3 * D <= 8 * ROW_W
    o_ref[...] = jnp.zeros_like(o_ref)
    _put_flat(o_ref, _LOSS, loss_ref[...])
    _put_flat(o_ref, _DMODC, ctx_ref[0:1, :])
    _put_flat(o_ref, _DMODC + D, ctx_ref[1:2, :])
    _put_flat(o_ref, _NW, dnw_ref[...] + ctx_ref[2:3, :])
    dq = dqnw_ref[...]
    qn = dq[:, 0:HD]
    for h in range(1, DA // HD):
        qn = qn + dq[:, h * HD:(h + 1) * HD]
    _put_flat(o_ref, _QN, qn)
    dk = dknw_ref[...] + dknwc_ref[...]
    _put_flat(o_ref, _KN, dk[:, 0:HD] + dk[:, HD:2 * HD])
    _put_flat(o_ref, _BPW, conv_ref[0:1, :])
    _put_flat(o_ref, _LW, conv_ref[1:2, :])
    _put_flat(o_ref, _LB, conv_ref[2:3, :])
    _put_flat(o_ref, _CB, conv_ref[3:4, :])
    for b in range(bl):
        _put_flat(o_ref, SMALL_W + b * 3 * D, dss_ref[b, 0:1, :])
        _put_flat(o_ref, SMALL_W + b * 3 * D + D, dss_ref[b, 1:2, :])
        _put_flat(o_ref, SMALL_W + b * 3 * D + 2 * D, dgate_ref[b])


_SMALL = (("b_mod", None), ("norm_w", _NW), ("q_norm_w", _QN), ("k_norm_w", _KN), ("conv_b", _CB),
          ("conv_ln_w", _LW), ("conv_ln_b", _LB), ("b_pw", _BPW), ("c_ctx", None))


def _epilogue(parts_in, pieces, c_rows, w_mod_loc):
    bl = pieces[7].shape[0]
    n_ex = N_DEV * bl
    n_mod = w_mod_loc.shape[1]
    rb = 32
    shp = parts_in.shape[1:]
    rows_in = shp[0]

    def body(*refs):
        it = iter(refs)
        take = lambda k: [next(it) for _ in range(k)]
        (parts,) = take(1)
        piece_refs = take(9)
        (c_ref, wm_ref) = take(2)
        (g_in, g_wm, sum_ref, gb_ref, gc_all, loss_ref) = take(6)
        (mine, got_sib, stage, got_chip, payload, gathered, dmod_full, gc_mine) = take(8)
        (d2d_send, d2d_recv, ici_send, ici_recv, local_sems, sg_send, sg_recv, gc_send, gc_recv, misc_sems) = take(10)

        x, y, c = _coords()
        me = _lin(x, y, c)
        sib = (x, y, 1 - c)
        home = 2 * x + y

        def rows_loop(fn):
            def step(i, carry):
                fn(pl.ds(pl.multiple_of(i * rb, rb), rb))
                return carry
            lax.fori_loop(0, rows_in // rb, step, 0)

        def direct_gather(src, dst, send_sems, recv_sems, local_sem):
            cps = [pltpu.make_async_copy(src, dst.at[me], local_sem)]
            for k in range(1, N_DEV):
                peer = (1 - x if k & 4 else x, 1 - y if k & 2 else y, 1 - c if k & 1 else c)
                cps.append(pltpu.make_async_remote_copy(
                    src_ref=src, dst_ref=dst.at[me], send_sem=send_sems.at[k - 1], recv_sem=recv_sems.at[k - 1],
                    device_id=peer, device_id_type=MESH_ID))
            for cp in cps:
                cp.start()
            return cps

        _pack_small_body(*piece_refs, payload)
        small_cps = direct_gather(payload, gathered, sg_send, sg_recv, misc_sems.at[0])

        local, d2d, ici = [], [], []
        for s in range(4):
            cp = pltpu.make_async_copy(parts.at[_lin(s // 2, s % 2, c)], mine.at[s], local_sems.at[s])
            cp.start()
            local.append(cp)
            rc = pltpu.make_async_remote_copy(
                src_ref=parts.at[_lin(s // 2, s % 2, 1 - c)], dst_ref=got_sib.at[s],
                send_sem=d2d_send.at[s], recv_sem=d2d_recv.at[s], device_id=sib, device_id_type=MESH_ID)
            rc.start()
            d2d.append(rc)

        for cp in small_cps[1:]:
            cp.wait_recv()
        small_cps[0].wait()
        tot = gathered[0]
        for j in range(1, N_DEV):
            tot = tot + gathered[j]
        summed = _get_flat(tot, 0, SMALL_W)
        dmod_full[...] = jnp.zeros_like(dmod_full)
        for j in range(N_DEV):
            arr = gathered[j]
            for b in range(bl):
                dmod_full[j * bl + b:j * bl + b + 1, :] = _get_flat(arr, SMALL_W + b * 3 * D, 3 * D)
        dmod_full[n_ex:n_ex + 1, :] = summed[:, _DMODC:_DMODC + 3 * D]
        sum_ref[...] = summed
        gb_ref[...] = jnp.sum(dmod_full[...], axis=0, keepdims=True)
        loss_ref[...] = (0.5 / D) * jnp.sum(summed[:, _LOSS:_LOSS + D], axis=-1, keepdims=True)

        north = c == 1
        first = (jnp.where(north, 1 - x, x), jnp.where(north, y, 1 - y))
        second = (jnp.where(north, x, 1 - x), jnp.where(north, 1 - y, y))
        for s in range(4):
            local[s].wait()
            d2d[s].wait_recv()

        def chip_sum(k, chip, relayed):
            slot = 2 * chip[0] + chip[1]

            def pair_sum(rs):
                acc = mine[slot, rs, :].astype(F32) + got_sib[slot, rs, :].astype(F32)
                if relayed:
                    acc = acc + got_chip[1, rs, :].astype(F32)
                stage[k, rs, :] = acc.astype(BF16)

            rows_loop(pair_sum)

        def send(k, to):
            rc = pltpu.make_async_remote_copy(
                src_ref=stage.at[k], dst_ref=got_chip.at[k], send_sem=ici_send.at[k], recv_sem=ici_recv.at[k],
                device_id=(to[0], to[1], c), device_id_type=MESH_ID)
            rc.start()
            ici.append(rc)

        chip_sum(0, first, False)
        send(0, first)
        chip_sum(1, (1 - x, 1 - y), False)
        send(1, first)

        cr = c_ref[...]
        act = (cr * _sigmoid(cr)).astype(BF16)
        dm = dmod_full[:, pl.ds(pl.multiple_of(me * n_mod, 128), n_mod)].astype(BF16)
        g_wm[...] = _dot_tn(act, dm)
        gc_mine[...] = _dot_nt(dm[n_ex:n_ex + 8, :], wm_ref[...].astype(BF16))
        gc_cps = direct_gather(gc_mine, gc_all, gc_send, gc_recv, misc_sems.at[1])

        ici[1].wait_recv()
        chip_sum(2, second, True)
        send(2, second)
        ici[0].wait_recv()
        ici[2].wait_recv()

        def finish(rs):
            gsum = mine[home, rs, :].astype(F32) + got_sib[home, rs, :].astype(F32)
            g_in[rs, :] = gsum + got_chip[0, rs, :].astype(F32) + got_chip[2, rs, :].astype(F32)

        rows_loop(finish)

        for cp in gc_cps[1:]:
            cp.wait_recv()
        gc_cps[0].wait()
        for rc in d2d + ici + small_cps[1:] + gc_cps[1:]:
            rc.wait_send()

    vm = pl.BlockSpec(memory_space=pltpu.VMEM)
    anyspace = pl.BlockSpec(memory_space=pl.ANY)
    assert rows_in % rb == 0 and parts_in.dtype == BF16
    args = [parts_in, *pieces, c_rows, w_mod_loc]
    in_specs = [anyspace] + [vm] * (len(args) - 1)
    out_shape = [jax.ShapeDtypeStruct(shp, F32), jax.ShapeDtypeStruct(w_mod_loc.shape, F32),
                 jax.ShapeDtypeStruct((1, SMALL_W), F32), jax.ShapeDtypeStruct((1, 3 * D), F32),
                 jax.ShapeDtypeStruct((N_DEV, 8, D), F32), jax.ShapeDtypeStruct((1, 1), F32)]
    scratch = [pltpu.VMEM((4,) + shp, BF16), pltpu.VMEM((4,) + shp, BF16), pltpu.VMEM((3,) + shp, BF16),
               pltpu.VMEM((3,) + shp, BF16), pltpu.VMEM((8, ROW_W), F32), pltpu.VMEM((N_DEV, 8, ROW_W), F32),
               pltpu.VMEM((n_ex + 8, 3 * D), F32), pltpu.VMEM((8, D), F32),
               pltpu.SemaphoreType.DMA((4,)), pltpu.SemaphoreType.DMA((4,)), pltpu.SemaphoreType.DMA((3,)),
               pltpu.SemaphoreType.DMA((3,)), pltpu.SemaphoreType.DMA((4,)),
               pltpu.SemaphoreType.DMA((N_DEV - 1,)), pltpu.SemaphoreType.DMA((N_DEV - 1,)),
               pltpu.SemaphoreType.DMA((N_DEV - 1,)), pltpu.SemaphoreType.DMA((N_DEV - 1,)),
               pltpu.SemaphoreType.DMA((2,))]
    return pl.pallas_call(
        body, name="epilogue", out_shape=out_shape, in_specs=in_specs, out_specs=[vm] * len(out_shape),
        scratch_shapes=scratch, compiler_params=pltpu.CompilerParams(vmem_limit_bytes=VMEM_LIMIT),
    )(*args)


def _final_adamw(tiled, streamed_sums, local_sums, summed, g_bmod, gc_all, small_w, small_m, small_v):
    ns = len(_SMALL)
    n_tiled, n_sums, n_local = len(tiled), len(streamed_sums), len(local_sums)
    streams = list(tiled) + list(streamed_sums)
    n_results = [4] * len(streams)
    tile_jobs = [[(a, t * (w.shape[0] // ADAM_STEPS), w.shape[0] // ADAM_STEPS) for a, (_, w, _, _) in enumerate(tiled)]
                 for t in range(ADAM_STEPS)]
    sum_jobs = [(n_tiled + a, 0, w.shape[0]) for a, (_, w, _, _) in enumerate(streamed_sums)]
    jobs = sum(tile_jobs[:ADAM_STEPS // 2], []) + sum_jobs + sum(tile_jobs[ADAM_STEPS // 2:], [])
    for _, w, _, _ in tiled:
        assert w.shape[0] % (8 * ADAM_STEPS) == 0
    for got, w, _, _ in list(streamed_sums) + list(local_sums):
        assert got.shape[0] == N_DEV and got.shape[1] >= w.shape[0] and got.shape[2] == w.shape[-1]
    n_reads = 4 * len(jobs)
    n_writes = sum(n_results[a] for a, _, _ in jobs)

    def body(*refs):
        it = iter(refs)
        take = lambda k: [next(it) for _ in range(k)]
        src = [take(4) for _ in streams]
        loc = [take(4) for _ in range(n_local)]
        sum_ref, gb_ref, gc_ref = take(3)
        sw, sm, sv = take(ns), take(ns), take(ns)
        dst = [take(k) for k in n_results]
        loc_out = [take(4) for _ in range(n_local)]
        souts = take(4 * ns)
        src_buf = [take(4) for _ in streams]
        dst_buf = [take(k) for k in n_results]
        read_sem, write_sem = take(2)

        def rows_of(ref, r0, nr):
            return ref.at[:, pl.ds(r0, nr), :] if len(ref.shape) == 3 else ref.at[pl.ds(r0, nr), :]

        reads = [[pltpu.make_async_copy(rows_of(src[a][k], r0, nr), rows_of(src_buf[a][k], r0, nr),
                                        read_sem.at[4 * j + k]) for k in range(4)]
                 for j, (a, r0, nr) in enumerate(jobs)]
        for job_reads in reads:
            for cp in job_reads:
                cp.start()

        def summed_shares(got, n_rows):
            g = got[0].astype(F32)
            for d in range(1, N_DEV):
                g = g + got[d].astype(F32)
            return g[0:n_rows, :]

        for (got, w_r, m_r, v_r), outs4 in zip(loc, loc_out):
            g = summed_shares(got, w_r.shape[0])
            if len(w_r.shape) == 3:
                for r in range(w_r.shape[0]):
                    g_row = g[r:r + 1, :]
                    for o_r, val in zip(outs4, (g_row,) + _adamw(w_r[r], g_row, m_r[r], v_r[r])):
                        o_r[r] = val
            else:
                for o_r, val in zip(outs4, (g,) + _adamw(w_r[...], g, m_r[...], v_r[...])):
                    o_r[...] = val
        for k, (name, off) in enumerate(_SMALL):
            w = sw[k][...]
            if name == "b_mod":
                gk = gb_ref[...]
            elif name == "c_ctx":
                acc = gc_ref[0, 0:1, :]
                for j in range(1, N_DEV):
                    acc = acc + gc_ref[j, 0:1, :]
                sg = _sigmoid(w)
                gk = acc * (sg * (1.0 + w * (1.0 - sg)))
            else:
                gk = sum_ref[:, off:off + w.shape[1]]
            dl, m_new, v_new = _adamw(w, gk, sm[k][...], sv[k][...])
            souts[k][...] = gk
            souts[ns + k][...] = dl
            souts[2 * ns + k][...] = m_new
            souts[3 * ns + k][...] = v_new

        writes, n_started = [], 0
        for j, (a, r0, nr) in enumerate(jobs):
            for cp in reads[j]:
                cp.wait()
            rows = pl.ds(r0, nr)
            g_b, w_b, m_b, v_b = src_buf[a]
            if a < n_tiled:
                g = g_b[rows, :]
                vals = (g,) + _adamw(w_b[rows, :], g, m_b[rows, :], v_b[rows, :])
            else:
                g = summed_shares(g_b, nr)
                vals = (g,) + _adamw(w_b[...], g, m_b[...], v_b[...])
            for k, val in enumerate(vals):
                dst_buf[a][k][rows, :] = val
                cp = pltpu.make_async_copy(rows_of(dst_buf[a][k], r0, nr), rows_of(dst[a][k], r0, nr),
                                           write_sem.at[n_started])
                cp.start()
                writes.append(cp)
                n_started += 1
        for cp in writes:
            cp.wait()

    vm = pl.BlockSpec(memory_space=pltpu.VMEM)
    anyspace = pl.BlockSpec(memory_space=pl.ANY)
    args, out_shape, scratch = [], [], []
    for item in streams:
        args += list(item)
        scratch += [pltpu.VMEM(a.shape, a.dtype) for a in item]
    for item in local_sums:
        args += list(item)
    args += [summed, g_bmod, gc_all, *small_w, *small_m, *small_v]
    for (_, w, _, _), k in zip(streams, n_results):
        out_shape += [jax.ShapeDtypeStruct(w.shape, F32)] * k
    scratch += [pltpu.VMEM(s.shape, F32) for s in out_shape]
    n_streamed_out = len(out_shape)
    for _, w, _, _ in local_sums:
        out_shape += [jax.ShapeDtypeStruct(w.shape, F32)] * 4
    out_shape += [jax.ShapeDtypeStruct(w.shape, F32) for w in small_w] * 4
    scratch += [pltpu.SemaphoreType.DMA((n_reads,)), pltpu.SemaphoreType.DMA((n_writes,))]
    outs = pl.pallas_call(
        body, name="final_adamw", out_shape=out_shape,
        in_specs=[anyspace] * (4 * len(streams)) + [vm] * (len(args) - 4 * len(streams)),
        out_specs=[anyspace] * n_streamed_out + [vm] * (len(out_shape) - n_streamed_out),
        scratch_shapes=scratch, compiler_params=pltpu.CompilerParams(vmem_limit_bytes=VMEM_LIMIT),
    )(*args)
    it = iter(outs)
    take = lambda k: tuple(next(it) for _ in range(k))
    r_tiled = [take(4) for _ in tiled]
    r_sums = [take(4) for _ in streamed_sums]
    r_local = [take(4) for _ in local_sums]
    small_outs = [list(take(ns)) for _ in range(4)]
    return r_tiled, r_sums, r_local, small_outs


def _rope_tables(s):
    t = jnp.arange(s, dtype=jnp.int32)
    row = (t // GRID_W).astype(F32)
    col = (t % GRID_W).astype(F32)
    freqs = ROPE_THETA ** (-jnp.arange(0, HD // 2, 2, dtype=F32) / (HD // 2))
    ang_r = row[:, None] * freqs[None, :]
    ang_c = col[:, None] * freqs[None, :]
    cr, sr, cc, sc = jnp.cos(ang_r), jnp.sin(ang_r), jnp.cos(ang_c), jnp.sin(ang_c)
    cos = jnp.concatenate([cr, cr, cc, cc], axis=-1)
    sins = jnp.concatenate([-sr, sr, -sc, sc], axis=-1)
    return jnp.tile(cos, (1, KVW // HD)), jnp.tile(sins, (1, KVW // HD))


def kernel(x, c, ctx, c_ctx, w_mod, b_mod, norm_w, w_in, q_norm_w, k_norm_w, conv_w, conv_b, conv_ln_w, conv_ln_b, w_pw, b_pw, w_out, loss_target, m_c_ctx, m_w_mod, m_b_mod, m_norm_w, m_w_in, m_q_norm_w, m_k_norm_w, m_conv_w, m_conv_b, m_conv_ln_w, m_conv_ln_b, m_w_pw, m_b_pw, m_w_out, v_c_ctx, v_w_mod, v_b_mod, v_norm_w, v_w_in, v_q_norm_w, v_k_norm_w, v_conv_w, v_conv_b, v_conv_ln_w, v_conv_ln_b, v_w_pw, v_b_pw, v_w_out):
    bl, s, _ = x.shape
    cl = ctx.shape[1]
    me = _lin(*_coords())

    conv_w_pad = jnp.pad(conv_w[0], ((0, 32 - KW), (0, 0)))
    n_ex = N_DEV * bl
    g_win, c_rows, g_mod = _prologue(w_in[0].T, c, c_ctx[None, :], w_mod[0], b_mod)
    w_in_b = g_win.reshape(D_IN, D)
    mod_all = g_mod.transpose(1, 0, 2).reshape(n_ex + 8, 3 * D)
    modrows = lax.dynamic_slice_in_dim(mod_all, me * bl, bl, axis=0).reshape(bl, 3, D)
    modc = mod_all[n_ex].reshape(1, 3, D)

    cos, sins = _rope_tables(s)
    qnw_t = jnp.tile(q_norm_w, (1, DA // HD))
    knw_t = jnp.tile(k_norm_w, (1, KVW // HD))
    lane = jnp.arange(DA, dtype=jnp.int32) // HD
    ones_bd = (lane[:, None] == lane[None, :]).astype(BF16)
    ones_kv = ones_bd[0:KVW, 0:KVW]
    w_kv_b = w_in_b

    k_ctx, v_ctx, pkv_c = _ctx_fwd(ctx, modc, norm_w, w_kv_b, knw_t, ones_kv, cl + s)
    (q_h, k_h, v_h, pq, pkv, za, glu, zc), (g_wout, g_wpw, g_cw) = _fwd_in(
        x, modrows, norm_w, w_in_b, cos, sins, qnw_t, knw_t, ones_bd, k_ctx, v_ctx,
        [w_out[0], w_pw[0], conv_w_pad], [BF16, BF16, F32])
    w_out_b = g_wout.reshape(D, D)
    w_pw_b = g_wpw.reshape(DC, DC)
    conv_w_full = g_cw.transpose(1, 0, 2).reshape(32, DC)
    attn, lse = _attn_fwd(q_h, k_h, v_h)
    y_conv, cp = _conv_fwd(glu, conv_w_full, conv_b, conv_ln_w, conv_ln_b, w_pw_b, b_pw)

    do_h, dza, dy_conv, dzc, dh, dgate, gw_out, loss_row, gw_pw, conv_rows = _out_fwd_bwd(
        attn, za, cp, zc, x, loss_target, modrows, w_out_b, y_conv, conv_ln_w, conv_ln_b, w_pw_b)
    dglu, parts_cw = _conv_bwd_depthwise(glu, dy_conv, conv_w_full)
    parts_out = gw_out.reshape(N_DEV, D // N_DEV, D)
    parts_pw = gw_pw.reshape(N_DEV, DC // N_DEV, DC)
    (dq, dk_h, dv_h), (got_out, got_pw, got_cw) = _attn_bwd(
        q_h, k_h, v_h, do_h, attn, lse, [parts_out, parts_pw, parts_cw])
    gw_kv, ctx_rows, dknw_c = _ctx_bwd(ctx, modc, norm_w, w_kv_b, pkv_c, dk_h, dv_h, knw_t, ones_kv)
    grad_x, gw_in, dmod_ss, dnw, dqnw, dknw = _bwd_in(
        x, modrows, norm_w, w_in_b, cos, sins, qnw_t, knw_t, ones_bd,
        pq, pkv, dq, dk_h, dv_h, dza, dglu, dzc, dh, gw_kv)

    given = {"c_ctx": (c_ctx, m_c_ctx, v_c_ctx), "b_mod": (b_mod, m_b_mod, v_b_mod), "norm_w": (norm_w, m_norm_w, v_norm_w),
             "q_norm_w": (q_norm_w, m_q_norm_w, v_q_norm_w), "k_norm_w": (k_norm_w, m_k_norm_w, v_k_norm_w),
             "conv_b": (conv_b, m_conv_b, v_conv_b), "conv_ln_w": (conv_ln_w, m_conv_ln_w, v_conv_ln_w),
             "conv_ln_b": (conv_ln_b, m_conv_ln_b, v_conv_ln_b), "b_pw": (b_pw, m_b_pw, v_b_pw)}
    as_rows = [[given[name][which].reshape(1, -1) for name, _ in _SMALL] for which in range(3)]
    g_in_t, g_wmod, summed, g_bmod, gc_all, loss11 = _epilogue(
        gw_in.reshape(N_DEV, D_IN // N_DEV, D),
        [loss_row, ctx_rows, dnw, dqnw, dknw, dknw_c, conv_rows, dmod_ss, dgate], c_rows, w_mod[0])
    (r_in, r_wmod), (r_out,), (r_pw, r_cw), small_outs = _final_adamw(
        [(g_in_t, w_in[0].T, m_w_in[0].T, v_w_in[0].T), (g_wmod, w_mod[0], m_w_mod[0], v_w_mod[0])],
        [(got_out, w_out[0], m_w_out[0], v_w_out[0])],
        [(got_pw, w_pw[0], m_w_pw[0], v_w_pw[0]),
         (got_cw, conv_w.transpose(1, 0, 2), m_conv_w.transpose(1, 0, 2), v_conv_w.transpose(1, 0, 2))],
        summed, g_bmod, gc_all, *as_rows)
    r_in = tuple(a.T for a in r_in)
    r_cw = tuple(a.transpose(1, 0, 2)[0] for a in r_cw)

    big = {"w_mod": r_wmod, "w_in": r_in, "conv_w": r_cw, "w_pw": r_pw, "w_out": r_out}
    order = ["c_ctx", "w_mod", "b_mod", "norm_w", "w_in", "q_norm_w", "k_norm_w", "conv_w", "conv_b", "conv_ln_w",
             "conv_ln_b", "w_pw", "b_pw", "w_out"]
    small_index = {name: k for k, (name, _) in enumerate(_SMALL)}
    outs = [loss11.reshape(()), grad_x]
    for which in range(4):
        for name in order:
            if name in big:
                outs.append(big[name][which][None])
            else:
                outs.append(small_outs[which][small_index[name]].reshape(given[name][0].shape))
    return tuple(outs)
```

```python
import jax
import jax.numpy as jnp
import numpy as np
from jax import lax
from jax.experimental import pallas as pl
from jax.experimental.pallas import tpu as pltpu

F32, BF16 = jnp.float32, jnp.bfloat16
MESH_ID = pl.DeviceIdType.MESH

N_DEV = 8
D = 1024
D_IN = 2816
DA = 512
DC = 512
HD = 64
KVW = 128
KW = 31
HALO = 16
EPS = 1e-6
ROPE_THETA = 10000.0
GRID_W = 64

ADAM_LR, ADAM_B1, ADAM_B2, ADAM_EPS, ADAM_WD, ADAM_STEP = 0.001, 0.9, 0.999, 1e-08, 0.01, 10

VMEM_LIMIT = 56 * 1024 * 1024

TM = 256
TQ = 128
TOKEN_PARTS = 2
OUT_TM = 512
BWD_PARTS = 4
FWD_PARTS = 8
TC = 512
CH = 32
ADAM_STEPS = 4


def _params(sem, vmem=VMEM_LIMIT):
    return pltpu.CompilerParams(dimension_semantics=sem, vmem_limit_bytes=vmem)


def _dot(a, b):
    return jnp.dot(a, b, preferred_element_type=F32)


def _dot_nt(a, b):
    return lax.dot_general(a, b, (((1,), (1,)), ((), ())), preferred_element_type=F32)


def _dot_tn(a, b):
    return lax.dot_general(a, b, (((0,), (0,)), ((), ())), preferred_element_type=F32)


def _sigmoid(z):
    return 1.0 / (1.0 + jnp.exp(-z))


def _segsum(v, ones_bd):
    return _dot(v.astype(BF16), ones_bd)


def _swap16(x):
    w = x.shape[-1]
    lane = lax.broadcasted_iota(jnp.int32, x.shape, 1)
    return jnp.where((lane % 32) < 16, pltpu.roll(x, w - 16, 1), pltpu.roll(x, 16, 1))


def _with_ones_column(v):
    one = (lax.broadcasted_iota(jnp.int32, v.shape, 1) == 0).astype(v.dtype)
    return jnp.concatenate([v, one], axis=-1)


def _rope(x, cos, sins):
    return x * cos + _swap16(x) * sins


def _rope_bwd(d, cos, sins):
    return d * cos + _swap16(d * sins)


def _adamw(w, g, m, v):
    m2 = ADAM_B1 * m + (1.0 - ADAM_B1) * g
    v2 = ADAM_B2 * v + (1.0 - ADAM_B2) * (g * g)
    m_hat = m2 / (1.0 - ADAM_B1 ** ADAM_STEP)
    v_hat = v2 / (1.0 - ADAM_B2 ** ADAM_STEP)
    delta = -ADAM_LR * (m_hat / (jnp.sqrt(v_hat) + ADAM_EPS) + ADAM_WD * w)
    return delta, m2, v2


def _coords():
    return lax.axis_index("x"), lax.axis_index("y"), lax.axis_index("c")


def _lin(x, y, c):
    return 4 * x + 2 * y + c


def _prologue(w_in_t, c, c_ctx_row, w_mod_loc, b_mod):
    bl = c.shape[0]
    n_ex = N_DEV * bl
    n_mod = w_mod_loc.shape[1]

    def body(w32_ref, c_in_ref, cctx_ref, wm_ref, b_ref, out_w, crows_ref, mod_out, w_ref, c_ref, c_gath, mod_mine,
             w_send, w_recv, c_send, c_recv, m_send, m_recv, local_sems):
        x, y, c = _coords()
        me_lin = _lin(x, y, c)
        c_ref[...] = jnp.zeros_like(c_ref)
        c_ref[0:bl, :] = c_in_ref[...]
        w_ref[...] = w32_ref[...].astype(BF16)
        me, sib = (x, y, c), (x, y, 1 - c)
        xnb, ynb, diag = (1 - x, y), (x, 1 - y), (1 - x, 1 - y)
        north = c == 1

        def direct_gather(src, dst, send_sems, recv_sems, local_sem):
            cps = [pltpu.make_async_copy(src, dst.at[me_lin], local_sem)]
            for k in range(1, N_DEV):
                peer = (1 - x if k & 4 else x, 1 - y if k & 2 else y, 1 - c if k & 1 else c)
                cps.append(pltpu.make_async_remote_copy(
                    src_ref=src, dst_ref=dst.at[me_lin], send_sem=send_sems.at[k - 1], recv_sem=recv_sems.at[k - 1],
                    device_id=peer, device_id_type=MESH_ID))
            for cp in cps:
                cp.start()
            return cps

        def copy(k, block, to, src=None):
            slot = out_w.at[_lin(*block)]
            return pltpu.make_async_remote_copy(
                src_ref=slot if src is None else src, dst_ref=slot, send_sem=w_send.at[k], recv_sem=w_recv.at[k],
                device_id=to, device_id_type=MESH_ID)

        c_cps = direct_gather(c_ref, c_gath, c_send, c_recv, local_sems.at[0])
        mine = pltpu.make_async_copy(w_ref, out_w.at[me_lin], local_sems.at[1])
        mine.start()
        first = [copy(0, me, sib, src=w_ref), copy(1, me, (*xnb, c), src=w_ref), copy(2, me, (*ynb, c), src=w_ref)]
        for cp in first:
            cp.start()

        for cp in c_cps[1:]:
            cp.wait_recv()
        c_cps[0].wait()
        crows_ref[...] = jnp.zeros_like(crows_ref)
        for j in range(N_DEV):
            crows_ref[j * bl:(j + 1) * bl, :] = c_gath[j, 0:bl, :]
        crows_ref[n_ex:n_ex + 1, :] = cctx_ref[...]
        cr = crows_ref[...]
        act = (cr * _sigmoid(cr)).astype(BF16)
        mod_mine[...] = _dot(act, wm_ref[...].astype(BF16)) + b_ref[:, pl.ds(pl.multiple_of(me_lin * n_mod, 128), n_mod)]
        mod_cps = direct_gather(mod_mine, mod_out, m_send, m_recv, local_sems.at[2])

        relay_north = copy(3, (*xnb, c), (*ynb, c))
        relay_south = copy(3, (*ynb, c), (*xnb, c))
        passed = []
        copy(1, (*xnb, c), me).wait_recv()
        pl.when(north)(relay_north.start)
        passed.append(copy(4, (*xnb, c), sib))
        passed[-1].start()
        copy(2, (*ynb, c), me).wait_recv()
        pl.when(jnp.logical_not(north))(relay_south.start)
        passed.append(copy(5, (*ynb, c), sib))
        passed[-1].start()
        copy(3, (*diag, c), me).wait_recv()
        passed.append(copy(6, (*diag, c), sib))
        passed[-1].start()
        copy(0, sib, me).wait_recv()
        for k, chip in ((4, xnb), (5, ynb), (6, diag)):
            copy(k, (*chip, 1 - c), me).wait_recv()
        for cp in mod_cps[1:]:
            cp.wait_recv()
        mod_cps[0].wait()
        for cp in first + passed + [relay_north] + c_cps[1:] + mod_cps[1:]:
            cp.wait_send()
        mine.wait()

    vm = pl.BlockSpec(memory_space=pltpu.VMEM)
    seven = pltpu.SemaphoreType.DMA((N_DEV - 1,))
    return pl.pallas_call(
        body, name="prologue",
        out_shape=[jax.ShapeDtypeStruct((N_DEV,) + w_in_t.shape, BF16), jax.ShapeDtypeStruct((n_ex + 8, D), F32),
                   jax.ShapeDtypeStruct((N_DEV, n_ex + 8, n_mod), F32)],
        in_specs=[vm] * 5, out_specs=[pl.BlockSpec(memory_space=pl.ANY), vm, vm],
        scratch_shapes=[pltpu.VMEM(w_in_t.shape, BF16), pltpu.VMEM((8, D), F32), pltpu.VMEM((N_DEV, 8, D), F32),
                        pltpu.VMEM((n_ex + 8, n_mod), F32),
                        seven, seven, seven, seven, seven, seven, pltpu.SemaphoreType.DMA((3,))],
        compiler_params=pltpu.CompilerParams(vmem_limit_bytes=VMEM_LIMIT),
    )(w_in_t, c, c_ctx_row, w_mod_loc, b_mod)


def _exchange_copies(in_refs, out_refs, send_sems, recv_sems, local_sems, scatter):
    x, y, c = _coords()
    me = _lin(x, y, c)
    local, remote = [], []
    for a, (src, dst) in enumerate(zip(in_refs, out_refs)):
        local.append(pltpu.make_async_copy(src.at[me] if scatter else src, dst.at[me], local_sems.at[a]))
        for k in range(1, N_DEV):
            peer = (1 - x if k & 4 else x, 1 - y if k & 2 else y, 1 - c if k & 1 else c)
            remote.append(pltpu.make_async_remote_copy(
                src_ref=src.at[_lin(*peer)] if scatter else src, dst_ref=dst.at[me],
                send_sem=send_sems.at[a * (N_DEV - 1) + k - 1], recv_sem=recv_sems.at[a * (N_DEV - 1) + k - 1],
                device_id=peer, device_id_type=MESH_ID))
    return local, remote


def _exchange_scratch(n):
    return [pltpu.SemaphoreType.DMA((n * (N_DEV - 1),)), pltpu.SemaphoreType.DMA((n * (N_DEV - 1),)),
            pltpu.SemaphoreType.DMA((n,))]


def _fwd_in(x, modrows, norm_w, w_in_b, cos, sins, qnw_t, knw_t, ones_bd, k_all, v_all, shards, wire_dtypes):
    bl, s, _ = x.shape
    tm = TOKEN_PARTS * TM
    nt = s // tm
    n_sh = len(shards)

    def body(*refs):
        (x_ref, mod_ref, nw_ref, win_ref, cos_ref, sin_ref, qnw_ref, knw_ref, bd_ref, kin_ref, vin_ref) = refs[:11]
        shard_refs = refs[11:11 + n_sh]
        q_ref, k_ref, v_ref, pq_ref, pkv_ref, za_ref, glu_ref, zc_ref = refs[11 + n_sh:19 + n_sh]
        gathered_refs = refs[19 + n_sh:19 + 2 * n_sh]
        stage_refs = refs[19 + 2 * n_sh:19 + 3 * n_sh]
        send_sems, recv_sems, local_sems = refs[19 + 3 * n_sh:]
        b, i = pl.program_id(0), pl.program_id(1)
        local, remote = _exchange_copies(stage_refs, gathered_refs, send_sems, recv_sems, local_sems, scatter=False)

        @pl.when((b == 0) & (i == 0))
        def _():
            for src, stage in zip(shard_refs, stage_refs):
                stage[...] = src[...].astype(stage.dtype)
            for cp in local + remote:
                cp.start()

        shift = mod_ref[0, 0:1, :]
        scale = mod_ref[0, 1:2, :]
        for part in range(TOKEN_PARTS):
            rows = pl.ds(part * TM, TM)
            xv = x_ref[0, rows, :]
            r = lax.rsqrt(jnp.mean(xv * xv, axis=-1, keepdims=True) + EPS)
            u = (xv * r * nw_ref[...]) * (1.0 + scale) + shift
            p = _dot_nt(u.astype(BF16), win_ref[...])
            pq = p[:, 0:DA]
            pk = p[:, DA:DA + HD * 2]
            ck = cos_ref[rows, :]
            sk = sin_ref[rows, :]
            cs = jnp.concatenate([ck] * (DA // KVW), axis=-1)
            sn = jnp.concatenate([sk] * (DA // KVW), axis=-1)
            rq = lax.rsqrt(_segsum(pq * pq, bd_ref[...]) * (1.0 / HD) + EPS)
            qn = pq * rq * qnw_ref[...]
            qr = _rope(qn, cs, sn) * 0.125
            for h in range(DA // HD):
                q_ref[0, h, rows, :] = qr[:, h * HD:(h + 1) * HD].astype(BF16)
            rk = lax.rsqrt(_segsum(pk * pk, bd_ref[0:KVW, 0:KVW]) * (1.0 / HD) + EPS)
            kn = pk * rk * knw_ref[...]
            kr = _rope(kn, ck, sk)
            pv = p[:, 640:768]
            for h in range(KVW // HD):
                k_ref[0, h, rows, :] = kr[:, h * HD:(h + 1) * HD].astype(BF16)
                v_ref[0, h, rows, :] = _with_ones_column(pv[:, h * HD:(h + 1) * HD]).astype(BF16)
            pq_ref[0, rows, :] = pq
            pkv_ref[0, rows, :] = p[:, 512:768]
            za_ref[0, rows, :] = p[:, 768:1280]
            glu_ref[0, rows, :] = p[:, 1280:2304]
            zc_ref[0, rows, :] = p[:, 2304:2816]

        @pl.when((b == bl - 1) & (i == nt - 1))
        def _():
            for cp in remote:
                cp.wait_recv()
            for cp in remote:
                cp.wait_send()
            for cp in local:
                cp.wait()

    def tile(w):
        return pl.BlockSpec((1, tm, w), lambda b, i: (b, i, 0))

    def const(shape):
        return pl.BlockSpec(shape, lambda b, i: (0,) * len(shape))

    outs = [(DA, F32), (2 * KVW, F32), (DA, F32), (2 * DC, F32), (DC, F32)]
    anyspace = pl.BlockSpec(memory_space=pl.ANY)
    rope = pl.BlockSpec((tm, KVW), lambda b, i: (i, 0))
    k_tile = pl.BlockSpec((1, KVW // HD, tm, HD), lambda b, i: (b, 0, i, 0))
    v_tile = pl.BlockSpec((1, KVW // HD, tm, 2 * HD), lambda b, i: (b, 0, i, 0))
    res = pl.pallas_call(
        body, name="fwd_in", grid=(bl, nt),
        in_specs=[tile(D), pl.BlockSpec((1, 3, D), lambda b, i: (b, 0, 0)), const((1, D)), const((D_IN, D)),
                  rope, rope, const((1, DA)), const((1, KVW)), const((DA, DA)), anyspace, anyspace]
        + [const(a.shape) for a in shards],
        out_specs=[pl.BlockSpec((1, DA // HD, tm, HD), lambda b, i: (b, 0, i, 0)), k_tile, v_tile]
        + [tile(w) for w, _ in outs] + [anyspace] * n_sh,
        out_shape=[jax.ShapeDtypeStruct((bl, DA // HD, s, HD), BF16), jax.ShapeDtypeStruct(k_all.shape, BF16),
                   jax.ShapeDtypeStruct(v_all.shape, BF16)]
        + [jax.ShapeDtypeStruct((bl, s, w), dt) for w, dt in outs]
        + [jax.ShapeDtypeStruct((N_DEV,) + a.shape, dt) for a, dt in zip(shards, wire_dtypes)],
        input_output_aliases={9: 1, 10: 2},
        scratch_shapes=[pltpu.VMEM(a.shape, dt) for a, dt in zip(shards, wire_dtypes)] + _exchange_scratch(n_sh),
        compiler_params=_params(("arbitrary", "arbitrary")),
    )(x, modrows, norm_w, w_in_b, cos, sins, qnw_t, knw_t, ones_bd, k_all, v_all, *shards)
    return res[:8], res[8:]


_KV_ROWS_OF_W_IN_T = pl.BlockSpec((2 * KVW, D), lambda b: (DA // (2 * KVW), 0))


def _ctx_fwd(ctx, modc, norm_w, w_kv_b, knw_t, ones_bd, n_keys):
    bl, cl, _ = ctx.shape

    def body(x_ref, mod_ref, nw_ref, w_ref, knw_ref, bd_ref, k_ref, v_ref, pkv_ref):
        xv = x_ref[0]
        shift = mod_ref[0, 0:1, :]
        scale = mod_ref[0, 1:2, :]
        r = lax.rsqrt(jnp.mean(xv * xv, axis=-1, keepdims=True) + EPS)
        u = (xv * r * nw_ref[...]) * (1.0 + scale) + shift
        p = _dot_nt(u.astype(BF16), w_ref[...])
        pk = p[:, 0:KVW]
        rk = lax.rsqrt(_segsum(pk * pk, bd_ref[...]) * (1.0 / HD) + EPS)
        kn = pk * rk * knw_ref[...]
        pv = p[:, KVW:2 * KVW]
        for h in range(KVW // HD):
            k_ref[0, h] = kn[:, h * HD:(h + 1) * HD].astype(BF16)
            v_ref[0, h] = _with_ones_column(pv[:, h * HD:(h + 1) * HD]).astype(BF16)
        pkv_ref[0] = p

    def const(shape):
        return pl.BlockSpec(shape, lambda b: (0,) * len(shape))

    def tile(w):
        return pl.BlockSpec((1, cl, w), lambda b: (b, 0, 0))

    ctx_block = (n_keys - cl) // cl
    assert ctx_block * cl + cl == n_keys
    k_tile = pl.BlockSpec((1, KVW // HD, cl, HD), lambda b: (b, 0, ctx_block, 0))
    v_tile = pl.BlockSpec((1, KVW // HD, cl, 2 * HD), lambda b: (b, 0, ctx_block, 0))
    return pl.pallas_call(
        body, name="ctx_fwd", grid=(bl,),
        in_specs=[tile(D), const((1, 3, D)), const((1, D)), _KV_ROWS_OF_W_IN_T, const((1, KVW)), const((KVW, KVW))],
        out_specs=[k_tile, v_tile, tile(2 * KVW)],
        out_shape=[jax.ShapeDtypeStruct((bl, KVW // HD, n_keys, HD), BF16),
                   jax.ShapeDtypeStruct((bl, KVW // HD, n_keys, 2 * HD), BF16),
                   jax.ShapeDtypeStruct((bl, cl, 2 * KVW), F32)],
        compiler_params=_params(("arbitrary",)),
    )(ctx, modc, norm_w, w_kv_b, knw_t, ones_bd)


def _attn_fwd(q, k, v1):
    bl, _, s, _ = q.shape
    n_keys = k.shape[2]

    def body(q_ref, k_ref, v_ref, o_ref, lse_ref):
        kv = k_ref[0, 0]
        vv = v_ref[0, 0]
        lane = lax.broadcasted_iota(jnp.int32, (TQ, 2 * HD), 1)
        for part in range(FWD_PARTS):
            rows = pl.ds(part * TQ, TQ)
            lse = jnp.zeros((TQ, 2 * HD), F32)
            heads = []
            sc_all = _dot_nt(q_ref[0, :, rows, :].reshape(4 * TQ, HD), kv)
            for h in range(4):
                sc = sc_all[h * TQ:(h + 1) * TQ, :]
                m = jnp.max(sc, axis=-1, keepdims=True)
                e = jnp.exp(sc - m).astype(BF16)
                ov = _dot(e, vv)
                denom = ov[:, HD:HD + 1]
                heads.append(ov[:, 0:HD] * (1.0 / denom))
                lse = jnp.where(lane == h, m + jnp.log(denom), lse)
            o_ref[0, rows, :] = jnp.concatenate(heads, axis=-1)
            lse_ref[0, 0, rows, :] = lse

    tq = FWD_PARTS * TQ
    ks = pl.BlockSpec((1, 1, n_keys, HD), lambda b, g, i: (b, g, 0, 0))
    qs = pl.BlockSpec((1, 4, tq, HD), lambda b, g, i: (b, g, i, 0))
    vs = pl.BlockSpec((1, 1, n_keys, 2 * HD), lambda b, g, i: (b, g, 0, 0))
    return pl.pallas_call(
        body, name="attn_fwd", grid=(bl, 2, s // tq), in_specs=[qs, ks, vs],
        out_specs=[pl.BlockSpec((1, tq, 4 * HD), lambda b, g, i: (b, i, g)),
                   pl.BlockSpec((1, 1, tq, 2 * HD), lambda b, g, i: (b, g, i, 0))],
        out_shape=[jax.ShapeDtypeStruct((bl, s, DA), F32), jax.ShapeDtypeStruct((bl, 2, s, 2 * HD), F32)],
        compiler_params=_params(("arbitrary", "arbitrary", "arbitrary")),
    )(q, k, v1)


def _attn_bwd(q, k, v1, do, o, lse, exchange):
    bl, _, s, _ = q.shape
    n_keys = k.shape[2]
    tq = BWD_PARTS * TQ
    nq = s // tq
    n_ex = len(exchange)

    def body(*refs):
        q_ref, k_ref, v_ref, do_ref, o_ref, lse_ref = refs[:6]
        part_refs = refs[6:6 + n_ex]
        dq_ref, dk_ref, dv_ref = refs[6 + n_ex:9 + n_ex]
        got_refs = refs[9 + n_ex:9 + 2 * n_ex]
        p_sc, ds_sc, dkt, dvt, send_sems, recv_sems, local_sems = refs[9 + 2 * n_ex:]
        i = pl.program_id(2)
        first = (pl.program_id(0) == 0) & (pl.program_id(1) == 0) & (i == 0)
        last = (pl.program_id(0) == bl - 1) & (pl.program_id(1) == 1) & (i == nq - 1)
        local, remote = _exchange_copies(part_refs, got_refs, send_sems, recv_sems, local_sems, scatter=True)

        @pl.when(first)
        def _():
            for cp in local + remote:
                cp.start()

        @pl.when(i == 0)
        def _():
            dkt[...] = jnp.zeros_like(dkt)
            dvt[...] = jnp.zeros_like(dvt)

        kv = k_ref[0, 0]
        vv = v_ref[0, 0][:, 0:HD]
        for part in range(BWD_PARTS):
            tq_rows = pl.ds(part * TQ, TQ)
            lse = lse_ref[0, 0, tq_rows, :]
            ov = o_ref[0, tq_rows, :]
            dqs = []
            q_cat = q_ref[0, :, tq_rows, :].reshape(4 * TQ, HD)
            do_cat = do_ref[0, :, tq_rows, :].reshape(4 * TQ, HD)
            sc_all = _dot_nt(q_cat, kv)
            for h in range(4):
                doh = do_cat[h * TQ:(h + 1) * TQ, :]
                delta = jnp.sum(ov[:, h * HD:(h + 1) * HD] * doh.astype(F32), axis=-1, keepdims=True)
                rows = pl.ds((part * 4 + h) * TQ, TQ)
                p = jnp.exp(sc_all[h * TQ:(h + 1) * TQ, :] - lse[:, h:h + 1])
                ds = (p * (_dot_nt(doh, vv) - delta)).astype(BF16)
                p_sc[rows, :] = p.astype(BF16)
                ds_sc[rows, :] = ds
                dqs.append(_dot(ds, kv) * 0.125)
            dq_ref[0, tq_rows, :] = jnp.concatenate(dqs, axis=-1)
            part_rows = pl.ds(part * 4 * TQ, 4 * TQ)
            dvt[...] += _dot_tn(do_cat, p_sc[part_rows, :])
            dkt[...] += _dot_tn(q_cat, ds_sc[part_rows, :])

        @pl.when(i == nq - 1)
        def _():
            dk_ref[0, 0] = dkt[...].T
            dv_ref[0, 0] = dvt[...].T

        @pl.when(last)
        def _():
            for cp in remote:
                cp.wait_recv()
            for cp in remote:
                cp.wait_send()
            for cp in local:
                cp.wait()

    qs = pl.BlockSpec((1, 4, tq, HD), lambda b, g, i: (b, g, i, 0))
    ks = pl.BlockSpec((1, 1, n_keys, HD), lambda b, g, i: (b, g, 0, 0))
    vs = pl.BlockSpec((1, 1, n_keys, 2 * HD), lambda b, g, i: (b, g, 0, 0))
    os_ = pl.BlockSpec((1, tq, 4 * HD), lambda b, g, i: (b, i, g))
    kshape = jax.ShapeDtypeStruct(k.shape, F32)
    anyspace = pl.BlockSpec(memory_space=pl.ANY)
    res = pl.pallas_call(
        body, name="attn_bwd", grid=(bl, 2, nq),
        in_specs=[qs, ks, vs, qs, os_, pl.BlockSpec((1, 1, tq, 2 * HD), lambda b, g, i: (b, g, i, 0))]
        + [anyspace] * n_ex,
        out_specs=[os_, ks, ks] + [anyspace] * n_ex,
        out_shape=[jax.ShapeDtypeStruct((bl, s, DA), F32), kshape, kshape]
        + [jax.ShapeDtypeStruct(a.shape, a.dtype) for a in exchange],
        scratch_shapes=[pltpu.VMEM((4 * tq, n_keys), BF16), pltpu.VMEM((4 * tq, n_keys), BF16),
                        pltpu.VMEM((HD, n_keys), F32), pltpu.VMEM((HD, n_keys), F32)] + _exchange_scratch(n_ex),
        compiler_params=_params(("arbitrary", "arbitrary", "arbitrary")),
    )(q, k, v1, do, o, lse, *exchange)
    return res[:3], res[3:]


def _halo_specs(width, s):
    per = TC // HALO
    last = s // HALO - 1
    main = pl.BlockSpec((1, TC, width), lambda b, i: (b, i, 0))
    prev = pl.BlockSpec((1, HALO, width), lambda b, i: (b, jnp.maximum(i * per - 1, 0), 0))
    nxt = pl.BlockSpec((1, HALO, width), lambda b, i: (b, jnp.minimum((i + 1) * per, last), 0))
    return main, prev, nxt


def _glu(g):
    return g[:, 0:DC] * _sigmoid(g[:, DC:2 * DC])


def _fill_padded(pad_ref, main, prev, nxt, first, last):
    tc = main.shape[0]
    pad_ref[0:HALO, :] = jnp.where(first, 0.0, prev)
    pad_ref[HALO:HALO + tc, :] = main
    pad_ref[HALO + tc:2 * HALO + tc, :] = jnp.where(last, 0.0, nxt)


PLANE_ROWS = TC + 2 * HALO - 8


def _shift_planes(pad_ref, planes_ref):
    for r in range(1, 8):
        planes_ref[r - 1] = pad_ref[pl.ds(r, planes_ref.shape[1]), :]


def _tap_rows(pad_ref, planes_ref, offset, start, n):
    a, r = divmod(offset, 8)
    if r == 0:
        return pad_ref[pl.ds(start + 8 * a, n), :]
    return planes_ref[r - 1, pl.ds(start + 8 * a, n), :]


def _conv_fwd(glu, conv_w, conv_b, ln_w, ln_b, w_pw_b, b_pw):
    bl, s, _ = glu.shape
    nt = s // TC

    def body(g_ref, gp_ref, gn_ref, cw_ref, cb_ref, lw_ref, lb_ref, wpw_ref, bpw_ref, y_ref, cp_ref, pad_ref, planes_ref):
        i = pl.program_id(1)
        _fill_padded(pad_ref, _glu(g_ref[0]), _glu(gp_ref[0]), _glu(gn_ref[0]), i == 0, i == nt - 1)
        _shift_planes(pad_ref, planes_ref)
        for ck in range(TC // CH):
            acc = jnp.zeros((CH, DC), F32) + cb_ref[...]
            for t in range(KW):
                acc = acc + _tap_rows(pad_ref, planes_ref, 1 + t, ck * CH, CH) * cw_ref[t:t + 1, :]
            y_ref[0, pl.ds(ck * CH, CH), :] = acc
        y = y_ref[0]
        mu = jnp.mean(y, axis=-1, keepdims=True)
        yc = y - mu
        var = jnp.mean(yc * yc, axis=-1, keepdims=True)
        z = yc * lax.rsqrt(var + EPS) * lw_ref[...] + lb_ref[...]
        act = z * _sigmoid(z)
        cp_ref[0] = _dot(act.astype(BF16), wpw_ref[...]) + bpw_ref[...]

    def const(shape):
        return pl.BlockSpec(shape, lambda b, i: (0,) * len(shape))

    main, prev, nxt = _halo_specs(2 * DC, s)
    tile = pl.BlockSpec((1, TC, DC), lambda b, i: (b, i, 0))
    return pl.pallas_call(
        body, name="conv_fwd", grid=(bl, nt),
        in_specs=[main, prev, nxt, const((32, DC)), const((1, DC)), const((1, DC)), const((1, DC)),
                  const((DC, DC)), const((1, DC))],
        out_specs=[tile, tile],
        out_shape=[jax.ShapeDtypeStruct((bl, s, DC), F32)] * 2,
        scratch_shapes=[pltpu.VMEM((TC + 2 * HALO, DC), F32), pltpu.VMEM((7, PLANE_ROWS, DC), F32)],
        compiler_params=_params(("arbitrary", "arbitrary")),
    )(glu, glu, glu, conv_w, conv_b, ln_w, ln_b, w_pw_b, b_pw)


def _conv_bwd_depthwise(glu, dy, conv_w):
    bl, s, _ = glu.shape
    nt = s // TC

    def body(g_ref, gp_ref, gn_ref, d_ref, dp_ref, dn_ref, cw_ref, dglu_ref, shares_ref,
             padu_ref, padd_ref, planes_u, planes_d, dcw_ref):
        i = pl.program_id(1)

        @pl.when((pl.program_id(0) == 0) & (i == 0))
        def _():
            dcw_ref[...] = jnp.zeros_like(dcw_ref)

        first, last = i == 0, i == nt - 1
        _fill_padded(padu_ref, _glu(g_ref[0]), _glu(gp_ref[0]), _glu(gn_ref[0]), first, last)
        _fill_padded(padd_ref, d_ref[0], dp_ref[0], dn_ref[0], first, last)
        _shift_planes(padu_ref, planes_u)
        _shift_planes(padd_ref, planes_d)
        for ck in range(TC // CH):
            acc = jnp.zeros((CH, DC), F32)
            for t in range(KW):
                acc = acc + _tap_rows(padd_ref, planes_d, 2 * HALO - 1 - t, ck * CH, CH) * cw_ref[t:t + 1, :]
            g = g_ref[0, pl.ds(ck * CH, CH), :]
            a = g[:, 0:DC]
            sg = _sigmoid(g[:, DC:2 * DC])
            dglu_ref[0, pl.ds(ck * CH, CH), 0:DC] = (acc * sg).astype(BF16)
            dglu_ref[0, pl.ds(ck * CH, CH), DC:2 * DC] = (acc * a * sg * (1.0 - sg)).astype(BF16)
        group = 4
        for t0 in range(0, KW, group):
            taps = range(t0, min(t0 + group, KW))
            acc8 = [jnp.zeros((8, DC), F32) for _ in taps]
            for ck in range(TC // CH):
                dchunk = d_ref[0, pl.ds(ck * CH, CH), :]
                for n, t in enumerate(taps):
                    prod = _tap_rows(padu_ref, planes_u, 1 + t, ck * CH, CH) * dchunk
                    acc8[n] = acc8[n] + jnp.sum(prod.reshape(CH // 8, 8, DC), axis=0)
            for n, t in enumerate(taps):
                dcw_ref[t:t + 1, :] += jnp.sum(acc8[n], axis=0, keepdims=True)

        @pl.when((pl.program_id(0) == bl - 1) & last)
        def _():
            for j in range(N_DEV):
                shares_ref[j] = dcw_ref[:, j * cpd:(j + 1) * cpd].astype(BF16)

    gmain, gprev, gnext = _halo_specs(2 * DC, s)
    dmain, dprev, dnext = _halo_specs(DC, s)
    cw = pl.BlockSpec((32, DC), lambda b, i: (0, 0))
    cpd = DC // N_DEV
    return pl.pallas_call(
        body, name="conv_bwd_depthwise", grid=(bl, nt),
        in_specs=[gmain, gprev, gnext, dmain, dprev, dnext, cw],
        out_specs=[gmain, pl.BlockSpec((N_DEV, 32, cpd), lambda b, i: (0, 0, 0))],
        out_shape=[jax.ShapeDtypeStruct((bl, s, 2 * DC), BF16), jax.ShapeDtypeStruct((N_DEV, 32, cpd), BF16)],
        scratch_shapes=[pltpu.VMEM((TC + 2 * HALO, DC), F32)] * 2 + [pltpu.VMEM((7, PLANE_ROWS, DC), F32)] * 2
        + [pltpu.VMEM((32, DC), F32)],
        compiler_params=_params(("arbitrary", "arbitrary")),
    )(glu, glu, glu, dy, dy, dy, conv_w)


def _out_fwd_bwd(attn, za, cp, zc, x, target, modrows, w_out_b, y_conv, ln_w, ln_b, w_pw_b):
    bl, s, _ = x.shape
    tm = OUT_TM

    def body(o_ref, za_ref, cp_ref, zc_ref, x_ref, t_ref, mod_ref, w_ref, y_ref, lw_ref, lb_ref, wpw_ref,
             do_ref, dza_ref, dy_ref, dzc_ref, dh_ref, dgate_ref, gwb_ref, loss_ref, gpwb_ref, rows_ref,
             gw_ref, gpw_ref):
        b, i = pl.program_id(0), pl.program_id(1)

        @pl.when((b == 0) & (i == 0))
        def _():
            gw_ref[...] = jnp.zeros_like(gw_ref)
            gpw_ref[...] = jnp.zeros_like(gpw_ref)
            rows_ref[...] = jnp.zeros_like(rows_ref)
            loss_ref[...] = jnp.zeros_like(loss_ref)

        @pl.when(i == 0)
        def _():
            dgate_ref[...] = jnp.zeros_like(dgate_ref)

        gate = mod_ref[0, 2:3, :]
        w = w_ref[...]
        o, za_v, cp_v, zc_v = o_ref[0], za_ref[0], cp_ref[0], zc_ref[0]
        sa = _sigmoid(za_v)
        sc = _sigmoid(zc_v)
        silu_a = za_v * sa
        silu_c = zc_v * sc
        mix = jnp.concatenate([(o * silu_a).astype(BF16), (cp_v * silu_c).astype(BF16)], axis=-1)
        out = _dot(mix, w)
        err = x_ref[0] + gate * out - t_ref[0]
        loss_ref[...] += jnp.sum(err * err, axis=0, keepdims=True)
        dh = err * (1.0 / D)
        dh_ref[0] = dh
        dgate_ref[0] += jnp.sum(dh * out, axis=0, keepdims=True)
        dout = (dh * gate).astype(BF16)
        gw_ref[...] += _dot_tn(mix, dout)
        dmix = _dot_nt(dout, w)
        dga = dmix[:, 0:DA]
        dgc = dmix[:, DA:DA + DC]
        dov = dga * silu_a
        for h in range(DA // HD):
            do_ref[0, h] = dov[:, h * HD:(h + 1) * HD].astype(BF16)
        dza_ref[0] = (dga * o * (sa * (1.0 + za_v * (1.0 - sa)))).astype(BF16)
        dzc_ref[0] = (dgc * cp_v * (sc * (1.0 + zc_v * (1.0 - sc)))).astype(BF16)
        dcp = dgc * silu_c
        y = y_ref[0]
        yc = y - jnp.mean(y, axis=-1, keepdims=True)
        rstd = lax.rsqrt(jnp.mean(yc * yc, axis=-1, keepdims=True) + EPS)
        yn = yc * rstd
        lw = lw_ref[...]
        z = yn * lw + lb_ref[...]
        sg = _sigmoid(z)
        dcp_b = dcp.astype(BF16)
        gpw_ref[...] += _dot_tn((z * sg).astype(BF16), dcp_b)
        dz = _dot_nt(dcp_b, wpw_ref[...]) * (sg * (1.0 + z * (1.0 - sg)))
        dyn = dz * lw
        dy = rstd * (dyn - jnp.mean(dyn, axis=-1, keepdims=True) - yn * jnp.mean(dyn * yn, axis=-1, keepdims=True))
        dy_ref[0] = dy
        rows_ref[0:1, :] += jnp.sum(dcp, axis=0, keepdims=True)
        rows_ref[1:2, :] += jnp.sum(dz * yn, axis=0, keepdims=True)
        rows_ref[2:3, :] += jnp.sum(dz, axis=0, keepdims=True)
        rows_ref[3:4, :] += jnp.sum(dy, axis=0, keepdims=True)

        @pl.when((b == bl - 1) & (i == s // tm - 1))
        def _():
            gwb_ref[...] = gw_ref[...].astype(BF16)
            gpwb_ref[...] = gpw_ref[...].astype(BF16)

    def const(shape):
        return pl.BlockSpec(shape, lambda b, i: (0,) * len(shape))

    def tile(w):
        return pl.BlockSpec((1, tm, w), lambda b, i: (b, i, 0))

    return pl.pallas_call(
        body, name="out_fwd_bwd", grid=(bl, s // tm),
        in_specs=[tile(DA), tile(DA), tile(DC), tile(DC), tile(D), tile(D),
                  pl.BlockSpec((1, 3, D), lambda b, i: (b, 0, 0)), const((D, D)),
                  tile(DC), const((1, DC)), const((1, DC)), const((DC, DC))],
        out_specs=[pl.BlockSpec((1, DA // HD, tm, HD), lambda b, i: (b, 0, i, 0)), tile(DA), tile(DC), tile(DC), tile(D),
                   pl.BlockSpec((1, 1, D), lambda b, i: (b, 0, 0)), const((D, D)), const((1, D)),
                   const((DC, DC)), const((8, DC))],
        out_shape=[jax.ShapeDtypeStruct((bl, DA // HD, s, HD), BF16), jax.ShapeDtypeStruct((bl, s, DA), BF16),
                   jax.ShapeDtypeStruct((bl, s, DC), F32), jax.ShapeDtypeStruct((bl, s, DC), BF16),
                   jax.ShapeDtypeStruct((bl, s, D), F32), jax.ShapeDtypeStruct((bl, 1, D), F32),
                   jax.ShapeDtypeStruct((D, D), BF16), jax.ShapeDtypeStruct((1, D), F32),
                   jax.ShapeDtypeStruct((DC, DC), BF16), jax.ShapeDtypeStruct((8, DC), F32)],
        scratch_shapes=[pltpu.VMEM((D, D), F32), pltpu.VMEM((DC, DC), F32)],
        compiler_params=_params(("arbitrary", "arbitrary")),
    )(attn, za, cp, zc, x, target, modrows, w_out_b, y_conv, ln_w, ln_b, w_pw_b)


def _rms_heads_bwd(dy, x, w_t, ones_bd):
    r = lax.rsqrt(_segsum(x * x, ones_bd) * (1.0 / HD) + EPS)
    xh = x * r
    g = dy * w_t
    dx = r * (g - xh * (_segsum(g * xh, ones_bd) * (1.0 / HD)))
    return dx, dy * xh


def _ctx_bwd(ctx, modc, norm_w, w_kv_b, pkv_c, dk_c, dv_c, knw_t, ones_bd):
    bl, cl, _ = ctx.shape

    def body(x_ref, mod_ref, nw_ref, w_ref, p_ref, dk_ref, dv_ref, knw_ref, bd_ref, gw_ref, rows_ref, dknw_ref):
        @pl.when(pl.program_id(0) == 0)
        def _():
            gw_ref[...] = jnp.zeros_like(gw_ref)
            rows_ref[...] = jnp.zeros_like(rows_ref)
            dknw_ref[...] = jnp.zeros_like(dknw_ref)

        xv = x_ref[0]
        shift = mod_ref[0, 0:1, :]
        scale = mod_ref[0, 1:2, :]
        nw = nw_ref[...]
        r = lax.rsqrt(jnp.mean(xv * xv, axis=-1, keepdims=True) + EPS)
        xn = xv * r
        yv = xn * nw
        u = yv * (1.0 + scale) + shift
        dkv = jnp.concatenate([dk_ref[0, 0], dk_ref[0, 1]], axis=-1)
        dpk, dknw = _rms_heads_bwd(dkv, p_ref[0][:, 0:KVW], knw_ref[...], bd_ref[...])
        dp = jnp.concatenate([dpk.astype(BF16), dv_ref[0, 0].astype(BF16), dv_ref[0, 1].astype(BF16)], axis=-1)
        gw_ref[...] += _dot_tn(dp, u.astype(BF16))
        du = _dot(dp, w_ref[...])
        rows_ref[0:1, :] += jnp.sum(du, axis=0, keepdims=True)
        rows_ref[1:2, :] += jnp.sum(du * yv, axis=0, keepdims=True)
        rows_ref[2:3, :] += jnp.sum(du * (1.0 + scale) * xn, axis=0, keepdims=True)
        dknw_ref[...] += jnp.sum(dknw, axis=0, keepdims=True)

    def const(shape):
        return pl.BlockSpec(shape, lambda b: (0,) * len(shape))

    def tile(w):
        return pl.BlockSpec((1, cl, w), lambda b: (b, 0, 0))

    ctx_block = (dk_c.shape[2] - cl) // cl
    kv_tile = pl.BlockSpec((1, KVW // HD, cl, HD), lambda b: (b, 0, ctx_block, 0))
    return pl.pallas_call(
        body, name="ctx_bwd", grid=(bl,),
        in_specs=[tile(D), const((1, 3, D)), const((1, D)), _KV_ROWS_OF_W_IN_T, tile(2 * KVW), kv_tile, kv_tile,
                  const((1, KVW)), const((KVW, KVW))],
        out_specs=[const((2 * KVW, D)), const((8, D)), const((1, KVW))],
        out_shape=[jax.ShapeDtypeStruct((2 * KVW, D), F32), jax.ShapeDtypeStruct((8, D), F32),
                   jax.ShapeDtypeStruct((1, KVW), F32)],
        compiler_params=_params(("arbitrary",)),
    )(ctx, modc, norm_w, w_kv_b, pkv_c, dk_c, dv_c, knw_t, ones_bd)


def _bwd_in(x, modrows, norm_w, w_in_b, cos, sins, qnw_t, knw_t, ones_bd,
            pq, pkv, dq, dk, dv, dza, dglu, dzc, dh, gw_kv):
    bl, s, _ = x.shape
    tm = TOKEN_PARTS * TM
    nt = s // tm

    def body(x_ref, mod_ref, nw_ref, win_hbm, cos_ref, sin_ref, qnw_ref, knw_ref, bd_ref,
             pq_ref, pkv_ref, dq_ref, dk_ref, dv_ref, dza_ref, dglu_ref, dzc_ref, dh_ref, gwkv_ref,
             gx_ref, gw_hbm, dmod_ref, dnw_ref, dqnw_ref, dknw_ref, win_ref, gw_acc, sem):
        b, i = pl.program_id(0), pl.program_id(1)

        @pl.when((b == 0) & (i == 0))
        def _():
            cp = pltpu.make_async_copy(win_hbm, win_ref, sem)
            cp.start()
            gw_acc[...] = jnp.zeros_like(gw_acc)
            dnw_ref[...] = jnp.zeros_like(dnw_ref)
            dqnw_ref[...] = jnp.zeros_like(dqnw_ref)
            dknw_ref[...] = jnp.zeros_like(dknw_ref)
            cp.wait()

        @pl.when(i == 0)
        def _():
            dmod_ref[...] = jnp.zeros_like(dmod_ref)

        bd = bd_ref[...]
        shift = mod_ref[0, 0:1, :]
        scale = mod_ref[0, 1:2, :]
        nw = nw_ref[...]
        dps, us = [], []
        for part in range(TOKEN_PARTS):
            rows = pl.ds(part * TM, TM)
            ck = cos_ref[rows, :]
            sk = sin_ref[rows, :]
            cs = jnp.concatenate([ck] * (DA // KVW), axis=-1)
            sn = jnp.concatenate([sk] * (DA // KVW), axis=-1)
            dqn = _rope_bwd(dq_ref[0, rows, :], cs, sn)
            dpq, dqnw = _rms_heads_bwd(dqn, pq_ref[0, rows, :], qnw_ref[...], bd)
            dkn = _rope_bwd(jnp.concatenate([dk_ref[0, 0, rows, :], dk_ref[0, 1, rows, :]], axis=-1), ck, sk)
            dpk, dknw = _rms_heads_bwd(dkn, pkv_ref[0, rows, 0:KVW], knw_ref[...], bd[0:KVW, 0:KVW])
            dqnw_ref[...] += jnp.sum(dqnw, axis=0, keepdims=True)
            dknw_ref[...] += jnp.sum(dknw, axis=0, keepdims=True)
            dp = jnp.concatenate(
                [dpq.astype(BF16), dpk.astype(BF16), dv_ref[0, 0, rows, :].astype(BF16), dv_ref[0, 1, rows, :].astype(BF16),
                 dza_ref[0, rows, :], dglu_ref[0, rows, :], dzc_ref[0, rows, :]], axis=-1)

            xv = x_ref[0, rows, :]
            r = lax.rsqrt(jnp.mean(xv * xv, axis=-1, keepdims=True) + EPS)
            xn = xv * r
            yv = xn * nw
            u = yv * (1.0 + scale) + shift
            dps.append(dp)
            us.append(u.astype(BF16))
            du = _dot(dp, win_ref[...])
            dmod_ref[0, 0:1, :] += jnp.sum(du, axis=0, keepdims=True)
            dmod_ref[0, 1:2, :] += jnp.sum(du * yv, axis=0, keepdims=True)
            dy = du * (1.0 + scale)
            dnw_ref[...] += jnp.sum(dy * xn, axis=0, keepdims=True)
            dxn = dy * nw
            gx_ref[0, rows, :] = dh_ref[0, rows, :] + r * (dxn - xn * jnp.mean(dxn * xn, axis=-1, keepdims=True))
        gw_acc[...] += _dot_tn(jnp.concatenate(dps, axis=0), jnp.concatenate(us, axis=0))

        @pl.when((b == bl - 1) & (i == nt - 1))
        def _():
            gw_acc[DA:DA + 2 * KVW, :] += gwkv_ref[...]

            def to_bf16(j, carry):
                rows = pl.ds(pl.multiple_of(j * 2 * KVW, 2 * KVW), 2 * KVW)
                win_ref[rows, :] = gw_acc[rows, :].astype(BF16)
                return carry

            lax.fori_loop(0, D_IN // (2 * KVW), to_bf16, 0)
            pltpu.sync_copy(win_ref, gw_hbm)

    def tile(w):
        return pl.BlockSpec((1, tm, w), lambda b, i: (b, i, 0))

    def const(shape):
        return pl.BlockSpec(shape, lambda b, i: (0,) * len(shape))

    anyspace = pl.BlockSpec(memory_space=pl.ANY)
    rope = pl.BlockSpec((tm, KVW), lambda b, i: (i, 0))
    kv_tile = pl.BlockSpec((1, KVW // HD, tm, HD), lambda b, i: (b, 0, i, 0))
    return pl.pallas_call(
        body, name="bwd_in", grid=(bl, nt),
        in_specs=[tile(D), pl.BlockSpec((1, 3, D), lambda b, i: (b, 0, 0)), const((1, D)), anyspace, rope, rope,
                  const((1, DA)), const((1, KVW)), const((DA, DA)),
                  tile(DA), tile(2 * KVW), tile(DA), kv_tile, kv_tile, tile(DA), tile(2 * DC), tile(DC), tile(D),
                  const((2 * KVW, D))],
        out_specs=[tile(D), anyspace, pl.BlockSpec((1, 2, D), lambda b, i: (b, 0, 0)), const((1, D)),
                   const((1, DA)), const((1, KVW))],
        out_shape=[jax.ShapeDtypeStruct((bl, s, D), F32), jax.ShapeDtypeStruct((D_IN, D), BF16),
                   jax.ShapeDtypeStruct((bl, 2, D), F32), jax.ShapeDtypeStruct((1, D), F32),
                   jax.ShapeDtypeStruct((1, DA), F32), jax.ShapeDtypeStruct((1, KVW), F32)],
        scratch_shapes=[pltpu.VMEM((D_IN, D), BF16), pltpu.VMEM((D_IN, D), F32), pltpu.SemaphoreType.DMA],
        compiler_params=_params(("arbitrary", "arbitrary")),
    )(x, modrows, norm_w, w_in_b, cos, sins, qnw_t, knw_t, ones_bd,
      pq, pkv, dq, dk, dv, dza, dglu, dzc, dh, gw_kv)


_LOSS, _DMODC, _NW, _QN, _KN, _CB, _LW, _LB, _BPW, SMALL_W = 0, 1024, 4096, 5120, 5248, 5376, 5888, 6400, 6912, 7424


ROW_W = 1792


def _put_flat(ref, off, value):
    n, done = value.shape[1], 0
    while done < n:
        r, c = divmod(off + done, ROW_W)
        take = min(n - done, ROW_W - c)
        ref[r:r + 1, c:c + take] = value[:, done:done + take]
        done += take


def _get_flat(arr, off, n):
    parts, done = [], 0
    while done < n:
        r, c = divmod(off + done, ROW_W)
        take = min(n - done, ROW_W - c)
        parts.append(arr[r:r + 1, c:c + take])
        done += take
    return parts[0] if len(parts) == 1 else jnp.concatenate(parts, axis=-1)


def _pack_small_body(loss_ref, ctx_ref, dnw_ref, dqnw_ref, dknw_ref, dknwc_ref, conv_ref, dss_ref, dgate_ref, o_ref):
    bl = dss_ref.shape[0]
    assert SMALL_W + bl * 3 * D <= 8 * ROW_W
    o_ref[...] = jnp.zeros_like(o_ref)
    _put_flat(o_ref, _LOSS, loss_ref[...])
    _put_flat(o_ref, _DMODC, ctx_ref[0:1, :])
    _put_flat(o_ref, _DMODC + D, ctx_ref[1:2, :])
    _put_flat(o_ref, _NW, dnw_ref[...] + ctx_ref[2:3, :])
    dq = dqnw_ref[...]
    qn = dq[:, 0:HD]
    for h in range(1, DA // HD):
        qn = qn + dq[:, h * HD:(h + 1) * HD]
    _put_flat(o_ref, _QN, qn)
    dk = dknw_ref[...] + dknwc_ref[...]
    _put_flat(o_ref, _KN, dk[:, 0:HD] + dk[:, HD:2 * HD])
    _put_flat(o_ref, _BPW, conv_ref[0:1, :])
    _put_flat(o_ref, _LW, conv_ref[1:2, :])
    _put_flat(o_ref, _LB, conv_ref[2:3, :])
    _put_flat(o_ref, _CB, conv_ref[3:4, :])
    for b in range(bl):
        _put_flat(o_ref, SMALL_W + b * 3 * D, dss_ref[b, 0:1, :])
        _put_flat(o_ref, SMALL_W + b * 3 * D + D, dss_ref[b, 1:2, :])
        _put_flat(o_ref, SMALL_W + b * 3 * D + 2 * D, dgate_ref[b])


_SMALL = (("b_mod", None), ("norm_w", _NW), ("q_norm_w", _QN), ("k_norm_w", _KN), ("conv_b", _CB),
          ("conv_ln_w", _LW), ("conv_ln_b", _LB), ("b_pw", _BPW), ("c_ctx", None))


def _epilogue(parts_in, pieces, c_rows, w_mod_loc):
    bl = pieces[7].shape[0]
    n_ex = N_DEV * bl
    n_mod = w_mod_loc.shape[1]
    rb = 32
    shp = parts_in.shape[1:]
    rows_in = shp[0]

    def body(*refs):
        it = iter(refs)
        take = lambda k: [next(it) for _ in range(k)]
        (parts,) = take(1)
        piece_refs = take(9)
        (c_ref, wm_ref) = take(2)
        (g_in, g_wm, sum_ref, gb_ref, gc_all, loss_ref) = take(6)
        (mine, got_sib, stage, got_chip, payload, gathered, dmod_full, gc_mine) = take(8)
        (d2d_send, d2d_recv, ici_send, ici_recv, local_sems, sg_send, sg_recv, gc_send, gc_recv, misc_sems) = take(10)

        x, y, c = _coords()
        me = _lin(x, y, c)
        sib = (x, y, 1 - c)
        home = 2 * x + y

        def rows_loop(fn):
            def step(i, carry):
                fn(pl.ds(pl.multiple_of(i * rb, rb), rb))
                return carry
            lax.fori_loop(0, rows_in // rb, step, 0)

        def direct_gather(src, dst, send_sems, recv_sems, local_sem):
            cps = [pltpu.make_async_copy(src, dst.at[me], local_sem)]
            for k in range(1, N_DEV):
                peer = (1 - x if k & 4 else x, 1 - y if k & 2 else y, 1 - c if k & 1 else c)
                cps.append(pltpu.make_async_remote_copy(
                    src_ref=src, dst_ref=dst.at[me], send_sem=send_sems.at[k - 1], recv_sem=recv_sems.at[k - 1],
                    device_id=peer, device_id_type=MESH_ID))
            for cp in cps:
                cp.start()
            return cps

        _pack_small_body(*piece_refs, payload)
        small_cps = direct_gather(payload, gathered, sg_send, sg_recv, misc_sems.at[0])

        local, d2d, ici = [], [], []
        for s in range(4):
            cp = pltpu.make_async_copy(parts.at[_lin(s // 2, s % 2, c)], mine.at[s], local_sems.at[s])
            cp.start()
            local.append(cp)
            rc = pltpu.make_async_remote_copy(
                src_ref=parts.at[_lin(s // 2, s % 2, 1 - c)], dst_ref=got_sib.at[s],
                send_sem=d2d_send.at[s], recv_sem=d2d_recv.at[s], device_id=sib, device_id_type=MESH_ID)
            rc.start()
            d2d.append(rc)

        for cp in small_cps[1:]:
            cp.wait_recv()
        small_cps[0].wait()
        tot = gathered[0]
        for j in range(1, N_DEV):
            tot = tot + gathered[j]
        summed = _get_flat(tot, 0, SMALL_W)
        dmod_full[...] = jnp.zeros_like(dmod_full)
        for j in range(N_DEV):
            arr = gathered[j]
            for b in range(bl):
                dmod_full[j * bl + b:j * bl + b + 1, :] = _get_flat(arr, SMALL_W + b * 3 * D, 3 * D)
        dmod_full[n_ex:n_ex + 1, :] = summed[:, _DMODC:_DMODC + 3 * D]
        sum_ref[...] = summed
        gb_ref[...] = jnp.sum(dmod_full[...], axis=0, keepdims=True)
        loss_ref[...] = (0.5 / D) * jnp.sum(summed[:, _LOSS:_LOSS + D], axis=-1, keepdims=True)

        north = c == 1
        first = (jnp.where(north, 1 - x, x), jnp.where(north, y, 1 - y))
        second = (jnp.where(north, x, 1 - x), jnp.where(north, 1 - y, y))
        for s in range(4):
            local[s].wait()
            d2d[s].wait_recv()

        def chip_sum(k, chip, relayed):
            slot = 2 * chip[0] + chip[1]

            def pair_sum(rs):
                acc = mine[slot, rs, :].astype(F32) + got_sib[slot, rs, :].astype(F32)
                if relayed:
                    acc = acc + got_chip[1, rs, :].astype(F32)
                stage[k, rs, :] = acc.astype(BF16)

            rows_loop(pair_sum)

        def send(k, to):
            rc = pltpu.make_async_remote_copy(
                src_ref=stage.at[k], dst_ref=got_chip.at[k], send_sem=ici_send.at[k], recv_sem=ici_recv.at[k],
                device_id=(to[0], to[1], c), device_id_type=MESH_ID)
            rc.start()
            ici.append(rc)

        chip_sum(0, first, False)
        send(0, first)
        chip_sum(1, (1 - x, 1 - y), False)
        send(1, first)

        cr = c_ref[...]
        act = (cr * _sigmoid(cr)).astype(BF16)
        dm = dmod_full[:, pl.ds(pl.multiple_of(me * n_mod, 128), n_mod)].astype(BF16)
        g_wm[...] = _dot_tn(act, dm)
        gc_mine[...] = _dot_nt(dm[n_ex:n_ex + 8, :], wm_ref[...].astype(BF16))
        gc_cps = direct_gather(gc_mine, gc_all, gc_send, gc_recv, misc_sems.at[1])

        ici[1].wait_recv()
        chip_sum(2, second, True)
        send(2, second)
        ici[0].wait_recv()
        ici[2].wait_recv()

        def finish(rs):
            gsum = mine[home, rs, :].astype(F32) + got_sib[home, rs, :].astype(F32)
            g_in[rs, :] = gsum + got_chip[0, rs, :].astype(F32) + got_chip[2, rs, :].astype(F32)

        rows_loop(finish)

        for cp in gc_cps[1:]:
            cp.wait_recv()
        gc_cps[0].wait()
        for rc in d2d + ici + small_cps[1:] + gc_cps[1:]:
            rc.wait_send()

    vm = pl.BlockSpec(memory_space=pltpu.VMEM)
    anyspace = pl.BlockSpec(memory_space=pl.ANY)
    assert rows_in % rb == 0 and parts_in.dtype == BF16
    args = [parts_in, *pieces, c_rows, w_mod_loc]
    in_specs = [anyspace] + [vm] * (len(args) - 1)
    out_shape = [jax.ShapeDtypeStruct(shp, F32), jax.ShapeDtypeStruct(w_mod_loc.shape, F32),
                 jax.ShapeDtypeStruct((1, SMALL_W), F32), jax.ShapeDtypeStruct((1, 3 * D), F32),
                 jax.ShapeDtypeStruct((N_DEV, 8, D), F32), jax.ShapeDtypeStruct((1, 1), F32)]
    scratch = [pltpu.VMEM((4,) + shp, BF16), pltpu.VMEM((4,) + shp, BF16), pltpu.VMEM((3,) + shp, BF16),
               pltpu.VMEM((3,) + shp, BF16), pltpu.VMEM((8, ROW_W), F32), pltpu.VMEM((N_DEV, 8, ROW_W), F32),
               pltpu.VMEM((n_ex + 8, 3 * D), F32), pltpu.VMEM((8, D), F32),
               pltpu.SemaphoreType.DMA((4,)), pltpu.SemaphoreType.DMA((4,)), pltpu.SemaphoreType.DMA((3,)),
               pltpu.SemaphoreType.DMA((3,)), pltpu.SemaphoreType.DMA((4,)),
               pltpu.SemaphoreType.DMA((N_DEV - 1,)), pltpu.SemaphoreType.DMA((N_DEV - 1,)),
               pltpu.SemaphoreType.DMA((N_DEV - 1,)), pltpu.SemaphoreType.DMA((N_DEV - 1,)),
               pltpu.SemaphoreType.DMA((2,))]
    return pl.pallas_call(
        body, name="epilogue", out_shape=out_shape, in_specs=in_specs, out_specs=[vm] * len(out_shape),
        scratch_shapes=scratch, compiler_params=pltpu.CompilerParams(vmem_limit_bytes=VMEM_LIMIT),
    )(*args)


def _final_adamw(tiled, streamed_sums, local_sums, summed, g_bmod, gc_all, small_w, small_m, small_v):
    ns = len(_SMALL)
    n_tiled, n_sums, n_local = len(tiled), len(streamed_sums), len(local_sums)
    streams = list(tiled) + list(streamed_sums)
    n_results = [4] * len(streams)
    tile_jobs = [[(a, t * (w.shape[0] // ADAM_STEPS), w.shape[0] // ADAM_STEPS) for a, (_, w, _, _) in enumerate(tiled)]
                 for t in range(ADAM_STEPS)]
    sum_jobs = [(n_tiled + a, 0, w.shape[0]) for a, (_, w, _, _) in enumerate(streamed_sums)]
    jobs = sum(tile_jobs[:ADAM_STEPS // 2], []) + sum_jobs + sum(tile_jobs[ADAM_STEPS // 2:], [])
    for _, w, _, _ in tiled:
        assert w.shape[0] % (8 * ADAM_STEPS) == 0
    for got, w, _, _ in list(streamed_sums) + list(local_sums):
        assert got.shape[0] == N_DEV and got.shape[1] >= w.shape[0] and got.shape[2] == w.shape[-1]
    n_reads = 4 * len(jobs)
    n_writes = sum(n_results[a] for a, _, _ in jobs)

    def body(*refs):
        it = iter(refs)
        take = lambda k: [next(it) for _ in range(k)]
        src = [take(4) for _ in streams]
        loc = [take(4) for _ in range(n_local)]
        sum_ref, gb_ref, gc_ref = take(3)
        sw, sm, sv = take(ns), take(ns), take(ns)
        dst = [take(k) for k in n_results]
        loc_out = [take(4) for _ in range(n_local)]
        souts = take(4 * ns)
        src_buf = [take(4) for _ in streams]
        dst_buf = [take(k) for k in n_results]
        read_sem, write_sem = take(2)

        def rows_of(ref, r0, nr):
            return ref.at[:, pl.ds(r0, nr), :] if len(ref.shape) == 3 else ref.at[pl.ds(r0, nr), :]

        reads = [[pltpu.make_async_copy(rows_of(src[a][k], r0, nr), rows_of(src_buf[a][k], r0, nr),
                                        read_sem.at[4 * j + k]) for k in range(4)]
                 for j, (a, r0, nr) in enumerate(jobs)]
        for job_reads in reads:
            for cp in job_reads:
                cp.start()

        def summed_shares(got, n_rows):
            g = got[0].astype(F32)
            for d in range(1, N_DEV):
                g = g + got[d].astype(F32)
            return g[0:n_rows, :]

        for (got, w_r, m_r, v_r), outs4 in zip(loc, loc_out):
            g = summed_shares(got, w_r.shape[0])
            if len(w_r.shape) == 3:
                for r in range(w_r.shape[0]):
                    g_row = g[r:r + 1, :]
                    for o_r, val in zip(outs4, (g_row,) + _adamw(w_r[r], g_row, m_r[r], v_r[r])):
                        o_r[r] = val
            else:
                for o_r, val in zip(outs4, (g,) + _adamw(w_r[...], g, m_r[...], v_r[...])):
                    o_r[...] = val
        for k, (name, off) in enumerate(_SMALL):
            w = sw[k][...]
            if name == "b_mod":
                gk = gb_ref[...]
            elif name == "c_ctx":
                acc = gc_ref[0, 0:1, :]
                for j in range(1, N_DEV):
                    acc = acc + gc_ref[j, 0:1, :]
                sg = _sigmoid(w)
                gk = acc * (sg * (1.0 + w * (1.0 - sg)))
            else:
                gk = sum_ref[:, off:off + w.shape[1]]
            dl, m_new, v_new = _adamw(w, gk, sm[k][...], sv[k][...])
            souts[k][...] = gk
            souts[ns + k][...] = dl
            souts[2 * ns + k][...] = m_new
            souts[3 * ns + k][...] = v_new

        writes, n_started = [], 0
        for j, (a, r0, nr) in enumerate(jobs):
            for cp in reads[j]:
                cp.wait()
            rows = pl.ds(r0, nr)
            g_b, w_b, m_b, v_b = src_buf[a]
            if a < n_tiled:
                g = g_b[rows, :]
                vals = (g,) + _adamw(w_b[rows, :], g, m_b[rows, :], v_b[rows, :])
            else:
                g = summed_shares(g_b, nr)
                vals = (g,) + _adamw(w_b[...], g, m_b[...], v_b[...])
            for k, val in enumerate(vals):
                dst_buf[a][k][rows, :] = val
                cp = pltpu.make_async_copy(rows_of(dst_buf[a][k], r0, nr), rows_of(dst[a][k], r0, nr),
                                           write_sem.at[n_started])
                cp.start()
                writes.append(cp)
                n_started += 1
        for cp in writes:
            cp.wait()

    vm = pl.BlockSpec(memory_space=pltpu.VMEM)
    anyspace = pl.BlockSpec(memory_space=pl.ANY)
    args, out_shape, scratch = [], [], []
    for item in streams:
        args += list(item)
        scratch += [pltpu.VMEM(a.shape, a.dtype) for a in item]
    for item in local_sums:
        args += list(item)
    args += [summed, g_bmod, gc_all, *small_w, *small_m, *small_v]
    for (_, w, _, _), k in zip(streams, n_results):
        out_shape += [jax.ShapeDtypeStruct(w.shape, F32)] * k
    scratch += [pltpu.VMEM(s.shape, F32) for s in out_shape]
    n_streamed_out = len(out_shape)
    for _, w, _, _ in local_sums:
        out_shape += [jax.ShapeDtypeStruct(w.shape, F32)] * 4
    out_shape += [jax.ShapeDtypeStruct(w.shape, F32) for w in small_w] * 4
    scratch += [pltpu.SemaphoreType.DMA((n_reads,)), pltpu.SemaphoreType.DMA((n_writes,))]
    outs = pl.pallas_call(
        body, name="final_adamw", out_shape=out_shape,
        in_specs=[anyspace] * (4 * len(streams)) + [vm] * (len(args) - 4 * len(streams)),
        out_specs=[anyspace] * n_streamed_out + [vm] * (len(out_shape) - n_streamed_out),
        scratch_shapes=scratch, compiler_params=pltpu.CompilerParams(vmem_limit_bytes=VMEM_LIMIT),
    )(*args)
    it = iter(outs)
    take = lambda k: tuple(next(it) for _ in range(k))
    r_tiled = [take(4) for _ in tiled]
    r_sums = [take(4) for _ in streamed_sums]
    r_local = [take(4) for _ in local_sums]
    small_outs = [list(take(ns)) for _ in range(4)]
    return r_tiled, r_sums, r_local, small_outs


def _rope_tables(s):
    t = np.arange(s, dtype=np.int32)
    row = (t // GRID_W).astype(np.float32)
    col = (t % GRID_W).astype(np.float32)
    freqs = (np.float32(ROPE_THETA) ** (-np.arange(0, HD // 2, 2, dtype=np.float32) / np.float32(HD // 2))).astype(np.float32)
    ang_r = row[:, None] * freqs[None, :]
    ang_c = col[:, None] * freqs[None, :]
    cr, sr, cc, sc = np.cos(ang_r), np.sin(ang_r), np.cos(ang_c), np.sin(ang_c)
    cos = np.concatenate([cr, cr, cc, cc], axis=-1)
    sins = np.concatenate([-sr, sr, -sc, sc], axis=-1)
    return (jnp.asarray(np.tile(cos, (1, KVW // HD)), dtype=F32),
            jnp.asarray(np.tile(sins, (1, KVW // HD)), dtype=F32))


def kernel(x, c, ctx, c_ctx, w_mod, b_mod, norm_w, w_in, q_norm_w, k_norm_w, conv_w, conv_b, conv_ln_w, conv_ln_b, w_pw, b_pw, w_out, loss_target, m_c_ctx, m_w_mod, m_b_mod, m_norm_w, m_w_in, m_q_norm_w, m_k_norm_w, m_conv_w, m_conv_b, m_conv_ln_w, m_conv_ln_b, m_w_pw, m_b_pw, m_w_out, v_c_ctx, v_w_mod, v_b_mod, v_norm_w, v_w_in, v_q_norm_w, v_k_norm_w, v_conv_w, v_conv_b, v_conv_ln_w, v_conv_ln_b, v_w_pw, v_b_pw, v_w_out):
    bl, s, _ = x.shape
    cl = ctx.shape[1]
    me = _lin(*_coords())

    conv_w_pad = jnp.pad(conv_w[0], ((0, 32 - KW), (0, 0)))
    n_ex = N_DEV * bl
    g_win, c_rows, g_mod = _prologue(w_in[0].T, c, c_ctx[None, :], w_mod[0], b_mod)
    w_in_b = g_win.reshape(D_IN, D)
    mod_all = g_mod.transpose(1, 0, 2).reshape(n_ex + 8, 3 * D)
    modrows = lax.dynamic_slice_in_dim(mod_all, me * bl, bl, axis=0).reshape(bl, 3, D)
    modc = mod_all[n_ex].reshape(1, 3, D)

    cos, sins = _rope_tables(s)
    qnw_t = jnp.tile(q_norm_w, (1, DA // HD))
    knw_t = jnp.tile(k_norm_w, (1, KVW // HD))
    lane = jnp.arange(DA, dtype=jnp.int32) // HD
    ones_bd = (lane[:, None] == lane[None, :]).astype(BF16)
    ones_kv = ones_bd[0:KVW, 0:KVW]
    w_kv_b = w_in_b

    k_ctx, v_ctx, pkv_c = _ctx_fwd(ctx, modc, norm_w, w_kv_b, knw_t, ones_kv, cl + s)
    (q_h, k_h, v_h, pq, pkv, za, glu, zc), (g_wout, g_wpw, g_cw) = _fwd_in(
        x, modrows, norm_w, w_in_b, cos, sins, qnw_t, knw_t, ones_bd, k_ctx, v_ctx,
        [w_out[0], w_pw[0], conv_w_pad], [BF16, BF16, F32])
    w_out_b = g_wout.reshape(D, D)
    w_pw_b = g_wpw.reshape(DC, DC)
    conv_w_full = g_cw.transpose(1, 0, 2).reshape(32, DC)
    attn, lse = _attn_fwd(q_h, k_h, v_h)
    y_conv, cp = _conv_fwd(glu, conv_w_full, conv_b, conv_ln_w, conv_ln_b, w_pw_b, b_pw)

    do_h, dza, dy_conv, dzc, dh, dgate, gw_out, loss_row, gw_pw, conv_rows = _out_fwd_bwd(
        attn, za, cp, zc, x, loss_target, modrows, w_out_b, y_conv, conv_ln_w, conv_ln_b, w_pw_b)
    dglu, parts_cw = _conv_bwd_depthwise(glu, dy_conv, conv_w_full)
    parts_out = gw_out.reshape(N_DEV, D // N_DEV, D)
    parts_pw = gw_pw.reshape(N_DEV, DC // N_DEV, DC)
    (dq, dk_h, dv_h), (got_out, got_pw, got_cw) = _attn_bwd(
        q_h, k_h, v_h, do_h, attn, lse, [parts_out, parts_pw, parts_cw])
    gw_kv, ctx_rows, dknw_c = _ctx_bwd(ctx, modc, norm_w, w_kv_b, pkv_c, dk_h, dv_h, knw_t, ones_kv)
    grad_x, gw_in, dmod_ss, dnw, dqnw, dknw = _bwd_in(
        x, modrows, norm_w, w_in_b, cos, sins, qnw_t, knw_t, ones_bd,
        pq, pkv, dq, dk_h, dv_h, dza, dglu, dzc, dh, gw_kv)

    given = {"c_ctx": (c_ctx, m_c_ctx, v_c_ctx), "b_mod": (b_mod, m_b_mod, v_b_mod), "norm_w": (norm_w, m_norm_w, v_norm_w),
             "q_norm_w": (q_norm_w, m_q_norm_w, v_q_norm_w), "k_norm_w": (k_norm_w, m_k_norm_w, v_k_norm_w),
             "conv_b": (conv_b, m_conv_b, v_conv_b), "conv_ln_w": (conv_ln_w, m_conv_ln_w, v_conv_ln_w),
             "conv_ln_b": (conv_ln_b, m_conv_ln_b, v_conv_ln_b), "b_pw": (b_pw, m_b_pw, v_b_pw)}
    as_rows = [[given[name][which].reshape(1, -1) for name, _ in _SMALL] for which in range(3)]
    g_in_t, g_wmod, summed, g_bmod, gc_all, loss11 = _epilogue(
        gw_in.reshape(N_DEV, D_IN // N_DEV, D),
        [loss_row, ctx_rows, dnw, dqnw, dknw, dknw_c, conv_rows, dmod_ss, dgate], c_rows, w_mod[0])
    (r_in, r_wmod), (r_out,), (r_pw, r_cw), small_outs = _final_adamw(
        [(g_in_t, w_in[0].T, m_w_in[0].T, v_w_in[0].T), (g_wmod, w_mod[0], m_w_mod[0], v_w_mod[0])],
        [(got_out, w_out[0], m_w_out[0], v_w_out[0])],
        [(got_pw, w_pw[0], m_w_pw[0], v_w_pw[0]),
         (got_cw, conv_w.transpose(1, 0, 2), m_conv_w.transpose(1, 0, 2), v_conv_w.transpose(1, 0, 2))],
        summed, g_bmod, gc_all, *as_rows)
    r_in = tuple(a.T for a in r_in)
    r_cw = tuple(a.transpose(1, 0, 2)[0] for a in r_cw)

    big = {"w_mod": r_wmod, "w_in": r_in, "conv_w": r_cw, "w_pw": r_pw, "w_out": r_out}
    order = ["c_ctx", "w_mod", "b_mod", "norm_w", "w_in", "q_norm_w", "k_norm_w", "conv_w", "conv_b", "conv_ln_w",
             "conv_ln_b", "w_pw", "b_pw", "w_out"]
    small_index = {name: k for k, (name, _) in enumerate(_SMALL)}
    outs = [loss11.reshape(()), grad_x]
    for which in range(4):
        for name in order:
            if name in big:
                outs.append(big[name][which][None])
            else:
                outs.append(small_outs[which][small_index[name]].reshape(given[name][0].shape))
    return tuple(outs)
```

```python
import jax
import jax.numpy as jnp
import numpy as np
from jax import lax
from jax.experimental import pallas as pl
from jax.experimental.pallas import tpu as pltpu

F32, BF16 = jnp.float32, jnp.bfloat16
MESH_ID = pl.DeviceIdType.MESH

N_DEV = 8
D = 1024
D_IN = 2816
DA = 512
DC = 512
HD = 64
KVW = 128
KW = 31
HALO = 16
EPS = 1e-6
ROPE_THETA = 10000.0
GRID_W = 64

ADAM_LR, ADAM_B1, ADAM_B2, ADAM_EPS, ADAM_WD, ADAM_STEP = 0.001, 0.9, 0.999, 1e-08, 0.01, 10

VMEM_LIMIT = 56 * 1024 * 1024

TM = 256
TQ = 128
TOKEN_PARTS = 2
OUT_TM = 512
BWD_PARTS = 4
FWD_PARTS = 8
TC = 512
CH = 32
ADAM_STEPS = 4


def _params(sem, vmem=VMEM_LIMIT):
    return pltpu.CompilerParams(dimension_semantics=sem, vmem_limit_bytes=vmem)


def _dot(a, b):
    return jnp.dot(a, b, preferred_element_type=F32)


def _dot_nt(a, b):
    return lax.dot_general(a, b, (((1,), (1,)), ((), ())), preferred_element_type=F32)


def _dot_tn(a, b):
    return lax.dot_general(a, b, (((0,), (0,)), ((), ())), preferred_element_type=F32)


def _sigmoid(z):
    return 1.0 / (1.0 + jnp.exp(-z))


def _segsum(v, ones_bd):
    return _dot(v.astype(BF16), ones_bd)


def _swap16(x):
    w = x.shape[-1]
    lane = lax.broadcasted_iota(jnp.int32, x.shape, 1)
    return jnp.where((lane % 32) < 16, pltpu.roll(x, w - 16, 1), pltpu.roll(x, 16, 1))


def _with_ones_column(v):
    one = (lax.broadcasted_iota(jnp.int32, v.shape, 1) == 0).astype(v.dtype)
    return jnp.concatenate([v, one], axis=-1)


def _rope(x, cos, sins):
    return x * cos + _swap16(x) * sins


def _rope_bwd(d, cos, sins):
    return d * cos + _swap16(d * sins)


def _adamw(w, g, m, v):
    m2 = ADAM_B1 * m + (1.0 - ADAM_B1) * g
    v2 = ADAM_B2 * v + (1.0 - ADAM_B2) * (g * g)
    m_hat = m2 / (1.0 - ADAM_B1 ** ADAM_STEP)
    v_hat = v2 / (1.0 - ADAM_B2 ** ADAM_STEP)
    delta = -ADAM_LR * (m_hat / (jnp.sqrt(v_hat) + ADAM_EPS) + ADAM_WD * w)
    return delta, m2, v2


def _coords():
    return lax.axis_index("x"), lax.axis_index("y"), lax.axis_index("c")


def _lin(x, y, c):
    return 4 * x + 2 * y + c


def _prologue(w_in_t, c, c_ctx_row, w_mod_loc, b_mod):
    bl = c.shape[0]
    n_ex = N_DEV * bl
    n_mod = w_mod_loc.shape[1]

    def body(w32_ref, c_in_ref, cctx_ref, wm_ref, b_ref, out_w, crows_ref, mod_out, w_ref, c_ref, c_gath, mod_mine,
             w_send, w_recv, c_send, c_recv, m_send, m_recv, local_sems):
        x, y, c = _coords()
        me_lin = _lin(x, y, c)
        c_ref[...] = jnp.zeros_like(c_ref)
        c_ref[0:bl, :] = c_in_ref[...]
        w_ref[...] = w32_ref[...].astype(BF16)
        me, sib = (x, y, c), (x, y, 1 - c)
        xnb, ynb, diag = (1 - x, y), (x, 1 - y), (1 - x, 1 - y)
        north = c == 1

        def direct_gather(src, dst, send_sems, recv_sems, local_sem):
            cps = [pltpu.make_async_copy(src, dst.at[me_lin], local_sem)]
            for k in range(1, N_DEV):
                peer = (1 - x if k & 4 else x, 1 - y if k & 2 else y, 1 - c if k & 1 else c)
                cps.append(pltpu.make_async_remote_copy(
                    src_ref=src, dst_ref=dst.at[me_lin], send_sem=send_sems.at[k - 1], recv_sem=recv_sems.at[k - 1],
                    device_id=peer, device_id_type=MESH_ID))
            for cp in cps:
                cp.start()
            return cps

        def copy(k, block, to, src=None):
            slot = out_w.at[_lin(*block)]
            return pltpu.make_async_remote_copy(
                src_ref=slot if src is None else src, dst_ref=slot, send_sem=w_send.at[k], recv_sem=w_recv.at[k],
                device_id=to, device_id_type=MESH_ID)

        c_cps = direct_gather(c_ref, c_gath, c_send, c_recv, local_sems.at[0])
        mine = pltpu.make_async_copy(w_ref, out_w.at[me_lin], local_sems.at[1])
        mine.start()
        first = [copy(0, me, sib, src=w_ref), copy(1, me, (*xnb, c), src=w_ref), copy(2, me, (*ynb, c), src=w_ref)]
        for cp in first:
            cp.start()

        for cp in c_cps[1:]:
            cp.wait_recv()
        c_cps[0].wait()
        crows_ref[...] = jnp.zeros_like(crows_ref)
        for j in range(N_DEV):
            crows_ref[j * bl:(j + 1) * bl, :] = c_gath[j, 0:bl, :]
        crows_ref[n_ex:n_ex + 1, :] = cctx_ref[...]
        cr = crows_ref[...]
        act = (cr * _sigmoid(cr)).astype(BF16)
        mod_mine[...] = _dot(act, wm_ref[...].astype(BF16)) + b_ref[:, pl.ds(pl.multiple_of(me_lin * n_mod, 128), n_mod)]
        mod_cps = direct_gather(mod_mine, mod_out, m_send, m_recv, local_sems.at[2])

        relay_north = copy(3, (*xnb, c), (*ynb, c))
        relay_south = copy(3, (*ynb, c), (*xnb, c))
        passed = []
        copy(1, (*xnb, c), me).wait_recv()
        pl.when(north)(relay_north.start)
        passed.append(copy(4, (*xnb, c), sib))
        passed[-1].start()
        copy(2, (*ynb, c), me).wait_recv()
        pl.when(jnp.logical_not(north))(relay_south.start)
        passed.append(copy(5, (*ynb, c), sib))
        passed[-1].start()
        copy(3, (*diag, c), me).wait_recv()
        passed.append(copy(6, (*diag, c), sib))
        passed[-1].start()
        copy(0, sib, me).wait_recv()
        for k, chip in ((4, xnb), (5, ynb), (6, diag)):
            copy(k, (*chip, 1 - c), me).wait_recv()
        for cp in mod_cps[1:]:
            cp.wait_recv()
        mod_cps[0].wait()
        for cp in first + passed + [relay_north] + c_cps[1:] + mod_cps[1:]:
            cp.wait_send()
        mine.wait()

    vm = pl.BlockSpec(memory_space=pltpu.VMEM)
    seven = pltpu.SemaphoreType.DMA((N_DEV - 1,))
    return pl.pallas_call(
        body, name="prologue",
        out_shape=[jax.ShapeDtypeStruct((N_DEV,) + w_in_t.shape, BF16), jax.ShapeDtypeStruct((n_ex + 8, D), F32),
                   jax.ShapeDtypeStruct((N_DEV, n_ex + 8, n_mod), F32)],
        in_specs=[vm] * 5, out_specs=[pl.BlockSpec(memory_space=pl.ANY), vm, vm],
        scratch_shapes=[pltpu.VMEM(w_in_t.shape, BF16), pltpu.VMEM((8, D), F32), pltpu.VMEM((N_DEV, 8, D), F32),
                        pltpu.VMEM((n_ex + 8, n_mod), F32),
                        seven, seven, seven, seven, seven, seven, pltpu.SemaphoreType.DMA((3,))],
        compiler_params=pltpu.CompilerParams(vmem_limit_bytes=VMEM_LIMIT),
    )(w_in_t, c, c_ctx_row, w_mod_loc, b_mod)


def _exchange_copies(in_refs, out_refs, send_sems, recv_sems, local_sems, scatter):
    x, y, c = _coords()
    me = _lin(x, y, c)
    local, remote = [], []
    for a, (src, dst) in enumerate(zip(in_refs, out_refs)):
        local.append(pltpu.make_async_copy(src.at[me] if scatter else src, dst.at[me], local_sems.at[a]))
        for k in range(1, N_DEV):
            peer = (1 - x if k & 4 else x, 1 - y if k & 2 else y, 1 - c if k & 1 else c)
            remote.append(pltpu.make_async_remote_copy(
                src_ref=src.at[_lin(*peer)] if scatter else src, dst_ref=dst.at[me],
                send_sem=send_sems.at[a * (N_DEV - 1) + k - 1], recv_sem=recv_sems.at[a * (N_DEV - 1) + k - 1],
                device_id=peer, device_id_type=MESH_ID))
    return local, remote


def _exchange_scratch(n):
    return [pltpu.SemaphoreType.DMA((n * (N_DEV - 1),)), pltpu.SemaphoreType.DMA((n * (N_DEV - 1),)),
            pltpu.SemaphoreType.DMA((n,))]


def _fwd_in(x, modrows, norm_w, w_in_b, cos, sins, qnw_t, knw_t, ones_bd, k_all, v_all, shards, wire_dtypes):
    bl, s, _ = x.shape
    tm = TOKEN_PARTS * TM
    nt = s // tm
    n_sh = len(shards)

    def body(*refs):
        (x_ref, mod_ref, nw_ref, win_ref, cos_ref, sin_ref, qnw_ref, knw_ref, bd_ref, kin_ref, vin_ref) = refs[:11]
        shard_refs = refs[11:11 + n_sh]
        q_ref, k_ref, v_ref, pq_ref, pkv_ref, za_ref, glu_ref, zc_ref = refs[11 + n_sh:19 + n_sh]
        gathered_refs = refs[19 + n_sh:19 + 2 * n_sh]
        stage_refs = refs[19 + 2 * n_sh:19 + 3 * n_sh]
        send_sems, recv_sems, local_sems = refs[19 + 3 * n_sh:]
        b, i = pl.program_id(0), pl.program_id(1)
        local, remote = _exchange_copies(stage_refs, gathered_refs, send_sems, recv_sems, local_sems, scatter=False)

        @pl.when((b == 0) & (i == 0))
        def _():
            for src, stage in zip(shard_refs, stage_refs):
                stage[...] = src[...].astype(stage.dtype)
            for cp in local + remote:
                cp.start()

        shift = mod_ref[0, 0:1, :]
        scale = mod_ref[0, 1:2, :]
        for part in range(TOKEN_PARTS):
            rows = pl.ds(part * TM, TM)
            xv = x_ref[0, rows, :]
            r = lax.rsqrt(jnp.mean(xv * xv, axis=-1, keepdims=True) + EPS)
            u = (xv * r * nw_ref[...]) * (1.0 + scale) + shift
            p = _dot_nt(u.astype(BF16), win_ref[...])
            pq = p[:, 0:DA]
            pk = p[:, DA:DA + HD * 2]
            ck = cos_ref[rows, :]
            sk = sin_ref[rows, :]
            cs = jnp.concatenate([ck] * (DA // KVW), axis=-1)
            sn = jnp.concatenate([sk] * (DA // KVW), axis=-1)
            rq = lax.rsqrt(_segsum(pq * pq, bd_ref[...]) * (1.0 / HD) + EPS)
            qn = pq * rq * qnw_ref[...]
            qr = _rope(qn, cs, sn) * 0.125
            for h in range(DA // HD):
                q_ref[0, h, rows, :] = qr[:, h * HD:(h + 1) * HD].astype(BF16)
            rk = lax.rsqrt(_segsum(pk * pk, bd_ref[0:KVW, 0:KVW]) * (1.0 / HD) + EPS)
            kn = pk * rk * knw_ref[...]
            kr = _rope(kn, ck, sk)
            pv = p[:, 640:768]
            for h in range(KVW // HD):
                k_ref[0, h, rows, :] = kr[:, h * HD:(h + 1) * HD].astype(BF16)
                v_ref[0, h, rows, :] = _with_ones_column(pv[:, h * HD:(h + 1) * HD]).astype(BF16)
            pq_ref[0, rows, :] = pq
            pkv_ref[0, rows, :] = p[:, 512:768]
            za_ref[0, rows, :] = p[:, 768:1280]
            glu_ref[0, rows, :] = p[:, 1280:2304]
            zc_ref[0, rows, :] = p[:, 2304:2816]

        @pl.when((b == bl - 1) & (i == nt - 1))
        def _():
            for cp in remote:
                cp.wait_recv()
            for cp in remote:
                cp.wait_send()
            for cp in local:
                cp.wait()

    def tile(w):
        return pl.BlockSpec((1, tm, w), lambda b, i: (b, i, 0))

    def const(shape):
        return pl.BlockSpec(shape, lambda b, i: (0,) * len(shape))

    outs = [(DA, F32), (2 * KVW, F32), (DA, F32), (2 * DC, F32), (DC, F32)]
    anyspace = pl.BlockSpec(memory_space=pl.ANY)
    rope = pl.BlockSpec((tm, KVW), lambda b, i: (i, 0))
    k_tile = pl.BlockSpec((1, KVW // HD, tm, HD), lambda b, i: (b, 0, i, 0))
    v_tile = pl.BlockSpec((1, KVW // HD, tm, 2 * HD), lambda b, i: (b, 0, i, 0))
    res = pl.pallas_call(
        body, name="fwd_in", grid=(bl, nt),
        in_specs=[tile(D), pl.BlockSpec((1, 3, D), lambda b, i: (b, 0, 0)), const((1, D)), const((D_IN, D)),
                  rope, rope, const((1, DA)), const((1, KVW)), const((DA, DA)), anyspace, anyspace]
        + [const(a.shape) for a in shards],
        out_specs=[pl.BlockSpec((1, DA // HD, tm, HD), lambda b, i: (b, 0, i, 0)), k_tile, v_tile]
        + [tile(w) for w, _ in outs] + [anyspace] * n_sh,
        out_shape=[jax.ShapeDtypeStruct((bl, DA // HD, s, HD), BF16), jax.ShapeDtypeStruct(k_all.shape, BF16),
                   jax.ShapeDtypeStruct(v_all.shape, BF16)]
        + [jax.ShapeDtypeStruct((bl, s, w), dt) for w, dt in outs]
        + [jax.ShapeDtypeStruct((N_DEV,) + a.shape, dt) for a, dt in zip(shards, wire_dtypes)],
        input_output_aliases={9: 1, 10: 2},
        scratch_shapes=[pltpu.VMEM(a.shape, dt) for a, dt in zip(shards, wire_dtypes)] + _exchange_scratch(n_sh),
        compiler_params=_params(("arbitrary", "arbitrary")),
    )(x, modrows, norm_w, w_in_b, cos, sins, qnw_t, knw_t, ones_bd, k_all, v_all, *shards)
    return res[:8], res[8:]


_KV_ROWS_OF_W_IN_T = pl.BlockSpec((2 * KVW, D), lambda b: (DA // (2 * KVW), 0))


def _ctx_fwd(ctx, modc, norm_w, w_kv_b, knw_t, ones_bd, n_keys):
    bl, cl, _ = ctx.shape

    def body(x_ref, mod_ref, nw_ref, w_ref, knw_ref, bd_ref, k_ref, v_ref, pkv_ref):
        xv = x_ref[0]
        shift = mod_ref[0, 0:1, :]
        scale = mod_ref[0, 1:2, :]
        r = lax.rsqrt(jnp.mean(xv * xv, axis=-1, keepdims=True) + EPS)
        u = (xv * r * nw_ref[...]) * (1.0 + scale) + shift
        p = _dot_nt(u.astype(BF16), w_ref[...])
        pk = p[:, 0:KVW]
        rk = lax.rsqrt(_segsum(pk * pk, bd_ref[...]) * (1.0 / HD) + EPS)
        kn = pk * rk * knw_ref[...]
        pv = p[:, KVW:2 * KVW]
        for h in range(KVW // HD):
            k_ref[0, h] = kn[:, h * HD:(h + 1) * HD].astype(BF16)
            v_ref[0, h] = _with_ones_column(pv[:, h * HD:(h + 1) * HD]).astype(BF16)
        pkv_ref[0] = p

    def const(shape):
        return pl.BlockSpec(shape, lambda b: (0,) * len(shape))

    def tile(w):
        return pl.BlockSpec((1, cl, w), lambda b: (b, 0, 0))

    ctx_block = (n_keys - cl) // cl
    assert ctx_block * cl + cl == n_keys
    k_tile = pl.BlockSpec((1, KVW // HD, cl, HD), lambda b: (b, 0, ctx_block, 0))
    v_tile = pl.BlockSpec((1, KVW // HD, cl, 2 * HD), lambda b: (b, 0, ctx_block, 0))
    return pl.pallas_call(
        body, name="ctx_fwd", grid=(bl,),
        in_specs=[tile(D), const((1, 3, D)), const((1, D)), _KV_ROWS_OF_W_IN_T, const((1, KVW)), const((KVW, KVW))],
        out_specs=[k_tile, v_tile, tile(2 * KVW)],
        out_shape=[jax.ShapeDtypeStruct((bl, KVW // HD, n_keys, HD), BF16),
                   jax.ShapeDtypeStruct((bl, KVW // HD, n_keys, 2 * HD), BF16),
                   jax.ShapeDtypeStruct((bl, cl, 2 * KVW), F32)],
        compiler_params=_params(("arbitrary",)),
    )(ctx, modc, norm_w, w_kv_b, knw_t, ones_bd)


def _attn_fwd(q, k, v1):
    bl, _, s, _ = q.shape
    n_keys = k.shape[2]

    def body(q_ref, k_ref, v_ref, o_ref, lse_ref):
        kv = k_ref[0, 0]
        vv = v_ref[0, 0]
        lane = lax.broadcasted_iota(jnp.int32, (TQ, 2 * HD), 1)
        for part in range(FWD_PARTS):
            rows = pl.ds(part * TQ, TQ)
            lse = jnp.zeros((TQ, 2 * HD), F32)
            heads = []
            sc_all = _dot_nt(q_ref[0, :, rows, :].reshape(4 * TQ, HD), kv)
            for h in range(4):
                sc = sc_all[h * TQ:(h + 1) * TQ, :]
                m = jnp.max(sc, axis=-1, keepdims=True)
                e = jnp.exp(sc - m).astype(BF16)
                ov = _dot(e, vv)
                denom = ov[:, HD:HD + 1]
                heads.append(ov[:, 0:HD] * (1.0 / denom))
                lse = jnp.where(lane == h, m + jnp.log(denom), lse)
            o_ref[0, rows, :] = jnp.concatenate(heads, axis=-1)
            lse_ref[0, 0, rows, :] = lse

    tq = FWD_PARTS * TQ
    ks = pl.BlockSpec((1, 1, n_keys, HD), lambda b, g, i: (b, g, 0, 0))
    qs = pl.BlockSpec((1, 4, tq, HD), lambda b, g, i: (b, g, i, 0))
    vs = pl.BlockSpec((1, 1, n_keys, 2 * HD), lambda b, g, i: (b, g, 0, 0))
    return pl.pallas_call(
        body, name="attn_fwd", grid=(bl, 2, s // tq), in_specs=[qs, ks, vs],
        out_specs=[pl.BlockSpec((1, tq, 4 * HD), lambda b, g, i: (b, i, g)),
                   pl.BlockSpec((1, 1, tq, 2 * HD), lambda b, g, i: (b, g, i, 0))],
        out_shape=[jax.ShapeDtypeStruct((bl, s, DA), F32), jax.ShapeDtypeStruct((bl, 2, s, 2 * HD), F32)],
        compiler_params=_params(("arbitrary", "arbitrary", "arbitrary")),
    )(q, k, v1)


def _attn_bwd(q, k, v1, do, o, lse, exchange):
    bl, _, s, _ = q.shape
    n_keys = k.shape[2]
    tq = BWD_PARTS * TQ
    nq = s // tq
    n_ex = len(exchange)

    def body(*refs):
        q_ref, k_ref, v_ref, do_ref, o_ref, lse_ref = refs[:6]
        part_refs = refs[6:6 + n_ex]
        dq_ref, dk_ref, dv_ref = refs[6 + n_ex:9 + n_ex]
        got_refs = refs[9 + n_ex:9 + 2 * n_ex]
        p_sc, ds_sc, dkt, dvt, send_sems, recv_sems, local_sems = refs[9 + 2 * n_ex:]
        i = pl.program_id(2)
        first = (pl.program_id(0) == 0) & (pl.program_id(1) == 0) & (i == 0)
        last = (pl.program_id(0) == bl - 1) & (pl.program_id(1) == 1) & (i == nq - 1)
        local, remote = _exchange_copies(part_refs, got_refs, send_sems, recv_sems, local_sems, scatter=True)

        @pl.when(first)
        def _():
            for cp in local + remote:
                cp.start()

        @pl.when(i == 0)
        def _():
            dkt[...] = jnp.zeros_like(dkt)
            dvt[...] = jnp.zeros_like(dvt)

        kv = k_ref[0, 0]
        vv = v_ref[0, 0][:, 0:HD]
        for part in range(BWD_PARTS):
            tq_rows = pl.ds(part * TQ, TQ)
            lse = lse_ref[0, 0, tq_rows, :]
            ov = o_ref[0, tq_rows, :]
            dqs = []
            q_cat = q_ref[0, :, tq_rows, :].reshape(4 * TQ, HD)
            do_cat = do_ref[0, :, tq_rows, :].reshape(4 * TQ, HD)
            sc_all = _dot_nt(q_cat, kv)
            for h in range(4):
                doh = do_cat[h * TQ:(h + 1) * TQ, :]
                delta = jnp.sum(ov[:, h * HD:(h + 1) * HD] * doh.astype(F32), axis=-1, keepdims=True)
                rows = pl.ds((part * 4 + h) * TQ, TQ)
                p = jnp.exp(sc_all[h * TQ:(h + 1) * TQ, :] - lse[:, h:h + 1])
                ds = (p * (_dot_nt(doh, vv) - delta)).astype(BF16)
                p_sc[rows, :] = p.astype(BF16)
                ds_sc[rows, :] = ds
                dqs.append(_dot(ds, kv) * 0.125)
            dq_ref[0, tq_rows, :] = jnp.concatenate(dqs, axis=-1)
            part_rows = pl.ds(part * 4 * TQ, 4 * TQ)
            dvt[...] += _dot_tn(do_cat, p_sc[part_rows, :])
            dkt[...] += _dot_tn(q_cat, ds_sc[part_rows, :])

        @pl.when(i == nq - 1)
        def _():
            dk_ref[0, 0] = dkt[...].T
            dv_ref[0, 0] = dvt[...].T

        @pl.when(last)
        def _():
            for cp in remote:
                cp.wait_recv()
            for cp in remote:
                cp.wait_send()
            for cp in local:
                cp.wait()

    qs = pl.BlockSpec((1, 4, tq, HD), lambda b, g, i: (b, g, i, 0))
    ks = pl.BlockSpec((1, 1, n_keys, HD), lambda b, g, i: (b, g, 0, 0))
    vs = pl.BlockSpec((1, 1, n_keys, 2 * HD), lambda b, g, i: (b, g, 0, 0))
    os_ = pl.BlockSpec((1, tq, 4 * HD), lambda b, g, i: (b, i, g))
    kshape = jax.ShapeDtypeStruct(k.shape, F32)
    anyspace = pl.BlockSpec(memory_space=pl.ANY)
    res = pl.pallas_call(
        body, name="attn_bwd", grid=(bl, 2, nq),
        in_specs=[qs, ks, vs, qs, os_, pl.BlockSpec((1, 1, tq, 2 * HD), lambda b, g, i: (b, g, i, 0))]
        + [anyspace] * n_ex,
        out_specs=[os_, ks, ks] + [anyspace] * n_ex,
        out_shape=[jax.ShapeDtypeStruct((bl, s, DA), F32), kshape, kshape]
        + [jax.ShapeDtypeStruct(a.shape, a.dtype) for a in exchange],
        scratch_shapes=[pltpu.VMEM((4 * tq, n_keys), BF16), pltpu.VMEM((4 * tq, n_keys), BF16),
                        pltpu.VMEM((HD, n_keys), F32), pltpu.VMEM((HD, n_keys), F32)] + _exchange_scratch(n_ex),
        compiler_params=_params(("arbitrary", "arbitrary", "arbitrary")),
    )(q, k, v1, do, o, lse, *exchange)
    return res[:3], res[3:]


def _halo_specs(width, s):
    per = TC // HALO
    last = s // HALO - 1
    main = pl.BlockSpec((1, TC, width), lambda b, i: (b, i, 0))
    prev = pl.BlockSpec((1, HALO, width), lambda b, i: (b, jnp.maximum(i * per - 1, 0), 0))
    nxt = pl.BlockSpec((1, HALO, width), lambda b, i: (b, jnp.minimum((i + 1) * per, last), 0))
    return main, prev, nxt


def _glu(g):
    return g[:, 0:DC] * _sigmoid(g[:, DC:2 * DC])


def _fill_padded(pad_ref, main, prev, nxt, first, last):
    tc = main.shape[0]
    pad_ref[0:HALO, :] = jnp.where(first, 0.0, prev)
    pad_ref[HALO:HALO + tc, :] = main
    pad_ref[HALO + tc:2 * HALO + tc, :] = jnp.where(last, 0.0, nxt)


PLANE_ROWS = TC + 2 * HALO - 8


def _shift_planes(pad_ref, planes_ref):
    for r in range(1, 8):
        planes_ref[r - 1] = pad_ref[pl.ds(r, planes_ref.shape[1]), :]


def _tap_rows(pad_ref, planes_ref, offset, start, n):
    a, r = divmod(offset, 8)
    if r == 0:
        return pad_ref[pl.ds(start + 8 * a, n), :]
    return planes_ref[r - 1, pl.ds(start + 8 * a, n), :]


def _conv_fwd(glu, conv_w, conv_b, ln_w, ln_b, w_pw_b, b_pw):
    bl, s, _ = glu.shape
    nt = s // TC

    def body(g_ref, gp_ref, gn_ref, cw_ref, cb_ref, lw_ref, lb_ref, wpw_ref, bpw_ref, y_ref, cp_ref, pad_ref, planes_ref):
        i = pl.program_id(1)
        _fill_padded(pad_ref, _glu(g_ref[0]), _glu(gp_ref[0]), _glu(gn_ref[0]), i == 0, i == nt - 1)
        _shift_planes(pad_ref, planes_ref)
        for ck in range(TC // CH):
            acc = jnp.zeros((CH, DC), F32) + cb_ref[...]
            for t in range(KW):
                acc = acc + _tap_rows(pad_ref, planes_ref, 1 + t, ck * CH, CH) * cw_ref[t:t + 1, :]
            y_ref[0, pl.ds(ck * CH, CH), :] = acc
        y = y_ref[0]
        mu = jnp.mean(y, axis=-1, keepdims=True)
        yc = y - mu
        var = jnp.mean(yc * yc, axis=-1, keepdims=True)
        z = yc * lax.rsqrt(var + EPS) * lw_ref[...] + lb_ref[...]
        act = z * _sigmoid(z)
        cp_ref[0] = _dot(act.astype(BF16), wpw_ref[...]) + bpw_ref[...]

    def const(shape):
        return pl.BlockSpec(shape, lambda b, i: (0,) * len(shape))

    main, prev, nxt = _halo_specs(2 * DC, s)
    tile = pl.BlockSpec((1, TC, DC), lambda b, i: (b, i, 0))
    return pl.pallas_call(
        body, name="conv_fwd", grid=(bl, nt),
        in_specs=[main, prev, nxt, const((32, DC)), const((1, DC)), const((1, DC)), const((1, DC)),
                  const((DC, DC)), const((1, DC))],
        out_specs=[tile, tile],
        out_shape=[jax.ShapeDtypeStruct((bl, s, DC), F32)] * 2,
        scratch_shapes=[pltpu.VMEM((TC + 2 * HALO, DC), F32), pltpu.VMEM((7, PLANE_ROWS, DC), F32)],
        compiler_params=_params(("arbitrary", "arbitrary")),
    )(glu, glu, glu, conv_w, conv_b, ln_w, ln_b, w_pw_b, b_pw)


def _conv_bwd_depthwise(glu, dy, conv_w):
    bl, s, _ = glu.shape
    nt = s // TC

    def body(g_ref, gp_ref, gn_ref, d_ref, dp_ref, dn_ref, cw_ref, dglu_ref, shares_ref,
             padu_ref, padd_ref, planes_u, planes_d, dcw_ref):
        i = pl.program_id(1)

        @pl.when((pl.program_id(0) == 0) & (i == 0))
        def _():
            dcw_ref[...] = jnp.zeros_like(dcw_ref)

        first, last = i == 0, i == nt - 1
        _fill_padded(padu_ref, _glu(g_ref[0]), _glu(gp_ref[0]), _glu(gn_ref[0]), first, last)
        _fill_padded(padd_ref, d_ref[0], dp_ref[0], dn_ref[0], first, last)
        _shift_planes(padu_ref, planes_u)
        _shift_planes(padd_ref, planes_d)
        for ck in range(TC // CH):
            acc = jnp.zeros((CH, DC), F32)
            for t in range(KW):
                acc = acc + _tap_rows(padd_ref, planes_d, 2 * HALO - 1 - t, ck * CH, CH) * cw_ref[t:t + 1, :]
            g = g_ref[0, pl.ds(ck * CH, CH), :]
            a = g[:, 0:DC]
            sg = _sigmoid(g[:, DC:2 * DC])
            dglu_ref[0, pl.ds(ck * CH, CH), 0:DC] = (acc * sg).astype(BF16)
            dglu_ref[0, pl.ds(ck * CH, CH), DC:2 * DC] = (acc * a * sg * (1.0 - sg)).astype(BF16)
        group = 4
        for t0 in range(0, KW, group):
            taps = range(t0, min(t0 + group, KW))
            acc8 = [jnp.zeros((8, DC), F32) for _ in taps]
            for ck in range(TC // CH):
                dchunk = d_ref[0, pl.ds(ck * CH, CH), :]
                for n, t in enumerate(taps):
                    prod = _tap_rows(padu_ref, planes_u, 1 + t, ck * CH, CH) * dchunk
                    acc8[n] = acc8[n] + jnp.sum(prod.reshape(CH // 8, 8, DC), axis=0)
            for n, t in enumerate(taps):
                dcw_ref[t:t + 1, :] += jnp.sum(acc8[n], axis=0, keepdims=True)

        @pl.when((pl.program_id(0) == bl - 1) & last)
        def _():
            for j in range(N_DEV):
                shares_ref[j] = dcw_ref[:, j * cpd:(j + 1) * cpd].astype(BF16)

    gmain, gprev, gnext = _halo_specs(2 * DC, s)
    dmain, dprev, dnext = _halo_specs(DC, s)
    cw = pl.BlockSpec((32, DC), lambda b, i: (0, 0))
    cpd = DC // N_DEV
    return pl.pallas_call(
        body, name="conv_bwd_depthwise", grid=(bl, nt),
        in_specs=[gmain, gprev, gnext, dmain, dprev, dnext, cw],
        out_specs=[gmain, pl.BlockSpec((N_DEV, 32, cpd), lambda b, i: (0, 0, 0))],
        out_shape=[jax.ShapeDtypeStruct((bl, s, 2 * DC), BF16), jax.ShapeDtypeStruct((N_DEV, 32, cpd), BF16)],
        scratch_shapes=[pltpu.VMEM((TC + 2 * HALO, DC), F32)] * 2 + [pltpu.VMEM((7, PLANE_ROWS, DC), F32)] * 2
        + [pltpu.VMEM((32, DC), F32)],
        compiler_params=_params(("arbitrary", "arbitrary")),
    )(glu, glu, glu, dy, dy, dy, conv_w)


def _out_fwd_bwd(attn, za, cp, zc, x, target, modrows, w_out_b, y_conv, ln_w, ln_b, w_pw_b):
    bl, s, _ = x.shape
    tm = OUT_TM

    def body(o_ref, za_ref, cp_ref, zc_ref, x_ref, t_ref, mod_ref, w_ref, y_ref, lw_ref, lb_ref, wpw_ref,
             do_ref, dza_ref, dy_ref, dzc_ref, dh_ref, dgate_ref, gwb_ref, loss_ref, gpwb_ref, rows_ref,
             gw_ref, gpw_ref):
        b, i = pl.program_id(0), pl.program_id(1)

        @pl.when((b == 0) & (i == 0))
        def _():
            gw_ref[...] = jnp.zeros_like(gw_ref)
            gpw_ref[...] = jnp.zeros_like(gpw_ref)
            rows_ref[...] = jnp.zeros_like(rows_ref)
            loss_ref[...] = jnp.zeros_like(loss_ref)

        @pl.when(i == 0)
        def _():
            dgate_ref[...] = jnp.zeros_like(dgate_ref)

        gate = mod_ref[0, 2:3, :]
        w = w_ref[...]
        o, za_v, cp_v, zc_v = o_ref[0], za_ref[0], cp_ref[0], zc_ref[0]
        sa = _sigmoid(za_v)
        sc = _sigmoid(zc_v)
        silu_a = za_v * sa
        silu_c = zc_v * sc
        mix = jnp.concatenate([(o * silu_a).astype(BF16), (cp_v * silu_c).astype(BF16)], axis=-1)
        out = _dot(mix, w)
        err = x_ref[0] + gate * out - t_ref[0]
        loss_ref[...] += jnp.sum(err * err, axis=0, keepdims=True)
        dh = err * (1.0 / D)
        dh_ref[0] = dh
        dgate_ref[0] += jnp.sum(dh * out, axis=0, keepdims=True)
        dout = (dh * gate).astype(BF16)
        gw_ref[...] += _dot_tn(mix, dout)
        dmix = _dot_nt(dout, w)
        dga = dmix[:, 0:DA]
        dgc = dmix[:, DA:DA + DC]
        dov = dga * silu_a
        for h in range(DA // HD):
            do_ref[0, h] = dov[:, h * HD:(h + 1) * HD].astype(BF16)
        dza_ref[0] = (dga * o * (sa * (1.0 + za_v * (1.0 - sa)))).astype(BF16)
        dzc_ref[0] = (dgc * cp_v * (sc * (1.0 + zc_v * (1.0 - sc)))).astype(BF16)
        dcp = dgc * silu_c
        y = y_ref[0]
        yc = y - jnp.mean(y, axis=-1, keepdims=True)
        rstd = lax.rsqrt(jnp.mean(yc * yc, axis=-1, keepdims=True) + EPS)
        yn = yc * rstd
        lw = lw_ref[...]
        z = yn * lw + lb_ref[...]
        sg = _sigmoid(z)
        dcp_b = dcp.astype(BF16)
        gpw_ref[...] += _dot_tn((z * sg).astype(BF16), dcp_b)
        dz = _dot_nt(dcp_b, wpw_ref[...]) * (sg * (1.0 + z * (1.0 - sg)))
        dyn = dz * lw
        dy = rstd * (dyn - jnp.mean(dyn, axis=-1, keepdims=True) - yn * jnp.mean(dyn * yn, axis=-1, keepdims=True))
        dy_ref[0] = dy
        rows_ref[0:1, :] += jnp.sum(dcp, axis=0, keepdims=True)
        rows_ref[1:2, :] += jnp.sum(dz * yn, axis=0, keepdims=True)
        rows_ref[2:3, :] += jnp.sum(dz, axis=0, keepdims=True)
        rows_ref[3:4, :] += jnp.sum(dy, axis=0, keepdims=True)

        @pl.when((b == bl - 1) & (i == s // tm - 1))
        def _():
            gwb_ref[...] = gw_ref[...].astype(BF16)
            gpwb_ref[...] = gpw_ref[...].astype(BF16)

    def const(shape):
        return pl.BlockSpec(shape, lambda b, i: (0,) * len(shape))

    def tile(w):
        return pl.BlockSpec((1, tm, w), lambda b, i: (b, i, 0))

    return pl.pallas_call(
        body, name="out_fwd_bwd", grid=(bl, s // tm),
        in_specs=[tile(DA), tile(DA), tile(DC), tile(DC), tile(D), tile(D),
                  pl.BlockSpec((1, 3, D), lambda b, i: (b, 0, 0)), const((D, D)),
                  tile(DC), const((1, DC)), const((1, DC)), const((DC, DC))],
        out_specs=[pl.BlockSpec((1, DA // HD, tm, HD), lambda b, i: (b, 0, i, 0)), tile(DA), tile(DC), tile(DC), tile(D),
                   pl.BlockSpec((1, 1, D), lambda b, i: (b, 0, 0)), const((D, D)), const((1, D)),
                   const((DC, DC)), const((8, DC))],
        out_shape=[jax.ShapeDtypeStruct((bl, DA // HD, s, HD), BF16), jax.ShapeDtypeStruct((bl, s, DA), BF16),
                   jax.ShapeDtypeStruct((bl, s, DC), F32), jax.ShapeDtypeStruct((bl, s, DC), BF16),
                   jax.ShapeDtypeStruct((bl, s, D), F32), jax.ShapeDtypeStruct((bl, 1, D), F32),
                   jax.ShapeDtypeStruct((D, D), BF16), jax.ShapeDtypeStruct((1, D), F32),
                   jax.ShapeDtypeStruct((DC, DC), BF16), jax.ShapeDtypeStruct((8, DC), F32)],
        scratch_shapes=[pltpu.VMEM((D, D), F32), pltpu.VMEM((DC, DC), F32)],
        compiler_params=_params(("arbitrary", "arbitrary")),
    )(attn, za, cp, zc, x, target, modrows, w_out_b, y_conv, ln_w, ln_b, w_pw_b)


def _rms_heads_bwd(dy, x, w_t, ones_bd):
    r = lax.rsqrt(_segsum(x * x, ones_bd) * (1.0 / HD) + EPS)
    xh = x * r
    g = dy * w_t
    dx = r * (g - xh * (_segsum(g * xh, ones_bd) * (1.0 / HD)))
    return dx, dy * xh


def _ctx_bwd(ctx, modc, norm_w, w_kv_b, pkv_c, dk_c, dv_c, knw_t, ones_bd):
    bl, cl, _ = ctx.shape

    def body(x_ref, mod_ref, nw_ref, w_ref, p_ref, dk_ref, dv_ref, knw_ref, bd_ref, gw_ref, rows_ref, dknw_ref):
        @pl.when(pl.program_id(0) == 0)
        def _():
            gw_ref[...] = jnp.zeros_like(gw_ref)
            rows_ref[...] = jnp.zeros_like(rows_ref)
            dknw_ref[...] = jnp.zeros_like(dknw_ref)

        xv = x_ref[0]
        shift = mod_ref[0, 0:1, :]
        scale = mod_ref[0, 1:2, :]
        nw = nw_ref[...]
        r = lax.rsqrt(jnp.mean(xv * xv, axis=-1, keepdims=True) + EPS)
        xn = xv * r
        yv = xn * nw
        u = yv * (1.0 + scale) + shift
        dkv = jnp.concatenate([dk_ref[0, 0], dk_ref[0, 1]], axis=-1)
        dpk, dknw = _rms_heads_bwd(dkv, p_ref[0][:, 0:KVW], knw_ref[...], bd_ref[...])
        dp = jnp.concatenate([dpk.astype(BF16), dv_ref[0, 0].astype(BF16), dv_ref[0, 1].astype(BF16)], axis=-1)
        gw_ref[...] += _dot_tn(dp, u.astype(BF16))
        du = _dot(dp, w_ref[...])
        rows_ref[0:1, :] += jnp.sum(du, axis=0, keepdims=True)
        rows_ref[1:2, :] += jnp.sum(du * yv, axis=0, keepdims=True)
        rows_ref[2:3, :] += jnp.sum(du * (1.0 + scale) * xn, axis=0, keepdims=True)
        dknw_ref[...] += jnp.sum(dknw, axis=0, keepdims=True)

    def const(shape):
        return pl.BlockSpec(shape, lambda b: (0,) * len(shape))

    def tile(w):
        return pl.BlockSpec((1, cl, w), lambda b: (b, 0, 0))

    ctx_block = (dk_c.shape[2] - cl) // cl
    kv_tile = pl.BlockSpec((1, KVW // HD, cl, HD), lambda b: (b, 0, ctx_block, 0))
    return pl.pallas_call(
        body, name="ctx_bwd", grid=(bl,),
        in_specs=[tile(D), const((1, 3, D)), const((1, D)), _KV_ROWS_OF_W_IN_T, tile(2 * KVW), kv_tile, kv_tile,
                  const((1, KVW)), const((KVW, KVW))],
        out_specs=[const((2 * KVW, D)), const((8, D)), const((1, KVW))],
        out_shape=[jax.ShapeDtypeStruct((2 * KVW, D), F32), jax.ShapeDtypeStruct((8, D), F32),
                   jax.ShapeDtypeStruct((1, KVW), F32)],
        compiler_params=_params(("arbitrary",)),
    )(ctx, modc, norm_w, w_kv_b, pkv_c, dk_c, dv_c, knw_t, ones_bd)


def _bwd_in(x, modrows, norm_w, w_in_b, cos, sins, qnw_t, knw_t, ones_bd,
            pq, pkv, dq, dk, dv, dza, dglu, dzc, dh, gw_kv):
    bl, s, _ = x.shape
    tm = TOKEN_PARTS * TM
    nt = s // tm

    def body(x_ref, mod_ref, nw_ref, win_hbm, cos_ref, sin_ref, qnw_ref, knw_ref, bd_ref,
             pq_ref, pkv_ref, dq_ref, dk_ref, dv_ref, dza_ref, dglu_ref, dzc_ref, dh_ref, gwkv_ref,
             gx_ref, gw_hbm, dmod_ref, dnw_ref, dqnw_ref, dknw_ref, win_ref, gw_acc, sem):
        b, i = pl.program_id(0), pl.program_id(1)

        @pl.when((b == 0) & (i == 0))
        def _():
            cp = pltpu.make_async_copy(win_hbm, win_ref, sem)
            cp.start()
            gw_acc[...] = jnp.zeros_like(gw_acc)
            dnw_ref[...] = jnp.zeros_like(dnw_ref)
            dqnw_ref[...] = jnp.zeros_like(dqnw_ref)
            dknw_ref[...] = jnp.zeros_like(dknw_ref)
            cp.wait()

        @pl.when(i == 0)
        def _():
            dmod_ref[...] = jnp.zeros_like(dmod_ref)

        bd = bd_ref[...]
        shift = mod_ref[0, 0:1, :]
        scale = mod_ref[0, 1:2, :]
        nw = nw_ref[...]
        dps, us = [], []
        for part in range(TOKEN_PARTS):
            rows = pl.ds(part * TM, TM)
            ck = cos_ref[rows, :]
            sk = sin_ref[rows, :]
            cs = jnp.concatenate([ck] * (DA // KVW), axis=-1)
            sn = jnp.concatenate([sk] * (DA // KVW), axis=-1)
            dqn = _rope_bwd(dq_ref[0, rows, :], cs, sn)
            dpq, dqnw = _rms_heads_bwd(dqn, pq_ref[0, rows, :], qnw_ref[...], bd)
            dkn = _rope_bwd(jnp.concatenate([dk_ref[0, 0, rows, :], dk_ref[0, 1, rows, :]], axis=-1), ck, sk)
            dpk, dknw = _rms_heads_bwd(dkn, pkv_ref[0, rows, 0:KVW], knw_ref[...], bd[0:KVW, 0:KVW])
            dqnw_ref[...] += jnp.sum(dqnw, axis=0, keepdims=True)
            dknw_ref[...] += jnp.sum(dknw, axis=0, keepdims=True)
            dp = jnp.concatenate(
                [dpq.astype(BF16), dpk.astype(BF16), dv_ref[0, 0, rows, :].astype(BF16), dv_ref[0, 1, rows, :].astype(BF16),
                 dza_ref[0, rows, :], dglu_ref[0, rows, :], dzc_ref[0, rows, :]], axis=-1)

            xv = x_ref[0, rows, :]
            r = lax.rsqrt(jnp.mean(xv * xv, axis=-1, keepdims=True) + EPS)
            xn = xv * r
            yv = xn * nw
            u = yv * (1.0 + scale) + shift
            dps.append(dp)
            us.append(u.astype(BF16))
            du = _dot(dp, win_ref[...])
            dmod_ref[0, 0:1, :] += jnp.sum(du, axis=0, keepdims=True)
            dmod_ref[0, 1:2, :] += jnp.sum(du * yv, axis=0, keepdims=True)
            dy = du * (1.0 + scale)
            dnw_ref[...] += jnp.sum(dy * xn, axis=0, keepdims=True)
            dxn = dy * nw
            gx_ref[0, rows, :] = dh_ref[0, rows, :] + r * (dxn - xn * jnp.mean(dxn * xn, axis=-1, keepdims=True))
        gw_acc[...] += _dot_tn(jnp.concatenate(dps, axis=0), jnp.concatenate(us, axis=0))

        @pl.when((b == bl - 1) & (i == nt - 1))
        def _():
            gw_acc[DA:DA + 2 * KVW, :] += gwkv_ref[...]

            def to_bf16(j, carry):
                rows = pl.ds(pl.multiple_of(j * 2 * KVW, 2 * KVW), 2 * KVW)
                win_ref[rows, :] = gw_acc[rows, :].astype(BF16)
                return carry

            lax.fori_loop(0, D_IN // (2 * KVW), to_bf16, 0)
            pltpu.sync_copy(win_ref, gw_hbm)

    def tile(w):
        return pl.BlockSpec((1, tm, w), lambda b, i: (b, i, 0))

    def const(shape):
        return pl.BlockSpec(shape, lambda b, i: (0,) * len(shape))

    anyspace = pl.BlockSpec(memory_space=pl.ANY)
    rope = pl.BlockSpec((tm, KVW), lambda b, i: (i, 0))
    kv_tile = pl.BlockSpec((1, KVW // HD, tm, HD), lambda b, i: (b, 0, i, 0))
    return pl.pallas_call(
        body, name="bwd_in", grid=(bl, nt),
        in_specs=[tile(D), pl.BlockSpec((1, 3, D), lambda b, i: (b, 0, 0)), const((1, D)), anyspace, rope, rope,
                  const((1, DA)), const((1, KVW)), const((DA, DA)),
                  tile(DA), tile(2 * KVW), tile(DA), kv_tile, kv_tile, tile(DA), tile(2 * DC), tile(DC), tile(D),
                  const((2 * KVW, D))],
        out_specs=[tile(D), anyspace, pl.BlockSpec((1, 2, D), lambda b, i: (b, 0, 0)), const((1, D)),
                   const((1, DA)), const((1, KVW))],
        out_shape=[jax.ShapeDtypeStruct((bl, s, D), F32), jax.ShapeDtypeStruct((D_IN, D), BF16),
                   jax.ShapeDtypeStruct((bl, 2, D), F32), jax.ShapeDtypeStruct((1, D), F32),
                   jax.ShapeDtypeStruct((1, DA), F32), jax.ShapeDtypeStruct((1, KVW), F32)],
        scratch_shapes=[pltpu.VMEM((D_IN, D), BF16), pltpu.VMEM((D_IN, D), F32), pltpu.SemaphoreType.DMA],
        compiler_params=_params(("arbitrary", "arbitrary")),
    )(x, modrows, norm_w, w_in_b, cos, sins, qnw_t, knw_t, ones_bd,
      pq, pkv, dq, dk, dv, dza, dglu, dzc, dh, gw_kv)


_LOSS, _DMODC, _NW, _QN, _KN, _CB, _LW, _LB, _BPW, SMALL_W = 0, 1024, 4096, 5120, 5248, 5376, 5888, 6400, 6912, 7424


ROW_W = 1792


def _put_flat(ref, off, value):
    n, done = value.shape[1], 0
    while done < n:
        r, c = divmod(off + done, ROW_W)
        take = min(n - done, ROW_W - c)
        ref[r:r + 1, c:c + take] = value[:, done:done + take]
        done += take


def _get_flat(arr, off, n):
    parts, done = [], 0
    while done < n:
        r, c = divmod(off + done, ROW_W)
        take = min(n - done, ROW_W - c)
        parts.append(arr[r:r + 1, c:c + take])
        done += take
    return parts[0] if len(parts) == 1 else jnp.concatenate(parts, axis=-1)


def _pack_small_body(loss_ref, ctx_ref, dnw_ref, dqnw_ref, dknw_ref, dknwc_ref, conv_ref, dss_ref, dgate_ref, o_ref):
    bl = dss_ref.shape[0]
    assert SMALL_W + bl * 3 * D <= 8 * ROW_W
    o_ref[...] = jnp.zeros_like(o_ref)
    _put_flat(o_ref, _LOSS, loss_ref[...])
    _put_flat(o_ref, _DMODC, ctx_ref[0:1, :])
    _put_flat(o_ref, _DMODC + D, ctx_ref[1:2, :])
    _put_flat(o_ref, _NW, dnw_ref[...] + ctx_ref[2:3, :])
    dq = dqnw_ref[...]
    qn = dq[:, 0:HD]
    for h in range(1, DA // HD):
        qn = qn + dq[:, h * HD:(h + 1) * HD]
    _put_flat(o_ref, _QN, qn)
    dk = dknw_ref[...] + dknwc_ref[...]
    _put_flat(o_ref, _KN, dk[:, 0:HD] + dk[:, HD:2 * HD])
    _put_flat(o_ref, _BPW, conv_ref[0:1, :])
    _put_flat(o_ref, _LW, conv_ref[1:2, :])
    _put_flat(o_ref, _LB, conv_ref[2:3, :])
    _put_flat(o_ref, _CB, conv_ref[3:4, :])
    for b in range(bl):
        _put_flat(o_ref, SMALL_W + b * 3 * D, dss_ref[b, 0:1, :])
        _put_flat(o_ref, SMALL_W + b * 3 * D + D, dss_ref[b, 1:2, :])
        _put_flat(o_ref, SMALL_W + b * 3 * D + 2 * D, dgate_ref[b])


_SMALL = (("b_mod", None), ("norm_w", _NW), ("q_norm_w", _QN), ("k_norm_w", _KN), ("conv_b", _CB),
          ("conv_ln_w", _LW), ("conv_ln_b", _LB), ("b_pw", _BPW), ("c_ctx", None))


def _epilogue(parts_in, pieces, c_rows, w_mod_loc):
    bl = pieces[7].shape[0]
    n_ex = N_DEV * bl
    n_mod = w_mod_loc.shape[1]
    rb = 32
    shp = parts_in.shape[1:]
    rows_in = shp[0]

    def body(*refs):
        it = iter(refs)
        take = lambda k: [next(it) for _ in range(k)]
        (parts,) = take(1)
        piece_refs = take(9)
        (c_ref, wm_ref) = take(2)
        (g_in, g_wm, sum_ref, gb_ref, gc_all, loss_ref) = take(6)
        (mine, got_sib, stage, got_chip, payload, gathered, dmod_full, gc_mine) = take(8)
        (d2d_send, d2d_recv, ici_send, ici_recv, local_sems, sg_send, sg_recv, gc_send, gc_recv, misc_sems) = take(10)

        x, y, c = _coords()
        me = _lin(x, y, c)
        sib = (x, y, 1 - c)
        home = 2 * x + y

        def rows_loop(fn):
            def step(i, carry):
                fn(pl.ds(pl.multiple_of(i * rb, rb), rb))
                return carry
            lax.fori_loop(0, rows_in // rb, step, 0)

        def direct_gather(src, dst, send_sems, recv_sems, local_sem):
            cps = [pltpu.make_async_copy(src, dst.at[me], local_sem)]
            for k in range(1, N_DEV):
                peer = (1 - x if k & 4 else x, 1 - y if k & 2 else y, 1 - c if k & 1 else c)
                cps.append(pltpu.make_async_remote_copy(
                    src_ref=src, dst_ref=dst.at[me], send_sem=send_sems.at[k - 1], recv_sem=recv_sems.at[k - 1],
                    device_id=peer, device_id_type=MESH_ID))
            for cp in cps:
                cp.start()
            return cps

        _pack_small_body(*piece_refs, payload)
        small_cps = direct_gather(payload, gathered, sg_send, sg_recv, misc_sems.at[0])

        local, d2d, ici = [], [], []
        for s in range(4):
            cp = pltpu.make_async_copy(parts.at[_lin(s // 2, s % 2, c)], mine.at[s], local_sems.at[s])
            cp.start()
            local.append(cp)
            rc = pltpu.make_async_remote_copy(
                src_ref=parts.at[_lin(s // 2, s % 2, 1 - c)], dst_ref=got_sib.at[s],
                send_sem=d2d_send.at[s], recv_sem=d2d_recv.at[s], device_id=sib, device_id_type=MESH_ID)
            rc.start()
            d2d.append(rc)

        for cp in small_cps[1:]:
            cp.wait_recv()
        small_cps[0].wait()
        tot = gathered[0]
        for j in range(1, N_DEV):
            tot = tot + gathered[j]
        summed = _get_flat(tot, 0, SMALL_W)
        dmod_full[...] = jnp.zeros_like(dmod_full)
        for j in range(N_DEV):
            arr = gathered[j]
            for b in range(bl):
                dmod_full[j * bl + b:j * bl + b + 1, :] = _get_flat(arr, SMALL_W + b * 3 * D, 3 * D)
        dmod_full[n_ex:n_ex + 1, :] = summed[:, _DMODC:_DMODC + 3 * D]
        sum_ref[...] = summed
        gb_ref[...] = jnp.sum(dmod_full[...], axis=0, keepdims=True)
        loss_ref[...] = (0.5 / D) * jnp.sum(summed[:, _LOSS:_LOSS + D], axis=-1, keepdims=True)

        north = c == 1
        first = (jnp.where(north, 1 - x, x), jnp.where(north, y, 1 - y))
        second = (jnp.where(north, x, 1 - x), jnp.where(north, 1 - y, y))
        for s in range(4):
            local[s].wait()
            d2d[s].wait_recv()

        def chip_sum(k, chip, relayed):
            slot = 2 * chip[0] + chip[1]

            def pair_sum(rs):
                acc = mine[slot, rs, :].astype(F32) + got_sib[slot, rs, :].astype(F32)
                if relayed:
                    acc = acc + got_chip[1, rs, :].astype(F32)
                stage[k, rs, :] = acc.astype(BF16)

            rows_loop(pair_sum)

        def send(k, to):
            rc = pltpu.make_async_remote_copy(
                src_ref=stage.at[k], dst_ref=got_chip.at[k], send_sem=ici_send.at[k], recv_sem=ici_recv.at[k],
                device_id=(to[0], to[1], c), device_id_type=MESH_ID)
            rc.start()
            ici.append(rc)

        chip_sum(0, first, False)
        send(0, first)
        chip_sum(1, (1 - x, 1 - y), False)
        send(1, first)

        cr = c_ref[...]
        act = (cr * _sigmoid(cr)).astype(BF16)
        dm = dmod_full[:, pl.ds(pl.multiple_of(me * n_mod, 128), n_mod)].astype(BF16)
        g_wm[...] = _dot_tn(act, dm)
        gc_mine[...] = _dot_nt(dm[n_ex:n_ex + 8, :], wm_ref[...].astype(BF16))
        gc_cps = direct_gather(gc_mine, gc_all, gc_send, gc_recv, misc_sems.at[1])

        ici[1].wait_recv()
        chip_sum(2, second, True)
        send(2, second)
        ici[0].wait_recv()
        ici[2].wait_recv()

        def finish(rs):
            gsum = mine[home, rs, :].astype(F32) + got_sib[home, rs, :].astype(F32)
            g_in[rs, :] = gsum + got_chip[0, rs, :].astype(F32) + got_chip[2, rs, :].astype(F32)

        rows_loop(finish)

        for cp in gc_cps[1:]:
            cp.wait_recv()
        gc_cps[0].wait()
        for rc in d2d + ici + small_cps[1:] + gc_cps[1:]:
            rc.wait_send()

    vm = pl.BlockSpec(memory_space=pltpu.VMEM)
    anyspace = pl.BlockSpec(memory_space=pl.ANY)
    assert rows_in % rb == 0 and parts_in.dtype == BF16
    args = [parts_in, *pieces, c_rows, w_mod_loc]
    in_specs = [anyspace] + [vm] * (len(args) - 1)
    out_shape = [jax.ShapeDtypeStruct(shp, F32), jax.ShapeDtypeStruct(w_mod_loc.shape, F32),
                 jax.ShapeDtypeStruct((1, SMALL_W), F32), jax.ShapeDtypeStruct((1, 3 * D), F32),
                 jax.ShapeDtypeStruct((N_DEV, 8, D), F32), jax.ShapeDtypeStruct((1, 1), F32)]
    scratch = [pltpu.VMEM((4,) + shp, BF16), pltpu.VMEM((4,) + shp, BF16), pltpu.VMEM((3,) + shp, BF16),
               pltpu.VMEM((3,) + shp, BF16), pltpu.VMEM((8, ROW_W), F32), pltpu.VMEM((N_DEV, 8, ROW_W), F32),
               pltpu.VMEM((n_ex + 8, 3 * D), F32), pltpu.VMEM((8, D), F32),
               pltpu.SemaphoreType.DMA((4,)), pltpu.SemaphoreType.DMA((4,)), pltpu.SemaphoreType.DMA((3,)),
               pltpu.SemaphoreType.DMA((3,)), pltpu.SemaphoreType.DMA((4,)),
               pltpu.SemaphoreType.DMA((N_DEV - 1,)), pltpu.SemaphoreType.DMA((N_DEV - 1,)),
               pltpu.SemaphoreType.DMA((N_DEV - 1,)), pltpu.SemaphoreType.DMA((N_DEV - 1,)),
               pltpu.SemaphoreType.DMA((2,))]
    return pl.pallas_call(
        body, name="epilogue", out_shape=out_shape, in_specs=in_specs, out_specs=[vm] * len(out_shape),
        scratch_shapes=scratch, compiler_params=pltpu.CompilerParams(vmem_limit_bytes=VMEM_LIMIT),
    )(*args)


def _final_adamw(tiled, streamed_sums, local_sums, summed, g_bmod, gc_all, small_w, small_m, small_v):
    ns = len(_SMALL)
    n_tiled, n_sums, n_local = len(tiled), len(streamed_sums), len(local_sums)
    streams = list(tiled) + list(streamed_sums)
    n_results = [4] * len(streams)
    tile_jobs = [[(a, t * (w.shape[0] // ADAM_STEPS), w.shape[0] // ADAM_STEPS) for a, (_, w, _, _) in enumerate(tiled)]
                 for t in range(ADAM_STEPS)]
    sum_jobs = [(n_tiled + a, 0, w.shape[0]) for a, (_, w, _, _) in enumerate(streamed_sums)]
    jobs = sum(tile_jobs[:ADAM_STEPS // 2], []) + sum_jobs + sum(tile_jobs[ADAM_STEPS // 2:], [])
    for _, w, _, _ in tiled:
        assert w.shape[0] % (8 * ADAM_STEPS) == 0
    for got, w, _, _ in list(streamed_sums) + list(local_sums):
        assert got.shape[0] == N_DEV and got.shape[1] >= w.shape[0] and got.shape[2] == w.shape[-1]
    n_reads = 4 * len(jobs)
    n_writes = sum(n_results[a] for a, _, _ in jobs)

    def body(*refs):
        it = iter(refs)
        take = lambda k: [next(it) for _ in range(k)]
        src = [take(4) for _ in streams]
        loc = [take(4) for _ in range(n_local)]
        sum_ref, gb_ref, gc_ref = take(3)
        sw, sm, sv = take(ns), take(ns), take(ns)
        dst = [take(k) for k in n_results]
        loc_out = [take(4) for _ in range(n_local)]
        souts = take(4 * ns)
        src_buf = [take(4) for _ in streams]
        dst_buf = [take(k) for k in n_results]
        read_sem, write_sem = take(2)

        def rows_of(ref, r0, nr):
            return ref.at[:, pl.ds(r0, nr), :] if len(ref.shape) == 3 else ref.at[pl.ds(r0, nr), :]

        reads = [[pltpu.make_async_copy(rows_of(src[a][k], r0, nr), rows_of(src_buf[a][k], r0, nr),
                                        read_sem.at[4 * j + k]) for k in range(4)]
                 for j, (a, r0, nr) in enumerate(jobs)]
        for job_reads in reads:
            for cp in job_reads:
                cp.start()

        def summed_shares(got, n_rows):
            g = got[0].astype(F32)
            for d in range(1, N_DEV):
                g = g + got[d].astype(F32)
            return g[0:n_rows, :]

        for (got, w_r, m_r, v_r), outs4 in zip(loc, loc_out):
            g = summed_shares(got, w_r.shape[0])
            if len(w_r.shape) == 3:
                for r in range(w_r.shape[0]):
                    g_row = g[r:r + 1, :]
                    for o_r, val in zip(outs4, (g_row,) + _adamw(w_r[r], g_row, m_r[r], v_r[r])):
                        o_r[r] = val
            else:
                for o_r, val in zip(outs4, (g,) + _adamw(w_r[...], g, m_r[...], v_r[...])):
                    o_r[...] = val
        for k, (name, off) in enumerate(_SMALL):
            w = sw[k][...]
            if name == "b_mod":
                gk = gb_ref[...]
            elif name == "c_ctx":
                acc = gc_ref[0, 0:1, :]
                for j in range(1, N_DEV):
                    acc = acc + gc_ref[j, 0:1, :]
                sg = _sigmoid(w)
                gk = acc * (sg * (1.0 + w * (1.0 - sg)))
            else:
                gk = sum_ref[:, off:off + w.shape[1]]
            dl, m_new, v_new = _adamw(w, gk, sm[k][...], sv[k][...])
            souts[k][...] = gk
            souts[ns + k][...] = dl
            souts[2 * ns + k][...] = m_new
            souts[3 * ns + k][...] = v_new

        writes, n_started = [], 0
        for j, (a, r0, nr) in enumerate(jobs):
            for cp in reads[j]:
                cp.wait()
            rows = pl.ds(r0, nr)
            g_b, w_b, m_b, v_b = src_buf[a]
            if a < n_tiled:
                g = g_b[rows, :]
                vals = (g,) + _adamw(w_b[rows, :], g, m_b[rows, :], v_b[rows, :])
            else:
                g = summed_shares(g_b, nr)
                vals = (g,) + _adamw(w_b[...], g, m_b[...], v_b[...])
            for k, val in enumerate(vals):
                dst_buf[a][k][rows, :] = val
                cp = pltpu.make_async_copy(rows_of(dst_buf[a][k], r0, nr), rows_of(dst[a][k], r0, nr),
                                           write_sem.at[n_started])
                cp.start()
                writes.append(cp)
                n_started += 1
        for cp in writes:
            cp.wait()

    vm = pl.BlockSpec(memory_space=pltpu.VMEM)
    anyspace = pl.BlockSpec(memory_space=pl.ANY)
    args, out_shape, scratch = [], [], []
    for item in streams:
        args += list(item)
        scratch += [pltpu.VMEM(a.shape, a.dtype) for a in item]
    for item in local_sums:
        args += list(item)
    args += [summed, g_bmod, gc_all, *small_w, *small_m, *small_v]
    for (_, w, _, _), k in zip(streams, n_results):
        out_shape += [jax.ShapeDtypeStruct(w.shape, F32)] * k
    scratch += [pltpu.VMEM(s.shape, F32) for s in out_shape]
    n_streamed_out = len(out_shape)
    for _, w, _, _ in local_sums:
        out_shape += [jax.ShapeDtypeStruct(w.shape, F32)] * 4
    out_shape += [jax.ShapeDtypeStruct(w.shape, F32) for w in small_w] * 4
    scratch += [pltpu.SemaphoreType.DMA((n_reads,)), pltpu.SemaphoreType.DMA((n_writes,))]
    outs = pl.pallas_call(
        body, name="final_adamw", out_shape=out_shape,
        in_specs=[anyspace] * (4 * len(streams)) + [vm] * (len(args) - 4 * len(streams)),
        out_specs=[anyspace] * n_streamed_out + [vm] * (len(out_shape) - n_streamed_out),
        scratch_shapes=scratch, compiler_params=pltpu.CompilerParams(vmem_limit_bytes=VMEM_LIMIT),
    )(*args)
    it = iter(outs)
    take = lambda k: tuple(next(it) for _ in range(k))
    r_tiled = [take(4) for _ in tiled]
    r_sums = [take(4) for _ in streamed_sums]
    r_local = [take(4) for _ in local_sums]
    small_outs = [list(take(ns)) for _ in range(4)]
    return r_tiled, r_sums, r_local, small_outs


def _rope_tables(s):
    t = np.arange(s, dtype=np.int32)
    row = (t // GRID_W).astype(np.float32)
    col = (t % GRID_W).astype(np.float32)
    freqs = (np.float32(ROPE_THETA) ** (-np.arange(0, HD // 2, 2, dtype=np.float32) / np.float32(HD // 2))).astype(np.float32)
    ang_r = row[:, None] * freqs[None, :]
    ang_c = col[:, None] * freqs[None, :]
    cr, sr, cc, sc = np.cos(ang_r), np.sin(ang_r), np.cos(ang_c), np.sin(ang_c)
    cos = np.concatenate([cr, cr, cc, cc], axis=-1)
    sins = np.concatenate([-sr, sr, -sc, sc], axis=-1)
    return (jnp.asarray(np.tile(cos, (1, KVW // HD)), dtype=F32),
            jnp.asarray(np.tile(sins, (1, KVW // HD)), dtype=F32))


def kernel(x, c, ctx, c_ctx, w_mod, b_mod, norm_w, w_in, q_norm_w, k_norm_w, conv_w, conv_b, conv_ln_w, conv_ln_b, w_pw, b_pw, w_out, loss_target, m_c_ctx, m_w_mod, m_b_mod, m_norm_w, m_w_in, m_q_norm_w, m_k_norm_w, m_conv_w, m_conv_b, m_conv_ln_w, m_conv_ln_b, m_w_pw, m_b_pw, m_w_out, v_c_ctx, v_w_mod, v_b_mod, v_norm_w, v_w_in, v_q_norm_w, v_k_norm_w, v_conv_w, v_conv_b, v_conv_ln_w, v_conv_ln_b, v_w_pw, v_b_pw, v_w_out):
    bl, s, _ = x.shape
    cl = ctx.shape[1]
    me = _lin(*_coords())

    conv_w_pad = jnp.pad(conv_w[0], ((0, 32 - KW), (0, 0)))
    n_ex = N_DEV * bl
    g_win, c_rows, g_mod = _prologue(w_in[0].T, c, c_ctx[None, :], w_mod[0], b_mod)
    w_in_b = g_win.reshape(D_IN, D)
    mod_all = g_mod.transpose(1, 0, 2).reshape(n_ex + 8, 3 * D)
    modrows = lax.dynamic_slice_in_dim(mod_all, me * bl, bl, axis=0).reshape(bl, 3, D)
    modc = mod_all[n_ex].reshape(1, 3, D)

    cos, sins = _rope_tables(s)
    qnw_t = jnp.tile(q_norm_w, (1, DA // HD))
    knw_t = jnp.tile(k_norm_w, (1, KVW // HD))
    lane = np.arange(DA, dtype=np.int32) // HD
    ones_host = (lane[:, None] == lane[None, :]).astype(np.float32)
    ones_bd = jnp.asarray(ones_host, dtype=BF16)
    ones_kv = jnp.asarray(ones_host[0:KVW, 0:KVW], dtype=BF16)
    w_kv_b = w_in_b

    k_ctx, v_ctx, pkv_c = _ctx_fwd(ctx, modc, norm_w, w_kv_b, knw_t, ones_kv, cl + s)
    (q_h, k_h, v_h, pq, pkv, za, glu, zc), (g_wout, g_wpw, g_cw) = _fwd_in(
        x, modrows, norm_w, w_in_b, cos, sins, qnw_t, knw_t, ones_bd, k_ctx, v_ctx,
        [w_out[0], w_pw[0], conv_w_pad], [BF16, BF16, F32])
    w_out_b = g_wout.reshape(D, D)
    w_pw_b = g_wpw.reshape(DC, DC)
    conv_w_full = g_cw.transpose(1, 0, 2).reshape(32, DC)
    attn, lse = _attn_fwd(q_h, k_h, v_h)
    y_conv, cp = _conv_fwd(glu, conv_w_full, conv_b, conv_ln_w, conv_ln_b, w_pw_b, b_pw)

    do_h, dza, dy_conv, dzc, dh, dgate, gw_out, loss_row, gw_pw, conv_rows = _out_fwd_bwd(
        attn, za, cp, zc, x, loss_target, modrows, w_out_b, y_conv, conv_ln_w, conv_ln_b, w_pw_b)
    dglu, parts_cw = _conv_bwd_depthwise(glu, dy_conv, conv_w_full)
    parts_out = gw_out.reshape(N_DEV, D // N_DEV, D)
    parts_pw = gw_pw.reshape(N_DEV, DC // N_DEV, DC)
    (dq, dk_h, dv_h), (got_out, got_pw, got_cw) = _attn_bwd(
        q_h, k_h, v_h, do_h, attn, lse, [parts_out, parts_pw, parts_cw])
    gw_kv, ctx_rows, dknw_c = _ctx_bwd(ctx, modc, norm_w, w_kv_b, pkv_c, dk_h, dv_h, knw_t, ones_kv)
    grad_x, gw_in, dmod_ss, dnw, dqnw, dknw = _bwd_in(
        x, modrows, norm_w, w_in_b, cos, sins, qnw_t, knw_t, ones_bd,
        pq, pkv, dq, dk_h, dv_h, dza, dglu, dzc, dh, gw_kv)

    given = {"c_ctx": (c_ctx, m_c_ctx, v_c_ctx), "b_mod": (b_mod, m_b_mod, v_b_mod), "norm_w": (norm_w, m_norm_w, v_norm_w),
             "q_norm_w": (q_norm_w, m_q_norm_w, v_q_norm_w), "k_norm_w": (k_norm_w, m_k_norm_w, v_k_norm_w),
             "conv_b": (conv_b, m_conv_b, v_conv_b), "conv_ln_w": (conv_ln_w, m_conv_ln_w, v_conv_ln_w),
             "conv_ln_b": (conv_ln_b, m_conv_ln_b, v_conv_ln_b), "b_pw": (b_pw, m_b_pw, v_b_pw)}
    as_rows = [[given[name][which].reshape(1, -1) for name, _ in _SMALL] for which in range(3)]
    g_in_t, g_wmod, summed, g_bmod, gc_all, loss11 = _epilogue(
        gw_in.reshape(N_DEV, D_IN // N_DEV, D),
        [loss_row, ctx_rows, dnw, dqnw, dknw, dknw_c, conv_rows, dmod_ss, dgate], c_rows, w_mod[0])
    (r_in, r_wmod), (r_out,), (r_pw, r_cw), small_outs = _final_adamw(
        [(g_in_t, w_in[0].T, m_w_in[0].T, v_w_in[0].T), (g_wmod, w_mod[0], m_w_mod[0], v_w_mod[0])],
        [(got_out, w_out[0], m_w_out[0], v_w_out[0])],
        [(got_pw, w_pw[0], m_w_pw[0], v_w_pw[0]),
         (got_cw, conv_w.transpose(1, 0, 2), m_conv_w.transpose(1, 0, 2), v_conv_w.transpose(1, 0, 2))],
        summed, g_bmod, gc_all, *as_rows)
    r_in = tuple(a.T for a in r_in)
    r_cw = tuple(a.transpose(1, 0, 2)[0] for a in r_cw)

    big = {"w_mod": r_wmod, "w_in": r_in, "conv_w": r_cw, "w_pw": r_pw, "w_out": r_out}
    order = ["c_ctx", "w_mod", "b_mod", "norm_w", "w_in", "q_norm_w", "k_norm_w", "conv_w", "conv_b", "conv_ln_w",
             "conv_ln_b", "w_pw", "b_pw", "w_out"]
    small_index = {name: k for k, (name, _) in enumerate(_SMALL)}
    outs = [loss11.reshape(()), grad_x]
    for which in range(4):
        for name in order:
            if name in big:
                outs.append(big[name][which][None])
            else:
                outs.append(small_outs[which][small_index[name]].reshape(given[name][0].shape))
    return tuple(outs)
```
